```python
import math
import jax, jax.numpy as jnp
from jax import lax
import numpy as np

D_MODEL = 1024
BATCH = 8
SEQ = 8192
DEPTH = 1

N_META = 16
HEAD_DIM = 64
N_Q_HEADS = 16
N_KV_HEADS = 4
GQA_GROUP = N_Q_HEADS // N_KV_HEADS
WINDOW = 128
SSM_WIDTH = D_MODEL
SSM_GROUP_CH = 16
SSM_GROUPS = SSM_WIDTH // SSM_GROUP_CH
SSM_STATE = 64
D_FF = -(-8 * D_MODEL // (3 * 256)) * 256
Q_W = N_Q_HEADS * HEAD_DIM
KV_W = N_KV_HEADS * HEAD_DIM
IN_COLS = Q_W + 2 * KV_W + SSM_WIDTH + 2 * D_MODEL
EPS = 1e-6
DT_MIN = 1e-3
DT_MAX = 1e-1

kernel_name = "hybrid_swa_sink_s5_gated_block"


def rmsnorm(x, g):
    xf = x.astype(jnp.float32)
    xf = xf * lax.rsqrt(jnp.mean(xf * xf, axis=-1, keepdims=True) + EPS)
    return (xf * g.astype(jnp.float32)).astype(x.dtype)


def softmax_with_sink(s, sink):
    sink = sink.astype(jnp.float32)[:, :, None, None]
    m = jnp.maximum(jnp.max(s, axis=-1, keepdims=True), sink)
    e = jnp.exp(s - m)
    return e / (jnp.sum(e, axis=-1, keepdims=True) + jnp.exp(sink - m))


def sliding_window_attention(q, k, v, sinks):
    b, L = q.shape[0], q.shape[1]
    n_blk = (L - N_META) // WINDOW
    scale = HEAD_DIM ** -0.5
    sink = sinks.reshape(N_KV_HEADS, GQA_GROUP)
    qm = q[:, :N_META].reshape(b, N_META, N_KV_HEADS, GQA_GROUP, HEAD_DIM)
    km, vm = k[:, :N_META], v[:, :N_META]
    s_m = jnp.einsum('bqkrd,bskd->bkrqs', qm, km, preferred_element_type=jnp.float32) * scale
    causal = jnp.tril(jnp.ones((N_META, N_META), dtype=bool))
    s_m = jnp.where(causal, s_m, -jnp.inf)
    p_m = softmax_with_sink(s_m, sink).astype(v.dtype)
    o_m = jnp.einsum('bkrqs,bskd->bqkrd', p_m, vm).reshape(b, N_META, Q_W)
    qb = q[:, N_META:].reshape(b, n_blk, WINDOW, N_KV_HEADS, GQA_GROUP, HEAD_DIM)
    kb = k[:, N_META:].reshape(b, n_blk, WINDOW, N_KV_HEADS, HEAD_DIM)
    vb = v[:, N_META:].reshape(b, n_blk, WINDOW, N_KV_HEADS, HEAD_DIM)
    pad = ((0, 0), (1, 0), (0, 0), (0, 0), (0, 0))
    k_prev = jnp.pad(kb[:, :-1], pad)
    v_prev = jnp.pad(vb[:, :-1], pad)
    meta_shape = (b, n_blk, N_META, N_KV_HEADS, HEAD_DIM)
    k_win = jnp.concatenate([jnp.broadcast_to(km[:, None], meta_shape), k_prev, kb], axis=2)
    v_win = jnp.concatenate([jnp.broadcast_to(vm[:, None], meta_shape), v_prev, vb], axis=2)
    s = jnp.einsum('bnqkrd,bnskd->bnkrqs', qb, k_win, preferred_element_type=jnp.float32) * scale
    qi = jnp.arange(WINDOW)[:, None]
    kj = jnp.arange(WINDOW)[None, :]
    blk = jnp.arange(n_blk)[:, None, None]
    meta_vis = jnp.ones((n_blk, WINDOW, N_META), dtype=bool)
    prev_vis = (kj > qi)[None] & (blk > 0)
    cur_vis = jnp.broadcast_to((kj <= qi)[None], (n_blk, WINDOW, WINDOW))
    mask = jnp.concatenate([meta_vis, prev_vis, cur_vis], axis=-1)
    s = jnp.where(mask[None, :, None, None], s, -jnp.inf)
    p = softmax_with_sink(s, sink).astype(v.dtype)
    o_r = jnp.einsum('bnkrqs,bnskd->bnqkrd', p, v_win).reshape(b, n_blk * WINDOW, Q_W)
    return jnp.concatenate([o_m, o_r], axis=1)


def s5_ssm(u, lam_re, lam_im, log_dt, b_re, b_im, c_re, c_im, d_skip):
    bsz, L = u.shape[0], u.shape[1]
    uf = u.astype(jnp.float32)
    ug = uf.reshape(bsz, L, SSM_GROUPS, SSM_GROUP_CH)
    dt = jnp.exp(log_dt.astype(jnp.float32))[:, None]
    lr, li = lam_re.astype(jnp.float32), lam_im.astype(jnp.float32)
    mag = jnp.exp(lr * dt)
    ar, ai = mag * jnp.cos(li * dt), mag * jnp.sin(li * dt)
    den = lr * lr + li * li
    nr, ni = ar - 1.0, ai
    fr, fi = (nr * lr + ni * li) / den, (ni * lr - nr * li) / den
    br, bi = b_re.astype(jnp.float32), b_im.astype(jnp.float32)
    bbar_re = fr[..., None] * br - fi[..., None] * bi
    bbar_im = fr[..., None] * bi + fi[..., None] * br
    xr = jnp.einsum('gpc,blgc->blgp', bbar_re, ug)
    xi = jnp.einsum('gpc,blgc->blgp', bbar_im, ug)
    a_re = jnp.broadcast_to(ar[None, None], (1, L, SSM_GROUPS, SSM_STATE))
    a_im = jnp.broadcast_to(ai[None, None], (1, L, SSM_GROUPS, SSM_STATE))

    def combine(e1, e2):
        a1r, a1i, b1r, b1i = e1
        a2r, a2i, b2r, b2i = e2
        return (a1r * a2r - a1i * a2i,
                a1r * a2i + a1i * a2r,
                a2r * b1r - a2i * b1i + b2r,
                a2r * b1i + a2i * b1r + b2i)

    _, _, sr, si = lax.associative_scan(combine, (a_re, a_im, xr, xi), axis=1)
    y = (jnp.einsum('gcp,blgp->blgc', c_re.astype(jnp.float32), sr)
         - jnp.einsum('gcp,blgp->blgc', c_im.astype(jnp.float32), si))
    y = y.reshape(bsz, L, SSM_WIDTH) + d_skip.astype(jnp.float32) * uf
    return y


def _fwd_setup_inputs(seed: int = 0) -> dict:
    key = jax.random.key(seed)
    ks = jax.random.split(key, 24)
    f32 = jnp.float32
    nrm = lambda k, shape, s: jax.random.normal(k, shape, f32) * s
    gain = lambda k, shape: 1.0 + 0.02 * jax.random.normal(k, shape, f32)
    n_idx = jnp.arange(SSM_STATE, dtype=f32)
    return {
        "x": nrm(ks[0], (BATCH, SEQ, D_MODEL), 1.0),
        "meta_tokens": nrm(ks[1], (N_META, D_MODEL), 1.0),
        "norm_mix": gain(ks[2], (DEPTH, D_MODEL)),
        "w_in": nrm(ks[3], (DEPTH, D_MODEL, IN_COLS), D_MODEL ** -0.5),
        "q_norm": gain(ks[4], (DEPTH, HEAD_DIM)),
        "k_norm": gain(ks[5], (DEPTH, HEAD_DIM)),
        "attn_sinks": nrm(ks[6], (DEPTH, N_Q_HEADS), 0.5),
        "lam_re": -0.5 + nrm(ks[7], (DEPTH, SSM_GROUPS, SSM_STATE), 0.01),
        "lam_im": math.pi * n_idx + nrm(ks[8], (DEPTH, SSM_GROUPS, SSM_STATE), 0.01),
        "log_dt": jax.random.uniform(ks[9], (DEPTH, SSM_GROUPS), f32, math.log(DT_MIN), math.log(DT_MAX)),
        "ssm_b_re": nrm(ks[10], (DEPTH, SSM_GROUPS, SSM_STATE, SSM_GROUP_CH), (2 * SSM_GROUP_CH) ** -0.5),
        "ssm_b_im": nrm(ks[11], (DEPTH, SSM_GROUPS, SSM_STATE, SSM_GROUP_CH), (2 * SSM_GROUP_CH) ** -0.5),
        "ssm_c_re": nrm(ks[12], (DEPTH, SSM_GROUPS, SSM_GROUP_CH, SSM_STATE), (2 * SSM_STATE) ** -0.5),
        "ssm_c_im": nrm(ks[13], (DEPTH, SSM_GROUPS, SSM_GROUP_CH, SSM_STATE), (2 * SSM_STATE) ** -0.5),
        "ssm_d": nrm(ks[14], (DEPTH, SSM_WIDTH), 1.0),
        "w_glu": nrm(ks[15], (DEPTH, SSM_WIDTH, 2 * D_MODEL), SSM_WIDTH ** -0.5),
        "attn_branch_norm": gain(ks[16], (DEPTH, D_MODEL)),
        "ssm_branch_norm": gain(ks[17], (DEPTH, D_MODEL)),
        "w_out": nrm(ks[18], (DEPTH, D_MODEL, D_MODEL), D_MODEL ** -0.5),
        "norm_ffn": gain(ks[19], (DEPTH, D_MODEL)),
        "w_ffn_in": nrm(ks[20], (DEPTH, D_MODEL, 2 * D_FF), D_MODEL ** -0.5),
        "w_ffn_out": nrm(ks[21], (DEPTH, D_FF, D_MODEL), D_FF ** -0.5),
    }


def _fwd_reference(x, meta_tokens, norm_mix, w_in, q_norm, k_norm, attn_sinks, lam_re, lam_im, log_dt,
              ssm_b_re, ssm_b_im, ssm_c_re, ssm_c_im, ssm_d, w_glu, attn_branch_norm, ssm_branch_norm,
              w_out, norm_ffn, w_ffn_in, w_ffn_out):
    b = x.shape[0]
    meta = jnp.broadcast_to(meta_tokens.astype(x.dtype)[None], (b, N_META, D_MODEL))
    h = jnp.concatenate([meta, x], axis=1)
    L = h.shape[1]
    offs = np.cumsum([Q_W, KV_W, KV_W, SSM_WIDTH, D_MODEL]).tolist()
    for i in range(DEPTH):
        xn = rmsnorm(h, norm_mix[i])
        proj = xn @ w_in[i]
        q, k, v, u, g_att, g_ssm = jnp.split(proj, offs, axis=-1)
        q = rmsnorm(q.reshape(b, L, N_Q_HEADS, HEAD_DIM), q_norm[i])
        k = rmsnorm(k.reshape(b, L, N_KV_HEADS, HEAD_DIM), k_norm[i])
        v = v.reshape(b, L, N_KV_HEADS, HEAD_DIM)
        attn = sliding_window_attention(q, k, v, attn_sinks[i])
        y = s5_ssm(u, lam_re[i], lam_im[i], log_dt[i], ssm_b_re[i], ssm_b_im[i],
                   ssm_c_re[i], ssm_c_im[i], ssm_d[i])
        z = jax.nn.gelu(y).astype(h.dtype)
        za, zb = jnp.split(z @ w_glu[i], 2, axis=-1)
        ssm = za * jax.nn.sigmoid(zb)
        merged = (jax.nn.sigmoid(g_att) * rmsnorm(attn, attn_branch_norm[i])
                  + jax.nn.sigmoid(g_ssm) * rmsnorm(ssm, ssm_branch_norm[i]))
        h = h + merged @ w_out[i]
        hn = rmsnorm(h, norm_ffn[i])
        gate, up = jnp.split(hn @ w_ffn_in[i], 2, axis=-1)
        h = h + (jax.nn.silu(gate) * up) @ w_ffn_out[i]
    return h[:, N_META:]


import jax as _jax
import jax.numpy as _jnp

TWIN_FORMAT = 'train_step'
FWD_PARAMS = ['x', 'meta_tokens', 'norm_mix', 'w_in', 'q_norm', 'k_norm', 'attn_sinks', 'lam_re', 'lam_im', 'log_dt', 'ssm_b_re', 'ssm_b_im', 'ssm_c_re', 'ssm_c_im', 'ssm_d', 'w_glu', 'attn_branch_norm', 'ssm_branch_norm', 'w_out', 'norm_ffn', 'w_ffn_in', 'w_ffn_out']
TWIN_WEIGHTS = ['meta_tokens', 'norm_mix', 'w_in', 'q_norm', 'k_norm', 'attn_sinks', 'lam_re', 'lam_im', 'log_dt', 'ssm_b_re', 'ssm_b_im', 'ssm_c_re', 'ssm_c_im', 'ssm_d', 'w_glu', 'attn_branch_norm', 'ssm_branch_norm', 'w_out', 'norm_ffn', 'w_ffn_in', 'w_ffn_out']
TWIN_DIFF_INPUT = 'x'
TWIN_INPUTS = ['x', 'meta_tokens', 'norm_mix', 'w_in', 'q_norm', 'k_norm', 'attn_sinks', 'lam_re', 'lam_im', 'log_dt', 'ssm_b_re', 'ssm_b_im', 'ssm_c_re', 'ssm_c_im', 'ssm_d', 'w_glu', 'attn_branch_norm', 'ssm_branch_norm', 'w_out', 'norm_ffn', 'w_ffn_in', 'w_ffn_out', 'loss_target', 'm_meta_tokens', 'm_norm_mix', 'm_w_in', 'm_q_norm', 'm_k_norm', 'm_attn_sinks', 'm_lam_re', 'm_lam_im', 'm_log_dt', 'm_ssm_b_re', 'm_ssm_b_im', 'm_ssm_c_re', 'm_ssm_c_im', 'm_ssm_d', 'm_w_glu', 'm_attn_branch_norm', 'm_ssm_branch_norm', 'm_w_out', 'm_norm_ffn', 'm_w_ffn_in', 'm_w_ffn_out', 'v_meta_tokens', 'v_norm_mix', 'v_w_in', 'v_q_norm', 'v_k_norm', 'v_attn_sinks', 'v_lam_re', 'v_lam_im', 'v_log_dt', 'v_ssm_b_re', 'v_ssm_b_im', 'v_ssm_c_re', 'v_ssm_c_im', 'v_ssm_d', 'v_w_glu', 'v_attn_branch_norm', 'v_ssm_branch_norm', 'v_w_out', 'v_norm_ffn', 'v_w_ffn_in', 'v_w_ffn_out']
TWIN_OUTPUTS = ['loss', 'grad_x', 'grad_meta_tokens', 'grad_norm_mix', 'grad_w_in', 'grad_q_norm', 'grad_k_norm', 'grad_attn_sinks', 'grad_lam_re', 'grad_lam_im', 'grad_log_dt', 'grad_ssm_b_re', 'grad_ssm_b_im', 'grad_ssm_c_re', 'grad_ssm_c_im', 'grad_ssm_d', 'grad_w_glu', 'grad_attn_branch_norm', 'grad_ssm_branch_norm', 'grad_w_out', 'grad_norm_ffn', 'grad_w_ffn_in', 'grad_w_ffn_out', 'delta_meta_tokens', 'delta_norm_mix', 'delta_w_in', 'delta_q_norm', 'delta_k_norm', 'delta_attn_sinks', 'delta_lam_re', 'delta_lam_im', 'delta_log_dt', 'delta_ssm_b_re', 'delta_ssm_b_im', 'delta_ssm_c_re', 'delta_ssm_c_im', 'delta_ssm_d', 'delta_w_glu', 'delta_attn_branch_norm', 'delta_ssm_branch_norm', 'delta_w_out', 'delta_norm_ffn', 'delta_w_ffn_in', 'delta_w_ffn_out', 'new_m_meta_tokens', 'new_m_norm_mix', 'new_m_w_in', 'new_m_q_norm', 'new_m_k_norm', 'new_m_attn_sinks', 'new_m_lam_re', 'new_m_lam_im', 'new_m_log_dt', 'new_m_ssm_b_re', 'new_m_ssm_b_im', 'new_m_ssm_c_re', 'new_m_ssm_c_im', 'new_m_ssm_d', 'new_m_w_glu', 'new_m_attn_branch_norm', 'new_m_ssm_branch_norm', 'new_m_w_out', 'new_m_norm_ffn', 'new_m_w_ffn_in', 'new_m_w_ffn_out', 'new_v_meta_tokens', 'new_v_norm_mix', 'new_v_w_in', 'new_v_q_norm', 'new_v_k_norm', 'new_v_attn_sinks', 'new_v_lam_re', 'new_v_lam_im', 'new_v_log_dt', 'new_v_ssm_b_re', 'new_v_ssm_b_im', 'new_v_ssm_c_re', 'new_v_ssm_c_im', 'new_v_ssm_d', 'new_v_w_glu', 'new_v_attn_branch_norm', 'new_v_ssm_branch_norm', 'new_v_w_out', 'new_v_norm_ffn', 'new_v_w_ffn_in', 'new_v_w_ffn_out']
TWIN_LEAF_KINDS = {'loss': 'loss', 'grad_x': 'grad_x', 'grad_meta_tokens': 'grad_w', 'grad_norm_mix': 'grad_w', 'grad_w_in': 'grad_w', 'grad_q_norm': 'grad_w', 'grad_k_norm': 'grad_w', 'grad_attn_sinks': 'grad_w', 'grad_lam_re': 'grad_w', 'grad_lam_im': 'grad_w', 'grad_log_dt': 'grad_w', 'grad_ssm_b_re': 'grad_w', 'grad_ssm_b_im': 'grad_w', 'grad_ssm_c_re': 'grad_w', 'grad_ssm_c_im': 'grad_w', 'grad_ssm_d': 'grad_w', 'grad_w_glu': 'grad_w', 'grad_attn_branch_norm': 'grad_w', 'grad_ssm_branch_norm': 'grad_w', 'grad_w_out': 'grad_w', 'grad_norm_ffn': 'grad_w', 'grad_w_ffn_in': 'grad_w', 'grad_w_ffn_out': 'grad_w', 'delta_meta_tokens': 'delta_w', 'delta_norm_mix': 'delta_w', 'delta_w_in': 'delta_w', 'delta_q_norm': 'delta_w', 'delta_k_norm': 'delta_w', 'delta_attn_sinks': 'delta_w', 'delta_lam_re': 'delta_w', 'delta_lam_im': 'delta_w', 'delta_log_dt': 'delta_w', 'delta_ssm_b_re': 'delta_w', 'delta_ssm_b_im': 'delta_w', 'delta_ssm_c_re': 'delta_w', 'delta_ssm_c_im': 'delta_w', 'delta_ssm_d': 'delta_w', 'delta_w_glu': 'delta_w', 'delta_attn_branch_norm': 'delta_w', 'delta_ssm_branch_norm': 'delta_w', 'delta_w_out': 'delta_w', 'delta_norm_ffn': 'delta_w', 'delta_w_ffn_in': 'delta_w', 'delta_w_ffn_out': 'delta_w', 'new_m_meta_tokens': 'new_m', 'new_m_norm_mix': 'new_m', 'new_m_w_in': 'new_m', 'new_m_q_norm': 'new_m', 'new_m_k_norm': 'new_m', 'new_m_attn_sinks': 'new_m', 'new_m_lam_re': 'new_m', 'new_m_lam_im': 'new_m', 'new_m_log_dt': 'new_m', 'new_m_ssm_b_re': 'new_m', 'new_m_ssm_b_im': 'new_m', 'new_m_ssm_c_re': 'new_m', 'new_m_ssm_c_im': 'new_m', 'new_m_ssm_d': 'new_m', 'new_m_w_glu': 'new_m', 'new_m_attn_branch_norm': 'new_m', 'new_m_ssm_branch_norm': 'new_m', 'new_m_w_out': 'new_m', 'new_m_norm_ffn': 'new_m', 'new_m_w_ffn_in': 'new_m', 'new_m_w_ffn_out': 'new_m', 'new_v_meta_tokens': 'new_v', 'new_v_norm_mix': 'new_v', 'new_v_w_in': 'new_v', 'new_v_q_norm': 'new_v', 'new_v_k_norm': 'new_v', 'new_v_attn_sinks': 'new_v', 'new_v_lam_re': 'new_v', 'new_v_lam_im': 'new_v', 'new_v_log_dt': 'new_v', 'new_v_ssm_b_re': 'new_v', 'new_v_ssm_b_im': 'new_v', 'new_v_ssm_c_re': 'new_v', 'new_v_ssm_c_im': 'new_v', 'new_v_ssm_d': 'new_v', 'new_v_w_glu': 'new_v', 'new_v_attn_branch_norm': 'new_v', 'new_v_ssm_branch_norm': 'new_v', 'new_v_w_out': 'new_v', 'new_v_norm_ffn': 'new_v', 'new_v_w_ffn_in': 'new_v', 'new_v_w_ffn_out': 'new_v'}


def _forward(args):
    return _fwd_reference(*[args[k] for k in FWD_PARAMS])


def _output_shape():
    def fwd():
        inp = _fwd_setup_inputs(0)
        return _fwd_reference(*[inp[k] for k in FWD_PARAMS])
    out = _jax.eval_shape(fwd)
    return out.shape, out.dtype

N_MICROBATCH = 1
ADAM_LR = 0.001
ADAM_B1 = 0.9
ADAM_B2 = 0.999
ADAM_EPS = 1e-08
ADAM_WD = 0.01
ADAM_STEP = 10
PER_EXAMPLE_BATCH_AXIS = {'x': 0, 'loss_target': 0}
SHARED_INPUTS = []
_WEIGHT_DTYPES = {'meta_tokens': _jnp.float32, 'norm_mix': _jnp.float32, 'w_in': _jnp.float32, 'q_norm': _jnp.float32, 'k_norm': _jnp.float32, 'attn_sinks': _jnp.float32, 'lam_re': _jnp.float32, 'lam_im': _jnp.float32, 'log_dt': _jnp.float32, 'ssm_b_re': _jnp.float32, 'ssm_b_im': _jnp.float32, 'ssm_c_re': _jnp.float32, 'ssm_c_im': _jnp.float32, 'ssm_d': _jnp.float32, 'w_glu': _jnp.float32, 'attn_branch_norm': _jnp.float32, 'ssm_branch_norm': _jnp.float32, 'w_out': _jnp.float32, 'norm_ffn': _jnp.float32, 'w_ffn_in': _jnp.float32, 'w_ffn_out': _jnp.float32}
MOMENT_SCALE = {'meta_tokens': 4.490389e-01, 'norm_mix': 3.859274e+00, 'w_in': 8.835346e-01, 'q_norm': 1.504124e+00, 'k_norm': 1.509811e+00, 'attn_sinks': 7.442460e-02, 'lam_re': 1.910607e-02, 'lam_im': 1.954409e-02, 'log_dt': 1.031607e+01, 'ssm_b_re': 1.312368e-02, 'ssm_b_im': 1.242054e-02, 'ssm_c_re': 2.592787e-02, 'ssm_c_im': 2.595787e-02, 'ssm_d': 4.505993e+00, 'w_glu': 3.022530e+00, 'attn_branch_norm': 1.882206e+01, 'ssm_branch_norm': 2.071387e+01, 'w_out': 4.527086e+00, 'norm_ffn': 4.913330e+01, 'w_ffn_in': 9.615729e-01, 'w_ffn_out': 8.848018e-01}


def _to_microbatches(a, axis):
    t = _jnp.moveaxis(a, axis, 0)
    t = t.reshape((N_MICROBATCH, t.shape[0] // N_MICROBATCH) + t.shape[1:])
    return _jnp.moveaxis(t, 1, axis + 1)


def setup_inputs(seed: int = 0) -> dict:
    inp = _fwd_setup_inputs(seed)
    key = _jax.random.fold_in(_jax.random.key(seed), 7919)
    shape, _ = _output_shape()
    out = dict(inp)
    out["loss_target"] = _jax.random.normal(_jax.random.fold_in(key, 0), shape, _jnp.float32)
    for i, name in enumerate(TWIN_WEIGHTS):
        w = inp[name].astype(_jnp.float32)
        if MOMENT_SCALE is None:
            s = _jnp.sqrt(_jnp.mean(_jnp.square(w)) + 1e-30)
        else:
            s = MOMENT_SCALE[name]
        km, kv = _jax.random.split(_jax.random.fold_in(key, i + 1))
        out[name] = w
        out["m_" + name] = s * _jax.random.normal(km, w.shape, _jnp.float32)
        out["v_" + name] = (s * s) * _jax.random.uniform(kv, w.shape, _jnp.float32, 0.5, 1.5)
    if N_MICROBATCH > 1:
        for name, axis in PER_EXAMPLE_BATCH_AXIS.items():
            out[name] = _to_microbatches(out[name], axis)
    return {'x': out['x'], 'meta_tokens': out['meta_tokens'], 'norm_mix': out['norm_mix'], 'w_in': out['w_in'], 'q_norm': out['q_norm'], 'k_norm': out['k_norm'], 'attn_sinks': out['attn_sinks'], 'lam_re': out['lam_re'], 'lam_im': out['lam_im'], 'log_dt': out['log_dt'], 'ssm_b_re': out['ssm_b_re'], 'ssm_b_im': out['ssm_b_im'], 'ssm_c_re': out['ssm_c_re'], 'ssm_c_im': out['ssm_c_im'], 'ssm_d': out['ssm_d'], 'w_glu': out['w_glu'], 'attn_branch_norm': out['attn_branch_norm'], 'ssm_branch_norm': out['ssm_branch_norm'], 'w_out': out['w_out'], 'norm_ffn': out['norm_ffn'], 'w_ffn_in': out['w_ffn_in'], 'w_ffn_out': out['w_ffn_out'], 'loss_target': out['loss_target'], 'm_meta_tokens': out['m_meta_tokens'], 'm_norm_mix': out['m_norm_mix'], 'm_w_in': out['m_w_in'], 'm_q_norm': out['m_q_norm'], 'm_k_norm': out['m_k_norm'], 'm_attn_sinks': out['m_attn_sinks'], 'm_lam_re': out['m_lam_re'], 'm_lam_im': out['m_lam_im'], 'm_log_dt': out['m_log_dt'], 'm_ssm_b_re': out['m_ssm_b_re'], 'm_ssm_b_im': out['m_ssm_b_im'], 'm_ssm_c_re': out['m_ssm_c_re'], 'm_ssm_c_im': out['m_ssm_c_im'], 'm_ssm_d': out['m_ssm_d'], 'm_w_glu': out['m_w_glu'], 'm_attn_branch_norm': out['m_attn_branch_norm'], 'm_ssm_branch_norm': out['m_ssm_branch_norm'], 'm_w_out': out['m_w_out'], 'm_norm_ffn': out['m_norm_ffn'], 'm_w_ffn_in': out['m_w_ffn_in'], 'm_w_ffn_out': out['m_w_ffn_out'], 'v_meta_tokens': out['v_meta_tokens'], 'v_norm_mix': out['v_norm_mix'], 'v_w_in': out['v_w_in'], 'v_q_norm': out['v_q_norm'], 'v_k_norm': out['v_k_norm'], 'v_attn_sinks': out['v_attn_sinks'], 'v_lam_re': out['v_lam_re'], 'v_lam_im': out['v_lam_im'], 'v_log_dt': out['v_log_dt'], 'v_ssm_b_re': out['v_ssm_b_re'], 'v_ssm_b_im': out['v_ssm_b_im'], 'v_ssm_c_re': out['v_ssm_c_re'], 'v_ssm_c_im': out['v_ssm_c_im'], 'v_ssm_d': out['v_ssm_d'], 'v_w_glu': out['v_w_glu'], 'v_attn_branch_norm': out['v_attn_branch_norm'], 'v_ssm_branch_norm': out['v_ssm_branch_norm'], 'v_w_out': out['v_w_out'], 'v_norm_ffn': out['v_norm_ffn'], 'v_w_ffn_in': out['v_w_ffn_in'], 'v_w_ffn_out': out['v_w_ffn_out']}


def _loss(weights, diff, rest, loss_target):
    with _jax.named_scope("forward"):
        args = {**rest, TWIN_DIFF_INPUT: diff, **{k: w.astype(_WEIGHT_DTYPES[k]) for k, w in weights.items()}}
        y = _forward(args)
    with _jax.named_scope("loss_head"):
        err = _jnp.square(y.astype(_jnp.float32) - loss_target)
        return 0.5 * _jnp.sum(_jnp.mean(err, axis=-1)) if err.ndim else 0.5 * err


def _adamw(w, g, m, v):
    m = ADAM_B1 * m + (1.0 - ADAM_B1) * g
    v = ADAM_B2 * v + (1.0 - ADAM_B2) * _jnp.square(g)
    m_hat = m / (1.0 - ADAM_B1 ** ADAM_STEP)
    v_hat = v / (1.0 - ADAM_B2 ** ADAM_STEP)
    delta = -ADAM_LR * (m_hat / (_jnp.sqrt(v_hat) + ADAM_EPS) + ADAM_WD * w)
    return delta, m, v


def reference(x, meta_tokens, norm_mix, w_in, q_norm, k_norm, attn_sinks, lam_re, lam_im, log_dt, ssm_b_re, ssm_b_im, ssm_c_re, ssm_c_im, ssm_d, w_glu, attn_branch_norm, ssm_branch_norm, w_out, norm_ffn, w_ffn_in, w_ffn_out, loss_target, m_meta_tokens, m_norm_mix, m_w_in, m_q_norm, m_k_norm, m_attn_sinks, m_lam_re, m_lam_im, m_log_dt, m_ssm_b_re, m_ssm_b_im, m_ssm_c_re, m_ssm_c_im, m_ssm_d, m_w_glu, m_attn_branch_norm, m_ssm_branch_norm, m_w_out, m_norm_ffn, m_w_ffn_in, m_w_ffn_out, v_meta_tokens, v_norm_mix, v_w_in, v_q_norm, v_k_norm, v_attn_sinks, v_lam_re, v_lam_im, v_log_dt, v_ssm_b_re, v_ssm_b_im, v_ssm_c_re, v_ssm_c_im, v_ssm_d, v_w_glu, v_attn_branch_norm, v_ssm_branch_norm, v_w_out, v_norm_ffn, v_w_ffn_in, v_w_ffn_out):
    given = dict(x=x, meta_tokens=meta_tokens, norm_mix=norm_mix, w_in=w_in, q_norm=q_norm, k_norm=k_norm, attn_sinks=attn_sinks, lam_re=lam_re, lam_im=lam_im, log_dt=log_dt, ssm_b_re=ssm_b_re, ssm_b_im=ssm_b_im, ssm_c_re=ssm_c_re, ssm_c_im=ssm_c_im, ssm_d=ssm_d, w_glu=w_glu, attn_branch_norm=attn_branch_norm, ssm_branch_norm=ssm_branch_norm, w_out=w_out, norm_ffn=norm_ffn, w_ffn_in=w_ffn_in, w_ffn_out=w_ffn_out, loss_target=loss_target, m_meta_tokens=m_meta_tokens, m_norm_mix=m_norm_mix, m_w_in=m_w_in, m_q_norm=m_q_norm, m_k_norm=m_k_norm, m_attn_sinks=m_attn_sinks, m_lam_re=m_lam_re, m_lam_im=m_lam_im, m_log_dt=m_log_dt, m_ssm_b_re=m_ssm_b_re, m_ssm_b_im=m_ssm_b_im, m_ssm_c_re=m_ssm_c_re, m_ssm_c_im=m_ssm_c_im, m_ssm_d=m_ssm_d, m_w_glu=m_w_glu, m_attn_branch_norm=m_attn_branch_norm, m_ssm_branch_norm=m_ssm_branch_norm, m_w_out=m_w_out, m_norm_ffn=m_norm_ffn, m_w_ffn_in=m_w_ffn_in, m_w_ffn_out=m_w_ffn_out, v_meta_tokens=v_meta_tokens, v_norm_mix=v_norm_mix, v_w_in=v_w_in, v_q_norm=v_q_norm, v_k_norm=v_k_norm, v_attn_sinks=v_attn_sinks, v_lam_re=v_lam_re, v_lam_im=v_lam_im, v_log_dt=v_log_dt, v_ssm_b_re=v_ssm_b_re, v_ssm_b_im=v_ssm_b_im, v_ssm_c_re=v_ssm_c_re, v_ssm_c_im=v_ssm_c_im, v_ssm_d=v_ssm_d, v_w_glu=v_w_glu, v_attn_branch_norm=v_attn_branch_norm, v_ssm_branch_norm=v_ssm_branch_norm, v_w_out=v_w_out, v_norm_ffn=v_norm_ffn, v_w_ffn_in=v_w_ffn_in, v_w_ffn_out=v_w_ffn_out)
    weights = {n: given[n] for n in TWIN_WEIGHTS}
    shared = {n: given[n] for n in SHARED_INPUTS}
    per_example = {n: given[n] for n in ['x']}
    grad_fn = _jax.value_and_grad(_loss, argnums=(0, 1))

    def one_microbatch(ex, loss_target):
        ex = dict(ex)
        diff = ex.pop(TWIN_DIFF_INPUT)
        return grad_fn(weights, diff, {**shared, **ex}, loss_target)

    if N_MICROBATCH == 1:
        loss, (grad_w, grad_x) = one_microbatch(per_example, given["loss_target"])
    else:
        def body(carry, xs):
            loss_sum, grad_sum = carry
            l_k, (gw_k, gx_k) = one_microbatch(xs[0], xs[1])
            with _jax.named_scope("update"):
                return (loss_sum + l_k, _jax.tree.map(_jnp.add, grad_sum, gw_k)), gx_k

        init = (_jnp.zeros((), _jnp.float32), _jax.tree.map(_jnp.zeros_like, weights))
        (loss, grad_w), grad_x = _jax.lax.scan(body, init, (per_example, given["loss_target"]))
    with _jax.named_scope("update"):
        delta_w, new_m, new_v = {}, {}, {}
        for n in TWIN_WEIGHTS:
            delta_w[n], new_m[n], new_v[n] = _adamw(weights[n], grad_w[n], given["m_" + n], given["v_" + n])
    return (loss, grad_x, *[grad_w[n] for n in TWIN_WEIGHTS], *[delta_w[n] for n in TWIN_WEIGHTS],
            *[new_m[n] for n in TWIN_WEIGHTS], *[new_v[n] for n in TWIN_WEIGHTS])
```

```python
import math

import jax
import jax.numpy as jnp
from jax import lax
from jax.experimental import pallas as pl
from jax.experimental.pallas import tpu as pltpu

F32 = jnp.float32
BF16 = jnp.bfloat16

D_MODEL = 1024
N_META = 16
HEAD_DIM = 64
N_Q_HEADS = 16
N_KV_HEADS = 4
Q_W = N_Q_HEADS * HEAD_DIM
KV_W = N_KV_HEADS * HEAD_DIM
SSM_GROUPS = 64
SSM_GROUP_CH = 16
SSM_STATE = 64
N_STATE = SSM_GROUPS * SSM_STATE
D_FF = 2816
IN_COLS = Q_W + 2 * KV_W + 3 * D_MODEL
EPS = 1e-6
BLK = 128
PAD = BLK - N_META
N_DEV = 8
NEG = -1e30
SSM_KB = 8
ST_KB = N_STATE // SSM_KB

ADAM_LR = 0.001
ADAM_B1 = 0.9
ADAM_B2 = 0.999
ADAM_EPS = 1e-08
ADAM_WD = 0.01
ADAM_STEP = 10

VMEM_LIMIT = 48 * 1024 * 1024
MESH = pl.DeviceIdType.MESH

SMALL = ["norm_mix", "q_norm", "k_norm", "attn_sinks", "lam_re", "lam_im", "log_dt", "ssm_b_re", "ssm_b_im",
         "ssm_c_re", "ssm_c_im", "ssm_d", "attn_branch_norm", "ssm_branch_norm", "norm_ffn"]
WEIGHTS = ["meta_tokens", "norm_mix", "w_in", "q_norm", "k_norm", "attn_sinks", "lam_re", "lam_im", "log_dt",
           "ssm_b_re", "ssm_b_im", "ssm_c_re", "ssm_c_im", "ssm_d", "w_glu", "attn_branch_norm",
           "ssm_branch_norm", "w_out", "norm_ffn", "w_ffn_in", "w_ffn_out"]


def _params(**kw):
    return pltpu.CompilerParams(vmem_limit_bytes=VMEM_LIMIT, **kw)


def _pick(n, cap, mult=16):
    best = None
    for d in range(mult, min(n, cap) + 1, mult):
        if n % d == 0:
            best = d
    assert best is not None, (n, cap, mult)
    return best


def _my_index():
    return 4 * lax.axis_index("x") + 2 * lax.axis_index("y") + lax.axis_index("c")


def _rms(x, g):
    r = lax.rsqrt(jnp.mean(x * x, axis=-1, keepdims=True) + EPS)
    return x * r * g


def _rms_bwd(x, g, dy):
    r = lax.rsqrt(jnp.mean(x * x, axis=-1, keepdims=True) + EPS)
    t = dy * g
    dx = r * t - x * (r * r * r) * jnp.mean(t * x, axis=-1, keepdims=True)
    dg = jnp.sum(dy * (x * r), axis=0, keepdims=True)
    return dx, dg


def _sigmoid(x):
    return 1.0 / (1.0 + jnp.exp(-x))


def _gelu(x):
    k = math.sqrt(2.0 / math.pi)
    return 0.5 * x * (1.0 + jnp.tanh(k * (x + 0.044715 * (x * x * x))))


def _gelu_grad(x):
    k = math.sqrt(2.0 / math.pi)
    t = jnp.tanh(k * (x + 0.044715 * (x * x * x)))
    return 0.5 * (1.0 + t) + 0.5 * x * (1.0 - t * t) * (k * (1.0 + 3.0 * 0.044715 * (x * x)))


def _head_mean(x, e_ref):
    hi = x.astype(BF16)
    r1 = x - hi.astype(F32)
    mid = r1.astype(BF16)
    lo = (r1 - mid.astype(F32)).astype(BF16)
    e = e_ref[...]
    out = []
    for b in range(x.shape[1] // 256):
        sl = slice(256 * b, 256 * b + 256)
        s = (jnp.dot(hi[:, sl], e, preferred_element_type=F32)
             + jnp.dot(mid[:, sl], e, preferred_element_type=F32)
             + jnp.dot(lo[:, sl], e, preferred_element_type=F32))
        out.append(s)
    s = out[0] if len(out) == 1 else jnp.concatenate(out, axis=1)
    return s * (1.0 / HEAD_DIM)


def _head_rms(x, g, e_ref):
    r = lax.rsqrt(_head_mean(x * x, e_ref) + EPS)
    return x * r * g


def _head_rms_bwd(x, g, dy, e_ref):
    r = lax.rsqrt(_head_mean(x * x, e_ref) + EPS)
    t = dy * g
    dx = r * t - x * (r * r * r) * _head_mean(t * x, e_ref)
    dg = jnp.sum(dy * (x * r), axis=0, keepdims=True)
    return dx, dg


def _lane_half(shape):
    lane = lax.broadcasted_iota(jnp.int32, shape, len(shape) - 1)
    return (lane >> 6) & 1


def _matmul(name, a, w, *, nt, tm, tn, tk, n=None, w_off=0, res=None, out_dtype=F32):
    m_dim, k_dim = a.shape
    n_dim = n if n is not None else (w.shape[0] if nt else w.shape[1])
    gm, gn, gk = m_dim // tm, n_dim // tn, k_dim // tk
    assert gm * tm == m_dim and gn * tn == n_dim and gk * tk == k_dim, (name, a.shape, w.shape, tm, tn, tk)
    dn = (((1,), (1,)), ((), ())) if nt else (((1,), (0,)), ((), ()))

    def body(*refs):
        if res is None:
            a_ref, w_ref, o_ref, acc = refs
            r_ref = None
        else:
            a_ref, w_ref, r_ref, o_ref, acc = refs
        k = pl.program_id(2)

        @pl.when(k == 0)
        def _():
            acc[...] = jnp.zeros_like(acc)

        acc[...] += lax.dot_general(a_ref[...], w_ref[...], dn, preferred_element_type=F32)

        @pl.when(k == gk - 1)
        def _():
            r = acc[...]
            if r_ref is not None:
                r = r_ref[...] + r
            o_ref[...] = r.astype(out_dtype)

    if nt:
        w_spec = pl.BlockSpec((tn, tk), lambda i, j, k: (j + w_off, k))
    else:
        w_spec = pl.BlockSpec((tk, tn), lambda i, j, k: (k, j))
    in_specs = [pl.BlockSpec((tm, tk), lambda i, j, k: (i, k)), w_spec]
    args = [a, w]
    if res is not None:
        in_specs.append(pl.BlockSpec((tm, tn), lambda i, j, k: (i, j)))
        args.append(res)
    return pl.pallas_call(
        body, name=name, grid=(gm, gn, gk),
        in_specs=in_specs,
        out_specs=pl.BlockSpec((tm, tn), lambda i, j, k: (i, j)),
        out_shape=jax.ShapeDtypeStruct((m_dim, n_dim), out_dtype),
        scratch_shapes=[pltpu.VMEM((tm, tn), F32)],
        compiler_params=_params(dimension_semantics=("parallel", "parallel", "arbitrary")),
    )(*args)


def _matmul_tn(name, a, b, *, tm, tn, tl):
    l_dim, m_dim = a.shape
    n_dim = b.shape[1]
    gm, gn, gl = m_dim // tm, n_dim // tn, l_dim // tl
    assert gm * tm == m_dim and gn * tn == n_dim and gl * tl == l_dim, (name, a.shape, b.shape, tm, tn, tl)

    def body(a_ref, b_ref, o_ref):
        @pl.when(pl.program_id(2) == 0)
        def _():
            o_ref[...] = jnp.zeros_like(o_ref)

        o_ref[...] += lax.dot_general(a_ref[...], b_ref[...], (((0,), (0,)), ((), ())),
                                      preferred_element_type=F32)

    return pl.pallas_call(
        body, name=name, grid=(gm, gn, gl),
        in_specs=[pl.BlockSpec((tl, tm), lambda i, j, l: (l, i)),
                  pl.BlockSpec((tl, tn), lambda i, j, l: (l, j))],
        out_specs=pl.BlockSpec((tm, tn), lambda i, j, l: (i, j)),
        out_shape=jax.ShapeDtypeStruct((m_dim, n_dim), F32),
        compiler_params=_params(dimension_semantics=("parallel", "parallel", "arbitrary")),
    )(a, b)


def _row_spec(tm, cols, f=None):
    if f is None:
        return pl.BlockSpec((tm, cols), lambda i: (i, 0))
    return pl.BlockSpec((tm, cols), lambda i: (f(i), 0))


def _full_spec(shape):
    nd = len(shape)
    return pl.BlockSpec(shape, lambda i: (0,) * nd)


def _embed_norm(x2d, meta_pad, g):
    s_len = x2d.shape[0]
    nb = s_len // BLK + 1

    def body(x_ref, mp_ref, g_ref, h_ref, xn_ref):
        i = pl.program_id(0)

        @pl.when(i == 0)
        def _():
            h_ref[...] = mp_ref[...]

        @pl.when(i > 0)
        def _():
            h_ref[...] = x_ref[...]

        xn_ref[...] = _rms(h_ref[...], g_ref[...]).astype(BF16)

    return pl.pallas_call(
        body, name="embed_norm", grid=(nb,),
        in_specs=[_row_spec(BLK, D_MODEL, lambda i: jnp.maximum(i - 1, 0)),
                  _full_spec((BLK, D_MODEL)), _full_spec((1, D_MODEL))],
        out_specs=[_row_spec(BLK, D_MODEL), _row_spec(BLK, D_MODEL)],
        out_shape=[jax.ShapeDtypeStruct((nb * BLK, D_MODEL), F32),
                   jax.ShapeDtypeStruct((nb * BLK, D_MODEL), BF16)],
        compiler_params=_params(),
    )(x2d, meta_pad, g)


def _qk_prep(qkv, q_norm_t, k_norm_t, e_mat, tm):
    l_dim = qkv.shape[0]

    def body(x_ref, qg_ref, kg_ref, e_ref, q_ref, kf_ref, vf_ref):
        x = x_ref[...]
        q = _head_rms(x[:, :Q_W], qg_ref[...], e_ref) * (HEAD_DIM ** -0.5)
        q_ref[...] = q.astype(BF16)
        k = _head_rms(x[:, Q_W:Q_W + KV_W], kg_ref[...], e_ref)
        v = x[:, Q_W + KV_W:Q_W + 2 * KV_W]
        half = _lane_half((tm, BLK))
        for src, dst in ((k, kf_ref), (v, vf_ref)):
            for kv in range(N_KV_HEADS):
                blk = src[:, BLK * (kv // 2):BLK * (kv // 2) + BLK]
                swapped = pltpu.roll(blk, HEAD_DIM, axis=1)
                for e in range(2):
                    val = blk if kv % 2 == e else swapped
                    idx = 2 * kv + e
                    dst[:, BLK * idx:BLK * idx + BLK] = jnp.where(half == e, val, 0.0).astype(BF16)

    return pl.pallas_call(
        body, name="qk_prep", grid=(l_dim // tm,),
        in_specs=[_row_spec(tm, Q_W + 2 * KV_W), _full_spec((1, Q_W)), _full_spec((1, KV_W)),
                  _full_spec((256, 256))],
        out_specs=[_row_spec(tm, Q_W), _row_spec(tm, 8 * BLK), _row_spec(tm, 8 * BLK)],
        out_shape=[jax.ShapeDtypeStruct((l_dim, Q_W), BF16),
                   jax.ShapeDtypeStruct((l_dim, 8 * BLK), BF16),
                   jax.ShapeDtypeStruct((l_dim, 8 * BLK), BF16)],
        compiler_params=_params(),
    )(qkv, q_norm_t, k_norm_t, e_mat)


def _merge_fwd(attn, zab, gates, abn, sbn, tm):
    l_dim = attn.shape[0]

    def body(a_ref, z_ref, g_ref, an_ref, sn_ref, o_ref):
        z = z_ref[...]
        g = g_ref[...]
        ssm = z[:, :D_MODEL] * _sigmoid(z[:, D_MODEL:])
        merged = (_sigmoid(g[:, :D_MODEL]) * _rms(a_ref[...], an_ref[...])
                  + _sigmoid(g[:, D_MODEL:]) * _rms(ssm, sn_ref[...]))
        o_ref[...] = merged.astype(BF16)

    return pl.pallas_call(
        body, name="merge_fwd", grid=(l_dim // tm,),
        in_specs=[_row_spec(tm, D_MODEL), _row_spec(tm, 2 * D_MODEL), _row_spec(tm, 2 * D_MODEL),
                  _full_spec((1, D_MODEL)), _full_spec((1, D_MODEL))],
        out_specs=_row_spec(tm, D_MODEL),
        out_shape=jax.ShapeDtypeStruct((l_dim, D_MODEL), BF16),
        compiler_params=_params(),
    )(attn, zab, gates, abn, sbn)


def _merge_bwd(attn, zab, gates, abn, sbn, dmerged, tm):
    l_dim = attn.shape[0]

    def body(a_ref, z_ref, g_ref, an_ref, sn_ref, dm_ref, da_ref, dz_ref, dg_ref, dan_ref, dsn_ref):
        @pl.when(pl.program_id(0) == 0)
        def _():
            dan_ref[...] = jnp.zeros_like(dan_ref)
            dsn_ref[...] = jnp.zeros_like(dsn_ref)

        z = z_ref[...]
        g = g_ref[...]
        dm = dm_ref[...]
        attn_v = a_ref[...]
        za, zb = z[:, :D_MODEL], z[:, D_MODEL:]
        sb = _sigmoid(zb)
        ssm = za * sb
        s_ga, s_gs = _sigmoid(g[:, :D_MODEL]), _sigmoid(g[:, D_MODEL:])
        a_n = _rms(attn_v, an_ref[...])
        s_n = _rms(ssm, sn_ref[...])
        dg_ref[:, :D_MODEL] = (dm * a_n * s_ga * (1.0 - s_ga)).astype(BF16)
        dg_ref[:, D_MODEL:] = (dm * s_n * s_gs * (1.0 - s_gs)).astype(BF16)
        dattn, dan = _rms_bwd(attn_v, an_ref[...], dm * s_ga)
        dssm, dsn = _rms_bwd(ssm, sn_ref[...], dm * s_gs)
        da_ref[...] = dattn
        dz_ref[:, :D_MODEL] = (dssm * sb).astype(BF16)
        dz_ref[:, D_MODEL:] = (dssm * za * sb * (1.0 - sb)).astype(BF16)
        dan_ref[...] += dan
        dsn_ref[...] += dsn

    return pl.pallas_call(
        body, name="merge_bwd", grid=(l_dim // tm,),
        in_specs=[_row_spec(tm, D_MODEL), _row_spec(tm, 2 * D_MODEL), _row_spec(tm, 2 * D_MODEL),
                  _full_spec((1, D_MODEL)), _full_spec((1, D_MODEL)), _row_spec(tm, D_MODEL)],
        out_specs=[_row_spec(tm, D_MODEL), _row_spec(tm, 2 * D_MODEL), _row_spec(tm, 2 * D_MODEL),
                   _full_spec((1, D_MODEL)), _full_spec((1, D_MODEL))],
        out_shape=[jax.ShapeDtypeStruct((l_dim, D_MODEL), F32),
                   jax.ShapeDtypeStruct((l_dim, 2 * D_MODEL), BF16),
                   jax.ShapeDtypeStruct((l_dim, 2 * D_MODEL), BF16),
                   jax.ShapeDtypeStruct((1, D_MODEL), F32), jax.ShapeDtypeStruct((1, D_MODEL), F32)],
        compiler_params=_params(),
    )(attn, zab, gates, abn, sbn, dmerged)


def _norm_cast(name, h, g, tm):
    l_dim = h.shape[0]

    def body(h_ref, g_ref, o_ref):
        o_ref[...] = _rms(h_ref[...], g_ref[...]).astype(BF16)

    return pl.pallas_call(
        body, name=name, grid=(l_dim // tm,),
        in_specs=[_row_spec(tm, D_MODEL), _full_spec((1, D_MODEL))],
        out_specs=_row_spec(tm, D_MODEL),
        out_shape=jax.ShapeDtypeStruct((l_dim, D_MODEL), BF16),
        compiler_params=_params(),
    )(h, g)


def _swiglu_fwd(gu, tm):
    l_dim = gu.shape[0]

    def body(x_ref, o_ref):
        x = x_ref[...]
        gate, up = x[:, :D_FF], x[:, D_FF:]
        o_ref[...] = (gate * _sigmoid(gate) * up).astype(BF16)

    return pl.pallas_call(
        body, name="swiglu_fwd", grid=(l_dim // tm,),
        in_specs=[_row_spec(tm, 2 * D_FF)],
        out_specs=_row_spec(tm, D_FF),
        out_shape=jax.ShapeDtypeStruct((l_dim, D_FF), BF16),
        compiler_params=_params(),
    )(gu)


def _swiglu_bwd(gu, dact, tm):
    l_dim = gu.shape[0]

    def body(x_ref, d_ref, o_ref):
        x = x_ref[...]
        d = d_ref[...]
        gate, up = x[:, :D_FF], x[:, D_FF:]
        s = _sigmoid(gate)
        o_ref[:, :D_FF] = (d * up * (s * (1.0 + gate * (1.0 - s)))).astype(BF16)
        o_ref[:, D_FF:] = (d * (gate * s)).astype(BF16)

    return pl.pallas_call(
        body, name="swiglu_bwd", grid=(l_dim // tm,),
        in_specs=[_row_spec(tm, 2 * D_FF), _row_spec(tm, D_FF)],
        out_specs=_row_spec(tm, 2 * D_FF),
        out_shape=jax.ShapeDtypeStruct((l_dim, 2 * D_FF), BF16),
        compiler_params=_params(),
    )(gu, dact)


def _loss_grad(h2, target2d):
    l_dim = h2.shape[0]
    nb = l_dim // BLK

    def body(h_ref, t_ref, d_ref, db_ref, loss_ref):
        i = pl.program_id(0)

        @pl.when(i == 0)
        def _():
            loss_ref[...] = jnp.zeros_like(loss_ref)
            d_ref[...] = jnp.zeros_like(d_ref)
            db_ref[...] = jnp.zeros_like(db_ref)

        @pl.when(i > 0)
        def _():
            err = h_ref[...] - t_ref[...]
            d = err * (1.0 / D_MODEL)
            d_ref[...] = d
            db_ref[...] = d.astype(BF16)
            loss_ref[...] += 0.5 * jnp.sum(jnp.mean(err * err, axis=-1, keepdims=True), axis=0, keepdims=True)

    return pl.pallas_call(
        body, name="loss_grad", grid=(nb,),
        in_specs=[_row_spec(BLK, D_MODEL), _row_spec(BLK, D_MODEL, lambda i: jnp.maximum(i - 1, 0))],
        out_specs=[_row_spec(BLK, D_MODEL), _row_spec(BLK, D_MODEL), _full_spec((1, 1))],
        out_shape=[jax.ShapeDtypeStruct((l_dim, D_MODEL), F32), jax.ShapeDtypeStruct((l_dim, D_MODEL), BF16),
                   jax.ShapeDtypeStruct((1, 1), F32)],
        compiler_params=_params(),
    )(h2, target2d)


def _norm_bwd_res(name, h, g, dy, dres, tm):
    l_dim = h.shape[0]

    def body(h_ref, g_ref, dy_ref, dr_ref, o_ref, ob_ref, dg_ref):
        @pl.when(pl.program_id(0) == 0)
        def _():
            dg_ref[...] = jnp.zeros_like(dg_ref)

        dx, dg = _rms_bwd(h_ref[...], g_ref[...], dy_ref[...])
        out = dr_ref[...] + dx
        o_ref[...] = out
        ob_ref[...] = out.astype(BF16)
        dg_ref[...] += dg

    return pl.pallas_call(
        body, name=name, grid=(l_dim // tm,),
        in_specs=[_row_spec(tm, D_MODEL), _full_spec((1, D_MODEL)), _row_spec(tm, D_MODEL), _row_spec(tm, D_MODEL)],
        out_specs=[_row_spec(tm, D_MODEL), _row_spec(tm, D_MODEL), _full_spec((1, D_MODEL))],
        out_shape=[jax.ShapeDtypeStruct((l_dim, D_MODEL), F32), jax.ShapeDtypeStruct((l_dim, D_MODEL), BF16),
                   jax.ShapeDtypeStruct((1, D_MODEL), F32)],
        compiler_params=_params(),
    )(h, g, dy, dres)


def _final_bwd(h0, g, dxn, dh1):
    l_dim = h0.shape[0]
    nb = l_dim // BLK

    def body(h_ref, g_ref, dy_ref, dr_ref, gx_ref, gm_ref, dg_ref):
        i = pl.program_id(0)

        @pl.when(i == 0)
        def _():
            dg_ref[...] = jnp.zeros_like(dg_ref)

        dx, dg = _rms_bwd(h_ref[...], g_ref[...], dy_ref[...])
        out = dr_ref[...] + dx
        dg_ref[...] += dg

        @pl.when(i == 0)
        def _():
            gm_ref[...] = out

        @pl.when(i > 0)
        def _():
            gx_ref[...] = out

    return pl.pallas_call(
        body, name="final_bwd", grid=(nb,),
        in_specs=[_row_spec(BLK, D_MODEL), _full_spec((1, D_MODEL)), _row_spec(BLK, D_MODEL),
                  _row_spec(BLK, D_MODEL)],
        out_specs=[_row_spec(BLK, D_MODEL, lambda i: jnp.maximum(i - 1, 0)), _full_spec((BLK, D_MODEL)),
                   _full_spec((1, D_MODEL))],
        out_shape=[jax.ShapeDtypeStruct((l_dim - BLK, D_MODEL), F32), jax.ShapeDtypeStruct((BLK, D_MODEL), F32),
                   jax.ShapeDtypeStruct((1, D_MODEL), F32)],
        compiler_params=_params(),
    )(h0, g, dxn, dh1)


def _attn_valid(n):
    shape = (2 * BLK, 3 * BLK)
    qi = lax.broadcasted_iota(jnp.int32, shape, 0) & (BLK - 1)
    col = lax.broadcasted_iota(jnp.int32, shape, 1)
    kj = col & (BLK - 1)
    part = col >> 7
    nn = jnp.zeros(shape, jnp.int32) + n
    meta_ok = (part == 0) & (kj >= PAD) & (nn >= 1)
    prev_ok = (part == 1) & (kj > qi) & (nn >= 2)
    cur_ok = (part == 2) & (kj <= qi) & ((nn >= 1) | (kj >= PAD))
    return meta_ok | prev_ok | cur_ok


def _attn_probs(q_ref, kwin, sk_ref, valid, kv, e):
    qs = jnp.concatenate([q_ref[:, BLK * (2 * kv):BLK * (2 * kv) + BLK],
                          q_ref[:, BLK * (2 * kv + 1):BLK * (2 * kv + 1) + BLK]], axis=0)
    s = lax.dot_general(qs, kwin, (((1,), (1,)), ((), ())), preferred_element_type=F32)
    s = jnp.where(valid, s, NEG)
    h0 = 4 * kv + e
    row = lax.broadcasted_iota(jnp.int32, (2 * BLK, 1), 0)
    sink = jnp.where(row < BLK, sk_ref[:, h0:h0 + 1], sk_ref[:, h0 + 2:h0 + 3])
    m = jnp.maximum(jnp.max(s, axis=-1, keepdims=True), sink)
    ex = jnp.exp(s - m)
    es = jnp.exp(sink - m)
    inv = 1.0 / (jnp.sum(ex, axis=-1, keepdims=True) + es)
    return qs, ex * inv, es * inv


def _attn_specs(nb):
    prev = lambda i: jnp.maximum(i - 1, 0)
    zero = lambda i: 0
    kv_specs = [_row_spec(BLK, 8 * BLK, zero), _row_spec(BLK, 8 * BLK, prev), _row_spec(BLK, 8 * BLK)]
    return kv_specs


def _attn_fwd(qn, kf, vf, sinks):
    l_dim = qn.shape[0]
    nb = l_dim // BLK

    def body(q_ref, km_ref, kp_ref, kc_ref, vm_ref, vp_ref, vc_ref, sk_ref, o_ref):
        valid = _attn_valid(pl.program_id(0))
        for kv in range(N_KV_HEADS):
            outs = []
            for e in range(2):
                sl = slice(BLK * (2 * kv + e), BLK * (2 * kv + e) + BLK)
                kwin = jnp.concatenate([km_ref[:, sl], kp_ref[:, sl], kc_ref[:, sl]], axis=0)
                vwin = jnp.concatenate([vm_ref[:, sl], vp_ref[:, sl], vc_ref[:, sl]], axis=0)
                _, p, _ = _attn_probs(q_ref, kwin, sk_ref, valid, kv, e)
                outs.append(jnp.dot(p.astype(BF16), vwin, preferred_element_type=F32))
            o = outs[0] + outs[1]
            o_ref[:, BLK * (2 * kv):BLK * (2 * kv) + BLK] = o[:BLK]
            o_ref[:, BLK * (2 * kv + 1):BLK * (2 * kv + 1) + BLK] = o[BLK:]

    kv_specs = _attn_specs(nb)
    return pl.pallas_call(
        body, name="attn_fwd", grid=(nb,),
        in_specs=[_row_spec(BLK, Q_W)] + kv_specs + kv_specs + [_full_spec((1, N_Q_HEADS))],
        out_specs=_row_spec(BLK, Q_W),
        out_shape=jax.ShapeDtypeStruct((l_dim, Q_W), F32),
        compiler_params=_params(),
    )(qn, kf, kf, kf, vf, vf, vf, sinks)


def _attn_bwd(qn, kf, vf, sinks, attn, dattn):
    l_dim = qn.shape[0]
    nb = l_dim // BLK
    wide = 8 * BLK

    def body(q_ref, km_ref, kp_ref, kc_ref, vm_ref, vp_ref, vc_ref, sk_ref, o_ref, do_ref,
             dq_ref, dkc_ref, dkp_ref, dkm_ref, dvc_ref, dvp_ref, dvm_ref, dsk_ref):
        @pl.when(pl.program_id(0) == 0)
        def _():
            dkm_ref[...] = jnp.zeros_like(dkm_ref)
            dvm_ref[...] = jnp.zeros_like(dvm_ref)
            dsk_ref[...] = jnp.zeros_like(dsk_ref)

        valid = _attn_valid(pl.program_id(0))
        half = _lane_half((BLK, BLK))
        lane16 = lax.broadcasted_iota(jnp.int32, (1, N_Q_HEADS), 1)
        dsk = jnp.zeros((1, N_Q_HEADS), F32)
        for kv in range(N_KV_HEADS):
            j0, j1 = 2 * kv, 2 * kv + 1
            do0 = do_ref[:, BLK * j0:BLK * j0 + BLK]
            do1 = do_ref[:, BLK * j1:BLK * j1 + BLK]
            prod0 = do0 * o_ref[:, BLK * j0:BLK * j0 + BLK]
            prod1 = do1 * o_ref[:, BLK * j1:BLK * j1 + BLK]
            dos = jnp.concatenate([do0, do1], axis=0).astype(BF16)
            dqs = []
            for e in range(2):
                sl = slice(BLK * (2 * kv + e), BLK * (2 * kv + e) + BLK)
                kwin = jnp.concatenate([km_ref[:, sl], kp_ref[:, sl], kc_ref[:, sl]], axis=0)
                vwin = jnp.concatenate([vm_ref[:, sl], vp_ref[:, sl], vc_ref[:, sl]], axis=0)
                qs, p, p_sink = _attn_probs(q_ref, kwin, sk_ref, valid, kv, e)
                delta = jnp.concatenate(
                    [jnp.sum(jnp.where(half == e, prod0, 0.0), axis=-1, keepdims=True),
                     jnp.sum(jnp.where(half == e, prod1, 0.0), axis=-1, keepdims=True)], axis=0)
                dp = lax.dot_general(dos, vwin, (((1,), (1,)), ((), ())), preferred_element_type=F32)
                ds = (p * (dp - delta)).astype(BF16)
                pb = p.astype(BF16)
                dqs.append(jnp.dot(ds, kwin, preferred_element_type=F32))
                dk = lax.dot_general(ds, qs, (((0,), (0,)), ((), ())), preferred_element_type=F32)
                dv = lax.dot_general(pb, dos, (((0,), (0,)), ((), ())), preferred_element_type=F32)
                dkm_ref[:, sl] += dk[:BLK]
                dkp_ref[:, sl] = dk[BLK:2 * BLK]
                dkc_ref[:, sl] = dk[2 * BLK:]
                dvm_ref[:, sl] += dv[:BLK]
                dvp_ref[:, sl] = dv[BLK:2 * BLK]
                dvc_ref[:, sl] = dv[2 * BLK:]
                sink_g = -(p_sink * delta)
                g_lo = jnp.sum(sink_g[:BLK], axis=0, keepdims=True)
                g_hi = jnp.sum(sink_g[BLK:], axis=0, keepdims=True)
                dsk = dsk + jnp.where(lane16 == 4 * kv + e, g_lo, 0.0) + jnp.where(lane16 == 4 * kv + 2 + e, g_hi, 0.0)
            dq = jnp.where(jnp.concatenate([half, half], axis=0) == 0, dqs[0], dqs[1])
            dq_ref[:, BLK * j0:BLK * j0 + BLK] = dq[:BLK]
            dq_ref[:, BLK * j1:BLK * j1 + BLK] = dq[BLK:]
        dsk_ref[...] += dsk

    kv_specs = _attn_specs(nb)
    row_wide = _row_spec(BLK, wide)
    acc_wide = _full_spec((BLK, wide))
    big = jax.ShapeDtypeStruct((l_dim, wide), F32)
    return pl.pallas_call(
        body, name="attn_bwd", grid=(nb,),
        in_specs=[_row_spec(BLK, Q_W)] + kv_specs + kv_specs
        + [_full_spec((1, N_Q_HEADS)), _row_spec(BLK, Q_W), _row_spec(BLK, Q_W)],
        out_specs=[_row_spec(BLK, Q_W), row_wide, row_wide, acc_wide, row_wide, row_wide, acc_wide,
                   _full_spec((1, N_Q_HEADS))],
        out_shape=[jax.ShapeDtypeStruct((l_dim, Q_W), F32), big, big, jax.ShapeDtypeStruct((BLK, wide), F32),
                   big, big, jax.ShapeDtypeStruct((BLK, wide), F32), jax.ShapeDtypeStruct((1, N_Q_HEADS), F32)],
        compiler_params=_params(),
    )(qn, kf, kf, kf, vf, vf, vf, sinks, attn, dattn)


def _qk_bwd(qkv, q_norm_t, k_norm_t, e_mat, dq, dkc, dkp, dkm, dvc, dvp, dvm):
    l_dim = qkv.shape[0]
    nb = l_dim // BLK
    wide = 8 * BLK

    def fold(x):
        half = _lane_half((BLK, BLK))
        blocks = []
        for kb in range(2):
            t = []
            for kv in (2 * kb, 2 * kb + 1):
                own = kv % 2
                a = x[:, BLK * (2 * kv + own):BLK * (2 * kv + own) + BLK]
                b = pltpu.roll(x[:, BLK * (2 * kv + 1 - own):BLK * (2 * kv + 1 - own) + BLK], HEAD_DIM, axis=1)
                t.append(a + b)
            blocks.append(jnp.where(half == 0, t[0], t[1]))
        return jnp.concatenate(blocks, axis=1)

    def body(x_ref, qg_ref, kg_ref, e_ref, dq_ref, dkc_ref, dkp_ref, dkm_ref, dvc_ref, dvp_ref, dvm_ref,
             o_ref, dqg_ref, dkg_ref):
        i = pl.program_id(0)

        @pl.when(i == 0)
        def _():
            dqg_ref[...] = jnp.zeros_like(dqg_ref)
            dkg_ref[...] = jnp.zeros_like(dkg_ref)

        first = jnp.where(i == 0, 1.0, 0.0)
        not_last = jnp.where(i < nb - 1, 1.0, 0.0)
        dk_x = dkc_ref[...] + not_last * dkp_ref[...] + first * dkm_ref[...]
        dv_x = dvc_ref[...] + not_last * dvp_ref[...] + first * dvm_ref[...]
        x = x_ref[...]
        dqx, dqg = _head_rms_bwd(x[:, :Q_W], qg_ref[...], dq_ref[...] * (HEAD_DIM ** -0.5), e_ref)
        dkx, dkg = _head_rms_bwd(x[:, Q_W:Q_W + KV_W], kg_ref[...], fold(dk_x), e_ref)
        o_ref[:, :Q_W] = dqx.astype(BF16)
        o_ref[:, Q_W:Q_W + KV_W] = dkx.astype(BF16)
        o_ref[:, Q_W + KV_W:] = fold(dv_x).astype(BF16)
        dqg_ref[...] += dqg
        dkg_ref[...] += dkg

    nxt = lambda i: jnp.minimum(i + 1, nb - 1)
    row_wide = _row_spec(BLK, wide)
    nxt_wide = _row_spec(BLK, wide, nxt)
    acc_wide = _full_spec((BLK, wide))
    return pl.pallas_call(
        body, name="qk_bwd", grid=(nb,),
        in_specs=[_row_spec(BLK, Q_W + 2 * KV_W), _full_spec((1, Q_W)), _full_spec((1, KV_W)),
                  _full_spec((256, 256)), _row_spec(BLK, Q_W),
                  row_wide, nxt_wide, acc_wide, row_wide, nxt_wide, acc_wide],
        out_specs=[_row_spec(BLK, Q_W + 2 * KV_W), _full_spec((1, Q_W)), _full_spec((1, KV_W))],
        out_shape=[jax.ShapeDtypeStruct((l_dim, Q_W + 2 * KV_W), BF16),
                   jax.ShapeDtypeStruct((1, Q_W), F32), jax.ShapeDtypeStruct((1, KV_W), F32)],
        compiler_params=_params(),
    )(qkv, q_norm_t, k_norm_t, e_mat, dq, dkc, dkp, dkm, dvc, dvp, dvm)


def _scan_steps(q):
    steps = []
    s = 1
    while s < q:
        steps.append(s)
        s *= 2
    return steps


def _chunk_scan(xr, xi, a2r_ref, a2i_ref, lanes, conj, reverse):
    q = xr.shape[0]
    row = lax.broadcasted_iota(jnp.int32, xr.shape, 0)
    for idx, s in enumerate(_scan_steps(q)):
        ar = a2r_ref[idx:idx + 1, lanes]
        ai = a2i_ref[idx:idx + 1, lanes]
        if conj:
            ai = -ai
        if reverse:
            keep = row < q - s
            tr = jnp.where(keep, pltpu.roll(xr, q - s, axis=0), 0.0)
            ti = jnp.where(keep, pltpu.roll(xi, q - s, axis=0), 0.0)
        else:
            keep = row >= s
            tr = jnp.where(keep, pltpu.roll(xr, s, axis=0), 0.0)
            ti = jnp.where(keep, pltpu.roll(xi, s, axis=0), 0.0)
        xr, xi = xr + ar * tr - ai * ti, xi + ar * ti + ai * tr
    return xr, xi


def _ssm_fwd(u, wb_re, wb_im, wc_re, wc_im, d_skip, a2r, a2i, apr, api):
    l_dim = u.shape[0]
    nb = l_dim // BLK

    def body(u_ref, wbr_ref, wbi_ref, wcr_ref, wci_ref, d_ref, a2r_ref, a2i_ref, apr_ref, api_ref,
             y_ref, z_ref, sr_ref, si_ref, cr_ref, ci_ref):
        @pl.when(pl.program_id(0) == 0)
        def _():
            cr_ref[...] = jnp.zeros_like(cr_ref)
            ci_ref[...] = jnp.zeros_like(ci_ref)

        for kb in range(SSM_KB):
            ch = slice(BLK * kb, BLK * kb + BLK)
            lanes = slice(ST_KB * kb, ST_KB * kb + ST_KB)
            u_kb = u_ref[:, ch]
            ub = u_kb.astype(BF16)
            xr = jnp.dot(ub, wbr_ref[kb], preferred_element_type=F32)
            xi = jnp.dot(ub, wbi_ref[kb], preferred_element_type=F32)
            xr, xi = _chunk_scan(xr, xi, a2r_ref, a2i_ref, lanes, conj=False, reverse=False)
            cr = cr_ref[0:1, lanes]
            ci = ci_ref[0:1, lanes]
            pr = apr_ref[:, lanes]
            pi = api_ref[:, lanes]
            xr, xi = xr + pr * cr - pi * ci, xi + pr * ci + pi * cr
            sr_ref[:, lanes] = xr
            si_ref[:, lanes] = xi
            cr_ref[0:1, lanes] = sr_ref[BLK - 1:BLK, lanes]
            ci_ref[0:1, lanes] = si_ref[BLK - 1:BLK, lanes]
            y = (jnp.dot(xr.astype(BF16), wcr_ref[kb], preferred_element_type=F32)
                 - jnp.dot(xi.astype(BF16), wci_ref[kb], preferred_element_type=F32)
                 + d_ref[:, ch] * u_kb)
            y_ref[:, ch] = y
            z_ref[:, ch] = _gelu(y).astype(BF16)

    wb_spec = _full_spec((SSM_KB, BLK, ST_KB))
    wc_spec = _full_spec((SSM_KB, ST_KB, BLK))
    a2_spec = _full_spec(a2r.shape)
    ap_spec = _full_spec((BLK, N_STATE))
    return pl.pallas_call(
        body, name="ssm_fwd", grid=(nb,),
        in_specs=[_row_spec(BLK, D_MODEL), wb_spec, wb_spec, wc_spec, wc_spec, _full_spec((1, D_MODEL)),
                  a2_spec, a2_spec, ap_spec, ap_spec],
        out_specs=[_row_spec(BLK, D_MODEL), _row_spec(BLK, D_MODEL), _row_spec(BLK, N_STATE),
                   _row_spec(BLK, N_STATE)],
        out_shape=[jax.ShapeDtypeStruct((l_dim, D_MODEL), F32), jax.ShapeDtypeStruct((l_dim, D_MODEL), BF16),
                   jax.ShapeDtypeStruct((l_dim, N_STATE), F32), jax.ShapeDtypeStruct((l_dim, N_STATE), F32)],
        scratch_shapes=[pltpu.VMEM((8, N_STATE), F32), pltpu.VMEM((8, N_STATE), F32)],
        compiler_params=_params(),
    )(u, wb_re, wb_im, wc_re, wc_im, d_skip, a2r, a2i, apr, api)


def _ssm_bwd(dz, y, u, s_re, s_im, wb_re, wb_im, wc_re, wc_im, d_skip, a2r, a2i, aqr, aqi):
    l_dim = u.shape[0]
    nb = l_dim // BLK

    def body(dz_ref, y_ref, u_ref, sr_ref, si_ref, wbr_ref, wbi_ref, wcr_ref, wci_ref, d_ref,
             a2r_ref, a2i_ref, aqr_ref, aqi_ref,
             du_ref, dd_ref, dar_ref, dai_ref, dwbr_ref, dwbi_ref, dwcr_ref, dwci_ref, cr_ref, ci_ref):
        @pl.when(pl.program_id(0) == 0)
        def _():
            for r in (cr_ref, ci_ref, dd_ref, dar_ref, dai_ref, dwbr_ref, dwbi_ref, dwcr_ref, dwci_ref):
                r[...] = jnp.zeros_like(r)

        tn = (((0,), (0,)), ((), ()))
        nt = (((1,), (1,)), ((), ()))
        last_row = lax.broadcasted_iota(jnp.int32, (BLK, ST_KB), 0) == BLK - 1
        for kb in range(SSM_KB):
            ch = slice(BLK * kb, BLK * kb + BLK)
            lanes = slice(ST_KB * kb, ST_KB * kb + ST_KB)
            u_kb = u_ref[:, ch]
            dy = dz_ref[:, ch] * _gelu_grad(y_ref[:, ch])
            dyb = dy.astype(BF16)
            ub = u_kb.astype(BF16)
            dd_ref[:, ch] += jnp.sum(dy * u_kb, axis=0, keepdims=True)
            gr = lax.dot_general(dyb, wcr_ref[kb], nt, preferred_element_type=F32)
            gi = -lax.dot_general(dyb, wci_ref[kb], nt, preferred_element_type=F32)
            gr, gi = _chunk_scan(gr, gi, a2r_ref, a2i_ref, lanes, conj=True, reverse=True)
            cr = cr_ref[0:1, lanes]
            ci = ci_ref[0:1, lanes]
            pr = aqr_ref[:, lanes]
            pi = -aqi_ref[:, lanes]
            gr, gi = gr + pr * cr - pi * ci, gi + pr * ci + pi * cr
            sr = sr_ref[:, lanes]
            si = si_ref[:, lanes]
            gsr = jnp.where(last_row, cr, pltpu.roll(gr, BLK - 1, axis=0))
            gsi = jnp.where(last_row, ci, pltpu.roll(gi, BLK - 1, axis=0))
            dar_ref[:, lanes] += jnp.sum(gsr * sr + gsi * si, axis=0, keepdims=True)
            dai_ref[:, lanes] += jnp.sum(gsi * sr - gsr * si, axis=0, keepdims=True)
            first_row = lax.broadcasted_iota(jnp.int32, (BLK, ST_KB), 0) == 0
            cr_ref[0:1, lanes] = jnp.sum(jnp.where(first_row, gr, 0.0), axis=0, keepdims=True)
            ci_ref[0:1, lanes] = jnp.sum(jnp.where(first_row, gi, 0.0), axis=0, keepdims=True)
            grb = gr.astype(BF16)
            gib = gi.astype(BF16)
            du = (lax.dot_general(grb, wbr_ref[kb], nt, preferred_element_type=F32)
                  + lax.dot_general(gib, wbi_ref[kb], nt, preferred_element_type=F32)
                  + d_ref[:, ch] * dy)
            du_ref[:, ch] = du.astype(BF16)
            dwbr_ref[kb] += lax.dot_general(ub, grb, tn, preferred_element_type=F32)
            dwbi_ref[kb] += lax.dot_general(ub, gib, tn, preferred_element_type=F32)
            dwcr_ref[kb] += lax.dot_general(sr.astype(BF16), dyb, tn, preferred_element_type=F32)
            dwci_ref[kb] -= lax.dot_general(si.astype(BF16), dyb, tn, preferred_element_type=F32)

    rev = lambda i: nb - 1 - i
    wb_spec = _full_spec((SSM_KB, BLK, ST_KB))
    wc_spec = _full_spec((SSM_KB, ST_KB, BLK))
    a2_spec = _full_spec(a2r.shape)
    ap_spec = _full_spec((BLK, N_STATE))
    vec = _full_spec((1, D_MODEL))
    svec = _full_spec((1, N_STATE))
    return pl.pallas_call(
        body, name="ssm_bwd", grid=(nb,),
        in_specs=[_row_spec(BLK, D_MODEL, rev), _row_spec(BLK, D_MODEL, rev), _row_spec(BLK, D_MODEL, rev),
                  _row_spec(BLK, N_STATE, rev), _row_spec(BLK, N_STATE, rev),
                  wb_spec, wb_spec, wc_spec, wc_spec, vec, a2_spec, a2_spec, ap_spec, ap_spec],
        out_specs=[_row_spec(BLK, D_MODEL, rev), vec, svec, svec, wb_spec, wb_spec, wc_spec, wc_spec],
        out_shape=[jax.ShapeDtypeStruct((l_dim, D_MODEL), BF16), jax.ShapeDtypeStruct((1, D_MODEL), F32),
                   jax.ShapeDtypeStruct((1, N_STATE), F32), jax.ShapeDtypeStruct((1, N_STATE), F32),
                   jax.ShapeDtypeStruct((SSM_KB, BLK, ST_KB), F32), jax.ShapeDtypeStruct((SSM_KB, BLK, ST_KB), F32),
                   jax.ShapeDtypeStruct((SSM_KB, ST_KB, BLK), F32), jax.ShapeDtypeStruct((SSM_KB, ST_KB, BLK), F32)],
        scratch_shapes=[pltpu.VMEM((8, N_STATE), F32), pltpu.VMEM((8, N_STATE), F32)],
        compiler_params=_params(),
    )(dz, y, u, s_re, s_im, wb_re, wb_im, wc_re, wc_im, d_skip, a2r, a2i, aqr, aqi)


def _discretize(lam_re, lam_im, log_dt, b_re, b_im):
    dt = jnp.exp(log_dt)[:, None]
    mag = jnp.exp(lam_re * dt)
    ar, ai = mag * jnp.cos(lam_im * dt), mag * jnp.sin(lam_im * dt)
    den = lam_re * lam_re + lam_im * lam_im
    nr, ni = ar - 1.0, ai
    fr, fi = (nr * lam_re + ni * lam_im) / den, (ni * lam_re - nr * lam_im) / den
    bbar_re = fr[..., None] * b_re - fi[..., None] * b_im
    bbar_im = fr[..., None] * b_im + fi[..., None] * b_re
    return ar, ai, bbar_re, bbar_im


def _block_diag_b(bbar):
    eye = jnp.eye(8, dtype=bbar.dtype)
    return jnp.einsum("kgpc,gh->kgchp", bbar.reshape(8, 8, SSM_STATE, SSM_GROUP_CH), eye).reshape(8, BLK, ST_KB)


def _block_diag_b_t(dwb):
    eye = jnp.eye(8, dtype=dwb.dtype)
    return jnp.einsum("kgchp,gh->kgpc", dwb.reshape(8, 8, SSM_GROUP_CH, 8, SSM_STATE), eye).reshape(
        SSM_GROUPS, SSM_STATE, SSM_GROUP_CH)


def _block_diag_c(c):
    eye = jnp.eye(8, dtype=c.dtype)
    return jnp.einsum("kgcp,gh->kgphc", c.reshape(8, 8, SSM_GROUP_CH, SSM_STATE), eye).reshape(8, ST_KB, BLK)


def _block_diag_c_t(dwc):
    eye = jnp.eye(8, dtype=dwc.dtype)
    return jnp.einsum("kgphc,gh->kgcp", dwc.reshape(8, 8, SSM_STATE, 8, SSM_GROUP_CH), eye).reshape(
        SSM_GROUPS, SSM_GROUP_CH, SSM_STATE)


def _power_tables(ar, ai):
    sq_r, sq_i = [ar], [ai]
    for _ in range(len(_scan_steps(BLK)) - 1):
        r, i = sq_r[-1], sq_i[-1]
        sq_r.append(r * r - i * i)
        sq_i.append(2.0 * r * i)
    pr, pi = ar, ai
    for k in range(len(sq_r)):
        r, i = sq_r[k], sq_i[k]
        pr, pi = (jnp.concatenate([pr, pr * r - pi * i], axis=0),
                  jnp.concatenate([pi, pr * i + pi * r], axis=0))
    pad = jnp.zeros((8 - len(sq_r), ar.shape[1]), F32)
    a2r = jnp.concatenate(sq_r + [pad], axis=0)
    a2i = jnp.concatenate(sq_i + [pad], axis=0)
    return a2r, a2i, pr, pi


def _all_gather(name, arrs):
    n = len(arrs)

    def body(*refs):
        ins, outs = refs[:n], refs[n:2 * n]
        send_sems, recv_sems, local_sems = refs[2 * n:]
        x, y, c = lax.axis_index("x"), lax.axis_index("y"), lax.axis_index("c")
        me, sibling = (x, y, c), (x, y, 1 - c)
        chips = [(1 - x, y), (x, 1 - y), (1 - x, 1 - y)]

        def rows(a, px, py, pc):
            m = ins[a].shape[0]
            return outs[a].at[pl.ds((4 * px + 2 * py + pc) * m, m), :]

        def copy(a, k, block, to, src=None):
            return pltpu.make_async_remote_copy(
                src_ref=rows(a, *block) if src is None else src, dst_ref=rows(a, *block),
                send_sem=send_sems.at[a, k], recv_sem=recv_sems.at[a, k],
                device_id=to, device_id_type=MESH)

        mine = [pltpu.make_async_copy(ins[a], rows(a, *me), local_sems.at[a]) for a in range(n)]
        for cp in mine:
            cp.start()
        first = []
        for a in range(n):
            first.append(copy(a, 0, me, sibling, src=ins[a]))
            first += [copy(a, 1 + j, me, (*chip, c), src=ins[a]) for j, chip in enumerate(chips)]
        for cp in first:
            cp.start()
        passed = []
        for j, chip in enumerate(chips):
            for a in range(n):
                copy(a, 1 + j, (*chip, c), me).wait_recv()
                cp = copy(a, 4 + j, (*chip, c), sibling)
                cp.start()
                passed.append(cp)
        for a in range(n):
            copy(a, 0, sibling, me).wait_recv()
            for j, chip in enumerate(chips):
                copy(a, 4 + j, (*chip, 1 - c), me).wait_recv()
        for cp in first + passed:
            cp.wait_send()
        for cp in mine:
            cp.wait()

    any_spec = pl.BlockSpec(memory_space=pl.ANY)
    return pl.pallas_call(
        body, name=name,
        in_specs=[any_spec] * n, out_specs=[any_spec] * n,
        out_shape=[jax.ShapeDtypeStruct((N_DEV * a.shape[0], a.shape[1]), a.dtype) for a in arrs],
        scratch_shapes=[pltpu.SemaphoreType.DMA((n, 7)), pltpu.SemaphoreType.DMA((n, 7)),
                        pltpu.SemaphoreType.DMA((n,))],
        compiler_params=_params(has_side_effects=True),
    )(*arrs)


def _exchange(name, arrs):
    n = len(arrs)

    def body(*refs):
        ins, outs = refs[:n], refs[n:2 * n]
        send_sems, recv_sems, local_sems = refs[2 * n:]
        x, y, c = lax.axis_index("x"), lax.axis_index("y"), lax.axis_index("c")
        me = 4 * x + 2 * y + c
        peers = []
        for rel in range(1, N_DEV):
            bx, by, bc = (rel >> 2) & 1, (rel >> 1) & 1, rel & 1
            peers.append((x + bx - 2 * x * bx, y + by - 2 * y * by, c + bc - 2 * c * bc))

        copies = []
        for a in range(n):
            m = ins[a].shape[0] // N_DEV
            local = pltpu.make_async_copy(ins[a].at[pl.ds(me * m, m), :], outs[a].at[pl.ds(me * m, m), :],
                                          local_sems.at[a])
            local.start()
            copies.append(local)
        remote = []
        for a in range(n):
            m = ins[a].shape[0] // N_DEV
            for k, peer in enumerate(peers):
                p_idx = 4 * peer[0] + 2 * peer[1] + peer[2]
                cp = pltpu.make_async_remote_copy(
                    src_ref=ins[a].at[pl.ds(p_idx * m, m), :], dst_ref=outs[a].at[pl.ds(me * m, m), :],
                    send_sem=send_sems.at[a, k], recv_sem=recv_sems.at[a, k],
                    device_id=peer, device_id_type=MESH)
                cp.start()
                remote.append(cp)
        for cp in remote:
            cp.wait_recv()
        for cp in remote:
            cp.wait_send()
        for cp in copies:
            cp.wait()

    any_spec = pl.BlockSpec(memory_space=pl.ANY)
    return pl.pallas_call(
        body, name=name,
        in_specs=[any_spec] * n, out_specs=[any_spec] * n,
        out_shape=[jax.ShapeDtypeStruct(a.shape, a.dtype) for a in arrs],
        scratch_shapes=[pltpu.SemaphoreType.DMA((n, 7)), pltpu.SemaphoreType.DMA((n, 7)),
                        pltpu.SemaphoreType.DMA((n,))],
        compiler_params=_params(has_side_effects=True),
    )(*arrs)


def _sum_slots(name, recv, own):
    m, ncol = own.shape
    tr = m // 2 if (m // 2) % 16 == 0 else m
    g = m // tr

    def body(*refs):
        slots, own_ref, o_ref = refs[:N_DEV], refs[N_DEV], refs[N_DEV + 1]
        me = _my_index()
        tot = None
        for s in range(N_DEV):
            v = jnp.where(me == s, own_ref[...], slots[s][...].astype(F32))
            tot = v if tot is None else tot + v
        o_ref[...] = tot

    def slot_spec(s):
        return pl.BlockSpec((tr, ncol), lambda i: (s * g + i, 0))

    return pl.pallas_call(
        body, name=name, grid=(g,),
        in_specs=[slot_spec(s) for s in range(N_DEV)] + [pl.BlockSpec((tr, ncol), lambda i: (i, 0))],
        out_specs=pl.BlockSpec((tr, ncol), lambda i: (i, 0)),
        out_shape=jax.ShapeDtypeStruct((m, ncol), F32),
        compiler_params=_params(),
    )(*([recv] * N_DEV), own)


def _sum_gathered(name, gathered, rows):
    tr = _pick(rows, 512, 8)
    g = rows // tr

    def body(*refs):
        o_ref = refs[N_DEV]
        tot = refs[0][...]
        for s in range(1, N_DEV):
            tot = tot + refs[s][...]
        o_ref[...] = tot

    return pl.pallas_call(
        body, name=name, grid=(g,),
        in_specs=[pl.BlockSpec((tr, BLK), (lambda i, s=s: (s * g + i, 0))) for s in range(N_DEV)],
        out_specs=pl.BlockSpec((tr, BLK), lambda i: (i, 0)),
        out_shape=jax.ShapeDtypeStruct((rows, BLK), F32),
        compiler_params=_params(),
    )(*([gathered] * N_DEV))


def _adamw(name, w, g, m, v):
    r, c = w.shape
    tr = _pick(r, 256, 8) if r % 8 == 0 else r
    c1 = 1.0 - ADAM_B1 ** ADAM_STEP
    c2 = 1.0 - ADAM_B2 ** ADAM_STEP

    def body(w_ref, g_ref, m_ref, v_ref, d_ref, nm_ref, nv_ref):
        gv = g_ref[...]
        nm = ADAM_B1 * m_ref[...] + (1.0 - ADAM_B1) * gv
        nv = ADAM_B2 * v_ref[...] + (1.0 - ADAM_B2) * (gv * gv)
        m_hat = nm / c1
        v_hat = nv / c2
        d_ref[...] = -ADAM_LR * (m_hat / (jnp.sqrt(v_hat) + ADAM_EPS) + ADAM_WD * w_ref[...])
        nm_ref[...] = nm
        nv_ref[...] = nv

    spec = pl.BlockSpec((tr, c), lambda i: (i, 0))
    shape = jax.ShapeDtypeStruct((r, c), F32)
    return pl.pallas_call(
        body, name=name, grid=(r // tr,),
        in_specs=[spec] * 4, out_specs=[spec] * 3, out_shape=[shape] * 3,
        compiler_params=_params(),
    )(w, g, m, v)


def _pack(parts):
    flat = []
    for p in parts:
        v = p.reshape(-1)
        flat.append(jnp.pad(v, (0, (-v.shape[0]) % BLK)))
    v = jnp.concatenate(flat)
    v = jnp.pad(v, (0, (-v.shape[0]) % (8 * BLK)))
    return v.reshape(-1, BLK)


def _unpack(buf, shapes):
    flat = buf.reshape(-1)
    out, off = [], 0
    for shp in shapes:
        size = math.prod(shp)
        out.append(flat[off:off + size].reshape(shp))
        off += size + (-size) % BLK
    return out


def kernel(x, meta_tokens, norm_mix, w_in, q_norm, k_norm, attn_sinks, lam_re, lam_im, log_dt, ssm_b_re, ssm_b_im, ssm_c_re, ssm_c_im, ssm_d, w_glu, attn_branch_norm, ssm_branch_norm, w_out, norm_ffn, w_ffn_in, w_ffn_out, loss_target, m_meta_tokens, m_norm_mix, m_w_in, m_q_norm, m_k_norm, m_attn_sinks, m_lam_re, m_lam_im, m_log_dt, m_ssm_b_re, m_ssm_b_im, m_ssm_c_re, m_ssm_c_im, m_ssm_d, m_w_glu, m_attn_branch_norm, m_ssm_branch_norm, m_w_out, m_norm_ffn, m_w_ffn_in, m_w_ffn_out, v_meta_tokens, v_norm_mix, v_w_in, v_q_norm, v_k_norm, v_attn_sinks, v_lam_re, v_lam_im, v_log_dt, v_ssm_b_re, v_ssm_b_im, v_ssm_c_re, v_ssm_c_im, v_ssm_d, v_w_glu, v_attn_branch_norm, v_ssm_branch_norm, v_w_out, v_norm_ffn, v_w_ffn_in, v_w_ffn_out):
    args = dict(locals())
    weights = {n: args[n] for n in WEIGHTS}
    mom_m = {n: args["m_" + n] for n in WEIGHTS}
    mom_v = {n: args["v_" + n] for n in WEIGHTS}

    x2d = x[0]
    target2d = loss_target[0]
    s_len = x2d.shape[0]
    l_dim = s_len + BLK
    tm_row = _pick(l_dim, 320)
    tm_mm = _pick(l_dim, 1040)
    tl_tn = _pick(l_dim, 832)

    shard_in = w_in[0].T.astype(BF16)
    shard_glu = w_glu[0].T.astype(BF16)
    shard_out = w_out[0].astype(BF16)
    shard_ffn_in = w_ffn_in[0].T.astype(BF16)
    shard_ffn_out = w_ffn_out[0].astype(BF16)
    shard_meta = meta_tokens.T
    wt_in, wt_glu, w_out_f, wt_ffn_in, w_ffn_out_f, meta_t = _all_gather(
        "gather_weights", [shard_in, shard_glu, shard_out, shard_ffn_in, shard_ffn_out, shard_meta])
    meta_full = meta_t.T
    meta_pad = jnp.pad(meta_full, ((PAD, 0), (0, 0)))

    nm_t, qn_t, kn_t = norm_mix, jnp.tile(q_norm, (1, N_Q_HEADS)), jnp.tile(k_norm, (1, N_KV_HEADS))
    e_mat = jnp.kron(jnp.eye(4, dtype=F32), jnp.ones((HEAD_DIM, HEAD_DIM), F32)).astype(BF16)

    def disc(lr, li, ldt, br, bi):
        return _discretize(lr[0], li[0], ldt[0], br[0], bi[0])

    (abar_re, abar_im, bbar_re, bbar_im), disc_vjp = jax.vjp(disc, lam_re, lam_im, log_dt, ssm_b_re, ssm_b_im)
    wb_re, wb_im = _block_diag_b(bbar_re).astype(BF16), _block_diag_b(bbar_im).astype(BF16)
    wc_re, wc_im = _block_diag_c(ssm_c_re[0]).astype(BF16), _block_diag_c(ssm_c_im[0]).astype(BF16)
    a2r, a2i, apr, api = _power_tables(abar_re.reshape(1, N_STATE), abar_im.reshape(1, N_STATE))
    aqr, aqi = apr[::-1], api[::-1]

    h0, xn = _embed_norm(x2d, meta_pad, nm_t)
    qkv = _matmul("proj_qkv", xn, wt_in, nt=True, tm=tm_mm, tn=512, tk=D_MODEL, n=Q_W + 2 * KV_W, w_off=0)
    u = _matmul("proj_u", xn, wt_in, nt=True, tm=tm_mm, tn=512, tk=D_MODEL, n=D_MODEL, w_off=3)
    gates = _matmul("proj_gates", xn, wt_in, nt=True, tm=tm_mm, tn=512, tk=D_MODEL, n=2 * D_MODEL, w_off=5)
    qn, kf, vf = _qk_prep(qkv, qn_t, kn_t, e_mat, tm_row)
    attn = _attn_fwd(qn, kf, vf, attn_sinks)
    y, z, s_re, s_im = _ssm_fwd(u, wb_re, wb_im, wc_re, wc_im, ssm_d, a2r, a2i, apr, api)
    zab = _matmul("glu_proj", z, wt_glu, nt=True, tm=tm_mm, tn=1024, tk=D_MODEL)
    merged = _merge_fwd(attn, zab, gates, attn_branch_norm, ssm_branch_norm, tm_row)
    h1 = _matmul("out_proj", merged, w_out_f, nt=False, tm=tm_mm, tn=1024, tk=D_MODEL, res=h0)
    hn = _norm_cast("ffn_norm", h1, norm_ffn, tm_row)
    gu = _matmul("ffn_in", hn, wt_ffn_in, nt=True, tm=tm_mm, tn=512, tk=D_MODEL)
    act = _swiglu_fwd(gu, tm_row)
    h2 = _matmul("ffn_out", act, w_ffn_out_f, nt=False, tm=tm_mm, tn=1024, tk=1408, res=h1)
    dh2, dh2_b, loss_part = _loss_grad(h2, target2d)

    dact = _matmul("d_act", dh2_b, w_ffn_out_f, nt=True, tm=tm_mm, tn=1408, tk=D_MODEL)
    dgu = _swiglu_bwd(gu, dact, tm_row)
    g_ffn_out = _matmul_tn("g_ffn_out", act, dh2_b, tm=1408, tn=1024, tl=tl_tn)
    g_ffn_in_t = _matmul_tn("g_ffn_in", dgu, hn, tm=1408, tn=1024, tl=tl_tn)
    dhn = _matmul("d_hn", dgu, wt_ffn_in, nt=False, tm=tm_mm, tn=1024, tk=1408)
    dh1, dh1_b, g_norm_ffn = _norm_bwd_res("ffn_norm_bwd", h1, norm_ffn, dhn, dh2, tm_row)
    dmerged = _matmul("d_merged", dh1_b, w_out_f, nt=True, tm=tm_mm, tn=1024, tk=D_MODEL)
    dattn, dzab, dgates, g_abn, g_sbn = _merge_bwd(attn, zab, gates, attn_branch_norm, ssm_branch_norm,
                                                     dmerged, tm_row)
    g_out = _matmul_tn("g_out", merged, dh1_b, tm=1024, tn=1024, tl=tl_tn)
    dz = _matmul("d_z", dzab, wt_glu, nt=False, tm=tm_mm, tn=1024, tk=1024)
    g_glu_t = _matmul_tn("g_glu", dzab, z, tm=1024, tn=1024, tl=tl_tn)
    du, g_ssm_d, g_ar, g_ai, g_wbr, g_wbi, g_wcr, g_wci = _ssm_bwd(
        dz, y, u, s_re, s_im, wb_re, wb_im, wc_re, wc_im, ssm_d, a2r, a2i, aqr, aqi)
    dq, dkc, dkp, dkm, dvc, dvp, dvm, g_sinks = _attn_bwd(qn, kf, vf, attn_sinks, attn, dattn)
    dqkv, g_qn_t, g_kn_t = _qk_bwd(qkv, qn_t, kn_t, e_mat, dq, dkc, dkp, dkm, dvc, dvp, dvm)
    dproj = jnp.concatenate([dqkv, du, dgates], axis=1)
    g_in_t = _matmul_tn("g_in", dproj, xn, tm=1152, tn=1024, tl=tl_tn)
    dxn = _matmul("d_xn", dproj, wt_in, nt=False, tm=tm_mm, tn=1024, tk=1152)
    grad_x2d, dmeta_blk, g_norm_mix = _final_bwd(h0, nm_t, dxn, dh1)

    g_lam_re, g_lam_im, g_log_dt, g_b_re, g_b_im = disc_vjp(
        (g_ar.reshape(SSM_GROUPS, SSM_STATE), g_ai.reshape(SSM_GROUPS, SSM_STATE),
         _block_diag_b_t(g_wbr), _block_diag_b_t(g_wbi)))
    small_grads = {
        "norm_mix": g_norm_mix, "q_norm": g_qn_t.reshape(N_Q_HEADS, HEAD_DIM).sum(0)[None],
        "k_norm": g_kn_t.reshape(N_KV_HEADS, HEAD_DIM).sum(0)[None], "attn_sinks": g_sinks,
        "lam_re": g_lam_re, "lam_im": g_lam_im, "log_dt": g_log_dt, "ssm_b_re": g_b_re, "ssm_b_im": g_b_im,
        "ssm_c_re": _block_diag_c_t(g_wcr)[None], "ssm_c_im": _block_diag_c_t(g_wci)[None],
        "ssm_d": g_ssm_d, "attn_branch_norm": g_abn, "ssm_branch_norm": g_sbn, "norm_ffn": g_norm_ffn,
    }
    small_shapes = [weights[n].shape for n in SMALL] + [(N_META, D_MODEL)]
    packed = _pack([small_grads[n] for n in SMALL] + [dmeta_blk[PAD:]])
    rows = packed.shape[0]
    (gathered,) = _all_gather("gather_small", [packed])
    g_small = _sum_gathered("sum_small", gathered, rows)
    w_small = _pack([weights[n] for n in SMALL] + [jnp.zeros((N_META, D_MODEL), F32)])
    m_small = _pack([mom_m[n] for n in SMALL] + [jnp.zeros((N_META, D_MODEL), F32)])
    v_small = _pack([mom_v[n] for n in SMALL] + [jnp.zeros((N_META, D_MODEL), F32)])
    d_small, nm_small, nv_small = _adamw("adamw_small", w_small, g_small, m_small, v_small)
    grads = dict(zip(SMALL + ["_meta"], _unpack(g_small, small_shapes)))
    deltas = dict(zip(SMALL, _unpack(d_small, small_shapes[:-1])))
    new_m = dict(zip(SMALL, _unpack(nm_small, small_shapes[:-1])))
    new_v = dict(zip(SMALL, _unpack(nv_small, small_shapes[:-1])))

    me = _my_index()
    grads["meta_tokens"] = lax.dynamic_slice(grads.pop("_meta"), (0, me * BLK), (N_META, BLK))

    big = [("w_in", g_in_t, True), ("w_glu", g_glu_t, True), ("w_out", g_out, False),
           ("w_ffn_in", g_ffn_in_t, True), ("w_ffn_out", g_ffn_out, False)]
    recvs = _exchange("exchange_grads", [g.astype(BF16) for _, g, _ in big])
    for (name, g_full, transposed), recv in zip(big, recvs):
        m_rows = g_full.shape[0] // N_DEV
        own = lax.dynamic_slice(g_full, (me * m_rows, 0), (m_rows, g_full.shape[1]))
        g_shard = _sum_slots("sum_" + name, recv, own)
        grads[name] = (g_shard.T if transposed else g_shard)[None]

    for name in ["meta_tokens", "w_in", "w_glu", "w_out", "w_ffn_in", "w_ffn_out"]:
        shp = weights[name].shape
        as2d = lambda a: a.reshape(shp[-2], shp[-1])
        d, nm, nv = _adamw("adamw_" + name, as2d(weights[name]), as2d(grads[name]), as2d(mom_m[name]),
                           as2d(mom_v[name]))
        deltas[name], new_m[name], new_v[name] = d.reshape(shp), nm.reshape(shp), nv.reshape(shp)

    loss = lax.psum(loss_part[0, 0], ("x", "y", "c"))
    return (loss, grad_x2d[None], *[grads[n] for n in WEIGHTS], *[deltas[n] for n in WEIGHTS],
            *[new_m[n] for n in WEIGHTS], *[new_v[n] for n in WEIGHTS])
```

```python
import math

import jax
import jax.numpy as jnp
from jax import lax
from jax.experimental import pallas as pl
from jax.experimental.pallas import tpu as pltpu

F32 = jnp.float32
BF16 = jnp.bfloat16

D_MODEL = 1024
N_META = 16
HEAD_DIM = 64
N_Q_HEADS = 16
N_KV_HEADS = 4
Q_W = N_Q_HEADS * HEAD_DIM
KV_W = N_KV_HEADS * HEAD_DIM
SSM_GROUPS = 64
SSM_GROUP_CH = 16
SSM_STATE = 64
N_STATE = SSM_GROUPS * SSM_STATE
D_FF = 2816
IN_COLS = Q_W + 2 * KV_W + 3 * D_MODEL
EPS = 1e-6
BLK = 128
PAD = BLK - N_META
N_DEV = 8
NEG = -1e30
SSM_KB = 8
ST_KB = N_STATE // SSM_KB
LB_KB = ST_KB // BLK
N_LB = N_STATE // BLK

ADAM_LR = 0.001
ADAM_B1 = 0.9
ADAM_B2 = 0.999
ADAM_EPS = 1e-08
ADAM_WD = 0.01
ADAM_STEP = 10

VMEM_LIMIT = 48 * 1024 * 1024
MESH = pl.DeviceIdType.MESH

SMALL = ["norm_mix", "q_norm", "k_norm", "attn_sinks", "lam_re", "lam_im", "log_dt", "ssm_b_re", "ssm_b_im",
         "ssm_c_re", "ssm_c_im", "ssm_d", "attn_branch_norm", "ssm_branch_norm", "norm_ffn"]
WEIGHTS = ["meta_tokens", "norm_mix", "w_in", "q_norm", "k_norm", "attn_sinks", "lam_re", "lam_im", "log_dt",
           "ssm_b_re", "ssm_b_im", "ssm_c_re", "ssm_c_im", "ssm_d", "w_glu", "attn_branch_norm",
           "ssm_branch_norm", "w_out", "norm_ffn", "w_ffn_in", "w_ffn_out"]


def _params(**kw):
    return pltpu.CompilerParams(vmem_limit_bytes=VMEM_LIMIT, **kw)


def _pick(n, cap, mult=16):
    best = None
    for d in range(mult, min(n, cap) + 1, mult):
        if n % d == 0:
            best = d
    assert best is not None, (n, cap, mult)
    return best


def _my_index():
    return 4 * lax.axis_index("x") + 2 * lax.axis_index("y") + lax.axis_index("c")


def _rms(x, g):
    r = lax.rsqrt(jnp.mean(x * x, axis=-1, keepdims=True) + EPS)
    return x * r * g


def _rms_bwd(x, g, dy):
    r = lax.rsqrt(jnp.mean(x * x, axis=-1, keepdims=True) + EPS)
    t = dy * g
    dx = r * t - x * (r * r * r) * jnp.mean(t * x, axis=-1, keepdims=True)
    dg = jnp.sum(dy * (x * r), axis=0, keepdims=True)
    return dx, dg


def _sigmoid(x):
    return 1.0 / (1.0 + jnp.exp(-x))


def _gelu(x):
    k = math.sqrt(2.0 / math.pi)
    return 0.5 * x * (1.0 + jnp.tanh(k * (x + 0.044715 * (x * x * x))))


def _gelu_grad(x):
    k = math.sqrt(2.0 / math.pi)
    t = jnp.tanh(k * (x + 0.044715 * (x * x * x)))
    return 0.5 * (1.0 + t) + 0.5 * x * (1.0 - t * t) * (k * (1.0 + 3.0 * 0.044715 * (x * x)))


def _head_mean(x, e_ref):
    hi = x.astype(BF16)
    r1 = x - hi.astype(F32)
    mid = r1.astype(BF16)
    lo = (r1 - mid.astype(F32)).astype(BF16)
    e = e_ref[...]
    out = []
    for b in range(x.shape[1] // 256):
        sl = slice(256 * b, 256 * b + 256)
        s = (jnp.dot(hi[:, sl], e, preferred_element_type=F32)
             + jnp.dot(mid[:, sl], e, preferred_element_type=F32)
             + jnp.dot(lo[:, sl], e, preferred_element_type=F32))
        out.append(s)
    s = out[0] if len(out) == 1 else jnp.concatenate(out, axis=1)
    return s * (1.0 / HEAD_DIM)


def _head_rms(x, g, e_ref):
    r = lax.rsqrt(_head_mean(x * x, e_ref) + EPS)
    return x * r * g


def _head_rms_bwd(x, g, dy, e_ref):
    r = lax.rsqrt(_head_mean(x * x, e_ref) + EPS)
    t = dy * g
    dx = r * t - x * (r * r * r) * _head_mean(t * x, e_ref)
    dg = jnp.sum(dy * (x * r), axis=0, keepdims=True)
    return dx, dg


def _lane_half(shape):
    lane = lax.broadcasted_iota(jnp.int32, shape, len(shape) - 1)
    return (lane >> 6) & 1


def _matmul(name, a, w, *, nt, tm, tn, tk, n=None, w_off=0, res=None, out_dtype=F32):
    m_dim, k_dim = a.shape
    n_dim = n if n is not None else (w.shape[0] if nt else w.shape[1])
    gm, gn, gk = m_dim // tm, n_dim // tn, k_dim // tk
    assert gm * tm == m_dim and gn * tn == n_dim and gk * tk == k_dim, (name, a.shape, w.shape, tm, tn, tk)
    dn = (((1,), (1,)), ((), ())) if nt else (((1,), (0,)), ((), ()))

    def body(*refs):
        if res is None:
            a_ref, w_ref, o_ref, acc = refs
            r_ref = None
        else:
            a_ref, w_ref, r_ref, o_ref, acc = refs
        k = pl.program_id(2)

        @pl.when(k == 0)
        def _():
            acc[...] = jnp.zeros_like(acc)

        acc[...] += lax.dot_general(a_ref[...], w_ref[...], dn, preferred_element_type=F32)

        @pl.when(k == gk - 1)
        def _():
            r = acc[...]
            if r_ref is not None:
                r = r_ref[...] + r
            o_ref[...] = r.astype(out_dtype)

    if nt:
        w_spec = pl.BlockSpec((tn, tk), lambda i, j, k: (j + w_off, k))
    else:
        w_spec = pl.BlockSpec((tk, tn), lambda i, j, k: (k, j))
    in_specs = [pl.BlockSpec((tm, tk), lambda i, j, k: (i, k)), w_spec]
    args = [a, w]
    if res is not None:
        in_specs.append(pl.BlockSpec((tm, tn), lambda i, j, k: (i, j)))
        args.append(res)
    return pl.pallas_call(
        body, name=name, grid=(gm, gn, gk),
        in_specs=in_specs,
        out_specs=pl.BlockSpec((tm, tn), lambda i, j, k: (i, j)),
        out_shape=jax.ShapeDtypeStruct((m_dim, n_dim), out_dtype),
        scratch_shapes=[pltpu.VMEM((tm, tn), F32)],
        compiler_params=_params(dimension_semantics=("parallel", "parallel", "arbitrary")),
    )(*args)


def _matmul_tn(name, a, b, *, tm, tn, tl):
    l_dim, m_dim = a.shape
    n_dim = b.shape[1]
    gm, gn, gl = m_dim // tm, n_dim // tn, l_dim // tl
    assert gm * tm == m_dim and gn * tn == n_dim and gl * tl == l_dim, (name, a.shape, b.shape, tm, tn, tl)

    def body(a_ref, b_ref, o_ref):
        @pl.when(pl.program_id(2) == 0)
        def _():
            o_ref[...] = jnp.zeros_like(o_ref)

        o_ref[...] += lax.dot_general(a_ref[...], b_ref[...], (((0,), (0,)), ((), ())),
                                      preferred_element_type=F32)

    return pl.pallas_call(
        body, name=name, grid=(gm, gn, gl),
        in_specs=[pl.BlockSpec((tl, tm), lambda i, j, l: (l, i)),
                  pl.BlockSpec((tl, tn), lambda i, j, l: (l, j))],
        out_specs=pl.BlockSpec((tm, tn), lambda i, j, l: (i, j)),
        out_shape=jax.ShapeDtypeStruct((m_dim, n_dim), F32),
        compiler_params=_params(dimension_semantics=("parallel", "parallel", "arbitrary")),
    )(a, b)


def _row_spec(tm, cols, f=None):
    if f is None:
        return pl.BlockSpec((tm, cols), lambda i: (i, 0))
    return pl.BlockSpec((tm, cols), lambda i: (f(i), 0))


def _full_spec(shape):
    nd = len(shape)
    return pl.BlockSpec(shape, lambda i: (0,) * nd)


def _embed_norm(x2d, meta_pad, g):
    s_len = x2d.shape[0]
    nb = s_len // BLK + 1

    def body(x_ref, mp_ref, g_ref, h_ref, xn_ref):
        i = pl.program_id(0)

        @pl.when(i == 0)
        def _():
            h_ref[...] = mp_ref[...]

        @pl.when(i > 0)
        def _():
            h_ref[...] = x_ref[...]

        xn_ref[...] = _rms(h_ref[...], g_ref[...]).astype(BF16)

    return pl.pallas_call(
        body, name="embed_norm", grid=(nb,),
        in_specs=[_row_spec(BLK, D_MODEL, lambda i: jnp.maximum(i - 1, 0)),
                  _full_spec((BLK, D_MODEL)), _full_spec((1, D_MODEL))],
        out_specs=[_row_spec(BLK, D_MODEL), _row_spec(BLK, D_MODEL)],
        out_shape=[jax.ShapeDtypeStruct((nb * BLK, D_MODEL), F32),
                   jax.ShapeDtypeStruct((nb * BLK, D_MODEL), BF16)],
        compiler_params=_params(),
    )(x2d, meta_pad, g)


def _qk_prep(qkv, q_norm_t, k_norm_t, e_mat, tm):
    l_dim = qkv.shape[0]

    def body(x_ref, qg_ref, kg_ref, e_ref, q_ref, kf_ref, vf_ref):
        x = x_ref[...]
        q = _head_rms(x[:, :Q_W], qg_ref[...], e_ref) * (HEAD_DIM ** -0.5)
        q_ref[...] = q.astype(BF16)
        k = _head_rms(x[:, Q_W:Q_W + KV_W], kg_ref[...], e_ref)
        v = x[:, Q_W + KV_W:Q_W + 2 * KV_W]
        half = _lane_half((tm, BLK))
        for src, dst in ((k, kf_ref), (v, vf_ref)):
            for kv in range(N_KV_HEADS):
                blk = src[:, BLK * (kv // 2):BLK * (kv // 2) + BLK]
                swapped = pltpu.roll(blk, HEAD_DIM, axis=1)
                for e in range(2):
                    val = blk if kv % 2 == e else swapped
                    idx = 2 * kv + e
                    dst[:, BLK * idx:BLK * idx + BLK] = jnp.where(half == e, val, 0.0).astype(BF16)

    return pl.pallas_call(
        body, name="qk_prep", grid=(l_dim // tm,),
        in_specs=[_row_spec(tm, Q_W + 2 * KV_W), _full_spec((1, Q_W)), _full_spec((1, KV_W)),
                  _full_spec((256, 256))],
        out_specs=[_row_spec(tm, Q_W), _row_spec(tm, 8 * BLK), _row_spec(tm, 8 * BLK)],
        out_shape=[jax.ShapeDtypeStruct((l_dim, Q_W), BF16),
                   jax.ShapeDtypeStruct((l_dim, 8 * BLK), BF16),
                   jax.ShapeDtypeStruct((l_dim, 8 * BLK), BF16)],
        compiler_params=_params(),
    )(qkv, q_norm_t, k_norm_t, e_mat)


def _merge_fwd(attn, zab, gates, abn, sbn, tm):
    l_dim = attn.shape[0]

    def body(a_ref, z_ref, g_ref, an_ref, sn_ref, o_ref):
        z = z_ref[...]
        g = g_ref[...]
        ssm = z[:, :D_MODEL] * _sigmoid(z[:, D_MODEL:])
        merged = (_sigmoid(g[:, :D_MODEL]) * _rms(a_ref[...], an_ref[...])
                  + _sigmoid(g[:, D_MODEL:]) * _rms(ssm, sn_ref[...]))
        o_ref[...] = merged.astype(BF16)

    return pl.pallas_call(
        body, name="merge_fwd", grid=(l_dim // tm,),
        in_specs=[_row_spec(tm, D_MODEL), _row_spec(tm, 2 * D_MODEL), _row_spec(tm, 2 * D_MODEL),
                  _full_spec((1, D_MODEL)), _full_spec((1, D_MODEL))],
        out_specs=_row_spec(tm, D_MODEL),
        out_shape=jax.ShapeDtypeStruct((l_dim, D_MODEL), BF16),
        compiler_params=_params(),
    )(attn, zab, gates, abn, sbn)


def _merge_bwd(attn, zab, gates, abn, sbn, dmerged, tm):
    l_dim = attn.shape[0]

    def body(a_ref, z_ref, g_ref, an_ref, sn_ref, dm_ref, da_ref, dz_ref, dg_ref, dan_ref, dsn_ref):
        @pl.when(pl.program_id(0) == 0)
        def _():
            dan_ref[...] = jnp.zeros_like(dan_ref)
            dsn_ref[...] = jnp.zeros_like(dsn_ref)

        z = z_ref[...]
        g = g_ref[...]
        dm = dm_ref[...]
        attn_v = a_ref[...]
        za, zb = z[:, :D_MODEL], z[:, D_MODEL:]
        sb = _sigmoid(zb)
        ssm = za * sb
        s_ga, s_gs = _sigmoid(g[:, :D_MODEL]), _sigmoid(g[:, D_MODEL:])
        a_n = _rms(attn_v, an_ref[...])
        s_n = _rms(ssm, sn_ref[...])
        dg_ref[:, :D_MODEL] = (dm * a_n * s_ga * (1.0 - s_ga)).astype(BF16)
        dg_ref[:, D_MODEL:] = (dm * s_n * s_gs * (1.0 - s_gs)).astype(BF16)
        dattn, dan = _rms_bwd(attn_v, an_ref[...], dm * s_ga)
        dssm, dsn = _rms_bwd(ssm, sn_ref[...], dm * s_gs)
        da_ref[...] = dattn
        dz_ref[:, :D_MODEL] = (dssm * sb).astype(BF16)
        dz_ref[:, D_MODEL:] = (dssm * za * sb * (1.0 - sb)).astype(BF16)
        dan_ref[...] += dan
        dsn_ref[...] += dsn

    return pl.pallas_call(
        body, name="merge_bwd", grid=(l_dim // tm,),
        in_specs=[_row_spec(tm, D_MODEL), _row_spec(tm, 2 * D_MODEL), _row_spec(tm, 2 * D_MODEL),
                  _full_spec((1, D_MODEL)), _full_spec((1, D_MODEL)), _row_spec(tm, D_MODEL)],
        out_specs=[_row_spec(tm, D_MODEL), _row_spec(tm, 2 * D_MODEL), _row_spec(tm, 2 * D_MODEL),
                   _full_spec((1, D_MODEL)), _full_spec((1, D_MODEL))],
        out_shape=[jax.ShapeDtypeStruct((l_dim, D_MODEL), F32),
                   jax.ShapeDtypeStruct((l_dim, 2 * D_MODEL), BF16),
                   jax.ShapeDtypeStruct((l_dim, 2 * D_MODEL), BF16),
                   jax.ShapeDtypeStruct((1, D_MODEL), F32), jax.ShapeDtypeStruct((1, D_MODEL), F32)],
        compiler_params=_params(),
    )(attn, zab, gates, abn, sbn, dmerged)


def _norm_cast(name, h, g, tm):
    l_dim = h.shape[0]

    def body(h_ref, g_ref, o_ref):
        o_ref[...] = _rms(h_ref[...], g_ref[...]).astype(BF16)

    return pl.pallas_call(
        body, name=name, grid=(l_dim // tm,),
        in_specs=[_row_spec(tm, D_MODEL), _full_spec((1, D_MODEL))],
        out_specs=_row_spec(tm, D_MODEL),
        out_shape=jax.ShapeDtypeStruct((l_dim, D_MODEL), BF16),
        compiler_params=_params(),
    )(h, g)


def _swiglu_fwd(gu, tm):
    l_dim = gu.shape[0]

    def body(x_ref, o_ref):
        x = x_ref[...]
        gate, up = x[:, :D_FF], x[:, D_FF:]
        o_ref[...] = (gate * _sigmoid(gate) * up).astype(BF16)

    return pl.pallas_call(
        body, name="swiglu_fwd", grid=(l_dim // tm,),
        in_specs=[_row_spec(tm, 2 * D_FF)],
        out_specs=_row_spec(tm, D_FF),
        out_shape=jax.ShapeDtypeStruct((l_dim, D_FF), BF16),
        compiler_params=_params(),
    )(gu)


def _swiglu_bwd(gu, dact, tm):
    l_dim = gu.shape[0]

    def body(x_ref, d_ref, o_ref):
        x = x_ref[...]
        d = d_ref[...]
        gate, up = x[:, :D_FF], x[:, D_FF:]
        s = _sigmoid(gate)
        o_ref[:, :D_FF] = (d * up * (s * (1.0 + gate * (1.0 - s)))).astype(BF16)
        o_ref[:, D_FF:] = (d * (gate * s)).astype(BF16)

    return pl.pallas_call(
        body, name="swiglu_bwd", grid=(l_dim // tm,),
        in_specs=[_row_spec(tm, 2 * D_FF), _row_spec(tm, D_FF)],
        out_specs=_row_spec(tm, 2 * D_FF),
        out_shape=jax.ShapeDtypeStruct((l_dim, 2 * D_FF), BF16),
        compiler_params=_params(),
    )(gu, dact)


def _loss_grad(h2, target2d):
    l_dim = h2.shape[0]
    nb = l_dim // BLK

    def body(h_ref, t_ref, d_ref, db_ref, loss_ref):
        i = pl.program_id(0)

        @pl.when(i == 0)
        def _():
            loss_ref[...] = jnp.zeros_like(loss_ref)
            d_ref[...] = jnp.zeros_like(d_ref)
            db_ref[...] = jnp.zeros_like(db_ref)

        @pl.when(i > 0)
        def _():
            err = h_ref[...] - t_ref[...]
            d = err * (1.0 / D_MODEL)
            d_ref[...] = d
            db_ref[...] = d.astype(BF16)
            loss_ref[...] += 0.5 * jnp.sum(jnp.mean(err * err, axis=-1, keepdims=True), axis=0, keepdims=True)

    return pl.pallas_call(
        body, name="loss_grad", grid=(nb,),
        in_specs=[_row_spec(BLK, D_MODEL), _row_spec(BLK, D_MODEL, lambda i: jnp.maximum(i - 1, 0))],
        out_specs=[_row_spec(BLK, D_MODEL), _row_spec(BLK, D_MODEL), _full_spec((1, 1))],
        out_shape=[jax.ShapeDtypeStruct((l_dim, D_MODEL), F32), jax.ShapeDtypeStruct((l_dim, D_MODEL), BF16),
                   jax.ShapeDtypeStruct((1, 1), F32)],
        compiler_params=_params(),
    )(h2, target2d)


def _norm_bwd_res(name, h, g, dy, dres, tm):
    l_dim = h.shape[0]

    def body(h_ref, g_ref, dy_ref, dr_ref, o_ref, ob_ref, dg_ref):
        @pl.when(pl.program_id(0) == 0)
        def _():
            dg_ref[...] = jnp.zeros_like(dg_ref)

        dx, dg = _rms_bwd(h_ref[...], g_ref[...], dy_ref[...])
        out = dr_ref[...] + dx
        o_ref[...] = out
        ob_ref[...] = out.astype(BF16)
        dg_ref[...] += dg

    return pl.pallas_call(
        body, name=name, grid=(l_dim // tm,),
        in_specs=[_row_spec(tm, D_MODEL), _full_spec((1, D_MODEL)), _row_spec(tm, D_MODEL), _row_spec(tm, D_MODEL)],
        out_specs=[_row_spec(tm, D_MODEL), _row_spec(tm, D_MODEL), _full_spec((1, D_MODEL))],
        out_shape=[jax.ShapeDtypeStruct((l_dim, D_MODEL), F32), jax.ShapeDtypeStruct((l_dim, D_MODEL), BF16),
                   jax.ShapeDtypeStruct((1, D_MODEL), F32)],
        compiler_params=_params(),
    )(h, g, dy, dres)


def _final_bwd(h0, g, dxn, dh1):
    l_dim = h0.shape[0]
    nb = l_dim // BLK

    def body(h_ref, g_ref, dy_ref, dr_ref, gx_ref, gm_ref, dg_ref):
        i = pl.program_id(0)

        @pl.when(i == 0)
        def _():
            dg_ref[...] = jnp.zeros_like(dg_ref)

        dx, dg = _rms_bwd(h_ref[...], g_ref[...], dy_ref[...])
        out = dr_ref[...] + dx
        dg_ref[...] += dg

        @pl.when(i == 0)
        def _():
            gm_ref[...] = out

        @pl.when(i > 0)
        def _():
            gx_ref[...] = out

    return pl.pallas_call(
        body, name="final_bwd", grid=(nb,),
        in_specs=[_row_spec(BLK, D_MODEL), _full_spec((1, D_MODEL)), _row_spec(BLK, D_MODEL),
                  _row_spec(BLK, D_MODEL)],
        out_specs=[_row_spec(BLK, D_MODEL, lambda i: jnp.maximum(i - 1, 0)), _full_spec((BLK, D_MODEL)),
                   _full_spec((1, D_MODEL))],
        out_shape=[jax.ShapeDtypeStruct((l_dim - BLK, D_MODEL), F32), jax.ShapeDtypeStruct((BLK, D_MODEL), F32),
                   jax.ShapeDtypeStruct((1, D_MODEL), F32)],
        compiler_params=_params(),
    )(h0, g, dxn, dh1)


def _attn_valid(n):
    shape = (2 * BLK, 3 * BLK)
    qi = lax.broadcasted_iota(jnp.int32, shape, 0) & (BLK - 1)
    col = lax.broadcasted_iota(jnp.int32, shape, 1)
    kj = col & (BLK - 1)
    part = col >> 7
    nn = jnp.zeros(shape, jnp.int32) + n
    meta_ok = (part == 0) & (kj >= PAD) & (nn >= 1)
    prev_ok = (part == 1) & (kj > qi) & (nn >= 2)
    cur_ok = (part == 2) & (kj <= qi) & ((nn >= 1) | (kj >= PAD))
    return meta_ok | prev_ok | cur_ok


def _attn_probs(q_ref, kwin, sk_ref, valid, kv, e):
    qs = jnp.concatenate([q_ref[:, BLK * (2 * kv):BLK * (2 * kv) + BLK],
                          q_ref[:, BLK * (2 * kv + 1):BLK * (2 * kv + 1) + BLK]], axis=0)
    s = lax.dot_general(qs, kwin, (((1,), (1,)), ((), ())), preferred_element_type=F32)
    s = jnp.where(valid, s, NEG)
    h0 = 4 * kv + e
    row = lax.broadcasted_iota(jnp.int32, (2 * BLK, 1), 0)
    sink = jnp.where(row < BLK, sk_ref[:, h0:h0 + 1], sk_ref[:, h0 + 2:h0 + 3])
    m = jnp.maximum(jnp.max(s, axis=-1, keepdims=True), sink)
    ex = jnp.exp(s - m)
    es = jnp.exp(sink - m)
    inv = 1.0 / (jnp.sum(ex, axis=-1, keepdims=True) + es)
    return qs, ex * inv, es * inv


def _attn_specs(nb):
    prev = lambda i: jnp.maximum(i - 1, 0)
    zero = lambda i: 0
    kv_specs = [_row_spec(BLK, 8 * BLK, zero), _row_spec(BLK, 8 * BLK, prev), _row_spec(BLK, 8 * BLK)]
    return kv_specs


def _attn_fwd(qn, kf, vf, sinks):
    l_dim = qn.shape[0]
    nb = l_dim // BLK

    def body(q_ref, km_ref, kp_ref, kc_ref, vm_ref, vp_ref, vc_ref, sk_ref, o_ref):
        valid = _attn_valid(pl.program_id(0))
        for kv in range(N_KV_HEADS):
            outs = []
            for e in range(2):
                sl = slice(BLK * (2 * kv + e), BLK * (2 * kv + e) + BLK)
                kwin = jnp.concatenate([km_ref[:, sl], kp_ref[:, sl], kc_ref[:, sl]], axis=0)
                vwin = jnp.concatenate([vm_ref[:, sl], vp_ref[:, sl], vc_ref[:, sl]], axis=0)
                _, p, _ = _attn_probs(q_ref, kwin, sk_ref, valid, kv, e)
                outs.append(jnp.dot(p.astype(BF16), vwin, preferred_element_type=F32))
            o = outs[0] + outs[1]
            o_ref[:, BLK * (2 * kv):BLK * (2 * kv) + BLK] = o[:BLK]
            o_ref[:, BLK * (2 * kv + 1):BLK * (2 * kv + 1) + BLK] = o[BLK:]

    kv_specs = _attn_specs(nb)
    return pl.pallas_call(
        body, name="attn_fwd", grid=(nb,),
        in_specs=[_row_spec(BLK, Q_W)] + kv_specs + kv_specs + [_full_spec((1, N_Q_HEADS))],
        out_specs=_row_spec(BLK, Q_W),
        out_shape=jax.ShapeDtypeStruct((l_dim, Q_W), F32),
        compiler_params=_params(),
    )(qn, kf, kf, kf, vf, vf, vf, sinks)


def _attn_bwd(qn, kf, vf, sinks, attn, dattn):
    l_dim = qn.shape[0]
    nb = l_dim // BLK
    wide = 8 * BLK

    def body(q_ref, km_ref, kp_ref, kc_ref, vm_ref, vp_ref, vc_ref, sk_ref, o_ref, do_ref,
             dq_ref, dkc_ref, dkp_ref, dkm_ref, dvc_ref, dvp_ref, dvm_ref, dsk_ref):
        @pl.when(pl.program_id(0) == 0)
        def _():
            dkm_ref[...] = jnp.zeros_like(dkm_ref)
            dvm_ref[...] = jnp.zeros_like(dvm_ref)
            dsk_ref[...] = jnp.zeros_like(dsk_ref)

        valid = _attn_valid(pl.program_id(0))
        half = _lane_half((BLK, BLK))
        lane16 = lax.broadcasted_iota(jnp.int32, (1, N_Q_HEADS), 1)
        dsk = jnp.zeros((1, N_Q_HEADS), F32)
        for kv in range(N_KV_HEADS):
            j0, j1 = 2 * kv, 2 * kv + 1
            do0 = do_ref[:, BLK * j0:BLK * j0 + BLK]
            do1 = do_ref[:, BLK * j1:BLK * j1 + BLK]
            prod0 = do0 * o_ref[:, BLK * j0:BLK * j0 + BLK]
            prod1 = do1 * o_ref[:, BLK * j1:BLK * j1 + BLK]
            dos = jnp.concatenate([do0, do1], axis=0).astype(BF16)
            dqs = []
            for e in range(2):
                sl = slice(BLK * (2 * kv + e), BLK * (2 * kv + e) + BLK)
                kwin = jnp.concatenate([km_ref[:, sl], kp_ref[:, sl], kc_ref[:, sl]], axis=0)
                vwin = jnp.concatenate([vm_ref[:, sl], vp_ref[:, sl], vc_ref[:, sl]], axis=0)
                qs, p, p_sink = _attn_probs(q_ref, kwin, sk_ref, valid, kv, e)
                delta = jnp.concatenate(
                    [jnp.sum(jnp.where(half == e, prod0, 0.0), axis=-1, keepdims=True),
                     jnp.sum(jnp.where(half == e, prod1, 0.0), axis=-1, keepdims=True)], axis=0)
                dp = lax.dot_general(dos, vwin, (((1,), (1,)), ((), ())), preferred_element_type=F32)
                ds = (p * (dp - delta)).astype(BF16)
                pb = p.astype(BF16)
                dqs.append(jnp.dot(ds, kwin, preferred_element_type=F32))
                dk = lax.dot_general(ds, qs, (((0,), (0,)), ((), ())), preferred_element_type=F32)
                dv = lax.dot_general(pb, dos, (((0,), (0,)), ((), ())), preferred_element_type=F32)
                dkm_ref[:, sl] += dk[:BLK]
                dkp_ref[:, sl] = dk[BLK:2 * BLK]
                dkc_ref[:, sl] = dk[2 * BLK:]
                dvm_ref[:, sl] += dv[:BLK]
                dvp_ref[:, sl] = dv[BLK:2 * BLK]
                dvc_ref[:, sl] = dv[2 * BLK:]
                sink_g = -(p_sink * delta)
                g_lo = jnp.sum(sink_g[:BLK], axis=0, keepdims=True)
                g_hi = jnp.sum(sink_g[BLK:], axis=0, keepdims=True)
                dsk = dsk + jnp.where(lane16 == 4 * kv + e, g_lo, 0.0) + jnp.where(lane16 == 4 * kv + 2 + e, g_hi, 0.0)
            dq = jnp.where(jnp.concatenate([half, half], axis=0) == 0, dqs[0], dqs[1])
            dq_ref[:, BLK * j0:BLK * j0 + BLK] = dq[:BLK]
            dq_ref[:, BLK * j1:BLK * j1 + BLK] = dq[BLK:]
        dsk_ref[...] += dsk

    kv_specs = _attn_specs(nb)
    row_wide = _row_spec(BLK, wide)
    acc_wide = _full_spec((BLK, wide))
    big = jax.ShapeDtypeStruct((l_dim, wide), F32)
    return pl.pallas_call(
        body, name="attn_bwd", grid=(nb,),
        in_specs=[_row_spec(BLK, Q_W)] + kv_specs + kv_specs
        + [_full_spec((1, N_Q_HEADS)), _row_spec(BLK, Q_W), _row_spec(BLK, Q_W)],
        out_specs=[_row_spec(BLK, Q_W), row_wide, row_wide, acc_wide, row_wide, row_wide, acc_wide,
                   _full_spec((1, N_Q_HEADS))],
        out_shape=[jax.ShapeDtypeStruct((l_dim, Q_W), F32), big, big, jax.ShapeDtypeStruct((BLK, wide), F32),
                   big, big, jax.ShapeDtypeStruct((BLK, wide), F32), jax.ShapeDtypeStruct((1, N_Q_HEADS), F32)],
        compiler_params=_params(),
    )(qn, kf, kf, kf, vf, vf, vf, sinks, attn, dattn)


def _qk_bwd(qkv, q_norm_t, k_norm_t, e_mat, dq, dkc, dkp, dkm, dvc, dvp, dvm):
    l_dim = qkv.shape[0]
    nb = l_dim // BLK
    wide = 8 * BLK

    def fold(x):
        half = _lane_half((BLK, BLK))
        blocks = []
        for kb in range(2):
            t = []
            for kv in (2 * kb, 2 * kb + 1):
                own = kv % 2
                a = x[:, BLK * (2 * kv + own):BLK * (2 * kv + own) + BLK]
                b = pltpu.roll(x[:, BLK * (2 * kv + 1 - own):BLK * (2 * kv + 1 - own) + BLK], HEAD_DIM, axis=1)
                t.append(a + b)
            blocks.append(jnp.where(half == 0, t[0], t[1]))
        return jnp.concatenate(blocks, axis=1)

    def body(x_ref, qg_ref, kg_ref, e_ref, dq_ref, dkc_ref, dkp_ref, dkm_ref, dvc_ref, dvp_ref, dvm_ref,
             o_ref, dqg_ref, dkg_ref):
        i = pl.program_id(0)

        @pl.when(i == 0)
        def _():
            dqg_ref[...] = jnp.zeros_like(dqg_ref)
            dkg_ref[...] = jnp.zeros_like(dkg_ref)

        first = jnp.where(i == 0, 1.0, 0.0)
        not_last = jnp.where(i < nb - 1, 1.0, 0.0)
        dk_x = dkc_ref[...] + not_last * dkp_ref[...] + first * dkm_ref[...]
        dv_x = dvc_ref[...] + not_last * dvp_ref[...] + first * dvm_ref[...]
        x = x_ref[...]
        dqx, dqg = _head_rms_bwd(x[:, :Q_W], qg_ref[...], dq_ref[...] * (HEAD_DIM ** -0.5), e_ref)
        dkx, dkg = _head_rms_bwd(x[:, Q_W:Q_W + KV_W], kg_ref[...], fold(dk_x), e_ref)
        o_ref[:, :Q_W] = dqx.astype(BF16)
        o_ref[:, Q_W:Q_W + KV_W] = dkx.astype(BF16)
        o_ref[:, Q_W + KV_W:] = fold(dv_x).astype(BF16)
        dqg_ref[...] += dqg
        dkg_ref[...] += dkg

    nxt = lambda i: jnp.minimum(i + 1, nb - 1)
    row_wide = _row_spec(BLK, wide)
    nxt_wide = _row_spec(BLK, wide, nxt)
    acc_wide = _full_spec((BLK, wide))
    return pl.pallas_call(
        body, name="qk_bwd", grid=(nb,),
        in_specs=[_row_spec(BLK, Q_W + 2 * KV_W), _full_spec((1, Q_W)), _full_spec((1, KV_W)),
                  _full_spec((256, 256)), _row_spec(BLK, Q_W),
                  row_wide, nxt_wide, acc_wide, row_wide, nxt_wide, acc_wide],
        out_specs=[_row_spec(BLK, Q_W + 2 * KV_W), _full_spec((1, Q_W)), _full_spec((1, KV_W))],
        out_shape=[jax.ShapeDtypeStruct((l_dim, Q_W + 2 * KV_W), BF16),
                   jax.ShapeDtypeStruct((1, Q_W), F32), jax.ShapeDtypeStruct((1, KV_W), F32)],
        compiler_params=_params(),
    )(qkv, q_norm_t, k_norm_t, e_mat, dq, dkc, dkp, dkm, dvc, dvp, dvm)


GRP = 8


def _strided(r, g):
    return pl.ds(r, g, stride=GRP)


def _slab(ref, base, r, g):
    return jnp.concatenate([ref[base + i, _strided(r, g), :] for i in range(LB_KB)], axis=1)


def _slab_store(ref, base, r, g, val):
    for i in range(LB_KB):
        ref[base + i, _strided(r, g), :] = val[:, BLK * i:BLK * i + BLK]


def _group_totals(xr_ref, xi_ref, base, ar, ai, reverse):
    g = xr_ref.shape[1] // GRP
    sr = si = None
    for r in (range(GRP - 1, -1, -1) if reverse else range(GRP)):
        xr, xi = _slab(xr_ref, base, r, g), _slab(xi_ref, base, r, g)
        if sr is not None:
            xr, xi = xr + ar * sr - ai * si, xi + ar * si + ai * sr
        sr, si = xr, xi
    return sr, si


def _carry_scan(tr, ti, sqr_ref, sqi_ref, lanes, sign, reverse):
    g = tr.shape[0]
    row = lax.broadcasted_iota(jnp.int32, tr.shape, 0)
    idx, s = 0, 1
    while s < g:
        ar = sqr_ref[idx:idx + 1, lanes]
        ai = sign * sqi_ref[idx:idx + 1, lanes]
        shift, keep = (g - s, row < g - s) if reverse else (s, row >= s)
        pr = jnp.where(keep, pltpu.roll(tr, shift, axis=0), 0.0)
        pi = jnp.where(keep, pltpu.roll(ti, shift, axis=0), 0.0)
        tr, ti = tr + ar * pr - ai * pi, ti + ar * pi + ai * pr
        idx, s = idx + 1, 2 * s
    return tr, ti


def _ssm_fwd(u, wb_re, wb_im, wc_re, wc_im, d_skip, tabs):
    l_dim = u.shape[0]
    nb = l_dim // BLK
    g = BLK // GRP

    def body(u_ref, wbr_ref, wbi_ref, wcr_ref, wci_ref, d_ref, a1r_ref, a1i_ref, sqr_ref, sqi_ref,
             seqr_ref, seqi_ref, y_ref, z_ref, sr_ref, si_ref, cr_ref, ci_ref, xr_ref, xi_ref):
        @pl.when(pl.program_id(0) == 0)
        def _():
            cr_ref[...] = jnp.zeros_like(cr_ref)
            ci_ref[...] = jnp.zeros_like(ci_ref)

        row = lax.broadcasted_iota(jnp.int32, (g, ST_KB), 0)
        for kb in range(SSM_KB):
            ch = slice(BLK * kb, BLK * kb + BLK)
            lanes = slice(ST_KB * kb, ST_KB * kb + ST_KB)
            u_kb = u_ref[:, ch]
            ub = u_kb.astype(BF16)
            xr = jnp.dot(ub, wbr_ref[kb], preferred_element_type=F32)
            xi = jnp.dot(ub, wbi_ref[kb], preferred_element_type=F32)
            base = LB_KB * kb
            for i in range(LB_KB):
                xr_ref[base + i] = xr[:, BLK * i:BLK * i + BLK]
                xi_ref[base + i] = xi[:, BLK * i:BLK * i + BLK]
            ar, ai = a1r_ref[0:1, lanes], a1i_ref[0:1, lanes]
            tr, ti = _group_totals(xr_ref, xi_ref, base, ar, ai, reverse=False)
            tr, ti = _carry_scan(tr, ti, sqr_ref, sqi_ref, lanes, 1.0, reverse=False)
            cin_r, cin_i = cr_ref[0:1, lanes], ci_ref[0:1, lanes]
            qr, qi = seqr_ref[:, lanes], seqi_ref[:, lanes]
            tr, ti = tr + qr * cin_r - qi * cin_i, ti + qr * cin_i + qi * cin_r
            cr_ref[0:1, lanes] = jnp.sum(jnp.where(row == g - 1, tr, 0.0), axis=0, keepdims=True)
            ci_ref[0:1, lanes] = jnp.sum(jnp.where(row == g - 1, ti, 0.0), axis=0, keepdims=True)
            pr = jnp.where(row == 0, cin_r, pltpu.roll(tr, 1, axis=0))
            pi = jnp.where(row == 0, cin_i, pltpu.roll(ti, 1, axis=0))
            for r in range(GRP):
                pr, pi = (_slab(xr_ref, base, r, g) + ar * pr - ai * pi,
                          _slab(xi_ref, base, r, g) + ar * pi + ai * pr)
                _slab_store(sr_ref, base, r, g, pr)
                _slab_store(si_ref, base, r, g, pi)
            s_r = jnp.concatenate([sr_ref[LB_KB * kb + i] for i in range(LB_KB)], axis=1)
            s_i = jnp.concatenate([si_ref[LB_KB * kb + i] for i in range(LB_KB)], axis=1)
            y = (jnp.dot(s_r.astype(BF16), wcr_ref[kb], preferred_element_type=F32)
                 - jnp.dot(s_i.astype(BF16), wci_ref[kb], preferred_element_type=F32)
                 + d_ref[:, ch] * u_kb)
            y_ref[:, ch] = y
            z_ref[:, ch] = _gelu(y).astype(BF16)

    wb_spec = _full_spec((SSM_KB, BLK, ST_KB))
    wc_spec = _full_spec((SSM_KB, ST_KB, BLK))
    tab_specs = [_full_spec(t.shape) for t in tabs[:6]]
    state_spec = pl.BlockSpec((N_LB, BLK, BLK), lambda i: (0, i, 0))
    state_shape = jax.ShapeDtypeStruct((N_LB, l_dim, BLK), F32)
    return pl.pallas_call(
        body, name="ssm_fwd", grid=(nb,),
        in_specs=[_row_spec(BLK, D_MODEL), wb_spec, wb_spec, wc_spec, wc_spec, _full_spec((1, D_MODEL))]
        + tab_specs,
        out_specs=[_row_spec(BLK, D_MODEL), _row_spec(BLK, D_MODEL), state_spec, state_spec],
        out_shape=[jax.ShapeDtypeStruct((l_dim, D_MODEL), F32), jax.ShapeDtypeStruct((l_dim, D_MODEL), BF16),
                   state_shape, state_shape],
        scratch_shapes=[pltpu.VMEM((8, N_STATE), F32), pltpu.VMEM((8, N_STATE), F32),
                        pltpu.VMEM((N_LB, BLK, BLK), F32), pltpu.VMEM((N_LB, BLK, BLK), F32)],
        compiler_params=_params(),
    )(u, wb_re, wb_im, wc_re, wc_im, d_skip, *tabs[:6])


def _ssm_bwd(dz, y, u, s_re, s_im, wb_re, wb_im, wc_re, wc_im, d_skip, tabs):
    l_dim = u.shape[0]
    nb = l_dim // BLK
    g = BLK // GRP

    def body(dz_ref, y_ref, u_ref, sr_ref, si_ref, wbr_ref, wbi_ref, wcr_ref, wci_ref, d_ref,
             a1r_ref, a1i_ref, sqr_ref, sqi_ref, revr_ref, revi_ref,
             du_ref, dd_ref, dar_ref, dai_ref, dwbr_ref, dwbi_ref, dwcr_ref, dwci_ref,
             cr_ref, ci_ref, gr_ref, gi_ref):
        @pl.when(pl.program_id(0) == 0)
        def _():
            for r in (cr_ref, ci_ref, dd_ref, dar_ref, dai_ref, dwbr_ref, dwbi_ref, dwcr_ref, dwci_ref):
                r[...] = jnp.zeros_like(r)

        tn = (((0,), (0,)), ((), ()))
        nt = (((1,), (1,)), ((), ()))
        row = lax.broadcasted_iota(jnp.int32, (g, ST_KB), 0)
        for kb in range(SSM_KB):
            ch = slice(BLK * kb, BLK * kb + BLK)
            lanes = slice(ST_KB * kb, ST_KB * kb + ST_KB)
            u_kb = u_ref[:, ch]
            dy = dz_ref[:, ch] * _gelu_grad(y_ref[:, ch])
            dyb = dy.astype(BF16)
            ub = u_kb.astype(BF16)
            dd_ref[:, ch] += jnp.sum(dy * u_kb, axis=0, keepdims=True)
            ds_r = lax.dot_general(dyb, wcr_ref[kb], nt, preferred_element_type=F32)
            ds_i = -lax.dot_general(dyb, wci_ref[kb], nt, preferred_element_type=F32)
            base = LB_KB * kb
            for i in range(LB_KB):
                gr_ref[base + i] = ds_r[:, BLK * i:BLK * i + BLK]
                gi_ref[base + i] = ds_i[:, BLK * i:BLK * i + BLK]
            ar, ai = a1r_ref[0:1, lanes], -a1i_ref[0:1, lanes]
            tr, ti = _group_totals(gr_ref, gi_ref, base, ar, ai, reverse=True)
            tr, ti = _carry_scan(tr, ti, sqr_ref, sqi_ref, lanes, -1.0, reverse=True)
            cin_r, cin_i = cr_ref[0:1, lanes], ci_ref[0:1, lanes]
            qr, qi = revr_ref[:, lanes], -revi_ref[:, lanes]
            tr, ti = tr + qr * cin_r - qi * cin_i, ti + qr * cin_i + qi * cin_r
            cr_ref[0:1, lanes] = jnp.sum(jnp.where(row == 0, tr, 0.0), axis=0, keepdims=True)
            ci_ref[0:1, lanes] = jnp.sum(jnp.where(row == 0, ti, 0.0), axis=0, keepdims=True)
            nr = jnp.where(row == g - 1, cin_r, pltpu.roll(tr, g - 1, axis=0))
            ni = jnp.where(row == g - 1, cin_i, pltpu.roll(ti, g - 1, axis=0))
            acc_r = jnp.zeros((g, ST_KB), F32)
            acc_i = jnp.zeros((g, ST_KB), F32)
            for r in range(GRP - 1, -1, -1):
                s_r, s_i = _slab(sr_ref, base, r, g), _slab(si_ref, base, r, g)
                acc_r = acc_r + (nr * s_r + ni * s_i)
                acc_i = acc_i + (ni * s_r - nr * s_i)
                nr, ni = (_slab(gr_ref, base, r, g) + ar * nr - ai * ni,
                          _slab(gi_ref, base, r, g) + ar * ni + ai * nr)
                _slab_store(gr_ref, base, r, g, nr)
                _slab_store(gi_ref, base, r, g, ni)
            dar_ref[:, lanes] += jnp.sum(acc_r, axis=0, keepdims=True)
            dai_ref[:, lanes] += jnp.sum(acc_i, axis=0, keepdims=True)
            grb = jnp.concatenate([gr_ref[base + i] for i in range(LB_KB)], axis=1).astype(BF16)
            gib = jnp.concatenate([gi_ref[base + i] for i in range(LB_KB)], axis=1).astype(BF16)
            srb = jnp.concatenate([sr_ref[base + i] for i in range(LB_KB)], axis=1).astype(BF16)
            sib = jnp.concatenate([si_ref[base + i] for i in range(LB_KB)], axis=1).astype(BF16)
            du = (lax.dot_general(grb, wbr_ref[kb], nt, preferred_element_type=F32)
                  + lax.dot_general(gib, wbi_ref[kb], nt, preferred_element_type=F32)
                  + d_ref[:, ch] * dy)
            du_ref[:, ch] = du.astype(BF16)
            dwbr_ref[kb] += lax.dot_general(ub, grb, tn, preferred_element_type=F32)
            dwbi_ref[kb] += lax.dot_general(ub, gib, tn, preferred_element_type=F32)
            dwcr_ref[kb] += lax.dot_general(srb, dyb, tn, preferred_element_type=F32)
            dwci_ref[kb] -= lax.dot_general(sib, dyb, tn, preferred_element_type=F32)

    rev = lambda i: nb - 1 - i
    wb_spec = _full_spec((SSM_KB, BLK, ST_KB))
    wc_spec = _full_spec((SSM_KB, ST_KB, BLK))
    tab_in = [tabs[0], tabs[1], tabs[2], tabs[3], tabs[6], tabs[7]]
    tab_specs = [_full_spec(t.shape) for t in tab_in]
    vec = _full_spec((1, D_MODEL))
    svec = _full_spec((1, N_STATE))
    state_spec = pl.BlockSpec((N_LB, BLK, BLK), lambda i: (0, nb - 1 - i, 0))
    return pl.pallas_call(
        body, name="ssm_bwd", grid=(nb,),
        in_specs=[_row_spec(BLK, D_MODEL, rev), _row_spec(BLK, D_MODEL, rev), _row_spec(BLK, D_MODEL, rev),
                  state_spec, state_spec,
                  wb_spec, wb_spec, wc_spec, wc_spec, vec] + tab_specs,
        out_specs=[_row_spec(BLK, D_MODEL, rev), vec, svec, svec, wb_spec, wb_spec, wc_spec, wc_spec],
        out_shape=[jax.ShapeDtypeStruct((l_dim, D_MODEL), BF16), jax.ShapeDtypeStruct((1, D_MODEL), F32),
                   jax.ShapeDtypeStruct((1, N_STATE), F32), jax.ShapeDtypeStruct((1, N_STATE), F32),
                   jax.ShapeDtypeStruct((SSM_KB, BLK, ST_KB), F32), jax.ShapeDtypeStruct((SSM_KB, BLK, ST_KB), F32),
                   jax.ShapeDtypeStruct((SSM_KB, ST_KB, BLK), F32), jax.ShapeDtypeStruct((SSM_KB, ST_KB, BLK), F32)],
        scratch_shapes=[pltpu.VMEM((8, N_STATE), F32), pltpu.VMEM((8, N_STATE), F32),
                        pltpu.VMEM((N_LB, BLK, BLK), F32), pltpu.VMEM((N_LB, BLK, BLK), F32)],
        compiler_params=_params(),
    )(dz, y, u, s_re, s_im, wb_re, wb_im, wc_re, wc_im, d_skip, *tab_in)


def _discretize(lam_re, lam_im, log_dt, b_re, b_im):
    dt = jnp.exp(log_dt)[:, None]
    mag = jnp.exp(lam_re * dt)
    ar, ai = mag * jnp.cos(lam_im * dt), mag * jnp.sin(lam_im * dt)
    den = lam_re * lam_re + lam_im * lam_im
    nr, ni = ar - 1.0, ai
    fr, fi = (nr * lam_re + ni * lam_im) / den, (ni * lam_re - nr * lam_im) / den
    bbar_re = fr[..., None] * b_re - fi[..., None] * b_im
    bbar_im = fr[..., None] * b_im + fi[..., None] * b_re
    return ar, ai, bbar_re, bbar_im


def _block_diag_b(bbar):
    eye = jnp.eye(8, dtype=bbar.dtype)
    return jnp.einsum("kgpc,gh->kgchp", bbar.reshape(8, 8, SSM_STATE, SSM_GROUP_CH), eye).reshape(8, BLK, ST_KB)


def _block_diag_b_t(dwb):
    eye = jnp.eye(8, dtype=dwb.dtype)
    return jnp.einsum("kgchp,gh->kgpc", dwb.reshape(8, 8, SSM_GROUP_CH, 8, SSM_STATE), eye).reshape(
        SSM_GROUPS, SSM_STATE, SSM_GROUP_CH)


def _block_diag_c(c):
    eye = jnp.eye(8, dtype=c.dtype)
    return jnp.einsum("kgcp,gh->kgphc", c.reshape(8, 8, SSM_GROUP_CH, SSM_STATE), eye).reshape(8, ST_KB, BLK)


def _block_diag_c_t(dwc):
    eye = jnp.eye(8, dtype=dwc.dtype)
    return jnp.einsum("kgphc,gh->kgcp", dwc.reshape(8, 8, SSM_STATE, 8, SSM_GROUP_CH), eye).reshape(
        SSM_GROUPS, SSM_GROUP_CH, SSM_STATE)


def _powers(br, bi, n):
    pr, pi = br, bi
    cr, ci = br, bi
    while pr.shape[0] < n:
        pr, pi = (jnp.concatenate([pr, pr * cr - pi * ci], axis=0),
                  jnp.concatenate([pi, pr * ci + pi * cr], axis=0))
        cr, ci = cr * cr - ci * ci, 2.0 * cr * ci
    return pr[:n], pi[:n]


def _power_tables(ar, ai):
    g = BLK // GRP
    a1r, a1i = _powers(ar, ai, GRP)
    seqr, seqi = _powers(a1r[GRP - 1:], a1i[GRP - 1:], g)
    sq_r, sq_i = [seqr[0:1]], [seqi[0:1]]
    while len(sq_r) < 8:
        r, i = sq_r[-1], sq_i[-1]
        sq_r.append(r * r - i * i)
        sq_i.append(2.0 * r * i)
    sqr, sqi = jnp.concatenate(sq_r, axis=0), jnp.concatenate(sq_i, axis=0)
    return a1r, a1i, sqr, sqi, seqr, seqi, seqr[::-1], seqi[::-1]


def _all_gather(name, arrs):
    n = len(arrs)

    def body(*refs):
        ins, outs = refs[:n], refs[n:2 * n]
        send_sems, recv_sems, local_sems = refs[2 * n:]
        x, y, c = lax.axis_index("x"), lax.axis_index("y"), lax.axis_index("c")
        me, sibling = (x, y, c), (x, y, 1 - c)
        chips = [(1 - x, y), (x, 1 - y), (1 - x, 1 - y)]

        def rows(a, px, py, pc):
            m = ins[a].shape[0]
            return outs[a].at[pl.ds((4 * px + 2 * py + pc) * m, m), :]

        def copy(a, k, block, to, src=None):
            return pltpu.make_async_remote_copy(
                src_ref=rows(a, *block) if src is None else src, dst_ref=rows(a, *block),
                send_sem=send_sems.at[a, k], recv_sem=recv_sems.at[a, k],
                device_id=to, device_id_type=MESH)

        mine = [pltpu.make_async_copy(ins[a], rows(a, *me), local_sems.at[a]) for a in range(n)]
        for cp in mine:
            cp.start()
        first = []
        for a in range(n):
            first.append(copy(a, 0, me, sibling, src=ins[a]))
            first += [copy(a, 1 + j, me, (*chip, c), src=ins[a]) for j, chip in enumerate(chips)]
        for cp in first:
            cp.start()
        passed = []
        for j, chip in enumerate(chips):
            for a in range(n):
                copy(a, 1 + j, (*chip, c), me).wait_recv()
                cp = copy(a, 4 + j, (*chip, c), sibling)
                cp.start()
                passed.append(cp)
        for a in range(n):
            copy(a, 0, sibling, me).wait_recv()
            for j, chip in enumerate(chips):
                copy(a, 4 + j, (*chip, 1 - c), me).wait_recv()
        for cp in first + passed:
            cp.wait_send()
        for cp in mine:
            cp.wait()

    any_spec = pl.BlockSpec(memory_space=pl.ANY)
    return pl.pallas_call(
        body, name=name,
        in_specs=[any_spec] * n, out_specs=[any_spec] * n,
        out_shape=[jax.ShapeDtypeStruct((N_DEV * a.shape[0], a.shape[1]), a.dtype) for a in arrs],
        scratch_shapes=[pltpu.SemaphoreType.DMA((n, 7)), pltpu.SemaphoreType.DMA((n, 7)),
                        pltpu.SemaphoreType.DMA((n,))],
        compiler_params=_params(has_side_effects=True),
    )(*arrs)


def _exchange(name, arrs):
    n = len(arrs)

    def body(*refs):
        ins, outs = refs[:n], refs[n:2 * n]
        send_sems, recv_sems, local_sems = refs[2 * n:]
        x, y, c = lax.axis_index("x"), lax.axis_index("y"), lax.axis_index("c")
        me = 4 * x + 2 * y + c
        peers = []
        for rel in range(1, N_DEV):
            bx, by, bc = (rel >> 2) & 1, (rel >> 1) & 1, rel & 1
            peers.append((x + bx - 2 * x * bx, y + by - 2 * y * by, c + bc - 2 * c * bc))

        copies = []
        for a in range(n):
            m = ins[a].shape[0] // N_DEV
            local = pltpu.make_async_copy(ins[a].at[pl.ds(me * m, m), :], outs[a].at[pl.ds(me * m, m), :],
                                          local_sems.at[a])
            local.start()
            copies.append(local)
        remote = []
        for a in range(n):
            m = ins[a].shape[0] // N_DEV
            for k, peer in enumerate(peers):
                p_idx = 4 * peer[0] + 2 * peer[1] + peer[2]
                cp = pltpu.make_async_remote_copy(
                    src_ref=ins[a].at[pl.ds(p_idx * m, m), :], dst_ref=outs[a].at[pl.ds(me * m, m), :],
                    send_sem=send_sems.at[a, k], recv_sem=recv_sems.at[a, k],
                    device_id=peer, device_id_type=MESH)
                cp.start()
                remote.append(cp)
        for cp in remote:
            cp.wait_recv()
        for cp in remote:
            cp.wait_send()
        for cp in copies:
            cp.wait()

    any_spec = pl.BlockSpec(memory_space=pl.ANY)
    return pl.pallas_call(
        body, name=name,
        in_specs=[any_spec] * n, out_specs=[any_spec] * n,
        out_shape=[jax.ShapeDtypeStruct(a.shape, a.dtype) for a in arrs],
        scratch_shapes=[pltpu.SemaphoreType.DMA((n, 7)), pltpu.SemaphoreType.DMA((n, 7)),
                        pltpu.SemaphoreType.DMA((n,))],
        compiler_params=_params(has_side_effects=True),
    )(*arrs)


def _sum_slots(name, recv, own):
    m, ncol = own.shape
    tr = m // 2 if (m // 2) % 16 == 0 else m
    g = m // tr

    def body(*refs):
        slots, own_ref, o_ref = refs[:N_DEV], refs[N_DEV], refs[N_DEV + 1]
        me = _my_index()
        tot = None
        for s in range(N_DEV):
            v = jnp.where(me == s, own_ref[...], slots[s][...].astype(F32))
            tot = v if tot is None else tot + v
        o_ref[...] = tot

    def slot_spec(s):
        return pl.BlockSpec((tr, ncol), lambda i: (s * g + i, 0))

    return pl.pallas_call(
        body, name=name, grid=(g,),
        in_specs=[slot_spec(s) for s in range(N_DEV)] + [pl.BlockSpec((tr, ncol), lambda i: (i, 0))],
        out_specs=pl.BlockSpec((tr, ncol), lambda i: (i, 0)),
        out_shape=jax.ShapeDtypeStruct((m, ncol), F32),
        compiler_params=_params(),
    )(*([recv] * N_DEV), own)


def _sum_gathered(name, gathered, rows):
    tr = _pick(rows, 512, 8)
    g = rows // tr

    def body(*refs):
        o_ref = refs[N_DEV]
        tot = refs[0][...]
        for s in range(1, N_DEV):
            tot = tot + refs[s][...]
        o_ref[...] = tot

    return pl.pallas_call(
        body, name=name, grid=(g,),
        in_specs=[pl.BlockSpec((tr, BLK), (lambda i, s=s: (s * g + i, 0))) for s in range(N_DEV)],
        out_specs=pl.BlockSpec((tr, BLK), lambda i: (i, 0)),
        out_shape=jax.ShapeDtypeStruct((rows, BLK), F32),
        compiler_params=_params(),
    )(*([gathered] * N_DEV))


def _adamw(name, w, g, m, v):
    r, c = w.shape
    tr = _pick(r, 256, 8) if r % 8 == 0 else r
    c1 = 1.0 - ADAM_B1 ** ADAM_STEP
    c2 = 1.0 - ADAM_B2 ** ADAM_STEP

    def body(w_ref, g_ref, m_ref, v_ref, d_ref, nm_ref, nv_ref):
        gv = g_ref[...]
        nm = ADAM_B1 * m_ref[...] + (1.0 - ADAM_B1) * gv
        nv = ADAM_B2 * v_ref[...] + (1.0 - ADAM_B2) * (gv * gv)
        m_hat = nm / c1
        v_hat = nv / c2
        d_ref[...] = -ADAM_LR * (m_hat / (jnp.sqrt(v_hat) + ADAM_EPS) + ADAM_WD * w_ref[...])
        nm_ref[...] = nm
        nv_ref[...] = nv

    spec = pl.BlockSpec((tr, c), lambda i: (i, 0))
    shape = jax.ShapeDtypeStruct((r, c), F32)
    return pl.pallas_call(
        body, name=name, grid=(r // tr,),
        in_specs=[spec] * 4, out_specs=[spec] * 3, out_shape=[shape] * 3,
        compiler_params=_params(),
    )(w, g, m, v)


def _pack(parts):
    flat = []
    for p in parts:
        v = p.reshape(-1)
        flat.append(jnp.pad(v, (0, (-v.shape[0]) % BLK)))
    v = jnp.concatenate(flat)
    v = jnp.pad(v, (0, (-v.shape[0]) % (8 * BLK)))
    return v.reshape(-1, BLK)


def _unpack(buf, shapes):
    flat = buf.reshape(-1)
    out, off = [], 0
    for shp in shapes:
        size = math.prod(shp)
        out.append(flat[off:off + size].reshape(shp))
        off += size + (-size) % BLK
    return out


def kernel(x, meta_tokens, norm_mix, w_in, q_norm, k_norm, attn_sinks, lam_re, lam_im, log_dt, ssm_b_re, ssm_b_im, ssm_c_re, ssm_c_im, ssm_d, w_glu, attn_branch_norm, ssm_branch_norm, w_out, norm_ffn, w_ffn_in, w_ffn_out, loss_target, m_meta_tokens, m_norm_mix, m_w_in, m_q_norm, m_k_norm, m_attn_sinks, m_lam_re, m_lam_im, m_log_dt, m_ssm_b_re, m_ssm_b_im, m_ssm_c_re, m_ssm_c_im, m_ssm_d, m_w_glu, m_attn_branch_norm, m_ssm_branch_norm, m_w_out, m_norm_ffn, m_w_ffn_in, m_w_ffn_out, v_meta_tokens, v_norm_mix, v_w_in, v_q_norm, v_k_norm, v_attn_sinks, v_lam_re, v_lam_im, v_log_dt, v_ssm_b_re, v_ssm_b_im, v_ssm_c_re, v_ssm_c_im, v_ssm_d, v_w_glu, v_attn_branch_norm, v_ssm_branch_norm, v_w_out, v_norm_ffn, v_w_ffn_in, v_w_ffn_out):
    args = dict(locals())
    weights = {n: args[n] for n in WEIGHTS}
    mom_m = {n: args["m_" + n] for n in WEIGHTS}
    mom_v = {n: args["v_" + n] for n in WEIGHTS}

    x2d = x[0]
    target2d = loss_target[0]
    s_len = x2d.shape[0]
    l_dim = s_len + BLK
    tm_row = _pick(l_dim, 320)
    tm_mm = _pick(l_dim, 1040)
    tl_tn = _pick(l_dim, 832)

    shard_in = w_in[0].T.astype(BF16)
    shard_glu = w_glu[0].T.astype(BF16)
    shard_out = w_out[0].astype(BF16)
    shard_ffn_in = w_ffn_in[0].T.astype(BF16)
    shard_ffn_out = w_ffn_out[0].astype(BF16)
    shard_meta = meta_tokens.T
    wt_in, wt_glu, w_out_f, wt_ffn_in, w_ffn_out_f, meta_t = _all_gather(
        "gather_weights", [shard_in, shard_glu, shard_out, shard_ffn_in, shard_ffn_out, shard_meta])
    meta_full = meta_t.T
    meta_pad = jnp.pad(meta_full, ((PAD, 0), (0, 0)))

    nm_t, qn_t, kn_t = norm_mix, jnp.tile(q_norm, (1, N_Q_HEADS)), jnp.tile(k_norm, (1, N_KV_HEADS))
    e_mat = jnp.kron(jnp.eye(4, dtype=F32), jnp.ones((HEAD_DIM, HEAD_DIM), F32)).astype(BF16)

    def disc(lr, li, ldt, br, bi):
        return _discretize(lr[0], li[0], ldt[0], br[0], bi[0])

    (abar_re, abar_im, bbar_re, bbar_im), disc_vjp = jax.vjp(disc, lam_re, lam_im, log_dt, ssm_b_re, ssm_b_im)
    wb_re, wb_im = _block_diag_b(bbar_re).astype(BF16), _block_diag_b(bbar_im).astype(BF16)
    wc_re, wc_im = _block_diag_c(ssm_c_re[0]).astype(BF16), _block_diag_c(ssm_c_im[0]).astype(BF16)
    tabs = _power_tables(abar_re.reshape(1, N_STATE), abar_im.reshape(1, N_STATE))

    h0, xn = _embed_norm(x2d, meta_pad, nm_t)
    qkv = _matmul("proj_qkv", xn, wt_in, nt=True, tm=tm_mm, tn=512, tk=D_MODEL, n=Q_W + 2 * KV_W, w_off=0)
    u = _matmul("proj_u", xn, wt_in, nt=True, tm=tm_mm, tn=512, tk=D_MODEL, n=D_MODEL, w_off=3)
    gates = _matmul("proj_gates", xn, wt_in, nt=True, tm=tm_mm, tn=512, tk=D_MODEL, n=2 * D_MODEL, w_off=5)
    qn, kf, vf = _qk_prep(qkv, qn_t, kn_t, e_mat, tm_row)
    attn = _attn_fwd(qn, kf, vf, attn_sinks)
    y, z, s_re, s_im = _ssm_fwd(u, wb_re, wb_im, wc_re, wc_im, ssm_d, tabs)
    zab = _matmul("glu_proj", z, wt_glu, nt=True, tm=tm_mm, tn=1024, tk=D_MODEL)
    merged = _merge_fwd(attn, zab, gates, attn_branch_norm, ssm_branch_norm, tm_row)
    h1 = _matmul("out_proj", merged, w_out_f, nt=False, tm=tm_mm, tn=1024, tk=D_MODEL, res=h0)
    hn = _norm_cast("ffn_norm", h1, norm_ffn, tm_row)
    gu = _matmul("ffn_in", hn, wt_ffn_in, nt=True, tm=tm_mm, tn=512, tk=D_MODEL)
    act = _swiglu_fwd(gu, tm_row)
    h2 = _matmul("ffn_out", act, w_ffn_out_f, nt=False, tm=tm_mm, tn=1024, tk=1408, res=h1)
    dh2, dh2_b, loss_part = _loss_grad(h2, target2d)

    dact = _matmul("d_act", dh2_b, w_ffn_out_f, nt=True, tm=tm_mm, tn=1408, tk=D_MODEL)
    dgu = _swiglu_bwd(gu, dact, tm_row)
    g_ffn_out = _matmul_tn("g_ffn_out", act, dh2_b, tm=1408, tn=1024, tl=tl_tn)
    g_ffn_in_t = _matmul_tn("g_ffn_in", dgu, hn, tm=1408, tn=1024, tl=tl_tn)
    dhn = _matmul("d_hn", dgu, wt_ffn_in, nt=False, tm=tm_mm, tn=1024, tk=1408)
    dh1, dh1_b, g_norm_ffn = _norm_bwd_res("ffn_norm_bwd", h1, norm_ffn, dhn, dh2, tm_row)
    dmerged = _matmul("d_merged", dh1_b, w_out_f, nt=True, tm=tm_mm, tn=1024, tk=D_MODEL)
    dattn, dzab, dgates, g_abn, g_sbn = _merge_bwd(attn, zab, gates, attn_branch_norm, ssm_branch_norm,
                                                     dmerged, tm_row)
    g_out = _matmul_tn("g_out", merged, dh1_b, tm=1024, tn=1024, tl=tl_tn)
    dz = _matmul("d_z", dzab, wt_glu, nt=False, tm=tm_mm, tn=1024, tk=1024)
    g_glu_t = _matmul_tn("g_glu", dzab, z, tm=1024, tn=1024, tl=tl_tn)
    du, g_ssm_d, g_ar, g_ai, g_wbr, g_wbi, g_wcr, g_wci = _ssm_bwd(
        dz, y, u, s_re, s_im, wb_re, wb_im, wc_re, wc_im, ssm_d, tabs)
    dq, dkc, dkp, dkm, dvc, dvp, dvm, g_sinks = _attn_bwd(qn, kf, vf, attn_sinks, attn, dattn)
    dqkv, g_qn_t, g_kn_t = _qk_bwd(qkv, qn_t, kn_t, e_mat, dq, dkc, dkp, dkm, dvc, dvp, dvm)
    dproj = jnp.concatenate([dqkv, du, dgates], axis=1)
    g_in_t = _matmul_tn("g_in", dproj, xn, tm=1152, tn=1024, tl=tl_tn)
    dxn = _matmul("d_xn", dproj, wt_in, nt=False, tm=tm_mm, tn=1024, tk=1152)
    grad_x2d, dmeta_blk, g_norm_mix = _final_bwd(h0, nm_t, dxn, dh1)

    g_lam_re, g_lam_im, g_log_dt, g_b_re, g_b_im = disc_vjp(
        (g_ar.reshape(SSM_GROUPS, SSM_STATE), g_ai.reshape(SSM_GROUPS, SSM_STATE),
         _block_diag_b_t(g_wbr), _block_diag_b_t(g_wbi)))
    small_grads = {
        "norm_mix": g_norm_mix, "q_norm": g_qn_t.reshape(N_Q_HEADS, HEAD_DIM).sum(0)[None],
        "k_norm": g_kn_t.reshape(N_KV_HEADS, HEAD_DIM).sum(0)[None], "attn_sinks": g_sinks,
        "lam_re": g_lam_re, "lam_im": g_lam_im, "log_dt": g_log_dt, "ssm_b_re": g_b_re, "ssm_b_im": g_b_im,
        "ssm_c_re": _block_diag_c_t(g_wcr)[None], "ssm_c_im": _block_diag_c_t(g_wci)[None],
        "ssm_d": g_ssm_d, "attn_branch_norm": g_abn, "ssm_branch_norm": g_sbn, "norm_ffn": g_norm_ffn,
    }
    small_shapes = [weights[n].shape for n in SMALL] + [(N_META, D_MODEL)]
    packed = _pack([small_grads[n] for n in SMALL] + [dmeta_blk[PAD:]])
    rows = packed.shape[0]
    (gathered,) = _all_gather("gather_small", [packed])
    g_small = _sum_gathered("sum_small", gathered, rows)
    w_small = _pack([weights[n] for n in SMALL] + [jnp.zeros((N_META, D_MODEL), F32)])
    m_small = _pack([mom_m[n] for n in SMALL] + [jnp.zeros((N_META, D_MODEL), F32)])
    v_small = _pack([mom_v[n] for n in SMALL] + [jnp.zeros((N_META, D_MODEL), F32)])
    d_small, nm_small, nv_small = _adamw("adamw_small", w_small, g_small, m_small, v_small)
    grads = dict(zip(SMALL + ["_meta"], _unpack(g_small, small_shapes)))
    deltas = dict(zip(SMALL, _unpack(d_small, small_shapes[:-1])))
    new_m = dict(zip(SMALL, _unpack(nm_small, small_shapes[:-1])))
    new_v = dict(zip(SMALL, _unpack(nv_small, small_shapes[:-1])))

    me = _my_index()
    grads["meta_tokens"] = lax.dynamic_slice(grads.pop("_meta"), (0, me * BLK), (N_META, BLK))

    big = [("w_in", g_in_t, True), ("w_glu", g_glu_t, True), ("w_out", g_out, False),
           ("w_ffn_in", g_ffn_in_t, True), ("w_ffn_out", g_ffn_out, False)]
    recvs = _exchange("exchange_grads", [g.astype(BF16) for _, g, _ in big])
    for (name, g_full, transposed), recv in zip(big, recvs):
        m_rows = g_full.shape[0] // N_DEV
        own = lax.dynamic_slice(g_full, (me * m_rows, 0), (m_rows, g_full.shape[1]))
        g_shard = _sum_slots("sum_" + name, recv, own)
        grads[name] = (g_shard.T if transposed else g_shard)[None]

    for name in ["meta_tokens", "w_in", "w_glu", "w_out", "w_ffn_in", "w_ffn_out"]:
        shp = weights[name].shape
        as2d = lambda a: a.reshape(shp[-2], shp[-1])
        d, nm, nv = _adamw("adamw_" + name, as2d(weights[name]), as2d(grads[name]), as2d(mom_m[name]),
                           as2d(mom_v[name]))
        deltas[name], new_m[name], new_v[name] = d.reshape(shp), nm.reshape(shp), nv.reshape(shp)

    loss = lax.psum(loss_part[0, 0], ("x", "y", "c"))
    return (loss, grad_x2d[None], *[grads[n] for n in WEIGHTS], *[deltas[n] for n in WEIGHTS],
            *[new_m[n] for n in WEIGHTS], *[new_v[n] for n in WEIGHTS])
```

```python
import math

import jax
import jax.numpy as jnp
from jax import lax
from jax.experimental import pallas as pl
from jax.experimental.pallas import tpu as pltpu

F32 = jnp.float32
BF16 = jnp.bfloat16

D_MODEL = 1024
N_META = 16
HEAD_DIM = 64
N_Q_HEADS = 16
N_KV_HEADS = 4
Q_W = N_Q_HEADS * HEAD_DIM
KV_W = N_KV_HEADS * HEAD_DIM
SSM_GROUPS = 64
SSM_GROUP_CH = 16
SSM_STATE = 64
N_STATE = SSM_GROUPS * SSM_STATE
D_FF = 2816
IN_COLS = Q_W + 2 * KV_W + 3 * D_MODEL
EPS = 1e-6
BLK = 128
PAD = BLK - N_META
N_DEV = 8
NEG = -1e30
SSM_KB = 8
ST_KB = N_STATE // SSM_KB
LB_KB = ST_KB // BLK
N_LB = N_STATE // BLK

ADAM_LR = 0.001
ADAM_B1 = 0.9
ADAM_B2 = 0.999
ADAM_EPS = 1e-08
ADAM_WD = 0.01
ADAM_STEP = 10

VMEM_LIMIT = 48 * 1024 * 1024
MESH = pl.DeviceIdType.MESH

SMALL = ["norm_mix", "q_norm", "k_norm", "attn_sinks", "lam_re", "lam_im", "log_dt", "ssm_b_re", "ssm_b_im",
         "ssm_c_re", "ssm_c_im", "ssm_d", "attn_branch_norm", "ssm_branch_norm", "norm_ffn"]
WEIGHTS = ["meta_tokens", "norm_mix", "w_in", "q_norm", "k_norm", "attn_sinks", "lam_re", "lam_im", "log_dt",
           "ssm_b_re", "ssm_b_im", "ssm_c_re", "ssm_c_im", "ssm_d", "w_glu", "attn_branch_norm",
           "ssm_branch_norm", "w_out", "norm_ffn", "w_ffn_in", "w_ffn_out"]


def _params(**kw):
    return pltpu.CompilerParams(vmem_limit_bytes=VMEM_LIMIT, **kw)


def _pick(n, cap, mult=16):
    best = None
    for d in range(mult, min(n, cap) + 1, mult):
        if n % d == 0:
            best = d
    assert best is not None, (n, cap, mult)
    return best


def _my_index():
    return 4 * lax.axis_index("x") + 2 * lax.axis_index("y") + lax.axis_index("c")


def _rms(x, g):
    r = lax.rsqrt(jnp.mean(x * x, axis=-1, keepdims=True) + EPS)
    return x * r * g


def _rms_bwd(x, g, dy):
    r = lax.rsqrt(jnp.mean(x * x, axis=-1, keepdims=True) + EPS)
    t = dy * g
    dx = r * t - x * (r * r * r) * jnp.mean(t * x, axis=-1, keepdims=True)
    dg = jnp.sum(dy * (x * r), axis=0, keepdims=True)
    return dx, dg


def _sigmoid(x):
    return 1.0 / (1.0 + jnp.exp(-x))


def _gelu(x):
    k = math.sqrt(2.0 / math.pi)
    return 0.5 * x * (1.0 + jnp.tanh(k * (x + 0.044715 * (x * x * x))))


def _gelu_grad(x):
    k = math.sqrt(2.0 / math.pi)
    t = jnp.tanh(k * (x + 0.044715 * (x * x * x)))
    return 0.5 * (1.0 + t) + 0.5 * x * (1.0 - t * t) * (k * (1.0 + 3.0 * 0.044715 * (x * x)))


def _head_mean(x, e_ref):
    hi = x.astype(BF16)
    r1 = x - hi.astype(F32)
    mid = r1.astype(BF16)
    lo = (r1 - mid.astype(F32)).astype(BF16)
    e = e_ref[...]
    out = []
    for b in range(x.shape[1] // 256):
        sl = slice(256 * b, 256 * b + 256)
        s = (jnp.dot(hi[:, sl], e, preferred_element_type=F32)
             + jnp.dot(mid[:, sl], e, preferred_element_type=F32)
             + jnp.dot(lo[:, sl], e, preferred_element_type=F32))
        out.append(s)
    s = out[0] if len(out) == 1 else jnp.concatenate(out, axis=1)
    return s * (1.0 / HEAD_DIM)


def _head_rms(x, g, e_ref):
    r = lax.rsqrt(_head_mean(x * x, e_ref) + EPS)
    return x * r * g


def _head_rms_bwd(x, g, dy, e_ref):
    r = lax.rsqrt(_head_mean(x * x, e_ref) + EPS)
    t = dy * g
    dx = r * t - x * (r * r * r) * _head_mean(t * x, e_ref)
    dg = jnp.sum(dy * (x * r), axis=0, keepdims=True)
    return dx, dg


def _lane_half(shape):
    lane = lax.broadcasted_iota(jnp.int32, shape, len(shape) - 1)
    return (lane >> 6) & 1


def _matmul(name, a, w, *, nt, tm, tn, tk, n=None, w_off=0, res=None, out_dtype=F32):
    m_dim, k_dim = a.shape
    n_dim = n if n is not None else (w.shape[0] if nt else w.shape[1])
    gm, gn, gk = m_dim // tm, n_dim // tn, k_dim // tk
    assert gm * tm == m_dim and gn * tn == n_dim and gk * tk == k_dim, (name, a.shape, w.shape, tm, tn, tk)
    dn = (((1,), (1,)), ((), ())) if nt else (((1,), (0,)), ((), ()))

    def body(*refs):
        if res is None:
            a_ref, w_ref, o_ref, acc = refs
            r_ref = None
        else:
            a_ref, w_ref, r_ref, o_ref, acc = refs
        k = pl.program_id(2)

        @pl.when(k == 0)
        def _():
            acc[...] = jnp.zeros_like(acc)

        acc[...] += lax.dot_general(a_ref[...], w_ref[...], dn, preferred_element_type=F32)

        @pl.when(k == gk - 1)
        def _():
            r = acc[...]
            if r_ref is not None:
                r = r_ref[...] + r
            o_ref[...] = r.astype(out_dtype)

    if nt:
        w_spec = pl.BlockSpec((tn, tk), lambda i, j, k: (j + w_off, k))
    else:
        w_spec = pl.BlockSpec((tk, tn), lambda i, j, k: (k, j))
    in_specs = [pl.BlockSpec((tm, tk), lambda i, j, k: (i, k)), w_spec]
    args = [a, w]
    if res is not None:
        in_specs.append(pl.BlockSpec((tm, tn), lambda i, j, k: (i, j)))
        args.append(res)
    return pl.pallas_call(
        body, name=name, grid=(gm, gn, gk),
        in_specs=in_specs,
        out_specs=pl.BlockSpec((tm, tn), lambda i, j, k: (i, j)),
        out_shape=jax.ShapeDtypeStruct((m_dim, n_dim), out_dtype),
        scratch_shapes=[pltpu.VMEM((tm, tn), F32)],
        compiler_params=_params(dimension_semantics=("parallel", "parallel", "arbitrary")),
    )(*args)


def _matmul_tn(name, a, b, *, tm, tn, tl):
    l_dim, m_dim = a.shape
    n_dim = b.shape[1]
    gm, gn, gl = m_dim // tm, n_dim // tn, l_dim // tl
    assert gm * tm == m_dim and gn * tn == n_dim and gl * tl == l_dim, (name, a.shape, b.shape, tm, tn, tl)

    def body(a_ref, b_ref, o_ref, ob_ref):
        @pl.when(pl.program_id(2) == 0)
        def _():
            o_ref[...] = jnp.zeros_like(o_ref)

        o_ref[...] += lax.dot_general(a_ref[...], b_ref[...], (((0,), (0,)), ((), ())),
                                      preferred_element_type=F32)

        @pl.when(pl.program_id(2) == gl - 1)
        def _():
            ob_ref[...] = o_ref[...].astype(BF16)

    out_spec = pl.BlockSpec((tm, tn), lambda i, j, l: (i, j))
    return pl.pallas_call(
        body, name=name, grid=(gm, gn, gl),
        in_specs=[pl.BlockSpec((tl, tm), lambda i, j, l: (l, i)),
                  pl.BlockSpec((tl, tn), lambda i, j, l: (l, j))],
        out_specs=[out_spec, out_spec],
        out_shape=[jax.ShapeDtypeStruct((m_dim, n_dim), F32), jax.ShapeDtypeStruct((m_dim, n_dim), BF16)],
        compiler_params=_params(dimension_semantics=("parallel", "parallel", "arbitrary")),
    )(a, b)


def _row_spec(tm, cols, f=None):
    if f is None:
        return pl.BlockSpec((tm, cols), lambda i: (i, 0))
    return pl.BlockSpec((tm, cols), lambda i: (f(i), 0))


def _full_spec(shape):
    nd = len(shape)
    return pl.BlockSpec(shape, lambda i: (0,) * nd)


def _embed_norm(x2d, g):
    s_len = x2d.shape[0]
    nb = s_len // BLK + 1

    def body(x_ref, g_ref, h_ref, xn_ref):
        h_ref[...] = x_ref[...]
        xn_ref[...] = _rms(x_ref[...], g_ref[...]).astype(BF16)

    return pl.pallas_call(
        body, name="embed_norm", grid=(nb - 1,),
        in_specs=[_row_spec(BLK, D_MODEL), _full_spec((1, D_MODEL))],
        out_specs=[_row_spec(BLK, D_MODEL, lambda i: i + 1), _row_spec(BLK, D_MODEL, lambda i: i + 1)],
        out_shape=[jax.ShapeDtypeStruct((nb * BLK, D_MODEL), F32),
                   jax.ShapeDtypeStruct((nb * BLK, D_MODEL), BF16)],
        compiler_params=_params(),
    )(x2d, g)


def _embed_meta(meta_pad, g, h0, xn):
    def body(mp_ref, g_ref, h_in, xn_in, h_ref, xn_ref):
        h_ref[...] = mp_ref[...]
        xn_ref[...] = _rms(mp_ref[...], g_ref[...]).astype(BF16)

    any_spec = pl.BlockSpec(memory_space=pl.ANY)
    return pl.pallas_call(
        body, name="embed_meta", grid=(1,),
        in_specs=[_full_spec((BLK, D_MODEL)), _full_spec((1, D_MODEL)), any_spec, any_spec],
        out_specs=[_row_spec(BLK, D_MODEL), _row_spec(BLK, D_MODEL)],
        out_shape=[jax.ShapeDtypeStruct(h0.shape, F32), jax.ShapeDtypeStruct(xn.shape, BF16)],
        input_output_aliases={2: 0, 3: 1},
        compiler_params=_params(),
    )(meta_pad, g, h0, xn)


def _qk_prep(qkv, q_norm_t, k_norm_t, e_mat, tm):
    l_dim = qkv.shape[0]

    def body(x_ref, qg_ref, kg_ref, e_ref, q_ref, kf_ref, vf_ref):
        x = x_ref[...]
        q = _head_rms(x[:, :Q_W], qg_ref[...], e_ref) * (HEAD_DIM ** -0.5)
        q_ref[...] = q.astype(BF16)
        k = _head_rms(x[:, Q_W:Q_W + KV_W], kg_ref[...], e_ref)
        v = x[:, Q_W + KV_W:Q_W + 2 * KV_W]
        half = _lane_half((tm, BLK))
        for src, dst in ((k, kf_ref), (v, vf_ref)):
            for kv in range(N_KV_HEADS):
                blk = src[:, BLK * (kv // 2):BLK * (kv // 2) + BLK]
                swapped = pltpu.roll(blk, HEAD_DIM, axis=1)
                for e in range(2):
                    val = blk if kv % 2 == e else swapped
                    idx = 2 * kv + e
                    dst[:, BLK * idx:BLK * idx + BLK] = jnp.where(half == e, val, 0.0).astype(BF16)

    return pl.pallas_call(
        body, name="qk_prep", grid=(l_dim // tm,),
        in_specs=[_row_spec(tm, Q_W + 2 * KV_W), _full_spec((1, Q_W)), _full_spec((1, KV_W)),
                  _full_spec((256, 256))],
        out_specs=[_row_spec(tm, Q_W), _row_spec(tm, 8 * BLK), _row_spec(tm, 8 * BLK)],
        out_shape=[jax.ShapeDtypeStruct((l_dim, Q_W), BF16),
                   jax.ShapeDtypeStruct((l_dim, 8 * BLK), BF16),
                   jax.ShapeDtypeStruct((l_dim, 8 * BLK), BF16)],
        compiler_params=_params(),
    )(qkv, q_norm_t, k_norm_t, e_mat)


def _merge_fwd(attn, zab, gates, abn, sbn, tm):
    l_dim = attn.shape[0]

    def body(a_ref, z_ref, g_ref, an_ref, sn_ref, o_ref):
        z = z_ref[...]
        g = g_ref[...]
        ssm = z[:, :D_MODEL] * _sigmoid(z[:, D_MODEL:])
        merged = (_sigmoid(g[:, :D_MODEL]) * _rms(a_ref[...], an_ref[...])
                  + _sigmoid(g[:, D_MODEL:]) * _rms(ssm, sn_ref[...]))
        o_ref[...] = merged.astype(BF16)

    return pl.pallas_call(
        body, name="merge_fwd", grid=(l_dim // tm,),
        in_specs=[_row_spec(tm, D_MODEL), _row_spec(tm, 2 * D_MODEL), _row_spec(tm, 2 * D_MODEL),
                  _full_spec((1, D_MODEL)), _full_spec((1, D_MODEL))],
        out_specs=_row_spec(tm, D_MODEL),
        out_shape=jax.ShapeDtypeStruct((l_dim, D_MODEL), BF16),
        compiler_params=_params(),
    )(attn, zab, gates, abn, sbn)


def _merge_bwd(attn, zab, gates, abn, sbn, dmerged, tm):
    l_dim = attn.shape[0]

    def body(a_ref, z_ref, g_ref, an_ref, sn_ref, dm_ref, da_ref, dz_ref, dg_ref, dan_ref, dsn_ref):
        @pl.when(pl.program_id(0) == 0)
        def _():
            dan_ref[...] = jnp.zeros_like(dan_ref)
            dsn_ref[...] = jnp.zeros_like(dsn_ref)

        z = z_ref[...]
        g = g_ref[...]
        dm = dm_ref[...]
        attn_v = a_ref[...]
        za, zb = z[:, :D_MODEL], z[:, D_MODEL:]
        sb = _sigmoid(zb)
        ssm = za * sb
        s_ga, s_gs = _sigmoid(g[:, :D_MODEL]), _sigmoid(g[:, D_MODEL:])
        a_n = _rms(attn_v, an_ref[...])
        s_n = _rms(ssm, sn_ref[...])
        dg_ref[:, :D_MODEL] = (dm * a_n * s_ga * (1.0 - s_ga)).astype(BF16)
        dg_ref[:, D_MODEL:] = (dm * s_n * s_gs * (1.0 - s_gs)).astype(BF16)
        dattn, dan = _rms_bwd(attn_v, an_ref[...], dm * s_ga)
        dssm, dsn = _rms_bwd(ssm, sn_ref[...], dm * s_gs)
        da_ref[...] = dattn
        dz_ref[:, :D_MODEL] = (dssm * sb).astype(BF16)
        dz_ref[:, D_MODEL:] = (dssm * za * sb * (1.0 - sb)).astype(BF16)
        dan_ref[...] += dan
        dsn_ref[...] += dsn

    return pl.pallas_call(
        body, name="merge_bwd", grid=(l_dim // tm,),
        in_specs=[_row_spec(tm, D_MODEL), _row_spec(tm, 2 * D_MODEL), _row_spec(tm, 2 * D_MODEL),
                  _full_spec((1, D_MODEL)), _full_spec((1, D_MODEL)), _row_spec(tm, D_MODEL)],
        out_specs=[_row_spec(tm, D_MODEL), _row_spec(tm, 2 * D_MODEL), _row_spec(tm, 2 * D_MODEL),
                   _full_spec((1, D_MODEL)), _full_spec((1, D_MODEL))],
        out_shape=[jax.ShapeDtypeStruct((l_dim, D_MODEL), F32),
                   jax.ShapeDtypeStruct((l_dim, 2 * D_MODEL), BF16),
                   jax.ShapeDtypeStruct((l_dim, 2 * D_MODEL), BF16),
                   jax.ShapeDtypeStruct((1, D_MODEL), F32), jax.ShapeDtypeStruct((1, D_MODEL), F32)],
        compiler_params=_params(),
    )(attn, zab, gates, abn, sbn, dmerged)


def _norm_cast(name, h, g, tm):
    l_dim = h.shape[0]

    def body(h_ref, g_ref, o_ref):
        o_ref[...] = _rms(h_ref[...], g_ref[...]).astype(BF16)

    return pl.pallas_call(
        body, name=name, grid=(l_dim // tm,),
        in_specs=[_row_spec(tm, D_MODEL), _full_spec((1, D_MODEL))],
        out_specs=_row_spec(tm, D_MODEL),
        out_shape=jax.ShapeDtypeStruct((l_dim, D_MODEL), BF16),
        compiler_params=_params(),
    )(h, g)


def _swiglu_fwd(gu, tm):
    l_dim = gu.shape[0]

    def body(x_ref, o_ref):
        x = x_ref[...]
        gate, up = x[:, :D_FF], x[:, D_FF:]
        o_ref[...] = (gate * _sigmoid(gate) * up).astype(BF16)

    return pl.pallas_call(
        body, name="swiglu_fwd", grid=(l_dim // tm,),
        in_specs=[_row_spec(tm, 2 * D_FF)],
        out_specs=_row_spec(tm, D_FF),
        out_shape=jax.ShapeDtypeStruct((l_dim, D_FF), BF16),
        compiler_params=_params(),
    )(gu)


def _swiglu_bwd(gu, dact, tm):
    l_dim = gu.shape[0]

    def body(x_ref, d_ref, o_ref):
        x = x_ref[...]
        d = d_ref[...]
        gate, up = x[:, :D_FF], x[:, D_FF:]
        s = _sigmoid(gate)
        o_ref[:, :D_FF] = (d * up * (s * (1.0 + gate * (1.0 - s)))).astype(BF16)
        o_ref[:, D_FF:] = (d * (gate * s)).astype(BF16)

    return pl.pallas_call(
        body, name="swiglu_bwd", grid=(l_dim // tm,),
        in_specs=[_row_spec(tm, 2 * D_FF), _row_spec(tm, D_FF)],
        out_specs=_row_spec(tm, 2 * D_FF),
        out_shape=jax.ShapeDtypeStruct((l_dim, 2 * D_FF), BF16),
        compiler_params=_params(),
    )(gu, dact)


def _loss_grad(h2, target2d):
    l_dim = h2.shape[0]
    nb = l_dim // BLK

    def body(h_ref, t_ref, d_ref, db_ref, loss_ref):
        i = pl.program_id(0)

        @pl.when(i == 0)
        def _():
            loss_ref[...] = jnp.zeros_like(loss_ref)
            d_ref[...] = jnp.zeros_like(d_ref)
            db_ref[...] = jnp.zeros_like(db_ref)

        @pl.when(i > 0)
        def _():
            err = h_ref[...] - t_ref[...]
            d = err * (1.0 / D_MODEL)
            d_ref[...] = d
            db_ref[...] = d.astype(BF16)
            loss_ref[...] += 0.5 * jnp.sum(jnp.mean(err * err, axis=-1, keepdims=True), axis=0, keepdims=True)

    return pl.pallas_call(
        body, name="loss_grad", grid=(nb,),
        in_specs=[_row_spec(BLK, D_MODEL), _row_spec(BLK, D_MODEL, lambda i: jnp.maximum(i - 1, 0))],
        out_specs=[_row_spec(BLK, D_MODEL), _row_spec(BLK, D_MODEL), _full_spec((1, 1))],
        out_shape=[jax.ShapeDtypeStruct((l_dim, D_MODEL), F32), jax.ShapeDtypeStruct((l_dim, D_MODEL), BF16),
                   jax.ShapeDtypeStruct((1, 1), F32)],
        compiler_params=_params(),
    )(h2, target2d)


def _norm_bwd_res(name, h, g, dy, dres, tm):
    l_dim = h.shape[0]

    def body(h_ref, g_ref, dy_ref, dr_ref, o_ref, ob_ref, dg_ref):
        @pl.when(pl.program_id(0) == 0)
        def _():
            dg_ref[...] = jnp.zeros_like(dg_ref)

        dx, dg = _rms_bwd(h_ref[...], g_ref[...], dy_ref[...])
        out = dr_ref[...] + dx
        o_ref[...] = out
        ob_ref[...] = out.astype(BF16)
        dg_ref[...] += dg

    return pl.pallas_call(
        body, name=name, grid=(l_dim // tm,),
        in_specs=[_row_spec(tm, D_MODEL), _full_spec((1, D_MODEL)), _row_spec(tm, D_MODEL), _row_spec(tm, D_MODEL)],
        out_specs=[_row_spec(tm, D_MODEL), _row_spec(tm, D_MODEL), _full_spec((1, D_MODEL))],
        out_shape=[jax.ShapeDtypeStruct((l_dim, D_MODEL), F32), jax.ShapeDtypeStruct((l_dim, D_MODEL), BF16),
                   jax.ShapeDtypeStruct((1, D_MODEL), F32)],
        compiler_params=_params(),
    )(h, g, dy, dres)


def _final_bwd(h0, g, dxn, dh1):
    l_dim = h0.shape[0]
    nb = l_dim // BLK

    def body(h_ref, g_ref, dy_ref, dr_ref, gx_ref, gm_ref, dg_ref):
        i = pl.program_id(0)

        @pl.when(i == 0)
        def _():
            dg_ref[...] = jnp.zeros_like(dg_ref)

        dx, dg = _rms_bwd(h_ref[...], g_ref[...], dy_ref[...])
        out = dr_ref[...] + dx
        dg_ref[...] += dg

        @pl.when(i == 0)
        def _():
            gm_ref[...] = out

        @pl.when(i > 0)
        def _():
            gx_ref[...] = out

    return pl.pallas_call(
        body, name="final_bwd", grid=(nb,),
        in_specs=[_row_spec(BLK, D_MODEL), _full_spec((1, D_MODEL)), _row_spec(BLK, D_MODEL),
                  _row_spec(BLK, D_MODEL)],
        out_specs=[_row_spec(BLK, D_MODEL, lambda i: jnp.maximum(i - 1, 0)), _full_spec((BLK, D_MODEL)),
                   _full_spec((1, D_MODEL))],
        out_shape=[jax.ShapeDtypeStruct((l_dim - BLK, D_MODEL), F32), jax.ShapeDtypeStruct((BLK, D_MODEL), F32),
                   jax.ShapeDtypeStruct((1, D_MODEL), F32)],
        compiler_params=_params(),
    )(h0, g, dxn, dh1)


def _attn_valid(n):
    shape = (2 * BLK, 3 * BLK)
    qi = lax.broadcasted_iota(jnp.int32, shape, 0) & (BLK - 1)
    col = lax.broadcasted_iota(jnp.int32, shape, 1)
    kj = col & (BLK - 1)
    part = col >> 7
    nn = jnp.zeros(shape, jnp.int32) + n
    meta_ok = (part == 0) & (kj >= PAD) & (nn >= 1)
    prev_ok = (part == 1) & (kj > qi) & (nn >= 2)
    cur_ok = (part == 2) & (kj <= qi) & ((nn >= 1) | (kj >= PAD))
    return meta_ok | prev_ok | cur_ok


def _attn_probs(q_ref, kwin, sk_ref, valid, kv, e):
    qs = jnp.concatenate([q_ref[:, BLK * (2 * kv):BLK * (2 * kv) + BLK],
                          q_ref[:, BLK * (2 * kv + 1):BLK * (2 * kv + 1) + BLK]], axis=0)
    s = lax.dot_general(qs, kwin, (((1,), (1,)), ((), ())), preferred_element_type=F32)
    s = jnp.where(valid, s, NEG)
    h0 = 4 * kv + e
    row = lax.broadcasted_iota(jnp.int32, (2 * BLK, 1), 0)
    sink = jnp.where(row < BLK, sk_ref[:, h0:h0 + 1], sk_ref[:, h0 + 2:h0 + 3])
    m = jnp.maximum(jnp.max(s, axis=-1, keepdims=True), sink)
    ex = jnp.exp(s - m)
    es = jnp.exp(sink - m)
    inv = 1.0 / (jnp.sum(ex, axis=-1, keepdims=True) + es)
    return qs, ex * inv, es * inv


def _attn_specs(nb):
    prev = lambda i: jnp.maximum(i - 1, 0)
    zero = lambda i: 0
    kv_specs = [_row_spec(BLK, 8 * BLK, zero), _row_spec(BLK, 8 * BLK, prev), _row_spec(BLK, 8 * BLK)]
    return kv_specs


def _attn_fwd(qn, kf, vf, sinks):
    l_dim = qn.shape[0]
    nb = l_dim // BLK

    def body(q_ref, km_ref, kp_ref, kc_ref, vm_ref, vp_ref, vc_ref, sk_ref, o_ref):
        valid = _attn_valid(pl.program_id(0))
        for kv in range(N_KV_HEADS):
            outs = []
            for e in range(2):
                sl = slice(BLK * (2 * kv + e), BLK * (2 * kv + e) + BLK)
                kwin = jnp.concatenate([km_ref[:, sl], kp_ref[:, sl], kc_ref[:, sl]], axis=0)
                vwin = jnp.concatenate([vm_ref[:, sl], vp_ref[:, sl], vc_ref[:, sl]], axis=0)
                _, p, _ = _attn_probs(q_ref, kwin, sk_ref, valid, kv, e)
                outs.append(jnp.dot(p.astype(BF16), vwin, preferred_element_type=F32))
            o = outs[0] + outs[1]
            o_ref[:, BLK * (2 * kv):BLK * (2 * kv) + BLK] = o[:BLK]
            o_ref[:, BLK * (2 * kv + 1):BLK * (2 * kv + 1) + BLK] = o[BLK:]

    kv_specs = _attn_specs(nb)
    return pl.pallas_call(
        body, name="attn_fwd", grid=(nb,),
        in_specs=[_row_spec(BLK, Q_W)] + kv_specs + kv_specs + [_full_spec((1, N_Q_HEADS))],
        out_specs=_row_spec(BLK, Q_W),
        out_shape=jax.ShapeDtypeStruct((l_dim, Q_W), F32),
        compiler_params=_params(),
    )(qn, kf, kf, kf, vf, vf, vf, sinks)


def _attn_bwd(qn, kf, vf, sinks, attn, dattn):
    l_dim = qn.shape[0]
    nb = l_dim // BLK
    wide = 8 * BLK

    def body(q_ref, km_ref, kp_ref, kc_ref, vm_ref, vp_ref, vc_ref, sk_ref, o_ref, do_ref,
             dq_ref, dkc_ref, dkp_ref, dkm_ref, dvc_ref, dvp_ref, dvm_ref, dsk_ref):
        @pl.when(pl.program_id(0) == 0)
        def _():
            dkm_ref[...] = jnp.zeros_like(dkm_ref)
            dvm_ref[...] = jnp.zeros_like(dvm_ref)
            dsk_ref[...] = jnp.zeros_like(dsk_ref)

        valid = _attn_valid(pl.program_id(0))
        half = _lane_half((BLK, BLK))
        lane16 = lax.broadcasted_iota(jnp.int32, (1, N_Q_HEADS), 1)
        dsk = jnp.zeros((1, N_Q_HEADS), F32)
        for kv in range(N_KV_HEADS):
            j0, j1 = 2 * kv, 2 * kv + 1
            do0 = do_ref[:, BLK * j0:BLK * j0 + BLK]
            do1 = do_ref[:, BLK * j1:BLK * j1 + BLK]
            prod0 = do0 * o_ref[:, BLK * j0:BLK * j0 + BLK]
            prod1 = do1 * o_ref[:, BLK * j1:BLK * j1 + BLK]
            dos = jnp.concatenate([do0, do1], axis=0).astype(BF16)
            dqs = []
            for e in range(2):
                sl = slice(BLK * (2 * kv + e), BLK * (2 * kv + e) + BLK)
                kwin = jnp.concatenate([km_ref[:, sl], kp_ref[:, sl], kc_ref[:, sl]], axis=0)
                vwin = jnp.concatenate([vm_ref[:, sl], vp_ref[:, sl], vc_ref[:, sl]], axis=0)
                qs, p, p_sink = _attn_probs(q_ref, kwin, sk_ref, valid, kv, e)
                delta = jnp.concatenate(
                    [jnp.sum(jnp.where(half == e, prod0, 0.0), axis=-1, keepdims=True),
                     jnp.sum(jnp.where(half == e, prod1, 0.0), axis=-1, keepdims=True)], axis=0)
                dp = lax.dot_general(dos, vwin, (((1,), (1,)), ((), ())), preferred_element_type=F32)
                ds = (p * (dp - delta)).astype(BF16)
                pb = p.astype(BF16)
                dqs.append(jnp.dot(ds, kwin, preferred_element_type=F32))
                dk = lax.dot_general(ds, qs, (((0,), (0,)), ((), ())), preferred_element_type=F32)
                dv = lax.dot_general(pb, dos, (((0,), (0,)), ((), ())), preferred_element_type=F32)
                dkm_ref[:, sl] += dk[:BLK]
                dkp_ref[:, sl] = dk[BLK:2 * BLK]
                dkc_ref[:, sl] = dk[2 * BLK:]
                dvm_ref[:, sl] += dv[:BLK]
                dvp_ref[:, sl] = dv[BLK:2 * BLK]
                dvc_ref[:, sl] = dv[2 * BLK:]
                sink_g = -(p_sink * delta)
                g_lo = jnp.sum(sink_g[:BLK], axis=0, keepdims=True)
                g_hi = jnp.sum(sink_g[BLK:], axis=0, keepdims=True)
                dsk = dsk + jnp.where(lane16 == 4 * kv + e, g_lo, 0.0) + jnp.where(lane16 == 4 * kv + 2 + e, g_hi, 0.0)
            dq = jnp.where(jnp.concatenate([half, half], axis=0) == 0, dqs[0], dqs[1])
            dq_ref[:, BLK * j0:BLK * j0 + BLK] = dq[:BLK]
            dq_ref[:, BLK * j1:BLK * j1 + BLK] = dq[BLK:]
        dsk_ref[...] += dsk

    kv_specs = _attn_specs(nb)
    row_wide = _row_spec(BLK, wide)
    acc_wide = _full_spec((BLK, wide))
    big = jax.ShapeDtypeStruct((l_dim, wide), F32)
    return pl.pallas_call(
        body, name="attn_bwd", grid=(nb,),
        in_specs=[_row_spec(BLK, Q_W)] + kv_specs + kv_specs
        + [_full_spec((1, N_Q_HEADS)), _row_spec(BLK, Q_W), _row_spec(BLK, Q_W)],
        out_specs=[_row_spec(BLK, Q_W), row_wide, row_wide, acc_wide, row_wide, row_wide, acc_wide,
                   _full_spec((1, N_Q_HEADS))],
        out_shape=[jax.ShapeDtypeStruct((l_dim, Q_W), F32), big, big, jax.ShapeDtypeStruct((BLK, wide), F32),
                   big, big, jax.ShapeDtypeStruct((BLK, wide), F32), jax.ShapeDtypeStruct((1, N_Q_HEADS), F32)],
        compiler_params=_params(),
    )(qn, kf, kf, kf, vf, vf, vf, sinks, attn, dattn)


def _qk_bwd(qkv, q_norm_t, k_norm_t, e_mat, dq, dkc, dkp, dkm, dvc, dvp, dvm):
    l_dim = qkv.shape[0]
    nb = l_dim // BLK
    wide = 8 * BLK

    def fold(x):
        half = _lane_half((BLK, BLK))
        blocks = []
        for kb in range(2):
            t = []
            for kv in (2 * kb, 2 * kb + 1):
                own = kv % 2
                a = x[:, BLK * (2 * kv + own):BLK * (2 * kv + own) + BLK]
                b = pltpu.roll(x[:, BLK * (2 * kv + 1 - own):BLK * (2 * kv + 1 - own) + BLK], HEAD_DIM, axis=1)
                t.append(a + b)
            blocks.append(jnp.where(half == 0, t[0], t[1]))
        return jnp.concatenate(blocks, axis=1)

    def body(x_ref, qg_ref, kg_ref, e_ref, dq_ref, dkc_ref, dkp_ref, dkm_ref, dvc_ref, dvp_ref, dvm_ref,
             o_ref, dqg_ref, dkg_ref):
        i = pl.program_id(0)

        @pl.when(i == 0)
        def _():
            dqg_ref[...] = jnp.zeros_like(dqg_ref)
            dkg_ref[...] = jnp.zeros_like(dkg_ref)

        first = jnp.where(i == 0, 1.0, 0.0)
        not_last = jnp.where(i < nb - 1, 1.0, 0.0)
        dk_x = dkc_ref[...] + not_last * dkp_ref[...] + first * dkm_ref[...]
        dv_x = dvc_ref[...] + not_last * dvp_ref[...] + first * dvm_ref[...]
        x = x_ref[...]
        dqx, dqg = _head_rms_bwd(x[:, :Q_W], qg_ref[...], dq_ref[...] * (HEAD_DIM ** -0.5), e_ref)
        dkx, dkg = _head_rms_bwd(x[:, Q_W:Q_W + KV_W], kg_ref[...], fold(dk_x), e_ref)
        o_ref[:, :Q_W] = dqx.astype(BF16)
        o_ref[:, Q_W:Q_W + KV_W] = dkx.astype(BF16)
        o_ref[:, Q_W + KV_W:] = fold(dv_x).astype(BF16)
        dqg_ref[...] += dqg
        dkg_ref[...] += dkg

    nxt = lambda i: jnp.minimum(i + 1, nb - 1)
    row_wide = _row_spec(BLK, wide)
    nxt_wide = _row_spec(BLK, wide, nxt)
    acc_wide = _full_spec((BLK, wide))
    return pl.pallas_call(
        body, name="qk_bwd", grid=(nb,),
        in_specs=[_row_spec(BLK, Q_W + 2 * KV_W), _full_spec((1, Q_W)), _full_spec((1, KV_W)),
                  _full_spec((256, 256)), _row_spec(BLK, Q_W),
                  row_wide, nxt_wide, acc_wide, row_wide, nxt_wide, acc_wide],
        out_specs=[_row_spec(BLK, Q_W + 2 * KV_W), _full_spec((1, Q_W)), _full_spec((1, KV_W))],
        out_shape=[jax.ShapeDtypeStruct((l_dim, Q_W + 2 * KV_W), BF16),
                   jax.ShapeDtypeStruct((1, Q_W), F32), jax.ShapeDtypeStruct((1, KV_W), F32)],
        compiler_params=_params(),
    )(qkv, q_norm_t, k_norm_t, e_mat, dq, dkc, dkp, dkm, dvc, dvp, dvm)


GRP = 8


def _strided(r, g):
    return pl.ds(r, g, stride=GRP)


def _slab(ref, base, r, g):
    return jnp.concatenate([ref[base + i, _strided(r, g), :] for i in range(LB_KB)], axis=1)


def _slab_store(ref, base, r, g, val):
    for i in range(LB_KB):
        ref[base + i, _strided(r, g), :] = val[:, BLK * i:BLK * i + BLK]


def _group_totals(xr_ref, xi_ref, base, ar, ai, reverse):
    g = xr_ref.shape[1] // GRP
    sr = si = None
    for r in (range(GRP - 1, -1, -1) if reverse else range(GRP)):
        xr, xi = _slab(xr_ref, base, r, g), _slab(xi_ref, base, r, g)
        if sr is not None:
            xr, xi = xr + ar * sr - ai * si, xi + ar * si + ai * sr
        sr, si = xr, xi
    return sr, si


def _carry_scan(tr, ti, sqr_ref, sqi_ref, lanes, sign, reverse):
    g = tr.shape[0]
    row = lax.broadcasted_iota(jnp.int32, tr.shape, 0)
    idx, s = 0, 1
    while s < g:
        ar = sqr_ref[idx:idx + 1, lanes]
        ai = sign * sqi_ref[idx:idx + 1, lanes]
        shift, keep = (g - s, row < g - s) if reverse else (s, row >= s)
        pr = jnp.where(keep, pltpu.roll(tr, shift, axis=0), 0.0)
        pi = jnp.where(keep, pltpu.roll(ti, shift, axis=0), 0.0)
        tr, ti = tr + ar * pr - ai * pi, ti + ar * pi + ai * pr
        idx, s = idx + 1, 2 * s
    return tr, ti


def _ssm_fwd(u, wb_re, wb_im, wc_re, wc_im, d_skip, tabs):
    l_dim = u.shape[0]
    nb = l_dim // BLK
    g = BLK // GRP

    def body(u_ref, wbr_ref, wbi_ref, wcr_ref, wci_ref, d_ref, a1r_ref, a1i_ref, sqr_ref, sqi_ref,
             seqr_ref, seqi_ref, y_ref, z_ref, sr_ref, si_ref, cr_ref, ci_ref, xr_ref, xi_ref):
        @pl.when(pl.program_id(0) == 0)
        def _():
            cr_ref[...] = jnp.zeros_like(cr_ref)
            ci_ref[...] = jnp.zeros_like(ci_ref)

        row = lax.broadcasted_iota(jnp.int32, (g, ST_KB), 0)
        for kb in range(SSM_KB):
            ch = slice(BLK * kb, BLK * kb + BLK)
            lanes = slice(ST_KB * kb, ST_KB * kb + ST_KB)
            u_kb = u_ref[:, ch]
            ub = u_kb.astype(BF16)
            xr = jnp.dot(ub, wbr_ref[kb], preferred_element_type=F32)
            xi = jnp.dot(ub, wbi_ref[kb], preferred_element_type=F32)
            base = LB_KB * kb
            for i in range(LB_KB):
                xr_ref[base + i] = xr[:, BLK * i:BLK * i + BLK]
                xi_ref[base + i] = xi[:, BLK * i:BLK * i + BLK]
            ar, ai = a1r_ref[0:1, lanes], a1i_ref[0:1, lanes]
            tr, ti = _group_totals(xr_ref, xi_ref, base, ar, ai, reverse=False)
            tr, ti = _carry_scan(tr, ti, sqr_ref, sqi_ref, lanes, 1.0, reverse=False)
            cin_r, cin_i = cr_ref[0:1, lanes], ci_ref[0:1, lanes]
            qr, qi = seqr_ref[:, lanes], seqi_ref[:, lanes]
            tr, ti = tr + qr * cin_r - qi * cin_i, ti + qr * cin_i + qi * cin_r
            cr_ref[0:1, lanes] = jnp.sum(jnp.where(row == g - 1, tr, 0.0), axis=0, keepdims=True)
            ci_ref[0:1, lanes] = jnp.sum(jnp.where(row == g - 1, ti, 0.0), axis=0, keepdims=True)
            pr = jnp.where(row == 0, cin_r, pltpu.roll(tr, 1, axis=0))
            pi = jnp.where(row == 0, cin_i, pltpu.roll(ti, 1, axis=0))
            for r in range(GRP):
                pr, pi = (_slab(xr_ref, base, r, g) + ar * pr - ai * pi,
                          _slab(xi_ref, base, r, g) + ar * pi + ai * pr)
                _slab_store(sr_ref, base, r, g, pr)
                _slab_store(si_ref, base, r, g, pi)
            s_r = jnp.concatenate([sr_ref[LB_KB * kb + i] for i in range(LB_KB)], axis=1)
            s_i = jnp.concatenate([si_ref[LB_KB * kb + i] for i in range(LB_KB)], axis=1)
            y = (jnp.dot(s_r.astype(BF16), wcr_ref[kb], preferred_element_type=F32)
                 - jnp.dot(s_i.astype(BF16), wci_ref[kb], preferred_element_type=F32)
                 + d_ref[:, ch] * u_kb)
            y_ref[:, ch] = y
            z_ref[:, ch] = _gelu(y).astype(BF16)

    wb_spec = _full_spec((SSM_KB, BLK, ST_KB))
    wc_spec = _full_spec((SSM_KB, ST_KB, BLK))
    tab_specs = [_full_spec(t.shape) for t in tabs[:6]]
    state_spec = pl.BlockSpec((N_LB, BLK, BLK), lambda i: (0, i, 0))
    state_shape = jax.ShapeDtypeStruct((N_LB, l_dim, BLK), F32)
    return pl.pallas_call(
        body, name="ssm_fwd", grid=(nb,),
        in_specs=[_row_spec(BLK, D_MODEL), wb_spec, wb_spec, wc_spec, wc_spec, _full_spec((1, D_MODEL))]
        + tab_specs,
        out_specs=[_row_spec(BLK, D_MODEL), _row_spec(BLK, D_MODEL), state_spec, state_spec],
        out_shape=[jax.ShapeDtypeStruct((l_dim, D_MODEL), F32), jax.ShapeDtypeStruct((l_dim, D_MODEL), BF16),
                   state_shape, state_shape],
        scratch_shapes=[pltpu.VMEM((8, N_STATE), F32), pltpu.VMEM((8, N_STATE), F32),
                        pltpu.VMEM((N_LB, BLK, BLK), F32), pltpu.VMEM((N_LB, BLK, BLK), F32)],
        compiler_params=_params(),
    )(u, wb_re, wb_im, wc_re, wc_im, d_skip, *tabs[:6])


def _ssm_bwd(dz, y, u, s_re, s_im, wb_re, wb_im, wc_re, wc_im, d_skip, tabs):
    l_dim = u.shape[0]
    nb = l_dim // BLK
    g = BLK // GRP

    def body(dz_ref, y_ref, u_ref, sr_ref, si_ref, wbr_ref, wbi_ref, wcr_ref, wci_ref, d_ref,
             a1r_ref, a1i_ref, sqr_ref, sqi_ref, revr_ref, revi_ref,
             du_ref, dd_ref, dar_ref, dai_ref, dwbr_ref, dwbi_ref, dwcr_ref, dwci_ref,
             cr_ref, ci_ref, gr_ref, gi_ref):
        @pl.when(pl.program_id(0) == 0)
        def _():
            for r in (cr_ref, ci_ref, dd_ref, dar_ref, dai_ref, dwbr_ref, dwbi_ref, dwcr_ref, dwci_ref):
                r[...] = jnp.zeros_like(r)

        tn = (((0,), (0,)), ((), ()))
        nt = (((1,), (1,)), ((), ()))
        row = lax.broadcasted_iota(jnp.int32, (g, ST_KB), 0)
        for kb in range(SSM_KB):
            ch = slice(BLK * kb, BLK * kb + BLK)
            lanes = slice(ST_KB * kb, ST_KB * kb + ST_KB)
            u_kb = u_ref[:, ch]
            dy = dz_ref[:, ch] * _gelu_grad(y_ref[:, ch])
            dyb = dy.astype(BF16)
            ub = u_kb.astype(BF16)
            dd_ref[:, ch] += jnp.sum(dy * u_kb, axis=0, keepdims=True)
            ds_r = lax.dot_general(dyb, wcr_ref[kb], nt, preferred_element_type=F32)
            ds_i = -lax.dot_general(dyb, wci_ref[kb], nt, preferred_element_type=F32)
            base = LB_KB * kb
            for i in range(LB_KB):
                gr_ref[base + i] = ds_r[:, BLK * i:BLK * i + BLK]
                gi_ref[base + i] = ds_i[:, BLK * i:BLK * i + BLK]
            ar, ai = a1r_ref[0:1, lanes], -a1i_ref[0:1, lanes]
            tr, ti = _group_totals(gr_ref, gi_ref, base, ar, ai, reverse=True)
            tr, ti = _carry_scan(tr, ti, sqr_ref, sqi_ref, lanes, -1.0, reverse=True)
            cin_r, cin_i = cr_ref[0:1, lanes], ci_ref[0:1, lanes]
            qr, qi = revr_ref[:, lanes], -revi_ref[:, lanes]
            tr, ti = tr + qr * cin_r - qi * cin_i, ti + qr * cin_i + qi * cin_r
            cr_ref[0:1, lanes] = jnp.sum(jnp.where(row == 0, tr, 0.0), axis=0, keepdims=True)
            ci_ref[0:1, lanes] = jnp.sum(jnp.where(row == 0, ti, 0.0), axis=0, keepdims=True)
            nr = jnp.where(row == g - 1, cin_r, pltpu.roll(tr, g - 1, axis=0))
            ni = jnp.where(row == g - 1, cin_i, pltpu.roll(ti, g - 1, axis=0))
            acc_r = jnp.zeros((g, ST_KB), F32)
            acc_i = jnp.zeros((g, ST_KB), F32)
            for r in range(GRP - 1, -1, -1):
                s_r, s_i = _slab(sr_ref, base, r, g), _slab(si_ref, base, r, g)
                acc_r = acc_r + (nr * s_r + ni * s_i)
                acc_i = acc_i + (ni * s_r - nr * s_i)
                nr, ni = (_slab(gr_ref, base, r, g) + ar * nr - ai * ni,
                          _slab(gi_ref, base, r, g) + ar * ni + ai * nr)
                _slab_store(gr_ref, base, r, g, nr)
                _slab_store(gi_ref, base, r, g, ni)
            dar_ref[:, lanes] += jnp.sum(acc_r, axis=0, keepdims=True)
            dai_ref[:, lanes] += jnp.sum(acc_i, axis=0, keepdims=True)
            grb = jnp.concatenate([gr_ref[base + i] for i in range(LB_KB)], axis=1).astype(BF16)
            gib = jnp.concatenate([gi_ref[base + i] for i in range(LB_KB)], axis=1).astype(BF16)
            srb = jnp.concatenate([sr_ref[base + i] for i in range(LB_KB)], axis=1).astype(BF16)
            sib = jnp.concatenate([si_ref[base + i] for i in range(LB_KB)], axis=1).astype(BF16)
            du = (lax.dot_general(grb, wbr_ref[kb], nt, preferred_element_type=F32)
                  + lax.dot_general(gib, wbi_ref[kb], nt, preferred_element_type=F32)
                  + d_ref[:, ch] * dy)
            du_ref[:, ch] = du.astype(BF16)
            dwbr_ref[kb] += lax.dot_general(ub, grb, tn, preferred_element_type=F32)
            dwbi_ref[kb] += lax.dot_general(ub, gib, tn, preferred_element_type=F32)
            dwcr_ref[kb] += lax.dot_general(srb, dyb, tn, preferred_element_type=F32)
            dwci_ref[kb] -= lax.dot_general(sib, dyb, tn, preferred_element_type=F32)

    rev = lambda i: nb - 1 - i
    wb_spec = _full_spec((SSM_KB, BLK, ST_KB))
    wc_spec = _full_spec((SSM_KB, ST_KB, BLK))
    tab_in = [tabs[0], tabs[1], tabs[2], tabs[3], tabs[6], tabs[7]]
    tab_specs = [_full_spec(t.shape) for t in tab_in]
    vec = _full_spec((1, D_MODEL))
    svec = _full_spec((1, N_STATE))
    state_spec = pl.BlockSpec((N_LB, BLK, BLK), lambda i: (0, nb - 1 - i, 0))
    return pl.pallas_call(
        body, name="ssm_bwd", grid=(nb,),
        in_specs=[_row_spec(BLK, D_MODEL, rev), _row_spec(BLK, D_MODEL, rev), _row_spec(BLK, D_MODEL, rev),
                  state_spec, state_spec,
                  wb_spec, wb_spec, wc_spec, wc_spec, vec] + tab_specs,
        out_specs=[_row_spec(BLK, D_MODEL, rev), vec, svec, svec, wb_spec, wb_spec, wc_spec, wc_spec],
        out_shape=[jax.ShapeDtypeStruct((l_dim, D_MODEL), BF16), jax.ShapeDtypeStruct((1, D_MODEL), F32),
                   jax.ShapeDtypeStruct((1, N_STATE), F32), jax.ShapeDtypeStruct((1, N_STATE), F32),
                   jax.ShapeDtypeStruct((SSM_KB, BLK, ST_KB), F32), jax.ShapeDtypeStruct((SSM_KB, BLK, ST_KB), F32),
                   jax.ShapeDtypeStruct((SSM_KB, ST_KB, BLK), F32), jax.ShapeDtypeStruct((SSM_KB, ST_KB, BLK), F32)],
        scratch_shapes=[pltpu.VMEM((8, N_STATE), F32), pltpu.VMEM((8, N_STATE), F32),
                        pltpu.VMEM((N_LB, BLK, BLK), F32), pltpu.VMEM((N_LB, BLK, BLK), F32)],
        compiler_params=_params(),
    )(dz, y, u, s_re, s_im, wb_re, wb_im, wc_re, wc_im, d_skip, *tab_in)


def _discretize(lam_re, lam_im, log_dt, b_re, b_im):
    dt = jnp.exp(log_dt)[:, None]
    mag = jnp.exp(lam_re * dt)
    ar, ai = mag * jnp.cos(lam_im * dt), mag * jnp.sin(lam_im * dt)
    den = lam_re * lam_re + lam_im * lam_im
    nr, ni = ar - 1.0, ai
    fr, fi = (nr * lam_re + ni * lam_im) / den, (ni * lam_re - nr * lam_im) / den
    bbar_re = fr[..., None] * b_re - fi[..., None] * b_im
    bbar_im = fr[..., None] * b_im + fi[..., None] * b_re
    return ar, ai, bbar_re, bbar_im


def _block_diag_b(bbar):
    eye = jnp.eye(8, dtype=bbar.dtype)
    return jnp.einsum("kgpc,gh->kgchp", bbar.reshape(8, 8, SSM_STATE, SSM_GROUP_CH), eye).reshape(8, BLK, ST_KB)


def _block_diag_b_t(dwb):
    eye = jnp.eye(8, dtype=dwb.dtype)
    return jnp.einsum("kgchp,gh->kgpc", dwb.reshape(8, 8, SSM_GROUP_CH, 8, SSM_STATE), eye).reshape(
        SSM_GROUPS, SSM_STATE, SSM_GROUP_CH)


def _block_diag_c(c):
    eye = jnp.eye(8, dtype=c.dtype)
    return jnp.einsum("kgcp,gh->kgphc", c.reshape(8, 8, SSM_GROUP_CH, SSM_STATE), eye).reshape(8, ST_KB, BLK)


def _block_diag_c_t(dwc):
    eye = jnp.eye(8, dtype=dwc.dtype)
    return jnp.einsum("kgphc,gh->kgcp", dwc.reshape(8, 8, SSM_STATE, 8, SSM_GROUP_CH), eye).reshape(
        SSM_GROUPS, SSM_GROUP_CH, SSM_STATE)


def _powers(br, bi, n):
    pr, pi = br, bi
    cr, ci = br, bi
    while pr.shape[0] < n:
        pr, pi = (jnp.concatenate([pr, pr * cr - pi * ci], axis=0),
                  jnp.concatenate([pi, pr * ci + pi * cr], axis=0))
        cr, ci = cr * cr - ci * ci, 2.0 * cr * ci
    return pr[:n], pi[:n]


def _power_tables(ar, ai):
    g = BLK // GRP
    a1r, a1i = _powers(ar, ai, GRP)
    seqr, seqi = _powers(a1r[GRP - 1:], a1i[GRP - 1:], g)
    sq_r, sq_i = [seqr[0:1]], [seqi[0:1]]
    while len(sq_r) < 8:
        r, i = sq_r[-1], sq_i[-1]
        sq_r.append(r * r - i * i)
        sq_i.append(2.0 * r * i)
    sqr, sqi = jnp.concatenate(sq_r, axis=0), jnp.concatenate(sq_i, axis=0)
    return a1r, a1i, sqr, sqi, seqr, seqi, seqr[::-1], seqi[::-1]


def _all_gather(name, arrs):
    n = len(arrs)

    def body(*refs):
        ins, outs = refs[:n], refs[n:2 * n]
        send_sems, recv_sems, local_sems = refs[2 * n:]
        x, y, c = lax.axis_index("x"), lax.axis_index("y"), lax.axis_index("c")
        me, sibling = (x, y, c), (x, y, 1 - c)
        chips = [(1 - x, y), (x, 1 - y), (1 - x, 1 - y)]

        def rows(a, px, py, pc):
            m = ins[a].shape[0]
            return outs[a].at[pl.ds((4 * px + 2 * py + pc) * m, m), :]

        def copy(a, k, block, to, src=None):
            return pltpu.make_async_remote_copy(
                src_ref=rows(a, *block) if src is None else src, dst_ref=rows(a, *block),
                send_sem=send_sems.at[a, k], recv_sem=recv_sems.at[a, k],
                device_id=to, device_id_type=MESH)

        mine = [pltpu.make_async_copy(ins[a], rows(a, *me), local_sems.at[a]) for a in range(n)]
        for cp in mine:
            cp.start()
        first = []
        for a in range(n):
            first.append(copy(a, 0, me, sibling, src=ins[a]))
            first += [copy(a, 1 + j, me, (*chip, c), src=ins[a]) for j, chip in enumerate(chips)]
        for cp in first:
            cp.start()
        passed = []
        for j, chip in enumerate(chips):
            for a in range(n):
                copy(a, 1 + j, (*chip, c), me).wait_recv()
                cp = copy(a, 4 + j, (*chip, c), sibling)
                cp.start()
                passed.append(cp)
        for a in range(n):
            copy(a, 0, sibling, me).wait_recv()
            for j, chip in enumerate(chips):
                copy(a, 4 + j, (*chip, 1 - c), me).wait_recv()
        for cp in first + passed:
            cp.wait_send()
        for cp in mine:
            cp.wait()

    any_spec = pl.BlockSpec(memory_space=pl.ANY)
    return pl.pallas_call(
        body, name=name,
        in_specs=[any_spec] * n, out_specs=[any_spec] * n,
        out_shape=[jax.ShapeDtypeStruct((N_DEV * a.shape[0], a.shape[1]), a.dtype) for a in arrs],
        scratch_shapes=[pltpu.SemaphoreType.DMA((n, 7)), pltpu.SemaphoreType.DMA((n, 7)),
                        pltpu.SemaphoreType.DMA((n,))],
        compiler_params=_params(has_side_effects=True),
    )(*arrs)


HBM_SPEC = pl.BlockSpec(memory_space=pltpu.HBM)
SEM_SPEC = pl.BlockSpec(memory_space=pltpu.SEMAPHORE)
DATAFLOW = pltpu.SideEffectType.DATAFLOW_SIDE_EFFECTING


def _peer_copies(src_refs, land_refs, send_sems, recv_sems, chunked):
    x, y, c = lax.axis_index("x"), lax.axis_index("y"), lax.axis_index("c")
    me = 4 * x + 2 * y + c
    copies = []
    for a, (src, land) in enumerate(zip(src_refs, land_refs)):
        m = land.shape[0] // N_DEV
        for k in range(N_DEV - 1):
            rel = k + 1
            bx, by, bc = (rel >> 2) & 1, (rel >> 1) & 1, rel & 1
            peer = (x + bx - 2 * x * bx, y + by - 2 * y * by, c + bc - 2 * c * bc)
            p_idx = 4 * peer[0] + 2 * peer[1] + peer[2]
            copies.append(pltpu.make_async_remote_copy(
                src_ref=src.at[pl.ds(p_idx * m, m), :] if chunked else src,
                dst_ref=land.at[pl.ds(me * m, m), :],
                send_sem=send_sems[a * (N_DEV - 1) + k], recv_sem=recv_sems[a * (N_DEV - 1) + k],
                device_id=peer, device_id_type=MESH))
    return copies


def _send_start(name, srcs, lands, chunked):
    n = len(srcs)
    ns = n * (N_DEV - 1)

    def body(*refs):
        src_refs, land_refs = refs[:n], refs[n:2 * n]
        send_sems, recv_sems = refs[2 * n:2 * n + ns], refs[2 * n + ns:2 * n + 2 * ns]
        token = refs[-1]
        for cp in _peer_copies(src_refs, land_refs, send_sems, recv_sems, chunked):
            cp.start()
        token[...] = jnp.zeros_like(token)

    ins = [pltpu.with_memory_space_constraint(a, pltpu.HBM) for a in list(srcs) + list(lands)]
    out = pl.pallas_call(
        body, name=name,
        in_specs=[HBM_SPEC] * (2 * n),
        out_specs=[SEM_SPEC] * (2 * ns) + [HBM_SPEC] * (2 * n) + [pl.BlockSpec(memory_space=pltpu.VMEM)],
        out_shape=[pltpu.SemaphoreType.DMA(())] * (2 * ns)
        + [pltpu.HBM(a.shape, a.dtype) for a in list(srcs) + list(lands)]
        + [jax.ShapeDtypeStruct((8, BLK), F32)],
        input_output_aliases={i: i + 2 * ns for i in range(2 * n)},
        compiler_params=pltpu.CompilerParams(has_side_effects=DATAFLOW),
    )(*ins)
    return out[:ns], out[ns:2 * ns], out[2 * ns:2 * ns + n], out[2 * ns + n:2 * ns + 2 * n], out[-1]


def _send_wait(name, send_sems, recv_sems, srcs, lands, after, chunked):
    n = len(srcs)
    ns = n * (N_DEV - 1)

    def body(*refs):
        src_refs, land_refs = refs[:n], refs[n:2 * n]
        s_sems, r_sems = refs[2 * n:2 * n + ns], refs[2 * n + ns:2 * n + 2 * ns]
        copies = _peer_copies(src_refs, land_refs, s_sems, r_sems, chunked)
        for cp in copies:
            cp.wait_send()
        for cp in copies:
            cp.wait_recv()

    out = pl.pallas_call(
        body, name=name,
        in_specs=[HBM_SPEC] * (2 * n) + [SEM_SPEC] * (2 * ns) + [pl.BlockSpec(memory_space=pl.ANY)],
        out_specs=[HBM_SPEC] * (2 * n),
        out_shape=[pltpu.HBM(a.shape, a.dtype) for a in list(srcs) + list(lands)],
        input_output_aliases={i: i for i in range(2 * n)},
        compiler_params=pltpu.CompilerParams(has_side_effects=DATAFLOW),
    )(*srcs, *lands, *send_sems, *recv_sems, after)
    return out[n:]


def _sum_slots(name, recv, own):
    m, ncol = own.shape
    tr = m // 2 if (m // 2) % 16 == 0 else m
    g = m // tr

    def body(*refs):
        slots, own_ref, o_ref = refs[:N_DEV], refs[N_DEV], refs[N_DEV + 1]
        me = _my_index()
        tot = None
        for s in range(N_DEV):
            v = jnp.where(me == s, own_ref[...], slots[s][...].astype(F32))
            tot = v if tot is None else tot + v
        o_ref[...] = tot

    def slot_spec(s):
        return pl.BlockSpec((tr, ncol), lambda i: (s * g + i, 0))

    return pl.pallas_call(
        body, name=name, grid=(g,),
        in_specs=[slot_spec(s) for s in range(N_DEV)] + [pl.BlockSpec((tr, ncol), lambda i: (i, 0))],
        out_specs=pl.BlockSpec((tr, ncol), lambda i: (i, 0)),
        out_shape=jax.ShapeDtypeStruct((m, ncol), F32),
        compiler_params=_params(),
    )(*([recv] * N_DEV), own)


def _sum_gathered(name, gathered, rows):
    tr = _pick(rows, 512, 8)
    g = rows // tr

    def body(*refs):
        o_ref = refs[N_DEV]
        tot = refs[0][...]
        for s in range(1, N_DEV):
            tot = tot + refs[s][...]
        o_ref[...] = tot

    return pl.pallas_call(
        body, name=name, grid=(g,),
        in_specs=[pl.BlockSpec((tr, BLK), (lambda i, s=s: (s * g + i, 0))) for s in range(N_DEV)],
        out_specs=pl.BlockSpec((tr, BLK), lambda i: (i, 0)),
        out_shape=jax.ShapeDtypeStruct((rows, BLK), F32),
        compiler_params=_params(),
    )(*([gathered] * N_DEV))


def _adamw(name, w, g, m, v):
    r, c = w.shape
    tr = _pick(r, 256, 8) if r % 8 == 0 else r
    c1 = 1.0 - ADAM_B1 ** ADAM_STEP
    c2 = 1.0 - ADAM_B2 ** ADAM_STEP

    def body(w_ref, g_ref, m_ref, v_ref, d_ref, nm_ref, nv_ref):
        gv = g_ref[...]
        nm = ADAM_B1 * m_ref[...] + (1.0 - ADAM_B1) * gv
        nv = ADAM_B2 * v_ref[...] + (1.0 - ADAM_B2) * (gv * gv)
        m_hat = nm / c1
        v_hat = nv / c2
        d_ref[...] = -ADAM_LR * (m_hat / (jnp.sqrt(v_hat) + ADAM_EPS) + ADAM_WD * w_ref[...])
        nm_ref[...] = nm
        nv_ref[...] = nv

    spec = pl.BlockSpec((tr, c), lambda i: (i, 0))
    shape = jax.ShapeDtypeStruct((r, c), F32)
    return pl.pallas_call(
        body, name=name, grid=(r // tr,),
        in_specs=[spec] * 4, out_specs=[spec] * 3, out_shape=[shape] * 3,
        compiler_params=_params(),
    )(w, g, m, v)


def _pack(parts):
    flat = []
    for p in parts:
        v = p.reshape(-1)
        flat.append(jnp.pad(v, (0, (-v.shape[0]) % BLK)))
    v = jnp.concatenate(flat)
    v = jnp.pad(v, (0, (-v.shape[0]) % (8 * BLK)))
    return v.reshape(-1, BLK)


def _unpack(buf, shapes):
    flat = buf.reshape(-1)
    out, off = [], 0
    for shp in shapes:
        size = math.prod(shp)
        out.append(flat[off:off + size].reshape(shp))
        off += size + (-size) % BLK
    return out


def kernel(x, meta_tokens, norm_mix, w_in, q_norm, k_norm, attn_sinks, lam_re, lam_im, log_dt, ssm_b_re, ssm_b_im, ssm_c_re, ssm_c_im, ssm_d, w_glu, attn_branch_norm, ssm_branch_norm, w_out, norm_ffn, w_ffn_in, w_ffn_out, loss_target, m_meta_tokens, m_norm_mix, m_w_in, m_q_norm, m_k_norm, m_attn_sinks, m_lam_re, m_lam_im, m_log_dt, m_ssm_b_re, m_ssm_b_im, m_ssm_c_re, m_ssm_c_im, m_ssm_d, m_w_glu, m_attn_branch_norm, m_ssm_branch_norm, m_w_out, m_norm_ffn, m_w_ffn_in, m_w_ffn_out, v_meta_tokens, v_norm_mix, v_w_in, v_q_norm, v_k_norm, v_attn_sinks, v_lam_re, v_lam_im, v_log_dt, v_ssm_b_re, v_ssm_b_im, v_ssm_c_re, v_ssm_c_im, v_ssm_d, v_w_glu, v_attn_branch_norm, v_ssm_branch_norm, v_w_out, v_norm_ffn, v_w_ffn_in, v_w_ffn_out):
    args = dict(locals())
    weights = {n: args[n] for n in WEIGHTS}
    mom_m = {n: args["m_" + n] for n in WEIGHTS}
    mom_v = {n: args["v_" + n] for n in WEIGHTS}

    x2d = x[0]
    target2d = loss_target[0]
    s_len = x2d.shape[0]
    l_dim = s_len + BLK
    tm_row = _pick(l_dim, 320)
    tm_mm = _pick(l_dim, 1040)
    tl_tn = _pick(l_dim, 832)

    shard_in = w_in[0].T.astype(BF16)
    shard_glu = w_glu[0].T.astype(BF16)
    shard_out = w_out[0].astype(BF16)
    shard_ffn_in = w_ffn_in[0].T.astype(BF16)
    shard_ffn_out = w_ffn_out[0].astype(BF16)
    shard_meta = meta_tokens.T
    me = _my_index()

    def landing(shard):
        m_rows, cols = shard.shape
        return lax.dynamic_update_slice(lax.empty((N_DEV * m_rows, cols), shard.dtype), shard, (me * m_rows, 0))

    first = [shard_in, shard_meta]
    later = [shard_glu, shard_out, shard_ffn_in, shard_ffn_out]
    ga = _send_start("gather_start_a", first, [landing(s) for s in first], chunked=False)
    gb = _send_start("gather_start_b", later, [landing(s) for s in later], chunked=False)

    nm_t = norm_mix + (ga[4][0:1, 0:1] + gb[4][0:1, 0:1])
    qn_t, kn_t = jnp.tile(q_norm, (1, N_Q_HEADS)), jnp.tile(k_norm, (1, N_KV_HEADS))
    e_mat = jnp.kron(jnp.eye(4, dtype=F32), jnp.ones((HEAD_DIM, HEAD_DIM), F32)).astype(BF16)

    def disc(lr, li, ldt, br, bi):
        return _discretize(lr[0], li[0], ldt[0], br[0], bi[0])

    (abar_re, abar_im, bbar_re, bbar_im), disc_vjp = jax.vjp(disc, lam_re, lam_im, log_dt, ssm_b_re, ssm_b_im)
    wb_re, wb_im = _block_diag_b(bbar_re).astype(BF16), _block_diag_b(bbar_im).astype(BF16)
    wc_re, wc_im = _block_diag_c(ssm_c_re[0]).astype(BF16), _block_diag_c(ssm_c_im[0]).astype(BF16)
    tabs = _power_tables(abar_re.reshape(1, N_STATE), abar_im.reshape(1, N_STATE))

    h0, xn = _embed_norm(x2d, nm_t)
    wt_in, meta_t = _send_wait("gather_wait_a", ga[0], ga[1], ga[2], ga[3], xn, chunked=False)
    meta_pad = jnp.pad(meta_t.T, ((PAD, 0), (0, 0)))
    h0, xn = _embed_meta(meta_pad, nm_t, h0, xn)
    qkv = _matmul("proj_qkv", xn, wt_in, nt=True, tm=tm_mm, tn=512, tk=D_MODEL, n=Q_W + 2 * KV_W, w_off=0)
    u = _matmul("proj_u", xn, wt_in, nt=True, tm=tm_mm, tn=512, tk=D_MODEL, n=D_MODEL, w_off=3)
    gates = _matmul("proj_gates", xn, wt_in, nt=True, tm=tm_mm, tn=512, tk=D_MODEL, n=2 * D_MODEL, w_off=5)
    qn, kf, vf = _qk_prep(qkv, qn_t, kn_t, e_mat, tm_row)
    attn = _attn_fwd(qn, kf, vf, attn_sinks)
    y, z, s_re, s_im = _ssm_fwd(u, wb_re, wb_im, wc_re, wc_im, ssm_d, tabs)
    wt_glu, w_out_f, wt_ffn_in, w_ffn_out_f = _send_wait("gather_wait_b", gb[0], gb[1], gb[2], gb[3], z,
                                                         chunked=False)
    zab = _matmul("glu_proj", z, wt_glu, nt=True, tm=tm_mm, tn=1024, tk=D_MODEL)
    merged = _merge_fwd(attn, zab, gates, attn_branch_norm, ssm_branch_norm, tm_row)
    h1 = _matmul("out_proj", merged, w_out_f, nt=False, tm=tm_mm, tn=1024, tk=D_MODEL, res=h0)
    hn = _norm_cast("ffn_norm", h1, norm_ffn, tm_row)
    gu = _matmul("ffn_in", hn, wt_ffn_in, nt=True, tm=tm_mm, tn=512, tk=D_MODEL)
    act = _swiglu_fwd(gu, tm_row)
    h2 = _matmul("ffn_out", act, w_ffn_out_f, nt=False, tm=tm_mm, tn=1024, tk=1408, res=h1)
    dh2, dh2_b, loss_part = _loss_grad(h2, target2d)

    dact = _matmul("d_act", dh2_b, w_ffn_out_f, nt=True, tm=tm_mm, tn=1408, tk=D_MODEL)
    dgu = _swiglu_bwd(gu, dact, tm_row)
    def exchange_start(name, grads_b):
        return _send_start(name, grads_b, [jnp.zeros(g.shape, BF16) for g in grads_b], chunked=True)

    g_ffn_out, g_ffn_out_b = _matmul_tn("g_ffn_out", act, dh2_b, tm=1408, tn=1024, tl=tl_tn)
    g_ffn_in_t, g_ffn_in_b = _matmul_tn("g_ffn_in", dgu, hn, tm=1408, tn=1024, tl=tl_tn)
    ex1 = exchange_start("exchange_start_ffn", [g_ffn_in_b, g_ffn_out_b])
    dhn = _matmul("d_hn", dgu, wt_ffn_in, nt=False, tm=tm_mm, tn=1024, tk=1408)
    dh1, dh1_b, g_norm_ffn = _norm_bwd_res("ffn_norm_bwd", h1, norm_ffn + ex1[4][0:1, 0:1], dhn, dh2, tm_row)
    dmerged = _matmul("d_merged", dh1_b, w_out_f, nt=True, tm=tm_mm, tn=1024, tk=D_MODEL)
    dattn, dzab, dgates, g_abn, g_sbn = _merge_bwd(attn, zab, gates, attn_branch_norm, ssm_branch_norm,
                                                     dmerged, tm_row)
    g_out, g_out_b = _matmul_tn("g_out", merged, dh1_b, tm=1024, tn=1024, tl=tl_tn)
    dz = _matmul("d_z", dzab, wt_glu, nt=False, tm=tm_mm, tn=1024, tk=1024)
    g_glu_t, g_glu_b = _matmul_tn("g_glu", dzab, z, tm=1024, tn=1024, tl=tl_tn)
    ex2 = exchange_start("exchange_start_mix", [g_glu_b, g_out_b])
    du, g_ssm_d, g_ar, g_ai, g_wbr, g_wbi, g_wcr, g_wci = _ssm_bwd(
        dz, y, u, s_re, s_im, wb_re, wb_im, wc_re, wc_im, ssm_d + ex2[4][0:1, 0:1], tabs)
    dq, dkc, dkp, dkm, dvc, dvp, dvm, g_sinks = _attn_bwd(qn, kf, vf, attn_sinks, attn, dattn)
    dqkv, g_qn_t, g_kn_t = _qk_bwd(qkv, qn_t, kn_t, e_mat, dq, dkc, dkp, dkm, dvc, dvp, dvm)
    dproj = jnp.concatenate([dqkv, du, dgates], axis=1)
    g_in_t, g_in_b = _matmul_tn("g_in", dproj, xn, tm=1152, tn=1024, tl=tl_tn)
    ex3 = exchange_start("exchange_start_in", [g_in_b])
    dxn = _matmul("d_xn", dproj, wt_in, nt=False, tm=tm_mm, tn=1024, tk=1152)
    grad_x2d, dmeta_blk, g_norm_mix = _final_bwd(h0, nm_t + ex3[4][0:1, 0:1], dxn, dh1)

    g_lam_re, g_lam_im, g_log_dt, g_b_re, g_b_im = disc_vjp(
        (g_ar.reshape(SSM_GROUPS, SSM_STATE), g_ai.reshape(SSM_GROUPS, SSM_STATE),
         _block_diag_b_t(g_wbr), _block_diag_b_t(g_wbi)))
    small_grads = {
        "norm_mix": g_norm_mix, "q_norm": g_qn_t.reshape(N_Q_HEADS, HEAD_DIM).sum(0)[None],
        "k_norm": g_kn_t.reshape(N_KV_HEADS, HEAD_DIM).sum(0)[None], "attn_sinks": g_sinks,
        "lam_re": g_lam_re, "lam_im": g_lam_im, "log_dt": g_log_dt, "ssm_b_re": g_b_re, "ssm_b_im": g_b_im,
        "ssm_c_re": _block_diag_c_t(g_wcr)[None], "ssm_c_im": _block_diag_c_t(g_wci)[None],
        "ssm_d": g_ssm_d, "attn_branch_norm": g_abn, "ssm_branch_norm": g_sbn, "norm_ffn": g_norm_ffn,
    }
    small_shapes = [weights[n].shape for n in SMALL] + [(N_META, D_MODEL)]
    packed = _pack([small_grads[n] for n in SMALL] + [dmeta_blk[PAD:]])
    rows = packed.shape[0]
    (gathered,) = _all_gather("gather_small", [packed])
    g_small = _sum_gathered("sum_small", gathered, rows)
    w_small = _pack([weights[n] for n in SMALL] + [jnp.zeros((N_META, D_MODEL), F32)])
    m_small = _pack([mom_m[n] for n in SMALL] + [jnp.zeros((N_META, D_MODEL), F32)])
    v_small = _pack([mom_v[n] for n in SMALL] + [jnp.zeros((N_META, D_MODEL), F32)])
    d_small, nm_small, nv_small = _adamw("adamw_small", w_small, g_small, m_small, v_small)
    grads = dict(zip(SMALL + ["_meta"], _unpack(g_small, small_shapes)))
    deltas = dict(zip(SMALL, _unpack(d_small, small_shapes[:-1])))
    new_m = dict(zip(SMALL, _unpack(nm_small, small_shapes[:-1])))
    new_v = dict(zip(SMALL, _unpack(nv_small, small_shapes[:-1])))

    grads["meta_tokens"] = lax.dynamic_slice(grads.pop("_meta"), (0, me * BLK), (N_META, BLK))

    recv_ffn_in, recv_ffn_out = _send_wait("exchange_wait_ffn", ex1[0], ex1[1], ex1[2], ex1[3], g_small,
                                           chunked=True)
    recv_glu, recv_out = _send_wait("exchange_wait_mix", ex2[0], ex2[1], ex2[2], ex2[3], recv_ffn_in,
                                    chunked=True)
    (recv_in,) = _send_wait("exchange_wait_in", ex3[0], ex3[1], ex3[2], ex3[3], recv_glu, chunked=True)
    big = [("w_in", g_in_t, True, recv_in), ("w_glu", g_glu_t, True, recv_glu), ("w_out", g_out, False, recv_out),
           ("w_ffn_in", g_ffn_in_t, True, recv_ffn_in), ("w_ffn_out", g_ffn_out, False, recv_ffn_out)]
    for name, g_full, transposed, recv in big:
        m_rows = g_full.shape[0] // N_DEV
        own = lax.dynamic_slice(g_full, (me * m_rows, 0), (m_rows, g_full.shape[1]))
        g_shard = _sum_slots("sum_" + name, recv, own)
        grads[name] = (g_shard.T if transposed else g_shard)[None]

    for name in ["meta_tokens", "w_in", "w_glu", "w_out", "w_ffn_in", "w_ffn_out"]:
        shp = weights[name].shape
        as2d = lambda a: a.reshape(shp[-2], shp[-1])
        d, nm, nv = _adamw("adamw_" + name, as2d(weights[name]), as2d(grads[name]), as2d(mom_m[name]),
                           as2d(mom_v[name]))
        deltas[name], new_m[name], new_v[name] = d.reshape(shp), nm.reshape(shp), nv.reshape(shp)

    loss = lax.psum(loss_part[0, 0], ("x", "y", "c"))
    return (loss, grad_x2d[None], *[grads[n] for n in WEIGHTS], *[deltas[n] for n in WEIGHTS],
            *[new_m[n] for n in WEIGHTS], *[new_v[n] for n in WEIGHTS])
```

```python
import math

import jax
import jax.numpy as jnp
from jax import lax
from jax.experimental import pallas as pl
from jax.experimental.pallas import tpu as pltpu

F32 = jnp.float32
BF16 = jnp.bfloat16

D_MODEL = 1024
N_META = 16
HEAD_DIM = 64
N_Q_HEADS = 16
N_KV_HEADS = 4
Q_W = N_Q_HEADS * HEAD_DIM
KV_W = N_KV_HEADS * HEAD_DIM
SSM_GROUPS = 64
SSM_GROUP_CH = 16
SSM_STATE = 64
N_STATE = SSM_GROUPS * SSM_STATE
D_FF = 2816
IN_COLS = Q_W + 2 * KV_W + 3 * D_MODEL
EPS = 1e-6
BLK = 128
PAD = BLK - N_META
N_DEV = 8
NEG = -1e30
SSM_KB = 8
ST_KB = N_STATE // SSM_KB
LB_KB = ST_KB // BLK
N_LB = N_STATE // BLK

ADAM_LR = 0.001
ADAM_B1 = 0.9
ADAM_B2 = 0.999
ADAM_EPS = 1e-08
ADAM_WD = 0.01
ADAM_STEP = 10

VMEM_LIMIT = 48 * 1024 * 1024
MESH = pl.DeviceIdType.MESH

SMALL = ["norm_mix", "q_norm", "k_norm", "attn_sinks", "lam_re", "lam_im", "log_dt", "ssm_b_re", "ssm_b_im",
         "ssm_c_re", "ssm_c_im", "ssm_d", "attn_branch_norm", "ssm_branch_norm", "norm_ffn"]
WEIGHTS = ["meta_tokens", "norm_mix", "w_in", "q_norm", "k_norm", "attn_sinks", "lam_re", "lam_im", "log_dt",
           "ssm_b_re", "ssm_b_im", "ssm_c_re", "ssm_c_im", "ssm_d", "w_glu", "attn_branch_norm",
           "ssm_branch_norm", "w_out", "norm_ffn", "w_ffn_in", "w_ffn_out"]


def _params(**kw):
    return pltpu.CompilerParams(vmem_limit_bytes=VMEM_LIMIT, **kw)


def _pick(n, cap, mult=16):
    best = None
    for d in range(mult, min(n, cap) + 1, mult):
        if n % d == 0:
            best = d
    assert best is not None, (n, cap, mult)
    return best


def _my_index():
    return 4 * lax.axis_index("x") + 2 * lax.axis_index("y") + lax.axis_index("c")


def _rms(x, g):
    r = lax.rsqrt(jnp.mean(x * x, axis=-1, keepdims=True) + EPS)
    return x * r * g


def _rms_bwd(x, g, dy):
    r = lax.rsqrt(jnp.mean(x * x, axis=-1, keepdims=True) + EPS)
    t = dy * g
    dx = r * t - x * (r * r * r) * jnp.mean(t * x, axis=-1, keepdims=True)
    dg = jnp.sum(dy * (x * r), axis=0, keepdims=True)
    return dx, dg


def _sigmoid(x):
    return 1.0 / (1.0 + jnp.exp(-x))


def _gelu(x):
    k = math.sqrt(2.0 / math.pi)
    return 0.5 * x * (1.0 + jnp.tanh(k * (x + 0.044715 * (x * x * x))))


def _gelu_grad(x):
    k = math.sqrt(2.0 / math.pi)
    t = jnp.tanh(k * (x + 0.044715 * (x * x * x)))
    return 0.5 * (1.0 + t) + 0.5 * x * (1.0 - t * t) * (k * (1.0 + 3.0 * 0.044715 * (x * x)))


def _head_mean(x, e_ref):
    hi = x.astype(BF16)
    r1 = x - hi.astype(F32)
    mid = r1.astype(BF16)
    lo = (r1 - mid.astype(F32)).astype(BF16)
    e = e_ref[...]
    out = []
    for b in range(x.shape[1] // 256):
        sl = slice(256 * b, 256 * b + 256)
        s = (jnp.dot(hi[:, sl], e, preferred_element_type=F32)
             + jnp.dot(mid[:, sl], e, preferred_element_type=F32)
             + jnp.dot(lo[:, sl], e, preferred_element_type=F32))
        out.append(s)
    s = out[0] if len(out) == 1 else jnp.concatenate(out, axis=1)
    return s * (1.0 / HEAD_DIM)


def _head_rms(x, g, e_ref):
    r = lax.rsqrt(_head_mean(x * x, e_ref) + EPS)
    return x * r * g


def _head_rms_bwd(x, g, dy, e_ref):
    r = lax.rsqrt(_head_mean(x * x, e_ref) + EPS)
    t = dy * g
    dx = r * t - x * (r * r * r) * _head_mean(t * x, e_ref)
    dg = jnp.sum(dy * (x * r), axis=0, keepdims=True)
    return dx, dg


def _lane_half(shape):
    lane = lax.broadcasted_iota(jnp.int32, shape, len(shape) - 1)
    return (lane >> 6) & 1


def _matmul(name, a, w, *, nt, tm, tn, tk, n=None, w_off=0, res=None, out_dtype=F32):
    m_dim, k_dim = a.shape
    n_dim = n if n is not None else (w.shape[0] if nt else w.shape[1])
    gm, gn, gk = m_dim // tm, n_dim // tn, k_dim // tk
    assert gm * tm == m_dim and gn * tn == n_dim and gk * tk == k_dim, (name, a.shape, w.shape, tm, tn, tk)
    dn = (((1,), (1,)), ((), ())) if nt else (((1,), (0,)), ((), ()))

    def body(*refs):
        if res is None:
            a_ref, w_ref, o_ref, acc = refs
            r_ref = None
        else:
            a_ref, w_ref, r_ref, o_ref, acc = refs
        k = pl.program_id(2)

        @pl.when(k == 0)
        def _():
            acc[...] = jnp.zeros_like(acc)

        acc[...] += lax.dot_general(a_ref[...], w_ref[...], dn, preferred_element_type=F32)

        @pl.when(k == gk - 1)
        def _():
            r = acc[...]
            if r_ref is not None:
                r = r_ref[...] + r
            o_ref[...] = r.astype(out_dtype)

    if nt:
        w_spec = pl.BlockSpec((tn, tk), lambda i, j, k: (j + w_off, k))
    else:
        w_spec = pl.BlockSpec((tk, tn), lambda i, j, k: (k, j))
    in_specs = [pl.BlockSpec((tm, tk), lambda i, j, k: (i, k)), w_spec]
    args = [a, w]
    if res is not None:
        in_specs.append(pl.BlockSpec((tm, tn), lambda i, j, k: (i, j)))
        args.append(res)
    return pl.pallas_call(
        body, name=name, grid=(gm, gn, gk),
        in_specs=in_specs,
        out_specs=pl.BlockSpec((tm, tn), lambda i, j, k: (i, j)),
        out_shape=jax.ShapeDtypeStruct((m_dim, n_dim), out_dtype),
        scratch_shapes=[pltpu.VMEM((tm, tn), F32)],
        compiler_params=_params(dimension_semantics=("parallel", "parallel", "arbitrary")),
    )(*args)


def _matmul_tn(name, a, b, *, tm, tn, tl):
    l_dim, m_dim = a.shape
    n_dim = b.shape[1]
    gm, gn, gl = m_dim // tm, n_dim // tn, l_dim // tl
    assert gm * tm == m_dim and gn * tn == n_dim and gl * tl == l_dim, (name, a.shape, b.shape, tm, tn, tl)

    def body(a_ref, b_ref, o_ref, ob_ref):
        @pl.when(pl.program_id(2) == 0)
        def _():
            o_ref[...] = jnp.zeros_like(o_ref)

        o_ref[...] += lax.dot_general(a_ref[...], b_ref[...], (((0,), (0,)), ((), ())),
                                      preferred_element_type=F32)

        @pl.when(pl.program_id(2) == gl - 1)
        def _():
            ob_ref[...] = o_ref[...].astype(BF16)

    out_spec = pl.BlockSpec((tm, tn), lambda i, j, l: (i, j))
    return pl.pallas_call(
        body, name=name, grid=(gm, gn, gl),
        in_specs=[pl.BlockSpec((tl, tm), lambda i, j, l: (l, i)),
                  pl.BlockSpec((tl, tn), lambda i, j, l: (l, j))],
        out_specs=[out_spec, out_spec],
        out_shape=[jax.ShapeDtypeStruct((m_dim, n_dim), F32), jax.ShapeDtypeStruct((m_dim, n_dim), BF16)],
        compiler_params=_params(dimension_semantics=("parallel", "parallel", "arbitrary")),
    )(a, b)


def _row_spec(tm, cols, f=None):
    if f is None:
        return pl.BlockSpec((tm, cols), lambda i: (i, 0))
    return pl.BlockSpec((tm, cols), lambda i: (f(i), 0))


def _full_spec(shape):
    nd = len(shape)
    return pl.BlockSpec(shape, lambda i: (0,) * nd)


def _embed_norm(x2d, g):
    s_len = x2d.shape[0]
    nb = s_len // BLK + 1

    def body(x_ref, g_ref, h_ref, xn_ref):
        h_ref[...] = x_ref[...]
        xn_ref[...] = _rms(x_ref[...], g_ref[...]).astype(BF16)

    return pl.pallas_call(
        body, name="embed_norm", grid=(nb - 1,),
        in_specs=[_row_spec(BLK, D_MODEL), _full_spec((1, D_MODEL))],
        out_specs=[_row_spec(BLK, D_MODEL, lambda i: i + 1), _row_spec(BLK, D_MODEL, lambda i: i + 1)],
        out_shape=[jax.ShapeDtypeStruct((nb * BLK, D_MODEL), F32),
                   jax.ShapeDtypeStruct((nb * BLK, D_MODEL), BF16)],
        compiler_params=_params(),
    )(x2d, g)


def _embed_meta(meta_pad, g, h0, xn):
    def body(mp_ref, g_ref, h_in, xn_in, h_ref, xn_ref):
        h_ref[...] = mp_ref[...]
        xn_ref[...] = _rms(mp_ref[...], g_ref[...]).astype(BF16)

    any_spec = pl.BlockSpec(memory_space=pl.ANY)
    return pl.pallas_call(
        body, name="embed_meta", grid=(1,),
        in_specs=[_full_spec((BLK, D_MODEL)), _full_spec((1, D_MODEL)), any_spec, any_spec],
        out_specs=[_row_spec(BLK, D_MODEL), _row_spec(BLK, D_MODEL)],
        out_shape=[jax.ShapeDtypeStruct(h0.shape, F32), jax.ShapeDtypeStruct(xn.shape, BF16)],
        input_output_aliases={2: 0, 3: 1},
        compiler_params=_params(),
    )(meta_pad, g, h0, xn)


def _qk_prep(qkv, q_norm_t, k_norm_t, e_mat, tm):
    l_dim = qkv.shape[0]

    def body(x_ref, qg_ref, kg_ref, e_ref, q_ref, kf_ref, vf_ref):
        x = x_ref[...]
        q = _head_rms(x[:, :Q_W], qg_ref[...], e_ref) * (HEAD_DIM ** -0.5)
        q_ref[...] = q.astype(BF16)
        k = _head_rms(x[:, Q_W:Q_W + KV_W], kg_ref[...], e_ref)
        v = x[:, Q_W + KV_W:Q_W + 2 * KV_W]
        half = _lane_half((tm, BLK))
        for src, dst in ((k, kf_ref), (v, vf_ref)):
            for kv in range(N_KV_HEADS):
                blk = src[:, BLK * (kv // 2):BLK * (kv // 2) + BLK]
                swapped = pltpu.roll(blk, HEAD_DIM, axis=1)
                for e in range(2):
                    val = blk if kv % 2 == e else swapped
                    idx = 2 * kv + e
                    dst[:, BLK * idx:BLK * idx + BLK] = jnp.where(half == e, val, 0.0).astype(BF16)

    return pl.pallas_call(
        body, name="qk_prep", grid=(l_dim // tm,),
        in_specs=[_row_spec(tm, Q_W + 2 * KV_W), _full_spec((1, Q_W)), _full_spec((1, KV_W)),
                  _full_spec((256, 256))],
        out_specs=[_row_spec(tm, Q_W), _row_spec(tm, 8 * BLK), _row_spec(tm, 8 * BLK)],
        out_shape=[jax.ShapeDtypeStruct((l_dim, Q_W), BF16),
                   jax.ShapeDtypeStruct((l_dim, 8 * BLK), BF16),
                   jax.ShapeDtypeStruct((l_dim, 8 * BLK), BF16)],
        compiler_params=_params(),
    )(qkv, q_norm_t, k_norm_t, e_mat)


def _merge_fwd(attn, zab, gates, abn, sbn, tm):
    l_dim = attn.shape[0]

    def body(a_ref, z_ref, g_ref, an_ref, sn_ref, o_ref):
        z = z_ref[...]
        g = g_ref[...]
        ssm = z[:, :D_MODEL] * _sigmoid(z[:, D_MODEL:])
        merged = (_sigmoid(g[:, :D_MODEL]) * _rms(a_ref[...], an_ref[...])
                  + _sigmoid(g[:, D_MODEL:]) * _rms(ssm, sn_ref[...]))
        o_ref[...] = merged.astype(BF16)

    return pl.pallas_call(
        body, name="merge_fwd", grid=(l_dim // tm,),
        in_specs=[_row_spec(tm, D_MODEL), _row_spec(tm, 2 * D_MODEL), _row_spec(tm, 2 * D_MODEL),
                  _full_spec((1, D_MODEL)), _full_spec((1, D_MODEL))],
        out_specs=_row_spec(tm, D_MODEL),
        out_shape=jax.ShapeDtypeStruct((l_dim, D_MODEL), BF16),
        compiler_params=_params(),
    )(attn, zab, gates, abn, sbn)


def _merge_bwd(attn, zab, gates, abn, sbn, dmerged, tm):
    l_dim = attn.shape[0]

    def body(a_ref, z_ref, g_ref, an_ref, sn_ref, dm_ref, da_ref, dz_ref, dg_ref, dan_ref, dsn_ref):
        @pl.when(pl.program_id(0) == 0)
        def _():
            dan_ref[...] = jnp.zeros_like(dan_ref)
            dsn_ref[...] = jnp.zeros_like(dsn_ref)

        z = z_ref[...]
        g = g_ref[...]
        dm = dm_ref[...]
        attn_v = a_ref[...]
        za, zb = z[:, :D_MODEL], z[:, D_MODEL:]
        sb = _sigmoid(zb)
        ssm = za * sb
        s_ga, s_gs = _sigmoid(g[:, :D_MODEL]), _sigmoid(g[:, D_MODEL:])
        a_n = _rms(attn_v, an_ref[...])
        s_n = _rms(ssm, sn_ref[...])
        dg_ref[:, :D_MODEL] = (dm * a_n * s_ga * (1.0 - s_ga)).astype(BF16)
        dg_ref[:, D_MODEL:] = (dm * s_n * s_gs * (1.0 - s_gs)).astype(BF16)
        dattn, dan = _rms_bwd(attn_v, an_ref[...], dm * s_ga)
        dssm, dsn = _rms_bwd(ssm, sn_ref[...], dm * s_gs)
        da_ref[...] = dattn
        dz_ref[:, :D_MODEL] = (dssm * sb).astype(BF16)
        dz_ref[:, D_MODEL:] = (dssm * za * sb * (1.0 - sb)).astype(BF16)
        dan_ref[...] += dan
        dsn_ref[...] += dsn

    return pl.pallas_call(
        body, name="merge_bwd", grid=(l_dim // tm,),
        in_specs=[_row_spec(tm, D_MODEL), _row_spec(tm, 2 * D_MODEL), _row_spec(tm, 2 * D_MODEL),
                  _full_spec((1, D_MODEL)), _full_spec((1, D_MODEL)), _row_spec(tm, D_MODEL)],
        out_specs=[_row_spec(tm, D_MODEL), _row_spec(tm, 2 * D_MODEL), _row_spec(tm, 2 * D_MODEL),
                   _full_spec((1, D_MODEL)), _full_spec((1, D_MODEL))],
        out_shape=[jax.ShapeDtypeStruct((l_dim, D_MODEL), F32),
                   jax.ShapeDtypeStruct((l_dim, 2 * D_MODEL), BF16),
                   jax.ShapeDtypeStruct((l_dim, 2 * D_MODEL), BF16),
                   jax.ShapeDtypeStruct((1, D_MODEL), F32), jax.ShapeDtypeStruct((1, D_MODEL), F32)],
        compiler_params=_params(),
    )(attn, zab, gates, abn, sbn, dmerged)


def _norm_cast(name, h, g, tm):
    l_dim = h.shape[0]

    def body(h_ref, g_ref, o_ref):
        o_ref[...] = _rms(h_ref[...], g_ref[...]).astype(BF16)

    return pl.pallas_call(
        body, name=name, grid=(l_dim // tm,),
        in_specs=[_row_spec(tm, D_MODEL), _full_spec((1, D_MODEL))],
        out_specs=_row_spec(tm, D_MODEL),
        out_shape=jax.ShapeDtypeStruct((l_dim, D_MODEL), BF16),
        compiler_params=_params(),
    )(h, g)


FF_TILE = D_FF // 2


def _ffn_in_swiglu(hn, wt_ffn_in, tm):
    l_dim = hn.shape[0]
    nt = (((1,), (1,)), ((), ()))

    def body(a_ref, w_ref, gu_ref, act_ref):
        r = lax.dot_general(a_ref[...], w_ref[...], nt, preferred_element_type=F32)
        gate, up = r[:, :FF_TILE], r[:, FF_TILE:]
        gu_ref[...] = r.astype(BF16)
        act_ref[...] = (gate * _sigmoid(gate) * up).astype(BF16)

    return pl.pallas_call(
        body, name="ffn_in_swiglu", grid=(l_dim // tm, 2),
        in_specs=[pl.BlockSpec((tm, D_MODEL), lambda i, j: (i, 0)),
                  pl.BlockSpec((2 * FF_TILE, D_MODEL), lambda i, j: (j, 0))],
        out_specs=[pl.BlockSpec((tm, 2 * FF_TILE), lambda i, j: (i, j)),
                   pl.BlockSpec((tm, FF_TILE), lambda i, j: (i, j))],
        out_shape=[jax.ShapeDtypeStruct((l_dim, 2 * D_FF), BF16), jax.ShapeDtypeStruct((l_dim, D_FF), BF16)],
        compiler_params=_params(dimension_semantics=("parallel", "parallel")),
    )(hn, wt_ffn_in)


def _d_act_swiglu(dh2_b, w_ffn_out, gu, tm):
    l_dim = dh2_b.shape[0]
    nt = (((1,), (1,)), ((), ()))

    def body(d_ref, w_ref, gu_ref, o_ref):
        d = lax.dot_general(d_ref[...], w_ref[...], nt, preferred_element_type=F32)
        gate = gu_ref[:, :FF_TILE].astype(F32)
        up = gu_ref[:, FF_TILE:].astype(F32)
        s = _sigmoid(gate)
        o_ref[:, :FF_TILE] = (d * up * (s * (1.0 + gate * (1.0 - s)))).astype(BF16)
        o_ref[:, FF_TILE:] = (d * (gate * s)).astype(BF16)

    return pl.pallas_call(
        body, name="d_act_swiglu", grid=(l_dim // tm, 2),
        in_specs=[pl.BlockSpec((tm, D_MODEL), lambda i, j: (i, 0)),
                  pl.BlockSpec((FF_TILE, D_MODEL), lambda i, j: (j, 0)),
                  pl.BlockSpec((tm, 2 * FF_TILE), lambda i, j: (i, j))],
        out_specs=pl.BlockSpec((tm, 2 * FF_TILE), lambda i, j: (i, j)),
        out_shape=jax.ShapeDtypeStruct((l_dim, 2 * D_FF), BF16),
        compiler_params=_params(dimension_semantics=("parallel", "parallel")),
    )(dh2_b, w_ffn_out, gu)


def _loss_grad(h2, target2d):
    l_dim = h2.shape[0]
    nb = l_dim // BLK

    def body(h_ref, t_ref, d_ref, db_ref, loss_ref):
        i = pl.program_id(0)

        @pl.when(i == 0)
        def _():
            loss_ref[...] = jnp.zeros_like(loss_ref)
            d_ref[...] = jnp.zeros_like(d_ref)
            db_ref[...] = jnp.zeros_like(db_ref)

        @pl.when(i > 0)
        def _():
            err = h_ref[...] - t_ref[...]
            d = err * (1.0 / D_MODEL)
            d_ref[...] = d
            db_ref[...] = d.astype(BF16)
            loss_ref[...] += 0.5 * jnp.sum(jnp.mean(err * err, axis=-1, keepdims=True), axis=0, keepdims=True)

    return pl.pallas_call(
        body, name="loss_grad", grid=(nb,),
        in_specs=[_row_spec(BLK, D_MODEL), _row_spec(BLK, D_MODEL, lambda i: jnp.maximum(i - 1, 0))],
        out_specs=[_row_spec(BLK, D_MODEL), _row_spec(BLK, D_MODEL), _full_spec((1, 1))],
        out_shape=[jax.ShapeDtypeStruct((l_dim, D_MODEL), F32), jax.ShapeDtypeStruct((l_dim, D_MODEL), BF16),
                   jax.ShapeDtypeStruct((1, 1), F32)],
        compiler_params=_params(),
    )(h2, target2d)


def _norm_bwd_res(name, h, g, dy, dres, tm):
    l_dim = h.shape[0]

    def body(h_ref, g_ref, dy_ref, dr_ref, o_ref, ob_ref, dg_ref):
        @pl.when(pl.program_id(0) == 0)
        def _():
            dg_ref[...] = jnp.zeros_like(dg_ref)

        dx, dg = _rms_bwd(h_ref[...], g_ref[...], dy_ref[...])
        out = dr_ref[...] + dx
        o_ref[...] = out
        ob_ref[...] = out.astype(BF16)
        dg_ref[...] += dg

    return pl.pallas_call(
        body, name=name, grid=(l_dim // tm,),
        in_specs=[_row_spec(tm, D_MODEL), _full_spec((1, D_MODEL)), _row_spec(tm, D_MODEL), _row_spec(tm, D_MODEL)],
        out_specs=[_row_spec(tm, D_MODEL), _row_spec(tm, D_MODEL), _full_spec((1, D_MODEL))],
        out_shape=[jax.ShapeDtypeStruct((l_dim, D_MODEL), F32), jax.ShapeDtypeStruct((l_dim, D_MODEL), BF16),
                   jax.ShapeDtypeStruct((1, D_MODEL), F32)],
        compiler_params=_params(),
    )(h, g, dy, dres)


def _final_bwd(h0, g, dxn, dh1):
    l_dim = h0.shape[0]
    nb = l_dim // BLK

    def body(h_ref, g_ref, dy_ref, dr_ref, gx_ref, gm_ref, dg_ref):
        i = pl.program_id(0)

        @pl.when(i == 0)
        def _():
            dg_ref[...] = jnp.zeros_like(dg_ref)

        dx, dg = _rms_bwd(h_ref[...], g_ref[...], dy_ref[...])
        out = dr_ref[...] + dx
        dg_ref[...] += dg

        @pl.when(i == 0)
        def _():
            gm_ref[...] = out

        @pl.when(i > 0)
        def _():
            gx_ref[...] = out

    return pl.pallas_call(
        body, name="final_bwd", grid=(nb,),
        in_specs=[_row_spec(BLK, D_MODEL), _full_spec((1, D_MODEL)), _row_spec(BLK, D_MODEL),
                  _row_spec(BLK, D_MODEL)],
        out_specs=[_row_spec(BLK, D_MODEL, lambda i: jnp.maximum(i - 1, 0)), _full_spec((BLK, D_MODEL)),
                   _full_spec((1, D_MODEL))],
        out_shape=[jax.ShapeDtypeStruct((l_dim - BLK, D_MODEL), F32), jax.ShapeDtypeStruct((BLK, D_MODEL), F32),
                   jax.ShapeDtypeStruct((1, D_MODEL), F32)],
        compiler_params=_params(),
    )(h0, g, dxn, dh1)


def _attn_valid(n):
    shape = (2 * BLK, 3 * BLK)
    qi = lax.broadcasted_iota(jnp.int32, shape, 0) & (BLK - 1)
    col = lax.broadcasted_iota(jnp.int32, shape, 1)
    kj = col & (BLK - 1)
    part = col >> 7
    nn = jnp.zeros(shape, jnp.int32) + n
    meta_ok = (part == 0) & (kj >= PAD) & (nn >= 1)
    prev_ok = (part == 1) & (kj > qi) & (nn >= 2)
    cur_ok = (part == 2) & (kj <= qi) & ((nn >= 1) | (kj >= PAD))
    return meta_ok | prev_ok | cur_ok


def _attn_probs(q_ref, kwin, sk_ref, valid, kv, e):
    qs = jnp.concatenate([q_ref[:, BLK * (2 * kv):BLK * (2 * kv) + BLK],
                          q_ref[:, BLK * (2 * kv + 1):BLK * (2 * kv + 1) + BLK]], axis=0)
    s = lax.dot_general(qs, kwin, (((1,), (1,)), ((), ())), preferred_element_type=F32)
    s = jnp.where(valid, s, NEG)
    h0 = 4 * kv + e
    row = lax.broadcasted_iota(jnp.int32, (2 * BLK, 1), 0)
    sink = jnp.where(row < BLK, sk_ref[:, h0:h0 + 1], sk_ref[:, h0 + 2:h0 + 3])
    m = jnp.maximum(jnp.max(s, axis=-1, keepdims=True), sink)
    ex = jnp.exp(s - m)
    es = jnp.exp(sink - m)
    inv = 1.0 / (jnp.sum(ex, axis=-1, keepdims=True) + es)
    return qs, ex * inv, es * inv


def _attn_specs(nb):
    prev = lambda i: jnp.maximum(i - 1, 0)
    zero = lambda i: 0
    kv_specs = [_row_spec(BLK, 8 * BLK, zero), _row_spec(BLK, 8 * BLK, prev), _row_spec(BLK, 8 * BLK)]
    return kv_specs


def _attn_fwd(qn, kf, vf, sinks):
    l_dim = qn.shape[0]
    nb = l_dim // BLK

    def body(q_ref, km_ref, kp_ref, kc_ref, vm_ref, vp_ref, vc_ref, sk_ref, o_ref):
        valid = _attn_valid(pl.program_id(0))
        for kv in range(N_KV_HEADS):
            outs = []
            for e in range(2):
                sl = slice(BLK * (2 * kv + e), BLK * (2 * kv + e) + BLK)
                kwin = jnp.concatenate([km_ref[:, sl], kp_ref[:, sl], kc_ref[:, sl]], axis=0)
                vwin = jnp.concatenate([vm_ref[:, sl], vp_ref[:, sl], vc_ref[:, sl]], axis=0)
                _, p, _ = _attn_probs(q_ref, kwin, sk_ref, valid, kv, e)
                outs.append(jnp.dot(p.astype(BF16), vwin, preferred_element_type=F32))
            o = outs[0] + outs[1]
            o_ref[:, BLK * (2 * kv):BLK * (2 * kv) + BLK] = o[:BLK]
            o_ref[:, BLK * (2 * kv + 1):BLK * (2 * kv + 1) + BLK] = o[BLK:]

    kv_specs = _attn_specs(nb)
    return pl.pallas_call(
        body, name="attn_fwd", grid=(nb,),
        in_specs=[_row_spec(BLK, Q_W)] + kv_specs + kv_specs + [_full_spec((1, N_Q_HEADS))],
        out_specs=_row_spec(BLK, Q_W),
        out_shape=jax.ShapeDtypeStruct((l_dim, Q_W), F32),
        compiler_params=_params(),
    )(qn, kf, kf, kf, vf, vf, vf, sinks)


def _attn_bwd(qn, kf, vf, sinks, attn, dattn):
    l_dim = qn.shape[0]
    nb = l_dim // BLK
    wide = 8 * BLK

    def body(q_ref, km_ref, kp_ref, kc_ref, vm_ref, vp_ref, vc_ref, sk_ref, o_ref, do_ref,
             dq_ref, dkc_ref, dkp_ref, dkm_ref, dvc_ref, dvp_ref, dvm_ref, dsk_ref):
        @pl.when(pl.program_id(0) == 0)
        def _():
            dkm_ref[...] = jnp.zeros_like(dkm_ref)
            dvm_ref[...] = jnp.zeros_like(dvm_ref)
            dsk_ref[...] = jnp.zeros_like(dsk_ref)

        valid = _attn_valid(pl.program_id(0))
        half = _lane_half((BLK, BLK))
        lane16 = lax.broadcasted_iota(jnp.int32, (1, N_Q_HEADS), 1)
        dsk = jnp.zeros((1, N_Q_HEADS), F32)
        for kv in range(N_KV_HEADS):
            j0, j1 = 2 * kv, 2 * kv + 1
            do0 = do_ref[:, BLK * j0:BLK * j0 + BLK]
            do1 = do_ref[:, BLK * j1:BLK * j1 + BLK]
            prod0 = do0 * o_ref[:, BLK * j0:BLK * j0 + BLK]
            prod1 = do1 * o_ref[:, BLK * j1:BLK * j1 + BLK]
            dos = jnp.concatenate([do0, do1], axis=0).astype(BF16)
            dqs = []
            for e in range(2):
                sl = slice(BLK * (2 * kv + e), BLK * (2 * kv + e) + BLK)
                kwin = jnp.concatenate([km_ref[:, sl], kp_ref[:, sl], kc_ref[:, sl]], axis=0)
                vwin = jnp.concatenate([vm_ref[:, sl], vp_ref[:, sl], vc_ref[:, sl]], axis=0)
                qs, p, p_sink = _attn_probs(q_ref, kwin, sk_ref, valid, kv, e)
                delta = jnp.concatenate(
                    [jnp.sum(jnp.where(half == e, prod0, 0.0), axis=-1, keepdims=True),
                     jnp.sum(jnp.where(half == e, prod1, 0.0), axis=-1, keepdims=True)], axis=0)
                dp = lax.dot_general(dos, vwin, (((1,), (1,)), ((), ())), preferred_element_type=F32)
                ds = (p * (dp - delta)).astype(BF16)
                pb = p.astype(BF16)
                dqs.append(jnp.dot(ds, kwin, preferred_element_type=F32))
                dk = lax.dot_general(ds, qs, (((0,), (0,)), ((), ())), preferred_element_type=F32)
                dv = lax.dot_general(pb, dos, (((0,), (0,)), ((), ())), preferred_element_type=F32)
                dkm_ref[:, sl] += dk[:BLK]
                dkp_ref[:, sl] = dk[BLK:2 * BLK]
                dkc_ref[:, sl] = dk[2 * BLK:]
                dvm_ref[:, sl] += dv[:BLK]
                dvp_ref[:, sl] = dv[BLK:2 * BLK]
                dvc_ref[:, sl] = dv[2 * BLK:]
                sink_g = -(p_sink * delta)
                g_lo = jnp.sum(sink_g[:BLK], axis=0, keepdims=True)
                g_hi = jnp.sum(sink_g[BLK:], axis=0, keepdims=True)
                dsk = dsk + jnp.where(lane16 == 4 * kv + e, g_lo, 0.0) + jnp.where(lane16 == 4 * kv + 2 + e, g_hi, 0.0)
            dq = jnp.where(jnp.concatenate([half, half], axis=0) == 0, dqs[0], dqs[1])
            dq_ref[:, BLK * j0:BLK * j0 + BLK] = dq[:BLK]
            dq_ref[:, BLK * j1:BLK * j1 + BLK] = dq[BLK:]
        dsk_ref[...] += dsk

    kv_specs = _attn_specs(nb)
    row_wide = _row_spec(BLK, wide)
    acc_wide = _full_spec((BLK, wide))
    big = jax.ShapeDtypeStruct((l_dim, wide), F32)
    return pl.pallas_call(
        body, name="attn_bwd", grid=(nb,),
        in_specs=[_row_spec(BLK, Q_W)] + kv_specs + kv_specs
        + [_full_spec((1, N_Q_HEADS)), _row_spec(BLK, Q_W), _row_spec(BLK, Q_W)],
        out_specs=[_row_spec(BLK, Q_W), row_wide, row_wide, acc_wide, row_wide, row_wide, acc_wide,
                   _full_spec((1, N_Q_HEADS))],
        out_shape=[jax.ShapeDtypeStruct((l_dim, Q_W), F32), big, big, jax.ShapeDtypeStruct((BLK, wide), F32),
                   big, big, jax.ShapeDtypeStruct((BLK, wide), F32), jax.ShapeDtypeStruct((1, N_Q_HEADS), F32)],
        compiler_params=_params(),
    )(qn, kf, kf, kf, vf, vf, vf, sinks, attn, dattn)


def _qk_bwd(qkv, q_norm_t, k_norm_t, e_mat, dq, dkc, dkp, dkm, dvc, dvp, dvm):
    l_dim = qkv.shape[0]
    nb = l_dim // BLK
    wide = 8 * BLK

    def fold(x):
        half = _lane_half((BLK, BLK))
        blocks = []
        for kb in range(2):
            t = []
            for kv in (2 * kb, 2 * kb + 1):
                own = kv % 2
                a = x[:, BLK * (2 * kv + own):BLK * (2 * kv + own) + BLK]
                b = pltpu.roll(x[:, BLK * (2 * kv + 1 - own):BLK * (2 * kv + 1 - own) + BLK], HEAD_DIM, axis=1)
                t.append(a + b)
            blocks.append(jnp.where(half == 0, t[0], t[1]))
        return jnp.concatenate(blocks, axis=1)

    def body(x_ref, qg_ref, kg_ref, e_ref, dq_ref, dkc_ref, dkp_ref, dkm_ref, dvc_ref, dvp_ref, dvm_ref,
             o_ref, dqg_ref, dkg_ref):
        i = pl.program_id(0)

        @pl.when(i == 0)
        def _():
            dqg_ref[...] = jnp.zeros_like(dqg_ref)
            dkg_ref[...] = jnp.zeros_like(dkg_ref)

        first = jnp.where(i == 0, 1.0, 0.0)
        not_last = jnp.where(i < nb - 1, 1.0, 0.0)
        dk_x = dkc_ref[...] + not_last * dkp_ref[...] + first * dkm_ref[...]
        dv_x = dvc_ref[...] + not_last * dvp_ref[...] + first * dvm_ref[...]
        x = x_ref[...]
        dqx, dqg = _head_rms_bwd(x[:, :Q_W], qg_ref[...], dq_ref[...] * (HEAD_DIM ** -0.5), e_ref)
        dkx, dkg = _head_rms_bwd(x[:, Q_W:Q_W + KV_W], kg_ref[...], fold(dk_x), e_ref)
        o_ref[:, :Q_W] = dqx.astype(BF16)
        o_ref[:, Q_W:Q_W + KV_W] = dkx.astype(BF16)
        o_ref[:, Q_W + KV_W:] = fold(dv_x).astype(BF16)
        dqg_ref[...] += dqg
        dkg_ref[...] += dkg

    nxt = lambda i: jnp.minimum(i + 1, nb - 1)
    row_wide = _row_spec(BLK, wide)
    nxt_wide = _row_spec(BLK, wide, nxt)
    acc_wide = _full_spec((BLK, wide))
    return pl.pallas_call(
        body, name="qk_bwd", grid=(nb,),
        in_specs=[_row_spec(BLK, Q_W + 2 * KV_W), _full_spec((1, Q_W)), _full_spec((1, KV_W)),
                  _full_spec((256, 256)), _row_spec(BLK, Q_W),
                  row_wide, nxt_wide, acc_wide, row_wide, nxt_wide, acc_wide],
        out_specs=[_row_spec(BLK, Q_W + 2 * KV_W), _full_spec((1, Q_W)), _full_spec((1, KV_W))],
        out_shape=[jax.ShapeDtypeStruct((l_dim, Q_W + 2 * KV_W), BF16),
                   jax.ShapeDtypeStruct((1, Q_W), F32), jax.ShapeDtypeStruct((1, KV_W), F32)],
        compiler_params=_params(),
    )(qkv, q_norm_t, k_norm_t, e_mat, dq, dkc, dkp, dkm, dvc, dvp, dvm)


GRP = 8


def _strided(r, g):
    return pl.ds(r, g, stride=GRP)


def _slab(ref, base, r, g):
    return jnp.concatenate([ref[base + i, _strided(r, g), :] for i in range(LB_KB)], axis=1)


def _slab_store(ref, base, r, g, val):
    for i in range(LB_KB):
        ref[base + i, _strided(r, g), :] = val[:, BLK * i:BLK * i + BLK]


def _group_totals(xr_ref, xi_ref, base, ar, ai, reverse):
    g = xr_ref.shape[1] // GRP
    sr = si = None
    for r in (range(GRP - 1, -1, -1) if reverse else range(GRP)):
        xr, xi = _slab(xr_ref, base, r, g), _slab(xi_ref, base, r, g)
        if sr is not None:
            xr, xi = xr + ar * sr - ai * si, xi + ar * si + ai * sr
        sr, si = xr, xi
    return sr, si


def _carry_scan(tr, ti, sqr_ref, sqi_ref, lanes, sign, reverse):
    g = tr.shape[0]
    row = lax.broadcasted_iota(jnp.int32, tr.shape, 0)
    idx, s = 0, 1
    while s < g:
        ar = sqr_ref[idx:idx + 1, lanes]
        ai = sign * sqi_ref[idx:idx + 1, lanes]
        shift, keep = (g - s, row < g - s) if reverse else (s, row >= s)
        pr = jnp.where(keep, pltpu.roll(tr, shift, axis=0), 0.0)
        pi = jnp.where(keep, pltpu.roll(ti, shift, axis=0), 0.0)
        tr, ti = tr + ar * pr - ai * pi, ti + ar * pi + ai * pr
        idx, s = idx + 1, 2 * s
    return tr, ti


def _ssm_fwd(u, wb_re, wb_im, wc_re, wc_im, d_skip, tabs):
    l_dim = u.shape[0]
    nb = l_dim // BLK
    g = BLK // GRP

    def body(u_ref, wbr_ref, wbi_ref, wcr_ref, wci_ref, d_ref, a1r_ref, a1i_ref, sqr_ref, sqi_ref,
             seqr_ref, seqi_ref, y_ref, z_ref, sr_ref, si_ref, cr_ref, ci_ref, xr_ref, xi_ref):
        @pl.when(pl.program_id(0) == 0)
        def _():
            cr_ref[...] = jnp.zeros_like(cr_ref)
            ci_ref[...] = jnp.zeros_like(ci_ref)

        row = lax.broadcasted_iota(jnp.int32, (g, ST_KB), 0)
        for kb in range(SSM_KB):
            ch = slice(BLK * kb, BLK * kb + BLK)
            lanes = slice(ST_KB * kb, ST_KB * kb + ST_KB)
            u_kb = u_ref[:, ch]
            ub = u_kb.astype(BF16)
            xr = jnp.dot(ub, wbr_ref[kb], preferred_element_type=F32)
            xi = jnp.dot(ub, wbi_ref[kb], preferred_element_type=F32)
            base = LB_KB * kb
            for i in range(LB_KB):
                xr_ref[base + i] = xr[:, BLK * i:BLK * i + BLK]
                xi_ref[base + i] = xi[:, BLK * i:BLK * i + BLK]
            ar, ai = a1r_ref[0:1, lanes], a1i_ref[0:1, lanes]
            tr, ti = _group_totals(xr_ref, xi_ref, base, ar, ai, reverse=False)
            tr, ti = _carry_scan(tr, ti, sqr_ref, sqi_ref, lanes, 1.0, reverse=False)
            cin_r, cin_i = cr_ref[0:1, lanes], ci_ref[0:1, lanes]
            qr, qi = seqr_ref[:, lanes], seqi_ref[:, lanes]
            tr, ti = tr + qr * cin_r - qi * cin_i, ti + qr * cin_i + qi * cin_r
            cr_ref[0:1, lanes] = jnp.sum(jnp.where(row == g - 1, tr, 0.0), axis=0, keepdims=True)
            ci_ref[0:1, lanes] = jnp.sum(jnp.where(row == g - 1, ti, 0.0), axis=0, keepdims=True)
            pr = jnp.where(row == 0, cin_r, pltpu.roll(tr, 1, axis=0))
            pi = jnp.where(row == 0, cin_i, pltpu.roll(ti, 1, axis=0))
            for r in range(GRP):
                pr, pi = (_slab(xr_ref, base, r, g) + ar * pr - ai * pi,
                          _slab(xi_ref, base, r, g) + ar * pi + ai * pr)
                _slab_store(sr_ref, base, r, g, pr)
                _slab_store(si_ref, base, r, g, pi)
            s_r = jnp.concatenate([sr_ref[LB_KB * kb + i] for i in range(LB_KB)], axis=1)
            s_i = jnp.concatenate([si_ref[LB_KB * kb + i] for i in range(LB_KB)], axis=1)
            y = (jnp.dot(s_r.astype(BF16), wcr_ref[kb], preferred_element_type=F32)
                 - jnp.dot(s_i.astype(BF16), wci_ref[kb], preferred_element_type=F32)
                 + d_ref[:, ch] * u_kb)
            y_ref[:, ch] = y
            z_ref[:, ch] = _gelu(y).astype(BF16)

    wb_spec = _full_spec((SSM_KB, BLK, ST_KB))
    wc_spec = _full_spec((SSM_KB, ST_KB, BLK))
    tab_specs = [_full_spec(t.shape) for t in tabs[:6]]
    state_spec = pl.BlockSpec((N_LB, BLK, BLK), lambda i: (0, i, 0))
    state_shape = jax.ShapeDtypeStruct((N_LB, l_dim, BLK), F32)
    return pl.pallas_call(
        body, name="ssm_fwd", grid=(nb,),
        in_specs=[_row_spec(BLK, D_MODEL), wb_spec, wb_spec, wc_spec, wc_spec, _full_spec((1, D_MODEL))]
        + tab_specs,
        out_specs=[_row_spec(BLK, D_MODEL), _row_spec(BLK, D_MODEL), state_spec, state_spec],
        out_shape=[jax.ShapeDtypeStruct((l_dim, D_MODEL), F32), jax.ShapeDtypeStruct((l_dim, D_MODEL), BF16),
                   state_shape, state_shape],
        scratch_shapes=[pltpu.VMEM((8, N_STATE), F32), pltpu.VMEM((8, N_STATE), F32),
                        pltpu.VMEM((N_LB, BLK, BLK), F32), pltpu.VMEM((N_LB, BLK, BLK), F32)],
        compiler_params=_params(),
    )(u, wb_re, wb_im, wc_re, wc_im, d_skip, *tabs[:6])


def _ssm_bwd(dz, y, u, s_re, s_im, wb_re, wb_im, wc_re, wc_im, d_skip, tabs):
    l_dim = u.shape[0]
    nb = l_dim // BLK
    g = BLK // GRP

    def body(dz_ref, y_ref, u_ref, sr_ref, si_ref, wbr_ref, wbi_ref, wcr_ref, wci_ref, d_ref,
             a1r_ref, a1i_ref, sqr_ref, sqi_ref, revr_ref, revi_ref,
             du_ref, dd_ref, dar_ref, dai_ref, dwbr_ref, dwbi_ref, dwcr_ref, dwci_ref,
             cr_ref, ci_ref, gr_ref, gi_ref):
        @pl.when(pl.program_id(0) == 0)
        def _():
            for r in (cr_ref, ci_ref, dd_ref, dar_ref, dai_ref, dwbr_ref, dwbi_ref, dwcr_ref, dwci_ref):
                r[...] = jnp.zeros_like(r)

        tn = (((0,), (0,)), ((), ()))
        nt = (((1,), (1,)), ((), ()))
        row = lax.broadcasted_iota(jnp.int32, (g, ST_KB), 0)
        for kb in range(SSM_KB):
            ch = slice(BLK * kb, BLK * kb + BLK)
            lanes = slice(ST_KB * kb, ST_KB * kb + ST_KB)
            u_kb = u_ref[:, ch]
            dy = dz_ref[:, ch] * _gelu_grad(y_ref[:, ch])
            dyb = dy.astype(BF16)
            ub = u_kb.astype(BF16)
            dd_ref[:, ch] += jnp.sum(dy * u_kb, axis=0, keepdims=True)
            ds_r = lax.dot_general(dyb, wcr_ref[kb], nt, preferred_element_type=F32)
            ds_i = -lax.dot_general(dyb, wci_ref[kb], nt, preferred_element_type=F32)
            base = LB_KB * kb
            for i in range(LB_KB):
                gr_ref[base + i] = ds_r[:, BLK * i:BLK * i + BLK]
                gi_ref[base + i] = ds_i[:, BLK * i:BLK * i + BLK]
            ar, ai = a1r_ref[0:1, lanes], -a1i_ref[0:1, lanes]
            tr, ti = _group_totals(gr_ref, gi_ref, base, ar, ai, reverse=True)
            tr, ti = _carry_scan(tr, ti, sqr_ref, sqi_ref, lanes, -1.0, reverse=True)
            cin_r, cin_i = cr_ref[0:1, lanes], ci_ref[0:1, lanes]
            qr, qi = revr_ref[:, lanes], -revi_ref[:, lanes]
            tr, ti = tr + qr * cin_r - qi * cin_i, ti + qr * cin_i + qi * cin_r
            cr_ref[0:1, lanes] = jnp.sum(jnp.where(row == 0, tr, 0.0), axis=0, keepdims=True)
            ci_ref[0:1, lanes] = jnp.sum(jnp.where(row == 0, ti, 0.0), axis=0, keepdims=True)
            nr = jnp.where(row == g - 1, cin_r, pltpu.roll(tr, g - 1, axis=0))
            ni = jnp.where(row == g - 1, cin_i, pltpu.roll(ti, g - 1, axis=0))
            acc_r = jnp.zeros((g, ST_KB), F32)
            acc_i = jnp.zeros((g, ST_KB), F32)
            for r in range(GRP - 1, -1, -1):
                s_r, s_i = _slab(sr_ref, base, r, g), _slab(si_ref, base, r, g)
                acc_r = acc_r + (nr * s_r + ni * s_i)
                acc_i = acc_i + (ni * s_r - nr * s_i)
                nr, ni = (_slab(gr_ref, base, r, g) + ar * nr - ai * ni,
                          _slab(gi_ref, base, r, g) + ar * ni + ai * nr)
                _slab_store(gr_ref, base, r, g, nr)
                _slab_store(gi_ref, base, r, g, ni)
            dar_ref[:, lanes] += jnp.sum(acc_r, axis=0, keepdims=True)
            dai_ref[:, lanes] += jnp.sum(acc_i, axis=0, keepdims=True)
            grb = jnp.concatenate([gr_ref[base + i] for i in range(LB_KB)], axis=1).astype(BF16)
            gib = jnp.concatenate([gi_ref[base + i] for i in range(LB_KB)], axis=1).astype(BF16)
            srb = jnp.concatenate([sr_ref[base + i] for i in range(LB_KB)], axis=1).astype(BF16)
            sib = jnp.concatenate([si_ref[base + i] for i in range(LB_KB)], axis=1).astype(BF16)
            du = (lax.dot_general(grb, wbr_ref[kb], nt, preferred_element_type=F32)
                  + lax.dot_general(gib, wbi_ref[kb], nt, preferred_element_type=F32)
                  + d_ref[:, ch] * dy)
            du_ref[:, ch] = du.astype(BF16)
            dwbr_ref[kb] += lax.dot_general(ub, grb, tn, preferred_element_type=F32)
            dwbi_ref[kb] += lax.dot_general(ub, gib, tn, preferred_element_type=F32)
            dwcr_ref[kb] += lax.dot_general(srb, dyb, tn, preferred_element_type=F32)
            dwci_ref[kb] -= lax.dot_general(sib, dyb, tn, preferred_element_type=F32)

    rev = lambda i: nb - 1 - i
    wb_spec = _full_spec((SSM_KB, BLK, ST_KB))
    wc_spec = _full_spec((SSM_KB, ST_KB, BLK))
    tab_in = [tabs[0], tabs[1], tabs[2], tabs[3], tabs[6], tabs[7]]
    tab_specs = [_full_spec(t.shape) for t in tab_in]
    vec = _full_spec((1, D_MODEL))
    svec = _full_spec((1, N_STATE))
    state_spec = pl.BlockSpec((N_LB, BLK, BLK), lambda i: (0, nb - 1 - i, 0))
    return pl.pallas_call(
        body, name="ssm_bwd", grid=(nb,),
        in_specs=[_row_spec(BLK, D_MODEL, rev), _row_spec(BLK, D_MODEL, rev), _row_spec(BLK, D_MODEL, rev),
                  state_spec, state_spec,
                  wb_spec, wb_spec, wc_spec, wc_spec, vec] + tab_specs,
        out_specs=[_row_spec(BLK, D_MODEL, rev), vec, svec, svec, wb_spec, wb_spec, wc_spec, wc_spec],
        out_shape=[jax.ShapeDtypeStruct((l_dim, D_MODEL), BF16), jax.ShapeDtypeStruct((1, D_MODEL), F32),
                   jax.ShapeDtypeStruct((1, N_STATE), F32), jax.ShapeDtypeStruct((1, N_STATE), F32),
                   jax.ShapeDtypeStruct((SSM_KB, BLK, ST_KB), F32), jax.ShapeDtypeStruct((SSM_KB, BLK, ST_KB), F32),
                   jax.ShapeDtypeStruct((SSM_KB, ST_KB, BLK), F32), jax.ShapeDtypeStruct((SSM_KB, ST_KB, BLK), F32)],
        scratch_shapes=[pltpu.VMEM((8, N_STATE), F32), pltpu.VMEM((8, N_STATE), F32),
                        pltpu.VMEM((N_LB, BLK, BLK), F32), pltpu.VMEM((N_LB, BLK, BLK), F32)],
        compiler_params=_params(),
    )(dz, y, u, s_re, s_im, wb_re, wb_im, wc_re, wc_im, d_skip, *tab_in)


def _discretize(lam_re, lam_im, log_dt, b_re, b_im):
    dt = jnp.exp(log_dt)[:, None]
    mag = jnp.exp(lam_re * dt)
    ar, ai = mag * jnp.cos(lam_im * dt), mag * jnp.sin(lam_im * dt)
    den = lam_re * lam_re + lam_im * lam_im
    nr, ni = ar - 1.0, ai
    fr, fi = (nr * lam_re + ni * lam_im) / den, (ni * lam_re - nr * lam_im) / den
    bbar_re = fr[..., None] * b_re - fi[..., None] * b_im
    bbar_im = fr[..., None] * b_im + fi[..., None] * b_re
    return ar, ai, bbar_re, bbar_im


def _block_diag_b(bbar):
    eye = jnp.eye(8, dtype=bbar.dtype)
    return jnp.einsum("kgpc,gh->kgchp", bbar.reshape(8, 8, SSM_STATE, SSM_GROUP_CH), eye).reshape(8, BLK, ST_KB)


def _block_diag_b_t(dwb):
    eye = jnp.eye(8, dtype=dwb.dtype)
    return jnp.einsum("kgchp,gh->kgpc", dwb.reshape(8, 8, SSM_GROUP_CH, 8, SSM_STATE), eye).reshape(
        SSM_GROUPS, SSM_STATE, SSM_GROUP_CH)


def _block_diag_c(c):
    eye = jnp.eye(8, dtype=c.dtype)
    return jnp.einsum("kgcp,gh->kgphc", c.reshape(8, 8, SSM_GROUP_CH, SSM_STATE), eye).reshape(8, ST_KB, BLK)


def _block_diag_c_t(dwc):
    eye = jnp.eye(8, dtype=dwc.dtype)
    return jnp.einsum("kgphc,gh->kgcp", dwc.reshape(8, 8, SSM_STATE, 8, SSM_GROUP_CH), eye).reshape(
        SSM_GROUPS, SSM_GROUP_CH, SSM_STATE)


def _powers(br, bi, n):
    pr, pi = br, bi
    cr, ci = br, bi
    while pr.shape[0] < n:
        pr, pi = (jnp.concatenate([pr, pr * cr - pi * ci], axis=0),
                  jnp.concatenate([pi, pr * ci + pi * cr], axis=0))
        cr, ci = cr * cr - ci * ci, 2.0 * cr * ci
    return pr[:n], pi[:n]


def _power_tables(ar, ai):
    g = BLK // GRP
    a1r, a1i = _powers(ar, ai, GRP)
    seqr, seqi = _powers(a1r[GRP - 1:], a1i[GRP - 1:], g)
    sq_r, sq_i = [seqr[0:1]], [seqi[0:1]]
    while len(sq_r) < 8:
        r, i = sq_r[-1], sq_i[-1]
        sq_r.append(r * r - i * i)
        sq_i.append(2.0 * r * i)
    sqr, sqi = jnp.concatenate(sq_r, axis=0), jnp.concatenate(sq_i, axis=0)
    return a1r, a1i, sqr, sqi, seqr, seqi, seqr[::-1], seqi[::-1]


def _all_gather(name, arrs):
    n = len(arrs)

    def body(*refs):
        ins, outs = refs[:n], refs[n:2 * n]
        send_sems, recv_sems, local_sems = refs[2 * n:]
        x, y, c = lax.axis_index("x"), lax.axis_index("y"), lax.axis_index("c")
        me, sibling = (x, y, c), (x, y, 1 - c)
        chips = [(1 - x, y), (x, 1 - y), (1 - x, 1 - y)]

        def rows(a, px, py, pc):
            m = ins[a].shape[0]
            return outs[a].at[pl.ds((4 * px + 2 * py + pc) * m, m), :]

        def copy(a, k, block, to, src=None):
            return pltpu.make_async_remote_copy(
                src_ref=rows(a, *block) if src is None else src, dst_ref=rows(a, *block),
                send_sem=send_sems.at[a, k], recv_sem=recv_sems.at[a, k],
                device_id=to, device_id_type=MESH)

        mine = [pltpu.make_async_copy(ins[a], rows(a, *me), local_sems.at[a]) for a in range(n)]
        for cp in mine:
            cp.start()
        first = []
        for a in range(n):
            first.append(copy(a, 0, me, sibling, src=ins[a]))
            first += [copy(a, 1 + j, me, (*chip, c), src=ins[a]) for j, chip in enumerate(chips)]
        for cp in first:
            cp.start()
        passed = []
        for j, chip in enumerate(chips):
            for a in range(n):
                copy(a, 1 + j, (*chip, c), me).wait_recv()
                cp = copy(a, 4 + j, (*chip, c), sibling)
                cp.start()
                passed.append(cp)
        for a in range(n):
            copy(a, 0, sibling, me).wait_recv()
            for j, chip in enumerate(chips):
                copy(a, 4 + j, (*chip, 1 - c), me).wait_recv()
        for cp in first + passed:
            cp.wait_send()
        for cp in mine:
            cp.wait()

    any_spec = pl.BlockSpec(memory_space=pl.ANY)
    return pl.pallas_call(
        body, name=name,
        in_specs=[any_spec] * n, out_specs=[any_spec] * n,
        out_shape=[jax.ShapeDtypeStruct((N_DEV * a.shape[0], a.shape[1]), a.dtype) for a in arrs],
        scratch_shapes=[pltpu.SemaphoreType.DMA((n, 7)), pltpu.SemaphoreType.DMA((n, 7)),
                        pltpu.SemaphoreType.DMA((n,))],
        compiler_params=_params(has_side_effects=True),
    )(*arrs)


HBM_SPEC = pl.BlockSpec(memory_space=pltpu.HBM)
SEM_SPEC = pl.BlockSpec(memory_space=pltpu.SEMAPHORE)
DATAFLOW = pltpu.SideEffectType.DATAFLOW_SIDE_EFFECTING


def _plain_rows(p, m):
    return p * m


def _ffn_in_rows(p, m):
    return ((p & 3) >> 1) * (4 * m) + (p >> 2) * (2 * m) + (p & 1) * m


def _peer_copies(src_refs, land_refs, send_sems, recv_sems, chunked, row_fns):
    x, y, c = lax.axis_index("x"), lax.axis_index("y"), lax.axis_index("c")
    me = 4 * x + 2 * y + c
    copies = []
    for a, (src, land) in enumerate(zip(src_refs, land_refs)):
        m = land.shape[0] // N_DEV
        for k in range(N_DEV - 1):
            rel = k + 1
            bx, by, bc = (rel >> 2) & 1, (rel >> 1) & 1, rel & 1
            peer = (x + bx - 2 * x * bx, y + by - 2 * y * by, c + bc - 2 * c * bc)
            p_idx = 4 * peer[0] + 2 * peer[1] + peer[2]
            copies.append(pltpu.make_async_remote_copy(
                src_ref=src.at[pl.ds(row_fns[a](p_idx, m), m), :] if chunked else src,
                dst_ref=land.at[pl.ds(me * m if chunked else row_fns[a](me, m), m), :],
                send_sem=send_sems[a * (N_DEV - 1) + k], recv_sem=recv_sems[a * (N_DEV - 1) + k],
                device_id=peer, device_id_type=MESH))
    return copies


def _send_start(name, srcs, lands, chunked, row_fns=None):
    n = len(srcs)
    ns = n * (N_DEV - 1)
    row_fns = row_fns or [_plain_rows] * n

    def body(*refs):
        src_refs, land_refs = refs[:n], refs[n:2 * n]
        send_sems, recv_sems = refs[2 * n:2 * n + ns], refs[2 * n + ns:2 * n + 2 * ns]
        token = refs[-1]
        for cp in _peer_copies(src_refs, land_refs, send_sems, recv_sems, chunked, row_fns):
            cp.start()
        token[...] = jnp.zeros_like(token)

    ins = [pltpu.with_memory_space_constraint(a, pltpu.HBM) for a in list(srcs) + list(lands)]
    out = pl.pallas_call(
        body, name=name,
        in_specs=[HBM_SPEC] * (2 * n),
        out_specs=[SEM_SPEC] * (2 * ns) + [HBM_SPEC] * (2 * n) + [pl.BlockSpec(memory_space=pltpu.VMEM)],
        out_shape=[pltpu.SemaphoreType.DMA(())] * (2 * ns)
        + [pltpu.HBM(a.shape, a.dtype) for a in list(srcs) + list(lands)]
        + [jax.ShapeDtypeStruct((8, BLK), F32)],
        input_output_aliases={i: i + 2 * ns for i in range(2 * n)},
        compiler_params=pltpu.CompilerParams(has_side_effects=DATAFLOW),
    )(*ins)
    return out[:ns], out[ns:2 * ns], out[2 * ns:2 * ns + n], out[2 * ns + n:2 * ns + 2 * n], out[-1]


def _send_wait(name, send_sems, recv_sems, srcs, lands, after, chunked, row_fns=None):
    n = len(srcs)
    ns = n * (N_DEV - 1)
    row_fns = row_fns or [_plain_rows] * n

    def body(*refs):
        src_refs, land_refs = refs[:n], refs[n:2 * n]
        s_sems, r_sems = refs[2 * n:2 * n + ns], refs[2 * n + ns:2 * n + 2 * ns]
        copies = _peer_copies(src_refs, land_refs, s_sems, r_sems, chunked, row_fns)
        for cp in copies:
            cp.wait_send()
        for cp in copies:
            cp.wait_recv()

    out = pl.pallas_call(
        body, name=name,
        in_specs=[HBM_SPEC] * (2 * n) + [SEM_SPEC] * (2 * ns) + [pl.BlockSpec(memory_space=pl.ANY)],
        out_specs=[HBM_SPEC] * (2 * n),
        out_shape=[pltpu.HBM(a.shape, a.dtype) for a in list(srcs) + list(lands)],
        input_output_aliases={i: i for i in range(2 * n)},
        compiler_params=pltpu.CompilerParams(has_side_effects=DATAFLOW),
    )(*srcs, *lands, *send_sems, *recv_sems, after)
    return out[n:]


def _sum_slots(name, recv, own):
    m, ncol = own.shape
    tr = m // 2 if (m // 2) % 16 == 0 else m
    g = m // tr

    def body(*refs):
        slots, own_ref, o_ref = refs[:N_DEV], refs[N_DEV], refs[N_DEV + 1]
        me = _my_index()
        tot = None
        for s in range(N_DEV):
            v = jnp.where(me == s, own_ref[...], slots[s][...].astype(F32))
            tot = v if tot is None else tot + v
        o_ref[...] = tot

    def slot_spec(s):
        return pl.BlockSpec((tr, ncol), lambda i: (s * g + i, 0))

    return pl.pallas_call(
        body, name=name, grid=(g,),
        in_specs=[slot_spec(s) for s in range(N_DEV)] + [pl.BlockSpec((tr, ncol), lambda i: (i, 0))],
        out_specs=pl.BlockSpec((tr, ncol), lambda i: (i, 0)),
        out_shape=jax.ShapeDtypeStruct((m, ncol), F32),
        compiler_params=_params(),
    )(*([recv] * N_DEV), own)


def _sum_gathered(name, gathered, rows):
    tr = _pick(rows, 512, 8)
    g = rows // tr

    def body(*refs):
        o_ref = refs[N_DEV]
        tot = refs[0][...]
        for s in range(1, N_DEV):
            tot = tot + refs[s][...]
        o_ref[...] = tot

    return pl.pallas_call(
        body, name=name, grid=(g,),
        in_specs=[pl.BlockSpec((tr, BLK), (lambda i, s=s: (s * g + i, 0))) for s in range(N_DEV)],
        out_specs=pl.BlockSpec((tr, BLK), lambda i: (i, 0)),
        out_shape=jax.ShapeDtypeStruct((rows, BLK), F32),
        compiler_params=_params(),
    )(*([gathered] * N_DEV))


def _adamw(name, w, g, m, v):
    r, c = w.shape
    tr = _pick(r, 256, 8) if r % 8 == 0 else r
    c1 = 1.0 - ADAM_B1 ** ADAM_STEP
    c2 = 1.0 - ADAM_B2 ** ADAM_STEP

    def body(w_ref, g_ref, m_ref, v_ref, d_ref, nm_ref, nv_ref):
        gv = g_ref[...]
        nm = ADAM_B1 * m_ref[...] + (1.0 - ADAM_B1) * gv
        nv = ADAM_B2 * v_ref[...] + (1.0 - ADAM_B2) * (gv * gv)
        m_hat = nm / c1
        v_hat = nv / c2
        d_ref[...] = -ADAM_LR * (m_hat / (jnp.sqrt(v_hat) + ADAM_EPS) + ADAM_WD * w_ref[...])
        nm_ref[...] = nm
        nv_ref[...] = nv

    spec = pl.BlockSpec((tr, c), lambda i: (i, 0))
    shape = jax.ShapeDtypeStruct((r, c), F32)
    return pl.pallas_call(
        body, name=name, grid=(r // tr,),
        in_specs=[spec] * 4, out_specs=[spec] * 3, out_shape=[shape] * 3,
        compiler_params=_params(),
    )(w, g, m, v)


def _pack(parts):
    flat = []
    for p in parts:
        v = p.reshape(-1)
        flat.append(jnp.pad(v, (0, (-v.shape[0]) % BLK)))
    v = jnp.concatenate(flat)
    v = jnp.pad(v, (0, (-v.shape[0]) % (8 * BLK)))
    return v.reshape(-1, BLK)


def _unpack(buf, shapes):
    flat = buf.reshape(-1)
    out, off = [], 0
    for shp in shapes:
        size = math.prod(shp)
        out.append(flat[off:off + size].reshape(shp))
        off += size + (-size) % BLK
    return out


def kernel(x, meta_tokens, norm_mix, w_in, q_norm, k_norm, attn_sinks, lam_re, lam_im, log_dt, ssm_b_re, ssm_b_im, ssm_c_re, ssm_c_im, ssm_d, w_glu, attn_branch_norm, ssm_branch_norm, w_out, norm_ffn, w_ffn_in, w_ffn_out, loss_target, m_meta_tokens, m_norm_mix, m_w_in, m_q_norm, m_k_norm, m_attn_sinks, m_lam_re, m_lam_im, m_log_dt, m_ssm_b_re, m_ssm_b_im, m_ssm_c_re, m_ssm_c_im, m_ssm_d, m_w_glu, m_attn_branch_norm, m_ssm_branch_norm, m_w_out, m_norm_ffn, m_w_ffn_in, m_w_ffn_out, v_meta_tokens, v_norm_mix, v_w_in, v_q_norm, v_k_norm, v_attn_sinks, v_lam_re, v_lam_im, v_log_dt, v_ssm_b_re, v_ssm_b_im, v_ssm_c_re, v_ssm_c_im, v_ssm_d, v_w_glu, v_attn_branch_norm, v_ssm_branch_norm, v_w_out, v_norm_ffn, v_w_ffn_in, v_w_ffn_out):
    args = dict(locals())
    weights = {n: args[n] for n in WEIGHTS}
    mom_m = {n: args["m_" + n] for n in WEIGHTS}
    mom_v = {n: args["v_" + n] for n in WEIGHTS}

    x2d = x[0]
    target2d = loss_target[0]
    s_len = x2d.shape[0]
    l_dim = s_len + BLK
    tm_row = _pick(l_dim, 320)
    tm_mm = _pick(l_dim, 1040)
    tl_tn = _pick(l_dim, 832)
    tm_ffn = _pick(l_dim, 640)

    shard_in = w_in[0].T.astype(BF16)
    shard_glu = w_glu[0].T.astype(BF16)
    shard_out = w_out[0].astype(BF16)
    shard_ffn_in = w_ffn_in[0].T.astype(BF16)
    shard_ffn_out = w_ffn_out[0].astype(BF16)
    shard_meta = meta_tokens.T
    me = _my_index()

    def landing(shard, row_fn=_plain_rows):
        m_rows, cols = shard.shape
        return lax.dynamic_update_slice(lax.empty((N_DEV * m_rows, cols), shard.dtype), shard,
                                        (row_fn(me, m_rows), 0))

    first = [shard_in, shard_meta]
    later = [shard_glu, shard_out, shard_ffn_in, shard_ffn_out]
    later_fns = [_plain_rows, _plain_rows, _ffn_in_rows, _plain_rows]
    ga = _send_start("gather_start_a", first, [landing(s) for s in first], chunked=False)
    gb = _send_start("gather_start_b", later, [landing(s, f) for s, f in zip(later, later_fns)], chunked=False,
                     row_fns=later_fns)

    nm_t = norm_mix + (ga[4][0:1, 0:1] + gb[4][0:1, 0:1])
    qn_t, kn_t = jnp.tile(q_norm, (1, N_Q_HEADS)), jnp.tile(k_norm, (1, N_KV_HEADS))
    e_mat = jnp.kron(jnp.eye(4, dtype=F32), jnp.ones((HEAD_DIM, HEAD_DIM), F32)).astype(BF16)

    def disc(lr, li, ldt, br, bi):
        return _discretize(lr[0], li[0], ldt[0], br[0], bi[0])

    (abar_re, abar_im, bbar_re, bbar_im), disc_vjp = jax.vjp(disc, lam_re, lam_im, log_dt, ssm_b_re, ssm_b_im)
    wb_re, wb_im = _block_diag_b(bbar_re).astype(BF16), _block_diag_b(bbar_im).astype(BF16)
    wc_re, wc_im = _block_diag_c(ssm_c_re[0]).astype(BF16), _block_diag_c(ssm_c_im[0]).astype(BF16)
    tabs = _power_tables(abar_re.reshape(1, N_STATE), abar_im.reshape(1, N_STATE))

    h0, xn = _embed_norm(x2d, nm_t)
    wt_in, meta_t = _send_wait("gather_wait_a", ga[0], ga[1], ga[2], ga[3], xn, chunked=False)
    meta_pad = jnp.pad(meta_t.T, ((PAD, 0), (0, 0)))
    h0, xn = _embed_meta(meta_pad, nm_t, h0, xn)
    qkv = _matmul("proj_qkv", xn, wt_in, nt=True, tm=tm_mm, tn=512, tk=D_MODEL, n=Q_W + 2 * KV_W, w_off=0)
    u = _matmul("proj_u", xn, wt_in, nt=True, tm=tm_mm, tn=512, tk=D_MODEL, n=D_MODEL, w_off=3)
    gates = _matmul("proj_gates", xn, wt_in, nt=True, tm=tm_mm, tn=512, tk=D_MODEL, n=2 * D_MODEL, w_off=5)
    qn, kf, vf = _qk_prep(qkv, qn_t, kn_t, e_mat, tm_row)
    attn = _attn_fwd(qn, kf, vf, attn_sinks)
    y, z, s_re, s_im = _ssm_fwd(u, wb_re, wb_im, wc_re, wc_im, ssm_d, tabs)
    wt_glu, w_out_f, wt_ffn_in, w_ffn_out_f = _send_wait("gather_wait_b", gb[0], gb[1], gb[2], gb[3], z,
                                                         chunked=False, row_fns=later_fns)
    zab = _matmul("glu_proj", z, wt_glu, nt=True, tm=tm_mm, tn=1024, tk=D_MODEL)
    merged = _merge_fwd(attn, zab, gates, attn_branch_norm, ssm_branch_norm, tm_row)
    h1 = _matmul("out_proj", merged, w_out_f, nt=False, tm=tm_mm, tn=1024, tk=D_MODEL, res=h0)
    hn = _norm_cast("ffn_norm", h1, norm_ffn, tm_row)
    gu, act = _ffn_in_swiglu(hn, wt_ffn_in, tm_ffn)
    h2 = _matmul("ffn_out", act, w_ffn_out_f, nt=False, tm=tm_mm, tn=1024, tk=1408, res=h1)
    dh2, dh2_b, loss_part = _loss_grad(h2, target2d)

    dgu = _d_act_swiglu(dh2_b, w_ffn_out_f, gu, tm_ffn)

    def exchange_start(name, grads_b, row_fns=None):
        return _send_start(name, grads_b, [jnp.zeros(g.shape, BF16) for g in grads_b], chunked=True,
                           row_fns=row_fns)

    g_ffn_out, g_ffn_out_b = _matmul_tn("g_ffn_out", act, dh2_b, tm=1408, tn=1024, tl=tl_tn)
    g_ffn_in_t, g_ffn_in_b = _matmul_tn("g_ffn_in", dgu, hn, tm=1408, tn=1024, tl=tl_tn)
    ffn_fns = [_ffn_in_rows, _plain_rows]
    ex1 = exchange_start("exchange_start_ffn", [g_ffn_in_b, g_ffn_out_b], ffn_fns)
    dhn = _matmul("d_hn", dgu, wt_ffn_in, nt=False, tm=tm_mm, tn=1024, tk=1408)
    dh1, dh1_b, g_norm_ffn = _norm_bwd_res("ffn_norm_bwd", h1, norm_ffn + ex1[4][0:1, 0:1], dhn, dh2, tm_row)
    dmerged = _matmul("d_merged", dh1_b, w_out_f, nt=True, tm=tm_mm, tn=1024, tk=D_MODEL)
    dattn, dzab, dgates, g_abn, g_sbn = _merge_bwd(attn, zab, gates, attn_branch_norm, ssm_branch_norm,
                                                     dmerged, tm_row)
    g_out, g_out_b = _matmul_tn("g_out", merged, dh1_b, tm=1024, tn=1024, tl=tl_tn)
    dz = _matmul("d_z", dzab, wt_glu, nt=False, tm=tm_mm, tn=1024, tk=1024)
    g_glu_t, g_glu_b = _matmul_tn("g_glu", dzab, z, tm=1024, tn=1024, tl=tl_tn)
    ex2 = exchange_start("exchange_start_mix", [g_glu_b, g_out_b])
    du, g_ssm_d, g_ar, g_ai, g_wbr, g_wbi, g_wcr, g_wci = _ssm_bwd(
        dz, y, u, s_re, s_im, wb_re, wb_im, wc_re, wc_im, ssm_d + ex2[4][0:1, 0:1], tabs)
    dq, dkc, dkp, dkm, dvc, dvp, dvm, g_sinks = _attn_bwd(qn, kf, vf, attn_sinks, attn, dattn)
    dqkv, g_qn_t, g_kn_t = _qk_bwd(qkv, qn_t, kn_t, e_mat, dq, dkc, dkp, dkm, dvc, dvp, dvm)
    dproj = jnp.concatenate([dqkv, du, dgates], axis=1)
    g_in_t, g_in_b = _matmul_tn("g_in", dproj, xn, tm=1152, tn=1024, tl=tl_tn)
    ex3 = exchange_start("exchange_start_in", [g_in_b])
    dxn = _matmul("d_xn", dproj, wt_in, nt=False, tm=tm_mm, tn=1024, tk=1152)
    grad_x2d, dmeta_blk, g_norm_mix = _final_bwd(h0, nm_t + ex3[4][0:1, 0:1], dxn, dh1)

    g_lam_re, g_lam_im, g_log_dt, g_b_re, g_b_im = disc_vjp(
        (g_ar.reshape(SSM_GROUPS, SSM_STATE), g_ai.reshape(SSM_GROUPS, SSM_STATE),
         _block_diag_b_t(g_wbr), _block_diag_b_t(g_wbi)))
    small_grads = {
        "norm_mix": g_norm_mix, "q_norm": g_qn_t.reshape(N_Q_HEADS, HEAD_DIM).sum(0)[None],
        "k_norm": g_kn_t.reshape(N_KV_HEADS, HEAD_DIM).sum(0)[None], "attn_sinks": g_sinks,
        "lam_re": g_lam_re, "lam_im": g_lam_im, "log_dt": g_log_dt, "ssm_b_re": g_b_re, "ssm_b_im": g_b_im,
        "ssm_c_re": _block_diag_c_t(g_wcr)[None], "ssm_c_im": _block_diag_c_t(g_wci)[None],
        "ssm_d": g_ssm_d, "attn_branch_norm": g_abn, "ssm_branch_norm": g_sbn, "norm_ffn": g_norm_ffn,
    }
    small_shapes = [weights[n].shape for n in SMALL] + [(N_META, D_MODEL)]
    packed = _pack([small_grads[n] for n in SMALL] + [dmeta_blk[PAD:]])
    rows = packed.shape[0]
    (gathered,) = _all_gather("gather_small", [packed])
    g_small = _sum_gathered("sum_small", gathered, rows)
    w_small = _pack([weights[n] for n in SMALL] + [jnp.zeros((N_META, D_MODEL), F32)])
    m_small = _pack([mom_m[n] for n in SMALL] + [jnp.zeros((N_META, D_MODEL), F32)])
    v_small = _pack([mom_v[n] for n in SMALL] + [jnp.zeros((N_META, D_MODEL), F32)])
    d_small, nm_small, nv_small = _adamw("adamw_small", w_small, g_small, m_small, v_small)
    grads = dict(zip(SMALL + ["_meta"], _unpack(g_small, small_shapes)))
    deltas = dict(zip(SMALL, _unpack(d_small, small_shapes[:-1])))
    new_m = dict(zip(SMALL, _unpack(nm_small, small_shapes[:-1])))
    new_v = dict(zip(SMALL, _unpack(nv_small, small_shapes[:-1])))

    grads["meta_tokens"] = lax.dynamic_slice(grads.pop("_meta"), (0, me * BLK), (N_META, BLK))

    recv_ffn_in, recv_ffn_out = _send_wait("exchange_wait_ffn", ex1[0], ex1[1], ex1[2], ex1[3], g_small,
                                           chunked=True, row_fns=ffn_fns)
    recv_glu, recv_out = _send_wait("exchange_wait_mix", ex2[0], ex2[1], ex2[2], ex2[3], recv_ffn_in,
                                    chunked=True)
    (recv_in,) = _send_wait("exchange_wait_in", ex3[0], ex3[1], ex3[2], ex3[3], recv_glu, chunked=True)
    big = [("w_in", g_in_t, True, recv_in, _plain_rows), ("w_glu", g_glu_t, True, recv_glu, _plain_rows),
           ("w_out", g_out, False, recv_out, _plain_rows), ("w_ffn_in", g_ffn_in_t, True, recv_ffn_in, _ffn_in_rows),
           ("w_ffn_out", g_ffn_out, False, recv_ffn_out, _plain_rows)]
    for name, g_full, transposed, recv, row_fn in big:
        m_rows = g_full.shape[0] // N_DEV
        own = lax.dynamic_slice(g_full, (row_fn(me, m_rows), 0), (m_rows, g_full.shape[1]))
        g_shard = _sum_slots("sum_" + name, recv, own)
        grads[name] = (g_shard.T if transposed else g_shard)[None]

    for name in ["meta_tokens", "w_in", "w_glu", "w_out", "w_ffn_in", "w_ffn_out"]:
        shp = weights[name].shape
        as2d = lambda a: a.reshape(shp[-2], shp[-1])
        d, nm, nv = _adamw("adamw_" + name, as2d(weights[name]), as2d(grads[name]), as2d(mom_m[name]),
                           as2d(mom_v[name]))
        deltas[name], new_m[name], new_v[name] = d.reshape(shp), nm.reshape(shp), nv.reshape(shp)

    loss = lax.psum(loss_part[0, 0], ("x", "y", "c"))
    return (loss, grad_x2d[None], *[grads[n] for n in WEIGHTS], *[deltas[n] for n in WEIGHTS],
            *[new_m[n] for n in WEIGHTS], *[new_v[n] for n in WEIGHTS])
```

```python
import math

import jax
import jax.numpy as jnp
from jax import lax
from jax.experimental import pallas as pl
from jax.experimental.pallas import tpu as pltpu

F32 = jnp.float32
BF16 = jnp.bfloat16

D_MODEL = 1024
N_META = 16
HEAD_DIM = 64
N_Q_HEADS = 16
N_KV_HEADS = 4
Q_W = N_Q_HEADS * HEAD_DIM
KV_W = N_KV_HEADS * HEAD_DIM
SSM_GROUPS = 64
SSM_GROUP_CH = 16
SSM_STATE = 64
N_STATE = SSM_GROUPS * SSM_STATE
D_FF = 2816
IN_COLS = Q_W + 2 * KV_W + 3 * D_MODEL
EPS = 1e-6
BLK = 128
PAD = BLK - N_META
N_DEV = 8
NEG = -1e30
SSM_KB = 8
ST_KB = N_STATE // SSM_KB
LB_KB = ST_KB // BLK
N_LB = N_STATE // BLK

ADAM_LR = 0.001
ADAM_B1 = 0.9
ADAM_B2 = 0.999
ADAM_EPS = 1e-08
ADAM_WD = 0.01
ADAM_STEP = 10

VMEM_LIMIT = 48 * 1024 * 1024
MESH = pl.DeviceIdType.MESH

SMALL = ["norm_mix", "q_norm", "k_norm", "attn_sinks", "lam_re", "lam_im", "log_dt", "ssm_b_re", "ssm_b_im",
         "ssm_c_re", "ssm_c_im", "ssm_d", "attn_branch_norm", "ssm_branch_norm", "norm_ffn"]
WEIGHTS = ["meta_tokens", "norm_mix", "w_in", "q_norm", "k_norm", "attn_sinks", "lam_re", "lam_im", "log_dt",
           "ssm_b_re", "ssm_b_im", "ssm_c_re", "ssm_c_im", "ssm_d", "w_glu", "attn_branch_norm",
           "ssm_branch_norm", "w_out", "norm_ffn", "w_ffn_in", "w_ffn_out"]


def _params(**kw):
    return pltpu.CompilerParams(vmem_limit_bytes=VMEM_LIMIT, **kw)


def _pick(n, cap, mult=16):
    best = None
    for d in range(mult, min(n, cap) + 1, mult):
        if n % d == 0:
            best = d
    assert best is not None, (n, cap, mult)
    return best


def _my_index():
    return 4 * lax.axis_index("x") + 2 * lax.axis_index("y") + lax.axis_index("c")


def _rms(x, g):
    r = lax.rsqrt(jnp.mean(x * x, axis=-1, keepdims=True) + EPS)
    return x * r * g


def _rms_bwd(x, g, dy):
    r = lax.rsqrt(jnp.mean(x * x, axis=-1, keepdims=True) + EPS)
    t = dy * g
    dx = r * t - x * (r * r * r) * jnp.mean(t * x, axis=-1, keepdims=True)
    dg = jnp.sum(dy * (x * r), axis=0, keepdims=True)
    return dx, dg


def _sigmoid(x):
    return 1.0 / (1.0 + jnp.exp(-x))


def _gelu(x):
    k = math.sqrt(2.0 / math.pi)
    return 0.5 * x * (1.0 + jnp.tanh(k * (x + 0.044715 * (x * x * x))))


def _gelu_grad(x):
    k = math.sqrt(2.0 / math.pi)
    t = jnp.tanh(k * (x + 0.044715 * (x * x * x)))
    return 0.5 * (1.0 + t) + 0.5 * x * (1.0 - t * t) * (k * (1.0 + 3.0 * 0.044715 * (x * x)))


def _head_mean(x, e_ref):
    hi = x.astype(BF16)
    r1 = x - hi.astype(F32)
    mid = r1.astype(BF16)
    lo = (r1 - mid.astype(F32)).astype(BF16)
    e = e_ref[...]
    out = []
    for b in range(x.shape[1] // 256):
        sl = slice(256 * b, 256 * b + 256)
        s = (jnp.dot(hi[:, sl], e, preferred_element_type=F32)
             + jnp.dot(mid[:, sl], e, preferred_element_type=F32)
             + jnp.dot(lo[:, sl], e, preferred_element_type=F32))
        out.append(s)
    s = out[0] if len(out) == 1 else jnp.concatenate(out, axis=1)
    return s * (1.0 / HEAD_DIM)


def _head_rms(x, g, e_ref):
    r = lax.rsqrt(_head_mean(x * x, e_ref) + EPS)
    return x * r * g


def _head_rms_bwd(x, g, dy, e_ref):
    r = lax.rsqrt(_head_mean(x * x, e_ref) + EPS)
    t = dy * g
    dx = r * t - x * (r * r * r) * _head_mean(t * x, e_ref)
    dg = jnp.sum(dy * (x * r), axis=0, keepdims=True)
    return dx, dg


def _lane_half(shape):
    lane = lax.broadcasted_iota(jnp.int32, shape, len(shape) - 1)
    return (lane >> 6) & 1


def _matmul(name, a, w, *, nt, tm, tn, tk, n=None, w_off=0, res=None, norm_g=None, out_dtype=F32):
    m_dim, k_dim = a.shape
    n_dim = n if n is not None else (w.shape[0] if nt else w.shape[1])
    gm, gn, gk = m_dim // tm, n_dim // tn, k_dim // tk
    assert gm * tm == m_dim and gn * tn == n_dim and gk * tk == k_dim, (name, a.shape, w.shape, tm, tn, tk)
    assert norm_g is None or tn == n_dim
    dn = (((1,), (1,)), ((), ())) if nt else (((1,), (0,)), ((), ()))

    def body(*refs):
        refs = list(refs)
        a_ref, w_ref = refs[0], refs[1]
        pos = 2
        r_ref = g_ref = on_ref = None
        if res is not None:
            r_ref, pos = refs[pos], pos + 1
        if norm_g is not None:
            g_ref, pos = refs[pos], pos + 1
        o_ref, pos = refs[pos], pos + 1
        if norm_g is not None:
            on_ref, pos = refs[pos], pos + 1
        acc = refs[pos]
        k = pl.program_id(2)

        @pl.when(k == 0)
        def _():
            acc[...] = jnp.zeros_like(acc)

        acc[...] += lax.dot_general(a_ref[...], w_ref[...], dn, preferred_element_type=F32)

        @pl.when(k == gk - 1)
        def _():
            r = acc[...]
            if r_ref is not None:
                r = r_ref[...] + r
            o_ref[...] = r.astype(out_dtype)
            if on_ref is not None:
                on_ref[...] = _rms(r, g_ref[...]).astype(BF16)

    if nt:
        w_spec = pl.BlockSpec((tn, tk), lambda i, j, k: (j + w_off, k))
    else:
        w_spec = pl.BlockSpec((tk, tn), lambda i, j, k: (k, j))
    in_specs = [pl.BlockSpec((tm, tk), lambda i, j, k: (i, k)), w_spec]
    args = [a, w]
    out_spec = pl.BlockSpec((tm, tn), lambda i, j, k: (i, j))
    out_specs, out_shape = [out_spec], [jax.ShapeDtypeStruct((m_dim, n_dim), out_dtype)]
    if res is not None:
        in_specs.append(out_spec)
        args.append(res)
    if norm_g is not None:
        in_specs.append(pl.BlockSpec((1, tn), lambda i, j, k: (0, 0)))
        args.append(norm_g)
        out_specs.append(out_spec)
        out_shape.append(jax.ShapeDtypeStruct((m_dim, n_dim), BF16))
    out = pl.pallas_call(
        body, name=name, grid=(gm, gn, gk),
        in_specs=in_specs, out_specs=out_specs, out_shape=out_shape,
        scratch_shapes=[pltpu.VMEM((tm, tn), F32)],
        compiler_params=_params(dimension_semantics=("parallel", "parallel", "arbitrary")),
    )(*args)
    return out if norm_g is not None else out[0]


def _matmul_tn(name, a, b, *, tm, tn, tl):
    l_dim, m_dim = a.shape
    n_dim = b.shape[1]
    gm, gn, gl = m_dim // tm, n_dim // tn, l_dim // tl
    assert gm * tm == m_dim and gn * tn == n_dim and gl * tl == l_dim, (name, a.shape, b.shape, tm, tn, tl)

    def body(a_ref, b_ref, o_ref, ob_ref):
        @pl.when(pl.program_id(2) == 0)
        def _():
            o_ref[...] = jnp.zeros_like(o_ref)

        o_ref[...] += lax.dot_general(a_ref[...], b_ref[...], (((0,), (0,)), ((), ())),
                                      preferred_element_type=F32)

        @pl.when(pl.program_id(2) == gl - 1)
        def _():
            ob_ref[...] = o_ref[...].astype(BF16)

    out_spec = pl.BlockSpec((tm, tn), lambda i, j, l: (i, j))
    return pl.pallas_call(
        body, name=name, grid=(gm, gn, gl),
        in_specs=[pl.BlockSpec((tl, tm), lambda i, j, l: (l, i)),
                  pl.BlockSpec((tl, tn), lambda i, j, l: (l, j))],
        out_specs=[out_spec, out_spec],
        out_shape=[jax.ShapeDtypeStruct((m_dim, n_dim), F32), jax.ShapeDtypeStruct((m_dim, n_dim), BF16)],
        compiler_params=_params(dimension_semantics=("parallel", "parallel", "arbitrary")),
    )(a, b)


def _row_spec(tm, cols, f=None):
    if f is None:
        return pl.BlockSpec((tm, cols), lambda i: (i, 0))
    return pl.BlockSpec((tm, cols), lambda i: (f(i), 0))


def _full_spec(shape):
    nd = len(shape)
    return pl.BlockSpec(shape, lambda i: (0,) * nd)


def _embed_norm(x2d, g):
    s_len = x2d.shape[0]
    nb = s_len // BLK + 1

    def body(x_ref, g_ref, h_ref, xn_ref):
        h_ref[...] = x_ref[...]
        xn_ref[...] = _rms(x_ref[...], g_ref[...]).astype(BF16)

    return pl.pallas_call(
        body, name="embed_norm", grid=(nb - 1,),
        in_specs=[_row_spec(BLK, D_MODEL), _full_spec((1, D_MODEL))],
        out_specs=[_row_spec(BLK, D_MODEL, lambda i: i + 1), _row_spec(BLK, D_MODEL, lambda i: i + 1)],
        out_shape=[jax.ShapeDtypeStruct((nb * BLK, D_MODEL), F32),
                   jax.ShapeDtypeStruct((nb * BLK, D_MODEL), BF16)],
        compiler_params=_params(),
    )(x2d, g)


def _embed_meta(meta_pad, g, h0, xn):
    def body(mp_ref, g_ref, h_in, xn_in, h_ref, xn_ref):
        h_ref[...] = mp_ref[...]
        xn_ref[...] = _rms(mp_ref[...], g_ref[...]).astype(BF16)

    any_spec = pl.BlockSpec(memory_space=pl.ANY)
    return pl.pallas_call(
        body, name="embed_meta", grid=(1,),
        in_specs=[_full_spec((BLK, D_MODEL)), _full_spec((1, D_MODEL)), any_spec, any_spec],
        out_specs=[_row_spec(BLK, D_MODEL), _row_spec(BLK, D_MODEL)],
        out_shape=[jax.ShapeDtypeStruct(h0.shape, F32), jax.ShapeDtypeStruct(xn.shape, BF16)],
        input_output_aliases={2: 0, 3: 1},
        compiler_params=_params(),
    )(meta_pad, g, h0, xn)


def _qk_prep(qkv, q_norm_t, k_norm_t, e_mat, tm):
    l_dim = qkv.shape[0]

    def body(x_ref, qg_ref, kg_ref, e_ref, q_ref, kf_ref, vf_ref):
        x = x_ref[...]
        q = _head_rms(x[:, :Q_W], qg_ref[...], e_ref) * (HEAD_DIM ** -0.5)
        q_ref[...] = q.astype(BF16)
        k = _head_rms(x[:, Q_W:Q_W + KV_W], kg_ref[...], e_ref)
        v = x[:, Q_W + KV_W:Q_W + 2 * KV_W]
        half = _lane_half((tm, BLK))
        for src, dst in ((k, kf_ref), (v, vf_ref)):
            for kv in range(N_KV_HEADS):
                blk = src[:, BLK * (kv // 2):BLK * (kv // 2) + BLK]
                swapped = pltpu.roll(blk, HEAD_DIM, axis=1)
                for e in range(2):
                    val = blk if kv % 2 == e else swapped
                    idx = 2 * kv + e
                    dst[:, BLK * idx:BLK * idx + BLK] = jnp.where(half == e, val, 0.0).astype(BF16)

    return pl.pallas_call(
        body, name="qk_prep", grid=(l_dim // tm,),
        in_specs=[_row_spec(tm, Q_W + 2 * KV_W), _full_spec((1, Q_W)), _full_spec((1, KV_W)),
                  _full_spec((256, 256))],
        out_specs=[_row_spec(tm, Q_W), _row_spec(tm, 8 * BLK), _row_spec(tm, 8 * BLK)],
        out_shape=[jax.ShapeDtypeStruct((l_dim, Q_W), BF16),
                   jax.ShapeDtypeStruct((l_dim, 8 * BLK), BF16),
                   jax.ShapeDtypeStruct((l_dim, 8 * BLK), BF16)],
        compiler_params=_params(),
    )(qkv, q_norm_t, k_norm_t, e_mat)


def _merge_fwd(attn, zab, gates, abn, sbn, tm):
    l_dim = attn.shape[0]

    def body(a_ref, z_ref, g_ref, an_ref, sn_ref, o_ref):
        z = z_ref[...]
        g = g_ref[...]
        ssm = z[:, :D_MODEL] * _sigmoid(z[:, D_MODEL:])
        merged = (_sigmoid(g[:, :D_MODEL]) * _rms(a_ref[...], an_ref[...])
                  + _sigmoid(g[:, D_MODEL:]) * _rms(ssm, sn_ref[...]))
        o_ref[...] = merged.astype(BF16)

    return pl.pallas_call(
        body, name="merge_fwd", grid=(l_dim // tm,),
        in_specs=[_row_spec(tm, D_MODEL), _row_spec(tm, 2 * D_MODEL), _row_spec(tm, 2 * D_MODEL),
                  _full_spec((1, D_MODEL)), _full_spec((1, D_MODEL))],
        out_specs=_row_spec(tm, D_MODEL),
        out_shape=jax.ShapeDtypeStruct((l_dim, D_MODEL), BF16),
        compiler_params=_params(),
    )(attn, zab, gates, abn, sbn)


def _merge_bwd(attn, zab, gates, abn, sbn, dmerged, tm):
    l_dim = attn.shape[0]

    def body(a_ref, z_ref, g_ref, an_ref, sn_ref, dm_ref, da_ref, dz_ref, dg_ref, dan_ref, dsn_ref):
        @pl.when(pl.program_id(0) == 0)
        def _():
            dan_ref[...] = jnp.zeros_like(dan_ref)
            dsn_ref[...] = jnp.zeros_like(dsn_ref)

        z = z_ref[...]
        g = g_ref[...]
        dm = dm_ref[...]
        attn_v = a_ref[...]
        za, zb = z[:, :D_MODEL], z[:, D_MODEL:]
        sb = _sigmoid(zb)
        ssm = za * sb
        s_ga, s_gs = _sigmoid(g[:, :D_MODEL]), _sigmoid(g[:, D_MODEL:])
        a_n = _rms(attn_v, an_ref[...])
        s_n = _rms(ssm, sn_ref[...])
        dg_ref[:, :D_MODEL] = (dm * a_n * s_ga * (1.0 - s_ga)).astype(BF16)
        dg_ref[:, D_MODEL:] = (dm * s_n * s_gs * (1.0 - s_gs)).astype(BF16)
        dattn, dan = _rms_bwd(attn_v, an_ref[...], dm * s_ga)
        dssm, dsn = _rms_bwd(ssm, sn_ref[...], dm * s_gs)
        da_ref[...] = dattn
        dz_ref[:, :D_MODEL] = (dssm * sb).astype(BF16)
        dz_ref[:, D_MODEL:] = (dssm * za * sb * (1.0 - sb)).astype(BF16)
        dan_ref[...] += dan
        dsn_ref[...] += dsn

    return pl.pallas_call(
        body, name="merge_bwd", grid=(l_dim // tm,),
        in_specs=[_row_spec(tm, D_MODEL), _row_spec(tm, 2 * D_MODEL), _row_spec(tm, 2 * D_MODEL),
                  _full_spec((1, D_MODEL)), _full_spec((1, D_MODEL)), _row_spec(tm, D_MODEL)],
        out_specs=[_row_spec(tm, D_MODEL), _row_spec(tm, 2 * D_MODEL), _row_spec(tm, 2 * D_MODEL),
                   _full_spec((1, D_MODEL)), _full_spec((1, D_MODEL))],
        out_shape=[jax.ShapeDtypeStruct((l_dim, D_MODEL), F32),
                   jax.ShapeDtypeStruct((l_dim, 2 * D_MODEL), BF16),
                   jax.ShapeDtypeStruct((l_dim, 2 * D_MODEL), BF16),
                   jax.ShapeDtypeStruct((1, D_MODEL), F32), jax.ShapeDtypeStruct((1, D_MODEL), F32)],
        compiler_params=_params(),
    )(attn, zab, gates, abn, sbn, dmerged)


FF_TILE = D_FF // 2


def _ffn_in_swiglu(hn, wt_ffn_in, tm):
    l_dim = hn.shape[0]
    nt = (((1,), (1,)), ((), ()))

    def body(a_ref, w_ref, gu_ref, act_ref):
        r = lax.dot_general(a_ref[...], w_ref[...], nt, preferred_element_type=F32)
        gate, up = r[:, :FF_TILE], r[:, FF_TILE:]
        gu_ref[...] = r.astype(BF16)
        act_ref[...] = (gate * _sigmoid(gate) * up).astype(BF16)

    return pl.pallas_call(
        body, name="ffn_in_swiglu", grid=(l_dim // tm, 2),
        in_specs=[pl.BlockSpec((tm, D_MODEL), lambda i, j: (i, 0)),
                  pl.BlockSpec((2 * FF_TILE, D_MODEL), lambda i, j: (j, 0))],
        out_specs=[pl.BlockSpec((tm, 2 * FF_TILE), lambda i, j: (i, j)),
                   pl.BlockSpec((tm, FF_TILE), lambda i, j: (i, j))],
        out_shape=[jax.ShapeDtypeStruct((l_dim, 2 * D_FF), BF16), jax.ShapeDtypeStruct((l_dim, D_FF), BF16)],
        compiler_params=_params(dimension_semantics=("parallel", "parallel")),
    )(hn, wt_ffn_in)


def _d_act_swiglu(dh2_b, w_ffn_out, gu, tm):
    l_dim = dh2_b.shape[0]
    nt = (((1,), (1,)), ((), ()))

    def body(d_ref, w_ref, gu_ref, o_ref):
        d = lax.dot_general(d_ref[...], w_ref[...], nt, preferred_element_type=F32)
        gate = gu_ref[:, :FF_TILE].astype(F32)
        up = gu_ref[:, FF_TILE:].astype(F32)
        s = _sigmoid(gate)
        o_ref[:, :FF_TILE] = (d * up * (s * (1.0 + gate * (1.0 - s)))).astype(BF16)
        o_ref[:, FF_TILE:] = (d * (gate * s)).astype(BF16)

    return pl.pallas_call(
        body, name="d_act_swiglu", grid=(l_dim // tm, 2),
        in_specs=[pl.BlockSpec((tm, D_MODEL), lambda i, j: (i, 0)),
                  pl.BlockSpec((FF_TILE, D_MODEL), lambda i, j: (j, 0)),
                  pl.BlockSpec((tm, 2 * FF_TILE), lambda i, j: (i, j))],
        out_specs=pl.BlockSpec((tm, 2 * FF_TILE), lambda i, j: (i, j)),
        out_shape=jax.ShapeDtypeStruct((l_dim, 2 * D_FF), BF16),
        compiler_params=_params(dimension_semantics=("parallel", "parallel")),
    )(dh2_b, w_ffn_out, gu)


def _loss_grad(h2, target2d):
    l_dim = h2.shape[0]
    nb = l_dim // BLK

    def body(h_ref, t_ref, d_ref, db_ref, loss_ref):
        i = pl.program_id(0)

        @pl.when(i == 0)
        def _():
            loss_ref[...] = jnp.zeros_like(loss_ref)
            d_ref[...] = jnp.zeros_like(d_ref)
            db_ref[...] = jnp.zeros_like(db_ref)

        @pl.when(i > 0)
        def _():
            err = h_ref[...] - t_ref[...]
            d = err * (1.0 / D_MODEL)
            d_ref[...] = d
            db_ref[...] = d.astype(BF16)
            loss_ref[...] += 0.5 * jnp.sum(jnp.mean(err * err, axis=-1, keepdims=True), axis=0, keepdims=True)

    return pl.pallas_call(
        body, name="loss_grad", grid=(nb,),
        in_specs=[_row_spec(BLK, D_MODEL), _row_spec(BLK, D_MODEL, lambda i: jnp.maximum(i - 1, 0))],
        out_specs=[_row_spec(BLK, D_MODEL), _row_spec(BLK, D_MODEL), _full_spec((1, 1))],
        out_shape=[jax.ShapeDtypeStruct((l_dim, D_MODEL), F32), jax.ShapeDtypeStruct((l_dim, D_MODEL), BF16),
                   jax.ShapeDtypeStruct((1, 1), F32)],
        compiler_params=_params(),
    )(h2, target2d)


def _norm_bwd_res(name, h, g, dy, dres, tm):
    l_dim = h.shape[0]

    def body(h_ref, g_ref, dy_ref, dr_ref, o_ref, ob_ref, dg_ref):
        @pl.when(pl.program_id(0) == 0)
        def _():
            dg_ref[...] = jnp.zeros_like(dg_ref)

        dx, dg = _rms_bwd(h_ref[...], g_ref[...], dy_ref[...])
        out = dr_ref[...] + dx
        o_ref[...] = out
        ob_ref[...] = out.astype(BF16)
        dg_ref[...] += dg

    return pl.pallas_call(
        body, name=name, grid=(l_dim // tm,),
        in_specs=[_row_spec(tm, D_MODEL), _full_spec((1, D_MODEL)), _row_spec(tm, D_MODEL), _row_spec(tm, D_MODEL)],
        out_specs=[_row_spec(tm, D_MODEL), _row_spec(tm, D_MODEL), _full_spec((1, D_MODEL))],
        out_shape=[jax.ShapeDtypeStruct((l_dim, D_MODEL), F32), jax.ShapeDtypeStruct((l_dim, D_MODEL), BF16),
                   jax.ShapeDtypeStruct((1, D_MODEL), F32)],
        compiler_params=_params(),
    )(h, g, dy, dres)


def _final_bwd(h0, g, dxn, dh1):
    l_dim = h0.shape[0]
    nb = l_dim // BLK

    def body(h_ref, g_ref, dy_ref, dr_ref, gx_ref, gm_ref, dg_ref):
        i = pl.program_id(0)

        @pl.when(i == 0)
        def _():
            dg_ref[...] = jnp.zeros_like(dg_ref)

        dx, dg = _rms_bwd(h_ref[...], g_ref[...], dy_ref[...])
        out = dr_ref[...] + dx
        dg_ref[...] += dg

        @pl.when(i == 0)
        def _():
            gm_ref[...] = out

        @pl.when(i > 0)
        def _():
            gx_ref[...] = out

    return pl.pallas_call(
        body, name="final_bwd", grid=(nb,),
        in_specs=[_row_spec(BLK, D_MODEL), _full_spec((1, D_MODEL)), _row_spec(BLK, D_MODEL),
                  _row_spec(BLK, D_MODEL)],
        out_specs=[_row_spec(BLK, D_MODEL, lambda i: jnp.maximum(i - 1, 0)), _full_spec((BLK, D_MODEL)),
                   _full_spec((1, D_MODEL))],
        out_shape=[jax.ShapeDtypeStruct((l_dim - BLK, D_MODEL), F32), jax.ShapeDtypeStruct((BLK, D_MODEL), F32),
                   jax.ShapeDtypeStruct((1, D_MODEL), F32)],
        compiler_params=_params(),
    )(h0, g, dxn, dh1)


def _attn_valid(n):
    shape = (2 * BLK, 3 * BLK)
    qi = lax.broadcasted_iota(jnp.int32, shape, 0) & (BLK - 1)
    col = lax.broadcasted_iota(jnp.int32, shape, 1)
    kj = col & (BLK - 1)
    part = col >> 7
    nn = jnp.zeros(shape, jnp.int32) + n
    meta_ok = (part == 0) & (kj >= PAD) & (nn >= 1)
    prev_ok = (part == 1) & (kj > qi) & (nn >= 2)
    cur_ok = (part == 2) & (kj <= qi) & ((nn >= 1) | (kj >= PAD))
    return meta_ok | prev_ok | cur_ok


def _attn_probs(q_ref, kwin, sk_ref, valid, kv, e):
    qs = jnp.concatenate([q_ref[:, BLK * (2 * kv):BLK * (2 * kv) + BLK],
                          q_ref[:, BLK * (2 * kv + 1):BLK * (2 * kv + 1) + BLK]], axis=0)
    s = lax.dot_general(qs, kwin, (((1,), (1,)), ((), ())), preferred_element_type=F32)
    s = jnp.where(valid, s, NEG)
    h0 = 4 * kv + e
    row = lax.broadcasted_iota(jnp.int32, (2 * BLK, 1), 0)
    sink = jnp.where(row < BLK, sk_ref[:, h0:h0 + 1], sk_ref[:, h0 + 2:h0 + 3])
    m = jnp.maximum(jnp.max(s, axis=-1, keepdims=True), sink)
    ex = jnp.exp(s - m)
    es = jnp.exp(sink - m)
    inv = 1.0 / (jnp.sum(ex, axis=-1, keepdims=True) + es)
    return qs, ex * inv, es * inv


def _attn_specs(nb):
    prev = lambda i: jnp.maximum(i - 1, 0)
    zero = lambda i: 0
    kv_specs = [_row_spec(BLK, 8 * BLK, zero), _row_spec(BLK, 8 * BLK, prev), _row_spec(BLK, 8 * BLK)]
    return kv_specs


def _attn_fwd(qn, kf, vf, sinks):
    l_dim = qn.shape[0]
    nb = l_dim // BLK

    def body(q_ref, km_ref, kp_ref, kc_ref, vm_ref, vp_ref, vc_ref, sk_ref, o_ref):
        valid = _attn_valid(pl.program_id(0))
        for kv in range(N_KV_HEADS):
            outs = []
            for e in range(2):
                sl = slice(BLK * (2 * kv + e), BLK * (2 * kv + e) + BLK)
                kwin = jnp.concatenate([km_ref[:, sl], kp_ref[:, sl], kc_ref[:, sl]], axis=0)
                vwin = jnp.concatenate([vm_ref[:, sl], vp_ref[:, sl], vc_ref[:, sl]], axis=0)
                _, p, _ = _attn_probs(q_ref, kwin, sk_ref, valid, kv, e)
                outs.append(jnp.dot(p.astype(BF16), vwin, preferred_element_type=F32))
            o = outs[0] + outs[1]
            o_ref[:, BLK * (2 * kv):BLK * (2 * kv) + BLK] = o[:BLK]
            o_ref[:, BLK * (2 * kv + 1):BLK * (2 * kv + 1) + BLK] = o[BLK:]

    kv_specs = _attn_specs(nb)
    return pl.pallas_call(
        body, name="attn_fwd", grid=(nb,),
        in_specs=[_row_spec(BLK, Q_W)] + kv_specs + kv_specs + [_full_spec((1, N_Q_HEADS))],
        out_specs=_row_spec(BLK, Q_W),
        out_shape=jax.ShapeDtypeStruct((l_dim, Q_W), F32),
        compiler_params=_params(),
    )(qn, kf, kf, kf, vf, vf, vf, sinks)


def _attn_bwd(qn, kf, vf, sinks, attn, dattn):
    l_dim = qn.shape[0]
    nb = l_dim // BLK
    wide = 8 * BLK

    def body(q_ref, km_ref, kp_ref, kc_ref, vm_ref, vp_ref, vc_ref, sk_ref, o_ref, do_ref,
             dq_ref, dkc_ref, dkp_ref, dkm_ref, dvc_ref, dvp_ref, dvm_ref, dsk_ref):
        @pl.when(pl.program_id(0) == 0)
        def _():
            dkm_ref[...] = jnp.zeros_like(dkm_ref)
            dvm_ref[...] = jnp.zeros_like(dvm_ref)
            dsk_ref[...] = jnp.zeros_like(dsk_ref)

        valid = _attn_valid(pl.program_id(0))
        half = _lane_half((BLK, BLK))
        lane16 = lax.broadcasted_iota(jnp.int32, (1, N_Q_HEADS), 1)
        dsk = jnp.zeros((1, N_Q_HEADS), F32)
        for kv in range(N_KV_HEADS):
            j0, j1 = 2 * kv, 2 * kv + 1
            do0 = do_ref[:, BLK * j0:BLK * j0 + BLK]
            do1 = do_ref[:, BLK * j1:BLK * j1 + BLK]
            prod0 = do0 * o_ref[:, BLK * j0:BLK * j0 + BLK]
            prod1 = do1 * o_ref[:, BLK * j1:BLK * j1 + BLK]
            dos = jnp.concatenate([do0, do1], axis=0).astype(BF16)
            dqs = []
            for e in range(2):
                sl = slice(BLK * (2 * kv + e), BLK * (2 * kv + e) + BLK)
                kwin = jnp.concatenate([km_ref[:, sl], kp_ref[:, sl], kc_ref[:, sl]], axis=0)
                vwin = jnp.concatenate([vm_ref[:, sl], vp_ref[:, sl], vc_ref[:, sl]], axis=0)
                qs, p, p_sink = _attn_probs(q_ref, kwin, sk_ref, valid, kv, e)
                delta = jnp.concatenate(
                    [jnp.sum(jnp.where(half == e, prod0, 0.0), axis=-1, keepdims=True),
                     jnp.sum(jnp.where(half == e, prod1, 0.0), axis=-1, keepdims=True)], axis=0)
                dp = lax.dot_general(dos, vwin, (((1,), (1,)), ((), ())), preferred_element_type=F32)
                ds = (p * (dp - delta)).astype(BF16)
                pb = p.astype(BF16)
                dqs.append(jnp.dot(ds, kwin, preferred_element_type=F32))
                dk = lax.dot_general(ds, qs, (((0,), (0,)), ((), ())), preferred_element_type=F32)
                dv = lax.dot_general(pb, dos, (((0,), (0,)), ((), ())), preferred_element_type=F32)
                dkm_ref[:, sl] += dk[:BLK]
                dkp_ref[:, sl] = dk[BLK:2 * BLK]
                dkc_ref[:, sl] = dk[2 * BLK:]
                dvm_ref[:, sl] += dv[:BLK]
                dvp_ref[:, sl] = dv[BLK:2 * BLK]
                dvc_ref[:, sl] = dv[2 * BLK:]
                sink_g = -(p_sink * delta)
                g_lo = jnp.sum(sink_g[:BLK], axis=0, keepdims=True)
                g_hi = jnp.sum(sink_g[BLK:], axis=0, keepdims=True)
                dsk = dsk + jnp.where(lane16 == 4 * kv + e, g_lo, 0.0) + jnp.where(lane16 == 4 * kv + 2 + e, g_hi, 0.0)
            dq = jnp.where(jnp.concatenate([half, half], axis=0) == 0, dqs[0], dqs[1])
            dq_ref[:, BLK * j0:BLK * j0 + BLK] = dq[:BLK]
            dq_ref[:, BLK * j1:BLK * j1 + BLK] = dq[BLK:]
        dsk_ref[...] += dsk

    kv_specs = _attn_specs(nb)
    row_wide = _row_spec(BLK, wide)
    acc_wide = _full_spec((BLK, wide))
    big = jax.ShapeDtypeStruct((l_dim, wide), F32)
    return pl.pallas_call(
        body, name="attn_bwd", grid=(nb,),
        in_specs=[_row_spec(BLK, Q_W)] + kv_specs + kv_specs
        + [_full_spec((1, N_Q_HEADS)), _row_spec(BLK, Q_W), _row_spec(BLK, Q_W)],
        out_specs=[_row_spec(BLK, Q_W), row_wide, row_wide, acc_wide, row_wide, row_wide, acc_wide,
                   _full_spec((1, N_Q_HEADS))],
        out_shape=[jax.ShapeDtypeStruct((l_dim, Q_W), F32), big, big, jax.ShapeDtypeStruct((BLK, wide), F32),
                   big, big, jax.ShapeDtypeStruct((BLK, wide), F32), jax.ShapeDtypeStruct((1, N_Q_HEADS), F32)],
        compiler_params=_params(),
    )(qn, kf, kf, kf, vf, vf, vf, sinks, attn, dattn)


def _qk_bwd(qkv, q_norm_t, k_norm_t, e_mat, dq, dkc, dkp, dkm, dvc, dvp, dvm):
    l_dim = qkv.shape[0]
    nb = l_dim // BLK
    wide = 8 * BLK

    def fold(x):
        half = _lane_half((BLK, BLK))
        blocks = []
        for kb in range(2):
            t = []
            for kv in (2 * kb, 2 * kb + 1):
                own = kv % 2
                a = x[:, BLK * (2 * kv + own):BLK * (2 * kv + own) + BLK]
                b = pltpu.roll(x[:, BLK * (2 * kv + 1 - own):BLK * (2 * kv + 1 - own) + BLK], HEAD_DIM, axis=1)
                t.append(a + b)
            blocks.append(jnp.where(half == 0, t[0], t[1]))
        return jnp.concatenate(blocks, axis=1)

    def body(x_ref, qg_ref, kg_ref, e_ref, dq_ref, dkc_ref, dkp_ref, dkm_ref, dvc_ref, dvp_ref, dvm_ref,
             o_ref, dqg_ref, dkg_ref):
        i = pl.program_id(0)

        @pl.when(i == 0)
        def _():
            dqg_ref[...] = jnp.zeros_like(dqg_ref)
            dkg_ref[...] = jnp.zeros_like(dkg_ref)

        first = jnp.where(i == 0, 1.0, 0.0)
        not_last = jnp.where(i < nb - 1, 1.0, 0.0)
        dk_x = dkc_ref[...] + not_last * dkp_ref[...] + first * dkm_ref[...]
        dv_x = dvc_ref[...] + not_last * dvp_ref[...] + first * dvm_ref[...]
        x = x_ref[...]
        dqx, dqg = _head_rms_bwd(x[:, :Q_W], qg_ref[...], dq_ref[...] * (HEAD_DIM ** -0.5), e_ref)
        dkx, dkg = _head_rms_bwd(x[:, Q_W:Q_W + KV_W], kg_ref[...], fold(dk_x), e_ref)
        o_ref[:, :Q_W] = dqx.astype(BF16)
        o_ref[:, Q_W:Q_W + KV_W] = dkx.astype(BF16)
        o_ref[:, Q_W + KV_W:] = fold(dv_x).astype(BF16)
        dqg_ref[...] += dqg
        dkg_ref[...] += dkg

    nxt = lambda i: jnp.minimum(i + 1, nb - 1)
    row_wide = _row_spec(BLK, wide)
    nxt_wide = _row_spec(BLK, wide, nxt)
    acc_wide = _full_spec((BLK, wide))
    return pl.pallas_call(
        body, name="qk_bwd", grid=(nb,),
        in_specs=[_row_spec(BLK, Q_W + 2 * KV_W), _full_spec((1, Q_W)), _full_spec((1, KV_W)),
                  _full_spec((256, 256)), _row_spec(BLK, Q_W),
                  row_wide, nxt_wide, acc_wide, row_wide, nxt_wide, acc_wide],
        out_specs=[_row_spec(BLK, Q_W + 2 * KV_W), _full_spec((1, Q_W)), _full_spec((1, KV_W))],
        out_shape=[jax.ShapeDtypeStruct((l_dim, Q_W + 2 * KV_W), BF16),
                   jax.ShapeDtypeStruct((1, Q_W), F32), jax.ShapeDtypeStruct((1, KV_W), F32)],
        compiler_params=_params(),
    )(qkv, q_norm_t, k_norm_t, e_mat, dq, dkc, dkp, dkm, dvc, dvp, dvm)


GRP = 8


def _strided(r, g):
    return pl.ds(r, g, stride=GRP)


def _slab(ref, base, r, g):
    return jnp.concatenate([ref[base + i, _strided(r, g), :] for i in range(LB_KB)], axis=1)


def _slab_store(ref, base, r, g, val):
    for i in range(LB_KB):
        ref[base + i, _strided(r, g), :] = val[:, BLK * i:BLK * i + BLK]


def _group_totals(xr_ref, xi_ref, base, ar, ai, reverse):
    g = xr_ref.shape[1] // GRP
    sr = si = None
    for r in (range(GRP - 1, -1, -1) if reverse else range(GRP)):
        xr, xi = _slab(xr_ref, base, r, g), _slab(xi_ref, base, r, g)
        if sr is not None:
            xr, xi = xr + ar * sr - ai * si, xi + ar * si + ai * sr
        sr, si = xr, xi
    return sr, si


def _carry_scan(tr, ti, sqr_ref, sqi_ref, lanes, sign, reverse):
    g = tr.shape[0]
    row = lax.broadcasted_iota(jnp.int32, tr.shape, 0)
    idx, s = 0, 1
    while s < g:
        ar = sqr_ref[idx:idx + 1, lanes]
        ai = sign * sqi_ref[idx:idx + 1, lanes]
        shift, keep = (g - s, row < g - s) if reverse else (s, row >= s)
        pr = jnp.where(keep, pltpu.roll(tr, shift, axis=0), 0.0)
        pi = jnp.where(keep, pltpu.roll(ti, shift, axis=0), 0.0)
        tr, ti = tr + ar * pr - ai * pi, ti + ar * pi + ai * pr
        idx, s = idx + 1, 2 * s
    return tr, ti


def _ssm_fwd(u, wb_re, wb_im, wc_re, wc_im, d_skip, tabs):
    l_dim = u.shape[0]
    nb = l_dim // BLK
    g = BLK // GRP

    def body(u_ref, wbr_ref, wbi_ref, wcr_ref, wci_ref, d_ref, a1r_ref, a1i_ref, sqr_ref, sqi_ref,
             seqr_ref, seqi_ref, y_ref, z_ref, sr_ref, si_ref, cr_ref, ci_ref, xr_ref, xi_ref):
        @pl.when(pl.program_id(0) == 0)
        def _():
            cr_ref[...] = jnp.zeros_like(cr_ref)
            ci_ref[...] = jnp.zeros_like(ci_ref)

        row = lax.broadcasted_iota(jnp.int32, (g, ST_KB), 0)
        for kb in range(SSM_KB):
            ch = slice(BLK * kb, BLK * kb + BLK)
            lanes = slice(ST_KB * kb, ST_KB * kb + ST_KB)
            u_kb = u_ref[:, ch]
            ub = u_kb.astype(BF16)
            xr = jnp.dot(ub, wbr_ref[kb], preferred_element_type=F32)
            xi = jnp.dot(ub, wbi_ref[kb], preferred_element_type=F32)
            base = LB_KB * kb
            for i in range(LB_KB):
                xr_ref[base + i] = xr[:, BLK * i:BLK * i + BLK]
                xi_ref[base + i] = xi[:, BLK * i:BLK * i + BLK]
            ar, ai = a1r_ref[0:1, lanes], a1i_ref[0:1, lanes]
            tr, ti = _group_totals(xr_ref, xi_ref, base, ar, ai, reverse=False)
            tr, ti = _carry_scan(tr, ti, sqr_ref, sqi_ref, lanes, 1.0, reverse=False)
            cin_r, cin_i = cr_ref[0:1, lanes], ci_ref[0:1, lanes]
            qr, qi = seqr_ref[:, lanes], seqi_ref[:, lanes]
            tr, ti = tr + qr * cin_r - qi * cin_i, ti + qr * cin_i + qi * cin_r
            cr_ref[0:1, lanes] = jnp.sum(jnp.where(row == g - 1, tr, 0.0), axis=0, keepdims=True)
            ci_ref[0:1, lanes] = jnp.sum(jnp.where(row == g - 1, ti, 0.0), axis=0, keepdims=True)
            pr = jnp.where(row == 0, cin_r, pltpu.roll(tr, 1, axis=0))
            pi = jnp.where(row == 0, cin_i, pltpu.roll(ti, 1, axis=0))
            for r in range(GRP):
                pr, pi = (_slab(xr_ref, base, r, g) + ar * pr - ai * pi,
                          _slab(xi_ref, base, r, g) + ar * pi + ai * pr)
                _slab_store(sr_ref, base, r, g, pr)
                _slab_store(si_ref, base, r, g, pi)
            s_r = jnp.concatenate([sr_ref[LB_KB * kb + i] for i in range(LB_KB)], axis=1)
            s_i = jnp.concatenate([si_ref[LB_KB * kb + i] for i in range(LB_KB)], axis=1)
            y = (jnp.dot(s_r.astype(BF16), wcr_ref[kb], preferred_element_type=F32)
                 - jnp.dot(s_i.astype(BF16), wci_ref[kb], preferred_element_type=F32)
                 + d_ref[:, ch] * u_kb)
            y_ref[:, ch] = y
            z_ref[:, ch] = _gelu(y).astype(BF16)

    wb_spec = _full_spec((SSM_KB, BLK, ST_KB))
    wc_spec = _full_spec((SSM_KB, ST_KB, BLK))
    tab_specs = [_full_spec(t.shape) for t in tabs[:6]]
    state_spec = pl.BlockSpec((N_LB, BLK, BLK), lambda i: (0, i, 0))
    state_shape = jax.ShapeDtypeStruct((N_LB, l_dim, BLK), F32)
    return pl.pallas_call(
        body, name="ssm_fwd", grid=(nb,),
        in_specs=[_row_spec(BLK, D_MODEL), wb_spec, wb_spec, wc_spec, wc_spec, _full_spec((1, D_MODEL))]
        + tab_specs,
        out_specs=[_row_spec(BLK, D_MODEL), _row_spec(BLK, D_MODEL), state_spec, state_spec],
        out_shape=[jax.ShapeDtypeStruct((l_dim, D_MODEL), F32), jax.ShapeDtypeStruct((l_dim, D_MODEL), BF16),
                   state_shape, state_shape],
        scratch_shapes=[pltpu.VMEM((8, N_STATE), F32), pltpu.VMEM((8, N_STATE), F32),
                        pltpu.VMEM((N_LB, BLK, BLK), F32), pltpu.VMEM((N_LB, BLK, BLK), F32)],
        compiler_params=_params(),
    )(u, wb_re, wb_im, wc_re, wc_im, d_skip, *tabs[:6])


def _ssm_bwd(dz, y, u, s_re, s_im, wb_re, wb_im, wc_re, wc_im, d_skip, tabs):
    l_dim = u.shape[0]
    nb = l_dim // BLK
    g = BLK // GRP

    def body(dz_ref, y_ref, u_ref, sr_ref, si_ref, wbr_ref, wbi_ref, wcr_ref, wci_ref, d_ref,
             a1r_ref, a1i_ref, sqr_ref, sqi_ref, revr_ref, revi_ref,
             du_ref, dd_ref, dar_ref, dai_ref, dwbr_ref, dwbi_ref, dwcr_ref, dwci_ref,
             cr_ref, ci_ref, gr_ref, gi_ref):
        @pl.when(pl.program_id(0) == 0)
        def _():
            for r in (cr_ref, ci_ref, dd_ref, dar_ref, dai_ref, dwbr_ref, dwbi_ref, dwcr_ref, dwci_ref):
                r[...] = jnp.zeros_like(r)

        tn = (((0,), (0,)), ((), ()))
        nt = (((1,), (1,)), ((), ()))
        row = lax.broadcasted_iota(jnp.int32, (g, ST_KB), 0)
        for kb in range(SSM_KB):
            ch = slice(BLK * kb, BLK * kb + BLK)
            lanes = slice(ST_KB * kb, ST_KB * kb + ST_KB)
            u_kb = u_ref[:, ch]
            dy = dz_ref[:, ch] * _gelu_grad(y_ref[:, ch])
            dyb = dy.astype(BF16)
            ub = u_kb.astype(BF16)
            dd_ref[:, ch] += jnp.sum(dy * u_kb, axis=0, keepdims=True)
            ds_r = lax.dot_general(dyb, wcr_ref[kb], nt, preferred_element_type=F32)
            ds_i = -lax.dot_general(dyb, wci_ref[kb], nt, preferred_element_type=F32)
            base = LB_KB * kb
            for i in range(LB_KB):
                gr_ref[base + i] = ds_r[:, BLK * i:BLK * i + BLK]
                gi_ref[base + i] = ds_i[:, BLK * i:BLK * i + BLK]
            ar, ai = a1r_ref[0:1, lanes], -a1i_ref[0:1, lanes]
            tr, ti = _group_totals(gr_ref, gi_ref, base, ar, ai, reverse=True)
            tr, ti = _carry_scan(tr, ti, sqr_ref, sqi_ref, lanes, -1.0, reverse=True)
            cin_r, cin_i = cr_ref[0:1, lanes], ci_ref[0:1, lanes]
            qr, qi = revr_ref[:, lanes], -revi_ref[:, lanes]
            tr, ti = tr + qr * cin_r - qi * cin_i, ti + qr * cin_i + qi * cin_r
            cr_ref[0:1, lanes] = jnp.sum(jnp.where(row == 0, tr, 0.0), axis=0, keepdims=True)
            ci_ref[0:1, lanes] = jnp.sum(jnp.where(row == 0, ti, 0.0), axis=0, keepdims=True)
            nr = jnp.where(row == g - 1, cin_r, pltpu.roll(tr, g - 1, axis=0))
            ni = jnp.where(row == g - 1, cin_i, pltpu.roll(ti, g - 1, axis=0))
            acc_r = jnp.zeros((g, ST_KB), F32)
            acc_i = jnp.zeros((g, ST_KB), F32)
            for r in range(GRP - 1, -1, -1):
                s_r, s_i = _slab(sr_ref, base, r, g), _slab(si_ref, base, r, g)
                acc_r = acc_r + (nr * s_r + ni * s_i)
                acc_i = acc_i + (ni * s_r - nr * s_i)
                nr, ni = (_slab(gr_ref, base, r, g) + ar * nr - ai * ni,
                          _slab(gi_ref, base, r, g) + ar * ni + ai * nr)
                _slab_store(gr_ref, base, r, g, nr)
                _slab_store(gi_ref, base, r, g, ni)
            dar_ref[:, lanes] += jnp.sum(acc_r, axis=0, keepdims=True)
            dai_ref[:, lanes] += jnp.sum(acc_i, axis=0, keepdims=True)
            grb = jnp.concatenate([gr_ref[base + i] for i in range(LB_KB)], axis=1).astype(BF16)
            gib = jnp.concatenate([gi_ref[base + i] for i in range(LB_KB)], axis=1).astype(BF16)
            srb = jnp.concatenate([sr_ref[base + i] for i in range(LB_KB)], axis=1).astype(BF16)
            sib = jnp.concatenate([si_ref[base + i] for i in range(LB_KB)], axis=1).astype(BF16)
            du = (lax.dot_general(grb, wbr_ref[kb], nt, preferred_element_type=F32)
                  + lax.dot_general(gib, wbi_ref[kb], nt, preferred_element_type=F32)
                  + d_ref[:, ch] * dy)
            du_ref[:, ch] = du.astype(BF16)
            dwbr_ref[kb] += lax.dot_general(ub, grb, tn, preferred_element_type=F32)
            dwbi_ref[kb] += lax.dot_general(ub, gib, tn, preferred_element_type=F32)
            dwcr_ref[kb] += lax.dot_general(srb, dyb, tn, preferred_element_type=F32)
            dwci_ref[kb] -= lax.dot_general(sib, dyb, tn, preferred_element_type=F32)

    rev = lambda i: nb - 1 - i
    wb_spec = _full_spec((SSM_KB, BLK, ST_KB))
    wc_spec = _full_spec((SSM_KB, ST_KB, BLK))
    tab_in = [tabs[0], tabs[1], tabs[2], tabs[3], tabs[6], tabs[7]]
    tab_specs = [_full_spec(t.shape) for t in tab_in]
    vec = _full_spec((1, D_MODEL))
    svec = _full_spec((1, N_STATE))
    state_spec = pl.BlockSpec((N_LB, BLK, BLK), lambda i: (0, nb - 1 - i, 0))
    return pl.pallas_call(
        body, name="ssm_bwd", grid=(nb,),
        in_specs=[_row_spec(BLK, D_MODEL, rev), _row_spec(BLK, D_MODEL, rev), _row_spec(BLK, D_MODEL, rev),
                  state_spec, state_spec,
                  wb_spec, wb_spec, wc_spec, wc_spec, vec] + tab_specs,
        out_specs=[_row_spec(BLK, D_MODEL, rev), vec, svec, svec, wb_spec, wb_spec, wc_spec, wc_spec],
        out_shape=[jax.ShapeDtypeStruct((l_dim, D_MODEL), BF16), jax.ShapeDtypeStruct((1, D_MODEL), F32),
                   jax.ShapeDtypeStruct((1, N_STATE), F32), jax.ShapeDtypeStruct((1, N_STATE), F32),
                   jax.ShapeDtypeStruct((SSM_KB, BLK, ST_KB), F32), jax.ShapeDtypeStruct((SSM_KB, BLK, ST_KB), F32),
                   jax.ShapeDtypeStruct((SSM_KB, ST_KB, BLK), F32), jax.ShapeDtypeStruct((SSM_KB, ST_KB, BLK), F32)],
        scratch_shapes=[pltpu.VMEM((8, N_STATE), F32), pltpu.VMEM((8, N_STATE), F32),
                        pltpu.VMEM((N_LB, BLK, BLK), F32), pltpu.VMEM((N_LB, BLK, BLK), F32)],
        compiler_params=_params(),
    )(dz, y, u, s_re, s_im, wb_re, wb_im, wc_re, wc_im, d_skip, *tab_in)


def _discretize(lam_re, lam_im, log_dt, b_re, b_im):
    dt = jnp.exp(log_dt)[:, None]
    mag = jnp.exp(lam_re * dt)
    ar, ai = mag * jnp.cos(lam_im * dt), mag * jnp.sin(lam_im * dt)
    den = lam_re * lam_re + lam_im * lam_im
    nr, ni = ar - 1.0, ai
    fr, fi = (nr * lam_re + ni * lam_im) / den, (ni * lam_re - nr * lam_im) / den
    bbar_re = fr[..., None] * b_re - fi[..., None] * b_im
    bbar_im = fr[..., None] * b_im + fi[..., None] * b_re
    return ar, ai, bbar_re, bbar_im


def _block_diag_b(bbar):
    eye = jnp.eye(8, dtype=bbar.dtype)
    return jnp.einsum("kgpc,gh->kgchp", bbar.reshape(8, 8, SSM_STATE, SSM_GROUP_CH), eye).reshape(8, BLK, ST_KB)


def _block_diag_b_t(dwb):
    eye = jnp.eye(8, dtype=dwb.dtype)
    return jnp.einsum("kgchp,gh->kgpc", dwb.reshape(8, 8, SSM_GROUP_CH, 8, SSM_STATE), eye).reshape(
        SSM_GROUPS, SSM_STATE, SSM_GROUP_CH)


def _block_diag_c(c):
    eye = jnp.eye(8, dtype=c.dtype)
    return jnp.einsum("kgcp,gh->kgphc", c.reshape(8, 8, SSM_GROUP_CH, SSM_STATE), eye).reshape(8, ST_KB, BLK)


def _block_diag_c_t(dwc):
    eye = jnp.eye(8, dtype=dwc.dtype)
    return jnp.einsum("kgphc,gh->kgcp", dwc.reshape(8, 8, SSM_STATE, 8, SSM_GROUP_CH), eye).reshape(
        SSM_GROUPS, SSM_GROUP_CH, SSM_STATE)


def _powers(br, bi, n):
    pr, pi = br, bi
    cr, ci = br, bi
    while pr.shape[0] < n:
        pr, pi = (jnp.concatenate([pr, pr * cr - pi * ci], axis=0),
                  jnp.concatenate([pi, pr * ci + pi * cr], axis=0))
        cr, ci = cr * cr - ci * ci, 2.0 * cr * ci
    return pr[:n], pi[:n]


def _powers_desc(br, bi, n):
    pr, pi = br, bi
    cr, ci = br, bi
    while pr.shape[0] < n:
        pr, pi = (jnp.concatenate([pr * cr - pi * ci, pr], axis=0),
                  jnp.concatenate([pr * ci + pi * cr, pi], axis=0))
        cr, ci = cr * cr - ci * ci, 2.0 * cr * ci
    return pr, pi


def _power_tables(ar, ai):
    g = BLK // GRP
    a1r, a1i = _powers(ar, ai, GRP)
    seqr, seqi = _powers(a1r[GRP - 1:], a1i[GRP - 1:], g)
    revr, revi = _powers_desc(a1r[GRP - 1:], a1i[GRP - 1:], g)
    sq_r, sq_i = [seqr[0:1]], [seqi[0:1]]
    while len(sq_r) < 8:
        r, i = sq_r[-1], sq_i[-1]
        sq_r.append(r * r - i * i)
        sq_i.append(2.0 * r * i)
    sqr, sqi = jnp.concatenate(sq_r, axis=0), jnp.concatenate(sq_i, axis=0)
    return a1r, a1i, sqr, sqi, seqr, seqi, revr, revi


HBM_SPEC = pl.BlockSpec(memory_space=pltpu.HBM)
SEM_SPEC = pl.BlockSpec(memory_space=pltpu.SEMAPHORE)
DATAFLOW = pltpu.SideEffectType.DATAFLOW_SIDE_EFFECTING


def _plain_rows(p, m):
    return p * m


def _ffn_in_rows(p, m):
    return ((p & 3) >> 1) * (4 * m) + (p >> 2) * (2 * m) + (p & 1) * m


def _peer_copies(src_refs, land_refs, send_sems, recv_sems, chunked, row_fns):
    x, y, c = lax.axis_index("x"), lax.axis_index("y"), lax.axis_index("c")
    me = 4 * x + 2 * y + c
    copies = []
    for a, (src, land) in enumerate(zip(src_refs, land_refs)):
        m = land.shape[0] // N_DEV
        for k in range(N_DEV - 1):
            rel = k + 1
            bx, by, bc = (rel >> 2) & 1, (rel >> 1) & 1, rel & 1
            peer = (x + bx - 2 * x * bx, y + by - 2 * y * by, c + bc - 2 * c * bc)
            p_idx = 4 * peer[0] + 2 * peer[1] + peer[2]
            copies.append(pltpu.make_async_remote_copy(
                src_ref=src.at[pl.ds(row_fns[a](p_idx, m), m), :] if chunked else src,
                dst_ref=land.at[pl.ds(me * m if chunked else row_fns[a](me, m), m), :],
                send_sem=send_sems[a * (N_DEV - 1) + k], recv_sem=recv_sems[a * (N_DEV - 1) + k],
                device_id=peer, device_id_type=MESH))
    return copies


def _send_start(name, srcs, lands, chunked, row_fns=None):
    n = len(srcs)
    ns = n * (N_DEV - 1)
    row_fns = row_fns or [_plain_rows] * n

    def body(*refs):
        src_refs, land_refs = refs[:n], refs[n:2 * n]
        send_sems, recv_sems = refs[2 * n:2 * n + ns], refs[2 * n + ns:2 * n + 2 * ns]
        token = refs[-1]
        for cp in _peer_copies(src_refs, land_refs, send_sems, recv_sems, chunked, row_fns):
            cp.start()
        token[...] = jnp.zeros_like(token)

    ins = [pltpu.with_memory_space_constraint(a, pltpu.HBM) for a in list(srcs) + list(lands)]
    out = pl.pallas_call(
        body, name=name,
        in_specs=[HBM_SPEC] * (2 * n),
        out_specs=[SEM_SPEC] * (2 * ns) + [HBM_SPEC] * (2 * n) + [pl.BlockSpec(memory_space=pltpu.VMEM)],
        out_shape=[pltpu.SemaphoreType.DMA(())] * (2 * ns)
        + [pltpu.HBM(a.shape, a.dtype) for a in list(srcs) + list(lands)]
        + [jax.ShapeDtypeStruct((8, BLK), F32)],
        input_output_aliases={i: i + 2 * ns for i in range(2 * n)},
        compiler_params=pltpu.CompilerParams(has_side_effects=DATAFLOW),
    )(*ins)
    return out[:ns], out[ns:2 * ns], out[2 * ns:2 * ns + n], out[2 * ns + n:2 * ns + 2 * n], out[-1]


def _send_wait(name, send_sems, recv_sems, srcs, lands, after, chunked, row_fns=None):
    n = len(srcs)
    ns = n * (N_DEV - 1)
    row_fns = row_fns or [_plain_rows] * n

    def body(*refs):
        src_refs, land_refs = refs[:n], refs[n:2 * n]
        s_sems, r_sems = refs[2 * n:2 * n + ns], refs[2 * n + ns:2 * n + 2 * ns]
        copies = _peer_copies(src_refs, land_refs, s_sems, r_sems, chunked, row_fns)
        for cp in copies:
            cp.wait_send()
        for cp in copies:
            cp.wait_recv()

    out = pl.pallas_call(
        body, name=name,
        in_specs=[HBM_SPEC] * (2 * n) + [SEM_SPEC] * (2 * ns) + [pl.BlockSpec(memory_space=pl.ANY)],
        out_specs=[HBM_SPEC] * (2 * n),
        out_shape=[pltpu.HBM(a.shape, a.dtype) for a in list(srcs) + list(lands)],
        input_output_aliases={i: i for i in range(2 * n)},
        compiler_params=pltpu.CompilerParams(has_side_effects=DATAFLOW),
    )(*srcs, *lands, *send_sems, *recv_sems, after)
    return out[n:]


def _sum_slots(name, recv, own):
    m, ncol = own.shape
    tr = m // 2 if (m // 2) % 16 == 0 else m
    g = m // tr

    def body(*refs):
        slots, own_ref, o_ref = refs[:N_DEV], refs[N_DEV], refs[N_DEV + 1]
        me = _my_index()
        tot = None
        for s in range(N_DEV):
            v = jnp.where(me == s, own_ref[...], slots[s][...].astype(F32))
            tot = v if tot is None else tot + v
        o_ref[...] = tot

    def slot_spec(s):
        return pl.BlockSpec((tr, ncol), lambda i: (s * g + i, 0))

    return pl.pallas_call(
        body, name=name, grid=(g,),
        in_specs=[slot_spec(s) for s in range(N_DEV)] + [pl.BlockSpec((tr, ncol), lambda i: (i, 0))],
        out_specs=pl.BlockSpec((tr, ncol), lambda i: (i, 0)),
        out_shape=jax.ShapeDtypeStruct((m, ncol), F32),
        compiler_params=_params(),
    )(*([recv] * N_DEV), own)


def _sum_gathered(name, gathered, rows):
    tr = _pick(rows, 512, 8)
    g = rows // tr

    def body(*refs):
        o_ref = refs[N_DEV]
        tot = refs[0][...]
        for s in range(1, N_DEV):
            tot = tot + refs[s][...]
        o_ref[...] = tot

    return pl.pallas_call(
        body, name=name, grid=(g,),
        in_specs=[pl.BlockSpec((tr, BLK), (lambda i, s=s: (s * g + i, 0))) for s in range(N_DEV)],
        out_specs=pl.BlockSpec((tr, BLK), lambda i: (i, 0)),
        out_shape=jax.ShapeDtypeStruct((rows, BLK), F32),
        compiler_params=_params(),
    )(*([gathered] * N_DEV))


def _adamw(name, w, g, m, v):
    r, c = w.shape
    tr = _pick(r, 256, 8) if r % 8 == 0 else r
    c1 = 1.0 - ADAM_B1 ** ADAM_STEP
    c2 = 1.0 - ADAM_B2 ** ADAM_STEP

    def body(w_ref, g_ref, m_ref, v_ref, d_ref, nm_ref, nv_ref):
        gv = g_ref[...]
        nm = ADAM_B1 * m_ref[...] + (1.0 - ADAM_B1) * gv
        nv = ADAM_B2 * v_ref[...] + (1.0 - ADAM_B2) * (gv * gv)
        m_hat = nm / c1
        v_hat = nv / c2
        d_ref[...] = -ADAM_LR * (m_hat / (jnp.sqrt(v_hat) + ADAM_EPS) + ADAM_WD * w_ref[...])
        nm_ref[...] = nm
        nv_ref[...] = nv

    spec = pl.BlockSpec((tr, c), lambda i: (i, 0))
    shape = jax.ShapeDtypeStruct((r, c), F32)
    return pl.pallas_call(
        body, name=name, grid=(r // tr,),
        in_specs=[spec] * 4, out_specs=[spec] * 3, out_shape=[shape] * 3,
        compiler_params=_params(),
    )(w, g, m, v)


def _pack(parts):
    flat = []
    for p in parts:
        v = p.reshape(-1)
        flat.append(jnp.pad(v, (0, (-v.shape[0]) % BLK)))
    v = jnp.concatenate(flat)
    v = jnp.pad(v, (0, (-v.shape[0]) % (8 * BLK)))
    return v.reshape(-1, BLK)


def _unpack(buf, shapes):
    flat = buf.reshape(-1)
    out, off = [], 0
    for shp in shapes:
        size = math.prod(shp)
        out.append(flat[off:off + size].reshape(shp))
        off += size + (-size) % BLK
    return out


def kernel(x, meta_tokens, norm_mix, w_in, q_norm, k_norm, attn_sinks, lam_re, lam_im, log_dt, ssm_b_re, ssm_b_im, ssm_c_re, ssm_c_im, ssm_d, w_glu, attn_branch_norm, ssm_branch_norm, w_out, norm_ffn, w_ffn_in, w_ffn_out, loss_target, m_meta_tokens, m_norm_mix, m_w_in, m_q_norm, m_k_norm, m_attn_sinks, m_lam_re, m_lam_im, m_log_dt, m_ssm_b_re, m_ssm_b_im, m_ssm_c_re, m_ssm_c_im, m_ssm_d, m_w_glu, m_attn_branch_norm, m_ssm_branch_norm, m_w_out, m_norm_ffn, m_w_ffn_in, m_w_ffn_out, v_meta_tokens, v_norm_mix, v_w_in, v_q_norm, v_k_norm, v_attn_sinks, v_lam_re, v_lam_im, v_log_dt, v_ssm_b_re, v_ssm_b_im, v_ssm_c_re, v_ssm_c_im, v_ssm_d, v_w_glu, v_attn_branch_norm, v_ssm_branch_norm, v_w_out, v_norm_ffn, v_w_ffn_in, v_w_ffn_out):
    args = dict(locals())
    weights = {n: args[n] for n in WEIGHTS}
    mom_m = {n: args["m_" + n] for n in WEIGHTS}
    mom_v = {n: args["v_" + n] for n in WEIGHTS}

    x2d = x[0]
    target2d = loss_target[0]
    s_len = x2d.shape[0]
    l_dim = s_len + BLK
    tm_row = _pick(l_dim, 320)
    tm_mm = _pick(l_dim, 1040)
    tl_tn = _pick(l_dim, 832)
    tm_ffn = _pick(l_dim, 640)

    shard_in = w_in[0].T.astype(BF16)
    shard_glu = w_glu[0].T.astype(BF16)
    shard_out = w_out[0].astype(BF16)
    shard_ffn_in = w_ffn_in[0].T.astype(BF16)
    shard_ffn_out = w_ffn_out[0].astype(BF16)
    shard_meta = meta_tokens.T
    me = _my_index()

    def landing(shard, row_fn=_plain_rows):
        m_rows, cols = shard.shape
        return lax.dynamic_update_slice(lax.empty((N_DEV * m_rows, cols), shard.dtype), shard,
                                        (row_fn(me, m_rows), 0))

    first = [shard_in, shard_meta]
    ga = _send_start("gather_start_a", first, [landing(s) for s in first], chunked=False)
    later = [shard_glu + ga[4][0:1, 0:1].astype(BF16), shard_out, shard_ffn_in, shard_ffn_out]
    later_fns = [_plain_rows, _plain_rows, _ffn_in_rows, _plain_rows]
    gb = _send_start("gather_start_b", later, [landing(s, f) for s, f in zip(later, later_fns)], chunked=False,
                     row_fns=later_fns)

    nm_t = norm_mix + (ga[4][0:1, 0:1] + gb[4][0:1, 0:1])
    qn_t, kn_t = jnp.tile(q_norm, (1, N_Q_HEADS)), jnp.tile(k_norm, (1, N_KV_HEADS))
    e_mat = jnp.kron(jnp.eye(4, dtype=F32), jnp.ones((HEAD_DIM, HEAD_DIM), F32)).astype(BF16)

    def disc(lr, li, ldt, br, bi):
        return _discretize(lr[0], li[0], ldt[0], br[0], bi[0])

    (abar_re, abar_im, bbar_re, bbar_im), disc_vjp = jax.vjp(disc, lam_re, lam_im, log_dt, ssm_b_re, ssm_b_im)
    wb_re, wb_im = _block_diag_b(bbar_re).astype(BF16), _block_diag_b(bbar_im).astype(BF16)
    wc_re, wc_im = _block_diag_c(ssm_c_re[0]).astype(BF16), _block_diag_c(ssm_c_im[0]).astype(BF16)
    tabs = _power_tables(abar_re.reshape(1, N_STATE), abar_im.reshape(1, N_STATE))

    h0, xn = _embed_norm(x2d, nm_t)
    wt_in, meta_t = _send_wait("gather_wait_a", ga[0], ga[1], ga[2], ga[3], xn, chunked=False)
    meta_pad = jnp.pad(meta_t.T, ((PAD, 0), (0, 0)))
    h0, xn = _embed_meta(meta_pad, nm_t, h0, xn)
    qkv = _matmul("proj_qkv", xn, wt_in, nt=True, tm=tm_mm, tn=512, tk=D_MODEL, n=Q_W + 2 * KV_W, w_off=0)
    u = _matmul("proj_u", xn, wt_in, nt=True, tm=tm_mm, tn=512, tk=D_MODEL, n=D_MODEL, w_off=3)
    gates = _matmul("proj_gates", xn, wt_in, nt=True, tm=tm_mm, tn=512, tk=D_MODEL, n=2 * D_MODEL, w_off=5)
    qn, kf, vf = _qk_prep(qkv, qn_t, kn_t, e_mat, tm_row)
    attn = _attn_fwd(qn, kf, vf, attn_sinks)
    y, z, s_re, s_im = _ssm_fwd(u, wb_re, wb_im, wc_re, wc_im, ssm_d, tabs)
    wt_glu, w_out_f, wt_ffn_in, w_ffn_out_f = _send_wait("gather_wait_b", gb[0], gb[1], gb[2], gb[3], z,
                                                         chunked=False, row_fns=later_fns)
    zab = _matmul("glu_proj", z, wt_glu, nt=True, tm=tm_mm, tn=1024, tk=D_MODEL)
    merged = _merge_fwd(attn, zab, gates, attn_branch_norm, ssm_branch_norm, tm_row)
    h1, hn = _matmul("out_proj", merged, w_out_f, nt=False, tm=tm_mm, tn=1024, tk=D_MODEL, res=h0,
                     norm_g=norm_ffn)
    gu, act = _ffn_in_swiglu(hn, wt_ffn_in, tm_ffn)
    h2 = _matmul("ffn_out", act, w_ffn_out_f, nt=False, tm=tm_mm, tn=1024, tk=1408, res=h1)
    dh2, dh2_b, loss_part = _loss_grad(h2, target2d)

    dgu = _d_act_swiglu(dh2_b, w_ffn_out_f, gu, tm_ffn)

    def exchange_start(name, grads_b, row_fns=None):
        return _send_start(name, grads_b, [jnp.zeros(g.shape, BF16) for g in grads_b], chunked=True,
                           row_fns=row_fns)

    g_ffn_out, g_ffn_out_b = _matmul_tn("g_ffn_out", act, dh2_b, tm=1408, tn=1024, tl=tl_tn)
    g_ffn_in_t, g_ffn_in_b = _matmul_tn("g_ffn_in", dgu, hn, tm=1408, tn=1024, tl=tl_tn)
    ffn_fns = [_ffn_in_rows, _plain_rows]
    ex1 = exchange_start("exchange_start_ffn", [g_ffn_in_b, g_ffn_out_b], ffn_fns)
    dhn = _matmul("d_hn", dgu, wt_ffn_in, nt=False, tm=tm_mm, tn=1024, tk=1408)
    dh1, dh1_b, g_norm_ffn = _norm_bwd_res("ffn_norm_bwd", h1, norm_ffn + ex1[4][0:1, 0:1], dhn, dh2, tm_row)
    dmerged = _matmul("d_merged", dh1_b, w_out_f, nt=True, tm=tm_mm, tn=1024, tk=D_MODEL)
    dattn, dzab, dgates, g_abn, g_sbn = _merge_bwd(attn, zab, gates, attn_branch_norm, ssm_branch_norm,
                                                     dmerged, tm_row)
    g_out, g_out_b = _matmul_tn("g_out", merged, dh1_b, tm=1024, tn=1024, tl=tl_tn)
    dz = _matmul("d_z", dzab, wt_glu, nt=False, tm=tm_mm, tn=1024, tk=1024)
    g_glu_t, g_glu_b = _matmul_tn("g_glu", dzab, z, tm=1024, tn=1024, tl=tl_tn)
    ex2 = exchange_start("exchange_start_mix", [g_glu_b, g_out_b])
    du, g_ssm_d, g_ar, g_ai, g_wbr, g_wbi, g_wcr, g_wci = _ssm_bwd(
        dz, y, u, s_re, s_im, wb_re, wb_im, wc_re, wc_im, ssm_d + ex2[4][0:1, 0:1], tabs)
    dq, dkc, dkp, dkm, dvc, dvp, dvm, g_sinks = _attn_bwd(qn, kf, vf, attn_sinks, attn, dattn)
    dqkv, g_qn_t, g_kn_t = _qk_bwd(qkv, qn_t, kn_t, e_mat, dq, dkc, dkp, dkm, dvc, dvp, dvm)
    dproj = jnp.concatenate([dqkv, du, dgates], axis=1)
    g_in_t, g_in_b = _matmul_tn("g_in", dproj, xn, tm=1152, tn=1024, tl=tl_tn)
    ex3 = exchange_start("exchange_start_in", [g_in_b])
    dxn = _matmul("d_xn", dproj, wt_in, nt=False, tm=tm_mm, tn=1024, tk=1152)
    grad_x2d, dmeta_blk, g_norm_mix = _final_bwd(h0, nm_t + ex3[4][0:1, 0:1], dxn, dh1)

    g_lam_re, g_lam_im, g_log_dt, g_b_re, g_b_im = disc_vjp(
        (g_ar.reshape(SSM_GROUPS, SSM_STATE), g_ai.reshape(SSM_GROUPS, SSM_STATE),
         _block_diag_b_t(g_wbr), _block_diag_b_t(g_wbi)))
    small_grads = {
        "norm_mix": g_norm_mix, "q_norm": g_qn_t.reshape(N_Q_HEADS, HEAD_DIM).sum(0)[None],
        "k_norm": g_kn_t.reshape(N_KV_HEADS, HEAD_DIM).sum(0)[None], "attn_sinks": g_sinks,
        "lam_re": g_lam_re, "lam_im": g_lam_im, "log_dt": g_log_dt, "ssm_b_re": g_b_re, "ssm_b_im": g_b_im,
        "ssm_c_re": _block_diag_c_t(g_wcr)[None], "ssm_c_im": _block_diag_c_t(g_wci)[None],
        "ssm_d": g_ssm_d, "attn_branch_norm": g_abn, "ssm_branch_norm": g_sbn, "norm_ffn": g_norm_ffn,
    }
    small_shapes = [weights[n].shape for n in SMALL] + [(N_META, D_MODEL)]
    packed = _pack([small_grads[n] for n in SMALL] + [dmeta_blk[PAD:]])
    rows = packed.shape[0]
    gs = _send_start("small_start", [packed], [landing(packed)], chunked=False)
    grads, deltas, new_m, new_v = {}, {}, {}, {}

    recv_ffn_in, recv_ffn_out = _send_wait("exchange_wait_ffn", ex1[0], ex1[1], ex1[2], ex1[3], gs[4],
                                           chunked=True, row_fns=ffn_fns)
    recv_glu, recv_out = _send_wait("exchange_wait_mix", ex2[0], ex2[1], ex2[2], ex2[3], recv_ffn_in,
                                    chunked=True)
    (recv_in,) = _send_wait("exchange_wait_in", ex3[0], ex3[1], ex3[2], ex3[3], recv_glu, chunked=True)
    big = [("w_in", g_in_t, True, recv_in, _plain_rows), ("w_glu", g_glu_t, True, recv_glu, _plain_rows),
           ("w_out", g_out, False, recv_out, _plain_rows), ("w_ffn_in", g_ffn_in_t, True, recv_ffn_in, _ffn_in_rows),
           ("w_ffn_out", g_ffn_out, False, recv_ffn_out, _plain_rows)]
    for name, g_full, transposed, recv, row_fn in big:
        m_rows = g_full.shape[0] // N_DEV
        own = lax.dynamic_slice(g_full, (row_fn(me, m_rows), 0), (m_rows, g_full.shape[1]))
        g_shard = _sum_slots("sum_" + name, recv, own)
        grads[name] = (g_shard.T if transposed else g_shard)[None]

    def adamw_2d(name):
        shp = weights[name].shape
        as2d = lambda a: a.reshape(shp[-2], shp[-1])
        d, nm, nv = _adamw("adamw_" + name, as2d(weights[name]), as2d(grads[name]), as2d(mom_m[name]),
                           as2d(mom_v[name]))
        deltas[name], new_m[name], new_v[name] = d.reshape(shp), nm.reshape(shp), nv.reshape(shp)
        return d

    for name in ["w_in", "w_glu", "w_out", "w_ffn_in", "w_ffn_out"]:
        last = adamw_2d(name)

    (gathered,) = _send_wait("small_wait", gs[0], gs[1], gs[2], gs[3], last, chunked=False)
    g_small = _sum_gathered("sum_small", gathered, rows)
    w_small = _pack([weights[n] for n in SMALL] + [jnp.zeros((N_META, D_MODEL), F32)])
    m_small = _pack([mom_m[n] for n in SMALL] + [jnp.zeros((N_META, D_MODEL), F32)])
    v_small = _pack([mom_v[n] for n in SMALL] + [jnp.zeros((N_META, D_MODEL), F32)])
    d_small, nm_small, nv_small = _adamw("adamw_small", w_small, g_small, m_small, v_small)
    grads.update(zip(SMALL + ["_meta"], _unpack(g_small, small_shapes)))
    deltas.update(zip(SMALL, _unpack(d_small, small_shapes[:-1])))
    new_m.update(zip(SMALL, _unpack(nm_small, small_shapes[:-1])))
    new_v.update(zip(SMALL, _unpack(nv_small, small_shapes[:-1])))
    grads["meta_tokens"] = lax.dynamic_slice(grads.pop("_meta"), (0, me * BLK), (N_META, BLK))
    adamw_2d("meta_tokens")

    loss = lax.psum(loss_part[0, 0], ("x", "y", "c"))
    return (loss, grad_x2d[None], *[grads[n] for n in WEIGHTS], *[deltas[n] for n in WEIGHTS],
            *[new_m[n] for n in WEIGHTS], *[new_v[n] for n in WEIGHTS])
```

```python
import math

import jax
import jax.numpy as jnp
from jax import lax
from jax.experimental import pallas as pl
from jax.experimental.pallas import tpu as pltpu

F32 = jnp.float32
BF16 = jnp.bfloat16

D_MODEL = 1024
N_META = 16
HEAD_DIM = 64
N_Q_HEADS = 16
N_KV_HEADS = 4
Q_W = N_Q_HEADS * HEAD_DIM
KV_W = N_KV_HEADS * HEAD_DIM
SSM_GROUPS = 64
SSM_GROUP_CH = 16
SSM_STATE = 64
N_STATE = SSM_GROUPS * SSM_STATE
D_FF = 2816
IN_COLS = Q_W + 2 * KV_W + 3 * D_MODEL
EPS = 1e-6
BLK = 128
PAD = BLK - N_META
N_DEV = 8
NEG = -1e30
SSM_KB = 8
ST_KB = N_STATE // SSM_KB
LB_KB = ST_KB // BLK
N_LB = N_STATE // BLK

ADAM_LR = 0.001
ADAM_B1 = 0.9
ADAM_B2 = 0.999
ADAM_EPS = 1e-08
ADAM_WD = 0.01
ADAM_STEP = 10

VMEM_LIMIT = 48 * 1024 * 1024
MESH = pl.DeviceIdType.MESH

SMALL = ["norm_mix", "q_norm", "k_norm", "attn_sinks", "lam_re", "lam_im", "log_dt", "ssm_b_re", "ssm_b_im",
         "ssm_c_re", "ssm_c_im", "ssm_d", "attn_branch_norm", "ssm_branch_norm", "norm_ffn"]
WEIGHTS = ["meta_tokens", "norm_mix", "w_in", "q_norm", "k_norm", "attn_sinks", "lam_re", "lam_im", "log_dt",
           "ssm_b_re", "ssm_b_im", "ssm_c_re", "ssm_c_im", "ssm_d", "w_glu", "attn_branch_norm",
           "ssm_branch_norm", "w_out", "norm_ffn", "w_ffn_in", "w_ffn_out"]


def _params(**kw):
    return pltpu.CompilerParams(vmem_limit_bytes=VMEM_LIMIT, **kw)


def _pick(n, cap, mult=16):
    best = None
    for d in range(mult, min(n, cap) + 1, mult):
        if n % d == 0:
            best = d
    assert best is not None, (n, cap, mult)
    return best


def _my_index():
    return 4 * lax.axis_index("x") + 2 * lax.axis_index("y") + lax.axis_index("c")


def _rms(x, g):
    r = lax.rsqrt(jnp.mean(x * x, axis=-1, keepdims=True) + EPS)
    return x * r * g


def _rms_bwd(x, g, dy):
    r = lax.rsqrt(jnp.mean(x * x, axis=-1, keepdims=True) + EPS)
    t = dy * g
    dx = r * t - x * (r * r * r) * jnp.mean(t * x, axis=-1, keepdims=True)
    dg = jnp.sum(dy * (x * r), axis=0, keepdims=True)
    return dx, dg


def _sigmoid(x):
    return 1.0 / (1.0 + jnp.exp(-x))


def _gelu(x):
    k = math.sqrt(2.0 / math.pi)
    return 0.5 * x * (1.0 + jnp.tanh(k * (x + 0.044715 * (x * x * x))))


def _gelu_grad(x):
    k = math.sqrt(2.0 / math.pi)
    t = jnp.tanh(k * (x + 0.044715 * (x * x * x)))
    return 0.5 * (1.0 + t) + 0.5 * x * (1.0 - t * t) * (k * (1.0 + 3.0 * 0.044715 * (x * x)))


def _head_mean(x, e_ref):
    hi = x.astype(BF16)
    r1 = x - hi.astype(F32)
    mid = r1.astype(BF16)
    lo = (r1 - mid.astype(F32)).astype(BF16)
    e = e_ref[...]
    out = []
    for b in range(x.shape[1] // 256):
        sl = slice(256 * b, 256 * b + 256)
        s = (jnp.dot(hi[:, sl], e, preferred_element_type=F32)
             + jnp.dot(mid[:, sl], e, preferred_element_type=F32)
             + jnp.dot(lo[:, sl], e, preferred_element_type=F32))
        out.append(s)
    s = out[0] if len(out) == 1 else jnp.concatenate(out, axis=1)
    return s * (1.0 / HEAD_DIM)


def _head_rms(x, g, e_ref):
    r = lax.rsqrt(_head_mean(x * x, e_ref) + EPS)
    return x * r * g


def _head_rms_bwd(x, g, dy, e_ref):
    r = lax.rsqrt(_head_mean(x * x, e_ref) + EPS)
    t = dy * g
    dx = r * t - x * (r * r * r) * _head_mean(t * x, e_ref)
    dg = jnp.sum(dy * (x * r), axis=0, keepdims=True)
    return dx, dg


def _lane_half(shape):
    lane = lax.broadcasted_iota(jnp.int32, shape, len(shape) - 1)
    return (lane >> 6) & 1


def _matmul(name, a, w, *, nt, tm, tn, tk, n=None, w_off=0, res=None, norm_g=None, out_dtype=F32):
    m_dim, k_dim = a.shape
    n_dim = n if n is not None else (w.shape[0] if nt else w.shape[1])
    gm, gn, gk = m_dim // tm, n_dim // tn, k_dim // tk
    assert gm * tm == m_dim and gn * tn == n_dim and gk * tk == k_dim, (name, a.shape, w.shape, tm, tn, tk)
    assert norm_g is None or tn == n_dim
    dn = (((1,), (1,)), ((), ())) if nt else (((1,), (0,)), ((), ()))

    def body(*refs):
        refs = list(refs)
        a_ref, w_ref = refs[0], refs[1]
        pos = 2
        r_ref = g_ref = on_ref = None
        if res is not None:
            r_ref, pos = refs[pos], pos + 1
        if norm_g is not None:
            g_ref, pos = refs[pos], pos + 1
        o_ref, pos = refs[pos], pos + 1
        if norm_g is not None:
            on_ref, pos = refs[pos], pos + 1
        acc = refs[pos]
        k = pl.program_id(2)

        @pl.when(k == 0)
        def _():
            acc[...] = jnp.zeros_like(acc)

        acc[...] += lax.dot_general(a_ref[...], w_ref[...], dn, preferred_element_type=F32)

        @pl.when(k == gk - 1)
        def _():
            r = acc[...]
            if r_ref is not None:
                r = r_ref[...] + r
            o_ref[...] = r.astype(out_dtype)
            if on_ref is not None:
                on_ref[...] = _rms(r, g_ref[...]).astype(BF16)

    if nt:
        w_spec = pl.BlockSpec((tn, tk), lambda i, j, k: (j + w_off, k))
    else:
        w_spec = pl.BlockSpec((tk, tn), lambda i, j, k: (k, j))
    in_specs = [pl.BlockSpec((tm, tk), lambda i, j, k: (i, k)), w_spec]
    args = [a, w]
    out_spec = pl.BlockSpec((tm, tn), lambda i, j, k: (i, j))
    out_specs, out_shape = [out_spec], [jax.ShapeDtypeStruct((m_dim, n_dim), out_dtype)]
    if res is not None:
        in_specs.append(out_spec)
        args.append(res)
    if norm_g is not None:
        in_specs.append(pl.BlockSpec((1, tn), lambda i, j, k: (0, 0)))
        args.append(norm_g)
        out_specs.append(out_spec)
        out_shape.append(jax.ShapeDtypeStruct((m_dim, n_dim), BF16))
    out = pl.pallas_call(
        body, name=name, grid=(gm, gn, gk),
        in_specs=in_specs, out_specs=out_specs, out_shape=out_shape,
        scratch_shapes=[pltpu.VMEM((tm, tn), F32)],
        compiler_params=_params(dimension_semantics=("parallel", "parallel", "arbitrary")),
    )(*args)
    return out if norm_g is not None else out[0]


def _matmul_tn(name, a, b, *, tm, tn, tl, after=None):
    l_dim, m_dim = a.shape
    n_dim = b.shape[1]
    gm, gn, gl = m_dim // tm, n_dim // tn, l_dim // tl
    assert gm * tm == m_dim and gn * tn == n_dim and gl * tl == l_dim, (name, a.shape, b.shape, tm, tn, tl)

    def body(*refs):
        a_ref, b_ref = refs[0], refs[1]
        o_ref, ob_ref = refs[-2], refs[-1]

        @pl.when(pl.program_id(2) == 0)
        def _():
            o_ref[...] = jnp.zeros_like(o_ref)

        o_ref[...] += lax.dot_general(a_ref[...], b_ref[...], (((0,), (0,)), ((), ())),
                                      preferred_element_type=F32)

        @pl.when(pl.program_id(2) == gl - 1)
        def _():
            ob_ref[...] = o_ref[...].astype(BF16)

    out_spec = pl.BlockSpec((tm, tn), lambda i, j, l: (i, j))
    in_specs = [pl.BlockSpec((tl, tm), lambda i, j, l: (l, i)), pl.BlockSpec((tl, tn), lambda i, j, l: (l, j))]
    args = [a, b]
    if after is not None:
        in_specs.append(pl.BlockSpec(memory_space=pl.ANY))
        args.append(after)
    return pl.pallas_call(
        body, name=name, grid=(gm, gn, gl),
        in_specs=in_specs,
        out_specs=[out_spec, out_spec],
        out_shape=[jax.ShapeDtypeStruct((m_dim, n_dim), F32), jax.ShapeDtypeStruct((m_dim, n_dim), BF16)],
        compiler_params=_params(dimension_semantics=("parallel", "parallel", "arbitrary")),
    )(*args)


def _row_spec(tm, cols, f=None):
    if f is None:
        return pl.BlockSpec((tm, cols), lambda i: (i, 0))
    return pl.BlockSpec((tm, cols), lambda i: (f(i), 0))


def _full_spec(shape):
    nd = len(shape)
    return pl.BlockSpec(shape, lambda i: (0,) * nd)


def _embed_norm(x2d, g):
    s_len = x2d.shape[0]
    nb = s_len // BLK + 1

    def body(x_ref, g_ref, h_ref, xn_ref):
        h_ref[...] = x_ref[...]
        xn_ref[...] = _rms(x_ref[...], g_ref[...]).astype(BF16)

    return pl.pallas_call(
        body, name="embed_norm", grid=(nb - 1,),
        in_specs=[_row_spec(BLK, D_MODEL), _full_spec((1, D_MODEL))],
        out_specs=[_row_spec(BLK, D_MODEL, lambda i: i + 1), _row_spec(BLK, D_MODEL, lambda i: i + 1)],
        out_shape=[jax.ShapeDtypeStruct((nb * BLK, D_MODEL), F32),
                   jax.ShapeDtypeStruct((nb * BLK, D_MODEL), BF16)],
        compiler_params=_params(),
    )(x2d, g)


def _embed_meta(meta_pad, g, h0, xn):
    def body(mp_ref, g_ref, h_in, xn_in, h_ref, xn_ref):
        h_ref[...] = mp_ref[...]
        xn_ref[...] = _rms(mp_ref[...], g_ref[...]).astype(BF16)

    any_spec = pl.BlockSpec(memory_space=pl.ANY)
    return pl.pallas_call(
        body, name="embed_meta", grid=(1,),
        in_specs=[_full_spec((BLK, D_MODEL)), _full_spec((1, D_MODEL)), any_spec, any_spec],
        out_specs=[_row_spec(BLK, D_MODEL), _row_spec(BLK, D_MODEL)],
        out_shape=[jax.ShapeDtypeStruct(h0.shape, F32), jax.ShapeDtypeStruct(xn.shape, BF16)],
        input_output_aliases={2: 0, 3: 1},
        compiler_params=_params(),
    )(meta_pad, g, h0, xn)


def _qk_prep(qkv, q_norm_t, k_norm_t, e_mat, tm):
    l_dim = qkv.shape[0]

    def body(x_ref, qg_ref, kg_ref, e_ref, q_ref, kf_ref, vf_ref):
        x = x_ref[...]
        q = _head_rms(x[:, :Q_W], qg_ref[...], e_ref) * (HEAD_DIM ** -0.5)
        q_ref[...] = q.astype(BF16)
        k = _head_rms(x[:, Q_W:Q_W + KV_W], kg_ref[...], e_ref)
        v = x[:, Q_W + KV_W:Q_W + 2 * KV_W]
        half = _lane_half((tm, BLK))
        for src, dst in ((k, kf_ref), (v, vf_ref)):
            for kv in range(N_KV_HEADS):
                blk = src[:, BLK * (kv // 2):BLK * (kv // 2) + BLK]
                swapped = pltpu.roll(blk, HEAD_DIM, axis=1)
                for e in range(2):
                    val = blk if kv % 2 == e else swapped
                    idx = 2 * kv + e
                    dst[:, BLK * idx:BLK * idx + BLK] = jnp.where(half == e, val, 0.0).astype(BF16)

    return pl.pallas_call(
        body, name="qk_prep", grid=(l_dim // tm,),
        in_specs=[_row_spec(tm, Q_W + 2 * KV_W), _full_spec((1, Q_W)), _full_spec((1, KV_W)),
                  _full_spec((256, 256))],
        out_specs=[_row_spec(tm, Q_W), _row_spec(tm, 8 * BLK), _row_spec(tm, 8 * BLK)],
        out_shape=[jax.ShapeDtypeStruct((l_dim, Q_W), BF16),
                   jax.ShapeDtypeStruct((l_dim, 8 * BLK), BF16),
                   jax.ShapeDtypeStruct((l_dim, 8 * BLK), BF16)],
        compiler_params=_params(),
    )(qkv, q_norm_t, k_norm_t, e_mat)


def _merge_fwd(attn, zab, gates, abn, sbn, tm):
    l_dim = attn.shape[0]

    def body(a_ref, z_ref, g_ref, an_ref, sn_ref, o_ref):
        z = z_ref[...]
        g = g_ref[...]
        ssm = z[:, :D_MODEL] * _sigmoid(z[:, D_MODEL:])
        merged = (_sigmoid(g[:, :D_MODEL]) * _rms(a_ref[...], an_ref[...])
                  + _sigmoid(g[:, D_MODEL:]) * _rms(ssm, sn_ref[...]))
        o_ref[...] = merged.astype(BF16)

    return pl.pallas_call(
        body, name="merge_fwd", grid=(l_dim // tm,),
        in_specs=[_row_spec(tm, D_MODEL), _row_spec(tm, 2 * D_MODEL), _row_spec(tm, 2 * D_MODEL),
                  _full_spec((1, D_MODEL)), _full_spec((1, D_MODEL))],
        out_specs=_row_spec(tm, D_MODEL),
        out_shape=jax.ShapeDtypeStruct((l_dim, D_MODEL), BF16),
        compiler_params=_params(),
    )(attn, zab, gates, abn, sbn)


def _merge_bwd(attn, zab, gates, abn, sbn, dmerged, tm):
    l_dim = attn.shape[0]

    def body(a_ref, z_ref, g_ref, an_ref, sn_ref, dm_ref, da_ref, dz_ref, dg_ref, dan_ref, dsn_ref):
        @pl.when(pl.program_id(0) == 0)
        def _():
            dan_ref[...] = jnp.zeros_like(dan_ref)
            dsn_ref[...] = jnp.zeros_like(dsn_ref)

        z = z_ref[...]
        g = g_ref[...]
        dm = dm_ref[...]
        attn_v = a_ref[...]
        za, zb = z[:, :D_MODEL], z[:, D_MODEL:]
        sb = _sigmoid(zb)
        ssm = za * sb
        s_ga, s_gs = _sigmoid(g[:, :D_MODEL]), _sigmoid(g[:, D_MODEL:])
        a_n = _rms(attn_v, an_ref[...])
        s_n = _rms(ssm, sn_ref[...])
        dg_ref[:, :D_MODEL] = (dm * a_n * s_ga * (1.0 - s_ga)).astype(BF16)
        dg_ref[:, D_MODEL:] = (dm * s_n * s_gs * (1.0 - s_gs)).astype(BF16)
        dattn, dan = _rms_bwd(attn_v, an_ref[...], dm * s_ga)
        dssm, dsn = _rms_bwd(ssm, sn_ref[...], dm * s_gs)
        da_ref[...] = dattn
        dz_ref[:, :D_MODEL] = (dssm * sb).astype(BF16)
        dz_ref[:, D_MODEL:] = (dssm * za * sb * (1.0 - sb)).astype(BF16)
        dan_ref[...] += dan
        dsn_ref[...] += dsn

    return pl.pallas_call(
        body, name="merge_bwd", grid=(l_dim // tm,),
        in_specs=[_row_spec(tm, D_MODEL), _row_spec(tm, 2 * D_MODEL), _row_spec(tm, 2 * D_MODEL),
                  _full_spec((1, D_MODEL)), _full_spec((1, D_MODEL)), _row_spec(tm, D_MODEL)],
        out_specs=[_row_spec(tm, D_MODEL), _row_spec(tm, 2 * D_MODEL), _row_spec(tm, 2 * D_MODEL),
                   _full_spec((1, D_MODEL)), _full_spec((1, D_MODEL))],
        out_shape=[jax.ShapeDtypeStruct((l_dim, D_MODEL), F32),
                   jax.ShapeDtypeStruct((l_dim, 2 * D_MODEL), BF16),
                   jax.ShapeDtypeStruct((l_dim, 2 * D_MODEL), BF16),
                   jax.ShapeDtypeStruct((1, D_MODEL), F32), jax.ShapeDtypeStruct((1, D_MODEL), F32)],
        compiler_params=_params(),
    )(attn, zab, gates, abn, sbn, dmerged)


FF_TILE = D_FF // 2


def _ffn_in_swiglu(hn, wt_ffn_in, tm):
    l_dim = hn.shape[0]
    nt = (((1,), (1,)), ((), ()))

    def body(a_ref, w_ref, gu_ref, act_ref):
        r = lax.dot_general(a_ref[...], w_ref[...], nt, preferred_element_type=F32)
        gate, up = r[:, :FF_TILE], r[:, FF_TILE:]
        gu_ref[...] = r.astype(BF16)
        act_ref[...] = (gate * _sigmoid(gate) * up).astype(BF16)

    return pl.pallas_call(
        body, name="ffn_in_swiglu", grid=(l_dim // tm, 2),
        in_specs=[pl.BlockSpec((tm, D_MODEL), lambda i, j: (i, 0)),
                  pl.BlockSpec((2 * FF_TILE, D_MODEL), lambda i, j: (j, 0))],
        out_specs=[pl.BlockSpec((tm, 2 * FF_TILE), lambda i, j: (i, j)),
                   pl.BlockSpec((tm, FF_TILE), lambda i, j: (i, j))],
        out_shape=[jax.ShapeDtypeStruct((l_dim, 2 * D_FF), BF16), jax.ShapeDtypeStruct((l_dim, D_FF), BF16)],
        compiler_params=_params(dimension_semantics=("parallel", "parallel")),
    )(hn, wt_ffn_in)


def _d_act_swiglu(dh2_b, w_ffn_out, gu, tm):
    l_dim = dh2_b.shape[0]
    nt = (((1,), (1,)), ((), ()))

    def body(d_ref, w_ref, gu_ref, o_ref):
        d = lax.dot_general(d_ref[...], w_ref[...], nt, preferred_element_type=F32)
        gate = gu_ref[:, :FF_TILE].astype(F32)
        up = gu_ref[:, FF_TILE:].astype(F32)
        s = _sigmoid(gate)
        o_ref[:, :FF_TILE] = (d * up * (s * (1.0 + gate * (1.0 - s)))).astype(BF16)
        o_ref[:, FF_TILE:] = (d * (gate * s)).astype(BF16)

    return pl.pallas_call(
        body, name="d_act_swiglu", grid=(l_dim // tm, 2),
        in_specs=[pl.BlockSpec((tm, D_MODEL), lambda i, j: (i, 0)),
                  pl.BlockSpec((FF_TILE, D_MODEL), lambda i, j: (j, 0)),
                  pl.BlockSpec((tm, 2 * FF_TILE), lambda i, j: (i, j))],
        out_specs=pl.BlockSpec((tm, 2 * FF_TILE), lambda i, j: (i, j)),
        out_shape=jax.ShapeDtypeStruct((l_dim, 2 * D_FF), BF16),
        compiler_params=_params(dimension_semantics=("parallel", "parallel")),
    )(dh2_b, w_ffn_out, gu)


def _loss_grad(h2, target2d):
    l_dim = h2.shape[0]
    nb = l_dim // BLK

    def body(h_ref, t_ref, d_ref, db_ref, loss_ref):
        i = pl.program_id(0)

        @pl.when(i == 0)
        def _():
            loss_ref[...] = jnp.zeros_like(loss_ref)
            d_ref[...] = jnp.zeros_like(d_ref)
            db_ref[...] = jnp.zeros_like(db_ref)

        @pl.when(i > 0)
        def _():
            err = h_ref[...] - t_ref[...]
            d = err * (1.0 / D_MODEL)
            d_ref[...] = d
            db_ref[...] = d.astype(BF16)
            loss_ref[...] += 0.5 * jnp.sum(jnp.mean(err * err, axis=-1, keepdims=True), axis=0, keepdims=True)

    return pl.pallas_call(
        body, name="loss_grad", grid=(nb,),
        in_specs=[_row_spec(BLK, D_MODEL), _row_spec(BLK, D_MODEL, lambda i: jnp.maximum(i - 1, 0))],
        out_specs=[_row_spec(BLK, D_MODEL), _row_spec(BLK, D_MODEL), _full_spec((1, 1))],
        out_shape=[jax.ShapeDtypeStruct((l_dim, D_MODEL), F32), jax.ShapeDtypeStruct((l_dim, D_MODEL), BF16),
                   jax.ShapeDtypeStruct((1, 1), F32)],
        compiler_params=_params(),
    )(h2, target2d)


def _norm_bwd_res(name, h, g, dy, dres, tm):
    l_dim = h.shape[0]

    def body(h_ref, g_ref, dy_ref, dr_ref, o_ref, ob_ref, dg_ref):
        @pl.when(pl.program_id(0) == 0)
        def _():
            dg_ref[...] = jnp.zeros_like(dg_ref)

        dx, dg = _rms_bwd(h_ref[...], g_ref[...], dy_ref[...])
        out = dr_ref[...] + dx
        o_ref[...] = out
        ob_ref[...] = out.astype(BF16)
        dg_ref[...] += dg

    return pl.pallas_call(
        body, name=name, grid=(l_dim // tm,),
        in_specs=[_row_spec(tm, D_MODEL), _full_spec((1, D_MODEL)), _row_spec(tm, D_MODEL), _row_spec(tm, D_MODEL)],
        out_specs=[_row_spec(tm, D_MODEL), _row_spec(tm, D_MODEL), _full_spec((1, D_MODEL))],
        out_shape=[jax.ShapeDtypeStruct((l_dim, D_MODEL), F32), jax.ShapeDtypeStruct((l_dim, D_MODEL), BF16),
                   jax.ShapeDtypeStruct((1, D_MODEL), F32)],
        compiler_params=_params(),
    )(h, g, dy, dres)


def _final_bwd(h0, g, dxn, dh1):
    l_dim = h0.shape[0]
    nb = l_dim // BLK

    def body(h_ref, g_ref, dy_ref, dr_ref, gx_ref, gm_ref, dg_ref):
        i = pl.program_id(0)

        @pl.when(i == 0)
        def _():
            dg_ref[...] = jnp.zeros_like(dg_ref)

        dx, dg = _rms_bwd(h_ref[...], g_ref[...], dy_ref[...])
        out = dr_ref[...] + dx
        dg_ref[...] += dg

        @pl.when(i == 0)
        def _():
            gm_ref[...] = out

        @pl.when(i > 0)
        def _():
            gx_ref[...] = out

    return pl.pallas_call(
        body, name="final_bwd", grid=(nb,),
        in_specs=[_row_spec(BLK, D_MODEL), _full_spec((1, D_MODEL)), _row_spec(BLK, D_MODEL),
                  _row_spec(BLK, D_MODEL)],
        out_specs=[_row_spec(BLK, D_MODEL, lambda i: jnp.maximum(i - 1, 0)), _full_spec((BLK, D_MODEL)),
                   _full_spec((1, D_MODEL))],
        out_shape=[jax.ShapeDtypeStruct((l_dim - BLK, D_MODEL), F32), jax.ShapeDtypeStruct((BLK, D_MODEL), F32),
                   jax.ShapeDtypeStruct((1, D_MODEL), F32)],
        compiler_params=_params(),
    )(h0, g, dxn, dh1)


def _attn_valid(n):
    shape = (2 * BLK, 3 * BLK)
    qi = lax.broadcasted_iota(jnp.int32, shape, 0) & (BLK - 1)
    col = lax.broadcasted_iota(jnp.int32, shape, 1)
    kj = col & (BLK - 1)
    part = col >> 7
    nn = jnp.zeros(shape, jnp.int32) + n
    meta_ok = (part == 0) & (kj >= PAD) & (nn >= 1)
    prev_ok = (part == 1) & (kj > qi) & (nn >= 2)
    cur_ok = (part == 2) & (kj <= qi) & ((nn >= 1) | (kj >= PAD))
    return meta_ok | prev_ok | cur_ok


def _attn_probs(q_ref, kwin, sk_ref, valid, kv, e):
    qs = jnp.concatenate([q_ref[:, BLK * (2 * kv):BLK * (2 * kv) + BLK],
                          q_ref[:, BLK * (2 * kv + 1):BLK * (2 * kv + 1) + BLK]], axis=0)
    s = lax.dot_general(qs, kwin, (((1,), (1,)), ((), ())), preferred_element_type=F32)
    s = jnp.where(valid, s, NEG)
    h0 = 4 * kv + e
    row = lax.broadcasted_iota(jnp.int32, (2 * BLK, 1), 0)
    sink = jnp.where(row < BLK, sk_ref[:, h0:h0 + 1], sk_ref[:, h0 + 2:h0 + 3])
    m = jnp.maximum(jnp.max(s, axis=-1, keepdims=True), sink)
    ex = jnp.exp(s - m)
    es = jnp.exp(sink - m)
    inv = 1.0 / (jnp.sum(ex, axis=-1, keepdims=True) + es)
    return qs, ex * inv, es * inv


def _attn_specs(nb):
    prev = lambda i: jnp.maximum(i - 1, 0)
    zero = lambda i: 0
    kv_specs = [_row_spec(BLK, 8 * BLK, zero), _row_spec(BLK, 8 * BLK, prev), _row_spec(BLK, 8 * BLK)]
    return kv_specs


def _attn_fwd(qn, kf, vf, sinks):
    l_dim = qn.shape[0]
    nb = l_dim // BLK

    def body(q_ref, km_ref, kp_ref, kc_ref, vm_ref, vp_ref, vc_ref, sk_ref, o_ref):
        valid = _attn_valid(pl.program_id(0))
        for kv in range(N_KV_HEADS):
            outs = []
            for e in range(2):
                sl = slice(BLK * (2 * kv + e), BLK * (2 * kv + e) + BLK)
                kwin = jnp.concatenate([km_ref[:, sl], kp_ref[:, sl], kc_ref[:, sl]], axis=0)
                vwin = jnp.concatenate([vm_ref[:, sl], vp_ref[:, sl], vc_ref[:, sl]], axis=0)
                _, p, _ = _attn_probs(q_ref, kwin, sk_ref, valid, kv, e)
                outs.append(jnp.dot(p.astype(BF16), vwin, preferred_element_type=F32))
            o = outs[0] + outs[1]
            o_ref[:, BLK * (2 * kv):BLK * (2 * kv) + BLK] = o[:BLK]
            o_ref[:, BLK * (2 * kv + 1):BLK * (2 * kv + 1) + BLK] = o[BLK:]

    kv_specs = _attn_specs(nb)
    return pl.pallas_call(
        body, name="attn_fwd", grid=(nb,),
        in_specs=[_row_spec(BLK, Q_W)] + kv_specs + kv_specs + [_full_spec((1, N_Q_HEADS))],
        out_specs=_row_spec(BLK, Q_W),
        out_shape=jax.ShapeDtypeStruct((l_dim, Q_W), F32),
        compiler_params=_params(),
    )(qn, kf, kf, kf, vf, vf, vf, sinks)


def _attn_bwd(qn, kf, vf, sinks, attn, dattn):
    l_dim = qn.shape[0]
    nb = l_dim // BLK
    wide = 8 * BLK

    def body(q_ref, km_ref, kp_ref, kc_ref, vm_ref, vp_ref, vc_ref, sk_ref, o_ref, do_ref,
             dq_ref, dkc_ref, dkp_ref, dkm_ref, dvc_ref, dvp_ref, dvm_ref, dsk_ref):
        @pl.when(pl.program_id(0) == 0)
        def _():
            dkm_ref[...] = jnp.zeros_like(dkm_ref)
            dvm_ref[...] = jnp.zeros_like(dvm_ref)
            dsk_ref[...] = jnp.zeros_like(dsk_ref)

        valid = _attn_valid(pl.program_id(0))
        half = _lane_half((BLK, BLK))
        lane16 = lax.broadcasted_iota(jnp.int32, (1, N_Q_HEADS), 1)
        dsk = jnp.zeros((1, N_Q_HEADS), F32)
        for kv in range(N_KV_HEADS):
            j0, j1 = 2 * kv, 2 * kv + 1
            do0 = do_ref[:, BLK * j0:BLK * j0 + BLK]
            do1 = do_ref[:, BLK * j1:BLK * j1 + BLK]
            prod0 = do0 * o_ref[:, BLK * j0:BLK * j0 + BLK]
            prod1 = do1 * o_ref[:, BLK * j1:BLK * j1 + BLK]
            dos = jnp.concatenate([do0, do1], axis=0).astype(BF16)
            dqs = []
            for e in range(2):
                sl = slice(BLK * (2 * kv + e), BLK * (2 * kv + e) + BLK)
                kwin = jnp.concatenate([km_ref[:, sl], kp_ref[:, sl], kc_ref[:, sl]], axis=0)
                vwin = jnp.concatenate([vm_ref[:, sl], vp_ref[:, sl], vc_ref[:, sl]], axis=0)
                qs, p, p_sink = _attn_probs(q_ref, kwin, sk_ref, valid, kv, e)
                delta = jnp.concatenate(
                    [jnp.sum(jnp.where(half == e, prod0, 0.0), axis=-1, keepdims=True),
                     jnp.sum(jnp.where(half == e, prod1, 0.0), axis=-1, keepdims=True)], axis=0)
                dp = lax.dot_general(dos, vwin, (((1,), (1,)), ((), ())), preferred_element_type=F32)
                ds = (p * (dp - delta)).astype(BF16)
                pb = p.astype(BF16)
                dqs.append(jnp.dot(ds, kwin, preferred_element_type=F32))
                dk = lax.dot_general(ds, qs, (((0,), (0,)), ((), ())), preferred_element_type=F32)
                dv = lax.dot_general(pb, dos, (((0,), (0,)), ((), ())), preferred_element_type=F32)
                dkm_ref[:, sl] += dk[:BLK]
                dkp_ref[:, sl] = dk[BLK:2 * BLK]
                dkc_ref[:, sl] = dk[2 * BLK:]
                dvm_ref[:, sl] += dv[:BLK]
                dvp_ref[:, sl] = dv[BLK:2 * BLK]
                dvc_ref[:, sl] = dv[2 * BLK:]
                sink_g = -(p_sink * delta)
                g_lo = jnp.sum(sink_g[:BLK], axis=0, keepdims=True)
                g_hi = jnp.sum(sink_g[BLK:], axis=0, keepdims=True)
                dsk = dsk + jnp.where(lane16 == 4 * kv + e, g_lo, 0.0) + jnp.where(lane16 == 4 * kv + 2 + e, g_hi, 0.0)
            dq = jnp.where(jnp.concatenate([half, half], axis=0) == 0, dqs[0], dqs[1])
            dq_ref[:, BLK * j0:BLK * j0 + BLK] = dq[:BLK]
            dq_ref[:, BLK * j1:BLK * j1 + BLK] = dq[BLK:]
        dsk_ref[...] += dsk

    kv_specs = _attn_specs(nb)
    row_wide = _row_spec(BLK, wide)
    acc_wide = _full_spec((BLK, wide))
    big = jax.ShapeDtypeStruct((l_dim, wide), F32)
    return pl.pallas_call(
        body, name="attn_bwd", grid=(nb,),
        in_specs=[_row_spec(BLK, Q_W)] + kv_specs + kv_specs
        + [_full_spec((1, N_Q_HEADS)), _row_spec(BLK, Q_W), _row_spec(BLK, Q_W)],
        out_specs=[_row_spec(BLK, Q_W), row_wide, row_wide, acc_wide, row_wide, row_wide, acc_wide,
                   _full_spec((1, N_Q_HEADS))],
        out_shape=[jax.ShapeDtypeStruct((l_dim, Q_W), F32), big, big, jax.ShapeDtypeStruct((BLK, wide), F32),
                   big, big, jax.ShapeDtypeStruct((BLK, wide), F32), jax.ShapeDtypeStruct((1, N_Q_HEADS), F32)],
        compiler_params=_params(),
    )(qn, kf, kf, kf, vf, vf, vf, sinks, attn, dattn)


def _qk_bwd(qkv, q_norm_t, k_norm_t, e_mat, dq, dkc, dkp, dkm, dvc, dvp, dvm):
    l_dim = qkv.shape[0]
    nb = l_dim // BLK
    wide = 8 * BLK

    def fold(x):
        half = _lane_half((BLK, BLK))
        blocks = []
        for kb in range(2):
            t = []
            for kv in (2 * kb, 2 * kb + 1):
                own = kv % 2
                a = x[:, BLK * (2 * kv + own):BLK * (2 * kv + own) + BLK]
                b = pltpu.roll(x[:, BLK * (2 * kv + 1 - own):BLK * (2 * kv + 1 - own) + BLK], HEAD_DIM, axis=1)
                t.append(a + b)
            blocks.append(jnp.where(half == 0, t[0], t[1]))
        return jnp.concatenate(blocks, axis=1)

    def body(x_ref, qg_ref, kg_ref, e_ref, dq_ref, dkc_ref, dkp_ref, dkm_ref, dvc_ref, dvp_ref, dvm_ref,
             o_ref, dqg_ref, dkg_ref):
        i = pl.program_id(0)

        @pl.when(i == 0)
        def _():
            dqg_ref[...] = jnp.zeros_like(dqg_ref)
            dkg_ref[...] = jnp.zeros_like(dkg_ref)

        first = jnp.where(i == 0, 1.0, 0.0)
        not_last = jnp.where(i < nb - 1, 1.0, 0.0)
        dk_x = dkc_ref[...] + not_last * dkp_ref[...] + first * dkm_ref[...]
        dv_x = dvc_ref[...] + not_last * dvp_ref[...] + first * dvm_ref[...]
        x = x_ref[...]
        dqx, dqg = _head_rms_bwd(x[:, :Q_W], qg_ref[...], dq_ref[...] * (HEAD_DIM ** -0.5), e_ref)
        dkx, dkg = _head_rms_bwd(x[:, Q_W:Q_W + KV_W], kg_ref[...], fold(dk_x), e_ref)
        o_ref[:, :Q_W] = dqx.astype(BF16)
        o_ref[:, Q_W:Q_W + KV_W] = dkx.astype(BF16)
        o_ref[:, Q_W + KV_W:] = fold(dv_x).astype(BF16)
        dqg_ref[...] += dqg
        dkg_ref[...] += dkg

    nxt = lambda i: jnp.minimum(i + 1, nb - 1)
    row_wide = _row_spec(BLK, wide)
    nxt_wide = _row_spec(BLK, wide, nxt)
    acc_wide = _full_spec((BLK, wide))
    return pl.pallas_call(
        body, name="qk_bwd", grid=(nb,),
        in_specs=[_row_spec(BLK, Q_W + 2 * KV_W), _full_spec((1, Q_W)), _full_spec((1, KV_W)),
                  _full_spec((256, 256)), _row_spec(BLK, Q_W),
                  row_wide, nxt_wide, acc_wide, row_wide, nxt_wide, acc_wide],
        out_specs=[_row_spec(BLK, Q_W + 2 * KV_W), _full_spec((1, Q_W)), _full_spec((1, KV_W))],
        out_shape=[jax.ShapeDtypeStruct((l_dim, Q_W + 2 * KV_W), BF16),
                   jax.ShapeDtypeStruct((1, Q_W), F32), jax.ShapeDtypeStruct((1, KV_W), F32)],
        compiler_params=_params(),
    )(qkv, q_norm_t, k_norm_t, e_mat, dq, dkc, dkp, dkm, dvc, dvp, dvm)


GRP = 8


def _strided(r, g):
    return pl.ds(r, g, stride=GRP)


def _slab(ref, base, r, g):
    return jnp.concatenate([ref[base + i, _strided(r, g), :] for i in range(LB_KB)], axis=1)


def _slab_store(ref, base, r, g, val):
    for i in range(LB_KB):
        ref[base + i, _strided(r, g), :] = val[:, BLK * i:BLK * i + BLK]


def _group_totals(xr_ref, xi_ref, base, ar, ai, reverse):
    g = xr_ref.shape[1] // GRP
    sr = si = None
    for r in (range(GRP - 1, -1, -1) if reverse else range(GRP)):
        xr, xi = _slab(xr_ref, base, r, g), _slab(xi_ref, base, r, g)
        if sr is not None:
            xr, xi = xr + ar * sr - ai * si, xi + ar * si + ai * sr
        sr, si = xr, xi
    return sr, si


def _carry_scan(tr, ti, sqr_ref, sqi_ref, lanes, sign, reverse):
    g = tr.shape[0]
    row = lax.broadcasted_iota(jnp.int32, tr.shape, 0)
    idx, s = 0, 1
    while s < g:
        ar = sqr_ref[idx:idx + 1, lanes]
        ai = sign * sqi_ref[idx:idx + 1, lanes]
        shift, keep = (g - s, row < g - s) if reverse else (s, row >= s)
        pr = jnp.where(keep, pltpu.roll(tr, shift, axis=0), 0.0)
        pi = jnp.where(keep, pltpu.roll(ti, shift, axis=0), 0.0)
        tr, ti = tr + ar * pr - ai * pi, ti + ar * pi + ai * pr
        idx, s = idx + 1, 2 * s
    return tr, ti


def _ssm_fwd(u, wb_re, wb_im, wc_re, wc_im, d_skip, tabs):
    l_dim = u.shape[0]
    nb = l_dim // BLK
    g = BLK // GRP

    def body(u_ref, wbr_ref, wbi_ref, wcr_ref, wci_ref, d_ref, a1r_ref, a1i_ref, sqr_ref, sqi_ref,
             seqr_ref, seqi_ref, y_ref, z_ref, sr_ref, si_ref, cr_ref, ci_ref, xr_ref, xi_ref):
        @pl.when(pl.program_id(0) == 0)
        def _():
            cr_ref[...] = jnp.zeros_like(cr_ref)
            ci_ref[...] = jnp.zeros_like(ci_ref)

        row = lax.broadcasted_iota(jnp.int32, (g, ST_KB), 0)
        for kb in range(SSM_KB):
            ch = slice(BLK * kb, BLK * kb + BLK)
            lanes = slice(ST_KB * kb, ST_KB * kb + ST_KB)
            u_kb = u_ref[:, ch]
            ub = u_kb.astype(BF16)
            xr = jnp.dot(ub, wbr_ref[kb], preferred_element_type=F32)
            xi = jnp.dot(ub, wbi_ref[kb], preferred_element_type=F32)
            base = LB_KB * kb
            for i in range(LB_KB):
                xr_ref[base + i] = xr[:, BLK * i:BLK * i + BLK]
                xi_ref[base + i] = xi[:, BLK * i:BLK * i + BLK]
            ar, ai = a1r_ref[0:1, lanes], a1i_ref[0:1, lanes]
            tr, ti = _group_totals(xr_ref, xi_ref, base, ar, ai, reverse=False)
            tr, ti = _carry_scan(tr, ti, sqr_ref, sqi_ref, lanes, 1.0, reverse=False)
            cin_r, cin_i = cr_ref[0:1, lanes], ci_ref[0:1, lanes]
            qr, qi = seqr_ref[:, lanes], seqi_ref[:, lanes]
            tr, ti = tr + qr * cin_r - qi * cin_i, ti + qr * cin_i + qi * cin_r
            cr_ref[0:1, lanes] = jnp.sum(jnp.where(row == g - 1, tr, 0.0), axis=0, keepdims=True)
            ci_ref[0:1, lanes] = jnp.sum(jnp.where(row == g - 1, ti, 0.0), axis=0, keepdims=True)
            pr = jnp.where(row == 0, cin_r, pltpu.roll(tr, 1, axis=0))
            pi = jnp.where(row == 0, cin_i, pltpu.roll(ti, 1, axis=0))
            for r in range(GRP):
                pr, pi = (_slab(xr_ref, base, r, g) + ar * pr - ai * pi,
                          _slab(xi_ref, base, r, g) + ar * pi + ai * pr)
                _slab_store(sr_ref, base, r, g, pr)
                _slab_store(si_ref, base, r, g, pi)
            s_r = jnp.concatenate([sr_ref[LB_KB * kb + i] for i in range(LB_KB)], axis=1)
            s_i = jnp.concatenate([si_ref[LB_KB * kb + i] for i in range(LB_KB)], axis=1)
            y = (jnp.dot(s_r.astype(BF16), wcr_ref[kb], preferred_element_type=F32)
                 - jnp.dot(s_i.astype(BF16), wci_ref[kb], preferred_element_type=F32)
                 + d_ref[:, ch] * u_kb)
            y_ref[:, ch] = y
            z_ref[:, ch] = _gelu(y).astype(BF16)

    wb_spec = _full_spec((SSM_KB, BLK, ST_KB))
    wc_spec = _full_spec((SSM_KB, ST_KB, BLK))
    tab_specs = [_full_spec(t.shape) for t in tabs[:6]]
    state_spec = pl.BlockSpec((N_LB, BLK, BLK), lambda i: (0, i, 0))
    state_shape = jax.ShapeDtypeStruct((N_LB, l_dim, BLK), F32)
    return pl.pallas_call(
        body, name="ssm_fwd", grid=(nb,),
        in_specs=[_row_spec(BLK, D_MODEL), wb_spec, wb_spec, wc_spec, wc_spec, _full_spec((1, D_MODEL))]
        + tab_specs,
        out_specs=[_row_spec(BLK, D_MODEL), _row_spec(BLK, D_MODEL), state_spec, state_spec],
        out_shape=[jax.ShapeDtypeStruct((l_dim, D_MODEL), F32), jax.ShapeDtypeStruct((l_dim, D_MODEL), BF16),
                   state_shape, state_shape],
        scratch_shapes=[pltpu.VMEM((8, N_STATE), F32), pltpu.VMEM((8, N_STATE), F32),
                        pltpu.VMEM((N_LB, BLK, BLK), F32), pltpu.VMEM((N_LB, BLK, BLK), F32)],
        compiler_params=_params(),
    )(u, wb_re, wb_im, wc_re, wc_im, d_skip, *tabs[:6])


def _ssm_bwd(dz, y, u, s_re, s_im, wb_re, wb_im, wc_re, wc_im, d_skip, tabs):
    l_dim = u.shape[0]
    nb = l_dim // BLK
    g = BLK // GRP

    def body(dz_ref, y_ref, u_ref, sr_ref, si_ref, wbr_ref, wbi_ref, wcr_ref, wci_ref, d_ref,
             a1r_ref, a1i_ref, sqr_ref, sqi_ref, revr_ref, revi_ref,
             du_ref, dd_ref, dar_ref, dai_ref, dwbr_ref, dwbi_ref, dwcr_ref, dwci_ref,
             cr_ref, ci_ref, gr_ref, gi_ref):
        @pl.when(pl.program_id(0) == 0)
        def _():
            for r in (cr_ref, ci_ref, dd_ref, dar_ref, dai_ref, dwbr_ref, dwbi_ref, dwcr_ref, dwci_ref):
                r[...] = jnp.zeros_like(r)

        tn = (((0,), (0,)), ((), ()))
        nt = (((1,), (1,)), ((), ()))
        row = lax.broadcasted_iota(jnp.int32, (g, ST_KB), 0)
        for kb in range(SSM_KB):
            ch = slice(BLK * kb, BLK * kb + BLK)
            lanes = slice(ST_KB * kb, ST_KB * kb + ST_KB)
            u_kb = u_ref[:, ch]
            dy = dz_ref[:, ch] * _gelu_grad(y_ref[:, ch])
            dyb = dy.astype(BF16)
            ub = u_kb.astype(BF16)
            dd_ref[:, ch] += jnp.sum(dy * u_kb, axis=0, keepdims=True)
            ds_r = lax.dot_general(dyb, wcr_ref[kb], nt, preferred_element_type=F32)
            ds_i = -lax.dot_general(dyb, wci_ref[kb], nt, preferred_element_type=F32)
            base = LB_KB * kb
            for i in range(LB_KB):
                gr_ref[base + i] = ds_r[:, BLK * i:BLK * i + BLK]
                gi_ref[base + i] = ds_i[:, BLK * i:BLK * i + BLK]
            ar, ai = a1r_ref[0:1, lanes], -a1i_ref[0:1, lanes]
            tr, ti = _group_totals(gr_ref, gi_ref, base, ar, ai, reverse=True)
            tr, ti = _carry_scan(tr, ti, sqr_ref, sqi_ref, lanes, -1.0, reverse=True)
            cin_r, cin_i = cr_ref[0:1, lanes], ci_ref[0:1, lanes]
            qr, qi = revr_ref[:, lanes], -revi_ref[:, lanes]
            tr, ti = tr + qr * cin_r - qi * cin_i, ti + qr * cin_i + qi * cin_r
            cr_ref[0:1, lanes] = jnp.sum(jnp.where(row == 0, tr, 0.0), axis=0, keepdims=True)
            ci_ref[0:1, lanes] = jnp.sum(jnp.where(row == 0, ti, 0.0), axis=0, keepdims=True)
            nr = jnp.where(row == g - 1, cin_r, pltpu.roll(tr, g - 1, axis=0))
            ni = jnp.where(row == g - 1, cin_i, pltpu.roll(ti, g - 1, axis=0))
            acc_r = jnp.zeros((g, ST_KB), F32)
            acc_i = jnp.zeros((g, ST_KB), F32)
            for r in range(GRP - 1, -1, -1):
                s_r, s_i = _slab(sr_ref, base, r, g), _slab(si_ref, base, r, g)
                acc_r = acc_r + (nr * s_r + ni * s_i)
                acc_i = acc_i + (ni * s_r - nr * s_i)
                nr, ni = (_slab(gr_ref, base, r, g) + ar * nr - ai * ni,
                          _slab(gi_ref, base, r, g) + ar * ni + ai * nr)
                _slab_store(gr_ref, base, r, g, nr)
                _slab_store(gi_ref, base, r, g, ni)
            dar_ref[:, lanes] += jnp.sum(acc_r, axis=0, keepdims=True)
            dai_ref[:, lanes] += jnp.sum(acc_i, axis=0, keepdims=True)
            grb = jnp.concatenate([gr_ref[base + i] for i in range(LB_KB)], axis=1).astype(BF16)
            gib = jnp.concatenate([gi_ref[base + i] for i in range(LB_KB)], axis=1).astype(BF16)
            srb = jnp.concatenate([sr_ref[base + i] for i in range(LB_KB)], axis=1).astype(BF16)
            sib = jnp.concatenate([si_ref[base + i] for i in range(LB_KB)], axis=1).astype(BF16)
            du = (lax.dot_general(grb, wbr_ref[kb], nt, preferred_element_type=F32)
                  + lax.dot_general(gib, wbi_ref[kb], nt, preferred_element_type=F32)
                  + d_ref[:, ch] * dy)
            du_ref[:, ch] = du.astype(BF16)
            dwbr_ref[kb] += lax.dot_general(ub, grb, tn, preferred_element_type=F32)
            dwbi_ref[kb] += lax.dot_general(ub, gib, tn, preferred_element_type=F32)
            dwcr_ref[kb] += lax.dot_general(srb, dyb, tn, preferred_element_type=F32)
            dwci_ref[kb] -= lax.dot_general(sib, dyb, tn, preferred_element_type=F32)

    rev = lambda i: nb - 1 - i
    wb_spec = _full_spec((SSM_KB, BLK, ST_KB))
    wc_spec = _full_spec((SSM_KB, ST_KB, BLK))
    tab_in = [tabs[0], tabs[1], tabs[2], tabs[3], tabs[6], tabs[7]]
    tab_specs = [_full_spec(t.shape) for t in tab_in]
    vec = _full_spec((1, D_MODEL))
    svec = _full_spec((1, N_STATE))
    state_spec = pl.BlockSpec((N_LB, BLK, BLK), lambda i: (0, nb - 1 - i, 0))
    return pl.pallas_call(
        body, name="ssm_bwd", grid=(nb,),
        in_specs=[_row_spec(BLK, D_MODEL, rev), _row_spec(BLK, D_MODEL, rev), _row_spec(BLK, D_MODEL, rev),
                  state_spec, state_spec,
                  wb_spec, wb_spec, wc_spec, wc_spec, vec] + tab_specs,
        out_specs=[_row_spec(BLK, D_MODEL, rev), vec, svec, svec, wb_spec, wb_spec, wc_spec, wc_spec],
        out_shape=[jax.ShapeDtypeStruct((l_dim, D_MODEL), BF16), jax.ShapeDtypeStruct((1, D_MODEL), F32),
                   jax.ShapeDtypeStruct((1, N_STATE), F32), jax.ShapeDtypeStruct((1, N_STATE), F32),
                   jax.ShapeDtypeStruct((SSM_KB, BLK, ST_KB), F32), jax.ShapeDtypeStruct((SSM_KB, BLK, ST_KB), F32),
                   jax.ShapeDtypeStruct((SSM_KB, ST_KB, BLK), F32), jax.ShapeDtypeStruct((SSM_KB, ST_KB, BLK), F32)],
        scratch_shapes=[pltpu.VMEM((8, N_STATE), F32), pltpu.VMEM((8, N_STATE), F32),
                        pltpu.VMEM((N_LB, BLK, BLK), F32), pltpu.VMEM((N_LB, BLK, BLK), F32)],
        compiler_params=_params(),
    )(dz, y, u, s_re, s_im, wb_re, wb_im, wc_re, wc_im, d_skip, *tab_in)


def _discretize(lam_re, lam_im, log_dt, b_re, b_im):
    dt = jnp.exp(log_dt)[:, None]
    mag = jnp.exp(lam_re * dt)
    ar, ai = mag * jnp.cos(lam_im * dt), mag * jnp.sin(lam_im * dt)
    den = lam_re * lam_re + lam_im * lam_im
    nr, ni = ar - 1.0, ai
    fr, fi = (nr * lam_re + ni * lam_im) / den, (ni * lam_re - nr * lam_im) / den
    bbar_re = fr[..., None] * b_re - fi[..., None] * b_im
    bbar_im = fr[..., None] * b_im + fi[..., None] * b_re
    return ar, ai, bbar_re, bbar_im


def _block_diag_b(bbar):
    eye = jnp.eye(8, dtype=bbar.dtype)
    return jnp.einsum("kgpc,gh->kgchp", bbar.reshape(8, 8, SSM_STATE, SSM_GROUP_CH), eye).reshape(8, BLK, ST_KB)


def _block_diag_b_t(dwb):
    eye = jnp.eye(8, dtype=dwb.dtype)
    return jnp.einsum("kgchp,gh->kgpc", dwb.reshape(8, 8, SSM_GROUP_CH, 8, SSM_STATE), eye).reshape(
        SSM_GROUPS, SSM_STATE, SSM_GROUP_CH)


def _block_diag_c(c):
    eye = jnp.eye(8, dtype=c.dtype)
    return jnp.einsum("kgcp,gh->kgphc", c.reshape(8, 8, SSM_GROUP_CH, SSM_STATE), eye).reshape(8, ST_KB, BLK)


def _block_diag_c_t(dwc):
    eye = jnp.eye(8, dtype=dwc.dtype)
    return jnp.einsum("kgphc,gh->kgcp", dwc.reshape(8, 8, SSM_STATE, 8, SSM_GROUP_CH), eye).reshape(
        SSM_GROUPS, SSM_GROUP_CH, SSM_STATE)


def _powers(br, bi, n):
    pr, pi = br, bi
    cr, ci = br, bi
    while pr.shape[0] < n:
        pr, pi = (jnp.concatenate([pr, pr * cr - pi * ci], axis=0),
                  jnp.concatenate([pi, pr * ci + pi * cr], axis=0))
        cr, ci = cr * cr - ci * ci, 2.0 * cr * ci
    return pr[:n], pi[:n]


def _powers_desc(br, bi, n):
    pr, pi = br, bi
    cr, ci = br, bi
    while pr.shape[0] < n:
        pr, pi = (jnp.concatenate([pr * cr - pi * ci, pr], axis=0),
                  jnp.concatenate([pr * ci + pi * cr, pi], axis=0))
        cr, ci = cr * cr - ci * ci, 2.0 * cr * ci
    return pr, pi


def _power_tables(ar, ai):
    g = BLK // GRP
    a1r, a1i = _powers(ar, ai, GRP)
    seqr, seqi = _powers(a1r[GRP - 1:], a1i[GRP - 1:], g)
    revr, revi = _powers_desc(a1r[GRP - 1:], a1i[GRP - 1:], g)
    sq_r, sq_i = [seqr[0:1]], [seqi[0:1]]
    while len(sq_r) < 8:
        r, i = sq_r[-1], sq_i[-1]
        sq_r.append(r * r - i * i)
        sq_i.append(2.0 * r * i)
    sqr, sqi = jnp.concatenate(sq_r, axis=0), jnp.concatenate(sq_i, axis=0)
    return a1r, a1i, sqr, sqi, seqr, seqi, revr, revi


HBM_SPEC = pl.BlockSpec(memory_space=pltpu.HBM)
SEM_SPEC = pl.BlockSpec(memory_space=pltpu.SEMAPHORE)
DATAFLOW = pltpu.SideEffectType.DATAFLOW_SIDE_EFFECTING


def _plain_rows(p, m):
    return p * m


def _ffn_in_rows(p, m):
    return ((p & 3) >> 1) * (4 * m) + (p >> 2) * (2 * m) + (p & 1) * m


def _peer_copies(src_refs, land_refs, send_sems, recv_sems, chunked, row_fns):
    x, y, c = lax.axis_index("x"), lax.axis_index("y"), lax.axis_index("c")
    me = 4 * x + 2 * y + c
    copies = []
    for a, (src, land) in enumerate(zip(src_refs, land_refs)):
        m = land.shape[0] // N_DEV
        for k in range(N_DEV - 1):
            rel = k + 1
            bx, by, bc = (rel >> 2) & 1, (rel >> 1) & 1, rel & 1
            peer = (x + bx - 2 * x * bx, y + by - 2 * y * by, c + bc - 2 * c * bc)
            p_idx = 4 * peer[0] + 2 * peer[1] + peer[2]
            copies.append(pltpu.make_async_remote_copy(
                src_ref=src.at[pl.ds(row_fns[a](p_idx, m), m), :] if chunked else src,
                dst_ref=land.at[pl.ds(me * m if chunked else row_fns[a](me, m), m), :],
                send_sem=send_sems[a * (N_DEV - 1) + k], recv_sem=recv_sems[a * (N_DEV - 1) + k],
                device_id=peer, device_id_type=MESH))
    return copies


def _send_start(name, srcs, lands, chunked, row_fns=None):
    n = len(srcs)
    ns = n * (N_DEV - 1)
    row_fns = row_fns or [_plain_rows] * n

    def body(*refs):
        src_refs, land_refs = refs[:n], refs[n:2 * n]
        send_sems, recv_sems = refs[2 * n:2 * n + ns], refs[2 * n + ns:2 * n + 2 * ns]
        token = refs[-1]
        for cp in _peer_copies(src_refs, land_refs, send_sems, recv_sems, chunked, row_fns):
            cp.start()
        token[...] = jnp.zeros_like(token)

    ins = [pltpu.with_memory_space_constraint(a, pltpu.HBM) for a in list(srcs) + list(lands)]
    out = pl.pallas_call(
        body, name=name,
        in_specs=[HBM_SPEC] * (2 * n),
        out_specs=[SEM_SPEC] * (2 * ns) + [HBM_SPEC] * (2 * n) + [pl.BlockSpec(memory_space=pltpu.VMEM)],
        out_shape=[pltpu.SemaphoreType.DMA(())] * (2 * ns)
        + [pltpu.HBM(a.shape, a.dtype) for a in list(srcs) + list(lands)]
        + [jax.ShapeDtypeStruct((8, BLK), F32)],
        input_output_aliases={i: i + 2 * ns for i in range(2 * n)},
        compiler_params=pltpu.CompilerParams(has_side_effects=DATAFLOW),
    )(*ins)
    return out[:ns], out[ns:2 * ns], out[2 * ns:2 * ns + n], out[2 * ns + n:2 * ns + 2 * n], out[-1]


def _send_wait(name, send_sems, recv_sems, srcs, lands, after, chunked, row_fns=None):
    n = len(srcs)
    ns = n * (N_DEV - 1)
    row_fns = row_fns or [_plain_rows] * n

    def body(*refs):
        src_refs, land_refs = refs[:n], refs[n:2 * n]
        s_sems, r_sems = refs[2 * n:2 * n + ns], refs[2 * n + ns:2 * n + 2 * ns]
        copies = _peer_copies(src_refs, land_refs, s_sems, r_sems, chunked, row_fns)
        for cp in copies:
            cp.wait_send()
        for cp in copies:
            cp.wait_recv()

    out = pl.pallas_call(
        body, name=name,
        in_specs=[HBM_SPEC] * (2 * n) + [SEM_SPEC] * (2 * ns) + [pl.BlockSpec(memory_space=pl.ANY)],
        out_specs=[HBM_SPEC] * (2 * n),
        out_shape=[pltpu.HBM(a.shape, a.dtype) for a in list(srcs) + list(lands)],
        input_output_aliases={i: i for i in range(2 * n)},
        compiler_params=pltpu.CompilerParams(has_side_effects=DATAFLOW),
    )(*srcs, *lands, *send_sems, *recv_sems, after)
    return out[n:]


def _sum_slots(name, recv, own):
    m, ncol = own.shape
    tr = m // 2 if (m // 2) % 16 == 0 else m
    g = m // tr

    def body(*refs):
        slots, own_ref, o_ref = refs[:N_DEV], refs[N_DEV], refs[N_DEV + 1]
        me = _my_index()
        tot = None
        for s in range(N_DEV):
            v = jnp.where(me == s, own_ref[...], slots[s][...].astype(F32))
            tot = v if tot is None else tot + v
        o_ref[...] = tot

    def slot_spec(s):
        return pl.BlockSpec((tr, ncol), lambda i: (s * g + i, 0))

    return pl.pallas_call(
        body, name=name, grid=(g,),
        in_specs=[slot_spec(s) for s in range(N_DEV)] + [pl.BlockSpec((tr, ncol), lambda i: (i, 0))],
        out_specs=pl.BlockSpec((tr, ncol), lambda i: (i, 0)),
        out_shape=jax.ShapeDtypeStruct((m, ncol), F32),
        compiler_params=_params(),
    )(*([recv] * N_DEV), own)


def _sum_gathered(name, gathered, rows):
    tr = _pick(rows, 512, 8)
    g = rows // tr

    def body(*refs):
        o_ref = refs[N_DEV]
        tot = refs[0][...]
        for s in range(1, N_DEV):
            tot = tot + refs[s][...]
        o_ref[...] = tot

    return pl.pallas_call(
        body, name=name, grid=(g,),
        in_specs=[pl.BlockSpec((tr, BLK), (lambda i, s=s: (s * g + i, 0))) for s in range(N_DEV)],
        out_specs=pl.BlockSpec((tr, BLK), lambda i: (i, 0)),
        out_shape=jax.ShapeDtypeStruct((rows, BLK), F32),
        compiler_params=_params(),
    )(*([gathered] * N_DEV))


def _adamw(name, w, g, m, v):
    r, c = w.shape
    tr = _pick(r, 256, 8) if r % 8 == 0 else r
    c1 = 1.0 - ADAM_B1 ** ADAM_STEP
    c2 = 1.0 - ADAM_B2 ** ADAM_STEP

    def body(w_ref, g_ref, m_ref, v_ref, d_ref, nm_ref, nv_ref):
        gv = g_ref[...]
        nm = ADAM_B1 * m_ref[...] + (1.0 - ADAM_B1) * gv
        nv = ADAM_B2 * v_ref[...] + (1.0 - ADAM_B2) * (gv * gv)
        m_hat = nm / c1
        v_hat = nv / c2
        d_ref[...] = -ADAM_LR * (m_hat / (jnp.sqrt(v_hat) + ADAM_EPS) + ADAM_WD * w_ref[...])
        nm_ref[...] = nm
        nv_ref[...] = nv

    spec = pl.BlockSpec((tr, c), lambda i: (i, 0))
    shape = jax.ShapeDtypeStruct((r, c), F32)
    return pl.pallas_call(
        body, name=name, grid=(r // tr,),
        in_specs=[spec] * 4, out_specs=[spec] * 3, out_shape=[shape] * 3,
        compiler_params=_params(),
    )(w, g, m, v)


def _pack(parts):
    flat = []
    for p in parts:
        v = p.reshape(-1)
        flat.append(jnp.pad(v, (0, (-v.shape[0]) % BLK)))
    v = jnp.concatenate(flat)
    v = jnp.pad(v, (0, (-v.shape[0]) % (8 * BLK)))
    return v.reshape(-1, BLK)


def _unpack(buf, shapes):
    flat = buf.reshape(-1)
    out, off = [], 0
    for shp in shapes:
        size = math.prod(shp)
        out.append(flat[off:off + size].reshape(shp))
        off += size + (-size) % BLK
    return out


def kernel(x, meta_tokens, norm_mix, w_in, q_norm, k_norm, attn_sinks, lam_re, lam_im, log_dt, ssm_b_re, ssm_b_im, ssm_c_re, ssm_c_im, ssm_d, w_glu, attn_branch_norm, ssm_branch_norm, w_out, norm_ffn, w_ffn_in, w_ffn_out, loss_target, m_meta_tokens, m_norm_mix, m_w_in, m_q_norm, m_k_norm, m_attn_sinks, m_lam_re, m_lam_im, m_log_dt, m_ssm_b_re, m_ssm_b_im, m_ssm_c_re, m_ssm_c_im, m_ssm_d, m_w_glu, m_attn_branch_norm, m_ssm_branch_norm, m_w_out, m_norm_ffn, m_w_ffn_in, m_w_ffn_out, v_meta_tokens, v_norm_mix, v_w_in, v_q_norm, v_k_norm, v_attn_sinks, v_lam_re, v_lam_im, v_log_dt, v_ssm_b_re, v_ssm_b_im, v_ssm_c_re, v_ssm_c_im, v_ssm_d, v_w_glu, v_attn_branch_norm, v_ssm_branch_norm, v_w_out, v_norm_ffn, v_w_ffn_in, v_w_ffn_out):
    args = dict(locals())
    weights = {n: args[n] for n in WEIGHTS}
    mom_m = {n: args["m_" + n] for n in WEIGHTS}
    mom_v = {n: args["v_" + n] for n in WEIGHTS}

    x2d = x[0]
    target2d = loss_target[0]
    s_len = x2d.shape[0]
    l_dim = s_len + BLK
    tm_row = _pick(l_dim, 320)
    tm_mm = _pick(l_dim, 1040)
    tl_tn = _pick(l_dim, 832)
    tm_ffn = _pick(l_dim, 640)
    tm_big = _pick(l_dim, 2080)

    shard_in = w_in[0].T.astype(BF16)
    shard_glu = w_glu[0].T.astype(BF16)
    shard_out = w_out[0].astype(BF16)
    shard_ffn_in = w_ffn_in[0].T.astype(BF16)
    shard_ffn_out = w_ffn_out[0].astype(BF16)
    shard_meta = meta_tokens.T
    me = _my_index()

    def landing(shard, row_fn=_plain_rows):
        m_rows, cols = shard.shape
        return lax.dynamic_update_slice(lax.empty((N_DEV * m_rows, cols), shard.dtype), shard,
                                        (row_fn(me, m_rows), 0))

    first = [shard_in, shard_meta]
    ga = _send_start("gather_start_a", first, [landing(s) for s in first], chunked=False)
    later = [shard_glu + ga[4][0:1, 0:1].astype(BF16), shard_out, shard_ffn_in, shard_ffn_out]
    later_fns = [_plain_rows, _plain_rows, _ffn_in_rows, _plain_rows]
    gb = _send_start("gather_start_b", later, [landing(s, f) for s, f in zip(later, later_fns)], chunked=False,
                     row_fns=later_fns)

    nm_t = norm_mix + (ga[4][0:1, 0:1] + gb[4][0:1, 0:1])
    qn_t, kn_t = jnp.tile(q_norm, (1, N_Q_HEADS)), jnp.tile(k_norm, (1, N_KV_HEADS))
    e_mat = jnp.kron(jnp.eye(4, dtype=F32), jnp.ones((HEAD_DIM, HEAD_DIM), F32)).astype(BF16)

    def disc(lr, li, ldt, br, bi):
        return _discretize(lr[0], li[0], ldt[0], br[0], bi[0])

    (abar_re, abar_im, bbar_re, bbar_im), disc_vjp = jax.vjp(disc, lam_re, lam_im, log_dt, ssm_b_re, ssm_b_im)
    wb_re, wb_im = _block_diag_b(bbar_re).astype(BF16), _block_diag_b(bbar_im).astype(BF16)
    wc_re, wc_im = _block_diag_c(ssm_c_re[0]).astype(BF16), _block_diag_c(ssm_c_im[0]).astype(BF16)
    tabs = _power_tables(abar_re.reshape(1, N_STATE), abar_im.reshape(1, N_STATE))

    h0, xn = _embed_norm(x2d, nm_t)
    wt_in, meta_t = _send_wait("gather_wait_a", ga[0], ga[1], ga[2], ga[3], xn, chunked=False)
    meta_pad = jnp.pad(meta_t.T, ((PAD, 0), (0, 0)))
    h0, xn = _embed_meta(meta_pad, nm_t, h0, xn)
    qkv = _matmul("proj_qkv", xn, wt_in, nt=True, tm=tm_big, tn=512, tk=D_MODEL, n=Q_W + 2 * KV_W, w_off=0)
    u = _matmul("proj_u", xn, wt_in, nt=True, tm=tm_big, tn=512, tk=D_MODEL, n=D_MODEL, w_off=3)
    gates = _matmul("proj_gates", xn, wt_in, nt=True, tm=tm_big, tn=512, tk=D_MODEL, n=2 * D_MODEL, w_off=5)
    qn, kf, vf = _qk_prep(qkv, qn_t, kn_t, e_mat, tm_row)
    attn = _attn_fwd(qn, kf, vf, attn_sinks)
    y, z, s_re, s_im = _ssm_fwd(u, wb_re, wb_im, wc_re, wc_im, ssm_d, tabs)
    wt_glu, w_out_f, wt_ffn_in, w_ffn_out_f = _send_wait("gather_wait_b", gb[0], gb[1], gb[2], gb[3], z,
                                                         chunked=False, row_fns=later_fns)
    zab = _matmul("glu_proj", z, wt_glu, nt=True, tm=tm_big, tn=1024, tk=D_MODEL)
    merged = _merge_fwd(attn, zab, gates, attn_branch_norm, ssm_branch_norm, tm_row)
    h1, hn = _matmul("out_proj", merged, w_out_f, nt=False, tm=tm_mm, tn=1024, tk=D_MODEL, res=h0,
                     norm_g=norm_ffn)
    gu, act = _ffn_in_swiglu(hn, wt_ffn_in, tm_ffn)
    h2 = _matmul("ffn_out", act, w_ffn_out_f, nt=False, tm=tm_mm, tn=1024, tk=1408, res=h1)
    dh2, dh2_b, loss_part = _loss_grad(h2, target2d)

    dgu = _d_act_swiglu(dh2_b, w_ffn_out_f, gu, tm_ffn)

    def exchange_start(name, grads_b, row_fns=None):
        return _send_start(name, grads_b, [jnp.zeros(g.shape, BF16) for g in grads_b], chunked=True,
                           row_fns=row_fns)

    g_ffn_out, g_ffn_out_b = _matmul_tn("g_ffn_out", act, dh2_b, tm=1408, tn=1024, tl=tl_tn)
    g_ffn_in_t, g_ffn_in_b = _matmul_tn("g_ffn_in", dgu, hn, tm=1408, tn=1024, tl=tl_tn)
    ffn_fns = [_ffn_in_rows, _plain_rows]
    ex1 = exchange_start("exchange_start_ffn", [g_ffn_in_b, g_ffn_out_b], ffn_fns)
    dhn = _matmul("d_hn", dgu, wt_ffn_in, nt=False, tm=tm_mm, tn=1024, tk=1408)
    dh1, dh1_b, g_norm_ffn = _norm_bwd_res("ffn_norm_bwd", h1, norm_ffn + ex1[4][0:1, 0:1], dhn, dh2, tm_row)
    dmerged = _matmul("d_merged", dh1_b, w_out_f, nt=True, tm=tm_big, tn=1024, tk=D_MODEL)
    dattn, dzab, dgates, g_abn, g_sbn = _merge_bwd(attn, zab, gates, attn_branch_norm, ssm_branch_norm,
                                                     dmerged, tm_row)
    g_out, g_out_b = _matmul_tn("g_out", merged, dh1_b, tm=1024, tn=1024, tl=tl_tn)
    dz = _matmul("d_z", dzab, wt_glu, nt=False, tm=tm_big, tn=1024, tk=1024)
    g_glu_t, g_glu_b = _matmul_tn("g_glu", dzab, z, tm=1024, tn=1024, tl=tl_tn)
    ex2 = exchange_start("exchange_start_mix", [g_glu_b, g_out_b])
    du, g_ssm_d, g_ar, g_ai, g_wbr, g_wbi, g_wcr, g_wci = _ssm_bwd(
        dz, y, u, s_re, s_im, wb_re, wb_im, wc_re, wc_im, ssm_d + ex2[4][0:1, 0:1], tabs)
    dq, dkc, dkp, dkm, dvc, dvp, dvm, g_sinks = _attn_bwd(qn, kf, vf, attn_sinks, attn, dattn)
    dqkv, g_qn_t, g_kn_t = _qk_bwd(qkv, qn_t, kn_t, e_mat, dq, dkc, dkp, dkm, dvc, dvp, dvm)
    g_lam_re, g_lam_im, g_log_dt, g_b_re, g_b_im = disc_vjp(
        (g_ar.reshape(SSM_GROUPS, SSM_STATE), g_ai.reshape(SSM_GROUPS, SSM_STATE),
         _block_diag_b_t(g_wbr), _block_diag_b_t(g_wbi)))
    small_grads = {
        "q_norm": g_qn_t.reshape(N_Q_HEADS, HEAD_DIM).sum(0)[None],
        "k_norm": g_kn_t.reshape(N_KV_HEADS, HEAD_DIM).sum(0)[None], "attn_sinks": g_sinks,
        "lam_re": g_lam_re, "lam_im": g_lam_im, "log_dt": g_log_dt, "ssm_b_re": g_b_re, "ssm_b_im": g_b_im,
        "ssm_c_re": _block_diag_c_t(g_wcr)[None], "ssm_c_im": _block_diag_c_t(g_wci)[None],
        "ssm_d": g_ssm_d, "attn_branch_norm": g_abn, "ssm_branch_norm": g_sbn, "norm_ffn": g_norm_ffn,
    }
    early = [n for n in SMALL if n != "norm_mix"]
    packed_e = _pack([small_grads[n] for n in early])
    gs_e = _send_start("small_start_a", [packed_e], [landing(packed_e)], chunked=False)

    dproj = jnp.concatenate([dqkv, du, dgates], axis=1)
    g_in_t, g_in_b = _matmul_tn("g_in", dproj, xn, tm=1152, tn=1024, tl=tl_tn, after=gs_e[4])
    ex3 = exchange_start("exchange_start_in", [g_in_b])
    dxn = _matmul("d_xn", dproj, wt_in, nt=False, tm=tm_mm, tn=1024, tk=1152)
    grad_x2d, dmeta_blk, g_norm_mix = _final_bwd(h0, nm_t + ex3[4][0:1, 0:1], dxn, dh1)
    packed_l = _pack([g_norm_mix, dmeta_blk[PAD:]])
    gs_l = _send_start("small_start_b", [packed_l], [landing(packed_l)], chunked=False)
    grads, deltas, new_m, new_v = {}, {}, {}, {}

    recv_ffn_in, recv_ffn_out = _send_wait("exchange_wait_ffn", ex1[0], ex1[1], ex1[2], ex1[3], gs_l[4],
                                           chunked=True, row_fns=ffn_fns)
    recv_glu, recv_out = _send_wait("exchange_wait_mix", ex2[0], ex2[1], ex2[2], ex2[3], recv_ffn_in,
                                    chunked=True)
    (recv_in,) = _send_wait("exchange_wait_in", ex3[0], ex3[1], ex3[2], ex3[3], recv_glu, chunked=True)
    big = [("w_in", g_in_t, True, recv_in, _plain_rows), ("w_glu", g_glu_t, True, recv_glu, _plain_rows),
           ("w_out", g_out, False, recv_out, _plain_rows), ("w_ffn_in", g_ffn_in_t, True, recv_ffn_in, _ffn_in_rows),
           ("w_ffn_out", g_ffn_out, False, recv_ffn_out, _plain_rows)]
    for name, g_full, transposed, recv, row_fn in big:
        m_rows = g_full.shape[0] // N_DEV
        own = lax.dynamic_slice(g_full, (row_fn(me, m_rows), 0), (m_rows, g_full.shape[1]))
        g_shard = _sum_slots("sum_" + name, recv, own)
        grads[name] = (g_shard.T if transposed else g_shard)[None]

    def adamw_2d(name):
        shp = weights[name].shape
        as2d = lambda a: a.reshape(shp[-2], shp[-1])
        d, nm, nv = _adamw("adamw_" + name, as2d(weights[name]), as2d(grads[name]), as2d(mom_m[name]),
                           as2d(mom_v[name]))
        deltas[name], new_m[name], new_v[name] = d.reshape(shp), nm.reshape(shp), nv.reshape(shp)
        return d

    for name in ["w_in", "w_glu", "w_out", "w_ffn_in", "w_ffn_out"]:
        last = adamw_2d(name)

    def small_update(tag, names, gs, packed, after, filler=()):
        (gathered,) = _send_wait("small_wait_" + tag, gs[0], gs[1], gs[2], gs[3], after, chunked=False)
        g_sum = _sum_gathered("sum_small_" + tag, gathered, packed.shape[0])
        d, nm, nv = _adamw("adamw_small_" + tag, _pack([weights[n] for n in names] + list(filler)), g_sum,
                           _pack([mom_m[n] for n in names] + list(filler)),
                           _pack([mom_v[n] for n in names] + list(filler)))
        shapes = [weights[n].shape for n in names]
        deltas.update(zip(names, _unpack(d, shapes)))
        new_m.update(zip(names, _unpack(nm, shapes)))
        new_v.update(zip(names, _unpack(nv, shapes)))
        return g_sum

    g_sum_e = small_update("a", early, gs_e, packed_e, last)
    grads.update(zip(early, _unpack(g_sum_e, [weights[n].shape for n in early])))
    g_sum_l = small_update("b", ["norm_mix"], gs_l, packed_l, g_sum_e, filler=[jnp.zeros((N_META, D_MODEL), F32)])
    grads["norm_mix"], g_meta = _unpack(g_sum_l, [weights["norm_mix"].shape, (N_META, D_MODEL)])
    grads["meta_tokens"] = lax.dynamic_slice(g_meta, (0, me * BLK), (N_META, BLK))
    adamw_2d("meta_tokens")

    loss = lax.psum(loss_part[0, 0], ("x", "y", "c"))
    return (loss, grad_x2d[None], *[grads[n] for n in WEIGHTS], *[deltas[n] for n in WEIGHTS],
            *[new_m[n] for n in WEIGHTS], *[new_v[n] for n in WEIGHTS])
```

```python
import math

import jax
import jax.numpy as jnp
from jax import lax
from jax.experimental import pallas as pl
from jax.experimental.pallas import tpu as pltpu

F32 = jnp.float32
BF16 = jnp.bfloat16

D_MODEL = 1024
N_META = 16
HEAD_DIM = 64
N_Q_HEADS = 16
N_KV_HEADS = 4
Q_W = N_Q_HEADS * HEAD_DIM
KV_W = N_KV_HEADS * HEAD_DIM
SSM_GROUPS = 64
SSM_GROUP_CH = 16
SSM_STATE = 64
N_STATE = SSM_GROUPS * SSM_STATE
D_FF = 2816
IN_COLS = Q_W + 2 * KV_W + 3 * D_MODEL
EPS = 1e-6
BLK = 128
PAD = BLK - N_META
N_DEV = 8
NEG = -1e30
SSM_KB = 8
ST_KB = N_STATE // SSM_KB
LB_KB = ST_KB // BLK
N_LB = N_STATE // BLK

ADAM_LR = 0.001
ADAM_B1 = 0.9
ADAM_B2 = 0.999
ADAM_EPS = 1e-08
ADAM_WD = 0.01
ADAM_STEP = 10

VMEM_LIMIT = 48 * 1024 * 1024
MESH = pl.DeviceIdType.MESH

SMALL = ["norm_mix", "q_norm", "k_norm", "attn_sinks", "lam_re", "lam_im", "log_dt", "ssm_b_re", "ssm_b_im",
         "ssm_c_re", "ssm_c_im", "ssm_d", "attn_branch_norm", "ssm_branch_norm", "norm_ffn"]
WEIGHTS = ["meta_tokens", "norm_mix", "w_in", "q_norm", "k_norm", "attn_sinks", "lam_re", "lam_im", "log_dt",
           "ssm_b_re", "ssm_b_im", "ssm_c_re", "ssm_c_im", "ssm_d", "w_glu", "attn_branch_norm",
           "ssm_branch_norm", "w_out", "norm_ffn", "w_ffn_in", "w_ffn_out"]


def _params(**kw):
    return pltpu.CompilerParams(vmem_limit_bytes=VMEM_LIMIT, **kw)


def _pick(n, cap, mult=16):
    best = None
    for d in range(mult, min(n, cap) + 1, mult):
        if n % d == 0:
            best = d
    assert best is not None, (n, cap, mult)
    return best


def _my_index():
    return 4 * lax.axis_index("x") + 2 * lax.axis_index("y") + lax.axis_index("c")


def _rms(x, g):
    r = lax.rsqrt(jnp.mean(x * x, axis=-1, keepdims=True) + EPS)
    return x * r * g


def _rms_bwd(x, g, dy):
    r = lax.rsqrt(jnp.mean(x * x, axis=-1, keepdims=True) + EPS)
    t = dy * g
    dx = r * t - x * (r * r * r) * jnp.mean(t * x, axis=-1, keepdims=True)
    dg = jnp.sum(dy * (x * r), axis=0, keepdims=True)
    return dx, dg


def _sigmoid(x):
    return 1.0 / (1.0 + jnp.exp(-x))


def _gelu(x):
    k = math.sqrt(2.0 / math.pi)
    return 0.5 * x * (1.0 + jnp.tanh(k * (x + 0.044715 * (x * x * x))))


def _gelu_grad(x):
    k = math.sqrt(2.0 / math.pi)
    t = jnp.tanh(k * (x + 0.044715 * (x * x * x)))
    return 0.5 * (1.0 + t) + 0.5 * x * (1.0 - t * t) * (k * (1.0 + 3.0 * 0.044715 * (x * x)))


def _head_mean(x, e_ref):
    hi = x.astype(BF16)
    r1 = x - hi.astype(F32)
    mid = r1.astype(BF16)
    lo = (r1 - mid.astype(F32)).astype(BF16)
    e = e_ref[...]
    out = []
    for b in range(x.shape[1] // 256):
        sl = slice(256 * b, 256 * b + 256)
        s = (jnp.dot(hi[:, sl], e, preferred_element_type=F32)
             + jnp.dot(mid[:, sl], e, preferred_element_type=F32)
             + jnp.dot(lo[:, sl], e, preferred_element_type=F32))
        out.append(s)
    s = out[0] if len(out) == 1 else jnp.concatenate(out, axis=1)
    return s * (1.0 / HEAD_DIM)


def _head_rms(x, g, e_ref):
    r = lax.rsqrt(_head_mean(x * x, e_ref) + EPS)
    return x * r * g


def _head_rms_bwd(x, g, dy, e_ref):
    r = lax.rsqrt(_head_mean(x * x, e_ref) + EPS)
    t = dy * g
    dx = r * t - x * (r * r * r) * _head_mean(t * x, e_ref)
    dg = jnp.sum(dy * (x * r), axis=0, keepdims=True)
    return dx, dg


def _lane_half(shape):
    lane = lax.broadcasted_iota(jnp.int32, shape, len(shape) - 1)
    return (lane >> 6) & 1


def _matmul(name, a, w, *, nt, tm, tn, tk, n=None, w_off=0, res=None, norm_g=None, out_dtype=F32):
    m_dim, k_dim = a.shape
    n_dim = n if n is not None else (w.shape[0] if nt else w.shape[1])
    gm, gn, gk = m_dim // tm, n_dim // tn, k_dim // tk
    assert gm * tm == m_dim and gn * tn == n_dim and gk * tk == k_dim, (name, a.shape, w.shape, tm, tn, tk)
    assert norm_g is None or tn == n_dim
    dn = (((1,), (1,)), ((), ())) if nt else (((1,), (0,)), ((), ()))

    def body(*refs):
        refs = list(refs)
        a_ref, w_ref = refs[0], refs[1]
        pos = 2
        r_ref = g_ref = on_ref = None
        if res is not None:
            r_ref, pos = refs[pos], pos + 1
        if norm_g is not None:
            g_ref, pos = refs[pos], pos + 1
        o_ref, pos = refs[pos], pos + 1
        if norm_g is not None:
            on_ref, pos = refs[pos], pos + 1
        acc = refs[pos]
        k = pl.program_id(2)

        @pl.when(k == 0)
        def _():
            acc[...] = jnp.zeros_like(acc)

        acc[...] += lax.dot_general(a_ref[...], w_ref[...], dn, preferred_element_type=F32)

        @pl.when(k == gk - 1)
        def _():
            r = acc[...]
            if r_ref is not None:
                r = r_ref[...] + r
            o_ref[...] = r.astype(out_dtype)
            if on_ref is not None:
                on_ref[...] = _rms(r, g_ref[...]).astype(BF16)

    if nt:
        w_spec = pl.BlockSpec((tn, tk), lambda i, j, k: (j + w_off, k))
    else:
        w_spec = pl.BlockSpec((tk, tn), lambda i, j, k: (k, j))
    in_specs = [pl.BlockSpec((tm, tk), lambda i, j, k: (i, k)), w_spec]
    args = [a, w]
    out_spec = pl.BlockSpec((tm, tn), lambda i, j, k: (i, j))
    out_specs, out_shape = [out_spec], [jax.ShapeDtypeStruct((m_dim, n_dim), out_dtype)]
    if res is not None:
        in_specs.append(out_spec)
        args.append(res)
    if norm_g is not None:
        in_specs.append(pl.BlockSpec((1, tn), lambda i, j, k: (0, 0)))
        args.append(norm_g)
        out_specs.append(out_spec)
        out_shape.append(jax.ShapeDtypeStruct((m_dim, n_dim), BF16))
    out = pl.pallas_call(
        body, name=name, grid=(gm, gn, gk),
        in_specs=in_specs, out_specs=out_specs, out_shape=out_shape,
        scratch_shapes=[pltpu.VMEM((tm, tn), F32)],
        compiler_params=_params(dimension_semantics=("parallel", "parallel", "arbitrary")),
    )(*args)
    return out if norm_g is not None else out[0]


def _matmul_tn(name, a, b, *, tm, tn, tl, after=None):
    l_dim, m_dim = a.shape
    n_dim = b.shape[1]
    gm, gn, gl = m_dim // tm, n_dim // tn, l_dim // tl
    assert gm * tm == m_dim and gn * tn == n_dim and gl * tl == l_dim, (name, a.shape, b.shape, tm, tn, tl)

    def body(*refs):
        a_ref, b_ref = refs[0], refs[1]
        o_ref, ob_ref = refs[-2], refs[-1]

        @pl.when(pl.program_id(2) == 0)
        def _():
            o_ref[...] = jnp.zeros_like(o_ref)

        o_ref[...] += lax.dot_general(a_ref[...], b_ref[...], (((0,), (0,)), ((), ())),
                                      preferred_element_type=F32)

        @pl.when(pl.program_id(2) == gl - 1)
        def _():
            ob_ref[...] = o_ref[...].astype(BF16)

    out_spec = pl.BlockSpec((tm, tn), lambda i, j, l: (i, j))
    in_specs = [pl.BlockSpec((tl, tm), lambda i, j, l: (l, i)), pl.BlockSpec((tl, tn), lambda i, j, l: (l, j))]
    args = [a, b]
    if after is not None:
        in_specs.append(pl.BlockSpec(memory_space=pl.ANY))
        args.append(after)
    return pl.pallas_call(
        body, name=name, grid=(gm, gn, gl),
        in_specs=in_specs,
        out_specs=[out_spec, out_spec],
        out_shape=[jax.ShapeDtypeStruct((m_dim, n_dim), F32), jax.ShapeDtypeStruct((m_dim, n_dim), BF16)],
        compiler_params=_params(dimension_semantics=("parallel", "parallel", "arbitrary")),
    )(*args)


def _row_spec(tm, cols, f=None):
    if f is None:
        return pl.BlockSpec((tm, cols), lambda i: (i, 0))
    return pl.BlockSpec((tm, cols), lambda i: (f(i), 0))


def _full_spec(shape):
    nd = len(shape)
    return pl.BlockSpec(shape, lambda i: (0,) * nd)


def _embed_norm(x2d, g):
    s_len = x2d.shape[0]
    nb = s_len // BLK + 1

    def body(x_ref, g_ref, h_ref, xn_ref):
        h_ref[...] = x_ref[...]
        xn_ref[...] = _rms(x_ref[...], g_ref[...]).astype(BF16)

    return pl.pallas_call(
        body, name="embed_norm", grid=(nb - 1,),
        in_specs=[_row_spec(BLK, D_MODEL), _full_spec((1, D_MODEL))],
        out_specs=[_row_spec(BLK, D_MODEL, lambda i: i + 1), _row_spec(BLK, D_MODEL, lambda i: i + 1)],
        out_shape=[jax.ShapeDtypeStruct((nb * BLK, D_MODEL), F32),
                   jax.ShapeDtypeStruct((nb * BLK, D_MODEL), BF16)],
        compiler_params=_params(),
    )(x2d, g)


def _embed_meta(meta_pad, g, h0, xn):
    def body(mp_ref, g_ref, h_in, xn_in, h_ref, xn_ref):
        h_ref[...] = mp_ref[...]
        xn_ref[...] = _rms(mp_ref[...], g_ref[...]).astype(BF16)

    any_spec = pl.BlockSpec(memory_space=pl.ANY)
    return pl.pallas_call(
        body, name="embed_meta", grid=(1,),
        in_specs=[_full_spec((BLK, D_MODEL)), _full_spec((1, D_MODEL)), any_spec, any_spec],
        out_specs=[_row_spec(BLK, D_MODEL), _row_spec(BLK, D_MODEL)],
        out_shape=[jax.ShapeDtypeStruct(h0.shape, F32), jax.ShapeDtypeStruct(xn.shape, BF16)],
        input_output_aliases={2: 0, 3: 1},
        compiler_params=_params(),
    )(meta_pad, g, h0, xn)


def _qk_prep(qkv, q_norm_t, k_norm_t, e_mat, tm):
    l_dim = qkv.shape[0]

    def body(x_ref, qg_ref, kg_ref, e_ref, q_ref, kf_ref, vf_ref):
        x = x_ref[...]
        q = _head_rms(x[:, :Q_W], qg_ref[...], e_ref) * (HEAD_DIM ** -0.5)
        q_ref[...] = q.astype(BF16)
        k = _head_rms(x[:, Q_W:Q_W + KV_W], kg_ref[...], e_ref)
        v = x[:, Q_W + KV_W:Q_W + 2 * KV_W]
        half = _lane_half((tm, BLK))
        for src, dst in ((k, kf_ref), (v, vf_ref)):
            for kv in range(N_KV_HEADS):
                blk = src[:, BLK * (kv // 2):BLK * (kv // 2) + BLK]
                swapped = pltpu.roll(blk, HEAD_DIM, axis=1)
                for e in range(2):
                    val = blk if kv % 2 == e else swapped
                    idx = 2 * kv + e
                    dst[:, BLK * idx:BLK * idx + BLK] = jnp.where(half == e, val, 0.0).astype(BF16)

    return pl.pallas_call(
        body, name="qk_prep", grid=(l_dim // tm,),
        in_specs=[_row_spec(tm, Q_W + 2 * KV_W), _full_spec((1, Q_W)), _full_spec((1, KV_W)),
                  _full_spec((256, 256))],
        out_specs=[_row_spec(tm, Q_W), _row_spec(tm, 8 * BLK), _row_spec(tm, 8 * BLK)],
        out_shape=[jax.ShapeDtypeStruct((l_dim, Q_W), BF16),
                   jax.ShapeDtypeStruct((l_dim, 8 * BLK), BF16),
                   jax.ShapeDtypeStruct((l_dim, 8 * BLK), BF16)],
        compiler_params=_params(),
    )(qkv, q_norm_t, k_norm_t, e_mat)


def _merge_fwd(attn, zab, gates, abn, sbn, tm):
    l_dim = attn.shape[0]

    def body(a_ref, z_ref, g_ref, an_ref, sn_ref, o_ref):
        z = z_ref[...]
        g = g_ref[...]
        ssm = z[:, :D_MODEL] * _sigmoid(z[:, D_MODEL:])
        merged = (_sigmoid(g[:, :D_MODEL]) * _rms(a_ref[...], an_ref[...])
                  + _sigmoid(g[:, D_MODEL:]) * _rms(ssm, sn_ref[...]))
        o_ref[...] = merged.astype(BF16)

    return pl.pallas_call(
        body, name="merge_fwd", grid=(l_dim // tm,),
        in_specs=[_row_spec(tm, D_MODEL), _row_spec(tm, 2 * D_MODEL), _row_spec(tm, 2 * D_MODEL),
                  _full_spec((1, D_MODEL)), _full_spec((1, D_MODEL))],
        out_specs=_row_spec(tm, D_MODEL),
        out_shape=jax.ShapeDtypeStruct((l_dim, D_MODEL), BF16),
        compiler_params=_params(),
    )(attn, zab, gates, abn, sbn)


def _merge_bwd(attn, zab, gates, abn, sbn, dmerged, tm):
    l_dim = attn.shape[0]

    def body(a_ref, z_ref, g_ref, an_ref, sn_ref, dm_ref, da_ref, dz_ref, dg_ref, dan_ref, dsn_ref):
        @pl.when(pl.program_id(0) == 0)
        def _():
            dan_ref[...] = jnp.zeros_like(dan_ref)
            dsn_ref[...] = jnp.zeros_like(dsn_ref)

        z = z_ref[...]
        g = g_ref[...]
        dm = dm_ref[...]
        attn_v = a_ref[...]
        za, zb = z[:, :D_MODEL], z[:, D_MODEL:]
        sb = _sigmoid(zb)
        ssm = za * sb
        s_ga, s_gs = _sigmoid(g[:, :D_MODEL]), _sigmoid(g[:, D_MODEL:])
        a_n = _rms(attn_v, an_ref[...])
        s_n = _rms(ssm, sn_ref[...])
        dg_ref[:, :D_MODEL] = (dm * a_n * s_ga * (1.0 - s_ga)).astype(BF16)
        dg_ref[:, D_MODEL:] = (dm * s_n * s_gs * (1.0 - s_gs)).astype(BF16)
        dattn, dan = _rms_bwd(attn_v, an_ref[...], dm * s_ga)
        dssm, dsn = _rms_bwd(ssm, sn_ref[...], dm * s_gs)
        da_ref[...] = dattn
        dz_ref[:, :D_MODEL] = (dssm * sb).astype(BF16)
        dz_ref[:, D_MODEL:] = (dssm * za * sb * (1.0 - sb)).astype(BF16)
        dan_ref[...] += dan
        dsn_ref[...] += dsn

    return pl.pallas_call(
        body, name="merge_bwd", grid=(l_dim // tm,),
        in_specs=[_row_spec(tm, D_MODEL), _row_spec(tm, 2 * D_MODEL), _row_spec(tm, 2 * D_MODEL),
                  _full_spec((1, D_MODEL)), _full_spec((1, D_MODEL)), _row_spec(tm, D_MODEL)],
        out_specs=[_row_spec(tm, D_MODEL), _row_spec(tm, 2 * D_MODEL), _row_spec(tm, 2 * D_MODEL),
                   _full_spec((1, D_MODEL)), _full_spec((1, D_MODEL))],
        out_shape=[jax.ShapeDtypeStruct((l_dim, D_MODEL), F32),
                   jax.ShapeDtypeStruct((l_dim, 2 * D_MODEL), BF16),
                   jax.ShapeDtypeStruct((l_dim, 2 * D_MODEL), BF16),
                   jax.ShapeDtypeStruct((1, D_MODEL), F32), jax.ShapeDtypeStruct((1, D_MODEL), F32)],
        compiler_params=_params(),
    )(attn, zab, gates, abn, sbn, dmerged)


FF_TILE = D_FF // 2


def _ffn_in_swiglu(hn, wt_ffn_in, tm):
    l_dim = hn.shape[0]
    nt = (((1,), (1,)), ((), ()))

    def body(a_ref, w_ref, gu_ref, act_ref):
        r = lax.dot_general(a_ref[...], w_ref[...], nt, preferred_element_type=F32)
        gate, up = r[:, :FF_TILE], r[:, FF_TILE:]
        gu_ref[...] = r.astype(BF16)
        act_ref[...] = (gate * _sigmoid(gate) * up).astype(BF16)

    return pl.pallas_call(
        body, name="ffn_in_swiglu", grid=(l_dim // tm, 2),
        in_specs=[pl.BlockSpec((tm, D_MODEL), lambda i, j: (i, 0)),
                  pl.BlockSpec((2 * FF_TILE, D_MODEL), lambda i, j: (j, 0))],
        out_specs=[pl.BlockSpec((tm, 2 * FF_TILE), lambda i, j: (i, j)),
                   pl.BlockSpec((tm, FF_TILE), lambda i, j: (i, j))],
        out_shape=[jax.ShapeDtypeStruct((l_dim, 2 * D_FF), BF16), jax.ShapeDtypeStruct((l_dim, D_FF), BF16)],
        compiler_params=_params(dimension_semantics=("parallel", "parallel")),
    )(hn, wt_ffn_in)


def _d_act_swiglu(dh2_b, w_ffn_out, gu, tm):
    l_dim = dh2_b.shape[0]
    nt = (((1,), (1,)), ((), ()))

    def body(d_ref, w_ref, gu_ref, o_ref):
        d = lax.dot_general(d_ref[...], w_ref[...], nt, preferred_element_type=F32)
        gate = gu_ref[:, :FF_TILE].astype(F32)
        up = gu_ref[:, FF_TILE:].astype(F32)
        s = _sigmoid(gate)
        o_ref[:, :FF_TILE] = (d * up * (s * (1.0 + gate * (1.0 - s)))).astype(BF16)
        o_ref[:, FF_TILE:] = (d * (gate * s)).astype(BF16)

    return pl.pallas_call(
        body, name="d_act_swiglu", grid=(l_dim // tm, 2),
        in_specs=[pl.BlockSpec((tm, D_MODEL), lambda i, j: (i, 0)),
                  pl.BlockSpec((FF_TILE, D_MODEL), lambda i, j: (j, 0)),
                  pl.BlockSpec((tm, 2 * FF_TILE), lambda i, j: (i, j))],
        out_specs=pl.BlockSpec((tm, 2 * FF_TILE), lambda i, j: (i, j)),
        out_shape=jax.ShapeDtypeStruct((l_dim, 2 * D_FF), BF16),
        compiler_params=_params(dimension_semantics=("parallel", "parallel")),
    )(dh2_b, w_ffn_out, gu)


def _loss_grad(h2, target2d):
    l_dim = h2.shape[0]
    nb = l_dim // BLK

    def body(h_ref, t_ref, d_ref, db_ref, loss_ref):
        i = pl.program_id(0)

        @pl.when(i == 0)
        def _():
            loss_ref[...] = jnp.zeros_like(loss_ref)
            d_ref[...] = jnp.zeros_like(d_ref)
            db_ref[...] = jnp.zeros_like(db_ref)

        @pl.when(i > 0)
        def _():
            err = h_ref[...] - t_ref[...]
            d = err * (1.0 / D_MODEL)
            d_ref[...] = d
            db_ref[...] = d.astype(BF16)
            loss_ref[...] += 0.5 * jnp.sum(jnp.mean(err * err, axis=-1, keepdims=True), axis=0, keepdims=True)

    return pl.pallas_call(
        body, name="loss_grad", grid=(nb,),
        in_specs=[_row_spec(BLK, D_MODEL), _row_spec(BLK, D_MODEL, lambda i: jnp.maximum(i - 1, 0))],
        out_specs=[_row_spec(BLK, D_MODEL), _row_spec(BLK, D_MODEL), _full_spec((1, 1))],
        out_shape=[jax.ShapeDtypeStruct((l_dim, D_MODEL), F32), jax.ShapeDtypeStruct((l_dim, D_MODEL), BF16),
                   jax.ShapeDtypeStruct((1, 1), F32)],
        compiler_params=_params(),
    )(h2, target2d)


def _norm_bwd_res(name, h, g, dy, dres, tm):
    l_dim = h.shape[0]

    def body(h_ref, g_ref, dy_ref, dr_ref, o_ref, ob_ref, dg_ref):
        @pl.when(pl.program_id(0) == 0)
        def _():
            dg_ref[...] = jnp.zeros_like(dg_ref)

        dx, dg = _rms_bwd(h_ref[...], g_ref[...], dy_ref[...])
        out = dr_ref[...] + dx
        o_ref[...] = out
        ob_ref[...] = out.astype(BF16)
        dg_ref[...] += dg

    return pl.pallas_call(
        body, name=name, grid=(l_dim // tm,),
        in_specs=[_row_spec(tm, D_MODEL), _full_spec((1, D_MODEL)), _row_spec(tm, D_MODEL), _row_spec(tm, D_MODEL)],
        out_specs=[_row_spec(tm, D_MODEL), _row_spec(tm, D_MODEL), _full_spec((1, D_MODEL))],
        out_shape=[jax.ShapeDtypeStruct((l_dim, D_MODEL), F32), jax.ShapeDtypeStruct((l_dim, D_MODEL), BF16),
                   jax.ShapeDtypeStruct((1, D_MODEL), F32)],
        compiler_params=_params(),
    )(h, g, dy, dres)


def _final_bwd(h0, g, dxn, dh1):
    l_dim = h0.shape[0]
    nb = l_dim // BLK

    def body(h_ref, g_ref, dy_ref, dr_ref, gx_ref, gm_ref, dg_ref):
        i = pl.program_id(0)

        @pl.when(i == 0)
        def _():
            dg_ref[...] = jnp.zeros_like(dg_ref)

        dx, dg = _rms_bwd(h_ref[...], g_ref[...], dy_ref[...])
        out = dr_ref[...] + dx
        dg_ref[...] += dg

        @pl.when(i == 0)
        def _():
            gm_ref[...] = out

        @pl.when(i > 0)
        def _():
            gx_ref[...] = out

    return pl.pallas_call(
        body, name="final_bwd", grid=(nb,),
        in_specs=[_row_spec(BLK, D_MODEL), _full_spec((1, D_MODEL)), _row_spec(BLK, D_MODEL),
                  _row_spec(BLK, D_MODEL)],
        out_specs=[_row_spec(BLK, D_MODEL, lambda i: jnp.maximum(i - 1, 0)), _full_spec((BLK, D_MODEL)),
                   _full_spec((1, D_MODEL))],
        out_shape=[jax.ShapeDtypeStruct((l_dim - BLK, D_MODEL), F32), jax.ShapeDtypeStruct((BLK, D_MODEL), F32),
                   jax.ShapeDtypeStruct((1, D_MODEL), F32)],
        compiler_params=_params(),
    )(h0, g, dxn, dh1)


def _attn_valid(n):
    shape = (2 * BLK, 3 * BLK)
    qi = lax.broadcasted_iota(jnp.int32, shape, 0) & (BLK - 1)
    col = lax.broadcasted_iota(jnp.int32, shape, 1)
    kj = col & (BLK - 1)
    part = col >> 7
    nn = jnp.zeros(shape, jnp.int32) + n
    meta_ok = (part == 0) & (kj >= PAD) & (nn >= 1)
    prev_ok = (part == 1) & (kj > qi) & (nn >= 2)
    cur_ok = (part == 2) & (kj <= qi) & ((nn >= 1) | (kj >= PAD))
    return meta_ok | prev_ok | cur_ok


def _attn_probs(q_ref, kwin, sk_ref, valid, kv, e):
    qs = jnp.concatenate([q_ref[:, BLK * (2 * kv):BLK * (2 * kv) + BLK],
                          q_ref[:, BLK * (2 * kv + 1):BLK * (2 * kv + 1) + BLK]], axis=0)
    s = lax.dot_general(qs, kwin, (((1,), (1,)), ((), ())), preferred_element_type=F32)
    s = jnp.where(valid, s, NEG)
    h0 = 4 * kv + e
    row = lax.broadcasted_iota(jnp.int32, (2 * BLK, 1), 0)
    sink = jnp.where(row < BLK, sk_ref[:, h0:h0 + 1], sk_ref[:, h0 + 2:h0 + 3])
    m = jnp.maximum(jnp.max(s, axis=-1, keepdims=True), sink)
    ex = jnp.exp(s - m)
    es = jnp.exp(sink - m)
    inv = 1.0 / (jnp.sum(ex, axis=-1, keepdims=True) + es)
    return qs, ex * inv, es * inv


def _attn_specs(nb):
    prev = lambda i: jnp.maximum(i - 1, 0)
    zero = lambda i: 0
    kv_specs = [_row_spec(BLK, 8 * BLK, zero), _row_spec(BLK, 8 * BLK, prev), _row_spec(BLK, 8 * BLK)]
    return kv_specs


def _attn_fwd(qn, kf, vf, sinks):
    l_dim = qn.shape[0]
    nb = l_dim // BLK

    def body(q_ref, km_ref, kp_ref, kc_ref, vm_ref, vp_ref, vc_ref, sk_ref, o_ref):
        valid = _attn_valid(pl.program_id(0))
        for kv in range(N_KV_HEADS):
            outs = []
            for e in range(2):
                sl = slice(BLK * (2 * kv + e), BLK * (2 * kv + e) + BLK)
                kwin = jnp.concatenate([km_ref[:, sl], kp_ref[:, sl], kc_ref[:, sl]], axis=0)
                vwin = jnp.concatenate([vm_ref[:, sl], vp_ref[:, sl], vc_ref[:, sl]], axis=0)
                _, p, _ = _attn_probs(q_ref, kwin, sk_ref, valid, kv, e)
                outs.append(jnp.dot(p.astype(BF16), vwin, preferred_element_type=F32))
            o = outs[0] + outs[1]
            o_ref[:, BLK * (2 * kv):BLK * (2 * kv) + BLK] = o[:BLK]
            o_ref[:, BLK * (2 * kv + 1):BLK * (2 * kv + 1) + BLK] = o[BLK:]

    kv_specs = _attn_specs(nb)
    return pl.pallas_call(
        body, name="attn_fwd", grid=(nb,),
        in_specs=[_row_spec(BLK, Q_W)] + kv_specs + kv_specs + [_full_spec((1, N_Q_HEADS))],
        out_specs=_row_spec(BLK, Q_W),
        out_shape=jax.ShapeDtypeStruct((l_dim, Q_W), F32),
        compiler_params=_params(),
    )(qn, kf, kf, kf, vf, vf, vf, sinks)


def _attn_bwd(qn, kf, vf, sinks, attn, dattn):
    l_dim = qn.shape[0]
    nb = l_dim // BLK
    wide = 8 * BLK

    def body(q_ref, km_ref, kp_ref, kc_ref, vm_ref, vp_ref, vc_ref, sk_ref, o_ref, do_ref,
             dq_ref, dkc_ref, dkp_ref, dkm_ref, dvc_ref, dvp_ref, dvm_ref, dsk_ref):
        @pl.when(pl.program_id(0) == 0)
        def _():
            dkm_ref[...] = jnp.zeros_like(dkm_ref)
            dvm_ref[...] = jnp.zeros_like(dvm_ref)
            dsk_ref[...] = jnp.zeros_like(dsk_ref)

        valid = _attn_valid(pl.program_id(0))
        half = _lane_half((BLK, BLK))
        lane16 = lax.broadcasted_iota(jnp.int32, (1, N_Q_HEADS), 1)
        dsk = jnp.zeros((1, N_Q_HEADS), F32)
        for kv in range(N_KV_HEADS):
            j0, j1 = 2 * kv, 2 * kv + 1
            do0 = do_ref[:, BLK * j0:BLK * j0 + BLK]
            do1 = do_ref[:, BLK * j1:BLK * j1 + BLK]
            prod0 = do0 * o_ref[:, BLK * j0:BLK * j0 + BLK]
            prod1 = do1 * o_ref[:, BLK * j1:BLK * j1 + BLK]
            dos = jnp.concatenate([do0, do1], axis=0).astype(BF16)
            dqs = []
            for e in range(2):
                sl = slice(BLK * (2 * kv + e), BLK * (2 * kv + e) + BLK)
                kwin = jnp.concatenate([km_ref[:, sl], kp_ref[:, sl], kc_ref[:, sl]], axis=0)
                vwin = jnp.concatenate([vm_ref[:, sl], vp_ref[:, sl], vc_ref[:, sl]], axis=0)
                qs, p, p_sink = _attn_probs(q_ref, kwin, sk_ref, valid, kv, e)
                delta = jnp.concatenate(
                    [jnp.sum(jnp.where(half == e, prod0, 0.0), axis=-1, keepdims=True),
                     jnp.sum(jnp.where(half == e, prod1, 0.0), axis=-1, keepdims=True)], axis=0)
                dp = lax.dot_general(dos, vwin, (((1,), (1,)), ((), ())), preferred_element_type=F32)
                ds = (p * (dp - delta)).astype(BF16)
                pb = p.astype(BF16)
                dqs.append(jnp.dot(ds, kwin, preferred_element_type=F32))
                dk = lax.dot_general(ds, qs, (((0,), (0,)), ((), ())), preferred_element_type=F32)
                dv = lax.dot_general(pb, dos, (((0,), (0,)), ((), ())), preferred_element_type=F32)
                dkm_ref[:, sl] += dk[:BLK]
                dkp_ref[:, sl] = dk[BLK:2 * BLK]
                dkc_ref[:, sl] = dk[2 * BLK:]
                dvm_ref[:, sl] += dv[:BLK]
                dvp_ref[:, sl] = dv[BLK:2 * BLK]
                dvc_ref[:, sl] = dv[2 * BLK:]
                sink_g = -(p_sink * delta)
                g_lo = jnp.sum(sink_g[:BLK], axis=0, keepdims=True)
                g_hi = jnp.sum(sink_g[BLK:], axis=0, keepdims=True)
                dsk = dsk + jnp.where(lane16 == 4 * kv + e, g_lo, 0.0) + jnp.where(lane16 == 4 * kv + 2 + e, g_hi, 0.0)
            dq = jnp.where(jnp.concatenate([half, half], axis=0) == 0, dqs[0], dqs[1])
            dq_ref[:, BLK * j0:BLK * j0 + BLK] = dq[:BLK]
            dq_ref[:, BLK * j1:BLK * j1 + BLK] = dq[BLK:]
        dsk_ref[...] += dsk

    kv_specs = _attn_specs(nb)
    row_wide = _row_spec(BLK, wide)
    acc_wide = _full_spec((BLK, wide))
    big = jax.ShapeDtypeStruct((l_dim, wide), F32)
    return pl.pallas_call(
        body, name="attn_bwd", grid=(nb,),
        in_specs=[_row_spec(BLK, Q_W)] + kv_specs + kv_specs
        + [_full_spec((1, N_Q_HEADS)), _row_spec(BLK, Q_W), _row_spec(BLK, Q_W)],
        out_specs=[_row_spec(BLK, Q_W), row_wide, row_wide, acc_wide, row_wide, row_wide, acc_wide,
                   _full_spec((1, N_Q_HEADS))],
        out_shape=[jax.ShapeDtypeStruct((l_dim, Q_W), F32), big, big, jax.ShapeDtypeStruct((BLK, wide), F32),
                   big, big, jax.ShapeDtypeStruct((BLK, wide), F32), jax.ShapeDtypeStruct((1, N_Q_HEADS), F32)],
        compiler_params=_params(),
    )(qn, kf, kf, kf, vf, vf, vf, sinks, attn, dattn)


def _qk_bwd(qkv, q_norm_t, k_norm_t, e_mat, dq, dkc, dkp, dkm, dvc, dvp, dvm):
    l_dim = qkv.shape[0]
    nb = l_dim // BLK
    wide = 8 * BLK

    def fold(x):
        half = _lane_half((BLK, BLK))
        blocks = []
        for kb in range(2):
            t = []
            for kv in (2 * kb, 2 * kb + 1):
                own = kv % 2
                a = x[:, BLK * (2 * kv + own):BLK * (2 * kv + own) + BLK]
                b = pltpu.roll(x[:, BLK * (2 * kv + 1 - own):BLK * (2 * kv + 1 - own) + BLK], HEAD_DIM, axis=1)
                t.append(a + b)
            blocks.append(jnp.where(half == 0, t[0], t[1]))
        return jnp.concatenate(blocks, axis=1)

    def body(x_ref, qg_ref, kg_ref, e_ref, dq_ref, dkc_ref, dkp_ref, dkm_ref, dvc_ref, dvp_ref, dvm_ref,
             o_ref, dqg_ref, dkg_ref):
        i = pl.program_id(0)

        @pl.when(i == 0)
        def _():
            dqg_ref[...] = jnp.zeros_like(dqg_ref)
            dkg_ref[...] = jnp.zeros_like(dkg_ref)

        first = jnp.where(i == 0, 1.0, 0.0)
        not_last = jnp.where(i < nb - 1, 1.0, 0.0)
        dk_x = dkc_ref[...] + not_last * dkp_ref[...] + first * dkm_ref[...]
        dv_x = dvc_ref[...] + not_last * dvp_ref[...] + first * dvm_ref[...]
        x = x_ref[...]
        dqx, dqg = _head_rms_bwd(x[:, :Q_W], qg_ref[...], dq_ref[...] * (HEAD_DIM ** -0.5), e_ref)
        dkx, dkg = _head_rms_bwd(x[:, Q_W:Q_W + KV_W], kg_ref[...], fold(dk_x), e_ref)
        o_ref[:, :Q_W] = dqx.astype(BF16)
        o_ref[:, Q_W:Q_W + KV_W] = dkx.astype(BF16)
        o_ref[:, Q_W + KV_W:] = fold(dv_x).astype(BF16)
        dqg_ref[...] += dqg
        dkg_ref[...] += dkg

    nxt = lambda i: jnp.minimum(i + 1, nb - 1)
    row_wide = _row_spec(BLK, wide)
    nxt_wide = _row_spec(BLK, wide, nxt)
    acc_wide = _full_spec((BLK, wide))
    return pl.pallas_call(
        body, name="qk_bwd", grid=(nb,),
        in_specs=[_row_spec(BLK, Q_W + 2 * KV_W), _full_spec((1, Q_W)), _full_spec((1, KV_W)),
                  _full_spec((256, 256)), _row_spec(BLK, Q_W),
                  row_wide, nxt_wide, acc_wide, row_wide, nxt_wide, acc_wide],
        out_specs=[_row_spec(BLK, Q_W + 2 * KV_W), _full_spec((1, Q_W)), _full_spec((1, KV_W))],
        out_shape=[jax.ShapeDtypeStruct((l_dim, Q_W + 2 * KV_W), BF16),
                   jax.ShapeDtypeStruct((1, Q_W), F32), jax.ShapeDtypeStruct((1, KV_W), F32)],
        compiler_params=_params(),
    )(qkv, q_norm_t, k_norm_t, e_mat, dq, dkc, dkp, dkm, dvc, dvp, dvm)


GRP = 8


def _strided(r, g):
    return pl.ds(r, g, stride=GRP)


def _slab(ref, base, r, g):
    return jnp.concatenate([ref[base + i, _strided(r, g), :] for i in range(LB_KB)], axis=1)


def _slab_store(ref, base, r, g, val):
    for i in range(LB_KB):
        ref[base + i, _strided(r, g), :] = val[:, BLK * i:BLK * i + BLK]


def _group_totals(xr_ref, xi_ref, base, ar, ai, reverse):
    g = xr_ref.shape[1] // GRP
    sr = si = None
    for r in (range(GRP - 1, -1, -1) if reverse else range(GRP)):
        xr, xi = _slab(xr_ref, base, r, g), _slab(xi_ref, base, r, g)
        if sr is not None:
            xr, xi = xr + ar * sr - ai * si, xi + ar * si + ai * sr
        sr, si = xr, xi
    return sr, si


def _carry_scan(tr, ti, sqr_ref, sqi_ref, lanes, sign, reverse):
    g = tr.shape[0]
    row = lax.broadcasted_iota(jnp.int32, tr.shape, 0)
    idx, s = 0, 1
    while s < g:
        ar = sqr_ref[idx:idx + 1, lanes]
        ai = sign * sqi_ref[idx:idx + 1, lanes]
        shift, keep = (g - s, row < g - s) if reverse else (s, row >= s)
        pr = jnp.where(keep, pltpu.roll(tr, shift, axis=0), 0.0)
        pi = jnp.where(keep, pltpu.roll(ti, shift, axis=0), 0.0)
        tr, ti = tr + ar * pr - ai * pi, ti + ar * pi + ai * pr
        idx, s = idx + 1, 2 * s
    return tr, ti


def _ssm_fwd(u, wb_re, wb_im, wc_re, wc_im, d_skip, tabs):
    l_dim = u.shape[0]
    nb = l_dim // BLK
    g = BLK // GRP

    def body(u_ref, wbr_ref, wbi_ref, wcr_ref, wci_ref, d_ref, a1r_ref, a1i_ref, sqr_ref, sqi_ref,
             seqr_ref, seqi_ref, y_ref, z_ref, sr_ref, si_ref, cr_ref, ci_ref, xr_ref, xi_ref):
        @pl.when(pl.program_id(0) == 0)
        def _():
            cr_ref[...] = jnp.zeros_like(cr_ref)
            ci_ref[...] = jnp.zeros_like(ci_ref)

        row = lax.broadcasted_iota(jnp.int32, (g, ST_KB), 0)
        for kb in range(SSM_KB):
            ch = slice(BLK * kb, BLK * kb + BLK)
            lanes = slice(ST_KB * kb, ST_KB * kb + ST_KB)
            u_kb = u_ref[:, ch]
            ub = u_kb.astype(BF16)
            xr = jnp.dot(ub, wbr_ref[kb], preferred_element_type=F32)
            xi = jnp.dot(ub, wbi_ref[kb], preferred_element_type=F32)
            base = LB_KB * kb
            for i in range(LB_KB):
                xr_ref[base + i] = xr[:, BLK * i:BLK * i + BLK]
                xi_ref[base + i] = xi[:, BLK * i:BLK * i + BLK]
            ar, ai = a1r_ref[0:1, lanes], a1i_ref[0:1, lanes]
            tr, ti = _group_totals(xr_ref, xi_ref, base, ar, ai, reverse=False)
            tr, ti = _carry_scan(tr, ti, sqr_ref, sqi_ref, lanes, 1.0, reverse=False)
            cin_r, cin_i = cr_ref[0:1, lanes], ci_ref[0:1, lanes]
            qr, qi = seqr_ref[:, lanes], seqi_ref[:, lanes]
            tr, ti = tr + qr * cin_r - qi * cin_i, ti + qr * cin_i + qi * cin_r
            cr_ref[0:1, lanes] = jnp.sum(jnp.where(row == g - 1, tr, 0.0), axis=0, keepdims=True)
            ci_ref[0:1, lanes] = jnp.sum(jnp.where(row == g - 1, ti, 0.0), axis=0, keepdims=True)
            pr = jnp.where(row == 0, cin_r, pltpu.roll(tr, 1, axis=0))
            pi = jnp.where(row == 0, cin_i, pltpu.roll(ti, 1, axis=0))
            for r in range(GRP):
                pr, pi = (_slab(xr_ref, base, r, g) + ar * pr - ai * pi,
                          _slab(xi_ref, base, r, g) + ar * pi + ai * pr)
                _slab_store(sr_ref, base, r, g, pr)
                _slab_store(si_ref, base, r, g, pi)
            s_r = jnp.concatenate([sr_ref[LB_KB * kb + i] for i in range(LB_KB)], axis=1)
            s_i = jnp.concatenate([si_ref[LB_KB * kb + i] for i in range(LB_KB)], axis=1)
            y = (jnp.dot(s_r.astype(BF16), wcr_ref[kb], preferred_element_type=F32)
                 - jnp.dot(s_i.astype(BF16), wci_ref[kb], preferred_element_type=F32)
                 + d_ref[:, ch] * u_kb)
            y_ref[:, ch] = y
            z_ref[:, ch] = _gelu(y).astype(BF16)

    wb_spec = _full_spec((SSM_KB, BLK, ST_KB))
    wc_spec = _full_spec((SSM_KB, ST_KB, BLK))
    tab_specs = [_full_spec(t.shape) for t in tabs[:6]]
    state_spec = pl.BlockSpec((N_LB, BLK, BLK), lambda i: (0, i, 0))
    state_shape = jax.ShapeDtypeStruct((N_LB, l_dim, BLK), F32)
    return pl.pallas_call(
        body, name="ssm_fwd", grid=(nb,),
        in_specs=[_row_spec(BLK, D_MODEL), wb_spec, wb_spec, wc_spec, wc_spec, _full_spec((1, D_MODEL))]
        + tab_specs,
        out_specs=[_row_spec(BLK, D_MODEL), _row_spec(BLK, D_MODEL), state_spec, state_spec],
        out_shape=[jax.ShapeDtypeStruct((l_dim, D_MODEL), F32), jax.ShapeDtypeStruct((l_dim, D_MODEL), BF16),
                   state_shape, state_shape],
        scratch_shapes=[pltpu.VMEM((8, N_STATE), F32), pltpu.VMEM((8, N_STATE), F32),
                        pltpu.VMEM((N_LB, BLK, BLK), F32), pltpu.VMEM((N_LB, BLK, BLK), F32)],
        compiler_params=_params(),
    )(u, wb_re, wb_im, wc_re, wc_im, d_skip, *tabs[:6])


def _ssm_bwd(dz, y, u, s_re, s_im, wb_re, wb_im, wc_re, wc_im, d_skip, tabs):
    l_dim = u.shape[0]
    nb = l_dim // BLK
    g = BLK // GRP

    def body(dz_ref, y_ref, u_ref, sr_ref, si_ref, wbr_ref, wbi_ref, wcr_ref, wci_ref, d_ref,
             a1r_ref, a1i_ref, sqr_ref, sqi_ref, revr_ref, revi_ref,
             du_ref, dd_ref, dar_ref, dai_ref, dwbr_ref, dwbi_ref, dwcr_ref, dwci_ref,
             cr_ref, ci_ref, gr_ref, gi_ref):
        @pl.when(pl.program_id(0) == 0)
        def _():
            for r in (cr_ref, ci_ref, dd_ref, dar_ref, dai_ref, dwbr_ref, dwbi_ref, dwcr_ref, dwci_ref):
                r[...] = jnp.zeros_like(r)

        tn = (((0,), (0,)), ((), ()))
        nt = (((1,), (1,)), ((), ()))
        row = lax.broadcasted_iota(jnp.int32, (g, ST_KB), 0)
        for kb in range(SSM_KB):
            ch = slice(BLK * kb, BLK * kb + BLK)
            lanes = slice(ST_KB * kb, ST_KB * kb + ST_KB)
            u_kb = u_ref[:, ch]
            dy = dz_ref[:, ch] * _gelu_grad(y_ref[:, ch])
            dyb = dy.astype(BF16)
            ub = u_kb.astype(BF16)
            dd_ref[:, ch] += jnp.sum(dy * u_kb, axis=0, keepdims=True)
            ds_r = lax.dot_general(dyb, wcr_ref[kb], nt, preferred_element_type=F32)
            ds_i = -lax.dot_general(dyb, wci_ref[kb], nt, preferred_element_type=F32)
            base = LB_KB * kb
            for i in range(LB_KB):
                gr_ref[base + i] = ds_r[:, BLK * i:BLK * i + BLK]
                gi_ref[base + i] = ds_i[:, BLK * i:BLK * i + BLK]
            ar, ai = a1r_ref[0:1, lanes], -a1i_ref[0:1, lanes]
            tr, ti = _group_totals(gr_ref, gi_ref, base, ar, ai, reverse=True)
            tr, ti = _carry_scan(tr, ti, sqr_ref, sqi_ref, lanes, -1.0, reverse=True)
            cin_r, cin_i = cr_ref[0:1, lanes], ci_ref[0:1, lanes]
            qr, qi = revr_ref[:, lanes], -revi_ref[:, lanes]
            tr, ti = tr + qr * cin_r - qi * cin_i, ti + qr * cin_i + qi * cin_r
            cr_ref[0:1, lanes] = jnp.sum(jnp.where(row == 0, tr, 0.0), axis=0, keepdims=True)
            ci_ref[0:1, lanes] = jnp.sum(jnp.where(row == 0, ti, 0.0), axis=0, keepdims=True)
            nr = jnp.where(row == g - 1, cin_r, pltpu.roll(tr, g - 1, axis=0))
            ni = jnp.where(row == g - 1, cin_i, pltpu.roll(ti, g - 1, axis=0))
            acc_r = jnp.zeros((g, ST_KB), F32)
            acc_i = jnp.zeros((g, ST_KB), F32)
            for r in range(GRP - 1, -1, -1):
                s_r, s_i = _slab(sr_ref, base, r, g), _slab(si_ref, base, r, g)
                acc_r = acc_r + (nr * s_r + ni * s_i)
                acc_i = acc_i + (ni * s_r - nr * s_i)
                nr, ni = (_slab(gr_ref, base, r, g) + ar * nr - ai * ni,
                          _slab(gi_ref, base, r, g) + ar * ni + ai * nr)
                _slab_store(gr_ref, base, r, g, nr)
                _slab_store(gi_ref, base, r, g, ni)
            dar_ref[:, lanes] += jnp.sum(acc_r, axis=0, keepdims=True)
            dai_ref[:, lanes] += jnp.sum(acc_i, axis=0, keepdims=True)
            grb = jnp.concatenate([gr_ref[base + i] for i in range(LB_KB)], axis=1).astype(BF16)
            gib = jnp.concatenate([gi_ref[base + i] for i in range(LB_KB)], axis=1).astype(BF16)
            srb = jnp.concatenate([sr_ref[base + i] for i in range(LB_KB)], axis=1).astype(BF16)
            sib = jnp.concatenate([si_ref[base + i] for i in range(LB_KB)], axis=1).astype(BF16)
            du = (lax.dot_general(grb, wbr_ref[kb], nt, preferred_element_type=F32)
                  + lax.dot_general(gib, wbi_ref[kb], nt, preferred_element_type=F32)
                  + d_ref[:, ch] * dy)
            du_ref[:, ch] = du.astype(BF16)
            dwbr_ref[kb] += lax.dot_general(ub, grb, tn, preferred_element_type=F32)
            dwbi_ref[kb] += lax.dot_general(ub, gib, tn, preferred_element_type=F32)
            dwcr_ref[kb] += lax.dot_general(srb, dyb, tn, preferred_element_type=F32)
            dwci_ref[kb] -= lax.dot_general(sib, dyb, tn, preferred_element_type=F32)

    rev = lambda i: nb - 1 - i
    wb_spec = _full_spec((SSM_KB, BLK, ST_KB))
    wc_spec = _full_spec((SSM_KB, ST_KB, BLK))
    tab_in = [tabs[0], tabs[1], tabs[2], tabs[3], tabs[6], tabs[7]]
    tab_specs = [_full_spec(t.shape) for t in tab_in]
    vec = _full_spec((1, D_MODEL))
    svec = _full_spec((1, N_STATE))
    state_spec = pl.BlockSpec((N_LB, BLK, BLK), lambda i: (0, nb - 1 - i, 0))
    return pl.pallas_call(
        body, name="ssm_bwd", grid=(nb,),
        in_specs=[_row_spec(BLK, D_MODEL, rev), _row_spec(BLK, D_MODEL, rev), _row_spec(BLK, D_MODEL, rev),
                  state_spec, state_spec,
                  wb_spec, wb_spec, wc_spec, wc_spec, vec] + tab_specs,
        out_specs=[_row_spec(BLK, D_MODEL, rev), vec, svec, svec, wb_spec, wb_spec, wc_spec, wc_spec],
        out_shape=[jax.ShapeDtypeStruct((l_dim, D_MODEL), BF16), jax.ShapeDtypeStruct((1, D_MODEL), F32),
                   jax.ShapeDtypeStruct((1, N_STATE), F32), jax.ShapeDtypeStruct((1, N_STATE), F32),
                   jax.ShapeDtypeStruct((SSM_KB, BLK, ST_KB), F32), jax.ShapeDtypeStruct((SSM_KB, BLK, ST_KB), F32),
                   jax.ShapeDtypeStruct((SSM_KB, ST_KB, BLK), F32), jax.ShapeDtypeStruct((SSM_KB, ST_KB, BLK), F32)],
        scratch_shapes=[pltpu.VMEM((8, N_STATE), F32), pltpu.VMEM((8, N_STATE), F32),
                        pltpu.VMEM((N_LB, BLK, BLK), F32), pltpu.VMEM((N_LB, BLK, BLK), F32)],
        compiler_params=_params(),
    )(dz, y, u, s_re, s_im, wb_re, wb_im, wc_re, wc_im, d_skip, *tab_in)


def _lane_slab(ref, i, r, g):
    return ref[i, _strided(r, g), :]


def _chunk_carries(xr_ref, xi_ref, i, ar, ai, sqr_ref, sqi_ref, seq_r, seq_i, cin_r, cin_i, sign, reverse):
    g = xr_ref.shape[1] // GRP
    sr = si = None
    for r in (range(GRP - 1, -1, -1) if reverse else range(GRP)):
        xr, xi = _lane_slab(xr_ref, i, r, g), _lane_slab(xi_ref, i, r, g)
        if sr is not None:
            xr, xi = xr + ar * sr - ai * si, xi + ar * si + ai * sr
        sr, si = xr, xi
    row = lax.broadcasted_iota(jnp.int32, sr.shape, 0)
    idx, s = 0, 1
    while s < g:
        br = sqr_ref[idx:idx + 1, BLK * i:BLK * i + BLK]
        bi = sign * sqi_ref[idx:idx + 1, BLK * i:BLK * i + BLK]
        shift, keep = (g - s, row < g - s) if reverse else (s, row >= s)
        pr = jnp.where(keep, pltpu.roll(sr, shift, axis=0), 0.0)
        pi = jnp.where(keep, pltpu.roll(si, shift, axis=0), 0.0)
        sr, si = sr + br * pr - bi * pi, si + br * pi + bi * pr
        idx, s = idx + 1, 2 * s
    return sr + seq_r * cin_r - seq_i * cin_i, si + seq_r * cin_i + seq_i * cin_r


def _ssm_fwd_kb(u, wb_re, wb_im, wc_re, wc_im, d_skip, tabs, q):
    l_dim = u.shape[0]
    nc = l_dim // q
    g = q // GRP

    def body(u_ref, wbr_ref, wbi_ref, wcr_ref, wci_ref, d_ref, a1r_ref, a1i_ref, sqr_ref, sqi_ref,
             seqr_ref, seqi_ref, y_ref, z_ref, sr_ref, si_ref, cr_ref, ci_ref, xr_ref, xi_ref):
        @pl.when(pl.program_id(1) == 0)
        def _():
            cr_ref[...] = jnp.zeros_like(cr_ref)
            ci_ref[...] = jnp.zeros_like(ci_ref)

        u_kb = u_ref[...]
        ub = u_kb.astype(BF16)
        xr = jnp.dot(ub, wbr_ref[0], preferred_element_type=F32)
        xi = jnp.dot(ub, wbi_ref[0], preferred_element_type=F32)
        for i in range(LB_KB):
            xr_ref[i] = xr[:, BLK * i:BLK * i + BLK]
            xi_ref[i] = xi[:, BLK * i:BLK * i + BLK]
        row = lax.broadcasted_iota(jnp.int32, (g, BLK), 0)
        for i in range(LB_KB):
            lanes = slice(BLK * i, BLK * i + BLK)
            ar, ai = a1r_ref[0:1, lanes], a1i_ref[0:1, lanes]
            cin_r, cin_i = cr_ref[0:1, lanes], ci_ref[0:1, lanes]
            tr, ti = _chunk_carries(xr_ref, xi_ref, i, ar, ai, sqr_ref, sqi_ref, seqr_ref[:, lanes],
                                    seqi_ref[:, lanes], cin_r, cin_i, 1.0, reverse=False)
            cr_ref[0:1, lanes] = jnp.sum(jnp.where(row == g - 1, tr, 0.0), axis=0, keepdims=True)
            ci_ref[0:1, lanes] = jnp.sum(jnp.where(row == g - 1, ti, 0.0), axis=0, keepdims=True)
            pr = jnp.where(row == 0, cin_r, pltpu.roll(tr, 1, axis=0))
            pi = jnp.where(row == 0, cin_i, pltpu.roll(ti, 1, axis=0))
            for r in range(GRP):
                pr, pi = (_lane_slab(xr_ref, i, r, g) + ar * pr - ai * pi,
                          _lane_slab(xi_ref, i, r, g) + ar * pi + ai * pr)
                sr_ref[i, _strided(r, g), :] = pr
                si_ref[i, _strided(r, g), :] = pi
        s_r = jnp.concatenate([sr_ref[i] for i in range(LB_KB)], axis=1)
        s_i = jnp.concatenate([si_ref[i] for i in range(LB_KB)], axis=1)
        y = (jnp.dot(s_r.astype(BF16), wcr_ref[0], preferred_element_type=F32)
             - jnp.dot(s_i.astype(BF16), wci_ref[0], preferred_element_type=F32)
             + d_ref[...] * u_kb)
        y_ref[...] = y
        z_ref[...] = _gelu(y).astype(BF16)

    chan = pl.BlockSpec((q, BLK), lambda k, c: (c, k))
    wb_spec = pl.BlockSpec((1, BLK, ST_KB), lambda k, c: (k, 0, 0))
    wc_spec = pl.BlockSpec((1, ST_KB, BLK), lambda k, c: (k, 0, 0))
    tab_specs = [pl.BlockSpec((t.shape[0], ST_KB), lambda k, c: (0, k)) for t in tabs[:6]]
    state_spec = pl.BlockSpec((LB_KB, q, BLK), lambda k, c: (k, c, 0))
    state_shape = jax.ShapeDtypeStruct((N_LB, l_dim, BLK), F32)
    return pl.pallas_call(
        body, name="ssm_fwd", grid=(SSM_KB, nc),
        in_specs=[chan, wb_spec, wb_spec, wc_spec, wc_spec, pl.BlockSpec((1, BLK), lambda k, c: (0, k))] + tab_specs,
        out_specs=[chan, chan, state_spec, state_spec],
        out_shape=[jax.ShapeDtypeStruct((l_dim, D_MODEL), F32), jax.ShapeDtypeStruct((l_dim, D_MODEL), BF16),
                   state_shape, state_shape],
        scratch_shapes=[pltpu.VMEM((8, ST_KB), F32), pltpu.VMEM((8, ST_KB), F32),
                        pltpu.VMEM((LB_KB, q, BLK), F32), pltpu.VMEM((LB_KB, q, BLK), F32)],
        compiler_params=_params(dimension_semantics=("parallel", "arbitrary")),
    )(u, wb_re, wb_im, wc_re, wc_im, d_skip, *tabs[:6])


def _ssm_bwd_kb(dz, y, u, s_re, s_im, wb_re, wb_im, wc_re, wc_im, d_skip, tabs, q):
    l_dim = u.shape[0]
    nc = l_dim // q
    g = q // GRP

    def body(dz_ref, y_ref, u_ref, sr_ref, si_ref, wbr_ref, wbi_ref, wcr_ref, wci_ref, d_ref,
             a1r_ref, a1i_ref, sqr_ref, sqi_ref, revr_ref, revi_ref,
             du_ref, dd_ref, dar_ref, dai_ref, dwbr_ref, dwbi_ref, dwcr_ref, dwci_ref,
             cr_ref, ci_ref, gr_ref, gi_ref):
        @pl.when(pl.program_id(1) == 0)
        def _():
            for ref in (cr_ref, ci_ref, dd_ref, dar_ref, dai_ref, dwbr_ref, dwbi_ref, dwcr_ref, dwci_ref):
                ref[...] = jnp.zeros_like(ref)

        tn = (((0,), (0,)), ((), ()))
        nt = (((1,), (1,)), ((), ()))
        u_kb = u_ref[...]
        dy = dz_ref[...] * _gelu_grad(y_ref[...])
        dyb = dy.astype(BF16)
        ub = u_kb.astype(BF16)
        dd_ref[...] += jnp.sum(dy * u_kb, axis=0, keepdims=True)
        ds_r = lax.dot_general(dyb, wcr_ref[0], nt, preferred_element_type=F32)
        ds_i = -lax.dot_general(dyb, wci_ref[0], nt, preferred_element_type=F32)
        for i in range(LB_KB):
            gr_ref[i] = ds_r[:, BLK * i:BLK * i + BLK]
            gi_ref[i] = ds_i[:, BLK * i:BLK * i + BLK]
        row = lax.broadcasted_iota(jnp.int32, (g, BLK), 0)
        for i in range(LB_KB):
            lanes = slice(BLK * i, BLK * i + BLK)
            ar, ai = a1r_ref[0:1, lanes], -a1i_ref[0:1, lanes]
            cin_r, cin_i = cr_ref[0:1, lanes], ci_ref[0:1, lanes]
            tr, ti = _chunk_carries(gr_ref, gi_ref, i, ar, ai, sqr_ref, sqi_ref, revr_ref[:, lanes],
                                    -revi_ref[:, lanes], cin_r, cin_i, -1.0, reverse=True)
            cr_ref[0:1, lanes] = jnp.sum(jnp.where(row == 0, tr, 0.0), axis=0, keepdims=True)
            ci_ref[0:1, lanes] = jnp.sum(jnp.where(row == 0, ti, 0.0), axis=0, keepdims=True)
            nr = jnp.where(row == g - 1, cin_r, pltpu.roll(tr, g - 1, axis=0))
            ni = jnp.where(row == g - 1, cin_i, pltpu.roll(ti, g - 1, axis=0))
            acc_r = jnp.zeros((g, BLK), F32)
            acc_i = jnp.zeros((g, BLK), F32)
            for r in range(GRP - 1, -1, -1):
                s_r, s_i = _lane_slab(sr_ref, i, r, g), _lane_slab(si_ref, i, r, g)
                acc_r = acc_r + (nr * s_r + ni * s_i)
                acc_i = acc_i + (ni * s_r - nr * s_i)
                nr, ni = (_lane_slab(gr_ref, i, r, g) + ar * nr - ai * ni,
                          _lane_slab(gi_ref, i, r, g) + ar * ni + ai * nr)
                gr_ref[i, _strided(r, g), :] = nr
                gi_ref[i, _strided(r, g), :] = ni
            dar_ref[:, lanes] += jnp.sum(acc_r, axis=0, keepdims=True)
            dai_ref[:, lanes] += jnp.sum(acc_i, axis=0, keepdims=True)
        grb = jnp.concatenate([gr_ref[i] for i in range(LB_KB)], axis=1).astype(BF16)
        gib = jnp.concatenate([gi_ref[i] for i in range(LB_KB)], axis=1).astype(BF16)
        srb = jnp.concatenate([sr_ref[i] for i in range(LB_KB)], axis=1).astype(BF16)
        sib = jnp.concatenate([si_ref[i] for i in range(LB_KB)], axis=1).astype(BF16)
        du = (lax.dot_general(grb, wbr_ref[0], nt, preferred_element_type=F32)
              + lax.dot_general(gib, wbi_ref[0], nt, preferred_element_type=F32)
              + d_ref[...] * dy)
        du_ref[...] = du.astype(BF16)
        dwbr_ref[0] += lax.dot_general(ub, grb, tn, preferred_element_type=F32)
        dwbi_ref[0] += lax.dot_general(ub, gib, tn, preferred_element_type=F32)
        dwcr_ref[0] += lax.dot_general(srb, dyb, tn, preferred_element_type=F32)
        dwci_ref[0] -= lax.dot_general(sib, dyb, tn, preferred_element_type=F32)

    chan = pl.BlockSpec((q, BLK), lambda k, c: (nc - 1 - c, k))
    wb_spec = pl.BlockSpec((1, BLK, ST_KB), lambda k, c: (k, 0, 0))
    wc_spec = pl.BlockSpec((1, ST_KB, BLK), lambda k, c: (k, 0, 0))
    tab_in = [tabs[0], tabs[1], tabs[2], tabs[3], tabs[6], tabs[7]]
    tab_specs = [pl.BlockSpec((t.shape[0], ST_KB), lambda k, c: (0, k)) for t in tab_in]
    vec = pl.BlockSpec((1, BLK), lambda k, c: (0, k))
    svec = pl.BlockSpec((1, ST_KB), lambda k, c: (0, k))
    state_spec = pl.BlockSpec((LB_KB, q, BLK), lambda k, c: (k, nc - 1 - c, 0))
    return pl.pallas_call(
        body, name="ssm_bwd", grid=(SSM_KB, nc),
        in_specs=[chan, chan, chan, state_spec, state_spec, wb_spec, wb_spec, wc_spec, wc_spec, vec] + tab_specs,
        out_specs=[chan, vec, svec, svec, wb_spec, wb_spec, wc_spec, wc_spec],
        out_shape=[jax.ShapeDtypeStruct((l_dim, D_MODEL), BF16), jax.ShapeDtypeStruct((1, D_MODEL), F32),
                   jax.ShapeDtypeStruct((1, N_STATE), F32), jax.ShapeDtypeStruct((1, N_STATE), F32),
                   jax.ShapeDtypeStruct((SSM_KB, BLK, ST_KB), F32), jax.ShapeDtypeStruct((SSM_KB, BLK, ST_KB), F32),
                   jax.ShapeDtypeStruct((SSM_KB, ST_KB, BLK), F32), jax.ShapeDtypeStruct((SSM_KB, ST_KB, BLK), F32)],
        scratch_shapes=[pltpu.VMEM((8, ST_KB), F32), pltpu.VMEM((8, ST_KB), F32),
                        pltpu.VMEM((LB_KB, q, BLK), F32), pltpu.VMEM((LB_KB, q, BLK), F32)],
        compiler_params=_params(dimension_semantics=("parallel", "arbitrary")),
    )(dz, y, u, s_re, s_im, wb_re, wb_im, wc_re, wc_im, d_skip, *tab_in)


def _discretize(lam_re, lam_im, log_dt, b_re, b_im):
    dt = jnp.exp(log_dt)[:, None]
    mag = jnp.exp(lam_re * dt)
    ar, ai = mag * jnp.cos(lam_im * dt), mag * jnp.sin(lam_im * dt)
    den = lam_re * lam_re + lam_im * lam_im
    nr, ni = ar - 1.0, ai
    fr, fi = (nr * lam_re + ni * lam_im) / den, (ni * lam_re - nr * lam_im) / den
    bbar_re = fr[..., None] * b_re - fi[..., None] * b_im
    bbar_im = fr[..., None] * b_im + fi[..., None] * b_re
    return ar, ai, bbar_re, bbar_im


def _block_diag_b(bbar):
    eye = jnp.eye(8, dtype=bbar.dtype)
    return jnp.einsum("kgpc,gh->kgchp", bbar.reshape(8, 8, SSM_STATE, SSM_GROUP_CH), eye).reshape(8, BLK, ST_KB)


def _block_diag_b_t(dwb):
    eye = jnp.eye(8, dtype=dwb.dtype)
    return jnp.einsum("kgchp,gh->kgpc", dwb.reshape(8, 8, SSM_GROUP_CH, 8, SSM_STATE), eye).reshape(
        SSM_GROUPS, SSM_STATE, SSM_GROUP_CH)


def _block_diag_c(c):
    eye = jnp.eye(8, dtype=c.dtype)
    return jnp.einsum("kgcp,gh->kgphc", c.reshape(8, 8, SSM_GROUP_CH, SSM_STATE), eye).reshape(8, ST_KB, BLK)


def _block_diag_c_t(dwc):
    eye = jnp.eye(8, dtype=dwc.dtype)
    return jnp.einsum("kgphc,gh->kgcp", dwc.reshape(8, 8, SSM_STATE, 8, SSM_GROUP_CH), eye).reshape(
        SSM_GROUPS, SSM_GROUP_CH, SSM_STATE)


def _powers(br, bi, n):
    pr, pi = br, bi
    cr, ci = br, bi
    while pr.shape[0] < n:
        pr, pi = (jnp.concatenate([pr, pr * cr - pi * ci], axis=0),
                  jnp.concatenate([pi, pr * ci + pi * cr], axis=0))
        cr, ci = cr * cr - ci * ci, 2.0 * cr * ci
    return pr[:n], pi[:n]


def _powers_desc(br, bi, n):
    pr, pi = br, bi
    cr, ci = br, bi
    while pr.shape[0] < n:
        pr, pi = (jnp.concatenate([pr * cr - pi * ci, pr], axis=0),
                  jnp.concatenate([pr * ci + pi * cr, pi], axis=0))
        cr, ci = cr * cr - ci * ci, 2.0 * cr * ci
    return pr, pi


def _power_tables(ar, ai, g):
    a1r, a1i = _powers(ar, ai, GRP)
    seqr, seqi = _powers(a1r[GRP - 1:], a1i[GRP - 1:], g)
    g2 = 1 << (g - 1).bit_length()
    revr, revi = _powers_desc(a1r[GRP - 1:], a1i[GRP - 1:], g2)
    revr, revi = revr[g2 - g:], revi[g2 - g:]
    sq_r, sq_i = [seqr[0:1]], [seqi[0:1]]
    while len(sq_r) < 8:
        r, i = sq_r[-1], sq_i[-1]
        sq_r.append(r * r - i * i)
        sq_i.append(2.0 * r * i)
    sqr, sqi = jnp.concatenate(sq_r, axis=0), jnp.concatenate(sq_i, axis=0)
    return a1r, a1i, sqr, sqi, seqr, seqi, revr, revi


HBM_SPEC = pl.BlockSpec(memory_space=pltpu.HBM)
SEM_SPEC = pl.BlockSpec(memory_space=pltpu.SEMAPHORE)
DATAFLOW = pltpu.SideEffectType.DATAFLOW_SIDE_EFFECTING


def _plain_rows(p, m):
    return p * m


def _ffn_in_rows(p, m):
    return ((p & 3) >> 1) * (4 * m) + (p >> 2) * (2 * m) + (p & 1) * m


def _peer_copies(src_refs, land_refs, send_sems, recv_sems, chunked, row_fns):
    x, y, c = lax.axis_index("x"), lax.axis_index("y"), lax.axis_index("c")
    me = 4 * x + 2 * y + c
    copies = []
    for a, (src, land) in enumerate(zip(src_refs, land_refs)):
        m = land.shape[0] // N_DEV
        for k in range(N_DEV - 1):
            rel = k + 1
            bx, by, bc = (rel >> 2) & 1, (rel >> 1) & 1, rel & 1
            peer = (x + bx - 2 * x * bx, y + by - 2 * y * by, c + bc - 2 * c * bc)
            p_idx = 4 * peer[0] + 2 * peer[1] + peer[2]
            copies.append(pltpu.make_async_remote_copy(
                src_ref=src.at[pl.ds(row_fns[a](p_idx, m), m), :] if chunked else src,
                dst_ref=land.at[pl.ds(me * m if chunked else row_fns[a](me, m), m), :],
                send_sem=send_sems[a * (N_DEV - 1) + k], recv_sem=recv_sems[a * (N_DEV - 1) + k],
                device_id=peer, device_id_type=MESH))
    return copies


def _send_start(name, srcs, lands, chunked, row_fns=None):
    n = len(srcs)
    ns = n * (N_DEV - 1)
    row_fns = row_fns or [_plain_rows] * n

    def body(*refs):
        src_refs, land_refs = refs[:n], refs[n:2 * n]
        send_sems, recv_sems = refs[2 * n:2 * n + ns], refs[2 * n + ns:2 * n + 2 * ns]
        token = refs[-1]
        for cp in _peer_copies(src_refs, land_refs, send_sems, recv_sems, chunked, row_fns):
            cp.start()
        token[...] = jnp.zeros_like(token)

    ins = [pltpu.with_memory_space_constraint(a, pltpu.HBM) for a in list(srcs) + list(lands)]
    out = pl.pallas_call(
        body, name=name,
        in_specs=[HBM_SPEC] * (2 * n),
        out_specs=[SEM_SPEC] * (2 * ns) + [HBM_SPEC] * (2 * n) + [pl.BlockSpec(memory_space=pltpu.VMEM)],
        out_shape=[pltpu.SemaphoreType.DMA(())] * (2 * ns)
        + [pltpu.HBM(a.shape, a.dtype) for a in list(srcs) + list(lands)]
        + [jax.ShapeDtypeStruct((8, BLK), F32)],
        input_output_aliases={i: i + 2 * ns for i in range(2 * n)},
        compiler_params=pltpu.CompilerParams(has_side_effects=DATAFLOW),
    )(*ins)
    return out[:ns], out[ns:2 * ns], out[2 * ns:2 * ns + n], out[2 * ns + n:2 * ns + 2 * n], out[-1]


def _send_wait(name, send_sems, recv_sems, srcs, lands, after, chunked, row_fns=None):
    n = len(srcs)
    ns = n * (N_DEV - 1)
    row_fns = row_fns or [_plain_rows] * n

    def body(*refs):
        src_refs, land_refs = refs[:n], refs[n:2 * n]
        s_sems, r_sems = refs[2 * n:2 * n + ns], refs[2 * n + ns:2 * n + 2 * ns]
        copies = _peer_copies(src_refs, land_refs, s_sems, r_sems, chunked, row_fns)
        for cp in copies:
            cp.wait_send()
        for cp in copies:
            cp.wait_recv()

    out = pl.pallas_call(
        body, name=name,
        in_specs=[HBM_SPEC] * (2 * n) + [SEM_SPEC] * (2 * ns) + [pl.BlockSpec(memory_space=pl.ANY)],
        out_specs=[HBM_SPEC] * (2 * n),
        out_shape=[pltpu.HBM(a.shape, a.dtype) for a in list(srcs) + list(lands)],
        input_output_aliases={i: i for i in range(2 * n)},
        compiler_params=pltpu.CompilerParams(has_side_effects=DATAFLOW),
    )(*srcs, *lands, *send_sems, *recv_sems, after)
    return out[n:]


def _sum_slots(name, recv, own):
    m, ncol = own.shape
    tr = m // 2 if (m // 2) % 16 == 0 else m
    g = m // tr

    def body(*refs):
        slots, own_ref, o_ref = refs[:N_DEV], refs[N_DEV], refs[N_DEV + 1]
        me = _my_index()
        tot = None
        for s in range(N_DEV):
            v = jnp.where(me == s, own_ref[...], slots[s][...].astype(F32))
            tot = v if tot is None else tot + v
        o_ref[...] = tot

    def slot_spec(s):
        return pl.BlockSpec((tr, ncol), lambda i: (s * g + i, 0))

    return pl.pallas_call(
        body, name=name, grid=(g,),
        in_specs=[slot_spec(s) for s in range(N_DEV)] + [pl.BlockSpec((tr, ncol), lambda i: (i, 0))],
        out_specs=pl.BlockSpec((tr, ncol), lambda i: (i, 0)),
        out_shape=jax.ShapeDtypeStruct((m, ncol), F32),
        compiler_params=_params(),
    )(*([recv] * N_DEV), own)


def _sum_gathered(name, gathered, rows):
    tr = _pick(rows, 512, 8)
    g = rows // tr

    def body(*refs):
        o_ref = refs[N_DEV]
        tot = refs[0][...]
        for s in range(1, N_DEV):
            tot = tot + refs[s][...]
        o_ref[...] = tot

    return pl.pallas_call(
        body, name=name, grid=(g,),
        in_specs=[pl.BlockSpec((tr, BLK), (lambda i, s=s: (s * g + i, 0))) for s in range(N_DEV)],
        out_specs=pl.BlockSpec((tr, BLK), lambda i: (i, 0)),
        out_shape=jax.ShapeDtypeStruct((rows, BLK), F32),
        compiler_params=_params(),
    )(*([gathered] * N_DEV))


def _adamw(name, w, g, m, v):
    r, c = w.shape
    tr = _pick(r, 256, 8) if r % 8 == 0 else r
    c1 = 1.0 - ADAM_B1 ** ADAM_STEP
    c2 = 1.0 - ADAM_B2 ** ADAM_STEP

    def body(w_ref, g_ref, m_ref, v_ref, d_ref, nm_ref, nv_ref):
        gv = g_ref[...]
        nm = ADAM_B1 * m_ref[...] + (1.0 - ADAM_B1) * gv
        nv = ADAM_B2 * v_ref[...] + (1.0 - ADAM_B2) * (gv * gv)
        m_hat = nm / c1
        v_hat = nv / c2
        d_ref[...] = -ADAM_LR * (m_hat / (jnp.sqrt(v_hat) + ADAM_EPS) + ADAM_WD * w_ref[...])
        nm_ref[...] = nm
        nv_ref[...] = nv

    spec = pl.BlockSpec((tr, c), lambda i: (i, 0))
    shape = jax.ShapeDtypeStruct((r, c), F32)
    return pl.pallas_call(
        body, name=name, grid=(r // tr,),
        in_specs=[spec] * 4, out_specs=[spec] * 3, out_shape=[shape] * 3,
        compiler_params=_params(),
    )(w, g, m, v)


PACK_ROWS = 128


def _pack(parts):
    flat = []
    for p in parts:
        v = p.reshape(-1)
        flat.append(jnp.pad(v, (0, (-v.shape[0]) % BLK)))
    v = jnp.concatenate(flat)
    v = jnp.pad(v, (0, (-v.shape[0]) % (PACK_ROWS * BLK)))
    return v.reshape(-1, BLK)


def _unpack(buf, shapes):
    flat = buf.reshape(-1)
    out, off = [], 0
    for shp in shapes:
        size = math.prod(shp)
        out.append(flat[off:off + size].reshape(shp))
        off += size + (-size) % BLK
    return out


def kernel(x, meta_tokens, norm_mix, w_in, q_norm, k_norm, attn_sinks, lam_re, lam_im, log_dt, ssm_b_re, ssm_b_im, ssm_c_re, ssm_c_im, ssm_d, w_glu, attn_branch_norm, ssm_branch_norm, w_out, norm_ffn, w_ffn_in, w_ffn_out, loss_target, m_meta_tokens, m_norm_mix, m_w_in, m_q_norm, m_k_norm, m_attn_sinks, m_lam_re, m_lam_im, m_log_dt, m_ssm_b_re, m_ssm_b_im, m_ssm_c_re, m_ssm_c_im, m_ssm_d, m_w_glu, m_attn_branch_norm, m_ssm_branch_norm, m_w_out, m_norm_ffn, m_w_ffn_in, m_w_ffn_out, v_meta_tokens, v_norm_mix, v_w_in, v_q_norm, v_k_norm, v_attn_sinks, v_lam_re, v_lam_im, v_log_dt, v_ssm_b_re, v_ssm_b_im, v_ssm_c_re, v_ssm_c_im, v_ssm_d, v_w_glu, v_attn_branch_norm, v_ssm_branch_norm, v_w_out, v_norm_ffn, v_w_ffn_in, v_w_ffn_out):
    args = dict(locals())
    weights = {n: args[n] for n in WEIGHTS}
    mom_m = {n: args["m_" + n] for n in WEIGHTS}
    mom_v = {n: args["v_" + n] for n in WEIGHTS}

    x2d = x[0]
    target2d = loss_target[0]
    s_len = x2d.shape[0]
    l_dim = s_len + BLK
    tm_row = _pick(l_dim, 320)
    tm_mm = _pick(l_dim, 1040)
    tl_tn = _pick(l_dim, 832)
    tm_ffn = _pick(l_dim, 640)
    tm_big = _pick(l_dim, 2080)

    shard_in = w_in[0].T.astype(BF16)
    shard_glu = w_glu[0].T.astype(BF16)
    shard_out = w_out[0].astype(BF16)
    shard_ffn_in = w_ffn_in[0].T.astype(BF16)
    shard_ffn_out = w_ffn_out[0].astype(BF16)
    shard_meta = meta_tokens.T
    me = _my_index()

    def landing(shard, row_fn=_plain_rows):
        m_rows, cols = shard.shape
        return lax.dynamic_update_slice(lax.empty((N_DEV * m_rows, cols), shard.dtype), shard,
                                        (row_fn(me, m_rows), 0))

    first = [shard_in, shard_meta]
    ga = _send_start("gather_start_a", first, [landing(s) for s in first], chunked=False)
    later = [shard_glu + ga[4][0:1, 0:1].astype(BF16), shard_out, shard_ffn_in, shard_ffn_out]
    later_fns = [_plain_rows, _plain_rows, _ffn_in_rows, _plain_rows]
    gb = _send_start("gather_start_b", later, [landing(s, f) for s, f in zip(later, later_fns)], chunked=False,
                     row_fns=later_fns)

    nm_t = norm_mix + (ga[4][0:1, 0:1] + gb[4][0:1, 0:1])
    qn_t, kn_t = jnp.tile(q_norm, (1, N_Q_HEADS)), jnp.tile(k_norm, (1, N_KV_HEADS))
    e_mat = jnp.kron(jnp.eye(4, dtype=F32), jnp.ones((HEAD_DIM, HEAD_DIM), F32)).astype(BF16)

    def disc(lr, li, ldt, br, bi):
        return _discretize(lr[0], li[0], ldt[0], br[0], bi[0])

    (abar_re, abar_im, bbar_re, bbar_im), disc_vjp = jax.vjp(disc, lam_re, lam_im, log_dt, ssm_b_re, ssm_b_im)
    wb_re, wb_im = _block_diag_b(bbar_re).astype(BF16), _block_diag_b(bbar_im).astype(BF16)
    wc_re, wc_im = _block_diag_c(ssm_c_re[0]).astype(BF16), _block_diag_c(ssm_c_im[0]).astype(BF16)
    q_ssm = _pick(l_dim, 640, 64)
    tabs = _power_tables(abar_re.reshape(1, N_STATE), abar_im.reshape(1, N_STATE), q_ssm // GRP)

    h0, xn = _embed_norm(x2d, nm_t)
    wt_in, meta_t = _send_wait("gather_wait_a", ga[0], ga[1], ga[2], ga[3], xn, chunked=False)
    meta_pad = jnp.pad(meta_t.T, ((PAD, 0), (0, 0)))
    h0, xn = _embed_meta(meta_pad, nm_t, h0, xn)
    qkv = _matmul("proj_qkv", xn, wt_in, nt=True, tm=tm_big, tn=512, tk=D_MODEL, n=Q_W + 2 * KV_W, w_off=0)
    u = _matmul("proj_u", xn, wt_in, nt=True, tm=tm_big, tn=512, tk=D_MODEL, n=D_MODEL, w_off=3)
    gates = _matmul("proj_gates", xn, wt_in, nt=True, tm=tm_big, tn=512, tk=D_MODEL, n=2 * D_MODEL, w_off=5)
    qn, kf, vf = _qk_prep(qkv, qn_t, kn_t, e_mat, tm_row)
    attn = _attn_fwd(qn, kf, vf, attn_sinks)
    y, z, s_re, s_im = _ssm_fwd_kb(u, wb_re, wb_im, wc_re, wc_im, ssm_d, tabs, q_ssm)
    wt_glu, w_out_f, wt_ffn_in, w_ffn_out_f = _send_wait("gather_wait_b", gb[0], gb[1], gb[2], gb[3], z,
                                                         chunked=False, row_fns=later_fns)
    zab = _matmul("glu_proj", z, wt_glu, nt=True, tm=tm_big, tn=1024, tk=D_MODEL)
    merged = _merge_fwd(attn, zab, gates, attn_branch_norm, ssm_branch_norm, tm_row)
    h1, hn = _matmul("out_proj", merged, w_out_f, nt=False, tm=tm_mm, tn=1024, tk=D_MODEL, res=h0,
                     norm_g=norm_ffn)
    gu, act = _ffn_in_swiglu(hn, wt_ffn_in, tm_ffn)
    h2 = _matmul("ffn_out", act, w_ffn_out_f, nt=False, tm=tm_mm, tn=1024, tk=1408, res=h1)
    dh2, dh2_b, loss_part = _loss_grad(h2, target2d)

    dgu = _d_act_swiglu(dh2_b, w_ffn_out_f, gu, tm_ffn)

    def exchange_start(name, grads_b, row_fns=None):
        return _send_start(name, grads_b, [jnp.zeros(g.shape, BF16) for g in grads_b], chunked=True,
                           row_fns=row_fns)

    g_ffn_out, g_ffn_out_b = _matmul_tn("g_ffn_out", act, dh2_b, tm=1408, tn=1024, tl=tl_tn)
    g_ffn_in_t, g_ffn_in_b = _matmul_tn("g_ffn_in", dgu, hn, tm=1408, tn=1024, tl=tl_tn)
    ffn_fns = [_ffn_in_rows, _plain_rows]
    ex1 = exchange_start("exchange_start_ffn", [g_ffn_in_b, g_ffn_out_b], ffn_fns)
    dhn = _matmul("d_hn", dgu, wt_ffn_in, nt=False, tm=tm_mm, tn=1024, tk=1408)
    dh1, dh1_b, g_norm_ffn = _norm_bwd_res("ffn_norm_bwd", h1, norm_ffn + ex1[4][0:1, 0:1], dhn, dh2, tm_row)
    dmerged = _matmul("d_merged", dh1_b, w_out_f, nt=True, tm=tm_big, tn=1024, tk=D_MODEL)
    dattn, dzab, dgates, g_abn, g_sbn = _merge_bwd(attn, zab, gates, attn_branch_norm, ssm_branch_norm,
                                                     dmerged, tm_row)
    g_out, g_out_b = _matmul_tn("g_out", merged, dh1_b, tm=1024, tn=1024, tl=tl_tn)
    dz = _matmul("d_z", dzab, wt_glu, nt=False, tm=tm_big, tn=1024, tk=1024)
    g_glu_t, g_glu_b = _matmul_tn("g_glu", dzab, z, tm=1024, tn=1024, tl=tl_tn)
    ex2 = exchange_start("exchange_start_mix", [g_glu_b, g_out_b])
    du, g_ssm_d, g_ar, g_ai, g_wbr, g_wbi, g_wcr, g_wci = _ssm_bwd_kb(
        dz, y, u, s_re, s_im, wb_re, wb_im, wc_re, wc_im, ssm_d + ex2[4][0:1, 0:1], tabs, q_ssm)
    dq, dkc, dkp, dkm, dvc, dvp, dvm, g_sinks = _attn_bwd(qn, kf, vf, attn_sinks, attn, dattn)
    dqkv, g_qn_t, g_kn_t = _qk_bwd(qkv, qn_t, kn_t, e_mat, dq, dkc, dkp, dkm, dvc, dvp, dvm)
    g_lam_re, g_lam_im, g_log_dt, g_b_re, g_b_im = disc_vjp(
        (g_ar.reshape(SSM_GROUPS, SSM_STATE), g_ai.reshape(SSM_GROUPS, SSM_STATE),
         _block_diag_b_t(g_wbr), _block_diag_b_t(g_wbi)))
    small_grads = {
        "q_norm": g_qn_t.reshape(N_Q_HEADS, HEAD_DIM).sum(0)[None],
        "k_norm": g_kn_t.reshape(N_KV_HEADS, HEAD_DIM).sum(0)[None], "attn_sinks": g_sinks,
        "lam_re": g_lam_re, "lam_im": g_lam_im, "log_dt": g_log_dt, "ssm_b_re": g_b_re, "ssm_b_im": g_b_im,
        "ssm_c_re": _block_diag_c_t(g_wcr)[None], "ssm_c_im": _block_diag_c_t(g_wci)[None],
        "ssm_d": g_ssm_d, "attn_branch_norm": g_abn, "ssm_branch_norm": g_sbn, "norm_ffn": g_norm_ffn,
    }
    early = [n for n in SMALL if n != "norm_mix"]
    packed_e = _pack([small_grads[n] for n in early])
    gs_e = _send_start("small_start_a", [packed_e], [landing(packed_e)], chunked=False)

    dproj = jnp.concatenate([dqkv, du, dgates], axis=1)
    g_in_t, g_in_b = _matmul_tn("g_in", dproj, xn, tm=1152, tn=1024, tl=tl_tn, after=gs_e[4])
    ex3 = exchange_start("exchange_start_in", [g_in_b])
    dxn = _matmul("d_xn", dproj, wt_in, nt=False, tm=tm_mm, tn=1024, tk=1152)
    grad_x2d, dmeta_blk, g_norm_mix = _final_bwd(h0, nm_t + ex3[4][0:1, 0:1], dxn, dh1)
    packed_l = _pack([g_norm_mix, dmeta_blk[PAD:]])
    gs_l = _send_start("small_start_b", [packed_l], [landing(packed_l)], chunked=False)
    grads, deltas, new_m, new_v = {}, {}, {}, {}

    recv_ffn_in, recv_ffn_out = _send_wait("exchange_wait_ffn", ex1[0], ex1[1], ex1[2], ex1[3], gs_l[4],
                                           chunked=True, row_fns=ffn_fns)
    recv_glu, recv_out = _send_wait("exchange_wait_mix", ex2[0], ex2[1], ex2[2], ex2[3], recv_ffn_in,
                                    chunked=True)
    (recv_in,) = _send_wait("exchange_wait_in", ex3[0], ex3[1], ex3[2], ex3[3], recv_glu, chunked=True)
    big = [("w_in", g_in_t, True, recv_in, _plain_rows), ("w_glu", g_glu_t, True, recv_glu, _plain_rows),
           ("w_out", g_out, False, recv_out, _plain_rows), ("w_ffn_in", g_ffn_in_t, True, recv_ffn_in, _ffn_in_rows),
           ("w_ffn_out", g_ffn_out, False, recv_ffn_out, _plain_rows)]
    for name, g_full, transposed, recv, row_fn in big:
        m_rows = g_full.shape[0] // N_DEV
        own = lax.dynamic_slice(g_full, (row_fn(me, m_rows), 0), (m_rows, g_full.shape[1]))
        g_shard = _sum_slots("sum_" + name, recv, own)
        grads[name] = (g_shard.T if transposed else g_shard)[None]

    def adamw_2d(name):
        shp = weights[name].shape
        as2d = lambda a: a.reshape(shp[-2], shp[-1])
        d, nm, nv = _adamw("adamw_" + name, as2d(weights[name]), as2d(grads[name]), as2d(mom_m[name]),
                           as2d(mom_v[name]))
        deltas[name], new_m[name], new_v[name] = d.reshape(shp), nm.reshape(shp), nv.reshape(shp)
        return d

    for name in ["w_in", "w_glu", "w_out", "w_ffn_in", "w_ffn_out"]:
        last = adamw_2d(name)

    def small_update(tag, names, gs, packed, after, filler=()):
        (gathered,) = _send_wait("small_wait_" + tag, gs[0], gs[1], gs[2], gs[3], after, chunked=False)
        g_sum = _sum_gathered("sum_small_" + tag, gathered, packed.shape[0])
        d, nm, nv = _adamw("adamw_small_" + tag, _pack([weights[n] for n in names] + list(filler)), g_sum,
                           _pack([mom_m[n] for n in names] + list(filler)),
                           _pack([mom_v[n] for n in names] + list(filler)))
        shapes = [weights[n].shape for n in names]
        deltas.update(zip(names, _unpack(d, shapes)))
        new_m.update(zip(names, _unpack(nm, shapes)))
        new_v.update(zip(names, _unpack(nv, shapes)))
        return g_sum

    g_sum_e = small_update("a", early, gs_e, packed_e, last)
    grads.update(zip(early, _unpack(g_sum_e, [weights[n].shape for n in early])))
    g_sum_l = small_update("b", ["norm_mix"], gs_l, packed_l, g_sum_e, filler=[jnp.zeros((N_META, D_MODEL), F32)])
    grads["norm_mix"], g_meta = _unpack(g_sum_l, [weights["norm_mix"].shape, (N_META, D_MODEL)])
    grads["meta_tokens"] = lax.dynamic_slice(g_meta, (0, me * BLK), (N_META, BLK))
    adamw_2d("meta_tokens")

    loss = lax.psum(loss_part[0, 0], ("x", "y", "c"))
    return (loss, grad_x2d[None], *[grads[n] for n in WEIGHTS], *[deltas[n] for n in WEIGHTS],
            *[new_m[n] for n in WEIGHTS], *[new_v[n] for n in WEIGHTS])
```

```python
import math

import jax
import jax.numpy as jnp
from jax import lax
from jax.experimental import pallas as pl
from jax.experimental.pallas import tpu as pltpu

F32 = jnp.float32
BF16 = jnp.bfloat16

D_MODEL = 1024
N_META = 16
HEAD_DIM = 64
N_Q_HEADS = 16
N_KV_HEADS = 4
Q_W = N_Q_HEADS * HEAD_DIM
KV_W = N_KV_HEADS * HEAD_DIM
SSM_GROUPS = 64
SSM_GROUP_CH = 16
SSM_STATE = 64
N_STATE = SSM_GROUPS * SSM_STATE
D_FF = 2816
IN_COLS = Q_W + 2 * KV_W + 3 * D_MODEL
EPS = 1e-6
BLK = 128
PAD = BLK - N_META
N_DEV = 8
NEG = -1e30
SSM_KB = 8
ST_KB = N_STATE // SSM_KB
LB_KB = ST_KB // BLK
N_LB = N_STATE // BLK

ADAM_LR = 0.001
ADAM_B1 = 0.9
ADAM_B2 = 0.999
ADAM_EPS = 1e-08
ADAM_WD = 0.01
ADAM_STEP = 10

VMEM_LIMIT = 48 * 1024 * 1024
MESH = pl.DeviceIdType.MESH

SMALL = ["norm_mix", "q_norm", "k_norm", "attn_sinks", "lam_re", "lam_im", "log_dt", "ssm_b_re", "ssm_b_im",
         "ssm_c_re", "ssm_c_im", "ssm_d", "attn_branch_norm", "ssm_branch_norm", "norm_ffn"]
WEIGHTS = ["meta_tokens", "norm_mix", "w_in", "q_norm", "k_norm", "attn_sinks", "lam_re", "lam_im", "log_dt",
           "ssm_b_re", "ssm_b_im", "ssm_c_re", "ssm_c_im", "ssm_d", "w_glu", "attn_branch_norm",
           "ssm_branch_norm", "w_out", "norm_ffn", "w_ffn_in", "w_ffn_out"]


def _params(**kw):
    return pltpu.CompilerParams(vmem_limit_bytes=VMEM_LIMIT, **kw)


def _pick(n, cap, mult=16):
    best = None
    for d in range(mult, min(n, cap) + 1, mult):
        if n % d == 0:
            best = d
    assert best is not None, (n, cap, mult)
    return best


def _my_index():
    return 4 * lax.axis_index("x") + 2 * lax.axis_index("y") + lax.axis_index("c")


def _rms(x, g):
    r = lax.rsqrt(jnp.mean(x * x, axis=-1, keepdims=True) + EPS)
    return x * r * g


def _rms_bwd(x, g, dy):
    r = lax.rsqrt(jnp.mean(x * x, axis=-1, keepdims=True) + EPS)
    t = dy * g
    dx = r * t - x * (r * r * r) * jnp.mean(t * x, axis=-1, keepdims=True)
    dg = jnp.sum(dy * (x * r), axis=0, keepdims=True)
    return dx, dg


def _sigmoid(x):
    return 1.0 / (1.0 + jnp.exp(-x))


def _gelu(x):
    k = math.sqrt(2.0 / math.pi)
    return 0.5 * x * (1.0 + jnp.tanh(k * (x + 0.044715 * (x * x * x))))


def _gelu_grad(x):
    k = math.sqrt(2.0 / math.pi)
    t = jnp.tanh(k * (x + 0.044715 * (x * x * x)))
    return 0.5 * (1.0 + t) + 0.5 * x * (1.0 - t * t) * (k * (1.0 + 3.0 * 0.044715 * (x * x)))


def _head_mean(x, e_ref):
    hi = x.astype(BF16)
    r1 = x - hi.astype(F32)
    mid = r1.astype(BF16)
    lo = (r1 - mid.astype(F32)).astype(BF16)
    e = e_ref[...]
    out = []
    for b in range(x.shape[1] // 256):
        sl = slice(256 * b, 256 * b + 256)
        s = (jnp.dot(hi[:, sl], e, preferred_element_type=F32)
             + jnp.dot(mid[:, sl], e, preferred_element_type=F32)
             + jnp.dot(lo[:, sl], e, preferred_element_type=F32))
        out.append(s)
    s = out[0] if len(out) == 1 else jnp.concatenate(out, axis=1)
    return s * (1.0 / HEAD_DIM)


def _head_rms(x, g, e_ref):
    r = lax.rsqrt(_head_mean(x * x, e_ref) + EPS)
    return x * r * g


def _head_rms_bwd(x, g, dy, e_ref):
    r = lax.rsqrt(_head_mean(x * x, e_ref) + EPS)
    t = dy * g
    dx = r * t - x * (r * r * r) * _head_mean(t * x, e_ref)
    dg = jnp.sum(dy * (x * r), axis=0, keepdims=True)
    return dx, dg


def _lane_half(shape):
    lane = lax.broadcasted_iota(jnp.int32, shape, len(shape) - 1)
    return (lane >> 6) & 1


def _matmul(name, a, w, *, nt, tm, tn, tk, n=None, w_off=0, res=None, norm_g=None, out_dtype=F32):
    m_dim, k_dim = a.shape
    n_dim = n if n is not None else (w.shape[0] if nt else w.shape[1])
    gm, gn, gk = m_dim // tm, n_dim // tn, k_dim // tk
    assert gm * tm == m_dim and gn * tn == n_dim and gk * tk == k_dim, (name, a.shape, w.shape, tm, tn, tk)
    assert norm_g is None or tn == n_dim
    dn = (((1,), (1,)), ((), ())) if nt else (((1,), (0,)), ((), ()))

    def body(*refs):
        refs = list(refs)
        a_ref, w_ref = refs[0], refs[1]
        pos = 2
        r_ref = g_ref = on_ref = None
        if res is not None:
            r_ref, pos = refs[pos], pos + 1
        if norm_g is not None:
            g_ref, pos = refs[pos], pos + 1
        o_ref, pos = refs[pos], pos + 1
        if norm_g is not None:
            on_ref, pos = refs[pos], pos + 1
        acc = refs[pos]
        k = pl.program_id(2)

        @pl.when(k == 0)
        def _():
            acc[...] = jnp.zeros_like(acc)

        acc[...] += lax.dot_general(a_ref[...], w_ref[...], dn, preferred_element_type=F32)

        @pl.when(k == gk - 1)
        def _():
            r = acc[...]
            if r_ref is not None:
                r = r_ref[...] + r
            o_ref[...] = r.astype(out_dtype)
            if on_ref is not None:
                on_ref[...] = _rms(r, g_ref[...]).astype(BF16)

    if nt:
        w_spec = pl.BlockSpec((tn, tk), lambda i, j, k: (j + w_off, k))
    else:
        w_spec = pl.BlockSpec((tk, tn), lambda i, j, k: (k, j))
    in_specs = [pl.BlockSpec((tm, tk), lambda i, j, k: (i, k)), w_spec]
    args = [a, w]
    out_spec = pl.BlockSpec((tm, tn), lambda i, j, k: (i, j))
    out_specs, out_shape = [out_spec], [jax.ShapeDtypeStruct((m_dim, n_dim), out_dtype)]
    if res is not None:
        in_specs.append(out_spec)
        args.append(res)
    if norm_g is not None:
        in_specs.append(pl.BlockSpec((1, tn), lambda i, j, k: (0, 0)))
        args.append(norm_g)
        out_specs.append(out_spec)
        out_shape.append(jax.ShapeDtypeStruct((m_dim, n_dim), BF16))
    out = pl.pallas_call(
        body, name=name, grid=(gm, gn, gk),
        in_specs=in_specs, out_specs=out_specs, out_shape=out_shape,
        scratch_shapes=[pltpu.VMEM((tm, tn), F32)],
        compiler_params=_params(dimension_semantics=("parallel", "parallel", "arbitrary")),
    )(*args)
    return out if norm_g is not None else out[0]


def _matmul_tn(name, a, b, *, tm, tn, tl, after=None):
    l_dim, m_dim = a.shape
    n_dim = b.shape[1]
    gm, gn, gl = m_dim // tm, n_dim // tn, l_dim // tl
    assert gm * tm == m_dim and gn * tn == n_dim and gl * tl == l_dim, (name, a.shape, b.shape, tm, tn, tl)

    def body(*refs):
        a_ref, b_ref = refs[0], refs[1]
        o_ref, ob_ref = refs[-2], refs[-1]

        @pl.when(pl.program_id(2) == 0)
        def _():
            o_ref[...] = jnp.zeros_like(o_ref)

        o_ref[...] += lax.dot_general(a_ref[...], b_ref[...], (((0,), (0,)), ((), ())),
                                      preferred_element_type=F32)

        @pl.when(pl.program_id(2) == gl - 1)
        def _():
            ob_ref[...] = o_ref[...].astype(BF16)

    out_spec = pl.BlockSpec((tm, tn), lambda i, j, l: (i, j))
    in_specs = [pl.BlockSpec((tl, tm), lambda i, j, l: (l, i)), pl.BlockSpec((tl, tn), lambda i, j, l: (l, j))]
    args = [a, b]
    if after is not None:
        in_specs.append(pl.BlockSpec(memory_space=pl.ANY))
        args.append(after)
    return pl.pallas_call(
        body, name=name, grid=(gm, gn, gl),
        in_specs=in_specs,
        out_specs=[out_spec, out_spec],
        out_shape=[jax.ShapeDtypeStruct((m_dim, n_dim), F32), jax.ShapeDtypeStruct((m_dim, n_dim), BF16)],
        compiler_params=_params(dimension_semantics=("parallel", "parallel", "arbitrary")),
    )(*args)


def _row_spec(tm, cols, f=None):
    if f is None:
        return pl.BlockSpec((tm, cols), lambda i: (i, 0))
    return pl.BlockSpec((tm, cols), lambda i: (f(i), 0))


def _full_spec(shape):
    nd = len(shape)
    return pl.BlockSpec(shape, lambda i: (0,) * nd)


def _embed_norm(x2d, g):
    s_len = x2d.shape[0]
    nb = s_len // BLK + 1

    def body(x_ref, g_ref, h_ref, xn_ref):
        h_ref[...] = x_ref[...]
        xn_ref[...] = _rms(x_ref[...], g_ref[...]).astype(BF16)

    return pl.pallas_call(
        body, name="embed_norm", grid=(nb - 1,),
        in_specs=[_row_spec(BLK, D_MODEL), _full_spec((1, D_MODEL))],
        out_specs=[_row_spec(BLK, D_MODEL, lambda i: i + 1), _row_spec(BLK, D_MODEL, lambda i: i + 1)],
        out_shape=[jax.ShapeDtypeStruct((nb * BLK, D_MODEL), F32),
                   jax.ShapeDtypeStruct((nb * BLK, D_MODEL), BF16)],
        compiler_params=_params(),
    )(x2d, g)


def _embed_meta(meta_pad, g, h0, xn):
    def body(mp_ref, g_ref, h_in, xn_in, h_ref, xn_ref):
        h_ref[...] = mp_ref[...]
        xn_ref[...] = _rms(mp_ref[...], g_ref[...]).astype(BF16)

    any_spec = pl.BlockSpec(memory_space=pl.ANY)
    return pl.pallas_call(
        body, name="embed_meta", grid=(1,),
        in_specs=[_full_spec((BLK, D_MODEL)), _full_spec((1, D_MODEL)), any_spec, any_spec],
        out_specs=[_row_spec(BLK, D_MODEL), _row_spec(BLK, D_MODEL)],
        out_shape=[jax.ShapeDtypeStruct(h0.shape, F32), jax.ShapeDtypeStruct(xn.shape, BF16)],
        input_output_aliases={2: 0, 3: 1},
        compiler_params=_params(),
    )(meta_pad, g, h0, xn)


def _qk_prep(qkv, q_norm_t, k_norm_t, e_mat, tm):
    l_dim = qkv.shape[0]

    def body(x_ref, qg_ref, kg_ref, e_ref, q_ref, kf_ref, vf_ref):
        x = x_ref[...]
        q = _head_rms(x[:, :Q_W], qg_ref[...], e_ref) * (HEAD_DIM ** -0.5)
        q_ref[...] = q.astype(BF16)
        k = _head_rms(x[:, Q_W:Q_W + KV_W], kg_ref[...], e_ref)
        v = x[:, Q_W + KV_W:Q_W + 2 * KV_W]
        half = _lane_half((tm, BLK))
        for src, dst in ((k, kf_ref), (v, vf_ref)):
            for kv in range(N_KV_HEADS):
                blk = src[:, BLK * (kv // 2):BLK * (kv // 2) + BLK]
                swapped = pltpu.roll(blk, HEAD_DIM, axis=1)
                for e in range(2):
                    val = blk if kv % 2 == e else swapped
                    idx = 2 * kv + e
                    dst[:, BLK * idx:BLK * idx + BLK] = jnp.where(half == e, val, 0.0).astype(BF16)

    return pl.pallas_call(
        body, name="qk_prep", grid=(l_dim // tm,),
        in_specs=[_row_spec(tm, Q_W + 2 * KV_W), _full_spec((1, Q_W)), _full_spec((1, KV_W)),
                  _full_spec((256, 256))],
        out_specs=[_row_spec(tm, Q_W), _row_spec(tm, 8 * BLK), _row_spec(tm, 8 * BLK)],
        out_shape=[jax.ShapeDtypeStruct((l_dim, Q_W), BF16),
                   jax.ShapeDtypeStruct((l_dim, 8 * BLK), BF16),
                   jax.ShapeDtypeStruct((l_dim, 8 * BLK), BF16)],
        compiler_params=_params(),
    )(qkv, q_norm_t, k_norm_t, e_mat)


def _merge_fwd(attn, zab, gates, abn, sbn, tm):
    l_dim = attn.shape[0]

    def body(a_ref, z_ref, g_ref, an_ref, sn_ref, o_ref):
        z = z_ref[...].astype(F32)
        g = g_ref[...].astype(F32)
        ssm = z[:, :D_MODEL] * _sigmoid(z[:, D_MODEL:])
        merged = (_sigmoid(g[:, :D_MODEL]) * _rms(a_ref[...], an_ref[...])
                  + _sigmoid(g[:, D_MODEL:]) * _rms(ssm, sn_ref[...]))
        o_ref[...] = merged.astype(BF16)

    return pl.pallas_call(
        body, name="merge_fwd", grid=(l_dim // tm,),
        in_specs=[_row_spec(tm, D_MODEL), _row_spec(tm, 2 * D_MODEL), _row_spec(tm, 2 * D_MODEL),
                  _full_spec((1, D_MODEL)), _full_spec((1, D_MODEL))],
        out_specs=_row_spec(tm, D_MODEL),
        out_shape=jax.ShapeDtypeStruct((l_dim, D_MODEL), BF16),
        compiler_params=_params(),
    )(attn, zab, gates, abn, sbn)


def _merge_bwd(attn, zab, gates, abn, sbn, dmerged, tm):
    l_dim = attn.shape[0]

    def body(a_ref, z_ref, g_ref, an_ref, sn_ref, dm_ref, da_ref, dz_ref, dg_ref, dan_ref, dsn_ref):
        @pl.when(pl.program_id(0) == 0)
        def _():
            dan_ref[...] = jnp.zeros_like(dan_ref)
            dsn_ref[...] = jnp.zeros_like(dsn_ref)

        z = z_ref[...].astype(F32)
        g = g_ref[...].astype(F32)
        dm = dm_ref[...]
        attn_v = a_ref[...]
        za, zb = z[:, :D_MODEL], z[:, D_MODEL:]
        sb = _sigmoid(zb)
        ssm = za * sb
        s_ga, s_gs = _sigmoid(g[:, :D_MODEL]), _sigmoid(g[:, D_MODEL:])
        a_n = _rms(attn_v, an_ref[...])
        s_n = _rms(ssm, sn_ref[...])
        dg_ref[:, :D_MODEL] = (dm * a_n * s_ga * (1.0 - s_ga)).astype(BF16)
        dg_ref[:, D_MODEL:] = (dm * s_n * s_gs * (1.0 - s_gs)).astype(BF16)
        dattn, dan = _rms_bwd(attn_v, an_ref[...], dm * s_ga)
        dssm, dsn = _rms_bwd(ssm, sn_ref[...], dm * s_gs)
        da_ref[...] = dattn.astype(BF16)
        dz_ref[:, :D_MODEL] = (dssm * sb).astype(BF16)
        dz_ref[:, D_MODEL:] = (dssm * za * sb * (1.0 - sb)).astype(BF16)
        dan_ref[...] += dan
        dsn_ref[...] += dsn

    return pl.pallas_call(
        body, name="merge_bwd", grid=(l_dim // tm,),
        in_specs=[_row_spec(tm, D_MODEL), _row_spec(tm, 2 * D_MODEL), _row_spec(tm, 2 * D_MODEL),
                  _full_spec((1, D_MODEL)), _full_spec((1, D_MODEL)), _row_spec(tm, D_MODEL)],
        out_specs=[_row_spec(tm, D_MODEL), _row_spec(tm, 2 * D_MODEL), _row_spec(tm, 2 * D_MODEL),
                   _full_spec((1, D_MODEL)), _full_spec((1, D_MODEL))],
        out_shape=[jax.ShapeDtypeStruct((l_dim, D_MODEL), BF16),
                   jax.ShapeDtypeStruct((l_dim, 2 * D_MODEL), BF16),
                   jax.ShapeDtypeStruct((l_dim, 2 * D_MODEL), BF16),
                   jax.ShapeDtypeStruct((1, D_MODEL), F32), jax.ShapeDtypeStruct((1, D_MODEL), F32)],
        compiler_params=_params(),
    )(attn, zab, gates, abn, sbn, dmerged)


FF_TILE = D_FF // 2


def _ffn_in_swiglu(hn, wt_ffn_in, tm):
    l_dim = hn.shape[0]
    nt = (((1,), (1,)), ((), ()))

    def body(a_ref, w_ref, gu_ref, act_ref):
        r = lax.dot_general(a_ref[...], w_ref[...], nt, preferred_element_type=F32)
        gate, up = r[:, :FF_TILE], r[:, FF_TILE:]
        gu_ref[...] = r.astype(BF16)
        act_ref[...] = (gate * _sigmoid(gate) * up).astype(BF16)

    return pl.pallas_call(
        body, name="ffn_in_swiglu", grid=(l_dim // tm, 2),
        in_specs=[pl.BlockSpec((tm, D_MODEL), lambda i, j: (i, 0)),
                  pl.BlockSpec((2 * FF_TILE, D_MODEL), lambda i, j: (j, 0))],
        out_specs=[pl.BlockSpec((tm, 2 * FF_TILE), lambda i, j: (i, j)),
                   pl.BlockSpec((tm, FF_TILE), lambda i, j: (i, j))],
        out_shape=[jax.ShapeDtypeStruct((l_dim, 2 * D_FF), BF16), jax.ShapeDtypeStruct((l_dim, D_FF), BF16)],
        compiler_params=_params(dimension_semantics=("parallel", "parallel")),
    )(hn, wt_ffn_in)


def _d_act_swiglu(dh2_b, w_ffn_out, gu, tm):
    l_dim = dh2_b.shape[0]
    nt = (((1,), (1,)), ((), ()))

    def body(d_ref, w_ref, gu_ref, o_ref):
        d = lax.dot_general(d_ref[...], w_ref[...], nt, preferred_element_type=F32)
        gate = gu_ref[:, :FF_TILE].astype(F32)
        up = gu_ref[:, FF_TILE:].astype(F32)
        s = _sigmoid(gate)
        o_ref[:, :FF_TILE] = (d * up * (s * (1.0 + gate * (1.0 - s)))).astype(BF16)
        o_ref[:, FF_TILE:] = (d * (gate * s)).astype(BF16)

    return pl.pallas_call(
        body, name="d_act_swiglu", grid=(l_dim // tm, 2),
        in_specs=[pl.BlockSpec((tm, D_MODEL), lambda i, j: (i, 0)),
                  pl.BlockSpec((FF_TILE, D_MODEL), lambda i, j: (j, 0)),
                  pl.BlockSpec((tm, 2 * FF_TILE), lambda i, j: (i, j))],
        out_specs=pl.BlockSpec((tm, 2 * FF_TILE), lambda i, j: (i, j)),
        out_shape=jax.ShapeDtypeStruct((l_dim, 2 * D_FF), BF16),
        compiler_params=_params(dimension_semantics=("parallel", "parallel")),
    )(dh2_b, w_ffn_out, gu)


def _loss_grad(h2, target2d):
    l_dim = h2.shape[0]
    nb = l_dim // BLK

    def body(h_ref, t_ref, d_ref, db_ref, loss_ref):
        i = pl.program_id(0)

        @pl.when(i == 0)
        def _():
            loss_ref[...] = jnp.zeros_like(loss_ref)
            d_ref[...] = jnp.zeros_like(d_ref)
            db_ref[...] = jnp.zeros_like(db_ref)

        @pl.when(i > 0)
        def _():
            err = h_ref[...] - t_ref[...]
            d = err * (1.0 / D_MODEL)
            d_ref[...] = d
            db_ref[...] = d.astype(BF16)
            loss_ref[...] += 0.5 * jnp.sum(jnp.mean(err * err, axis=-1, keepdims=True), axis=0, keepdims=True)

    return pl.pallas_call(
        body, name="loss_grad", grid=(nb,),
        in_specs=[_row_spec(BLK, D_MODEL), _row_spec(BLK, D_MODEL, lambda i: jnp.maximum(i - 1, 0))],
        out_specs=[_row_spec(BLK, D_MODEL), _row_spec(BLK, D_MODEL), _full_spec((1, 1))],
        out_shape=[jax.ShapeDtypeStruct((l_dim, D_MODEL), F32), jax.ShapeDtypeStruct((l_dim, D_MODEL), BF16),
                   jax.ShapeDtypeStruct((1, 1), F32)],
        compiler_params=_params(),
    )(h2, target2d)


def _norm_bwd_res(name, h, g, dy, dres, tm):
    l_dim = h.shape[0]

    def body(h_ref, g_ref, dy_ref, dr_ref, o_ref, ob_ref, dg_ref):
        @pl.when(pl.program_id(0) == 0)
        def _():
            dg_ref[...] = jnp.zeros_like(dg_ref)

        dx, dg = _rms_bwd(h_ref[...], g_ref[...], dy_ref[...])
        out = dr_ref[...] + dx
        o_ref[...] = out
        ob_ref[...] = out.astype(BF16)
        dg_ref[...] += dg

    return pl.pallas_call(
        body, name=name, grid=(l_dim // tm,),
        in_specs=[_row_spec(tm, D_MODEL), _full_spec((1, D_MODEL)), _row_spec(tm, D_MODEL), _row_spec(tm, D_MODEL)],
        out_specs=[_row_spec(tm, D_MODEL), _row_spec(tm, D_MODEL), _full_spec((1, D_MODEL))],
        out_shape=[jax.ShapeDtypeStruct((l_dim, D_MODEL), F32), jax.ShapeDtypeStruct((l_dim, D_MODEL), BF16),
                   jax.ShapeDtypeStruct((1, D_MODEL), F32)],
        compiler_params=_params(),
    )(h, g, dy, dres)


def _final_bwd(h0, g, dxn, dh1):
    l_dim = h0.shape[0]
    nb = l_dim // BLK

    def body(h_ref, g_ref, dy_ref, dr_ref, gx_ref, gm_ref, dg_ref):
        i = pl.program_id(0)

        @pl.when(i == 0)
        def _():
            dg_ref[...] = jnp.zeros_like(dg_ref)

        dx, dg = _rms_bwd(h_ref[...], g_ref[...], dy_ref[...])
        out = dr_ref[...] + dx
        dg_ref[...] += dg

        @pl.when(i == 0)
        def _():
            gm_ref[...] = out

        @pl.when(i > 0)
        def _():
            gx_ref[...] = out

    return pl.pallas_call(
        body, name="final_bwd", grid=(nb,),
        in_specs=[_row_spec(BLK, D_MODEL), _full_spec((1, D_MODEL)), _row_spec(BLK, D_MODEL),
                  _row_spec(BLK, D_MODEL)],
        out_specs=[_row_spec(BLK, D_MODEL, lambda i: jnp.maximum(i - 1, 0)), _full_spec((BLK, D_MODEL)),
                   _full_spec((1, D_MODEL))],
        out_shape=[jax.ShapeDtypeStruct((l_dim - BLK, D_MODEL), F32), jax.ShapeDtypeStruct((BLK, D_MODEL), F32),
                   jax.ShapeDtypeStruct((1, D_MODEL), F32)],
        compiler_params=_params(),
    )(h0, g, dxn, dh1)


def _attn_valid(n):
    shape = (2 * BLK, 3 * BLK)
    qi = lax.broadcasted_iota(jnp.int32, shape, 0) & (BLK - 1)
    col = lax.broadcasted_iota(jnp.int32, shape, 1)
    kj = col & (BLK - 1)
    part = col >> 7
    nn = jnp.zeros(shape, jnp.int32) + n
    meta_ok = (part == 0) & (kj >= PAD) & (nn >= 1)
    prev_ok = (part == 1) & (kj > qi) & (nn >= 2)
    cur_ok = (part == 2) & (kj <= qi) & ((nn >= 1) | (kj >= PAD))
    return meta_ok | prev_ok | cur_ok


def _attn_probs(q_ref, kwin, sk_ref, valid, kv, e):
    qs = jnp.concatenate([q_ref[:, BLK * (2 * kv):BLK * (2 * kv) + BLK],
                          q_ref[:, BLK * (2 * kv + 1):BLK * (2 * kv + 1) + BLK]], axis=0)
    s = lax.dot_general(qs, kwin, (((1,), (1,)), ((), ())), preferred_element_type=F32)
    s = jnp.where(valid, s, NEG)
    h0 = 4 * kv + e
    row = lax.broadcasted_iota(jnp.int32, (2 * BLK, 1), 0)
    sink = jnp.where(row < BLK, sk_ref[:, h0:h0 + 1], sk_ref[:, h0 + 2:h0 + 3])
    m = jnp.maximum(jnp.max(s, axis=-1, keepdims=True), sink)
    ex = jnp.exp(s - m)
    es = jnp.exp(sink - m)
    inv = 1.0 / (jnp.sum(ex, axis=-1, keepdims=True) + es)
    return qs, ex * inv, es * inv


def _attn_specs(nb):
    prev = lambda i: jnp.maximum(i - 1, 0)
    zero = lambda i: 0
    kv_specs = [_row_spec(BLK, 8 * BLK, zero), _row_spec(BLK, 8 * BLK, prev), _row_spec(BLK, 8 * BLK)]
    return kv_specs


def _attn_fwd(qn, kf, vf, sinks):
    l_dim = qn.shape[0]
    nb = l_dim // BLK

    def body(q_ref, km_ref, kp_ref, kc_ref, vm_ref, vp_ref, vc_ref, sk_ref, o_ref):
        valid = _attn_valid(pl.program_id(0))
        for kv in range(N_KV_HEADS):
            outs = []
            for e in range(2):
                sl = slice(BLK * (2 * kv + e), BLK * (2 * kv + e) + BLK)
                kwin = jnp.concatenate([km_ref[:, sl], kp_ref[:, sl], kc_ref[:, sl]], axis=0)
                vwin = jnp.concatenate([vm_ref[:, sl], vp_ref[:, sl], vc_ref[:, sl]], axis=0)
                _, p, _ = _attn_probs(q_ref, kwin, sk_ref, valid, kv, e)
                outs.append(jnp.dot(p.astype(BF16), vwin, preferred_element_type=F32))
            o = outs[0] + outs[1]
            o_ref[:, BLK * (2 * kv):BLK * (2 * kv) + BLK] = o[:BLK]
            o_ref[:, BLK * (2 * kv + 1):BLK * (2 * kv + 1) + BLK] = o[BLK:]

    kv_specs = _attn_specs(nb)
    return pl.pallas_call(
        body, name="attn_fwd", grid=(nb,),
        in_specs=[_row_spec(BLK, Q_W)] + kv_specs + kv_specs + [_full_spec((1, N_Q_HEADS))],
        out_specs=_row_spec(BLK, Q_W),
        out_shape=jax.ShapeDtypeStruct((l_dim, Q_W), F32),
        compiler_params=_params(),
    )(qn, kf, kf, kf, vf, vf, vf, sinks)


def _attn_bwd(qn, kf, vf, sinks, attn, dattn):
    l_dim = qn.shape[0]
    nb = l_dim // BLK
    wide = 8 * BLK

    def body(q_ref, km_ref, kp_ref, kc_ref, vm_ref, vp_ref, vc_ref, sk_ref, o_ref, do_ref,
             dq_ref, dkc_ref, dkp_ref, dkm_ref, dvc_ref, dvp_ref, dvm_ref, dsk_ref):
        @pl.when(pl.program_id(0) == 0)
        def _():
            dkm_ref[...] = jnp.zeros_like(dkm_ref)
            dvm_ref[...] = jnp.zeros_like(dvm_ref)
            dsk_ref[...] = jnp.zeros_like(dsk_ref)

        valid = _attn_valid(pl.program_id(0))
        half = _lane_half((BLK, BLK))
        lane16 = lax.broadcasted_iota(jnp.int32, (1, N_Q_HEADS), 1)
        dsk = jnp.zeros((1, N_Q_HEADS), F32)
        for kv in range(N_KV_HEADS):
            j0, j1 = 2 * kv, 2 * kv + 1
            do0 = do_ref[:, BLK * j0:BLK * j0 + BLK]
            do1 = do_ref[:, BLK * j1:BLK * j1 + BLK]
            prod0 = do0.astype(F32) * o_ref[:, BLK * j0:BLK * j0 + BLK]
            prod1 = do1.astype(F32) * o_ref[:, BLK * j1:BLK * j1 + BLK]
            dos = jnp.concatenate([do0, do1], axis=0)
            dqs = []
            for e in range(2):
                sl = slice(BLK * (2 * kv + e), BLK * (2 * kv + e) + BLK)
                kwin = jnp.concatenate([km_ref[:, sl], kp_ref[:, sl], kc_ref[:, sl]], axis=0)
                vwin = jnp.concatenate([vm_ref[:, sl], vp_ref[:, sl], vc_ref[:, sl]], axis=0)
                qs, p, p_sink = _attn_probs(q_ref, kwin, sk_ref, valid, kv, e)
                delta = jnp.concatenate(
                    [jnp.sum(jnp.where(half == e, prod0, 0.0), axis=-1, keepdims=True),
                     jnp.sum(jnp.where(half == e, prod1, 0.0), axis=-1, keepdims=True)], axis=0)
                dp = lax.dot_general(dos, vwin, (((1,), (1,)), ((), ())), preferred_element_type=F32)
                ds = (p * (dp - delta)).astype(BF16)
                pb = p.astype(BF16)
                dqs.append(jnp.dot(ds, kwin, preferred_element_type=F32))
                dk = lax.dot_general(ds, qs, (((0,), (0,)), ((), ())), preferred_element_type=F32)
                dv = lax.dot_general(pb, dos, (((0,), (0,)), ((), ())), preferred_element_type=F32)
                dkm_ref[:, sl] += dk[:BLK]
                dkp_ref[:, sl] = dk[BLK:2 * BLK].astype(BF16)
                dkc_ref[:, sl] = dk[2 * BLK:].astype(BF16)
                dvm_ref[:, sl] += dv[:BLK]
                dvp_ref[:, sl] = dv[BLK:2 * BLK].astype(BF16)
                dvc_ref[:, sl] = dv[2 * BLK:].astype(BF16)
                sink_g = -(p_sink * delta)
                g_lo = jnp.sum(sink_g[:BLK], axis=0, keepdims=True)
                g_hi = jnp.sum(sink_g[BLK:], axis=0, keepdims=True)
                dsk = dsk + jnp.where(lane16 == 4 * kv + e, g_lo, 0.0) + jnp.where(lane16 == 4 * kv + 2 + e, g_hi, 0.0)
            dq = jnp.where(jnp.concatenate([half, half], axis=0) == 0, dqs[0], dqs[1])
            dq_ref[:, BLK * j0:BLK * j0 + BLK] = dq[:BLK].astype(BF16)
            dq_ref[:, BLK * j1:BLK * j1 + BLK] = dq[BLK:].astype(BF16)
        dsk_ref[...] += dsk

    kv_specs = _attn_specs(nb)
    row_wide = _row_spec(BLK, wide)
    acc_wide = _full_spec((BLK, wide))
    big = jax.ShapeDtypeStruct((l_dim, wide), BF16)
    return pl.pallas_call(
        body, name="attn_bwd", grid=(nb,),
        in_specs=[_row_spec(BLK, Q_W)] + kv_specs + kv_specs
        + [_full_spec((1, N_Q_HEADS)), _row_spec(BLK, Q_W), _row_spec(BLK, Q_W)],
        out_specs=[_row_spec(BLK, Q_W), row_wide, row_wide, acc_wide, row_wide, row_wide, acc_wide,
                   _full_spec((1, N_Q_HEADS))],
        out_shape=[jax.ShapeDtypeStruct((l_dim, Q_W), BF16), big, big, jax.ShapeDtypeStruct((BLK, wide), F32),
                   big, big, jax.ShapeDtypeStruct((BLK, wide), F32), jax.ShapeDtypeStruct((1, N_Q_HEADS), F32)],
        compiler_params=_params(),
    )(qn, kf, kf, kf, vf, vf, vf, sinks, attn, dattn)


def _qk_bwd(qkv, q_norm_t, k_norm_t, e_mat, dq, dkc, dkp, dkm, dvc, dvp, dvm):
    l_dim = qkv.shape[0]
    nb = l_dim // BLK
    wide = 8 * BLK

    def fold(x):
        half = _lane_half((BLK, BLK))
        blocks = []
        for kb in range(2):
            t = []
            for kv in (2 * kb, 2 * kb + 1):
                own = kv % 2
                a = x[:, BLK * (2 * kv + own):BLK * (2 * kv + own) + BLK]
                b = pltpu.roll(x[:, BLK * (2 * kv + 1 - own):BLK * (2 * kv + 1 - own) + BLK], HEAD_DIM, axis=1)
                t.append(a + b)
            blocks.append(jnp.where(half == 0, t[0], t[1]))
        return jnp.concatenate(blocks, axis=1)

    def body(x_ref, qg_ref, kg_ref, e_ref, dq_ref, dkc_ref, dkp_ref, dkm_ref, dvc_ref, dvp_ref, dvm_ref,
             o_ref, dqg_ref, dkg_ref):
        i = pl.program_id(0)

        @pl.when(i == 0)
        def _():
            dqg_ref[...] = jnp.zeros_like(dqg_ref)
            dkg_ref[...] = jnp.zeros_like(dkg_ref)

        first = jnp.where(i == 0, 1.0, 0.0)
        not_last = jnp.where(i < nb - 1, 1.0, 0.0)
        dk_x = dkc_ref[...].astype(F32) + not_last * dkp_ref[...].astype(F32) + first * dkm_ref[...]
        dv_x = dvc_ref[...].astype(F32) + not_last * dvp_ref[...].astype(F32) + first * dvm_ref[...]
        x = x_ref[...]
        dqx, dqg = _head_rms_bwd(x[:, :Q_W], qg_ref[...], dq_ref[...].astype(F32) * (HEAD_DIM ** -0.5), e_ref)
        dkx, dkg = _head_rms_bwd(x[:, Q_W:Q_W + KV_W], kg_ref[...], fold(dk_x), e_ref)
        o_ref[:, :Q_W] = dqx.astype(BF16)
        o_ref[:, Q_W:Q_W + KV_W] = dkx.astype(BF16)
        o_ref[:, Q_W + KV_W:] = fold(dv_x).astype(BF16)
        dqg_ref[...] += dqg
        dkg_ref[...] += dkg

    nxt = lambda i: jnp.minimum(i + 1, nb - 1)
    row_wide = _row_spec(BLK, wide)
    nxt_wide = _row_spec(BLK, wide, nxt)
    acc_wide = _full_spec((BLK, wide))
    return pl.pallas_call(
        body, name="qk_bwd", grid=(nb,),
        in_specs=[_row_spec(BLK, Q_W + 2 * KV_W), _full_spec((1, Q_W)), _full_spec((1, KV_W)),
                  _full_spec((256, 256)), _row_spec(BLK, Q_W),
                  row_wide, nxt_wide, acc_wide, row_wide, nxt_wide, acc_wide],
        out_specs=[_row_spec(BLK, Q_W + 2 * KV_W), _full_spec((1, Q_W)), _full_spec((1, KV_W))],
        out_shape=[jax.ShapeDtypeStruct((l_dim, Q_W + 2 * KV_W), BF16),
                   jax.ShapeDtypeStruct((1, Q_W), F32), jax.ShapeDtypeStruct((1, KV_W), F32)],
        compiler_params=_params(),
    )(qkv, q_norm_t, k_norm_t, e_mat, dq, dkc, dkp, dkm, dvc, dvp, dvm)


GRP = 8


def _strided(r, g):
    return pl.ds(r, g, stride=GRP)


def _slab(ref, base, r, g):
    return jnp.concatenate([ref[base + i, _strided(r, g), :] for i in range(LB_KB)], axis=1)


def _slab_store(ref, base, r, g, val):
    for i in range(LB_KB):
        ref[base + i, _strided(r, g), :] = val[:, BLK * i:BLK * i + BLK]


def _group_totals(xr_ref, xi_ref, base, ar, ai, reverse):
    g = xr_ref.shape[1] // GRP
    sr = si = None
    for r in (range(GRP - 1, -1, -1) if reverse else range(GRP)):
        xr, xi = _slab(xr_ref, base, r, g), _slab(xi_ref, base, r, g)
        if sr is not None:
            xr, xi = xr + ar * sr - ai * si, xi + ar * si + ai * sr
        sr, si = xr, xi
    return sr, si


def _carry_scan(tr, ti, sqr_ref, sqi_ref, lanes, sign, reverse):
    g = tr.shape[0]
    row = lax.broadcasted_iota(jnp.int32, tr.shape, 0)
    idx, s = 0, 1
    while s < g:
        ar = sqr_ref[idx:idx + 1, lanes]
        ai = sign * sqi_ref[idx:idx + 1, lanes]
        shift, keep = (g - s, row < g - s) if reverse else (s, row >= s)
        pr = jnp.where(keep, pltpu.roll(tr, shift, axis=0), 0.0)
        pi = jnp.where(keep, pltpu.roll(ti, shift, axis=0), 0.0)
        tr, ti = tr + ar * pr - ai * pi, ti + ar * pi + ai * pr
        idx, s = idx + 1, 2 * s
    return tr, ti


def _ssm_fwd(u, wb_re, wb_im, wc_re, wc_im, d_skip, tabs):
    l_dim = u.shape[0]
    nb = l_dim // BLK
    g = BLK // GRP

    def body(u_ref, wbr_ref, wbi_ref, wcr_ref, wci_ref, d_ref, a1r_ref, a1i_ref, sqr_ref, sqi_ref,
             seqr_ref, seqi_ref, y_ref, z_ref, sr_ref, si_ref, cr_ref, ci_ref, xr_ref, xi_ref):
        @pl.when(pl.program_id(0) == 0)
        def _():
            cr_ref[...] = jnp.zeros_like(cr_ref)
            ci_ref[...] = jnp.zeros_like(ci_ref)

        row = lax.broadcasted_iota(jnp.int32, (g, ST_KB), 0)
        for kb in range(SSM_KB):
            ch = slice(BLK * kb, BLK * kb + BLK)
            lanes = slice(ST_KB * kb, ST_KB * kb + ST_KB)
            u_kb = u_ref[:, ch]
            ub = u_kb.astype(BF16)
            xr = jnp.dot(ub, wbr_ref[kb], preferred_element_type=F32)
            xi = jnp.dot(ub, wbi_ref[kb], preferred_element_type=F32)
            base = LB_KB * kb
            for i in range(LB_KB):
                xr_ref[base + i] = xr[:, BLK * i:BLK * i + BLK]
                xi_ref[base + i] = xi[:, BLK * i:BLK * i + BLK]
            ar, ai = a1r_ref[0:1, lanes], a1i_ref[0:1, lanes]
            tr, ti = _group_totals(xr_ref, xi_ref, base, ar, ai, reverse=False)
            tr, ti = _carry_scan(tr, ti, sqr_ref, sqi_ref, lanes, 1.0, reverse=False)
            cin_r, cin_i = cr_ref[0:1, lanes], ci_ref[0:1, lanes]
            qr, qi = seqr_ref[:, lanes], seqi_ref[:, lanes]
            tr, ti = tr + qr * cin_r - qi * cin_i, ti + qr * cin_i + qi * cin_r
            cr_ref[0:1, lanes] = jnp.sum(jnp.where(row == g - 1, tr, 0.0), axis=0, keepdims=True)
            ci_ref[0:1, lanes] = jnp.sum(jnp.where(row == g - 1, ti, 0.0), axis=0, keepdims=True)
            pr = jnp.where(row == 0, cin_r, pltpu.roll(tr, 1, axis=0))
            pi = jnp.where(row == 0, cin_i, pltpu.roll(ti, 1, axis=0))
            for r in range(GRP):
                pr, pi = (_slab(xr_ref, base, r, g) + ar * pr - ai * pi,
                          _slab(xi_ref, base, r, g) + ar * pi + ai * pr)
                _slab_store(sr_ref, base, r, g, pr)
                _slab_store(si_ref, base, r, g, pi)
            s_r = jnp.concatenate([sr_ref[LB_KB * kb + i] for i in range(LB_KB)], axis=1)
            s_i = jnp.concatenate([si_ref[LB_KB * kb + i] for i in range(LB_KB)], axis=1)
            y = (jnp.dot(s_r.astype(BF16), wcr_ref[kb], preferred_element_type=F32)
                 - jnp.dot(s_i.astype(BF16), wci_ref[kb], preferred_element_type=F32)
                 + d_ref[:, ch] * u_kb)
            y_ref[:, ch] = y
            z_ref[:, ch] = _gelu(y).astype(BF16)

    wb_spec = _full_spec((SSM_KB, BLK, ST_KB))
    wc_spec = _full_spec((SSM_KB, ST_KB, BLK))
    tab_specs = [_full_spec(t.shape) for t in tabs[:6]]
    state_spec = pl.BlockSpec((N_LB, BLK, BLK), lambda i: (0, i, 0))
    state_shape = jax.ShapeDtypeStruct((N_LB, l_dim, BLK), F32)
    return pl.pallas_call(
        body, name="ssm_fwd", grid=(nb,),
        in_specs=[_row_spec(BLK, D_MODEL), wb_spec, wb_spec, wc_spec, wc_spec, _full_spec((1, D_MODEL))]
        + tab_specs,
        out_specs=[_row_spec(BLK, D_MODEL), _row_spec(BLK, D_MODEL), state_spec, state_spec],
        out_shape=[jax.ShapeDtypeStruct((l_dim, D_MODEL), F32), jax.ShapeDtypeStruct((l_dim, D_MODEL), BF16),
                   state_shape, state_shape],
        scratch_shapes=[pltpu.VMEM((8, N_STATE), F32), pltpu.VMEM((8, N_STATE), F32),
                        pltpu.VMEM((N_LB, BLK, BLK), F32), pltpu.VMEM((N_LB, BLK, BLK), F32)],
        compiler_params=_params(),
    )(u, wb_re, wb_im, wc_re, wc_im, d_skip, *tabs[:6])


def _ssm_bwd(dz, y, u, s_re, s_im, wb_re, wb_im, wc_re, wc_im, d_skip, tabs):
    l_dim = u.shape[0]
    nb = l_dim // BLK
    g = BLK // GRP

    def body(dz_ref, y_ref, u_ref, sr_ref, si_ref, wbr_ref, wbi_ref, wcr_ref, wci_ref, d_ref,
             a1r_ref, a1i_ref, sqr_ref, sqi_ref, revr_ref, revi_ref,
             du_ref, dd_ref, dar_ref, dai_ref, dwbr_ref, dwbi_ref, dwcr_ref, dwci_ref,
             cr_ref, ci_ref, gr_ref, gi_ref):
        @pl.when(pl.program_id(0) == 0)
        def _():
            for r in (cr_ref, ci_ref, dd_ref, dar_ref, dai_ref, dwbr_ref, dwbi_ref, dwcr_ref, dwci_ref):
                r[...] = jnp.zeros_like(r)

        tn = (((0,), (0,)), ((), ()))
        nt = (((1,), (1,)), ((), ()))
        row = lax.broadcasted_iota(jnp.int32, (g, ST_KB), 0)
        for kb in range(SSM_KB):
            ch = slice(BLK * kb, BLK * kb + BLK)
            lanes = slice(ST_KB * kb, ST_KB * kb + ST_KB)
            u_kb = u_ref[:, ch]
            dy = dz_ref[:, ch] * _gelu_grad(y_ref[:, ch])
            dyb = dy.astype(BF16)
            ub = u_kb.astype(BF16)
            dd_ref[:, ch] += jnp.sum(dy * u_kb, axis=0, keepdims=True)
            ds_r = lax.dot_general(dyb, wcr_ref[kb], nt, preferred_element_type=F32)
            ds_i = -lax.dot_general(dyb, wci_ref[kb], nt, preferred_element_type=F32)
            base = LB_KB * kb
            for i in range(LB_KB):
                gr_ref[base + i] = ds_r[:, BLK * i:BLK * i + BLK]
                gi_ref[base + i] = ds_i[:, BLK * i:BLK * i + BLK]
            ar, ai = a1r_ref[0:1, lanes], -a1i_ref[0:1, lanes]
            tr, ti = _group_totals(gr_ref, gi_ref, base, ar, ai, reverse=True)
            tr, ti = _carry_scan(tr, ti, sqr_ref, sqi_ref, lanes, -1.0, reverse=True)
            cin_r, cin_i = cr_ref[0:1, lanes], ci_ref[0:1, lanes]
            qr, qi = revr_ref[:, lanes], -revi_ref[:, lanes]
            tr, ti = tr + qr * cin_r - qi * cin_i, ti + qr * cin_i + qi * cin_r
            cr_ref[0:1, lanes] = jnp.sum(jnp.where(row == 0, tr, 0.0), axis=0, keepdims=True)
            ci_ref[0:1, lanes] = jnp.sum(jnp.where(row == 0, ti, 0.0), axis=0, keepdims=True)
            nr = jnp.where(row == g - 1, cin_r, pltpu.roll(tr, g - 1, axis=0))
            ni = jnp.where(row == g - 1, cin_i, pltpu.roll(ti, g - 1, axis=0))
            acc_r = jnp.zeros((g, ST_KB), F32)
            acc_i = jnp.zeros((g, ST_KB), F32)
            for r in range(GRP - 1, -1, -1):
                s_r, s_i = _slab(sr_ref, base, r, g), _slab(si_ref, base, r, g)
                acc_r = acc_r + (nr * s_r + ni * s_i)
                acc_i = acc_i + (ni * s_r - nr * s_i)
                nr, ni = (_slab(gr_ref, base, r, g) + ar * nr - ai * ni,
                          _slab(gi_ref, base, r, g) + ar * ni + ai * nr)
                _slab_store(gr_ref, base, r, g, nr)
                _slab_store(gi_ref, base, r, g, ni)
            dar_ref[:, lanes] += jnp.sum(acc_r, axis=0, keepdims=True)
            dai_ref[:, lanes] += jnp.sum(acc_i, axis=0, keepdims=True)
            grb = jnp.concatenate([gr_ref[base + i] for i in range(LB_KB)], axis=1).astype(BF16)
            gib = jnp.concatenate([gi_ref[base + i] for i in range(LB_KB)], axis=1).astype(BF16)
            srb = jnp.concatenate([sr_ref[base + i] for i in range(LB_KB)], axis=1).astype(BF16)
            sib = jnp.concatenate([si_ref[base + i] for i in range(LB_KB)], axis=1).astype(BF16)
            du = (lax.dot_general(grb, wbr_ref[kb], nt, preferred_element_type=F32)
                  + lax.dot_general(gib, wbi_ref[kb], nt, preferred_element_type=F32)
                  + d_ref[:, ch] * dy)
            du_ref[:, ch] = du.astype(BF16)
            dwbr_ref[kb] += lax.dot_general(ub, grb, tn, preferred_element_type=F32)
            dwbi_ref[kb] += lax.dot_general(ub, gib, tn, preferred_element_type=F32)
            dwcr_ref[kb] += lax.dot_general(srb, dyb, tn, preferred_element_type=F32)
            dwci_ref[kb] -= lax.dot_general(sib, dyb, tn, preferred_element_type=F32)

    rev = lambda i: nb - 1 - i
    wb_spec = _full_spec((SSM_KB, BLK, ST_KB))
    wc_spec = _full_spec((SSM_KB, ST_KB, BLK))
    tab_in = [tabs[0], tabs[1], tabs[2], tabs[3], tabs[6], tabs[7]]
    tab_specs = [_full_spec(t.shape) for t in tab_in]
    vec = _full_spec((1, D_MODEL))
    svec = _full_spec((1, N_STATE))
    state_spec = pl.BlockSpec((N_LB, BLK, BLK), lambda i: (0, nb - 1 - i, 0))
    return pl.pallas_call(
        body, name="ssm_bwd", grid=(nb,),
        in_specs=[_row_spec(BLK, D_MODEL, rev), _row_spec(BLK, D_MODEL, rev), _row_spec(BLK, D_MODEL, rev),
                  state_spec, state_spec,
                  wb_spec, wb_spec, wc_spec, wc_spec, vec] + tab_specs,
        out_specs=[_row_spec(BLK, D_MODEL, rev), vec, svec, svec, wb_spec, wb_spec, wc_spec, wc_spec],
        out_shape=[jax.ShapeDtypeStruct((l_dim, D_MODEL), BF16), jax.ShapeDtypeStruct((1, D_MODEL), F32),
                   jax.ShapeDtypeStruct((1, N_STATE), F32), jax.ShapeDtypeStruct((1, N_STATE), F32),
                   jax.ShapeDtypeStruct((SSM_KB, BLK, ST_KB), F32), jax.ShapeDtypeStruct((SSM_KB, BLK, ST_KB), F32),
                   jax.ShapeDtypeStruct((SSM_KB, ST_KB, BLK), F32), jax.ShapeDtypeStruct((SSM_KB, ST_KB, BLK), F32)],
        scratch_shapes=[pltpu.VMEM((8, N_STATE), F32), pltpu.VMEM((8, N_STATE), F32),
                        pltpu.VMEM((N_LB, BLK, BLK), F32), pltpu.VMEM((N_LB, BLK, BLK), F32)],
        compiler_params=_params(),
    )(dz, y, u, s_re, s_im, wb_re, wb_im, wc_re, wc_im, d_skip, *tab_in)


def _lane_slab(ref, i, r, g):
    return ref[i, _strided(r, g), :]


def _chunk_carries(xr_ref, xi_ref, i, ar, ai, sqr_ref, sqi_ref, seq_r, seq_i, cin_r, cin_i, sign, reverse):
    g = xr_ref.shape[1] // GRP
    sr = si = None
    for r in (range(GRP - 1, -1, -1) if reverse else range(GRP)):
        xr, xi = _lane_slab(xr_ref, i, r, g), _lane_slab(xi_ref, i, r, g)
        if sr is not None:
            xr, xi = xr + ar * sr - ai * si, xi + ar * si + ai * sr
        sr, si = xr, xi
    row = lax.broadcasted_iota(jnp.int32, sr.shape, 0)
    idx, s = 0, 1
    while s < g:
        br = sqr_ref[idx:idx + 1, BLK * i:BLK * i + BLK]
        bi = sign * sqi_ref[idx:idx + 1, BLK * i:BLK * i + BLK]
        shift, keep = (g - s, row < g - s) if reverse else (s, row >= s)
        pr = jnp.where(keep, pltpu.roll(sr, shift, axis=0), 0.0)
        pi = jnp.where(keep, pltpu.roll(si, shift, axis=0), 0.0)
        sr, si = sr + br * pr - bi * pi, si + br * pi + bi * pr
        idx, s = idx + 1, 2 * s
    return sr + seq_r * cin_r - seq_i * cin_i, si + seq_r * cin_i + seq_i * cin_r


def _ssm_fwd_kb(u, wb_re, wb_im, wc_re, wc_im, d_skip, tabs, q):
    l_dim = u.shape[0]
    nc = l_dim // q
    g = q // GRP

    def body(u_ref, wbr_ref, wbi_ref, wcr_ref, wci_ref, d_ref, a1r_ref, a1i_ref, sqr_ref, sqi_ref,
             seqr_ref, seqi_ref, y_ref, z_ref, sr_ref, si_ref, cr_ref, ci_ref, xr_ref, xi_ref):
        @pl.when(pl.program_id(1) == 0)
        def _():
            cr_ref[...] = jnp.zeros_like(cr_ref)
            ci_ref[...] = jnp.zeros_like(ci_ref)

        u_kb = u_ref[...]
        ub = u_kb.astype(BF16)
        xr = jnp.dot(ub, wbr_ref[0], preferred_element_type=F32)
        xi = jnp.dot(ub, wbi_ref[0], preferred_element_type=F32)
        for i in range(LB_KB):
            xr_ref[i] = xr[:, BLK * i:BLK * i + BLK]
            xi_ref[i] = xi[:, BLK * i:BLK * i + BLK]
        row = lax.broadcasted_iota(jnp.int32, (g, BLK), 0)
        for i in range(LB_KB):
            lanes = slice(BLK * i, BLK * i + BLK)
            ar, ai = a1r_ref[0:1, lanes], a1i_ref[0:1, lanes]
            cin_r, cin_i = cr_ref[0:1, lanes], ci_ref[0:1, lanes]
            tr, ti = _chunk_carries(xr_ref, xi_ref, i, ar, ai, sqr_ref, sqi_ref, seqr_ref[:, lanes],
                                    seqi_ref[:, lanes], cin_r, cin_i, 1.0, reverse=False)
            cr_ref[0:1, lanes] = jnp.sum(jnp.where(row == g - 1, tr, 0.0), axis=0, keepdims=True)
            ci_ref[0:1, lanes] = jnp.sum(jnp.where(row == g - 1, ti, 0.0), axis=0, keepdims=True)
            pr = jnp.where(row == 0, cin_r, pltpu.roll(tr, 1, axis=0))
            pi = jnp.where(row == 0, cin_i, pltpu.roll(ti, 1, axis=0))
            for r in range(GRP):
                pr, pi = (_lane_slab(xr_ref, i, r, g) + ar * pr - ai * pi,
                          _lane_slab(xi_ref, i, r, g) + ar * pi + ai * pr)
                sr_ref[i, _strided(r, g), :] = pr
                si_ref[i, _strided(r, g), :] = pi
        s_r = jnp.concatenate([sr_ref[i] for i in range(LB_KB)], axis=1)
        s_i = jnp.concatenate([si_ref[i] for i in range(LB_KB)], axis=1)
        y = (jnp.dot(s_r.astype(BF16), wcr_ref[0], preferred_element_type=F32)
             - jnp.dot(s_i.astype(BF16), wci_ref[0], preferred_element_type=F32)
             + d_ref[...] * u_kb)
        y_ref[...] = y.astype(BF16)
        z_ref[...] = _gelu(y).astype(BF16)

    chan = pl.BlockSpec((q, BLK), lambda k, c: (c, k))
    wb_spec = pl.BlockSpec((1, BLK, ST_KB), lambda k, c: (k, 0, 0))
    wc_spec = pl.BlockSpec((1, ST_KB, BLK), lambda k, c: (k, 0, 0))
    tab_specs = [pl.BlockSpec((t.shape[0], ST_KB), lambda k, c: (0, k)) for t in tabs[:6]]
    state_spec = pl.BlockSpec((LB_KB, q, BLK), lambda k, c: (k, c, 0))
    state_shape = jax.ShapeDtypeStruct((N_LB, l_dim, BLK), F32)
    return pl.pallas_call(
        body, name="ssm_fwd", grid=(SSM_KB, nc),
        in_specs=[chan, wb_spec, wb_spec, wc_spec, wc_spec, pl.BlockSpec((1, BLK), lambda k, c: (0, k))] + tab_specs,
        out_specs=[chan, chan, state_spec, state_spec],
        out_shape=[jax.ShapeDtypeStruct((l_dim, D_MODEL), BF16), jax.ShapeDtypeStruct((l_dim, D_MODEL), BF16),
                   state_shape, state_shape],
        scratch_shapes=[pltpu.VMEM((8, ST_KB), F32), pltpu.VMEM((8, ST_KB), F32),
                        pltpu.VMEM((LB_KB, q, BLK), F32), pltpu.VMEM((LB_KB, q, BLK), F32)],
        compiler_params=_params(dimension_semantics=("parallel", "arbitrary")),
    )(u, wb_re, wb_im, wc_re, wc_im, d_skip, *tabs[:6])


def _ssm_bwd_kb(dz, y, u, s_re, s_im, wb_re, wb_im, wc_re, wc_im, d_skip, tabs, q):
    l_dim = u.shape[0]
    nc = l_dim // q
    g = q // GRP

    def body(dz_ref, y_ref, u_ref, sr_ref, si_ref, wbr_ref, wbi_ref, wcr_ref, wci_ref, d_ref,
             a1r_ref, a1i_ref, sqr_ref, sqi_ref, revr_ref, revi_ref,
             du_ref, dd_ref, dar_ref, dai_ref, dwbr_ref, dwbi_ref, dwcr_ref, dwci_ref,
             cr_ref, ci_ref, gr_ref, gi_ref):
        @pl.when(pl.program_id(1) == 0)
        def _():
            for ref in (cr_ref, ci_ref, dd_ref, dar_ref, dai_ref, dwbr_ref, dwbi_ref, dwcr_ref, dwci_ref):
                ref[...] = jnp.zeros_like(ref)

        tn = (((0,), (0,)), ((), ()))
        nt = (((1,), (1,)), ((), ()))
        u_kb = u_ref[...]
        dy = dz_ref[...] * _gelu_grad(y_ref[...].astype(F32))
        dyb = dy.astype(BF16)
        ub = u_kb.astype(BF16)
        dd_ref[...] += jnp.sum(dy * u_kb, axis=0, keepdims=True)
        ds_r = lax.dot_general(dyb, wcr_ref[0], nt, preferred_element_type=F32)
        ds_i = -lax.dot_general(dyb, wci_ref[0], nt, preferred_element_type=F32)
        for i in range(LB_KB):
            gr_ref[i] = ds_r[:, BLK * i:BLK * i + BLK]
            gi_ref[i] = ds_i[:, BLK * i:BLK * i + BLK]
        row = lax.broadcasted_iota(jnp.int32, (g, BLK), 0)
        for i in range(LB_KB):
            lanes = slice(BLK * i, BLK * i + BLK)
            ar, ai = a1r_ref[0:1, lanes], -a1i_ref[0:1, lanes]
            cin_r, cin_i = cr_ref[0:1, lanes], ci_ref[0:1, lanes]
            tr, ti = _chunk_carries(gr_ref, gi_ref, i, ar, ai, sqr_ref, sqi_ref, revr_ref[:, lanes],
                                    -revi_ref[:, lanes], cin_r, cin_i, -1.0, reverse=True)
            cr_ref[0:1, lanes] = jnp.sum(jnp.where(row == 0, tr, 0.0), axis=0, keepdims=True)
            ci_ref[0:1, lanes] = jnp.sum(jnp.where(row == 0, ti, 0.0), axis=0, keepdims=True)
            nr = jnp.where(row == g - 1, cin_r, pltpu.roll(tr, g - 1, axis=0))
            ni = jnp.where(row == g - 1, cin_i, pltpu.roll(ti, g - 1, axis=0))
            acc_r = jnp.zeros((g, BLK), F32)
            acc_i = jnp.zeros((g, BLK), F32)
            for r in range(GRP - 1, -1, -1):
                s_r, s_i = _lane_slab(sr_ref, i, r, g), _lane_slab(si_ref, i, r, g)
                acc_r = acc_r + (nr * s_r + ni * s_i)
                acc_i = acc_i + (ni * s_r - nr * s_i)
                nr, ni = (_lane_slab(gr_ref, i, r, g) + ar * nr - ai * ni,
                          _lane_slab(gi_ref, i, r, g) + ar * ni + ai * nr)
                gr_ref[i, _strided(r, g), :] = nr
                gi_ref[i, _strided(r, g), :] = ni
            dar_ref[:, lanes] += jnp.sum(acc_r, axis=0, keepdims=True)
            dai_ref[:, lanes] += jnp.sum(acc_i, axis=0, keepdims=True)
        grb = jnp.concatenate([gr_ref[i] for i in range(LB_KB)], axis=1).astype(BF16)
        gib = jnp.concatenate([gi_ref[i] for i in range(LB_KB)], axis=1).astype(BF16)
        srb = jnp.concatenate([sr_ref[i] for i in range(LB_KB)], axis=1).astype(BF16)
        sib = jnp.concatenate([si_ref[i] for i in range(LB_KB)], axis=1).astype(BF16)
        du = (lax.dot_general(grb, wbr_ref[0], nt, preferred_element_type=F32)
              + lax.dot_general(gib, wbi_ref[0], nt, preferred_element_type=F32)
              + d_ref[...] * dy)
        du_ref[...] = du.astype(BF16)
        dwbr_ref[0] += lax.dot_general(ub, grb, tn, preferred_element_type=F32)
        dwbi_ref[0] += lax.dot_general(ub, gib, tn, preferred_element_type=F32)
        dwcr_ref[0] += lax.dot_general(srb, dyb, tn, preferred_element_type=F32)
        dwci_ref[0] -= lax.dot_general(sib, dyb, tn, preferred_element_type=F32)

    chan = pl.BlockSpec((q, BLK), lambda k, c: (nc - 1 - c, k))
    wb_spec = pl.BlockSpec((1, BLK, ST_KB), lambda k, c: (k, 0, 0))
    wc_spec = pl.BlockSpec((1, ST_KB, BLK), lambda k, c: (k, 0, 0))
    tab_in = [tabs[0], tabs[1], tabs[2], tabs[3], tabs[6], tabs[7]]
    tab_specs = [pl.BlockSpec((t.shape[0], ST_KB), lambda k, c: (0, k)) for t in tab_in]
    vec = pl.BlockSpec((1, BLK), lambda k, c: (0, k))
    svec = pl.BlockSpec((1, ST_KB), lambda k, c: (0, k))
    state_spec = pl.BlockSpec((LB_KB, q, BLK), lambda k, c: (k, nc - 1 - c, 0))
    return pl.pallas_call(
        body, name="ssm_bwd", grid=(SSM_KB, nc),
        in_specs=[chan, chan, chan, state_spec, state_spec, wb_spec, wb_spec, wc_spec, wc_spec, vec] + tab_specs,
        out_specs=[chan, vec, svec, svec, wb_spec, wb_spec, wc_spec, wc_spec],
        out_shape=[jax.ShapeDtypeStruct((l_dim, D_MODEL), BF16), jax.ShapeDtypeStruct((1, D_MODEL), F32),
                   jax.ShapeDtypeStruct((1, N_STATE), F32), jax.ShapeDtypeStruct((1, N_STATE), F32),
                   jax.ShapeDtypeStruct((SSM_KB, BLK, ST_KB), F32), jax.ShapeDtypeStruct((SSM_KB, BLK, ST_KB), F32),
                   jax.ShapeDtypeStruct((SSM_KB, ST_KB, BLK), F32), jax.ShapeDtypeStruct((SSM_KB, ST_KB, BLK), F32)],
        scratch_shapes=[pltpu.VMEM((8, ST_KB), F32), pltpu.VMEM((8, ST_KB), F32),
                        pltpu.VMEM((LB_KB, q, BLK), F32), pltpu.VMEM((LB_KB, q, BLK), F32)],
        compiler_params=_params(dimension_semantics=("parallel", "arbitrary")),
    )(dz, y, u, s_re, s_im, wb_re, wb_im, wc_re, wc_im, d_skip, *tab_in)


def _discretize(lam_re, lam_im, log_dt, b_re, b_im):
    dt = jnp.exp(log_dt)[:, None]
    mag = jnp.exp(lam_re * dt)
    ar, ai = mag * jnp.cos(lam_im * dt), mag * jnp.sin(lam_im * dt)
    den = lam_re * lam_re + lam_im * lam_im
    nr, ni = ar - 1.0, ai
    fr, fi = (nr * lam_re + ni * lam_im) / den, (ni * lam_re - nr * lam_im) / den
    bbar_re = fr[..., None] * b_re - fi[..., None] * b_im
    bbar_im = fr[..., None] * b_im + fi[..., None] * b_re
    return ar, ai, bbar_re, bbar_im


def _block_diag_b(bbar):
    eye = jnp.eye(8, dtype=bbar.dtype)
    return jnp.einsum("kgpc,gh->kgchp", bbar.reshape(8, 8, SSM_STATE, SSM_GROUP_CH), eye).reshape(8, BLK, ST_KB)


def _block_diag_b_t(dwb):
    eye = jnp.eye(8, dtype=dwb.dtype)
    return jnp.einsum("kgchp,gh->kgpc", dwb.reshape(8, 8, SSM_GROUP_CH, 8, SSM_STATE), eye).reshape(
        SSM_GROUPS, SSM_STATE, SSM_GROUP_CH)


def _block_diag_c(c):
    eye = jnp.eye(8, dtype=c.dtype)
    return jnp.einsum("kgcp,gh->kgphc", c.reshape(8, 8, SSM_GROUP_CH, SSM_STATE), eye).reshape(8, ST_KB, BLK)


def _block_diag_c_t(dwc):
    eye = jnp.eye(8, dtype=dwc.dtype)
    return jnp.einsum("kgphc,gh->kgcp", dwc.reshape(8, 8, SSM_STATE, 8, SSM_GROUP_CH), eye).reshape(
        SSM_GROUPS, SSM_GROUP_CH, SSM_STATE)


def _powers(br, bi, n):
    pr, pi = br, bi
    cr, ci = br, bi
    while pr.shape[0] < n:
        pr, pi = (jnp.concatenate([pr, pr * cr - pi * ci], axis=0),
                  jnp.concatenate([pi, pr * ci + pi * cr], axis=0))
        cr, ci = cr * cr - ci * ci, 2.0 * cr * ci
    return pr[:n], pi[:n]


def _powers_desc(br, bi, n):
    pr, pi = br, bi
    cr, ci = br, bi
    while pr.shape[0] < n:
        pr, pi = (jnp.concatenate([pr * cr - pi * ci, pr], axis=0),
                  jnp.concatenate([pr * ci + pi * cr, pi], axis=0))
        cr, ci = cr * cr - ci * ci, 2.0 * cr * ci
    return pr, pi


def _power_tables(ar, ai, g):
    a1r, a1i = _powers(ar, ai, GRP)
    seqr, seqi = _powers(a1r[GRP - 1:], a1i[GRP - 1:], g)
    g2 = 1 << (g - 1).bit_length()
    revr, revi = _powers_desc(a1r[GRP - 1:], a1i[GRP - 1:], g2)
    revr, revi = revr[g2 - g:], revi[g2 - g:]
    sq_r, sq_i = [seqr[0:1]], [seqi[0:1]]
    while len(sq_r) < 8:
        r, i = sq_r[-1], sq_i[-1]
        sq_r.append(r * r - i * i)
        sq_i.append(2.0 * r * i)
    sqr, sqi = jnp.concatenate(sq_r, axis=0), jnp.concatenate(sq_i, axis=0)
    return a1r, a1i, sqr, sqi, seqr, seqi, revr, revi


HBM_SPEC = pl.BlockSpec(memory_space=pltpu.HBM)
SEM_SPEC = pl.BlockSpec(memory_space=pltpu.SEMAPHORE)
DATAFLOW = pltpu.SideEffectType.DATAFLOW_SIDE_EFFECTING


def _plain_rows(p, m):
    return p * m


def _ffn_in_rows(p, m):
    return ((p & 3) >> 1) * (4 * m) + (p >> 2) * (2 * m) + (p & 1) * m


def _peer_copies(src_refs, land_refs, send_sems, recv_sems, chunked, row_fns):
    x, y, c = lax.axis_index("x"), lax.axis_index("y"), lax.axis_index("c")
    me = 4 * x + 2 * y + c
    copies = []
    for a, (src, land) in enumerate(zip(src_refs, land_refs)):
        m = land.shape[0] // N_DEV
        for k in range(N_DEV - 1):
            rel = k + 1
            bx, by, bc = (rel >> 2) & 1, (rel >> 1) & 1, rel & 1
            peer = (x + bx - 2 * x * bx, y + by - 2 * y * by, c + bc - 2 * c * bc)
            p_idx = 4 * peer[0] + 2 * peer[1] + peer[2]
            copies.append(pltpu.make_async_remote_copy(
                src_ref=src.at[pl.ds(row_fns[a](p_idx, m), m), :] if chunked else src,
                dst_ref=land.at[pl.ds(me * m if chunked else row_fns[a](me, m), m), :],
                send_sem=send_sems[a * (N_DEV - 1) + k], recv_sem=recv_sems[a * (N_DEV - 1) + k],
                device_id=peer, device_id_type=MESH))
    return copies


def _send_start(name, srcs, lands, chunked, row_fns=None):
    n = len(srcs)
    ns = n * (N_DEV - 1)
    row_fns = row_fns or [_plain_rows] * n

    def body(*refs):
        src_refs, land_refs = refs[:n], refs[n:2 * n]
        send_sems, recv_sems = refs[2 * n:2 * n + ns], refs[2 * n + ns:2 * n + 2 * ns]
        token = refs[-1]
        for cp in _peer_copies(src_refs, land_refs, send_sems, recv_sems, chunked, row_fns):
            cp.start()
        token[...] = jnp.zeros_like(token)

    ins = [pltpu.with_memory_space_constraint(a, pltpu.HBM) for a in list(srcs) + list(lands)]
    out = pl.pallas_call(
        body, name=name,
        in_specs=[HBM_SPEC] * (2 * n),
        out_specs=[SEM_SPEC] * (2 * ns) + [HBM_SPEC] * (2 * n) + [pl.BlockSpec(memory_space=pltpu.VMEM)],
        out_shape=[pltpu.SemaphoreType.DMA(())] * (2 * ns)
        + [pltpu.HBM(a.shape, a.dtype) for a in list(srcs) + list(lands)]
        + [jax.ShapeDtypeStruct((8, BLK), F32)],
        input_output_aliases={i: i + 2 * ns for i in range(2 * n)},
        compiler_params=pltpu.CompilerParams(has_side_effects=DATAFLOW),
    )(*ins)
    return out[:ns], out[ns:2 * ns], out[2 * ns:2 * ns + n], out[2 * ns + n:2 * ns + 2 * n], out[-1]


def _send_wait(name, send_sems, recv_sems, srcs, lands, after, chunked, row_fns=None):
    n = len(srcs)
    ns = n * (N_DEV - 1)
    row_fns = row_fns or [_plain_rows] * n

    def body(*refs):
        src_refs, land_refs = refs[:n], refs[n:2 * n]
        s_sems, r_sems = refs[2 * n:2 * n + ns], refs[2 * n + ns:2 * n + 2 * ns]
        copies = _peer_copies(src_refs, land_refs, s_sems, r_sems, chunked, row_fns)
        for cp in copies:
            cp.wait_send()
        for cp in copies:
            cp.wait_recv()

    out = pl.pallas_call(
        body, name=name,
        in_specs=[HBM_SPEC] * (2 * n) + [SEM_SPEC] * (2 * ns) + [pl.BlockSpec(memory_space=pl.ANY)],
        out_specs=[HBM_SPEC] * (2 * n),
        out_shape=[pltpu.HBM(a.shape, a.dtype) for a in list(srcs) + list(lands)],
        input_output_aliases={i: i for i in range(2 * n)},
        compiler_params=pltpu.CompilerParams(has_side_effects=DATAFLOW),
    )(*srcs, *lands, *send_sems, *recv_sems, after)
    return out[n:]


def _sum_slots(name, recv, own):
    m, ncol = own.shape
    tr = m // 2 if (m // 2) % 16 == 0 else m
    g = m // tr

    def body(*refs):
        slots, own_ref, o_ref = refs[:N_DEV], refs[N_DEV], refs[N_DEV + 1]
        me = _my_index()
        tot = None
        for s in range(N_DEV):
            v = jnp.where(me == s, own_ref[...], slots[s][...].astype(F32))
            tot = v if tot is None else tot + v
        o_ref[...] = tot

    def slot_spec(s):
        return pl.BlockSpec((tr, ncol), lambda i: (s * g + i, 0))

    return pl.pallas_call(
        body, name=name, grid=(g,),
        in_specs=[slot_spec(s) for s in range(N_DEV)] + [pl.BlockSpec((tr, ncol), lambda i: (i, 0))],
        out_specs=pl.BlockSpec((tr, ncol), lambda i: (i, 0)),
        out_shape=jax.ShapeDtypeStruct((m, ncol), F32),
        compiler_params=_params(),
    )(*([recv] * N_DEV), own)


def _sum_gathered(name, gathered, rows):
    tr = _pick(rows, 512, 8)
    g = rows // tr

    def body(*refs):
        o_ref = refs[N_DEV]
        tot = refs[0][...]
        for s in range(1, N_DEV):
            tot = tot + refs[s][...]
        o_ref[...] = tot

    return pl.pallas_call(
        body, name=name, grid=(g,),
        in_specs=[pl.BlockSpec((tr, BLK), (lambda i, s=s: (s * g + i, 0))) for s in range(N_DEV)],
        out_specs=pl.BlockSpec((tr, BLK), lambda i: (i, 0)),
        out_shape=jax.ShapeDtypeStruct((rows, BLK), F32),
        compiler_params=_params(),
    )(*([gathered] * N_DEV))


def _adamw(name, w, g, m, v):
    r, c = w.shape
    tr = _pick(r, 256, 8) if r % 8 == 0 else r
    c1 = 1.0 - ADAM_B1 ** ADAM_STEP
    c2 = 1.0 - ADAM_B2 ** ADAM_STEP

    def body(w_ref, g_ref, m_ref, v_ref, d_ref, nm_ref, nv_ref):
        gv = g_ref[...]
        nm = ADAM_B1 * m_ref[...] + (1.0 - ADAM_B1) * gv
        nv = ADAM_B2 * v_ref[...] + (1.0 - ADAM_B2) * (gv * gv)
        m_hat = nm / c1
        v_hat = nv / c2
        d_ref[...] = -ADAM_LR * (m_hat / (jnp.sqrt(v_hat) + ADAM_EPS) + ADAM_WD * w_ref[...])
        nm_ref[...] = nm
        nv_ref[...] = nv

    spec = pl.BlockSpec((tr, c), lambda i: (i, 0))
    shape = jax.ShapeDtypeStruct((r, c), F32)
    return pl.pallas_call(
        body, name=name, grid=(r // tr,),
        in_specs=[spec] * 4, out_specs=[spec] * 3, out_shape=[shape] * 3,
        compiler_params=_params(),
    )(w, g, m, v)


def _adamw_many(name, ws, gs, ms, vs):
    n = len(ws)
    c1 = 1.0 - ADAM_B1 ** ADAM_STEP
    c2 = 1.0 - ADAM_B2 ** ADAM_STEP

    def body(*refs):
        for a in range(n):
            w_ref, g_ref, m_ref, v_ref = refs[a], refs[n + a], refs[2 * n + a], refs[3 * n + a]
            d_ref, nm_ref, nv_ref = refs[4 * n + a], refs[5 * n + a], refs[6 * n + a]
            gv = g_ref[...]
            nm = ADAM_B1 * m_ref[...] + (1.0 - ADAM_B1) * gv
            nv = ADAM_B2 * v_ref[...] + (1.0 - ADAM_B2) * (gv * gv)
            d_ref[...] = -ADAM_LR * ((nm / c1) / (jnp.sqrt(nv / c2) + ADAM_EPS) + ADAM_WD * w_ref[...])
            nm_ref[...] = nm
            nv_ref[...] = nv

    vmem = pl.BlockSpec(memory_space=pltpu.VMEM)
    shapes = [jax.ShapeDtypeStruct(w.shape, F32) for w in ws]
    out = pl.pallas_call(
        body, name=name,
        in_specs=[vmem] * (4 * n), out_specs=[vmem] * (3 * n), out_shape=shapes * 3,
        compiler_params=_params(),
    )(*ws, *gs, *ms, *vs)
    return out[:n], out[n:2 * n], out[2 * n:]


PACK_ROWS = 128


def _pack(parts):
    flat = []
    for p in parts:
        v = p.reshape(-1)
        flat.append(jnp.pad(v, (0, (-v.shape[0]) % BLK)))
    v = jnp.concatenate(flat)
    v = jnp.pad(v, (0, (-v.shape[0]) % (PACK_ROWS * BLK)))
    return v.reshape(-1, BLK)


def _unpack(buf, shapes):
    flat = buf.reshape(-1)
    out, off = [], 0
    for shp in shapes:
        size = math.prod(shp)
        out.append(flat[off:off + size].reshape(shp))
        off += size + (-size) % BLK
    return out


def kernel(x, meta_tokens, norm_mix, w_in, q_norm, k_norm, attn_sinks, lam_re, lam_im, log_dt, ssm_b_re, ssm_b_im, ssm_c_re, ssm_c_im, ssm_d, w_glu, attn_branch_norm, ssm_branch_norm, w_out, norm_ffn, w_ffn_in, w_ffn_out, loss_target, m_meta_tokens, m_norm_mix, m_w_in, m_q_norm, m_k_norm, m_attn_sinks, m_lam_re, m_lam_im, m_log_dt, m_ssm_b_re, m_ssm_b_im, m_ssm_c_re, m_ssm_c_im, m_ssm_d, m_w_glu, m_attn_branch_norm, m_ssm_branch_norm, m_w_out, m_norm_ffn, m_w_ffn_in, m_w_ffn_out, v_meta_tokens, v_norm_mix, v_w_in, v_q_norm, v_k_norm, v_attn_sinks, v_lam_re, v_lam_im, v_log_dt, v_ssm_b_re, v_ssm_b_im, v_ssm_c_re, v_ssm_c_im, v_ssm_d, v_w_glu, v_attn_branch_norm, v_ssm_branch_norm, v_w_out, v_norm_ffn, v_w_ffn_in, v_w_ffn_out):
    args = dict(locals())
    weights = {n: args[n] for n in WEIGHTS}
    mom_m = {n: args["m_" + n] for n in WEIGHTS}
    mom_v = {n: args["v_" + n] for n in WEIGHTS}

    x2d = x[0]
    target2d = loss_target[0]
    s_len = x2d.shape[0]
    l_dim = s_len + BLK
    tm_row = _pick(l_dim, 320)
    tm_mm = _pick(l_dim, 1040)
    tl_tn = _pick(l_dim, 832)
    tm_ffn = _pick(l_dim, 640)
    tm_big = _pick(l_dim, 2080)

    shard_in = w_in[0].T.astype(BF16)
    shard_glu = w_glu[0].T.astype(BF16)
    shard_out = w_out[0].astype(BF16)
    shard_ffn_in = w_ffn_in[0].T.astype(BF16)
    shard_ffn_out = w_ffn_out[0].astype(BF16)
    shard_meta = meta_tokens.T
    me = _my_index()

    def landing(shard, row_fn=_plain_rows):
        m_rows, cols = shard.shape
        return lax.dynamic_update_slice(lax.empty((N_DEV * m_rows, cols), shard.dtype), shard,
                                        (row_fn(me, m_rows), 0))

    first = [shard_in, shard_meta]
    ga = _send_start("gather_start_a", first, [landing(s) for s in first], chunked=False)
    later = [shard_glu + ga[4][0:1, 0:1].astype(BF16), shard_out, shard_ffn_in, shard_ffn_out]
    later_fns = [_plain_rows, _plain_rows, _ffn_in_rows, _plain_rows]
    gb = _send_start("gather_start_b", later, [landing(s, f) for s, f in zip(later, later_fns)], chunked=False,
                     row_fns=later_fns)

    nm_t = norm_mix + (ga[4][0:1, 0:1] + gb[4][0:1, 0:1])
    qn_t, kn_t = jnp.tile(q_norm, (1, N_Q_HEADS)), jnp.tile(k_norm, (1, N_KV_HEADS))
    e_mat = jnp.kron(jnp.eye(4, dtype=F32), jnp.ones((HEAD_DIM, HEAD_DIM), F32)).astype(BF16)

    def disc(lr, li, ldt, br, bi):
        return _discretize(lr[0], li[0], ldt[0], br[0], bi[0])

    (abar_re, abar_im, bbar_re, bbar_im), disc_vjp = jax.vjp(disc, lam_re, lam_im, log_dt, ssm_b_re, ssm_b_im)
    wb_re, wb_im = _block_diag_b(bbar_re).astype(BF16), _block_diag_b(bbar_im).astype(BF16)
    wc_re, wc_im = _block_diag_c(ssm_c_re[0]).astype(BF16), _block_diag_c(ssm_c_im[0]).astype(BF16)
    q_ssm = _pick(l_dim, 640, 64)
    tabs = _power_tables(abar_re.reshape(1, N_STATE), abar_im.reshape(1, N_STATE), q_ssm // GRP)

    h0, xn = _embed_norm(x2d, nm_t)
    wt_in, meta_t = _send_wait("gather_wait_a", ga[0], ga[1], ga[2], ga[3], xn, chunked=False)
    meta_pad = jnp.pad(meta_t.T, ((PAD, 0), (0, 0)))
    h0, xn = _embed_meta(meta_pad, nm_t, h0, xn)
    qkv = _matmul("proj_qkv", xn, wt_in, nt=True, tm=tm_big, tn=512, tk=D_MODEL, n=Q_W + 2 * KV_W, w_off=0)
    u = _matmul("proj_u", xn, wt_in, nt=True, tm=tm_big, tn=512, tk=D_MODEL, n=D_MODEL, w_off=3)
    gates = _matmul("proj_gates", xn, wt_in, nt=True, tm=tm_big, tn=512, tk=D_MODEL, n=2 * D_MODEL, w_off=5,
                    out_dtype=BF16)
    qn, kf, vf = _qk_prep(qkv, qn_t, kn_t, e_mat, tm_row)
    attn = _attn_fwd(qn, kf, vf, attn_sinks)
    y, z, s_re, s_im = _ssm_fwd_kb(u, wb_re, wb_im, wc_re, wc_im, ssm_d, tabs, q_ssm)
    wt_glu, w_out_f, wt_ffn_in, w_ffn_out_f = _send_wait("gather_wait_b", gb[0], gb[1], gb[2], gb[3], z,
                                                         chunked=False, row_fns=later_fns)
    zab = _matmul("glu_proj", z, wt_glu, nt=True, tm=tm_big, tn=1024, tk=D_MODEL, out_dtype=BF16)
    merged = _merge_fwd(attn, zab, gates, attn_branch_norm, ssm_branch_norm, tm_row)
    h1, hn = _matmul("out_proj", merged, w_out_f, nt=False, tm=tm_mm, tn=1024, tk=D_MODEL, res=h0,
                     norm_g=norm_ffn)
    gu, act = _ffn_in_swiglu(hn, wt_ffn_in, tm_ffn)
    h2 = _matmul("ffn_out", act, w_ffn_out_f, nt=False, tm=tm_mm, tn=1024, tk=1408, res=h1)
    dh2, dh2_b, loss_part = _loss_grad(h2, target2d)

    dgu = _d_act_swiglu(dh2_b, w_ffn_out_f, gu, tm_ffn)

    def exchange_start(name, grads_b, row_fns=None):
        return _send_start(name, grads_b, [jnp.zeros(g.shape, BF16) for g in grads_b], chunked=True,
                           row_fns=row_fns)

    g_ffn_out, g_ffn_out_b = _matmul_tn("g_ffn_out", act, dh2_b, tm=1408, tn=1024, tl=tl_tn)
    g_ffn_in_t, g_ffn_in_b = _matmul_tn("g_ffn_in", dgu, hn, tm=1408, tn=1024, tl=tl_tn)
    ffn_fns = [_ffn_in_rows, _plain_rows]
    ex1 = exchange_start("exchange_start_ffn", [g_ffn_in_b, g_ffn_out_b], ffn_fns)
    dhn = _matmul("d_hn", dgu, wt_ffn_in, nt=False, tm=tm_mm, tn=1024, tk=1408)
    dh1, dh1_b, g_norm_ffn = _norm_bwd_res("ffn_norm_bwd", h1, norm_ffn + ex1[4][0:1, 0:1], dhn, dh2, tm_row)
    dmerged = _matmul("d_merged", dh1_b, w_out_f, nt=True, tm=tm_big, tn=1024, tk=D_MODEL)
    dattn, dzab, dgates, g_abn, g_sbn = _merge_bwd(attn, zab, gates, attn_branch_norm, ssm_branch_norm,
                                                     dmerged, tm_row)
    g_out, g_out_b = _matmul_tn("g_out", merged, dh1_b, tm=1024, tn=1024, tl=tl_tn)
    dz = _matmul("d_z", dzab, wt_glu, nt=False, tm=tm_big, tn=1024, tk=1024)
    g_glu_t, g_glu_b = _matmul_tn("g_glu", dzab, z, tm=1024, tn=1024, tl=tl_tn)
    ex2 = exchange_start("exchange_start_mix", [g_glu_b, g_out_b])
    du, g_ssm_d, g_ar, g_ai, g_wbr, g_wbi, g_wcr, g_wci = _ssm_bwd_kb(
        dz, y, u, s_re, s_im, wb_re, wb_im, wc_re, wc_im, ssm_d + ex2[4][0:1, 0:1], tabs, q_ssm)
    dq, dkc, dkp, dkm, dvc, dvp, dvm, g_sinks = _attn_bwd(qn, kf, vf, attn_sinks, attn, dattn)
    dqkv, g_qn_t, g_kn_t = _qk_bwd(qkv, qn_t, kn_t, e_mat, dq, dkc, dkp, dkm, dvc, dvp, dvm)
    g_lam_re, g_lam_im, g_log_dt, g_b_re, g_b_im = disc_vjp(
        (g_ar.reshape(SSM_GROUPS, SSM_STATE), g_ai.reshape(SSM_GROUPS, SSM_STATE),
         _block_diag_b_t(g_wbr), _block_diag_b_t(g_wbi)))
    small_grads = {
        "q_norm": g_qn_t.reshape(N_Q_HEADS, HEAD_DIM).sum(0)[None],
        "k_norm": g_kn_t.reshape(N_KV_HEADS, HEAD_DIM).sum(0)[None], "attn_sinks": g_sinks,
        "lam_re": g_lam_re, "lam_im": g_lam_im, "log_dt": g_log_dt, "ssm_b_re": g_b_re, "ssm_b_im": g_b_im,
        "ssm_c_re": _block_diag_c_t(g_wcr)[None], "ssm_c_im": _block_diag_c_t(g_wci)[None],
        "ssm_d": g_ssm_d, "attn_branch_norm": g_abn, "ssm_branch_norm": g_sbn, "norm_ffn": g_norm_ffn,
    }
    early = [n for n in SMALL if n != "norm_mix"]
    packed_e = _pack([small_grads[n] for n in early])
    gs_e = _send_start("small_start_a", [packed_e], [landing(packed_e)], chunked=False)

    dproj = jnp.concatenate([dqkv, du, dgates], axis=1)
    g_in_t, g_in_b = _matmul_tn("g_in", dproj, xn, tm=1152, tn=1024, tl=tl_tn, after=gs_e[4])
    ex3 = exchange_start("exchange_start_in", [g_in_b])
    dxn = _matmul("d_xn", dproj, wt_in, nt=False, tm=tm_mm, tn=1024, tk=1152)
    grad_x2d, dmeta_blk, g_norm_mix = _final_bwd(h0, nm_t + ex3[4][0:1, 0:1], dxn, dh1)
    packed_l = _pack([g_norm_mix, dmeta_blk[PAD:]])
    gs_l = _send_start("small_start_b", [packed_l], [landing(packed_l)], chunked=False)
    grads, deltas, new_m, new_v = {}, {}, {}, {}

    recv_ffn_in, recv_ffn_out = _send_wait("exchange_wait_ffn", ex1[0], ex1[1], ex1[2], ex1[3], gs_l[4],
                                           chunked=True, row_fns=ffn_fns)
    recv_glu, recv_out = _send_wait("exchange_wait_mix", ex2[0], ex2[1], ex2[2], ex2[3], recv_ffn_in,
                                    chunked=True)
    (recv_in,) = _send_wait("exchange_wait_in", ex3[0], ex3[1], ex3[2], ex3[3], recv_glu, chunked=True)
    big = [("w_in", g_in_t, True, recv_in, _plain_rows), ("w_glu", g_glu_t, True, recv_glu, _plain_rows),
           ("w_out", g_out, False, recv_out, _plain_rows), ("w_ffn_in", g_ffn_in_t, True, recv_ffn_in, _ffn_in_rows),
           ("w_ffn_out", g_ffn_out, False, recv_ffn_out, _plain_rows)]
    for name, g_full, transposed, recv, row_fn in big:
        m_rows = g_full.shape[0] // N_DEV
        own = lax.dynamic_slice(g_full, (row_fn(me, m_rows), 0), (m_rows, g_full.shape[1]))
        g_shard = _sum_slots("sum_" + name, recv, own)
        grads[name] = (g_shard.T if transposed else g_shard)[None]

    def adamw_2d(name):
        shp = weights[name].shape
        as2d = lambda a: a.reshape(shp[-2], shp[-1])
        d, nm, nv = _adamw("adamw_" + name, as2d(weights[name]), as2d(grads[name]), as2d(mom_m[name]),
                           as2d(mom_v[name]))
        deltas[name], new_m[name], new_v[name] = d.reshape(shp), nm.reshape(shp), nv.reshape(shp)
        return d

    for name in ["w_in", "w_glu", "w_out", "w_ffn_in", "w_ffn_out"]:
        last = adamw_2d(name)

    def small_sum(tag, gs, packed, after):
        (gathered,) = _send_wait("small_wait_" + tag, gs[0], gs[1], gs[2], gs[3], after, chunked=False)
        return _sum_gathered("sum_small_" + tag, gathered, packed.shape[0])

    def small_adamw(tag, names):
        d, nm, nv = _adamw_many("adamw_small_" + tag, [weights[n] for n in names], [grads[n] for n in names],
                                [mom_m[n] for n in names], [mom_v[n] for n in names])
        deltas.update(zip(names, d))
        new_m.update(zip(names, nm))
        new_v.update(zip(names, nv))

    g_sum_e = small_sum("a", gs_e, packed_e, last)
    grads.update(zip(early, _unpack(g_sum_e, [weights[n].shape for n in early])))
    wide = [n for n in early if n.startswith(("ssm_b", "ssm_c"))]
    small_adamw("wide", wide)
    g_sum_l = small_sum("b", gs_l, packed_l, g_sum_e)
    grads["norm_mix"], g_meta = _unpack(g_sum_l, [weights["norm_mix"].shape, (N_META, D_MODEL)])
    small_adamw("rest", [n for n in SMALL if n not in wide])
    grads["meta_tokens"] = lax.dynamic_slice(g_meta, (0, me * BLK), (N_META, BLK))
    adamw_2d("meta_tokens")

    loss = lax.psum(loss_part[0, 0], ("x", "y", "c"))
    return (loss, grad_x2d[None], *[grads[n] for n in WEIGHTS], *[deltas[n] for n in WEIGHTS],
            *[new_m[n] for n in WEIGHTS], *[new_v[n] for n in WEIGHTS])
```

```python
import math

import jax
import jax.numpy as jnp
from jax import lax
from jax.experimental import pallas as pl
from jax.experimental.pallas import tpu as pltpu

F32 = jnp.float32
BF16 = jnp.bfloat16

D_MODEL = 1024
N_META = 16
HEAD_DIM = 64
N_Q_HEADS = 16
N_KV_HEADS = 4
Q_W = N_Q_HEADS * HEAD_DIM
KV_W = N_KV_HEADS * HEAD_DIM
SSM_GROUPS = 64
SSM_GROUP_CH = 16
SSM_STATE = 64
N_STATE = SSM_GROUPS * SSM_STATE
D_FF = 2816
IN_COLS = Q_W + 2 * KV_W + 3 * D_MODEL
EPS = 1e-6
BLK = 128
PAD = BLK - N_META
N_DEV = 8
NEG = -1e30
SSM_KB = 8
ST_KB = N_STATE // SSM_KB
LB_KB = ST_KB // BLK
N_LB = N_STATE // BLK

ADAM_LR = 0.001
ADAM_B1 = 0.9
ADAM_B2 = 0.999
ADAM_EPS = 1e-08
ADAM_WD = 0.01
ADAM_STEP = 10

VMEM_LIMIT = 48 * 1024 * 1024
MESH = pl.DeviceIdType.MESH

SMALL = ["norm_mix", "q_norm", "k_norm", "attn_sinks", "lam_re", "lam_im", "log_dt", "ssm_b_re", "ssm_b_im",
         "ssm_c_re", "ssm_c_im", "ssm_d", "attn_branch_norm", "ssm_branch_norm", "norm_ffn"]
WEIGHTS = ["meta_tokens", "norm_mix", "w_in", "q_norm", "k_norm", "attn_sinks", "lam_re", "lam_im", "log_dt",
           "ssm_b_re", "ssm_b_im", "ssm_c_re", "ssm_c_im", "ssm_d", "w_glu", "attn_branch_norm",
           "ssm_branch_norm", "w_out", "norm_ffn", "w_ffn_in", "w_ffn_out"]


def _params(**kw):
    return pltpu.CompilerParams(vmem_limit_bytes=VMEM_LIMIT, **kw)


def _pick(n, cap, mult=16):
    best = None
    for d in range(mult, min(n, cap) + 1, mult):
        if n % d == 0:
            best = d
    assert best is not None, (n, cap, mult)
    return best


def _my_index():
    return 4 * lax.axis_index("x") + 2 * lax.axis_index("y") + lax.axis_index("c")


def _rms(x, g):
    r = lax.rsqrt(jnp.mean(x * x, axis=-1, keepdims=True) + EPS)
    return x * r * g


def _rms_bwd(x, g, dy):
    r = lax.rsqrt(jnp.mean(x * x, axis=-1, keepdims=True) + EPS)
    t = dy * g
    dx = r * t - x * (r * r * r) * jnp.mean(t * x, axis=-1, keepdims=True)
    dg = jnp.sum(dy * (x * r), axis=0, keepdims=True)
    return dx, dg


def _sigmoid(x):
    return 1.0 / (1.0 + jnp.exp(-x))


def _gelu(x):
    k = math.sqrt(2.0 / math.pi)
    return 0.5 * x * (1.0 + jnp.tanh(k * (x + 0.044715 * (x * x * x))))


def _gelu_grad(x):
    k = math.sqrt(2.0 / math.pi)
    t = jnp.tanh(k * (x + 0.044715 * (x * x * x)))
    return 0.5 * (1.0 + t) + 0.5 * x * (1.0 - t * t) * (k * (1.0 + 3.0 * 0.044715 * (x * x)))


def _head_mean(x, e_ref):
    hi = x.astype(BF16)
    r1 = x - hi.astype(F32)
    mid = r1.astype(BF16)
    lo = (r1 - mid.astype(F32)).astype(BF16)
    e = e_ref[...]
    out = []
    for b in range(x.shape[1] // 256):
        sl = slice(256 * b, 256 * b + 256)
        s = (jnp.dot(hi[:, sl], e, preferred_element_type=F32)
             + jnp.dot(mid[:, sl], e, preferred_element_type=F32)
             + jnp.dot(lo[:, sl], e, preferred_element_type=F32))
        out.append(s)
    s = out[0] if len(out) == 1 else jnp.concatenate(out, axis=1)
    return s * (1.0 / HEAD_DIM)


def _head_rms(x, g, e_ref):
    r = lax.rsqrt(_head_mean(x * x, e_ref) + EPS)
    return x * r * g


def _head_rms_bwd(x, g, dy, e_ref):
    r = lax.rsqrt(_head_mean(x * x, e_ref) + EPS)
    t = dy * g
    dx = r * t - x * (r * r * r) * _head_mean(t * x, e_ref)
    dg = jnp.sum(dy * (x * r), axis=0, keepdims=True)
    return dx, dg


def _lane_half(shape):
    lane = lax.broadcasted_iota(jnp.int32, shape, len(shape) - 1)
    return (lane >> 6) & 1


def _matmul(name, a, w, *, nt, tm, tn, tk, n=None, w_off=0, res=None, norm_g=None, out_dtype=F32):
    m_dim, k_dim = a.shape
    n_dim = n if n is not None else (w.shape[0] if nt else w.shape[1])
    gm, gn, gk = m_dim // tm, n_dim // tn, k_dim // tk
    assert gm * tm == m_dim and gn * tn == n_dim and gk * tk == k_dim, (name, a.shape, w.shape, tm, tn, tk)
    assert norm_g is None or tn == n_dim
    dn = (((1,), (1,)), ((), ())) if nt else (((1,), (0,)), ((), ()))

    def body(*refs):
        refs = list(refs)
        a_ref, w_ref = refs[0], refs[1]
        pos = 2
        r_ref = g_ref = on_ref = None
        if res is not None:
            r_ref, pos = refs[pos], pos + 1
        if norm_g is not None:
            g_ref, pos = refs[pos], pos + 1
        o_ref, pos = refs[pos], pos + 1
        if norm_g is not None:
            on_ref, pos = refs[pos], pos + 1
        acc = refs[pos]
        k = pl.program_id(2)

        @pl.when(k == 0)
        def _():
            acc[...] = jnp.zeros_like(acc)

        acc[...] += lax.dot_general(a_ref[...], w_ref[...], dn, preferred_element_type=F32)

        @pl.when(k == gk - 1)
        def _():
            r = acc[...]
            if r_ref is not None:
                r = r_ref[...] + r
            o_ref[...] = r.astype(out_dtype)
            if on_ref is not None:
                on_ref[...] = _rms(r, g_ref[...]).astype(BF16)

    if nt:
        w_spec = pl.BlockSpec((tn, tk), lambda i, j, k: (j + w_off, k))
    else:
        w_spec = pl.BlockSpec((tk, tn), lambda i, j, k: (k, j))
    in_specs = [pl.BlockSpec((tm, tk), lambda i, j, k: (i, k)), w_spec]
    args = [a, w]
    out_spec = pl.BlockSpec((tm, tn), lambda i, j, k: (i, j))
    out_specs, out_shape = [out_spec], [jax.ShapeDtypeStruct((m_dim, n_dim), out_dtype)]
    if res is not None:
        in_specs.append(out_spec)
        args.append(res)
    if norm_g is not None:
        in_specs.append(pl.BlockSpec((1, tn), lambda i, j, k: (0, 0)))
        args.append(norm_g)
        out_specs.append(out_spec)
        out_shape.append(jax.ShapeDtypeStruct((m_dim, n_dim), BF16))
    out = pl.pallas_call(
        body, name=name, grid=(gm, gn, gk),
        in_specs=in_specs, out_specs=out_specs, out_shape=out_shape,
        scratch_shapes=[pltpu.VMEM((tm, tn), F32)],
        compiler_params=_params(dimension_semantics=("parallel", "parallel", "arbitrary")),
    )(*args)
    return out if norm_g is not None else out[0]


def _matmul_tn(name, a, b, *, tm, tn, tl, after=None):
    l_dim, m_dim = a.shape
    n_dim = b.shape[1]
    gm, gn, gl = m_dim // tm, n_dim // tn, l_dim // tl
    assert gm * tm == m_dim and gn * tn == n_dim and gl * tl == l_dim, (name, a.shape, b.shape, tm, tn, tl)

    def body(*refs):
        a_ref, b_ref = refs[0], refs[1]
        o_ref, ob_ref = refs[-2], refs[-1]

        @pl.when(pl.program_id(2) == 0)
        def _():
            o_ref[...] = jnp.zeros_like(o_ref)

        o_ref[...] += lax.dot_general(a_ref[...], b_ref[...], (((0,), (0,)), ((), ())),
                                      preferred_element_type=F32)

        @pl.when(pl.program_id(2) == gl - 1)
        def _():
            ob_ref[...] = o_ref[...].astype(BF16)

    out_spec = pl.BlockSpec((tm, tn), lambda i, j, l: (i, j))
    in_specs = [pl.BlockSpec((tl, tm), lambda i, j, l: (l, i)), pl.BlockSpec((tl, tn), lambda i, j, l: (l, j))]
    args = [a, b]
    if after is not None:
        in_specs.append(pl.BlockSpec(memory_space=pl.ANY))
        args.append(after)
    return pl.pallas_call(
        body, name=name, grid=(gm, gn, gl),
        in_specs=in_specs,
        out_specs=[out_spec, out_spec],
        out_shape=[jax.ShapeDtypeStruct((m_dim, n_dim), F32), jax.ShapeDtypeStruct((m_dim, n_dim), BF16)],
        compiler_params=_params(dimension_semantics=("parallel", "parallel", "arbitrary")),
    )(*args)


def _row_spec(tm, cols, f=None):
    if f is None:
        return pl.BlockSpec((tm, cols), lambda i: (i, 0))
    return pl.BlockSpec((tm, cols), lambda i: (f(i), 0))


def _full_spec(shape):
    nd = len(shape)
    return pl.BlockSpec(shape, lambda i: (0,) * nd)


def _embed_norm(x2d, g):
    s_len = x2d.shape[0]
    nb = s_len // BLK + 1

    def body(x_ref, g_ref, h_ref, xn_ref):
        h_ref[...] = x_ref[...]
        xn_ref[...] = _rms(x_ref[...], g_ref[...]).astype(BF16)

    return pl.pallas_call(
        body, name="embed_norm", grid=(nb - 1,),
        in_specs=[_row_spec(BLK, D_MODEL), _full_spec((1, D_MODEL))],
        out_specs=[_row_spec(BLK, D_MODEL, lambda i: i + 1), _row_spec(BLK, D_MODEL, lambda i: i + 1)],
        out_shape=[jax.ShapeDtypeStruct((nb * BLK, D_MODEL), F32),
                   jax.ShapeDtypeStruct((nb * BLK, D_MODEL), BF16)],
        compiler_params=_params(),
    )(x2d, g)


def _embed_meta(meta_pad, g, h0, xn):
    def body(mp_ref, g_ref, h_in, xn_in, h_ref, xn_ref):
        h_ref[...] = mp_ref[...]
        xn_ref[...] = _rms(mp_ref[...], g_ref[...]).astype(BF16)

    any_spec = pl.BlockSpec(memory_space=pl.ANY)
    return pl.pallas_call(
        body, name="embed_meta", grid=(1,),
        in_specs=[_full_spec((BLK, D_MODEL)), _full_spec((1, D_MODEL)), any_spec, any_spec],
        out_specs=[_row_spec(BLK, D_MODEL), _row_spec(BLK, D_MODEL)],
        out_shape=[jax.ShapeDtypeStruct(h0.shape, F32), jax.ShapeDtypeStruct(xn.shape, BF16)],
        input_output_aliases={2: 0, 3: 1},
        compiler_params=_params(),
    )(meta_pad, g, h0, xn)


KVX_W = 2 * N_KV_HEADS * BLK


def _qk_prep(qkv, q_norm_t, k_norm_t, e_mat, tm):
    l_dim = qkv.shape[0]

    def body(x_ref, qg_ref, kg_ref, e_ref, q_ref, kf_ref, vf_ref):
        x = x_ref[...]
        q = _head_rms(x[:, :Q_W], qg_ref[...], e_ref) * (HEAD_DIM ** -0.5)
        q_ref[...] = q.astype(BF16)
        k = _head_rms(x[:, Q_W:Q_W + KV_W], kg_ref[...], e_ref)
        v = x[:, Q_W + KV_W:Q_W + 2 * KV_W]
        half = _lane_half((tm, BLK))
        for src, dst, fill in ((k, kf_ref, 0.0), (v, vf_ref, 0.0)):
            for kv in range(N_KV_HEADS):
                blk = src[:, BLK * (kv // 2):BLK * (kv // 2) + BLK]
                swapped = pltpu.roll(blk, HEAD_DIM, axis=1)
                for e in range(2):
                    val = blk if kv % 2 == e else swapped
                    idx = 2 * kv + e
                    dst[:, BLK * idx:BLK * idx + BLK] = jnp.where(half == e, val, fill).astype(BF16)

    return pl.pallas_call(
        body, name="qk_prep", grid=(l_dim // tm,),
        in_specs=[_row_spec(tm, Q_W + 2 * KV_W), _full_spec((1, Q_W)), _full_spec((1, KV_W)),
                  _full_spec((256, 256))],
        out_specs=[_row_spec(tm, Q_W), _row_spec(tm, KVX_W), _row_spec(tm, KVX_W)],
        out_shape=[jax.ShapeDtypeStruct((l_dim, Q_W), BF16),
                   jax.ShapeDtypeStruct((l_dim, KVX_W), BF16),
                   jax.ShapeDtypeStruct((l_dim, KVX_W), BF16)],
        compiler_params=_params(),
    )(qkv, q_norm_t, k_norm_t, e_mat)


def _merge_fwd(attn, zab, gates, abn, sbn, tm):
    l_dim = attn.shape[0]

    def body(a_ref, z_ref, g_ref, an_ref, sn_ref, o_ref):
        z = z_ref[...].astype(F32)
        g = g_ref[...].astype(F32)
        ssm = z[:, :D_MODEL] * _sigmoid(z[:, D_MODEL:])
        merged = (_sigmoid(g[:, :D_MODEL]) * _rms(a_ref[...], an_ref[...])
                  + _sigmoid(g[:, D_MODEL:]) * _rms(ssm, sn_ref[...]))
        o_ref[...] = merged.astype(BF16)

    return pl.pallas_call(
        body, name="merge_fwd", grid=(l_dim // tm,),
        in_specs=[_row_spec(tm, D_MODEL), _row_spec(tm, 2 * D_MODEL), _row_spec(tm, 2 * D_MODEL),
                  _full_spec((1, D_MODEL)), _full_spec((1, D_MODEL))],
        out_specs=_row_spec(tm, D_MODEL),
        out_shape=jax.ShapeDtypeStruct((l_dim, D_MODEL), BF16),
        compiler_params=_params(),
    )(attn, zab, gates, abn, sbn)


def _merge_bwd(attn, zab, gates, abn, sbn, dmerged, tm):
    l_dim = attn.shape[0]

    def body(a_ref, z_ref, g_ref, an_ref, sn_ref, dm_ref, da_ref, dz_ref, dg_ref, dan_ref, dsn_ref):
        @pl.when(pl.program_id(0) == 0)
        def _():
            dan_ref[...] = jnp.zeros_like(dan_ref)
            dsn_ref[...] = jnp.zeros_like(dsn_ref)

        z = z_ref[...].astype(F32)
        g = g_ref[...].astype(F32)
        dm = dm_ref[...]
        attn_v = a_ref[...]
        za, zb = z[:, :D_MODEL], z[:, D_MODEL:]
        sb = _sigmoid(zb)
        ssm = za * sb
        s_ga, s_gs = _sigmoid(g[:, :D_MODEL]), _sigmoid(g[:, D_MODEL:])
        a_n = _rms(attn_v, an_ref[...])
        s_n = _rms(ssm, sn_ref[...])
        dg_ref[:, :D_MODEL] = (dm * a_n * s_ga * (1.0 - s_ga)).astype(BF16)
        dg_ref[:, D_MODEL:] = (dm * s_n * s_gs * (1.0 - s_gs)).astype(BF16)
        dattn, dan = _rms_bwd(attn_v, an_ref[...], dm * s_ga)
        dssm, dsn = _rms_bwd(ssm, sn_ref[...], dm * s_gs)
        da_ref[...] = dattn.astype(BF16)
        dz_ref[:, :D_MODEL] = (dssm * sb).astype(BF16)
        dz_ref[:, D_MODEL:] = (dssm * za * sb * (1.0 - sb)).astype(BF16)
        dan_ref[...] += dan
        dsn_ref[...] += dsn

    return pl.pallas_call(
        body, name="merge_bwd", grid=(l_dim // tm,),
        in_specs=[_row_spec(tm, D_MODEL), _row_spec(tm, 2 * D_MODEL), _row_spec(tm, 2 * D_MODEL),
                  _full_spec((1, D_MODEL)), _full_spec((1, D_MODEL)), _row_spec(tm, D_MODEL)],
        out_specs=[_row_spec(tm, D_MODEL), _row_spec(tm, 2 * D_MODEL), _row_spec(tm, 2 * D_MODEL),
                   _full_spec((1, D_MODEL)), _full_spec((1, D_MODEL))],
        out_shape=[jax.ShapeDtypeStruct((l_dim, D_MODEL), BF16),
                   jax.ShapeDtypeStruct((l_dim, 2 * D_MODEL), BF16),
                   jax.ShapeDtypeStruct((l_dim, 2 * D_MODEL), BF16),
                   jax.ShapeDtypeStruct((1, D_MODEL), F32), jax.ShapeDtypeStruct((1, D_MODEL), F32)],
        compiler_params=_params(),
    )(attn, zab, gates, abn, sbn, dmerged)


FF_TILE = D_FF // 2


def _ffn_in_swiglu(hn, wt_ffn_in, tm):
    l_dim = hn.shape[0]
    nt = (((1,), (1,)), ((), ()))

    def body(a_ref, w_ref, gu_ref, act_ref):
        r = lax.dot_general(a_ref[...], w_ref[...], nt, preferred_element_type=F32)
        gate, up = r[:, :FF_TILE], r[:, FF_TILE:]
        gu_ref[...] = r.astype(BF16)
        act_ref[...] = (gate * _sigmoid(gate) * up).astype(BF16)

    return pl.pallas_call(
        body, name="ffn_in_swiglu", grid=(l_dim // tm, 2),
        in_specs=[pl.BlockSpec((tm, D_MODEL), lambda i, j: (i, 0)),
                  pl.BlockSpec((2 * FF_TILE, D_MODEL), lambda i, j: (j, 0))],
        out_specs=[pl.BlockSpec((tm, 2 * FF_TILE), lambda i, j: (i, j)),
                   pl.BlockSpec((tm, FF_TILE), lambda i, j: (i, j))],
        out_shape=[jax.ShapeDtypeStruct((l_dim, 2 * D_FF), BF16), jax.ShapeDtypeStruct((l_dim, D_FF), BF16)],
        compiler_params=_params(dimension_semantics=("parallel", "parallel")),
    )(hn, wt_ffn_in)


def _d_act_swiglu(dh2_b, w_ffn_out, gu, tm):
    l_dim = dh2_b.shape[0]
    nt = (((1,), (1,)), ((), ()))

    def body(d_ref, w_ref, gu_ref, o_ref):
        d = lax.dot_general(d_ref[...], w_ref[...], nt, preferred_element_type=F32)
        gate = gu_ref[:, :FF_TILE].astype(F32)
        up = gu_ref[:, FF_TILE:].astype(F32)
        s = _sigmoid(gate)
        o_ref[:, :FF_TILE] = (d * up * (s * (1.0 + gate * (1.0 - s)))).astype(BF16)
        o_ref[:, FF_TILE:] = (d * (gate * s)).astype(BF16)

    return pl.pallas_call(
        body, name="d_act_swiglu", grid=(l_dim // tm, 2),
        in_specs=[pl.BlockSpec((tm, D_MODEL), lambda i, j: (i, 0)),
                  pl.BlockSpec((FF_TILE, D_MODEL), lambda i, j: (j, 0)),
                  pl.BlockSpec((tm, 2 * FF_TILE), lambda i, j: (i, j))],
        out_specs=pl.BlockSpec((tm, 2 * FF_TILE), lambda i, j: (i, j)),
        out_shape=jax.ShapeDtypeStruct((l_dim, 2 * D_FF), BF16),
        compiler_params=_params(dimension_semantics=("parallel", "parallel")),
    )(dh2_b, w_ffn_out, gu)


def _loss_grad(h2, target2d):
    l_dim = h2.shape[0]
    nb = l_dim // BLK

    def body(h_ref, t_ref, d_ref, db_ref, loss_ref):
        i = pl.program_id(0)

        @pl.when(i == 0)
        def _():
            loss_ref[...] = jnp.zeros_like(loss_ref)
            d_ref[...] = jnp.zeros_like(d_ref)
            db_ref[...] = jnp.zeros_like(db_ref)

        @pl.when(i > 0)
        def _():
            err = h_ref[...] - t_ref[...]
            d = err * (1.0 / D_MODEL)
            d_ref[...] = d
            db_ref[...] = d.astype(BF16)
            loss_ref[...] += 0.5 * jnp.sum(jnp.mean(err * err, axis=-1, keepdims=True), axis=0, keepdims=True)

    return pl.pallas_call(
        body, name="loss_grad", grid=(nb,),
        in_specs=[_row_spec(BLK, D_MODEL), _row_spec(BLK, D_MODEL, lambda i: jnp.maximum(i - 1, 0))],
        out_specs=[_row_spec(BLK, D_MODEL), _row_spec(BLK, D_MODEL), _full_spec((1, 1))],
        out_shape=[jax.ShapeDtypeStruct((l_dim, D_MODEL), F32), jax.ShapeDtypeStruct((l_dim, D_MODEL), BF16),
                   jax.ShapeDtypeStruct((1, 1), F32)],
        compiler_params=_params(),
    )(h2, target2d)


def _norm_bwd_res(name, h, g, dy, dres, tm):
    l_dim = h.shape[0]

    def body(h_ref, g_ref, dy_ref, dr_ref, o_ref, ob_ref, dg_ref):
        @pl.when(pl.program_id(0) == 0)
        def _():
            dg_ref[...] = jnp.zeros_like(dg_ref)

        dx, dg = _rms_bwd(h_ref[...], g_ref[...], dy_ref[...])
        out = dr_ref[...] + dx
        o_ref[...] = out
        ob_ref[...] = out.astype(BF16)
        dg_ref[...] += dg

    return pl.pallas_call(
        body, name=name, grid=(l_dim // tm,),
        in_specs=[_row_spec(tm, D_MODEL), _full_spec((1, D_MODEL)), _row_spec(tm, D_MODEL), _row_spec(tm, D_MODEL)],
        out_specs=[_row_spec(tm, D_MODEL), _row_spec(tm, D_MODEL), _full_spec((1, D_MODEL))],
        out_shape=[jax.ShapeDtypeStruct((l_dim, D_MODEL), F32), jax.ShapeDtypeStruct((l_dim, D_MODEL), BF16),
                   jax.ShapeDtypeStruct((1, D_MODEL), F32)],
        compiler_params=_params(),
    )(h, g, dy, dres)


def _final_bwd(h0, g, dxn, dh1):
    l_dim = h0.shape[0]
    nb = l_dim // BLK

    def body(h_ref, g_ref, dy_ref, dr_ref, gx_ref, gm_ref, dg_ref):
        i = pl.program_id(0)

        @pl.when(i == 0)
        def _():
            dg_ref[...] = jnp.zeros_like(dg_ref)

        dx, dg = _rms_bwd(h_ref[...], g_ref[...], dy_ref[...])
        out = dr_ref[...] + dx
        dg_ref[...] += dg

        @pl.when(i == 0)
        def _():
            gm_ref[...] = out

        @pl.when(i > 0)
        def _():
            gx_ref[...] = out

    return pl.pallas_call(
        body, name="final_bwd", grid=(nb,),
        in_specs=[_row_spec(BLK, D_MODEL), _full_spec((1, D_MODEL)), _row_spec(BLK, D_MODEL),
                  _row_spec(BLK, D_MODEL)],
        out_specs=[_row_spec(BLK, D_MODEL, lambda i: jnp.maximum(i - 1, 0)), _full_spec((BLK, D_MODEL)),
                   _full_spec((1, D_MODEL))],
        out_shape=[jax.ShapeDtypeStruct((l_dim - BLK, D_MODEL), F32), jax.ShapeDtypeStruct((BLK, D_MODEL), F32),
                   jax.ShapeDtypeStruct((1, D_MODEL), F32)],
        compiler_params=_params(),
    )(h0, g, dxn, dh1)


def _attn_valid(n):
    shape = (2 * BLK, 3 * BLK)
    qi = lax.broadcasted_iota(jnp.int32, shape, 0) & (BLK - 1)
    col = lax.broadcasted_iota(jnp.int32, shape, 1)
    kj = col & (BLK - 1)
    part = col >> 7
    nn = jnp.zeros(shape, jnp.int32) + n
    meta_ok = (part == 0) & (kj >= PAD) & (nn >= 1)
    prev_ok = (part == 1) & (kj > qi) & (nn >= 2)
    cur_ok = (part == 2) & (kj <= qi) & ((nn >= 1) | (kj >= PAD))
    return meta_ok | prev_ok | cur_ok


def _attn_scores(q_ref, kwin, sk_ref, valid, kv, e):
    qs = jnp.concatenate([q_ref[:, BLK * (2 * kv):BLK * (2 * kv) + BLK],
                          q_ref[:, BLK * (2 * kv + 1):BLK * (2 * kv + 1) + BLK]], axis=0)
    s = lax.dot_general(qs, kwin, (((1,), (1,)), ((), ())), preferred_element_type=F32)
    h0 = 4 * kv + e
    row = lax.broadcasted_iota(jnp.int32, (2 * BLK, 1), 0)
    sink = jnp.where(row < BLK, sk_ref[:, h0:h0 + 1], sk_ref[:, h0 + 2:h0 + 3])
    return qs, jnp.where(valid, s, NEG), sink


def _attn_specs(nb):
    prev = lambda i: jnp.maximum(i - 1, 0)
    zero = lambda i: 0
    kv_specs = [_row_spec(BLK, KVX_W, zero), _row_spec(BLK, KVX_W, prev), _row_spec(BLK, KVX_W)]
    return kv_specs


def _attn_fwd(qn, kf, vf, sinks):
    l_dim = qn.shape[0]
    nb = l_dim // BLK

    def body(q_ref, km_ref, kp_ref, kc_ref, vm_ref, vp_ref, vc_ref, sk_ref, o_ref, lse_ref):
        valid = _attn_valid(pl.program_id(0))
        lane = lax.broadcasted_iota(jnp.int32, (BLK, BLK), 1)
        lse_all = jnp.zeros((BLK, BLK), F32)
        for kv in range(N_KV_HEADS):
            outs = []
            for e in range(2):
                sl = slice(BLK * (2 * kv + e), BLK * (2 * kv + e) + BLK)
                kwin = jnp.concatenate([km_ref[:, sl], kp_ref[:, sl], kc_ref[:, sl]], axis=0)
                vwin = jnp.concatenate([vm_ref[:, sl], vp_ref[:, sl], vc_ref[:, sl]], axis=0)
                _, s, sink = _attn_scores(q_ref, kwin, sk_ref, valid, kv, e)
                m = jnp.maximum(jnp.max(s, axis=-1, keepdims=True), sink)
                ex = jnp.exp(s - m)
                den = jnp.sum(ex, axis=-1, keepdims=True) + jnp.exp(sink - m)
                p = ex * (1.0 / den)
                outs.append(jnp.dot(p.astype(BF16), vwin, preferred_element_type=F32))
                lse = m + jnp.log(den)
                lse_all = jnp.where(lane == 4 * kv + e, lse[:BLK], lse_all)
                lse_all = jnp.where(lane == 4 * kv + 2 + e, lse[BLK:], lse_all)
            o = outs[0] + outs[1]
            o_ref[:, BLK * (2 * kv):BLK * (2 * kv) + BLK] = o[:BLK]
            o_ref[:, BLK * (2 * kv + 1):BLK * (2 * kv + 1) + BLK] = o[BLK:]
        lse_ref[...] = lse_all

    kv_specs = _attn_specs(nb)
    return pl.pallas_call(
        body, name="attn_fwd", grid=(nb,),
        in_specs=[_row_spec(BLK, Q_W)] + kv_specs + kv_specs + [_full_spec((1, N_Q_HEADS))],
        out_specs=[_row_spec(BLK, Q_W), _row_spec(BLK, BLK)],
        out_shape=[jax.ShapeDtypeStruct((l_dim, Q_W), F32), jax.ShapeDtypeStruct((l_dim, BLK), F32)],
        compiler_params=_params(),
    )(qn, kf, kf, kf, vf, vf, vf, sinks)


def _attn_bwd(qn, kf, vf, sinks, lse, attn, dattn):
    l_dim = qn.shape[0]
    nb = l_dim // BLK
    wide = KVX_W
    tn = (((0,), (0,)), ((), ()))
    nt = (((1,), (1,)), ((), ()))

    def body(q_ref, km_ref, kp_ref, kc_ref, vm_ref, vp_ref, vc_ref, sk_ref, lse_ref, o_ref, do_ref,
             dq_ref, dkc_ref, dkp_ref, dkm_ref, dvc_ref, dvp_ref, dvm_ref, dsk_ref):
        @pl.when(pl.program_id(0) == 0)
        def _():
            dkm_ref[...] = jnp.zeros_like(dkm_ref)
            dvm_ref[...] = jnp.zeros_like(dvm_ref)
            dsk_ref[...] = jnp.zeros_like(dsk_ref)

        valid = _attn_valid(pl.program_id(0))
        half = _lane_half((BLK, BLK))
        half2 = _lane_half((2 * BLK, BLK))
        lane16 = lax.broadcasted_iota(jnp.int32, (1, N_Q_HEADS), 1)
        dsk = jnp.zeros((1, N_Q_HEADS), F32)
        for kv in range(N_KV_HEADS):
            j0, j1 = 2 * kv, 2 * kv + 1
            do0 = do_ref[:, BLK * j0:BLK * j0 + BLK]
            do1 = do_ref[:, BLK * j1:BLK * j1 + BLK]
            do0f, do1f = do0.astype(F32), do1.astype(F32)
            prod0 = do0f * o_ref[:, BLK * j0:BLK * j0 + BLK]
            prod1 = do1f * o_ref[:, BLK * j1:BLK * j1 + BLK]
            dos = jnp.concatenate([do0, do1], axis=0)
            dqs = []
            for e in range(2):
                sl = slice(BLK * (2 * kv + e), BLK * (2 * kv + e) + BLK)
                kwin = jnp.concatenate([km_ref[:, sl], kp_ref[:, sl], kc_ref[:, sl]], axis=0)
                vwin = jnp.concatenate([vm_ref[:, sl], vp_ref[:, sl], vc_ref[:, sl]], axis=0)
                qs, s, sink = _attn_scores(q_ref, kwin, sk_ref, valid, kv, e)
                h0 = 4 * kv + e
                lse_rows = jnp.concatenate([lse_ref[:, h0:h0 + 1], lse_ref[:, h0 + 2:h0 + 3]], axis=0)
                p = jnp.exp(s - lse_rows)
                p_sink = jnp.exp(sink - lse_rows)
                delta = jnp.concatenate(
                    [jnp.sum(jnp.where(half == e, prod0, 0.0), axis=-1, keepdims=True),
                     jnp.sum(jnp.where(half == e, prod1, 0.0), axis=-1, keepdims=True)], axis=0)
                dp = lax.dot_general(dos, vwin, nt, preferred_element_type=F32)
                ds = (p * (dp - delta)).astype(BF16)
                pb = p.astype(BF16)
                dqs.append(jnp.dot(ds, kwin, preferred_element_type=F32))
                dk = lax.dot_general(ds, qs, tn, preferred_element_type=F32)
                dv = lax.dot_general(pb, dos, tn, preferred_element_type=F32)
                dkm_ref[:, sl] += dk[:BLK]
                dkp_ref[:, sl] = dk[BLK:2 * BLK].astype(BF16)
                dkc_ref[:, sl] = dk[2 * BLK:].astype(BF16)
                dvm_ref[:, sl] += dv[:BLK]
                dvp_ref[:, sl] = dv[BLK:2 * BLK].astype(BF16)
                dvc_ref[:, sl] = dv[2 * BLK:].astype(BF16)
                sink_g = -(p_sink * delta)
                g_lo = jnp.sum(sink_g[:BLK], axis=0, keepdims=True)
                g_hi = jnp.sum(sink_g[BLK:], axis=0, keepdims=True)
                dsk = dsk + jnp.where(lane16 == h0, g_lo, 0.0) + jnp.where(lane16 == h0 + 2, g_hi, 0.0)
            dq = jnp.where(half2 == 0, dqs[0], dqs[1])
            dq_ref[:, BLK * j0:BLK * j0 + BLK] = dq[:BLK].astype(BF16)
            dq_ref[:, BLK * j1:BLK * j1 + BLK] = dq[BLK:].astype(BF16)
        dsk_ref[...] += dsk

    kv_specs = _attn_specs(nb)
    row_wide = _row_spec(BLK, wide)
    acc_wide = _full_spec((BLK, wide))
    big = jax.ShapeDtypeStruct((l_dim, wide), BF16)
    return pl.pallas_call(
        body, name="attn_bwd", grid=(nb,),
        in_specs=[_row_spec(BLK, Q_W)] + kv_specs + kv_specs
        + [_full_spec((1, N_Q_HEADS)), _row_spec(BLK, BLK), _row_spec(BLK, Q_W), _row_spec(BLK, Q_W)],
        out_specs=[_row_spec(BLK, Q_W), row_wide, row_wide, acc_wide, row_wide, row_wide, acc_wide,
                   _full_spec((1, N_Q_HEADS))],
        out_shape=[jax.ShapeDtypeStruct((l_dim, Q_W), BF16), big, big, jax.ShapeDtypeStruct((BLK, wide), F32),
                   big, big, jax.ShapeDtypeStruct((BLK, wide), F32), jax.ShapeDtypeStruct((1, N_Q_HEADS), F32)],
        compiler_params=_params(),
    )(qn, kf, kf, kf, vf, vf, vf, sinks, lse, attn, dattn)


def _qk_bwd(qkv, q_norm_t, k_norm_t, e_mat, dq, dkc, dkp, dkm, dvc, dvp, dvm):
    l_dim = qkv.shape[0]
    nb = l_dim // BLK
    wide = KVX_W

    def fold(x):
        half = _lane_half((BLK, BLK))
        blocks = []
        for kb in range(2):
            t = []
            for kv in (2 * kb, 2 * kb + 1):
                own = kv % 2
                a = x[:, BLK * (2 * kv + own):BLK * (2 * kv + own) + BLK]
                b = pltpu.roll(x[:, BLK * (2 * kv + 1 - own):BLK * (2 * kv + 1 - own) + BLK], HEAD_DIM, axis=1)
                t.append(a + b)
            blocks.append(jnp.where(half == 0, t[0], t[1]))
        return jnp.concatenate(blocks, axis=1)

    def body(x_ref, qg_ref, kg_ref, e_ref, dq_ref, dkc_ref, dkp_ref, dkm_ref, dvc_ref, dvp_ref, dvm_ref,
             o_ref, dqg_ref, dkg_ref):
        i = pl.program_id(0)

        @pl.when(i == 0)
        def _():
            dqg_ref[...] = jnp.zeros_like(dqg_ref)
            dkg_ref[...] = jnp.zeros_like(dkg_ref)

        first = jnp.where(i == 0, 1.0, 0.0)
        not_last = jnp.where(i < nb - 1, 1.0, 0.0)
        dk_x = dkc_ref[...].astype(F32) + not_last * dkp_ref[...].astype(F32) + first * dkm_ref[...]
        dv_x = dvc_ref[...].astype(F32) + not_last * dvp_ref[...].astype(F32) + first * dvm_ref[...]
        x = x_ref[...]
        dqx, dqg = _head_rms_bwd(x[:, :Q_W], qg_ref[...], dq_ref[...].astype(F32) * (HEAD_DIM ** -0.5), e_ref)
        dkx, dkg = _head_rms_bwd(x[:, Q_W:Q_W + KV_W], kg_ref[...], fold(dk_x), e_ref)
        o_ref[:, :Q_W] = dqx.astype(BF16)
        o_ref[:, Q_W:Q_W + KV_W] = dkx.astype(BF16)
        o_ref[:, Q_W + KV_W:] = fold(dv_x).astype(BF16)
        dqg_ref[...] += dqg
        dkg_ref[...] += dkg

    nxt = lambda i: jnp.minimum(i + 1, nb - 1)
    row_wide = _row_spec(BLK, wide)
    nxt_wide = _row_spec(BLK, wide, nxt)
    acc_wide = _full_spec((BLK, wide))
    return pl.pallas_call(
        body, name="qk_bwd", grid=(nb,),
        in_specs=[_row_spec(BLK, Q_W + 2 * KV_W), _full_spec((1, Q_W)), _full_spec((1, KV_W)),
                  _full_spec((256, 256)), _row_spec(BLK, Q_W),
                  row_wide, nxt_wide, acc_wide, row_wide, nxt_wide, acc_wide],
        out_specs=[_row_spec(BLK, Q_W + 2 * KV_W), _full_spec((1, Q_W)), _full_spec((1, KV_W))],
        out_shape=[jax.ShapeDtypeStruct((l_dim, Q_W + 2 * KV_W), BF16),
                   jax.ShapeDtypeStruct((1, Q_W), F32), jax.ShapeDtypeStruct((1, KV_W), F32)],
        compiler_params=_params(),
    )(qkv, q_norm_t, k_norm_t, e_mat, dq, dkc, dkp, dkm, dvc, dvp, dvm)


GRP = 8


def _strided(r, g):
    return pl.ds(r, g, stride=GRP)


def _slab(ref, base, r, g):
    return jnp.concatenate([ref[base + i, _strided(r, g), :] for i in range(LB_KB)], axis=1)


def _slab_store(ref, base, r, g, val):
    for i in range(LB_KB):
        ref[base + i, _strided(r, g), :] = val[:, BLK * i:BLK * i + BLK]


def _group_totals(xr_ref, xi_ref, base, ar, ai, reverse):
    g = xr_ref.shape[1] // GRP
    sr = si = None
    for r in (range(GRP - 1, -1, -1) if reverse else range(GRP)):
        xr, xi = _slab(xr_ref, base, r, g), _slab(xi_ref, base, r, g)
        if sr is not None:
            xr, xi = xr + ar * sr - ai * si, xi + ar * si + ai * sr
        sr, si = xr, xi
    return sr, si


def _carry_scan(tr, ti, sqr_ref, sqi_ref, lanes, sign, reverse):
    g = tr.shape[0]
    row = lax.broadcasted_iota(jnp.int32, tr.shape, 0)
    idx, s = 0, 1
    while s < g:
        ar = sqr_ref[idx:idx + 1, lanes]
        ai = sign * sqi_ref[idx:idx + 1, lanes]
        shift, keep = (g - s, row < g - s) if reverse else (s, row >= s)
        pr = jnp.where(keep, pltpu.roll(tr, shift, axis=0), 0.0)
        pi = jnp.where(keep, pltpu.roll(ti, shift, axis=0), 0.0)
        tr, ti = tr + ar * pr - ai * pi, ti + ar * pi + ai * pr
        idx, s = idx + 1, 2 * s
    return tr, ti


def _ssm_fwd(u, wb_re, wb_im, wc_re, wc_im, d_skip, tabs):
    l_dim = u.shape[0]
    nb = l_dim // BLK
    g = BLK // GRP

    def body(u_ref, wbr_ref, wbi_ref, wcr_ref, wci_ref, d_ref, a1r_ref, a1i_ref, sqr_ref, sqi_ref,
             seqr_ref, seqi_ref, y_ref, z_ref, sr_ref, si_ref, cr_ref, ci_ref, xr_ref, xi_ref):
        @pl.when(pl.program_id(0) == 0)
        def _():
            cr_ref[...] = jnp.zeros_like(cr_ref)
            ci_ref[...] = jnp.zeros_like(ci_ref)

        row = lax.broadcasted_iota(jnp.int32, (g, ST_KB), 0)
        for kb in range(SSM_KB):
            ch = slice(BLK * kb, BLK * kb + BLK)
            lanes = slice(ST_KB * kb, ST_KB * kb + ST_KB)
            u_kb = u_ref[:, ch]
            ub = u_kb.astype(BF16)
            xr = jnp.dot(ub, wbr_ref[kb], preferred_element_type=F32)
            xi = jnp.dot(ub, wbi_ref[kb], preferred_element_type=F32)
            base = LB_KB * kb
            for i in range(LB_KB):
                xr_ref[base + i] = xr[:, BLK * i:BLK * i + BLK]
                xi_ref[base + i] = xi[:, BLK * i:BLK * i + BLK]
            ar, ai = a1r_ref[0:1, lanes], a1i_ref[0:1, lanes]
            tr, ti = _group_totals(xr_ref, xi_ref, base, ar, ai, reverse=False)
            tr, ti = _carry_scan(tr, ti, sqr_ref, sqi_ref, lanes, 1.0, reverse=False)
            cin_r, cin_i = cr_ref[0:1, lanes], ci_ref[0:1, lanes]
            qr, qi = seqr_ref[:, lanes], seqi_ref[:, lanes]
            tr, ti = tr + qr * cin_r - qi * cin_i, ti + qr * cin_i + qi * cin_r
            cr_ref[0:1, lanes] = jnp.sum(jnp.where(row == g - 1, tr, 0.0), axis=0, keepdims=True)
            ci_ref[0:1, lanes] = jnp.sum(jnp.where(row == g - 1, ti, 0.0), axis=0, keepdims=True)
            pr = jnp.where(row == 0, cin_r, pltpu.roll(tr, 1, axis=0))
            pi = jnp.where(row == 0, cin_i, pltpu.roll(ti, 1, axis=0))
            for r in range(GRP):
                pr, pi = (_slab(xr_ref, base, r, g) + ar * pr - ai * pi,
                          _slab(xi_ref, base, r, g) + ar * pi + ai * pr)
                _slab_store(sr_ref, base, r, g, pr)
                _slab_store(si_ref, base, r, g, pi)
            s_r = jnp.concatenate([sr_ref[LB_KB * kb + i] for i in range(LB_KB)], axis=1)
            s_i = jnp.concatenate([si_ref[LB_KB * kb + i] for i in range(LB_KB)], axis=1)
            y = (jnp.dot(s_r.astype(BF16), wcr_ref[kb], preferred_element_type=F32)
                 - jnp.dot(s_i.astype(BF16), wci_ref[kb], preferred_element_type=F32)
                 + d_ref[:, ch] * u_kb)
            y_ref[:, ch] = y
            z_ref[:, ch] = _gelu(y).astype(BF16)

    wb_spec = _full_spec((SSM_KB, BLK, ST_KB))
    wc_spec = _full_spec((SSM_KB, ST_KB, BLK))
    tab_specs = [_full_spec(t.shape) for t in tabs[:6]]
    state_spec = pl.BlockSpec((N_LB, BLK, BLK), lambda i: (0, i, 0))
    state_shape = jax.ShapeDtypeStruct((N_LB, l_dim, BLK), F32)
    return pl.pallas_call(
        body, name="ssm_fwd", grid=(nb,),
        in_specs=[_row_spec(BLK, D_MODEL), wb_spec, wb_spec, wc_spec, wc_spec, _full_spec((1, D_MODEL))]
        + tab_specs,
        out_specs=[_row_spec(BLK, D_MODEL), _row_spec(BLK, D_MODEL), state_spec, state_spec],
        out_shape=[jax.ShapeDtypeStruct((l_dim, D_MODEL), F32), jax.ShapeDtypeStruct((l_dim, D_MODEL), BF16),
                   state_shape, state_shape],
        scratch_shapes=[pltpu.VMEM((8, N_STATE), F32), pltpu.VMEM((8, N_STATE), F32),
                        pltpu.VMEM((N_LB, BLK, BLK), F32), pltpu.VMEM((N_LB, BLK, BLK), F32)],
        compiler_params=_params(),
    )(u, wb_re, wb_im, wc_re, wc_im, d_skip, *tabs[:6])


def _ssm_bwd(dz, y, u, s_re, s_im, wb_re, wb_im, wc_re, wc_im, d_skip, tabs):
    l_dim = u.shape[0]
    nb = l_dim // BLK
    g = BLK // GRP

    def body(dz_ref, y_ref, u_ref, sr_ref, si_ref, wbr_ref, wbi_ref, wcr_ref, wci_ref, d_ref,
             a1r_ref, a1i_ref, sqr_ref, sqi_ref, revr_ref, revi_ref,
             du_ref, dd_ref, dar_ref, dai_ref, dwbr_ref, dwbi_ref, dwcr_ref, dwci_ref,
             cr_ref, ci_ref, gr_ref, gi_ref):
        @pl.when(pl.program_id(0) == 0)
        def _():
            for r in (cr_ref, ci_ref, dd_ref, dar_ref, dai_ref, dwbr_ref, dwbi_ref, dwcr_ref, dwci_ref):
                r[...] = jnp.zeros_like(r)

        tn = (((0,), (0,)), ((), ()))
        nt = (((1,), (1,)), ((), ()))
        row = lax.broadcasted_iota(jnp.int32, (g, ST_KB), 0)
        for kb in range(SSM_KB):
            ch = slice(BLK * kb, BLK * kb + BLK)
            lanes = slice(ST_KB * kb, ST_KB * kb + ST_KB)
            u_kb = u_ref[:, ch]
            dy = dz_ref[:, ch] * _gelu_grad(y_ref[:, ch])
            dyb = dy.astype(BF16)
            ub = u_kb.astype(BF16)
            dd_ref[:, ch] += jnp.sum(dy * u_kb, axis=0, keepdims=True)
            ds_r = lax.dot_general(dyb, wcr_ref[kb], nt, preferred_element_type=F32)
            ds_i = -lax.dot_general(dyb, wci_ref[kb], nt, preferred_element_type=F32)
            base = LB_KB * kb
            for i in range(LB_KB):
                gr_ref[base + i] = ds_r[:, BLK * i:BLK * i + BLK]
                gi_ref[base + i] = ds_i[:, BLK * i:BLK * i + BLK]
            ar, ai = a1r_ref[0:1, lanes], -a1i_ref[0:1, lanes]
            tr, ti = _group_totals(gr_ref, gi_ref, base, ar, ai, reverse=True)
            tr, ti = _carry_scan(tr, ti, sqr_ref, sqi_ref, lanes, -1.0, reverse=True)
            cin_r, cin_i = cr_ref[0:1, lanes], ci_ref[0:1, lanes]
            qr, qi = revr_ref[:, lanes], -revi_ref[:, lanes]
            tr, ti = tr + qr * cin_r - qi * cin_i, ti + qr * cin_i + qi * cin_r
            cr_ref[0:1, lanes] = jnp.sum(jnp.where(row == 0, tr, 0.0), axis=0, keepdims=True)
            ci_ref[0:1, lanes] = jnp.sum(jnp.where(row == 0, ti, 0.0), axis=0, keepdims=True)
            nr = jnp.where(row == g - 1, cin_r, pltpu.roll(tr, g - 1, axis=0))
            ni = jnp.where(row == g - 1, cin_i, pltpu.roll(ti, g - 1, axis=0))
            acc_r = jnp.zeros((g, ST_KB), F32)
            acc_i = jnp.zeros((g, ST_KB), F32)
            for r in range(GRP - 1, -1, -1):
                s_r, s_i = _slab(sr_ref, base, r, g), _slab(si_ref, base, r, g)
                acc_r = acc_r + (nr * s_r + ni * s_i)
                acc_i = acc_i + (ni * s_r - nr * s_i)
                nr, ni = (_slab(gr_ref, base, r, g) + ar * nr - ai * ni,
                          _slab(gi_ref, base, r, g) + ar * ni + ai * nr)
                _slab_store(gr_ref, base, r, g, nr)
                _slab_store(gi_ref, base, r, g, ni)
            dar_ref[:, lanes] += jnp.sum(acc_r, axis=0, keepdims=True)
            dai_ref[:, lanes] += jnp.sum(acc_i, axis=0, keepdims=True)
            grb = jnp.concatenate([gr_ref[base + i] for i in range(LB_KB)], axis=1).astype(BF16)
            gib = jnp.concatenate([gi_ref[base + i] for i in range(LB_KB)], axis=1).astype(BF16)
            srb = jnp.concatenate([sr_ref[base + i] for i in range(LB_KB)], axis=1).astype(BF16)
            sib = jnp.concatenate([si_ref[base + i] for i in range(LB_KB)], axis=1).astype(BF16)
            du = (lax.dot_general(grb, wbr_ref[kb], nt, preferred_element_type=F32)
                  + lax.dot_general(gib, wbi_ref[kb], nt, preferred_element_type=F32)
                  + d_ref[:, ch] * dy)
            du_ref[:, ch] = du.astype(BF16)
            dwbr_ref[kb] += lax.dot_general(ub, grb, tn, preferred_element_type=F32)
            dwbi_ref[kb] += lax.dot_general(ub, gib, tn, preferred_element_type=F32)
            dwcr_ref[kb] += lax.dot_general(srb, dyb, tn, preferred_element_type=F32)
            dwci_ref[kb] -= lax.dot_general(sib, dyb, tn, preferred_element_type=F32)

    rev = lambda i: nb - 1 - i
    wb_spec = _full_spec((SSM_KB, BLK, ST_KB))
    wc_spec = _full_spec((SSM_KB, ST_KB, BLK))
    tab_in = [tabs[0], tabs[1], tabs[2], tabs[3], tabs[6], tabs[7]]
    tab_specs = [_full_spec(t.shape) for t in tab_in]
    vec = _full_spec((1, D_MODEL))
    svec = _full_spec((1, N_STATE))
    state_spec = pl.BlockSpec((N_LB, BLK, BLK), lambda i: (0, nb - 1 - i, 0))
    return pl.pallas_call(
        body, name="ssm_bwd", grid=(nb,),
        in_specs=[_row_spec(BLK, D_MODEL, rev), _row_spec(BLK, D_MODEL, rev), _row_spec(BLK, D_MODEL, rev),
                  state_spec, state_spec,
                  wb_spec, wb_spec, wc_spec, wc_spec, vec] + tab_specs,
        out_specs=[_row_spec(BLK, D_MODEL, rev), vec, svec, svec, wb_spec, wb_spec, wc_spec, wc_spec],
        out_shape=[jax.ShapeDtypeStruct((l_dim, D_MODEL), BF16), jax.ShapeDtypeStruct((1, D_MODEL), F32),
                   jax.ShapeDtypeStruct((1, N_STATE), F32), jax.ShapeDtypeStruct((1, N_STATE), F32),
                   jax.ShapeDtypeStruct((SSM_KB, BLK, ST_KB), F32), jax.ShapeDtypeStruct((SSM_KB, BLK, ST_KB), F32),
                   jax.ShapeDtypeStruct((SSM_KB, ST_KB, BLK), F32), jax.ShapeDtypeStruct((SSM_KB, ST_KB, BLK), F32)],
        scratch_shapes=[pltpu.VMEM((8, N_STATE), F32), pltpu.VMEM((8, N_STATE), F32),
                        pltpu.VMEM((N_LB, BLK, BLK), F32), pltpu.VMEM((N_LB, BLK, BLK), F32)],
        compiler_params=_params(),
    )(dz, y, u, s_re, s_im, wb_re, wb_im, wc_re, wc_im, d_skip, *tab_in)


def _lane_slab(ref, i, r, g):
    return ref[i, _strided(r, g), :]


def _chunk_carries(xr_ref, xi_ref, i, ar, ai, sqr_ref, sqi_ref, seq_r, seq_i, cin_r, cin_i, sign, reverse):
    g = xr_ref.shape[1] // GRP
    sr = si = None
    for r in (range(GRP - 1, -1, -1) if reverse else range(GRP)):
        xr, xi = _lane_slab(xr_ref, i, r, g), _lane_slab(xi_ref, i, r, g)
        if sr is not None:
            xr, xi = xr + ar * sr - ai * si, xi + ar * si + ai * sr
        sr, si = xr, xi
    row = lax.broadcasted_iota(jnp.int32, sr.shape, 0)
    idx, s = 0, 1
    while s < g:
        br = sqr_ref[idx:idx + 1, BLK * i:BLK * i + BLK]
        bi = sign * sqi_ref[idx:idx + 1, BLK * i:BLK * i + BLK]
        shift, keep = (g - s, row < g - s) if reverse else (s, row >= s)
        pr = jnp.where(keep, pltpu.roll(sr, shift, axis=0), 0.0)
        pi = jnp.where(keep, pltpu.roll(si, shift, axis=0), 0.0)
        sr, si = sr + br * pr - bi * pi, si + br * pi + bi * pr
        idx, s = idx + 1, 2 * s
    return sr + seq_r * cin_r - seq_i * cin_i, si + seq_r * cin_i + seq_i * cin_r


def _ssm_fwd_kb(u, wb_re, wb_im, wc_re, wc_im, d_skip, tabs, q):
    l_dim = u.shape[0]
    nc = l_dim // q
    g = q // GRP

    def body(u_ref, wbr_ref, wbi_ref, wcr_ref, wci_ref, d_ref, a1r_ref, a1i_ref, sqr_ref, sqi_ref,
             seqr_ref, seqi_ref, y_ref, z_ref, sr_ref, si_ref, cr_ref, ci_ref, xr_ref, xi_ref):
        @pl.when(pl.program_id(1) == 0)
        def _():
            cr_ref[...] = jnp.zeros_like(cr_ref)
            ci_ref[...] = jnp.zeros_like(ci_ref)

        u_kb = u_ref[...]
        ub = u_kb.astype(BF16)
        xr = jnp.dot(ub, wbr_ref[0], preferred_element_type=F32)
        xi = jnp.dot(ub, wbi_ref[0], preferred_element_type=F32)
        for i in range(LB_KB):
            xr_ref[i] = xr[:, BLK * i:BLK * i + BLK]
            xi_ref[i] = xi[:, BLK * i:BLK * i + BLK]
        row = lax.broadcasted_iota(jnp.int32, (g, BLK), 0)
        for i in range(LB_KB):
            lanes = slice(BLK * i, BLK * i + BLK)
            ar, ai = a1r_ref[0:1, lanes], a1i_ref[0:1, lanes]
            cin_r, cin_i = cr_ref[0:1, lanes], ci_ref[0:1, lanes]
            tr, ti = _chunk_carries(xr_ref, xi_ref, i, ar, ai, sqr_ref, sqi_ref, seqr_ref[:, lanes],
                                    seqi_ref[:, lanes], cin_r, cin_i, 1.0, reverse=False)
            cr_ref[0:1, lanes] = jnp.sum(jnp.where(row == g - 1, tr, 0.0), axis=0, keepdims=True)
            ci_ref[0:1, lanes] = jnp.sum(jnp.where(row == g - 1, ti, 0.0), axis=0, keepdims=True)
            pr = jnp.where(row == 0, cin_r, pltpu.roll(tr, 1, axis=0))
            pi = jnp.where(row == 0, cin_i, pltpu.roll(ti, 1, axis=0))
            for r in range(GRP):
                pr, pi = (_lane_slab(xr_ref, i, r, g) + ar * pr - ai * pi,
                          _lane_slab(xi_ref, i, r, g) + ar * pi + ai * pr)
                sr_ref[i, _strided(r, g), :] = pr
                si_ref[i, _strided(r, g), :] = pi
        s_r = jnp.concatenate([sr_ref[i] for i in range(LB_KB)], axis=1)
        s_i = jnp.concatenate([si_ref[i] for i in range(LB_KB)], axis=1)
        y = (jnp.dot(s_r.astype(BF16), wcr_ref[0], preferred_element_type=F32)
             - jnp.dot(s_i.astype(BF16), wci_ref[0], preferred_element_type=F32)
             + d_ref[...] * u_kb)
        y_ref[...] = y.astype(BF16)
        z_ref[...] = _gelu(y).astype(BF16)

    chan = pl.BlockSpec((q, BLK), lambda k, c: (c, k))
    wb_spec = pl.BlockSpec((1, BLK, ST_KB), lambda k, c: (k, 0, 0))
    wc_spec = pl.BlockSpec((1, ST_KB, BLK), lambda k, c: (k, 0, 0))
    tab_specs = [pl.BlockSpec((t.shape[0], ST_KB), lambda k, c: (0, k)) for t in tabs[:6]]
    state_spec = pl.BlockSpec((LB_KB, q, BLK), lambda k, c: (k, c, 0))
    state_shape = jax.ShapeDtypeStruct((N_LB, l_dim, BLK), F32)
    return pl.pallas_call(
        body, name="ssm_fwd", grid=(SSM_KB, nc),
        in_specs=[chan, wb_spec, wb_spec, wc_spec, wc_spec, pl.BlockSpec((1, BLK), lambda k, c: (0, k))] + tab_specs,
        out_specs=[chan, chan, state_spec, state_spec],
        out_shape=[jax.ShapeDtypeStruct((l_dim, D_MODEL), BF16), jax.ShapeDtypeStruct((l_dim, D_MODEL), BF16),
                   state_shape, state_shape],
        scratch_shapes=[pltpu.VMEM((8, ST_KB), F32), pltpu.VMEM((8, ST_KB), F32),
                        pltpu.VMEM((LB_KB, q, BLK), F32), pltpu.VMEM((LB_KB, q, BLK), F32)],
        compiler_params=_params(dimension_semantics=("parallel", "arbitrary")),
    )(u, wb_re, wb_im, wc_re, wc_im, d_skip, *tabs[:6])


def _ssm_bwd_kb(dz, y, u, s_re, s_im, wb_re, wb_im, wc_re, wc_im, d_skip, tabs, q):
    l_dim = u.shape[0]
    nc = l_dim // q
    g = q // GRP

    def body(dz_ref, y_ref, u_ref, sr_ref, si_ref, wbr_ref, wbi_ref, wcr_ref, wci_ref, d_ref,
             a1r_ref, a1i_ref, sqr_ref, sqi_ref, revr_ref, revi_ref,
             du_ref, dd_ref, dar_ref, dai_ref, dwbr_ref, dwbi_ref, dwcr_ref, dwci_ref,
             cr_ref, ci_ref, gr_ref, gi_ref):
        @pl.when(pl.program_id(1) == 0)
        def _():
            for ref in (cr_ref, ci_ref, dd_ref, dar_ref, dai_ref, dwbr_ref, dwbi_ref, dwcr_ref, dwci_ref):
                ref[...] = jnp.zeros_like(ref)

        tn = (((0,), (0,)), ((), ()))
        nt = (((1,), (1,)), ((), ()))
        u_kb = u_ref[...]
        dy = dz_ref[...] * _gelu_grad(y_ref[...].astype(F32))
        dyb = dy.astype(BF16)
        ub = u_kb.astype(BF16)
        dd_ref[...] += jnp.sum(dy * u_kb, axis=0, keepdims=True)
        ds_r = lax.dot_general(dyb, wcr_ref[0], nt, preferred_element_type=F32)
        ds_i = -lax.dot_general(dyb, wci_ref[0], nt, preferred_element_type=F32)
        for i in range(LB_KB):
            gr_ref[i] = ds_r[:, BLK * i:BLK * i + BLK]
            gi_ref[i] = ds_i[:, BLK * i:BLK * i + BLK]
        row = lax.broadcasted_iota(jnp.int32, (g, BLK), 0)
        for i in range(LB_KB):
            lanes = slice(BLK * i, BLK * i + BLK)
            ar, ai = a1r_ref[0:1, lanes], -a1i_ref[0:1, lanes]
            cin_r, cin_i = cr_ref[0:1, lanes], ci_ref[0:1, lanes]
            tr, ti = _chunk_carries(gr_ref, gi_ref, i, ar, ai, sqr_ref, sqi_ref, revr_ref[:, lanes],
                                    -revi_ref[:, lanes], cin_r, cin_i, -1.0, reverse=True)
            cr_ref[0:1, lanes] = jnp.sum(jnp.where(row == 0, tr, 0.0), axis=0, keepdims=True)
            ci_ref[0:1, lanes] = jnp.sum(jnp.where(row == 0, ti, 0.0), axis=0, keepdims=True)
            nr = jnp.where(row == g - 1, cin_r, pltpu.roll(tr, g - 1, axis=0))
            ni = jnp.where(row == g - 1, cin_i, pltpu.roll(ti, g - 1, axis=0))
            acc_r = jnp.zeros((g, BLK), F32)
            acc_i = jnp.zeros((g, BLK), F32)
            for r in range(GRP - 1, -1, -1):
                s_r, s_i = _lane_slab(sr_ref, i, r, g), _lane_slab(si_ref, i, r, g)
                acc_r = acc_r + (nr * s_r + ni * s_i)
                acc_i = acc_i + (ni * s_r - nr * s_i)
                nr, ni = (_lane_slab(gr_ref, i, r, g) + ar * nr - ai * ni,
                          _lane_slab(gi_ref, i, r, g) + ar * ni + ai * nr)
                gr_ref[i, _strided(r, g), :] = nr
                gi_ref[i, _strided(r, g), :] = ni
            dar_ref[:, lanes] += jnp.sum(acc_r, axis=0, keepdims=True)
            dai_ref[:, lanes] += jnp.sum(acc_i, axis=0, keepdims=True)
        grb = jnp.concatenate([gr_ref[i] for i in range(LB_KB)], axis=1).astype(BF16)
        gib = jnp.concatenate([gi_ref[i] for i in range(LB_KB)], axis=1).astype(BF16)
        srb = jnp.concatenate([sr_ref[i] for i in range(LB_KB)], axis=1).astype(BF16)
        sib = jnp.concatenate([si_ref[i] for i in range(LB_KB)], axis=1).astype(BF16)
        du = (lax.dot_general(grb, wbr_ref[0], nt, preferred_element_type=F32)
              + lax.dot_general(gib, wbi_ref[0], nt, preferred_element_type=F32)
              + d_ref[...] * dy)
        du_ref[...] = du.astype(BF16)
        dwbr_ref[0] += lax.dot_general(ub, grb, tn, preferred_element_type=F32)
        dwbi_ref[0] += lax.dot_general(ub, gib, tn, preferred_element_type=F32)
        dwcr_ref[0] += lax.dot_general(srb, dyb, tn, preferred_element_type=F32)
        dwci_ref[0] -= lax.dot_general(sib, dyb, tn, preferred_element_type=F32)

    chan = pl.BlockSpec((q, BLK), lambda k, c: (nc - 1 - c, k))
    wb_spec = pl.BlockSpec((1, BLK, ST_KB), lambda k, c: (k, 0, 0))
    wc_spec = pl.BlockSpec((1, ST_KB, BLK), lambda k, c: (k, 0, 0))
    tab_in = [tabs[0], tabs[1], tabs[2], tabs[3], tabs[6], tabs[7]]
    tab_specs = [pl.BlockSpec((t.shape[0], ST_KB), lambda k, c: (0, k)) for t in tab_in]
    vec = pl.BlockSpec((1, BLK), lambda k, c: (0, k))
    svec = pl.BlockSpec((1, ST_KB), lambda k, c: (0, k))
    state_spec = pl.BlockSpec((LB_KB, q, BLK), lambda k, c: (k, nc - 1 - c, 0))
    return pl.pallas_call(
        body, name="ssm_bwd", grid=(SSM_KB, nc),
        in_specs=[chan, chan, chan, state_spec, state_spec, wb_spec, wb_spec, wc_spec, wc_spec, vec] + tab_specs,
        out_specs=[chan, vec, svec, svec, wb_spec, wb_spec, wc_spec, wc_spec],
        out_shape=[jax.ShapeDtypeStruct((l_dim, D_MODEL), BF16), jax.ShapeDtypeStruct((1, D_MODEL), F32),
                   jax.ShapeDtypeStruct((1, N_STATE), F32), jax.ShapeDtypeStruct((1, N_STATE), F32),
                   jax.ShapeDtypeStruct((SSM_KB, BLK, ST_KB), F32), jax.ShapeDtypeStruct((SSM_KB, BLK, ST_KB), F32),
                   jax.ShapeDtypeStruct((SSM_KB, ST_KB, BLK), F32), jax.ShapeDtypeStruct((SSM_KB, ST_KB, BLK), F32)],
        scratch_shapes=[pltpu.VMEM((8, ST_KB), F32), pltpu.VMEM((8, ST_KB), F32),
                        pltpu.VMEM((LB_KB, q, BLK), F32), pltpu.VMEM((LB_KB, q, BLK), F32)],
        compiler_params=_params(dimension_semantics=("parallel", "arbitrary")),
    )(dz, y, u, s_re, s_im, wb_re, wb_im, wc_re, wc_im, d_skip, *tab_in)


def _discretize(lam_re, lam_im, log_dt, b_re, b_im):
    dt = jnp.exp(log_dt)[:, None]
    mag = jnp.exp(lam_re * dt)
    ar, ai = mag * jnp.cos(lam_im * dt), mag * jnp.sin(lam_im * dt)
    den = lam_re * lam_re + lam_im * lam_im
    nr, ni = ar - 1.0, ai
    fr, fi = (nr * lam_re + ni * lam_im) / den, (ni * lam_re - nr * lam_im) / den
    bbar_re = fr[..., None] * b_re - fi[..., None] * b_im
    bbar_im = fr[..., None] * b_im + fi[..., None] * b_re
    return ar, ai, bbar_re, bbar_im


def _block_diag_b(bbar):
    eye = jnp.eye(8, dtype=bbar.dtype)
    return jnp.einsum("kgpc,gh->kgchp", bbar.reshape(8, 8, SSM_STATE, SSM_GROUP_CH), eye).reshape(8, BLK, ST_KB)


def _block_diag_b_t(dwb):
    eye = jnp.eye(8, dtype=dwb.dtype)
    return jnp.einsum("kgchp,gh->kgpc", dwb.reshape(8, 8, SSM_GROUP_CH, 8, SSM_STATE), eye).reshape(
        SSM_GROUPS, SSM_STATE, SSM_GROUP_CH)


def _block_diag_c(c):
    eye = jnp.eye(8, dtype=c.dtype)
    return jnp.einsum("kgcp,gh->kgphc", c.reshape(8, 8, SSM_GROUP_CH, SSM_STATE), eye).reshape(8, ST_KB, BLK)


def _block_diag_c_t(dwc):
    eye = jnp.eye(8, dtype=dwc.dtype)
    return jnp.einsum("kgphc,gh->kgcp", dwc.reshape(8, 8, SSM_STATE, 8, SSM_GROUP_CH), eye).reshape(
        SSM_GROUPS, SSM_GROUP_CH, SSM_STATE)


def _powers(br, bi, n):
    pr, pi = br, bi
    cr, ci = br, bi
    while pr.shape[0] < n:
        pr, pi = (jnp.concatenate([pr, pr * cr - pi * ci], axis=0),
                  jnp.concatenate([pi, pr * ci + pi * cr], axis=0))
        cr, ci = cr * cr - ci * ci, 2.0 * cr * ci
    return pr[:n], pi[:n]


def _powers_desc(br, bi, n):
    pr, pi = br, bi
    cr, ci = br, bi
    while pr.shape[0] < n:
        pr, pi = (jnp.concatenate([pr * cr - pi * ci, pr], axis=0),
                  jnp.concatenate([pr * ci + pi * cr, pi], axis=0))
        cr, ci = cr * cr - ci * ci, 2.0 * cr * ci
    return pr, pi


def _power_tables(ar, ai, g):
    a1r, a1i = _powers(ar, ai, GRP)
    seqr, seqi = _powers(a1r[GRP - 1:], a1i[GRP - 1:], g)
    g2 = 1 << (g - 1).bit_length()
    revr, revi = _powers_desc(a1r[GRP - 1:], a1i[GRP - 1:], g2)
    revr, revi = revr[g2 - g:], revi[g2 - g:]
    sq_r, sq_i = [seqr[0:1]], [seqi[0:1]]
    while len(sq_r) < 8:
        r, i = sq_r[-1], sq_i[-1]
        sq_r.append(r * r - i * i)
        sq_i.append(2.0 * r * i)
    sqr, sqi = jnp.concatenate(sq_r, axis=0), jnp.concatenate(sq_i, axis=0)
    return a1r, a1i, sqr, sqi, seqr, seqi, revr, revi


HBM_SPEC = pl.BlockSpec(memory_space=pltpu.HBM)
SEM_SPEC = pl.BlockSpec(memory_space=pltpu.SEMAPHORE)
DATAFLOW = pltpu.SideEffectType.DATAFLOW_SIDE_EFFECTING


def _plain_rows(p, m):
    return p * m


def _ffn_in_rows(p, m):
    return ((p & 3) >> 1) * (4 * m) + (p >> 2) * (2 * m) + (p & 1) * m


def _peer_copies(src_refs, land_refs, send_sems, recv_sems, chunked, row_fns):
    x, y, c = lax.axis_index("x"), lax.axis_index("y"), lax.axis_index("c")
    me = 4 * x + 2 * y + c
    copies = []
    for a, (src, land) in enumerate(zip(src_refs, land_refs)):
        m = land.shape[0] // N_DEV
        for k in range(N_DEV - 1):
            rel = k + 1
            bx, by, bc = (rel >> 2) & 1, (rel >> 1) & 1, rel & 1
            peer = (x + bx - 2 * x * bx, y + by - 2 * y * by, c + bc - 2 * c * bc)
            p_idx = 4 * peer[0] + 2 * peer[1] + peer[2]
            copies.append(pltpu.make_async_remote_copy(
                src_ref=src.at[pl.ds(row_fns[a](p_idx, m), m), :] if chunked else src,
                dst_ref=land.at[pl.ds(me * m if chunked else row_fns[a](me, m), m), :],
                send_sem=send_sems[a * (N_DEV - 1) + k], recv_sem=recv_sems[a * (N_DEV - 1) + k],
                device_id=peer, device_id_type=MESH))
    return copies


def _send_start(name, srcs, lands, chunked, row_fns=None):
    n = len(srcs)
    ns = n * (N_DEV - 1)
    row_fns = row_fns or [_plain_rows] * n

    def body(*refs):
        src_refs, land_refs = refs[:n], refs[n:2 * n]
        send_sems, recv_sems = refs[2 * n:2 * n + ns], refs[2 * n + ns:2 * n + 2 * ns]
        token = refs[-1]
        for cp in _peer_copies(src_refs, land_refs, send_sems, recv_sems, chunked, row_fns):
            cp.start()
        token[...] = jnp.zeros_like(token)

    ins = [pltpu.with_memory_space_constraint(a, pltpu.HBM) for a in list(srcs) + list(lands)]
    out = pl.pallas_call(
        body, name=name,
        in_specs=[HBM_SPEC] * (2 * n),
        out_specs=[SEM_SPEC] * (2 * ns) + [HBM_SPEC] * (2 * n) + [pl.BlockSpec(memory_space=pltpu.VMEM)],
        out_shape=[pltpu.SemaphoreType.DMA(())] * (2 * ns)
        + [pltpu.HBM(a.shape, a.dtype) for a in list(srcs) + list(lands)]
        + [jax.ShapeDtypeStruct((8, BLK), F32)],
        input_output_aliases={i: i + 2 * ns for i in range(2 * n)},
        compiler_params=pltpu.CompilerParams(has_side_effects=DATAFLOW),
    )(*ins)
    return out[:ns], out[ns:2 * ns], out[2 * ns:2 * ns + n], out[2 * ns + n:2 * ns + 2 * n], out[-1]


def _send_wait(name, send_sems, recv_sems, srcs, lands, after, chunked, row_fns=None):
    n = len(srcs)
    ns = n * (N_DEV - 1)
    row_fns = row_fns or [_plain_rows] * n

    def body(*refs):
        src_refs, land_refs = refs[:n], refs[n:2 * n]
        s_sems, r_sems = refs[2 * n:2 * n + ns], refs[2 * n + ns:2 * n + 2 * ns]
        copies = _peer_copies(src_refs, land_refs, s_sems, r_sems, chunked, row_fns)
        for cp in copies:
            cp.wait_send()
        for cp in copies:
            cp.wait_recv()

    out = pl.pallas_call(
        body, name=name,
        in_specs=[HBM_SPEC] * (2 * n) + [SEM_SPEC] * (2 * ns) + [pl.BlockSpec(memory_space=pl.ANY)],
        out_specs=[HBM_SPEC] * (2 * n),
        out_shape=[pltpu.HBM(a.shape, a.dtype) for a in list(srcs) + list(lands)],
        input_output_aliases={i: i for i in range(2 * n)},
        compiler_params=pltpu.CompilerParams(has_side_effects=DATAFLOW),
    )(*srcs, *lands, *send_sems, *recv_sems, after)
    return out[n:]


def _sum_slots(name, recv, own):
    m, ncol = own.shape
    tr = m // 2 if (m // 2) % 16 == 0 else m
    g = m // tr

    def body(*refs):
        slots, own_ref, o_ref = refs[:N_DEV], refs[N_DEV], refs[N_DEV + 1]
        me = _my_index()
        tot = None
        for s in range(N_DEV):
            v = jnp.where(me == s, own_ref[...], slots[s][...].astype(F32))
            tot = v if tot is None else tot + v
        o_ref[...] = tot

    def slot_spec(s):
        return pl.BlockSpec((tr, ncol), lambda i: (s * g + i, 0))

    return pl.pallas_call(
        body, name=name, grid=(g,),
        in_specs=[slot_spec(s) for s in range(N_DEV)] + [pl.BlockSpec((tr, ncol), lambda i: (i, 0))],
        out_specs=pl.BlockSpec((tr, ncol), lambda i: (i, 0)),
        out_shape=jax.ShapeDtypeStruct((m, ncol), F32),
        compiler_params=_params(),
    )(*([recv] * N_DEV), own)


def _sum_gathered(name, gathered, rows):
    tr = _pick(rows, 512, 8)
    g = rows // tr

    def body(*refs):
        o_ref = refs[N_DEV]
        tot = refs[0][...]
        for s in range(1, N_DEV):
            tot = tot + refs[s][...]
        o_ref[...] = tot

    return pl.pallas_call(
        body, name=name, grid=(g,),
        in_specs=[pl.BlockSpec((tr, BLK), (lambda i, s=s: (s * g + i, 0))) for s in range(N_DEV)],
        out_specs=pl.BlockSpec((tr, BLK), lambda i: (i, 0)),
        out_shape=jax.ShapeDtypeStruct((rows, BLK), F32),
        compiler_params=_params(),
    )(*([gathered] * N_DEV))


def _adamw(name, w, g, m, v):
    r, c = w.shape
    tr = _pick(r, 256, 8) if r % 8 == 0 else r
    c1 = 1.0 - ADAM_B1 ** ADAM_STEP
    c2 = 1.0 - ADAM_B2 ** ADAM_STEP

    def body(w_ref, g_ref, m_ref, v_ref, d_ref, nm_ref, nv_ref):
        gv = g_ref[...]
        nm = ADAM_B1 * m_ref[...] + (1.0 - ADAM_B1) * gv
        nv = ADAM_B2 * v_ref[...] + (1.0 - ADAM_B2) * (gv * gv)
        m_hat = nm / c1
        v_hat = nv / c2
        d_ref[...] = -ADAM_LR * (m_hat / (jnp.sqrt(v_hat) + ADAM_EPS) + ADAM_WD * w_ref[...])
        nm_ref[...] = nm
        nv_ref[...] = nv

    spec = pl.BlockSpec((tr, c), lambda i: (i, 0))
    shape = jax.ShapeDtypeStruct((r, c), F32)
    return pl.pallas_call(
        body, name=name, grid=(r // tr,),
        in_specs=[spec] * 4, out_specs=[spec] * 3, out_shape=[shape] * 3,
        compiler_params=_params(),
    )(w, g, m, v)


def _adamw_many(name, ws, gs, ms, vs):
    n = len(ws)
    c1 = 1.0 - ADAM_B1 ** ADAM_STEP
    c2 = 1.0 - ADAM_B2 ** ADAM_STEP

    def body(*refs):
        for a in range(n):
            w_ref, g_ref, m_ref, v_ref = refs[a], refs[n + a], refs[2 * n + a], refs[3 * n + a]
            d_ref, nm_ref, nv_ref = refs[4 * n + a], refs[5 * n + a], refs[6 * n + a]
            gv = g_ref[...]
            nm = ADAM_B1 * m_ref[...] + (1.0 - ADAM_B1) * gv
            nv = ADAM_B2 * v_ref[...] + (1.0 - ADAM_B2) * (gv * gv)
            d_ref[...] = -ADAM_LR * ((nm / c1) / (jnp.sqrt(nv / c2) + ADAM_EPS) + ADAM_WD * w_ref[...])
            nm_ref[...] = nm
            nv_ref[...] = nv

    vmem = pl.BlockSpec(memory_space=pltpu.VMEM)
    shapes = [jax.ShapeDtypeStruct(w.shape, F32) for w in ws]
    out = pl.pallas_call(
        body, name=name,
        in_specs=[vmem] * (4 * n), out_specs=[vmem] * (3 * n), out_shape=shapes * 3,
        compiler_params=_params(),
    )(*ws, *gs, *ms, *vs)
    return out[:n], out[n:2 * n], out[2 * n:]


PACK_ROWS = 128


def _pack(parts):
    flat = []
    for p in parts:
        v = p.reshape(-1)
        flat.append(jnp.pad(v, (0, (-v.shape[0]) % BLK)))
    v = jnp.concatenate(flat)
    v = jnp.pad(v, (0, (-v.shape[0]) % (PACK_ROWS * BLK)))
    return v.reshape(-1, BLK)


def _unpack(buf, shapes):
    flat = buf.reshape(-1)
    out, off = [], 0
    for shp in shapes:
        size = math.prod(shp)
        out.append(flat[off:off + size].reshape(shp))
        off += size + (-size) % BLK
    return out


def kernel(x, meta_tokens, norm_mix, w_in, q_norm, k_norm, attn_sinks, lam_re, lam_im, log_dt, ssm_b_re, ssm_b_im, ssm_c_re, ssm_c_im, ssm_d, w_glu, attn_branch_norm, ssm_branch_norm, w_out, norm_ffn, w_ffn_in, w_ffn_out, loss_target, m_meta_tokens, m_norm_mix, m_w_in, m_q_norm, m_k_norm, m_attn_sinks, m_lam_re, m_lam_im, m_log_dt, m_ssm_b_re, m_ssm_b_im, m_ssm_c_re, m_ssm_c_im, m_ssm_d, m_w_glu, m_attn_branch_norm, m_ssm_branch_norm, m_w_out, m_norm_ffn, m_w_ffn_in, m_w_ffn_out, v_meta_tokens, v_norm_mix, v_w_in, v_q_norm, v_k_norm, v_attn_sinks, v_lam_re, v_lam_im, v_log_dt, v_ssm_b_re, v_ssm_b_im, v_ssm_c_re, v_ssm_c_im, v_ssm_d, v_w_glu, v_attn_branch_norm, v_ssm_branch_norm, v_w_out, v_norm_ffn, v_w_ffn_in, v_w_ffn_out):
    args = dict(locals())
    weights = {n: args[n] for n in WEIGHTS}
    mom_m = {n: args["m_" + n] for n in WEIGHTS}
    mom_v = {n: args["v_" + n] for n in WEIGHTS}

    x2d = x[0]
    target2d = loss_target[0]
    s_len = x2d.shape[0]
    l_dim = s_len + BLK
    tm_row = _pick(l_dim, 320)
    tm_mm = _pick(l_dim, 1040)
    tl_tn = _pick(l_dim, 832)
    tm_ffn = _pick(l_dim, 640)
    tm_big = _pick(l_dim, 2080)

    shard_in = w_in[0].T.astype(BF16)
    shard_glu = w_glu[0].T.astype(BF16)
    shard_out = w_out[0].astype(BF16)
    shard_ffn_in = w_ffn_in[0].T.astype(BF16)
    shard_ffn_out = w_ffn_out[0].astype(BF16)
    shard_meta = meta_tokens.T
    me = _my_index()

    def landing(shard, row_fn=_plain_rows):
        m_rows, cols = shard.shape
        return lax.dynamic_update_slice(lax.empty((N_DEV * m_rows, cols), shard.dtype), shard,
                                        (row_fn(me, m_rows), 0))

    first = [shard_in, shard_meta]
    ga = _send_start("gather_start_a", first, [landing(s) for s in first], chunked=False)
    later = [shard_glu + ga[4][0:1, 0:1].astype(BF16), shard_out, shard_ffn_in, shard_ffn_out]
    later_fns = [_plain_rows, _plain_rows, _ffn_in_rows, _plain_rows]
    gb = _send_start("gather_start_b", later, [landing(s, f) for s, f in zip(later, later_fns)], chunked=False,
                     row_fns=later_fns)

    nm_t = norm_mix + (ga[4][0:1, 0:1] + gb[4][0:1, 0:1])
    qn_t, kn_t = jnp.tile(q_norm, (1, N_Q_HEADS)), jnp.tile(k_norm, (1, N_KV_HEADS))
    e_mat = jnp.kron(jnp.eye(4, dtype=F32), jnp.ones((HEAD_DIM, HEAD_DIM), F32)).astype(BF16)

    def disc(lr, li, ldt, br, bi):
        return _discretize(lr[0], li[0], ldt[0], br[0], bi[0])

    (abar_re, abar_im, bbar_re, bbar_im), disc_vjp = jax.vjp(disc, lam_re, lam_im, log_dt, ssm_b_re, ssm_b_im)
    wb_re, wb_im = _block_diag_b(bbar_re).astype(BF16), _block_diag_b(bbar_im).astype(BF16)
    wc_re, wc_im = _block_diag_c(ssm_c_re[0]).astype(BF16), _block_diag_c(ssm_c_im[0]).astype(BF16)
    q_ssm = _pick(l_dim, 640, 64)
    tabs = _power_tables(abar_re.reshape(1, N_STATE), abar_im.reshape(1, N_STATE), q_ssm // GRP)

    h0, xn = _embed_norm(x2d, nm_t)
    wt_in, meta_t = _send_wait("gather_wait_a", ga[0], ga[1], ga[2], ga[3], xn, chunked=False)
    meta_pad = jnp.pad(meta_t.T, ((PAD, 0), (0, 0)))
    h0, xn = _embed_meta(meta_pad, nm_t, h0, xn)
    qkv = _matmul("proj_qkv", xn, wt_in, nt=True, tm=tm_big, tn=512, tk=D_MODEL, n=Q_W + 2 * KV_W, w_off=0)
    u = _matmul("proj_u", xn, wt_in, nt=True, tm=tm_big, tn=512, tk=D_MODEL, n=D_MODEL, w_off=3)
    gates = _matmul("proj_gates", xn, wt_in, nt=True, tm=tm_big, tn=512, tk=D_MODEL, n=2 * D_MODEL, w_off=5,
                    out_dtype=BF16)
    qn, kf, vf = _qk_prep(qkv, qn_t, kn_t, e_mat, tm_row)
    attn, lse = _attn_fwd(qn, kf, vf, attn_sinks)
    y, z, s_re, s_im = _ssm_fwd_kb(u, wb_re, wb_im, wc_re, wc_im, ssm_d, tabs, q_ssm)
    wt_glu, w_out_f, wt_ffn_in, w_ffn_out_f = _send_wait("gather_wait_b", gb[0], gb[1], gb[2], gb[3], z,
                                                         chunked=False, row_fns=later_fns)
    zab = _matmul("glu_proj", z, wt_glu, nt=True, tm=tm_big, tn=1024, tk=D_MODEL, out_dtype=BF16)
    merged = _merge_fwd(attn, zab, gates, attn_branch_norm, ssm_branch_norm, tm_row)
    h1, hn = _matmul("out_proj", merged, w_out_f, nt=False, tm=tm_mm, tn=1024, tk=D_MODEL, res=h0,
                     norm_g=norm_ffn)
    gu, act = _ffn_in_swiglu(hn, wt_ffn_in, tm_ffn)
    h2 = _matmul("ffn_out", act, w_ffn_out_f, nt=False, tm=tm_mm, tn=1024, tk=1408, res=h1)
    dh2, dh2_b, loss_part = _loss_grad(h2, target2d)

    dgu = _d_act_swiglu(dh2_b, w_ffn_out_f, gu, tm_ffn)

    def exchange_start(name, grads_b, row_fns=None):
        return _send_start(name, grads_b, [jnp.zeros(g.shape, BF16) for g in grads_b], chunked=True,
                           row_fns=row_fns)

    g_ffn_out, g_ffn_out_b = _matmul_tn("g_ffn_out", act, dh2_b, tm=1408, tn=1024, tl=tl_tn)
    g_ffn_in_t, g_ffn_in_b = _matmul_tn("g_ffn_in", dgu, hn, tm=1408, tn=1024, tl=tl_tn)
    ffn_fns = [_ffn_in_rows, _plain_rows]
    ex1 = exchange_start("exchange_start_ffn", [g_ffn_in_b, g_ffn_out_b], ffn_fns)
    dhn = _matmul("d_hn", dgu, wt_ffn_in, nt=False, tm=tm_mm, tn=1024, tk=1408)
    dh1, dh1_b, g_norm_ffn = _norm_bwd_res("ffn_norm_bwd", h1, norm_ffn + ex1[4][0:1, 0:1], dhn, dh2, tm_row)
    dmerged = _matmul("d_merged", dh1_b, w_out_f, nt=True, tm=tm_big, tn=1024, tk=D_MODEL)
    dattn, dzab, dgates, g_abn, g_sbn = _merge_bwd(attn, zab, gates, attn_branch_norm, ssm_branch_norm,
                                                     dmerged, tm_row)
    g_out, g_out_b = _matmul_tn("g_out", merged, dh1_b, tm=1024, tn=1024, tl=tl_tn)
    dz = _matmul("d_z", dzab, wt_glu, nt=False, tm=tm_big, tn=1024, tk=1024)
    g_glu_t, g_glu_b = _matmul_tn("g_glu", dzab, z, tm=1024, tn=1024, tl=tl_tn)
    ex2 = exchange_start("exchange_start_mix", [g_glu_b, g_out_b])
    du, g_ssm_d, g_ar, g_ai, g_wbr, g_wbi, g_wcr, g_wci = _ssm_bwd_kb(
        dz, y, u, s_re, s_im, wb_re, wb_im, wc_re, wc_im, ssm_d + ex2[4][0:1, 0:1], tabs, q_ssm)
    dq, dkc, dkp, dkm, dvc, dvp, dvm, g_sinks = _attn_bwd(qn, kf, vf, attn_sinks, lse, attn, dattn)
    dqkv, g_qn_t, g_kn_t = _qk_bwd(qkv, qn_t, kn_t, e_mat, dq, dkc, dkp, dkm, dvc, dvp, dvm)
    g_lam_re, g_lam_im, g_log_dt, g_b_re, g_b_im = disc_vjp(
        (g_ar.reshape(SSM_GROUPS, SSM_STATE), g_ai.reshape(SSM_GROUPS, SSM_STATE),
         _block_diag_b_t(g_wbr), _block_diag_b_t(g_wbi)))
    small_grads = {
        "q_norm": g_qn_t.reshape(N_Q_HEADS, HEAD_DIM).sum(0)[None],
        "k_norm": g_kn_t.reshape(N_KV_HEADS, HEAD_DIM).sum(0)[None], "attn_sinks": g_sinks,
        "lam_re": g_lam_re, "lam_im": g_lam_im, "log_dt": g_log_dt, "ssm_b_re": g_b_re, "ssm_b_im": g_b_im,
        "ssm_c_re": _block_diag_c_t(g_wcr)[None], "ssm_c_im": _block_diag_c_t(g_wci)[None],
        "ssm_d": g_ssm_d, "attn_branch_norm": g_abn, "ssm_branch_norm": g_sbn, "norm_ffn": g_norm_ffn,
    }
    early = [n for n in SMALL if n != "norm_mix"]
    packed_e = _pack([small_grads[n] for n in early])
    gs_e = _send_start("small_start_a", [packed_e], [landing(packed_e)], chunked=False)

    dproj = jnp.concatenate([dqkv, du, dgates], axis=1)
    g_in_t, g_in_b = _matmul_tn("g_in", dproj, xn, tm=1152, tn=1024, tl=tl_tn, after=gs_e[4])
    ex3 = exchange_start("exchange_start_in", [g_in_b])
    dxn = _matmul("d_xn", dproj, wt_in, nt=False, tm=tm_mm, tn=1024, tk=1152)
    grad_x2d, dmeta_blk, g_norm_mix = _final_bwd(h0, nm_t + ex3[4][0:1, 0:1], dxn, dh1)
    packed_l = _pack([g_norm_mix, dmeta_blk[PAD:]])
    gs_l = _send_start("small_start_b", [packed_l], [landing(packed_l)], chunked=False)
    grads, deltas, new_m, new_v = {}, {}, {}, {}

    recv_ffn_in, recv_ffn_out = _send_wait("exchange_wait_ffn", ex1[0], ex1[1], ex1[2], ex1[3], gs_l[4],
                                           chunked=True, row_fns=ffn_fns)
    recv_glu, recv_out = _send_wait("exchange_wait_mix", ex2[0], ex2[1], ex2[2], ex2[3], recv_ffn_in,
                                    chunked=True)
    (recv_in,) = _send_wait("exchange_wait_in", ex3[0], ex3[1], ex3[2], ex3[3], recv_glu, chunked=True)
    big = [("w_in", g_in_t, True, recv_in, _plain_rows), ("w_glu", g_glu_t, True, recv_glu, _plain_rows),
           ("w_out", g_out, False, recv_out, _plain_rows), ("w_ffn_in", g_ffn_in_t, True, recv_ffn_in, _ffn_in_rows),
           ("w_ffn_out", g_ffn_out, False, recv_ffn_out, _plain_rows)]
    for name, g_full, transposed, recv, row_fn in big:
        m_rows = g_full.shape[0] // N_DEV
        own = lax.dynamic_slice(g_full, (row_fn(me, m_rows), 0), (m_rows, g_full.shape[1]))
        g_shard = _sum_slots("sum_" + name, recv, own)
        grads[name] = (g_shard.T if transposed else g_shard)[None]

    def adamw_2d(name):
        shp = weights[name].shape
        as2d = lambda a: a.reshape(shp[-2], shp[-1])
        d, nm, nv = _adamw("adamw_" + name, as2d(weights[name]), as2d(grads[name]), as2d(mom_m[name]),
                           as2d(mom_v[name]))
        deltas[name], new_m[name], new_v[name] = d.reshape(shp), nm.reshape(shp), nv.reshape(shp)
        return d

    for name in ["w_in", "w_glu", "w_out", "w_ffn_in", "w_ffn_out"]:
        last = adamw_2d(name)

    def small_sum(tag, gs, packed, after):
        (gathered,) = _send_wait("small_wait_" + tag, gs[0], gs[1], gs[2], gs[3], after, chunked=False)
        return _sum_gathered("sum_small_" + tag, gathered, packed.shape[0])

    def small_adamw(tag, names):
        d, nm, nv = _adamw_many("adamw_small_" + tag, [weights[n] for n in names], [grads[n] for n in names],
                                [mom_m[n] for n in names], [mom_v[n] for n in names])
        deltas.update(zip(names, d))
        new_m.update(zip(names, nm))
        new_v.update(zip(names, nv))

    g_sum_e = small_sum("a", gs_e, packed_e, last)
    grads.update(zip(early, _unpack(g_sum_e, [weights[n].shape for n in early])))
    wide = [n for n in early if n.startswith(("ssm_b", "ssm_c"))]
    small_adamw("wide", wide)
    g_sum_l = small_sum("b", gs_l, packed_l, g_sum_e)
    grads["norm_mix"], g_meta = _unpack(g_sum_l, [weights["norm_mix"].shape, (N_META, D_MODEL)])
    small_adamw("rest", [n for n in SMALL if n not in wide])
    grads["meta_tokens"] = lax.dynamic_slice(g_meta, (0, me * BLK), (N_META, BLK))
    adamw_2d("meta_tokens")

    loss = lax.psum(loss_part[0, 0], ("x", "y", "c"))
    return (loss, grad_x2d[None], *[grads[n] for n in WEIGHTS], *[deltas[n] for n in WEIGHTS],
            *[new_m[n] for n in WEIGHTS], *[new_v[n] for n in WEIGHTS])
```

```python
import math

import jax
import jax.numpy as jnp
from jax import lax
from jax.experimental import pallas as pl
from jax.experimental.pallas import tpu as pltpu

F32 = jnp.float32
BF16 = jnp.bfloat16

D_MODEL = 1024
N_META = 16
HEAD_DIM = 64
N_Q_HEADS = 16
N_KV_HEADS = 4
Q_W = N_Q_HEADS * HEAD_DIM
KV_W = N_KV_HEADS * HEAD_DIM
SSM_GROUPS = 64
SSM_GROUP_CH = 16
SSM_STATE = 64
N_STATE = SSM_GROUPS * SSM_STATE
D_FF = 2816
IN_COLS = Q_W + 2 * KV_W + 3 * D_MODEL
EPS = 1e-6
BLK = 128
PAD = BLK - N_META
N_DEV = 8
NEG = -1e30
SSM_KB = 8
ST_KB = N_STATE // SSM_KB
LB_KB = ST_KB // BLK
N_LB = N_STATE // BLK

ADAM_LR = 0.001
ADAM_B1 = 0.9
ADAM_B2 = 0.999
ADAM_EPS = 1e-08
ADAM_WD = 0.01
ADAM_STEP = 10

VMEM_LIMIT = 48 * 1024 * 1024
MESH = pl.DeviceIdType.MESH

SMALL = ["norm_mix", "q_norm", "k_norm", "attn_sinks", "lam_re", "lam_im", "log_dt", "ssm_b_re", "ssm_b_im",
         "ssm_c_re", "ssm_c_im", "ssm_d", "attn_branch_norm", "ssm_branch_norm", "norm_ffn"]
WEIGHTS = ["meta_tokens", "norm_mix", "w_in", "q_norm", "k_norm", "attn_sinks", "lam_re", "lam_im", "log_dt",
           "ssm_b_re", "ssm_b_im", "ssm_c_re", "ssm_c_im", "ssm_d", "w_glu", "attn_branch_norm",
           "ssm_branch_norm", "w_out", "norm_ffn", "w_ffn_in", "w_ffn_out"]


def _params(**kw):
    return pltpu.CompilerParams(vmem_limit_bytes=VMEM_LIMIT, **kw)


def _pick(n, cap, mult=16):
    best = None
    for d in range(mult, min(n, cap) + 1, mult):
        if n % d == 0:
            best = d
    assert best is not None, (n, cap, mult)
    return best


def _my_index():
    return 4 * lax.axis_index("x") + 2 * lax.axis_index("y") + lax.axis_index("c")


def _rms(x, g):
    r = lax.rsqrt(jnp.mean(x * x, axis=-1, keepdims=True) + EPS)
    return x * r * g


def _rms_bwd(x, g, dy):
    r = lax.rsqrt(jnp.mean(x * x, axis=-1, keepdims=True) + EPS)
    t = dy * g
    dx = r * t - x * (r * r * r) * jnp.mean(t * x, axis=-1, keepdims=True)
    dg = jnp.sum(dy * (x * r), axis=0, keepdims=True)
    return dx, dg


def _sigmoid(x):
    return 1.0 / (1.0 + jnp.exp(-x))


def _gelu(x):
    k = math.sqrt(2.0 / math.pi)
    return 0.5 * x * (1.0 + jnp.tanh(k * (x + 0.044715 * (x * x * x))))


def _gelu_grad(x):
    k = math.sqrt(2.0 / math.pi)
    t = jnp.tanh(k * (x + 0.044715 * (x * x * x)))
    return 0.5 * (1.0 + t) + 0.5 * x * (1.0 - t * t) * (k * (1.0 + 3.0 * 0.044715 * (x * x)))


def _head_mean(x, e_ref):
    hi = x.astype(BF16)
    r1 = x - hi.astype(F32)
    mid = r1.astype(BF16)
    lo = (r1 - mid.astype(F32)).astype(BF16)
    e = e_ref[...]
    out = []
    for b in range(x.shape[1] // 256):
        sl = slice(256 * b, 256 * b + 256)
        s = (jnp.dot(hi[:, sl], e, preferred_element_type=F32)
             + jnp.dot(mid[:, sl], e, preferred_element_type=F32)
             + jnp.dot(lo[:, sl], e, preferred_element_type=F32))
        out.append(s)
    s = out[0] if len(out) == 1 else jnp.concatenate(out, axis=1)
    return s * (1.0 / HEAD_DIM)


def _head_rms(x, g, e_ref):
    r = lax.rsqrt(_head_mean(x * x, e_ref) + EPS)
    return x * r * g


def _head_rms_bwd(x, g, dy, e_ref):
    r = lax.rsqrt(_head_mean(x * x, e_ref) + EPS)
    t = dy * g
    dx = r * t - x * (r * r * r) * _head_mean(t * x, e_ref)
    dg = jnp.sum(dy * (x * r), axis=0, keepdims=True)
    return dx, dg


def _lane_half(shape):
    lane = lax.broadcasted_iota(jnp.int32, shape, len(shape) - 1)
    return (lane >> 6) & 1


def _matmul(name, a, w, *, nt, tm, tn, tk, n=None, w_off=0, res=None, norm_g=None, out_dtype=F32):
    m_dim, k_dim = a.shape
    n_dim = n if n is not None else (w.shape[0] if nt else w.shape[1])
    gm, gn, gk = m_dim // tm, n_dim // tn, k_dim // tk
    assert gm * tm == m_dim and gn * tn == n_dim and gk * tk == k_dim, (name, a.shape, w.shape, tm, tn, tk)
    assert norm_g is None or tn == n_dim
    direct = out_dtype == F32
    dn = (((1,), (1,)), ((), ())) if nt else (((1,), (0,)), ((), ()))

    def body(*refs):
        refs = list(refs)
        a_ref, w_ref = refs[0], refs[1]
        pos = 2
        r_ref = g_ref = on_ref = None
        if res is not None:
            r_ref, pos = refs[pos], pos + 1
        if norm_g is not None:
            g_ref, pos = refs[pos], pos + 1
        o_ref, pos = refs[pos], pos + 1
        if norm_g is not None:
            on_ref, pos = refs[pos], pos + 1
        acc = o_ref if direct else refs[pos]
        k = pl.program_id(2)
        part = lax.dot_general(a_ref[...], w_ref[...], dn, preferred_element_type=F32)

        @pl.when(k == 0)
        def _():
            acc[...] = part if r_ref is None or not direct else r_ref[...] + part

        @pl.when(k > 0)
        def _():
            acc[...] += part

        @pl.when(k == gk - 1)
        def _():
            if not direct:
                r = acc[...]
                if r_ref is not None:
                    r = r_ref[...] + r
                o_ref[...] = r.astype(out_dtype)
            if on_ref is not None:
                on_ref[...] = _rms(o_ref[...].astype(F32), g_ref[...]).astype(BF16)

    if nt:
        w_spec = pl.BlockSpec((tn, tk), lambda i, j, k: (j + w_off, k))
    else:
        w_spec = pl.BlockSpec((tk, tn), lambda i, j, k: (k, j))
    in_specs = [pl.BlockSpec((tm, tk), lambda i, j, k: (i, k)), w_spec]
    args = [a, w]
    out_spec = pl.BlockSpec((tm, tn), lambda i, j, k: (i, j))
    out_specs, out_shape = [out_spec], [jax.ShapeDtypeStruct((m_dim, n_dim), out_dtype)]
    if res is not None:
        in_specs.append(out_spec)
        args.append(res)
    if norm_g is not None:
        in_specs.append(pl.BlockSpec((1, tn), lambda i, j, k: (0, 0)))
        args.append(norm_g)
        out_specs.append(out_spec)
        out_shape.append(jax.ShapeDtypeStruct((m_dim, n_dim), BF16))
    out = pl.pallas_call(
        body, name=name, grid=(gm, gn, gk),
        in_specs=in_specs, out_specs=out_specs, out_shape=out_shape,
        scratch_shapes=[] if direct else [pltpu.VMEM((tm, tn), F32)],
        compiler_params=_params(dimension_semantics=("parallel", "parallel", "arbitrary")),
    )(*args)
    return out if norm_g is not None else out[0]


def _matmul_tn(name, a, b, *, tm, tn, tl, after=None):
    l_dim, m_dim = a.shape
    n_dim = b.shape[1]
    gm, gn, gl = m_dim // tm, n_dim // tn, l_dim // tl
    assert gm * tm == m_dim and gn * tn == n_dim and gl * tl == l_dim, (name, a.shape, b.shape, tm, tn, tl)

    def body(*refs):
        a_ref, b_ref = refs[0], refs[1]
        o_ref, ob_ref = refs[-2], refs[-1]

        @pl.when(pl.program_id(2) == 0)
        def _():
            o_ref[...] = jnp.zeros_like(o_ref)

        o_ref[...] += lax.dot_general(a_ref[...], b_ref[...], (((0,), (0,)), ((), ())),
                                      preferred_element_type=F32)

        @pl.when(pl.program_id(2) == gl - 1)
        def _():
            ob_ref[...] = o_ref[...].astype(BF16)

    out_spec = pl.BlockSpec((tm, tn), lambda i, j, l: (i, j))
    in_specs = [pl.BlockSpec((tl, tm), lambda i, j, l: (l, i)), pl.BlockSpec((tl, tn), lambda i, j, l: (l, j))]
    args = [a, b]
    if after is not None:
        in_specs.append(pl.BlockSpec(memory_space=pl.ANY))
        args.append(after)
    return pl.pallas_call(
        body, name=name, grid=(gm, gn, gl),
        in_specs=in_specs,
        out_specs=[out_spec, out_spec],
        out_shape=[jax.ShapeDtypeStruct((m_dim, n_dim), F32), jax.ShapeDtypeStruct((m_dim, n_dim), BF16)],
        compiler_params=_params(dimension_semantics=("parallel", "parallel", "arbitrary")),
    )(*args)


def _row_spec(tm, cols, f=None):
    if f is None:
        return pl.BlockSpec((tm, cols), lambda i: (i, 0))
    return pl.BlockSpec((tm, cols), lambda i: (f(i), 0))


def _full_spec(shape):
    nd = len(shape)
    return pl.BlockSpec(shape, lambda i: (0,) * nd)


def _embed_norm(x2d, g):
    s_len = x2d.shape[0]
    nb = s_len // BLK + 1

    def body(x_ref, g_ref, h_ref, xn_ref):
        h_ref[...] = x_ref[...]
        xn_ref[...] = _rms(x_ref[...], g_ref[...]).astype(BF16)

    return pl.pallas_call(
        body, name="embed_norm", grid=(nb - 1,),
        in_specs=[_row_spec(BLK, D_MODEL), _full_spec((1, D_MODEL))],
        out_specs=[_row_spec(BLK, D_MODEL, lambda i: i + 1), _row_spec(BLK, D_MODEL, lambda i: i + 1)],
        out_shape=[jax.ShapeDtypeStruct((nb * BLK, D_MODEL), F32),
                   jax.ShapeDtypeStruct((nb * BLK, D_MODEL), BF16)],
        compiler_params=_params(),
    )(x2d, g)


def _embed_meta(meta_pad, g, h0, xn):
    def body(mp_ref, g_ref, h_in, xn_in, h_ref, xn_ref):
        h_ref[...] = mp_ref[...]
        xn_ref[...] = _rms(mp_ref[...], g_ref[...]).astype(BF16)

    any_spec = pl.BlockSpec(memory_space=pl.ANY)
    return pl.pallas_call(
        body, name="embed_meta", grid=(1,),
        in_specs=[_full_spec((BLK, D_MODEL)), _full_spec((1, D_MODEL)), any_spec, any_spec],
        out_specs=[_row_spec(BLK, D_MODEL), _row_spec(BLK, D_MODEL)],
        out_shape=[jax.ShapeDtypeStruct(h0.shape, F32), jax.ShapeDtypeStruct(xn.shape, BF16)],
        input_output_aliases={2: 0, 3: 1},
        compiler_params=_params(),
    )(meta_pad, g, h0, xn)


KVX_W = 2 * N_KV_HEADS * BLK


def _qk_prep(qkv, q_norm_t, k_norm_t, e_mat, tm):
    l_dim = qkv.shape[0]

    def body(x_ref, qg_ref, kg_ref, e_ref, q_ref, kf_ref, vf_ref):
        x = x_ref[...]
        q = _head_rms(x[:, :Q_W], qg_ref[...], e_ref) * (HEAD_DIM ** -0.5)
        q_ref[...] = q.astype(BF16)
        k = _head_rms(x[:, Q_W:Q_W + KV_W], kg_ref[...], e_ref)
        v = x[:, Q_W + KV_W:Q_W + 2 * KV_W]
        half = _lane_half((tm, BLK))
        for src, dst, fill in ((k, kf_ref, 0.0), (v, vf_ref, 0.0)):
            for kv in range(N_KV_HEADS):
                blk = src[:, BLK * (kv // 2):BLK * (kv // 2) + BLK]
                swapped = pltpu.roll(blk, HEAD_DIM, axis=1)
                for e in range(2):
                    val = blk if kv % 2 == e else swapped
                    idx = 2 * kv + e
                    dst[:, BLK * idx:BLK * idx + BLK] = jnp.where(half == e, val, fill).astype(BF16)

    return pl.pallas_call(
        body, name="qk_prep", grid=(l_dim // tm,),
        in_specs=[_row_spec(tm, Q_W + 2 * KV_W), _full_spec((1, Q_W)), _full_spec((1, KV_W)),
                  _full_spec((256, 256))],
        out_specs=[_row_spec(tm, Q_W), _row_spec(tm, KVX_W), _row_spec(tm, KVX_W)],
        out_shape=[jax.ShapeDtypeStruct((l_dim, Q_W), BF16),
                   jax.ShapeDtypeStruct((l_dim, KVX_W), BF16),
                   jax.ShapeDtypeStruct((l_dim, KVX_W), BF16)],
        compiler_params=_params(),
    )(qkv, q_norm_t, k_norm_t, e_mat)


def _merge_fwd(attn, zab, gates, abn, sbn, tm):
    l_dim = attn.shape[0]

    def body(a_ref, z_ref, g_ref, an_ref, sn_ref, o_ref):
        z = z_ref[...].astype(F32)
        g = g_ref[...].astype(F32)
        ssm = z[:, :D_MODEL] * _sigmoid(z[:, D_MODEL:])
        merged = (_sigmoid(g[:, :D_MODEL]) * _rms(a_ref[...], an_ref[...])
                  + _sigmoid(g[:, D_MODEL:]) * _rms(ssm, sn_ref[...]))
        o_ref[...] = merged.astype(BF16)

    return pl.pallas_call(
        body, name="merge_fwd", grid=(l_dim // tm,),
        in_specs=[_row_spec(tm, D_MODEL), _row_spec(tm, 2 * D_MODEL), _row_spec(tm, 2 * D_MODEL),
                  _full_spec((1, D_MODEL)), _full_spec((1, D_MODEL))],
        out_specs=_row_spec(tm, D_MODEL),
        out_shape=jax.ShapeDtypeStruct((l_dim, D_MODEL), BF16),
        compiler_params=_params(),
    )(attn, zab, gates, abn, sbn)


def _merge_bwd(attn, zab, gates, abn, sbn, dmerged, tm):
    l_dim = attn.shape[0]

    def body(a_ref, z_ref, g_ref, an_ref, sn_ref, dm_ref, da_ref, dz_ref, dg_ref, dan_ref, dsn_ref):
        @pl.when(pl.program_id(0) == 0)
        def _():
            dan_ref[...] = jnp.zeros_like(dan_ref)
            dsn_ref[...] = jnp.zeros_like(dsn_ref)

        z = z_ref[...].astype(F32)
        g = g_ref[...].astype(F32)
        dm = dm_ref[...]
        attn_v = a_ref[...]
        za, zb = z[:, :D_MODEL], z[:, D_MODEL:]
        sb = _sigmoid(zb)
        ssm = za * sb
        s_ga, s_gs = _sigmoid(g[:, :D_MODEL]), _sigmoid(g[:, D_MODEL:])
        a_n = _rms(attn_v, an_ref[...])
        s_n = _rms(ssm, sn_ref[...])
        dg_ref[:, :D_MODEL] = (dm * a_n * s_ga * (1.0 - s_ga)).astype(BF16)
        dg_ref[:, D_MODEL:] = (dm * s_n * s_gs * (1.0 - s_gs)).astype(BF16)
        dattn, dan = _rms_bwd(attn_v, an_ref[...], dm * s_ga)
        dssm, dsn = _rms_bwd(ssm, sn_ref[...], dm * s_gs)
        da_ref[...] = dattn.astype(BF16)
        dz_ref[:, :D_MODEL] = (dssm * sb).astype(BF16)
        dz_ref[:, D_MODEL:] = (dssm * za * sb * (1.0 - sb)).astype(BF16)
        dan_ref[...] += dan
        dsn_ref[...] += dsn

    return pl.pallas_call(
        body, name="merge_bwd", grid=(l_dim // tm,),
        in_specs=[_row_spec(tm, D_MODEL), _row_spec(tm, 2 * D_MODEL), _row_spec(tm, 2 * D_MODEL),
                  _full_spec((1, D_MODEL)), _full_spec((1, D_MODEL)), _row_spec(tm, D_MODEL)],
        out_specs=[_row_spec(tm, D_MODEL), _row_spec(tm, 2 * D_MODEL), _row_spec(tm, 2 * D_MODEL),
                   _full_spec((1, D_MODEL)), _full_spec((1, D_MODEL))],
        out_shape=[jax.ShapeDtypeStruct((l_dim, D_MODEL), BF16),
                   jax.ShapeDtypeStruct((l_dim, 2 * D_MODEL), BF16),
                   jax.ShapeDtypeStruct((l_dim, 2 * D_MODEL), BF16),
                   jax.ShapeDtypeStruct((1, D_MODEL), F32), jax.ShapeDtypeStruct((1, D_MODEL), F32)],
        compiler_params=_params(),
    )(attn, zab, gates, abn, sbn, dmerged)


FF_TILE = D_FF // 2


def _ffn_in_swiglu(hn, wt_ffn_in, tm):
    l_dim = hn.shape[0]
    nt = (((1,), (1,)), ((), ()))

    def body(a_ref, w_ref, gu_ref, act_ref):
        r = lax.dot_general(a_ref[...], w_ref[...], nt, preferred_element_type=F32)
        gate, up = r[:, :FF_TILE], r[:, FF_TILE:]
        gu_ref[...] = r.astype(BF16)
        act_ref[...] = (gate * _sigmoid(gate) * up).astype(BF16)

    return pl.pallas_call(
        body, name="ffn_in_swiglu", grid=(l_dim // tm, 2),
        in_specs=[pl.BlockSpec((tm, D_MODEL), lambda i, j: (i, 0)),
                  pl.BlockSpec((2 * FF_TILE, D_MODEL), lambda i, j: (j, 0))],
        out_specs=[pl.BlockSpec((tm, 2 * FF_TILE), lambda i, j: (i, j)),
                   pl.BlockSpec((tm, FF_TILE), lambda i, j: (i, j))],
        out_shape=[jax.ShapeDtypeStruct((l_dim, 2 * D_FF), BF16), jax.ShapeDtypeStruct((l_dim, D_FF), BF16)],
        compiler_params=_params(dimension_semantics=("parallel", "parallel")),
    )(hn, wt_ffn_in)


def _d_act_swiglu(dh2_b, w_ffn_out, gu, tm):
    l_dim = dh2_b.shape[0]
    nt = (((1,), (1,)), ((), ()))

    def body(d_ref, w_ref, gu_ref, o_ref):
        d = lax.dot_general(d_ref[...], w_ref[...], nt, preferred_element_type=F32)
        gate = gu_ref[:, :FF_TILE].astype(F32)
        up = gu_ref[:, FF_TILE:].astype(F32)
        s = _sigmoid(gate)
        o_ref[:, :FF_TILE] = (d * up * (s * (1.0 + gate * (1.0 - s)))).astype(BF16)
        o_ref[:, FF_TILE:] = (d * (gate * s)).astype(BF16)

    return pl.pallas_call(
        body, name="d_act_swiglu", grid=(l_dim // tm, 2),
        in_specs=[pl.BlockSpec((tm, D_MODEL), lambda i, j: (i, 0)),
                  pl.BlockSpec((FF_TILE, D_MODEL), lambda i, j: (j, 0)),
                  pl.BlockSpec((tm, 2 * FF_TILE), lambda i, j: (i, j))],
        out_specs=pl.BlockSpec((tm, 2 * FF_TILE), lambda i, j: (i, j)),
        out_shape=jax.ShapeDtypeStruct((l_dim, 2 * D_FF), BF16),
        compiler_params=_params(dimension_semantics=("parallel", "parallel")),
    )(dh2_b, w_ffn_out, gu)


def _loss_grad(h2, target2d):
    l_dim = h2.shape[0]
    nb = l_dim // BLK

    def body(h_ref, t_ref, d_ref, db_ref, loss_ref):
        i = pl.program_id(0)

        @pl.when(i == 0)
        def _():
            loss_ref[...] = jnp.zeros_like(loss_ref)
            d_ref[...] = jnp.zeros_like(d_ref)
            db_ref[...] = jnp.zeros_like(db_ref)

        @pl.when(i > 0)
        def _():
            err = h_ref[...] - t_ref[...]
            d = err * (1.0 / D_MODEL)
            d_ref[...] = d
            db_ref[...] = d.astype(BF16)
            loss_ref[...] += 0.5 * jnp.sum(jnp.mean(err * err, axis=-1, keepdims=True), axis=0, keepdims=True)

    return pl.pallas_call(
        body, name="loss_grad", grid=(nb,),
        in_specs=[_row_spec(BLK, D_MODEL), _row_spec(BLK, D_MODEL, lambda i: jnp.maximum(i - 1, 0))],
        out_specs=[_row_spec(BLK, D_MODEL), _row_spec(BLK, D_MODEL), _full_spec((1, 1))],
        out_shape=[jax.ShapeDtypeStruct((l_dim, D_MODEL), F32), jax.ShapeDtypeStruct((l_dim, D_MODEL), BF16),
                   jax.ShapeDtypeStruct((1, 1), F32)],
        compiler_params=_params(),
    )(h2, target2d)


def _norm_bwd_res(name, h, g, dy, dres, tm):
    l_dim = h.shape[0]

    def body(h_ref, g_ref, dy_ref, dr_ref, o_ref, ob_ref, dg_ref):
        @pl.when(pl.program_id(0) == 0)
        def _():
            dg_ref[...] = jnp.zeros_like(dg_ref)

        dx, dg = _rms_bwd(h_ref[...], g_ref[...], dy_ref[...])
        out = dr_ref[...] + dx
        o_ref[...] = out
        ob_ref[...] = out.astype(BF16)
        dg_ref[...] += dg

    return pl.pallas_call(
        body, name=name, grid=(l_dim // tm,),
        in_specs=[_row_spec(tm, D_MODEL), _full_spec((1, D_MODEL)), _row_spec(tm, D_MODEL), _row_spec(tm, D_MODEL)],
        out_specs=[_row_spec(tm, D_MODEL), _row_spec(tm, D_MODEL), _full_spec((1, D_MODEL))],
        out_shape=[jax.ShapeDtypeStruct((l_dim, D_MODEL), F32), jax.ShapeDtypeStruct((l_dim, D_MODEL), BF16),
                   jax.ShapeDtypeStruct((1, D_MODEL), F32)],
        compiler_params=_params(),
    )(h, g, dy, dres)


def _final_bwd(h0, g, dxn, dh1):
    l_dim = h0.shape[0]
    nb = l_dim // BLK

    def body(h_ref, g_ref, dy_ref, dr_ref, gx_ref, gm_ref, dg_ref):
        i = pl.program_id(0)

        @pl.when(i == 0)
        def _():
            dg_ref[...] = jnp.zeros_like(dg_ref)

        dx, dg = _rms_bwd(h_ref[...], g_ref[...], dy_ref[...])
        out = dr_ref[...] + dx
        dg_ref[...] += dg

        @pl.when(i == 0)
        def _():
            gm_ref[...] = out

        @pl.when(i > 0)
        def _():
            gx_ref[...] = out

    return pl.pallas_call(
        body, name="final_bwd", grid=(nb,),
        in_specs=[_row_spec(BLK, D_MODEL), _full_spec((1, D_MODEL)), _row_spec(BLK, D_MODEL),
                  _row_spec(BLK, D_MODEL)],
        out_specs=[_row_spec(BLK, D_MODEL, lambda i: jnp.maximum(i - 1, 0)), _full_spec((BLK, D_MODEL)),
                   _full_spec((1, D_MODEL))],
        out_shape=[jax.ShapeDtypeStruct((l_dim - BLK, D_MODEL), F32), jax.ShapeDtypeStruct((BLK, D_MODEL), F32),
                   jax.ShapeDtypeStruct((1, D_MODEL), F32)],
        compiler_params=_params(),
    )(h0, g, dxn, dh1)


def _attn_valid(n):
    shape = (2 * BLK, 3 * BLK)
    qi = lax.broadcasted_iota(jnp.int32, shape, 0) & (BLK - 1)
    col = lax.broadcasted_iota(jnp.int32, shape, 1)
    kj = col & (BLK - 1)
    part = col >> 7
    nn = jnp.zeros(shape, jnp.int32) + n
    meta_ok = (part == 0) & (kj >= PAD) & (nn >= 1)
    prev_ok = (part == 1) & (kj > qi) & (nn >= 2)
    cur_ok = (part == 2) & (kj <= qi) & ((nn >= 1) | (kj >= PAD))
    return meta_ok | prev_ok | cur_ok


def _attn_scores(q_ref, kwin, sk_ref, valid, kv, e):
    qs = jnp.concatenate([q_ref[:, BLK * (2 * kv):BLK * (2 * kv) + BLK],
                          q_ref[:, BLK * (2 * kv + 1):BLK * (2 * kv + 1) + BLK]], axis=0)
    s = lax.dot_general(qs, kwin, (((1,), (1,)), ((), ())), preferred_element_type=F32)
    h0 = 4 * kv + e
    row = lax.broadcasted_iota(jnp.int32, (2 * BLK, 1), 0)
    sink = jnp.where(row < BLK, sk_ref[:, h0:h0 + 1], sk_ref[:, h0 + 2:h0 + 3])
    return qs, jnp.where(valid, s, NEG), sink


def _attn_specs(nb):
    prev = lambda i: jnp.maximum(i - 1, 0)
    zero = lambda i: 0
    kv_specs = [_row_spec(BLK, KVX_W, zero), _row_spec(BLK, KVX_W, prev), _row_spec(BLK, KVX_W)]
    return kv_specs


def _attn_fwd(qn, kf, vf, sinks):
    l_dim = qn.shape[0]
    nb = l_dim // BLK

    def body(q_ref, km_ref, kp_ref, kc_ref, vm_ref, vp_ref, vc_ref, sk_ref, o_ref, lse_ref):
        valid = _attn_valid(pl.program_id(0))
        lane = lax.broadcasted_iota(jnp.int32, (BLK, BLK), 1)
        lse_all = jnp.zeros((BLK, BLK), F32)
        for kv in range(N_KV_HEADS):
            outs = []
            for e in range(2):
                sl = slice(BLK * (2 * kv + e), BLK * (2 * kv + e) + BLK)
                kwin = jnp.concatenate([km_ref[:, sl], kp_ref[:, sl], kc_ref[:, sl]], axis=0)
                vwin = jnp.concatenate([vm_ref[:, sl], vp_ref[:, sl], vc_ref[:, sl]], axis=0)
                _, s, sink = _attn_scores(q_ref, kwin, sk_ref, valid, kv, e)
                m = jnp.maximum(jnp.max(s, axis=-1, keepdims=True), sink)
                ex = jnp.exp(s - m)
                den = jnp.sum(ex, axis=-1, keepdims=True) + jnp.exp(sink - m)
                p = ex * (1.0 / den)
                outs.append(jnp.dot(p.astype(BF16), vwin, preferred_element_type=F32))
                lse = m + jnp.log(den)
                lse_all = jnp.where(lane == 4 * kv + e, lse[:BLK], lse_all)
                lse_all = jnp.where(lane == 4 * kv + 2 + e, lse[BLK:], lse_all)
            o = outs[0] + outs[1]
            o_ref[:, BLK * (2 * kv):BLK * (2 * kv) + BLK] = o[:BLK]
            o_ref[:, BLK * (2 * kv + 1):BLK * (2 * kv + 1) + BLK] = o[BLK:]
        lse_ref[...] = lse_all

    kv_specs = _attn_specs(nb)
    return pl.pallas_call(
        body, name="attn_fwd", grid=(nb,),
        in_specs=[_row_spec(BLK, Q_W)] + kv_specs + kv_specs + [_full_spec((1, N_Q_HEADS))],
        out_specs=[_row_spec(BLK, Q_W), _row_spec(BLK, BLK)],
        out_shape=[jax.ShapeDtypeStruct((l_dim, Q_W), F32), jax.ShapeDtypeStruct((l_dim, BLK), F32)],
        compiler_params=_params(),
    )(qn, kf, kf, kf, vf, vf, vf, sinks)


def _attn_bwd(qn, kf, vf, sinks, lse, attn, dattn):
    l_dim = qn.shape[0]
    nb = l_dim // BLK
    wide = KVX_W
    tn = (((0,), (0,)), ((), ()))
    nt = (((1,), (1,)), ((), ()))

    def body(q_ref, km_ref, kp_ref, kc_ref, vm_ref, vp_ref, vc_ref, sk_ref, lse_ref, o_ref, do_ref,
             dq_ref, dkc_ref, dkp_ref, dkm_ref, dvc_ref, dvp_ref, dvm_ref, dsk_ref):
        @pl.when(pl.program_id(0) == 0)
        def _():
            dkm_ref[...] = jnp.zeros_like(dkm_ref)
            dvm_ref[...] = jnp.zeros_like(dvm_ref)
            dsk_ref[...] = jnp.zeros_like(dsk_ref)

        valid = _attn_valid(pl.program_id(0))
        half = _lane_half((BLK, BLK))
        half2 = _lane_half((2 * BLK, BLK))
        lane16 = lax.broadcasted_iota(jnp.int32, (1, N_Q_HEADS), 1)
        dsk = jnp.zeros((1, N_Q_HEADS), F32)
        for kv in range(N_KV_HEADS):
            j0, j1 = 2 * kv, 2 * kv + 1
            do0 = do_ref[:, BLK * j0:BLK * j0 + BLK]
            do1 = do_ref[:, BLK * j1:BLK * j1 + BLK]
            do0f, do1f = do0.astype(F32), do1.astype(F32)
            prod0 = do0f * o_ref[:, BLK * j0:BLK * j0 + BLK]
            prod1 = do1f * o_ref[:, BLK * j1:BLK * j1 + BLK]
            dos = jnp.concatenate([do0, do1], axis=0)
            dqs = []
            for e in range(2):
                sl = slice(BLK * (2 * kv + e), BLK * (2 * kv + e) + BLK)
                kwin = jnp.concatenate([km_ref[:, sl], kp_ref[:, sl], kc_ref[:, sl]], axis=0)
                vwin = jnp.concatenate([vm_ref[:, sl], vp_ref[:, sl], vc_ref[:, sl]], axis=0)
                qs, s, sink = _attn_scores(q_ref, kwin, sk_ref, valid, kv, e)
                h0 = 4 * kv + e
                lse_rows = jnp.concatenate([lse_ref[:, h0:h0 + 1], lse_ref[:, h0 + 2:h0 + 3]], axis=0)
                p = jnp.exp(s - lse_rows)
                p_sink = jnp.exp(sink - lse_rows)
                delta = jnp.concatenate(
                    [jnp.sum(jnp.where(half == e, prod0, 0.0), axis=-1, keepdims=True),
                     jnp.sum(jnp.where(half == e, prod1, 0.0), axis=-1, keepdims=True)], axis=0)
                dp = lax.dot_general(dos, vwin, nt, preferred_element_type=F32)
                ds = (p * (dp - delta)).astype(BF16)
                pb = p.astype(BF16)
                dqs.append(jnp.dot(ds, kwin, preferred_element_type=F32))
                dk = lax.dot_general(ds, qs, tn, preferred_element_type=F32)
                dv = lax.dot_general(pb, dos, tn, preferred_element_type=F32)
                dkm_ref[:, sl] += dk[:BLK]
                dkp_ref[:, sl] = dk[BLK:2 * BLK].astype(BF16)
                dkc_ref[:, sl] = dk[2 * BLK:].astype(BF16)
                dvm_ref[:, sl] += dv[:BLK]
                dvp_ref[:, sl] = dv[BLK:2 * BLK].astype(BF16)
                dvc_ref[:, sl] = dv[2 * BLK:].astype(BF16)
                sink_g = -(p_sink * delta)
                g_lo = jnp.sum(sink_g[:BLK], axis=0, keepdims=True)
                g_hi = jnp.sum(sink_g[BLK:], axis=0, keepdims=True)
                dsk = dsk + jnp.where(lane16 == h0, g_lo, 0.0) + jnp.where(lane16 == h0 + 2, g_hi, 0.0)
            dq = jnp.where(half2 == 0, dqs[0], dqs[1])
            dq_ref[:, BLK * j0:BLK * j0 + BLK] = dq[:BLK].astype(BF16)
            dq_ref[:, BLK * j1:BLK * j1 + BLK] = dq[BLK:].astype(BF16)
        dsk_ref[...] += dsk

    kv_specs = _attn_specs(nb)
    row_wide = _row_spec(BLK, wide)
    acc_wide = _full_spec((BLK, wide))
    big = jax.ShapeDtypeStruct((l_dim, wide), BF16)
    return pl.pallas_call(
        body, name="attn_bwd", grid=(nb,),
        in_specs=[_row_spec(BLK, Q_W)] + kv_specs + kv_specs
        + [_full_spec((1, N_Q_HEADS)), _row_spec(BLK, BLK), _row_spec(BLK, Q_W), _row_spec(BLK, Q_W)],
        out_specs=[_row_spec(BLK, Q_W), row_wide, row_wide, acc_wide, row_wide, row_wide, acc_wide,
                   _full_spec((1, N_Q_HEADS))],
        out_shape=[jax.ShapeDtypeStruct((l_dim, Q_W), BF16), big, big, jax.ShapeDtypeStruct((BLK, wide), F32),
                   big, big, jax.ShapeDtypeStruct((BLK, wide), F32), jax.ShapeDtypeStruct((1, N_Q_HEADS), F32)],
        compiler_params=_params(),
    )(qn, kf, kf, kf, vf, vf, vf, sinks, lse, attn, dattn)


def _qk_bwd(qkv, q_norm_t, k_norm_t, e_mat, dq, dkc, dkp, dkm, dvc, dvp, dvm):
    l_dim = qkv.shape[0]
    nb = l_dim // BLK
    wide = KVX_W

    def fold(x):
        half = _lane_half((BLK, BLK))
        blocks = []
        for kb in range(2):
            t = []
            for kv in (2 * kb, 2 * kb + 1):
                own = kv % 2
                a = x[:, BLK * (2 * kv + own):BLK * (2 * kv + own) + BLK]
                b = pltpu.roll(x[:, BLK * (2 * kv + 1 - own):BLK * (2 * kv + 1 - own) + BLK], HEAD_DIM, axis=1)
                t.append(a + b)
            blocks.append(jnp.where(half == 0, t[0], t[1]))
        return jnp.concatenate(blocks, axis=1)

    def body(x_ref, qg_ref, kg_ref, e_ref, dq_ref, dkc_ref, dkp_ref, dkm_ref, dvc_ref, dvp_ref, dvm_ref,
             o_ref, dqg_ref, dkg_ref):
        i = pl.program_id(0)

        @pl.when(i == 0)
        def _():
            dqg_ref[...] = jnp.zeros_like(dqg_ref)
            dkg_ref[...] = jnp.zeros_like(dkg_ref)

        first = jnp.where(i == 0, 1.0, 0.0)
        not_last = jnp.where(i < nb - 1, 1.0, 0.0)
        dk_x = dkc_ref[...].astype(F32) + not_last * dkp_ref[...].astype(F32) + first * dkm_ref[...]
        dv_x = dvc_ref[...].astype(F32) + not_last * dvp_ref[...].astype(F32) + first * dvm_ref[...]
        x = x_ref[...]
        dqx, dqg = _head_rms_bwd(x[:, :Q_W], qg_ref[...], dq_ref[...].astype(F32) * (HEAD_DIM ** -0.5), e_ref)
        dkx, dkg = _head_rms_bwd(x[:, Q_W:Q_W + KV_W], kg_ref[...], fold(dk_x), e_ref)
        o_ref[:, :Q_W] = dqx.astype(BF16)
        o_ref[:, Q_W:Q_W + KV_W] = dkx.astype(BF16)
        o_ref[:, Q_W + KV_W:] = fold(dv_x).astype(BF16)
        dqg_ref[...] += dqg
        dkg_ref[...] += dkg

    nxt = lambda i: jnp.minimum(i + 1, nb - 1)
    row_wide = _row_spec(BLK, wide)
    nxt_wide = _row_spec(BLK, wide, nxt)
    acc_wide = _full_spec((BLK, wide))
    return pl.pallas_call(
        body, name="qk_bwd", grid=(nb,),
        in_specs=[_row_spec(BLK, Q_W + 2 * KV_W), _full_spec((1, Q_W)), _full_spec((1, KV_W)),
                  _full_spec((256, 256)), _row_spec(BLK, Q_W),
                  row_wide, nxt_wide, acc_wide, row_wide, nxt_wide, acc_wide],
        out_specs=[_row_spec(BLK, Q_W + 2 * KV_W), _full_spec((1, Q_W)), _full_spec((1, KV_W))],
        out_shape=[jax.ShapeDtypeStruct((l_dim, Q_W + 2 * KV_W), BF16),
                   jax.ShapeDtypeStruct((1, Q_W), F32), jax.ShapeDtypeStruct((1, KV_W), F32)],
        compiler_params=_params(),
    )(qkv, q_norm_t, k_norm_t, e_mat, dq, dkc, dkp, dkm, dvc, dvp, dvm)


GRP = 8


def _strided(r, g):
    return pl.ds(r, g, stride=GRP)


def _slab(ref, base, r, g):
    return jnp.concatenate([ref[base + i, _strided(r, g), :] for i in range(LB_KB)], axis=1)


def _slab_store(ref, base, r, g, val):
    for i in range(LB_KB):
        ref[base + i, _strided(r, g), :] = val[:, BLK * i:BLK * i + BLK]


def _group_totals(xr_ref, xi_ref, base, ar, ai, reverse):
    g = xr_ref.shape[1] // GRP
    sr = si = None
    for r in (range(GRP - 1, -1, -1) if reverse else range(GRP)):
        xr, xi = _slab(xr_ref, base, r, g), _slab(xi_ref, base, r, g)
        if sr is not None:
            xr, xi = xr + ar * sr - ai * si, xi + ar * si + ai * sr
        sr, si = xr, xi
    return sr, si


def _carry_scan(tr, ti, sqr_ref, sqi_ref, lanes, sign, reverse):
    g = tr.shape[0]
    row = lax.broadcasted_iota(jnp.int32, tr.shape, 0)
    idx, s = 0, 1
    while s < g:
        ar = sqr_ref[idx:idx + 1, lanes]
        ai = sign * sqi_ref[idx:idx + 1, lanes]
        shift, keep = (g - s, row < g - s) if reverse else (s, row >= s)
        pr = jnp.where(keep, pltpu.roll(tr, shift, axis=0), 0.0)
        pi = jnp.where(keep, pltpu.roll(ti, shift, axis=0), 0.0)
        tr, ti = tr + ar * pr - ai * pi, ti + ar * pi + ai * pr
        idx, s = idx + 1, 2 * s
    return tr, ti


def _ssm_fwd(u, wb_re, wb_im, wc_re, wc_im, d_skip, tabs):
    l_dim = u.shape[0]
    nb = l_dim // BLK
    g = BLK // GRP

    def body(u_ref, wbr_ref, wbi_ref, wcr_ref, wci_ref, d_ref, a1r_ref, a1i_ref, sqr_ref, sqi_ref,
             seqr_ref, seqi_ref, y_ref, z_ref, sr_ref, si_ref, cr_ref, ci_ref, xr_ref, xi_ref):
        @pl.when(pl.program_id(0) == 0)
        def _():
            cr_ref[...] = jnp.zeros_like(cr_ref)
            ci_ref[...] = jnp.zeros_like(ci_ref)

        row = lax.broadcasted_iota(jnp.int32, (g, ST_KB), 0)
        for kb in range(SSM_KB):
            ch = slice(BLK * kb, BLK * kb + BLK)
            lanes = slice(ST_KB * kb, ST_KB * kb + ST_KB)
            u_kb = u_ref[:, ch]
            ub = u_kb.astype(BF16)
            xr = jnp.dot(ub, wbr_ref[kb], preferred_element_type=F32)
            xi = jnp.dot(ub, wbi_ref[kb], preferred_element_type=F32)
            base = LB_KB * kb
            for i in range(LB_KB):
                xr_ref[base + i] = xr[:, BLK * i:BLK * i + BLK]
                xi_ref[base + i] = xi[:, BLK * i:BLK * i + BLK]
            ar, ai = a1r_ref[0:1, lanes], a1i_ref[0:1, lanes]
            tr, ti = _group_totals(xr_ref, xi_ref, base, ar, ai, reverse=False)
            tr, ti = _carry_scan(tr, ti, sqr_ref, sqi_ref, lanes, 1.0, reverse=False)
            cin_r, cin_i = cr_ref[0:1, lanes], ci_ref[0:1, lanes]
            qr, qi = seqr_ref[:, lanes], seqi_ref[:, lanes]
            tr, ti = tr + qr * cin_r - qi * cin_i, ti + qr * cin_i + qi * cin_r
            cr_ref[0:1, lanes] = jnp.sum(jnp.where(row == g - 1, tr, 0.0), axis=0, keepdims=True)
            ci_ref[0:1, lanes] = jnp.sum(jnp.where(row == g - 1, ti, 0.0), axis=0, keepdims=True)
            pr = jnp.where(row == 0, cin_r, pltpu.roll(tr, 1, axis=0))
            pi = jnp.where(row == 0, cin_i, pltpu.roll(ti, 1, axis=0))
            for r in range(GRP):
                pr, pi = (_slab(xr_ref, base, r, g) + ar * pr - ai * pi,
                          _slab(xi_ref, base, r, g) + ar * pi + ai * pr)
                _slab_store(sr_ref, base, r, g, pr)
                _slab_store(si_ref, base, r, g, pi)
            s_r = jnp.concatenate([sr_ref[LB_KB * kb + i] for i in range(LB_KB)], axis=1)
            s_i = jnp.concatenate([si_ref[LB_KB * kb + i] for i in range(LB_KB)], axis=1)
            y = (jnp.dot(s_r.astype(BF16), wcr_ref[kb], preferred_element_type=F32)
                 - jnp.dot(s_i.astype(BF16), wci_ref[kb], preferred_element_type=F32)
                 + d_ref[:, ch] * u_kb)
            y_ref[:, ch] = y
            z_ref[:, ch] = _gelu(y).astype(BF16)

    wb_spec = _full_spec((SSM_KB, BLK, ST_KB))
    wc_spec = _full_spec((SSM_KB, ST_KB, BLK))
    tab_specs = [_full_spec(t.shape) for t in tabs[:6]]
    state_spec = pl.BlockSpec((N_LB, BLK, BLK), lambda i: (0, i, 0))
    state_shape = jax.ShapeDtypeStruct((N_LB, l_dim, BLK), F32)
    return pl.pallas_call(
        body, name="ssm_fwd", grid=(nb,),
        in_specs=[_row_spec(BLK, D_MODEL), wb_spec, wb_spec, wc_spec, wc_spec, _full_spec((1, D_MODEL))]
        + tab_specs,
        out_specs=[_row_spec(BLK, D_MODEL), _row_spec(BLK, D_MODEL), state_spec, state_spec],
        out_shape=[jax.ShapeDtypeStruct((l_dim, D_MODEL), F32), jax.ShapeDtypeStruct((l_dim, D_MODEL), BF16),
                   state_shape, state_shape],
        scratch_shapes=[pltpu.VMEM((8, N_STATE), F32), pltpu.VMEM((8, N_STATE), F32),
                        pltpu.VMEM((N_LB, BLK, BLK), F32), pltpu.VMEM((N_LB, BLK, BLK), F32)],
        compiler_params=_params(),
    )(u, wb_re, wb_im, wc_re, wc_im, d_skip, *tabs[:6])


def _ssm_bwd(dz, y, u, s_re, s_im, wb_re, wb_im, wc_re, wc_im, d_skip, tabs):
    l_dim = u.shape[0]
    nb = l_dim // BLK
    g = BLK // GRP

    def body(dz_ref, y_ref, u_ref, sr_ref, si_ref, wbr_ref, wbi_ref, wcr_ref, wci_ref, d_ref,
             a1r_ref, a1i_ref, sqr_ref, sqi_ref, revr_ref, revi_ref,
             du_ref, dd_ref, dar_ref, dai_ref, dwbr_ref, dwbi_ref, dwcr_ref, dwci_ref,
             cr_ref, ci_ref, gr_ref, gi_ref):
        @pl.when(pl.program_id(0) == 0)
        def _():
            for r in (cr_ref, ci_ref, dd_ref, dar_ref, dai_ref, dwbr_ref, dwbi_ref, dwcr_ref, dwci_ref):
                r[...] = jnp.zeros_like(r)

        tn = (((0,), (0,)), ((), ()))
        nt = (((1,), (1,)), ((), ()))
        row = lax.broadcasted_iota(jnp.int32, (g, ST_KB), 0)
        for kb in range(SSM_KB):
            ch = slice(BLK * kb, BLK * kb + BLK)
            lanes = slice(ST_KB * kb, ST_KB * kb + ST_KB)
            u_kb = u_ref[:, ch]
            dy = dz_ref[:, ch] * _gelu_grad(y_ref[:, ch])
            dyb = dy.astype(BF16)
            ub = u_kb.astype(BF16)
            dd_ref[:, ch] += jnp.sum(dy * u_kb, axis=0, keepdims=True)
            ds_r = lax.dot_general(dyb, wcr_ref[kb], nt, preferred_element_type=F32)
            ds_i = -lax.dot_general(dyb, wci_ref[kb], nt, preferred_element_type=F32)
            base = LB_KB * kb
            for i in range(LB_KB):
                gr_ref[base + i] = ds_r[:, BLK * i:BLK * i + BLK]
                gi_ref[base + i] = ds_i[:, BLK * i:BLK * i + BLK]
            ar, ai = a1r_ref[0:1, lanes], -a1i_ref[0:1, lanes]
            tr, ti = _group_totals(gr_ref, gi_ref, base, ar, ai, reverse=True)
            tr, ti = _carry_scan(tr, ti, sqr_ref, sqi_ref, lanes, -1.0, reverse=True)
            cin_r, cin_i = cr_ref[0:1, lanes], ci_ref[0:1, lanes]
            qr, qi = revr_ref[:, lanes], -revi_ref[:, lanes]
            tr, ti = tr + qr * cin_r - qi * cin_i, ti + qr * cin_i + qi * cin_r
            cr_ref[0:1, lanes] = jnp.sum(jnp.where(row == 0, tr, 0.0), axis=0, keepdims=True)
            ci_ref[0:1, lanes] = jnp.sum(jnp.where(row == 0, ti, 0.0), axis=0, keepdims=True)
            nr = jnp.where(row == g - 1, cin_r, pltpu.roll(tr, g - 1, axis=0))
            ni = jnp.where(row == g - 1, cin_i, pltpu.roll(ti, g - 1, axis=0))
            acc_r = jnp.zeros((g, ST_KB), F32)
            acc_i = jnp.zeros((g, ST_KB), F32)
            for r in range(GRP - 1, -1, -1):
                s_r, s_i = _slab(sr_ref, base, r, g), _slab(si_ref, base, r, g)
                acc_r = acc_r + (nr * s_r + ni * s_i)
                acc_i = acc_i + (ni * s_r - nr * s_i)
                nr, ni = (_slab(gr_ref, base, r, g) + ar * nr - ai * ni,
                          _slab(gi_ref, base, r, g) + ar * ni + ai * nr)
                _slab_store(gr_ref, base, r, g, nr)
                _slab_store(gi_ref, base, r, g, ni)
            dar_ref[:, lanes] += jnp.sum(acc_r, axis=0, keepdims=True)
            dai_ref[:, lanes] += jnp.sum(acc_i, axis=0, keepdims=True)
            grb = jnp.concatenate([gr_ref[base + i] for i in range(LB_KB)], axis=1).astype(BF16)
            gib = jnp.concatenate([gi_ref[base + i] for i in range(LB_KB)], axis=1).astype(BF16)
            srb = jnp.concatenate([sr_ref[base + i] for i in range(LB_KB)], axis=1).astype(BF16)
            sib = jnp.concatenate([si_ref[base + i] for i in range(LB_KB)], axis=1).astype(BF16)
            du = (lax.dot_general(grb, wbr_ref[kb], nt, preferred_element_type=F32)
                  + lax.dot_general(gib, wbi_ref[kb], nt, preferred_element_type=F32)
                  + d_ref[:, ch] * dy)
            du_ref[:, ch] = du.astype(BF16)
            dwbr_ref[kb] += lax.dot_general(ub, grb, tn, preferred_element_type=F32)
            dwbi_ref[kb] += lax.dot_general(ub, gib, tn, preferred_element_type=F32)
            dwcr_ref[kb] += lax.dot_general(srb, dyb, tn, preferred_element_type=F32)
            dwci_ref[kb] -= lax.dot_general(sib, dyb, tn, preferred_element_type=F32)

    rev = lambda i: nb - 1 - i
    wb_spec = _full_spec((SSM_KB, BLK, ST_KB))
    wc_spec = _full_spec((SSM_KB, ST_KB, BLK))
    tab_in = [tabs[0], tabs[1], tabs[2], tabs[3], tabs[6], tabs[7]]
    tab_specs = [_full_spec(t.shape) for t in tab_in]
    vec = _full_spec((1, D_MODEL))
    svec = _full_spec((1, N_STATE))
    state_spec = pl.BlockSpec((N_LB, BLK, BLK), lambda i: (0, nb - 1 - i, 0))
    return pl.pallas_call(
        body, name="ssm_bwd", grid=(nb,),
        in_specs=[_row_spec(BLK, D_MODEL, rev), _row_spec(BLK, D_MODEL, rev), _row_spec(BLK, D_MODEL, rev),
                  state_spec, state_spec,
                  wb_spec, wb_spec, wc_spec, wc_spec, vec] + tab_specs,
        out_specs=[_row_spec(BLK, D_MODEL, rev), vec, svec, svec, wb_spec, wb_spec, wc_spec, wc_spec],
        out_shape=[jax.ShapeDtypeStruct((l_dim, D_MODEL), BF16), jax.ShapeDtypeStruct((1, D_MODEL), F32),
                   jax.ShapeDtypeStruct((1, N_STATE), F32), jax.ShapeDtypeStruct((1, N_STATE), F32),
                   jax.ShapeDtypeStruct((SSM_KB, BLK, ST_KB), F32), jax.ShapeDtypeStruct((SSM_KB, BLK, ST_KB), F32),
                   jax.ShapeDtypeStruct((SSM_KB, ST_KB, BLK), F32), jax.ShapeDtypeStruct((SSM_KB, ST_KB, BLK), F32)],
        scratch_shapes=[pltpu.VMEM((8, N_STATE), F32), pltpu.VMEM((8, N_STATE), F32),
                        pltpu.VMEM((N_LB, BLK, BLK), F32), pltpu.VMEM((N_LB, BLK, BLK), F32)],
        compiler_params=_params(),
    )(dz, y, u, s_re, s_im, wb_re, wb_im, wc_re, wc_im, d_skip, *tab_in)


def _lane_slab(ref, i, r, g):
    return ref[i, _strided(r, g), :]


def _chunk_carries(xr_ref, xi_ref, i, ar, ai, sqr_ref, sqi_ref, seq_r, seq_i, cin_r, cin_i, sign, reverse):
    g = xr_ref.shape[1] // GRP
    sr = si = None
    for r in (range(GRP - 1, -1, -1) if reverse else range(GRP)):
        xr, xi = _lane_slab(xr_ref, i, r, g), _lane_slab(xi_ref, i, r, g)
        if sr is not None:
            xr, xi = xr + ar * sr - ai * si, xi + ar * si + ai * sr
        sr, si = xr, xi
    row = lax.broadcasted_iota(jnp.int32, sr.shape, 0)
    idx, s = 0, 1
    while s < g:
        br = sqr_ref[idx:idx + 1, BLK * i:BLK * i + BLK]
        bi = sign * sqi_ref[idx:idx + 1, BLK * i:BLK * i + BLK]
        shift, keep = (g - s, row < g - s) if reverse else (s, row >= s)
        pr = jnp.where(keep, pltpu.roll(sr, shift, axis=0), 0.0)
        pi = jnp.where(keep, pltpu.roll(si, shift, axis=0), 0.0)
        sr, si = sr + br * pr - bi * pi, si + br * pi + bi * pr
        idx, s = idx + 1, 2 * s
    return sr + seq_r * cin_r - seq_i * cin_i, si + seq_r * cin_i + seq_i * cin_r


def _ssm_fwd_kb(u, wb_re, wb_im, wc_re, wc_im, d_skip, tabs, q):
    l_dim = u.shape[0]
    nc = l_dim // q
    g = q // GRP

    def body(u_ref, wbr_ref, wbi_ref, wcr_ref, wci_ref, d_ref, a1r_ref, a1i_ref, sqr_ref, sqi_ref,
             seqr_ref, seqi_ref, y_ref, z_ref, sr_ref, si_ref, cr_ref, ci_ref, xr_ref, xi_ref):
        @pl.when(pl.program_id(1) == 0)
        def _():
            cr_ref[...] = jnp.zeros_like(cr_ref)
            ci_ref[...] = jnp.zeros_like(ci_ref)

        u_kb = u_ref[...]
        ub = u_kb.astype(BF16)
        xr = jnp.dot(ub, wbr_ref[0], preferred_element_type=F32)
        xi = jnp.dot(ub, wbi_ref[0], preferred_element_type=F32)
        for i in range(LB_KB):
            xr_ref[i] = xr[:, BLK * i:BLK * i + BLK]
            xi_ref[i] = xi[:, BLK * i:BLK * i + BLK]
        row = lax.broadcasted_iota(jnp.int32, (g, BLK), 0)
        for i in range(LB_KB):
            lanes = slice(BLK * i, BLK * i + BLK)
            ar, ai = a1r_ref[0:1, lanes], a1i_ref[0:1, lanes]
            cin_r, cin_i = cr_ref[0:1, lanes], ci_ref[0:1, lanes]
            tr, ti = _chunk_carries(xr_ref, xi_ref, i, ar, ai, sqr_ref, sqi_ref, seqr_ref[:, lanes],
                                    seqi_ref[:, lanes], cin_r, cin_i, 1.0, reverse=False)
            cr_ref[0:1, lanes] = jnp.sum(jnp.where(row == g - 1, tr, 0.0), axis=0, keepdims=True)
            ci_ref[0:1, lanes] = jnp.sum(jnp.where(row == g - 1, ti, 0.0), axis=0, keepdims=True)
            pr = jnp.where(row == 0, cin_r, pltpu.roll(tr, 1, axis=0))
            pi = jnp.where(row == 0, cin_i, pltpu.roll(ti, 1, axis=0))
            for r in range(GRP):
                pr, pi = (_lane_slab(xr_ref, i, r, g) + ar * pr - ai * pi,
                          _lane_slab(xi_ref, i, r, g) + ar * pi + ai * pr)
                sr_ref[i, _strided(r, g), :] = pr
                si_ref[i, _strided(r, g), :] = pi
        s_r = jnp.concatenate([sr_ref[i] for i in range(LB_KB)], axis=1)
        s_i = jnp.concatenate([si_ref[i] for i in range(LB_KB)], axis=1)
        y = (jnp.dot(s_r.astype(BF16), wcr_ref[0], preferred_element_type=F32)
             - jnp.dot(s_i.astype(BF16), wci_ref[0], preferred_element_type=F32)
             + d_ref[...] * u_kb)
        y_ref[...] = y.astype(BF16)
        z_ref[...] = _gelu(y).astype(BF16)

    chan = pl.BlockSpec((q, BLK), lambda k, c: (c, k))
    wb_spec = pl.BlockSpec((1, BLK, ST_KB), lambda k, c: (k, 0, 0))
    wc_spec = pl.BlockSpec((1, ST_KB, BLK), lambda k, c: (k, 0, 0))
    tab_specs = [pl.BlockSpec((t.shape[0], ST_KB), lambda k, c: (0, k)) for t in tabs[:6]]
    state_spec = pl.BlockSpec((LB_KB, q, BLK), lambda k, c: (k, c, 0))
    state_shape = jax.ShapeDtypeStruct((N_LB, l_dim, BLK), F32)
    return pl.pallas_call(
        body, name="ssm_fwd", grid=(SSM_KB, nc),
        in_specs=[chan, wb_spec, wb_spec, wc_spec, wc_spec, pl.BlockSpec((1, BLK), lambda k, c: (0, k))] + tab_specs,
        out_specs=[chan, chan, state_spec, state_spec],
        out_shape=[jax.ShapeDtypeStruct((l_dim, D_MODEL), BF16), jax.ShapeDtypeStruct((l_dim, D_MODEL), BF16),
                   state_shape, state_shape],
        scratch_shapes=[pltpu.VMEM((8, ST_KB), F32), pltpu.VMEM((8, ST_KB), F32),
                        pltpu.VMEM((LB_KB, q, BLK), F32), pltpu.VMEM((LB_KB, q, BLK), F32)],
        compiler_params=_params(dimension_semantics=("parallel", "arbitrary")),
    )(u, wb_re, wb_im, wc_re, wc_im, d_skip, *tabs[:6])


def _ssm_bwd_kb(dz, y, u, s_re, s_im, wb_re, wb_im, wc_re, wc_im, d_skip, tabs, q):
    l_dim = u.shape[0]
    nc = l_dim // q
    g = q // GRP

    def body(dz_ref, y_ref, u_ref, sr_ref, si_ref, wbr_ref, wbi_ref, wcr_ref, wci_ref, d_ref,
             a1r_ref, a1i_ref, sqr_ref, sqi_ref, revr_ref, revi_ref,
             du_ref, dd_ref, dar_ref, dai_ref, dwbr_ref, dwbi_ref, dwcr_ref, dwci_ref,
             cr_ref, ci_ref, gr_ref, gi_ref):
        @pl.when(pl.program_id(1) == 0)
        def _():
            for ref in (cr_ref, ci_ref, dd_ref, dar_ref, dai_ref, dwbr_ref, dwbi_ref, dwcr_ref, dwci_ref):
                ref[...] = jnp.zeros_like(ref)

        tn = (((0,), (0,)), ((), ()))
        nt = (((1,), (1,)), ((), ()))
        u_kb = u_ref[...]
        dy = dz_ref[...] * _gelu_grad(y_ref[...].astype(F32))
        dyb = dy.astype(BF16)
        ub = u_kb.astype(BF16)
        dd_ref[...] += jnp.sum(dy * u_kb, axis=0, keepdims=True)
        ds_r = lax.dot_general(dyb, wcr_ref[0], nt, preferred_element_type=F32)
        ds_i = -lax.dot_general(dyb, wci_ref[0], nt, preferred_element_type=F32)
        for i in range(LB_KB):
            gr_ref[i] = ds_r[:, BLK * i:BLK * i + BLK]
            gi_ref[i] = ds_i[:, BLK * i:BLK * i + BLK]
        row = lax.broadcasted_iota(jnp.int32, (g, BLK), 0)
        for i in range(LB_KB):
            lanes = slice(BLK * i, BLK * i + BLK)
            ar, ai = a1r_ref[0:1, lanes], -a1i_ref[0:1, lanes]
            cin_r, cin_i = cr_ref[0:1, lanes], ci_ref[0:1, lanes]
            tr, ti = _chunk_carries(gr_ref, gi_ref, i, ar, ai, sqr_ref, sqi_ref, revr_ref[:, lanes],
                                    -revi_ref[:, lanes], cin_r, cin_i, -1.0, reverse=True)
            cr_ref[0:1, lanes] = jnp.sum(jnp.where(row == 0, tr, 0.0), axis=0, keepdims=True)
            ci_ref[0:1, lanes] = jnp.sum(jnp.where(row == 0, ti, 0.0), axis=0, keepdims=True)
            nr = jnp.where(row == g - 1, cin_r, pltpu.roll(tr, g - 1, axis=0))
            ni = jnp.where(row == g - 1, cin_i, pltpu.roll(ti, g - 1, axis=0))
            acc_r = jnp.zeros((g, BLK), F32)
            acc_i = jnp.zeros((g, BLK), F32)
            for r in range(GRP - 1, -1, -1):
                s_r, s_i = _lane_slab(sr_ref, i, r, g), _lane_slab(si_ref, i, r, g)
                acc_r = acc_r + (nr * s_r + ni * s_i)
                acc_i = acc_i + (ni * s_r - nr * s_i)
                nr, ni = (_lane_slab(gr_ref, i, r, g) + ar * nr - ai * ni,
                          _lane_slab(gi_ref, i, r, g) + ar * ni + ai * nr)
                gr_ref[i, _strided(r, g), :] = nr
                gi_ref[i, _strided(r, g), :] = ni
            dar_ref[:, lanes] += jnp.sum(acc_r, axis=0, keepdims=True)
            dai_ref[:, lanes] += jnp.sum(acc_i, axis=0, keepdims=True)
        grb = jnp.concatenate([gr_ref[i] for i in range(LB_KB)], axis=1).astype(BF16)
        gib = jnp.concatenate([gi_ref[i] for i in range(LB_KB)], axis=1).astype(BF16)
        srb = jnp.concatenate([sr_ref[i] for i in range(LB_KB)], axis=1).astype(BF16)
        sib = jnp.concatenate([si_ref[i] for i in range(LB_KB)], axis=1).astype(BF16)
        du = (lax.dot_general(grb, wbr_ref[0], nt, preferred_element_type=F32)
              + lax.dot_general(gib, wbi_ref[0], nt, preferred_element_type=F32)
              + d_ref[...] * dy)
        du_ref[...] = du.astype(BF16)
        dwbr_ref[0] += lax.dot_general(ub, grb, tn, preferred_element_type=F32)
        dwbi_ref[0] += lax.dot_general(ub, gib, tn, preferred_element_type=F32)
        dwcr_ref[0] += lax.dot_general(srb, dyb, tn, preferred_element_type=F32)
        dwci_ref[0] -= lax.dot_general(sib, dyb, tn, preferred_element_type=F32)

    chan = pl.BlockSpec((q, BLK), lambda k, c: (nc - 1 - c, k))
    wb_spec = pl.BlockSpec((1, BLK, ST_KB), lambda k, c: (k, 0, 0))
    wc_spec = pl.BlockSpec((1, ST_KB, BLK), lambda k, c: (k, 0, 0))
    tab_in = [tabs[0], tabs[1], tabs[2], tabs[3], tabs[6], tabs[7]]
    tab_specs = [pl.BlockSpec((t.shape[0], ST_KB), lambda k, c: (0, k)) for t in tab_in]
    vec = pl.BlockSpec((1, BLK), lambda k, c: (0, k))
    svec = pl.BlockSpec((1, ST_KB), lambda k, c: (0, k))
    state_spec = pl.BlockSpec((LB_KB, q, BLK), lambda k, c: (k, nc - 1 - c, 0))
    return pl.pallas_call(
        body, name="ssm_bwd", grid=(SSM_KB, nc),
        in_specs=[chan, chan, chan, state_spec, state_spec, wb_spec, wb_spec, wc_spec, wc_spec, vec] + tab_specs,
        out_specs=[chan, vec, svec, svec, wb_spec, wb_spec, wc_spec, wc_spec],
        out_shape=[jax.ShapeDtypeStruct((l_dim, D_MODEL), BF16), jax.ShapeDtypeStruct((1, D_MODEL), F32),
                   jax.ShapeDtypeStruct((1, N_STATE), F32), jax.ShapeDtypeStruct((1, N_STATE), F32),
                   jax.ShapeDtypeStruct((SSM_KB, BLK, ST_KB), F32), jax.ShapeDtypeStruct((SSM_KB, BLK, ST_KB), F32),
                   jax.ShapeDtypeStruct((SSM_KB, ST_KB, BLK), F32), jax.ShapeDtypeStruct((SSM_KB, ST_KB, BLK), F32)],
        scratch_shapes=[pltpu.VMEM((8, ST_KB), F32), pltpu.VMEM((8, ST_KB), F32),
                        pltpu.VMEM((LB_KB, q, BLK), F32), pltpu.VMEM((LB_KB, q, BLK), F32)],
        compiler_params=_params(dimension_semantics=("parallel", "arbitrary")),
    )(dz, y, u, s_re, s_im, wb_re, wb_im, wc_re, wc_im, d_skip, *tab_in)


def _discretize(lam_re, lam_im, log_dt, b_re, b_im):
    dt = jnp.exp(log_dt)[:, None]
    mag = jnp.exp(lam_re * dt)
    ar, ai = mag * jnp.cos(lam_im * dt), mag * jnp.sin(lam_im * dt)
    den = lam_re * lam_re + lam_im * lam_im
    nr, ni = ar - 1.0, ai
    fr, fi = (nr * lam_re + ni * lam_im) / den, (ni * lam_re - nr * lam_im) / den
    bbar_re = fr[..., None] * b_re - fi[..., None] * b_im
    bbar_im = fr[..., None] * b_im + fi[..., None] * b_re
    return ar, ai, bbar_re, bbar_im


def _block_diag_b(bbar):
    eye = jnp.eye(8, dtype=bbar.dtype)
    return jnp.einsum("kgpc,gh->kgchp", bbar.reshape(8, 8, SSM_STATE, SSM_GROUP_CH), eye).reshape(8, BLK, ST_KB)


def _block_diag_b_t(dwb):
    eye = jnp.eye(8, dtype=dwb.dtype)
    return jnp.einsum("kgchp,gh->kgpc", dwb.reshape(8, 8, SSM_GROUP_CH, 8, SSM_STATE), eye).reshape(
        SSM_GROUPS, SSM_STATE, SSM_GROUP_CH)


def _block_diag_c(c):
    eye = jnp.eye(8, dtype=c.dtype)
    return jnp.einsum("kgcp,gh->kgphc", c.reshape(8, 8, SSM_GROUP_CH, SSM_STATE), eye).reshape(8, ST_KB, BLK)


def _block_diag_c_t(dwc):
    eye = jnp.eye(8, dtype=dwc.dtype)
    return jnp.einsum("kgphc,gh->kgcp", dwc.reshape(8, 8, SSM_STATE, 8, SSM_GROUP_CH), eye).reshape(
        SSM_GROUPS, SSM_GROUP_CH, SSM_STATE)


def _powers(br, bi, n):
    pr, pi = br, bi
    cr, ci = br, bi
    while pr.shape[0] < n:
        pr, pi = (jnp.concatenate([pr, pr * cr - pi * ci], axis=0),
                  jnp.concatenate([pi, pr * ci + pi * cr], axis=0))
        cr, ci = cr * cr - ci * ci, 2.0 * cr * ci
    return pr[:n], pi[:n]


def _powers_desc(br, bi, n):
    pr, pi = br, bi
    cr, ci = br, bi
    while pr.shape[0] < n:
        pr, pi = (jnp.concatenate([pr * cr - pi * ci, pr], axis=0),
                  jnp.concatenate([pr * ci + pi * cr, pi], axis=0))
        cr, ci = cr * cr - ci * ci, 2.0 * cr * ci
    return pr, pi


def _power_tables(ar, ai, g):
    a1r, a1i = _powers(ar, ai, GRP)
    seqr, seqi = _powers(a1r[GRP - 1:], a1i[GRP - 1:], g)
    g2 = 1 << (g - 1).bit_length()
    revr, revi = _powers_desc(a1r[GRP - 1:], a1i[GRP - 1:], g2)
    revr, revi = revr[g2 - g:], revi[g2 - g:]
    sq_r, sq_i = [seqr[0:1]], [seqi[0:1]]
    while len(sq_r) < 8:
        r, i = sq_r[-1], sq_i[-1]
        sq_r.append(r * r - i * i)
        sq_i.append(2.0 * r * i)
    sqr, sqi = jnp.concatenate(sq_r, axis=0), jnp.concatenate(sq_i, axis=0)
    return a1r, a1i, sqr, sqi, seqr, seqi, revr, revi


HBM_SPEC = pl.BlockSpec(memory_space=pltpu.HBM)
SEM_SPEC = pl.BlockSpec(memory_space=pltpu.SEMAPHORE)
DATAFLOW = pltpu.SideEffectType.DATAFLOW_SIDE_EFFECTING


def _plain_rows(p, m):
    return p * m


def _ffn_in_rows(p, m):
    return ((p & 3) >> 1) * (4 * m) + (p >> 2) * (2 * m) + (p & 1) * m


def _peer_copies(src_refs, land_refs, send_sems, recv_sems, chunked, row_fns):
    x, y, c = lax.axis_index("x"), lax.axis_index("y"), lax.axis_index("c")
    me = 4 * x + 2 * y + c
    copies = []
    for a, (src, land) in enumerate(zip(src_refs, land_refs)):
        m = land.shape[0] // N_DEV
        for k in range(N_DEV - 1):
            rel = k + 1
            bx, by, bc = (rel >> 2) & 1, (rel >> 1) & 1, rel & 1
            peer = (x + bx - 2 * x * bx, y + by - 2 * y * by, c + bc - 2 * c * bc)
            p_idx = 4 * peer[0] + 2 * peer[1] + peer[2]
            copies.append(pltpu.make_async_remote_copy(
                src_ref=src.at[pl.ds(row_fns[a](p_idx, m), m), :] if chunked else src,
                dst_ref=land.at[pl.ds(me * m if chunked else row_fns[a](me, m), m), :],
                send_sem=send_sems[a * (N_DEV - 1) + k], recv_sem=recv_sems[a * (N_DEV - 1) + k],
                device_id=peer, device_id_type=MESH))
    return copies


def _send_start(name, srcs, lands, chunked, row_fns=None):
    n = len(srcs)
    ns = n * (N_DEV - 1)
    row_fns = row_fns or [_plain_rows] * n

    def body(*refs):
        src_refs, land_refs = refs[:n], refs[n:2 * n]
        send_sems, recv_sems = refs[2 * n:2 * n + ns], refs[2 * n + ns:2 * n + 2 * ns]
        token = refs[-1]
        for cp in _peer_copies(src_refs, land_refs, send_sems, recv_sems, chunked, row_fns):
            cp.start()
        token[...] = jnp.zeros_like(token)

    ins = [pltpu.with_memory_space_constraint(a, pltpu.HBM) for a in list(srcs) + list(lands)]
    out = pl.pallas_call(
        body, name=name,
        in_specs=[HBM_SPEC] * (2 * n),
        out_specs=[SEM_SPEC] * (2 * ns) + [HBM_SPEC] * (2 * n) + [pl.BlockSpec(memory_space=pltpu.VMEM)],
        out_shape=[pltpu.SemaphoreType.DMA(())] * (2 * ns)
        + [pltpu.HBM(a.shape, a.dtype) for a in list(srcs) + list(lands)]
        + [jax.ShapeDtypeStruct((8, BLK), F32)],
        input_output_aliases={i: i + 2 * ns for i in range(2 * n)},
        compiler_params=pltpu.CompilerParams(has_side_effects=DATAFLOW),
    )(*ins)
    return out[:ns], out[ns:2 * ns], out[2 * ns:2 * ns + n], out[2 * ns + n:2 * ns + 2 * n], out[-1]


def _send_wait(name, send_sems, recv_sems, srcs, lands, after, chunked, row_fns=None):
    n = len(srcs)
    ns = n * (N_DEV - 1)
    row_fns = row_fns or [_plain_rows] * n

    def body(*refs):
        src_refs, land_refs = refs[:n], refs[n:2 * n]
        s_sems, r_sems = refs[2 * n:2 * n + ns], refs[2 * n + ns:2 * n + 2 * ns]
        copies = _peer_copies(src_refs, land_refs, s_sems, r_sems, chunked, row_fns)
        for cp in copies:
            cp.wait_send()
        for cp in copies:
            cp.wait_recv()

    out = pl.pallas_call(
        body, name=name,
        in_specs=[HBM_SPEC] * (2 * n) + [SEM_SPEC] * (2 * ns) + [pl.BlockSpec(memory_space=pl.ANY)],
        out_specs=[HBM_SPEC] * (2 * n),
        out_shape=[pltpu.HBM(a.shape, a.dtype) for a in list(srcs) + list(lands)],
        input_output_aliases={i: i for i in range(2 * n)},
        compiler_params=pltpu.CompilerParams(has_side_effects=DATAFLOW),
    )(*srcs, *lands, *send_sems, *recv_sems, after)
    return out[n:]


def _sum_slots(name, recv, own):
    m, ncol = own.shape
    tr = m // 2 if (m // 2) % 16 == 0 else m
    g = m // tr

    def body(*refs):
        slots, own_ref, o_ref = refs[:N_DEV], refs[N_DEV], refs[N_DEV + 1]
        me = _my_index()
        tot = None
        for s in range(N_DEV):
            v = jnp.where(me == s, own_ref[...], slots[s][...].astype(F32))
            tot = v if tot is None else tot + v
        o_ref[...] = tot

    def slot_spec(s):
        return pl.BlockSpec((tr, ncol), lambda i: (s * g + i, 0))

    return pl.pallas_call(
        body, name=name, grid=(g,),
        in_specs=[slot_spec(s) for s in range(N_DEV)] + [pl.BlockSpec((tr, ncol), lambda i: (i, 0))],
        out_specs=pl.BlockSpec((tr, ncol), lambda i: (i, 0)),
        out_shape=jax.ShapeDtypeStruct((m, ncol), F32),
        compiler_params=_params(),
    )(*([recv] * N_DEV), own)


def _sum_gathered(name, gathered, rows):
    tr = _pick(rows, 512, 8)
    g = rows // tr

    def body(*refs):
        o_ref = refs[N_DEV]
        tot = refs[0][...]
        for s in range(1, N_DEV):
            tot = tot + refs[s][...]
        o_ref[...] = tot

    return pl.pallas_call(
        body, name=name, grid=(g,),
        in_specs=[pl.BlockSpec((tr, BLK), (lambda i, s=s: (s * g + i, 0))) for s in range(N_DEV)],
        out_specs=pl.BlockSpec((tr, BLK), lambda i: (i, 0)),
        out_shape=jax.ShapeDtypeStruct((rows, BLK), F32),
        compiler_params=_params(),
    )(*([gathered] * N_DEV))


def _adamw(name, w, g, m, v):
    r, c = w.shape
    tr = _pick(r, 256, 8) if r % 8 == 0 else r
    c1 = 1.0 - ADAM_B1 ** ADAM_STEP
    c2 = 1.0 - ADAM_B2 ** ADAM_STEP

    def body(w_ref, g_ref, m_ref, v_ref, d_ref, nm_ref, nv_ref):
        gv = g_ref[...]
        nm = ADAM_B1 * m_ref[...] + (1.0 - ADAM_B1) * gv
        nv = ADAM_B2 * v_ref[...] + (1.0 - ADAM_B2) * (gv * gv)
        m_hat = nm / c1
        v_hat = nv / c2
        d_ref[...] = -ADAM_LR * (m_hat / (jnp.sqrt(v_hat) + ADAM_EPS) + ADAM_WD * w_ref[...])
        nm_ref[...] = nm
        nv_ref[...] = nv

    spec = pl.BlockSpec((tr, c), lambda i: (i, 0))
    shape = jax.ShapeDtypeStruct((r, c), F32)
    return pl.pallas_call(
        body, name=name, grid=(r // tr,),
        in_specs=[spec] * 4, out_specs=[spec] * 3, out_shape=[shape] * 3,
        compiler_params=_params(),
    )(w, g, m, v)


def _adamw_many(name, ws, gs, ms, vs):
    n = len(ws)
    c1 = 1.0 - ADAM_B1 ** ADAM_STEP
    c2 = 1.0 - ADAM_B2 ** ADAM_STEP

    def body(*refs):
        for a in range(n):
            w_ref, g_ref, m_ref, v_ref = refs[a], refs[n + a], refs[2 * n + a], refs[3 * n + a]
            d_ref, nm_ref, nv_ref = refs[4 * n + a], refs[5 * n + a], refs[6 * n + a]
            gv = g_ref[...]
            nm = ADAM_B1 * m_ref[...] + (1.0 - ADAM_B1) * gv
            nv = ADAM_B2 * v_ref[...] + (1.0 - ADAM_B2) * (gv * gv)
            d_ref[...] = -ADAM_LR * ((nm / c1) / (jnp.sqrt(nv / c2) + ADAM_EPS) + ADAM_WD * w_ref[...])
            nm_ref[...] = nm
            nv_ref[...] = nv

    vmem = pl.BlockSpec(memory_space=pltpu.VMEM)
    shapes = [jax.ShapeDtypeStruct(w.shape, F32) for w in ws]
    out = pl.pallas_call(
        body, name=name,
        in_specs=[vmem] * (4 * n), out_specs=[vmem] * (3 * n), out_shape=shapes * 3,
        compiler_params=_params(),
    )(*ws, *gs, *ms, *vs)
    return out[:n], out[n:2 * n], out[2 * n:]


PACK_ROWS = 128


def _pack(parts):
    flat = []
    for p in parts:
        v = p.reshape(-1)
        flat.append(jnp.pad(v, (0, (-v.shape[0]) % BLK)))
    v = jnp.concatenate(flat)
    v = jnp.pad(v, (0, (-v.shape[0]) % (PACK_ROWS * BLK)))
    return v.reshape(-1, BLK)


def _unpack(buf, shapes):
    flat = buf.reshape(-1)
    out, off = [], 0
    for shp in shapes:
        size = math.prod(shp)
        out.append(flat[off:off + size].reshape(shp))
        off += size + (-size) % BLK
    return out


def kernel(x, meta_tokens, norm_mix, w_in, q_norm, k_norm, attn_sinks, lam_re, lam_im, log_dt, ssm_b_re, ssm_b_im, ssm_c_re, ssm_c_im, ssm_d, w_glu, attn_branch_norm, ssm_branch_norm, w_out, norm_ffn, w_ffn_in, w_ffn_out, loss_target, m_meta_tokens, m_norm_mix, m_w_in, m_q_norm, m_k_norm, m_attn_sinks, m_lam_re, m_lam_im, m_log_dt, m_ssm_b_re, m_ssm_b_im, m_ssm_c_re, m_ssm_c_im, m_ssm_d, m_w_glu, m_attn_branch_norm, m_ssm_branch_norm, m_w_out, m_norm_ffn, m_w_ffn_in, m_w_ffn_out, v_meta_tokens, v_norm_mix, v_w_in, v_q_norm, v_k_norm, v_attn_sinks, v_lam_re, v_lam_im, v_log_dt, v_ssm_b_re, v_ssm_b_im, v_ssm_c_re, v_ssm_c_im, v_ssm_d, v_w_glu, v_attn_branch_norm, v_ssm_branch_norm, v_w_out, v_norm_ffn, v_w_ffn_in, v_w_ffn_out):
    args = dict(locals())
    weights = {n: args[n] for n in WEIGHTS}
    mom_m = {n: args["m_" + n] for n in WEIGHTS}
    mom_v = {n: args["v_" + n] for n in WEIGHTS}

    x2d = x[0]
    target2d = loss_target[0]
    s_len = x2d.shape[0]
    l_dim = s_len + BLK
    tm_row = _pick(l_dim, 320)
    tm_mm = _pick(l_dim, 1040)
    tl_tn = _pick(l_dim, 1664)
    tm_ffn = _pick(l_dim, 640)
    tm_big = _pick(l_dim, 2080)

    shard_in = w_in[0].T.astype(BF16)
    shard_glu = w_glu[0].T.astype(BF16)
    shard_out = w_out[0].astype(BF16)
    shard_ffn_in = w_ffn_in[0].T.astype(BF16)
    shard_ffn_out = w_ffn_out[0].astype(BF16)
    shard_meta = meta_tokens.T
    me = _my_index()

    def landing(shard, row_fn=_plain_rows):
        m_rows, cols = shard.shape
        return lax.dynamic_update_slice(lax.empty((N_DEV * m_rows, cols), shard.dtype), shard,
                                        (row_fn(me, m_rows), 0))

    first = [shard_in, shard_meta]
    ga = _send_start("gather_start_a", first, [landing(s) for s in first], chunked=False)
    later = [shard_glu + ga[4][0:1, 0:1].astype(BF16), shard_out, shard_ffn_in, shard_ffn_out]
    later_fns = [_plain_rows, _plain_rows, _ffn_in_rows, _plain_rows]
    gb = _send_start("gather_start_b", later, [landing(s, f) for s, f in zip(later, later_fns)], chunked=False,
                     row_fns=later_fns)

    nm_t = norm_mix + (ga[4][0:1, 0:1] + gb[4][0:1, 0:1])
    qn_t, kn_t = jnp.tile(q_norm, (1, N_Q_HEADS)), jnp.tile(k_norm, (1, N_KV_HEADS))
    e_mat = jnp.kron(jnp.eye(4, dtype=F32), jnp.ones((HEAD_DIM, HEAD_DIM), F32)).astype(BF16)

    def disc(lr, li, ldt, br, bi):
        return _discretize(lr[0], li[0], ldt[0], br[0], bi[0])

    (abar_re, abar_im, bbar_re, bbar_im), disc_vjp = jax.vjp(disc, lam_re, lam_im, log_dt, ssm_b_re, ssm_b_im)
    wb_re, wb_im = _block_diag_b(bbar_re).astype(BF16), _block_diag_b(bbar_im).astype(BF16)
    wc_re, wc_im = _block_diag_c(ssm_c_re[0]).astype(BF16), _block_diag_c(ssm_c_im[0]).astype(BF16)
    q_ssm = _pick(l_dim, 640, 64)
    tabs = _power_tables(abar_re.reshape(1, N_STATE), abar_im.reshape(1, N_STATE), q_ssm // GRP)

    h0, xn = _embed_norm(x2d, nm_t)
    wt_in, meta_t = _send_wait("gather_wait_a", ga[0], ga[1], ga[2], ga[3], xn, chunked=False)
    meta_pad = jnp.pad(meta_t.T, ((PAD, 0), (0, 0)))
    h0, xn = _embed_meta(meta_pad, nm_t, h0, xn)
    qkv = _matmul("proj_qkv", xn, wt_in, nt=True, tm=tm_big, tn=512, tk=D_MODEL, n=Q_W + 2 * KV_W, w_off=0)
    u = _matmul("proj_u", xn, wt_in, nt=True, tm=tm_big, tn=512, tk=D_MODEL, n=D_MODEL, w_off=3)
    gates = _matmul("proj_gates", xn, wt_in, nt=True, tm=tm_big, tn=512, tk=D_MODEL, n=2 * D_MODEL, w_off=5,
                    out_dtype=BF16)
    qn, kf, vf = _qk_prep(qkv, qn_t, kn_t, e_mat, tm_row)
    attn, lse = _attn_fwd(qn, kf, vf, attn_sinks)
    y, z, s_re, s_im = _ssm_fwd_kb(u, wb_re, wb_im, wc_re, wc_im, ssm_d, tabs, q_ssm)
    wt_glu, w_out_f, wt_ffn_in, w_ffn_out_f = _send_wait("gather_wait_b", gb[0], gb[1], gb[2], gb[3], z,
                                                         chunked=False, row_fns=later_fns)
    zab = _matmul("glu_proj", z, wt_glu, nt=True, tm=tm_big, tn=1024, tk=D_MODEL, out_dtype=BF16)
    merged = _merge_fwd(attn, zab, gates, attn_branch_norm, ssm_branch_norm, tm_row)
    h1, hn = _matmul("out_proj", merged, w_out_f, nt=False, tm=tm_mm, tn=1024, tk=D_MODEL, res=h0,
                     norm_g=norm_ffn)
    gu, act = _ffn_in_swiglu(hn, wt_ffn_in, tm_ffn)
    h2 = _matmul("ffn_out", act, w_ffn_out_f, nt=False, tm=tm_mm, tn=1024, tk=1408, res=h1)
    dh2, dh2_b, loss_part = _loss_grad(h2, target2d)

    dgu = _d_act_swiglu(dh2_b, w_ffn_out_f, gu, tm_ffn)

    def exchange_start(name, grads_b, row_fns=None):
        return _send_start(name, grads_b, [jnp.zeros(g.shape, BF16) for g in grads_b], chunked=True,
                           row_fns=row_fns)

    g_ffn_out, g_ffn_out_b = _matmul_tn("g_ffn_out", act, dh2_b, tm=1408, tn=1024, tl=tl_tn)
    g_ffn_in_t, g_ffn_in_b = _matmul_tn("g_ffn_in", dgu, hn, tm=1408, tn=1024, tl=tl_tn)
    ffn_fns = [_ffn_in_rows, _plain_rows]
    ex1 = exchange_start("exchange_start_ffn", [g_ffn_in_b, g_ffn_out_b], ffn_fns)
    dhn = _matmul("d_hn", dgu, wt_ffn_in, nt=False, tm=tm_big, tn=1024, tk=1408)
    dh1, dh1_b, g_norm_ffn = _norm_bwd_res("ffn_norm_bwd", h1, norm_ffn + ex1[4][0:1, 0:1], dhn, dh2, tm_row)
    dmerged = _matmul("d_merged", dh1_b, w_out_f, nt=True, tm=tm_big, tn=1024, tk=D_MODEL)
    dattn, dzab, dgates, g_abn, g_sbn = _merge_bwd(attn, zab, gates, attn_branch_norm, ssm_branch_norm,
                                                     dmerged, tm_row)
    g_out, g_out_b = _matmul_tn("g_out", merged, dh1_b, tm=1024, tn=1024, tl=tl_tn)
    dz = _matmul("d_z", dzab, wt_glu, nt=False, tm=tm_big, tn=1024, tk=1024)
    g_glu_t, g_glu_b = _matmul_tn("g_glu", dzab, z, tm=1024, tn=1024, tl=tl_tn)
    ex2 = exchange_start("exchange_start_mix", [g_glu_b, g_out_b])
    du, g_ssm_d, g_ar, g_ai, g_wbr, g_wbi, g_wcr, g_wci = _ssm_bwd_kb(
        dz, y, u, s_re, s_im, wb_re, wb_im, wc_re, wc_im, ssm_d + ex2[4][0:1, 0:1], tabs, q_ssm)
    dq, dkc, dkp, dkm, dvc, dvp, dvm, g_sinks = _attn_bwd(qn, kf, vf, attn_sinks, lse, attn, dattn)
    dqkv, g_qn_t, g_kn_t = _qk_bwd(qkv, qn_t, kn_t, e_mat, dq, dkc, dkp, dkm, dvc, dvp, dvm)
    g_lam_re, g_lam_im, g_log_dt, g_b_re, g_b_im = disc_vjp(
        (g_ar.reshape(SSM_GROUPS, SSM_STATE), g_ai.reshape(SSM_GROUPS, SSM_STATE),
         _block_diag_b_t(g_wbr), _block_diag_b_t(g_wbi)))
    small_grads = {
        "q_norm": g_qn_t.reshape(N_Q_HEADS, HEAD_DIM).sum(0)[None],
        "k_norm": g_kn_t.reshape(N_KV_HEADS, HEAD_DIM).sum(0)[None], "attn_sinks": g_sinks,
        "lam_re": g_lam_re, "lam_im": g_lam_im, "log_dt": g_log_dt, "ssm_b_re": g_b_re, "ssm_b_im": g_b_im,
        "ssm_c_re": _block_diag_c_t(g_wcr)[None], "ssm_c_im": _block_diag_c_t(g_wci)[None],
        "ssm_d": g_ssm_d, "attn_branch_norm": g_abn, "ssm_branch_norm": g_sbn, "norm_ffn": g_norm_ffn,
    }
    early = [n for n in SMALL if n != "norm_mix"]
    packed_e = _pack([small_grads[n] for n in early])
    gs_e = _send_start("small_start_a", [packed_e], [landing(packed_e)], chunked=False)

    dproj = jnp.concatenate([dqkv, du, dgates], axis=1)
    g_in_t, g_in_b = _matmul_tn("g_in", dproj, xn, tm=1152, tn=1024, tl=tl_tn, after=gs_e[4])
    ex3 = exchange_start("exchange_start_in", [g_in_b])
    dxn = _matmul("d_xn", dproj, wt_in, nt=False, tm=tm_big, tn=1024, tk=1152)
    grad_x2d, dmeta_blk, g_norm_mix = _final_bwd(h0, nm_t + ex3[4][0:1, 0:1], dxn, dh1)
    packed_l = _pack([g_norm_mix, dmeta_blk[PAD:], loss_part])
    gs_l = _send_start("small_start_b", [packed_l], [landing(packed_l)], chunked=False)
    grads, deltas, new_m, new_v = {}, {}, {}, {}

    recv_ffn_in, recv_ffn_out = _send_wait("exchange_wait_ffn", ex1[0], ex1[1], ex1[2], ex1[3], gs_l[4],
                                           chunked=True, row_fns=ffn_fns)
    recv_glu, recv_out = _send_wait("exchange_wait_mix", ex2[0], ex2[1], ex2[2], ex2[3], recv_ffn_in,
                                    chunked=True)
    (recv_in,) = _send_wait("exchange_wait_in", ex3[0], ex3[1], ex3[2], ex3[3], recv_glu, chunked=True)
    big = [("w_in", g_in_t, True, recv_in, _plain_rows), ("w_glu", g_glu_t, True, recv_glu, _plain_rows),
           ("w_out", g_out, False, recv_out, _plain_rows), ("w_ffn_in", g_ffn_in_t, True, recv_ffn_in, _ffn_in_rows),
           ("w_ffn_out", g_ffn_out, False, recv_ffn_out, _plain_rows)]
    for name, g_full, transposed, recv, row_fn in big:
        m_rows = g_full.shape[0] // N_DEV
        own = lax.dynamic_slice(g_full, (row_fn(me, m_rows), 0), (m_rows, g_full.shape[1]))
        g_shard = _sum_slots("sum_" + name, recv, own)
        grads[name] = (g_shard.T if transposed else g_shard)[None]

    def adamw_2d(name):
        shp = weights[name].shape
        as2d = lambda a: a.reshape(shp[-2], shp[-1])
        d, nm, nv = _adamw("adamw_" + name, as2d(weights[name]), as2d(grads[name]), as2d(mom_m[name]),
                           as2d(mom_v[name]))
        deltas[name], new_m[name], new_v[name] = d.reshape(shp), nm.reshape(shp), nv.reshape(shp)
        return d

    for name in ["w_in", "w_glu", "w_out", "w_ffn_in", "w_ffn_out"]:
        last = adamw_2d(name)

    def small_sum(tag, gs, packed, after):
        (gathered,) = _send_wait("small_wait_" + tag, gs[0], gs[1], gs[2], gs[3], after, chunked=False)
        return _sum_gathered("sum_small_" + tag, gathered, packed.shape[0])

    def small_adamw(tag, names):
        d, nm, nv = _adamw_many("adamw_small_" + tag, [weights[n] for n in names], [grads[n] for n in names],
                                [mom_m[n] for n in names], [mom_v[n] for n in names])
        deltas.update(zip(names, d))
        new_m.update(zip(names, nm))
        new_v.update(zip(names, nv))

    g_sum_e = small_sum("a", gs_e, packed_e, last)
    grads.update(zip(early, _unpack(g_sum_e, [weights[n].shape for n in early])))
    wide = [n for n in early if n.startswith(("ssm_b", "ssm_c"))]
    small_adamw("wide", wide)
    g_sum_l = small_sum("b", gs_l, packed_l, g_sum_e)
    grads["norm_mix"], g_meta, loss_sum = _unpack(g_sum_l, [weights["norm_mix"].shape, (N_META, D_MODEL), (1, 1)])
    small_adamw("rest", [n for n in SMALL if n not in wide])
    grads["meta_tokens"] = lax.dynamic_slice(g_meta, (0, me * BLK), (N_META, BLK))
    adamw_2d("meta_tokens")

    loss = loss_sum[0, 0]
    return (loss, grad_x2d[None], *[grads[n] for n in WEIGHTS], *[deltas[n] for n in WEIGHTS],
            *[new_m[n] for n in WEIGHTS], *[new_v[n] for n in WEIGHTS])
```

```python
import math

import jax
import jax.numpy as jnp
from jax import lax
from jax.experimental import pallas as pl
from jax.experimental.pallas import tpu as pltpu

F32 = jnp.float32
BF16 = jnp.bfloat16

D_MODEL = 1024
N_META = 16
HEAD_DIM = 64
N_Q_HEADS = 16
N_KV_HEADS = 4
Q_W = N_Q_HEADS * HEAD_DIM
KV_W = N_KV_HEADS * HEAD_DIM
SSM_GROUPS = 64
SSM_GROUP_CH = 16
SSM_STATE = 64
N_STATE = SSM_GROUPS * SSM_STATE
D_FF = 2816
IN_COLS = Q_W + 2 * KV_W + 3 * D_MODEL
EPS = 1e-6
BLK = 128
PAD = BLK - N_META
N_DEV = 8
NEG = -1e30
SSM_KB = 8
ST_KB = N_STATE // SSM_KB
LB_KB = ST_KB // BLK
N_LB = N_STATE // BLK

ADAM_LR = 0.001
ADAM_B1 = 0.9
ADAM_B2 = 0.999
ADAM_EPS = 1e-08
ADAM_WD = 0.01
ADAM_STEP = 10

VMEM_LIMIT = 48 * 1024 * 1024
MESH = pl.DeviceIdType.MESH

SMALL = ["norm_mix", "q_norm", "k_norm", "attn_sinks", "lam_re", "lam_im", "log_dt", "ssm_b_re", "ssm_b_im",
         "ssm_c_re", "ssm_c_im", "ssm_d", "attn_branch_norm", "ssm_branch_norm", "norm_ffn"]
WEIGHTS = ["meta_tokens", "norm_mix", "w_in", "q_norm", "k_norm", "attn_sinks", "lam_re", "lam_im", "log_dt",
           "ssm_b_re", "ssm_b_im", "ssm_c_re", "ssm_c_im", "ssm_d", "w_glu", "attn_branch_norm",
           "ssm_branch_norm", "w_out", "norm_ffn", "w_ffn_in", "w_ffn_out"]


def _params(**kw):
    return pltpu.CompilerParams(vmem_limit_bytes=VMEM_LIMIT, **kw)


def _pick(n, cap, mult=16):
    best = None
    for d in range(mult, min(n, cap) + 1, mult):
        if n % d == 0:
            best = d
    assert best is not None, (n, cap, mult)
    return best


def _my_index():
    return 4 * lax.axis_index("x") + 2 * lax.axis_index("y") + lax.axis_index("c")


def _rms(x, g):
    r = lax.rsqrt(jnp.mean(x * x, axis=-1, keepdims=True) + EPS)
    return x * r * g


def _rms_bwd(x, g, dy):
    r = lax.rsqrt(jnp.mean(x * x, axis=-1, keepdims=True) + EPS)
    t = dy * g
    dx = r * t - x * (r * r * r) * jnp.mean(t * x, axis=-1, keepdims=True)
    dg = jnp.sum(dy * (x * r), axis=0, keepdims=True)
    return dx, dg


def _sigmoid(x):
    return jax.nn.sigmoid(x)


def _gelu(x):
    k = math.sqrt(2.0 / math.pi)
    return 0.5 * x * (1.0 + jnp.tanh(k * (x + 0.044715 * (x * x * x))))


def _gelu_grad(x):
    k = math.sqrt(2.0 / math.pi)
    t = jnp.tanh(k * (x + 0.044715 * (x * x * x)))
    return 0.5 * (1.0 + t) + 0.5 * x * (1.0 - t * t) * (k * (1.0 + 3.0 * 0.044715 * (x * x)))


def _head_mean(x, e_ref):
    hi = x.astype(BF16)
    r1 = x - hi.astype(F32)
    mid = r1.astype(BF16)
    lo = (r1 - mid.astype(F32)).astype(BF16)
    e = e_ref[...]
    out = []
    for b in range(x.shape[1] // 256):
        sl = slice(256 * b, 256 * b + 256)
        s = (jnp.dot(hi[:, sl], e, preferred_element_type=F32)
             + jnp.dot(mid[:, sl], e, preferred_element_type=F32)
             + jnp.dot(lo[:, sl], e, preferred_element_type=F32))
        out.append(s)
    s = out[0] if len(out) == 1 else jnp.concatenate(out, axis=1)
    return s * (1.0 / HEAD_DIM)


def _head_rms(x, g, e_ref):
    r = lax.rsqrt(_head_mean(x * x, e_ref) + EPS)
    return x * r * g


def _head_rms_bwd(x, g, dy, e_ref):
    r = lax.rsqrt(_head_mean(x * x, e_ref) + EPS)
    t = dy * g
    dx = r * t - x * (r * r * r) * _head_mean(t * x, e_ref)
    dg = jnp.sum(dy * (x * r), axis=0, keepdims=True)
    return dx, dg


def _lane_half(shape):
    lane = lax.broadcasted_iota(jnp.int32, shape, len(shape) - 1)
    return (lane >> 6) & 1


def _matmul(name, a, w, *, nt, tm, tn, tk, n=None, w_off=0, res=None, norm_g=None, out_dtype=F32):
    m_dim, k_dim = a.shape
    n_dim = n if n is not None else (w.shape[0] if nt else w.shape[1])
    gm, gn, gk = m_dim // tm, n_dim // tn, k_dim // tk
    assert gm * tm == m_dim and gn * tn == n_dim and gk * tk == k_dim, (name, a.shape, w.shape, tm, tn, tk)
    assert norm_g is None or tn == n_dim
    direct = out_dtype == F32
    dn = (((1,), (1,)), ((), ())) if nt else (((1,), (0,)), ((), ()))

    def body(*refs):
        refs = list(refs)
        a_ref, w_ref = refs[0], refs[1]
        pos = 2
        r_ref = g_ref = on_ref = None
        if res is not None:
            r_ref, pos = refs[pos], pos + 1
        if norm_g is not None:
            g_ref, pos = refs[pos], pos + 1
        o_ref, pos = refs[pos], pos + 1
        if norm_g is not None:
            on_ref, pos = refs[pos], pos + 1
        acc = o_ref if direct else refs[pos]
        k = pl.program_id(2)
        part = lax.dot_general(a_ref[...], w_ref[...], dn, preferred_element_type=F32)

        @pl.when(k == 0)
        def _():
            acc[...] = part if r_ref is None or not direct else r_ref[...] + part

        @pl.when(k > 0)
        def _():
            acc[...] += part

        @pl.when(k == gk - 1)
        def _():
            if not direct:
                r = acc[...]
                if r_ref is not None:
                    r = r_ref[...] + r
                o_ref[...] = r.astype(out_dtype)
            if on_ref is not None:
                on_ref[...] = _rms(o_ref[...].astype(F32), g_ref[...]).astype(BF16)

    if nt:
        w_spec = pl.BlockSpec((tn, tk), lambda i, j, k: (j + w_off, k))
    else:
        w_spec = pl.BlockSpec((tk, tn), lambda i, j, k: (k, j))
    in_specs = [pl.BlockSpec((tm, tk), lambda i, j, k: (i, k)), w_spec]
    args = [a, w]
    out_spec = pl.BlockSpec((tm, tn), lambda i, j, k: (i, j))
    out_specs, out_shape = [out_spec], [jax.ShapeDtypeStruct((m_dim, n_dim), out_dtype)]
    if res is not None:
        in_specs.append(out_spec)
        args.append(res)
    if norm_g is not None:
        in_specs.append(pl.BlockSpec((1, tn), lambda i, j, k: (0, 0)))
        args.append(norm_g)
        out_specs.append(out_spec)
        out_shape.append(jax.ShapeDtypeStruct((m_dim, n_dim), BF16))
    out = pl.pallas_call(
        body, name=name, grid=(gm, gn, gk),
        in_specs=in_specs, out_specs=out_specs, out_shape=out_shape,
        scratch_shapes=[] if direct else [pltpu.VMEM((tm, tn), F32)],
        compiler_params=_params(dimension_semantics=("parallel", "parallel", "arbitrary")),
    )(*args)
    return out if norm_g is not None else out[0]


def _matmul_tn(name, a, b, *, tm, tn, tl, after=None):
    l_dim, m_dim = a.shape
    n_dim = b.shape[1]
    gm, gn, gl = m_dim // tm, n_dim // tn, l_dim // tl
    assert gm * tm == m_dim and gn * tn == n_dim and gl * tl == l_dim, (name, a.shape, b.shape, tm, tn, tl)

    def body(*refs):
        a_ref, b_ref = refs[0], refs[1]
        o_ref, ob_ref = refs[-2], refs[-1]

        @pl.when(pl.program_id(2) == 0)
        def _():
            o_ref[...] = jnp.zeros_like(o_ref)

        o_ref[...] += lax.dot_general(a_ref[...], b_ref[...], (((0,), (0,)), ((), ())),
                                      preferred_element_type=F32)

        @pl.when(pl.program_id(2) == gl - 1)
        def _():
            ob_ref[...] = o_ref[...].astype(BF16)

    out_spec = pl.BlockSpec((tm, tn), lambda i, j, l: (i, j))
    in_specs = [pl.BlockSpec((tl, tm), lambda i, j, l: (l, i)), pl.BlockSpec((tl, tn), lambda i, j, l: (l, j))]
    args = [a, b]
    if after is not None:
        in_specs.append(pl.BlockSpec(memory_space=pl.ANY))
        args.append(after)
    return pl.pallas_call(
        body, name=name, grid=(gm, gn, gl),
        in_specs=in_specs,
        out_specs=[out_spec, out_spec],
        out_shape=[jax.ShapeDtypeStruct((m_dim, n_dim), F32), jax.ShapeDtypeStruct((m_dim, n_dim), BF16)],
        compiler_params=_params(dimension_semantics=("parallel", "parallel", "arbitrary")),
    )(*args)


def _row_spec(tm, cols, f=None):
    if f is None:
        return pl.BlockSpec((tm, cols), lambda i: (i, 0))
    return pl.BlockSpec((tm, cols), lambda i: (f(i), 0))


def _full_spec(shape):
    nd = len(shape)
    return pl.BlockSpec(shape, lambda i: (0,) * nd)


def _shifted_specs(n_sub, n_blocks):
    return [_row_spec(BLK, D_MODEL, (lambda i, k=k: jnp.clip(n_sub * i - 1 + k, 0, n_blocks - 1)))
            for k in range(n_sub)]


def _embed_norm(x2d, g, tm):
    s_len = x2d.shape[0]
    l_dim = s_len + BLK
    n_sub = tm // BLK

    def body(*refs):
        x_refs, g_ref, h_ref, xn_ref = refs[:n_sub], refs[n_sub], refs[n_sub + 1], refs[n_sub + 2]
        i = pl.program_id(0)
        for k in range(n_sub):
            rows = slice(BLK * k, BLK * k + BLK)
            h = x_refs[k][...] * jnp.where(n_sub * i + k >= 1, 1.0, 0.0)
            h_ref[rows, :] = h
            xn_ref[rows, :] = _rms(h, g_ref[...]).astype(BF16)

    return pl.pallas_call(
        body, name="embed_norm", grid=(l_dim // tm,),
        in_specs=_shifted_specs(n_sub, s_len // BLK) + [_full_spec((1, D_MODEL))],
        out_specs=[_row_spec(tm, D_MODEL), _row_spec(tm, D_MODEL)],
        out_shape=[jax.ShapeDtypeStruct((l_dim, D_MODEL), F32),
                   jax.ShapeDtypeStruct((l_dim, D_MODEL), BF16)],
        compiler_params=_params(),
    )(*([x2d] * n_sub), g)


def _embed_meta(meta_pad, g, h0, xn):
    def body(mp_ref, g_ref, h_in, xn_in, h_ref, xn_ref):
        h_ref[...] = mp_ref[...]
        xn_ref[...] = _rms(mp_ref[...], g_ref[...]).astype(BF16)

    any_spec = pl.BlockSpec(memory_space=pl.ANY)
    return pl.pallas_call(
        body, name="embed_meta", grid=(1,),
        in_specs=[_full_spec((BLK, D_MODEL)), _full_spec((1, D_MODEL)), any_spec, any_spec],
        out_specs=[_row_spec(BLK, D_MODEL), _row_spec(BLK, D_MODEL)],
        out_shape=[jax.ShapeDtypeStruct(h0.shape, F32), jax.ShapeDtypeStruct(xn.shape, BF16)],
        input_output_aliases={2: 0, 3: 1},
        compiler_params=_params(),
    )(meta_pad, g, h0, xn)


KVX_W = 2 * N_KV_HEADS * BLK


def _qk_prep(qkv, q_norm_t, k_norm_t, e_mat, tm):
    l_dim = qkv.shape[0]

    def body(x_ref, qg_ref, kg_ref, e_ref, q_ref, kf_ref, vf_ref):
        x = x_ref[...]
        q = _head_rms(x[:, :Q_W], qg_ref[...], e_ref) * (HEAD_DIM ** -0.5)
        q_ref[...] = q.astype(BF16)
        k = _head_rms(x[:, Q_W:Q_W + KV_W], kg_ref[...], e_ref)
        v = x[:, Q_W + KV_W:Q_W + 2 * KV_W]
        half = _lane_half((tm, BLK))
        for src, dst, fill in ((k, kf_ref, 0.0), (v, vf_ref, 0.0)):
            for kv in range(N_KV_HEADS):
                blk = src[:, BLK * (kv // 2):BLK * (kv // 2) + BLK]
                swapped = pltpu.roll(blk, HEAD_DIM, axis=1)
                for e in range(2):
                    val = blk if kv % 2 == e else swapped
                    idx = 2 * kv + e
                    dst[:, BLK * idx:BLK * idx + BLK] = jnp.where(half == e, val, fill).astype(BF16)

    return pl.pallas_call(
        body, name="qk_prep", grid=(l_dim // tm,),
        in_specs=[_row_spec(tm, Q_W + 2 * KV_W), _full_spec((1, Q_W)), _full_spec((1, KV_W)),
                  _full_spec((256, 256))],
        out_specs=[_row_spec(tm, Q_W), _row_spec(tm, KVX_W), _row_spec(tm, KVX_W)],
        out_shape=[jax.ShapeDtypeStruct((l_dim, Q_W), BF16),
                   jax.ShapeDtypeStruct((l_dim, KVX_W), BF16),
                   jax.ShapeDtypeStruct((l_dim, KVX_W), BF16)],
        compiler_params=_params(),
    )(qkv, q_norm_t, k_norm_t, e_mat)


def _merge_fwd(attn, zab, gates, abn, sbn, tm):
    l_dim = attn.shape[0]

    def body(a_ref, z_ref, g_ref, an_ref, sn_ref, o_ref):
        z = z_ref[...].astype(F32)
        g = g_ref[...].astype(F32)
        ssm = z[:, :D_MODEL] * _sigmoid(z[:, D_MODEL:])
        merged = (_sigmoid(g[:, :D_MODEL]) * _rms(a_ref[...], an_ref[...])
                  + _sigmoid(g[:, D_MODEL:]) * _rms(ssm, sn_ref[...]))
        o_ref[...] = merged.astype(BF16)

    return pl.pallas_call(
        body, name="merge_fwd", grid=(l_dim // tm,),
        in_specs=[_row_spec(tm, D_MODEL), _row_spec(tm, 2 * D_MODEL), _row_spec(tm, 2 * D_MODEL),
                  _full_spec((1, D_MODEL)), _full_spec((1, D_MODEL))],
        out_specs=_row_spec(tm, D_MODEL),
        out_shape=jax.ShapeDtypeStruct((l_dim, D_MODEL), BF16),
        compiler_params=_params(),
    )(attn, zab, gates, abn, sbn)


def _merge_bwd(attn, zab, gates, abn, sbn, dmerged, tm):
    l_dim = attn.shape[0]

    def body(a_ref, z_ref, g_ref, an_ref, sn_ref, dm_ref, da_ref, dz_ref, dg_ref, dan_ref, dsn_ref):
        @pl.when(pl.program_id(0) == 0)
        def _():
            dan_ref[...] = jnp.zeros_like(dan_ref)
            dsn_ref[...] = jnp.zeros_like(dsn_ref)

        z = z_ref[...].astype(F32)
        g = g_ref[...].astype(F32)
        dm = dm_ref[...].astype(F32)
        attn_v = a_ref[...]
        za, zb = z[:, :D_MODEL], z[:, D_MODEL:]
        sb = _sigmoid(zb)
        ssm = za * sb
        s_ga, s_gs = _sigmoid(g[:, :D_MODEL]), _sigmoid(g[:, D_MODEL:])
        a_n = _rms(attn_v, an_ref[...])
        s_n = _rms(ssm, sn_ref[...])
        dg_ref[:, :D_MODEL] = (dm * a_n * s_ga * (1.0 - s_ga)).astype(BF16)
        dg_ref[:, D_MODEL:] = (dm * s_n * s_gs * (1.0 - s_gs)).astype(BF16)
        dattn, dan = _rms_bwd(attn_v, an_ref[...], dm * s_ga)
        dssm, dsn = _rms_bwd(ssm, sn_ref[...], dm * s_gs)
        da_ref[...] = dattn.astype(BF16)
        dz_ref[:, :D_MODEL] = (dssm * sb).astype(BF16)
        dz_ref[:, D_MODEL:] = (dssm * za * sb * (1.0 - sb)).astype(BF16)
        dan_ref[...] += dan
        dsn_ref[...] += dsn

    return pl.pallas_call(
        body, name="merge_bwd", grid=(l_dim // tm,),
        in_specs=[_row_spec(tm, D_MODEL), _row_spec(tm, 2 * D_MODEL), _row_spec(tm, 2 * D_MODEL),
                  _full_spec((1, D_MODEL)), _full_spec((1, D_MODEL)), _row_spec(tm, D_MODEL)],
        out_specs=[_row_spec(tm, D_MODEL), _row_spec(tm, 2 * D_MODEL), _row_spec(tm, 2 * D_MODEL),
                   _full_spec((1, D_MODEL)), _full_spec((1, D_MODEL))],
        out_shape=[jax.ShapeDtypeStruct((l_dim, D_MODEL), BF16),
                   jax.ShapeDtypeStruct((l_dim, 2 * D_MODEL), BF16),
                   jax.ShapeDtypeStruct((l_dim, 2 * D_MODEL), BF16),
                   jax.ShapeDtypeStruct((1, D_MODEL), F32), jax.ShapeDtypeStruct((1, D_MODEL), F32)],
        compiler_params=_params(),
    )(attn, zab, gates, abn, sbn, dmerged)


FF_TILE = D_FF // 2


def _ffn_in_swiglu(hn, wt_ffn_in, tm):
    l_dim = hn.shape[0]
    nt = (((1,), (1,)), ((), ()))

    def body(a_ref, w_ref, gu_ref, act_ref):
        r = lax.dot_general(a_ref[...], w_ref[...], nt, preferred_element_type=F32)
        gate, up = r[:, :FF_TILE], r[:, FF_TILE:]
        gu_ref[...] = r.astype(BF16)
        act_ref[...] = (gate * _sigmoid(gate) * up).astype(BF16)

    return pl.pallas_call(
        body, name="ffn_in_swiglu", grid=(l_dim // tm, 2),
        in_specs=[pl.BlockSpec((tm, D_MODEL), lambda i, j: (i, 0)),
                  pl.BlockSpec((2 * FF_TILE, D_MODEL), lambda i, j: (j, 0))],
        out_specs=[pl.BlockSpec((tm, 2 * FF_TILE), lambda i, j: (i, j)),
                   pl.BlockSpec((tm, FF_TILE), lambda i, j: (i, j))],
        out_shape=[jax.ShapeDtypeStruct((l_dim, 2 * D_FF), BF16), jax.ShapeDtypeStruct((l_dim, D_FF), BF16)],
        compiler_params=_params(dimension_semantics=("parallel", "parallel")),
    )(hn, wt_ffn_in)


def _d_act_swiglu(dh2_b, w_ffn_out, gu, tm):
    l_dim = dh2_b.shape[0]
    nt = (((1,), (1,)), ((), ()))

    def body(d_ref, w_ref, gu_ref, o_ref):
        d = lax.dot_general(d_ref[...], w_ref[...], nt, preferred_element_type=F32)
        gate = gu_ref[:, :FF_TILE].astype(F32)
        up = gu_ref[:, FF_TILE:].astype(F32)
        s = _sigmoid(gate)
        o_ref[:, :FF_TILE] = (d * up * (s * (1.0 + gate * (1.0 - s)))).astype(BF16)
        o_ref[:, FF_TILE:] = (d * (gate * s)).astype(BF16)

    return pl.pallas_call(
        body, name="d_act_swiglu", grid=(l_dim // tm, 2),
        in_specs=[pl.BlockSpec((tm, D_MODEL), lambda i, j: (i, 0)),
                  pl.BlockSpec((FF_TILE, D_MODEL), lambda i, j: (j, 0)),
                  pl.BlockSpec((tm, 2 * FF_TILE), lambda i, j: (i, j))],
        out_specs=pl.BlockSpec((tm, 2 * FF_TILE), lambda i, j: (i, j)),
        out_shape=jax.ShapeDtypeStruct((l_dim, 2 * D_FF), BF16),
        compiler_params=_params(dimension_semantics=("parallel", "parallel")),
    )(dh2_b, w_ffn_out, gu)


def _loss_grad(h2, target2d, tm):
    l_dim = h2.shape[0]
    n_sub = tm // BLK

    def body(*refs):
        h_ref, t_refs = refs[0], refs[1:1 + n_sub]
        d_ref, db_ref, loss_ref = refs[1 + n_sub:]
        i = pl.program_id(0)

        @pl.when(i == 0)
        def _():
            loss_ref[...] = jnp.zeros_like(loss_ref)

        for k in range(n_sub):
            rows = slice(BLK * k, BLK * k + BLK)
            real = jnp.where(n_sub * i + k >= 1, 1.0, 0.0)
            err = (h_ref[rows, :] - t_refs[k][...]) * real
            d = err * (1.0 / D_MODEL)
            d_ref[rows, :] = d
            db_ref[rows, :] = d.astype(BF16)
            loss_ref[...] += 0.5 * jnp.sum(jnp.mean(err * err, axis=-1, keepdims=True), axis=0, keepdims=True)

    return pl.pallas_call(
        body, name="loss_grad", grid=(l_dim // tm,),
        in_specs=[_row_spec(tm, D_MODEL)] + _shifted_specs(n_sub, target2d.shape[0] // BLK),
        out_specs=[_row_spec(tm, D_MODEL), _row_spec(tm, D_MODEL), _full_spec((1, 1))],
        out_shape=[jax.ShapeDtypeStruct((l_dim, D_MODEL), F32), jax.ShapeDtypeStruct((l_dim, D_MODEL), BF16),
                   jax.ShapeDtypeStruct((1, 1), F32)],
        compiler_params=_params(),
    )(h2, *([target2d] * n_sub))


def _norm_bwd_res(name, h, g, dy, dres, tm):
    l_dim = h.shape[0]

    def body(h_ref, g_ref, dy_ref, dr_ref, o_ref, ob_ref, dg_ref):
        @pl.when(pl.program_id(0) == 0)
        def _():
            dg_ref[...] = jnp.zeros_like(dg_ref)

        dx, dg = _rms_bwd(h_ref[...], g_ref[...], dy_ref[...].astype(F32))
        out = dr_ref[...] + dx
        o_ref[...] = out
        ob_ref[...] = out.astype(BF16)
        dg_ref[...] += dg

    return pl.pallas_call(
        body, name=name, grid=(l_dim // tm,),
        in_specs=[_row_spec(tm, D_MODEL), _full_spec((1, D_MODEL)), _row_spec(tm, D_MODEL), _row_spec(tm, D_MODEL)],
        out_specs=[_row_spec(tm, D_MODEL), _row_spec(tm, D_MODEL), _full_spec((1, D_MODEL))],
        out_shape=[jax.ShapeDtypeStruct((l_dim, D_MODEL), F32), jax.ShapeDtypeStruct((l_dim, D_MODEL), BF16),
                   jax.ShapeDtypeStruct((1, D_MODEL), F32)],
        compiler_params=_params(),
    )(h, g, dy, dres)


def _final_bwd(h0, g, dxn, dh1):
    l_dim = h0.shape[0]
    nb = l_dim // BLK

    def body(h_ref, g_ref, dy_ref, dr_ref, gx_ref, gm_ref, dg_ref):
        i = pl.program_id(0)

        @pl.when(i == 0)
        def _():
            dg_ref[...] = jnp.zeros_like(dg_ref)

        dx, dg = _rms_bwd(h_ref[...], g_ref[...], dy_ref[...].astype(F32))
        out = dr_ref[...] + dx
        dg_ref[...] += dg

        @pl.when(i == 0)
        def _():
            gm_ref[...] = out

        @pl.when(i > 0)
        def _():
            gx_ref[...] = out

    return pl.pallas_call(
        body, name="final_bwd", grid=(nb,),
        in_specs=[_row_spec(BLK, D_MODEL), _full_spec((1, D_MODEL)), _row_spec(BLK, D_MODEL),
                  _row_spec(BLK, D_MODEL)],
        out_specs=[_row_spec(BLK, D_MODEL, lambda i: jnp.maximum(i - 1, 0)), _full_spec((BLK, D_MODEL)),
                   _full_spec((1, D_MODEL))],
        out_shape=[jax.ShapeDtypeStruct((l_dim - BLK, D_MODEL), F32), jax.ShapeDtypeStruct((BLK, D_MODEL), F32),
                   jax.ShapeDtypeStruct((1, D_MODEL), F32)],
        compiler_params=_params(),
    )(h0, g, dxn, dh1)


def _attn_valid(n):
    shape = (2 * BLK, 3 * BLK)
    qi = lax.broadcasted_iota(jnp.int32, shape, 0) & (BLK - 1)
    col = lax.broadcasted_iota(jnp.int32, shape, 1)
    kj = col & (BLK - 1)
    part = col >> 7
    nn = jnp.zeros(shape, jnp.int32) + n
    meta_ok = (part == 0) & (kj >= PAD) & (nn >= 1)
    prev_ok = (part == 1) & (kj > qi) & (nn >= 2)
    cur_ok = (part == 2) & (kj <= qi) & ((nn >= 1) | (kj >= PAD))
    return meta_ok | prev_ok | cur_ok


def _attn_scores(q_ref, kwin, sk_ref, valid, kv, e):
    qs = jnp.concatenate([q_ref[:, BLK * (2 * kv):BLK * (2 * kv) + BLK],
                          q_ref[:, BLK * (2 * kv + 1):BLK * (2 * kv + 1) + BLK]], axis=0)
    s = lax.dot_general(qs, kwin, (((1,), (1,)), ((), ())), preferred_element_type=F32)
    h0 = 4 * kv + e
    row = lax.broadcasted_iota(jnp.int32, (2 * BLK, 1), 0)
    sink = jnp.where(row < BLK, sk_ref[:, h0:h0 + 1], sk_ref[:, h0 + 2:h0 + 3])
    return qs, jnp.where(valid, s, NEG), sink


def _attn_specs(nb):
    prev = lambda i: jnp.maximum(i - 1, 0)
    zero = lambda i: 0
    kv_specs = [_row_spec(BLK, KVX_W, zero), _row_spec(BLK, KVX_W, prev), _row_spec(BLK, KVX_W)]
    return kv_specs


def _attn_fwd(qn, kf, vf, sinks):
    l_dim = qn.shape[0]
    nb = l_dim // BLK

    def body(q_ref, km_ref, kp_ref, kc_ref, vm_ref, vp_ref, vc_ref, sk_ref, o_ref, lse_ref):
        valid = _attn_valid(pl.program_id(0))
        lane = lax.broadcasted_iota(jnp.int32, (BLK, BLK), 1)
        lse_all = jnp.zeros((BLK, BLK), F32)
        for kv in range(N_KV_HEADS):
            outs = []
            for e in range(2):
                sl = slice(BLK * (2 * kv + e), BLK * (2 * kv + e) + BLK)
                kwin = jnp.concatenate([km_ref[:, sl], kp_ref[:, sl], kc_ref[:, sl]], axis=0)
                vwin = jnp.concatenate([vm_ref[:, sl], vp_ref[:, sl], vc_ref[:, sl]], axis=0)
                _, s, sink = _attn_scores(q_ref, kwin, sk_ref, valid, kv, e)
                m = jnp.maximum(jnp.max(s, axis=-1, keepdims=True), sink)
                ex = jnp.exp(s - m)
                den = jnp.sum(ex, axis=-1, keepdims=True) + jnp.exp(sink - m)
                p = ex * (1.0 / den)
                outs.append(jnp.dot(p.astype(BF16), vwin, preferred_element_type=F32))
                lse = m + jnp.log(den)
                lse_all = jnp.where(lane == 4 * kv + e, lse[:BLK], lse_all)
                lse_all = jnp.where(lane == 4 * kv + 2 + e, lse[BLK:], lse_all)
            o = outs[0] + outs[1]
            o_ref[:, BLK * (2 * kv):BLK * (2 * kv) + BLK] = o[:BLK]
            o_ref[:, BLK * (2 * kv + 1):BLK * (2 * kv + 1) + BLK] = o[BLK:]
        lse_ref[...] = lse_all

    kv_specs = _attn_specs(nb)
    return pl.pallas_call(
        body, name="attn_fwd", grid=(nb,),
        in_specs=[_row_spec(BLK, Q_W)] + kv_specs + kv_specs + [_full_spec((1, N_Q_HEADS))],
        out_specs=[_row_spec(BLK, Q_W), _row_spec(BLK, BLK)],
        out_shape=[jax.ShapeDtypeStruct((l_dim, Q_W), F32), jax.ShapeDtypeStruct((l_dim, BLK), F32)],
        compiler_params=_params(),
    )(qn, kf, kf, kf, vf, vf, vf, sinks)


def _attn_bwd(qn, kf, vf, sinks, lse, attn, dattn):
    l_dim = qn.shape[0]
    nb = l_dim // BLK
    wide = KVX_W
    tn = (((0,), (0,)), ((), ()))
    nt = (((1,), (1,)), ((), ()))

    def body(q_ref, km_ref, kp_ref, kc_ref, vm_ref, vp_ref, vc_ref, sk_ref, lse_ref, o_ref, do_ref,
             dq_ref, dkc_ref, dkp_ref, dkm_ref, dvc_ref, dvp_ref, dvm_ref, dsk_ref):
        @pl.when(pl.program_id(0) == 0)
        def _():
            dkm_ref[...] = jnp.zeros_like(dkm_ref)
            dvm_ref[...] = jnp.zeros_like(dvm_ref)
            dsk_ref[...] = jnp.zeros_like(dsk_ref)

        valid = _attn_valid(pl.program_id(0))
        half = _lane_half((BLK, BLK))
        half2 = _lane_half((2 * BLK, BLK))
        lane16 = lax.broadcasted_iota(jnp.int32, (1, N_Q_HEADS), 1)
        dsk = jnp.zeros((1, N_Q_HEADS), F32)
        for kv in range(N_KV_HEADS):
            j0, j1 = 2 * kv, 2 * kv + 1
            do0 = do_ref[:, BLK * j0:BLK * j0 + BLK]
            do1 = do_ref[:, BLK * j1:BLK * j1 + BLK]
            do0f, do1f = do0.astype(F32), do1.astype(F32)
            prod0 = do0f * o_ref[:, BLK * j0:BLK * j0 + BLK]
            prod1 = do1f * o_ref[:, BLK * j1:BLK * j1 + BLK]
            dos = jnp.concatenate([do0, do1], axis=0)
            dqs = []
            for e in range(2):
                sl = slice(BLK * (2 * kv + e), BLK * (2 * kv + e) + BLK)
                kwin = jnp.concatenate([km_ref[:, sl], kp_ref[:, sl], kc_ref[:, sl]], axis=0)
                vwin = jnp.concatenate([vm_ref[:, sl], vp_ref[:, sl], vc_ref[:, sl]], axis=0)
                qs, s, sink = _attn_scores(q_ref, kwin, sk_ref, valid, kv, e)
                h0 = 4 * kv + e
                lse_rows = jnp.concatenate([lse_ref[:, h0:h0 + 1], lse_ref[:, h0 + 2:h0 + 3]], axis=0)
                p = jnp.exp(s - lse_rows)
                p_sink = jnp.exp(sink - lse_rows)
                delta = jnp.concatenate(
                    [jnp.sum(jnp.where(half == e, prod0, 0.0), axis=-1, keepdims=True),
                     jnp.sum(jnp.where(half == e, prod1, 0.0), axis=-1, keepdims=True)], axis=0)
                dp = lax.dot_general(dos, vwin, nt, preferred_element_type=F32)
                ds = (p * (dp - delta)).astype(BF16)
                pb = p.astype(BF16)
                dqs.append(jnp.dot(ds, kwin, preferred_element_type=F32))
                dk = lax.dot_general(ds, qs, tn, preferred_element_type=F32)
                dv = lax.dot_general(pb, dos, tn, preferred_element_type=F32)
                dkm_ref[:, sl] += dk[:BLK]
                dkp_ref[:, sl] = dk[BLK:2 * BLK].astype(BF16)
                dkc_ref[:, sl] = dk[2 * BLK:].astype(BF16)
                dvm_ref[:, sl] += dv[:BLK]
                dvp_ref[:, sl] = dv[BLK:2 * BLK].astype(BF16)
                dvc_ref[:, sl] = dv[2 * BLK:].astype(BF16)
                sink_g = -(p_sink * delta)
                g_lo = jnp.sum(sink_g[:BLK], axis=0, keepdims=True)
                g_hi = jnp.sum(sink_g[BLK:], axis=0, keepdims=True)
                dsk = dsk + jnp.where(lane16 == h0, g_lo, 0.0) + jnp.where(lane16 == h0 + 2, g_hi, 0.0)
            dq = jnp.where(half2 == 0, dqs[0], dqs[1])
            dq_ref[:, BLK * j0:BLK * j0 + BLK] = dq[:BLK].astype(BF16)
            dq_ref[:, BLK * j1:BLK * j1 + BLK] = dq[BLK:].astype(BF16)
        dsk_ref[...] += dsk

    kv_specs = _attn_specs(nb)
    row_wide = _row_spec(BLK, wide)
    acc_wide = _full_spec((BLK, wide))
    big = jax.ShapeDtypeStruct((l_dim, wide), BF16)
    return pl.pallas_call(
        body, name="attn_bwd", grid=(nb,),
        in_specs=[_row_spec(BLK, Q_W)] + kv_specs + kv_specs
        + [_full_spec((1, N_Q_HEADS)), _row_spec(BLK, BLK), _row_spec(BLK, Q_W), _row_spec(BLK, Q_W)],
        out_specs=[_row_spec(BLK, Q_W), row_wide, row_wide, acc_wide, row_wide, row_wide, acc_wide,
                   _full_spec((1, N_Q_HEADS))],
        out_shape=[jax.ShapeDtypeStruct((l_dim, Q_W), BF16), big, big, jax.ShapeDtypeStruct((BLK, wide), F32),
                   big, big, jax.ShapeDtypeStruct((BLK, wide), F32), jax.ShapeDtypeStruct((1, N_Q_HEADS), F32)],
        compiler_params=_params(),
    )(qn, kf, kf, kf, vf, vf, vf, sinks, lse, attn, dattn)


def _qk_bwd(qkv, q_norm_t, k_norm_t, e_mat, dq, dkc, dkp, dkm, dvc, dvp, dvm):
    l_dim = qkv.shape[0]
    nb = l_dim // BLK
    wide = KVX_W

    def fold(x):
        half = _lane_half((BLK, BLK))
        blocks = []
        for kb in range(2):
            t = []
            for kv in (2 * kb, 2 * kb + 1):
                own = kv % 2
                a = x[:, BLK * (2 * kv + own):BLK * (2 * kv + own) + BLK]
                b = pltpu.roll(x[:, BLK * (2 * kv + 1 - own):BLK * (2 * kv + 1 - own) + BLK], HEAD_DIM, axis=1)
                t.append(a + b)
            blocks.append(jnp.where(half == 0, t[0], t[1]))
        return jnp.concatenate(blocks, axis=1)

    def body(x_ref, qg_ref, kg_ref, e_ref, dq_ref, dkc_ref, dkp_ref, dkm_ref, dvc_ref, dvp_ref, dvm_ref,
             o_ref, dqg_ref, dkg_ref):
        i = pl.program_id(0)

        @pl.when(i == 0)
        def _():
            dqg_ref[...] = jnp.zeros_like(dqg_ref)
            dkg_ref[...] = jnp.zeros_like(dkg_ref)

        first = jnp.where(i == 0, 1.0, 0.0)
        not_last = jnp.where(i < nb - 1, 1.0, 0.0)
        dk_x = dkc_ref[...].astype(F32) + not_last * dkp_ref[...].astype(F32) + first * dkm_ref[...]
        dv_x = dvc_ref[...].astype(F32) + not_last * dvp_ref[...].astype(F32) + first * dvm_ref[...]
        x = x_ref[...]
        dqx, dqg = _head_rms_bwd(x[:, :Q_W], qg_ref[...], dq_ref[...].astype(F32) * (HEAD_DIM ** -0.5), e_ref)
        dkx, dkg = _head_rms_bwd(x[:, Q_W:Q_W + KV_W], kg_ref[...], fold(dk_x), e_ref)
        o_ref[:, :Q_W] = dqx.astype(BF16)
        o_ref[:, Q_W:Q_W + KV_W] = dkx.astype(BF16)
        o_ref[:, Q_W + KV_W:] = fold(dv_x).astype(BF16)
        dqg_ref[...] += dqg
        dkg_ref[...] += dkg

    nxt = lambda i: jnp.minimum(i + 1, nb - 1)
    row_wide = _row_spec(BLK, wide)
    nxt_wide = _row_spec(BLK, wide, nxt)
    acc_wide = _full_spec((BLK, wide))
    return pl.pallas_call(
        body, name="qk_bwd", grid=(nb,),
        in_specs=[_row_spec(BLK, Q_W + 2 * KV_W), _full_spec((1, Q_W)), _full_spec((1, KV_W)),
                  _full_spec((256, 256)), _row_spec(BLK, Q_W),
                  row_wide, nxt_wide, acc_wide, row_wide, nxt_wide, acc_wide],
        out_specs=[_row_spec(BLK, Q_W + 2 * KV_W), _full_spec((1, Q_W)), _full_spec((1, KV_W))],
        out_shape=[jax.ShapeDtypeStruct((l_dim, Q_W + 2 * KV_W), BF16),
                   jax.ShapeDtypeStruct((1, Q_W), F32), jax.ShapeDtypeStruct((1, KV_W), F32)],
        compiler_params=_params(),
    )(qkv, q_norm_t, k_norm_t, e_mat, dq, dkc, dkp, dkm, dvc, dvp, dvm)


GRP = 8


def _strided(r, g):
    return pl.ds(r, g, stride=GRP)


def _slab(ref, base, r, g):
    return jnp.concatenate([ref[base + i, _strided(r, g), :] for i in range(LB_KB)], axis=1)


def _slab_store(ref, base, r, g, val):
    for i in range(LB_KB):
        ref[base + i, _strided(r, g), :] = val[:, BLK * i:BLK * i + BLK]


def _group_totals(xr_ref, xi_ref, base, ar, ai, reverse):
    g = xr_ref.shape[1] // GRP
    sr = si = None
    for r in (range(GRP - 1, -1, -1) if reverse else range(GRP)):
        xr, xi = _slab(xr_ref, base, r, g), _slab(xi_ref, base, r, g)
        if sr is not None:
            xr, xi = xr + ar * sr - ai * si, xi + ar * si + ai * sr
        sr, si = xr, xi
    return sr, si


def _carry_scan(tr, ti, sqr_ref, sqi_ref, lanes, sign, reverse):
    g = tr.shape[0]
    row = lax.broadcasted_iota(jnp.int32, tr.shape, 0)
    idx, s = 0, 1
    while s < g:
        ar = sqr_ref[idx:idx + 1, lanes]
        ai = sign * sqi_ref[idx:idx + 1, lanes]
        shift, keep = (g - s, row < g - s) if reverse else (s, row >= s)
        pr = jnp.where(keep, pltpu.roll(tr, shift, axis=0), 0.0)
        pi = jnp.where(keep, pltpu.roll(ti, shift, axis=0), 0.0)
        tr, ti = tr + ar * pr - ai * pi, ti + ar * pi + ai * pr
        idx, s = idx + 1, 2 * s
    return tr, ti


def _ssm_fwd(u, wb_re, wb_im, wc_re, wc_im, d_skip, tabs):
    l_dim = u.shape[0]
    nb = l_dim // BLK
    g = BLK // GRP

    def body(u_ref, wbr_ref, wbi_ref, wcr_ref, wci_ref, d_ref, a1r_ref, a1i_ref, sqr_ref, sqi_ref,
             seqr_ref, seqi_ref, y_ref, z_ref, sr_ref, si_ref, cr_ref, ci_ref, xr_ref, xi_ref):
        @pl.when(pl.program_id(0) == 0)
        def _():
            cr_ref[...] = jnp.zeros_like(cr_ref)
            ci_ref[...] = jnp.zeros_like(ci_ref)

        row = lax.broadcasted_iota(jnp.int32, (g, ST_KB), 0)
        for kb in range(SSM_KB):
            ch = slice(BLK * kb, BLK * kb + BLK)
            lanes = slice(ST_KB * kb, ST_KB * kb + ST_KB)
            u_kb = u_ref[:, ch]
            ub = u_kb.astype(BF16)
            xr = jnp.dot(ub, wbr_ref[kb], preferred_element_type=F32)
            xi = jnp.dot(ub, wbi_ref[kb], preferred_element_type=F32)
            base = LB_KB * kb
            for i in range(LB_KB):
                xr_ref[base + i] = xr[:, BLK * i:BLK * i + BLK]
                xi_ref[base + i] = xi[:, BLK * i:BLK * i + BLK]
            ar, ai = a1r_ref[0:1, lanes], a1i_ref[0:1, lanes]
            tr, ti = _group_totals(xr_ref, xi_ref, base, ar, ai, reverse=False)
            tr, ti = _carry_scan(tr, ti, sqr_ref, sqi_ref, lanes, 1.0, reverse=False)
            cin_r, cin_i = cr_ref[0:1, lanes], ci_ref[0:1, lanes]
            qr, qi = seqr_ref[:, lanes], seqi_ref[:, lanes]
            tr, ti = tr + qr * cin_r - qi * cin_i, ti + qr * cin_i + qi * cin_r
            cr_ref[0:1, lanes] = jnp.sum(jnp.where(row == g - 1, tr, 0.0), axis=0, keepdims=True)
            ci_ref[0:1, lanes] = jnp.sum(jnp.where(row == g - 1, ti, 0.0), axis=0, keepdims=True)
            pr = jnp.where(row == 0, cin_r, pltpu.roll(tr, 1, axis=0))
            pi = jnp.where(row == 0, cin_i, pltpu.roll(ti, 1, axis=0))
            for r in range(GRP):
                pr, pi = (_slab(xr_ref, base, r, g) + ar * pr - ai * pi,
                          _slab(xi_ref, base, r, g) + ar * pi + ai * pr)
                _slab_store(sr_ref, base, r, g, pr)
                _slab_store(si_ref, base, r, g, pi)
            s_r = jnp.concatenate([sr_ref[LB_KB * kb + i] for i in range(LB_KB)], axis=1)
            s_i = jnp.concatenate([si_ref[LB_KB * kb + i] for i in range(LB_KB)], axis=1)
            y = (jnp.dot(s_r.astype(BF16), wcr_ref[kb], preferred_element_type=F32)
                 - jnp.dot(s_i.astype(BF16), wci_ref[kb], preferred_element_type=F32)
                 + d_ref[:, ch] * u_kb)
            y_ref[:, ch] = y
            z_ref[:, ch] = _gelu(y).astype(BF16)

    wb_spec = _full_spec((SSM_KB, BLK, ST_KB))
    wc_spec = _full_spec((SSM_KB, ST_KB, BLK))
    tab_specs = [_full_spec(t.shape) for t in tabs[:6]]
    state_spec = pl.BlockSpec((N_LB, BLK, BLK), lambda i: (0, i, 0))
    state_shape = jax.ShapeDtypeStruct((N_LB, l_dim, BLK), F32)
    return pl.pallas_call(
        body, name="ssm_fwd", grid=(nb,),
        in_specs=[_row_spec(BLK, D_MODEL), wb_spec, wb_spec, wc_spec, wc_spec, _full_spec((1, D_MODEL))]
        + tab_specs,
        out_specs=[_row_spec(BLK, D_MODEL), _row_spec(BLK, D_MODEL), state_spec, state_spec],
        out_shape=[jax.ShapeDtypeStruct((l_dim, D_MODEL), F32), jax.ShapeDtypeStruct((l_dim, D_MODEL), BF16),
                   state_shape, state_shape],
        scratch_shapes=[pltpu.VMEM((8, N_STATE), F32), pltpu.VMEM((8, N_STATE), F32),
                        pltpu.VMEM((N_LB, BLK, BLK), F32), pltpu.VMEM((N_LB, BLK, BLK), F32)],
        compiler_params=_params(),
    )(u, wb_re, wb_im, wc_re, wc_im, d_skip, *tabs[:6])


def _ssm_bwd(dz, y, u, s_re, s_im, wb_re, wb_im, wc_re, wc_im, d_skip, tabs):
    l_dim = u.shape[0]
    nb = l_dim // BLK
    g = BLK // GRP

    def body(dz_ref, y_ref, u_ref, sr_ref, si_ref, wbr_ref, wbi_ref, wcr_ref, wci_ref, d_ref,
             a1r_ref, a1i_ref, sqr_ref, sqi_ref, revr_ref, revi_ref,
             du_ref, dd_ref, dar_ref, dai_ref, dwbr_ref, dwbi_ref, dwcr_ref, dwci_ref,
             cr_ref, ci_ref, gr_ref, gi_ref):
        @pl.when(pl.program_id(0) == 0)
        def _():
            for r in (cr_ref, ci_ref, dd_ref, dar_ref, dai_ref, dwbr_ref, dwbi_ref, dwcr_ref, dwci_ref):
                r[...] = jnp.zeros_like(r)

        tn = (((0,), (0,)), ((), ()))
        nt = (((1,), (1,)), ((), ()))
        row = lax.broadcasted_iota(jnp.int32, (g, ST_KB), 0)
        for kb in range(SSM_KB):
            ch = slice(BLK * kb, BLK * kb + BLK)
            lanes = slice(ST_KB * kb, ST_KB * kb + ST_KB)
            u_kb = u_ref[:, ch]
            dy = dz_ref[:, ch] * _gelu_grad(y_ref[:, ch])
            dyb = dy.astype(BF16)
            ub = u_kb.astype(BF16)
            dd_ref[:, ch] += jnp.sum(dy * u_kb, axis=0, keepdims=True)
            ds_r = lax.dot_general(dyb, wcr_ref[kb], nt, preferred_element_type=F32)
            ds_i = -lax.dot_general(dyb, wci_ref[kb], nt, preferred_element_type=F32)
            base = LB_KB * kb
            for i in range(LB_KB):
                gr_ref[base + i] = ds_r[:, BLK * i:BLK * i + BLK]
                gi_ref[base + i] = ds_i[:, BLK * i:BLK * i + BLK]
            ar, ai = a1r_ref[0:1, lanes], -a1i_ref[0:1, lanes]
            tr, ti = _group_totals(gr_ref, gi_ref, base, ar, ai, reverse=True)
            tr, ti = _carry_scan(tr, ti, sqr_ref, sqi_ref, lanes, -1.0, reverse=True)
            cin_r, cin_i = cr_ref[0:1, lanes], ci_ref[0:1, lanes]
            qr, qi = revr_ref[:, lanes], -revi_ref[:, lanes]
            tr, ti = tr + qr * cin_r - qi * cin_i, ti + qr * cin_i + qi * cin_r
            cr_ref[0:1, lanes] = jnp.sum(jnp.where(row == 0, tr, 0.0), axis=0, keepdims=True)
            ci_ref[0:1, lanes] = jnp.sum(jnp.where(row == 0, ti, 0.0), axis=0, keepdims=True)
            nr = jnp.where(row == g - 1, cin_r, pltpu.roll(tr, g - 1, axis=0))
            ni = jnp.where(row == g - 1, cin_i, pltpu.roll(ti, g - 1, axis=0))
            acc_r = jnp.zeros((g, ST_KB), F32)
            acc_i = jnp.zeros((g, ST_KB), F32)
            for r in range(GRP - 1, -1, -1):
                s_r, s_i = _slab(sr_ref, base, r, g), _slab(si_ref, base, r, g)
                acc_r = acc_r + (nr * s_r + ni * s_i)
                acc_i = acc_i + (ni * s_r - nr * s_i)
                nr, ni = (_slab(gr_ref, base, r, g) + ar * nr - ai * ni,
                          _slab(gi_ref, base, r, g) + ar * ni + ai * nr)
                _slab_store(gr_ref, base, r, g, nr)
                _slab_store(gi_ref, base, r, g, ni)
            dar_ref[:, lanes] += jnp.sum(acc_r, axis=0, keepdims=True)
            dai_ref[:, lanes] += jnp.sum(acc_i, axis=0, keepdims=True)
            grb = jnp.concatenate([gr_ref[base + i] for i in range(LB_KB)], axis=1).astype(BF16)
            gib = jnp.concatenate([gi_ref[base + i] for i in range(LB_KB)], axis=1).astype(BF16)
            srb = jnp.concatenate([sr_ref[base + i] for i in range(LB_KB)], axis=1).astype(BF16)
            sib = jnp.concatenate([si_ref[base + i] for i in range(LB_KB)], axis=1).astype(BF16)
            du = (lax.dot_general(grb, wbr_ref[kb], nt, preferred_element_type=F32)
                  + lax.dot_general(gib, wbi_ref[kb], nt, preferred_element_type=F32)
                  + d_ref[:, ch] * dy)
            du_ref[:, ch] = du.astype(BF16)
            dwbr_ref[kb] += lax.dot_general(ub, grb, tn, preferred_element_type=F32)
            dwbi_ref[kb] += lax.dot_general(ub, gib, tn, preferred_element_type=F32)
            dwcr_ref[kb] += lax.dot_general(srb, dyb, tn, preferred_element_type=F32)
            dwci_ref[kb] -= lax.dot_general(sib, dyb, tn, preferred_element_type=F32)

    rev = lambda i: nb - 1 - i
    wb_spec = _full_spec((SSM_KB, BLK, ST_KB))
    wc_spec = _full_spec((SSM_KB, ST_KB, BLK))
    tab_in = [tabs[0], tabs[1], tabs[2], tabs[3], tabs[6], tabs[7]]
    tab_specs = [_full_spec(t.shape) for t in tab_in]
    vec = _full_spec((1, D_MODEL))
    svec = _full_spec((1, N_STATE))
    state_spec = pl.BlockSpec((N_LB, BLK, BLK), lambda i: (0, nb - 1 - i, 0))
    return pl.pallas_call(
        body, name="ssm_bwd", grid=(nb,),
        in_specs=[_row_spec(BLK, D_MODEL, rev), _row_spec(BLK, D_MODEL, rev), _row_spec(BLK, D_MODEL, rev),
                  state_spec, state_spec,
                  wb_spec, wb_spec, wc_spec, wc_spec, vec] + tab_specs,
        out_specs=[_row_spec(BLK, D_MODEL, rev), vec, svec, svec, wb_spec, wb_spec, wc_spec, wc_spec],
        out_shape=[jax.ShapeDtypeStruct((l_dim, D_MODEL), BF16), jax.ShapeDtypeStruct((1, D_MODEL), F32),
                   jax.ShapeDtypeStruct((1, N_STATE), F32), jax.ShapeDtypeStruct((1, N_STATE), F32),
                   jax.ShapeDtypeStruct((SSM_KB, BLK, ST_KB), F32), jax.ShapeDtypeStruct((SSM_KB, BLK, ST_KB), F32),
                   jax.ShapeDtypeStruct((SSM_KB, ST_KB, BLK), F32), jax.ShapeDtypeStruct((SSM_KB, ST_KB, BLK), F32)],
        scratch_shapes=[pltpu.VMEM((8, N_STATE), F32), pltpu.VMEM((8, N_STATE), F32),
                        pltpu.VMEM((N_LB, BLK, BLK), F32), pltpu.VMEM((N_LB, BLK, BLK), F32)],
        compiler_params=_params(),
    )(dz, y, u, s_re, s_im, wb_re, wb_im, wc_re, wc_im, d_skip, *tab_in)


def _lane_slab(ref, i, r, g):
    return ref[i, _strided(r, g), :]


def _chunk_carries(xr_ref, xi_ref, i, ar, ai, sqr_ref, sqi_ref, seq_r, seq_i, cin_r, cin_i, sign, reverse):
    g = xr_ref.shape[1] // GRP
    sr = si = None
    for r in (range(GRP - 1, -1, -1) if reverse else range(GRP)):
        xr, xi = _lane_slab(xr_ref, i, r, g), _lane_slab(xi_ref, i, r, g)
        if sr is not None:
            xr, xi = xr + ar * sr - ai * si, xi + ar * si + ai * sr
        sr, si = xr, xi
    row = lax.broadcasted_iota(jnp.int32, sr.shape, 0)
    idx, s = 0, 1
    while s < g:
        br = sqr_ref[idx:idx + 1, BLK * i:BLK * i + BLK]
        bi = sign * sqi_ref[idx:idx + 1, BLK * i:BLK * i + BLK]
        shift, keep = (g - s, row < g - s) if reverse else (s, row >= s)
        pr = jnp.where(keep, pltpu.roll(sr, shift, axis=0), 0.0)
        pi = jnp.where(keep, pltpu.roll(si, shift, axis=0), 0.0)
        sr, si = sr + br * pr - bi * pi, si + br * pi + bi * pr
        idx, s = idx + 1, 2 * s
    return sr + seq_r * cin_r - seq_i * cin_i, si + seq_r * cin_i + seq_i * cin_r


def _ssm_fwd_kb(u, wb_re, wb_im, wc_re, wc_im, d_skip, tabs, q):
    l_dim = u.shape[0]
    nc = l_dim // q
    g = q // GRP

    def body(u_ref, wbr_ref, wbi_ref, wcr_ref, wci_ref, d_ref, a1r_ref, a1i_ref, sqr_ref, sqi_ref,
             seqr_ref, seqi_ref, y_ref, z_ref, sr_ref, si_ref, cr_ref, ci_ref, xr_ref, xi_ref):
        @pl.when(pl.program_id(1) == 0)
        def _():
            cr_ref[...] = jnp.zeros_like(cr_ref)
            ci_ref[...] = jnp.zeros_like(ci_ref)

        u_kb = u_ref[...]
        ub = u_kb.astype(BF16)
        xr = jnp.dot(ub, wbr_ref[0], preferred_element_type=F32)
        xi = jnp.dot(ub, wbi_ref[0], preferred_element_type=F32)
        for i in range(LB_KB):
            xr_ref[i] = xr[:, BLK * i:BLK * i + BLK]
            xi_ref[i] = xi[:, BLK * i:BLK * i + BLK]
        row = lax.broadcasted_iota(jnp.int32, (g, BLK), 0)
        for i in range(LB_KB):
            lanes = slice(BLK * i, BLK * i + BLK)
            ar, ai = a1r_ref[0:1, lanes], a1i_ref[0:1, lanes]
            cin_r, cin_i = cr_ref[0:1, lanes], ci_ref[0:1, lanes]
            tr, ti = _chunk_carries(xr_ref, xi_ref, i, ar, ai, sqr_ref, sqi_ref, seqr_ref[:, lanes],
                                    seqi_ref[:, lanes], cin_r, cin_i, 1.0, reverse=False)
            cr_ref[0:1, lanes] = jnp.sum(jnp.where(row == g - 1, tr, 0.0), axis=0, keepdims=True)
            ci_ref[0:1, lanes] = jnp.sum(jnp.where(row == g - 1, ti, 0.0), axis=0, keepdims=True)
            pr = jnp.where(row == 0, cin_r, pltpu.roll(tr, 1, axis=0))
            pi = jnp.where(row == 0, cin_i, pltpu.roll(ti, 1, axis=0))
            for r in range(GRP):
                pr, pi = (_lane_slab(xr_ref, i, r, g) + ar * pr - ai * pi,
                          _lane_slab(xi_ref, i, r, g) + ar * pi + ai * pr)
                sr_ref[i, _strided(r, g), :] = pr
                si_ref[i, _strided(r, g), :] = pi
        s_r = jnp.concatenate([sr_ref[i] for i in range(LB_KB)], axis=1)
        s_i = jnp.concatenate([si_ref[i] for i in range(LB_KB)], axis=1)
        y = (jnp.dot(s_r.astype(BF16), wcr_ref[0], preferred_element_type=F32)
             - jnp.dot(s_i.astype(BF16), wci_ref[0], preferred_element_type=F32)
             + d_ref[...] * u_kb)
        y_ref[...] = y.astype(BF16)
        z_ref[...] = _gelu(y).astype(BF16)

    chan = pl.BlockSpec((q, BLK), lambda k, c: (c, k))
    wb_spec = pl.BlockSpec((1, BLK, ST_KB), lambda k, c: (k, 0, 0))
    wc_spec = pl.BlockSpec((1, ST_KB, BLK), lambda k, c: (k, 0, 0))
    tab_specs = [pl.BlockSpec((t.shape[0], ST_KB), lambda k, c: (0, k)) for t in tabs[:6]]
    state_spec = pl.BlockSpec((LB_KB, q, BLK), lambda k, c: (k, c, 0))
    state_shape = jax.ShapeDtypeStruct((N_LB, l_dim, BLK), F32)
    return pl.pallas_call(
        body, name="ssm_fwd", grid=(SSM_KB, nc),
        in_specs=[chan, wb_spec, wb_spec, wc_spec, wc_spec, pl.BlockSpec((1, BLK), lambda k, c: (0, k))] + tab_specs,
        out_specs=[chan, chan, state_spec, state_spec],
        out_shape=[jax.ShapeDtypeStruct((l_dim, D_MODEL), BF16), jax.ShapeDtypeStruct((l_dim, D_MODEL), BF16),
                   state_shape, state_shape],
        scratch_shapes=[pltpu.VMEM((8, ST_KB), F32), pltpu.VMEM((8, ST_KB), F32),
                        pltpu.VMEM((LB_KB, q, BLK), F32), pltpu.VMEM((LB_KB, q, BLK), F32)],
        compiler_params=_params(dimension_semantics=("parallel", "arbitrary")),
    )(u, wb_re, wb_im, wc_re, wc_im, d_skip, *tabs[:6])


def _ssm_bwd_kb(dz, y, u, s_re, s_im, wb_re, wb_im, wc_re, wc_im, d_skip, tabs, q):
    l_dim = u.shape[0]
    nc = l_dim // q
    g = q // GRP

    def body(dz_ref, y_ref, u_ref, sr_ref, si_ref, wbr_ref, wbi_ref, wcr_ref, wci_ref, d_ref,
             a1r_ref, a1i_ref, sqr_ref, sqi_ref, revr_ref, revi_ref,
             du_ref, dd_ref, dar_ref, dai_ref, dwbr_ref, dwbi_ref, dwcr_ref, dwci_ref,
             cr_ref, ci_ref, gr_ref, gi_ref):
        @pl.when(pl.program_id(1) == 0)
        def _():
            for ref in (cr_ref, ci_ref, dd_ref, dar_ref, dai_ref, dwbr_ref, dwbi_ref, dwcr_ref, dwci_ref):
                ref[...] = jnp.zeros_like(ref)

        tn = (((0,), (0,)), ((), ()))
        nt = (((1,), (1,)), ((), ()))
        u_kb = u_ref[...]
        dy = dz_ref[...].astype(F32) * _gelu_grad(y_ref[...].astype(F32))
        dyb = dy.astype(BF16)
        ub = u_kb.astype(BF16)
        dd_ref[...] += jnp.sum(dy * u_kb, axis=0, keepdims=True)
        ds_r = lax.dot_general(dyb, wcr_ref[0], nt, preferred_element_type=F32)
        ds_i = -lax.dot_general(dyb, wci_ref[0], nt, preferred_element_type=F32)
        for i in range(LB_KB):
            gr_ref[i] = ds_r[:, BLK * i:BLK * i + BLK]
            gi_ref[i] = ds_i[:, BLK * i:BLK * i + BLK]
        row = lax.broadcasted_iota(jnp.int32, (g, BLK), 0)
        for i in range(LB_KB):
            lanes = slice(BLK * i, BLK * i + BLK)
            ar, ai = a1r_ref[0:1, lanes], -a1i_ref[0:1, lanes]
            cin_r, cin_i = cr_ref[0:1, lanes], ci_ref[0:1, lanes]
            tr, ti = _chunk_carries(gr_ref, gi_ref, i, ar, ai, sqr_ref, sqi_ref, revr_ref[:, lanes],
                                    -revi_ref[:, lanes], cin_r, cin_i, -1.0, reverse=True)
            cr_ref[0:1, lanes] = jnp.sum(jnp.where(row == 0, tr, 0.0), axis=0, keepdims=True)
            ci_ref[0:1, lanes] = jnp.sum(jnp.where(row == 0, ti, 0.0), axis=0, keepdims=True)
            nr = jnp.where(row == g - 1, cin_r, pltpu.roll(tr, g - 1, axis=0))
            ni = jnp.where(row == g - 1, cin_i, pltpu.roll(ti, g - 1, axis=0))
            acc_r = jnp.zeros((g, BLK), F32)
            acc_i = jnp.zeros((g, BLK), F32)
            for r in range(GRP - 1, -1, -1):
                s_r, s_i = _lane_slab(sr_ref, i, r, g), _lane_slab(si_ref, i, r, g)
                acc_r = acc_r + (nr * s_r + ni * s_i)
                acc_i = acc_i + (ni * s_r - nr * s_i)
                nr, ni = (_lane_slab(gr_ref, i, r, g) + ar * nr - ai * ni,
                          _lane_slab(gi_ref, i, r, g) + ar * ni + ai * nr)
                gr_ref[i, _strided(r, g), :] = nr
                gi_ref[i, _strided(r, g), :] = ni
            dar_ref[:, lanes] += jnp.sum(acc_r, axis=0, keepdims=True)
            dai_ref[:, lanes] += jnp.sum(acc_i, axis=0, keepdims=True)
        grb = jnp.concatenate([gr_ref[i] for i in range(LB_KB)], axis=1).astype(BF16)
        gib = jnp.concatenate([gi_ref[i] for i in range(LB_KB)], axis=1).astype(BF16)
        srb = jnp.concatenate([sr_ref[i] for i in range(LB_KB)], axis=1).astype(BF16)
        sib = jnp.concatenate([si_ref[i] for i in range(LB_KB)], axis=1).astype(BF16)
        du = (lax.dot_general(grb, wbr_ref[0], nt, preferred_element_type=F32)
              + lax.dot_general(gib, wbi_ref[0], nt, preferred_element_type=F32)
              + d_ref[...] * dy)
        du_ref[...] = du.astype(BF16)
        dwbr_ref[0] += lax.dot_general(ub, grb, tn, preferred_element_type=F32)
        dwbi_ref[0] += lax.dot_general(ub, gib, tn, preferred_element_type=F32)
        dwcr_ref[0] += lax.dot_general(srb, dyb, tn, preferred_element_type=F32)
        dwci_ref[0] -= lax.dot_general(sib, dyb, tn, preferred_element_type=F32)

    chan = pl.BlockSpec((q, BLK), lambda k, c: (nc - 1 - c, k))
    wb_spec = pl.BlockSpec((1, BLK, ST_KB), lambda k, c: (k, 0, 0))
    wc_spec = pl.BlockSpec((1, ST_KB, BLK), lambda k, c: (k, 0, 0))
    tab_in = [tabs[0], tabs[1], tabs[2], tabs[3], tabs[6], tabs[7]]
    tab_specs = [pl.BlockSpec((t.shape[0], ST_KB), lambda k, c: (0, k)) for t in tab_in]
    vec = pl.BlockSpec((1, BLK), lambda k, c: (0, k))
    svec = pl.BlockSpec((1, ST_KB), lambda k, c: (0, k))
    state_spec = pl.BlockSpec((LB_KB, q, BLK), lambda k, c: (k, nc - 1 - c, 0))
    return pl.pallas_call(
        body, name="ssm_bwd", grid=(SSM_KB, nc),
        in_specs=[chan, chan, chan, state_spec, state_spec, wb_spec, wb_spec, wc_spec, wc_spec, vec] + tab_specs,
        out_specs=[chan, vec, svec, svec, wb_spec, wb_spec, wc_spec, wc_spec],
        out_shape=[jax.ShapeDtypeStruct((l_dim, D_MODEL), BF16), jax.ShapeDtypeStruct((1, D_MODEL), F32),
                   jax.ShapeDtypeStruct((1, N_STATE), F32), jax.ShapeDtypeStruct((1, N_STATE), F32),
                   jax.ShapeDtypeStruct((SSM_KB, BLK, ST_KB), F32), jax.ShapeDtypeStruct((SSM_KB, BLK, ST_KB), F32),
                   jax.ShapeDtypeStruct((SSM_KB, ST_KB, BLK), F32), jax.ShapeDtypeStruct((SSM_KB, ST_KB, BLK), F32)],
        scratch_shapes=[pltpu.VMEM((8, ST_KB), F32), pltpu.VMEM((8, ST_KB), F32),
                        pltpu.VMEM((LB_KB, q, BLK), F32), pltpu.VMEM((LB_KB, q, BLK), F32)],
        compiler_params=_params(dimension_semantics=("parallel", "arbitrary")),
    )(dz, y, u, s_re, s_im, wb_re, wb_im, wc_re, wc_im, d_skip, *tab_in)


def _discretize(lam_re, lam_im, log_dt, b_re, b_im):
    dt = jnp.exp(log_dt)[:, None]
    mag = jnp.exp(lam_re * dt)
    ar, ai = mag * jnp.cos(lam_im * dt), mag * jnp.sin(lam_im * dt)
    den = lam_re * lam_re + lam_im * lam_im
    nr, ni = ar - 1.0, ai
    fr, fi = (nr * lam_re + ni * lam_im) / den, (ni * lam_re - nr * lam_im) / den
    bbar_re = fr[..., None] * b_re - fi[..., None] * b_im
    bbar_im = fr[..., None] * b_im + fi[..., None] * b_re
    return ar, ai, bbar_re, bbar_im


def _block_diag_b(bbar):
    eye = jnp.eye(8, dtype=bbar.dtype)
    return jnp.einsum("kgpc,gh->kgchp", bbar.reshape(8, 8, SSM_STATE, SSM_GROUP_CH), eye).reshape(8, BLK, ST_KB)


def _block_diag_b_t(dwb):
    eye = jnp.eye(8, dtype=dwb.dtype)
    return jnp.einsum("kgchp,gh->kgpc", dwb.reshape(8, 8, SSM_GROUP_CH, 8, SSM_STATE), eye).reshape(
        SSM_GROUPS, SSM_STATE, SSM_GROUP_CH)


def _block_diag_c(c):
    eye = jnp.eye(8, dtype=c.dtype)
    return jnp.einsum("kgcp,gh->kgphc", c.reshape(8, 8, SSM_GROUP_CH, SSM_STATE), eye).reshape(8, ST_KB, BLK)


def _block_diag_c_t(dwc):
    eye = jnp.eye(8, dtype=dwc.dtype)
    return jnp.einsum("kgphc,gh->kgcp", dwc.reshape(8, 8, SSM_STATE, 8, SSM_GROUP_CH), eye).reshape(
        SSM_GROUPS, SSM_GROUP_CH, SSM_STATE)


def _powers(br, bi, n):
    pr, pi = br, bi
    cr, ci = br, bi
    while pr.shape[0] < n:
        pr, pi = (jnp.concatenate([pr, pr * cr - pi * ci], axis=0),
                  jnp.concatenate([pi, pr * ci + pi * cr], axis=0))
        cr, ci = cr * cr - ci * ci, 2.0 * cr * ci
    return pr[:n], pi[:n]


def _powers_desc(br, bi, n):
    pr, pi = br, bi
    cr, ci = br, bi
    while pr.shape[0] < n:
        pr, pi = (jnp.concatenate([pr * cr - pi * ci, pr], axis=0),
                  jnp.concatenate([pr * ci + pi * cr, pi], axis=0))
        cr, ci = cr * cr - ci * ci, 2.0 * cr * ci
    return pr, pi


def _power_tables(ar, ai, g):
    a1r, a1i = _powers(ar, ai, GRP)
    seqr, seqi = _powers(a1r[GRP - 1:], a1i[GRP - 1:], g)
    g2 = 1 << (g - 1).bit_length()
    revr, revi = _powers_desc(a1r[GRP - 1:], a1i[GRP - 1:], g2)
    revr, revi = revr[g2 - g:], revi[g2 - g:]
    sq_r, sq_i = [seqr[0:1]], [seqi[0:1]]
    while len(sq_r) < 8:
        r, i = sq_r[-1], sq_i[-1]
        sq_r.append(r * r - i * i)
        sq_i.append(2.0 * r * i)
    sqr, sqi = jnp.concatenate(sq_r, axis=0), jnp.concatenate(sq_i, axis=0)
    return a1r, a1i, sqr, sqi, seqr, seqi, revr, revi


HBM_SPEC = pl.BlockSpec(memory_space=pltpu.HBM)
SEM_SPEC = pl.BlockSpec(memory_space=pltpu.SEMAPHORE)
DATAFLOW = pltpu.SideEffectType.DATAFLOW_SIDE_EFFECTING


def _plain_rows(p, m):
    return p * m


def _ffn_in_rows(p, m):
    return ((p & 3) >> 1) * (4 * m) + (p >> 2) * (2 * m) + (p & 1) * m


def _peer_copies(src_refs, land_refs, send_sems, recv_sems, chunked, row_fns):
    x, y, c = lax.axis_index("x"), lax.axis_index("y"), lax.axis_index("c")
    me = 4 * x + 2 * y + c
    copies = []
    for a, (src, land) in enumerate(zip(src_refs, land_refs)):
        m = land.shape[0] // N_DEV
        for k in range(N_DEV - 1):
            rel = k + 1
            bx, by, bc = (rel >> 2) & 1, (rel >> 1) & 1, rel & 1
            peer = (x + bx - 2 * x * bx, y + by - 2 * y * by, c + bc - 2 * c * bc)
            p_idx = 4 * peer[0] + 2 * peer[1] + peer[2]
            copies.append(pltpu.make_async_remote_copy(
                src_ref=src.at[pl.ds(row_fns[a](p_idx, m), m), :] if chunked else src,
                dst_ref=land.at[pl.ds(me * m if chunked else row_fns[a](me, m), m), :],
                send_sem=send_sems[a * (N_DEV - 1) + k], recv_sem=recv_sems[a * (N_DEV - 1) + k],
                device_id=peer, device_id_type=MESH))
    return copies


def _send_start(name, srcs, lands, chunked, row_fns=None):
    n = len(srcs)
    ns = n * (N_DEV - 1)
    row_fns = row_fns or [_plain_rows] * n

    def body(*refs):
        src_refs, land_refs = refs[:n], refs[n:2 * n]
        send_sems, recv_sems = refs[2 * n:2 * n + ns], refs[2 * n + ns:2 * n + 2 * ns]
        token = refs[-1]
        for cp in _peer_copies(src_refs, land_refs, send_sems, recv_sems, chunked, row_fns):
            cp.start()
        token[...] = jnp.zeros_like(token)

    ins = [pltpu.with_memory_space_constraint(a, pltpu.HBM) for a in list(srcs) + list(lands)]
    out = pl.pallas_call(
        body, name=name,
        in_specs=[HBM_SPEC] * (2 * n),
        out_specs=[SEM_SPEC] * (2 * ns) + [HBM_SPEC] * (2 * n) + [pl.BlockSpec(memory_space=pltpu.VMEM)],
        out_shape=[pltpu.SemaphoreType.DMA(())] * (2 * ns)
        + [pltpu.HBM(a.shape, a.dtype) for a in list(srcs) + list(lands)]
        + [jax.ShapeDtypeStruct((8, BLK), F32)],
        input_output_aliases={i: i + 2 * ns for i in range(2 * n)},
        compiler_params=pltpu.CompilerParams(has_side_effects=DATAFLOW),
    )(*ins)
    return out[:ns], out[ns:2 * ns], out[2 * ns:2 * ns + n], out[2 * ns + n:2 * ns + 2 * n], out[-1]


def _send_wait(name, send_sems, recv_sems, srcs, lands, after, chunked, row_fns=None):
    n = len(srcs)
    ns = n * (N_DEV - 1)
    row_fns = row_fns or [_plain_rows] * n

    def body(*refs):
        src_refs, land_refs = refs[:n], refs[n:2 * n]
        s_sems, r_sems = refs[2 * n:2 * n + ns], refs[2 * n + ns:2 * n + 2 * ns]
        copies = _peer_copies(src_refs, land_refs, s_sems, r_sems, chunked, row_fns)
        for cp in copies:
            cp.wait_send()
        for cp in copies:
            cp.wait_recv()

    out = pl.pallas_call(
        body, name=name,
        in_specs=[HBM_SPEC] * (2 * n) + [SEM_SPEC] * (2 * ns) + [pl.BlockSpec(memory_space=pl.ANY)],
        out_specs=[HBM_SPEC] * (2 * n),
        out_shape=[pltpu.HBM(a.shape, a.dtype) for a in list(srcs) + list(lands)],
        input_output_aliases={i: i for i in range(2 * n)},
        compiler_params=pltpu.CompilerParams(has_side_effects=DATAFLOW),
    )(*srcs, *lands, *send_sems, *recv_sems, after)
    return out[n:]


def _sum_slots(name, recv, own):
    m, ncol = own.shape
    tr = m // 2 if (m // 2) % 16 == 0 else m
    g = m // tr

    def body(*refs):
        slots, own_ref, o_ref = refs[:N_DEV], refs[N_DEV], refs[N_DEV + 1]
        me = _my_index()
        tot = None
        for s in range(N_DEV):
            v = jnp.where(me == s, own_ref[...], slots[s][...].astype(F32))
            tot = v if tot is None else tot + v
        o_ref[...] = tot

    def slot_spec(s):
        return pl.BlockSpec((tr, ncol), lambda i: (s * g + i, 0))

    return pl.pallas_call(
        body, name=name, grid=(g,),
        in_specs=[slot_spec(s) for s in range(N_DEV)] + [pl.BlockSpec((tr, ncol), lambda i: (i, 0))],
        out_specs=pl.BlockSpec((tr, ncol), lambda i: (i, 0)),
        out_shape=jax.ShapeDtypeStruct((m, ncol), F32),
        compiler_params=_params(),
    )(*([recv] * N_DEV), own)


def _sum_gathered(name, gathered, rows):
    tr = _pick(rows, 512, 8)
    g = rows // tr

    def body(*refs):
        o_ref = refs[N_DEV]
        tot = refs[0][...]
        for s in range(1, N_DEV):
            tot = tot + refs[s][...]
        o_ref[...] = tot

    return pl.pallas_call(
        body, name=name, grid=(g,),
        in_specs=[pl.BlockSpec((tr, BLK), (lambda i, s=s: (s * g + i, 0))) for s in range(N_DEV)],
        out_specs=pl.BlockSpec((tr, BLK), lambda i: (i, 0)),
        out_shape=jax.ShapeDtypeStruct((rows, BLK), F32),
        compiler_params=_params(),
    )(*([gathered] * N_DEV))


def _adamw(name, w, g, m, v):
    r, c = w.shape
    tr = _pick(r, 256, 8) if r % 8 == 0 else r
    c1 = 1.0 - ADAM_B1 ** ADAM_STEP
    c2 = 1.0 - ADAM_B2 ** ADAM_STEP

    def body(w_ref, g_ref, m_ref, v_ref, d_ref, nm_ref, nv_ref):
        gv = g_ref[...]
        nm = ADAM_B1 * m_ref[...] + (1.0 - ADAM_B1) * gv
        nv = ADAM_B2 * v_ref[...] + (1.0 - ADAM_B2) * (gv * gv)
        m_hat = nm / c1
        v_hat = nv / c2
        d_ref[...] = -ADAM_LR * (m_hat / (jnp.sqrt(v_hat) + ADAM_EPS) + ADAM_WD * w_ref[...])
        nm_ref[...] = nm
        nv_ref[...] = nv

    spec = pl.BlockSpec((tr, c), lambda i: (i, 0))
    shape = jax.ShapeDtypeStruct((r, c), F32)
    return pl.pallas_call(
        body, name=name, grid=(r // tr,),
        in_specs=[spec] * 4, out_specs=[spec] * 3, out_shape=[shape] * 3,
        compiler_params=_params(),
    )(w, g, m, v)


def _adamw_many(name, ws, gs, ms, vs):
    n = len(ws)
    c1 = 1.0 - ADAM_B1 ** ADAM_STEP
    c2 = 1.0 - ADAM_B2 ** ADAM_STEP

    def body(*refs):
        for a in range(n):
            w_ref, g_ref, m_ref, v_ref = refs[a], refs[n + a], refs[2 * n + a], refs[3 * n + a]
            d_ref, nm_ref, nv_ref = refs[4 * n + a], refs[5 * n + a], refs[6 * n + a]
            gv = g_ref[...]
            nm = ADAM_B1 * m_ref[...] + (1.0 - ADAM_B1) * gv
            nv = ADAM_B2 * v_ref[...] + (1.0 - ADAM_B2) * (gv * gv)
            d_ref[...] = -ADAM_LR * ((nm / c1) / (jnp.sqrt(nv / c2) + ADAM_EPS) + ADAM_WD * w_ref[...])
            nm_ref[...] = nm
            nv_ref[...] = nv

    vmem = pl.BlockSpec(memory_space=pltpu.VMEM)
    shapes = [jax.ShapeDtypeStruct(w.shape, F32) for w in ws]
    out = pl.pallas_call(
        body, name=name,
        in_specs=[vmem] * (4 * n), out_specs=[vmem] * (3 * n), out_shape=shapes * 3,
        compiler_params=_params(),
    )(*ws, *gs, *ms, *vs)
    return out[:n], out[n:2 * n], out[2 * n:]


PACK_ROWS = 128


def _pack(parts):
    flat = []
    for p in parts:
        v = p.reshape(-1)
        flat.append(jnp.pad(v, (0, (-v.shape[0]) % BLK)))
    v = jnp.concatenate(flat)
    v = jnp.pad(v, (0, (-v.shape[0]) % (PACK_ROWS * BLK)))
    return v.reshape(-1, BLK)


def _unpack(buf, shapes):
    flat = buf.reshape(-1)
    out, off = [], 0
    for shp in shapes:
        size = math.prod(shp)
        out.append(flat[off:off + size].reshape(shp))
        off += size + (-size) % BLK
    return out


def kernel(x, meta_tokens, norm_mix, w_in, q_norm, k_norm, attn_sinks, lam_re, lam_im, log_dt, ssm_b_re, ssm_b_im, ssm_c_re, ssm_c_im, ssm_d, w_glu, attn_branch_norm, ssm_branch_norm, w_out, norm_ffn, w_ffn_in, w_ffn_out, loss_target, m_meta_tokens, m_norm_mix, m_w_in, m_q_norm, m_k_norm, m_attn_sinks, m_lam_re, m_lam_im, m_log_dt, m_ssm_b_re, m_ssm_b_im, m_ssm_c_re, m_ssm_c_im, m_ssm_d, m_w_glu, m_attn_branch_norm, m_ssm_branch_norm, m_w_out, m_norm_ffn, m_w_ffn_in, m_w_ffn_out, v_meta_tokens, v_norm_mix, v_w_in, v_q_norm, v_k_norm, v_attn_sinks, v_lam_re, v_lam_im, v_log_dt, v_ssm_b_re, v_ssm_b_im, v_ssm_c_re, v_ssm_c_im, v_ssm_d, v_w_glu, v_attn_branch_norm, v_ssm_branch_norm, v_w_out, v_norm_ffn, v_w_ffn_in, v_w_ffn_out):
    args = dict(locals())
    weights = {n: args[n] for n in WEIGHTS}
    mom_m = {n: args["m_" + n] for n in WEIGHTS}
    mom_v = {n: args["v_" + n] for n in WEIGHTS}

    x2d = x[0]
    target2d = loss_target[0]
    s_len = x2d.shape[0]
    l_dim = s_len + BLK
    tm_row = _pick(l_dim, 320)
    tm_mm = _pick(l_dim, 1040)
    tl_tn = _pick(l_dim, 1664)
    tm_ffn = _pick(l_dim, 640)
    tm_big = _pick(l_dim, 2080)
    tm_shift = _pick(l_dim, 640, BLK)

    shard_in = w_in[0].T.astype(BF16)
    shard_glu = w_glu[0].T.astype(BF16)
    shard_out = w_out[0].astype(BF16)
    shard_ffn_in = w_ffn_in[0].T.astype(BF16)
    shard_ffn_out = w_ffn_out[0].astype(BF16)
    shard_meta = meta_tokens.T
    me = _my_index()

    def landing(shard, row_fn=_plain_rows):
        m_rows, cols = shard.shape
        return lax.dynamic_update_slice(lax.empty((N_DEV * m_rows, cols), shard.dtype), shard,
                                        (row_fn(me, m_rows), 0))

    first = [shard_in, shard_meta]
    ga = _send_start("gather_start_a", first, [landing(s) for s in first], chunked=False)
    later = [shard_glu + ga[4][0:1, 0:1].astype(BF16), shard_out, shard_ffn_in, shard_ffn_out]
    later_fns = [_plain_rows, _plain_rows, _ffn_in_rows, _plain_rows]
    gb = _send_start("gather_start_b", later, [landing(s, f) for s, f in zip(later, later_fns)], chunked=False,
                     row_fns=later_fns)

    nm_t = norm_mix + (ga[4][0:1, 0:1] + gb[4][0:1, 0:1])
    qn_t, kn_t = jnp.tile(q_norm, (1, N_Q_HEADS)), jnp.tile(k_norm, (1, N_KV_HEADS))
    e_mat = jnp.kron(jnp.eye(4, dtype=F32), jnp.ones((HEAD_DIM, HEAD_DIM), F32)).astype(BF16)

    def disc(lr, li, ldt, br, bi):
        return _discretize(lr[0], li[0], ldt[0], br[0], bi[0])

    (abar_re, abar_im, bbar_re, bbar_im), disc_vjp = jax.vjp(disc, lam_re, lam_im, log_dt, ssm_b_re, ssm_b_im)
    wb_re, wb_im = _block_diag_b(bbar_re).astype(BF16), _block_diag_b(bbar_im).astype(BF16)
    wc_re, wc_im = _block_diag_c(ssm_c_re[0]).astype(BF16), _block_diag_c(ssm_c_im[0]).astype(BF16)
    q_ssm = _pick(l_dim, 640, 64)
    tabs = _power_tables(abar_re.reshape(1, N_STATE), abar_im.reshape(1, N_STATE), q_ssm // GRP)

    h0, xn = _embed_norm(x2d, nm_t, tm_shift)
    wt_in, meta_t = _send_wait("gather_wait_a", ga[0], ga[1], ga[2], ga[3], xn, chunked=False)
    meta_pad = jnp.pad(meta_t.T, ((PAD, 0), (0, 0)))
    h0, xn = _embed_meta(meta_pad, nm_t, h0, xn)
    qkv = _matmul("proj_qkv", xn, wt_in, nt=True, tm=tm_big, tn=512, tk=D_MODEL, n=Q_W + 2 * KV_W, w_off=0)
    u = _matmul("proj_u", xn, wt_in, nt=True, tm=tm_big, tn=512, tk=D_MODEL, n=D_MODEL, w_off=3)
    gates = _matmul("proj_gates", xn, wt_in, nt=True, tm=tm_big, tn=512, tk=D_MODEL, n=2 * D_MODEL, w_off=5,
                    out_dtype=BF16)
    qn, kf, vf = _qk_prep(qkv, qn_t, kn_t, e_mat, tm_row)
    attn, lse = _attn_fwd(qn, kf, vf, attn_sinks)
    y, z, s_re, s_im = _ssm_fwd_kb(u, wb_re, wb_im, wc_re, wc_im, ssm_d, tabs, q_ssm)
    wt_glu, w_out_f, wt_ffn_in, w_ffn_out_f = _send_wait("gather_wait_b", gb[0], gb[1], gb[2], gb[3], z,
                                                         chunked=False, row_fns=later_fns)
    zab = _matmul("glu_proj", z, wt_glu, nt=True, tm=tm_big, tn=1024, tk=D_MODEL, out_dtype=BF16)
    merged = _merge_fwd(attn, zab, gates, attn_branch_norm, ssm_branch_norm, tm_row)
    h1, hn = _matmul("out_proj", merged, w_out_f, nt=False, tm=tm_mm, tn=1024, tk=D_MODEL, res=h0,
                     norm_g=norm_ffn)
    gu, act = _ffn_in_swiglu(hn, wt_ffn_in, tm_ffn)
    h2 = _matmul("ffn_out", act, w_ffn_out_f, nt=False, tm=tm_mm, tn=1024, tk=1408, res=h1)
    dh2, dh2_b, loss_part = _loss_grad(h2, target2d, tm_shift)

    dgu = _d_act_swiglu(dh2_b, w_ffn_out_f, gu, tm_ffn)

    def exchange_start(name, grads_b, row_fns=None):
        return _send_start(name, grads_b, [jnp.zeros(g.shape, BF16) for g in grads_b], chunked=True,
                           row_fns=row_fns)

    g_ffn_out, g_ffn_out_b = _matmul_tn("g_ffn_out", act, dh2_b, tm=1408, tn=1024, tl=tl_tn)
    g_ffn_in_t, g_ffn_in_b = _matmul_tn("g_ffn_in", dgu, hn, tm=1408, tn=1024, tl=tl_tn)
    ffn_fns = [_ffn_in_rows, _plain_rows]
    ex1 = exchange_start("exchange_start_ffn", [g_ffn_in_b, g_ffn_out_b], ffn_fns)
    dhn = _matmul("d_hn", dgu, wt_ffn_in, nt=False, tm=tm_big, tn=1024, tk=1408, out_dtype=BF16)
    dh1, dh1_b, g_norm_ffn = _norm_bwd_res("ffn_norm_bwd", h1, norm_ffn + ex1[4][0:1, 0:1], dhn, dh2, tm_row)
    dmerged = _matmul("d_merged", dh1_b, w_out_f, nt=True, tm=tm_big, tn=1024, tk=D_MODEL, out_dtype=BF16)
    dattn, dzab, dgates, g_abn, g_sbn = _merge_bwd(attn, zab, gates, attn_branch_norm, ssm_branch_norm,
                                                     dmerged, tm_row)
    g_out, g_out_b = _matmul_tn("g_out", merged, dh1_b, tm=1024, tn=1024, tl=tl_tn)
    dz = _matmul("d_z", dzab, wt_glu, nt=False, tm=tm_big, tn=1024, tk=1024, out_dtype=BF16)
    g_glu_t, g_glu_b = _matmul_tn("g_glu", dzab, z, tm=1024, tn=1024, tl=tl_tn)
    ex2 = exchange_start("exchange_start_mix", [g_glu_b, g_out_b])
    du, g_ssm_d, g_ar, g_ai, g_wbr, g_wbi, g_wcr, g_wci = _ssm_bwd_kb(
        dz, y, u, s_re, s_im, wb_re, wb_im, wc_re, wc_im, ssm_d + ex2[4][0:1, 0:1], tabs, q_ssm)
    dq, dkc, dkp, dkm, dvc, dvp, dvm, g_sinks = _attn_bwd(qn, kf, vf, attn_sinks, lse, attn, dattn)
    dqkv, g_qn_t, g_kn_t = _qk_bwd(qkv, qn_t, kn_t, e_mat, dq, dkc, dkp, dkm, dvc, dvp, dvm)
    g_lam_re, g_lam_im, g_log_dt, g_b_re, g_b_im = disc_vjp(
        (g_ar.reshape(SSM_GROUPS, SSM_STATE), g_ai.reshape(SSM_GROUPS, SSM_STATE),
         _block_diag_b_t(g_wbr), _block_diag_b_t(g_wbi)))
    small_grads = {
        "q_norm": g_qn_t.reshape(N_Q_HEADS, HEAD_DIM).sum(0)[None],
        "k_norm": g_kn_t.reshape(N_KV_HEADS, HEAD_DIM).sum(0)[None], "attn_sinks": g_sinks,
        "lam_re": g_lam_re, "lam_im": g_lam_im, "log_dt": g_log_dt, "ssm_b_re": g_b_re, "ssm_b_im": g_b_im,
        "ssm_c_re": _block_diag_c_t(g_wcr)[None], "ssm_c_im": _block_diag_c_t(g_wci)[None],
        "ssm_d": g_ssm_d, "attn_branch_norm": g_abn, "ssm_branch_norm": g_sbn, "norm_ffn": g_norm_ffn,
    }
    early = [n for n in SMALL if n != "norm_mix"]
    packed_e = _pack([small_grads[n] for n in early])
    gs_e = _send_start("small_start_a", [packed_e], [landing(packed_e)], chunked=False)

    dproj = jnp.concatenate([dqkv, du, dgates], axis=1)
    g_in_t, g_in_b = _matmul_tn("g_in", dproj, xn, tm=1152, tn=1024, tl=tl_tn, after=gs_e[4])
    ex3 = exchange_start("exchange_start_in", [g_in_b])
    dxn = _matmul("d_xn", dproj, wt_in, nt=False, tm=tm_big, tn=1024, tk=1152, out_dtype=BF16)
    grad_x2d, dmeta_blk, g_norm_mix = _final_bwd(h0, nm_t + ex3[4][0:1, 0:1], dxn, dh1)
    packed_l = _pack([g_norm_mix, dmeta_blk[PAD:], loss_part])
    gs_l = _send_start("small_start_b", [packed_l], [landing(packed_l)], chunked=False)
    grads, deltas, new_m, new_v = {}, {}, {}, {}

    recv_ffn_in, recv_ffn_out = _send_wait("exchange_wait_ffn", ex1[0], ex1[1], ex1[2], ex1[3], gs_l[4],
                                           chunked=True, row_fns=ffn_fns)
    recv_glu, recv_out = _send_wait("exchange_wait_mix", ex2[0], ex2[1], ex2[2], ex2[3], recv_ffn_in,
                                    chunked=True)
    (recv_in,) = _send_wait("exchange_wait_in", ex3[0], ex3[1], ex3[2], ex3[3], recv_glu, chunked=True)
    big = [("w_in", g_in_t, True, recv_in, _plain_rows), ("w_glu", g_glu_t, True, recv_glu, _plain_rows),
           ("w_out", g_out, False, recv_out, _plain_rows), ("w_ffn_in", g_ffn_in_t, True, recv_ffn_in, _ffn_in_rows),
           ("w_ffn_out", g_ffn_out, False, recv_ffn_out, _plain_rows)]
    for name, g_full, transposed, recv, row_fn in big:
        m_rows = g_full.shape[0] // N_DEV
        own = lax.dynamic_slice(g_full, (row_fn(me, m_rows), 0), (m_rows, g_full.shape[1]))
        g_shard = _sum_slots("sum_" + name, recv, own)
        grads[name] = (g_shard.T if transposed else g_shard)[None]

    def adamw_2d(name):
        shp = weights[name].shape
        as2d = lambda a: a.reshape(shp[-2], shp[-1])
        d, nm, nv = _adamw("adamw_" + name, as2d(weights[name]), as2d(grads[name]), as2d(mom_m[name]),
                           as2d(mom_v[name]))
        deltas[name], new_m[name], new_v[name] = d.reshape(shp), nm.reshape(shp), nv.reshape(shp)
        return d

    for name in ["w_in", "w_glu", "w_out", "w_ffn_in", "w_ffn_out"]:
        last = adamw_2d(name)

    def small_sum(tag, gs, packed, after):
        (gathered,) = _send_wait("small_wait_" + tag, gs[0], gs[1], gs[2], gs[3], after, chunked=False)
        return _sum_gathered("sum_small_" + tag, gathered, packed.shape[0])

    def small_adamw(tag, names):
        d, nm, nv = _adamw_many("adamw_small_" + tag, [weights[n] for n in names], [grads[n] for n in names],
                                [mom_m[n] for n in names], [mom_v[n] for n in names])
        deltas.update(zip(names, d))
        new_m.update(zip(names, nm))
        new_v.update(zip(names, nv))

    g_sum_e = small_sum("a", gs_e, packed_e, last)
    grads.update(zip(early, _unpack(g_sum_e, [weights[n].shape for n in early])))
    wide = [n for n in early if n.startswith(("ssm_b", "ssm_c"))]
    small_adamw("wide", wide)
    g_sum_l = small_sum("b", gs_l, packed_l, g_sum_e)
    grads["norm_mix"], g_meta, loss_sum = _unpack(g_sum_l, [weights["norm_mix"].shape, (N_META, D_MODEL), (1, 1)])
    small_adamw("rest", [n for n in SMALL if n not in wide])
    grads["meta_tokens"] = lax.dynamic_slice(g_meta, (0, me * BLK), (N_META, BLK))
    adamw_2d("meta_tokens")

    loss = loss_sum[0, 0]
    return (loss, grad_x2d[None], *[grads[n] for n in WEIGHTS], *[deltas[n] for n in WEIGHTS],
            *[new_m[n] for n in WEIGHTS], *[new_v[n] for n in WEIGHTS])
```

```python
import math

import jax
import jax.numpy as jnp
from jax import lax
from jax.experimental import pallas as pl
from jax.experimental.pallas import tpu as pltpu

F32 = jnp.float32
BF16 = jnp.bfloat16

D_MODEL = 1024
N_META = 16
HEAD_DIM = 64
N_Q_HEADS = 16
N_KV_HEADS = 4
Q_W = N_Q_HEADS * HEAD_DIM
KV_W = N_KV_HEADS * HEAD_DIM
SSM_GROUPS = 64
SSM_GROUP_CH = 16
SSM_STATE = 64
N_STATE = SSM_GROUPS * SSM_STATE
D_FF = 2816
IN_COLS = Q_W + 2 * KV_W + 3 * D_MODEL
EPS = 1e-6
BLK = 128
PAD = BLK - N_META
N_DEV = 8
NEG = -1e30
SSM_KB = 8
ST_KB = N_STATE // SSM_KB
LB_KB = ST_KB // BLK
N_LB = N_STATE // BLK

ADAM_LR = 0.001
ADAM_B1 = 0.9
ADAM_B2 = 0.999
ADAM_EPS = 1e-08
ADAM_WD = 0.01
ADAM_STEP = 10

VMEM_LIMIT = 48 * 1024 * 1024
MESH = pl.DeviceIdType.MESH

SMALL = ["norm_mix", "q_norm", "k_norm", "attn_sinks", "lam_re", "lam_im", "log_dt", "ssm_b_re", "ssm_b_im",
         "ssm_c_re", "ssm_c_im", "ssm_d", "attn_branch_norm", "ssm_branch_norm", "norm_ffn"]
WEIGHTS = ["meta_tokens", "norm_mix", "w_in", "q_norm", "k_norm", "attn_sinks", "lam_re", "lam_im", "log_dt",
           "ssm_b_re", "ssm_b_im", "ssm_c_re", "ssm_c_im", "ssm_d", "w_glu", "attn_branch_norm",
           "ssm_branch_norm", "w_out", "norm_ffn", "w_ffn_in", "w_ffn_out"]


def _params(**kw):
    return pltpu.CompilerParams(vmem_limit_bytes=VMEM_LIMIT, **kw)


def _pick(n, cap, mult=16):
    best = None
    for d in range(mult, min(n, cap) + 1, mult):
        if n % d == 0:
            best = d
    assert best is not None, (n, cap, mult)
    return best


def _my_index():
    return 4 * lax.axis_index("x") + 2 * lax.axis_index("y") + lax.axis_index("c")


def _rms(x, g):
    r = lax.rsqrt(jnp.mean(x * x, axis=-1, keepdims=True) + EPS)
    return x * r * g


def _rms_bwd(x, g, dy):
    r = lax.rsqrt(jnp.mean(x * x, axis=-1, keepdims=True) + EPS)
    t = dy * g
    dx = r * t - x * (r * r * r) * jnp.mean(t * x, axis=-1, keepdims=True)
    dg = jnp.sum(dy * (x * r), axis=0, keepdims=True)
    return dx, dg


def _sigmoid(x):
    return jax.nn.sigmoid(x)


def _gelu(x):
    k = math.sqrt(2.0 / math.pi)
    return 0.5 * x * (1.0 + jnp.tanh(k * (x + 0.044715 * (x * x * x))))


def _gelu_grad(x):
    k = math.sqrt(2.0 / math.pi)
    t = jnp.tanh(k * (x + 0.044715 * (x * x * x)))
    return 0.5 * (1.0 + t) + 0.5 * x * (1.0 - t * t) * (k * (1.0 + 3.0 * 0.044715 * (x * x)))


def _head_mean(x, e_ref):
    hi = x.astype(BF16)
    r1 = x - hi.astype(F32)
    mid = r1.astype(BF16)
    lo = (r1 - mid.astype(F32)).astype(BF16)
    e = e_ref[...]
    out = []
    for b in range(x.shape[1] // 256):
        sl = slice(256 * b, 256 * b + 256)
        s = (jnp.dot(hi[:, sl], e, preferred_element_type=F32)
             + jnp.dot(mid[:, sl], e, preferred_element_type=F32)
             + jnp.dot(lo[:, sl], e, preferred_element_type=F32))
        out.append(s)
    s = out[0] if len(out) == 1 else jnp.concatenate(out, axis=1)
    return s * (1.0 / HEAD_DIM)


def _head_rms(x, g, e_ref):
    r = lax.rsqrt(_head_mean(x * x, e_ref) + EPS)
    return x * r * g


def _head_rms_bwd(x, g, dy, e_ref):
    r = lax.rsqrt(_head_mean(x * x, e_ref) + EPS)
    t = dy * g
    dx = r * t - x * (r * r * r) * _head_mean(t * x, e_ref)
    dg = jnp.sum(dy * (x * r), axis=0, keepdims=True)
    return dx, dg


def _lane_half(shape):
    lane = lax.broadcasted_iota(jnp.int32, shape, len(shape) - 1)
    return (lane >> 6) & 1


def _matmul(name, a, w, *, nt, tm, tn, tk, n=None, w_off=0, res=None, norm_g=None, out_dtype=F32):
    m_dim, k_dim = a.shape
    n_dim = n if n is not None else (w.shape[0] if nt else w.shape[1])
    gm, gn, gk = m_dim // tm, n_dim // tn, k_dim // tk
    assert gm * tm == m_dim and gn * tn == n_dim and gk * tk == k_dim, (name, a.shape, w.shape, tm, tn, tk)
    assert norm_g is None or tn == n_dim
    direct = out_dtype == F32
    dn = (((1,), (1,)), ((), ())) if nt else (((1,), (0,)), ((), ()))

    def body(*refs):
        refs = list(refs)
        a_ref, w_ref = refs[0], refs[1]
        pos = 2
        r_ref = g_ref = on_ref = None
        if res is not None:
            r_ref, pos = refs[pos], pos + 1
        if norm_g is not None:
            g_ref, pos = refs[pos], pos + 1
        o_ref, pos = refs[pos], pos + 1
        if norm_g is not None:
            on_ref, pos = refs[pos], pos + 1
        acc = o_ref if direct else refs[pos]
        k = pl.program_id(2)
        part = lax.dot_general(a_ref[...], w_ref[...], dn, preferred_element_type=F32)

        @pl.when(k == 0)
        def _():
            acc[...] = part if r_ref is None or not direct else r_ref[...] + part

        @pl.when(k > 0)
        def _():
            acc[...] += part

        @pl.when(k == gk - 1)
        def _():
            if not direct:
                r = acc[...]
                if r_ref is not None:
                    r = r_ref[...] + r
                o_ref[...] = r.astype(out_dtype)
            if on_ref is not None:
                on_ref[...] = _rms(o_ref[...].astype(F32), g_ref[...]).astype(BF16)

    if nt:
        w_spec = pl.BlockSpec((tn, tk), lambda i, j, k: (j + w_off, k))
    else:
        w_spec = pl.BlockSpec((tk, tn), lambda i, j, k: (k, j))
    in_specs = [pl.BlockSpec((tm, tk), lambda i, j, k: (i, k)), w_spec]
    args = [a, w]
    out_spec = pl.BlockSpec((tm, tn), lambda i, j, k: (i, j))
    out_specs, out_shape = [out_spec], [jax.ShapeDtypeStruct((m_dim, n_dim), out_dtype)]
    if res is not None:
        in_specs.append(out_spec)
        args.append(res)
    if norm_g is not None:
        in_specs.append(pl.BlockSpec((1, tn), lambda i, j, k: (0, 0)))
        args.append(norm_g)
        out_specs.append(out_spec)
        out_shape.append(jax.ShapeDtypeStruct((m_dim, n_dim), BF16))
    out = pl.pallas_call(
        body, name=name, grid=(gm, gn, gk),
        in_specs=in_specs, out_specs=out_specs, out_shape=out_shape,
        scratch_shapes=[] if direct else [pltpu.VMEM((tm, tn), F32)],
        compiler_params=_params(dimension_semantics=("parallel", "parallel", "arbitrary")),
    )(*args)
    return out if norm_g is not None else out[0]


def _matmul_tn(name, a, b, *, tm, tn, tl, after=None):
    l_dim, m_dim = a.shape
    n_dim = b.shape[1]
    gm, gn, gl = m_dim // tm, n_dim // tn, l_dim // tl
    assert gm * tm == m_dim and gn * tn == n_dim and gl * tl == l_dim, (name, a.shape, b.shape, tm, tn, tl)

    def body(*refs):
        a_ref, b_ref = refs[0], refs[1]
        o_ref, ob_ref = refs[-2], refs[-1]

        @pl.when(pl.program_id(2) == 0)
        def _():
            o_ref[...] = jnp.zeros_like(o_ref)

        o_ref[...] += lax.dot_general(a_ref[...], b_ref[...], (((0,), (0,)), ((), ())),
                                      preferred_element_type=F32)

        @pl.when(pl.program_id(2) == gl - 1)
        def _():
            ob_ref[...] = o_ref[...].astype(BF16)

    out_spec = pl.BlockSpec((tm, tn), lambda i, j, l: (i, j))
    in_specs = [pl.BlockSpec((tl, tm), lambda i, j, l: (l, i)), pl.BlockSpec((tl, tn), lambda i, j, l: (l, j))]
    args = [a, b]
    if after is not None:
        in_specs.append(pl.BlockSpec(memory_space=pl.ANY))
        args.append(after)
    return pl.pallas_call(
        body, name=name, grid=(gm, gn, gl),
        in_specs=in_specs,
        out_specs=[out_spec, out_spec],
        out_shape=[jax.ShapeDtypeStruct((m_dim, n_dim), F32), jax.ShapeDtypeStruct((m_dim, n_dim), BF16)],
        compiler_params=_params(dimension_semantics=("parallel", "parallel", "arbitrary")),
    )(*args)


def _row_spec(tm, cols, f=None):
    if f is None:
        return pl.BlockSpec((tm, cols), lambda i: (i, 0))
    return pl.BlockSpec((tm, cols), lambda i: (f(i), 0))


def _full_spec(shape):
    nd = len(shape)
    return pl.BlockSpec(shape, lambda i: (0,) * nd)


def _shifted_specs(n_sub, n_blocks):
    return [_row_spec(BLK, D_MODEL, (lambda i, k=k: jnp.clip(n_sub * i - 1 + k, 0, n_blocks - 1)))
            for k in range(n_sub)]


def _embed_norm(x2d, g, tm):
    s_len = x2d.shape[0]
    l_dim = s_len + BLK
    n_sub = tm // BLK

    def body(*refs):
        x_refs, g_ref, h_ref, xn_ref = refs[:n_sub], refs[n_sub], refs[n_sub + 1], refs[n_sub + 2]
        i = pl.program_id(0)
        for k in range(n_sub):
            rows = slice(BLK * k, BLK * k + BLK)
            h = x_refs[k][...] * jnp.where(n_sub * i + k >= 1, 1.0, 0.0)
            h_ref[rows, :] = h
            xn_ref[rows, :] = _rms(h, g_ref[...]).astype(BF16)

    return pl.pallas_call(
        body, name="embed_norm", grid=(l_dim // tm,),
        in_specs=_shifted_specs(n_sub, s_len // BLK) + [_full_spec((1, D_MODEL))],
        out_specs=[_row_spec(tm, D_MODEL), _row_spec(tm, D_MODEL)],
        out_shape=[jax.ShapeDtypeStruct((l_dim, D_MODEL), F32),
                   jax.ShapeDtypeStruct((l_dim, D_MODEL), BF16)],
        compiler_params=_params(),
    )(*([x2d] * n_sub), g)


def _embed_meta(meta_pad, g, h0, xn):
    def body(mp_ref, g_ref, h_in, xn_in, h_ref, xn_ref):
        h_ref[...] = mp_ref[...]
        xn_ref[...] = _rms(mp_ref[...], g_ref[...]).astype(BF16)

    any_spec = pl.BlockSpec(memory_space=pl.ANY)
    return pl.pallas_call(
        body, name="embed_meta", grid=(1,),
        in_specs=[_full_spec((BLK, D_MODEL)), _full_spec((1, D_MODEL)), any_spec, any_spec],
        out_specs=[_row_spec(BLK, D_MODEL), _row_spec(BLK, D_MODEL)],
        out_shape=[jax.ShapeDtypeStruct(h0.shape, F32), jax.ShapeDtypeStruct(xn.shape, BF16)],
        input_output_aliases={2: 0, 3: 1},
        compiler_params=_params(),
    )(meta_pad, g, h0, xn)


KVX_W = 2 * N_KV_HEADS * BLK


def _qk_prep(qkv, q_norm_t, k_norm_t, e_mat, tm):
    l_dim = qkv.shape[0]

    def body(x_ref, qg_ref, kg_ref, e_ref, q_ref, kf_ref, vf_ref):
        x = x_ref[...]
        q = _head_rms(x[:, :Q_W], qg_ref[...], e_ref) * (HEAD_DIM ** -0.5)
        q_ref[...] = q.astype(BF16)
        k = _head_rms(x[:, Q_W:Q_W + KV_W], kg_ref[...], e_ref)
        v = x[:, Q_W + KV_W:Q_W + 2 * KV_W]
        half = _lane_half((tm, BLK))
        for src, dst in ((k, kf_ref), (v, vf_ref)):
            for kv in range(N_KV_HEADS):
                blk = src[:, BLK * (kv // 2):BLK * (kv // 2) + BLK]
                swapped = pltpu.roll(blk, HEAD_DIM, axis=1)
                for e in range(2):
                    val = blk if kv % 2 == e else swapped
                    idx = 2 * kv + e
                    dst[:, BLK * idx:BLK * idx + BLK] = jnp.where(half == e, val, 0.0).astype(BF16)

    return pl.pallas_call(
        body, name="qk_prep", grid=(l_dim // tm,),
        in_specs=[_row_spec(tm, Q_W + 2 * KV_W), _full_spec((1, Q_W)), _full_spec((1, KV_W)),
                  _full_spec((256, 256))],
        out_specs=[_row_spec(tm, Q_W), _row_spec(tm, KVX_W), _row_spec(tm, KVX_W)],
        out_shape=[jax.ShapeDtypeStruct((l_dim, Q_W), BF16),
                   jax.ShapeDtypeStruct((l_dim, KVX_W), BF16),
                   jax.ShapeDtypeStruct((l_dim, KVX_W), BF16)],
        compiler_params=_params(),
    )(qkv, q_norm_t, k_norm_t, e_mat)


def _merge_fwd(attn, zab, gates, abn, sbn, tm):
    l_dim = attn.shape[0]

    def body(a_ref, z_ref, g_ref, an_ref, sn_ref, o_ref):
        z = z_ref[...].astype(F32)
        g = g_ref[...].astype(F32)
        ssm = z[:, :D_MODEL] * _sigmoid(z[:, D_MODEL:])
        merged = (_sigmoid(g[:, :D_MODEL]) * _rms(a_ref[...], an_ref[...])
                  + _sigmoid(g[:, D_MODEL:]) * _rms(ssm, sn_ref[...]))
        o_ref[...] = merged.astype(BF16)

    return pl.pallas_call(
        body, name="merge_fwd", grid=(l_dim // tm,),
        in_specs=[_row_spec(tm, D_MODEL), _row_spec(tm, 2 * D_MODEL), _row_spec(tm, 2 * D_MODEL),
                  _full_spec((1, D_MODEL)), _full_spec((1, D_MODEL))],
        out_specs=_row_spec(tm, D_MODEL),
        out_shape=jax.ShapeDtypeStruct((l_dim, D_MODEL), BF16),
        compiler_params=_params(),
    )(attn, zab, gates, abn, sbn)


def _merge_bwd(attn, zab, gates, abn, sbn, dmerged, tm):
    l_dim = attn.shape[0]

    def body(a_ref, z_ref, g_ref, an_ref, sn_ref, dm_ref, da_ref, dz_ref, dg_ref, dan_ref, dsn_ref):
        @pl.when(pl.program_id(0) == 0)
        def _():
            dan_ref[...] = jnp.zeros_like(dan_ref)
            dsn_ref[...] = jnp.zeros_like(dsn_ref)

        z = z_ref[...].astype(F32)
        g = g_ref[...].astype(F32)
        dm = dm_ref[...].astype(F32)
        attn_v = a_ref[...]
        za, zb = z[:, :D_MODEL], z[:, D_MODEL:]
        sb = _sigmoid(zb)
        ssm = za * sb
        s_ga, s_gs = _sigmoid(g[:, :D_MODEL]), _sigmoid(g[:, D_MODEL:])
        a_n = _rms(attn_v, an_ref[...])
        s_n = _rms(ssm, sn_ref[...])
        dg_ref[:, :D_MODEL] = (dm * a_n * s_ga * (1.0 - s_ga)).astype(BF16)
        dg_ref[:, D_MODEL:] = (dm * s_n * s_gs * (1.0 - s_gs)).astype(BF16)
        dattn, dan = _rms_bwd(attn_v, an_ref[...], dm * s_ga)
        dssm, dsn = _rms_bwd(ssm, sn_ref[...], dm * s_gs)
        da_ref[...] = dattn.astype(BF16)
        dz_ref[:, :D_MODEL] = (dssm * sb).astype(BF16)
        dz_ref[:, D_MODEL:] = (dssm * za * sb * (1.0 - sb)).astype(BF16)
        dan_ref[...] += dan
        dsn_ref[...] += dsn

    return pl.pallas_call(
        body, name="merge_bwd", grid=(l_dim // tm,),
        in_specs=[_row_spec(tm, D_MODEL), _row_spec(tm, 2 * D_MODEL), _row_spec(tm, 2 * D_MODEL),
                  _full_spec((1, D_MODEL)), _full_spec((1, D_MODEL)), _row_spec(tm, D_MODEL)],
        out_specs=[_row_spec(tm, D_MODEL), _row_spec(tm, 2 * D_MODEL), _row_spec(tm, 2 * D_MODEL),
                   _full_spec((1, D_MODEL)), _full_spec((1, D_MODEL))],
        out_shape=[jax.ShapeDtypeStruct((l_dim, D_MODEL), BF16),
                   jax.ShapeDtypeStruct((l_dim, 2 * D_MODEL), BF16),
                   jax.ShapeDtypeStruct((l_dim, IN_COLS), BF16),
                   jax.ShapeDtypeStruct((1, D_MODEL), F32), jax.ShapeDtypeStruct((1, D_MODEL), F32)],
        compiler_params=_params(),
    )(attn, zab, gates, abn, sbn, dmerged)


FF_TILE = D_FF // 2


def _ffn_in_swiglu(hn, wt_ffn_in, tm):
    l_dim = hn.shape[0]
    nt = (((1,), (1,)), ((), ()))

    def body(a_ref, w_ref, gu_ref, act_ref):
        r = lax.dot_general(a_ref[...], w_ref[...], nt, preferred_element_type=F32)
        gate, up = r[:, :FF_TILE], r[:, FF_TILE:]
        gu_ref[...] = r.astype(BF16)
        act_ref[...] = (gate * _sigmoid(gate) * up).astype(BF16)

    return pl.pallas_call(
        body, name="ffn_in_swiglu", grid=(l_dim // tm, 2),
        in_specs=[pl.BlockSpec((tm, D_MODEL), lambda i, j: (i, 0)),
                  pl.BlockSpec((2 * FF_TILE, D_MODEL), lambda i, j: (j, 0))],
        out_specs=[pl.BlockSpec((tm, 2 * FF_TILE), lambda i, j: (i, j)),
                   pl.BlockSpec((tm, FF_TILE), lambda i, j: (i, j))],
        out_shape=[jax.ShapeDtypeStruct((l_dim, 2 * D_FF), BF16), jax.ShapeDtypeStruct((l_dim, D_FF), BF16)],
        compiler_params=_params(dimension_semantics=("parallel", "parallel")),
    )(hn, wt_ffn_in)


def _d_act_swiglu(dh2_b, w_ffn_out, gu, tm):
    l_dim = dh2_b.shape[0]
    nt = (((1,), (1,)), ((), ()))

    def body(d_ref, w_ref, gu_ref, o_ref):
        d = lax.dot_general(d_ref[...], w_ref[...], nt, preferred_element_type=F32)
        gate = gu_ref[:, :FF_TILE].astype(F32)
        up = gu_ref[:, FF_TILE:].astype(F32)
        s = _sigmoid(gate)
        o_ref[:, :FF_TILE] = (d * up * (s * (1.0 + gate * (1.0 - s)))).astype(BF16)
        o_ref[:, FF_TILE:] = (d * (gate * s)).astype(BF16)

    return pl.pallas_call(
        body, name="d_act_swiglu", grid=(l_dim // tm, 2),
        in_specs=[pl.BlockSpec((tm, D_MODEL), lambda i, j: (i, 0)),
                  pl.BlockSpec((FF_TILE, D_MODEL), lambda i, j: (j, 0)),
                  pl.BlockSpec((tm, 2 * FF_TILE), lambda i, j: (i, j))],
        out_specs=pl.BlockSpec((tm, 2 * FF_TILE), lambda i, j: (i, j)),
        out_shape=jax.ShapeDtypeStruct((l_dim, 2 * D_FF), BF16),
        compiler_params=_params(dimension_semantics=("parallel", "parallel")),
    )(dh2_b, w_ffn_out, gu)


def _loss_grad(h2, target2d, tm):
    l_dim = h2.shape[0]
    n_sub = tm // BLK

    def body(*refs):
        h_ref, t_refs = refs[0], refs[1:1 + n_sub]
        d_ref, db_ref, loss_ref = refs[1 + n_sub:]
        i = pl.program_id(0)

        @pl.when(i == 0)
        def _():
            loss_ref[...] = jnp.zeros_like(loss_ref)

        for k in range(n_sub):
            rows = slice(BLK * k, BLK * k + BLK)
            real = jnp.where(n_sub * i + k >= 1, 1.0, 0.0)
            err = (h_ref[rows, :] - t_refs[k][...]) * real
            d = err * (1.0 / D_MODEL)
            d_ref[rows, :] = d
            db_ref[rows, :] = d.astype(BF16)
            loss_ref[...] += 0.5 * jnp.sum(jnp.mean(err * err, axis=-1, keepdims=True), axis=0, keepdims=True)

    return pl.pallas_call(
        body, name="loss_grad", grid=(l_dim // tm,),
        in_specs=[_row_spec(tm, D_MODEL)] + _shifted_specs(n_sub, target2d.shape[0] // BLK),
        out_specs=[_row_spec(tm, D_MODEL), _row_spec(tm, D_MODEL), _full_spec((1, 1))],
        out_shape=[jax.ShapeDtypeStruct((l_dim, D_MODEL), F32), jax.ShapeDtypeStruct((l_dim, D_MODEL), BF16),
                   jax.ShapeDtypeStruct((1, 1), F32)],
        compiler_params=_params(),
    )(h2, *([target2d] * n_sub))


def _norm_bwd_res(name, h, g, dy, dres, tm):
    l_dim = h.shape[0]

    def body(h_ref, g_ref, dy_ref, dr_ref, o_ref, ob_ref, dg_ref):
        @pl.when(pl.program_id(0) == 0)
        def _():
            dg_ref[...] = jnp.zeros_like(dg_ref)

        dx, dg = _rms_bwd(h_ref[...], g_ref[...], dy_ref[...].astype(F32))
        out = dr_ref[...] + dx
        o_ref[...] = out
        ob_ref[...] = out.astype(BF16)
        dg_ref[...] += dg

    return pl.pallas_call(
        body, name=name, grid=(l_dim // tm,),
        in_specs=[_row_spec(tm, D_MODEL), _full_spec((1, D_MODEL)), _row_spec(tm, D_MODEL), _row_spec(tm, D_MODEL)],
        out_specs=[_row_spec(tm, D_MODEL), _row_spec(tm, D_MODEL), _full_spec((1, D_MODEL))],
        out_shape=[jax.ShapeDtypeStruct((l_dim, D_MODEL), F32), jax.ShapeDtypeStruct((l_dim, D_MODEL), BF16),
                   jax.ShapeDtypeStruct((1, D_MODEL), F32)],
        compiler_params=_params(),
    )(h, g, dy, dres)


def _final_bwd(h0, g, dxn, dh1, tm):
    l_dim = h0.shape[0]
    n_sub = tm // BLK
    n_tiles = (l_dim - BLK) // tm

    def sub_specs():
        return [_row_spec(BLK, D_MODEL, (lambda j, k=k: jnp.where(j < n_tiles, n_sub * j + 1 + k, 0)))
                for k in range(n_sub)]

    def body(*refs):
        h_refs, g_ref = refs[:n_sub], refs[n_sub]
        dy_refs, dr_refs = refs[n_sub + 1:2 * n_sub + 1], refs[2 * n_sub + 1:3 * n_sub + 1]
        gx_ref, gm_ref, dg_ref = refs[3 * n_sub + 1:]
        j = pl.program_id(0)

        @pl.when(j == 0)
        def _():
            dg_ref[...] = jnp.zeros_like(dg_ref)

        def block(k):
            dx, dg = _rms_bwd(h_refs[k][...], g_ref[...], dy_refs[k][...].astype(F32))
            dg_ref[...] += dg
            return dr_refs[k][...] + dx

        @pl.when(j < n_tiles)
        def _():
            for k in range(n_sub):
                gx_ref[BLK * k:BLK * k + BLK, :] = block(k)

        @pl.when(j == n_tiles)
        def _():
            gm_ref[...] = block(0)

    return pl.pallas_call(
        body, name="final_bwd", grid=(n_tiles + 1,),
        in_specs=sub_specs() + [_full_spec((1, D_MODEL))] + sub_specs() + sub_specs(),
        out_specs=[_row_spec(tm, D_MODEL, lambda j: jnp.minimum(j, n_tiles - 1)), _full_spec((BLK, D_MODEL)),
                   _full_spec((1, D_MODEL))],
        out_shape=[jax.ShapeDtypeStruct((l_dim - BLK, D_MODEL), F32), jax.ShapeDtypeStruct((BLK, D_MODEL), F32),
                   jax.ShapeDtypeStruct((1, D_MODEL), F32)],
        compiler_params=_params(),
    )(*([h0] * n_sub), g, *([dxn] * n_sub), *([dh1] * n_sub))


def _attn_valid(n):
    shape = (2 * BLK, 3 * BLK)
    qi = lax.broadcasted_iota(jnp.int32, shape, 0) & (BLK - 1)
    col = lax.broadcasted_iota(jnp.int32, shape, 1)
    kj = col & (BLK - 1)
    part = col >> 7
    nn = jnp.zeros(shape, jnp.int32) + n
    meta_ok = (part == 0) & (kj >= PAD) & (nn >= 1)
    prev_ok = (part == 1) & (kj > qi) & (nn >= 2)
    cur_ok = (part == 2) & (kj <= qi) & ((nn >= 1) | (kj >= PAD))
    return meta_ok | prev_ok | cur_ok


def _attn_scores(q_ref, kwin, sk_ref, valid, kv, e):
    qs = jnp.concatenate([q_ref[:, BLK * (2 * kv):BLK * (2 * kv) + BLK],
                          q_ref[:, BLK * (2 * kv + 1):BLK * (2 * kv + 1) + BLK]], axis=0)
    s = lax.dot_general(qs, kwin, (((1,), (1,)), ((), ())), preferred_element_type=F32)
    h0 = 4 * kv + e
    row = lax.broadcasted_iota(jnp.int32, (2 * BLK, 1), 0)
    sink = jnp.where(row < BLK, sk_ref[:, h0:h0 + 1], sk_ref[:, h0 + 2:h0 + 3])
    return qs, jnp.where(valid, s, NEG), sink


def _attn_specs(nb):
    prev = lambda i: jnp.maximum(i - 1, 0)
    zero = lambda i: 0
    kv_specs = [_row_spec(BLK, KVX_W, zero), _row_spec(BLK, KVX_W, prev), _row_spec(BLK, KVX_W)]
    return kv_specs


def _attn_fwd(qn, kf, vf, sinks):
    l_dim = qn.shape[0]
    nb = l_dim // BLK

    def body(q_ref, km_ref, kp_ref, kc_ref, vm_ref, vp_ref, vc_ref, sk_ref, o_ref, lse_ref):
        valid = _attn_valid(pl.program_id(0))
        lane = lax.broadcasted_iota(jnp.int32, (BLK, BLK), 1)
        lse_all = jnp.zeros((BLK, BLK), F32)
        for kv in range(N_KV_HEADS):
            outs = []
            for e in range(2):
                sl = slice(BLK * (2 * kv + e), BLK * (2 * kv + e) + BLK)
                kwin = jnp.concatenate([km_ref[:, sl], kp_ref[:, sl], kc_ref[:, sl]], axis=0)
                vwin = jnp.concatenate([vm_ref[:, sl], vp_ref[:, sl], vc_ref[:, sl]], axis=0)
                _, s, sink = _attn_scores(q_ref, kwin, sk_ref, valid, kv, e)
                m = jnp.maximum(jnp.max(s, axis=-1, keepdims=True), sink)
                ex = jnp.exp(s - m)
                den = jnp.sum(ex, axis=-1, keepdims=True) + jnp.exp(sink - m)
                p = ex * (1.0 / den)
                outs.append(jnp.dot(p.astype(BF16), vwin, preferred_element_type=F32))
                lse = m + jnp.log(den)
                lse_all = jnp.where(lane == 4 * kv + e, lse[:BLK], lse_all)
                lse_all = jnp.where(lane == 4 * kv + 2 + e, lse[BLK:], lse_all)
            o = outs[0] + outs[1]
            o_ref[:, BLK * (2 * kv):BLK * (2 * kv) + BLK] = o[:BLK]
            o_ref[:, BLK * (2 * kv + 1):BLK * (2 * kv + 1) + BLK] = o[BLK:]
        lse_ref[...] = lse_all

    kv_specs = _attn_specs(nb)
    return pl.pallas_call(
        body, name="attn_fwd", grid=(nb,),
        in_specs=[_row_spec(BLK, Q_W)] + kv_specs + kv_specs + [_full_spec((1, N_Q_HEADS))],
        out_specs=[_row_spec(BLK, Q_W), _row_spec(BLK, BLK)],
        out_shape=[jax.ShapeDtypeStruct((l_dim, Q_W), F32), jax.ShapeDtypeStruct((l_dim, BLK), F32)],
        compiler_params=_params(),
    )(qn, kf, kf, kf, vf, vf, vf, sinks)


def _attn_bwd(qn, kf, vf, sinks, lse, attn, dattn):
    l_dim = qn.shape[0]
    nb = l_dim // BLK
    wide = KVX_W
    tn = (((0,), (0,)), ((), ()))
    nt = (((1,), (1,)), ((), ()))

    def body(q_ref, km_ref, kp_ref, kc_ref, vm_ref, vp_ref, vc_ref, sk_ref, lse_ref, o_ref, do_ref,
             dq_ref, dkc_ref, dkp_ref, dkm_ref, dvc_ref, dvp_ref, dvm_ref, dsk_ref):
        @pl.when(pl.program_id(0) == 0)
        def _():
            dkm_ref[...] = jnp.zeros_like(dkm_ref)
            dvm_ref[...] = jnp.zeros_like(dvm_ref)
            dsk_ref[...] = jnp.zeros_like(dsk_ref)

        valid = _attn_valid(pl.program_id(0))
        half = _lane_half((BLK, BLK))
        half2 = _lane_half((2 * BLK, BLK))
        lane16 = lax.broadcasted_iota(jnp.int32, (1, N_Q_HEADS), 1)
        dsk = jnp.zeros((1, N_Q_HEADS), F32)
        for kv in range(N_KV_HEADS):
            j0, j1 = 2 * kv, 2 * kv + 1
            do0 = do_ref[:, BLK * j0:BLK * j0 + BLK]
            do1 = do_ref[:, BLK * j1:BLK * j1 + BLK]
            do0f, do1f = do0.astype(F32), do1.astype(F32)
            prod0 = do0f * o_ref[:, BLK * j0:BLK * j0 + BLK]
            prod1 = do1f * o_ref[:, BLK * j1:BLK * j1 + BLK]
            dos = jnp.concatenate([do0, do1], axis=0)
            dqs = []
            for e in range(2):
                sl = slice(BLK * (2 * kv + e), BLK * (2 * kv + e) + BLK)
                kwin = jnp.concatenate([km_ref[:, sl], kp_ref[:, sl], kc_ref[:, sl]], axis=0)
                vwin = jnp.concatenate([vm_ref[:, sl], vp_ref[:, sl], vc_ref[:, sl]], axis=0)
                qs, s, sink = _attn_scores(q_ref, kwin, sk_ref, valid, kv, e)
                h0 = 4 * kv + e
                lse_rows = jnp.concatenate([lse_ref[:, h0:h0 + 1], lse_ref[:, h0 + 2:h0 + 3]], axis=0)
                p = jnp.exp(s - lse_rows)
                p_sink = jnp.exp(sink - lse_rows)
                delta = jnp.concatenate(
                    [jnp.sum(jnp.where(half == e, prod0, 0.0), axis=-1, keepdims=True),
                     jnp.sum(jnp.where(half == e, prod1, 0.0), axis=-1, keepdims=True)], axis=0)
                dp = lax.dot_general(dos, vwin, nt, preferred_element_type=F32)
                ds = (p * (dp - delta)).astype(BF16)
                pb = p.astype(BF16)
                dqs.append(jnp.dot(ds, kwin, preferred_element_type=F32))
                dk = lax.dot_general(ds, qs, tn, preferred_element_type=F32)
                dv = lax.dot_general(pb, dos, tn, preferred_element_type=F32)
                dkm_ref[:, sl] += dk[:BLK]
                dkp_ref[:, sl] = dk[BLK:2 * BLK].astype(BF16)
                dkc_ref[:, sl] = dk[2 * BLK:].astype(BF16)
                dvm_ref[:, sl] += dv[:BLK]
                dvp_ref[:, sl] = dv[BLK:2 * BLK].astype(BF16)
                dvc_ref[:, sl] = dv[2 * BLK:].astype(BF16)
                sink_g = -(p_sink * delta)
                g_lo = jnp.sum(sink_g[:BLK], axis=0, keepdims=True)
                g_hi = jnp.sum(sink_g[BLK:], axis=0, keepdims=True)
                dsk = dsk + jnp.where(lane16 == h0, g_lo, 0.0) + jnp.where(lane16 == h0 + 2, g_hi, 0.0)
            dq = jnp.where(half2 == 0, dqs[0], dqs[1])
            dq_ref[:, BLK * j0:BLK * j0 + BLK] = dq[:BLK].astype(BF16)
            dq_ref[:, BLK * j1:BLK * j1 + BLK] = dq[BLK:].astype(BF16)
        dsk_ref[...] += dsk

    kv_specs = _attn_specs(nb)
    row_wide = _row_spec(BLK, wide)
    acc_wide = _full_spec((BLK, wide))
    big = jax.ShapeDtypeStruct((l_dim, wide), BF16)
    return pl.pallas_call(
        body, name="attn_bwd", grid=(nb,),
        in_specs=[_row_spec(BLK, Q_W)] + kv_specs + kv_specs
        + [_full_spec((1, N_Q_HEADS)), _row_spec(BLK, BLK), _row_spec(BLK, Q_W), _row_spec(BLK, Q_W)],
        out_specs=[_row_spec(BLK, Q_W), row_wide, row_wide, acc_wide, row_wide, row_wide, acc_wide,
                   _full_spec((1, N_Q_HEADS))],
        out_shape=[jax.ShapeDtypeStruct((l_dim, Q_W), BF16), big, big, jax.ShapeDtypeStruct((BLK, wide), F32),
                   big, big, jax.ShapeDtypeStruct((BLK, wide), F32), jax.ShapeDtypeStruct((1, N_Q_HEADS), F32)],
        compiler_params=_params(),
    )(qn, kf, kf, kf, vf, vf, vf, sinks, lse, attn, dattn)


def _qk_bwd(qkv, q_norm_t, k_norm_t, e_mat, dq, dkc, dkp, dkm, dvc, dvp, dvm, dproj):
    l_dim = qkv.shape[0]
    nb = l_dim // BLK
    wide = KVX_W

    def fold(x):
        half = _lane_half((BLK, BLK))
        blocks = []
        for kb in range(2):
            t = []
            for kv in (2 * kb, 2 * kb + 1):
                own = kv % 2
                a = x[:, BLK * (2 * kv + own):BLK * (2 * kv + own) + BLK]
                b = pltpu.roll(x[:, BLK * (2 * kv + 1 - own):BLK * (2 * kv + 1 - own) + BLK], HEAD_DIM, axis=1)
                t.append(a + b)
            blocks.append(jnp.where(half == 0, t[0], t[1]))
        return jnp.concatenate(blocks, axis=1)

    def body(x_ref, qg_ref, kg_ref, e_ref, dq_ref, dkc_ref, dkp_ref, dkm_ref, dvc_ref, dvp_ref, dvm_ref,
             dproj_ref, o_ref, dqg_ref, dkg_ref):
        i = pl.program_id(0)

        @pl.when(i == 0)
        def _():
            dqg_ref[...] = jnp.zeros_like(dqg_ref)
            dkg_ref[...] = jnp.zeros_like(dkg_ref)

        first = jnp.where(i == 0, 1.0, 0.0)
        not_last = jnp.where(i < nb - 1, 1.0, 0.0)
        dk_x = dkc_ref[...].astype(F32) + not_last * dkp_ref[...].astype(F32) + first * dkm_ref[...]
        dv_x = dvc_ref[...].astype(F32) + not_last * dvp_ref[...].astype(F32) + first * dvm_ref[...]
        x = x_ref[...]
        dqx, dqg = _head_rms_bwd(x[:, :Q_W], qg_ref[...], dq_ref[...].astype(F32) * (HEAD_DIM ** -0.5), e_ref)
        dkx, dkg = _head_rms_bwd(x[:, Q_W:Q_W + KV_W], kg_ref[...], fold(dk_x), e_ref)
        o_ref[:, :Q_W] = dqx.astype(BF16)
        o_ref[:, Q_W:Q_W + KV_W] = dkx.astype(BF16)
        o_ref[:, Q_W + KV_W:] = fold(dv_x).astype(BF16)
        dqg_ref[...] += dqg
        dkg_ref[...] += dkg

    nxt = lambda i: jnp.minimum(i + 1, nb - 1)
    row_wide = _row_spec(BLK, wide)
    nxt_wide = _row_spec(BLK, wide, nxt)
    acc_wide = _full_spec((BLK, wide))
    qkv_w = Q_W + 2 * KV_W
    in_specs = [_row_spec(BLK, qkv_w), _full_spec((1, Q_W)), _full_spec((1, KV_W)),
                _full_spec((256, 256)), _row_spec(BLK, Q_W),
                row_wide, nxt_wide, acc_wide, row_wide, nxt_wide, acc_wide, pl.BlockSpec(memory_space=pl.ANY)]
    return pl.pallas_call(
        body, name="qk_bwd", grid=(nb,),
        in_specs=in_specs,
        out_specs=[pl.BlockSpec((BLK, qkv_w), lambda i: (i, 3 * D_MODEL // qkv_w)),
                   _full_spec((1, Q_W)), _full_spec((1, KV_W))],
        input_output_aliases={len(in_specs) - 1: 0},
        out_shape=[jax.ShapeDtypeStruct(dproj.shape, BF16),
                   jax.ShapeDtypeStruct((1, Q_W), F32), jax.ShapeDtypeStruct((1, KV_W), F32)],
        compiler_params=_params(),
    )(qkv, q_norm_t, k_norm_t, e_mat, dq, dkc, dkp, dkm, dvc, dvp, dvm, dproj)


GRP = 8


def _strided(r, g):
    return pl.ds(r, g, stride=GRP)


def _lane_slab(ref, i, r, g):
    return ref[i, _strided(r, g), :]


def _chunk_carries(xr_ref, xi_ref, i, ar, ai, sqr_ref, sqi_ref, seq_r, seq_i, cin_r, cin_i, sign, reverse):
    g = xr_ref.shape[1] // GRP
    sr = si = None
    for r in (range(GRP - 1, -1, -1) if reverse else range(GRP)):
        xr, xi = _lane_slab(xr_ref, i, r, g), _lane_slab(xi_ref, i, r, g)
        if sr is not None:
            xr, xi = xr + ar * sr - ai * si, xi + ar * si + ai * sr
        sr, si = xr, xi
    row = lax.broadcasted_iota(jnp.int32, sr.shape, 0)
    idx, s = 0, 1
    while s < g:
        br = sqr_ref[idx:idx + 1, BLK * i:BLK * i + BLK]
        bi = sign * sqi_ref[idx:idx + 1, BLK * i:BLK * i + BLK]
        shift, keep = (g - s, row < g - s) if reverse else (s, row >= s)
        pr = jnp.where(keep, pltpu.roll(sr, shift, axis=0), 0.0)
        pi = jnp.where(keep, pltpu.roll(si, shift, axis=0), 0.0)
        sr, si = sr + br * pr - bi * pi, si + br * pi + bi * pr
        idx, s = idx + 1, 2 * s
    return sr + seq_r * cin_r - seq_i * cin_i, si + seq_r * cin_i + seq_i * cin_r


def _ssm_fwd_kb(u, wb_re, wb_im, wc_re, wc_im, d_skip, tabs, q):
    l_dim = u.shape[0]
    nc = l_dim // q
    g = q // GRP

    def body(u_ref, wbr_ref, wbi_ref, wcr_ref, wci_ref, d_ref, a1r_ref, a1i_ref, sqr_ref, sqi_ref,
             seqr_ref, seqi_ref, y_ref, z_ref, sr_ref, si_ref, cr_ref, ci_ref, xr_ref, xi_ref):
        @pl.when(pl.program_id(1) == 0)
        def _():
            cr_ref[...] = jnp.zeros_like(cr_ref)
            ci_ref[...] = jnp.zeros_like(ci_ref)

        u_kb = u_ref[...]
        ub = u_kb.astype(BF16)
        xr = jnp.dot(ub, wbr_ref[0], preferred_element_type=F32)
        xi = jnp.dot(ub, wbi_ref[0], preferred_element_type=F32)
        for i in range(LB_KB):
            xr_ref[i] = xr[:, BLK * i:BLK * i + BLK]
            xi_ref[i] = xi[:, BLK * i:BLK * i + BLK]
        row = lax.broadcasted_iota(jnp.int32, (g, BLK), 0)
        for i in range(LB_KB):
            lanes = slice(BLK * i, BLK * i + BLK)
            ar, ai = a1r_ref[0:1, lanes], a1i_ref[0:1, lanes]
            cin_r, cin_i = cr_ref[0:1, lanes], ci_ref[0:1, lanes]
            tr, ti = _chunk_carries(xr_ref, xi_ref, i, ar, ai, sqr_ref, sqi_ref, seqr_ref[:, lanes],
                                    seqi_ref[:, lanes], cin_r, cin_i, 1.0, reverse=False)
            cr_ref[0:1, lanes] = jnp.sum(jnp.where(row == g - 1, tr, 0.0), axis=0, keepdims=True)
            ci_ref[0:1, lanes] = jnp.sum(jnp.where(row == g - 1, ti, 0.0), axis=0, keepdims=True)
            pr = jnp.where(row == 0, cin_r, pltpu.roll(tr, 1, axis=0))
            pi = jnp.where(row == 0, cin_i, pltpu.roll(ti, 1, axis=0))
            for r in range(GRP):
                pr, pi = (_lane_slab(xr_ref, i, r, g) + ar * pr - ai * pi,
                          _lane_slab(xi_ref, i, r, g) + ar * pi + ai * pr)
                sr_ref[i, _strided(r, g), :] = pr
                si_ref[i, _strided(r, g), :] = pi
        s_r = jnp.concatenate([sr_ref[i] for i in range(LB_KB)], axis=1)
        s_i = jnp.concatenate([si_ref[i] for i in range(LB_KB)], axis=1)
        y = (jnp.dot(s_r.astype(BF16), wcr_ref[0], preferred_element_type=F32)
             - jnp.dot(s_i.astype(BF16), wci_ref[0], preferred_element_type=F32)
             + d_ref[...] * u_kb)
        y_ref[...] = y.astype(BF16)
        z_ref[...] = _gelu(y).astype(BF16)

    chan = pl.BlockSpec((q, BLK), lambda k, c: (c, k))
    wb_spec = pl.BlockSpec((1, BLK, ST_KB), lambda k, c: (k, 0, 0))
    wc_spec = pl.BlockSpec((1, ST_KB, BLK), lambda k, c: (k, 0, 0))
    tab_specs = [pl.BlockSpec((t.shape[0], ST_KB), lambda k, c: (0, k)) for t in tabs[:6]]
    state_spec = pl.BlockSpec((LB_KB, q, BLK), lambda k, c: (k, c, 0))
    state_shape = jax.ShapeDtypeStruct((N_LB, l_dim, BLK), F32)
    return pl.pallas_call(
        body, name="ssm_fwd", grid=(SSM_KB, nc),
        in_specs=[chan, wb_spec, wb_spec, wc_spec, wc_spec, pl.BlockSpec((1, BLK), lambda k, c: (0, k))] + tab_specs,
        out_specs=[chan, chan, state_spec, state_spec],
        out_shape=[jax.ShapeDtypeStruct((l_dim, D_MODEL), BF16), jax.ShapeDtypeStruct((l_dim, D_MODEL), BF16),
                   state_shape, state_shape],
        scratch_shapes=[pltpu.VMEM((8, ST_KB), F32), pltpu.VMEM((8, ST_KB), F32),
                        pltpu.VMEM((LB_KB, q, BLK), F32), pltpu.VMEM((LB_KB, q, BLK), F32)],
        compiler_params=_params(dimension_semantics=("parallel", "arbitrary")),
    )(u, wb_re, wb_im, wc_re, wc_im, d_skip, *tabs[:6])


def _ssm_bwd_kb(dz, y, u, s_re, s_im, wb_re, wb_im, wc_re, wc_im, d_skip, tabs, q, dproj):
    l_dim = u.shape[0]
    nc = l_dim // q
    g = q // GRP

    def body(dz_ref, y_ref, u_ref, sr_ref, si_ref, wbr_ref, wbi_ref, wcr_ref, wci_ref, d_ref,
             a1r_ref, a1i_ref, sqr_ref, sqi_ref, revr_ref, revi_ref, dproj_ref,
             du_ref, dd_ref, dar_ref, dai_ref, dwbr_ref, dwbi_ref, dwcr_ref, dwci_ref,
             cr_ref, ci_ref, gr_ref, gi_ref):
        @pl.when(pl.program_id(1) == 0)
        def _():
            for ref in (cr_ref, ci_ref, dd_ref, dar_ref, dai_ref, dwbr_ref, dwbi_ref, dwcr_ref, dwci_ref):
                ref[...] = jnp.zeros_like(ref)

        tn = (((0,), (0,)), ((), ()))
        nt = (((1,), (1,)), ((), ()))
        u_kb = u_ref[...]
        dy = dz_ref[...].astype(F32) * _gelu_grad(y_ref[...].astype(F32))
        dyb = dy.astype(BF16)
        ub = u_kb.astype(BF16)
        dd_ref[...] += jnp.sum(dy * u_kb, axis=0, keepdims=True)
        ds_r = lax.dot_general(dyb, wcr_ref[0], nt, preferred_element_type=F32)
        ds_i = -lax.dot_general(dyb, wci_ref[0], nt, preferred_element_type=F32)
        for i in range(LB_KB):
            gr_ref[i] = ds_r[:, BLK * i:BLK * i + BLK]
            gi_ref[i] = ds_i[:, BLK * i:BLK * i + BLK]
        row = lax.broadcasted_iota(jnp.int32, (g, BLK), 0)
        for i in range(LB_KB):
            lanes = slice(BLK * i, BLK * i + BLK)
            ar, ai = a1r_ref[0:1, lanes], -a1i_ref[0:1, lanes]
            cin_r, cin_i = cr_ref[0:1, lanes], ci_ref[0:1, lanes]
            tr, ti = _chunk_carries(gr_ref, gi_ref, i, ar, ai, sqr_ref, sqi_ref, revr_ref[:, lanes],
                                    -revi_ref[:, lanes], cin_r, cin_i, -1.0, reverse=True)
            cr_ref[0:1, lanes] = jnp.sum(jnp.where(row == 0, tr, 0.0), axis=0, keepdims=True)
            ci_ref[0:1, lanes] = jnp.sum(jnp.where(row == 0, ti, 0.0), axis=0, keepdims=True)
            nr = jnp.where(row == g - 1, cin_r, pltpu.roll(tr, g - 1, axis=0))
            ni = jnp.where(row == g - 1, cin_i, pltpu.roll(ti, g - 1, axis=0))
            acc_r = jnp.zeros((g, BLK), F32)
            acc_i = jnp.zeros((g, BLK), F32)
            for r in range(GRP - 1, -1, -1):
                s_r, s_i = _lane_slab(sr_ref, i, r, g), _lane_slab(si_ref, i, r, g)
                acc_r = acc_r + (nr * s_r + ni * s_i)
                acc_i = acc_i + (ni * s_r - nr * s_i)
                nr, ni = (_lane_slab(gr_ref, i, r, g) + ar * nr - ai * ni,
                          _lane_slab(gi_ref, i, r, g) + ar * ni + ai * nr)
                gr_ref[i, _strided(r, g), :] = nr
                gi_ref[i, _strided(r, g), :] = ni
            dar_ref[:, lanes] += jnp.sum(acc_r, axis=0, keepdims=True)
            dai_ref[:, lanes] += jnp.sum(acc_i, axis=0, keepdims=True)
        grb = jnp.concatenate([gr_ref[i] for i in range(LB_KB)], axis=1).astype(BF16)
        gib = jnp.concatenate([gi_ref[i] for i in range(LB_KB)], axis=1).astype(BF16)
        srb = jnp.concatenate([sr_ref[i] for i in range(LB_KB)], axis=1).astype(BF16)
        sib = jnp.concatenate([si_ref[i] for i in range(LB_KB)], axis=1).astype(BF16)
        du = (lax.dot_general(grb, wbr_ref[0], nt, preferred_element_type=F32)
              + lax.dot_general(gib, wbi_ref[0], nt, preferred_element_type=F32)
              + d_ref[...] * dy)
        du_ref[...] = du.astype(BF16)
        dwbr_ref[0] += lax.dot_general(ub, grb, tn, preferred_element_type=F32)
        dwbi_ref[0] += lax.dot_general(ub, gib, tn, preferred_element_type=F32)
        dwcr_ref[0] += lax.dot_general(srb, dyb, tn, preferred_element_type=F32)
        dwci_ref[0] -= lax.dot_general(sib, dyb, tn, preferred_element_type=F32)

    chan = pl.BlockSpec((q, BLK), lambda k, c: (nc - 1 - c, k))
    wb_spec = pl.BlockSpec((1, BLK, ST_KB), lambda k, c: (k, 0, 0))
    wc_spec = pl.BlockSpec((1, ST_KB, BLK), lambda k, c: (k, 0, 0))
    tab_in = [tabs[0], tabs[1], tabs[2], tabs[3], tabs[6], tabs[7]]
    tab_specs = [pl.BlockSpec((t.shape[0], ST_KB), lambda k, c: (0, k)) for t in tab_in]
    vec = pl.BlockSpec((1, BLK), lambda k, c: (0, k))
    svec = pl.BlockSpec((1, ST_KB), lambda k, c: (0, k))
    state_spec = pl.BlockSpec((LB_KB, q, BLK), lambda k, c: (k, nc - 1 - c, 0))
    du_spec = pl.BlockSpec((q, BLK), lambda k, c: (nc - 1 - c, 2 * D_MODEL // BLK + k))
    in_specs = ([chan, chan, chan, state_spec, state_spec, wb_spec, wb_spec, wc_spec, wc_spec, vec] + tab_specs
                + [pl.BlockSpec(memory_space=pl.ANY)])
    return pl.pallas_call(
        body, name="ssm_bwd", grid=(SSM_KB, nc),
        in_specs=in_specs,
        out_specs=[du_spec, vec, svec, svec, wb_spec, wb_spec, wc_spec, wc_spec],
        input_output_aliases={len(in_specs) - 1: 0},
        out_shape=[jax.ShapeDtypeStruct(dproj.shape, BF16), jax.ShapeDtypeStruct((1, D_MODEL), F32),
                   jax.ShapeDtypeStruct((1, N_STATE), F32), jax.ShapeDtypeStruct((1, N_STATE), F32),
                   jax.ShapeDtypeStruct((SSM_KB, BLK, ST_KB), F32), jax.ShapeDtypeStruct((SSM_KB, BLK, ST_KB), F32),
                   jax.ShapeDtypeStruct((SSM_KB, ST_KB, BLK), F32), jax.ShapeDtypeStruct((SSM_KB, ST_KB, BLK), F32)],
        scratch_shapes=[pltpu.VMEM((8, ST_KB), F32), pltpu.VMEM((8, ST_KB), F32),
                        pltpu.VMEM((LB_KB, q, BLK), F32), pltpu.VMEM((LB_KB, q, BLK), F32)],
        compiler_params=_params(dimension_semantics=("parallel", "arbitrary")),
    )(dz, y, u, s_re, s_im, wb_re, wb_im, wc_re, wc_im, d_skip, *tab_in, dproj)


def _discretize(lam_re, lam_im, log_dt, b_re, b_im):
    dt = jnp.exp(log_dt)[:, None]
    mag = jnp.exp(lam_re * dt)
    ar, ai = mag * jnp.cos(lam_im * dt), mag * jnp.sin(lam_im * dt)
    den = lam_re * lam_re + lam_im * lam_im
    nr, ni = ar - 1.0, ai
    fr, fi = (nr * lam_re + ni * lam_im) / den, (ni * lam_re - nr * lam_im) / den
    bbar_re = fr[..., None] * b_re - fi[..., None] * b_im
    bbar_im = fr[..., None] * b_im + fi[..., None] * b_re
    return ar, ai, bbar_re, bbar_im


def _block_diag_b(bbar):
    eye = jnp.eye(8, dtype=bbar.dtype)
    return jnp.einsum("kgpc,gh->kgchp", bbar.reshape(8, 8, SSM_STATE, SSM_GROUP_CH), eye).reshape(8, BLK, ST_KB)


def _block_diag_b_t(dwb):
    eye = jnp.eye(8, dtype=dwb.dtype)
    return jnp.einsum("kgchp,gh->kgpc", dwb.reshape(8, 8, SSM_GROUP_CH, 8, SSM_STATE), eye).reshape(
        SSM_GROUPS, SSM_STATE, SSM_GROUP_CH)


def _block_diag_c(c):
    eye = jnp.eye(8, dtype=c.dtype)
    return jnp.einsum("kgcp,gh->kgphc", c.reshape(8, 8, SSM_GROUP_CH, SSM_STATE), eye).reshape(8, ST_KB, BLK)


def _block_diag_c_t(dwc):
    eye = jnp.eye(8, dtype=dwc.dtype)
    return jnp.einsum("kgphc,gh->kgcp", dwc.reshape(8, 8, SSM_STATE, 8, SSM_GROUP_CH), eye).reshape(
        SSM_GROUPS, SSM_GROUP_CH, SSM_STATE)


def _powers(br, bi, n):
    pr, pi = br, bi
    cr, ci = br, bi
    while pr.shape[0] < n:
        pr, pi = (jnp.concatenate([pr, pr * cr - pi * ci], axis=0),
                  jnp.concatenate([pi, pr * ci + pi * cr], axis=0))
        cr, ci = cr * cr - ci * ci, 2.0 * cr * ci
    return pr[:n], pi[:n]


def _powers_desc(br, bi, n):
    pr, pi = br, bi
    cr, ci = br, bi
    while pr.shape[0] < n:
        pr, pi = (jnp.concatenate([pr * cr - pi * ci, pr], axis=0),
                  jnp.concatenate([pr * ci + pi * cr, pi], axis=0))
        cr, ci = cr * cr - ci * ci, 2.0 * cr * ci
    return pr, pi


def _power_tables(ar, ai, g):
    a1r, a1i = _powers(ar, ai, GRP)
    seqr, seqi = _powers(a1r[GRP - 1:], a1i[GRP - 1:], g)
    g2 = 1 << (g - 1).bit_length()
    revr, revi = _powers_desc(a1r[GRP - 1:], a1i[GRP - 1:], g2)
    revr, revi = revr[g2 - g:], revi[g2 - g:]
    sq_r, sq_i = [seqr[0:1]], [seqi[0:1]]
    while len(sq_r) < 8:
        r, i = sq_r[-1], sq_i[-1]
        sq_r.append(r * r - i * i)
        sq_i.append(2.0 * r * i)
    sqr, sqi = jnp.concatenate(sq_r, axis=0), jnp.concatenate(sq_i, axis=0)
    return a1r, a1i, sqr, sqi, seqr, seqi, revr, revi


HBM_SPEC = pl.BlockSpec(memory_space=pltpu.HBM)
SEM_SPEC = pl.BlockSpec(memory_space=pltpu.SEMAPHORE)
DATAFLOW = pltpu.SideEffectType.DATAFLOW_SIDE_EFFECTING


def _plain_rows(p, m):
    return p * m


def _ffn_in_rows(p, m):
    return ((p & 3) >> 1) * (4 * m) + (p >> 2) * (2 * m) + (p & 1) * m


def _peer_copies(src_refs, land_refs, send_sems, recv_sems, chunked, row_fns):
    x, y, c = lax.axis_index("x"), lax.axis_index("y"), lax.axis_index("c")
    me = 4 * x + 2 * y + c
    copies = []
    for a, (src, land) in enumerate(zip(src_refs, land_refs)):
        m = land.shape[0] // N_DEV
        for k in range(N_DEV - 1):
            rel = k + 1
            bx, by, bc = (rel >> 2) & 1, (rel >> 1) & 1, rel & 1
            peer = (x + bx - 2 * x * bx, y + by - 2 * y * by, c + bc - 2 * c * bc)
            p_idx = 4 * peer[0] + 2 * peer[1] + peer[2]
            copies.append(pltpu.make_async_remote_copy(
                src_ref=src.at[pl.ds(row_fns[a](p_idx, m), m), :] if chunked else src,
                dst_ref=land.at[pl.ds(me * m if chunked else row_fns[a](me, m), m), :],
                send_sem=send_sems[a * (N_DEV - 1) + k], recv_sem=recv_sems[a * (N_DEV - 1) + k],
                device_id=peer, device_id_type=MESH))
    return copies


def _send_start(name, srcs, lands, chunked, row_fns=None):
    n = len(srcs)
    ns = n * (N_DEV - 1)
    row_fns = row_fns or [_plain_rows] * n

    def body(*refs):
        src_refs, land_refs = refs[:n], refs[n:2 * n]
        send_sems, recv_sems = refs[2 * n:2 * n + ns], refs[2 * n + ns:2 * n + 2 * ns]
        token = refs[-1]
        for cp in _peer_copies(src_refs, land_refs, send_sems, recv_sems, chunked, row_fns):
            cp.start()
        token[...] = jnp.zeros_like(token)

    ins = [pltpu.with_memory_space_constraint(a, pltpu.HBM) for a in list(srcs) + list(lands)]
    out = pl.pallas_call(
        body, name=name,
        in_specs=[HBM_SPEC] * (2 * n),
        out_specs=[SEM_SPEC] * (2 * ns) + [HBM_SPEC] * (2 * n) + [pl.BlockSpec(memory_space=pltpu.VMEM)],
        out_shape=[pltpu.SemaphoreType.DMA(())] * (2 * ns)
        + [pltpu.HBM(a.shape, a.dtype) for a in list(srcs) + list(lands)]
        + [jax.ShapeDtypeStruct((8, BLK), F32)],
        input_output_aliases={i: i + 2 * ns for i in range(2 * n)},
        compiler_params=pltpu.CompilerParams(has_side_effects=DATAFLOW),
    )(*ins)
    return out[:ns], out[ns:2 * ns], out[2 * ns:2 * ns + n], out[2 * ns + n:2 * ns + 2 * n], out[-1]


def _send_wait(name, send_sems, recv_sems, srcs, lands, after, chunked, row_fns=None):
    n = len(srcs)
    ns = n * (N_DEV - 1)
    row_fns = row_fns or [_plain_rows] * n

    def body(*refs):
        src_refs, land_refs = refs[:n], refs[n:2 * n]
        s_sems, r_sems = refs[2 * n:2 * n + ns], refs[2 * n + ns:2 * n + 2 * ns]
        copies = _peer_copies(src_refs, land_refs, s_sems, r_sems, chunked, row_fns)
        for cp in copies:
            cp.wait_send()
        for cp in copies:
            cp.wait_recv()

    out = pl.pallas_call(
        body, name=name,
        in_specs=[HBM_SPEC] * (2 * n) + [SEM_SPEC] * (2 * ns) + [pl.BlockSpec(memory_space=pl.ANY)],
        out_specs=[HBM_SPEC] * (2 * n),
        out_shape=[pltpu.HBM(a.shape, a.dtype) for a in list(srcs) + list(lands)],
        input_output_aliases={i: i for i in range(2 * n)},
        compiler_params=pltpu.CompilerParams(has_side_effects=DATAFLOW),
    )(*srcs, *lands, *send_sems, *recv_sems, after)
    return out[n:]


def _sum_slots(name, recv, own):
    m, ncol = own.shape
    tr = m // 2 if (m // 2) % 16 == 0 else m
    g = m // tr

    def body(*refs):
        slots, own_ref, o_ref = refs[:N_DEV], refs[N_DEV], refs[N_DEV + 1]
        me = _my_index()
        tot = None
        for s in range(N_DEV):
            v = jnp.where(me == s, own_ref[...], slots[s][...].astype(F32))
            tot = v if tot is None else tot + v
        o_ref[...] = tot

    def slot_spec(s):
        return pl.BlockSpec((tr, ncol), lambda i: (s * g + i, 0))

    return pl.pallas_call(
        body, name=name, grid=(g,),
        in_specs=[slot_spec(s) for s in range(N_DEV)] + [pl.BlockSpec((tr, ncol), lambda i: (i, 0))],
        out_specs=pl.BlockSpec((tr, ncol), lambda i: (i, 0)),
        out_shape=jax.ShapeDtypeStruct((m, ncol), F32),
        compiler_params=_params(),
    )(*([recv] * N_DEV), own)


def _sum_gathered(name, gathered, rows):
    tr = _pick(rows, 512, 8)
    g = rows // tr

    def body(*refs):
        o_ref = refs[N_DEV]
        tot = refs[0][...]
        for s in range(1, N_DEV):
            tot = tot + refs[s][...]
        o_ref[...] = tot

    return pl.pallas_call(
        body, name=name, grid=(g,),
        in_specs=[pl.BlockSpec((tr, BLK), (lambda i, s=s: (s * g + i, 0))) for s in range(N_DEV)],
        out_specs=pl.BlockSpec((tr, BLK), lambda i: (i, 0)),
        out_shape=jax.ShapeDtypeStruct((rows, BLK), F32),
        compiler_params=_params(),
    )(*([gathered] * N_DEV))


def _adamw(name, w, g, m, v):
    r, c = w.shape
    tr = _pick(r, 256, 8) if r % 8 == 0 else r
    c1 = 1.0 - ADAM_B1 ** ADAM_STEP
    c2 = 1.0 - ADAM_B2 ** ADAM_STEP

    def body(w_ref, g_ref, m_ref, v_ref, d_ref, nm_ref, nv_ref):
        gv = g_ref[...]
        nm = ADAM_B1 * m_ref[...] + (1.0 - ADAM_B1) * gv
        nv = ADAM_B2 * v_ref[...] + (1.0 - ADAM_B2) * (gv * gv)
        m_hat = nm / c1
        v_hat = nv / c2
        d_ref[...] = -ADAM_LR * (m_hat / (jnp.sqrt(v_hat) + ADAM_EPS) + ADAM_WD * w_ref[...])
        nm_ref[...] = nm
        nv_ref[...] = nv

    spec = pl.BlockSpec((tr, c), lambda i: (i, 0))
    shape = jax.ShapeDtypeStruct((r, c), F32)
    return pl.pallas_call(
        body, name=name, grid=(r // tr,),
        in_specs=[spec] * 4, out_specs=[spec] * 3, out_shape=[shape] * 3,
        compiler_params=_params(),
    )(w, g, m, v)


def _adamw_many(name, ws, gs, ms, vs):
    n = len(ws)
    c1 = 1.0 - ADAM_B1 ** ADAM_STEP
    c2 = 1.0 - ADAM_B2 ** ADAM_STEP

    def body(*refs):
        for a in range(n):
            w_ref, g_ref, m_ref, v_ref = refs[a], refs[n + a], refs[2 * n + a], refs[3 * n + a]
            d_ref, nm_ref, nv_ref = refs[4 * n + a], refs[5 * n + a], refs[6 * n + a]
            gv = g_ref[...]
            nm = ADAM_B1 * m_ref[...] + (1.0 - ADAM_B1) * gv
            nv = ADAM_B2 * v_ref[...] + (1.0 - ADAM_B2) * (gv * gv)
            d_ref[...] = -ADAM_LR * ((nm / c1) / (jnp.sqrt(nv / c2) + ADAM_EPS) + ADAM_WD * w_ref[...])
            nm_ref[...] = nm
            nv_ref[...] = nv

    vmem = pl.BlockSpec(memory_space=pltpu.VMEM)
    shapes = [jax.ShapeDtypeStruct(w.shape, F32) for w in ws]
    out = pl.pallas_call(
        body, name=name,
        in_specs=[vmem] * (4 * n), out_specs=[vmem] * (3 * n), out_shape=shapes * 3,
        compiler_params=_params(),
    )(*ws, *gs, *ms, *vs)
    return out[:n], out[n:2 * n], out[2 * n:]


PACK_ROWS = 128


def _pack(parts):
    flat = []
    for p in parts:
        v = p.reshape(-1)
        flat.append(jnp.pad(v, (0, (-v.shape[0]) % BLK)))
    v = jnp.concatenate(flat)
    v = jnp.pad(v, (0, (-v.shape[0]) % (PACK_ROWS * BLK)))
    return v.reshape(-1, BLK)


def _unpack(buf, shapes):
    flat = buf.reshape(-1)
    out, off = [], 0
    for shp in shapes:
        size = math.prod(shp)
        out.append(flat[off:off + size].reshape(shp))
        off += size + (-size) % BLK
    return out


def kernel(x, meta_tokens, norm_mix, w_in, q_norm, k_norm, attn_sinks, lam_re, lam_im, log_dt, ssm_b_re, ssm_b_im, ssm_c_re, ssm_c_im, ssm_d, w_glu, attn_branch_norm, ssm_branch_norm, w_out, norm_ffn, w_ffn_in, w_ffn_out, loss_target, m_meta_tokens, m_norm_mix, m_w_in, m_q_norm, m_k_norm, m_attn_sinks, m_lam_re, m_lam_im, m_log_dt, m_ssm_b_re, m_ssm_b_im, m_ssm_c_re, m_ssm_c_im, m_ssm_d, m_w_glu, m_attn_branch_norm, m_ssm_branch_norm, m_w_out, m_norm_ffn, m_w_ffn_in, m_w_ffn_out, v_meta_tokens, v_norm_mix, v_w_in, v_q_norm, v_k_norm, v_attn_sinks, v_lam_re, v_lam_im, v_log_dt, v_ssm_b_re, v_ssm_b_im, v_ssm_c_re, v_ssm_c_im, v_ssm_d, v_w_glu, v_attn_branch_norm, v_ssm_branch_norm, v_w_out, v_norm_ffn, v_w_ffn_in, v_w_ffn_out):
    args = dict(locals())
    weights = {n: args[n] for n in WEIGHTS}
    mom_m = {n: args["m_" + n] for n in WEIGHTS}
    mom_v = {n: args["v_" + n] for n in WEIGHTS}

    x2d = x[0]
    target2d = loss_target[0]
    s_len = x2d.shape[0]
    l_dim = s_len + BLK
    tm_row = _pick(l_dim, 320)
    tm_mm = _pick(l_dim, 1040)
    tl_tn = _pick(l_dim, 1664)
    tm_ffn = _pick(l_dim, 640)
    tm_big = _pick(l_dim, 2080)
    tm_shift = _pick(l_dim, 640, BLK)

    shard_in = w_in[0].T.astype(BF16)
    shard_glu = w_glu[0].T.astype(BF16)
    shard_out = w_out[0].astype(BF16)
    shard_ffn_in = w_ffn_in[0].T.astype(BF16)
    shard_ffn_out = w_ffn_out[0].astype(BF16)
    shard_meta = meta_tokens.T
    me = _my_index()

    def landing(shard, row_fn=_plain_rows):
        m_rows, cols = shard.shape
        return lax.dynamic_update_slice(lax.empty((N_DEV * m_rows, cols), shard.dtype), shard,
                                        (row_fn(me, m_rows), 0))

    first = [shard_in, shard_meta]
    ga = _send_start("gather_start_a", first, [landing(s) for s in first], chunked=False)
    later = [shard_glu + ga[4][0:1, 0:1].astype(BF16), shard_out, shard_ffn_in, shard_ffn_out]
    later_fns = [_plain_rows, _plain_rows, _ffn_in_rows, _plain_rows]
    gb = _send_start("gather_start_b", later, [landing(s, f) for s, f in zip(later, later_fns)], chunked=False,
                     row_fns=later_fns)

    nm_t = norm_mix + (ga[4][0:1, 0:1] + gb[4][0:1, 0:1])
    qn_t, kn_t = jnp.tile(q_norm, (1, N_Q_HEADS)), jnp.tile(k_norm, (1, N_KV_HEADS))
    e_mat = jnp.kron(jnp.eye(4, dtype=F32), jnp.ones((HEAD_DIM, HEAD_DIM), F32)).astype(BF16)

    def disc(lr, li, ldt, br, bi):
        return _discretize(lr[0], li[0], ldt[0], br[0], bi[0])

    (abar_re, abar_im, bbar_re, bbar_im), disc_vjp = jax.vjp(disc, lam_re, lam_im, log_dt, ssm_b_re, ssm_b_im)
    wb_re, wb_im = _block_diag_b(bbar_re).astype(BF16), _block_diag_b(bbar_im).astype(BF16)
    wc_re, wc_im = _block_diag_c(ssm_c_re[0]).astype(BF16), _block_diag_c(ssm_c_im[0]).astype(BF16)
    q_ssm = _pick(l_dim, 640, 64)
    tabs = _power_tables(abar_re.reshape(1, N_STATE), abar_im.reshape(1, N_STATE), q_ssm // GRP)

    h0, xn = _embed_norm(x2d, nm_t, tm_shift)
    wt_in, meta_t = _send_wait("gather_wait_a", ga[0], ga[1], ga[2], ga[3], xn, chunked=False)
    meta_pad = jnp.pad(meta_t.T, ((PAD, 0), (0, 0)))
    h0, xn = _embed_meta(meta_pad, nm_t, h0, xn)
    qkv = _matmul("proj_qkv", xn, wt_in, nt=True, tm=tm_big, tn=512, tk=D_MODEL, n=Q_W + 2 * KV_W, w_off=0)
    u = _matmul("proj_u", xn, wt_in, nt=True, tm=tm_big, tn=512, tk=D_MODEL, n=D_MODEL, w_off=3)
    gates = _matmul("proj_gates", xn, wt_in, nt=True, tm=tm_big, tn=512, tk=D_MODEL, n=2 * D_MODEL, w_off=5,
                    out_dtype=BF16)
    qn, kf, vf = _qk_prep(qkv, qn_t, kn_t, e_mat, tm_row)
    attn, lse = _attn_fwd(qn, kf, vf, attn_sinks)
    y, z, s_re, s_im = _ssm_fwd_kb(u, wb_re, wb_im, wc_re, wc_im, ssm_d, tabs, q_ssm)
    wt_glu, w_out_f, wt_ffn_in, w_ffn_out_f = _send_wait("gather_wait_b", gb[0], gb[1], gb[2], gb[3], z,
                                                         chunked=False, row_fns=later_fns)
    zab = _matmul("glu_proj", z, wt_glu, nt=True, tm=tm_big, tn=1024, tk=D_MODEL, out_dtype=BF16)
    merged = _merge_fwd(attn, zab, gates, attn_branch_norm, ssm_branch_norm, tm_row)
    h1, hn = _matmul("out_proj", merged, w_out_f, nt=False, tm=tm_mm, tn=1024, tk=D_MODEL, res=h0,
                     norm_g=norm_ffn)
    gu, act = _ffn_in_swiglu(hn, wt_ffn_in, tm_ffn)
    h2 = _matmul("ffn_out", act, w_ffn_out_f, nt=False, tm=tm_mm, tn=1024, tk=1408, res=h1)
    dh2, dh2_b, loss_part = _loss_grad(h2, target2d, tm_shift)

    dgu = _d_act_swiglu(dh2_b, w_ffn_out_f, gu, tm_ffn)

    def exchange_start(name, grads_b, row_fns=None):
        return _send_start(name, grads_b, [jnp.zeros(g.shape, BF16) for g in grads_b], chunked=True,
                           row_fns=row_fns)

    g_ffn_out, g_ffn_out_b = _matmul_tn("g_ffn_out", act, dh2_b, tm=1408, tn=1024, tl=tl_tn)
    g_ffn_in_t, g_ffn_in_b = _matmul_tn("g_ffn_in", dgu, hn, tm=1408, tn=1024, tl=tl_tn)
    ffn_fns = [_ffn_in_rows, _plain_rows]
    ex1 = exchange_start("exchange_start_ffn", [g_ffn_in_b, g_ffn_out_b], ffn_fns)
    dhn = _matmul("d_hn", dgu, wt_ffn_in, nt=False, tm=tm_big, tn=1024, tk=1408, out_dtype=BF16)
    dh1, dh1_b, g_norm_ffn = _norm_bwd_res("ffn_norm_bwd", h1, norm_ffn + ex1[4][0:1, 0:1], dhn, dh2, tm_row)
    dmerged = _matmul("d_merged", dh1_b, w_out_f, nt=True, tm=tm_big, tn=1024, tk=D_MODEL, out_dtype=BF16)
    dattn, dzab, dproj, g_abn, g_sbn = _merge_bwd(attn, zab, gates, attn_branch_norm, ssm_branch_norm,
                                                    dmerged, tm_row)
    g_out, g_out_b = _matmul_tn("g_out", merged, dh1_b, tm=1024, tn=1024, tl=tl_tn)
    dz = _matmul("d_z", dzab, wt_glu, nt=False, tm=tm_big, tn=1024, tk=1024, out_dtype=BF16)
    g_glu_t, g_glu_b = _matmul_tn("g_glu", dzab, z, tm=1024, tn=1024, tl=tl_tn)
    ex2 = exchange_start("exchange_start_mix", [g_glu_b, g_out_b])
    dproj, g_ssm_d, g_ar, g_ai, g_wbr, g_wbi, g_wcr, g_wci = _ssm_bwd_kb(
        dz, y, u, s_re, s_im, wb_re, wb_im, wc_re, wc_im, ssm_d + ex2[4][0:1, 0:1], tabs, q_ssm, dproj)
    dq, dkc, dkp, dkm, dvc, dvp, dvm, g_sinks = _attn_bwd(qn, kf, vf, attn_sinks, lse, attn, dattn)
    dproj, g_qn_t, g_kn_t = _qk_bwd(qkv, qn_t, kn_t, e_mat, dq, dkc, dkp, dkm, dvc, dvp, dvm, dproj)
    g_lam_re, g_lam_im, g_log_dt, g_b_re, g_b_im = disc_vjp(
        (g_ar.reshape(SSM_GROUPS, SSM_STATE), g_ai.reshape(SSM_GROUPS, SSM_STATE),
         _block_diag_b_t(g_wbr), _block_diag_b_t(g_wbi)))
    small_grads = {
        "q_norm": g_qn_t.reshape(N_Q_HEADS, HEAD_DIM).sum(0)[None],
        "k_norm": g_kn_t.reshape(N_KV_HEADS, HEAD_DIM).sum(0)[None], "attn_sinks": g_sinks,
        "lam_re": g_lam_re, "lam_im": g_lam_im, "log_dt": g_log_dt, "ssm_b_re": g_b_re, "ssm_b_im": g_b_im,
        "ssm_c_re": _block_diag_c_t(g_wcr)[None], "ssm_c_im": _block_diag_c_t(g_wci)[None],
        "ssm_d": g_ssm_d, "attn_branch_norm": g_abn, "ssm_branch_norm": g_sbn, "norm_ffn": g_norm_ffn,
    }
    early = [n for n in SMALL if n != "norm_mix"]
    packed_e = _pack([small_grads[n] for n in early])
    gs_e = _send_start("small_start_a", [packed_e], [landing(packed_e)], chunked=False)

    qkv_w, u_end = Q_W + 2 * KV_W, Q_W + 2 * KV_W + D_MODEL
    to_back = lambda a: jnp.concatenate([a[u_end:], a[qkv_w:u_end], a[:qkv_w]], axis=0)
    to_front = lambda a: jnp.concatenate([a[3 * D_MODEL:], a[2 * D_MODEL:3 * D_MODEL], a[:2 * D_MODEL]], axis=0)
    g_in_p, g_in_pb = _matmul_tn("g_in", dproj, xn, tm=1152, tn=1024, tl=tl_tn, after=gs_e[4])
    g_in_t, g_in_b = to_front(g_in_p), to_front(g_in_pb)
    ex3 = exchange_start("exchange_start_in", [g_in_b])
    dxn = _matmul("d_xn", dproj, to_back(wt_in), nt=False, tm=tm_big, tn=1024, tk=1152, out_dtype=BF16)
    grad_x2d, dmeta_blk, g_norm_mix = _final_bwd(h0, nm_t + ex3[4][0:1, 0:1], dxn, dh1, _pick(s_len, 512, BLK))
    packed_l = _pack([g_norm_mix, dmeta_blk[PAD:], loss_part])
    gs_l = _send_start("small_start_b", [packed_l], [landing(packed_l)], chunked=False)
    grads, deltas, new_m, new_v = {}, {}, {}, {}

    recv_ffn_in, recv_ffn_out = _send_wait("exchange_wait_ffn", ex1[0], ex1[1], ex1[2], ex1[3], gs_l[4],
                                           chunked=True, row_fns=ffn_fns)
    recv_glu, recv_out = _send_wait("exchange_wait_mix", ex2[0], ex2[1], ex2[2], ex2[3], recv_ffn_in,
                                    chunked=True)
    (recv_in,) = _send_wait("exchange_wait_in", ex3[0], ex3[1], ex3[2], ex3[3], recv_glu, chunked=True)
    big = [("w_in", g_in_t, True, recv_in, _plain_rows), ("w_glu", g_glu_t, True, recv_glu, _plain_rows),
           ("w_out", g_out, False, recv_out, _plain_rows), ("w_ffn_in", g_ffn_in_t, True, recv_ffn_in, _ffn_in_rows),
           ("w_ffn_out", g_ffn_out, False, recv_ffn_out, _plain_rows)]
    for name, g_full, transposed, recv, row_fn in big:
        m_rows = g_full.shape[0] // N_DEV
        own = lax.dynamic_slice(g_full, (row_fn(me, m_rows), 0), (m_rows, g_full.shape[1]))
        g_shard = _sum_slots("sum_" + name, recv, own)
        grads[name] = (g_shard.T if transposed else g_shard)[None]

    def adamw_2d(name):
        shp = weights[name].shape
        as2d = lambda a: a.reshape(shp[-2], shp[-1])
        d, nm, nv = _adamw("adamw_" + name, as2d(weights[name]), as2d(grads[name]), as2d(mom_m[name]),
                           as2d(mom_v[name]))
        deltas[name], new_m[name], new_v[name] = d.reshape(shp), nm.reshape(shp), nv.reshape(shp)
        return d

    for name in ["w_in", "w_glu", "w_out", "w_ffn_in", "w_ffn_out"]:
        last = adamw_2d(name)

    def small_sum(tag, gs, packed, after):
        (gathered,) = _send_wait("small_wait_" + tag, gs[0], gs[1], gs[2], gs[3], after, chunked=False)
        return _sum_gathered("sum_small_" + tag, gathered, packed.shape[0])

    def small_adamw(tag, names):
        d, nm, nv = _adamw_many("adamw_small_" + tag, [weights[n] for n in names], [grads[n] for n in names],
                                [mom_m[n] for n in names], [mom_v[n] for n in names])
        deltas.update(zip(names, d))
        new_m.update(zip(names, nm))
        new_v.update(zip(names, nv))

    g_sum_e = small_sum("a", gs_e, packed_e, last)
    grads.update(zip(early, _unpack(g_sum_e, [weights[n].shape for n in early])))
    wide = [n for n in early if n.startswith(("ssm_b", "ssm_c"))]
    small_adamw("wide", wide)
    g_sum_l = small_sum("b", gs_l, packed_l, g_sum_e)
    grads["norm_mix"], g_meta, loss_sum = _unpack(g_sum_l, [weights["norm_mix"].shape, (N_META, D_MODEL), (1, 1)])
    small_adamw("rest", [n for n in SMALL if n not in wide])
    grads["meta_tokens"] = lax.dynamic_slice(g_meta, (0, me * BLK), (N_META, BLK))
    adamw_2d("meta_tokens")

    loss = loss_sum[0, 0]
    return (loss, grad_x2d[None], *[grads[n] for n in WEIGHTS], *[deltas[n] for n in WEIGHTS],
            *[new_m[n] for n in WEIGHTS], *[new_v[n] for n in WEIGHTS])
```

```python
import math

import jax
import jax.numpy as jnp
from jax import lax
from jax.experimental import pallas as pl
from jax.experimental.pallas import tpu as pltpu

F32 = jnp.float32
BF16 = jnp.bfloat16

D_MODEL = 1024
N_META = 16
HEAD_DIM = 64
N_Q_HEADS = 16
N_KV_HEADS = 4
Q_W = N_Q_HEADS * HEAD_DIM
KV_W = N_KV_HEADS * HEAD_DIM
SSM_GROUPS = 64
SSM_GROUP_CH = 16
SSM_STATE = 64
N_STATE = SSM_GROUPS * SSM_STATE
D_FF = 2816
IN_COLS = Q_W + 2 * KV_W + 3 * D_MODEL
EPS = 1e-6
BLK = 128
PAD = BLK - N_META
N_DEV = 8
NEG = -1e30
SSM_KB = 8
ST_KB = N_STATE // SSM_KB
LB_KB = ST_KB // BLK
N_LB = N_STATE // BLK

ADAM_LR = 0.001
ADAM_B1 = 0.9
ADAM_B2 = 0.999
ADAM_EPS = 1e-08
ADAM_WD = 0.01
ADAM_STEP = 10

VMEM_LIMIT = 48 * 1024 * 1024
MESH = pl.DeviceIdType.MESH

SMALL = ["norm_mix", "q_norm", "k_norm", "attn_sinks", "lam_re", "lam_im", "log_dt", "ssm_b_re", "ssm_b_im",
         "ssm_c_re", "ssm_c_im", "ssm_d", "attn_branch_norm", "ssm_branch_norm", "norm_ffn"]
WEIGHTS = ["meta_tokens", "norm_mix", "w_in", "q_norm", "k_norm", "attn_sinks", "lam_re", "lam_im", "log_dt",
           "ssm_b_re", "ssm_b_im", "ssm_c_re", "ssm_c_im", "ssm_d", "w_glu", "attn_branch_norm",
           "ssm_branch_norm", "w_out", "norm_ffn", "w_ffn_in", "w_ffn_out"]


def _params(**kw):
    return pltpu.CompilerParams(vmem_limit_bytes=VMEM_LIMIT, **kw)


def _pick(n, cap, mult=16):
    best = None
    for d in range(mult, min(n, cap) + 1, mult):
        if n % d == 0:
            best = d
    assert best is not None, (n, cap, mult)
    return best


def _my_index():
    return 4 * lax.axis_index("x") + 2 * lax.axis_index("y") + lax.axis_index("c")


def _rms(x, g):
    r = lax.rsqrt(jnp.mean(x * x, axis=-1, keepdims=True) + EPS)
    return x * r * g


def _rms_bwd(x, g, dy):
    r = lax.rsqrt(jnp.mean(x * x, axis=-1, keepdims=True) + EPS)
    t = dy * g
    dx = r * t - x * (r * r * r) * jnp.mean(t * x, axis=-1, keepdims=True)
    dg = jnp.sum(dy * (x * r), axis=0, keepdims=True)
    return dx, dg


def _sigmoid(x):
    return jax.nn.sigmoid(x)


def _gelu(x):
    k = math.sqrt(2.0 / math.pi)
    return 0.5 * x * (1.0 + jnp.tanh(k * (x + 0.044715 * (x * x * x))))


def _gelu_grad(x):
    k = math.sqrt(2.0 / math.pi)
    t = jnp.tanh(k * (x + 0.044715 * (x * x * x)))
    return 0.5 * (1.0 + t) + 0.5 * x * (1.0 - t * t) * (k * (1.0 + 3.0 * 0.044715 * (x * x)))


def _head_mean(x, e_ref):
    hi = x.astype(BF16)
    r1 = x - hi.astype(F32)
    mid = r1.astype(BF16)
    lo = (r1 - mid.astype(F32)).astype(BF16)
    e = e_ref[...]
    out = []
    for b in range(x.shape[1] // 256):
        sl = slice(256 * b, 256 * b + 256)
        s = (jnp.dot(hi[:, sl], e, preferred_element_type=F32)
             + jnp.dot(mid[:, sl], e, preferred_element_type=F32)
             + jnp.dot(lo[:, sl], e, preferred_element_type=F32))
        out.append(s)
    s = out[0] if len(out) == 1 else jnp.concatenate(out, axis=1)
    return s * (1.0 / HEAD_DIM)


def _head_rms(x, g, e_ref):
    r = lax.rsqrt(_head_mean(x * x, e_ref) + EPS)
    return x * r * g


def _head_rms_bwd(x, g, dy, e_ref):
    r = lax.rsqrt(_head_mean(x * x, e_ref) + EPS)
    t = dy * g
    dx = r * t - x * (r * r * r) * _head_mean(t * x, e_ref)
    dg = jnp.sum(dy * (x * r), axis=0, keepdims=True)
    return dx, dg


def _lane_half(shape):
    lane = lax.broadcasted_iota(jnp.int32, shape, len(shape) - 1)
    return (lane >> 6) & 1


def _matmul(name, a, w, *, nt, tm, tn, tk, n=None, w_off=0, res=None, norm_g=None, out_dtype=F32):
    m_dim, k_dim = a.shape
    n_dim = n if n is not None else (w.shape[0] if nt else w.shape[1])
    gm, gn, gk = m_dim // tm, n_dim // tn, k_dim // tk
    assert gm * tm == m_dim and gn * tn == n_dim and gk * tk == k_dim, (name, a.shape, w.shape, tm, tn, tk)
    assert norm_g is None or tn == n_dim
    direct = out_dtype == F32
    dn = (((1,), (1,)), ((), ())) if nt else (((1,), (0,)), ((), ()))

    def body(*refs):
        refs = list(refs)
        a_ref, w_ref = refs[0], refs[1]
        pos = 2
        r_ref = g_ref = on_ref = None
        if res is not None:
            r_ref, pos = refs[pos], pos + 1
        if norm_g is not None:
            g_ref, pos = refs[pos], pos + 1
        o_ref, pos = refs[pos], pos + 1
        if norm_g is not None:
            on_ref, pos = refs[pos], pos + 1
        part = lax.dot_general(a_ref[...], w_ref[...], dn, preferred_element_type=F32)
        if gk == 1:
            r = part if r_ref is None else r_ref[...] + part
            o_ref[...] = r.astype(out_dtype)
            if on_ref is not None:
                on_ref[...] = _rms(r, g_ref[...]).astype(BF16)
            return
        acc = o_ref if direct else refs[pos]
        k = pl.program_id(2)

        @pl.when(k == 0)
        def _():
            acc[...] = part if r_ref is None or not direct else r_ref[...] + part

        @pl.when(k > 0)
        def _():
            acc[...] += part

        @pl.when(k == gk - 1)
        def _():
            if not direct:
                r = acc[...]
                if r_ref is not None:
                    r = r_ref[...] + r
                o_ref[...] = r.astype(out_dtype)
            if on_ref is not None:
                on_ref[...] = _rms(o_ref[...].astype(F32), g_ref[...]).astype(BF16)

    if nt:
        w_spec = pl.BlockSpec((tn, tk), lambda i, j, k: (j + w_off, k))
    else:
        w_spec = pl.BlockSpec((tk, tn), lambda i, j, k: (k, j))
    in_specs = [pl.BlockSpec((tm, tk), lambda i, j, k: (i, k)), w_spec]
    args = [a, w]
    out_spec = pl.BlockSpec((tm, tn), lambda i, j, k: (i, j))
    out_specs, out_shape = [out_spec], [jax.ShapeDtypeStruct((m_dim, n_dim), out_dtype)]
    if res is not None:
        in_specs.append(out_spec)
        args.append(res)
    if norm_g is not None:
        in_specs.append(pl.BlockSpec((1, tn), lambda i, j, k: (0, 0)))
        args.append(norm_g)
        out_specs.append(out_spec)
        out_shape.append(jax.ShapeDtypeStruct((m_dim, n_dim), BF16))
    out = pl.pallas_call(
        body, name=name, grid=(gm, gn, gk),
        in_specs=in_specs, out_specs=out_specs, out_shape=out_shape,
        scratch_shapes=[] if direct or gk == 1 else [pltpu.VMEM((tm, tn), F32)],
        compiler_params=_params(dimension_semantics=("parallel", "parallel", "arbitrary")),
    )(*args)
    return out if norm_g is not None else out[0]


def _matmul_tn(name, a, b, *, tm, tn, tl, after=None):
    l_dim, m_dim = a.shape
    n_dim = b.shape[1]
    gm, gn, gl = m_dim // tm, n_dim // tn, l_dim // tl
    assert gm * tm == m_dim and gn * tn == n_dim and gl * tl == l_dim, (name, a.shape, b.shape, tm, tn, tl)

    def body(*refs):
        a_ref, b_ref = refs[0], refs[1]
        o_ref, ob_ref = refs[-2], refs[-1]

        @pl.when(pl.program_id(2) == 0)
        def _():
            o_ref[...] = jnp.zeros_like(o_ref)

        o_ref[...] += lax.dot_general(a_ref[...], b_ref[...], (((0,), (0,)), ((), ())),
                                      preferred_element_type=F32)

        @pl.when(pl.program_id(2) == gl - 1)
        def _():
            ob_ref[...] = o_ref[...].astype(BF16)

    out_spec = pl.BlockSpec((tm, tn), lambda i, j, l: (i, j))
    in_specs = [pl.BlockSpec((tl, tm), lambda i, j, l: (l, i)), pl.BlockSpec((tl, tn), lambda i, j, l: (l, j))]
    args = [a, b]
    if after is not None:
        in_specs.append(pl.BlockSpec(memory_space=pl.ANY))
        args.append(after)
    return pl.pallas_call(
        body, name=name, grid=(gm, gn, gl),
        in_specs=in_specs,
        out_specs=[out_spec, out_spec],
        out_shape=[jax.ShapeDtypeStruct((m_dim, n_dim), F32), jax.ShapeDtypeStruct((m_dim, n_dim), BF16)],
        compiler_params=_params(dimension_semantics=("parallel", "parallel", "arbitrary")),
    )(*args)


def _row_spec(tm, cols, f=None):
    if f is None:
        return pl.BlockSpec((tm, cols), lambda i: (i, 0))
    return pl.BlockSpec((tm, cols), lambda i: (f(i), 0))


def _full_spec(shape):
    nd = len(shape)
    return pl.BlockSpec(shape, lambda i: (0,) * nd)


def _shifted_specs(n_sub, n_blocks):
    return [_row_spec(BLK, D_MODEL, (lambda i, k=k: jnp.clip(n_sub * i - 1 + k, 0, n_blocks - 1)))
            for k in range(n_sub)]


def _embed_norm(x2d, g, tm):
    s_len = x2d.shape[0]
    l_dim = s_len + BLK
    n_sub = tm // BLK

    def body(*refs):
        x_refs, g_ref, h_ref, xn_ref = refs[:n_sub], refs[n_sub], refs[n_sub + 1], refs[n_sub + 2]
        i = pl.program_id(0)
        for k in range(n_sub):
            rows = slice(BLK * k, BLK * k + BLK)
            h = x_refs[k][...] * jnp.where(n_sub * i + k >= 1, 1.0, 0.0)
            h_ref[rows, :] = h
            xn_ref[rows, :] = _rms(h, g_ref[...]).astype(BF16)

    return pl.pallas_call(
        body, name="embed_norm", grid=(l_dim // tm,),
        in_specs=_shifted_specs(n_sub, s_len // BLK) + [_full_spec((1, D_MODEL))],
        out_specs=[_row_spec(tm, D_MODEL), _row_spec(tm, D_MODEL)],
        out_shape=[jax.ShapeDtypeStruct((l_dim, D_MODEL), F32),
                   jax.ShapeDtypeStruct((l_dim, D_MODEL), BF16)],
        compiler_params=_params(),
    )(*([x2d] * n_sub), g)


def _embed_meta(meta_pad, g, h0, xn):
    def body(mp_ref, g_ref, h_in, xn_in, h_ref, xn_ref):
        h_ref[...] = mp_ref[...]
        xn_ref[...] = _rms(mp_ref[...], g_ref[...]).astype(BF16)

    any_spec = pl.BlockSpec(memory_space=pl.ANY)
    return pl.pallas_call(
        body, name="embed_meta", grid=(1,),
        in_specs=[_full_spec((BLK, D_MODEL)), _full_spec((1, D_MODEL)), any_spec, any_spec],
        out_specs=[_row_spec(BLK, D_MODEL), _row_spec(BLK, D_MODEL)],
        out_shape=[jax.ShapeDtypeStruct(h0.shape, F32), jax.ShapeDtypeStruct(xn.shape, BF16)],
        input_output_aliases={2: 0, 3: 1},
        compiler_params=_params(),
    )(meta_pad, g, h0, xn)


KVX_W = 2 * N_KV_HEADS * BLK


def _qk_prep(qkv, q_norm_t, k_norm_t, e_mat, tm):
    l_dim = qkv.shape[0]

    def body(x_ref, qg_ref, kg_ref, e_ref, q_ref, kf_ref, vf_ref):
        x = x_ref[...]
        q = _head_rms(x[:, :Q_W], qg_ref[...], e_ref) * (HEAD_DIM ** -0.5)
        q_ref[...] = q.astype(BF16)
        k = _head_rms(x[:, Q_W:Q_W + KV_W], kg_ref[...], e_ref)
        v = x[:, Q_W + KV_W:Q_W + 2 * KV_W]
        half = _lane_half((tm, BLK))
        for src, dst in ((k, kf_ref), (v, vf_ref)):
            for kv in range(N_KV_HEADS):
                blk = src[:, BLK * (kv // 2):BLK * (kv // 2) + BLK]
                swapped = pltpu.roll(blk, HEAD_DIM, axis=1)
                for e in range(2):
                    val = blk if kv % 2 == e else swapped
                    idx = 2 * kv + e
                    dst[:, BLK * idx:BLK * idx + BLK] = jnp.where(half == e, val, 0.0).astype(BF16)

    return pl.pallas_call(
        body, name="qk_prep", grid=(l_dim // tm,),
        in_specs=[_row_spec(tm, Q_W + 2 * KV_W), _full_spec((1, Q_W)), _full_spec((1, KV_W)),
                  _full_spec((256, 256))],
        out_specs=[_row_spec(tm, Q_W), _row_spec(tm, KVX_W), _row_spec(tm, KVX_W)],
        out_shape=[jax.ShapeDtypeStruct((l_dim, Q_W), BF16),
                   jax.ShapeDtypeStruct((l_dim, KVX_W), BF16),
                   jax.ShapeDtypeStruct((l_dim, KVX_W), BF16)],
        compiler_params=_params(),
    )(qkv, q_norm_t, k_norm_t, e_mat)


def _merge_fwd(attn, zab, gates, abn, sbn, tm):
    l_dim = attn.shape[0]

    def body(a_ref, z_ref, g_ref, an_ref, sn_ref, o_ref):
        z = z_ref[...].astype(F32)
        g = g_ref[...].astype(F32)
        ssm = z[:, :D_MODEL] * _sigmoid(z[:, D_MODEL:])
        merged = (_sigmoid(g[:, :D_MODEL]) * _rms(a_ref[...], an_ref[...])
                  + _sigmoid(g[:, D_MODEL:]) * _rms(ssm, sn_ref[...]))
        o_ref[...] = merged.astype(BF16)

    return pl.pallas_call(
        body, name="merge_fwd", grid=(l_dim // tm,),
        in_specs=[_row_spec(tm, D_MODEL), _row_spec(tm, 2 * D_MODEL), _row_spec(tm, 2 * D_MODEL),
                  _full_spec((1, D_MODEL)), _full_spec((1, D_MODEL))],
        out_specs=_row_spec(tm, D_MODEL),
        out_shape=jax.ShapeDtypeStruct((l_dim, D_MODEL), BF16),
        compiler_params=_params(),
    )(attn, zab, gates, abn, sbn)


def _merge_bwd(attn, zab, gates, abn, sbn, dmerged, tm):
    l_dim = attn.shape[0]

    def body(a_ref, z_ref, g_ref, an_ref, sn_ref, dm_ref, da_ref, dz_ref, dg_ref, dan_ref, dsn_ref):
        @pl.when(pl.program_id(0) == 0)
        def _():
            dan_ref[...] = jnp.zeros_like(dan_ref)
            dsn_ref[...] = jnp.zeros_like(dsn_ref)

        z = z_ref[...].astype(F32)
        g = g_ref[...].astype(F32)
        dm = dm_ref[...].astype(F32)
        attn_v = a_ref[...]
        za, zb = z[:, :D_MODEL], z[:, D_MODEL:]
        sb = _sigmoid(zb)
        ssm = za * sb
        s_ga, s_gs = _sigmoid(g[:, :D_MODEL]), _sigmoid(g[:, D_MODEL:])
        a_n = _rms(attn_v, an_ref[...])
        s_n = _rms(ssm, sn_ref[...])
        dg_ref[:, :D_MODEL] = (dm * a_n * s_ga * (1.0 - s_ga)).astype(BF16)
        dg_ref[:, D_MODEL:] = (dm * s_n * s_gs * (1.0 - s_gs)).astype(BF16)
        dattn, dan = _rms_bwd(attn_v, an_ref[...], dm * s_ga)
        dssm, dsn = _rms_bwd(ssm, sn_ref[...], dm * s_gs)
        da_ref[...] = dattn.astype(BF16)
        dz_ref[:, :D_MODEL] = (dssm * sb).astype(BF16)
        dz_ref[:, D_MODEL:] = (dssm * za * sb * (1.0 - sb)).astype(BF16)
        dan_ref[...] += dan
        dsn_ref[...] += dsn

    return pl.pallas_call(
        body, name="merge_bwd", grid=(l_dim // tm,),
        in_specs=[_row_spec(tm, D_MODEL), _row_spec(tm, 2 * D_MODEL), _row_spec(tm, 2 * D_MODEL),
                  _full_spec((1, D_MODEL)), _full_spec((1, D_MODEL)), _row_spec(tm, D_MODEL)],
        out_specs=[_row_spec(tm, D_MODEL), _row_spec(tm, 2 * D_MODEL), _row_spec(tm, 2 * D_MODEL),
                   _full_spec((1, D_MODEL)), _full_spec((1, D_MODEL))],
        out_shape=[jax.ShapeDtypeStruct((l_dim, D_MODEL), BF16),
                   jax.ShapeDtypeStruct((l_dim, 2 * D_MODEL), BF16),
                   jax.ShapeDtypeStruct((l_dim, IN_COLS), BF16),
                   jax.ShapeDtypeStruct((1, D_MODEL), F32), jax.ShapeDtypeStruct((1, D_MODEL), F32)],
        compiler_params=_params(),
    )(attn, zab, gates, abn, sbn, dmerged)


FF_TILE = D_FF // 2


def _ffn_in_swiglu(hn, wt_ffn_in, tm):
    l_dim = hn.shape[0]
    nt = (((1,), (1,)), ((), ()))

    def body(a_ref, w_ref, gu_ref, act_ref):
        r = lax.dot_general(a_ref[...], w_ref[...], nt, preferred_element_type=F32)
        gate, up = r[:, :FF_TILE], r[:, FF_TILE:]
        gu_ref[...] = r.astype(BF16)
        act_ref[...] = (gate * _sigmoid(gate) * up).astype(BF16)

    return pl.pallas_call(
        body, name="ffn_in_swiglu", grid=(l_dim // tm, 2),
        in_specs=[pl.BlockSpec((tm, D_MODEL), lambda i, j: (i, 0)),
                  pl.BlockSpec((2 * FF_TILE, D_MODEL), lambda i, j: (j, 0))],
        out_specs=[pl.BlockSpec((tm, 2 * FF_TILE), lambda i, j: (i, j)),
                   pl.BlockSpec((tm, FF_TILE), lambda i, j: (i, j))],
        out_shape=[jax.ShapeDtypeStruct((l_dim, 2 * D_FF), BF16), jax.ShapeDtypeStruct((l_dim, D_FF), BF16)],
        compiler_params=_params(dimension_semantics=("parallel", "parallel")),
    )(hn, wt_ffn_in)


def _d_act_swiglu(dh2_b, w_ffn_out, gu, tm):
    l_dim = dh2_b.shape[0]
    nt = (((1,), (1,)), ((), ()))

    def body(d_ref, w_ref, gu_ref, o_ref):
        d = lax.dot_general(d_ref[...], w_ref[...], nt, preferred_element_type=F32)
        gate = gu_ref[:, :FF_TILE].astype(F32)
        up = gu_ref[:, FF_TILE:].astype(F32)
        s = _sigmoid(gate)
        o_ref[:, :FF_TILE] = (d * up * (s * (1.0 + gate * (1.0 - s)))).astype(BF16)
        o_ref[:, FF_TILE:] = (d * (gate * s)).astype(BF16)

    return pl.pallas_call(
        body, name="d_act_swiglu", grid=(l_dim // tm, 2),
        in_specs=[pl.BlockSpec((tm, D_MODEL), lambda i, j: (i, 0)),
                  pl.BlockSpec((FF_TILE, D_MODEL), lambda i, j: (j, 0)),
                  pl.BlockSpec((tm, 2 * FF_TILE), lambda i, j: (i, j))],
        out_specs=pl.BlockSpec((tm, 2 * FF_TILE), lambda i, j: (i, j)),
        out_shape=jax.ShapeDtypeStruct((l_dim, 2 * D_FF), BF16),
        compiler_params=_params(dimension_semantics=("parallel", "parallel")),
    )(dh2_b, w_ffn_out, gu)


def _loss_grad(h2, target2d, tm):
    l_dim = h2.shape[0]
    n_sub = tm // BLK

    def body(*refs):
        h_ref, t_refs = refs[0], refs[1:1 + n_sub]
        d_ref, db_ref, loss_ref = refs[1 + n_sub:]
        i = pl.program_id(0)

        @pl.when(i == 0)
        def _():
            loss_ref[...] = jnp.zeros_like(loss_ref)

        for k in range(n_sub):
            rows = slice(BLK * k, BLK * k + BLK)
            real = jnp.where(n_sub * i + k >= 1, 1.0, 0.0)
            err = (h_ref[rows, :] - t_refs[k][...]) * real
            d = err * (1.0 / D_MODEL)
            d_ref[rows, :] = d
            db_ref[rows, :] = d.astype(BF16)
            loss_ref[...] += 0.5 * jnp.sum(jnp.mean(err * err, axis=-1, keepdims=True), axis=0, keepdims=True)

    return pl.pallas_call(
        body, name="loss_grad", grid=(l_dim // tm,),
        in_specs=[_row_spec(tm, D_MODEL)] + _shifted_specs(n_sub, target2d.shape[0] // BLK),
        out_specs=[_row_spec(tm, D_MODEL), _row_spec(tm, D_MODEL), _full_spec((1, 1))],
        out_shape=[jax.ShapeDtypeStruct((l_dim, D_MODEL), F32), jax.ShapeDtypeStruct((l_dim, D_MODEL), BF16),
                   jax.ShapeDtypeStruct((1, 1), F32)],
        compiler_params=_params(),
    )(h2, *([target2d] * n_sub))


def _norm_bwd_res(name, h, g, dy, dres, tm):
    l_dim = h.shape[0]

    def body(h_ref, g_ref, dy_ref, dr_ref, o_ref, ob_ref, dg_ref):
        @pl.when(pl.program_id(0) == 0)
        def _():
            dg_ref[...] = jnp.zeros_like(dg_ref)

        dx, dg = _rms_bwd(h_ref[...], g_ref[...], dy_ref[...].astype(F32))
        out = dr_ref[...] + dx
        o_ref[...] = out
        ob_ref[...] = out.astype(BF16)
        dg_ref[...] += dg

    return pl.pallas_call(
        body, name=name, grid=(l_dim // tm,),
        in_specs=[_row_spec(tm, D_MODEL), _full_spec((1, D_MODEL)), _row_spec(tm, D_MODEL), _row_spec(tm, D_MODEL)],
        out_specs=[_row_spec(tm, D_MODEL), _row_spec(tm, D_MODEL), _full_spec((1, D_MODEL))],
        out_shape=[jax.ShapeDtypeStruct((l_dim, D_MODEL), F32), jax.ShapeDtypeStruct((l_dim, D_MODEL), BF16),
                   jax.ShapeDtypeStruct((1, D_MODEL), F32)],
        compiler_params=_params(),
    )(h, g, dy, dres)


def _final_bwd(h0, g, dxn, dh1, tm):
    l_dim = h0.shape[0]
    n_sub = tm // BLK
    n_tiles = (l_dim - BLK) // tm

    def sub_specs():
        return [_row_spec(BLK, D_MODEL, (lambda j, k=k: jnp.where(j < n_tiles, n_sub * j + 1 + k, 0)))
                for k in range(n_sub)]

    def body(*refs):
        h_refs, g_ref = refs[:n_sub], refs[n_sub]
        dy_refs, dr_refs = refs[n_sub + 1:2 * n_sub + 1], refs[2 * n_sub + 1:3 * n_sub + 1]
        gx_ref, gm_ref, dg_ref = refs[3 * n_sub + 1:]
        j = pl.program_id(0)

        @pl.when(j == 0)
        def _():
            dg_ref[...] = jnp.zeros_like(dg_ref)

        def block(k):
            dx, dg = _rms_bwd(h_refs[k][...], g_ref[...], dy_refs[k][...].astype(F32))
            dg_ref[...] += dg
            return dr_refs[k][...] + dx

        @pl.when(j < n_tiles)
        def _():
            for k in range(n_sub):
                gx_ref[BLK * k:BLK * k + BLK, :] = block(k)

        @pl.when(j == n_tiles)
        def _():
            gm_ref[...] = block(0)

    return pl.pallas_call(
        body, name="final_bwd", grid=(n_tiles + 1,),
        in_specs=sub_specs() + [_full_spec((1, D_MODEL))] + sub_specs() + sub_specs(),
        out_specs=[_row_spec(tm, D_MODEL, lambda j: jnp.minimum(j, n_tiles - 1)), _full_spec((BLK, D_MODEL)),
                   _full_spec((1, D_MODEL))],
        out_shape=[jax.ShapeDtypeStruct((l_dim - BLK, D_MODEL), F32), jax.ShapeDtypeStruct((BLK, D_MODEL), F32),
                   jax.ShapeDtypeStruct((1, D_MODEL), F32)],
        compiler_params=_params(),
    )(*([h0] * n_sub), g, *([dxn] * n_sub), *([dh1] * n_sub))


def _attn_valid(n):
    shape = (2 * BLK, 3 * BLK)
    qi = lax.broadcasted_iota(jnp.int32, shape, 0) & (BLK - 1)
    col = lax.broadcasted_iota(jnp.int32, shape, 1)
    kj = col & (BLK - 1)
    part = col >> 7
    nn = jnp.zeros(shape, jnp.int32) + n
    meta_ok = (part == 0) & (kj >= PAD) & (nn >= 1)
    prev_ok = (part == 1) & (kj > qi) & (nn >= 2)
    cur_ok = (part == 2) & (kj <= qi) & ((nn >= 1) | (kj >= PAD))
    return meta_ok | prev_ok | cur_ok


def _attn_scores(q_ref, kwin, sk_ref, valid, kv, e):
    qs = jnp.concatenate([q_ref[:, BLK * (2 * kv):BLK * (2 * kv) + BLK],
                          q_ref[:, BLK * (2 * kv + 1):BLK * (2 * kv + 1) + BLK]], axis=0)
    s = lax.dot_general(qs, kwin, (((1,), (1,)), ((), ())), preferred_element_type=F32)
    h0 = 4 * kv + e
    row = lax.broadcasted_iota(jnp.int32, (2 * BLK, 1), 0)
    sink = jnp.where(row < BLK, sk_ref[:, h0:h0 + 1], sk_ref[:, h0 + 2:h0 + 3])
    return qs, jnp.where(valid, s, NEG), sink


def _attn_specs(nb):
    prev = lambda i: jnp.maximum(i - 1, 0)
    zero = lambda i: 0
    kv_specs = [_row_spec(BLK, KVX_W, zero), _row_spec(BLK, KVX_W, prev), _row_spec(BLK, KVX_W)]
    return kv_specs


def _attn_fwd(qn, kf, vf, sinks):
    l_dim = qn.shape[0]
    nb = l_dim // BLK

    def body(q_ref, km_ref, kp_ref, kc_ref, vm_ref, vp_ref, vc_ref, sk_ref, o_ref, lse_ref):
        valid = _attn_valid(pl.program_id(0))
        lane = lax.broadcasted_iota(jnp.int32, (BLK, BLK), 1)
        lse_all = jnp.zeros((BLK, BLK), F32)
        for kv in range(N_KV_HEADS):
            outs = []
            for e in range(2):
                sl = slice(BLK * (2 * kv + e), BLK * (2 * kv + e) + BLK)
                kwin = jnp.concatenate([km_ref[:, sl], kp_ref[:, sl], kc_ref[:, sl]], axis=0)
                vwin = jnp.concatenate([vm_ref[:, sl], vp_ref[:, sl], vc_ref[:, sl]], axis=0)
                _, s, sink = _attn_scores(q_ref, kwin, sk_ref, valid, kv, e)
                m = jnp.maximum(jnp.max(s, axis=-1, keepdims=True), sink)
                ex = jnp.exp(s - m)
                den = jnp.sum(ex, axis=-1, keepdims=True) + jnp.exp(sink - m)
                p = ex * (1.0 / den)
                outs.append(jnp.dot(p.astype(BF16), vwin, preferred_element_type=F32))
                lse = m + jnp.log(den)
                lse_all = jnp.where(lane == 4 * kv + e, lse[:BLK], lse_all)
                lse_all = jnp.where(lane == 4 * kv + 2 + e, lse[BLK:], lse_all)
            o = outs[0] + outs[1]
            o_ref[:, BLK * (2 * kv):BLK * (2 * kv) + BLK] = o[:BLK]
            o_ref[:, BLK * (2 * kv + 1):BLK * (2 * kv + 1) + BLK] = o[BLK:]
        lse_ref[...] = lse_all

    kv_specs = _attn_specs(nb)
    return pl.pallas_call(
        body, name="attn_fwd", grid=(nb,),
        in_specs=[_row_spec(BLK, Q_W)] + kv_specs + kv_specs + [_full_spec((1, N_Q_HEADS))],
        out_specs=[_row_spec(BLK, Q_W), _row_spec(BLK, BLK)],
        out_shape=[jax.ShapeDtypeStruct((l_dim, Q_W), F32), jax.ShapeDtypeStruct((l_dim, BLK), F32)],
        compiler_params=_params(),
    )(qn, kf, kf, kf, vf, vf, vf, sinks)


def _attn_bwd(qn, kf, vf, sinks, lse, attn, dattn):
    l_dim = qn.shape[0]
    nb = l_dim // BLK
    wide = KVX_W
    tn = (((0,), (0,)), ((), ()))
    nt = (((1,), (1,)), ((), ()))

    def body(q_ref, km_ref, kp_ref, kc_ref, vm_ref, vp_ref, vc_ref, sk_ref, lse_ref, o_ref, do_ref,
             dq_ref, dkc_ref, dkp_ref, dkm_ref, dvc_ref, dvp_ref, dvm_ref, dsk_ref):
        @pl.when(pl.program_id(0) == 0)
        def _():
            dkm_ref[...] = jnp.zeros_like(dkm_ref)
            dvm_ref[...] = jnp.zeros_like(dvm_ref)
            dsk_ref[...] = jnp.zeros_like(dsk_ref)

        valid = _attn_valid(pl.program_id(0))
        half = _lane_half((BLK, BLK))
        half2 = _lane_half((2 * BLK, BLK))
        lane16 = lax.broadcasted_iota(jnp.int32, (1, N_Q_HEADS), 1)
        dsk = jnp.zeros((1, N_Q_HEADS), F32)
        for kv in range(N_KV_HEADS):
            j0, j1 = 2 * kv, 2 * kv + 1
            do0 = do_ref[:, BLK * j0:BLK * j0 + BLK]
            do1 = do_ref[:, BLK * j1:BLK * j1 + BLK]
            do0f, do1f = do0.astype(F32), do1.astype(F32)
            prod0 = do0f * o_ref[:, BLK * j0:BLK * j0 + BLK]
            prod1 = do1f * o_ref[:, BLK * j1:BLK * j1 + BLK]
            dos = jnp.concatenate([do0, do1], axis=0)
            dqs = []
            for e in range(2):
                sl = slice(BLK * (2 * kv + e), BLK * (2 * kv + e) + BLK)
                kwin = jnp.concatenate([km_ref[:, sl], kp_ref[:, sl], kc_ref[:, sl]], axis=0)
                vwin = jnp.concatenate([vm_ref[:, sl], vp_ref[:, sl], vc_ref[:, sl]], axis=0)
                qs, s, sink = _attn_scores(q_ref, kwin, sk_ref, valid, kv, e)
                h0 = 4 * kv + e
                lse_rows = jnp.concatenate([lse_ref[:, h0:h0 + 1], lse_ref[:, h0 + 2:h0 + 3]], axis=0)
                p = jnp.exp(s - lse_rows)
                p_sink = jnp.exp(sink - lse_rows)
                delta = jnp.concatenate(
                    [jnp.sum(jnp.where(half == e, prod0, 0.0), axis=-1, keepdims=True),
                     jnp.sum(jnp.where(half == e, prod1, 0.0), axis=-1, keepdims=True)], axis=0)
                dp = lax.dot_general(dos, vwin, nt, preferred_element_type=F32)
                ds = (p * (dp - delta)).astype(BF16)
                pb = p.astype(BF16)
                dqs.append(jnp.dot(ds, kwin, preferred_element_type=F32))
                dk = lax.dot_general(ds, qs, tn, preferred_element_type=F32)
                dv = lax.dot_general(pb, dos, tn, preferred_element_type=F32)
                dkm_ref[:, sl] += dk[:BLK]
                dkp_ref[:, sl] = dk[BLK:2 * BLK].astype(BF16)
                dkc_ref[:, sl] = dk[2 * BLK:].astype(BF16)
                dvm_ref[:, sl] += dv[:BLK]
                dvp_ref[:, sl] = dv[BLK:2 * BLK].astype(BF16)
                dvc_ref[:, sl] = dv[2 * BLK:].astype(BF16)
                sink_g = -(p_sink * delta)
                g_lo = jnp.sum(sink_g[:BLK], axis=0, keepdims=True)
                g_hi = jnp.sum(sink_g[BLK:], axis=0, keepdims=True)
                dsk = dsk + jnp.where(lane16 == h0, g_lo, 0.0) + jnp.where(lane16 == h0 + 2, g_hi, 0.0)
            dq = jnp.where(half2 == 0, dqs[0], dqs[1])
            dq_ref[:, BLK * j0:BLK * j0 + BLK] = dq[:BLK].astype(BF16)
            dq_ref[:, BLK * j1:BLK * j1 + BLK] = dq[BLK:].astype(BF16)
        dsk_ref[...] += dsk

    kv_specs = _attn_specs(nb)
    row_wide = _row_spec(BLK, wide)
    acc_wide = _full_spec((BLK, wide))
    big = jax.ShapeDtypeStruct((l_dim, wide), BF16)
    return pl.pallas_call(
        body, name="attn_bwd", grid=(nb,),
        in_specs=[_row_spec(BLK, Q_W)] + kv_specs + kv_specs
        + [_full_spec((1, N_Q_HEADS)), _row_spec(BLK, BLK), _row_spec(BLK, Q_W), _row_spec(BLK, Q_W)],
        out_specs=[_row_spec(BLK, Q_W), row_wide, row_wide, acc_wide, row_wide, row_wide, acc_wide,
                   _full_spec((1, N_Q_HEADS))],
        out_shape=[jax.ShapeDtypeStruct((l_dim, Q_W), BF16), big, big, jax.ShapeDtypeStruct((BLK, wide), F32),
                   big, big, jax.ShapeDtypeStruct((BLK, wide), F32), jax.ShapeDtypeStruct((1, N_Q_HEADS), F32)],
        compiler_params=_params(),
    )(qn, kf, kf, kf, vf, vf, vf, sinks, lse, attn, dattn)


def _qk_bwd(qkv, q_norm_t, k_norm_t, e_mat, dq, dkc, dkp, dkm, dvc, dvp, dvm, dproj):
    l_dim = qkv.shape[0]
    nb = l_dim // BLK
    wide = KVX_W

    def fold(x):
        half = _lane_half((BLK, BLK))
        blocks = []
        for kb in range(2):
            t = []
            for kv in (2 * kb, 2 * kb + 1):
                own = kv % 2
                a = x[:, BLK * (2 * kv + own):BLK * (2 * kv + own) + BLK]
                b = pltpu.roll(x[:, BLK * (2 * kv + 1 - own):BLK * (2 * kv + 1 - own) + BLK], HEAD_DIM, axis=1)
                t.append(a + b)
            blocks.append(jnp.where(half == 0, t[0], t[1]))
        return jnp.concatenate(blocks, axis=1)

    def body(x_ref, qg_ref, kg_ref, e_ref, dq_ref, dkc_ref, dkp_ref, dkm_ref, dvc_ref, dvp_ref, dvm_ref,
             dproj_ref, o_ref, dqg_ref, dkg_ref):
        i = pl.program_id(0)

        @pl.when(i == 0)
        def _():
            dqg_ref[...] = jnp.zeros_like(dqg_ref)
            dkg_ref[...] = jnp.zeros_like(dkg_ref)

        first = jnp.where(i == 0, 1.0, 0.0)
        not_last = jnp.where(i < nb - 1, 1.0, 0.0)
        dk_x = dkc_ref[...].astype(F32) + not_last * dkp_ref[...].astype(F32) + first * dkm_ref[...]
        dv_x = dvc_ref[...].astype(F32) + not_last * dvp_ref[...].astype(F32) + first * dvm_ref[...]
        x = x_ref[...]
        dqx, dqg = _head_rms_bwd(x[:, :Q_W], qg_ref[...], dq_ref[...].astype(F32) * (HEAD_DIM ** -0.5), e_ref)
        dkx, dkg = _head_rms_bwd(x[:, Q_W:Q_W + KV_W], kg_ref[...], fold(dk_x), e_ref)
        o_ref[:, :Q_W] = dqx.astype(BF16)
        o_ref[:, Q_W:Q_W + KV_W] = dkx.astype(BF16)
        o_ref[:, Q_W + KV_W:] = fold(dv_x).astype(BF16)
        dqg_ref[...] += dqg
        dkg_ref[...] += dkg

    nxt = lambda i: jnp.minimum(i + 1, nb - 1)
    row_wide = _row_spec(BLK, wide)
    nxt_wide = _row_spec(BLK, wide, nxt)
    acc_wide = _full_spec((BLK, wide))
    qkv_w = Q_W + 2 * KV_W
    in_specs = [_row_spec(BLK, qkv_w), _full_spec((1, Q_W)), _full_spec((1, KV_W)),
                _full_spec((256, 256)), _row_spec(BLK, Q_W),
                row_wide, nxt_wide, acc_wide, row_wide, nxt_wide, acc_wide, pl.BlockSpec(memory_space=pl.ANY)]
    return pl.pallas_call(
        body, name="qk_bwd", grid=(nb,),
        in_specs=in_specs,
        out_specs=[pl.BlockSpec((BLK, qkv_w), lambda i: (i, 3 * D_MODEL // qkv_w)),
                   _full_spec((1, Q_W)), _full_spec((1, KV_W))],
        input_output_aliases={len(in_specs) - 1: 0},
        out_shape=[jax.ShapeDtypeStruct(dproj.shape, BF16),
                   jax.ShapeDtypeStruct((1, Q_W), F32), jax.ShapeDtypeStruct((1, KV_W), F32)],
        compiler_params=_params(),
    )(qkv, q_norm_t, k_norm_t, e_mat, dq, dkc, dkp, dkm, dvc, dvp, dvm, dproj)


GRP = 8


def _strided(r, g):
    return pl.ds(r, g, stride=GRP)


def _lane_slab(ref, i, r, g):
    return ref[i, _strided(r, g), :]


def _chunk_carries(xr_ref, xi_ref, i, ar, ai, sqr_ref, sqi_ref, seq_r, seq_i, cin_r, cin_i, sign, reverse):
    g = xr_ref.shape[1] // GRP
    sr = si = None
    for r in (range(GRP - 1, -1, -1) if reverse else range(GRP)):
        xr, xi = _lane_slab(xr_ref, i, r, g), _lane_slab(xi_ref, i, r, g)
        if sr is not None:
            xr, xi = xr + ar * sr - ai * si, xi + ar * si + ai * sr
        sr, si = xr, xi
    row = lax.broadcasted_iota(jnp.int32, sr.shape, 0)
    idx, s = 0, 1
    while s < g:
        br = sqr_ref[idx:idx + 1, BLK * i:BLK * i + BLK]
        bi = sign * sqi_ref[idx:idx + 1, BLK * i:BLK * i + BLK]
        shift, keep = (g - s, row < g - s) if reverse else (s, row >= s)
        pr = jnp.where(keep, pltpu.roll(sr, shift, axis=0), 0.0)
        pi = jnp.where(keep, pltpu.roll(si, shift, axis=0), 0.0)
        sr, si = sr + br * pr - bi * pi, si + br * pi + bi * pr
        idx, s = idx + 1, 2 * s
    return sr + seq_r * cin_r - seq_i * cin_i, si + seq_r * cin_i + seq_i * cin_r


def _ssm_fwd_kb(u, wb_re, wb_im, wc_re, wc_im, d_skip, tabs, q):
    l_dim = u.shape[0]
    nc = l_dim // q
    g = q // GRP

    def body(u_ref, wbr_ref, wbi_ref, wcr_ref, wci_ref, d_ref, a1r_ref, a1i_ref, sqr_ref, sqi_ref,
             seqr_ref, seqi_ref, y_ref, z_ref, sr_ref, si_ref, cr_ref, ci_ref, xr_ref, xi_ref):
        @pl.when(pl.program_id(1) == 0)
        def _():
            cr_ref[...] = jnp.zeros_like(cr_ref)
            ci_ref[...] = jnp.zeros_like(ci_ref)

        u_kb = u_ref[...]
        ub = u_kb.astype(BF16)
        xr = jnp.dot(ub, wbr_ref[0], preferred_element_type=F32)
        xi = jnp.dot(ub, wbi_ref[0], preferred_element_type=F32)
        for i in range(LB_KB):
            xr_ref[i] = xr[:, BLK * i:BLK * i + BLK]
            xi_ref[i] = xi[:, BLK * i:BLK * i + BLK]
        row = lax.broadcasted_iota(jnp.int32, (g, BLK), 0)
        for i in range(LB_KB):
            lanes = slice(BLK * i, BLK * i + BLK)
            ar, ai = a1r_ref[0:1, lanes], a1i_ref[0:1, lanes]
            cin_r, cin_i = cr_ref[0:1, lanes], ci_ref[0:1, lanes]
            tr, ti = _chunk_carries(xr_ref, xi_ref, i, ar, ai, sqr_ref, sqi_ref, seqr_ref[:, lanes],
                                    seqi_ref[:, lanes], cin_r, cin_i, 1.0, reverse=False)
            cr_ref[0:1, lanes] = jnp.sum(jnp.where(row == g - 1, tr, 0.0), axis=0, keepdims=True)
            ci_ref[0:1, lanes] = jnp.sum(jnp.where(row == g - 1, ti, 0.0), axis=0, keepdims=True)
            pr = jnp.where(row == 0, cin_r, pltpu.roll(tr, 1, axis=0))
            pi = jnp.where(row == 0, cin_i, pltpu.roll(ti, 1, axis=0))
            for r in range(GRP):
                pr, pi = (_lane_slab(xr_ref, i, r, g) + ar * pr - ai * pi,
                          _lane_slab(xi_ref, i, r, g) + ar * pi + ai * pr)
                sr_ref[i, _strided(r, g), :] = pr
                si_ref[i, _strided(r, g), :] = pi
        s_r = jnp.concatenate([sr_ref[i] for i in range(LB_KB)], axis=1)
        s_i = jnp.concatenate([si_ref[i] for i in range(LB_KB)], axis=1)
        y = (jnp.dot(s_r.astype(BF16), wcr_ref[0], preferred_element_type=F32)
             - jnp.dot(s_i.astype(BF16), wci_ref[0], preferred_element_type=F32)
             + d_ref[...] * u_kb)
        y_ref[...] = y.astype(BF16)
        z_ref[...] = _gelu(y).astype(BF16)

    chan = pl.BlockSpec((q, BLK), lambda k, c: (c, k))
    wb_spec = pl.BlockSpec((1, BLK, ST_KB), lambda k, c: (k, 0, 0))
    wc_spec = pl.BlockSpec((1, ST_KB, BLK), lambda k, c: (k, 0, 0))
    tab_specs = [pl.BlockSpec((t.shape[0], ST_KB), lambda k, c: (0, k)) for t in tabs[:6]]
    state_spec = pl.BlockSpec((LB_KB, q, BLK), lambda k, c: (k, c, 0))
    state_shape = jax.ShapeDtypeStruct((N_LB, l_dim, BLK), F32)
    return pl.pallas_call(
        body, name="ssm_fwd", grid=(SSM_KB, nc),
        in_specs=[chan, wb_spec, wb_spec, wc_spec, wc_spec, pl.BlockSpec((1, BLK), lambda k, c: (0, k))] + tab_specs,
        out_specs=[chan, chan, state_spec, state_spec],
        out_shape=[jax.ShapeDtypeStruct((l_dim, D_MODEL), BF16), jax.ShapeDtypeStruct((l_dim, D_MODEL), BF16),
                   state_shape, state_shape],
        scratch_shapes=[pltpu.VMEM((8, ST_KB), F32), pltpu.VMEM((8, ST_KB), F32),
                        pltpu.VMEM((LB_KB, q, BLK), F32), pltpu.VMEM((LB_KB, q, BLK), F32)],
        compiler_params=_params(dimension_semantics=("parallel", "arbitrary")),
    )(u, wb_re, wb_im, wc_re, wc_im, d_skip, *tabs[:6])


def _ssm_bwd_kb(dz, y, u, s_re, s_im, wb_re, wb_im, wc_re, wc_im, d_skip, tabs, q, dproj):
    l_dim = u.shape[0]
    nc = l_dim // q
    g = q // GRP

    def body(dz_ref, y_ref, u_ref, sr_ref, si_ref, wbr_ref, wbi_ref, wcr_ref, wci_ref, d_ref,
             a1r_ref, a1i_ref, sqr_ref, sqi_ref, revr_ref, revi_ref, dproj_ref,
             du_ref, dd_ref, dar_ref, dai_ref, dwbr_ref, dwbi_ref, dwcr_ref, dwci_ref,
             cr_ref, ci_ref, gr_ref, gi_ref):
        @pl.when(pl.program_id(1) == 0)
        def _():
            for ref in (cr_ref, ci_ref, dd_ref, dar_ref, dai_ref, dwbr_ref, dwbi_ref, dwcr_ref, dwci_ref):
                ref[...] = jnp.zeros_like(ref)

        tn = (((0,), (0,)), ((), ()))
        nt = (((1,), (1,)), ((), ()))
        u_kb = u_ref[...]
        dy = dz_ref[...].astype(F32) * _gelu_grad(y_ref[...].astype(F32))
        dyb = dy.astype(BF16)
        ub = u_kb.astype(BF16)
        dd_ref[...] += jnp.sum(dy * u_kb, axis=0, keepdims=True)
        ds_r = lax.dot_general(dyb, wcr_ref[0], nt, preferred_element_type=F32)
        ds_i = -lax.dot_general(dyb, wci_ref[0], nt, preferred_element_type=F32)
        for i in range(LB_KB):
            gr_ref[i] = ds_r[:, BLK * i:BLK * i + BLK]
            gi_ref[i] = ds_i[:, BLK * i:BLK * i + BLK]
        row = lax.broadcasted_iota(jnp.int32, (g, BLK), 0)
        for i in range(LB_KB):
            lanes = slice(BLK * i, BLK * i + BLK)
            ar, ai = a1r_ref[0:1, lanes], -a1i_ref[0:1, lanes]
            cin_r, cin_i = cr_ref[0:1, lanes], ci_ref[0:1, lanes]
            tr, ti = _chunk_carries(gr_ref, gi_ref, i, ar, ai, sqr_ref, sqi_ref, revr_ref[:, lanes],
                                    -revi_ref[:, lanes], cin_r, cin_i, -1.0, reverse=True)
            cr_ref[0:1, lanes] = jnp.sum(jnp.where(row == 0, tr, 0.0), axis=0, keepdims=True)
            ci_ref[0:1, lanes] = jnp.sum(jnp.where(row == 0, ti, 0.0), axis=0, keepdims=True)
            nr = jnp.where(row == g - 1, cin_r, pltpu.roll(tr, g - 1, axis=0))
            ni = jnp.where(row == g - 1, cin_i, pltpu.roll(ti, g - 1, axis=0))
            acc_r = jnp.zeros((g, BLK), F32)
            acc_i = jnp.zeros((g, BLK), F32)
            for r in range(GRP - 1, -1, -1):
                s_r, s_i = _lane_slab(sr_ref, i, r, g), _lane_slab(si_ref, i, r, g)
                acc_r = acc_r + (nr * s_r + ni * s_i)
                acc_i = acc_i + (ni * s_r - nr * s_i)
                nr, ni = (_lane_slab(gr_ref, i, r, g) + ar * nr - ai * ni,
                          _lane_slab(gi_ref, i, r, g) + ar * ni + ai * nr)
                gr_ref[i, _strided(r, g), :] = nr
                gi_ref[i, _strided(r, g), :] = ni
            dar_ref[:, lanes] += jnp.sum(acc_r, axis=0, keepdims=True)
            dai_ref[:, lanes] += jnp.sum(acc_i, axis=0, keepdims=True)
        grb = jnp.concatenate([gr_ref[i] for i in range(LB_KB)], axis=1).astype(BF16)
        gib = jnp.concatenate([gi_ref[i] for i in range(LB_KB)], axis=1).astype(BF16)
        srb = jnp.concatenate([sr_ref[i] for i in range(LB_KB)], axis=1).astype(BF16)
        sib = jnp.concatenate([si_ref[i] for i in range(LB_KB)], axis=1).astype(BF16)
        du = (lax.dot_general(grb, wbr_ref[0], nt, preferred_element_type=F32)
              + lax.dot_general(gib, wbi_ref[0], nt, preferred_element_type=F32)
              + d_ref[...] * dy)
        du_ref[...] = du.astype(BF16)
        dwbr_ref[0] += lax.dot_general(ub, grb, tn, preferred_element_type=F32)
        dwbi_ref[0] += lax.dot_general(ub, gib, tn, preferred_element_type=F32)
        dwcr_ref[0] += lax.dot_general(srb, dyb, tn, preferred_element_type=F32)
        dwci_ref[0] -= lax.dot_general(sib, dyb, tn, preferred_element_type=F32)

    chan = pl.BlockSpec((q, BLK), lambda k, c: (nc - 1 - c, k))
    wb_spec = pl.BlockSpec((1, BLK, ST_KB), lambda k, c: (k, 0, 0))
    wc_spec = pl.BlockSpec((1, ST_KB, BLK), lambda k, c: (k, 0, 0))
    tab_in = [tabs[0], tabs[1], tabs[2], tabs[3], tabs[6], tabs[7]]
    tab_specs = [pl.BlockSpec((t.shape[0], ST_KB), lambda k, c: (0, k)) for t in tab_in]
    vec = pl.BlockSpec((1, BLK), lambda k, c: (0, k))
    svec = pl.BlockSpec((1, ST_KB), lambda k, c: (0, k))
    state_spec = pl.BlockSpec((LB_KB, q, BLK), lambda k, c: (k, nc - 1 - c, 0))
    du_spec = pl.BlockSpec((q, BLK), lambda k, c: (nc - 1 - c, 2 * D_MODEL // BLK + k))
    in_specs = ([chan, chan, chan, state_spec, state_spec, wb_spec, wb_spec, wc_spec, wc_spec, vec] + tab_specs
                + [pl.BlockSpec(memory_space=pl.ANY)])
    return pl.pallas_call(
        body, name="ssm_bwd", grid=(SSM_KB, nc),
        in_specs=in_specs,
        out_specs=[du_spec, vec, svec, svec, wb_spec, wb_spec, wc_spec, wc_spec],
        input_output_aliases={len(in_specs) - 1: 0},
        out_shape=[jax.ShapeDtypeStruct(dproj.shape, BF16), jax.ShapeDtypeStruct((1, D_MODEL), F32),
                   jax.ShapeDtypeStruct((1, N_STATE), F32), jax.ShapeDtypeStruct((1, N_STATE), F32),
                   jax.ShapeDtypeStruct((SSM_KB, BLK, ST_KB), F32), jax.ShapeDtypeStruct((SSM_KB, BLK, ST_KB), F32),
                   jax.ShapeDtypeStruct((SSM_KB, ST_KB, BLK), F32), jax.ShapeDtypeStruct((SSM_KB, ST_KB, BLK), F32)],
        scratch_shapes=[pltpu.VMEM((8, ST_KB), F32), pltpu.VMEM((8, ST_KB), F32),
                        pltpu.VMEM((LB_KB, q, BLK), F32), pltpu.VMEM((LB_KB, q, BLK), F32)],
        compiler_params=_params(dimension_semantics=("parallel", "arbitrary")),
    )(dz, y, u, s_re, s_im, wb_re, wb_im, wc_re, wc_im, d_skip, *tab_in, dproj)


def _discretize(lam_re, lam_im, log_dt, b_re, b_im):
    dt = jnp.exp(log_dt)[:, None]
    mag = jnp.exp(lam_re * dt)
    ar, ai = mag * jnp.cos(lam_im * dt), mag * jnp.sin(lam_im * dt)
    den = lam_re * lam_re + lam_im * lam_im
    nr, ni = ar - 1.0, ai
    fr, fi = (nr * lam_re + ni * lam_im) / den, (ni * lam_re - nr * lam_im) / den
    bbar_re = fr[..., None] * b_re - fi[..., None] * b_im
    bbar_im = fr[..., None] * b_im + fi[..., None] * b_re
    return ar, ai, bbar_re, bbar_im


def _block_diag_b(bbar):
    eye = jnp.eye(8, dtype=bbar.dtype)
    return jnp.einsum("kgpc,gh->kgchp", bbar.reshape(8, 8, SSM_STATE, SSM_GROUP_CH), eye).reshape(8, BLK, ST_KB)


def _block_diag_b_t(dwb):
    eye = jnp.eye(8, dtype=dwb.dtype)
    return jnp.einsum("kgchp,gh->kgpc", dwb.reshape(8, 8, SSM_GROUP_CH, 8, SSM_STATE), eye).reshape(
        SSM_GROUPS, SSM_STATE, SSM_GROUP_CH)


def _block_diag_c(c):
    eye = jnp.eye(8, dtype=c.dtype)
    return jnp.einsum("kgcp,gh->kgphc", c.reshape(8, 8, SSM_GROUP_CH, SSM_STATE), eye).reshape(8, ST_KB, BLK)


def _block_diag_c_t(dwc):
    eye = jnp.eye(8, dtype=dwc.dtype)
    return jnp.einsum("kgphc,gh->kgcp", dwc.reshape(8, 8, SSM_STATE, 8, SSM_GROUP_CH), eye).reshape(
        SSM_GROUPS, SSM_GROUP_CH, SSM_STATE)


def _powers(br, bi, n):
    pr, pi = br, bi
    cr, ci = br, bi
    while pr.shape[0] < n:
        pr, pi = (jnp.concatenate([pr, pr * cr - pi * ci], axis=0),
                  jnp.concatenate([pi, pr * ci + pi * cr], axis=0))
        cr, ci = cr * cr - ci * ci, 2.0 * cr * ci
    return pr[:n], pi[:n]


def _powers_desc(br, bi, n):
    pr, pi = br, bi
    cr, ci = br, bi
    while pr.shape[0] < n:
        pr, pi = (jnp.concatenate([pr * cr - pi * ci, pr], axis=0),
                  jnp.concatenate([pr * ci + pi * cr, pi], axis=0))
        cr, ci = cr * cr - ci * ci, 2.0 * cr * ci
    return pr, pi


def _power_tables(ar, ai, g):
    a1r, a1i = _powers(ar, ai, GRP)
    seqr, seqi = _powers(a1r[GRP - 1:], a1i[GRP - 1:], g)
    g2 = 1 << (g - 1).bit_length()
    revr, revi = _powers_desc(a1r[GRP - 1:], a1i[GRP - 1:], g2)
    revr, revi = revr[g2 - g:], revi[g2 - g:]
    sq_r, sq_i = [seqr[0:1]], [seqi[0:1]]
    while len(sq_r) < 8:
        r, i = sq_r[-1], sq_i[-1]
        sq_r.append(r * r - i * i)
        sq_i.append(2.0 * r * i)
    sqr, sqi = jnp.concatenate(sq_r, axis=0), jnp.concatenate(sq_i, axis=0)
    return a1r, a1i, sqr, sqi, seqr, seqi, revr, revi


HBM_SPEC = pl.BlockSpec(memory_space=pltpu.HBM)
SEM_SPEC = pl.BlockSpec(memory_space=pltpu.SEMAPHORE)
DATAFLOW = pltpu.SideEffectType.DATAFLOW_SIDE_EFFECTING


def _plain_rows(p, m):
    return p * m


def _ffn_in_rows(p, m):
    return ((p & 3) >> 1) * (4 * m) + (p >> 2) * (2 * m) + (p & 1) * m


def _peer_copies(src_refs, land_refs, send_sems, recv_sems, chunked, row_fns):
    x, y, c = lax.axis_index("x"), lax.axis_index("y"), lax.axis_index("c")
    me = 4 * x + 2 * y + c
    copies = []
    for a, (src, land) in enumerate(zip(src_refs, land_refs)):
        m = land.shape[0] // N_DEV
        for k in range(N_DEV - 1):
            rel = k + 1
            bx, by, bc = (rel >> 2) & 1, (rel >> 1) & 1, rel & 1
            peer = (x + bx - 2 * x * bx, y + by - 2 * y * by, c + bc - 2 * c * bc)
            p_idx = 4 * peer[0] + 2 * peer[1] + peer[2]
            copies.append(pltpu.make_async_remote_copy(
                src_ref=src.at[pl.ds(row_fns[a](p_idx, m), m), :] if chunked else src,
                dst_ref=land.at[pl.ds(me * m if chunked else row_fns[a](me, m), m), :],
                send_sem=send_sems[a * (N_DEV - 1) + k], recv_sem=recv_sems[a * (N_DEV - 1) + k],
                device_id=peer, device_id_type=MESH))
    return copies


def _send_start(name, srcs, lands, chunked, row_fns=None):
    n = len(srcs)
    ns = n * (N_DEV - 1)
    row_fns = row_fns or [_plain_rows] * n

    def body(*refs):
        src_refs, land_refs = refs[:n], refs[n:2 * n]
        send_sems, recv_sems = refs[2 * n:2 * n + ns], refs[2 * n + ns:2 * n + 2 * ns]
        token = refs[-1]
        for cp in _peer_copies(src_refs, land_refs, send_sems, recv_sems, chunked, row_fns):
            cp.start()
        token[...] = jnp.zeros_like(token)

    ins = [pltpu.with_memory_space_constraint(a, pltpu.HBM) for a in list(srcs) + list(lands)]
    out = pl.pallas_call(
        body, name=name,
        in_specs=[HBM_SPEC] * (2 * n),
        out_specs=[SEM_SPEC] * (2 * ns) + [HBM_SPEC] * (2 * n) + [pl.BlockSpec(memory_space=pltpu.VMEM)],
        out_shape=[pltpu.SemaphoreType.DMA(())] * (2 * ns)
        + [pltpu.HBM(a.shape, a.dtype) for a in list(srcs) + list(lands)]
        + [jax.ShapeDtypeStruct((8, BLK), F32)],
        input_output_aliases={i: i + 2 * ns for i in range(2 * n)},
        compiler_params=pltpu.CompilerParams(has_side_effects=DATAFLOW),
    )(*ins)
    return out[:ns], out[ns:2 * ns], out[2 * ns:2 * ns + n], out[2 * ns + n:2 * ns + 2 * n], out[-1]


def _send_wait(name, send_sems, recv_sems, srcs, lands, after, chunked, row_fns=None):
    n = len(srcs)
    ns = n * (N_DEV - 1)
    row_fns = row_fns or [_plain_rows] * n

    def body(*refs):
        src_refs, land_refs = refs[:n], refs[n:2 * n]
        s_sems, r_sems = refs[2 * n:2 * n + ns], refs[2 * n + ns:2 * n + 2 * ns]
        copies = _peer_copies(src_refs, land_refs, s_sems, r_sems, chunked, row_fns)
        for cp in copies:
            cp.wait_send()
        for cp in copies:
            cp.wait_recv()

    out = pl.pallas_call(
        body, name=name,
        in_specs=[HBM_SPEC] * (2 * n) + [SEM_SPEC] * (2 * ns) + [pl.BlockSpec(memory_space=pl.ANY)],
        out_specs=[HBM_SPEC] * (2 * n),
        out_shape=[pltpu.HBM(a.shape, a.dtype) for a in list(srcs) + list(lands)],
        input_output_aliases={i: i for i in range(2 * n)},
        compiler_params=pltpu.CompilerParams(has_side_effects=DATAFLOW),
    )(*srcs, *lands, *send_sems, *recv_sems, after)
    return out[n:]


def _sum_slots(name, recv, own):
    m, ncol = own.shape
    tr = m // 2 if (m // 2) % 16 == 0 else m
    g = m // tr

    def body(*refs):
        slots, own_ref, o_ref = refs[:N_DEV], refs[N_DEV], refs[N_DEV + 1]
        me = _my_index()
        tot = None
        for s in range(N_DEV):
            v = jnp.where(me == s, own_ref[...], slots[s][...].astype(F32))
            tot = v if tot is None else tot + v
        o_ref[...] = tot

    def slot_spec(s):
        return pl.BlockSpec((tr, ncol), lambda i: (s * g + i, 0))

    return pl.pallas_call(
        body, name=name, grid=(g,),
        in_specs=[slot_spec(s) for s in range(N_DEV)] + [pl.BlockSpec((tr, ncol), lambda i: (i, 0))],
        out_specs=pl.BlockSpec((tr, ncol), lambda i: (i, 0)),
        out_shape=jax.ShapeDtypeStruct((m, ncol), F32),
        compiler_params=_params(),
    )(*([recv] * N_DEV), own)


def _sum_gathered(name, gathered, rows):
    tr = _pick(rows, 512, 8)
    g = rows // tr

    def body(*refs):
        o_ref = refs[N_DEV]
        tot = refs[0][...]
        for s in range(1, N_DEV):
            tot = tot + refs[s][...]
        o_ref[...] = tot

    return pl.pallas_call(
        body, name=name, grid=(g,),
        in_specs=[pl.BlockSpec((tr, BLK), (lambda i, s=s: (s * g + i, 0))) for s in range(N_DEV)],
        out_specs=pl.BlockSpec((tr, BLK), lambda i: (i, 0)),
        out_shape=jax.ShapeDtypeStruct((rows, BLK), F32),
        compiler_params=_params(),
    )(*([gathered] * N_DEV))


def _adamw(name, w, g, m, v):
    r, c = w.shape
    tr = _pick(r, 256, 8) if r % 8 == 0 else r
    c1 = 1.0 - ADAM_B1 ** ADAM_STEP
    c2 = 1.0 - ADAM_B2 ** ADAM_STEP

    def body(w_ref, g_ref, m_ref, v_ref, d_ref, nm_ref, nv_ref):
        gv = g_ref[...]
        nm = ADAM_B1 * m_ref[...] + (1.0 - ADAM_B1) * gv
        nv = ADAM_B2 * v_ref[...] + (1.0 - ADAM_B2) * (gv * gv)
        m_hat = nm / c1
        v_hat = nv / c2
        d_ref[...] = -ADAM_LR * (m_hat / (jnp.sqrt(v_hat) + ADAM_EPS) + ADAM_WD * w_ref[...])
        nm_ref[...] = nm
        nv_ref[...] = nv

    spec = pl.BlockSpec((tr, c), lambda i: (i, 0))
    shape = jax.ShapeDtypeStruct((r, c), F32)
    return pl.pallas_call(
        body, name=name, grid=(r // tr,),
        in_specs=[spec] * 4, out_specs=[spec] * 3, out_shape=[shape] * 3,
        compiler_params=_params(),
    )(w, g, m, v)


def _adamw_many(name, ws, gs, ms, vs):
    n = len(ws)
    c1 = 1.0 - ADAM_B1 ** ADAM_STEP
    c2 = 1.0 - ADAM_B2 ** ADAM_STEP

    def body(*refs):
        for a in range(n):
            w_ref, g_ref, m_ref, v_ref = refs[a], refs[n + a], refs[2 * n + a], refs[3 * n + a]
            d_ref, nm_ref, nv_ref = refs[4 * n + a], refs[5 * n + a], refs[6 * n + a]
            gv = g_ref[...]
            nm = ADAM_B1 * m_ref[...] + (1.0 - ADAM_B1) * gv
            nv = ADAM_B2 * v_ref[...] + (1.0 - ADAM_B2) * (gv * gv)
            d_ref[...] = -ADAM_LR * ((nm / c1) / (jnp.sqrt(nv / c2) + ADAM_EPS) + ADAM_WD * w_ref[...])
            nm_ref[...] = nm
            nv_ref[...] = nv

    vmem = pl.BlockSpec(memory_space=pltpu.VMEM)
    shapes = [jax.ShapeDtypeStruct(w.shape, F32) for w in ws]
    out = pl.pallas_call(
        body, name=name,
        in_specs=[vmem] * (4 * n), out_specs=[vmem] * (3 * n), out_shape=shapes * 3,
        compiler_params=_params(),
    )(*ws, *gs, *ms, *vs)
    return out[:n], out[n:2 * n], out[2 * n:]


PACK_ROWS = 128


def _pack(parts):
    flat = []
    for p in parts:
        v = p.reshape(-1)
        flat.append(jnp.pad(v, (0, (-v.shape[0]) % BLK)))
    v = jnp.concatenate(flat)
    v = jnp.pad(v, (0, (-v.shape[0]) % (PACK_ROWS * BLK)))
    return v.reshape(-1, BLK)


def _unpack(buf, shapes):
    flat = buf.reshape(-1)
    out, off = [], 0
    for shp in shapes:
        size = math.prod(shp)
        out.append(flat[off:off + size].reshape(shp))
        off += size + (-size) % BLK
    return out


def kernel(x, meta_tokens, norm_mix, w_in, q_norm, k_norm, attn_sinks, lam_re, lam_im, log_dt, ssm_b_re, ssm_b_im, ssm_c_re, ssm_c_im, ssm_d, w_glu, attn_branch_norm, ssm_branch_norm, w_out, norm_ffn, w_ffn_in, w_ffn_out, loss_target, m_meta_tokens, m_norm_mix, m_w_in, m_q_norm, m_k_norm, m_attn_sinks, m_lam_re, m_lam_im, m_log_dt, m_ssm_b_re, m_ssm_b_im, m_ssm_c_re, m_ssm_c_im, m_ssm_d, m_w_glu, m_attn_branch_norm, m_ssm_branch_norm, m_w_out, m_norm_ffn, m_w_ffn_in, m_w_ffn_out, v_meta_tokens, v_norm_mix, v_w_in, v_q_norm, v_k_norm, v_attn_sinks, v_lam_re, v_lam_im, v_log_dt, v_ssm_b_re, v_ssm_b_im, v_ssm_c_re, v_ssm_c_im, v_ssm_d, v_w_glu, v_attn_branch_norm, v_ssm_branch_norm, v_w_out, v_norm_ffn, v_w_ffn_in, v_w_ffn_out):
    args = dict(locals())
    weights = {n: args[n] for n in WEIGHTS}
    mom_m = {n: args["m_" + n] for n in WEIGHTS}
    mom_v = {n: args["v_" + n] for n in WEIGHTS}

    x2d = x[0]
    target2d = loss_target[0]
    s_len = x2d.shape[0]
    l_dim = s_len + BLK
    tm_row = _pick(l_dim, 320)
    tm_mm = _pick(l_dim, 1040)
    tl_tn = _pick(l_dim, 2080)
    tm_ffn = _pick(l_dim, 640)
    tm_big = _pick(l_dim, 2080)
    tm_shift = _pick(l_dim, 640, BLK)

    shard_in = w_in[0].T.astype(BF16)
    shard_glu = w_glu[0].T.astype(BF16)
    shard_out = w_out[0].astype(BF16)
    shard_ffn_in = w_ffn_in[0].T.astype(BF16)
    shard_ffn_out = w_ffn_out[0].astype(BF16)
    shard_meta = meta_tokens.T
    me = _my_index()

    def landing(shard, row_fn=_plain_rows):
        m_rows, cols = shard.shape
        return lax.dynamic_update_slice(lax.empty((N_DEV * m_rows, cols), shard.dtype), shard,
                                        (row_fn(me, m_rows), 0))

    first = [shard_in, shard_meta]
    ga = _send_start("gather_start_a", first, [landing(s) for s in first], chunked=False)
    later = [shard_glu + ga[4][0:1, 0:1].astype(BF16), shard_out, shard_ffn_in, shard_ffn_out]
    later_fns = [_plain_rows, _plain_rows, _ffn_in_rows, _plain_rows]
    gb = _send_start("gather_start_b", later, [landing(s, f) for s, f in zip(later, later_fns)], chunked=False,
                     row_fns=later_fns)

    nm_t = norm_mix + (ga[4][0:1, 0:1] + gb[4][0:1, 0:1])
    qn_t, kn_t = jnp.tile(q_norm, (1, N_Q_HEADS)), jnp.tile(k_norm, (1, N_KV_HEADS))
    e_mat = jnp.kron(jnp.eye(4, dtype=F32), jnp.ones((HEAD_DIM, HEAD_DIM), F32)).astype(BF16)

    def disc(lr, li, ldt, br, bi):
        return _discretize(lr[0], li[0], ldt[0], br[0], bi[0])

    (abar_re, abar_im, bbar_re, bbar_im), disc_vjp = jax.vjp(disc, lam_re, lam_im, log_dt, ssm_b_re, ssm_b_im)
    wb_re, wb_im = _block_diag_b(bbar_re).astype(BF16), _block_diag_b(bbar_im).astype(BF16)
    wc_re, wc_im = _block_diag_c(ssm_c_re[0]).astype(BF16), _block_diag_c(ssm_c_im[0]).astype(BF16)
    q_ssm = _pick(l_dim, 640, 64)
    tabs = _power_tables(abar_re.reshape(1, N_STATE), abar_im.reshape(1, N_STATE), q_ssm // GRP)

    h0, xn = _embed_norm(x2d, nm_t, tm_shift)
    wt_in, meta_t = _send_wait("gather_wait_a", ga[0], ga[1], ga[2], ga[3], xn, chunked=False)
    meta_pad = jnp.pad(meta_t.T, ((PAD, 0), (0, 0)))
    h0, xn = _embed_meta(meta_pad, nm_t, h0, xn)
    qkv = _matmul("proj_qkv", xn, wt_in, nt=True, tm=tm_big, tn=512, tk=D_MODEL, n=Q_W + 2 * KV_W, w_off=0)
    u = _matmul("proj_u", xn, wt_in, nt=True, tm=tm_big, tn=512, tk=D_MODEL, n=D_MODEL, w_off=3)
    gates = _matmul("proj_gates", xn, wt_in, nt=True, tm=tm_big, tn=512, tk=D_MODEL, n=2 * D_MODEL, w_off=5,
                    out_dtype=BF16)
    qn, kf, vf = _qk_prep(qkv, qn_t, kn_t, e_mat, tm_row)
    attn, lse = _attn_fwd(qn, kf, vf, attn_sinks)
    y, z, s_re, s_im = _ssm_fwd_kb(u, wb_re, wb_im, wc_re, wc_im, ssm_d, tabs, q_ssm)
    wt_glu, w_out_f, wt_ffn_in, w_ffn_out_f = _send_wait("gather_wait_b", gb[0], gb[1], gb[2], gb[3], z,
                                                         chunked=False, row_fns=later_fns)
    zab = _matmul("glu_proj", z, wt_glu, nt=True, tm=tm_big, tn=1024, tk=D_MODEL, out_dtype=BF16)
    merged = _merge_fwd(attn, zab, gates, attn_branch_norm, ssm_branch_norm, tm_row)
    h1, hn = _matmul("out_proj", merged, w_out_f, nt=False, tm=tm_mm, tn=1024, tk=D_MODEL, res=h0,
                     norm_g=norm_ffn)
    gu, act = _ffn_in_swiglu(hn, wt_ffn_in, tm_ffn)
    h2 = _matmul("ffn_out", act, w_ffn_out_f, nt=False, tm=tm_mm, tn=1024, tk=D_FF, res=h1)
    dh2, dh2_b, loss_part = _loss_grad(h2, target2d, tm_shift)

    dgu = _d_act_swiglu(dh2_b, w_ffn_out_f, gu, tm_ffn)

    def exchange_start(name, grads_b, row_fns=None):
        return _send_start(name, grads_b, [jnp.zeros(g.shape, BF16) for g in grads_b], chunked=True,
                           row_fns=row_fns)

    g_ffn_out, g_ffn_out_b = _matmul_tn("g_ffn_out", act, dh2_b, tm=1408, tn=1024, tl=tl_tn)
    g_ffn_in_t, g_ffn_in_b = _matmul_tn("g_ffn_in", dgu, hn, tm=1408, tn=1024, tl=tl_tn)
    ffn_fns = [_ffn_in_rows, _plain_rows]
    ex1 = exchange_start("exchange_start_ffn", [g_ffn_in_b, g_ffn_out_b], ffn_fns)
    dhn = _matmul("d_hn", dgu, wt_ffn_in, nt=False, tm=tm_ffn, tn=1024, tk=2 * D_FF, out_dtype=BF16)
    dh1, dh1_b, g_norm_ffn = _norm_bwd_res("ffn_norm_bwd", h1, norm_ffn + ex1[4][0:1, 0:1], dhn, dh2, tm_row)
    dmerged = _matmul("d_merged", dh1_b, w_out_f, nt=True, tm=tm_big, tn=1024, tk=D_MODEL, out_dtype=BF16)
    dattn, dzab, dproj, g_abn, g_sbn = _merge_bwd(attn, zab, gates, attn_branch_norm, ssm_branch_norm,
                                                    dmerged, tm_row)
    g_out, g_out_b = _matmul_tn("g_out", merged, dh1_b, tm=1024, tn=1024, tl=tl_tn)
    dz = _matmul("d_z", dzab, wt_glu, nt=False, tm=tm_mm, tn=1024, tk=2 * D_MODEL, out_dtype=BF16)
    g_glu_t, g_glu_b = _matmul_tn("g_glu", dzab, z, tm=1024, tn=1024, tl=tl_tn)
    ex2 = exchange_start("exchange_start_mix", [g_glu_b, g_out_b])
    dproj, g_ssm_d, g_ar, g_ai, g_wbr, g_wbi, g_wcr, g_wci = _ssm_bwd_kb(
        dz, y, u, s_re, s_im, wb_re, wb_im, wc_re, wc_im, ssm_d + ex2[4][0:1, 0:1], tabs, q_ssm, dproj)
    dq, dkc, dkp, dkm, dvc, dvp, dvm, g_sinks = _attn_bwd(qn, kf, vf, attn_sinks, lse, attn, dattn)
    dproj, g_qn_t, g_kn_t = _qk_bwd(qkv, qn_t, kn_t, e_mat, dq, dkc, dkp, dkm, dvc, dvp, dvm, dproj)
    g_lam_re, g_lam_im, g_log_dt, g_b_re, g_b_im = disc_vjp(
        (g_ar.reshape(SSM_GROUPS, SSM_STATE), g_ai.reshape(SSM_GROUPS, SSM_STATE),
         _block_diag_b_t(g_wbr), _block_diag_b_t(g_wbi)))
    small_grads = {
        "q_norm": g_qn_t.reshape(N_Q_HEADS, HEAD_DIM).sum(0)[None],
        "k_norm": g_kn_t.reshape(N_KV_HEADS, HEAD_DIM).sum(0)[None], "attn_sinks": g_sinks,
        "lam_re": g_lam_re, "lam_im": g_lam_im, "log_dt": g_log_dt, "ssm_b_re": g_b_re, "ssm_b_im": g_b_im,
        "ssm_c_re": _block_diag_c_t(g_wcr)[None], "ssm_c_im": _block_diag_c_t(g_wci)[None],
        "ssm_d": g_ssm_d, "attn_branch_norm": g_abn, "ssm_branch_norm": g_sbn, "norm_ffn": g_norm_ffn,
    }
    early = [n for n in SMALL if n != "norm_mix"]
    packed_e = _pack([small_grads[n] for n in early])
    gs_e = _send_start("small_start_a", [packed_e], [landing(packed_e)], chunked=False)

    qkv_w, u_end = Q_W + 2 * KV_W, Q_W + 2 * KV_W + D_MODEL
    to_back = lambda a: jnp.concatenate([a[u_end:], a[qkv_w:u_end], a[:qkv_w]], axis=0)
    to_front = lambda a: jnp.concatenate([a[3 * D_MODEL:], a[2 * D_MODEL:3 * D_MODEL], a[:2 * D_MODEL]], axis=0)
    g_in_p, g_in_pb = _matmul_tn("g_in", dproj, xn, tm=1152, tn=1024, tl=tl_tn, after=gs_e[4])
    g_in_t, g_in_b = to_front(g_in_p), to_front(g_in_pb)
    ex3 = exchange_start("exchange_start_in", [g_in_b])
    dxn = _matmul("d_xn", dproj, to_back(wt_in), nt=False, tm=tm_ffn, tn=1024, tk=IN_COLS, out_dtype=BF16)
    grad_x2d, dmeta_blk, g_norm_mix = _final_bwd(h0, nm_t + ex3[4][0:1, 0:1], dxn, dh1, _pick(s_len, 512, BLK))
    packed_l = _pack([g_norm_mix, dmeta_blk[PAD:], loss_part])
    gs_l = _send_start("small_start_b", [packed_l], [landing(packed_l)], chunked=False)
    grads, deltas, new_m, new_v = {}, {}, {}, {}

    recv_ffn_in, recv_ffn_out = _send_wait("exchange_wait_ffn", ex1[0], ex1[1], ex1[2], ex1[3], gs_l[4],
                                           chunked=True, row_fns=ffn_fns)
    recv_glu, recv_out = _send_wait("exchange_wait_mix", ex2[0], ex2[1], ex2[2], ex2[3], recv_ffn_in,
                                    chunked=True)
    (recv_in,) = _send_wait("exchange_wait_in", ex3[0], ex3[1], ex3[2], ex3[3], recv_glu, chunked=True)
    big = [("w_in", g_in_t, True, recv_in, _plain_rows), ("w_glu", g_glu_t, True, recv_glu, _plain_rows),
           ("w_out", g_out, False, recv_out, _plain_rows), ("w_ffn_in", g_ffn_in_t, True, recv_ffn_in, _ffn_in_rows),
           ("w_ffn_out", g_ffn_out, False, recv_ffn_out, _plain_rows)]
    for name, g_full, transposed, recv, row_fn in big:
        m_rows = g_full.shape[0] // N_DEV
        own = lax.dynamic_slice(g_full, (row_fn(me, m_rows), 0), (m_rows, g_full.shape[1]))
        g_shard = _sum_slots("sum_" + name, recv, own)
        grads[name] = (g_shard.T if transposed else g_shard)[None]

    def adamw_2d(name):
        shp = weights[name].shape
        as2d = lambda a: a.reshape(shp[-2], shp[-1])
        d, nm, nv = _adamw("adamw_" + name, as2d(weights[name]), as2d(grads[name]), as2d(mom_m[name]),
                           as2d(mom_v[name]))
        deltas[name], new_m[name], new_v[name] = d.reshape(shp), nm.reshape(shp), nv.reshape(shp)
        return d

    for name in ["w_in", "w_glu", "w_out", "w_ffn_in", "w_ffn_out"]:
        last = adamw_2d(name)

    def small_sum(tag, gs, packed, after):
        (gathered,) = _send_wait("small_wait_" + tag, gs[0], gs[1], gs[2], gs[3], after, chunked=False)
        return _sum_gathered("sum_small_" + tag, gathered, packed.shape[0])

    def small_adamw(tag, names):
        d, nm, nv = _adamw_many("adamw_small_" + tag, [weights[n] for n in names], [grads[n] for n in names],
                                [mom_m[n] for n in names], [mom_v[n] for n in names])
        deltas.update(zip(names, d))
        new_m.update(zip(names, nm))
        new_v.update(zip(names, nv))

    g_sum_e = small_sum("a", gs_e, packed_e, last)
    grads.update(zip(early, _unpack(g_sum_e, [weights[n].shape for n in early])))
    wide = [n for n in early if n.startswith(("ssm_b", "ssm_c"))]
    small_adamw("wide", wide)
    g_sum_l = small_sum("b", gs_l, packed_l, g_sum_e)
    grads["norm_mix"], g_meta, loss_sum = _unpack(g_sum_l, [weights["norm_mix"].shape, (N_META, D_MODEL), (1, 1)])
    small_adamw("rest", [n for n in SMALL if n not in wide])
    grads["meta_tokens"] = lax.dynamic_slice(g_meta, (0, me * BLK), (N_META, BLK))
    adamw_2d("meta_tokens")

    loss = loss_sum[0, 0]
    return (loss, grad_x2d[None], *[grads[n] for n in WEIGHTS], *[deltas[n] for n in WEIGHTS],
            *[new_m[n] for n in WEIGHTS], *[new_v[n] for n in WEIGHTS])
```

```python
import math

import jax
import jax.numpy as jnp
from jax import lax
from jax.experimental import pallas as pl
from jax.experimental.pallas import tpu as pltpu

F32 = jnp.float32
BF16 = jnp.bfloat16

D_MODEL = 1024
N_META = 16
HEAD_DIM = 64
N_Q_HEADS = 16
N_KV_HEADS = 4
Q_W = N_Q_HEADS * HEAD_DIM
KV_W = N_KV_HEADS * HEAD_DIM
SSM_GROUPS = 64
SSM_GROUP_CH = 16
SSM_STATE = 64
N_STATE = SSM_GROUPS * SSM_STATE
D_FF = 2816
IN_COLS = Q_W + 2 * KV_W + 3 * D_MODEL
EPS = 1e-6
BLK = 128
PAD = BLK - N_META
N_DEV = 8
NEG = -1e30
SSM_KB = 8
ST_KB = N_STATE // SSM_KB
LB_KB = ST_KB // BLK
N_LB = N_STATE // BLK

ADAM_LR = 0.001
ADAM_B1 = 0.9
ADAM_B2 = 0.999
ADAM_EPS = 1e-08
ADAM_WD = 0.01
ADAM_STEP = 10

VMEM_LIMIT = 48 * 1024 * 1024
MESH = pl.DeviceIdType.MESH

SMALL = ["norm_mix", "q_norm", "k_norm", "attn_sinks", "lam_re", "lam_im", "log_dt", "ssm_b_re", "ssm_b_im",
         "ssm_c_re", "ssm_c_im", "ssm_d", "attn_branch_norm", "ssm_branch_norm", "norm_ffn"]
WEIGHTS = ["meta_tokens", "norm_mix", "w_in", "q_norm", "k_norm", "attn_sinks", "lam_re", "lam_im", "log_dt",
           "ssm_b_re", "ssm_b_im", "ssm_c_re", "ssm_c_im", "ssm_d", "w_glu", "attn_branch_norm",
           "ssm_branch_norm", "w_out", "norm_ffn", "w_ffn_in", "w_ffn_out"]


def _params(**kw):
    return pltpu.CompilerParams(vmem_limit_bytes=VMEM_LIMIT, **kw)


def _pick(n, cap, mult=16):
    best = None
    for d in range(mult, min(n, cap) + 1, mult):
        if n % d == 0:
            best = d
    assert best is not None, (n, cap, mult)
    return best


def _my_index():
    return 4 * lax.axis_index("x") + 2 * lax.axis_index("y") + lax.axis_index("c")


def _rms(x, g):
    r = lax.rsqrt(jnp.mean(x * x, axis=-1, keepdims=True) + EPS)
    return x * r * g


def _rms_bwd(x, g, dy):
    r = lax.rsqrt(jnp.mean(x * x, axis=-1, keepdims=True) + EPS)
    t = dy * g
    dx = r * t - x * (r * r * r) * jnp.mean(t * x, axis=-1, keepdims=True)
    dg = jnp.sum(dy * (x * r), axis=0, keepdims=True)
    return dx, dg


def _sigmoid(x):
    return jax.nn.sigmoid(x)


def _gelu(x):
    k = math.sqrt(2.0 / math.pi)
    return 0.5 * x * (1.0 + jnp.tanh(k * (x + 0.044715 * (x * x * x))))


def _gelu_grad(x):
    k = math.sqrt(2.0 / math.pi)
    t = jnp.tanh(k * (x + 0.044715 * (x * x * x)))
    return 0.5 * (1.0 + t) + 0.5 * x * (1.0 - t * t) * (k * (1.0 + 3.0 * 0.044715 * (x * x)))


def _head_mean(x, e_ref):
    hi = x.astype(BF16)
    lo = (x - hi.astype(F32)).astype(BF16)
    e = e_ref[...]
    out = []
    for b in range(x.shape[1] // 256):
        sl = slice(256 * b, 256 * b + 256)
        s = (jnp.dot(hi[:, sl], e, preferred_element_type=F32)
             + jnp.dot(lo[:, sl], e, preferred_element_type=F32))
        out.append(s)
    s = out[0] if len(out) == 1 else jnp.concatenate(out, axis=1)
    return s * (1.0 / HEAD_DIM)


def _head_rms(x, g, e_ref):
    r = lax.rsqrt(_head_mean(x * x, e_ref) + EPS)
    return x * r * g


def _head_rms_bwd(x, g, dy, e_ref):
    r = lax.rsqrt(_head_mean(x * x, e_ref) + EPS)
    t = dy * g
    dx = r * t - x * (r * r * r) * _head_mean(t * x, e_ref)
    dg = jnp.sum(dy * (x * r), axis=0, keepdims=True)
    return dx, dg


def _lane_half(shape):
    lane = lax.broadcasted_iota(jnp.int32, shape, len(shape) - 1)
    return (lane >> 6) & 1


def _matmul(name, a, w, *, nt, tm, tn, tk, n=None, w_off=0, res=None, norm_g=None, out_dtype=F32):
    m_dim, k_dim = a.shape
    n_dim = n if n is not None else (w.shape[0] if nt else w.shape[1])
    gm, gn, gk = m_dim // tm, n_dim // tn, k_dim // tk
    assert gm * tm == m_dim and gn * tn == n_dim and gk * tk == k_dim, (name, a.shape, w.shape, tm, tn, tk)
    assert norm_g is None or tn == n_dim
    direct = out_dtype == F32
    dn = (((1,), (1,)), ((), ())) if nt else (((1,), (0,)), ((), ()))

    def body(*refs):
        refs = list(refs)
        a_ref, w_ref = refs[0], refs[1]
        pos = 2
        r_ref = g_ref = on_ref = None
        if res is not None:
            r_ref, pos = refs[pos], pos + 1
        if norm_g is not None:
            g_ref, pos = refs[pos], pos + 1
        o_ref, pos = refs[pos], pos + 1
        if norm_g is not None:
            on_ref, pos = refs[pos], pos + 1
        part = lax.dot_general(a_ref[...], w_ref[...], dn, preferred_element_type=F32)
        if gk == 1:
            r = part if r_ref is None else r_ref[...] + part
            o_ref[...] = r.astype(out_dtype)
            if on_ref is not None:
                on_ref[...] = _rms(r, g_ref[...]).astype(BF16)
            return
        acc = o_ref if direct else refs[pos]
        k = pl.program_id(2)

        @pl.when(k == 0)
        def _():
            acc[...] = part if r_ref is None or not direct else r_ref[...] + part

        @pl.when(k > 0)
        def _():
            acc[...] += part

        @pl.when(k == gk - 1)
        def _():
            if not direct:
                r = acc[...]
                if r_ref is not None:
                    r = r_ref[...] + r
                o_ref[...] = r.astype(out_dtype)
            if on_ref is not None:
                on_ref[...] = _rms(o_ref[...].astype(F32), g_ref[...]).astype(BF16)

    if nt:
        w_spec = pl.BlockSpec((tn, tk), lambda i, j, k: (j + w_off, k))
    else:
        w_spec = pl.BlockSpec((tk, tn), lambda i, j, k: (k, j))
    in_specs = [pl.BlockSpec((tm, tk), lambda i, j, k: (i, k)), w_spec]
    args = [a, w]
    out_spec = pl.BlockSpec((tm, tn), lambda i, j, k: (i, j))
    out_specs, out_shape = [out_spec], [jax.ShapeDtypeStruct((m_dim, n_dim), out_dtype)]
    if res is not None:
        in_specs.append(out_spec)
        args.append(res)
    if norm_g is not None:
        in_specs.append(pl.BlockSpec((1, tn), lambda i, j, k: (0, 0)))
        args.append(norm_g)
        out_specs.append(out_spec)
        out_shape.append(jax.ShapeDtypeStruct((m_dim, n_dim), BF16))
    out = pl.pallas_call(
        body, name=name, grid=(gm, gn, gk),
        in_specs=in_specs, out_specs=out_specs, out_shape=out_shape,
        scratch_shapes=[] if direct or gk == 1 else [pltpu.VMEM((tm, tn), F32)],
        compiler_params=_params(dimension_semantics=("parallel", "parallel", "arbitrary")),
    )(*args)
    return out if norm_g is not None else out[0]


def _matmul_tn(name, a, b, *, tm, tn, tl, after=None):
    l_dim, m_dim = a.shape
    n_dim = b.shape[1]
    gm, gn, gl = m_dim // tm, n_dim // tn, l_dim // tl
    assert gm * tm == m_dim and gn * tn == n_dim and gl * tl == l_dim, (name, a.shape, b.shape, tm, tn, tl)

    def body(*refs):
        a_ref, b_ref = refs[0], refs[1]
        o_ref, ob_ref = refs[-2], refs[-1]

        @pl.when(pl.program_id(2) == 0)
        def _():
            o_ref[...] = jnp.zeros_like(o_ref)

        o_ref[...] += lax.dot_general(a_ref[...], b_ref[...], (((0,), (0,)), ((), ())),
                                      preferred_element_type=F32)

        @pl.when(pl.program_id(2) == gl - 1)
        def _():
            ob_ref[...] = o_ref[...].astype(BF16)

    out_spec = pl.BlockSpec((tm, tn), lambda i, j, l: (i, j))
    in_specs = [pl.BlockSpec((tl, tm), lambda i, j, l: (l, i)), pl.BlockSpec((tl, tn), lambda i, j, l: (l, j))]
    args = [a, b]
    if after is not None:
        in_specs.append(pl.BlockSpec(memory_space=pl.ANY))
        args.append(after)
    return pl.pallas_call(
        body, name=name, grid=(gm, gn, gl),
        in_specs=in_specs,
        out_specs=[out_spec, out_spec],
        out_shape=[jax.ShapeDtypeStruct((m_dim, n_dim), F32), jax.ShapeDtypeStruct((m_dim, n_dim), BF16)],
        compiler_params=_params(dimension_semantics=("parallel", "parallel", "arbitrary")),
    )(*args)


def _row_spec(tm, cols, f=None):
    if f is None:
        return pl.BlockSpec((tm, cols), lambda i: (i, 0))
    return pl.BlockSpec((tm, cols), lambda i: (f(i), 0))


def _full_spec(shape):
    nd = len(shape)
    return pl.BlockSpec(shape, lambda i: (0,) * nd)


def _shifted_specs(n_sub, n_blocks):
    return [_row_spec(BLK, D_MODEL, (lambda i, k=k: jnp.clip(n_sub * i - 1 + k, 0, n_blocks - 1)))
            for k in range(n_sub)]


def _embed_norm(x2d, g, tm):
    s_len = x2d.shape[0]
    l_dim = s_len + BLK
    n_sub = tm // BLK

    def body(*refs):
        x_refs, g_ref, h_ref, xn_ref = refs[:n_sub], refs[n_sub], refs[n_sub + 1], refs[n_sub + 2]
        i = pl.program_id(0)
        for k in range(n_sub):
            rows = slice(BLK * k, BLK * k + BLK)
            h = x_refs[k][...] * jnp.where(n_sub * i + k >= 1, 1.0, 0.0)
            h_ref[rows, :] = h
            xn_ref[rows, :] = _rms(h, g_ref[...]).astype(BF16)

    return pl.pallas_call(
        body, name="embed_norm", grid=(l_dim // tm,),
        in_specs=_shifted_specs(n_sub, s_len // BLK) + [_full_spec((1, D_MODEL))],
        out_specs=[_row_spec(tm, D_MODEL), _row_spec(tm, D_MODEL)],
        out_shape=[jax.ShapeDtypeStruct((l_dim, D_MODEL), F32),
                   jax.ShapeDtypeStruct((l_dim, D_MODEL), BF16)],
        compiler_params=_params(),
    )(*([x2d] * n_sub), g)


def _embed_meta(meta_pad, g, h0, xn):
    def body(mp_ref, g_ref, h_in, xn_in, h_ref, xn_ref):
        h_ref[...] = mp_ref[...]
        xn_ref[...] = _rms(mp_ref[...], g_ref[...]).astype(BF16)

    any_spec = pl.BlockSpec(memory_space=pl.ANY)
    return pl.pallas_call(
        body, name="embed_meta", grid=(1,),
        in_specs=[_full_spec((BLK, D_MODEL)), _full_spec((1, D_MODEL)), any_spec, any_spec],
        out_specs=[_row_spec(BLK, D_MODEL), _row_spec(BLK, D_MODEL)],
        out_shape=[jax.ShapeDtypeStruct(h0.shape, F32), jax.ShapeDtypeStruct(xn.shape, BF16)],
        input_output_aliases={2: 0, 3: 1},
        compiler_params=_params(),
    )(meta_pad, g, h0, xn)


KVX_W = 2 * N_KV_HEADS * BLK


def _qk_prep(qkv, q_norm_t, k_norm_t, e_mat, tm):
    l_dim = qkv.shape[0]

    def body(x_ref, qg_ref, kg_ref, e_ref, q_ref, kf_ref, vf_ref):
        x = x_ref[...]
        q = _head_rms(x[:, :Q_W], qg_ref[...], e_ref) * (HEAD_DIM ** -0.5)
        q_ref[...] = q.astype(BF16)
        k = _head_rms(x[:, Q_W:Q_W + KV_W], kg_ref[...], e_ref)
        v = x[:, Q_W + KV_W:Q_W + 2 * KV_W]
        half = _lane_half((tm, BLK))
        for src, dst in ((k, kf_ref), (v, vf_ref)):
            for kv in range(N_KV_HEADS):
                blk = src[:, BLK * (kv // 2):BLK * (kv // 2) + BLK]
                swapped = pltpu.roll(blk, HEAD_DIM, axis=1)
                for e in range(2):
                    val = blk if kv % 2 == e else swapped
                    idx = 2 * kv + e
                    dst[:, BLK * idx:BLK * idx + BLK] = jnp.where(half == e, val, 0.0).astype(BF16)

    return pl.pallas_call(
        body, name="qk_prep", grid=(l_dim // tm,),
        in_specs=[_row_spec(tm, Q_W + 2 * KV_W), _full_spec((1, Q_W)), _full_spec((1, KV_W)),
                  _full_spec((256, 256))],
        out_specs=[_row_spec(tm, Q_W), _row_spec(tm, KVX_W), _row_spec(tm, KVX_W)],
        out_shape=[jax.ShapeDtypeStruct((l_dim, Q_W), BF16),
                   jax.ShapeDtypeStruct((l_dim, KVX_W), BF16),
                   jax.ShapeDtypeStruct((l_dim, KVX_W), BF16)],
        compiler_params=_params(),
    )(qkv, q_norm_t, k_norm_t, e_mat)


def _merge_fwd(attn, zab, gates, abn, sbn, tm):
    l_dim = attn.shape[0]

    def body(a_ref, z_ref, g_ref, an_ref, sn_ref, o_ref):
        z = z_ref[...].astype(F32)
        g = g_ref[...].astype(F32)
        ssm = z[:, :D_MODEL] * _sigmoid(z[:, D_MODEL:])
        merged = (_sigmoid(g[:, :D_MODEL]) * _rms(a_ref[...], an_ref[...])
                  + _sigmoid(g[:, D_MODEL:]) * _rms(ssm, sn_ref[...]))
        o_ref[...] = merged.astype(BF16)

    return pl.pallas_call(
        body, name="merge_fwd", grid=(l_dim // tm,),
        in_specs=[_row_spec(tm, D_MODEL), _row_spec(tm, 2 * D_MODEL), _row_spec(tm, 2 * D_MODEL),
                  _full_spec((1, D_MODEL)), _full_spec((1, D_MODEL))],
        out_specs=_row_spec(tm, D_MODEL),
        out_shape=jax.ShapeDtypeStruct((l_dim, D_MODEL), BF16),
        compiler_params=_params(),
    )(attn, zab, gates, abn, sbn)


def _merge_bwd(attn, zab, gates, abn, sbn, dmerged, tm):
    l_dim = attn.shape[0]

    def body(a_ref, z_ref, g_ref, an_ref, sn_ref, dm_ref, da_ref, dz_ref, dg_ref, dan_ref, dsn_ref):
        @pl.when(pl.program_id(0) == 0)
        def _():
            dan_ref[...] = jnp.zeros_like(dan_ref)
            dsn_ref[...] = jnp.zeros_like(dsn_ref)

        z = z_ref[...].astype(F32)
        g = g_ref[...].astype(F32)
        dm = dm_ref[...].astype(F32)
        attn_v = a_ref[...]
        za, zb = z[:, :D_MODEL], z[:, D_MODEL:]
        sb = _sigmoid(zb)
        ssm = za * sb
        s_ga, s_gs = _sigmoid(g[:, :D_MODEL]), _sigmoid(g[:, D_MODEL:])
        a_n = _rms(attn_v, an_ref[...])
        s_n = _rms(ssm, sn_ref[...])
        dg_ref[:, :D_MODEL] = (dm * a_n * s_ga * (1.0 - s_ga)).astype(BF16)
        dg_ref[:, D_MODEL:] = (dm * s_n * s_gs * (1.0 - s_gs)).astype(BF16)
        dattn, dan = _rms_bwd(attn_v, an_ref[...], dm * s_ga)
        dssm, dsn = _rms_bwd(ssm, sn_ref[...], dm * s_gs)
        da_ref[...] = dattn.astype(BF16)
        dz_ref[:, :D_MODEL] = (dssm * sb).astype(BF16)
        dz_ref[:, D_MODEL:] = (dssm * za * sb * (1.0 - sb)).astype(BF16)
        dan_ref[...] += dan
        dsn_ref[...] += dsn

    return pl.pallas_call(
        body, name="merge_bwd", grid=(l_dim // tm,),
        in_specs=[_row_spec(tm, D_MODEL), _row_spec(tm, 2 * D_MODEL), _row_spec(tm, 2 * D_MODEL),
                  _full_spec((1, D_MODEL)), _full_spec((1, D_MODEL)), _row_spec(tm, D_MODEL)],
        out_specs=[_row_spec(tm, D_MODEL), _row_spec(tm, 2 * D_MODEL), _row_spec(tm, 2 * D_MODEL),
                   _full_spec((1, D_MODEL)), _full_spec((1, D_MODEL))],
        out_shape=[jax.ShapeDtypeStruct((l_dim, D_MODEL), BF16),
                   jax.ShapeDtypeStruct((l_dim, 2 * D_MODEL), BF16),
                   jax.ShapeDtypeStruct((l_dim, IN_COLS), BF16),
                   jax.ShapeDtypeStruct((1, D_MODEL), F32), jax.ShapeDtypeStruct((1, D_MODEL), F32)],
        compiler_params=_params(),
    )(attn, zab, gates, abn, sbn, dmerged)


FF_TILE = D_FF // 2


def _ffn_in_swiglu(hn, wt_ffn_in, tm):
    l_dim = hn.shape[0]
    nt = (((1,), (1,)), ((), ()))

    def body(a_ref, w_ref, gu_ref, act_ref):
        r = lax.dot_general(a_ref[...], w_ref[...], nt, preferred_element_type=F32)
        gate, up = r[:, :FF_TILE], r[:, FF_TILE:]
        gu_ref[...] = r.astype(BF16)
        act_ref[...] = (gate * _sigmoid(gate) * up).astype(BF16)

    return pl.pallas_call(
        body, name="ffn_in_swiglu", grid=(l_dim // tm, 2),
        in_specs=[pl.BlockSpec((tm, D_MODEL), lambda i, j: (i, 0)),
                  pl.BlockSpec((2 * FF_TILE, D_MODEL), lambda i, j: (j, 0))],
        out_specs=[pl.BlockSpec((tm, 2 * FF_TILE), lambda i, j: (i, j)),
                   pl.BlockSpec((tm, FF_TILE), lambda i, j: (i, j))],
        out_shape=[jax.ShapeDtypeStruct((l_dim, 2 * D_FF), BF16), jax.ShapeDtypeStruct((l_dim, D_FF), BF16)],
        compiler_params=_params(dimension_semantics=("parallel", "parallel")),
    )(hn, wt_ffn_in)


def _d_act_swiglu(dh2_b, w_ffn_out, gu, tm):
    l_dim = dh2_b.shape[0]
    nt = (((1,), (1,)), ((), ()))

    def body(d_ref, w_ref, gu_ref, o_ref):
        d = lax.dot_general(d_ref[...], w_ref[...], nt, preferred_element_type=F32)
        gate = gu_ref[:, :FF_TILE].astype(F32)
        up = gu_ref[:, FF_TILE:].astype(F32)
        s = _sigmoid(gate)
        o_ref[:, :FF_TILE] = (d * up * (s * (1.0 + gate * (1.0 - s)))).astype(BF16)
        o_ref[:, FF_TILE:] = (d * (gate * s)).astype(BF16)

    return pl.pallas_call(
        body, name="d_act_swiglu", grid=(l_dim // tm, 2),
        in_specs=[pl.BlockSpec((tm, D_MODEL), lambda i, j: (i, 0)),
                  pl.BlockSpec((FF_TILE, D_MODEL), lambda i, j: (j, 0)),
                  pl.BlockSpec((tm, 2 * FF_TILE), lambda i, j: (i, j))],
        out_specs=pl.BlockSpec((tm, 2 * FF_TILE), lambda i, j: (i, j)),
        out_shape=jax.ShapeDtypeStruct((l_dim, 2 * D_FF), BF16),
        compiler_params=_params(dimension_semantics=("parallel", "parallel")),
    )(dh2_b, w_ffn_out, gu)


def _loss_grad(h2, target2d, tm):
    l_dim = h2.shape[0]
    n_sub = tm // BLK

    def body(*refs):
        h_ref, t_refs = refs[0], refs[1:1 + n_sub]
        d_ref, db_ref, loss_ref = refs[1 + n_sub:]
        i = pl.program_id(0)

        @pl.when(i == 0)
        def _():
            loss_ref[...] = jnp.zeros_like(loss_ref)

        for k in range(n_sub):
            rows = slice(BLK * k, BLK * k + BLK)
            real = jnp.where(n_sub * i + k >= 1, 1.0, 0.0)
            err = (h_ref[rows, :] - t_refs[k][...]) * real
            d = err * (1.0 / D_MODEL)
            d_ref[rows, :] = d
            db_ref[rows, :] = d.astype(BF16)
            loss_ref[...] += 0.5 * jnp.sum(jnp.mean(err * err, axis=-1, keepdims=True), axis=0, keepdims=True)

    return pl.pallas_call(
        body, name="loss_grad", grid=(l_dim // tm,),
        in_specs=[_row_spec(tm, D_MODEL)] + _shifted_specs(n_sub, target2d.shape[0] // BLK),
        out_specs=[_row_spec(tm, D_MODEL), _row_spec(tm, D_MODEL), _full_spec((1, 1))],
        out_shape=[jax.ShapeDtypeStruct((l_dim, D_MODEL), F32), jax.ShapeDtypeStruct((l_dim, D_MODEL), BF16),
                   jax.ShapeDtypeStruct((1, 1), F32)],
        compiler_params=_params(),
    )(h2, *([target2d] * n_sub))


def _norm_bwd_res(name, h, g, dy, dres, tm):
    l_dim = h.shape[0]

    def body(h_ref, g_ref, dy_ref, dr_ref, o_ref, ob_ref, dg_ref):
        @pl.when(pl.program_id(0) == 0)
        def _():
            dg_ref[...] = jnp.zeros_like(dg_ref)

        dx, dg = _rms_bwd(h_ref[...], g_ref[...], dy_ref[...].astype(F32))
        out = dr_ref[...] + dx
        o_ref[...] = out
        ob_ref[...] = out.astype(BF16)
        dg_ref[...] += dg

    return pl.pallas_call(
        body, name=name, grid=(l_dim // tm,),
        in_specs=[_row_spec(tm, D_MODEL), _full_spec((1, D_MODEL)), _row_spec(tm, D_MODEL), _row_spec(tm, D_MODEL)],
        out_specs=[_row_spec(tm, D_MODEL), _row_spec(tm, D_MODEL), _full_spec((1, D_MODEL))],
        out_shape=[jax.ShapeDtypeStruct((l_dim, D_MODEL), F32), jax.ShapeDtypeStruct((l_dim, D_MODEL), BF16),
                   jax.ShapeDtypeStruct((1, D_MODEL), F32)],
        compiler_params=_params(),
    )(h, g, dy, dres)


def _final_bwd(h0, g, dxn, dh1, tm):
    l_dim = h0.shape[0]
    n_sub = tm // BLK
    n_tiles = (l_dim - BLK) // tm

    def sub_specs():
        return [_row_spec(BLK, D_MODEL, (lambda j, k=k: jnp.where(j < n_tiles, n_sub * j + 1 + k, 0)))
                for k in range(n_sub)]

    def body(*refs):
        h_refs, g_ref = refs[:n_sub], refs[n_sub]
        dy_refs, dr_refs = refs[n_sub + 1:2 * n_sub + 1], refs[2 * n_sub + 1:3 * n_sub + 1]
        gx_ref, gm_ref, dg_ref = refs[3 * n_sub + 1:]
        j = pl.program_id(0)

        @pl.when(j == 0)
        def _():
            dg_ref[...] = jnp.zeros_like(dg_ref)

        def block(k):
            dx, dg = _rms_bwd(h_refs[k][...], g_ref[...], dy_refs[k][...].astype(F32))
            dg_ref[...] += dg
            return dr_refs[k][...] + dx

        @pl.when(j < n_tiles)
        def _():
            for k in range(n_sub):
                gx_ref[BLK * k:BLK * k + BLK, :] = block(k)

        @pl.when(j == n_tiles)
        def _():
            gm_ref[...] = block(0)

    return pl.pallas_call(
        body, name="final_bwd", grid=(n_tiles + 1,),
        in_specs=sub_specs() + [_full_spec((1, D_MODEL))] + sub_specs() + sub_specs(),
        out_specs=[_row_spec(tm, D_MODEL, lambda j: jnp.minimum(j, n_tiles - 1)), _full_spec((BLK, D_MODEL)),
                   _full_spec((1, D_MODEL))],
        out_shape=[jax.ShapeDtypeStruct((l_dim - BLK, D_MODEL), F32), jax.ShapeDtypeStruct((BLK, D_MODEL), F32),
                   jax.ShapeDtypeStruct((1, D_MODEL), F32)],
        compiler_params=_params(),
    )(*([h0] * n_sub), g, *([dxn] * n_sub), *([dh1] * n_sub))


def _attn_valid(n):
    shape = (2 * BLK, 3 * BLK)
    qi = lax.broadcasted_iota(jnp.int32, shape, 0) & (BLK - 1)
    col = lax.broadcasted_iota(jnp.int32, shape, 1)
    kj = col & (BLK - 1)
    part = col >> 7
    nn = jnp.zeros(shape, jnp.int32) + n
    meta_ok = (part == 0) & (kj >= PAD) & (nn >= 1)
    prev_ok = (part == 1) & (kj > qi) & (nn >= 2)
    cur_ok = (part == 2) & (kj <= qi) & ((nn >= 1) | (kj >= PAD))
    return meta_ok | prev_ok | cur_ok


def _attn_scores(q_ref, kwin, sk_ref, valid, kv, e):
    qs = jnp.concatenate([q_ref[:, BLK * (2 * kv):BLK * (2 * kv) + BLK],
                          q_ref[:, BLK * (2 * kv + 1):BLK * (2 * kv + 1) + BLK]], axis=0)
    s = lax.dot_general(qs, kwin, (((1,), (1,)), ((), ())), preferred_element_type=F32)
    h0 = 4 * kv + e
    row = lax.broadcasted_iota(jnp.int32, (2 * BLK, 1), 0)
    sink = jnp.where(row < BLK, sk_ref[:, h0:h0 + 1], sk_ref[:, h0 + 2:h0 + 3])
    return qs, jnp.where(valid, s, NEG), sink


def _attn_specs(nb):
    prev = lambda i: jnp.maximum(i - 1, 0)
    zero = lambda i: 0
    kv_specs = [_row_spec(BLK, KVX_W, zero), _row_spec(BLK, KVX_W, prev), _row_spec(BLK, KVX_W)]
    return kv_specs


def _attn_fwd(qn, kf, vf, sinks):
    l_dim = qn.shape[0]
    nb = l_dim // BLK

    def body(q_ref, km_ref, kp_ref, kc_ref, vm_ref, vp_ref, vc_ref, sk_ref, o_ref, lse_ref):
        valid = _attn_valid(pl.program_id(0))
        lane = lax.broadcasted_iota(jnp.int32, (BLK, BLK), 1)
        lse_all = jnp.zeros((BLK, BLK), F32)
        for kv in range(N_KV_HEADS):
            outs = []
            for e in range(2):
                sl = slice(BLK * (2 * kv + e), BLK * (2 * kv + e) + BLK)
                kwin = jnp.concatenate([km_ref[:, sl], kp_ref[:, sl], kc_ref[:, sl]], axis=0)
                vwin = jnp.concatenate([vm_ref[:, sl], vp_ref[:, sl], vc_ref[:, sl]], axis=0)
                _, s, sink = _attn_scores(q_ref, kwin, sk_ref, valid, kv, e)
                m = jnp.maximum(jnp.max(s, axis=-1, keepdims=True), sink)
                ex = jnp.exp(s - m)
                den = jnp.sum(ex, axis=-1, keepdims=True) + jnp.exp(sink - m)
                p = ex * (1.0 / den)
                outs.append(jnp.dot(p.astype(BF16), vwin, preferred_element_type=F32))
                lse = m + jnp.log(den)
                lse_all = jnp.where(lane == 4 * kv + e, lse[:BLK], lse_all)
                lse_all = jnp.where(lane == 4 * kv + 2 + e, lse[BLK:], lse_all)
            o = outs[0] + outs[1]
            o_ref[:, BLK * (2 * kv):BLK * (2 * kv) + BLK] = o[:BLK]
            o_ref[:, BLK * (2 * kv + 1):BLK * (2 * kv + 1) + BLK] = o[BLK:]
        lse_ref[...] = lse_all

    kv_specs = _attn_specs(nb)
    return pl.pallas_call(
        body, name="attn_fwd", grid=(nb,),
        in_specs=[_row_spec(BLK, Q_W)] + kv_specs + kv_specs + [_full_spec((1, N_Q_HEADS))],
        out_specs=[_row_spec(BLK, Q_W), _row_spec(BLK, BLK)],
        out_shape=[jax.ShapeDtypeStruct((l_dim, Q_W), F32), jax.ShapeDtypeStruct((l_dim, BLK), F32)],
        compiler_params=_params(),
    )(qn, kf, kf, kf, vf, vf, vf, sinks)


def _attn_bwd(qn, kf, vf, sinks, lse, attn, dattn):
    l_dim = qn.shape[0]
    nb = l_dim // BLK
    wide = KVX_W
    tn = (((0,), (0,)), ((), ()))
    nt = (((1,), (1,)), ((), ()))

    def body(q_ref, km_ref, kp_ref, kc_ref, vm_ref, vp_ref, vc_ref, sk_ref, lse_ref, o_ref, do_ref,
             dq_ref, dkc_ref, dkp_ref, dkm_ref, dvc_ref, dvp_ref, dvm_ref, dsk_ref):
        @pl.when(pl.program_id(0) == 0)
        def _():
            dkm_ref[...] = jnp.zeros_like(dkm_ref)
            dvm_ref[...] = jnp.zeros_like(dvm_ref)
            dsk_ref[...] = jnp.zeros_like(dsk_ref)

        valid = _attn_valid(pl.program_id(0))
        half = _lane_half((BLK, BLK))
        half2 = _lane_half((2 * BLK, BLK))
        lane16 = lax.broadcasted_iota(jnp.int32, (1, N_Q_HEADS), 1)
        dsk = jnp.zeros((1, N_Q_HEADS), F32)
        for kv in range(N_KV_HEADS):
            j0, j1 = 2 * kv, 2 * kv + 1
            do0 = do_ref[:, BLK * j0:BLK * j0 + BLK]
            do1 = do_ref[:, BLK * j1:BLK * j1 + BLK]
            do0f, do1f = do0.astype(F32), do1.astype(F32)
            prod0 = do0f * o_ref[:, BLK * j0:BLK * j0 + BLK]
            prod1 = do1f * o_ref[:, BLK * j1:BLK * j1 + BLK]
            dos = jnp.concatenate([do0, do1], axis=0)
            dqs = []
            for e in range(2):
                sl = slice(BLK * (2 * kv + e), BLK * (2 * kv + e) + BLK)
                kwin = jnp.concatenate([km_ref[:, sl], kp_ref[:, sl], kc_ref[:, sl]], axis=0)
                vwin = jnp.concatenate([vm_ref[:, sl], vp_ref[:, sl], vc_ref[:, sl]], axis=0)
                qs, s, sink = _attn_scores(q_ref, kwin, sk_ref, valid, kv, e)
                h0 = 4 * kv + e
                lse_rows = jnp.concatenate([lse_ref[:, h0:h0 + 1], lse_ref[:, h0 + 2:h0 + 3]], axis=0)
                p = jnp.exp(s - lse_rows)
                p_sink = jnp.exp(sink - lse_rows)
                delta = jnp.concatenate(
                    [jnp.sum(jnp.where(half == e, prod0, 0.0), axis=-1, keepdims=True),
                     jnp.sum(jnp.where(half == e, prod1, 0.0), axis=-1, keepdims=True)], axis=0)
                dp = lax.dot_general(dos, vwin, nt, preferred_element_type=F32)
                ds = (p * (dp - delta)).astype(BF16)
                pb = p.astype(BF16)
                dqs.append(jnp.dot(ds, kwin, preferred_element_type=F32))
                dk = lax.dot_general(ds, qs, tn, preferred_element_type=F32)
                dv = lax.dot_general(pb, dos, tn, preferred_element_type=F32)
                dkm_ref[:, sl] += dk[:BLK]
                dkp_ref[:, sl] = dk[BLK:2 * BLK].astype(BF16)
                dkc_ref[:, sl] = dk[2 * BLK:].astype(BF16)
                dvm_ref[:, sl] += dv[:BLK]
                dvp_ref[:, sl] = dv[BLK:2 * BLK].astype(BF16)
                dvc_ref[:, sl] = dv[2 * BLK:].astype(BF16)
                sink_g = -(p_sink * delta)
                g_lo = jnp.sum(sink_g[:BLK], axis=0, keepdims=True)
                g_hi = jnp.sum(sink_g[BLK:], axis=0, keepdims=True)
                dsk = dsk + jnp.where(lane16 == h0, g_lo, 0.0) + jnp.where(lane16 == h0 + 2, g_hi, 0.0)
            dq = jnp.where(half2 == 0, dqs[0], dqs[1])
            dq_ref[:, BLK * j0:BLK * j0 + BLK] = dq[:BLK].astype(BF16)
            dq_ref[:, BLK * j1:BLK * j1 + BLK] = dq[BLK:].astype(BF16)
        dsk_ref[...] += dsk

    kv_specs = _attn_specs(nb)
    row_wide = _row_spec(BLK, wide)
    acc_wide = _full_spec((BLK, wide))
    big = jax.ShapeDtypeStruct((l_dim, wide), BF16)
    return pl.pallas_call(
        body, name="attn_bwd", grid=(nb,),
        in_specs=[_row_spec(BLK, Q_W)] + kv_specs + kv_specs
        + [_full_spec((1, N_Q_HEADS)), _row_spec(BLK, BLK), _row_spec(BLK, Q_W), _row_spec(BLK, Q_W)],
        out_specs=[_row_spec(BLK, Q_W), row_wide, row_wide, acc_wide, row_wide, row_wide, acc_wide,
                   _full_spec((1, N_Q_HEADS))],
        out_shape=[jax.ShapeDtypeStruct((l_dim, Q_W), BF16), big, big, jax.ShapeDtypeStruct((BLK, wide), F32),
                   big, big, jax.ShapeDtypeStruct((BLK, wide), F32), jax.ShapeDtypeStruct((1, N_Q_HEADS), F32)],
        compiler_params=_params(),
    )(qn, kf, kf, kf, vf, vf, vf, sinks, lse, attn, dattn)


def _qk_bwd(qkv, q_norm_t, k_norm_t, e_mat, dq, dkc, dkp, dkm, dvc, dvp, dvm, dproj):
    l_dim = qkv.shape[0]
    nb = l_dim // BLK
    wide = KVX_W

    def fold(x):
        half = _lane_half((BLK, BLK))
        blocks = []
        for kb in range(2):
            t = []
            for kv in (2 * kb, 2 * kb + 1):
                own = kv % 2
                a = x[:, BLK * (2 * kv + own):BLK * (2 * kv + own) + BLK]
                b = pltpu.roll(x[:, BLK * (2 * kv + 1 - own):BLK * (2 * kv + 1 - own) + BLK], HEAD_DIM, axis=1)
                t.append(a + b)
            blocks.append(jnp.where(half == 0, t[0], t[1]))
        return jnp.concatenate(blocks, axis=1)

    def body(x_ref, qg_ref, kg_ref, e_ref, dq_ref, dkc_ref, dkp_ref, dkm_ref, dvc_ref, dvp_ref, dvm_ref,
             dproj_ref, o_ref, dqg_ref, dkg_ref):
        i = pl.program_id(0)

        @pl.when(i == 0)
        def _():
            dqg_ref[...] = jnp.zeros_like(dqg_ref)
            dkg_ref[...] = jnp.zeros_like(dkg_ref)

        first = jnp.where(i == 0, 1.0, 0.0)
        not_last = jnp.where(i < nb - 1, 1.0, 0.0)
        dk_x = dkc_ref[...].astype(F32) + not_last * dkp_ref[...].astype(F32) + first * dkm_ref[...]
        dv_x = dvc_ref[...].astype(F32) + not_last * dvp_ref[...].astype(F32) + first * dvm_ref[...]
        x = x_ref[...]
        dqx, dqg = _head_rms_bwd(x[:, :Q_W], qg_ref[...], dq_ref[...].astype(F32) * (HEAD_DIM ** -0.5), e_ref)
        dkx, dkg = _head_rms_bwd(x[:, Q_W:Q_W + KV_W], kg_ref[...], fold(dk_x), e_ref)
        o_ref[:, :Q_W] = dqx.astype(BF16)
        o_ref[:, Q_W:Q_W + KV_W] = dkx.astype(BF16)
        o_ref[:, Q_W + KV_W:] = fold(dv_x).astype(BF16)
        dqg_ref[...] += dqg
        dkg_ref[...] += dkg

    nxt = lambda i: jnp.minimum(i + 1, nb - 1)
    row_wide = _row_spec(BLK, wide)
    nxt_wide = _row_spec(BLK, wide, nxt)
    acc_wide = _full_spec((BLK, wide))
    qkv_w = Q_W + 2 * KV_W
    in_specs = [_row_spec(BLK, qkv_w), _full_spec((1, Q_W)), _full_spec((1, KV_W)),
                _full_spec((256, 256)), _row_spec(BLK, Q_W),
                row_wide, nxt_wide, acc_wide, row_wide, nxt_wide, acc_wide, pl.BlockSpec(memory_space=pl.ANY)]
    return pl.pallas_call(
        body, name="qk_bwd", grid=(nb,),
        in_specs=in_specs,
        out_specs=[pl.BlockSpec((BLK, qkv_w), lambda i: (i, 3 * D_MODEL // qkv_w)),
                   _full_spec((1, Q_W)), _full_spec((1, KV_W))],
        input_output_aliases={len(in_specs) - 1: 0},
        out_shape=[jax.ShapeDtypeStruct(dproj.shape, BF16),
                   jax.ShapeDtypeStruct((1, Q_W), F32), jax.ShapeDtypeStruct((1, KV_W), F32)],
        compiler_params=_params(),
    )(qkv, q_norm_t, k_norm_t, e_mat, dq, dkc, dkp, dkm, dvc, dvp, dvm, dproj)


GRP = 8


def _strided(r, g):
    return pl.ds(r, g, stride=GRP)


def _lane_slab(ref, i, r, g):
    return ref[i, _strided(r, g), :]


def _chunk_carries(xr_ref, xi_ref, i, ar, ai, sqr_ref, sqi_ref, seq_r, seq_i, cin_r, cin_i, sign, reverse):
    g = xr_ref.shape[1] // GRP
    sr = si = None
    for r in (range(GRP - 1, -1, -1) if reverse else range(GRP)):
        xr, xi = _lane_slab(xr_ref, i, r, g), _lane_slab(xi_ref, i, r, g)
        if sr is not None:
            xr, xi = xr + ar * sr - ai * si, xi + ar * si + ai * sr
        sr, si = xr, xi
    row = lax.broadcasted_iota(jnp.int32, sr.shape, 0)
    idx, s = 0, 1
    while s < g:
        br = sqr_ref[idx:idx + 1, BLK * i:BLK * i + BLK]
        bi = sign * sqi_ref[idx:idx + 1, BLK * i:BLK * i + BLK]
        shift, keep = (g - s, row < g - s) if reverse else (s, row >= s)
        pr = jnp.where(keep, pltpu.roll(sr, shift, axis=0), 0.0)
        pi = jnp.where(keep, pltpu.roll(si, shift, axis=0), 0.0)
        sr, si = sr + br * pr - bi * pi, si + br * pi + bi * pr
        idx, s = idx + 1, 2 * s
    return sr + seq_r * cin_r - seq_i * cin_i, si + seq_r * cin_i + seq_i * cin_r


def _ssm_fwd_kb(u, wb_re, wb_im, wc_re, wc_im, d_skip, tabs, q):
    l_dim = u.shape[0]
    nc = l_dim // q
    g = q // GRP

    def body(u_ref, wbr_ref, wbi_ref, wcr_ref, wci_ref, d_ref, a1r_ref, a1i_ref, sqr_ref, sqi_ref,
             seqr_ref, seqi_ref, y_ref, z_ref, sr_ref, si_ref, cr_ref, ci_ref, xr_ref, xi_ref):
        @pl.when(pl.program_id(1) == 0)
        def _():
            cr_ref[...] = jnp.zeros_like(cr_ref)
            ci_ref[...] = jnp.zeros_like(ci_ref)

        u_kb = u_ref[...]
        ub = u_kb.astype(BF16)
        xr = jnp.dot(ub, wbr_ref[0], preferred_element_type=F32)
        xi = jnp.dot(ub, wbi_ref[0], preferred_element_type=F32)
        for i in range(LB_KB):
            xr_ref[i] = xr[:, BLK * i:BLK * i + BLK]
            xi_ref[i] = xi[:, BLK * i:BLK * i + BLK]
        row = lax.broadcasted_iota(jnp.int32, (g, BLK), 0)
        for i in range(LB_KB):
            lanes = slice(BLK * i, BLK * i + BLK)
            ar, ai = a1r_ref[0:1, lanes], a1i_ref[0:1, lanes]
            cin_r, cin_i = cr_ref[0:1, lanes], ci_ref[0:1, lanes]
            tr, ti = _chunk_carries(xr_ref, xi_ref, i, ar, ai, sqr_ref, sqi_ref, seqr_ref[:, lanes],
                                    seqi_ref[:, lanes], cin_r, cin_i, 1.0, reverse=False)
            cr_ref[0:1, lanes] = jnp.sum(jnp.where(row == g - 1, tr, 0.0), axis=0, keepdims=True)
            ci_ref[0:1, lanes] = jnp.sum(jnp.where(row == g - 1, ti, 0.0), axis=0, keepdims=True)
            pr = jnp.where(row == 0, cin_r, pltpu.roll(tr, 1, axis=0))
            pi = jnp.where(row == 0, cin_i, pltpu.roll(ti, 1, axis=0))
            for r in range(GRP):
                pr, pi = (_lane_slab(xr_ref, i, r, g) + ar * pr - ai * pi,
                          _lane_slab(xi_ref, i, r, g) + ar * pi + ai * pr)
                sr_ref[i, _strided(r, g), :] = pr
                si_ref[i, _strided(r, g), :] = pi
        s_r = jnp.concatenate([sr_ref[i] for i in range(LB_KB)], axis=1)
        s_i = jnp.concatenate([si_ref[i] for i in range(LB_KB)], axis=1)
        y = (jnp.dot(s_r.astype(BF16), wcr_ref[0], preferred_element_type=F32)
             - jnp.dot(s_i.astype(BF16), wci_ref[0], preferred_element_type=F32)
             + d_ref[...] * u_kb)
        y_ref[...] = y.astype(BF16)
        z_ref[...] = _gelu(y).astype(BF16)

    chan = pl.BlockSpec((q, BLK), lambda k, c: (c, k))
    wb_spec = pl.BlockSpec((1, BLK, ST_KB), lambda k, c: (k, 0, 0))
    wc_spec = pl.BlockSpec((1, ST_KB, BLK), lambda k, c: (k, 0, 0))
    tab_specs = [pl.BlockSpec((t.shape[0], ST_KB), lambda k, c: (0, k)) for t in tabs[:6]]
    state_spec = pl.BlockSpec((LB_KB, q, BLK), lambda k, c: (k, c, 0))
    state_shape = jax.ShapeDtypeStruct((N_LB, l_dim, BLK), F32)
    return pl.pallas_call(
        body, name="ssm_fwd", grid=(SSM_KB, nc),
        in_specs=[chan, wb_spec, wb_spec, wc_spec, wc_spec, pl.BlockSpec((1, BLK), lambda k, c: (0, k))] + tab_specs,
        out_specs=[chan, chan, state_spec, state_spec],
        out_shape=[jax.ShapeDtypeStruct((l_dim, D_MODEL), BF16), jax.ShapeDtypeStruct((l_dim, D_MODEL), BF16),
                   state_shape, state_shape],
        scratch_shapes=[pltpu.VMEM((8, ST_KB), F32), pltpu.VMEM((8, ST_KB), F32),
                        pltpu.VMEM((LB_KB, q, BLK), F32), pltpu.VMEM((LB_KB, q, BLK), F32)],
        compiler_params=_params(dimension_semantics=("parallel", "arbitrary")),
    )(u, wb_re, wb_im, wc_re, wc_im, d_skip, *tabs[:6])


def _ssm_bwd_kb(dz, y, u, s_re, s_im, wb_re, wb_im, wc_re, wc_im, d_skip, tabs, q, dproj):
    l_dim = u.shape[0]
    nc = l_dim // q
    g = q // GRP

    def body(dz_ref, y_ref, u_ref, sr_ref, si_ref, wbr_ref, wbi_ref, wcr_ref, wci_ref, d_ref,
             a1r_ref, a1i_ref, sqr_ref, sqi_ref, revr_ref, revi_ref, dproj_ref,
             du_ref, dd_ref, dar_ref, dai_ref, dwbr_ref, dwbi_ref, dwcr_ref, dwci_ref,
             cr_ref, ci_ref, gr_ref, gi_ref):
        @pl.when(pl.program_id(1) == 0)
        def _():
            for ref in (cr_ref, ci_ref, dd_ref, dar_ref, dai_ref, dwbr_ref, dwbi_ref, dwcr_ref, dwci_ref):
                ref[...] = jnp.zeros_like(ref)

        tn = (((0,), (0,)), ((), ()))
        nt = (((1,), (1,)), ((), ()))
        u_kb = u_ref[...]
        dy = dz_ref[...].astype(F32) * _gelu_grad(y_ref[...].astype(F32))
        dyb = dy.astype(BF16)
        ub = u_kb.astype(BF16)
        dd_ref[...] += jnp.sum(dy * u_kb, axis=0, keepdims=True)
        ds_r = lax.dot_general(dyb, wcr_ref[0], nt, preferred_element_type=F32)
        ds_i = -lax.dot_general(dyb, wci_ref[0], nt, preferred_element_type=F32)
        for i in range(LB_KB):
            gr_ref[i] = ds_r[:, BLK * i:BLK * i + BLK]
            gi_ref[i] = ds_i[:, BLK * i:BLK * i + BLK]
        row = lax.broadcasted_iota(jnp.int32, (g, BLK), 0)
        for i in range(LB_KB):
            lanes = slice(BLK * i, BLK * i + BLK)
            ar, ai = a1r_ref[0:1, lanes], -a1i_ref[0:1, lanes]
            cin_r, cin_i = cr_ref[0:1, lanes], ci_ref[0:1, lanes]
            tr, ti = _chunk_carries(gr_ref, gi_ref, i, ar, ai, sqr_ref, sqi_ref, revr_ref[:, lanes],
                                    -revi_ref[:, lanes], cin_r, cin_i, -1.0, reverse=True)
            cr_ref[0:1, lanes] = jnp.sum(jnp.where(row == 0, tr, 0.0), axis=0, keepdims=True)
            ci_ref[0:1, lanes] = jnp.sum(jnp.where(row == 0, ti, 0.0), axis=0, keepdims=True)
            nr = jnp.where(row == g - 1, cin_r, pltpu.roll(tr, g - 1, axis=0))
            ni = jnp.where(row == g - 1, cin_i, pltpu.roll(ti, g - 1, axis=0))
            acc_r = jnp.zeros((g, BLK), F32)
            acc_i = jnp.zeros((g, BLK), F32)
            for r in range(GRP - 1, -1, -1):
                s_r, s_i = _lane_slab(sr_ref, i, r, g), _lane_slab(si_ref, i, r, g)
                acc_r = acc_r + (nr * s_r + ni * s_i)
                acc_i = acc_i + (ni * s_r - nr * s_i)
                nr, ni = (_lane_slab(gr_ref, i, r, g) + ar * nr - ai * ni,
                          _lane_slab(gi_ref, i, r, g) + ar * ni + ai * nr)
                gr_ref[i, _strided(r, g), :] = nr
                gi_ref[i, _strided(r, g), :] = ni
            dar_ref[:, lanes] += jnp.sum(acc_r, axis=0, keepdims=True)
            dai_ref[:, lanes] += jnp.sum(acc_i, axis=0, keepdims=True)
        grb = jnp.concatenate([gr_ref[i] for i in range(LB_KB)], axis=1).astype(BF16)
        gib = jnp.concatenate([gi_ref[i] for i in range(LB_KB)], axis=1).astype(BF16)
        srb = jnp.concatenate([sr_ref[i] for i in range(LB_KB)], axis=1).astype(BF16)
        sib = jnp.concatenate([si_ref[i] for i in range(LB_KB)], axis=1).astype(BF16)
        du = (lax.dot_general(grb, wbr_ref[0], nt, preferred_element_type=F32)
              + lax.dot_general(gib, wbi_ref[0], nt, preferred_element_type=F32)
              + d_ref[...] * dy)
        du_ref[...] = du.astype(BF16)
        dwbr_ref[0] += lax.dot_general(ub, grb, tn, preferred_element_type=F32)
        dwbi_ref[0] += lax.dot_general(ub, gib, tn, preferred_element_type=F32)
        dwcr_ref[0] += lax.dot_general(srb, dyb, tn, preferred_element_type=F32)
        dwci_ref[0] -= lax.dot_general(sib, dyb, tn, preferred_element_type=F32)

    chan = pl.BlockSpec((q, BLK), lambda k, c: (nc - 1 - c, k))
    wb_spec = pl.BlockSpec((1, BLK, ST_KB), lambda k, c: (k, 0, 0))
    wc_spec = pl.BlockSpec((1, ST_KB, BLK), lambda k, c: (k, 0, 0))
    tab_in = [tabs[0], tabs[1], tabs[2], tabs[3], tabs[6], tabs[7]]
    tab_specs = [pl.BlockSpec((t.shape[0], ST_KB), lambda k, c: (0, k)) for t in tab_in]
    vec = pl.BlockSpec((1, BLK), lambda k, c: (0, k))
    svec = pl.BlockSpec((1, ST_KB), lambda k, c: (0, k))
    state_spec = pl.BlockSpec((LB_KB, q, BLK), lambda k, c: (k, nc - 1 - c, 0))
    du_spec = pl.BlockSpec((q, BLK), lambda k, c: (nc - 1 - c, 2 * D_MODEL // BLK + k))
    in_specs = ([chan, chan, chan, state_spec, state_spec, wb_spec, wb_spec, wc_spec, wc_spec, vec] + tab_specs
                + [pl.BlockSpec(memory_space=pl.ANY)])
    return pl.pallas_call(
        body, name="ssm_bwd", grid=(SSM_KB, nc),
        in_specs=in_specs,
        out_specs=[du_spec, vec, svec, svec, wb_spec, wb_spec, wc_spec, wc_spec],
        input_output_aliases={len(in_specs) - 1: 0},
        out_shape=[jax.ShapeDtypeStruct(dproj.shape, BF16), jax.ShapeDtypeStruct((1, D_MODEL), F32),
                   jax.ShapeDtypeStruct((1, N_STATE), F32), jax.ShapeDtypeStruct((1, N_STATE), F32),
                   jax.ShapeDtypeStruct((SSM_KB, BLK, ST_KB), F32), jax.ShapeDtypeStruct((SSM_KB, BLK, ST_KB), F32),
                   jax.ShapeDtypeStruct((SSM_KB, ST_KB, BLK), F32), jax.ShapeDtypeStruct((SSM_KB, ST_KB, BLK), F32)],
        scratch_shapes=[pltpu.VMEM((8, ST_KB), F32), pltpu.VMEM((8, ST_KB), F32),
                        pltpu.VMEM((LB_KB, q, BLK), F32), pltpu.VMEM((LB_KB, q, BLK), F32)],
        compiler_params=_params(dimension_semantics=("parallel", "arbitrary")),
    )(dz, y, u, s_re, s_im, wb_re, wb_im, wc_re, wc_im, d_skip, *tab_in, dproj)


def _discretize(lam_re, lam_im, log_dt, b_re, b_im):
    dt = jnp.exp(log_dt)[:, None]
    mag = jnp.exp(lam_re * dt)
    ar, ai = mag * jnp.cos(lam_im * dt), mag * jnp.sin(lam_im * dt)
    den = lam_re * lam_re + lam_im * lam_im
    nr, ni = ar - 1.0, ai
    fr, fi = (nr * lam_re + ni * lam_im) / den, (ni * lam_re - nr * lam_im) / den
    bbar_re = fr[..., None] * b_re - fi[..., None] * b_im
    bbar_im = fr[..., None] * b_im + fi[..., None] * b_re
    return ar, ai, bbar_re, bbar_im


def _block_diag_b(bbar):
    eye = jnp.eye(8, dtype=bbar.dtype)
    return jnp.einsum("kgpc,gh->kgchp", bbar.reshape(8, 8, SSM_STATE, SSM_GROUP_CH), eye).reshape(8, BLK, ST_KB)


def _block_diag_b_t(dwb):
    eye = jnp.eye(8, dtype=dwb.dtype)
    return jnp.einsum("kgchp,gh->kgpc", dwb.reshape(8, 8, SSM_GROUP_CH, 8, SSM_STATE), eye).reshape(
        SSM_GROUPS, SSM_STATE, SSM_GROUP_CH)


def _block_diag_c(c):
    eye = jnp.eye(8, dtype=c.dtype)
    return jnp.einsum("kgcp,gh->kgphc", c.reshape(8, 8, SSM_GROUP_CH, SSM_STATE), eye).reshape(8, ST_KB, BLK)


def _block_diag_c_t(dwc):
    eye = jnp.eye(8, dtype=dwc.dtype)
    return jnp.einsum("kgphc,gh->kgcp", dwc.reshape(8, 8, SSM_STATE, 8, SSM_GROUP_CH), eye).reshape(
        SSM_GROUPS, SSM_GROUP_CH, SSM_STATE)


def _powers(br, bi, n):
    pr, pi = br, bi
    cr, ci = br, bi
    while pr.shape[0] < n:
        pr, pi = (jnp.concatenate([pr, pr * cr - pi * ci], axis=0),
                  jnp.concatenate([pi, pr * ci + pi * cr], axis=0))
        cr, ci = cr * cr - ci * ci, 2.0 * cr * ci
    return pr[:n], pi[:n]


def _powers_desc(br, bi, n):
    pr, pi = br, bi
    cr, ci = br, bi
    while pr.shape[0] < n:
        pr, pi = (jnp.concatenate([pr * cr - pi * ci, pr], axis=0),
                  jnp.concatenate([pr * ci + pi * cr, pi], axis=0))
        cr, ci = cr * cr - ci * ci, 2.0 * cr * ci
    return pr, pi


def _power_tables(ar, ai, g):
    a1r, a1i = _powers(ar, ai, GRP)
    seqr, seqi = _powers(a1r[GRP - 1:], a1i[GRP - 1:], g)
    g2 = 1 << (g - 1).bit_length()
    revr, revi = _powers_desc(a1r[GRP - 1:], a1i[GRP - 1:], g2)
    revr, revi = revr[g2 - g:], revi[g2 - g:]
    sq_r, sq_i = [seqr[0:1]], [seqi[0:1]]
    while len(sq_r) < 8:
        r, i = sq_r[-1], sq_i[-1]
        sq_r.append(r * r - i * i)
        sq_i.append(2.0 * r * i)
    sqr, sqi = jnp.concatenate(sq_r, axis=0), jnp.concatenate(sq_i, axis=0)
    return a1r, a1i, sqr, sqi, seqr, seqi, revr, revi


HBM_SPEC = pl.BlockSpec(memory_space=pltpu.HBM)
SEM_SPEC = pl.BlockSpec(memory_space=pltpu.SEMAPHORE)
DATAFLOW = pltpu.SideEffectType.DATAFLOW_SIDE_EFFECTING


def _plain_rows(p, m):
    return p * m


def _ffn_in_rows(p, m):
    return ((p & 3) >> 1) * (4 * m) + (p >> 2) * (2 * m) + (p & 1) * m


def _peer_copies(src_refs, land_refs, send_sems, recv_sems, chunked, row_fns):
    x, y, c = lax.axis_index("x"), lax.axis_index("y"), lax.axis_index("c")
    me = 4 * x + 2 * y + c
    copies = []
    for a, (src, land) in enumerate(zip(src_refs, land_refs)):
        m = land.shape[0] // N_DEV
        for k in range(N_DEV - 1):
            rel = k + 1
            bx, by, bc = (rel >> 2) & 1, (rel >> 1) & 1, rel & 1
            peer = (x + bx - 2 * x * bx, y + by - 2 * y * by, c + bc - 2 * c * bc)
            p_idx = 4 * peer[0] + 2 * peer[1] + peer[2]
            copies.append(pltpu.make_async_remote_copy(
                src_ref=src.at[pl.ds(row_fns[a](p_idx, m), m), :] if chunked else src,
                dst_ref=land.at[pl.ds(me * m if chunked else row_fns[a](me, m), m), :],
                send_sem=send_sems[a * (N_DEV - 1) + k], recv_sem=recv_sems[a * (N_DEV - 1) + k],
                device_id=peer, device_id_type=MESH))
    return copies


def _send_start(name, srcs, lands, chunked, row_fns=None):
    n = len(srcs)
    ns = n * (N_DEV - 1)
    row_fns = row_fns or [_plain_rows] * n

    def body(*refs):
        src_refs, land_refs = refs[:n], refs[n:2 * n]
        send_sems, recv_sems = refs[2 * n:2 * n + ns], refs[2 * n + ns:2 * n + 2 * ns]
        token = refs[-1]
        for cp in _peer_copies(src_refs, land_refs, send_sems, recv_sems, chunked, row_fns):
            cp.start()
        token[...] = jnp.zeros_like(token)

    ins = [pltpu.with_memory_space_constraint(a, pltpu.HBM) for a in list(srcs) + list(lands)]
    out = pl.pallas_call(
        body, name=name,
        in_specs=[HBM_SPEC] * (2 * n),
        out_specs=[SEM_SPEC] * (2 * ns) + [HBM_SPEC] * (2 * n) + [pl.BlockSpec(memory_space=pltpu.VMEM)],
        out_shape=[pltpu.SemaphoreType.DMA(())] * (2 * ns)
        + [pltpu.HBM(a.shape, a.dtype) for a in list(srcs) + list(lands)]
        + [jax.ShapeDtypeStruct((8, BLK), F32)],
        input_output_aliases={i: i + 2 * ns for i in range(2 * n)},
        compiler_params=pltpu.CompilerParams(has_side_effects=DATAFLOW),
    )(*ins)
    return out[:ns], out[ns:2 * ns], out[2 * ns:2 * ns + n], out[2 * ns + n:2 * ns + 2 * n], out[-1]


def _send_wait(name, send_sems, recv_sems, srcs, lands, after, chunked, row_fns=None):
    n = len(srcs)
    ns = n * (N_DEV - 1)
    row_fns = row_fns or [_plain_rows] * n

    def body(*refs):
        src_refs, land_refs = refs[:n], refs[n:2 * n]
        s_sems, r_sems = refs[2 * n:2 * n + ns], refs[2 * n + ns:2 * n + 2 * ns]
        copies = _peer_copies(src_refs, land_refs, s_sems, r_sems, chunked, row_fns)
        for cp in copies:
            cp.wait_send()
        for cp in copies:
            cp.wait_recv()

    out = pl.pallas_call(
        body, name=name,
        in_specs=[HBM_SPEC] * (2 * n) + [SEM_SPEC] * (2 * ns) + [pl.BlockSpec(memory_space=pl.ANY)],
        out_specs=[HBM_SPEC] * (2 * n),
        out_shape=[pltpu.HBM(a.shape, a.dtype) for a in list(srcs) + list(lands)],
        input_output_aliases={i: i for i in range(2 * n)},
        compiler_params=pltpu.CompilerParams(has_side_effects=DATAFLOW),
    )(*srcs, *lands, *send_sems, *recv_sems, after)
    return out[n:]


def _sum_slots(name, recv, own):
    m, ncol = own.shape
    tr = m // 2 if (m // 2) % 16 == 0 else m
    g = m // tr

    def body(*refs):
        slots, own_ref, o_ref = refs[:N_DEV], refs[N_DEV], refs[N_DEV + 1]
        me = _my_index()
        tot = None
        for s in range(N_DEV):
            v = jnp.where(me == s, own_ref[...], slots[s][...].astype(F32))
            tot = v if tot is None else tot + v
        o_ref[...] = tot

    def slot_spec(s):
        return pl.BlockSpec((tr, ncol), lambda i: (s * g + i, 0))

    return pl.pallas_call(
        body, name=name, grid=(g,),
        in_specs=[slot_spec(s) for s in range(N_DEV)] + [pl.BlockSpec((tr, ncol), lambda i: (i, 0))],
        out_specs=pl.BlockSpec((tr, ncol), lambda i: (i, 0)),
        out_shape=jax.ShapeDtypeStruct((m, ncol), F32),
        compiler_params=_params(),
    )(*([recv] * N_DEV), own)


def _sum_gathered(name, gathered, rows):
    tr = _pick(rows, 512, 8)
    g = rows // tr

    def body(*refs):
        o_ref = refs[N_DEV]
        tot = refs[0][...]
        for s in range(1, N_DEV):
            tot = tot + refs[s][...]
        o_ref[...] = tot

    return pl.pallas_call(
        body, name=name, grid=(g,),
        in_specs=[pl.BlockSpec((tr, BLK), (lambda i, s=s: (s * g + i, 0))) for s in range(N_DEV)],
        out_specs=pl.BlockSpec((tr, BLK), lambda i: (i, 0)),
        out_shape=jax.ShapeDtypeStruct((rows, BLK), F32),
        compiler_params=_params(),
    )(*([gathered] * N_DEV))


def _adamw(name, w, g, m, v):
    r, c = w.shape
    tr = _pick(r, 256, 8) if r % 8 == 0 else r
    c1 = 1.0 - ADAM_B1 ** ADAM_STEP
    c2 = 1.0 - ADAM_B2 ** ADAM_STEP

    def body(w_ref, g_ref, m_ref, v_ref, d_ref, nm_ref, nv_ref):
        gv = g_ref[...]
        nm = ADAM_B1 * m_ref[...] + (1.0 - ADAM_B1) * gv
        nv = ADAM_B2 * v_ref[...] + (1.0 - ADAM_B2) * (gv * gv)
        m_hat = nm / c1
        v_hat = nv / c2
        d_ref[...] = -ADAM_LR * (m_hat / (jnp.sqrt(v_hat) + ADAM_EPS) + ADAM_WD * w_ref[...])
        nm_ref[...] = nm
        nv_ref[...] = nv

    spec = pl.BlockSpec((tr, c), lambda i: (i, 0))
    shape = jax.ShapeDtypeStruct((r, c), F32)
    return pl.pallas_call(
        body, name=name, grid=(r // tr,),
        in_specs=[spec] * 4, out_specs=[spec] * 3, out_shape=[shape] * 3,
        compiler_params=_params(),
    )(w, g, m, v)


def _adamw_many(name, ws, gs, ms, vs):
    n = len(ws)
    c1 = 1.0 - ADAM_B1 ** ADAM_STEP
    c2 = 1.0 - ADAM_B2 ** ADAM_STEP

    def body(*refs):
        for a in range(n):
            w_ref, g_ref, m_ref, v_ref = refs[a], refs[n + a], refs[2 * n + a], refs[3 * n + a]
            d_ref, nm_ref, nv_ref = refs[4 * n + a], refs[5 * n + a], refs[6 * n + a]
            gv = g_ref[...]
            nm = ADAM_B1 * m_ref[...] + (1.0 - ADAM_B1) * gv
            nv = ADAM_B2 * v_ref[...] + (1.0 - ADAM_B2) * (gv * gv)
            d_ref[...] = -ADAM_LR * ((nm / c1) / (jnp.sqrt(nv / c2) + ADAM_EPS) + ADAM_WD * w_ref[...])
            nm_ref[...] = nm
            nv_ref[...] = nv

    vmem = pl.BlockSpec(memory_space=pltpu.VMEM)
    shapes = [jax.ShapeDtypeStruct(w.shape, F32) for w in ws]
    out = pl.pallas_call(
        body, name=name,
        in_specs=[vmem] * (4 * n), out_specs=[vmem] * (3 * n), out_shape=shapes * 3,
        compiler_params=_params(),
    )(*ws, *gs, *ms, *vs)
    return out[:n], out[n:2 * n], out[2 * n:]


PACK_ROWS = 128


def _pack(parts):
    flat = []
    for p in parts:
        v = p.reshape(-1)
        flat.append(jnp.pad(v, (0, (-v.shape[0]) % BLK)))
    v = jnp.concatenate(flat)
    v = jnp.pad(v, (0, (-v.shape[0]) % (PACK_ROWS * BLK)))
    return v.reshape(-1, BLK)


def _unpack(buf, shapes):
    flat = buf.reshape(-1)
    out, off = [], 0
    for shp in shapes:
        size = math.prod(shp)
        out.append(flat[off:off + size].reshape(shp))
        off += size + (-size) % BLK
    return out


def kernel(x, meta_tokens, norm_mix, w_in, q_norm, k_norm, attn_sinks, lam_re, lam_im, log_dt, ssm_b_re, ssm_b_im, ssm_c_re, ssm_c_im, ssm_d, w_glu, attn_branch_norm, ssm_branch_norm, w_out, norm_ffn, w_ffn_in, w_ffn_out, loss_target, m_meta_tokens, m_norm_mix, m_w_in, m_q_norm, m_k_norm, m_attn_sinks, m_lam_re, m_lam_im, m_log_dt, m_ssm_b_re, m_ssm_b_im, m_ssm_c_re, m_ssm_c_im, m_ssm_d, m_w_glu, m_attn_branch_norm, m_ssm_branch_norm, m_w_out, m_norm_ffn, m_w_ffn_in, m_w_ffn_out, v_meta_tokens, v_norm_mix, v_w_in, v_q_norm, v_k_norm, v_attn_sinks, v_lam_re, v_lam_im, v_log_dt, v_ssm_b_re, v_ssm_b_im, v_ssm_c_re, v_ssm_c_im, v_ssm_d, v_w_glu, v_attn_branch_norm, v_ssm_branch_norm, v_w_out, v_norm_ffn, v_w_ffn_in, v_w_ffn_out):
    args = dict(locals())
    weights = {n: args[n] for n in WEIGHTS}
    mom_m = {n: args["m_" + n] for n in WEIGHTS}
    mom_v = {n: args["v_" + n] for n in WEIGHTS}

    x2d = x[0]
    target2d = loss_target[0]
    s_len = x2d.shape[0]
    l_dim = s_len + BLK
    tm_row = _pick(l_dim, 320)
    tm_mm = _pick(l_dim, 1040)
    tl_tn = _pick(l_dim, 2080)
    tm_ffn = _pick(l_dim, 640)
    tm_big = _pick(l_dim, 2080)
    tm_shift = _pick(l_dim, 640, BLK)

    shard_in = w_in[0].T.astype(BF16)
    shard_glu = w_glu[0].T.astype(BF16)
    shard_out = w_out[0].astype(BF16)
    shard_ffn_in = w_ffn_in[0].T.astype(BF16)
    shard_ffn_out = w_ffn_out[0].astype(BF16)
    shard_meta = meta_tokens.T
    me = _my_index()

    def landing(shard, row_fn=_plain_rows):
        m_rows, cols = shard.shape
        return lax.dynamic_update_slice(lax.empty((N_DEV * m_rows, cols), shard.dtype), shard,
                                        (row_fn(me, m_rows), 0))

    first = [shard_in, shard_meta]
    ga = _send_start("gather_start_a", first, [landing(s) for s in first], chunked=False)
    later = [shard_glu + ga[4][0:1, 0:1].astype(BF16), shard_out, shard_ffn_in, shard_ffn_out]
    later_fns = [_plain_rows, _plain_rows, _ffn_in_rows, _plain_rows]
    gb = _send_start("gather_start_b", later, [landing(s, f) for s, f in zip(later, later_fns)], chunked=False,
                     row_fns=later_fns)

    nm_t = norm_mix + (ga[4][0:1, 0:1] + gb[4][0:1, 0:1])
    qn_t, kn_t = jnp.tile(q_norm, (1, N_Q_HEADS)), jnp.tile(k_norm, (1, N_KV_HEADS))
    e_mat = jnp.kron(jnp.eye(4, dtype=F32), jnp.ones((HEAD_DIM, HEAD_DIM), F32)).astype(BF16)

    def disc(lr, li, ldt, br, bi):
        return _discretize(lr[0], li[0], ldt[0], br[0], bi[0])

    (abar_re, abar_im, bbar_re, bbar_im), disc_vjp = jax.vjp(disc, lam_re, lam_im, log_dt, ssm_b_re, ssm_b_im)
    wb_re, wb_im = _block_diag_b(bbar_re).astype(BF16), _block_diag_b(bbar_im).astype(BF16)
    wc_re, wc_im = _block_diag_c(ssm_c_re[0]).astype(BF16), _block_diag_c(ssm_c_im[0]).astype(BF16)
    q_ssm = _pick(l_dim, 640, 64)
    tabs = _power_tables(abar_re.reshape(1, N_STATE), abar_im.reshape(1, N_STATE), q_ssm // GRP)

    h0, xn = _embed_norm(x2d, nm_t, tm_shift)
    wt_in, meta_t = _send_wait("gather_wait_a", ga[0], ga[1], ga[2], ga[3], xn, chunked=False)
    meta_pad = jnp.pad(meta_t.T, ((PAD, 0), (0, 0)))
    h0, xn = _embed_meta(meta_pad, nm_t, h0, xn)
    qkv_w, u_end = Q_W + 2 * KV_W, Q_W + 2 * KV_W + D_MODEL
    wt_in_p = jnp.concatenate([wt_in[u_end:], wt_in[qkv_w:u_end], wt_in[:qkv_w]], axis=0)
    qkv = _matmul("proj_qkv", xn, wt_in_p, nt=True, tm=tm_big, tn=qkv_w, tk=D_MODEL, n=qkv_w, w_off=2)
    u = _matmul("proj_u", xn, wt_in_p, nt=True, tm=tm_big, tn=D_MODEL, tk=D_MODEL, n=D_MODEL, w_off=2)
    gates = _matmul("proj_gates", xn, wt_in_p, nt=True, tm=tm_big, tn=D_MODEL, tk=D_MODEL, n=2 * D_MODEL, w_off=0,
                    out_dtype=BF16)
    qn, kf, vf = _qk_prep(qkv, qn_t, kn_t, e_mat, tm_row)
    attn, lse = _attn_fwd(qn, kf, vf, attn_sinks)
    y, z, s_re, s_im = _ssm_fwd_kb(u, wb_re, wb_im, wc_re, wc_im, ssm_d, tabs, q_ssm)
    wt_glu, w_out_f, wt_ffn_in, w_ffn_out_f = _send_wait("gather_wait_b", gb[0], gb[1], gb[2], gb[3], z,
                                                         chunked=False, row_fns=later_fns)
    zab = _matmul("glu_proj", z, wt_glu, nt=True, tm=tm_big, tn=1024, tk=D_MODEL, out_dtype=BF16)
    merged = _merge_fwd(attn, zab, gates, attn_branch_norm, ssm_branch_norm, tm_row)
    h1, hn = _matmul("out_proj", merged, w_out_f, nt=False, tm=tm_mm, tn=1024, tk=D_MODEL, res=h0,
                     norm_g=norm_ffn)
    gu, act = _ffn_in_swiglu(hn, wt_ffn_in, tm_ffn)
    h2 = _matmul("ffn_out", act, w_ffn_out_f, nt=False, tm=tm_mm, tn=1024, tk=D_FF, res=h1)
    dh2, dh2_b, loss_part = _loss_grad(h2, target2d, tm_shift)

    dgu = _d_act_swiglu(dh2_b, w_ffn_out_f, gu, tm_ffn)

    def exchange_start(name, grads_b, row_fns=None):
        return _send_start(name, grads_b, [jnp.zeros(g.shape, BF16) for g in grads_b], chunked=True,
                           row_fns=row_fns)

    g_ffn_out, g_ffn_out_b = _matmul_tn("g_ffn_out", act, dh2_b, tm=1408, tn=1024, tl=tl_tn)
    g_ffn_in_t, g_ffn_in_b = _matmul_tn("g_ffn_in", dgu, hn, tm=1408, tn=1024, tl=tl_tn)
    ffn_fns = [_ffn_in_rows, _plain_rows]
    ex1 = exchange_start("exchange_start_ffn", [g_ffn_in_b, g_ffn_out_b], ffn_fns)
    dhn = _matmul("d_hn", dgu, wt_ffn_in, nt=False, tm=tm_ffn, tn=1024, tk=2 * D_FF, out_dtype=BF16)
    dh1, dh1_b, g_norm_ffn = _norm_bwd_res("ffn_norm_bwd", h1, norm_ffn + ex1[4][0:1, 0:1], dhn, dh2, tm_row)
    dmerged = _matmul("d_merged", dh1_b, w_out_f, nt=True, tm=tm_big, tn=1024, tk=D_MODEL, out_dtype=BF16)
    dattn, dzab, dproj, g_abn, g_sbn = _merge_bwd(attn, zab, gates, attn_branch_norm, ssm_branch_norm,
                                                    dmerged, tm_row)
    g_out, g_out_b = _matmul_tn("g_out", merged, dh1_b, tm=1024, tn=1024, tl=tl_tn)
    dz = _matmul("d_z", dzab, wt_glu, nt=False, tm=tm_mm, tn=1024, tk=2 * D_MODEL, out_dtype=BF16)
    g_glu_t, g_glu_b = _matmul_tn("g_glu", dzab, z, tm=1024, tn=1024, tl=tl_tn)
    ex2 = exchange_start("exchange_start_mix", [g_glu_b, g_out_b])
    dproj, g_ssm_d, g_ar, g_ai, g_wbr, g_wbi, g_wcr, g_wci = _ssm_bwd_kb(
        dz, y, u, s_re, s_im, wb_re, wb_im, wc_re, wc_im, ssm_d + ex2[4][0:1, 0:1], tabs, q_ssm, dproj)
    dq, dkc, dkp, dkm, dvc, dvp, dvm, g_sinks = _attn_bwd(qn, kf, vf, attn_sinks, lse, attn, dattn)
    dproj, g_qn_t, g_kn_t = _qk_bwd(qkv, qn_t, kn_t, e_mat, dq, dkc, dkp, dkm, dvc, dvp, dvm, dproj)
    g_lam_re, g_lam_im, g_log_dt, g_b_re, g_b_im = disc_vjp(
        (g_ar.reshape(SSM_GROUPS, SSM_STATE), g_ai.reshape(SSM_GROUPS, SSM_STATE),
         _block_diag_b_t(g_wbr), _block_diag_b_t(g_wbi)))
    small_grads = {
        "q_norm": g_qn_t.reshape(N_Q_HEADS, HEAD_DIM).sum(0)[None],
        "k_norm": g_kn_t.reshape(N_KV_HEADS, HEAD_DIM).sum(0)[None], "attn_sinks": g_sinks,
        "lam_re": g_lam_re, "lam_im": g_lam_im, "log_dt": g_log_dt, "ssm_b_re": g_b_re, "ssm_b_im": g_b_im,
        "ssm_c_re": _block_diag_c_t(g_wcr)[None], "ssm_c_im": _block_diag_c_t(g_wci)[None],
        "ssm_d": g_ssm_d, "attn_branch_norm": g_abn, "ssm_branch_norm": g_sbn, "norm_ffn": g_norm_ffn,
    }
    early = [n for n in SMALL if n != "norm_mix"]
    packed_e = _pack([small_grads[n] for n in early])
    gs_e = _send_start("small_start_a", [packed_e], [landing(packed_e)], chunked=False)

    to_front = lambda a: jnp.concatenate([a[3 * D_MODEL:], a[2 * D_MODEL:3 * D_MODEL], a[:2 * D_MODEL]], axis=0)
    g_in_p, g_in_pb = _matmul_tn("g_in", dproj, xn, tm=1152, tn=1024, tl=tl_tn, after=gs_e[4])
    g_in_t, g_in_b = to_front(g_in_p), to_front(g_in_pb)
    ex3 = exchange_start("exchange_start_in", [g_in_b])
    dxn = _matmul("d_xn", dproj, wt_in_p, nt=False, tm=tm_ffn, tn=1024, tk=IN_COLS, out_dtype=BF16)
    grad_x2d, dmeta_blk, g_norm_mix = _final_bwd(h0, nm_t + ex3[4][0:1, 0:1], dxn, dh1, _pick(s_len, 512, BLK))
    packed_l = _pack([g_norm_mix, dmeta_blk[PAD:], loss_part])
    gs_l = _send_start("small_start_b", [packed_l], [landing(packed_l)], chunked=False)
    grads, deltas, new_m, new_v = {}, {}, {}, {}

    recv_ffn_in, recv_ffn_out = _send_wait("exchange_wait_ffn", ex1[0], ex1[1], ex1[2], ex1[3], gs_l[4],
                                           chunked=True, row_fns=ffn_fns)
    recv_glu, recv_out = _send_wait("exchange_wait_mix", ex2[0], ex2[1], ex2[2], ex2[3], recv_ffn_in,
                                    chunked=True)
    (recv_in,) = _send_wait("exchange_wait_in", ex3[0], ex3[1], ex3[2], ex3[3], recv_glu, chunked=True)
    big = [("w_in", g_in_t, True, recv_in, _plain_rows), ("w_glu", g_glu_t, True, recv_glu, _plain_rows),
           ("w_out", g_out, False, recv_out, _plain_rows), ("w_ffn_in", g_ffn_in_t, True, recv_ffn_in, _ffn_in_rows),
           ("w_ffn_out", g_ffn_out, False, recv_ffn_out, _plain_rows)]
    for name, g_full, transposed, recv, row_fn in big:
        m_rows = g_full.shape[0] // N_DEV
        own = lax.dynamic_slice(g_full, (row_fn(me, m_rows), 0), (m_rows, g_full.shape[1]))
        g_shard = _sum_slots("sum_" + name, recv, own)
        grads[name] = (g_shard.T if transposed else g_shard)[None]

    def adamw_2d(name):
        shp = weights[name].shape
        as2d = lambda a: a.reshape(shp[-2], shp[-1])
        d, nm, nv = _adamw("adamw_" + name, as2d(weights[name]), as2d(grads[name]), as2d(mom_m[name]),
                           as2d(mom_v[name]))
        deltas[name], new_m[name], new_v[name] = d.reshape(shp), nm.reshape(shp), nv.reshape(shp)
        return d

    for name in ["w_in", "w_glu", "w_out", "w_ffn_in", "w_ffn_out"]:
        last = adamw_2d(name)

    def small_sum(tag, gs, packed, after):
        (gathered,) = _send_wait("small_wait_" + tag, gs[0], gs[1], gs[2], gs[3], after, chunked=False)
        return _sum_gathered("sum_small_" + tag, gathered, packed.shape[0])

    def small_adamw(tag, names):
        d, nm, nv = _adamw_many("adamw_small_" + tag, [weights[n] for n in names], [grads[n] for n in names],
                                [mom_m[n] for n in names], [mom_v[n] for n in names])
        deltas.update(zip(names, d))
        new_m.update(zip(names, nm))
        new_v.update(zip(names, nv))

    g_sum_e = small_sum("a", gs_e, packed_e, last)
    grads.update(zip(early, _unpack(g_sum_e, [weights[n].shape for n in early])))
    wide = [n for n in early if n.startswith(("ssm_b", "ssm_c"))]
    small_adamw("wide", wide)
    g_sum_l = small_sum("b", gs_l, packed_l, g_sum_e)
    grads["norm_mix"], g_meta, loss_sum = _unpack(g_sum_l, [weights["norm_mix"].shape, (N_META, D_MODEL), (1, 1)])
    small_adamw("rest", [n for n in SMALL if n not in wide])
    grads["meta_tokens"] = lax.dynamic_slice(g_meta, (0, me * BLK), (N_META, BLK))
    adamw_2d("meta_tokens")

    loss = loss_sum[0, 0]
    return (loss, grad_x2d[None], *[grads[n] for n in WEIGHTS], *[deltas[n] for n in WEIGHTS],
            *[new_m[n] for n in WEIGHTS], *[new_v[n] for n in WEIGHTS])
```

```python
import math

import jax
import jax.numpy as jnp
from jax import lax
from jax.experimental import pallas as pl
from jax.experimental.pallas import tpu as pltpu

F32 = jnp.float32
BF16 = jnp.bfloat16

D_MODEL = 1024
N_META = 16
HEAD_DIM = 64
N_Q_HEADS = 16
N_KV_HEADS = 4
Q_W = N_Q_HEADS * HEAD_DIM
KV_W = N_KV_HEADS * HEAD_DIM
SSM_GROUPS = 64
SSM_GROUP_CH = 16
SSM_STATE = 64
N_STATE = SSM_GROUPS * SSM_STATE
D_FF = 2816
IN_COLS = Q_W + 2 * KV_W + 3 * D_MODEL
EPS = 1e-6
BLK = 128
PAD = BLK - N_META
N_DEV = 8
NEG = -1e30
SSM_KB = 8
ST_KB = N_STATE // SSM_KB
LB_KB = ST_KB // BLK
N_LB = N_STATE // BLK

ADAM_LR = 0.001
ADAM_B1 = 0.9
ADAM_B2 = 0.999
ADAM_EPS = 1e-08
ADAM_WD = 0.01
ADAM_STEP = 10

VMEM_LIMIT = 48 * 1024 * 1024
MESH = pl.DeviceIdType.MESH

SMALL = ["norm_mix", "q_norm", "k_norm", "attn_sinks", "lam_re", "lam_im", "log_dt", "ssm_b_re", "ssm_b_im",
         "ssm_c_re", "ssm_c_im", "ssm_d", "attn_branch_norm", "ssm_branch_norm", "norm_ffn"]
WEIGHTS = ["meta_tokens", "norm_mix", "w_in", "q_norm", "k_norm", "attn_sinks", "lam_re", "lam_im", "log_dt",
           "ssm_b_re", "ssm_b_im", "ssm_c_re", "ssm_c_im", "ssm_d", "w_glu", "attn_branch_norm",
           "ssm_branch_norm", "w_out", "norm_ffn", "w_ffn_in", "w_ffn_out"]


def _params(**kw):
    return pltpu.CompilerParams(vmem_limit_bytes=VMEM_LIMIT, **kw)


def _pick(n, cap, mult=16):
    best = None
    for d in range(mult, min(n, cap) + 1, mult):
        if n % d == 0:
            best = d
    assert best is not None, (n, cap, mult)
    return best


def _my_index():
    return 4 * lax.axis_index("x") + 2 * lax.axis_index("y") + lax.axis_index("c")


def _rms(x, g):
    r = lax.rsqrt(jnp.mean(x * x, axis=-1, keepdims=True) + EPS)
    return x * r * g


def _rms_bwd(x, g, dy):
    r = lax.rsqrt(jnp.mean(x * x, axis=-1, keepdims=True) + EPS)
    t = dy * g
    dx = r * t - x * (r * r * r) * jnp.mean(t * x, axis=-1, keepdims=True)
    dg = jnp.sum(dy * (x * r), axis=0, keepdims=True)
    return dx, dg


def _sigmoid(x):
    return jax.nn.sigmoid(x)


def _gelu(x):
    k = math.sqrt(2.0 / math.pi)
    return 0.5 * x * (1.0 + jnp.tanh(k * (x + 0.044715 * (x * x * x))))


def _gelu_grad(x):
    k = math.sqrt(2.0 / math.pi)
    t = jnp.tanh(k * (x + 0.044715 * (x * x * x)))
    return 0.5 * (1.0 + t) + 0.5 * x * (1.0 - t * t) * (k * (1.0 + 3.0 * 0.044715 * (x * x)))


def _head_mean(x, e_ref):
    hi = x.astype(BF16)
    lo = (x - hi.astype(F32)).astype(BF16)
    e = e_ref[...]
    out = []
    for b in range(x.shape[1] // 256):
        sl = slice(256 * b, 256 * b + 256)
        s = (jnp.dot(hi[:, sl], e, preferred_element_type=F32)
             + jnp.dot(lo[:, sl], e, preferred_element_type=F32))
        out.append(s)
    s = out[0] if len(out) == 1 else jnp.concatenate(out, axis=1)
    return s * (1.0 / HEAD_DIM)


def _head_rms(x, g, e_ref):
    r = lax.rsqrt(_head_mean(x * x, e_ref) + EPS)
    return x * r * g


def _head_rms_bwd(x, g, dy, e_ref):
    r = lax.rsqrt(_head_mean(x * x, e_ref) + EPS)
    t = dy * g
    dx = r * t - x * (r * r * r) * _head_mean(t * x, e_ref)
    dg = jnp.sum(dy * (x * r), axis=0, keepdims=True)
    return dx, dg


def _lane_half(shape):
    lane = lax.broadcasted_iota(jnp.int32, shape, len(shape) - 1)
    return (lane >> 6) & 1


def _matmul(name, a, w, *, nt, tm, tn, tk, n=None, w_off=0, res=None, norm_g=None, out_dtype=F32):
    m_dim, k_dim = a.shape
    n_dim = n if n is not None else (w.shape[0] if nt else w.shape[1])
    gm, gn, gk = m_dim // tm, n_dim // tn, k_dim // tk
    assert gm * tm == m_dim and gn * tn == n_dim and gk * tk == k_dim, (name, a.shape, w.shape, tm, tn, tk)
    assert norm_g is None or tn == n_dim
    direct = out_dtype == F32
    dn = (((1,), (1,)), ((), ())) if nt else (((1,), (0,)), ((), ()))

    def body(*refs):
        refs = list(refs)
        a_ref, w_ref = refs[0], refs[1]
        pos = 2
        r_ref = g_ref = on_ref = None
        if res is not None:
            r_ref, pos = refs[pos], pos + 1
        if norm_g is not None:
            g_ref, pos = refs[pos], pos + 1
        o_ref, pos = refs[pos], pos + 1
        if norm_g is not None:
            on_ref, pos = refs[pos], pos + 1
        part = lax.dot_general(a_ref[...], w_ref[...], dn, preferred_element_type=F32)
        if gk == 1:
            r = part if r_ref is None else r_ref[...] + part
            o_ref[...] = r.astype(out_dtype)
            if on_ref is not None:
                on_ref[...] = _rms(r, g_ref[...]).astype(BF16)
            return
        acc = o_ref if direct else refs[pos]
        k = pl.program_id(2)

        @pl.when(k == 0)
        def _():
            acc[...] = part if r_ref is None or not direct else r_ref[...] + part

        @pl.when(k > 0)
        def _():
            acc[...] += part

        @pl.when(k == gk - 1)
        def _():
            if not direct:
                r = acc[...]
                if r_ref is not None:
                    r = r_ref[...] + r
                o_ref[...] = r.astype(out_dtype)
            if on_ref is not None:
                on_ref[...] = _rms(o_ref[...].astype(F32), g_ref[...]).astype(BF16)

    if nt:
        w_spec = pl.BlockSpec((tn, tk), lambda i, j, k: (j + w_off, k))
    else:
        w_spec = pl.BlockSpec((tk, tn), lambda i, j, k: (k, j))
    in_specs = [pl.BlockSpec((tm, tk), lambda i, j, k: (i, k)), w_spec]
    args = [a, w]
    out_spec = pl.BlockSpec((tm, tn), lambda i, j, k: (i, j))
    out_specs, out_shape = [out_spec], [jax.ShapeDtypeStruct((m_dim, n_dim), out_dtype)]
    if res is not None:
        in_specs.append(out_spec)
        args.append(res)
    if norm_g is not None:
        in_specs.append(pl.BlockSpec((1, tn), lambda i, j, k: (0, 0)))
        args.append(norm_g)
        out_specs.append(out_spec)
        out_shape.append(jax.ShapeDtypeStruct((m_dim, n_dim), BF16))
    out = pl.pallas_call(
        body, name=name, grid=(gm, gn, gk),
        in_specs=in_specs, out_specs=out_specs, out_shape=out_shape,
        scratch_shapes=[] if direct or gk == 1 else [pltpu.VMEM((tm, tn), F32)],
        compiler_params=_params(dimension_semantics=("parallel", "parallel", "arbitrary")),
    )(*args)
    return out if norm_g is not None else out[0]


def _matmul_tn(name, a, b, *, tm, tn, tl, after=None):
    l_dim, m_dim = a.shape
    n_dim = b.shape[1]
    gm, gn, gl = m_dim // tm, n_dim // tn, l_dim // tl
    assert gm * tm == m_dim and gn * tn == n_dim and gl * tl == l_dim, (name, a.shape, b.shape, tm, tn, tl)

    def body(*refs):
        a_ref, b_ref = refs[0], refs[1]
        o_ref, ob_ref = refs[-2], refs[-1]

        @pl.when(pl.program_id(2) == 0)
        def _():
            o_ref[...] = jnp.zeros_like(o_ref)

        o_ref[...] += lax.dot_general(a_ref[...], b_ref[...], (((0,), (0,)), ((), ())),
                                      preferred_element_type=F32)

        @pl.when(pl.program_id(2) == gl - 1)
        def _():
            ob_ref[...] = o_ref[...].astype(BF16)

    out_spec = pl.BlockSpec((tm, tn), lambda i, j, l: (i, j))
    in_specs = [pl.BlockSpec((tl, tm), lambda i, j, l: (l, i)), pl.BlockSpec((tl, tn), lambda i, j, l: (l, j))]
    args = [a, b]
    if after is not None:
        in_specs.append(pl.BlockSpec(memory_space=pl.ANY))
        args.append(after)
    return pl.pallas_call(
        body, name=name, grid=(gm, gn, gl),
        in_specs=in_specs,
        out_specs=[out_spec, out_spec],
        out_shape=[jax.ShapeDtypeStruct((m_dim, n_dim), F32), jax.ShapeDtypeStruct((m_dim, n_dim), BF16)],
        compiler_params=_params(dimension_semantics=("parallel", "parallel", "arbitrary")),
    )(*args)


def _row_spec(tm, cols, f=None):
    if f is None:
        return pl.BlockSpec((tm, cols), lambda i: (i, 0))
    return pl.BlockSpec((tm, cols), lambda i: (f(i), 0))


def _full_spec(shape):
    nd = len(shape)
    return pl.BlockSpec(shape, lambda i: (0,) * nd)


def _shifted_specs(n_sub, n_blocks):
    return [_row_spec(BLK, D_MODEL, (lambda i, k=k: jnp.clip(n_sub * i - 1 + k, 0, n_blocks - 1)))
            for k in range(n_sub)]


def _embed_norm(x2d, g, tm):
    s_len = x2d.shape[0]
    l_dim = s_len + BLK
    n_sub = tm // BLK

    def body(*refs):
        x_refs, g_ref, h_ref, xn_ref = refs[:n_sub], refs[n_sub], refs[n_sub + 1], refs[n_sub + 2]
        i = pl.program_id(0)
        for k in range(n_sub):
            rows = slice(BLK * k, BLK * k + BLK)
            h = x_refs[k][...] * jnp.where(n_sub * i + k >= 1, 1.0, 0.0)
            h_ref[rows, :] = h
            xn_ref[rows, :] = _rms(h, g_ref[...]).astype(BF16)

    return pl.pallas_call(
        body, name="embed_norm", grid=(l_dim // tm,),
        in_specs=_shifted_specs(n_sub, s_len // BLK) + [_full_spec((1, D_MODEL))],
        out_specs=[_row_spec(tm, D_MODEL), _row_spec(tm, D_MODEL)],
        out_shape=[jax.ShapeDtypeStruct((l_dim, D_MODEL), F32),
                   jax.ShapeDtypeStruct((l_dim, D_MODEL), BF16)],
        compiler_params=_params(),
    )(*([x2d] * n_sub), g)


def _embed_meta(meta_pad, g, h0, xn):
    def body(mp_ref, g_ref, h_in, xn_in, h_ref, xn_ref):
        h_ref[...] = mp_ref[...]
        xn_ref[...] = _rms(mp_ref[...], g_ref[...]).astype(BF16)

    any_spec = pl.BlockSpec(memory_space=pl.ANY)
    return pl.pallas_call(
        body, name="embed_meta", grid=(1,),
        in_specs=[_full_spec((BLK, D_MODEL)), _full_spec((1, D_MODEL)), any_spec, any_spec],
        out_specs=[_row_spec(BLK, D_MODEL), _row_spec(BLK, D_MODEL)],
        out_shape=[jax.ShapeDtypeStruct(h0.shape, F32), jax.ShapeDtypeStruct(xn.shape, BF16)],
        input_output_aliases={2: 0, 3: 1},
        compiler_params=_params(),
    )(meta_pad, g, h0, xn)


KVX_W = 2 * N_KV_HEADS * BLK


def _qk_prep(qkv, q_norm_t, k_norm_t, e_mat, tm):
    l_dim = qkv.shape[0]

    def body(x_ref, qg_ref, kg_ref, e_ref, q_ref, kf_ref, vf_ref):
        x = x_ref[...]
        q = _head_rms(x[:, :Q_W], qg_ref[...], e_ref) * (HEAD_DIM ** -0.5)
        q_ref[...] = q.astype(BF16)
        k = _head_rms(x[:, Q_W:Q_W + KV_W], kg_ref[...], e_ref)
        v = x[:, Q_W + KV_W:Q_W + 2 * KV_W]
        half = _lane_half((tm, BLK))
        for src, dst in ((k, kf_ref), (v, vf_ref)):
            for kv in range(N_KV_HEADS):
                blk = src[:, BLK * (kv // 2):BLK * (kv // 2) + BLK]
                swapped = pltpu.roll(blk, HEAD_DIM, axis=1)
                for e in range(2):
                    val = blk if kv % 2 == e else swapped
                    idx = 2 * kv + e
                    dst[:, BLK * idx:BLK * idx + BLK] = jnp.where(half == e, val, 0.0).astype(BF16)

    return pl.pallas_call(
        body, name="qk_prep", grid=(l_dim // tm,),
        in_specs=[_row_spec(tm, Q_W + 2 * KV_W), _full_spec((1, Q_W)), _full_spec((1, KV_W)),
                  _full_spec((256, 256))],
        out_specs=[_row_spec(tm, Q_W), _row_spec(tm, KVX_W), _row_spec(tm, KVX_W)],
        out_shape=[jax.ShapeDtypeStruct((l_dim, Q_W), BF16),
                   jax.ShapeDtypeStruct((l_dim, KVX_W), BF16),
                   jax.ShapeDtypeStruct((l_dim, KVX_W), BF16)],
        compiler_params=_params(),
    )(qkv, q_norm_t, k_norm_t, e_mat)


def _merge_fwd(attn, zab, gates, abn, sbn, tm):
    l_dim = attn.shape[0]

    def body(a_ref, z_ref, g_ref, an_ref, sn_ref, o_ref):
        z = z_ref[...].astype(F32)
        g = g_ref[...].astype(F32)
        ssm = z[:, :D_MODEL] * _sigmoid(z[:, D_MODEL:])
        merged = (_sigmoid(g[:, :D_MODEL]) * _rms(a_ref[...], an_ref[...])
                  + _sigmoid(g[:, D_MODEL:]) * _rms(ssm, sn_ref[...]))
        o_ref[...] = merged.astype(BF16)

    return pl.pallas_call(
        body, name="merge_fwd", grid=(l_dim // tm,),
        in_specs=[_row_spec(tm, D_MODEL), _row_spec(tm, 2 * D_MODEL), _row_spec(tm, 2 * D_MODEL),
                  _full_spec((1, D_MODEL)), _full_spec((1, D_MODEL))],
        out_specs=_row_spec(tm, D_MODEL),
        out_shape=jax.ShapeDtypeStruct((l_dim, D_MODEL), BF16),
        compiler_params=_params(),
    )(attn, zab, gates, abn, sbn)


def _merge_bwd(attn, zab, gates, abn, sbn, dmerged, tm):
    l_dim = attn.shape[0]

    def body(a_ref, z_ref, g_ref, an_ref, sn_ref, dm_ref, da_ref, dz_ref, dg_ref, dan_ref, dsn_ref):
        @pl.when(pl.program_id(0) == 0)
        def _():
            dan_ref[...] = jnp.zeros_like(dan_ref)
            dsn_ref[...] = jnp.zeros_like(dsn_ref)

        z = z_ref[...].astype(F32)
        g = g_ref[...].astype(F32)
        dm = dm_ref[...].astype(F32)
        attn_v = a_ref[...]
        za, zb = z[:, :D_MODEL], z[:, D_MODEL:]
        sb = _sigmoid(zb)
        ssm = za * sb
        s_ga, s_gs = _sigmoid(g[:, :D_MODEL]), _sigmoid(g[:, D_MODEL:])
        a_n = _rms(attn_v, an_ref[...])
        s_n = _rms(ssm, sn_ref[...])
        dg_ref[:, :D_MODEL] = (dm * a_n * s_ga * (1.0 - s_ga)).astype(BF16)
        dg_ref[:, D_MODEL:] = (dm * s_n * s_gs * (1.0 - s_gs)).astype(BF16)
        dattn, dan = _rms_bwd(attn_v, an_ref[...], dm * s_ga)
        dssm, dsn = _rms_bwd(ssm, sn_ref[...], dm * s_gs)
        da_ref[...] = dattn.astype(BF16)
        dz_ref[:, :D_MODEL] = (dssm * sb).astype(BF16)
        dz_ref[:, D_MODEL:] = (dssm * za * sb * (1.0 - sb)).astype(BF16)
        dan_ref[...] += dan
        dsn_ref[...] += dsn

    return pl.pallas_call(
        body, name="merge_bwd", grid=(l_dim // tm,),
        in_specs=[_row_spec(tm, D_MODEL), _row_spec(tm, 2 * D_MODEL), _row_spec(tm, 2 * D_MODEL),
                  _full_spec((1, D_MODEL)), _full_spec((1, D_MODEL)), _row_spec(tm, D_MODEL)],
        out_specs=[_row_spec(tm, D_MODEL), _row_spec(tm, 2 * D_MODEL), _row_spec(tm, 2 * D_MODEL),
                   _full_spec((1, D_MODEL)), _full_spec((1, D_MODEL))],
        out_shape=[jax.ShapeDtypeStruct((l_dim, D_MODEL), BF16),
                   jax.ShapeDtypeStruct((l_dim, 2 * D_MODEL), BF16),
                   jax.ShapeDtypeStruct((l_dim, IN_COLS), BF16),
                   jax.ShapeDtypeStruct((1, D_MODEL), F32), jax.ShapeDtypeStruct((1, D_MODEL), F32)],
        compiler_params=_params(),
    )(attn, zab, gates, abn, sbn, dmerged)


FF_TILE = D_FF // 2


def _ffn_in_swiglu(hn, wt_ffn_in, tm):
    l_dim = hn.shape[0]
    nt = (((1,), (1,)), ((), ()))

    def body(a_ref, w_ref, gu_ref, act_ref):
        r = lax.dot_general(a_ref[...], w_ref[...], nt, preferred_element_type=F32)
        gate, up = r[:, :FF_TILE], r[:, FF_TILE:]
        gu_ref[...] = r.astype(BF16)
        act_ref[...] = (gate * _sigmoid(gate) * up).astype(BF16)

    return pl.pallas_call(
        body, name="ffn_in_swiglu", grid=(l_dim // tm, 2),
        in_specs=[pl.BlockSpec((tm, D_MODEL), lambda i, j: (i, 0)),
                  pl.BlockSpec((2 * FF_TILE, D_MODEL), lambda i, j: (j, 0))],
        out_specs=[pl.BlockSpec((tm, 2 * FF_TILE), lambda i, j: (i, j)),
                   pl.BlockSpec((tm, FF_TILE), lambda i, j: (i, j))],
        out_shape=[jax.ShapeDtypeStruct((l_dim, 2 * D_FF), BF16), jax.ShapeDtypeStruct((l_dim, D_FF), BF16)],
        compiler_params=_params(dimension_semantics=("parallel", "parallel")),
    )(hn, wt_ffn_in)


def _d_act_swiglu(dh2_b, w_ffn_out, gu, tm):
    l_dim = dh2_b.shape[0]
    nt = (((1,), (1,)), ((), ()))

    def body(d_ref, w_ref, gu_ref, o_ref):
        d = lax.dot_general(d_ref[...], w_ref[...], nt, preferred_element_type=F32)
        gate = gu_ref[:, :FF_TILE].astype(F32)
        up = gu_ref[:, FF_TILE:].astype(F32)
        s = _sigmoid(gate)
        o_ref[:, :FF_TILE] = (d * up * (s * (1.0 + gate * (1.0 - s)))).astype(BF16)
        o_ref[:, FF_TILE:] = (d * (gate * s)).astype(BF16)

    return pl.pallas_call(
        body, name="d_act_swiglu", grid=(l_dim // tm, 2),
        in_specs=[pl.BlockSpec((tm, D_MODEL), lambda i, j: (i, 0)),
                  pl.BlockSpec((FF_TILE, D_MODEL), lambda i, j: (j, 0)),
                  pl.BlockSpec((tm, 2 * FF_TILE), lambda i, j: (i, j))],
        out_specs=pl.BlockSpec((tm, 2 * FF_TILE), lambda i, j: (i, j)),
        out_shape=jax.ShapeDtypeStruct((l_dim, 2 * D_FF), BF16),
        compiler_params=_params(dimension_semantics=("parallel", "parallel")),
    )(dh2_b, w_ffn_out, gu)


def _loss_grad(h2, target2d, tm):
    l_dim = h2.shape[0]
    n_sub = tm // BLK

    def body(*refs):
        h_ref, t_refs = refs[0], refs[1:1 + n_sub]
        d_ref, db_ref, loss_ref = refs[1 + n_sub:]
        i = pl.program_id(0)

        @pl.when(i == 0)
        def _():
            loss_ref[...] = jnp.zeros_like(loss_ref)

        for k in range(n_sub):
            rows = slice(BLK * k, BLK * k + BLK)
            real = jnp.where(n_sub * i + k >= 1, 1.0, 0.0)
            err = (h_ref[rows, :] - t_refs[k][...]) * real
            d = err * (1.0 / D_MODEL)
            d_ref[rows, :] = d
            db_ref[rows, :] = d.astype(BF16)
            loss_ref[...] += 0.5 * jnp.sum(jnp.mean(err * err, axis=-1, keepdims=True), axis=0, keepdims=True)

    return pl.pallas_call(
        body, name="loss_grad", grid=(l_dim // tm,),
        in_specs=[_row_spec(tm, D_MODEL)] + _shifted_specs(n_sub, target2d.shape[0] // BLK),
        out_specs=[_row_spec(tm, D_MODEL), _row_spec(tm, D_MODEL), _full_spec((1, 1))],
        out_shape=[jax.ShapeDtypeStruct((l_dim, D_MODEL), F32), jax.ShapeDtypeStruct((l_dim, D_MODEL), BF16),
                   jax.ShapeDtypeStruct((1, 1), F32)],
        compiler_params=_params(),
    )(h2, *([target2d] * n_sub))


def _norm_bwd_res(name, h, g, dy, dres, tm):
    l_dim = h.shape[0]

    def body(h_ref, g_ref, dy_ref, dr_ref, o_ref, ob_ref, dg_ref):
        @pl.when(pl.program_id(0) == 0)
        def _():
            dg_ref[...] = jnp.zeros_like(dg_ref)

        dx, dg = _rms_bwd(h_ref[...], g_ref[...], dy_ref[...].astype(F32))
        out = dr_ref[...] + dx
        o_ref[...] = out
        ob_ref[...] = out.astype(BF16)
        dg_ref[...] += dg

    return pl.pallas_call(
        body, name=name, grid=(l_dim // tm,),
        in_specs=[_row_spec(tm, D_MODEL), _full_spec((1, D_MODEL)), _row_spec(tm, D_MODEL), _row_spec(tm, D_MODEL)],
        out_specs=[_row_spec(tm, D_MODEL), _row_spec(tm, D_MODEL), _full_spec((1, D_MODEL))],
        out_shape=[jax.ShapeDtypeStruct((l_dim, D_MODEL), F32), jax.ShapeDtypeStruct((l_dim, D_MODEL), BF16),
                   jax.ShapeDtypeStruct((1, D_MODEL), F32)],
        compiler_params=_params(),
    )(h, g, dy, dres)


def _final_bwd(h0, g, dxn, dh1, tm):
    l_dim = h0.shape[0]
    n_sub = tm // BLK
    n_tiles = (l_dim - BLK) // tm

    def sub_specs():
        return [_row_spec(BLK, D_MODEL, (lambda j, k=k: jnp.where(j < n_tiles, n_sub * j + 1 + k, 0)))
                for k in range(n_sub)]

    def body(*refs):
        h_refs, g_ref = refs[:n_sub], refs[n_sub]
        dy_refs, dr_refs = refs[n_sub + 1:2 * n_sub + 1], refs[2 * n_sub + 1:3 * n_sub + 1]
        gx_ref, gm_ref, dg_ref = refs[3 * n_sub + 1:]
        j = pl.program_id(0)

        @pl.when(j == 0)
        def _():
            dg_ref[...] = jnp.zeros_like(dg_ref)

        def block(k):
            dx, dg = _rms_bwd(h_refs[k][...], g_ref[...], dy_refs[k][...].astype(F32))
            dg_ref[...] += dg
            return dr_refs[k][...] + dx

        @pl.when(j < n_tiles)
        def _():
            for k in range(n_sub):
                gx_ref[BLK * k:BLK * k + BLK, :] = block(k)

        @pl.when(j == n_tiles)
        def _():
            gm_ref[...] = block(0)

    return pl.pallas_call(
        body, name="final_bwd", grid=(n_tiles + 1,),
        in_specs=sub_specs() + [_full_spec((1, D_MODEL))] + sub_specs() + sub_specs(),
        out_specs=[_row_spec(tm, D_MODEL, lambda j: jnp.minimum(j, n_tiles - 1)), _full_spec((BLK, D_MODEL)),
                   _full_spec((1, D_MODEL))],
        out_shape=[jax.ShapeDtypeStruct((l_dim - BLK, D_MODEL), F32), jax.ShapeDtypeStruct((BLK, D_MODEL), F32),
                   jax.ShapeDtypeStruct((1, D_MODEL), F32)],
        compiler_params=_params(),
    )(*([h0] * n_sub), g, *([dxn] * n_sub), *([dh1] * n_sub))


def _attn_valid(n):
    shape = (2 * BLK, 3 * BLK)
    qi = lax.broadcasted_iota(jnp.int32, shape, 0) & (BLK - 1)
    col = lax.broadcasted_iota(jnp.int32, shape, 1)
    kj = col & (BLK - 1)
    part = col >> 7
    nn = jnp.zeros(shape, jnp.int32) + n
    meta_ok = (part == 0) & (kj >= PAD) & (nn >= 1)
    prev_ok = (part == 1) & (kj > qi) & (nn >= 2)
    cur_ok = (part == 2) & (kj <= qi) & ((nn >= 1) | (kj >= PAD))
    return meta_ok | prev_ok | cur_ok


def _attn_scores(q_ref, kwin, sk_ref, valid, kv, e):
    qs = jnp.concatenate([q_ref[:, BLK * (2 * kv):BLK * (2 * kv) + BLK],
                          q_ref[:, BLK * (2 * kv + 1):BLK * (2 * kv + 1) + BLK]], axis=0)
    s = lax.dot_general(qs, kwin, (((1,), (1,)), ((), ())), preferred_element_type=F32)
    h0 = 4 * kv + e
    row = lax.broadcasted_iota(jnp.int32, (2 * BLK, 1), 0)
    sink = jnp.where(row < BLK, sk_ref[:, h0:h0 + 1], sk_ref[:, h0 + 2:h0 + 3])
    return qs, jnp.where(valid, s, NEG), sink


def _attn_specs(nb):
    prev = lambda i: jnp.maximum(i - 1, 0)
    zero = lambda i: 0
    kv_specs = [_row_spec(BLK, KVX_W, zero), _row_spec(BLK, KVX_W, prev), _row_spec(BLK, KVX_W)]
    return kv_specs


def _attn_fwd(qn, kf, vf, sinks):
    l_dim = qn.shape[0]
    nb = l_dim // BLK

    def body(q_ref, km_ref, kp_ref, kc_ref, vm_ref, vp_ref, vc_ref, sk_ref, o_ref, lse_ref):
        valid = _attn_valid(pl.program_id(0))
        lane = lax.broadcasted_iota(jnp.int32, (BLK, BLK), 1)
        lse_all = jnp.zeros((BLK, BLK), F32)
        for kv in range(N_KV_HEADS):
            outs = []
            for e in range(2):
                sl = slice(BLK * (2 * kv + e), BLK * (2 * kv + e) + BLK)
                kwin = jnp.concatenate([km_ref[:, sl], kp_ref[:, sl], kc_ref[:, sl]], axis=0)
                vwin = jnp.concatenate([vm_ref[:, sl], vp_ref[:, sl], vc_ref[:, sl]], axis=0)
                _, s, sink = _attn_scores(q_ref, kwin, sk_ref, valid, kv, e)
                m = jnp.maximum(jnp.max(s, axis=-1, keepdims=True), sink)
                ex = jnp.exp(s - m)
                den = jnp.sum(ex, axis=-1, keepdims=True) + jnp.exp(sink - m)
                p = ex * (1.0 / den)
                outs.append(jnp.dot(p.astype(BF16), vwin, preferred_element_type=F32))
                lse = m + jnp.log(den)
                lse_all = jnp.where(lane == 4 * kv + e, lse[:BLK], lse_all)
                lse_all = jnp.where(lane == 4 * kv + 2 + e, lse[BLK:], lse_all)
            o = outs[0] + outs[1]
            o_ref[:, BLK * (2 * kv):BLK * (2 * kv) + BLK] = o[:BLK]
            o_ref[:, BLK * (2 * kv + 1):BLK * (2 * kv + 1) + BLK] = o[BLK:]
        lse_ref[...] = lse_all

    kv_specs = _attn_specs(nb)
    return pl.pallas_call(
        body, name="attn_fwd", grid=(nb,),
        in_specs=[_row_spec(BLK, Q_W)] + kv_specs + kv_specs + [_full_spec((1, N_Q_HEADS))],
        out_specs=[_row_spec(BLK, Q_W), _row_spec(BLK, BLK)],
        out_shape=[jax.ShapeDtypeStruct((l_dim, Q_W), F32), jax.ShapeDtypeStruct((l_dim, BLK), F32)],
        compiler_params=_params(),
    )(qn, kf, kf, kf, vf, vf, vf, sinks)


def _attn_bwd(qn, kf, vf, sinks, lse, attn, dattn):
    l_dim = qn.shape[0]
    nb = l_dim // BLK
    wide = KV_W
    tn = (((0,), (0,)), ((), ()))
    nt = (((1,), (1,)), ((), ()))

    def body(q_ref, km_ref, kp_ref, kc_ref, vm_ref, vp_ref, vc_ref, sk_ref, lse_ref, o_ref, do_ref,
             dq_ref, dkc_ref, dkp_ref, dkm_ref, dvc_ref, dvp_ref, dvm_ref, dsk_ref):
        @pl.when(pl.program_id(0) == 0)
        def _():
            dkm_ref[...] = jnp.zeros_like(dkm_ref)
            dvm_ref[...] = jnp.zeros_like(dvm_ref)
            dsk_ref[...] = jnp.zeros_like(dsk_ref)

        valid = _attn_valid(pl.program_id(0))
        half = _lane_half((BLK, BLK))
        half2 = _lane_half((2 * BLK, BLK))
        half3 = _lane_half((3 * BLK, BLK))
        lane16 = lax.broadcasted_iota(jnp.int32, (1, N_Q_HEADS), 1)
        dsk = jnp.zeros((1, N_Q_HEADS), F32)
        folded_k, folded_v = [], []
        for kv in range(N_KV_HEADS):
            j0, j1 = 2 * kv, 2 * kv + 1
            do0 = do_ref[:, BLK * j0:BLK * j0 + BLK]
            do1 = do_ref[:, BLK * j1:BLK * j1 + BLK]
            do0f, do1f = do0.astype(F32), do1.astype(F32)
            prod0 = do0f * o_ref[:, BLK * j0:BLK * j0 + BLK]
            prod1 = do1f * o_ref[:, BLK * j1:BLK * j1 + BLK]
            dos = jnp.concatenate([do0, do1], axis=0)
            dqs, dks, dvs = [], [], []
            for e in range(2):
                sl = slice(BLK * (2 * kv + e), BLK * (2 * kv + e) + BLK)
                kwin = jnp.concatenate([km_ref[:, sl], kp_ref[:, sl], kc_ref[:, sl]], axis=0)
                vwin = jnp.concatenate([vm_ref[:, sl], vp_ref[:, sl], vc_ref[:, sl]], axis=0)
                qs, s, sink = _attn_scores(q_ref, kwin, sk_ref, valid, kv, e)
                h0 = 4 * kv + e
                lse_rows = jnp.concatenate([lse_ref[:, h0:h0 + 1], lse_ref[:, h0 + 2:h0 + 3]], axis=0)
                p = jnp.exp(s - lse_rows)
                p_sink = jnp.exp(sink - lse_rows)
                delta = jnp.concatenate(
                    [jnp.sum(jnp.where(half == e, prod0, 0.0), axis=-1, keepdims=True),
                     jnp.sum(jnp.where(half == e, prod1, 0.0), axis=-1, keepdims=True)], axis=0)
                dp = lax.dot_general(dos, vwin, nt, preferred_element_type=F32)
                ds = (p * (dp - delta)).astype(BF16)
                pb = p.astype(BF16)
                dqs.append(jnp.dot(ds, kwin, preferred_element_type=F32))
                dks.append(lax.dot_general(ds, qs, tn, preferred_element_type=F32))
                dvs.append(lax.dot_general(pb, dos, tn, preferred_element_type=F32))
                sink_g = -(p_sink * delta)
                g_lo = jnp.sum(sink_g[:BLK], axis=0, keepdims=True)
                g_hi = jnp.sum(sink_g[BLK:], axis=0, keepdims=True)
                dsk = dsk + jnp.where(lane16 == h0, g_lo, 0.0) + jnp.where(lane16 == h0 + 2, g_hi, 0.0)
            dq = jnp.where(half2 == 0, dqs[0], dqs[1])
            dq_ref[:, BLK * j0:BLK * j0 + BLK] = dq[:BLK].astype(BF16)
            dq_ref[:, BLK * j1:BLK * j1 + BLK] = dq[BLK:].astype(BF16)
            own = kv % 2
            folded_k.append(dks[own] + pltpu.roll(dks[1 - own], HEAD_DIM, axis=1))
            folded_v.append(dvs[own] + pltpu.roll(dvs[1 - own], HEAD_DIM, axis=1))
            if own == 1:
                cols = slice(BLK * (kv // 2), BLK * (kv // 2) + BLK)
                for folded, m_ref, p_ref, c_ref in ((folded_k, dkm_ref, dkp_ref, dkc_ref),
                                                    (folded_v, dvm_ref, dvp_ref, dvc_ref)):
                    both = jnp.where(half3 == 0, folded[0], folded[1])
                    m_ref[:, cols] += both[:BLK]
                    p_ref[:, cols] = both[BLK:2 * BLK].astype(BF16)
                    c_ref[:, cols] = both[2 * BLK:].astype(BF16)
                folded_k, folded_v = [], []
        dsk_ref[...] += dsk

    kv_specs = _attn_specs(nb)
    row_wide = _row_spec(BLK, wide)
    acc_wide = _full_spec((BLK, wide))
    big = jax.ShapeDtypeStruct((l_dim, wide), BF16)
    return pl.pallas_call(
        body, name="attn_bwd", grid=(nb,),
        in_specs=[_row_spec(BLK, Q_W)] + kv_specs + kv_specs
        + [_full_spec((1, N_Q_HEADS)), _row_spec(BLK, BLK), _row_spec(BLK, Q_W), _row_spec(BLK, Q_W)],
        out_specs=[_row_spec(BLK, Q_W), row_wide, row_wide, acc_wide, row_wide, row_wide, acc_wide,
                   _full_spec((1, N_Q_HEADS))],
        out_shape=[jax.ShapeDtypeStruct((l_dim, Q_W), BF16), big, big, jax.ShapeDtypeStruct((BLK, wide), F32),
                   big, big, jax.ShapeDtypeStruct((BLK, wide), F32), jax.ShapeDtypeStruct((1, N_Q_HEADS), F32)],
        compiler_params=_params(),
    )(qn, kf, kf, kf, vf, vf, vf, sinks, lse, attn, dattn)


def _qk_bwd(qkv, q_norm_t, k_norm_t, e_mat, dq, dkc, dkp, dkm, dvc, dvp, dvm, dproj):
    l_dim = qkv.shape[0]
    nb = l_dim // BLK
    wide = KV_W

    def body(x_ref, qg_ref, kg_ref, e_ref, dq_ref, dkc_ref, dkp_ref, dkm_ref, dvc_ref, dvp_ref, dvm_ref,
             dproj_ref, o_ref, dqg_ref, dkg_ref):
        i = pl.program_id(0)

        @pl.when(i == 0)
        def _():
            dqg_ref[...] = jnp.zeros_like(dqg_ref)
            dkg_ref[...] = jnp.zeros_like(dkg_ref)

        first = jnp.where(i == 0, 1.0, 0.0)
        not_last = jnp.where(i < nb - 1, 1.0, 0.0)
        dk_x = dkc_ref[...].astype(F32) + not_last * dkp_ref[...].astype(F32) + first * dkm_ref[...]
        dv_x = dvc_ref[...].astype(F32) + not_last * dvp_ref[...].astype(F32) + first * dvm_ref[...]
        x = x_ref[...]
        dqx, dqg = _head_rms_bwd(x[:, :Q_W], qg_ref[...], dq_ref[...].astype(F32) * (HEAD_DIM ** -0.5), e_ref)
        dkx, dkg = _head_rms_bwd(x[:, Q_W:Q_W + KV_W], kg_ref[...], dk_x, e_ref)
        o_ref[:, :Q_W] = dqx.astype(BF16)
        o_ref[:, Q_W:Q_W + KV_W] = dkx.astype(BF16)
        o_ref[:, Q_W + KV_W:] = dv_x.astype(BF16)
        dqg_ref[...] += dqg
        dkg_ref[...] += dkg

    nxt = lambda i: jnp.minimum(i + 1, nb - 1)
    row_wide = _row_spec(BLK, wide)
    nxt_wide = _row_spec(BLK, wide, nxt)
    acc_wide = _full_spec((BLK, wide))
    qkv_w = Q_W + 2 * KV_W
    in_specs = [_row_spec(BLK, qkv_w), _full_spec((1, Q_W)), _full_spec((1, KV_W)),
                _full_spec((256, 256)), _row_spec(BLK, Q_W),
                row_wide, nxt_wide, acc_wide, row_wide, nxt_wide, acc_wide, pl.BlockSpec(memory_space=pl.ANY)]
    return pl.pallas_call(
        body, name="qk_bwd", grid=(nb,),
        in_specs=in_specs,
        out_specs=[pl.BlockSpec((BLK, qkv_w), lambda i: (i, 3 * D_MODEL // qkv_w)),
                   _full_spec((1, Q_W)), _full_spec((1, KV_W))],
        input_output_aliases={len(in_specs) - 1: 0},
        out_shape=[jax.ShapeDtypeStruct(dproj.shape, BF16),
                   jax.ShapeDtypeStruct((1, Q_W), F32), jax.ShapeDtypeStruct((1, KV_W), F32)],
        compiler_params=_params(),
    )(qkv, q_norm_t, k_norm_t, e_mat, dq, dkc, dkp, dkm, dvc, dvp, dvm, dproj)


GRP = 8


def _strided(r, g):
    return pl.ds(r, g, stride=GRP)


def _lane_slab(ref, i, r, g):
    return ref[i, _strided(r, g), :]


def _chunk_carries(xr_ref, xi_ref, i, ar, ai, sqr_ref, sqi_ref, seq_r, seq_i, cin_r, cin_i, sign, reverse):
    g = xr_ref.shape[1] // GRP
    sr = si = None
    for r in (range(GRP - 1, -1, -1) if reverse else range(GRP)):
        xr, xi = _lane_slab(xr_ref, i, r, g), _lane_slab(xi_ref, i, r, g)
        if sr is not None:
            xr, xi = xr + ar * sr - ai * si, xi + ar * si + ai * sr
        sr, si = xr, xi
    row = lax.broadcasted_iota(jnp.int32, sr.shape, 0)
    idx, s = 0, 1
    while s < g:
        br = sqr_ref[idx:idx + 1, BLK * i:BLK * i + BLK]
        bi = sign * sqi_ref[idx:idx + 1, BLK * i:BLK * i + BLK]
        shift, keep = (g - s, row < g - s) if reverse else (s, row >= s)
        pr = jnp.where(keep, pltpu.roll(sr, shift, axis=0), 0.0)
        pi = jnp.where(keep, pltpu.roll(si, shift, axis=0), 0.0)
        sr, si = sr + br * pr - bi * pi, si + br * pi + bi * pr
        idx, s = idx + 1, 2 * s
    return sr + seq_r * cin_r - seq_i * cin_i, si + seq_r * cin_i + seq_i * cin_r


def _ssm_fwd_kb(u, wb_re, wb_im, wc_re, wc_im, d_skip, tabs, q):
    l_dim = u.shape[0]
    nc = l_dim // q
    g = q // GRP

    def body(u_ref, wbr_ref, wbi_ref, wcr_ref, wci_ref, d_ref, a1r_ref, a1i_ref, sqr_ref, sqi_ref,
             seqr_ref, seqi_ref, y_ref, z_ref, sr_ref, si_ref, cr_ref, ci_ref, xr_ref, xi_ref):
        @pl.when(pl.program_id(1) == 0)
        def _():
            cr_ref[...] = jnp.zeros_like(cr_ref)
            ci_ref[...] = jnp.zeros_like(ci_ref)

        u_kb = u_ref[...]
        ub = u_kb.astype(BF16)
        xr = jnp.dot(ub, wbr_ref[0], preferred_element_type=F32)
        xi = jnp.dot(ub, wbi_ref[0], preferred_element_type=F32)
        for i in range(LB_KB):
            xr_ref[i] = xr[:, BLK * i:BLK * i + BLK]
            xi_ref[i] = xi[:, BLK * i:BLK * i + BLK]
        row = lax.broadcasted_iota(jnp.int32, (g, BLK), 0)
        for i in range(LB_KB):
            lanes = slice(BLK * i, BLK * i + BLK)
            ar, ai = a1r_ref[0:1, lanes], a1i_ref[0:1, lanes]
            cin_r, cin_i = cr_ref[0:1, lanes], ci_ref[0:1, lanes]
            tr, ti = _chunk_carries(xr_ref, xi_ref, i, ar, ai, sqr_ref, sqi_ref, seqr_ref[:, lanes],
                                    seqi_ref[:, lanes], cin_r, cin_i, 1.0, reverse=False)
            cr_ref[0:1, lanes] = jnp.sum(jnp.where(row == g - 1, tr, 0.0), axis=0, keepdims=True)
            ci_ref[0:1, lanes] = jnp.sum(jnp.where(row == g - 1, ti, 0.0), axis=0, keepdims=True)
            pr = jnp.where(row == 0, cin_r, pltpu.roll(tr, 1, axis=0))
            pi = jnp.where(row == 0, cin_i, pltpu.roll(ti, 1, axis=0))
            for r in range(GRP):
                pr, pi = (_lane_slab(xr_ref, i, r, g) + ar * pr - ai * pi,
                          _lane_slab(xi_ref, i, r, g) + ar * pi + ai * pr)
                sr_ref[i, _strided(r, g), :] = pr
                si_ref[i, _strided(r, g), :] = pi
        s_r = jnp.concatenate([sr_ref[i] for i in range(LB_KB)], axis=1)
        s_i = jnp.concatenate([si_ref[i] for i in range(LB_KB)], axis=1)
        y = (jnp.dot(s_r.astype(BF16), wcr_ref[0], preferred_element_type=F32)
             - jnp.dot(s_i.astype(BF16), wci_ref[0], preferred_element_type=F32)
             + d_ref[...] * u_kb)
        y_ref[...] = y.astype(BF16)
        z_ref[...] = _gelu(y).astype(BF16)

    chan = pl.BlockSpec((q, BLK), lambda k, c: (c, k))
    wb_spec = pl.BlockSpec((1, BLK, ST_KB), lambda k, c: (k, 0, 0))
    wc_spec = pl.BlockSpec((1, ST_KB, BLK), lambda k, c: (k, 0, 0))
    tab_specs = [pl.BlockSpec((t.shape[0], ST_KB), lambda k, c: (0, k)) for t in tabs[:6]]
    state_spec = pl.BlockSpec((LB_KB, q, BLK), lambda k, c: (k, c, 0))
    state_shape = jax.ShapeDtypeStruct((N_LB, l_dim, BLK), F32)
    return pl.pallas_call(
        body, name="ssm_fwd", grid=(SSM_KB, nc),
        in_specs=[chan, wb_spec, wb_spec, wc_spec, wc_spec, pl.BlockSpec((1, BLK), lambda k, c: (0, k))] + tab_specs,
        out_specs=[chan, chan, state_spec, state_spec],
        out_shape=[jax.ShapeDtypeStruct((l_dim, D_MODEL), BF16), jax.ShapeDtypeStruct((l_dim, D_MODEL), BF16),
                   state_shape, state_shape],
        scratch_shapes=[pltpu.VMEM((8, ST_KB), F32), pltpu.VMEM((8, ST_KB), F32),
                        pltpu.VMEM((LB_KB, q, BLK), F32), pltpu.VMEM((LB_KB, q, BLK), F32)],
        compiler_params=_params(dimension_semantics=("parallel", "arbitrary")),
    )(u, wb_re, wb_im, wc_re, wc_im, d_skip, *tabs[:6])


def _ssm_bwd_kb(dz, y, u, s_re, s_im, wb_re, wb_im, wc_re, wc_im, d_skip, tabs, q, dproj):
    l_dim = u.shape[0]
    nc = l_dim // q
    g = q // GRP

    def body(dz_ref, y_ref, u_ref, sr_ref, si_ref, wbr_ref, wbi_ref, wcr_ref, wci_ref, d_ref,
             a1r_ref, a1i_ref, sqr_ref, sqi_ref, revr_ref, revi_ref, dproj_ref,
             du_ref, dd_ref, dar_ref, dai_ref, dwbr_ref, dwbi_ref, dwcr_ref, dwci_ref,
             cr_ref, ci_ref, gr_ref, gi_ref):
        @pl.when(pl.program_id(1) == 0)
        def _():
            for ref in (cr_ref, ci_ref, dd_ref, dar_ref, dai_ref, dwbr_ref, dwbi_ref, dwcr_ref, dwci_ref):
                ref[...] = jnp.zeros_like(ref)

        tn = (((0,), (0,)), ((), ()))
        nt = (((1,), (1,)), ((), ()))
        u_kb = u_ref[...]
        dy = dz_ref[...].astype(F32) * _gelu_grad(y_ref[...].astype(F32))
        dyb = dy.astype(BF16)
        ub = u_kb.astype(BF16)
        dd_ref[...] += jnp.sum(dy * u_kb, axis=0, keepdims=True)
        ds_r = lax.dot_general(dyb, wcr_ref[0], nt, preferred_element_type=F32)
        ds_i = -lax.dot_general(dyb, wci_ref[0], nt, preferred_element_type=F32)
        for i in range(LB_KB):
            gr_ref[i] = ds_r[:, BLK * i:BLK * i + BLK]
            gi_ref[i] = ds_i[:, BLK * i:BLK * i + BLK]
        row = lax.broadcasted_iota(jnp.int32, (g, BLK), 0)
        for i in range(LB_KB):
            lanes = slice(BLK * i, BLK * i + BLK)
            ar, ai = a1r_ref[0:1, lanes], -a1i_ref[0:1, lanes]
            cin_r, cin_i = cr_ref[0:1, lanes], ci_ref[0:1, lanes]
            tr, ti = _chunk_carries(gr_ref, gi_ref, i, ar, ai, sqr_ref, sqi_ref, revr_ref[:, lanes],
                                    -revi_ref[:, lanes], cin_r, cin_i, -1.0, reverse=True)
            cr_ref[0:1, lanes] = jnp.sum(jnp.where(row == 0, tr, 0.0), axis=0, keepdims=True)
            ci_ref[0:1, lanes] = jnp.sum(jnp.where(row == 0, ti, 0.0), axis=0, keepdims=True)
            nr = jnp.where(row == g - 1, cin_r, pltpu.roll(tr, g - 1, axis=0))
            ni = jnp.where(row == g - 1, cin_i, pltpu.roll(ti, g - 1, axis=0))
            acc_r = jnp.zeros((g, BLK), F32)
            acc_i = jnp.zeros((g, BLK), F32)
            for r in range(GRP - 1, -1, -1):
                s_r, s_i = _lane_slab(sr_ref, i, r, g), _lane_slab(si_ref, i, r, g)
                acc_r = acc_r + (nr * s_r + ni * s_i)
                acc_i = acc_i + (ni * s_r - nr * s_i)
                nr, ni = (_lane_slab(gr_ref, i, r, g) + ar * nr - ai * ni,
                          _lane_slab(gi_ref, i, r, g) + ar * ni + ai * nr)
                gr_ref[i, _strided(r, g), :] = nr
                gi_ref[i, _strided(r, g), :] = ni
            dar_ref[:, lanes] += jnp.sum(acc_r, axis=0, keepdims=True)
            dai_ref[:, lanes] += jnp.sum(acc_i, axis=0, keepdims=True)
        grb = jnp.concatenate([gr_ref[i] for i in range(LB_KB)], axis=1).astype(BF16)
        gib = jnp.concatenate([gi_ref[i] for i in range(LB_KB)], axis=1).astype(BF16)
        srb = jnp.concatenate([sr_ref[i] for i in range(LB_KB)], axis=1).astype(BF16)
        sib = jnp.concatenate([si_ref[i] for i in range(LB_KB)], axis=1).astype(BF16)
        du = (lax.dot_general(grb, wbr_ref[0], nt, preferred_element_type=F32)
              + lax.dot_general(gib, wbi_ref[0], nt, preferred_element_type=F32)
              + d_ref[...] * dy)
        du_ref[...] = du.astype(BF16)
        dwbr_ref[0] += lax.dot_general(ub, grb, tn, preferred_element_type=F32)
        dwbi_ref[0] += lax.dot_general(ub, gib, tn, preferred_element_type=F32)
        dwcr_ref[0] += lax.dot_general(srb, dyb, tn, preferred_element_type=F32)
        dwci_ref[0] -= lax.dot_general(sib, dyb, tn, preferred_element_type=F32)

    chan = pl.BlockSpec((q, BLK), lambda k, c: (nc - 1 - c, k))
    wb_spec = pl.BlockSpec((1, BLK, ST_KB), lambda k, c: (k, 0, 0))
    wc_spec = pl.BlockSpec((1, ST_KB, BLK), lambda k, c: (k, 0, 0))
    tab_in = [tabs[0], tabs[1], tabs[2], tabs[3], tabs[6], tabs[7]]
    tab_specs = [pl.BlockSpec((t.shape[0], ST_KB), lambda k, c: (0, k)) for t in tab_in]
    vec = pl.BlockSpec((1, BLK), lambda k, c: (0, k))
    svec = pl.BlockSpec((1, ST_KB), lambda k, c: (0, k))
    state_spec = pl.BlockSpec((LB_KB, q, BLK), lambda k, c: (k, nc - 1 - c, 0))
    du_spec = pl.BlockSpec((q, BLK), lambda k, c: (nc - 1 - c, 2 * D_MODEL // BLK + k))
    in_specs = ([chan, chan, chan, state_spec, state_spec, wb_spec, wb_spec, wc_spec, wc_spec, vec] + tab_specs
                + [pl.BlockSpec(memory_space=pl.ANY)])
    return pl.pallas_call(
        body, name="ssm_bwd", grid=(SSM_KB, nc),
        in_specs=in_specs,
        out_specs=[du_spec, vec, svec, svec, wb_spec, wb_spec, wc_spec, wc_spec],
        input_output_aliases={len(in_specs) - 1: 0},
        out_shape=[jax.ShapeDtypeStruct(dproj.shape, BF16), jax.ShapeDtypeStruct((1, D_MODEL), F32),
                   jax.ShapeDtypeStruct((1, N_STATE), F32), jax.ShapeDtypeStruct((1, N_STATE), F32),
                   jax.ShapeDtypeStruct((SSM_KB, BLK, ST_KB), F32), jax.ShapeDtypeStruct((SSM_KB, BLK, ST_KB), F32),
                   jax.ShapeDtypeStruct((SSM_KB, ST_KB, BLK), F32), jax.ShapeDtypeStruct((SSM_KB, ST_KB, BLK), F32)],
        scratch_shapes=[pltpu.VMEM((8, ST_KB), F32), pltpu.VMEM((8, ST_KB), F32),
                        pltpu.VMEM((LB_KB, q, BLK), F32), pltpu.VMEM((LB_KB, q, BLK), F32)],
        compiler_params=_params(dimension_semantics=("parallel", "arbitrary")),
    )(dz, y, u, s_re, s_im, wb_re, wb_im, wc_re, wc_im, d_skip, *tab_in, dproj)


def _discretize(lam_re, lam_im, log_dt, b_re, b_im):
    dt = jnp.exp(log_dt)[:, None]
    mag = jnp.exp(lam_re * dt)
    ar, ai = mag * jnp.cos(lam_im * dt), mag * jnp.sin(lam_im * dt)
    den = lam_re * lam_re + lam_im * lam_im
    nr, ni = ar - 1.0, ai
    fr, fi = (nr * lam_re + ni * lam_im) / den, (ni * lam_re - nr * lam_im) / den
    bbar_re = fr[..., None] * b_re - fi[..., None] * b_im
    bbar_im = fr[..., None] * b_im + fi[..., None] * b_re
    return ar, ai, bbar_re, bbar_im


def _block_diag_b(bbar):
    eye = jnp.eye(8, dtype=bbar.dtype)
    return jnp.einsum("kgpc,gh->kgchp", bbar.reshape(8, 8, SSM_STATE, SSM_GROUP_CH), eye).reshape(8, BLK, ST_KB)


def _block_diag_b_t(dwb):
    eye = jnp.eye(8, dtype=dwb.dtype)
    return jnp.einsum("kgchp,gh->kgpc", dwb.reshape(8, 8, SSM_GROUP_CH, 8, SSM_STATE), eye).reshape(
        SSM_GROUPS, SSM_STATE, SSM_GROUP_CH)


def _block_diag_c(c):
    eye = jnp.eye(8, dtype=c.dtype)
    return jnp.einsum("kgcp,gh->kgphc", c.reshape(8, 8, SSM_GROUP_CH, SSM_STATE), eye).reshape(8, ST_KB, BLK)


def _block_diag_c_t(dwc):
    eye = jnp.eye(8, dtype=dwc.dtype)
    return jnp.einsum("kgphc,gh->kgcp", dwc.reshape(8, 8, SSM_STATE, 8, SSM_GROUP_CH), eye).reshape(
        SSM_GROUPS, SSM_GROUP_CH, SSM_STATE)


def _powers(br, bi, n):
    pr, pi = br, bi
    cr, ci = br, bi
    while pr.shape[0] < n:
        pr, pi = (jnp.concatenate([pr, pr * cr - pi * ci], axis=0),
                  jnp.concatenate([pi, pr * ci + pi * cr], axis=0))
        cr, ci = cr * cr - ci * ci, 2.0 * cr * ci
    return pr[:n], pi[:n]


def _powers_desc(br, bi, n):
    pr, pi = br, bi
    cr, ci = br, bi
    while pr.shape[0] < n:
        pr, pi = (jnp.concatenate([pr * cr - pi * ci, pr], axis=0),
                  jnp.concatenate([pr * ci + pi * cr, pi], axis=0))
        cr, ci = cr * cr - ci * ci, 2.0 * cr * ci
    return pr, pi


def _power_tables(ar, ai, g):
    a1r, a1i = _powers(ar, ai, GRP)
    seqr, seqi = _powers(a1r[GRP - 1:], a1i[GRP - 1:], g)
    g2 = 1 << (g - 1).bit_length()
    revr, revi = _powers_desc(a1r[GRP - 1:], a1i[GRP - 1:], g2)
    revr, revi = revr[g2 - g:], revi[g2 - g:]
    sq_r, sq_i = [seqr[0:1]], [seqi[0:1]]
    while len(sq_r) < 8:
        r, i = sq_r[-1], sq_i[-1]
        sq_r.append(r * r - i * i)
        sq_i.append(2.0 * r * i)
    sqr, sqi = jnp.concatenate(sq_r, axis=0), jnp.concatenate(sq_i, axis=0)
    return a1r, a1i, sqr, sqi, seqr, seqi, revr, revi


HBM_SPEC = pl.BlockSpec(memory_space=pltpu.HBM)
SEM_SPEC = pl.BlockSpec(memory_space=pltpu.SEMAPHORE)
DATAFLOW = pltpu.SideEffectType.DATAFLOW_SIDE_EFFECTING


def _plain_rows(p, m):
    return p * m


def _ffn_in_rows(p, m):
    return ((p & 3) >> 1) * (4 * m) + (p >> 2) * (2 * m) + (p & 1) * m


def _peer_copies(src_refs, land_refs, send_sems, recv_sems, chunked, row_fns):
    x, y, c = lax.axis_index("x"), lax.axis_index("y"), lax.axis_index("c")
    me = 4 * x + 2 * y + c
    copies = []
    for a, (src, land) in enumerate(zip(src_refs, land_refs)):
        m = land.shape[0] // N_DEV
        for k in range(N_DEV - 1):
            rel = k + 1
            bx, by, bc = (rel >> 2) & 1, (rel >> 1) & 1, rel & 1
            peer = (x + bx - 2 * x * bx, y + by - 2 * y * by, c + bc - 2 * c * bc)
            p_idx = 4 * peer[0] + 2 * peer[1] + peer[2]
            copies.append(pltpu.make_async_remote_copy(
                src_ref=src.at[pl.ds(row_fns[a](p_idx, m), m), :] if chunked else src,
                dst_ref=land.at[pl.ds(me * m if chunked else row_fns[a](me, m), m), :],
                send_sem=send_sems[a * (N_DEV - 1) + k], recv_sem=recv_sems[a * (N_DEV - 1) + k],
                device_id=peer, device_id_type=MESH))
    return copies


def _send_start(name, srcs, lands, chunked, row_fns=None):
    n = len(srcs)
    ns = n * (N_DEV - 1)
    row_fns = row_fns or [_plain_rows] * n

    def body(*refs):
        src_refs, land_refs = refs[:n], refs[n:2 * n]
        send_sems, recv_sems = refs[2 * n:2 * n + ns], refs[2 * n + ns:2 * n + 2 * ns]
        token = refs[-1]
        for cp in _peer_copies(src_refs, land_refs, send_sems, recv_sems, chunked, row_fns):
            cp.start()
        token[...] = jnp.zeros_like(token)

    ins = [pltpu.with_memory_space_constraint(a, pltpu.HBM) for a in list(srcs) + list(lands)]
    out = pl.pallas_call(
        body, name=name,
        in_specs=[HBM_SPEC] * (2 * n),
        out_specs=[SEM_SPEC] * (2 * ns) + [HBM_SPEC] * (2 * n) + [pl.BlockSpec(memory_space=pltpu.VMEM)],
        out_shape=[pltpu.SemaphoreType.DMA(())] * (2 * ns)
        + [pltpu.HBM(a.shape, a.dtype) for a in list(srcs) + list(lands)]
        + [jax.ShapeDtypeStruct((8, BLK), F32)],
        input_output_aliases={i: i + 2 * ns for i in range(2 * n)},
        compiler_params=pltpu.CompilerParams(has_side_effects=DATAFLOW),
    )(*ins)
    return out[:ns], out[ns:2 * ns], out[2 * ns:2 * ns + n], out[2 * ns + n:2 * ns + 2 * n], out[-1]


def _send_wait(name, send_sems, recv_sems, srcs, lands, after, chunked, row_fns=None):
    n = len(srcs)
    ns = n * (N_DEV - 1)
    row_fns = row_fns or [_plain_rows] * n

    def body(*refs):
        src_refs, land_refs = refs[:n], refs[n:2 * n]
        s_sems, r_sems = refs[2 * n:2 * n + ns], refs[2 * n + ns:2 * n + 2 * ns]
        copies = _peer_copies(src_refs, land_refs, s_sems, r_sems, chunked, row_fns)
        for cp in copies:
            cp.wait_send()
        for cp in copies:
            cp.wait_recv()

    out = pl.pallas_call(
        body, name=name,
        in_specs=[HBM_SPEC] * (2 * n) + [SEM_SPEC] * (2 * ns) + [pl.BlockSpec(memory_space=pl.ANY)],
        out_specs=[HBM_SPEC] * (2 * n),
        out_shape=[pltpu.HBM(a.shape, a.dtype) for a in list(srcs) + list(lands)],
        input_output_aliases={i: i for i in range(2 * n)},
        compiler_params=pltpu.CompilerParams(has_side_effects=DATAFLOW),
    )(*srcs, *lands, *send_sems, *recv_sems, after)
    return out[n:]


def _sum_slots(name, recv, own):
    m, ncol = own.shape
    tr = m // 2 if (m // 2) % 16 == 0 else m
    g = m // tr

    def body(*refs):
        slots, own_ref, o_ref = refs[:N_DEV], refs[N_DEV], refs[N_DEV + 1]
        me = _my_index()
        tot = None
        for s in range(N_DEV):
            v = jnp.where(me == s, own_ref[...], slots[s][...].astype(F32))
            tot = v if tot is None else tot + v
        o_ref[...] = tot

    def slot_spec(s):
        return pl.BlockSpec((tr, ncol), lambda i: (s * g + i, 0))

    return pl.pallas_call(
        body, name=name, grid=(g,),
        in_specs=[slot_spec(s) for s in range(N_DEV)] + [pl.BlockSpec((tr, ncol), lambda i: (i, 0))],
        out_specs=pl.BlockSpec((tr, ncol), lambda i: (i, 0)),
        out_shape=jax.ShapeDtypeStruct((m, ncol), F32),
        compiler_params=_params(),
    )(*([recv] * N_DEV), own)


def _sum_gathered(name, gathered, rows):
    tr = _pick(rows, 512, 8)
    g = rows // tr

    def body(*refs):
        o_ref = refs[N_DEV]
        tot = refs[0][...]
        for s in range(1, N_DEV):
            tot = tot + refs[s][...]
        o_ref[...] = tot

    return pl.pallas_call(
        body, name=name, grid=(g,),
        in_specs=[pl.BlockSpec((tr, BLK), (lambda i, s=s: (s * g + i, 0))) for s in range(N_DEV)],
        out_specs=pl.BlockSpec((tr, BLK), lambda i: (i, 0)),
        out_shape=jax.ShapeDtypeStruct((rows, BLK), F32),
        compiler_params=_params(),
    )(*([gathered] * N_DEV))


def _adamw(name, w, g, m, v):
    r, c = w.shape
    tr = _pick(r, 256, 8) if r % 8 == 0 else r
    c1 = 1.0 - ADAM_B1 ** ADAM_STEP
    c2 = 1.0 - ADAM_B2 ** ADAM_STEP

    def body(w_ref, g_ref, m_ref, v_ref, d_ref, nm_ref, nv_ref):
        gv = g_ref[...]
        nm = ADAM_B1 * m_ref[...] + (1.0 - ADAM_B1) * gv
        nv = ADAM_B2 * v_ref[...] + (1.0 - ADAM_B2) * (gv * gv)
        m_hat = nm / c1
        v_hat = nv / c2
        d_ref[...] = -ADAM_LR * (m_hat / (jnp.sqrt(v_hat) + ADAM_EPS) + ADAM_WD * w_ref[...])
        nm_ref[...] = nm
        nv_ref[...] = nv

    spec = pl.BlockSpec((tr, c), lambda i: (i, 0))
    shape = jax.ShapeDtypeStruct((r, c), F32)
    return pl.pallas_call(
        body, name=name, grid=(r // tr,),
        in_specs=[spec] * 4, out_specs=[spec] * 3, out_shape=[shape] * 3,
        compiler_params=_params(),
    )(w, g, m, v)


def _adamw_many(name, ws, gs, ms, vs):
    n = len(ws)
    c1 = 1.0 - ADAM_B1 ** ADAM_STEP
    c2 = 1.0 - ADAM_B2 ** ADAM_STEP

    def body(*refs):
        for a in range(n):
            w_ref, g_ref, m_ref, v_ref = refs[a], refs[n + a], refs[2 * n + a], refs[3 * n + a]
            d_ref, nm_ref, nv_ref = refs[4 * n + a], refs[5 * n + a], refs[6 * n + a]
            gv = g_ref[...]
            nm = ADAM_B1 * m_ref[...] + (1.0 - ADAM_B1) * gv
            nv = ADAM_B2 * v_ref[...] + (1.0 - ADAM_B2) * (gv * gv)
            d_ref[...] = -ADAM_LR * ((nm / c1) / (jnp.sqrt(nv / c2) + ADAM_EPS) + ADAM_WD * w_ref[...])
            nm_ref[...] = nm
            nv_ref[...] = nv

    vmem = pl.BlockSpec(memory_space=pltpu.VMEM)
    shapes = [jax.ShapeDtypeStruct(w.shape, F32) for w in ws]
    out = pl.pallas_call(
        body, name=name,
        in_specs=[vmem] * (4 * n), out_specs=[vmem] * (3 * n), out_shape=shapes * 3,
        compiler_params=_params(),
    )(*ws, *gs, *ms, *vs)
    return out[:n], out[n:2 * n], out[2 * n:]


PACK_ROWS = 128


def _pack(parts):
    flat = []
    for p in parts:
        v = p.reshape(-1)
        flat.append(jnp.pad(v, (0, (-v.shape[0]) % BLK)))
    v = jnp.concatenate(flat)
    v = jnp.pad(v, (0, (-v.shape[0]) % (PACK_ROWS * BLK)))
    return v.reshape(-1, BLK)


def _unpack(buf, shapes):
    flat = buf.reshape(-1)
    out, off = [], 0
    for shp in shapes:
        size = math.prod(shp)
        out.append(flat[off:off + size].reshape(shp))
        off += size + (-size) % BLK
    return out


def kernel(x, meta_tokens, norm_mix, w_in, q_norm, k_norm, attn_sinks, lam_re, lam_im, log_dt, ssm_b_re, ssm_b_im, ssm_c_re, ssm_c_im, ssm_d, w_glu, attn_branch_norm, ssm_branch_norm, w_out, norm_ffn, w_ffn_in, w_ffn_out, loss_target, m_meta_tokens, m_norm_mix, m_w_in, m_q_norm, m_k_norm, m_attn_sinks, m_lam_re, m_lam_im, m_log_dt, m_ssm_b_re, m_ssm_b_im, m_ssm_c_re, m_ssm_c_im, m_ssm_d, m_w_glu, m_attn_branch_norm, m_ssm_branch_norm, m_w_out, m_norm_ffn, m_w_ffn_in, m_w_ffn_out, v_meta_tokens, v_norm_mix, v_w_in, v_q_norm, v_k_norm, v_attn_sinks, v_lam_re, v_lam_im, v_log_dt, v_ssm_b_re, v_ssm_b_im, v_ssm_c_re, v_ssm_c_im, v_ssm_d, v_w_glu, v_attn_branch_norm, v_ssm_branch_norm, v_w_out, v_norm_ffn, v_w_ffn_in, v_w_ffn_out):
    args = dict(locals())
    weights = {n: args[n] for n in WEIGHTS}
    mom_m = {n: args["m_" + n] for n in WEIGHTS}
    mom_v = {n: args["v_" + n] for n in WEIGHTS}

    x2d = x[0]
    target2d = loss_target[0]
    s_len = x2d.shape[0]
    l_dim = s_len + BLK
    tm_row = _pick(l_dim, 320)
    tm_mm = _pick(l_dim, 1040)
    tl_tn = _pick(l_dim, 2080)
    tm_ffn = _pick(l_dim, 640)
    tm_big = _pick(l_dim, 2080)
    tm_shift = _pick(l_dim, 640, BLK)

    shard_in = w_in[0].T.astype(BF16)
    shard_glu = w_glu[0].T.astype(BF16)
    shard_out = w_out[0].astype(BF16)
    shard_ffn_in = w_ffn_in[0].T.astype(BF16)
    shard_ffn_out = w_ffn_out[0].astype(BF16)
    shard_meta = meta_tokens.T
    me = _my_index()

    def landing(shard, row_fn=_plain_rows):
        m_rows, cols = shard.shape
        return lax.dynamic_update_slice(lax.empty((N_DEV * m_rows, cols), shard.dtype), shard,
                                        (row_fn(me, m_rows), 0))

    first = [shard_in, shard_meta]
    ga = _send_start("gather_start_a", first, [landing(s) for s in first], chunked=False)
    later = [shard_glu + ga[4][0:1, 0:1].astype(BF16), shard_out, shard_ffn_in, shard_ffn_out]
    later_fns = [_plain_rows, _plain_rows, _ffn_in_rows, _plain_rows]
    gb = _send_start("gather_start_b", later, [landing(s, f) for s, f in zip(later, later_fns)], chunked=False,
                     row_fns=later_fns)

    nm_t = norm_mix + (ga[4][0:1, 0:1] + gb[4][0:1, 0:1])
    qn_t, kn_t = jnp.tile(q_norm, (1, N_Q_HEADS)), jnp.tile(k_norm, (1, N_KV_HEADS))
    e_mat = jnp.kron(jnp.eye(4, dtype=F32), jnp.ones((HEAD_DIM, HEAD_DIM), F32)).astype(BF16)

    def disc(lr, li, ldt, br, bi):
        return _discretize(lr[0], li[0], ldt[0], br[0], bi[0])

    (abar_re, abar_im, bbar_re, bbar_im), disc_vjp = jax.vjp(disc, lam_re, lam_im, log_dt, ssm_b_re, ssm_b_im)
    wb_re, wb_im = _block_diag_b(bbar_re).astype(BF16), _block_diag_b(bbar_im).astype(BF16)
    wc_re, wc_im = _block_diag_c(ssm_c_re[0]).astype(BF16), _block_diag_c(ssm_c_im[0]).astype(BF16)
    q_ssm = _pick(l_dim, 640, 64)
    tabs = _power_tables(abar_re.reshape(1, N_STATE), abar_im.reshape(1, N_STATE), q_ssm // GRP)

    h0, xn = _embed_norm(x2d, nm_t, tm_shift)
    wt_in, meta_t = _send_wait("gather_wait_a", ga[0], ga[1], ga[2], ga[3], xn, chunked=False)
    meta_pad = jnp.pad(meta_t.T, ((PAD, 0), (0, 0)))
    h0, xn = _embed_meta(meta_pad, nm_t, h0, xn)
    qkv_w, u_end = Q_W + 2 * KV_W, Q_W + 2 * KV_W + D_MODEL
    wt_in_p = jnp.concatenate([wt_in[u_end:], wt_in[qkv_w:u_end], wt_in[:qkv_w]], axis=0)
    qkv = _matmul("proj_qkv", xn, wt_in_p, nt=True, tm=tm_big, tn=qkv_w, tk=D_MODEL, n=qkv_w, w_off=2)
    u = _matmul("proj_u", xn, wt_in_p, nt=True, tm=tm_big, tn=D_MODEL, tk=D_MODEL, n=D_MODEL, w_off=2)
    gates = _matmul("proj_gates", xn, wt_in_p, nt=True, tm=tm_big, tn=D_MODEL, tk=D_MODEL, n=2 * D_MODEL, w_off=0,
                    out_dtype=BF16)
    qn, kf, vf = _qk_prep(qkv, qn_t, kn_t, e_mat, tm_row)
    attn, lse = _attn_fwd(qn, kf, vf, attn_sinks)
    y, z, s_re, s_im = _ssm_fwd_kb(u, wb_re, wb_im, wc_re, wc_im, ssm_d, tabs, q_ssm)
    wt_glu, w_out_f, wt_ffn_in, w_ffn_out_f = _send_wait("gather_wait_b", gb[0], gb[1], gb[2], gb[3], z,
                                                         chunked=False, row_fns=later_fns)
    zab = _matmul("glu_proj", z, wt_glu, nt=True, tm=tm_big, tn=1024, tk=D_MODEL, out_dtype=BF16)
    merged = _merge_fwd(attn, zab, gates, attn_branch_norm, ssm_branch_norm, tm_row)
    h1, hn = _matmul("out_proj", merged, w_out_f, nt=False, tm=tm_mm, tn=1024, tk=D_MODEL, res=h0,
                     norm_g=norm_ffn)
    gu, act = _ffn_in_swiglu(hn, wt_ffn_in, tm_ffn)
    h2 = _matmul("ffn_out", act, w_ffn_out_f, nt=False, tm=tm_mm, tn=1024, tk=D_FF, res=h1)
    dh2, dh2_b, loss_part = _loss_grad(h2, target2d, tm_shift)

    dgu = _d_act_swiglu(dh2_b, w_ffn_out_f, gu, tm_ffn)

    def exchange_start(name, grads_b, row_fns=None):
        return _send_start(name, grads_b, [jnp.zeros(g.shape, BF16) for g in grads_b], chunked=True,
                           row_fns=row_fns)

    g_ffn_out, g_ffn_out_b = _matmul_tn("g_ffn_out", act, dh2_b, tm=1408, tn=1024, tl=tl_tn)
    g_ffn_in_t, g_ffn_in_b = _matmul_tn("g_ffn_in", dgu, hn, tm=1408, tn=1024, tl=tl_tn)
    ffn_fns = [_ffn_in_rows, _plain_rows]
    ex1 = exchange_start("exchange_start_ffn", [g_ffn_in_b, g_ffn_out_b], ffn_fns)
    dhn = _matmul("d_hn", dgu, wt_ffn_in, nt=False, tm=tm_ffn, tn=1024, tk=2 * D_FF, out_dtype=BF16)
    dh1, dh1_b, g_norm_ffn = _norm_bwd_res("ffn_norm_bwd", h1, norm_ffn + ex1[4][0:1, 0:1], dhn, dh2, tm_row)
    dmerged = _matmul("d_merged", dh1_b, w_out_f, nt=True, tm=tm_big, tn=1024, tk=D_MODEL, out_dtype=BF16)
    dattn, dzab, dproj, g_abn, g_sbn = _merge_bwd(attn, zab, gates, attn_branch_norm, ssm_branch_norm,
                                                    dmerged, tm_row)
    g_out, g_out_b = _matmul_tn("g_out", merged, dh1_b, tm=1024, tn=1024, tl=tl_tn)
    dz = _matmul("d_z", dzab, wt_glu, nt=False, tm=tm_mm, tn=1024, tk=2 * D_MODEL, out_dtype=BF16)
    g_glu_t, g_glu_b = _matmul_tn("g_glu", dzab, z, tm=1024, tn=1024, tl=tl_tn)
    ex2 = exchange_start("exchange_start_mix", [g_glu_b, g_out_b])
    dproj, g_ssm_d, g_ar, g_ai, g_wbr, g_wbi, g_wcr, g_wci = _ssm_bwd_kb(
        dz, y, u, s_re, s_im, wb_re, wb_im, wc_re, wc_im, ssm_d + ex2[4][0:1, 0:1], tabs, q_ssm, dproj)
    dq, dkc, dkp, dkm, dvc, dvp, dvm, g_sinks = _attn_bwd(qn, kf, vf, attn_sinks, lse, attn, dattn)
    dproj, g_qn_t, g_kn_t = _qk_bwd(qkv, qn_t, kn_t, e_mat, dq, dkc, dkp, dkm, dvc, dvp, dvm, dproj)
    g_lam_re, g_lam_im, g_log_dt, g_b_re, g_b_im = disc_vjp(
        (g_ar.reshape(SSM_GROUPS, SSM_STATE), g_ai.reshape(SSM_GROUPS, SSM_STATE),
         _block_diag_b_t(g_wbr), _block_diag_b_t(g_wbi)))
    small_grads = {
        "q_norm": g_qn_t.reshape(N_Q_HEADS, HEAD_DIM).sum(0)[None],
        "k_norm": g_kn_t.reshape(N_KV_HEADS, HEAD_DIM).sum(0)[None], "attn_sinks": g_sinks,
        "lam_re": g_lam_re, "lam_im": g_lam_im, "log_dt": g_log_dt, "ssm_b_re": g_b_re, "ssm_b_im": g_b_im,
        "ssm_c_re": _block_diag_c_t(g_wcr)[None], "ssm_c_im": _block_diag_c_t(g_wci)[None],
        "ssm_d": g_ssm_d, "attn_branch_norm": g_abn, "ssm_branch_norm": g_sbn, "norm_ffn": g_norm_ffn,
    }
    early = [n for n in SMALL if n != "norm_mix"]
    packed_e = _pack([small_grads[n] for n in early])
    gs_e = _send_start("small_start_a", [packed_e], [landing(packed_e)], chunked=False)

    to_front = lambda a: jnp.concatenate([a[3 * D_MODEL:], a[2 * D_MODEL:3 * D_MODEL], a[:2 * D_MODEL]], axis=0)
    g_in_p, g_in_pb = _matmul_tn("g_in", dproj, xn, tm=1152, tn=1024, tl=tl_tn, after=gs_e[4])
    g_in_t, g_in_b = to_front(g_in_p), to_front(g_in_pb)
    ex3 = exchange_start("exchange_start_in", [g_in_b])
    dxn = _matmul("d_xn", dproj, wt_in_p, nt=False, tm=tm_ffn, tn=1024, tk=IN_COLS, out_dtype=BF16)
    grad_x2d, dmeta_blk, g_norm_mix = _final_bwd(h0, nm_t + ex3[4][0:1, 0:1], dxn, dh1, _pick(s_len, 512, BLK))
    packed_l = _pack([g_norm_mix, dmeta_blk[PAD:], loss_part])
    gs_l = _send_start("small_start_b", [packed_l], [landing(packed_l)], chunked=False)
    grads, deltas, new_m, new_v = {}, {}, {}, {}

    recv_ffn_in, recv_ffn_out = _send_wait("exchange_wait_ffn", ex1[0], ex1[1], ex1[2], ex1[3], gs_l[4],
                                           chunked=True, row_fns=ffn_fns)
    recv_glu, recv_out = _send_wait("exchange_wait_mix", ex2[0], ex2[1], ex2[2], ex2[3], recv_ffn_in,
                                    chunked=True)
    (recv_in,) = _send_wait("exchange_wait_in", ex3[0], ex3[1], ex3[2], ex3[3], recv_glu, chunked=True)
    big = [("w_in", g_in_t, True, recv_in, _plain_rows), ("w_glu", g_glu_t, True, recv_glu, _plain_rows),
           ("w_out", g_out, False, recv_out, _plain_rows), ("w_ffn_in", g_ffn_in_t, True, recv_ffn_in, _ffn_in_rows),
           ("w_ffn_out", g_ffn_out, False, recv_ffn_out, _plain_rows)]
    for name, g_full, transposed, recv, row_fn in big:
        m_rows = g_full.shape[0] // N_DEV
        own = lax.dynamic_slice(g_full, (row_fn(me, m_rows), 0), (m_rows, g_full.shape[1]))
        g_shard = _sum_slots("sum_" + name, recv, own)
        grads[name] = (g_shard.T if transposed else g_shard)[None]

    def adamw_2d(name):
        shp = weights[name].shape
        as2d = lambda a: a.reshape(shp[-2], shp[-1])
        d, nm, nv = _adamw("adamw_" + name, as2d(weights[name]), as2d(grads[name]), as2d(mom_m[name]),
                           as2d(mom_v[name]))
        deltas[name], new_m[name], new_v[name] = d.reshape(shp), nm.reshape(shp), nv.reshape(shp)
        return d

    for name in ["w_in", "w_glu", "w_out", "w_ffn_in", "w_ffn_out"]:
        last = adamw_2d(name)

    def small_sum(tag, gs, packed, after):
        (gathered,) = _send_wait("small_wait_" + tag, gs[0], gs[1], gs[2], gs[3], after, chunked=False)
        return _sum_gathered("sum_small_" + tag, gathered, packed.shape[0])

    def small_adamw(tag, names):
        d, nm, nv = _adamw_many("adamw_small_" + tag, [weights[n] for n in names], [grads[n] for n in names],
                                [mom_m[n] for n in names], [mom_v[n] for n in names])
        deltas.update(zip(names, d))
        new_m.update(zip(names, nm))
        new_v.update(zip(names, nv))

    g_sum_e = small_sum("a", gs_e, packed_e, last)
    grads.update(zip(early, _unpack(g_sum_e, [weights[n].shape for n in early])))
    wide = [n for n in early if n.startswith(("ssm_b", "ssm_c"))]
    small_adamw("wide", wide)
    g_sum_l = small_sum("b", gs_l, packed_l, g_sum_e)
    grads["norm_mix"], g_meta, loss_sum = _unpack(g_sum_l, [weights["norm_mix"].shape, (N_META, D_MODEL), (1, 1)])
    small_adamw("rest", [n for n in SMALL if n not in wide])
    grads["meta_tokens"] = lax.dynamic_slice(g_meta, (0, me * BLK), (N_META, BLK))
    adamw_2d("meta_tokens")

    loss = loss_sum[0, 0]
    return (loss, grad_x2d[None], *[grads[n] for n in WEIGHTS], *[deltas[n] for n in WEIGHTS],
            *[new_m[n] for n in WEIGHTS], *[new_v[n] for n in WEIGHTS])
```

```python
import math

import jax
import jax.numpy as jnp
from jax import lax
from jax.experimental import pallas as pl
from jax.experimental.pallas import tpu as pltpu

F32 = jnp.float32
BF16 = jnp.bfloat16

D_MODEL = 1024
N_META = 16
HEAD_DIM = 64
N_Q_HEADS = 16
N_KV_HEADS = 4
Q_W = N_Q_HEADS * HEAD_DIM
KV_W = N_KV_HEADS * HEAD_DIM
SSM_GROUPS = 64
SSM_GROUP_CH = 16
SSM_STATE = 64
N_STATE = SSM_GROUPS * SSM_STATE
D_FF = 2816
IN_COLS = Q_W + 2 * KV_W + 3 * D_MODEL
EPS = 1e-6
BLK = 128
PAD = BLK - N_META
N_DEV = 8
NEG = -1e30
SSM_KB = 8
ST_KB = N_STATE // SSM_KB
LB_KB = ST_KB // BLK
N_LB = N_STATE // BLK

ADAM_LR = 0.001
ADAM_B1 = 0.9
ADAM_B2 = 0.999
ADAM_EPS = 1e-08
ADAM_WD = 0.01
ADAM_STEP = 10

VMEM_LIMIT = 48 * 1024 * 1024
MESH = pl.DeviceIdType.MESH

SMALL = ["norm_mix", "q_norm", "k_norm", "attn_sinks", "lam_re", "lam_im", "log_dt", "ssm_b_re", "ssm_b_im",
         "ssm_c_re", "ssm_c_im", "ssm_d", "attn_branch_norm", "ssm_branch_norm", "norm_ffn"]
WEIGHTS = ["meta_tokens", "norm_mix", "w_in", "q_norm", "k_norm", "attn_sinks", "lam_re", "lam_im", "log_dt",
           "ssm_b_re", "ssm_b_im", "ssm_c_re", "ssm_c_im", "ssm_d", "w_glu", "attn_branch_norm",
           "ssm_branch_norm", "w_out", "norm_ffn", "w_ffn_in", "w_ffn_out"]


def _params(**kw):
    return pltpu.CompilerParams(vmem_limit_bytes=VMEM_LIMIT, **kw)


def _pick(n, cap, mult=16):
    best = None
    for d in range(mult, min(n, cap) + 1, mult):
        if n % d == 0:
            best = d
    assert best is not None, (n, cap, mult)
    return best


def _my_index():
    return 4 * lax.axis_index("x") + 2 * lax.axis_index("y") + lax.axis_index("c")


def _rms(x, g):
    r = lax.rsqrt(jnp.mean(x * x, axis=-1, keepdims=True) + EPS)
    return x * r * g


def _rms_bwd(x, g, dy):
    r = lax.rsqrt(jnp.mean(x * x, axis=-1, keepdims=True) + EPS)
    t = dy * g
    dx = r * t - x * (r * r * r) * jnp.mean(t * x, axis=-1, keepdims=True)
    dg = jnp.sum(dy * (x * r), axis=0, keepdims=True)
    return dx, dg


def _sigmoid(x):
    return jax.nn.sigmoid(x)


def _gelu(x):
    k = math.sqrt(2.0 / math.pi)
    return 0.5 * x * (1.0 + jnp.tanh(k * (x + 0.044715 * (x * x * x))))


def _gelu_grad(x):
    k = math.sqrt(2.0 / math.pi)
    t = jnp.tanh(k * (x + 0.044715 * (x * x * x)))
    return 0.5 * (1.0 + t) + 0.5 * x * (1.0 - t * t) * (k * (1.0 + 3.0 * 0.044715 * (x * x)))


def _head_mean(x, e_ref):
    hi = x.astype(BF16)
    lo = (x - hi.astype(F32)).astype(BF16)
    e = e_ref[...]
    out = []
    for b in range(x.shape[1] // 256):
        sl = slice(256 * b, 256 * b + 256)
        s = (jnp.dot(hi[:, sl], e, preferred_element_type=F32)
             + jnp.dot(lo[:, sl], e, preferred_element_type=F32))
        out.append(s)
    s = out[0] if len(out) == 1 else jnp.concatenate(out, axis=1)
    return s * (1.0 / HEAD_DIM)


def _head_rms(x, g, e_ref):
    r = lax.rsqrt(_head_mean(x * x, e_ref) + EPS)
    return x * r * g


def _head_rms_bwd(x, g, dy, e_ref):
    r = lax.rsqrt(_head_mean(x * x, e_ref) + EPS)
    t = dy * g
    dx = r * t - x * (r * r * r) * _head_mean(t * x, e_ref)
    dg = jnp.sum(dy * (x * r), axis=0, keepdims=True)
    return dx, dg


def _lane_half(shape):
    lane = lax.broadcasted_iota(jnp.int32, shape, len(shape) - 1)
    return (lane >> 6) & 1


def _matmul(name, a, w, *, nt, tm, tn, tk, n=None, w_off=0, res=None, norm_g=None, out_dtype=F32):
    m_dim, k_dim = a.shape
    n_dim = n if n is not None else (w.shape[0] if nt else w.shape[1])
    gm, gn, gk = m_dim // tm, n_dim // tn, k_dim // tk
    assert gm * tm == m_dim and gn * tn == n_dim and gk * tk == k_dim, (name, a.shape, w.shape, tm, tn, tk)
    assert norm_g is None or tn == n_dim
    direct = out_dtype == F32
    dn = (((1,), (1,)), ((), ())) if nt else (((1,), (0,)), ((), ()))

    def body(*refs):
        refs = list(refs)
        a_ref, w_ref = refs[0], refs[1]
        pos = 2
        r_ref = g_ref = on_ref = None
        if res is not None:
            r_ref, pos = refs[pos], pos + 1
        if norm_g is not None:
            g_ref, pos = refs[pos], pos + 1
        o_ref, pos = refs[pos], pos + 1
        if norm_g is not None:
            on_ref, pos = refs[pos], pos + 1
        part = lax.dot_general(a_ref[...], w_ref[...], dn, preferred_element_type=F32)
        if gk == 1:
            r = part if r_ref is None else r_ref[...] + part
            o_ref[...] = r.astype(out_dtype)
            if on_ref is not None:
                on_ref[...] = _rms(r, g_ref[...]).astype(BF16)
            return
        acc = o_ref if direct else refs[pos]
        k = pl.program_id(2)

        @pl.when(k == 0)
        def _():
            acc[...] = part if r_ref is None or not direct else r_ref[...] + part

        @pl.when(k > 0)
        def _():
            acc[...] += part

        @pl.when(k == gk - 1)
        def _():
            if not direct:
                r = acc[...]
                if r_ref is not None:
                    r = r_ref[...] + r
                o_ref[...] = r.astype(out_dtype)
            if on_ref is not None:
                on_ref[...] = _rms(o_ref[...].astype(F32), g_ref[...]).astype(BF16)

    if nt:
        w_spec = pl.BlockSpec((tn, tk), lambda i, j, k: (j + w_off, k))
    else:
        w_spec = pl.BlockSpec((tk, tn), lambda i, j, k: (k, j))
    in_specs = [pl.BlockSpec((tm, tk), lambda i, j, k: (i, k)), w_spec]
    args = [a, w]
    out_spec = pl.BlockSpec((tm, tn), lambda i, j, k: (i, j))
    out_specs, out_shape = [out_spec], [jax.ShapeDtypeStruct((m_dim, n_dim), out_dtype)]
    if res is not None:
        in_specs.append(out_spec)
        args.append(res)
    if norm_g is not None:
        in_specs.append(pl.BlockSpec((1, tn), lambda i, j, k: (0, 0)))
        args.append(norm_g)
        out_specs.append(out_spec)
        out_shape.append(jax.ShapeDtypeStruct((m_dim, n_dim), BF16))
    out = pl.pallas_call(
        body, name=name, grid=(gm, gn, gk),
        in_specs=in_specs, out_specs=out_specs, out_shape=out_shape,
        scratch_shapes=[] if direct or gk == 1 else [pltpu.VMEM((tm, tn), F32)],
        compiler_params=_params(dimension_semantics=("parallel", "parallel", "arbitrary")),
    )(*args)
    return out if norm_g is not None else out[0]


def _matmul_tn(name, a, b, *, tm, tn, tl, after=None):
    l_dim, m_dim = a.shape
    n_dim = b.shape[1]
    gm, gn, gl = m_dim // tm, n_dim // tn, l_dim // tl
    assert gm * tm == m_dim and gn * tn == n_dim and gl * tl == l_dim, (name, a.shape, b.shape, tm, tn, tl)

    def body(*refs):
        a_ref, b_ref = refs[0], refs[1]
        o_ref, ob_ref = refs[-2], refs[-1]

        @pl.when(pl.program_id(2) == 0)
        def _():
            o_ref[...] = jnp.zeros_like(o_ref)

        o_ref[...] += lax.dot_general(a_ref[...], b_ref[...], (((0,), (0,)), ((), ())),
                                      preferred_element_type=F32)

        @pl.when(pl.program_id(2) == gl - 1)
        def _():
            ob_ref[...] = o_ref[...].astype(BF16)

    out_spec = pl.BlockSpec((tm, tn), lambda i, j, l: (i, j))
    in_specs = [pl.BlockSpec((tl, tm), lambda i, j, l: (l, i)), pl.BlockSpec((tl, tn), lambda i, j, l: (l, j))]
    args = [a, b]
    if after is not None:
        in_specs.append(pl.BlockSpec(memory_space=pl.ANY))
        args.append(after)
    return pl.pallas_call(
        body, name=name, grid=(gm, gn, gl),
        in_specs=in_specs,
        out_specs=[out_spec, out_spec],
        out_shape=[jax.ShapeDtypeStruct((m_dim, n_dim), F32), jax.ShapeDtypeStruct((m_dim, n_dim), BF16)],
        compiler_params=_params(dimension_semantics=("parallel", "parallel", "arbitrary")),
    )(*args)


def _row_spec(tm, cols, f=None):
    if f is None:
        return pl.BlockSpec((tm, cols), lambda i: (i, 0))
    return pl.BlockSpec((tm, cols), lambda i: (f(i), 0))


def _full_spec(shape):
    nd = len(shape)
    return pl.BlockSpec(shape, lambda i: (0,) * nd)


def _shifted_specs(n_sub, n_blocks):
    return [_row_spec(BLK, D_MODEL, (lambda i, k=k: jnp.clip(n_sub * i - 1 + k, 0, n_blocks - 1)))
            for k in range(n_sub)]


def _embed_norm(x2d, g, tm):
    s_len = x2d.shape[0]
    l_dim = s_len + BLK
    n_sub = tm // BLK

    def body(*refs):
        x_refs, g_ref, h_ref, xn_ref = refs[:n_sub], refs[n_sub], refs[n_sub + 1], refs[n_sub + 2]
        i = pl.program_id(0)
        for k in range(n_sub):
            rows = slice(BLK * k, BLK * k + BLK)
            h = x_refs[k][...] * jnp.where(n_sub * i + k >= 1, 1.0, 0.0)
            h_ref[rows, :] = h
            xn_ref[rows, :] = _rms(h, g_ref[...]).astype(BF16)

    return pl.pallas_call(
        body, name="embed_norm", grid=(l_dim // tm,),
        in_specs=_shifted_specs(n_sub, s_len // BLK) + [_full_spec((1, D_MODEL))],
        out_specs=[_row_spec(tm, D_MODEL), _row_spec(tm, D_MODEL)],
        out_shape=[jax.ShapeDtypeStruct((l_dim, D_MODEL), F32),
                   jax.ShapeDtypeStruct((l_dim, D_MODEL), BF16)],
        compiler_params=_params(),
    )(*([x2d] * n_sub), g)


def _embed_meta(meta_pad, g, h0, xn):
    def body(mp_ref, g_ref, h_in, xn_in, h_ref, xn_ref):
        h_ref[...] = mp_ref[...]
        xn_ref[...] = _rms(mp_ref[...], g_ref[...]).astype(BF16)

    any_spec = pl.BlockSpec(memory_space=pl.ANY)
    return pl.pallas_call(
        body, name="embed_meta", grid=(1,),
        in_specs=[_full_spec((BLK, D_MODEL)), _full_spec((1, D_MODEL)), any_spec, any_spec],
        out_specs=[_row_spec(BLK, D_MODEL), _row_spec(BLK, D_MODEL)],
        out_shape=[jax.ShapeDtypeStruct(h0.shape, F32), jax.ShapeDtypeStruct(xn.shape, BF16)],
        input_output_aliases={2: 0, 3: 1},
        compiler_params=_params(),
    )(meta_pad, g, h0, xn)


KVX_W = 2 * N_KV_HEADS * BLK


def _qk_prep(qkv, q_norm_t, k_norm_t, e_mat, tm):
    l_dim = qkv.shape[0]

    def body(x_ref, qg_ref, kg_ref, e_ref, q_ref, kf_ref, vf_ref):
        x = x_ref[...]
        q = _head_rms(x[:, :Q_W], qg_ref[...], e_ref) * (HEAD_DIM ** -0.5)
        q_ref[...] = q.astype(BF16)
        k = _head_rms(x[:, Q_W:Q_W + KV_W], kg_ref[...], e_ref)
        v = x[:, Q_W + KV_W:Q_W + 2 * KV_W]
        half = _lane_half((tm, BLK))
        for src, dst in ((k, kf_ref), (v, vf_ref)):
            for kv in range(N_KV_HEADS):
                blk = src[:, BLK * (kv // 2):BLK * (kv // 2) + BLK]
                swapped = pltpu.roll(blk, HEAD_DIM, axis=1)
                for e in range(2):
                    val = blk if kv % 2 == e else swapped
                    idx = 2 * kv + e
                    dst[:, BLK * idx:BLK * idx + BLK] = jnp.where(half == e, val, 0.0).astype(BF16)

    return pl.pallas_call(
        body, name="qk_prep", grid=(l_dim // tm,),
        in_specs=[_row_spec(tm, Q_W + 2 * KV_W), _full_spec((1, Q_W)), _full_spec((1, KV_W)),
                  _full_spec((256, 256))],
        out_specs=[_row_spec(tm, Q_W), _row_spec(tm, KVX_W), _row_spec(tm, KVX_W)],
        out_shape=[jax.ShapeDtypeStruct((l_dim, Q_W), BF16),
                   jax.ShapeDtypeStruct((l_dim, KVX_W), BF16),
                   jax.ShapeDtypeStruct((l_dim, KVX_W), BF16)],
        compiler_params=_params(),
    )(qkv, q_norm_t, k_norm_t, e_mat)


def _merge_fwd(attn, zab, gates, abn, sbn, tm):
    l_dim = attn.shape[0]

    def body(a_ref, z_ref, g_ref, an_ref, sn_ref, o_ref):
        z = z_ref[...].astype(F32)
        g = g_ref[...].astype(F32)
        ssm = z[:, :D_MODEL] * _sigmoid(z[:, D_MODEL:])
        merged = (_sigmoid(g[:, :D_MODEL]) * _rms(a_ref[...], an_ref[...])
                  + _sigmoid(g[:, D_MODEL:]) * _rms(ssm, sn_ref[...]))
        o_ref[...] = merged.astype(BF16)

    return pl.pallas_call(
        body, name="merge_fwd", grid=(l_dim // tm,),
        in_specs=[_row_spec(tm, D_MODEL), _row_spec(tm, 2 * D_MODEL), _row_spec(tm, 2 * D_MODEL),
                  _full_spec((1, D_MODEL)), _full_spec((1, D_MODEL))],
        out_specs=_row_spec(tm, D_MODEL),
        out_shape=jax.ShapeDtypeStruct((l_dim, D_MODEL), BF16),
        compiler_params=_params(),
    )(attn, zab, gates, abn, sbn)


def _merge_bwd(attn, zab, gates, abn, sbn, dmerged, tm):
    l_dim = attn.shape[0]

    def body(a_ref, z_ref, g_ref, an_ref, sn_ref, dm_ref, da_ref, dz_ref, dg_ref, dan_ref, dsn_ref):
        @pl.when(pl.program_id(0) == 0)
        def _():
            dan_ref[...] = jnp.zeros_like(dan_ref)
            dsn_ref[...] = jnp.zeros_like(dsn_ref)

        z = z_ref[...].astype(F32)
        g = g_ref[...].astype(F32)
        dm = dm_ref[...].astype(F32)
        attn_v = a_ref[...]
        za, zb = z[:, :D_MODEL], z[:, D_MODEL:]
        sb = _sigmoid(zb)
        ssm = za * sb
        s_ga, s_gs = _sigmoid(g[:, :D_MODEL]), _sigmoid(g[:, D_MODEL:])
        a_n = _rms(attn_v, an_ref[...])
        s_n = _rms(ssm, sn_ref[...])
        dg_ref[:, :D_MODEL] = (dm * a_n * s_ga * (1.0 - s_ga)).astype(BF16)
        dg_ref[:, D_MODEL:] = (dm * s_n * s_gs * (1.0 - s_gs)).astype(BF16)
        dattn, dan = _rms_bwd(attn_v, an_ref[...], dm * s_ga)
        dssm, dsn = _rms_bwd(ssm, sn_ref[...], dm * s_gs)
        da_ref[...] = dattn.astype(BF16)
        dz_ref[:, :D_MODEL] = (dssm * sb).astype(BF16)
        dz_ref[:, D_MODEL:] = (dssm * za * sb * (1.0 - sb)).astype(BF16)
        dan_ref[...] += dan
        dsn_ref[...] += dsn

    return pl.pallas_call(
        body, name="merge_bwd", grid=(l_dim // tm,),
        in_specs=[_row_spec(tm, D_MODEL), _row_spec(tm, 2 * D_MODEL), _row_spec(tm, 2 * D_MODEL),
                  _full_spec((1, D_MODEL)), _full_spec((1, D_MODEL)), _row_spec(tm, D_MODEL)],
        out_specs=[_row_spec(tm, D_MODEL), _row_spec(tm, 2 * D_MODEL), _row_spec(tm, 2 * D_MODEL),
                   _full_spec((1, D_MODEL)), _full_spec((1, D_MODEL))],
        out_shape=[jax.ShapeDtypeStruct((l_dim, D_MODEL), BF16),
                   jax.ShapeDtypeStruct((l_dim, 2 * D_MODEL), BF16),
                   jax.ShapeDtypeStruct((l_dim, IN_COLS), BF16),
                   jax.ShapeDtypeStruct((1, D_MODEL), F32), jax.ShapeDtypeStruct((1, D_MODEL), F32)],
        compiler_params=_params(),
    )(attn, zab, gates, abn, sbn, dmerged)


FF_TILE = D_FF // 2


def _ffn_in_swiglu(hn, wt_ffn_in, tm):
    l_dim = hn.shape[0]
    nt = (((1,), (1,)), ((), ()))

    def body(a_ref, w_ref, gu_ref, act_ref):
        r = lax.dot_general(a_ref[...], w_ref[...], nt, preferred_element_type=F32)
        gate, up = r[:, :FF_TILE], r[:, FF_TILE:]
        gu_ref[...] = r.astype(BF16)
        act_ref[...] = (gate * _sigmoid(gate) * up).astype(BF16)

    return pl.pallas_call(
        body, name="ffn_in_swiglu", grid=(l_dim // tm, 2),
        in_specs=[pl.BlockSpec((tm, D_MODEL), lambda i, j: (i, 0)),
                  pl.BlockSpec((2 * FF_TILE, D_MODEL), lambda i, j: (j, 0))],
        out_specs=[pl.BlockSpec((tm, 2 * FF_TILE), lambda i, j: (i, j)),
                   pl.BlockSpec((tm, FF_TILE), lambda i, j: (i, j))],
        out_shape=[jax.ShapeDtypeStruct((l_dim, 2 * D_FF), BF16), jax.ShapeDtypeStruct((l_dim, D_FF), BF16)],
        compiler_params=_params(dimension_semantics=("parallel", "parallel")),
    )(hn, wt_ffn_in)


def _d_act_swiglu(dh2_b, w_ffn_out, gu, tm):
    l_dim = dh2_b.shape[0]
    nt = (((1,), (1,)), ((), ()))

    def body(d_ref, w_ref, gu_ref, o_ref):
        d = lax.dot_general(d_ref[...], w_ref[...], nt, preferred_element_type=F32)
        gate = gu_ref[:, :FF_TILE].astype(F32)
        up = gu_ref[:, FF_TILE:].astype(F32)
        s = _sigmoid(gate)
        o_ref[:, :FF_TILE] = (d * up * (s * (1.0 + gate * (1.0 - s)))).astype(BF16)
        o_ref[:, FF_TILE:] = (d * (gate * s)).astype(BF16)

    return pl.pallas_call(
        body, name="d_act_swiglu", grid=(l_dim // tm, 2),
        in_specs=[pl.BlockSpec((tm, D_MODEL), lambda i, j: (i, 0)),
                  pl.BlockSpec((FF_TILE, D_MODEL), lambda i, j: (j, 0)),
                  pl.BlockSpec((tm, 2 * FF_TILE), lambda i, j: (i, j))],
        out_specs=pl.BlockSpec((tm, 2 * FF_TILE), lambda i, j: (i, j)),
        out_shape=jax.ShapeDtypeStruct((l_dim, 2 * D_FF), BF16),
        compiler_params=_params(dimension_semantics=("parallel", "parallel")),
    )(dh2_b, w_ffn_out, gu)


def _loss_grad(h2, target2d, tm):
    l_dim = h2.shape[0]
    n_sub = tm // BLK

    def body(*refs):
        h_ref, t_refs = refs[0], refs[1:1 + n_sub]
        d_ref, db_ref, loss_ref = refs[1 + n_sub:]
        i = pl.program_id(0)

        @pl.when(i == 0)
        def _():
            loss_ref[...] = jnp.zeros_like(loss_ref)

        for k in range(n_sub):
            rows = slice(BLK * k, BLK * k + BLK)
            real = jnp.where(n_sub * i + k >= 1, 1.0, 0.0)
            err = (h_ref[rows, :] - t_refs[k][...]) * real
            d = err * (1.0 / D_MODEL)
            d_ref[rows, :] = d
            db_ref[rows, :] = d.astype(BF16)
            loss_ref[...] += 0.5 * jnp.sum(jnp.mean(err * err, axis=-1, keepdims=True), axis=0, keepdims=True)

    return pl.pallas_call(
        body, name="loss_grad", grid=(l_dim // tm,),
        in_specs=[_row_spec(tm, D_MODEL)] + _shifted_specs(n_sub, target2d.shape[0] // BLK),
        out_specs=[_row_spec(tm, D_MODEL), _row_spec(tm, D_MODEL), _full_spec((1, 1))],
        out_shape=[jax.ShapeDtypeStruct((l_dim, D_MODEL), F32), jax.ShapeDtypeStruct((l_dim, D_MODEL), BF16),
                   jax.ShapeDtypeStruct((1, 1), F32)],
        compiler_params=_params(),
    )(h2, *([target2d] * n_sub))


def _norm_bwd_res(name, h, g, dy, dres, tm):
    l_dim = h.shape[0]

    def body(h_ref, g_ref, dy_ref, dr_ref, o_ref, ob_ref, dg_ref):
        @pl.when(pl.program_id(0) == 0)
        def _():
            dg_ref[...] = jnp.zeros_like(dg_ref)

        dx, dg = _rms_bwd(h_ref[...], g_ref[...], dy_ref[...].astype(F32))
        out = dr_ref[...] + dx
        o_ref[...] = out
        ob_ref[...] = out.astype(BF16)
        dg_ref[...] += dg

    return pl.pallas_call(
        body, name=name, grid=(l_dim // tm,),
        in_specs=[_row_spec(tm, D_MODEL), _full_spec((1, D_MODEL)), _row_spec(tm, D_MODEL), _row_spec(tm, D_MODEL)],
        out_specs=[_row_spec(tm, D_MODEL), _row_spec(tm, D_MODEL), _full_spec((1, D_MODEL))],
        out_shape=[jax.ShapeDtypeStruct((l_dim, D_MODEL), F32), jax.ShapeDtypeStruct((l_dim, D_MODEL), BF16),
                   jax.ShapeDtypeStruct((1, D_MODEL), F32)],
        compiler_params=_params(),
    )(h, g, dy, dres)


def _final_bwd(h0, g, dxn, dh1, tm):
    l_dim = h0.shape[0]
    n_sub = tm // BLK
    n_tiles = (l_dim - BLK) // tm

    def sub_specs():
        return [_row_spec(BLK, D_MODEL, (lambda j, k=k: jnp.where(j < n_tiles, n_sub * j + 1 + k, 0)))
                for k in range(n_sub)]

    def body(*refs):
        h_refs, g_ref = refs[:n_sub], refs[n_sub]
        dy_refs, dr_refs = refs[n_sub + 1:2 * n_sub + 1], refs[2 * n_sub + 1:3 * n_sub + 1]
        gx_ref, gm_ref, dg_ref = refs[3 * n_sub + 1:]
        j = pl.program_id(0)

        @pl.when(j == 0)
        def _():
            dg_ref[...] = jnp.zeros_like(dg_ref)

        def block(k):
            dx, dg = _rms_bwd(h_refs[k][...], g_ref[...], dy_refs[k][...].astype(F32))
            dg_ref[...] += dg
            return dr_refs[k][...] + dx

        @pl.when(j < n_tiles)
        def _():
            for k in range(n_sub):
                gx_ref[BLK * k:BLK * k + BLK, :] = block(k)

        @pl.when(j == n_tiles)
        def _():
            gm_ref[...] = block(0)

    return pl.pallas_call(
        body, name="final_bwd", grid=(n_tiles + 1,),
        in_specs=sub_specs() + [_full_spec((1, D_MODEL))] + sub_specs() + sub_specs(),
        out_specs=[_row_spec(tm, D_MODEL, lambda j: jnp.minimum(j, n_tiles - 1)), _full_spec((BLK, D_MODEL)),
                   _full_spec((1, D_MODEL))],
        out_shape=[jax.ShapeDtypeStruct((l_dim - BLK, D_MODEL), F32), jax.ShapeDtypeStruct((BLK, D_MODEL), F32),
                   jax.ShapeDtypeStruct((1, D_MODEL), F32)],
        compiler_params=_params(),
    )(*([h0] * n_sub), g, *([dxn] * n_sub), *([dh1] * n_sub))


def _attn_valid(n):
    shape = (2 * BLK, 3 * BLK)
    qi = lax.broadcasted_iota(jnp.int32, shape, 0) & (BLK - 1)
    col = lax.broadcasted_iota(jnp.int32, shape, 1)
    kj = col & (BLK - 1)
    part = col >> 7
    nn = jnp.zeros(shape, jnp.int32) + n
    meta_ok = (part == 0) & (kj >= PAD) & (nn >= 1)
    prev_ok = (part == 1) & (kj > qi) & (nn >= 2)
    cur_ok = (part == 2) & (kj <= qi) & ((nn >= 1) | (kj >= PAD))
    return meta_ok | prev_ok | cur_ok


def _attn_scores(q_ref, kwin, sk_ref, valid, kv, e):
    qs = jnp.concatenate([q_ref[:, BLK * (2 * kv):BLK * (2 * kv) + BLK],
                          q_ref[:, BLK * (2 * kv + 1):BLK * (2 * kv + 1) + BLK]], axis=0)
    s = lax.dot_general(qs, kwin, (((1,), (1,)), ((), ())), preferred_element_type=F32)
    h0 = 4 * kv + e
    row = lax.broadcasted_iota(jnp.int32, (2 * BLK, 1), 0)
    sink = jnp.where(row < BLK, sk_ref[:, h0:h0 + 1], sk_ref[:, h0 + 2:h0 + 3])
    return qs, jnp.where(valid, s, NEG), sink


def _attn_specs(nb):
    prev = lambda i: jnp.maximum(i - 1, 0)
    zero = lambda i: 0
    kv_specs = [_row_spec(BLK, KVX_W, zero), _row_spec(BLK, KVX_W, prev), _row_spec(BLK, KVX_W)]
    return kv_specs


def _attn_fwd(qn, kf, vf, sinks):
    l_dim = qn.shape[0]
    nb = l_dim // BLK

    def body(q_ref, km_ref, kp_ref, kc_ref, vm_ref, vp_ref, vc_ref, sk_ref, o_ref, lse_ref):
        valid = _attn_valid(pl.program_id(0))
        lane = lax.broadcasted_iota(jnp.int32, (BLK, BLK), 1)
        lse_all = jnp.zeros((BLK, BLK), F32)
        for kv in range(N_KV_HEADS):
            outs = []
            for e in range(2):
                sl = slice(BLK * (2 * kv + e), BLK * (2 * kv + e) + BLK)
                kwin = jnp.concatenate([km_ref[:, sl], kp_ref[:, sl], kc_ref[:, sl]], axis=0)
                vwin = jnp.concatenate([vm_ref[:, sl], vp_ref[:, sl], vc_ref[:, sl]], axis=0)
                _, s, sink = _attn_scores(q_ref, kwin, sk_ref, valid, kv, e)
                m = jnp.maximum(jnp.max(s, axis=-1, keepdims=True), sink)
                ex = jnp.exp(s - m)
                den = jnp.sum(ex, axis=-1, keepdims=True) + jnp.exp(sink - m)
                p = ex * (1.0 / den)
                outs.append(jnp.dot(p.astype(BF16), vwin, preferred_element_type=F32))
                lse = m + jnp.log(den)
                lse_all = jnp.where(lane == 4 * kv + e, lse[:BLK], lse_all)
                lse_all = jnp.where(lane == 4 * kv + 2 + e, lse[BLK:], lse_all)
            o = outs[0] + outs[1]
            o_ref[:, BLK * (2 * kv):BLK * (2 * kv) + BLK] = o[:BLK]
            o_ref[:, BLK * (2 * kv + 1):BLK * (2 * kv + 1) + BLK] = o[BLK:]
        lse_ref[...] = lse_all

    kv_specs = _attn_specs(nb)
    return pl.pallas_call(
        body, name="attn_fwd", grid=(nb,),
        in_specs=[_row_spec(BLK, Q_W)] + kv_specs + kv_specs + [_full_spec((1, N_Q_HEADS))],
        out_specs=[_row_spec(BLK, Q_W), _row_spec(BLK, BLK)],
        out_shape=[jax.ShapeDtypeStruct((l_dim, Q_W), F32), jax.ShapeDtypeStruct((l_dim, BLK), F32)],
        compiler_params=_params(),
    )(qn, kf, kf, kf, vf, vf, vf, sinks)


def _attn_bwd(qn, kf, vf, sinks, lse, attn, dattn):
    l_dim = qn.shape[0]
    nb = l_dim // BLK
    wide = KV_W
    tn = (((0,), (0,)), ((), ()))
    nt = (((1,), (1,)), ((), ()))

    def body(q_ref, km_ref, kp_ref, kc_ref, vm_ref, vp_ref, vc_ref, sk_ref, lse_ref, o_ref, do_ref,
             dq_ref, dkc_ref, dkp_ref, dkm_ref, dvc_ref, dvp_ref, dvm_ref, dsk_ref):
        @pl.when(pl.program_id(0) == 0)
        def _():
            dkm_ref[...] = jnp.zeros_like(dkm_ref)
            dvm_ref[...] = jnp.zeros_like(dvm_ref)
            dsk_ref[...] = jnp.zeros_like(dsk_ref)

        valid = _attn_valid(pl.program_id(0))
        half = _lane_half((BLK, BLK))
        half2 = _lane_half((2 * BLK, BLK))
        half3 = _lane_half((3 * BLK, BLK))
        lane16 = lax.broadcasted_iota(jnp.int32, (1, N_Q_HEADS), 1)
        dsk = jnp.zeros((1, N_Q_HEADS), F32)
        folded_k, folded_v = [], []
        for kv in range(N_KV_HEADS):
            j0, j1 = 2 * kv, 2 * kv + 1
            do0 = do_ref[:, BLK * j0:BLK * j0 + BLK]
            do1 = do_ref[:, BLK * j1:BLK * j1 + BLK]
            do0f, do1f = do0.astype(F32), do1.astype(F32)
            prod0 = do0f * o_ref[:, BLK * j0:BLK * j0 + BLK]
            prod1 = do1f * o_ref[:, BLK * j1:BLK * j1 + BLK]
            dos = jnp.concatenate([do0, do1], axis=0)
            dqs, dks, dvs = [], [], []
            for e in range(2):
                sl = slice(BLK * (2 * kv + e), BLK * (2 * kv + e) + BLK)
                kwin = jnp.concatenate([km_ref[:, sl], kp_ref[:, sl], kc_ref[:, sl]], axis=0)
                vwin = jnp.concatenate([vm_ref[:, sl], vp_ref[:, sl], vc_ref[:, sl]], axis=0)
                qs, s, sink = _attn_scores(q_ref, kwin, sk_ref, valid, kv, e)
                h0 = 4 * kv + e
                lse_rows = jnp.concatenate([lse_ref[:, h0:h0 + 1], lse_ref[:, h0 + 2:h0 + 3]], axis=0)
                p = jnp.exp(s - lse_rows)
                p_sink = jnp.exp(sink - lse_rows)
                delta = jnp.concatenate(
                    [jnp.sum(jnp.where(half == e, prod0, 0.0), axis=-1, keepdims=True),
                     jnp.sum(jnp.where(half == e, prod1, 0.0), axis=-1, keepdims=True)], axis=0)
                dp = lax.dot_general(dos, vwin, nt, preferred_element_type=F32)
                ds = (p * (dp - delta)).astype(BF16)
                pb = p.astype(BF16)
                dqs.append(jnp.dot(ds, kwin, preferred_element_type=F32))
                dks.append(lax.dot_general(ds, qs, tn, preferred_element_type=F32))
                dvs.append(lax.dot_general(pb, dos, tn, preferred_element_type=F32))
                sink_g = -(p_sink * delta)
                g_lo = jnp.sum(sink_g[:BLK], axis=0, keepdims=True)
                g_hi = jnp.sum(sink_g[BLK:], axis=0, keepdims=True)
                dsk = dsk + jnp.where(lane16 == h0, g_lo, 0.0) + jnp.where(lane16 == h0 + 2, g_hi, 0.0)
            dq = jnp.where(half2 == 0, dqs[0], dqs[1])
            dq_ref[:, BLK * j0:BLK * j0 + BLK] = dq[:BLK].astype(BF16)
            dq_ref[:, BLK * j1:BLK * j1 + BLK] = dq[BLK:].astype(BF16)
            own = kv % 2
            folded_k.append(dks[own] + pltpu.roll(dks[1 - own], HEAD_DIM, axis=1))
            folded_v.append(dvs[own] + pltpu.roll(dvs[1 - own], HEAD_DIM, axis=1))
            if own == 1:
                cols = slice(BLK * (kv // 2), BLK * (kv // 2) + BLK)
                for folded, m_ref, p_ref, c_ref in ((folded_k, dkm_ref, dkp_ref, dkc_ref),
                                                    (folded_v, dvm_ref, dvp_ref, dvc_ref)):
                    both = jnp.where(half3 == 0, folded[0], folded[1])
                    m_ref[:, cols] += both[:BLK]
                    p_ref[:, cols] = both[BLK:2 * BLK].astype(BF16)
                    c_ref[:, cols] = both[2 * BLK:].astype(BF16)
                folded_k, folded_v = [], []
        dsk_ref[...] += dsk

    kv_specs = _attn_specs(nb)
    row_wide = _row_spec(BLK, wide)
    acc_wide = _full_spec((BLK, wide))
    big = jax.ShapeDtypeStruct((l_dim, wide), BF16)
    return pl.pallas_call(
        body, name="attn_bwd", grid=(nb,),
        in_specs=[_row_spec(BLK, Q_W)] + kv_specs + kv_specs
        + [_full_spec((1, N_Q_HEADS)), _row_spec(BLK, BLK), _row_spec(BLK, Q_W), _row_spec(BLK, Q_W)],
        out_specs=[_row_spec(BLK, Q_W), row_wide, row_wide, acc_wide, row_wide, row_wide, acc_wide,
                   _full_spec((1, N_Q_HEADS))],
        out_shape=[jax.ShapeDtypeStruct((l_dim, Q_W), BF16), big, big, jax.ShapeDtypeStruct((BLK, wide), F32),
                   big, big, jax.ShapeDtypeStruct((BLK, wide), F32), jax.ShapeDtypeStruct((1, N_Q_HEADS), F32)],
        compiler_params=_params(),
    )(qn, kf, kf, kf, vf, vf, vf, sinks, lse, attn, dattn)


def _qk_bwd(qkv, q_norm_t, k_norm_t, e_mat, dq, dkc, dkp, dkm, dvc, dvp, dvm, dproj):
    l_dim = qkv.shape[0]
    nb = l_dim // BLK
    wide = KV_W

    def body(x_ref, qg_ref, kg_ref, e_ref, dq_ref, dkc_ref, dkp_ref, dkm_ref, dvc_ref, dvp_ref, dvm_ref,
             dproj_ref, o_ref, dqg_ref, dkg_ref):
        i = pl.program_id(0)

        @pl.when(i == 0)
        def _():
            dqg_ref[...] = jnp.zeros_like(dqg_ref)
            dkg_ref[...] = jnp.zeros_like(dkg_ref)

        first = jnp.where(i == 0, 1.0, 0.0)
        not_last = jnp.where(i < nb - 1, 1.0, 0.0)
        dk_x = dkc_ref[...].astype(F32) + not_last * dkp_ref[...].astype(F32) + first * dkm_ref[...]
        dv_x = dvc_ref[...].astype(F32) + not_last * dvp_ref[...].astype(F32) + first * dvm_ref[...]
        x = x_ref[...]
        dqx, dqg = _head_rms_bwd(x[:, :Q_W], qg_ref[...], dq_ref[...].astype(F32) * (HEAD_DIM ** -0.5), e_ref)
        dkx, dkg = _head_rms_bwd(x[:, Q_W:Q_W + KV_W], kg_ref[...], dk_x, e_ref)
        o_ref[:, :Q_W] = dqx.astype(BF16)
        o_ref[:, Q_W:Q_W + KV_W] = dkx.astype(BF16)
        o_ref[:, Q_W + KV_W:] = dv_x.astype(BF16)
        dqg_ref[...] += dqg
        dkg_ref[...] += dkg

    nxt = lambda i: jnp.minimum(i + 1, nb - 1)
    row_wide = _row_spec(BLK, wide)
    nxt_wide = _row_spec(BLK, wide, nxt)
    acc_wide = _full_spec((BLK, wide))
    qkv_w = Q_W + 2 * KV_W
    in_specs = [_row_spec(BLK, qkv_w), _full_spec((1, Q_W)), _full_spec((1, KV_W)),
                _full_spec((256, 256)), _row_spec(BLK, Q_W),
                row_wide, nxt_wide, acc_wide, row_wide, nxt_wide, acc_wide, pl.BlockSpec(memory_space=pl.ANY)]
    return pl.pallas_call(
        body, name="qk_bwd", grid=(nb,),
        in_specs=in_specs,
        out_specs=[pl.BlockSpec((BLK, qkv_w), lambda i: (i, 3 * D_MODEL // qkv_w)),
                   _full_spec((1, Q_W)), _full_spec((1, KV_W))],
        input_output_aliases={len(in_specs) - 1: 0},
        out_shape=[jax.ShapeDtypeStruct(dproj.shape, BF16),
                   jax.ShapeDtypeStruct((1, Q_W), F32), jax.ShapeDtypeStruct((1, KV_W), F32)],
        compiler_params=_params(),
    )(qkv, q_norm_t, k_norm_t, e_mat, dq, dkc, dkp, dkm, dvc, dvp, dvm, dproj)


GRP = 8


def _strided(r, g):
    return pl.ds(r, g, stride=GRP)


def _lane_slab(ref, i, r, g):
    return ref[i, _strided(r, g), :]


def _chunk_carries(xr_ref, xi_ref, i, ar, ai, sqr_ref, sqi_ref, seq_r, seq_i, cin_r, cin_i, sign, reverse):
    g = xr_ref.shape[1] // GRP
    sr = si = None
    for r in (range(GRP - 1, -1, -1) if reverse else range(GRP)):
        xr, xi = _lane_slab(xr_ref, i, r, g), _lane_slab(xi_ref, i, r, g)
        if sr is not None:
            xr, xi = xr + ar * sr - ai * si, xi + ar * si + ai * sr
        sr, si = xr, xi
    row = lax.broadcasted_iota(jnp.int32, sr.shape, 0)
    idx, s = 0, 1
    while s < g:
        br = sqr_ref[idx:idx + 1, BLK * i:BLK * i + BLK]
        bi = sign * sqi_ref[idx:idx + 1, BLK * i:BLK * i + BLK]
        shift, keep = (g - s, row < g - s) if reverse else (s, row >= s)
        pr = jnp.where(keep, pltpu.roll(sr, shift, axis=0), 0.0)
        pi = jnp.where(keep, pltpu.roll(si, shift, axis=0), 0.0)
        sr, si = sr + br * pr - bi * pi, si + br * pi + bi * pr
        idx, s = idx + 1, 2 * s
    return sr + seq_r * cin_r - seq_i * cin_i, si + seq_r * cin_i + seq_i * cin_r


def _ssm_fwd_kb(u, wb_re, wb_im, wc_re, wc_im, d_skip, tabs, q):
    l_dim = u.shape[0]
    nc = l_dim // q
    g = q // GRP

    def body(u_ref, wbr_ref, wbi_ref, wcr_ref, wci_ref, d_ref, a1r_ref, a1i_ref, sqr_ref, sqi_ref,
             seqr_ref, seqi_ref, y_ref, z_ref, sr_ref, si_ref, cr_ref, ci_ref, xr_ref, xi_ref):
        @pl.when(pl.program_id(1) == 0)
        def _():
            cr_ref[...] = jnp.zeros_like(cr_ref)
            ci_ref[...] = jnp.zeros_like(ci_ref)

        u_kb = u_ref[...]
        ub = u_kb.astype(BF16)
        xr = jnp.dot(ub, wbr_ref[0], preferred_element_type=F32)
        xi = jnp.dot(ub, wbi_ref[0], preferred_element_type=F32)
        for i in range(LB_KB):
            xr_ref[i] = xr[:, BLK * i:BLK * i + BLK]
            xi_ref[i] = xi[:, BLK * i:BLK * i + BLK]
        row = lax.broadcasted_iota(jnp.int32, (g, BLK), 0)
        for i in range(LB_KB):
            lanes = slice(BLK * i, BLK * i + BLK)
            ar, ai = a1r_ref[0:1, lanes], a1i_ref[0:1, lanes]
            cin_r, cin_i = cr_ref[0:1, lanes], ci_ref[0:1, lanes]
            tr, ti = _chunk_carries(xr_ref, xi_ref, i, ar, ai, sqr_ref, sqi_ref, seqr_ref[:, lanes],
                                    seqi_ref[:, lanes], cin_r, cin_i, 1.0, reverse=False)
            cr_ref[0:1, lanes] = jnp.sum(jnp.where(row == g - 1, tr, 0.0), axis=0, keepdims=True)
            ci_ref[0:1, lanes] = jnp.sum(jnp.where(row == g - 1, ti, 0.0), axis=0, keepdims=True)
            pr = jnp.where(row == 0, cin_r, pltpu.roll(tr, 1, axis=0))
            pi = jnp.where(row == 0, cin_i, pltpu.roll(ti, 1, axis=0))
            for r in range(GRP):
                pr, pi = (_lane_slab(xr_ref, i, r, g) + ar * pr - ai * pi,
                          _lane_slab(xi_ref, i, r, g) + ar * pi + ai * pr)
                sr_ref[i, _strided(r, g), :] = pr
                si_ref[i, _strided(r, g), :] = pi
        s_r = jnp.concatenate([sr_ref[i] for i in range(LB_KB)], axis=1)
        s_i = jnp.concatenate([si_ref[i] for i in range(LB_KB)], axis=1)
        y = (jnp.dot(s_r.astype(BF16), wcr_ref[0], preferred_element_type=F32)
             - jnp.dot(s_i.astype(BF16), wci_ref[0], preferred_element_type=F32)
             + d_ref[...] * u_kb)
        y_ref[...] = y.astype(BF16)
        z_ref[...] = _gelu(y).astype(BF16)

    chan = pl.BlockSpec((q, BLK), lambda k, c: (c, k))
    wb_spec = pl.BlockSpec((1, BLK, ST_KB), lambda k, c: (k, 0, 0))
    wc_spec = pl.BlockSpec((1, ST_KB, BLK), lambda k, c: (k, 0, 0))
    tab_specs = [pl.BlockSpec((t.shape[0], ST_KB), lambda k, c: (0, k)) for t in tabs[:6]]
    state_spec = pl.BlockSpec((LB_KB, q, BLK), lambda k, c: (k, c, 0))
    state_shape = jax.ShapeDtypeStruct((N_LB, l_dim, BLK), F32)
    return pl.pallas_call(
        body, name="ssm_fwd", grid=(SSM_KB, nc),
        in_specs=[chan, wb_spec, wb_spec, wc_spec, wc_spec, pl.BlockSpec((1, BLK), lambda k, c: (0, k))] + tab_specs,
        out_specs=[chan, chan, state_spec, state_spec],
        out_shape=[jax.ShapeDtypeStruct((l_dim, D_MODEL), BF16), jax.ShapeDtypeStruct((l_dim, D_MODEL), BF16),
                   state_shape, state_shape],
        scratch_shapes=[pltpu.VMEM((8, ST_KB), F32), pltpu.VMEM((8, ST_KB), F32),
                        pltpu.VMEM((LB_KB, q, BLK), F32), pltpu.VMEM((LB_KB, q, BLK), F32)],
        compiler_params=_params(dimension_semantics=("parallel", "arbitrary")),
    )(u, wb_re, wb_im, wc_re, wc_im, d_skip, *tabs[:6])


def _ssm_bwd_kb(dz, y, u, s_re, s_im, wb_re, wb_im, wc_re, wc_im, d_skip, tabs, q, dproj):
    l_dim = u.shape[0]
    nc = l_dim // q
    g = q // GRP

    def body(dz_ref, y_ref, u_ref, sr_ref, si_ref, wbr_ref, wbi_ref, wcr_ref, wci_ref, d_ref,
             a1r_ref, a1i_ref, sqr_ref, sqi_ref, revr_ref, revi_ref, dproj_ref,
             du_ref, dd_ref, dar_ref, dai_ref, dwbr_ref, dwbi_ref, dwcr_ref, dwci_ref,
             cr_ref, ci_ref, gr_ref, gi_ref):
        @pl.when(pl.program_id(1) == 0)
        def _():
            for ref in (cr_ref, ci_ref, dd_ref, dar_ref, dai_ref, dwbr_ref, dwbi_ref, dwcr_ref, dwci_ref):
                ref[...] = jnp.zeros_like(ref)

        tn = (((0,), (0,)), ((), ()))
        nt = (((1,), (1,)), ((), ()))
        u_kb = u_ref[...]
        dy = dz_ref[...].astype(F32) * _gelu_grad(y_ref[...].astype(F32))
        dyb = dy.astype(BF16)
        ub = u_kb.astype(BF16)
        dd_ref[...] += jnp.sum(dy * u_kb, axis=0, keepdims=True)
        ds_r = lax.dot_general(dyb, wcr_ref[0], nt, preferred_element_type=F32)
        ds_i = -lax.dot_general(dyb, wci_ref[0], nt, preferred_element_type=F32)
        for i in range(LB_KB):
            gr_ref[i] = ds_r[:, BLK * i:BLK * i + BLK]
            gi_ref[i] = ds_i[:, BLK * i:BLK * i + BLK]
        row = lax.broadcasted_iota(jnp.int32, (g, BLK), 0)
        for i in range(LB_KB):
            lanes = slice(BLK * i, BLK * i + BLK)
            ar, ai = a1r_ref[0:1, lanes], -a1i_ref[0:1, lanes]
            cin_r, cin_i = cr_ref[0:1, lanes], ci_ref[0:1, lanes]
            tr, ti = _chunk_carries(gr_ref, gi_ref, i, ar, ai, sqr_ref, sqi_ref, revr_ref[:, lanes],
                                    -revi_ref[:, lanes], cin_r, cin_i, -1.0, reverse=True)
            cr_ref[0:1, lanes] = jnp.sum(jnp.where(row == 0, tr, 0.0), axis=0, keepdims=True)
            ci_ref[0:1, lanes] = jnp.sum(jnp.where(row == 0, ti, 0.0), axis=0, keepdims=True)
            nr = jnp.where(row == g - 1, cin_r, pltpu.roll(tr, g - 1, axis=0))
            ni = jnp.where(row == g - 1, cin_i, pltpu.roll(ti, g - 1, axis=0))
            acc_r = jnp.zeros((g, BLK), F32)
            acc_i = jnp.zeros((g, BLK), F32)
            for r in range(GRP - 1, -1, -1):
                s_r, s_i = _lane_slab(sr_ref, i, r, g), _lane_slab(si_ref, i, r, g)
                acc_r = acc_r + (nr * s_r + ni * s_i)
                acc_i = acc_i + (ni * s_r - nr * s_i)
                nr, ni = (_lane_slab(gr_ref, i, r, g) + ar * nr - ai * ni,
                          _lane_slab(gi_ref, i, r, g) + ar * ni + ai * nr)
                gr_ref[i, _strided(r, g), :] = nr
                gi_ref[i, _strided(r, g), :] = ni
            dar_ref[:, lanes] += jnp.sum(acc_r, axis=0, keepdims=True)
            dai_ref[:, lanes] += jnp.sum(acc_i, axis=0, keepdims=True)
        grb = jnp.concatenate([gr_ref[i] for i in range(LB_KB)], axis=1).astype(BF16)
        gib = jnp.concatenate([gi_ref[i] for i in range(LB_KB)], axis=1).astype(BF16)
        srb = jnp.concatenate([sr_ref[i] for i in range(LB_KB)], axis=1).astype(BF16)
        sib = jnp.concatenate([si_ref[i] for i in range(LB_KB)], axis=1).astype(BF16)
        du = (lax.dot_general(grb, wbr_ref[0], nt, preferred_element_type=F32)
              + lax.dot_general(gib, wbi_ref[0], nt, preferred_element_type=F32)
              + d_ref[...] * dy)
        du_ref[...] = du.astype(BF16)
        dwbr_ref[0] += lax.dot_general(ub, grb, tn, preferred_element_type=F32)
        dwbi_ref[0] += lax.dot_general(ub, gib, tn, preferred_element_type=F32)
        dwcr_ref[0] += lax.dot_general(srb, dyb, tn, preferred_element_type=F32)
        dwci_ref[0] -= lax.dot_general(sib, dyb, tn, preferred_element_type=F32)

    chan = pl.BlockSpec((q, BLK), lambda k, c: (nc - 1 - c, k))
    wb_spec = pl.BlockSpec((1, BLK, ST_KB), lambda k, c: (k, 0, 0))
    wc_spec = pl.BlockSpec((1, ST_KB, BLK), lambda k, c: (k, 0, 0))
    tab_in = [tabs[0], tabs[1], tabs[2], tabs[3], tabs[6], tabs[7]]
    tab_specs = [pl.BlockSpec((t.shape[0], ST_KB), lambda k, c: (0, k)) for t in tab_in]
    vec = pl.BlockSpec((1, BLK), lambda k, c: (0, k))
    svec = pl.BlockSpec((1, ST_KB), lambda k, c: (0, k))
    state_spec = pl.BlockSpec((LB_KB, q, BLK), lambda k, c: (k, nc - 1 - c, 0))
    du_spec = pl.BlockSpec((q, BLK), lambda k, c: (nc - 1 - c, 2 * D_MODEL // BLK + k))
    in_specs = ([chan, chan, chan, state_spec, state_spec, wb_spec, wb_spec, wc_spec, wc_spec, vec] + tab_specs
                + [pl.BlockSpec(memory_space=pl.ANY)])
    return pl.pallas_call(
        body, name="ssm_bwd", grid=(SSM_KB, nc),
        in_specs=in_specs,
        out_specs=[du_spec, vec, svec, svec, wb_spec, wb_spec, wc_spec, wc_spec],
        input_output_aliases={len(in_specs) - 1: 0},
        out_shape=[jax.ShapeDtypeStruct(dproj.shape, BF16), jax.ShapeDtypeStruct((1, D_MODEL), F32),
                   jax.ShapeDtypeStruct((1, N_STATE), F32), jax.ShapeDtypeStruct((1, N_STATE), F32),
                   jax.ShapeDtypeStruct((SSM_KB, BLK, ST_KB), F32), jax.ShapeDtypeStruct((SSM_KB, BLK, ST_KB), F32),
                   jax.ShapeDtypeStruct((SSM_KB, ST_KB, BLK), F32), jax.ShapeDtypeStruct((SSM_KB, ST_KB, BLK), F32)],
        scratch_shapes=[pltpu.VMEM((8, ST_KB), F32), pltpu.VMEM((8, ST_KB), F32),
                        pltpu.VMEM((LB_KB, q, BLK), F32), pltpu.VMEM((LB_KB, q, BLK), F32)],
        compiler_params=_params(dimension_semantics=("parallel", "arbitrary")),
    )(dz, y, u, s_re, s_im, wb_re, wb_im, wc_re, wc_im, d_skip, *tab_in, dproj)


def _discretize(lam_re, lam_im, log_dt, b_re, b_im):
    dt = jnp.exp(log_dt)[:, None]
    mag = jnp.exp(lam_re * dt)
    ar, ai = mag * jnp.cos(lam_im * dt), mag * jnp.sin(lam_im * dt)
    den = lam_re * lam_re + lam_im * lam_im
    nr, ni = ar - 1.0, ai
    fr, fi = (nr * lam_re + ni * lam_im) / den, (ni * lam_re - nr * lam_im) / den
    bbar_re = fr[..., None] * b_re - fi[..., None] * b_im
    bbar_im = fr[..., None] * b_im + fi[..., None] * b_re
    return ar, ai, bbar_re, bbar_im


def _block_diag_b(bbar):
    eye = jnp.eye(8, dtype=bbar.dtype)
    return jnp.einsum("kgpc,gh->kgchp", bbar.reshape(8, 8, SSM_STATE, SSM_GROUP_CH), eye).reshape(8, BLK, ST_KB)


def _block_diag_b_t(dwb):
    eye = jnp.eye(8, dtype=dwb.dtype)
    return jnp.einsum("kgchp,gh->kgpc", dwb.reshape(8, 8, SSM_GROUP_CH, 8, SSM_STATE), eye).reshape(
        SSM_GROUPS, SSM_STATE, SSM_GROUP_CH)


def _block_diag_c(c):
    eye = jnp.eye(8, dtype=c.dtype)
    return jnp.einsum("kgcp,gh->kgphc", c.reshape(8, 8, SSM_GROUP_CH, SSM_STATE), eye).reshape(8, ST_KB, BLK)


def _block_diag_c_t(dwc):
    eye = jnp.eye(8, dtype=dwc.dtype)
    return jnp.einsum("kgphc,gh->kgcp", dwc.reshape(8, 8, SSM_STATE, 8, SSM_GROUP_CH), eye).reshape(
        SSM_GROUPS, SSM_GROUP_CH, SSM_STATE)


def _powers(br, bi, n):
    pr, pi = br, bi
    cr, ci = br, bi
    while pr.shape[0] < n:
        pr, pi = (jnp.concatenate([pr, pr * cr - pi * ci], axis=0),
                  jnp.concatenate([pi, pr * ci + pi * cr], axis=0))
        cr, ci = cr * cr - ci * ci, 2.0 * cr * ci
    return pr[:n], pi[:n]


def _powers_desc(br, bi, n):
    pr, pi = br, bi
    cr, ci = br, bi
    while pr.shape[0] < n:
        pr, pi = (jnp.concatenate([pr * cr - pi * ci, pr], axis=0),
                  jnp.concatenate([pr * ci + pi * cr, pi], axis=0))
        cr, ci = cr * cr - ci * ci, 2.0 * cr * ci
    return pr, pi


def _power_tables(ar, ai, g):
    a1r, a1i = _powers(ar, ai, GRP)
    seqr, seqi = _powers(a1r[GRP - 1:], a1i[GRP - 1:], g)
    g2 = 1 << (g - 1).bit_length()
    revr, revi = _powers_desc(a1r[GRP - 1:], a1i[GRP - 1:], g2)
    revr, revi = revr[g2 - g:], revi[g2 - g:]
    sq_r, sq_i = [seqr[0:1]], [seqi[0:1]]
    while len(sq_r) < 8:
        r, i = sq_r[-1], sq_i[-1]
        sq_r.append(r * r - i * i)
        sq_i.append(2.0 * r * i)
    sqr, sqi = jnp.concatenate(sq_r, axis=0), jnp.concatenate(sq_i, axis=0)
    return a1r, a1i, sqr, sqi, seqr, seqi, revr, revi


HBM_SPEC = pl.BlockSpec(memory_space=pltpu.HBM)
SEM_SPEC = pl.BlockSpec(memory_space=pltpu.SEMAPHORE)
DATAFLOW = pltpu.SideEffectType.DATAFLOW_SIDE_EFFECTING


def _plain_rows(p, m):
    return p * m


def _ffn_in_rows(p, m):
    return ((p & 3) >> 1) * (4 * m) + (p >> 2) * (2 * m) + (p & 1) * m


def _peer_copies(src_refs, land_refs, send_sems, recv_sems, chunked, row_fns):
    x, y, c = lax.axis_index("x"), lax.axis_index("y"), lax.axis_index("c")
    me = 4 * x + 2 * y + c
    copies = []
    for a, (src, land) in enumerate(zip(src_refs, land_refs)):
        m = land.shape[0] // N_DEV
        for k in range(N_DEV - 1):
            rel = k + 1
            bx, by, bc = (rel >> 2) & 1, (rel >> 1) & 1, rel & 1
            peer = (x + bx - 2 * x * bx, y + by - 2 * y * by, c + bc - 2 * c * bc)
            p_idx = 4 * peer[0] + 2 * peer[1] + peer[2]
            copies.append(pltpu.make_async_remote_copy(
                src_ref=src.at[pl.ds(row_fns[a](p_idx, m), m), :] if chunked else src,
                dst_ref=land.at[pl.ds(me * m if chunked else row_fns[a](me, m), m), :],
                send_sem=send_sems[a * (N_DEV - 1) + k], recv_sem=recv_sems[a * (N_DEV - 1) + k],
                device_id=peer, device_id_type=MESH))
    return copies


def _send_start(name, srcs, lands, chunked, row_fns=None):
    n = len(srcs)
    ns = n * (N_DEV - 1)
    row_fns = row_fns or [_plain_rows] * n

    def body(*refs):
        src_refs, land_refs = refs[:n], refs[n:2 * n]
        send_sems, recv_sems = refs[2 * n:2 * n + ns], refs[2 * n + ns:2 * n + 2 * ns]
        token = refs[-1]
        for cp in _peer_copies(src_refs, land_refs, send_sems, recv_sems, chunked, row_fns):
            cp.start()
        token[...] = jnp.zeros_like(token)

    ins = [pltpu.with_memory_space_constraint(a, pltpu.HBM) for a in list(srcs) + list(lands)]
    out = pl.pallas_call(
        body, name=name,
        in_specs=[HBM_SPEC] * (2 * n),
        out_specs=[SEM_SPEC] * (2 * ns) + [HBM_SPEC] * (2 * n) + [pl.BlockSpec(memory_space=pltpu.VMEM)],
        out_shape=[pltpu.SemaphoreType.DMA(())] * (2 * ns)
        + [pltpu.HBM(a.shape, a.dtype) for a in list(srcs) + list(lands)]
        + [jax.ShapeDtypeStruct((8, BLK), F32)],
        input_output_aliases={i: i + 2 * ns for i in range(2 * n)},
        compiler_params=pltpu.CompilerParams(has_side_effects=DATAFLOW),
    )(*ins)
    return out[:ns], out[ns:2 * ns], out[2 * ns:2 * ns + n], out[2 * ns + n:2 * ns + 2 * n], out[-1]


def _send_wait(name, send_sems, recv_sems, srcs, lands, after, chunked, row_fns=None):
    n = len(srcs)
    ns = n * (N_DEV - 1)
    row_fns = row_fns or [_plain_rows] * n

    def body(*refs):
        src_refs, land_refs = refs[:n], refs[n:2 * n]
        s_sems, r_sems = refs[2 * n:2 * n + ns], refs[2 * n + ns:2 * n + 2 * ns]
        copies = _peer_copies(src_refs, land_refs, s_sems, r_sems, chunked, row_fns)
        for cp in copies:
            cp.wait_send()
        for cp in copies:
            cp.wait_recv()

    out = pl.pallas_call(
        body, name=name,
        in_specs=[HBM_SPEC] * (2 * n) + [SEM_SPEC] * (2 * ns) + [pl.BlockSpec(memory_space=pl.ANY)],
        out_specs=[HBM_SPEC] * (2 * n),
        out_shape=[pltpu.HBM(a.shape, a.dtype) for a in list(srcs) + list(lands)],
        input_output_aliases={i: i for i in range(2 * n)},
        compiler_params=pltpu.CompilerParams(has_side_effects=DATAFLOW),
    )(*srcs, *lands, *send_sems, *recv_sems, after)
    return out[n:]


def _sum_slots(name, recv, own):
    m, ncol = own.shape
    tr = m // 2 if (m // 2) % 16 == 0 else m
    g = m // tr

    def body(*refs):
        slots, own_ref, o_ref = refs[:N_DEV], refs[N_DEV], refs[N_DEV + 1]
        me = _my_index()
        tot = None
        for s in range(N_DEV):
            v = jnp.where(me == s, own_ref[...], slots[s][...].astype(F32))
            tot = v if tot is None else tot + v
        o_ref[...] = tot

    def slot_spec(s):
        return pl.BlockSpec((tr, ncol), lambda i: (s * g + i, 0))

    return pl.pallas_call(
        body, name=name, grid=(g,),
        in_specs=[slot_spec(s) for s in range(N_DEV)] + [pl.BlockSpec((tr, ncol), lambda i: (i, 0))],
        out_specs=pl.BlockSpec((tr, ncol), lambda i: (i, 0)),
        out_shape=jax.ShapeDtypeStruct((m, ncol), F32),
        compiler_params=_params(),
    )(*([recv] * N_DEV), own)


def _sum_gathered(name, gathered, rows):
    tr = _pick(rows, 512, 8)
    g = rows // tr

    def body(*refs):
        o_ref = refs[N_DEV]
        tot = refs[0][...]
        for s in range(1, N_DEV):
            tot = tot + refs[s][...]
        o_ref[...] = tot

    return pl.pallas_call(
        body, name=name, grid=(g,),
        in_specs=[pl.BlockSpec((tr, BLK), (lambda i, s=s: (s * g + i, 0))) for s in range(N_DEV)],
        out_specs=pl.BlockSpec((tr, BLK), lambda i: (i, 0)),
        out_shape=jax.ShapeDtypeStruct((rows, BLK), F32),
        compiler_params=_params(),
    )(*([gathered] * N_DEV))


def _adamw(name, w, g, m, v):
    r, c = w.shape
    tr = _pick(r, 256, 8) if r % 8 == 0 else r
    c1 = 1.0 - ADAM_B1 ** ADAM_STEP
    c2 = 1.0 - ADAM_B2 ** ADAM_STEP

    def body(w_ref, g_ref, m_ref, v_ref, d_ref, nm_ref, nv_ref):
        gv = g_ref[...]
        nm = ADAM_B1 * m_ref[...] + (1.0 - ADAM_B1) * gv
        nv = ADAM_B2 * v_ref[...] + (1.0 - ADAM_B2) * (gv * gv)
        m_hat = nm / c1
        v_hat = nv / c2
        d_ref[...] = -ADAM_LR * (m_hat / (jnp.sqrt(v_hat) + ADAM_EPS) + ADAM_WD * w_ref[...])
        nm_ref[...] = nm
        nv_ref[...] = nv

    spec = pl.BlockSpec((tr, c), lambda i: (i, 0))
    shape = jax.ShapeDtypeStruct((r, c), F32)
    return pl.pallas_call(
        body, name=name, grid=(r // tr,),
        in_specs=[spec] * 4, out_specs=[spec] * 3, out_shape=[shape] * 3,
        compiler_params=_params(),
    )(w, g, m, v)


def _adamw_many(name, ws, gs, ms, vs):
    n = len(ws)
    c1 = 1.0 - ADAM_B1 ** ADAM_STEP
    c2 = 1.0 - ADAM_B2 ** ADAM_STEP

    def body(*refs):
        for a in range(n):
            w_ref, g_ref, m_ref, v_ref = refs[a], refs[n + a], refs[2 * n + a], refs[3 * n + a]
            d_ref, nm_ref, nv_ref = refs[4 * n + a], refs[5 * n + a], refs[6 * n + a]
            gv = g_ref[...]
            nm = ADAM_B1 * m_ref[...] + (1.0 - ADAM_B1) * gv
            nv = ADAM_B2 * v_ref[...] + (1.0 - ADAM_B2) * (gv * gv)
            d_ref[...] = -ADAM_LR * ((nm / c1) / (jnp.sqrt(nv / c2) + ADAM_EPS) + ADAM_WD * w_ref[...])
            nm_ref[...] = nm
            nv_ref[...] = nv

    vmem = pl.BlockSpec(memory_space=pltpu.VMEM)
    shapes = [jax.ShapeDtypeStruct(w.shape, F32) for w in ws]
    out = pl.pallas_call(
        body, name=name,
        in_specs=[vmem] * (4 * n), out_specs=[vmem] * (3 * n), out_shape=shapes * 3,
        compiler_params=_params(),
    )(*ws, *gs, *ms, *vs)
    return out[:n], out[n:2 * n], out[2 * n:]


PACK_ROWS = 128


def _pack(parts):
    flat = []
    for p in parts:
        v = p.reshape(-1)
        flat.append(jnp.pad(v, (0, (-v.shape[0]) % BLK)))
    v = jnp.concatenate(flat)
    v = jnp.pad(v, (0, (-v.shape[0]) % (PACK_ROWS * BLK)))
    return v.reshape(-1, BLK)


def _unpack(buf, shapes):
    flat = buf.reshape(-1)
    out, off = [], 0
    for shp in shapes:
        size = math.prod(shp)
        out.append(flat[off:off + size].reshape(shp))
        off += size + (-size) % BLK
    return out


def kernel(x, meta_tokens, norm_mix, w_in, q_norm, k_norm, attn_sinks, lam_re, lam_im, log_dt, ssm_b_re, ssm_b_im, ssm_c_re, ssm_c_im, ssm_d, w_glu, attn_branch_norm, ssm_branch_norm, w_out, norm_ffn, w_ffn_in, w_ffn_out, loss_target, m_meta_tokens, m_norm_mix, m_w_in, m_q_norm, m_k_norm, m_attn_sinks, m_lam_re, m_lam_im, m_log_dt, m_ssm_b_re, m_ssm_b_im, m_ssm_c_re, m_ssm_c_im, m_ssm_d, m_w_glu, m_attn_branch_norm, m_ssm_branch_norm, m_w_out, m_norm_ffn, m_w_ffn_in, m_w_ffn_out, v_meta_tokens, v_norm_mix, v_w_in, v_q_norm, v_k_norm, v_attn_sinks, v_lam_re, v_lam_im, v_log_dt, v_ssm_b_re, v_ssm_b_im, v_ssm_c_re, v_ssm_c_im, v_ssm_d, v_w_glu, v_attn_branch_norm, v_ssm_branch_norm, v_w_out, v_norm_ffn, v_w_ffn_in, v_w_ffn_out):
    args = dict(locals())
    weights = {n: args[n] for n in WEIGHTS}
    mom_m = {n: args["m_" + n] for n in WEIGHTS}
    mom_v = {n: args["v_" + n] for n in WEIGHTS}

    x2d = x[0]
    target2d = loss_target[0]
    s_len = x2d.shape[0]
    l_dim = s_len + BLK
    tm_row = _pick(l_dim, 320)
    tm_mm = _pick(l_dim, 1040)
    tl_tn = _pick(l_dim, 2080)
    tm_ffn = _pick(l_dim, 640)
    tm_big = _pick(l_dim, 2080)
    tm_shift = _pick(l_dim, 640, BLK)

    shard_in = w_in[0].T.astype(BF16)
    shard_glu = w_glu[0].T.astype(BF16)
    shard_out = w_out[0].astype(BF16)
    shard_ffn_in = w_ffn_in[0].T.astype(BF16)
    shard_ffn_out = w_ffn_out[0].astype(BF16)
    shard_meta = meta_tokens.T
    me = _my_index()

    def landing(shard, row_fn=_plain_rows):
        m_rows, cols = shard.shape
        return lax.dynamic_update_slice(lax.empty((N_DEV * m_rows, cols), shard.dtype), shard,
                                        (row_fn(me, m_rows), 0))

    first = [shard_in, shard_meta]
    ga = _send_start("gather_start_a", first, [landing(s) for s in first], chunked=False)
    later = [shard_glu + ga[4][0:1, 0:1].astype(BF16), shard_out, shard_ffn_in, shard_ffn_out]
    later_fns = [_plain_rows, _plain_rows, _ffn_in_rows, _plain_rows]
    gb = _send_start("gather_start_b", later, [landing(s, f) for s, f in zip(later, later_fns)], chunked=False,
                     row_fns=later_fns)

    nm_t = norm_mix + (ga[4][0:1, 0:1] + gb[4][0:1, 0:1])
    qn_t, kn_t = jnp.tile(q_norm, (1, N_Q_HEADS)), jnp.tile(k_norm, (1, N_KV_HEADS))
    e_mat = jnp.kron(jnp.eye(4, dtype=F32), jnp.ones((HEAD_DIM, HEAD_DIM), F32)).astype(BF16)

    def disc(lr, li, ldt, br, bi):
        return _discretize(lr[0], li[0], ldt[0], br[0], bi[0])

    (abar_re, abar_im, bbar_re, bbar_im), disc_vjp = jax.vjp(disc, lam_re, lam_im, log_dt, ssm_b_re, ssm_b_im)
    wb_re, wb_im = _block_diag_b(bbar_re).astype(BF16), _block_diag_b(bbar_im).astype(BF16)
    wc_re, wc_im = _block_diag_c(ssm_c_re[0]).astype(BF16), _block_diag_c(ssm_c_im[0]).astype(BF16)
    q_ssm = _pick(l_dim, 640, 64)
    tabs = _power_tables(abar_re.reshape(1, N_STATE), abar_im.reshape(1, N_STATE), q_ssm // GRP)

    h0, xn = _embed_norm(x2d, nm_t, tm_shift)
    wt_in, meta_t = _send_wait("gather_wait_a", ga[0], ga[1], ga[2], ga[3], xn, chunked=False)
    meta_pad = jnp.pad(meta_t.T, ((PAD, 0), (0, 0)))
    h0, xn = _embed_meta(meta_pad, nm_t, h0, xn)
    qkv_w, u_end = Q_W + 2 * KV_W, Q_W + 2 * KV_W + D_MODEL
    wt_in_p = jnp.concatenate([wt_in[u_end:], wt_in[qkv_w:u_end], wt_in[:qkv_w]], axis=0)
    qkv = _matmul("proj_qkv", xn, wt_in_p, nt=True, tm=tm_big, tn=qkv_w, tk=D_MODEL, n=qkv_w, w_off=2)
    u = _matmul("proj_u", xn, wt_in_p, nt=True, tm=tm_big, tn=D_MODEL, tk=D_MODEL, n=D_MODEL, w_off=2)
    gates = _matmul("proj_gates", xn, wt_in_p, nt=True, tm=tm_big, tn=D_MODEL, tk=D_MODEL, n=2 * D_MODEL, w_off=0,
                    out_dtype=BF16)
    qn, kf, vf = _qk_prep(qkv, qn_t, kn_t, e_mat, tm_row)
    attn, lse = _attn_fwd(qn, kf, vf, attn_sinks)
    y, z, s_re, s_im = _ssm_fwd_kb(u, wb_re, wb_im, wc_re, wc_im, ssm_d, tabs, q_ssm)
    wt_glu, w_out_f, wt_ffn_in, w_ffn_out_f = _send_wait("gather_wait_b", gb[0], gb[1], gb[2], gb[3], z,
                                                         chunked=False, row_fns=later_fns)
    zab = _matmul("glu_proj", z, wt_glu, nt=True, tm=tm_big, tn=1024, tk=D_MODEL, out_dtype=BF16)
    merged = _merge_fwd(attn, zab, gates, attn_branch_norm, ssm_branch_norm, tm_row)
    h1, hn = _matmul("out_proj", merged, w_out_f, nt=False, tm=tm_mm, tn=1024, tk=D_MODEL, res=h0,
                     norm_g=norm_ffn)
    gu, act = _ffn_in_swiglu(hn, wt_ffn_in, tm_ffn)
    h2 = _matmul("ffn_out", act, w_ffn_out_f, nt=False, tm=tm_mm, tn=1024, tk=D_FF, res=h1)
    dh2, dh2_b, loss_part = _loss_grad(h2, target2d, tm_shift)

    dgu = _d_act_swiglu(dh2_b, w_ffn_out_f, gu, tm_ffn)

    def exchange_start(name, grads_b, row_fns=None):
        return _send_start(name, grads_b, [jnp.zeros(g.shape, BF16) for g in grads_b], chunked=True,
                           row_fns=row_fns)

    g_ffn_out, g_ffn_out_b = _matmul_tn("g_ffn_out", act, dh2_b, tm=1408, tn=1024, tl=tl_tn)
    g_ffn_in_t, g_ffn_in_b = _matmul_tn("g_ffn_in", dgu, hn, tm=1408, tn=1024, tl=tl_tn)
    ffn_fns = [_ffn_in_rows, _plain_rows]
    ex1 = exchange_start("exchange_start_ffn", [g_ffn_in_b, g_ffn_out_b], ffn_fns)
    dhn = _matmul("d_hn", dgu, wt_ffn_in, nt=False, tm=tm_ffn, tn=1024, tk=2 * D_FF, out_dtype=BF16)
    dh1, dh1_b, g_norm_ffn = _norm_bwd_res("ffn_norm_bwd", h1, norm_ffn + ex1[4][0:1, 0:1], dhn, dh2, tm_row)
    dmerged = _matmul("d_merged", dh1_b, w_out_f, nt=True, tm=tm_big, tn=1024, tk=D_MODEL, out_dtype=BF16)
    dattn, dzab, dproj, g_abn, g_sbn = _merge_bwd(attn, zab, gates, attn_branch_norm, ssm_branch_norm,
                                                    dmerged, tm_row)
    g_out, g_out_b = _matmul_tn("g_out", merged, dh1_b, tm=1024, tn=1024, tl=tl_tn)
    dz = _matmul("d_z", dzab, wt_glu, nt=False, tm=tm_mm, tn=1024, tk=2 * D_MODEL, out_dtype=BF16)
    g_glu_t, g_glu_b = _matmul_tn("g_glu", dzab, z, tm=1024, tn=1024, tl=tl_tn)
    ex2 = exchange_start("exchange_start_mix", [g_glu_b, g_out_b])
    dproj, g_ssm_d, g_ar, g_ai, g_wbr, g_wbi, g_wcr, g_wci = _ssm_bwd_kb(
        dz, y, u, s_re, s_im, wb_re, wb_im, wc_re, wc_im, ssm_d + ex2[4][0:1, 0:1], tabs, q_ssm, dproj)
    dq, dkc, dkp, dkm, dvc, dvp, dvm, g_sinks = _attn_bwd(qn, kf, vf, attn_sinks, lse, attn, dattn)
    dproj, g_qn_t, g_kn_t = _qk_bwd(qkv, qn_t, kn_t, e_mat, dq, dkc, dkp, dkm, dvc, dvp, dvm, dproj)
    g_lam_re, g_lam_im, g_log_dt, g_b_re, g_b_im = disc_vjp(
        (g_ar.reshape(SSM_GROUPS, SSM_STATE), g_ai.reshape(SSM_GROUPS, SSM_STATE),
         _block_diag_b_t(g_wbr), _block_diag_b_t(g_wbi)))
    small_grads = {
        "q_norm": g_qn_t.reshape(N_Q_HEADS, HEAD_DIM).sum(0)[None],
        "k_norm": g_kn_t.reshape(N_KV_HEADS, HEAD_DIM).sum(0)[None], "attn_sinks": g_sinks,
        "lam_re": g_lam_re, "lam_im": g_lam_im, "log_dt": g_log_dt, "ssm_b_re": g_b_re, "ssm_b_im": g_b_im,
        "ssm_c_re": _block_diag_c_t(g_wcr)[None], "ssm_c_im": _block_diag_c_t(g_wci)[None],
        "ssm_d": g_ssm_d, "attn_branch_norm": g_abn, "ssm_branch_norm": g_sbn, "norm_ffn": g_norm_ffn,
    }
    early = [n for n in SMALL if n != "norm_mix"]
    packed_e = _pack([small_grads[n] for n in early])
    gs_e = _send_start("small_start_a", [packed_e], [landing(packed_e)], chunked=False)

    to_front = lambda a: jnp.concatenate([a[3 * D_MODEL:], a[2 * D_MODEL:3 * D_MODEL], a[:2 * D_MODEL]], axis=0)
    g_in_p, g_in_pb = _matmul_tn("g_in", dproj, xn, tm=1152, tn=1024, tl=tl_tn, after=gs_e[4])
    g_in_t, g_in_b = to_front(g_in_p), to_front(g_in_pb)
    ex3 = exchange_start("exchange_start_in", [g_in_b])
    dxn = _matmul("d_xn", dproj, wt_in_p, nt=False, tm=tm_ffn, tn=1024, tk=IN_COLS, out_dtype=BF16)
    grad_x2d, dmeta_blk, g_norm_mix = _final_bwd(h0, nm_t + ex3[4][0:1, 0:1], dxn, dh1, _pick(s_len, 512, BLK))
    packed_l = _pack([g_norm_mix, dmeta_blk[PAD:], loss_part])
    gs_l = _send_start("small_start_b", [packed_l], [landing(packed_l)], chunked=False)
    grads, deltas, new_m, new_v = {}, {}, {}, {}

    recv_ffn_in, recv_ffn_out = _send_wait("exchange_wait_ffn", ex1[0], ex1[1], ex1[2], ex1[3], gs_l[4],
                                           chunked=True, row_fns=ffn_fns)
    recv_glu, recv_out = _send_wait("exchange_wait_mix", ex2[0], ex2[1], ex2[2], ex2[3], recv_ffn_in,
                                    chunked=True)
    (recv_in,) = _send_wait("exchange_wait_in", ex3[0], ex3[1], ex3[2], ex3[3], recv_glu, chunked=True)
    big = [("w_in", g_in_t, True, recv_in, _plain_rows), ("w_glu", g_glu_t, True, recv_glu, _plain_rows),
           ("w_out", g_out, False, recv_out, _plain_rows), ("w_ffn_in", g_ffn_in_t, True, recv_ffn_in, _ffn_in_rows),
           ("w_ffn_out", g_ffn_out, False, recv_ffn_out, _plain_rows)]
    for name, g_full, transposed, recv, row_fn in big:
        m_rows = g_full.shape[0] // N_DEV
        own = lax.dynamic_slice(g_full, (row_fn(me, m_rows), 0), (m_rows, g_full.shape[1]))
        g_shard = _sum_slots("sum_" + name, recv, own)
        grads[name] = (g_shard.T if transposed else g_shard)[None]

    def adamw_2d(name):
        shp = weights[name].shape
        if name in ("w_in", "w_ffn_in"):
            as2d, back = (lambda a: a.reshape(shp[-2], shp[-1]).T), (lambda a: a.T.reshape(shp))
        else:
            as2d, back = (lambda a: a.reshape(shp[-2], shp[-1])), (lambda a: a.reshape(shp))
        d, nm, nv = _adamw("adamw_" + name, as2d(weights[name]), as2d(grads[name]), as2d(mom_m[name]),
                           as2d(mom_v[name]))
        deltas[name], new_m[name], new_v[name] = back(d), back(nm), back(nv)
        return d

    for name in ["w_in", "w_glu", "w_out", "w_ffn_in", "w_ffn_out"]:
        last = adamw_2d(name)

    def small_sum(tag, gs, packed, after):
        (gathered,) = _send_wait("small_wait_" + tag, gs[0], gs[1], gs[2], gs[3], after, chunked=False)
        return _sum_gathered("sum_small_" + tag, gathered, packed.shape[0])

    def small_adamw(tag, names):
        view = lambda n, a: jnp.swapaxes(a, -1, -2) if n.startswith("ssm_b") else a
        d, nm, nv = _adamw_many("adamw_small_" + tag, [view(n, weights[n]) for n in names],
                                [view(n, grads[n]) for n in names], [view(n, mom_m[n]) for n in names],
                                [view(n, mom_v[n]) for n in names])
        deltas.update((n, view(n, a)) for n, a in zip(names, d))
        new_m.update((n, view(n, a)) for n, a in zip(names, nm))
        new_v.update((n, view(n, a)) for n, a in zip(names, nv))

    g_sum_e = small_sum("a", gs_e, packed_e, last)
    grads.update(zip(early, _unpack(g_sum_e, [weights[n].shape for n in early])))
    wide = [n for n in early if n.startswith(("ssm_b", "ssm_c"))]
    small_adamw("wide", wide)
    g_sum_l = small_sum("b", gs_l, packed_l, g_sum_e)
    grads["norm_mix"], g_meta, loss_sum = _unpack(g_sum_l, [weights["norm_mix"].shape, (N_META, D_MODEL), (1, 1)])
    small_adamw("rest", [n for n in SMALL if n not in wide])
    grads["meta_tokens"] = lax.dynamic_slice(g_meta, (0, me * BLK), (N_META, BLK))
    adamw_2d("meta_tokens")

    loss = loss_sum[0, 0]
    return (loss, grad_x2d[None], *[grads[n] for n in WEIGHTS], *[deltas[n] for n in WEIGHTS],
            *[new_m[n] for n in WEIGHTS], *[new_v[n] for n in WEIGHTS])
```

```python
import math

import jax
import jax.numpy as jnp
from jax import lax
from jax.experimental import pallas as pl
from jax.experimental.pallas import tpu as pltpu

F32 = jnp.float32
BF16 = jnp.bfloat16

D_MODEL = 1024
N_META = 16
HEAD_DIM = 64
N_Q_HEADS = 16
N_KV_HEADS = 4
Q_W = N_Q_HEADS * HEAD_DIM
KV_W = N_KV_HEADS * HEAD_DIM
SSM_GROUPS = 64
SSM_GROUP_CH = 16
SSM_STATE = 64
N_STATE = SSM_GROUPS * SSM_STATE
D_FF = 2816
IN_COLS = Q_W + 2 * KV_W + 3 * D_MODEL
EPS = 1e-6
BLK = 128
PAD = BLK - N_META
N_DEV = 8
NEG = -1e30
SSM_KB = 8
ST_KB = N_STATE // SSM_KB
LB_KB = ST_KB // BLK
N_LB = N_STATE // BLK

ADAM_LR = 0.001
ADAM_B1 = 0.9
ADAM_B2 = 0.999
ADAM_EPS = 1e-08
ADAM_WD = 0.01
ADAM_STEP = 10

VMEM_LIMIT = 48 * 1024 * 1024
MESH = pl.DeviceIdType.MESH

SMALL = ["norm_mix", "q_norm", "k_norm", "attn_sinks", "lam_re", "lam_im", "log_dt", "ssm_b_re", "ssm_b_im",
         "ssm_c_re", "ssm_c_im", "ssm_d", "attn_branch_norm", "ssm_branch_norm", "norm_ffn"]
WEIGHTS = ["meta_tokens", "norm_mix", "w_in", "q_norm", "k_norm", "attn_sinks", "lam_re", "lam_im", "log_dt",
           "ssm_b_re", "ssm_b_im", "ssm_c_re", "ssm_c_im", "ssm_d", "w_glu", "attn_branch_norm",
           "ssm_branch_norm", "w_out", "norm_ffn", "w_ffn_in", "w_ffn_out"]


def _params(**kw):
    return pltpu.CompilerParams(vmem_limit_bytes=VMEM_LIMIT, **kw)


def _pick(n, cap, mult=16):
    best = None
    for d in range(mult, min(n, cap) + 1, mult):
        if n % d == 0:
            best = d
    assert best is not None, (n, cap, mult)
    return best


def _my_index():
    return 4 * lax.axis_index("x") + 2 * lax.axis_index("y") + lax.axis_index("c")


def _rms(x, g):
    r = lax.rsqrt(jnp.mean(x * x, axis=-1, keepdims=True) + EPS)
    return x * r * g


def _rms_bwd(x, g, dy):
    r = lax.rsqrt(jnp.mean(x * x, axis=-1, keepdims=True) + EPS)
    t = dy * g
    dx = r * t - x * (r * r * r) * jnp.mean(t * x, axis=-1, keepdims=True)
    dg = jnp.sum(dy * (x * r), axis=0, keepdims=True)
    return dx, dg


def _sigmoid(x):
    return jax.nn.sigmoid(x)


def _gelu(x):
    k = math.sqrt(2.0 / math.pi)
    return 0.5 * x * (1.0 + jnp.tanh(k * (x + 0.044715 * (x * x * x))))


def _gelu_grad(x):
    k = math.sqrt(2.0 / math.pi)
    t = jnp.tanh(k * (x + 0.044715 * (x * x * x)))
    return 0.5 * (1.0 + t) + 0.5 * x * (1.0 - t * t) * (k * (1.0 + 3.0 * 0.044715 * (x * x)))


def _head_mean(x, e_ref):
    hi = x.astype(BF16)
    lo = (x - hi.astype(F32)).astype(BF16)
    e = e_ref[...]
    out = []
    for b in range(x.shape[1] // 256):
        sl = slice(256 * b, 256 * b + 256)
        s = (jnp.dot(hi[:, sl], e, preferred_element_type=F32)
             + jnp.dot(lo[:, sl], e, preferred_element_type=F32))
        out.append(s)
    s = out[0] if len(out) == 1 else jnp.concatenate(out, axis=1)
    return s * (1.0 / HEAD_DIM)


def _head_rms(x, g, e_ref):
    r = lax.rsqrt(_head_mean(x * x, e_ref) + EPS)
    return x * r * g


def _head_rms_bwd(x, g, dy, e_ref):
    r = lax.rsqrt(_head_mean(x * x, e_ref) + EPS)
    t = dy * g
    dx = r * t - x * (r * r * r) * _head_mean(t * x, e_ref)
    dg = jnp.sum(dy * (x * r), axis=0, keepdims=True)
    return dx, dg


def _lane_half(shape):
    lane = lax.broadcasted_iota(jnp.int32, shape, len(shape) - 1)
    return (lane >> 6) & 1


def _matmul(name, a, w, *, nt, tm, tn, tk, n=None, w_off=0, res=None, norm_g=None, out_dtype=F32):
    m_dim, k_dim = a.shape
    n_dim = n if n is not None else (w.shape[0] if nt else w.shape[1])
    gm, gn, gk = m_dim // tm, n_dim // tn, k_dim // tk
    assert gm * tm == m_dim and gn * tn == n_dim and gk * tk == k_dim, (name, a.shape, w.shape, tm, tn, tk)
    assert norm_g is None or tn == n_dim
    direct = out_dtype == F32
    dn = (((1,), (1,)), ((), ())) if nt else (((1,), (0,)), ((), ()))

    def body(*refs):
        refs = list(refs)
        a_ref, w_ref = refs[0], refs[1]
        pos = 2
        r_ref = g_ref = on_ref = None
        if res is not None:
            r_ref, pos = refs[pos], pos + 1
        if norm_g is not None:
            g_ref, pos = refs[pos], pos + 1
        o_ref, pos = refs[pos], pos + 1
        if norm_g is not None:
            on_ref, pos = refs[pos], pos + 1
        part = lax.dot_general(a_ref[...], w_ref[...], dn, preferred_element_type=F32)
        if gk == 1:
            r = part if r_ref is None else r_ref[...] + part
            o_ref[...] = r.astype(out_dtype)
            if on_ref is not None:
                on_ref[...] = _rms(r, g_ref[...]).astype(BF16)
            return
        acc = o_ref if direct else refs[pos]
        k = pl.program_id(2)

        @pl.when(k == 0)
        def _():
            acc[...] = part if r_ref is None or not direct else r_ref[...] + part

        @pl.when(k > 0)
        def _():
            acc[...] += part

        @pl.when(k == gk - 1)
        def _():
            if not direct:
                r = acc[...]
                if r_ref is not None:
                    r = r_ref[...] + r
                o_ref[...] = r.astype(out_dtype)
            if on_ref is not None:
                on_ref[...] = _rms(o_ref[...].astype(F32), g_ref[...]).astype(BF16)

    if nt:
        w_spec = pl.BlockSpec((tn, tk), lambda i, j, k: (j + w_off, k))
    else:
        w_spec = pl.BlockSpec((tk, tn), lambda i, j, k: (k, j))
    in_specs = [pl.BlockSpec((tm, tk), lambda i, j, k: (i, k)), w_spec]
    args = [a, w]
    out_spec = pl.BlockSpec((tm, tn), lambda i, j, k: (i, j))
    out_specs, out_shape = [out_spec], [jax.ShapeDtypeStruct((m_dim, n_dim), out_dtype)]
    if res is not None:
        in_specs.append(out_spec)
        args.append(res)
    if norm_g is not None:
        in_specs.append(pl.BlockSpec((1, tn), lambda i, j, k: (0, 0)))
        args.append(norm_g)
        out_specs.append(out_spec)
        out_shape.append(jax.ShapeDtypeStruct((m_dim, n_dim), BF16))
    out = pl.pallas_call(
        body, name=name, grid=(gm, gn, gk),
        in_specs=in_specs, out_specs=out_specs, out_shape=out_shape,
        scratch_shapes=[] if direct or gk == 1 else [pltpu.VMEM((tm, tn), F32)],
        compiler_params=_params(dimension_semantics=("parallel", "parallel", "arbitrary")),
    )(*args)
    return out if norm_g is not None else out[0]


def _matmul_tn(name, a, b, *, tm, tn, tl, after=None, out_rows=None):
    l_dim, m_dim = a.shape
    n_dim = b.shape[1]
    gm, gn, gl = m_dim // tm, n_dim // tn, l_dim // tl
    assert gm * tm == m_dim and gn * tn == n_dim and gl * tl == l_dim, (name, a.shape, b.shape, tm, tn, tl)

    def body(*refs):
        a_ref, b_ref = refs[0], refs[1]
        o_ref, ob_ref = refs[-2], refs[-1]

        @pl.when(pl.program_id(2) == 0)
        def _():
            o_ref[...] = jnp.zeros_like(o_ref)

        o_ref[...] += lax.dot_general(a_ref[...], b_ref[...], (((0,), (0,)), ((), ())),
                                      preferred_element_type=F32)

        @pl.when(pl.program_id(2) == gl - 1)
        def _():
            ob_ref[...] = o_ref[...].astype(BF16)

    out_row = out_rows if out_rows is not None else (lambda i: i)
    out_spec = pl.BlockSpec((tm, tn), lambda i, j, l: (out_row(i), j))
    in_specs = [pl.BlockSpec((tl, tm), lambda i, j, l: (l, i)), pl.BlockSpec((tl, tn), lambda i, j, l: (l, j))]
    args = [a, b]
    if after is not None:
        in_specs.append(pl.BlockSpec(memory_space=pl.ANY))
        args.append(after)
    return pl.pallas_call(
        body, name=name, grid=(gm, gn, gl),
        in_specs=in_specs,
        out_specs=[out_spec, out_spec],
        out_shape=[jax.ShapeDtypeStruct((m_dim, n_dim), F32), jax.ShapeDtypeStruct((m_dim, n_dim), BF16)],
        compiler_params=_params(dimension_semantics=("parallel", "parallel", "arbitrary")),
    )(*args)


def _row_spec(tm, cols, f=None):
    if f is None:
        return pl.BlockSpec((tm, cols), lambda i: (i, 0))
    return pl.BlockSpec((tm, cols), lambda i: (f(i), 0))


def _full_spec(shape):
    nd = len(shape)
    return pl.BlockSpec(shape, lambda i: (0,) * nd)


def _shifted_specs(n_sub, n_blocks):
    return [_row_spec(BLK, D_MODEL, (lambda i, k=k: jnp.clip(n_sub * i - 1 + k, 0, n_blocks - 1)))
            for k in range(n_sub)]


def _embed_norm(x2d, g, tm):
    s_len = x2d.shape[0]
    l_dim = s_len + BLK
    n_sub = tm // BLK

    def body(*refs):
        x_refs, g_ref, h_ref, xn_ref = refs[:n_sub], refs[n_sub], refs[n_sub + 1], refs[n_sub + 2]
        i = pl.program_id(0)
        for k in range(n_sub):
            rows = slice(BLK * k, BLK * k + BLK)
            h = x_refs[k][...] * jnp.where(n_sub * i + k >= 1, 1.0, 0.0)
            h_ref[rows, :] = h
            xn_ref[rows, :] = _rms(h, g_ref[...]).astype(BF16)

    return pl.pallas_call(
        body, name="embed_norm", grid=(l_dim // tm,),
        in_specs=_shifted_specs(n_sub, s_len // BLK) + [_full_spec((1, D_MODEL))],
        out_specs=[_row_spec(tm, D_MODEL), _row_spec(tm, D_MODEL)],
        out_shape=[jax.ShapeDtypeStruct((l_dim, D_MODEL), F32),
                   jax.ShapeDtypeStruct((l_dim, D_MODEL), BF16)],
        compiler_params=_params(),
    )(*([x2d] * n_sub), g)


def _embed_meta(meta_pad, g, h0, xn):
    def body(mp_ref, g_ref, h_in, xn_in, h_ref, xn_ref):
        h_ref[...] = mp_ref[...]
        xn_ref[...] = _rms(mp_ref[...], g_ref[...]).astype(BF16)

    any_spec = pl.BlockSpec(memory_space=pl.ANY)
    return pl.pallas_call(
        body, name="embed_meta", grid=(1,),
        in_specs=[_full_spec((BLK, D_MODEL)), _full_spec((1, D_MODEL)), any_spec, any_spec],
        out_specs=[_row_spec(BLK, D_MODEL), _row_spec(BLK, D_MODEL)],
        out_shape=[jax.ShapeDtypeStruct(h0.shape, F32), jax.ShapeDtypeStruct(xn.shape, BF16)],
        input_output_aliases={2: 0, 3: 1},
        compiler_params=_params(),
    )(meta_pad, g, h0, xn)


KVX_W = 2 * N_KV_HEADS * BLK


def _qk_prep(qkv, q_norm_t, k_norm_t, e_mat, tm):
    l_dim = qkv.shape[0]

    def body(x_ref, qg_ref, kg_ref, e_ref, q_ref, kf_ref, vf_ref):
        x = x_ref[...]
        q = _head_rms(x[:, :Q_W], qg_ref[...], e_ref) * (HEAD_DIM ** -0.5)
        q_ref[...] = q.astype(BF16)
        k = _head_rms(x[:, Q_W:Q_W + KV_W], kg_ref[...], e_ref)
        v = x[:, Q_W + KV_W:Q_W + 2 * KV_W]
        half = _lane_half((tm, BLK))
        for src, dst in ((k, kf_ref), (v, vf_ref)):
            for kv in range(N_KV_HEADS):
                blk = src[:, BLK * (kv // 2):BLK * (kv // 2) + BLK]
                swapped = pltpu.roll(blk, HEAD_DIM, axis=1)
                for e in range(2):
                    val = blk if kv % 2 == e else swapped
                    idx = 2 * kv + e
                    dst[:, BLK * idx:BLK * idx + BLK] = jnp.where(half == e, val, 0.0).astype(BF16)

    return pl.pallas_call(
        body, name="qk_prep", grid=(l_dim // tm,),
        in_specs=[_row_spec(tm, Q_W + 2 * KV_W), _full_spec((1, Q_W)), _full_spec((1, KV_W)),
                  _full_spec((256, 256))],
        out_specs=[_row_spec(tm, Q_W), _row_spec(tm, KVX_W), _row_spec(tm, KVX_W)],
        out_shape=[jax.ShapeDtypeStruct((l_dim, Q_W), BF16),
                   jax.ShapeDtypeStruct((l_dim, KVX_W), BF16),
                   jax.ShapeDtypeStruct((l_dim, KVX_W), BF16)],
        compiler_params=_params(),
    )(qkv, q_norm_t, k_norm_t, e_mat)


def _merge_fwd(attn, zab, gates, abn, sbn, tm):
    l_dim = attn.shape[0]

    def body(a_ref, z_ref, g_ref, an_ref, sn_ref, o_ref):
        z = z_ref[...].astype(F32)
        g = g_ref[...].astype(F32)
        ssm = z[:, :D_MODEL] * _sigmoid(z[:, D_MODEL:])
        merged = (_sigmoid(g[:, :D_MODEL]) * _rms(a_ref[...], an_ref[...])
                  + _sigmoid(g[:, D_MODEL:]) * _rms(ssm, sn_ref[...]))
        o_ref[...] = merged.astype(BF16)

    return pl.pallas_call(
        body, name="merge_fwd", grid=(l_dim // tm,),
        in_specs=[_row_spec(tm, D_MODEL), _row_spec(tm, 2 * D_MODEL), _row_spec(tm, 2 * D_MODEL),
                  _full_spec((1, D_MODEL)), _full_spec((1, D_MODEL))],
        out_specs=_row_spec(tm, D_MODEL),
        out_shape=jax.ShapeDtypeStruct((l_dim, D_MODEL), BF16),
        compiler_params=_params(),
    )(attn, zab, gates, abn, sbn)


def _merge_bwd(attn, zab, gates, abn, sbn, dmerged, tm):
    l_dim = attn.shape[0]

    def body(a_ref, z_ref, g_ref, an_ref, sn_ref, dm_ref, da_ref, dz_ref, dg_ref, dan_ref, dsn_ref):
        @pl.when(pl.program_id(0) == 0)
        def _():
            dan_ref[...] = jnp.zeros_like(dan_ref)
            dsn_ref[...] = jnp.zeros_like(dsn_ref)

        z = z_ref[...].astype(F32)
        g = g_ref[...].astype(F32)
        dm = dm_ref[...].astype(F32)
        attn_v = a_ref[...]
        za, zb = z[:, :D_MODEL], z[:, D_MODEL:]
        sb = _sigmoid(zb)
        ssm = za * sb
        s_ga, s_gs = _sigmoid(g[:, :D_MODEL]), _sigmoid(g[:, D_MODEL:])
        a_n = _rms(attn_v, an_ref[...])
        s_n = _rms(ssm, sn_ref[...])
        dg_ref[:, :D_MODEL] = (dm * a_n * s_ga * (1.0 - s_ga)).astype(BF16)
        dg_ref[:, D_MODEL:] = (dm * s_n * s_gs * (1.0 - s_gs)).astype(BF16)
        dattn, dan = _rms_bwd(attn_v, an_ref[...], dm * s_ga)
        dssm, dsn = _rms_bwd(ssm, sn_ref[...], dm * s_gs)
        da_ref[...] = dattn.astype(BF16)
        dz_ref[:, :D_MODEL] = (dssm * sb).astype(BF16)
        dz_ref[:, D_MODEL:] = (dssm * za * sb * (1.0 - sb)).astype(BF16)
        dan_ref[...] += dan
        dsn_ref[...] += dsn

    return pl.pallas_call(
        body, name="merge_bwd", grid=(l_dim // tm,),
        in_specs=[_row_spec(tm, D_MODEL), _row_spec(tm, 2 * D_MODEL), _row_spec(tm, 2 * D_MODEL),
                  _full_spec((1, D_MODEL)), _full_spec((1, D_MODEL)), _row_spec(tm, D_MODEL)],
        out_specs=[_row_spec(tm, D_MODEL), _row_spec(tm, 2 * D_MODEL), _row_spec(tm, 2 * D_MODEL),
                   _full_spec((1, D_MODEL)), _full_spec((1, D_MODEL))],
        out_shape=[jax.ShapeDtypeStruct((l_dim, D_MODEL), BF16),
                   jax.ShapeDtypeStruct((l_dim, 2 * D_MODEL), BF16),
                   jax.ShapeDtypeStruct((l_dim, IN_COLS), BF16),
                   jax.ShapeDtypeStruct((1, D_MODEL), F32), jax.ShapeDtypeStruct((1, D_MODEL), F32)],
        compiler_params=_params(),
    )(attn, zab, gates, abn, sbn, dmerged)


FF_TILE = D_FF // 2


def _ffn_in_swiglu(hn, wt_ffn_in, tm):
    l_dim = hn.shape[0]
    nt = (((1,), (1,)), ((), ()))

    def body(a_ref, w_ref, gu_ref, act_ref):
        r = lax.dot_general(a_ref[...], w_ref[...], nt, preferred_element_type=F32)
        gate, up = r[:, :FF_TILE], r[:, FF_TILE:]
        gu_ref[...] = r.astype(BF16)
        act_ref[...] = (gate * _sigmoid(gate) * up).astype(BF16)

    return pl.pallas_call(
        body, name="ffn_in_swiglu", grid=(l_dim // tm, 2),
        in_specs=[pl.BlockSpec((tm, D_MODEL), lambda i, j: (i, 0)),
                  pl.BlockSpec((2 * FF_TILE, D_MODEL), lambda i, j: (j, 0))],
        out_specs=[pl.BlockSpec((tm, 2 * FF_TILE), lambda i, j: (i, j)),
                   pl.BlockSpec((tm, FF_TILE), lambda i, j: (i, j))],
        out_shape=[jax.ShapeDtypeStruct((l_dim, 2 * D_FF), BF16), jax.ShapeDtypeStruct((l_dim, D_FF), BF16)],
        compiler_params=_params(dimension_semantics=("parallel", "parallel")),
    )(hn, wt_ffn_in)


def _d_act_swiglu(dh2_b, w_ffn_out, gu, tm):
    l_dim = dh2_b.shape[0]
    nt = (((1,), (1,)), ((), ()))

    def body(d_ref, w_ref, gu_ref, o_ref):
        d = lax.dot_general(d_ref[...], w_ref[...], nt, preferred_element_type=F32)
        gate = gu_ref[:, :FF_TILE].astype(F32)
        up = gu_ref[:, FF_TILE:].astype(F32)
        s = _sigmoid(gate)
        o_ref[:, :FF_TILE] = (d * up * (s * (1.0 + gate * (1.0 - s)))).astype(BF16)
        o_ref[:, FF_TILE:] = (d * (gate * s)).astype(BF16)

    return pl.pallas_call(
        body, name="d_act_swiglu", grid=(l_dim // tm, 2),
        in_specs=[pl.BlockSpec((tm, D_MODEL), lambda i, j: (i, 0)),
                  pl.BlockSpec((FF_TILE, D_MODEL), lambda i, j: (j, 0)),
                  pl.BlockSpec((tm, 2 * FF_TILE), lambda i, j: (i, j))],
        out_specs=pl.BlockSpec((tm, 2 * FF_TILE), lambda i, j: (i, j)),
        out_shape=jax.ShapeDtypeStruct((l_dim, 2 * D_FF), BF16),
        compiler_params=_params(dimension_semantics=("parallel", "parallel")),
    )(dh2_b, w_ffn_out, gu)


def _loss_grad(h2, target2d, tm):
    l_dim = h2.shape[0]
    n_sub = tm // BLK

    def body(*refs):
        h_ref, t_refs = refs[0], refs[1:1 + n_sub]
        d_ref, db_ref, loss_ref = refs[1 + n_sub:]
        i = pl.program_id(0)

        @pl.when(i == 0)
        def _():
            loss_ref[...] = jnp.zeros_like(loss_ref)

        for k in range(n_sub):
            rows = slice(BLK * k, BLK * k + BLK)
            real = jnp.where(n_sub * i + k >= 1, 1.0, 0.0)
            err = (h_ref[rows, :] - t_refs[k][...]) * real
            d = err * (1.0 / D_MODEL)
            d_ref[rows, :] = d
            db_ref[rows, :] = d.astype(BF16)
            loss_ref[...] += 0.5 * jnp.sum(jnp.mean(err * err, axis=-1, keepdims=True), axis=0, keepdims=True)

    return pl.pallas_call(
        body, name="loss_grad", grid=(l_dim // tm,),
        in_specs=[_row_spec(tm, D_MODEL)] + _shifted_specs(n_sub, target2d.shape[0] // BLK),
        out_specs=[_row_spec(tm, D_MODEL), _row_spec(tm, D_MODEL), _full_spec((1, 1))],
        out_shape=[jax.ShapeDtypeStruct((l_dim, D_MODEL), F32), jax.ShapeDtypeStruct((l_dim, D_MODEL), BF16),
                   jax.ShapeDtypeStruct((1, 1), F32)],
        compiler_params=_params(),
    )(h2, *([target2d] * n_sub))


def _norm_bwd_res(name, h, g, dy, dres, tm):
    l_dim = h.shape[0]

    def body(h_ref, g_ref, dy_ref, dr_ref, o_ref, ob_ref, dg_ref):
        @pl.when(pl.program_id(0) == 0)
        def _():
            dg_ref[...] = jnp.zeros_like(dg_ref)

        dx, dg = _rms_bwd(h_ref[...], g_ref[...], dy_ref[...].astype(F32))
        out = dr_ref[...] + dx
        o_ref[...] = out
        ob_ref[...] = out.astype(BF16)
        dg_ref[...] += dg

    return pl.pallas_call(
        body, name=name, grid=(l_dim // tm,),
        in_specs=[_row_spec(tm, D_MODEL), _full_spec((1, D_MODEL)), _row_spec(tm, D_MODEL), _row_spec(tm, D_MODEL)],
        out_specs=[_row_spec(tm, D_MODEL), _row_spec(tm, D_MODEL), _full_spec((1, D_MODEL))],
        out_shape=[jax.ShapeDtypeStruct((l_dim, D_MODEL), F32), jax.ShapeDtypeStruct((l_dim, D_MODEL), BF16),
                   jax.ShapeDtypeStruct((1, D_MODEL), F32)],
        compiler_params=_params(),
    )(h, g, dy, dres)


def _final_bwd(h0, g, dxn, dh1, tm):
    l_dim = h0.shape[0]
    n_sub = tm // BLK
    n_tiles = (l_dim - BLK) // tm

    def sub_specs():
        return [_row_spec(BLK, D_MODEL, (lambda j, k=k: jnp.where(j < n_tiles, n_sub * j + 1 + k, 0)))
                for k in range(n_sub)]

    def body(*refs):
        h_refs, g_ref = refs[:n_sub], refs[n_sub]
        dy_refs, dr_refs = refs[n_sub + 1:2 * n_sub + 1], refs[2 * n_sub + 1:3 * n_sub + 1]
        gx_ref, gm_ref, dg_ref = refs[3 * n_sub + 1:]
        j = pl.program_id(0)

        @pl.when(j == 0)
        def _():
            dg_ref[...] = jnp.zeros_like(dg_ref)

        def block(k):
            dx, dg = _rms_bwd(h_refs[k][...], g_ref[...], dy_refs[k][...].astype(F32))
            dg_ref[...] += dg
            return dr_refs[k][...] + dx

        @pl.when(j < n_tiles)
        def _():
            for k in range(n_sub):
                gx_ref[BLK * k:BLK * k + BLK, :] = block(k)

        @pl.when(j == n_tiles)
        def _():
            gm_ref[...] = block(0)

    return pl.pallas_call(
        body, name="final_bwd", grid=(n_tiles + 1,),
        in_specs=sub_specs() + [_full_spec((1, D_MODEL))] + sub_specs() + sub_specs(),
        out_specs=[_row_spec(tm, D_MODEL, lambda j: jnp.minimum(j, n_tiles - 1)), _full_spec((BLK, D_MODEL)),
                   _full_spec((1, D_MODEL))],
        out_shape=[jax.ShapeDtypeStruct((l_dim - BLK, D_MODEL), F32), jax.ShapeDtypeStruct((BLK, D_MODEL), F32),
                   jax.ShapeDtypeStruct((1, D_MODEL), F32)],
        compiler_params=_params(),
    )(*([h0] * n_sub), g, *([dxn] * n_sub), *([dh1] * n_sub))


def _attn_valid(n):
    shape = (2 * BLK, 3 * BLK)
    qi = lax.broadcasted_iota(jnp.int32, shape, 0) & (BLK - 1)
    col = lax.broadcasted_iota(jnp.int32, shape, 1)
    kj = col & (BLK - 1)
    part = col >> 7
    nn = jnp.zeros(shape, jnp.int32) + n
    meta_ok = (part == 0) & (kj >= PAD) & (nn >= 1)
    prev_ok = (part == 1) & (kj > qi) & (nn >= 2)
    cur_ok = (part == 2) & (kj <= qi) & ((nn >= 1) | (kj >= PAD))
    return meta_ok | prev_ok | cur_ok


def _attn_scores(q_ref, kwin, sk_ref, valid, kv, e):
    qs = jnp.concatenate([q_ref[:, BLK * (2 * kv):BLK * (2 * kv) + BLK],
                          q_ref[:, BLK * (2 * kv + 1):BLK * (2 * kv + 1) + BLK]], axis=0)
    s = lax.dot_general(qs, kwin, (((1,), (1,)), ((), ())), preferred_element_type=F32)
    h0 = 4 * kv + e
    row = lax.broadcasted_iota(jnp.int32, (2 * BLK, 1), 0)
    sink = jnp.where(row < BLK, sk_ref[:, h0:h0 + 1], sk_ref[:, h0 + 2:h0 + 3])
    return qs, jnp.where(valid, s, NEG), sink


def _attn_specs(nb):
    prev = lambda i: jnp.maximum(i - 1, 0)
    zero = lambda i: 0
    kv_specs = [_row_spec(BLK, KVX_W, zero), _row_spec(BLK, KVX_W, prev), _row_spec(BLK, KVX_W)]
    return kv_specs


def _attn_fwd(qn, kf, vf, sinks):
    l_dim = qn.shape[0]
    nb = l_dim // BLK

    def body(q_ref, km_ref, kp_ref, kc_ref, vm_ref, vp_ref, vc_ref, sk_ref, o_ref, lse_ref):
        valid = _attn_valid(pl.program_id(0))
        lane = lax.broadcasted_iota(jnp.int32, (BLK, BLK), 1)
        lse_all = jnp.zeros((BLK, BLK), F32)
        for kv in range(N_KV_HEADS):
            outs = []
            for e in range(2):
                sl = slice(BLK * (2 * kv + e), BLK * (2 * kv + e) + BLK)
                kwin = jnp.concatenate([km_ref[:, sl], kp_ref[:, sl], kc_ref[:, sl]], axis=0)
                vwin = jnp.concatenate([vm_ref[:, sl], vp_ref[:, sl], vc_ref[:, sl]], axis=0)
                _, s, sink = _attn_scores(q_ref, kwin, sk_ref, valid, kv, e)
                m = jnp.maximum(jnp.max(s, axis=-1, keepdims=True), sink)
                ex = jnp.exp(s - m)
                den = jnp.sum(ex, axis=-1, keepdims=True) + jnp.exp(sink - m)
                p = ex * (1.0 / den)
                outs.append(jnp.dot(p.astype(BF16), vwin, preferred_element_type=F32))
                lse = m + jnp.log(den)
                lse_all = jnp.where(lane == 4 * kv + e, lse[:BLK], lse_all)
                lse_all = jnp.where(lane == 4 * kv + 2 + e, lse[BLK:], lse_all)
            o = outs[0] + outs[1]
            o_ref[:, BLK * (2 * kv):BLK * (2 * kv) + BLK] = o[:BLK]
            o_ref[:, BLK * (2 * kv + 1):BLK * (2 * kv + 1) + BLK] = o[BLK:]
        lse_ref[...] = lse_all

    kv_specs = _attn_specs(nb)
    return pl.pallas_call(
        body, name="attn_fwd", grid=(nb,),
        in_specs=[_row_spec(BLK, Q_W)] + kv_specs + kv_specs + [_full_spec((1, N_Q_HEADS))],
        out_specs=[_row_spec(BLK, Q_W), _row_spec(BLK, BLK)],
        out_shape=[jax.ShapeDtypeStruct((l_dim, Q_W), F32), jax.ShapeDtypeStruct((l_dim, BLK), F32)],
        compiler_params=_params(),
    )(qn, kf, kf, kf, vf, vf, vf, sinks)


def _attn_bwd(qn, kf, vf, sinks, lse, attn, dattn):
    l_dim = qn.shape[0]
    nb = l_dim // BLK
    wide = KV_W
    tn = (((0,), (0,)), ((), ()))
    nt = (((1,), (1,)), ((), ()))

    def body(q_ref, km_ref, kp_ref, kc_ref, vm_ref, vp_ref, vc_ref, sk_ref, lse_ref, o_ref, do_ref,
             dq_ref, dkc_ref, dkp_ref, dkm_ref, dvc_ref, dvp_ref, dvm_ref, dsk_ref):
        @pl.when(pl.program_id(0) == 0)
        def _():
            dkm_ref[...] = jnp.zeros_like(dkm_ref)
            dvm_ref[...] = jnp.zeros_like(dvm_ref)
            dsk_ref[...] = jnp.zeros_like(dsk_ref)

        valid = _attn_valid(pl.program_id(0))
        half = _lane_half((BLK, BLK))
        half2 = _lane_half((2 * BLK, BLK))
        half3 = _lane_half((3 * BLK, BLK))
        lane16 = lax.broadcasted_iota(jnp.int32, (1, N_Q_HEADS), 1)
        dsk = jnp.zeros((1, N_Q_HEADS), F32)
        folded_k, folded_v = [], []
        for kv in range(N_KV_HEADS):
            j0, j1 = 2 * kv, 2 * kv + 1
            do0 = do_ref[:, BLK * j0:BLK * j0 + BLK]
            do1 = do_ref[:, BLK * j1:BLK * j1 + BLK]
            do0f, do1f = do0.astype(F32), do1.astype(F32)
            prod0 = do0f * o_ref[:, BLK * j0:BLK * j0 + BLK]
            prod1 = do1f * o_ref[:, BLK * j1:BLK * j1 + BLK]
            dos = jnp.concatenate([do0, do1], axis=0)
            dqs, dks, dvs = [], [], []
            for e in range(2):
                sl = slice(BLK * (2 * kv + e), BLK * (2 * kv + e) + BLK)
                kwin = jnp.concatenate([km_ref[:, sl], kp_ref[:, sl], kc_ref[:, sl]], axis=0)
                vwin = jnp.concatenate([vm_ref[:, sl], vp_ref[:, sl], vc_ref[:, sl]], axis=0)
                qs, s, sink = _attn_scores(q_ref, kwin, sk_ref, valid, kv, e)
                h0 = 4 * kv + e
                lse_rows = jnp.concatenate([lse_ref[:, h0:h0 + 1], lse_ref[:, h0 + 2:h0 + 3]], axis=0)
                p = jnp.exp(s - lse_rows)
                p_sink = jnp.exp(sink - lse_rows)
                delta = jnp.concatenate(
                    [jnp.sum(jnp.where(half == e, prod0, 0.0), axis=-1, keepdims=True),
                     jnp.sum(jnp.where(half == e, prod1, 0.0), axis=-1, keepdims=True)], axis=0)
                dp = lax.dot_general(dos, vwin, nt, preferred_element_type=F32)
                ds = (p * (dp - delta)).astype(BF16)
                pb = p.astype(BF16)
                dqs.append(jnp.dot(ds, kwin, preferred_element_type=F32))
                dks.append(lax.dot_general(ds, qs, tn, preferred_element_type=F32))
                dvs.append(lax.dot_general(pb, dos, tn, preferred_element_type=F32))
                sink_g = -(p_sink * delta)
                g_lo = jnp.sum(sink_g[:BLK], axis=0, keepdims=True)
                g_hi = jnp.sum(sink_g[BLK:], axis=0, keepdims=True)
                dsk = dsk + jnp.where(lane16 == h0, g_lo, 0.0) + jnp.where(lane16 == h0 + 2, g_hi, 0.0)
            dq = jnp.where(half2 == 0, dqs[0], dqs[1])
            dq_ref[:, BLK * j0:BLK * j0 + BLK] = dq[:BLK].astype(BF16)
            dq_ref[:, BLK * j1:BLK * j1 + BLK] = dq[BLK:].astype(BF16)
            own = kv % 2
            folded_k.append(dks[own] + pltpu.roll(dks[1 - own], HEAD_DIM, axis=1))
            folded_v.append(dvs[own] + pltpu.roll(dvs[1 - own], HEAD_DIM, axis=1))
            if own == 1:
                cols = slice(BLK * (kv // 2), BLK * (kv // 2) + BLK)
                for folded, m_ref, p_ref, c_ref in ((folded_k, dkm_ref, dkp_ref, dkc_ref),
                                                    (folded_v, dvm_ref, dvp_ref, dvc_ref)):
                    both = jnp.where(half3 == 0, folded[0], folded[1])
                    m_ref[:, cols] += both[:BLK]
                    p_ref[:, cols] = both[BLK:2 * BLK].astype(BF16)
                    c_ref[:, cols] = both[2 * BLK:].astype(BF16)
                folded_k, folded_v = [], []
        dsk_ref[...] += dsk

    kv_specs = _attn_specs(nb)
    row_wide = _row_spec(BLK, wide)
    acc_wide = _full_spec((BLK, wide))
    big = jax.ShapeDtypeStruct((l_dim, wide), BF16)
    return pl.pallas_call(
        body, name="attn_bwd", grid=(nb,),
        in_specs=[_row_spec(BLK, Q_W)] + kv_specs + kv_specs
        + [_full_spec((1, N_Q_HEADS)), _row_spec(BLK, BLK), _row_spec(BLK, Q_W), _row_spec(BLK, Q_W)],
        out_specs=[_row_spec(BLK, Q_W), row_wide, row_wide, acc_wide, row_wide, row_wide, acc_wide,
                   _full_spec((1, N_Q_HEADS))],
        out_shape=[jax.ShapeDtypeStruct((l_dim, Q_W), BF16), big, big, jax.ShapeDtypeStruct((BLK, wide), F32),
                   big, big, jax.ShapeDtypeStruct((BLK, wide), F32), jax.ShapeDtypeStruct((1, N_Q_HEADS), F32)],
        compiler_params=_params(),
    )(qn, kf, kf, kf, vf, vf, vf, sinks, lse, attn, dattn)


def _qk_bwd(qkv, q_norm_t, k_norm_t, e_mat, dq, dkc, dkp, dkm, dvc, dvp, dvm, dproj):
    l_dim = qkv.shape[0]
    nb = l_dim // BLK
    wide = KV_W

    def body(x_ref, qg_ref, kg_ref, e_ref, dq_ref, dkc_ref, dkp_ref, dkm_ref, dvc_ref, dvp_ref, dvm_ref,
             dproj_ref, o_ref, dqg_ref, dkg_ref):
        i = pl.program_id(0)

        @pl.when(i == 0)
        def _():
            dqg_ref[...] = jnp.zeros_like(dqg_ref)
            dkg_ref[...] = jnp.zeros_like(dkg_ref)

        first = jnp.where(i == 0, 1.0, 0.0)
        not_last = jnp.where(i < nb - 1, 1.0, 0.0)
        dk_x = dkc_ref[...].astype(F32) + not_last * dkp_ref[...].astype(F32) + first * dkm_ref[...]
        dv_x = dvc_ref[...].astype(F32) + not_last * dvp_ref[...].astype(F32) + first * dvm_ref[...]
        x = x_ref[...]
        dqx, dqg = _head_rms_bwd(x[:, :Q_W], qg_ref[...], dq_ref[...].astype(F32) * (HEAD_DIM ** -0.5), e_ref)
        dkx, dkg = _head_rms_bwd(x[:, Q_W:Q_W + KV_W], kg_ref[...], dk_x, e_ref)
        o_ref[:, :Q_W] = dqx.astype(BF16)
        o_ref[:, Q_W:Q_W + KV_W] = dkx.astype(BF16)
        o_ref[:, Q_W + KV_W:] = dv_x.astype(BF16)
        dqg_ref[...] += dqg
        dkg_ref[...] += dkg

    nxt = lambda i: jnp.minimum(i + 1, nb - 1)
    row_wide = _row_spec(BLK, wide)
    nxt_wide = _row_spec(BLK, wide, nxt)
    acc_wide = _full_spec((BLK, wide))
    qkv_w = Q_W + 2 * KV_W
    in_specs = [_row_spec(BLK, qkv_w), _full_spec((1, Q_W)), _full_spec((1, KV_W)),
                _full_spec((256, 256)), _row_spec(BLK, Q_W),
                row_wide, nxt_wide, acc_wide, row_wide, nxt_wide, acc_wide, pl.BlockSpec(memory_space=pl.ANY)]
    return pl.pallas_call(
        body, name="qk_bwd", grid=(nb,),
        in_specs=in_specs,
        out_specs=[pl.BlockSpec((BLK, qkv_w), lambda i: (i, 3 * D_MODEL // qkv_w)),
                   _full_spec((1, Q_W)), _full_spec((1, KV_W))],
        input_output_aliases={len(in_specs) - 1: 0},
        out_shape=[jax.ShapeDtypeStruct(dproj.shape, BF16),
                   jax.ShapeDtypeStruct((1, Q_W), F32), jax.ShapeDtypeStruct((1, KV_W), F32)],
        compiler_params=_params(),
    )(qkv, q_norm_t, k_norm_t, e_mat, dq, dkc, dkp, dkm, dvc, dvp, dvm, dproj)


GRP = 8


def _strided(r, g):
    return pl.ds(r, g, stride=GRP)


def _lane_slab(ref, i, r, g):
    return ref[i, _strided(r, g), :]


def _chunk_carries(xr_ref, xi_ref, i, ar, ai, sqr_ref, sqi_ref, seq_r, seq_i, cin_r, cin_i, sign, reverse):
    g = xr_ref.shape[1] // GRP
    sr = si = None
    for r in (range(GRP - 1, -1, -1) if reverse else range(GRP)):
        xr, xi = _lane_slab(xr_ref, i, r, g), _lane_slab(xi_ref, i, r, g)
        if sr is not None:
            xr, xi = xr + ar * sr - ai * si, xi + ar * si + ai * sr
        sr, si = xr, xi
    row = lax.broadcasted_iota(jnp.int32, sr.shape, 0)
    idx, s = 0, 1
    while s < g:
        br = sqr_ref[idx:idx + 1, BLK * i:BLK * i + BLK]
        bi = sign * sqi_ref[idx:idx + 1, BLK * i:BLK * i + BLK]
        shift, keep = (g - s, row < g - s) if reverse else (s, row >= s)
        pr = jnp.where(keep, pltpu.roll(sr, shift, axis=0), 0.0)
        pi = jnp.where(keep, pltpu.roll(si, shift, axis=0), 0.0)
        sr, si = sr + br * pr - bi * pi, si + br * pi + bi * pr
        idx, s = idx + 1, 2 * s
    return sr + seq_r * cin_r - seq_i * cin_i, si + seq_r * cin_i + seq_i * cin_r


def _ssm_fwd_kb(u, wb_re, wb_im, wc_re, wc_im, d_skip, tabs, q):
    l_dim = u.shape[0]
    nc = l_dim // q
    g = q // GRP

    def body(u_ref, wbr_ref, wbi_ref, wcr_ref, wci_ref, d_ref, a1r_ref, a1i_ref, sqr_ref, sqi_ref,
             seqr_ref, seqi_ref, y_ref, z_ref, sr_ref, si_ref, cr_ref, ci_ref, xr_ref, xi_ref):
        @pl.when(pl.program_id(1) == 0)
        def _():
            cr_ref[...] = jnp.zeros_like(cr_ref)
            ci_ref[...] = jnp.zeros_like(ci_ref)

        u_kb = u_ref[...]
        ub = u_kb.astype(BF16)
        xr = jnp.dot(ub, wbr_ref[0], preferred_element_type=F32)
        xi = jnp.dot(ub, wbi_ref[0], preferred_element_type=F32)
        for i in range(LB_KB):
            xr_ref[i] = xr[:, BLK * i:BLK * i + BLK]
            xi_ref[i] = xi[:, BLK * i:BLK * i + BLK]
        row = lax.broadcasted_iota(jnp.int32, (g, BLK), 0)
        for i in range(LB_KB):
            lanes = slice(BLK * i, BLK * i + BLK)
            ar, ai = a1r_ref[0:1, lanes], a1i_ref[0:1, lanes]
            cin_r, cin_i = cr_ref[0:1, lanes], ci_ref[0:1, lanes]
            tr, ti = _chunk_carries(xr_ref, xi_ref, i, ar, ai, sqr_ref, sqi_ref, seqr_ref[:, lanes],
                                    seqi_ref[:, lanes], cin_r, cin_i, 1.0, reverse=False)
            cr_ref[0:1, lanes] = jnp.sum(jnp.where(row == g - 1, tr, 0.0), axis=0, keepdims=True)
            ci_ref[0:1, lanes] = jnp.sum(jnp.where(row == g - 1, ti, 0.0), axis=0, keepdims=True)
            pr = jnp.where(row == 0, cin_r, pltpu.roll(tr, 1, axis=0))
            pi = jnp.where(row == 0, cin_i, pltpu.roll(ti, 1, axis=0))
            for r in range(GRP):
                pr, pi = (_lane_slab(xr_ref, i, r, g) + ar * pr - ai * pi,
                          _lane_slab(xi_ref, i, r, g) + ar * pi + ai * pr)
                sr_ref[i, _strided(r, g), :] = pr
                si_ref[i, _strided(r, g), :] = pi
        s_r = jnp.concatenate([sr_ref[i] for i in range(LB_KB)], axis=1)
        s_i = jnp.concatenate([si_ref[i] for i in range(LB_KB)], axis=1)
        y = (jnp.dot(s_r.astype(BF16), wcr_ref[0], preferred_element_type=F32)
             - jnp.dot(s_i.astype(BF16), wci_ref[0], preferred_element_type=F32)
             + d_ref[...] * u_kb)
        y_ref[...] = y.astype(BF16)
        z_ref[...] = _gelu(y).astype(BF16)

    chan = pl.BlockSpec((q, BLK), lambda k, c: (c, k))
    wb_spec = pl.BlockSpec((1, BLK, ST_KB), lambda k, c: (k, 0, 0))
    wc_spec = pl.BlockSpec((1, ST_KB, BLK), lambda k, c: (k, 0, 0))
    tab_specs = [pl.BlockSpec((t.shape[0], ST_KB), lambda k, c: (0, k)) for t in tabs[:6]]
    state_spec = pl.BlockSpec((LB_KB, q, BLK), lambda k, c: (k, c, 0))
    state_shape = jax.ShapeDtypeStruct((N_LB, l_dim, BLK), F32)
    return pl.pallas_call(
        body, name="ssm_fwd", grid=(SSM_KB, nc),
        in_specs=[chan, wb_spec, wb_spec, wc_spec, wc_spec, pl.BlockSpec((1, BLK), lambda k, c: (0, k))] + tab_specs,
        out_specs=[chan, chan, state_spec, state_spec],
        out_shape=[jax.ShapeDtypeStruct((l_dim, D_MODEL), BF16), jax.ShapeDtypeStruct((l_dim, D_MODEL), BF16),
                   state_shape, state_shape],
        scratch_shapes=[pltpu.VMEM((8, ST_KB), F32), pltpu.VMEM((8, ST_KB), F32),
                        pltpu.VMEM((LB_KB, q, BLK), F32), pltpu.VMEM((LB_KB, q, BLK), F32)],
        compiler_params=_params(dimension_semantics=("parallel", "arbitrary")),
    )(u, wb_re, wb_im, wc_re, wc_im, d_skip, *tabs[:6])


def _ssm_bwd_kb(dz, y, u, s_re, s_im, wb_re, wb_im, wc_re, wc_im, d_skip, tabs, q, dproj):
    l_dim = u.shape[0]
    nc = l_dim // q
    g = q // GRP

    def body(dz_ref, y_ref, u_ref, sr_ref, si_ref, wbr_ref, wbi_ref, wcr_ref, wci_ref, d_ref,
             a1r_ref, a1i_ref, sqr_ref, sqi_ref, revr_ref, revi_ref, dproj_ref,
             du_ref, dd_ref, dar_ref, dai_ref, dwbr_ref, dwbi_ref, dwcr_ref, dwci_ref,
             cr_ref, ci_ref, gr_ref, gi_ref):
        @pl.when(pl.program_id(1) == 0)
        def _():
            for ref in (cr_ref, ci_ref, dd_ref, dar_ref, dai_ref, dwbr_ref, dwbi_ref, dwcr_ref, dwci_ref):
                ref[...] = jnp.zeros_like(ref)

        tn = (((0,), (0,)), ((), ()))
        nt = (((1,), (1,)), ((), ()))
        u_kb = u_ref[...]
        dy = dz_ref[...].astype(F32) * _gelu_grad(y_ref[...].astype(F32))
        dyb = dy.astype(BF16)
        ub = u_kb.astype(BF16)
        dd_ref[...] += jnp.sum(dy * u_kb, axis=0, keepdims=True)
        ds_r = lax.dot_general(dyb, wcr_ref[0], nt, preferred_element_type=F32)
        ds_i = -lax.dot_general(dyb, wci_ref[0], nt, preferred_element_type=F32)
        for i in range(LB_KB):
            gr_ref[i] = ds_r[:, BLK * i:BLK * i + BLK]
            gi_ref[i] = ds_i[:, BLK * i:BLK * i + BLK]
        row = lax.broadcasted_iota(jnp.int32, (g, BLK), 0)
        for i in range(LB_KB):
            lanes = slice(BLK * i, BLK * i + BLK)
            ar, ai = a1r_ref[0:1, lanes], -a1i_ref[0:1, lanes]
            cin_r, cin_i = cr_ref[0:1, lanes], ci_ref[0:1, lanes]
            tr, ti = _chunk_carries(gr_ref, gi_ref, i, ar, ai, sqr_ref, sqi_ref, revr_ref[:, lanes],
                                    -revi_ref[:, lanes], cin_r, cin_i, -1.0, reverse=True)
            cr_ref[0:1, lanes] = jnp.sum(jnp.where(row == 0, tr, 0.0), axis=0, keepdims=True)
            ci_ref[0:1, lanes] = jnp.sum(jnp.where(row == 0, ti, 0.0), axis=0, keepdims=True)
            nr = jnp.where(row == g - 1, cin_r, pltpu.roll(tr, g - 1, axis=0))
            ni = jnp.where(row == g - 1, cin_i, pltpu.roll(ti, g - 1, axis=0))
            acc_r = jnp.zeros((g, BLK), F32)
            acc_i = jnp.zeros((g, BLK), F32)
            for r in range(GRP - 1, -1, -1):
                s_r, s_i = _lane_slab(sr_ref, i, r, g), _lane_slab(si_ref, i, r, g)
                acc_r = acc_r + (nr * s_r + ni * s_i)
                acc_i = acc_i + (ni * s_r - nr * s_i)
                nr, ni = (_lane_slab(gr_ref, i, r, g) + ar * nr - ai * ni,
                          _lane_slab(gi_ref, i, r, g) + ar * ni + ai * nr)
                gr_ref[i, _strided(r, g), :] = nr
                gi_ref[i, _strided(r, g), :] = ni
            dar_ref[:, lanes] += jnp.sum(acc_r, axis=0, keepdims=True)
            dai_ref[:, lanes] += jnp.sum(acc_i, axis=0, keepdims=True)
        grb = jnp.concatenate([gr_ref[i] for i in range(LB_KB)], axis=1).astype(BF16)
        gib = jnp.concatenate([gi_ref[i] for i in range(LB_KB)], axis=1).astype(BF16)
        srb = jnp.concatenate([sr_ref[i] for i in range(LB_KB)], axis=1).astype(BF16)
        sib = jnp.concatenate([si_ref[i] for i in range(LB_KB)], axis=1).astype(BF16)
        du = (lax.dot_general(grb, wbr_ref[0], nt, preferred_element_type=F32)
              + lax.dot_general(gib, wbi_ref[0], nt, preferred_element_type=F32)
              + d_ref[...] * dy)
        du_ref[...] = du.astype(BF16)
        dwbr_ref[0] += lax.dot_general(ub, grb, tn, preferred_element_type=F32)
        dwbi_ref[0] += lax.dot_general(ub, gib, tn, preferred_element_type=F32)
        dwcr_ref[0] += lax.dot_general(srb, dyb, tn, preferred_element_type=F32)
        dwci_ref[0] -= lax.dot_general(sib, dyb, tn, preferred_element_type=F32)

    chan = pl.BlockSpec((q, BLK), lambda k, c: (nc - 1 - c, k))
    wb_spec = pl.BlockSpec((1, BLK, ST_KB), lambda k, c: (k, 0, 0))
    wc_spec = pl.BlockSpec((1, ST_KB, BLK), lambda k, c: (k, 0, 0))
    tab_in = [tabs[0], tabs[1], tabs[2], tabs[3], tabs[6], tabs[7]]
    tab_specs = [pl.BlockSpec((t.shape[0], ST_KB), lambda k, c: (0, k)) for t in tab_in]
    vec = pl.BlockSpec((1, BLK), lambda k, c: (0, k))
    svec = pl.BlockSpec((1, ST_KB), lambda k, c: (0, k))
    state_spec = pl.BlockSpec((LB_KB, q, BLK), lambda k, c: (k, nc - 1 - c, 0))
    du_spec = pl.BlockSpec((q, BLK), lambda k, c: (nc - 1 - c, 2 * D_MODEL // BLK + k))
    in_specs = ([chan, chan, chan, state_spec, state_spec, wb_spec, wb_spec, wc_spec, wc_spec, vec] + tab_specs
                + [pl.BlockSpec(memory_space=pl.ANY)])
    return pl.pallas_call(
        body, name="ssm_bwd", grid=(SSM_KB, nc),
        in_specs=in_specs,
        out_specs=[du_spec, vec, svec, svec, wb_spec, wb_spec, wc_spec, wc_spec],
        input_output_aliases={len(in_specs) - 1: 0},
        out_shape=[jax.ShapeDtypeStruct(dproj.shape, BF16), jax.ShapeDtypeStruct((1, D_MODEL), F32),
                   jax.ShapeDtypeStruct((1, N_STATE), F32), jax.ShapeDtypeStruct((1, N_STATE), F32),
                   jax.ShapeDtypeStruct((SSM_KB, BLK, ST_KB), F32), jax.ShapeDtypeStruct((SSM_KB, BLK, ST_KB), F32),
                   jax.ShapeDtypeStruct((SSM_KB, ST_KB, BLK), F32), jax.ShapeDtypeStruct((SSM_KB, ST_KB, BLK), F32)],
        scratch_shapes=[pltpu.VMEM((8, ST_KB), F32), pltpu.VMEM((8, ST_KB), F32),
                        pltpu.VMEM((LB_KB, q, BLK), F32), pltpu.VMEM((LB_KB, q, BLK), F32)],
        compiler_params=_params(dimension_semantics=("parallel", "arbitrary")),
    )(dz, y, u, s_re, s_im, wb_re, wb_im, wc_re, wc_im, d_skip, *tab_in, dproj)


def _discretize(lam_re, lam_im, log_dt, b_re, b_im):
    dt = jnp.exp(log_dt)[:, None]
    mag = jnp.exp(lam_re * dt)
    ar, ai = mag * jnp.cos(lam_im * dt), mag * jnp.sin(lam_im * dt)
    den = lam_re * lam_re + lam_im * lam_im
    nr, ni = ar - 1.0, ai
    fr, fi = (nr * lam_re + ni * lam_im) / den, (ni * lam_re - nr * lam_im) / den
    bbar_re = fr[..., None] * b_re - fi[..., None] * b_im
    bbar_im = fr[..., None] * b_im + fi[..., None] * b_re
    return ar, ai, bbar_re, bbar_im


def _block_diag_b(bbar):
    eye = jnp.eye(8, dtype=bbar.dtype)
    return jnp.einsum("kgpc,gh->kgchp", bbar.reshape(8, 8, SSM_STATE, SSM_GROUP_CH), eye).reshape(8, BLK, ST_KB)


def _block_diag_b_t(dwb):
    eye = jnp.eye(8, dtype=dwb.dtype)
    return jnp.einsum("kgchp,gh->kgpc", dwb.reshape(8, 8, SSM_GROUP_CH, 8, SSM_STATE), eye).reshape(
        SSM_GROUPS, SSM_STATE, SSM_GROUP_CH)


def _block_diag_c(c):
    eye = jnp.eye(8, dtype=c.dtype)
    return jnp.einsum("kgcp,gh->kgphc", c.reshape(8, 8, SSM_GROUP_CH, SSM_STATE), eye).reshape(8, ST_KB, BLK)


def _block_diag_c_t(dwc):
    eye = jnp.eye(8, dtype=dwc.dtype)
    return jnp.einsum("kgphc,gh->kgcp", dwc.reshape(8, 8, SSM_STATE, 8, SSM_GROUP_CH), eye).reshape(
        SSM_GROUPS, SSM_GROUP_CH, SSM_STATE)


def _powers(br, bi, n):
    pr, pi = br, bi
    cr, ci = br, bi
    while pr.shape[0] < n:
        pr, pi = (jnp.concatenate([pr, pr * cr - pi * ci], axis=0),
                  jnp.concatenate([pi, pr * ci + pi * cr], axis=0))
        cr, ci = cr * cr - ci * ci, 2.0 * cr * ci
    return pr[:n], pi[:n]


def _powers_desc(br, bi, n):
    pr, pi = br, bi
    cr, ci = br, bi
    while pr.shape[0] < n:
        pr, pi = (jnp.concatenate([pr * cr - pi * ci, pr], axis=0),
                  jnp.concatenate([pr * ci + pi * cr, pi], axis=0))
        cr, ci = cr * cr - ci * ci, 2.0 * cr * ci
    return pr, pi


def _power_tables(ar, ai, g):
    a1r, a1i = _powers(ar, ai, GRP)
    seqr, seqi = _powers(a1r[GRP - 1:], a1i[GRP - 1:], g)
    g2 = 1 << (g - 1).bit_length()
    revr, revi = _powers_desc(a1r[GRP - 1:], a1i[GRP - 1:], g2)
    revr, revi = revr[g2 - g:], revi[g2 - g:]
    sq_r, sq_i = [seqr[0:1]], [seqi[0:1]]
    while len(sq_r) < 8:
        r, i = sq_r[-1], sq_i[-1]
        sq_r.append(r * r - i * i)
        sq_i.append(2.0 * r * i)
    sqr, sqi = jnp.concatenate(sq_r, axis=0), jnp.concatenate(sq_i, axis=0)
    return a1r, a1i, sqr, sqi, seqr, seqi, revr, revi


HBM_SPEC = pl.BlockSpec(memory_space=pltpu.HBM)
SEM_SPEC = pl.BlockSpec(memory_space=pltpu.SEMAPHORE)
DATAFLOW = pltpu.SideEffectType.DATAFLOW_SIDE_EFFECTING


def _plain_rows(p, m):
    return p * m


def _ffn_in_rows(p, m):
    return ((p & 3) >> 1) * (4 * m) + (p >> 2) * (2 * m) + (p & 1) * m


def _peer_copies(src_refs, land_refs, send_sems, recv_sems, chunked, row_fns):
    x, y, c = lax.axis_index("x"), lax.axis_index("y"), lax.axis_index("c")
    me = 4 * x + 2 * y + c
    copies = []
    for a, (src, land) in enumerate(zip(src_refs, land_refs)):
        m = land.shape[0] // N_DEV
        for k in range(N_DEV - 1):
            rel = k + 1
            bx, by, bc = (rel >> 2) & 1, (rel >> 1) & 1, rel & 1
            peer = (x + bx - 2 * x * bx, y + by - 2 * y * by, c + bc - 2 * c * bc)
            p_idx = 4 * peer[0] + 2 * peer[1] + peer[2]
            copies.append(pltpu.make_async_remote_copy(
                src_ref=src.at[pl.ds(row_fns[a](p_idx, m), m), :] if chunked else src,
                dst_ref=land.at[pl.ds(me * m if chunked else row_fns[a](me, m), m), :],
                send_sem=send_sems[a * (N_DEV - 1) + k], recv_sem=recv_sems[a * (N_DEV - 1) + k],
                device_id=peer, device_id_type=MESH))
    return copies


def _send_start(name, srcs, lands, chunked, row_fns=None):
    n = len(srcs)
    ns = n * (N_DEV - 1)
    row_fns = row_fns or [_plain_rows] * n

    def body(*refs):
        src_refs, land_refs = refs[:n], refs[n:2 * n]
        send_sems, recv_sems = refs[2 * n:2 * n + ns], refs[2 * n + ns:2 * n + 2 * ns]
        token = refs[-1]
        for cp in _peer_copies(src_refs, land_refs, send_sems, recv_sems, chunked, row_fns):
            cp.start()
        token[...] = jnp.zeros_like(token)

    ins = [pltpu.with_memory_space_constraint(a, pltpu.HBM) for a in list(srcs) + list(lands)]
    out = pl.pallas_call(
        body, name=name,
        in_specs=[HBM_SPEC] * (2 * n),
        out_specs=[SEM_SPEC] * (2 * ns) + [HBM_SPEC] * (2 * n) + [pl.BlockSpec(memory_space=pltpu.VMEM)],
        out_shape=[pltpu.SemaphoreType.DMA(())] * (2 * ns)
        + [pltpu.HBM(a.shape, a.dtype) for a in list(srcs) + list(lands)]
        + [jax.ShapeDtypeStruct((8, BLK), F32)],
        input_output_aliases={i: i + 2 * ns for i in range(2 * n)},
        compiler_params=pltpu.CompilerParams(has_side_effects=DATAFLOW),
    )(*ins)
    return out[:ns], out[ns:2 * ns], out[2 * ns:2 * ns + n], out[2 * ns + n:2 * ns + 2 * n], out[-1]


def _send_wait(name, send_sems, recv_sems, srcs, lands, after, chunked, row_fns=None):
    n = len(srcs)
    ns = n * (N_DEV - 1)
    row_fns = row_fns or [_plain_rows] * n

    def body(*refs):
        src_refs, land_refs = refs[:n], refs[n:2 * n]
        s_sems, r_sems = refs[2 * n:2 * n + ns], refs[2 * n + ns:2 * n + 2 * ns]
        land_outs, local_sems = refs[-1 - n:-1], refs[-1]
        me = _my_index()
        own = []
        for a in range(n):
            m = land_outs[a].shape[0] // N_DEV
            rows = row_fns[a](me, m)
            own.append(pltpu.make_async_copy(
                src_refs[a].at[pl.ds(rows, m), :] if chunked else src_refs[a],
                land_outs[a].at[pl.ds(me * m if chunked else rows, m), :], local_sems.at[a]))
            own[-1].start()
        copies = _peer_copies(src_refs, land_refs, s_sems, r_sems, chunked, row_fns)
        for cp in copies:
            cp.wait_send()
        for cp in copies:
            cp.wait_recv()
        for cp in own:
            cp.wait()

    out = pl.pallas_call(
        body, name=name,
        in_specs=[HBM_SPEC] * (2 * n) + [SEM_SPEC] * (2 * ns) + [pl.BlockSpec(memory_space=pl.ANY)],
        out_specs=[HBM_SPEC] * (2 * n),
        out_shape=[pltpu.HBM(a.shape, a.dtype) for a in list(srcs) + list(lands)],
        input_output_aliases={i: i for i in range(2 * n)},
        scratch_shapes=[pltpu.SemaphoreType.DMA((n,))],
        compiler_params=pltpu.CompilerParams(has_side_effects=DATAFLOW),
    )(*srcs, *lands, *send_sems, *recv_sems, after)
    return out[n:]


def _sum_slots(name, recv, own):
    m, ncol = own.shape
    tr = m // 2 if (m // 2) % 16 == 0 else m
    g = m // tr

    def body(*refs):
        slots, own_ref, o_ref = refs[:N_DEV], refs[N_DEV], refs[N_DEV + 1]
        me = _my_index()
        tot = None
        for s in range(N_DEV):
            v = jnp.where(me == s, own_ref[...], slots[s][...].astype(F32))
            tot = v if tot is None else tot + v
        o_ref[...] = tot

    def slot_spec(s):
        return pl.BlockSpec((tr, ncol), lambda i: (s * g + i, 0))

    return pl.pallas_call(
        body, name=name, grid=(g,),
        in_specs=[slot_spec(s) for s in range(N_DEV)] + [pl.BlockSpec((tr, ncol), lambda i: (i, 0))],
        out_specs=pl.BlockSpec((tr, ncol), lambda i: (i, 0)),
        out_shape=jax.ShapeDtypeStruct((m, ncol), F32),
        compiler_params=_params(),
    )(*([recv] * N_DEV), own)


def _sum_gathered(name, gathered, rows):
    tr = _pick(rows, 512, 8)
    g = rows // tr

    def body(*refs):
        o_ref = refs[N_DEV]
        tot = refs[0][...]
        for s in range(1, N_DEV):
            tot = tot + refs[s][...]
        o_ref[...] = tot

    return pl.pallas_call(
        body, name=name, grid=(g,),
        in_specs=[pl.BlockSpec((tr, BLK), (lambda i, s=s: (s * g + i, 0))) for s in range(N_DEV)],
        out_specs=pl.BlockSpec((tr, BLK), lambda i: (i, 0)),
        out_shape=jax.ShapeDtypeStruct((rows, BLK), F32),
        compiler_params=_params(),
    )(*([gathered] * N_DEV))


def _adamw(name, w, g, m, v):
    r, c = w.shape
    tr = _pick(r, 256, 8) if r % 8 == 0 else r
    c1 = 1.0 - ADAM_B1 ** ADAM_STEP
    c2 = 1.0 - ADAM_B2 ** ADAM_STEP

    def body(w_ref, g_ref, m_ref, v_ref, d_ref, nm_ref, nv_ref):
        gv = g_ref[...]
        nm = ADAM_B1 * m_ref[...] + (1.0 - ADAM_B1) * gv
        nv = ADAM_B2 * v_ref[...] + (1.0 - ADAM_B2) * (gv * gv)
        m_hat = nm / c1
        v_hat = nv / c2
        d_ref[...] = -ADAM_LR * (m_hat / (jnp.sqrt(v_hat) + ADAM_EPS) + ADAM_WD * w_ref[...])
        nm_ref[...] = nm
        nv_ref[...] = nv

    spec = pl.BlockSpec((tr, c), lambda i: (i, 0))
    shape = jax.ShapeDtypeStruct((r, c), F32)
    return pl.pallas_call(
        body, name=name, grid=(r // tr,),
        in_specs=[spec] * 4, out_specs=[spec] * 3, out_shape=[shape] * 3,
        compiler_params=_params(),
    )(w, g, m, v)


def _adamw_many(name, ws, gs, ms, vs):
    n = len(ws)
    c1 = 1.0 - ADAM_B1 ** ADAM_STEP
    c2 = 1.0 - ADAM_B2 ** ADAM_STEP

    def body(*refs):
        for a in range(n):
            w_ref, g_ref, m_ref, v_ref = refs[a], refs[n + a], refs[2 * n + a], refs[3 * n + a]
            d_ref, nm_ref, nv_ref = refs[4 * n + a], refs[5 * n + a], refs[6 * n + a]
            gv = g_ref[...]
            nm = ADAM_B1 * m_ref[...] + (1.0 - ADAM_B1) * gv
            nv = ADAM_B2 * v_ref[...] + (1.0 - ADAM_B2) * (gv * gv)
            d_ref[...] = -ADAM_LR * ((nm / c1) / (jnp.sqrt(nv / c2) + ADAM_EPS) + ADAM_WD * w_ref[...])
            nm_ref[...] = nm
            nv_ref[...] = nv

    specs = [_full_spec(w.shape) for w in ws]
    shapes = [jax.ShapeDtypeStruct(w.shape, F32) for w in ws]
    out = pl.pallas_call(
        body, name=name, grid=(1,),
        in_specs=specs * 4, out_specs=specs * 3, out_shape=shapes * 3,
        compiler_params=_params(),
    )(*ws, *gs, *ms, *vs)
    return out[:n], out[n:2 * n], out[2 * n:]


PACK_ROWS = 128


def _pack(parts):
    flat = []
    for p in parts:
        v = p.reshape(-1)
        flat.append(jnp.pad(v, (0, (-v.shape[0]) % BLK)))
    v = jnp.concatenate(flat)
    v = jnp.pad(v, (0, (-v.shape[0]) % (PACK_ROWS * BLK)))
    return v.reshape(-1, BLK)


def _unpack(buf, shapes):
    flat = buf.reshape(-1)
    out, off = [], 0
    for shp in shapes:
        size = math.prod(shp)
        out.append(flat[off:off + size].reshape(shp))
        off += size + (-size) % BLK
    return out


def kernel(x, meta_tokens, norm_mix, w_in, q_norm, k_norm, attn_sinks, lam_re, lam_im, log_dt, ssm_b_re, ssm_b_im, ssm_c_re, ssm_c_im, ssm_d, w_glu, attn_branch_norm, ssm_branch_norm, w_out, norm_ffn, w_ffn_in, w_ffn_out, loss_target, m_meta_tokens, m_norm_mix, m_w_in, m_q_norm, m_k_norm, m_attn_sinks, m_lam_re, m_lam_im, m_log_dt, m_ssm_b_re, m_ssm_b_im, m_ssm_c_re, m_ssm_c_im, m_ssm_d, m_w_glu, m_attn_branch_norm, m_ssm_branch_norm, m_w_out, m_norm_ffn, m_w_ffn_in, m_w_ffn_out, v_meta_tokens, v_norm_mix, v_w_in, v_q_norm, v_k_norm, v_attn_sinks, v_lam_re, v_lam_im, v_log_dt, v_ssm_b_re, v_ssm_b_im, v_ssm_c_re, v_ssm_c_im, v_ssm_d, v_w_glu, v_attn_branch_norm, v_ssm_branch_norm, v_w_out, v_norm_ffn, v_w_ffn_in, v_w_ffn_out):
    args = dict(locals())
    weights = {n: args[n] for n in WEIGHTS}
    mom_m = {n: args["m_" + n] for n in WEIGHTS}
    mom_v = {n: args["v_" + n] for n in WEIGHTS}

    x2d = x[0]
    target2d = loss_target[0]
    s_len = x2d.shape[0]
    l_dim = s_len + BLK
    tm_row = _pick(l_dim, 320)
    tm_mm = _pick(l_dim, 1040)
    tl_tn = _pick(l_dim, 2080)
    tm_ffn = _pick(l_dim, 640)
    tm_big = _pick(l_dim, 2080)
    tm_shift = _pick(l_dim, 640, BLK)

    shard_in = w_in[0].T.astype(BF16)
    shard_glu = w_glu[0].T.astype(BF16)
    shard_out = w_out[0].astype(BF16)
    shard_ffn_in = w_ffn_in[0].T.astype(BF16)
    shard_ffn_out = w_ffn_out[0].astype(BF16)
    shard_meta = meta_tokens.T
    me = _my_index()

    def landing(shard):
        return lax.empty((N_DEV * shard.shape[0], shard.shape[1]), shard.dtype)

    first = [shard_in, shard_meta]
    ga = _send_start("gather_start_a", first, [landing(s) for s in first], chunked=False)
    later = [shard_glu + ga[4][0:1, 0:1].astype(BF16), shard_out, shard_ffn_in, shard_ffn_out]
    later_fns = [_plain_rows, _plain_rows, _ffn_in_rows, _plain_rows]
    gb = _send_start("gather_start_b", later, [landing(s) for s in later], chunked=False,
                     row_fns=later_fns)

    nm_t = norm_mix + (ga[4][0:1, 0:1] + gb[4][0:1, 0:1])
    qn_t, kn_t = jnp.tile(q_norm, (1, N_Q_HEADS)), jnp.tile(k_norm, (1, N_KV_HEADS))
    e_mat = jnp.kron(jnp.eye(4, dtype=F32), jnp.ones((HEAD_DIM, HEAD_DIM), F32)).astype(BF16)

    def disc(lr, li, ldt, br, bi):
        return _discretize(lr[0], li[0], ldt[0], br[0], bi[0])

    (abar_re, abar_im, bbar_re, bbar_im), disc_vjp = jax.vjp(disc, lam_re, lam_im, log_dt, ssm_b_re, ssm_b_im)
    wb_re, wb_im = _block_diag_b(bbar_re).astype(BF16), _block_diag_b(bbar_im).astype(BF16)
    wc_re, wc_im = _block_diag_c(ssm_c_re[0]).astype(BF16), _block_diag_c(ssm_c_im[0]).astype(BF16)
    q_ssm = _pick(l_dim, 640, 64)
    tabs = _power_tables(abar_re.reshape(1, N_STATE), abar_im.reshape(1, N_STATE), q_ssm // GRP)

    h0, xn = _embed_norm(x2d, nm_t, tm_shift)
    wt_in, meta_t = _send_wait("gather_wait_a", ga[0], ga[1], ga[2], ga[3], xn, chunked=False)
    meta_pad = jnp.pad(meta_t.T, ((PAD, 0), (0, 0)))
    h0, xn = _embed_meta(meta_pad, nm_t, h0, xn)
    qkv_w, u_end = Q_W + 2 * KV_W, Q_W + 2 * KV_W + D_MODEL
    wt_in_p = jnp.concatenate([wt_in[u_end:], wt_in[qkv_w:u_end], wt_in[:qkv_w]], axis=0)
    qkv = _matmul("proj_qkv", xn, wt_in_p, nt=True, tm=tm_big, tn=qkv_w, tk=D_MODEL, n=qkv_w, w_off=2)
    u = _matmul("proj_u", xn, wt_in_p, nt=True, tm=tm_big, tn=D_MODEL, tk=D_MODEL, n=D_MODEL, w_off=2)
    gates = _matmul("proj_gates", xn, wt_in_p, nt=True, tm=tm_big, tn=D_MODEL, tk=D_MODEL, n=2 * D_MODEL, w_off=0,
                    out_dtype=BF16)
    qn, kf, vf = _qk_prep(qkv, qn_t, kn_t, e_mat, tm_row)
    attn, lse = _attn_fwd(qn, kf, vf, attn_sinks)
    y, z, s_re, s_im = _ssm_fwd_kb(u, wb_re, wb_im, wc_re, wc_im, ssm_d, tabs, q_ssm)
    wt_glu, w_out_f, wt_ffn_in, w_ffn_out_f = _send_wait("gather_wait_b", gb[0], gb[1], gb[2], gb[3], z,
                                                         chunked=False, row_fns=later_fns)
    zab = _matmul("glu_proj", z, wt_glu, nt=True, tm=tm_big, tn=1024, tk=D_MODEL, out_dtype=BF16)
    merged = _merge_fwd(attn, zab, gates, attn_branch_norm, ssm_branch_norm, tm_row)
    h1, hn = _matmul("out_proj", merged, w_out_f, nt=False, tm=tm_mm, tn=1024, tk=D_MODEL, res=h0,
                     norm_g=norm_ffn)
    gu, act = _ffn_in_swiglu(hn, wt_ffn_in, tm_ffn)
    h2 = _matmul("ffn_out", act, w_ffn_out_f, nt=False, tm=tm_mm, tn=1024, tk=D_FF, res=h1)
    dh2, dh2_b, loss_part = _loss_grad(h2, target2d, tm_shift)

    dgu = _d_act_swiglu(dh2_b, w_ffn_out_f, gu, tm_ffn)

    def exchange_start(name, grads_b, row_fns=None):
        return _send_start(name, grads_b, [lax.empty(g.shape, BF16) for g in grads_b], chunked=True,
                           row_fns=row_fns)

    g_ffn_out, g_ffn_out_b = _matmul_tn("g_ffn_out", act, dh2_b, tm=1408, tn=1024, tl=tl_tn)
    g_ffn_in_t, g_ffn_in_b = _matmul_tn("g_ffn_in", dgu, hn, tm=1408, tn=1024, tl=tl_tn)
    ffn_fns = [_ffn_in_rows, _plain_rows]
    ex1 = exchange_start("exchange_start_ffn", [g_ffn_in_b, g_ffn_out_b], ffn_fns)
    dhn = _matmul("d_hn", dgu, wt_ffn_in, nt=False, tm=tm_ffn, tn=1024, tk=2 * D_FF, out_dtype=BF16)
    dh1, dh1_b, g_norm_ffn = _norm_bwd_res("ffn_norm_bwd", h1, norm_ffn + ex1[4][0:1, 0:1], dhn, dh2, tm_row)
    dmerged = _matmul("d_merged", dh1_b, w_out_f, nt=True, tm=tm_big, tn=1024, tk=D_MODEL, out_dtype=BF16)
    dattn, dzab, dproj, g_abn, g_sbn = _merge_bwd(attn, zab, gates, attn_branch_norm, ssm_branch_norm,
                                                    dmerged, tm_row)
    g_out, g_out_b = _matmul_tn("g_out", merged, dh1_b, tm=1024, tn=1024, tl=tl_tn)
    dz = _matmul("d_z", dzab, wt_glu, nt=False, tm=tm_mm, tn=1024, tk=2 * D_MODEL, out_dtype=BF16)
    g_glu_t, g_glu_b = _matmul_tn("g_glu", dzab, z, tm=1024, tn=1024, tl=tl_tn)
    ex2 = exchange_start("exchange_start_mix", [g_glu_b, g_out_b])
    dproj, g_ssm_d, g_ar, g_ai, g_wbr, g_wbi, g_wcr, g_wci = _ssm_bwd_kb(
        dz, y, u, s_re, s_im, wb_re, wb_im, wc_re, wc_im, ssm_d + ex2[4][0:1, 0:1], tabs, q_ssm, dproj)
    dq, dkc, dkp, dkm, dvc, dvp, dvm, g_sinks = _attn_bwd(qn, kf, vf, attn_sinks, lse, attn, dattn)
    dproj, g_qn_t, g_kn_t = _qk_bwd(qkv, qn_t, kn_t, e_mat, dq, dkc, dkp, dkm, dvc, dvp, dvm, dproj)
    g_lam_re, g_lam_im, g_log_dt, g_b_re, g_b_im = disc_vjp(
        (g_ar.reshape(SSM_GROUPS, SSM_STATE), g_ai.reshape(SSM_GROUPS, SSM_STATE),
         _block_diag_b_t(g_wbr), _block_diag_b_t(g_wbi)))
    small_grads = {
        "q_norm": g_qn_t.reshape(N_Q_HEADS, HEAD_DIM).sum(0)[None],
        "k_norm": g_kn_t.reshape(N_KV_HEADS, HEAD_DIM).sum(0)[None], "attn_sinks": g_sinks,
        "lam_re": g_lam_re, "lam_im": g_lam_im, "log_dt": g_log_dt, "ssm_b_re": g_b_re, "ssm_b_im": g_b_im,
        "ssm_c_re": _block_diag_c_t(g_wcr)[None], "ssm_c_im": _block_diag_c_t(g_wci)[None],
        "ssm_d": g_ssm_d, "attn_branch_norm": g_abn, "ssm_branch_norm": g_sbn, "norm_ffn": g_norm_ffn,
    }
    early = [n for n in SMALL if n != "norm_mix"]
    packed_e = _pack([small_grads[n] for n in early])
    gs_e = _send_start("small_start_a", [packed_e], [landing(packed_e)], chunked=False)

    blocks_g, blocks_u, blocks_q = 2 * D_MODEL // 512, D_MODEL // 512, qkv_w // 512
    back = lambda i: jnp.where(i < blocks_g, i + blocks_q + blocks_u,
                               jnp.where(i < blocks_g + blocks_u, i - blocks_g + blocks_q, i - blocks_g - blocks_u))
    g_in_t, g_in_b = _matmul_tn("g_in", dproj, xn, tm=512, tn=1024, tl=tl_tn, after=gs_e[4], out_rows=back)
    ex3 = exchange_start("exchange_start_in", [g_in_b])
    dxn = _matmul("d_xn", dproj, wt_in_p, nt=False, tm=tm_ffn, tn=1024, tk=IN_COLS, out_dtype=BF16)
    grad_x2d, dmeta_blk, g_norm_mix = _final_bwd(h0, nm_t + ex3[4][0:1, 0:1], dxn, dh1, _pick(s_len, 512, BLK))
    packed_l = _pack([g_norm_mix, dmeta_blk[PAD:], loss_part])
    gs_l = _send_start("small_start_b", [packed_l], [landing(packed_l)], chunked=False)
    grads, deltas, new_m, new_v = {}, {}, {}, {}

    recv_ffn_in, recv_ffn_out = _send_wait("exchange_wait_ffn", ex1[0], ex1[1], ex1[2], ex1[3], gs_l[4],
                                           chunked=True, row_fns=ffn_fns)
    recv_glu, recv_out = _send_wait("exchange_wait_mix", ex2[0], ex2[1], ex2[2], ex2[3], recv_ffn_in,
                                    chunked=True)
    (recv_in,) = _send_wait("exchange_wait_in", ex3[0], ex3[1], ex3[2], ex3[3], recv_glu, chunked=True)
    big = [("w_in", g_in_t, True, recv_in, _plain_rows), ("w_glu", g_glu_t, True, recv_glu, _plain_rows),
           ("w_out", g_out, False, recv_out, _plain_rows), ("w_ffn_in", g_ffn_in_t, True, recv_ffn_in, _ffn_in_rows),
           ("w_ffn_out", g_ffn_out, False, recv_ffn_out, _plain_rows)]
    for name, g_full, transposed, recv, row_fn in big:
        m_rows = g_full.shape[0] // N_DEV
        own = lax.dynamic_slice(g_full, (row_fn(me, m_rows), 0), (m_rows, g_full.shape[1]))
        g_shard = _sum_slots("sum_" + name, recv, own)
        grads[name] = (g_shard.T if transposed else g_shard)[None]

    def adamw_2d(name):
        shp = weights[name].shape
        if name in ("w_in", "w_ffn_in"):
            as2d, back = (lambda a: a.reshape(shp[-2], shp[-1]).T), (lambda a: a.T.reshape(shp))
        else:
            as2d, back = (lambda a: a.reshape(shp[-2], shp[-1])), (lambda a: a.reshape(shp))
        d, nm, nv = _adamw("adamw_" + name, as2d(weights[name]), as2d(grads[name]), as2d(mom_m[name]),
                           as2d(mom_v[name]))
        deltas[name], new_m[name], new_v[name] = back(d), back(nm), back(nv)
        return d

    for name in ["w_in", "w_glu", "w_out", "w_ffn_in", "w_ffn_out"]:
        last = adamw_2d(name)

    def small_sum(tag, gs, packed, after):
        (gathered,) = _send_wait("small_wait_" + tag, gs[0], gs[1], gs[2], gs[3], after, chunked=False)
        return _sum_gathered("sum_small_" + tag, gathered, packed.shape[0])

    def small_adamw(tag, names):
        view = lambda n, a: jnp.swapaxes(a, -1, -2) if n.startswith("ssm_b") else a
        d, nm, nv = _adamw_many("adamw_small_" + tag, [view(n, weights[n]) for n in names],
                                [view(n, grads[n]) for n in names], [view(n, mom_m[n]) for n in names],
                                [view(n, mom_v[n]) for n in names])
        deltas.update((n, view(n, a)) for n, a in zip(names, d))
        new_m.update((n, view(n, a)) for n, a in zip(names, nm))
        new_v.update((n, view(n, a)) for n, a in zip(names, nv))

    g_sum_e = small_sum("a", gs_e, packed_e, last)
    grads.update(zip(early, _unpack(g_sum_e, [weights[n].shape for n in early])))
    wide = [n for n in early if n.startswith(("ssm_b", "ssm_c"))]
    small_adamw("wide", wide)
    g_sum_l = small_sum("b", gs_l, packed_l, g_sum_e)
    grads["norm_mix"], g_meta, loss_sum = _unpack(g_sum_l, [weights["norm_mix"].shape, (N_META, D_MODEL), (1, 1)])
    small_adamw("rest", [n for n in SMALL if n not in wide])
    grads["meta_tokens"] = lax.dynamic_slice(g_meta, (0, me * BLK), (N_META, BLK))
    adamw_2d("meta_tokens")

    loss = loss_sum[0, 0]
    return (loss, grad_x2d[None], *[grads[n] for n in WEIGHTS], *[deltas[n] for n in WEIGHTS],
            *[new_m[n] for n in WEIGHTS], *[new_v[n] for n in WEIGHTS])
```

```python
import math

import jax
import jax.numpy as jnp
from jax import lax
from jax.experimental import pallas as pl
from jax.experimental.pallas import tpu as pltpu

F32 = jnp.float32
BF16 = jnp.bfloat16

D_MODEL = 1024
N_META = 16
HEAD_DIM = 64
N_Q_HEADS = 16
N_KV_HEADS = 4
Q_W = N_Q_HEADS * HEAD_DIM
KV_W = N_KV_HEADS * HEAD_DIM
SSM_GROUPS = 64
SSM_GROUP_CH = 16
SSM_STATE = 64
N_STATE = SSM_GROUPS * SSM_STATE
D_FF = 2816
IN_COLS = Q_W + 2 * KV_W + 3 * D_MODEL
EPS = 1e-6
BLK = 128
PAD = BLK - N_META
N_DEV = 8
NEG = -1e30
SSM_KB = 8
ST_KB = N_STATE // SSM_KB
LB_KB = ST_KB // BLK
N_LB = N_STATE // BLK

ADAM_LR = 0.001
ADAM_B1 = 0.9
ADAM_B2 = 0.999
ADAM_EPS = 1e-08
ADAM_WD = 0.01
ADAM_STEP = 10

VMEM_LIMIT = 48 * 1024 * 1024
MESH = pl.DeviceIdType.MESH

SMALL = ["norm_mix", "q_norm", "k_norm", "attn_sinks", "lam_re", "lam_im", "log_dt", "ssm_b_re", "ssm_b_im",
         "ssm_c_re", "ssm_c_im", "ssm_d", "attn_branch_norm", "ssm_branch_norm", "norm_ffn"]
WEIGHTS = ["meta_tokens", "norm_mix", "w_in", "q_norm", "k_norm", "attn_sinks", "lam_re", "lam_im", "log_dt",
           "ssm_b_re", "ssm_b_im", "ssm_c_re", "ssm_c_im", "ssm_d", "w_glu", "attn_branch_norm",
           "ssm_branch_norm", "w_out", "norm_ffn", "w_ffn_in", "w_ffn_out"]


def _params(**kw):
    return pltpu.CompilerParams(vmem_limit_bytes=VMEM_LIMIT, **kw)


def _pick(n, cap, mult=16):
    best = None
    for d in range(mult, min(n, cap) + 1, mult):
        if n % d == 0:
            best = d
    assert best is not None, (n, cap, mult)
    return best


def _my_index():
    return 4 * lax.axis_index("x") + 2 * lax.axis_index("y") + lax.axis_index("c")


def _rms(x, g):
    r = lax.rsqrt(jnp.mean(x * x, axis=-1, keepdims=True) + EPS)
    return x * r * g


def _rms_bwd(x, g, dy):
    r = lax.rsqrt(jnp.mean(x * x, axis=-1, keepdims=True) + EPS)
    t = dy * g
    dx = r * t - x * (r * r * r) * jnp.mean(t * x, axis=-1, keepdims=True)
    dg = jnp.sum(dy * (x * r), axis=0, keepdims=True)
    return dx, dg


def _sigmoid(x):
    return jax.nn.sigmoid(x)


def _gelu(x):
    k = math.sqrt(2.0 / math.pi)
    return 0.5 * x * (1.0 + jnp.tanh(k * (x + 0.044715 * (x * x * x))))


def _gelu_grad(x):
    k = math.sqrt(2.0 / math.pi)
    t = jnp.tanh(k * (x + 0.044715 * (x * x * x)))
    return 0.5 * (1.0 + t) + 0.5 * x * (1.0 - t * t) * (k * (1.0 + 3.0 * 0.044715 * (x * x)))


def _head_mean(x, e_ref):
    hi = x.astype(BF16)
    lo = (x - hi.astype(F32)).astype(BF16)
    e = e_ref[...]
    out = []
    for b in range(x.shape[1] // 256):
        sl = slice(256 * b, 256 * b + 256)
        s = (jnp.dot(hi[:, sl], e, preferred_element_type=F32)
             + jnp.dot(lo[:, sl], e, preferred_element_type=F32))
        out.append(s)
    s = out[0] if len(out) == 1 else jnp.concatenate(out, axis=1)
    return s * (1.0 / HEAD_DIM)


def _head_rms(x, g, e_ref):
    r = lax.rsqrt(_head_mean(x * x, e_ref) + EPS)
    return x * r * g


def _head_rms_bwd(x, g, dy, e_ref):
    r = lax.rsqrt(_head_mean(x * x, e_ref) + EPS)
    t = dy * g
    dx = r * t - x * (r * r * r) * _head_mean(t * x, e_ref)
    dg = jnp.sum(dy * (x * r), axis=0, keepdims=True)
    return dx, dg


def _lane_half(shape):
    lane = lax.broadcasted_iota(jnp.int32, shape, len(shape) - 1)
    return (lane >> 6) & 1


def _matmul(name, a, w, *, nt, tm, tn, tk, n=None, w_off=0, res=None, norm_g=None, out_dtype=F32):
    m_dim, k_dim = a.shape
    n_dim = n if n is not None else (w.shape[0] if nt else w.shape[1])
    gm, gn, gk = m_dim // tm, n_dim // tn, k_dim // tk
    assert gm * tm == m_dim and gn * tn == n_dim and gk * tk == k_dim, (name, a.shape, w.shape, tm, tn, tk)
    assert norm_g is None or tn == n_dim
    direct = out_dtype == F32
    dn = (((1,), (1,)), ((), ())) if nt else (((1,), (0,)), ((), ()))

    def body(*refs):
        refs = list(refs)
        a_ref, w_ref = refs[0], refs[1]
        pos = 2
        r_ref = g_ref = on_ref = None
        if res is not None:
            r_ref, pos = refs[pos], pos + 1
        if norm_g is not None:
            g_ref, pos = refs[pos], pos + 1
        o_ref, pos = refs[pos], pos + 1
        if norm_g is not None:
            on_ref, pos = refs[pos], pos + 1
        part = lax.dot_general(a_ref[...], w_ref[...], dn, preferred_element_type=F32)
        if gk == 1:
            r = part if r_ref is None else r_ref[...] + part
            o_ref[...] = r.astype(out_dtype)
            if on_ref is not None:
                on_ref[...] = _rms(r, g_ref[...]).astype(BF16)
            return
        acc = o_ref if direct else refs[pos]
        k = pl.program_id(2)

        @pl.when(k == 0)
        def _():
            acc[...] = part if r_ref is None or not direct else r_ref[...] + part

        @pl.when(k > 0)
        def _():
            acc[...] += part

        @pl.when(k == gk - 1)
        def _():
            if not direct:
                r = acc[...]
                if r_ref is not None:
                    r = r_ref[...] + r
                o_ref[...] = r.astype(out_dtype)
            if on_ref is not None:
                on_ref[...] = _rms(o_ref[...].astype(F32), g_ref[...]).astype(BF16)

    if nt:
        w_spec = pl.BlockSpec((tn, tk), lambda i, j, k: (j + w_off, k))
    else:
        w_spec = pl.BlockSpec((tk, tn), lambda i, j, k: (k, j))
    in_specs = [pl.BlockSpec((tm, tk), lambda i, j, k: (i, k)), w_spec]
    args = [a, w]
    out_spec = pl.BlockSpec((tm, tn), lambda i, j, k: (i, j))
    out_specs, out_shape = [out_spec], [jax.ShapeDtypeStruct((m_dim, n_dim), out_dtype)]
    if res is not None:
        in_specs.append(out_spec)
        args.append(res)
    if norm_g is not None:
        in_specs.append(pl.BlockSpec((1, tn), lambda i, j, k: (0, 0)))
        args.append(norm_g)
        out_specs.append(out_spec)
        out_shape.append(jax.ShapeDtypeStruct((m_dim, n_dim), BF16))
    out = pl.pallas_call(
        body, name=name, grid=(gm, gn, gk),
        in_specs=in_specs, out_specs=out_specs, out_shape=out_shape,
        scratch_shapes=[] if direct or gk == 1 else [pltpu.VMEM((tm, tn), F32)],
        compiler_params=_params(dimension_semantics=("parallel", "parallel", "arbitrary")),
    )(*args)
    return out if norm_g is not None else out[0]


def _matmul_tn(name, a, b, *, tm, tn, tl, after=None, out_rows=None):
    l_dim, m_dim = a.shape
    n_dim = b.shape[1]
    gm, gn, gl = m_dim // tm, n_dim // tn, l_dim // tl
    assert gm * tm == m_dim and gn * tn == n_dim and gl * tl == l_dim, (name, a.shape, b.shape, tm, tn, tl)

    def body(*refs):
        a_ref, b_ref = refs[0], refs[1]
        o_ref, ob_ref = refs[-2], refs[-1]

        @pl.when(pl.program_id(2) == 0)
        def _():
            o_ref[...] = jnp.zeros_like(o_ref)

        o_ref[...] += lax.dot_general(a_ref[...], b_ref[...], (((0,), (0,)), ((), ())),
                                      preferred_element_type=F32)

        @pl.when(pl.program_id(2) == gl - 1)
        def _():
            ob_ref[...] = o_ref[...].astype(BF16)

    out_row = out_rows if out_rows is not None else (lambda i: i)
    out_spec = pl.BlockSpec((tm, tn), lambda i, j, l: (out_row(i), j))
    in_specs = [pl.BlockSpec((tl, tm), lambda i, j, l: (l, i)), pl.BlockSpec((tl, tn), lambda i, j, l: (l, j))]
    args = [a, b]
    if after is not None:
        in_specs.append(pl.BlockSpec(memory_space=pl.ANY))
        args.append(after)
    return pl.pallas_call(
        body, name=name, grid=(gm, gn, gl),
        in_specs=in_specs,
        out_specs=[out_spec, out_spec],
        out_shape=[jax.ShapeDtypeStruct((m_dim, n_dim), F32), jax.ShapeDtypeStruct((m_dim, n_dim), BF16)],
        compiler_params=_params(dimension_semantics=("parallel", "parallel", "arbitrary")),
    )(*args)


def _row_spec(tm, cols, f=None):
    if f is None:
        return pl.BlockSpec((tm, cols), lambda i: (i, 0))
    return pl.BlockSpec((tm, cols), lambda i: (f(i), 0))


def _full_spec(shape):
    nd = len(shape)
    return pl.BlockSpec(shape, lambda i: (0,) * nd)


def _shifted_specs(n_sub, n_blocks):
    return [_row_spec(BLK, D_MODEL, (lambda i, k=k: jnp.clip(n_sub * i - 1 + k, 0, n_blocks - 1)))
            for k in range(n_sub)]


def _embed_norm(x2d, g, tm):
    s_len = x2d.shape[0]
    l_dim = s_len + BLK
    n_sub = tm // BLK

    def body(*refs):
        x_refs, g_ref, h_ref, xn_ref = refs[:n_sub], refs[n_sub], refs[n_sub + 1], refs[n_sub + 2]
        i = pl.program_id(0)
        for k in range(n_sub):
            rows = slice(BLK * k, BLK * k + BLK)
            h = x_refs[k][...] * jnp.where(n_sub * i + k >= 1, 1.0, 0.0)
            h_ref[rows, :] = h
            xn_ref[rows, :] = _rms(h, g_ref[...]).astype(BF16)

    return pl.pallas_call(
        body, name="embed_norm", grid=(l_dim // tm,),
        in_specs=_shifted_specs(n_sub, s_len // BLK) + [_full_spec((1, D_MODEL))],
        out_specs=[_row_spec(tm, D_MODEL), _row_spec(tm, D_MODEL)],
        out_shape=[jax.ShapeDtypeStruct((l_dim, D_MODEL), F32),
                   jax.ShapeDtypeStruct((l_dim, D_MODEL), BF16)],
        compiler_params=_params(),
    )(*([x2d] * n_sub), g)


def _embed_meta(meta_pad, g, h0, xn):
    def body(mp_ref, g_ref, h_in, xn_in, h_ref, xn_ref):
        h_ref[...] = mp_ref[...]
        xn_ref[...] = _rms(mp_ref[...], g_ref[...]).astype(BF16)

    any_spec = pl.BlockSpec(memory_space=pl.ANY)
    return pl.pallas_call(
        body, name="embed_meta", grid=(1,),
        in_specs=[_full_spec((BLK, D_MODEL)), _full_spec((1, D_MODEL)), any_spec, any_spec],
        out_specs=[_row_spec(BLK, D_MODEL), _row_spec(BLK, D_MODEL)],
        out_shape=[jax.ShapeDtypeStruct(h0.shape, F32), jax.ShapeDtypeStruct(xn.shape, BF16)],
        input_output_aliases={2: 0, 3: 1},
        compiler_params=_params(),
    )(meta_pad, g, h0, xn)


KVX_W = 2 * N_KV_HEADS * BLK


def _qk_prep(qkv, q_norm_t, k_norm_t, e_mat, tm):
    l_dim = qkv.shape[0]

    def body(x_ref, qg_ref, kg_ref, e_ref, q_ref, kf_ref, vf_ref):
        x = x_ref[...]
        q = _head_rms(x[:, :Q_W], qg_ref[...], e_ref) * (HEAD_DIM ** -0.5)
        q_ref[...] = q.astype(BF16)
        k = _head_rms(x[:, Q_W:Q_W + KV_W], kg_ref[...], e_ref)
        v = x[:, Q_W + KV_W:Q_W + 2 * KV_W]
        half = _lane_half((tm, BLK))
        for src, dst in ((k, kf_ref), (v, vf_ref)):
            for kv in range(N_KV_HEADS):
                blk = src[:, BLK * (kv // 2):BLK * (kv // 2) + BLK]
                swapped = pltpu.roll(blk, HEAD_DIM, axis=1)
                for e in range(2):
                    val = blk if kv % 2 == e else swapped
                    idx = 2 * kv + e
                    dst[:, BLK * idx:BLK * idx + BLK] = jnp.where(half == e, val, 0.0).astype(BF16)

    return pl.pallas_call(
        body, name="qk_prep", grid=(l_dim // tm,),
        in_specs=[_row_spec(tm, Q_W + 2 * KV_W), _full_spec((1, Q_W)), _full_spec((1, KV_W)),
                  _full_spec((256, 256))],
        out_specs=[_row_spec(tm, Q_W), _row_spec(tm, KVX_W), _row_spec(tm, KVX_W)],
        out_shape=[jax.ShapeDtypeStruct((l_dim, Q_W), BF16),
                   jax.ShapeDtypeStruct((l_dim, KVX_W), BF16),
                   jax.ShapeDtypeStruct((l_dim, KVX_W), BF16)],
        compiler_params=_params(),
    )(qkv, q_norm_t, k_norm_t, e_mat)


def _merge_fwd(attn, zab, gates, abn, sbn, tm):
    l_dim = attn.shape[0]

    def body(a_ref, z_ref, g_ref, an_ref, sn_ref, o_ref):
        z = z_ref[...].astype(F32)
        g = g_ref[...].astype(F32)
        ssm = z[:, :D_MODEL] * _sigmoid(z[:, D_MODEL:])
        merged = (_sigmoid(g[:, :D_MODEL]) * _rms(a_ref[...], an_ref[...])
                  + _sigmoid(g[:, D_MODEL:]) * _rms(ssm, sn_ref[...]))
        o_ref[...] = merged.astype(BF16)

    return pl.pallas_call(
        body, name="merge_fwd", grid=(l_dim // tm,),
        in_specs=[_row_spec(tm, D_MODEL), _row_spec(tm, 2 * D_MODEL), _row_spec(tm, 2 * D_MODEL),
                  _full_spec((1, D_MODEL)), _full_spec((1, D_MODEL))],
        out_specs=_row_spec(tm, D_MODEL),
        out_shape=jax.ShapeDtypeStruct((l_dim, D_MODEL), BF16),
        compiler_params=_params(),
    )(attn, zab, gates, abn, sbn)


def _merge_bwd(attn, zab, gates, abn, sbn, dmerged, tm):
    l_dim = attn.shape[0]

    def body(a_ref, z_ref, g_ref, an_ref, sn_ref, dm_ref, da_ref, dz_ref, dg_ref, dan_ref, dsn_ref):
        @pl.when(pl.program_id(0) == 0)
        def _():
            dan_ref[...] = jnp.zeros_like(dan_ref)
            dsn_ref[...] = jnp.zeros_like(dsn_ref)

        z = z_ref[...].astype(F32)
        g = g_ref[...].astype(F32)
        dm = dm_ref[...].astype(F32)
        attn_v = a_ref[...]
        za, zb = z[:, :D_MODEL], z[:, D_MODEL:]
        sb = _sigmoid(zb)
        ssm = za * sb
        s_ga, s_gs = _sigmoid(g[:, :D_MODEL]), _sigmoid(g[:, D_MODEL:])
        a_n = _rms(attn_v, an_ref[...])
        s_n = _rms(ssm, sn_ref[...])
        dg_ref[:, :D_MODEL] = (dm * a_n * s_ga * (1.0 - s_ga)).astype(BF16)
        dg_ref[:, D_MODEL:] = (dm * s_n * s_gs * (1.0 - s_gs)).astype(BF16)
        dattn, dan = _rms_bwd(attn_v, an_ref[...], dm * s_ga)
        dssm, dsn = _rms_bwd(ssm, sn_ref[...], dm * s_gs)
        da_ref[...] = dattn.astype(BF16)
        dz_ref[:, :D_MODEL] = (dssm * sb).astype(BF16)
        dz_ref[:, D_MODEL:] = (dssm * za * sb * (1.0 - sb)).astype(BF16)
        dan_ref[...] += dan
        dsn_ref[...] += dsn

    return pl.pallas_call(
        body, name="merge_bwd", grid=(l_dim // tm,),
        in_specs=[_row_spec(tm, D_MODEL), _row_spec(tm, 2 * D_MODEL), _row_spec(tm, 2 * D_MODEL),
                  _full_spec((1, D_MODEL)), _full_spec((1, D_MODEL)), _row_spec(tm, D_MODEL)],
        out_specs=[_row_spec(tm, D_MODEL), _row_spec(tm, 2 * D_MODEL), _row_spec(tm, 2 * D_MODEL),
                   _full_spec((1, D_MODEL)), _full_spec((1, D_MODEL))],
        out_shape=[jax.ShapeDtypeStruct((l_dim, D_MODEL), BF16),
                   jax.ShapeDtypeStruct((l_dim, 2 * D_MODEL), BF16),
                   jax.ShapeDtypeStruct((l_dim, IN_COLS), BF16),
                   jax.ShapeDtypeStruct((1, D_MODEL), F32), jax.ShapeDtypeStruct((1, D_MODEL), F32)],
        compiler_params=_params(),
    )(attn, zab, gates, abn, sbn, dmerged)


FF_TILE = D_FF // 2


def _ffn_in_swiglu(hn, wt_ffn_in, tm):
    l_dim = hn.shape[0]
    nt = (((1,), (1,)), ((), ()))

    def body(a_ref, w_ref, gu_ref, act_ref):
        r = lax.dot_general(a_ref[...], w_ref[...], nt, preferred_element_type=F32)
        gate, up = r[:, :FF_TILE], r[:, FF_TILE:]
        gu_ref[...] = r.astype(BF16)
        act_ref[...] = (gate * _sigmoid(gate) * up).astype(BF16)

    return pl.pallas_call(
        body, name="ffn_in_swiglu", grid=(l_dim // tm, 2),
        in_specs=[pl.BlockSpec((tm, D_MODEL), lambda i, j: (i, 0)),
                  pl.BlockSpec((2 * FF_TILE, D_MODEL), lambda i, j: (j, 0))],
        out_specs=[pl.BlockSpec((tm, 2 * FF_TILE), lambda i, j: (i, j)),
                   pl.BlockSpec((tm, FF_TILE), lambda i, j: (i, j))],
        out_shape=[jax.ShapeDtypeStruct((l_dim, 2 * D_FF), BF16), jax.ShapeDtypeStruct((l_dim, D_FF), BF16)],
        compiler_params=_params(dimension_semantics=("parallel", "parallel")),
    )(hn, wt_ffn_in)


def _d_act_swiglu(dh2_b, w_ffn_out, gu, tm):
    l_dim = dh2_b.shape[0]
    nt = (((1,), (1,)), ((), ()))

    def body(d_ref, w_ref, gu_ref, o_ref):
        d = lax.dot_general(d_ref[...], w_ref[...], nt, preferred_element_type=F32)
        gate = gu_ref[:, :FF_TILE].astype(F32)
        up = gu_ref[:, FF_TILE:].astype(F32)
        s = _sigmoid(gate)
        o_ref[:, :FF_TILE] = (d * up * (s * (1.0 + gate * (1.0 - s)))).astype(BF16)
        o_ref[:, FF_TILE:] = (d * (gate * s)).astype(BF16)

    return pl.pallas_call(
        body, name="d_act_swiglu", grid=(l_dim // tm, 2),
        in_specs=[pl.BlockSpec((tm, D_MODEL), lambda i, j: (i, 0)),
                  pl.BlockSpec((FF_TILE, D_MODEL), lambda i, j: (j, 0)),
                  pl.BlockSpec((tm, 2 * FF_TILE), lambda i, j: (i, j))],
        out_specs=pl.BlockSpec((tm, 2 * FF_TILE), lambda i, j: (i, j)),
        out_shape=jax.ShapeDtypeStruct((l_dim, 2 * D_FF), BF16),
        compiler_params=_params(dimension_semantics=("parallel", "parallel")),
    )(dh2_b, w_ffn_out, gu)


def _loss_grad(h2, target2d, tm):
    l_dim = h2.shape[0]
    n_sub = tm // BLK

    def body(*refs):
        h_ref, t_refs = refs[0], refs[1:1 + n_sub]
        d_ref, db_ref, loss_ref = refs[1 + n_sub:]
        i = pl.program_id(0)

        @pl.when(i == 0)
        def _():
            loss_ref[...] = jnp.zeros_like(loss_ref)

        for k in range(n_sub):
            rows = slice(BLK * k, BLK * k + BLK)
            real = jnp.where(n_sub * i + k >= 1, 1.0, 0.0)
            err = (h_ref[rows, :] - t_refs[k][...]) * real
            d = err * (1.0 / D_MODEL)
            d_ref[rows, :] = d
            db_ref[rows, :] = d.astype(BF16)
            loss_ref[...] += 0.5 * jnp.sum(jnp.mean(err * err, axis=-1, keepdims=True), axis=0, keepdims=True)

    return pl.pallas_call(
        body, name="loss_grad", grid=(l_dim // tm,),
        in_specs=[_row_spec(tm, D_MODEL)] + _shifted_specs(n_sub, target2d.shape[0] // BLK),
        out_specs=[_row_spec(tm, D_MODEL), _row_spec(tm, D_MODEL), _full_spec((1, 1))],
        out_shape=[jax.ShapeDtypeStruct((l_dim, D_MODEL), F32), jax.ShapeDtypeStruct((l_dim, D_MODEL), BF16),
                   jax.ShapeDtypeStruct((1, 1), F32)],
        compiler_params=_params(),
    )(h2, *([target2d] * n_sub))


def _norm_bwd_res(name, h, g, dy, dres, tm):
    l_dim = h.shape[0]

    def body(h_ref, g_ref, dy_ref, dr_ref, o_ref, ob_ref, dg_ref):
        @pl.when(pl.program_id(0) == 0)
        def _():
            dg_ref[...] = jnp.zeros_like(dg_ref)

        dx, dg = _rms_bwd(h_ref[...], g_ref[...], dy_ref[...].astype(F32))
        out = dr_ref[...] + dx
        o_ref[...] = out
        ob_ref[...] = out.astype(BF16)
        dg_ref[...] += dg

    return pl.pallas_call(
        body, name=name, grid=(l_dim // tm,),
        in_specs=[_row_spec(tm, D_MODEL), _full_spec((1, D_MODEL)), _row_spec(tm, D_MODEL), _row_spec(tm, D_MODEL)],
        out_specs=[_row_spec(tm, D_MODEL), _row_spec(tm, D_MODEL), _full_spec((1, D_MODEL))],
        out_shape=[jax.ShapeDtypeStruct((l_dim, D_MODEL), F32), jax.ShapeDtypeStruct((l_dim, D_MODEL), BF16),
                   jax.ShapeDtypeStruct((1, D_MODEL), F32)],
        compiler_params=_params(),
    )(h, g, dy, dres)


def _final_bwd(h0, g, dxn, dh1, tm):
    l_dim = h0.shape[0]
    n_sub = tm // BLK
    n_tiles = (l_dim - BLK) // tm

    def sub_specs():
        return [_row_spec(BLK, D_MODEL, (lambda j, k=k: jnp.where(j < n_tiles, n_sub * j + 1 + k, 0)))
                for k in range(n_sub)]

    def body(*refs):
        h_refs, g_ref = refs[:n_sub], refs[n_sub]
        dy_refs, dr_refs = refs[n_sub + 1:2 * n_sub + 1], refs[2 * n_sub + 1:3 * n_sub + 1]
        gx_ref, gm_ref, dg_ref = refs[3 * n_sub + 1:]
        j = pl.program_id(0)

        @pl.when(j == 0)
        def _():
            dg_ref[...] = jnp.zeros_like(dg_ref)

        def block(k):
            dx, dg = _rms_bwd(h_refs[k][...], g_ref[...], dy_refs[k][...].astype(F32))
            dg_ref[...] += dg
            return dr_refs[k][...] + dx

        @pl.when(j < n_tiles)
        def _():
            for k in range(n_sub):
                gx_ref[BLK * k:BLK * k + BLK, :] = block(k)

        @pl.when(j == n_tiles)
        def _():
            gm_ref[...] = block(0)

    return pl.pallas_call(
        body, name="final_bwd", grid=(n_tiles + 1,),
        in_specs=sub_specs() + [_full_spec((1, D_MODEL))] + sub_specs() + sub_specs(),
        out_specs=[_row_spec(tm, D_MODEL, lambda j: jnp.minimum(j, n_tiles - 1)), _full_spec((BLK, D_MODEL)),
                   _full_spec((1, D_MODEL))],
        out_shape=[jax.ShapeDtypeStruct((l_dim - BLK, D_MODEL), F32), jax.ShapeDtypeStruct((BLK, D_MODEL), F32),
                   jax.ShapeDtypeStruct((1, D_MODEL), F32)],
        compiler_params=_params(),
    )(*([h0] * n_sub), g, *([dxn] * n_sub), *([dh1] * n_sub))


def _attn_valid(n):
    shape = (2 * BLK, 3 * BLK)
    qi = lax.broadcasted_iota(jnp.int32, shape, 0) & (BLK - 1)
    col = lax.broadcasted_iota(jnp.int32, shape, 1)
    kj = col & (BLK - 1)
    part = col >> 7
    nn = jnp.zeros(shape, jnp.int32) + n
    meta_ok = (part == 0) & (kj >= PAD) & (nn >= 1)
    prev_ok = (part == 1) & (kj > qi) & (nn >= 2)
    cur_ok = (part == 2) & (kj <= qi) & ((nn >= 1) | (kj >= PAD))
    return meta_ok | prev_ok | cur_ok


def _attn_scores(q_ref, kwin, sk_ref, valid, kv, e):
    qs = jnp.concatenate([q_ref[:, BLK * (2 * kv):BLK * (2 * kv) + BLK],
                          q_ref[:, BLK * (2 * kv + 1):BLK * (2 * kv + 1) + BLK]], axis=0)
    s = lax.dot_general(qs, kwin, (((1,), (1,)), ((), ())), preferred_element_type=F32)
    h0 = 4 * kv + e
    row = lax.broadcasted_iota(jnp.int32, (2 * BLK, 1), 0)
    sink = jnp.where(row < BLK, sk_ref[:, h0:h0 + 1], sk_ref[:, h0 + 2:h0 + 3])
    return qs, jnp.where(valid, s, NEG), sink


def _attn_specs(nb):
    prev = lambda i: jnp.maximum(i - 1, 0)
    zero = lambda i: 0
    kv_specs = [_row_spec(BLK, KVX_W, zero), _row_spec(BLK, KVX_W, prev), _row_spec(BLK, KVX_W)]
    return kv_specs


def _attn_fwd(qn, kf, vf, sinks):
    l_dim = qn.shape[0]
    nb = l_dim // BLK

    def body(q_ref, km_ref, kp_ref, kc_ref, vm_ref, vp_ref, vc_ref, sk_ref, o_ref, lse_ref):
        valid = _attn_valid(pl.program_id(0))
        lane = lax.broadcasted_iota(jnp.int32, (BLK, BLK), 1)
        lse_all = jnp.zeros((BLK, BLK), F32)
        for kv in range(N_KV_HEADS):
            outs = []
            for e in range(2):
                sl = slice(BLK * (2 * kv + e), BLK * (2 * kv + e) + BLK)
                kwin = jnp.concatenate([km_ref[:, sl], kp_ref[:, sl], kc_ref[:, sl]], axis=0)
                vwin = jnp.concatenate([vm_ref[:, sl], vp_ref[:, sl], vc_ref[:, sl]], axis=0)
                _, s, sink = _attn_scores(q_ref, kwin, sk_ref, valid, kv, e)
                m = jnp.maximum(jnp.max(s, axis=-1, keepdims=True), sink)
                ex = jnp.exp(s - m)
                den = jnp.sum(ex, axis=-1, keepdims=True) + jnp.exp(sink - m)
                p = ex * (1.0 / den)
                outs.append(jnp.dot(p.astype(BF16), vwin, preferred_element_type=F32))
                lse = m + jnp.log(den)
                lse_all = jnp.where(lane == 4 * kv + e, lse[:BLK], lse_all)
                lse_all = jnp.where(lane == 4 * kv + 2 + e, lse[BLK:], lse_all)
            o = outs[0] + outs[1]
            o_ref[:, BLK * (2 * kv):BLK * (2 * kv) + BLK] = o[:BLK]
            o_ref[:, BLK * (2 * kv + 1):BLK * (2 * kv + 1) + BLK] = o[BLK:]
        lse_ref[...] = lse_all

    kv_specs = _attn_specs(nb)
    return pl.pallas_call(
        body, name="attn_fwd", grid=(nb,),
        in_specs=[_row_spec(BLK, Q_W)] + kv_specs + kv_specs + [_full_spec((1, N_Q_HEADS))],
        out_specs=[_row_spec(BLK, Q_W), _row_spec(BLK, BLK)],
        out_shape=[jax.ShapeDtypeStruct((l_dim, Q_W), F32), jax.ShapeDtypeStruct((l_dim, BLK), F32)],
        compiler_params=_params(),
    )(qn, kf, kf, kf, vf, vf, vf, sinks)


def _attn_bwd(qn, kf, vf, sinks, lse, attn, dattn):
    l_dim = qn.shape[0]
    nb = l_dim // BLK
    wide = KV_W
    tn = (((0,), (0,)), ((), ()))
    nt = (((1,), (1,)), ((), ()))

    def body(q_ref, km_ref, kp_ref, kc_ref, vm_ref, vp_ref, vc_ref, sk_ref, lse_ref, o_ref, do_ref,
             dq_ref, dkc_ref, dkp_ref, dkm_ref, dvc_ref, dvp_ref, dvm_ref, dsk_ref):
        @pl.when(pl.program_id(0) == 0)
        def _():
            dkm_ref[...] = jnp.zeros_like(dkm_ref)
            dvm_ref[...] = jnp.zeros_like(dvm_ref)
            dsk_ref[...] = jnp.zeros_like(dsk_ref)

        valid = _attn_valid(pl.program_id(0))
        half = _lane_half((BLK, BLK))
        half2 = _lane_half((2 * BLK, BLK))
        half3 = _lane_half((3 * BLK, BLK))
        lane16 = lax.broadcasted_iota(jnp.int32, (1, N_Q_HEADS), 1)
        dsk = jnp.zeros((1, N_Q_HEADS), F32)
        folded_k, folded_v = [], []
        for kv in range(N_KV_HEADS):
            j0, j1 = 2 * kv, 2 * kv + 1
            do0 = do_ref[:, BLK * j0:BLK * j0 + BLK]
            do1 = do_ref[:, BLK * j1:BLK * j1 + BLK]
            do0f, do1f = do0.astype(F32), do1.astype(F32)
            prod0 = do0f * o_ref[:, BLK * j0:BLK * j0 + BLK]
            prod1 = do1f * o_ref[:, BLK * j1:BLK * j1 + BLK]
            dos = jnp.concatenate([do0, do1], axis=0)
            dqs, dks, dvs = [], [], []
            for e in range(2):
                sl = slice(BLK * (2 * kv + e), BLK * (2 * kv + e) + BLK)
                kwin = jnp.concatenate([km_ref[:, sl], kp_ref[:, sl], kc_ref[:, sl]], axis=0)
                vwin = jnp.concatenate([vm_ref[:, sl], vp_ref[:, sl], vc_ref[:, sl]], axis=0)
                qs, s, sink = _attn_scores(q_ref, kwin, sk_ref, valid, kv, e)
                h0 = 4 * kv + e
                lse_rows = jnp.concatenate([lse_ref[:, h0:h0 + 1], lse_ref[:, h0 + 2:h0 + 3]], axis=0)
                p = jnp.exp(s - lse_rows)
                p_sink = jnp.exp(sink - lse_rows)
                delta = jnp.concatenate(
                    [jnp.sum(jnp.where(half == e, prod0, 0.0), axis=-1, keepdims=True),
                     jnp.sum(jnp.where(half == e, prod1, 0.0), axis=-1, keepdims=True)], axis=0)
                dp = lax.dot_general(dos, vwin, nt, preferred_element_type=F32)
                ds = (p * (dp - delta)).astype(BF16)
                pb = p.astype(BF16)
                dqs.append(jnp.dot(ds, kwin, preferred_element_type=F32))
                dks.append(lax.dot_general(ds, qs, tn, preferred_element_type=F32))
                dvs.append(lax.dot_general(pb, dos, tn, preferred_element_type=F32))
                sink_g = -(p_sink * delta)
                g_lo = jnp.sum(sink_g[:BLK], axis=0, keepdims=True)
                g_hi = jnp.sum(sink_g[BLK:], axis=0, keepdims=True)
                dsk = dsk + jnp.where(lane16 == h0, g_lo, 0.0) + jnp.where(lane16 == h0 + 2, g_hi, 0.0)
            dq = jnp.where(half2 == 0, dqs[0], dqs[1])
            dq_ref[:, BLK * j0:BLK * j0 + BLK] = dq[:BLK].astype(BF16)
            dq_ref[:, BLK * j1:BLK * j1 + BLK] = dq[BLK:].astype(BF16)
            own = kv % 2
            folded_k.append(dks[own] + pltpu.roll(dks[1 - own], HEAD_DIM, axis=1))
            folded_v.append(dvs[own] + pltpu.roll(dvs[1 - own], HEAD_DIM, axis=1))
            if own == 1:
                cols = slice(BLK * (kv // 2), BLK * (kv // 2) + BLK)
                for folded, m_ref, p_ref, c_ref in ((folded_k, dkm_ref, dkp_ref, dkc_ref),
                                                    (folded_v, dvm_ref, dvp_ref, dvc_ref)):
                    both = jnp.where(half3 == 0, folded[0], folded[1])
                    m_ref[:, cols] += both[:BLK]
                    p_ref[:, cols] = both[BLK:2 * BLK].astype(BF16)
                    c_ref[:, cols] = both[2 * BLK:].astype(BF16)
                folded_k, folded_v = [], []
        dsk_ref[...] += dsk

    kv_specs = _attn_specs(nb)
    row_wide = _row_spec(BLK, wide)
    acc_wide = _full_spec((BLK, wide))
    big = jax.ShapeDtypeStruct((l_dim, wide), BF16)
    return pl.pallas_call(
        body, name="attn_bwd", grid=(nb,),
        in_specs=[_row_spec(BLK, Q_W)] + kv_specs + kv_specs
        + [_full_spec((1, N_Q_HEADS)), _row_spec(BLK, BLK), _row_spec(BLK, Q_W), _row_spec(BLK, Q_W)],
        out_specs=[_row_spec(BLK, Q_W), row_wide, row_wide, acc_wide, row_wide, row_wide, acc_wide,
                   _full_spec((1, N_Q_HEADS))],
        out_shape=[jax.ShapeDtypeStruct((l_dim, Q_W), BF16), big, big, jax.ShapeDtypeStruct((BLK, wide), F32),
                   big, big, jax.ShapeDtypeStruct((BLK, wide), F32), jax.ShapeDtypeStruct((1, N_Q_HEADS), F32)],
        compiler_params=_params(),
    )(qn, kf, kf, kf, vf, vf, vf, sinks, lse, attn, dattn)


def _qk_bwd(qkv, q_norm_t, k_norm_t, e_mat, dq, dkc, dkp, dkm, dvc, dvp, dvm, dproj):
    l_dim = qkv.shape[0]
    nb = l_dim // BLK
    wide = KV_W

    def body(x_ref, qg_ref, kg_ref, e_ref, dq_ref, dkc_ref, dkp_ref, dkm_ref, dvc_ref, dvp_ref, dvm_ref,
             dproj_ref, o_ref, dqg_ref, dkg_ref):
        i = pl.program_id(0)

        @pl.when(i == 0)
        def _():
            dqg_ref[...] = jnp.zeros_like(dqg_ref)
            dkg_ref[...] = jnp.zeros_like(dkg_ref)

        first = jnp.where(i == 0, 1.0, 0.0)
        not_last = jnp.where(i < nb - 1, 1.0, 0.0)
        dk_x = dkc_ref[...].astype(F32) + not_last * dkp_ref[...].astype(F32) + first * dkm_ref[...]
        dv_x = dvc_ref[...].astype(F32) + not_last * dvp_ref[...].astype(F32) + first * dvm_ref[...]
        x = x_ref[...]
        dqx, dqg = _head_rms_bwd(x[:, :Q_W], qg_ref[...], dq_ref[...].astype(F32) * (HEAD_DIM ** -0.5), e_ref)
        dkx, dkg = _head_rms_bwd(x[:, Q_W:Q_W + KV_W], kg_ref[...], dk_x, e_ref)
        o_ref[:, :Q_W] = dqx.astype(BF16)
        o_ref[:, Q_W:Q_W + KV_W] = dkx.astype(BF16)
        o_ref[:, Q_W + KV_W:] = dv_x.astype(BF16)
        dqg_ref[...] += dqg
        dkg_ref[...] += dkg

    nxt = lambda i: jnp.minimum(i + 1, nb - 1)
    row_wide = _row_spec(BLK, wide)
    nxt_wide = _row_spec(BLK, wide, nxt)
    acc_wide = _full_spec((BLK, wide))
    qkv_w = Q_W + 2 * KV_W
    in_specs = [_row_spec(BLK, qkv_w), _full_spec((1, Q_W)), _full_spec((1, KV_W)),
                _full_spec((256, 256)), _row_spec(BLK, Q_W),
                row_wide, nxt_wide, acc_wide, row_wide, nxt_wide, acc_wide, pl.BlockSpec(memory_space=pl.ANY)]
    return pl.pallas_call(
        body, name="qk_bwd", grid=(nb,),
        in_specs=in_specs,
        out_specs=[pl.BlockSpec((BLK, qkv_w), lambda i: (i, 3 * D_MODEL // qkv_w)),
                   _full_spec((1, Q_W)), _full_spec((1, KV_W))],
        input_output_aliases={len(in_specs) - 1: 0},
        out_shape=[jax.ShapeDtypeStruct(dproj.shape, BF16),
                   jax.ShapeDtypeStruct((1, Q_W), F32), jax.ShapeDtypeStruct((1, KV_W), F32)],
        compiler_params=_params(),
    )(qkv, q_norm_t, k_norm_t, e_mat, dq, dkc, dkp, dkm, dvc, dvp, dvm, dproj)


GRP = 8


def _strided(r, g):
    return pl.ds(r, g, stride=GRP)


def _lane_slab(ref, i, r, g):
    return ref[i, _strided(r, g), :]


def _chunk_carries(xr_ref, xi_ref, i, ar, ai, sqr_ref, sqi_ref, seq_r, seq_i, cin_r, cin_i, sign, reverse):
    g = xr_ref.shape[1] // GRP
    sr = si = None
    for r in (range(GRP - 1, -1, -1) if reverse else range(GRP)):
        xr, xi = _lane_slab(xr_ref, i, r, g), _lane_slab(xi_ref, i, r, g)
        if sr is not None:
            xr, xi = xr + ar * sr - ai * si, xi + ar * si + ai * sr
        sr, si = xr, xi
    row = lax.broadcasted_iota(jnp.int32, sr.shape, 0)
    idx, s = 0, 1
    while s < g:
        br = sqr_ref[idx:idx + 1, BLK * i:BLK * i + BLK]
        bi = sign * sqi_ref[idx:idx + 1, BLK * i:BLK * i + BLK]
        shift, keep = (g - s, row < g - s) if reverse else (s, row >= s)
        pr = jnp.where(keep, pltpu.roll(sr, shift, axis=0), 0.0)
        pi = jnp.where(keep, pltpu.roll(si, shift, axis=0), 0.0)
        sr, si = sr + br * pr - bi * pi, si + br * pi + bi * pr
        idx, s = idx + 1, 2 * s
    return sr + seq_r * cin_r - seq_i * cin_i, si + seq_r * cin_i + seq_i * cin_r


def _ssm_fwd_kb(u, wb_re, wb_im, wc_re, wc_im, d_skip, tabs, q):
    l_dim = u.shape[0]
    nc = l_dim // q
    g = q // GRP

    def body(u_ref, wbr_ref, wbi_ref, wcr_ref, wci_ref, d_ref, a1r_ref, a1i_ref, sqr_ref, sqi_ref,
             seqr_ref, seqi_ref, y_ref, z_ref, sr_ref, si_ref, cr_ref, ci_ref, xr_ref, xi_ref):
        @pl.when(pl.program_id(1) == 0)
        def _():
            cr_ref[...] = jnp.zeros_like(cr_ref)
            ci_ref[...] = jnp.zeros_like(ci_ref)

        u_kb = u_ref[...]
        ub = u_kb.astype(BF16)
        xr = jnp.dot(ub, wbr_ref[0], preferred_element_type=F32)
        xi = jnp.dot(ub, wbi_ref[0], preferred_element_type=F32)
        for i in range(LB_KB):
            xr_ref[i] = xr[:, BLK * i:BLK * i + BLK]
            xi_ref[i] = xi[:, BLK * i:BLK * i + BLK]
        row = lax.broadcasted_iota(jnp.int32, (g, BLK), 0)
        for i in range(LB_KB):
            lanes = slice(BLK * i, BLK * i + BLK)
            ar, ai = a1r_ref[0:1, lanes], a1i_ref[0:1, lanes]
            cin_r, cin_i = cr_ref[0:1, lanes], ci_ref[0:1, lanes]
            tr, ti = _chunk_carries(xr_ref, xi_ref, i, ar, ai, sqr_ref, sqi_ref, seqr_ref[:, lanes],
                                    seqi_ref[:, lanes], cin_r, cin_i, 1.0, reverse=False)
            cr_ref[0:1, lanes] = jnp.sum(jnp.where(row == g - 1, tr, 0.0), axis=0, keepdims=True)
            ci_ref[0:1, lanes] = jnp.sum(jnp.where(row == g - 1, ti, 0.0), axis=0, keepdims=True)
            pr = jnp.where(row == 0, cin_r, pltpu.roll(tr, 1, axis=0))
            pi = jnp.where(row == 0, cin_i, pltpu.roll(ti, 1, axis=0))
            for r in range(GRP):
                pr, pi = (_lane_slab(xr_ref, i, r, g) + ar * pr - ai * pi,
                          _lane_slab(xi_ref, i, r, g) + ar * pi + ai * pr)
                sr_ref[i, _strided(r, g), :] = pr
                si_ref[i, _strided(r, g), :] = pi
        s_r = jnp.concatenate([sr_ref[i] for i in range(LB_KB)], axis=1)
        s_i = jnp.concatenate([si_ref[i] for i in range(LB_KB)], axis=1)
        y = (jnp.dot(s_r.astype(BF16), wcr_ref[0], preferred_element_type=F32)
             - jnp.dot(s_i.astype(BF16), wci_ref[0], preferred_element_type=F32)
             + d_ref[...] * u_kb)
        y_ref[...] = y.astype(BF16)
        z_ref[...] = _gelu(y).astype(BF16)

    chan = pl.BlockSpec((q, BLK), lambda k, c: (c, k))
    wb_spec = pl.BlockSpec((1, BLK, ST_KB), lambda k, c: (k, 0, 0))
    wc_spec = pl.BlockSpec((1, ST_KB, BLK), lambda k, c: (k, 0, 0))
    tab_specs = [pl.BlockSpec((t.shape[0], ST_KB), lambda k, c: (0, k)) for t in tabs[:6]]
    state_spec = pl.BlockSpec((LB_KB, q, BLK), lambda k, c: (k, c, 0))
    state_shape = jax.ShapeDtypeStruct((N_LB, l_dim, BLK), F32)
    return pl.pallas_call(
        body, name="ssm_fwd", grid=(SSM_KB, nc),
        in_specs=[chan, wb_spec, wb_spec, wc_spec, wc_spec, pl.BlockSpec((1, BLK), lambda k, c: (0, k))] + tab_specs,
        out_specs=[chan, chan, state_spec, state_spec],
        out_shape=[jax.ShapeDtypeStruct((l_dim, D_MODEL), BF16), jax.ShapeDtypeStruct((l_dim, D_MODEL), BF16),
                   state_shape, state_shape],
        scratch_shapes=[pltpu.VMEM((8, ST_KB), F32), pltpu.VMEM((8, ST_KB), F32),
                        pltpu.VMEM((LB_KB, q, BLK), F32), pltpu.VMEM((LB_KB, q, BLK), F32)],
        compiler_params=_params(dimension_semantics=("parallel", "arbitrary")),
    )(u, wb_re, wb_im, wc_re, wc_im, d_skip, *tabs[:6])


def _ssm_bwd_kb(dz, y, u, s_re, s_im, wb_re, wb_im, wc_re, wc_im, d_skip, tabs, q, dproj):
    l_dim = u.shape[0]
    nc = l_dim // q
    g = q // GRP

    def body(dz_ref, y_ref, u_ref, sr_ref, si_ref, wbr_ref, wbi_ref, wcr_ref, wci_ref, d_ref,
             a1r_ref, a1i_ref, sqr_ref, sqi_ref, revr_ref, revi_ref, dproj_ref,
             du_ref, dd_ref, dar_ref, dai_ref, dwbr_ref, dwbi_ref, dwcr_ref, dwci_ref,
             cr_ref, ci_ref, gr_ref, gi_ref):
        @pl.when(pl.program_id(1) == 0)
        def _():
            for ref in (cr_ref, ci_ref, dd_ref, dar_ref, dai_ref, dwbr_ref, dwbi_ref, dwcr_ref, dwci_ref):
                ref[...] = jnp.zeros_like(ref)

        tn = (((0,), (0,)), ((), ()))
        nt = (((1,), (1,)), ((), ()))
        u_kb = u_ref[...]
        dy = dz_ref[...].astype(F32) * _gelu_grad(y_ref[...].astype(F32))
        dyb = dy.astype(BF16)
        ub = u_kb.astype(BF16)
        dd_ref[...] += jnp.sum(dy * u_kb, axis=0, keepdims=True)
        ds_r = lax.dot_general(dyb, wcr_ref[0], nt, preferred_element_type=F32)
        ds_i = -lax.dot_general(dyb, wci_ref[0], nt, preferred_element_type=F32)
        for i in range(LB_KB):
            gr_ref[i] = ds_r[:, BLK * i:BLK * i + BLK]
            gi_ref[i] = ds_i[:, BLK * i:BLK * i + BLK]
        row = lax.broadcasted_iota(jnp.int32, (g, BLK), 0)
        for i in range(LB_KB):
            lanes = slice(BLK * i, BLK * i + BLK)
            ar, ai = a1r_ref[0:1, lanes], -a1i_ref[0:1, lanes]
            cin_r, cin_i = cr_ref[0:1, lanes], ci_ref[0:1, lanes]
            tr, ti = _chunk_carries(gr_ref, gi_ref, i, ar, ai, sqr_ref, sqi_ref, revr_ref[:, lanes],
                                    -revi_ref[:, lanes], cin_r, cin_i, -1.0, reverse=True)
            cr_ref[0:1, lanes] = jnp.sum(jnp.where(row == 0, tr, 0.0), axis=0, keepdims=True)
            ci_ref[0:1, lanes] = jnp.sum(jnp.where(row == 0, ti, 0.0), axis=0, keepdims=True)
            nr = jnp.where(row == g - 1, cin_r, pltpu.roll(tr, g - 1, axis=0))
            ni = jnp.where(row == g - 1, cin_i, pltpu.roll(ti, g - 1, axis=0))
            acc_r = jnp.zeros((g, BLK), F32)
            acc_i = jnp.zeros((g, BLK), F32)
            for r in range(GRP - 1, -1, -1):
                s_r, s_i = _lane_slab(sr_ref, i, r, g), _lane_slab(si_ref, i, r, g)
                acc_r = acc_r + (nr * s_r + ni * s_i)
                acc_i = acc_i + (ni * s_r - nr * s_i)
                nr, ni = (_lane_slab(gr_ref, i, r, g) + ar * nr - ai * ni,
                          _lane_slab(gi_ref, i, r, g) + ar * ni + ai * nr)
                gr_ref[i, _strided(r, g), :] = nr
                gi_ref[i, _strided(r, g), :] = ni
            dar_ref[:, lanes] += jnp.sum(acc_r, axis=0, keepdims=True)
            dai_ref[:, lanes] += jnp.sum(acc_i, axis=0, keepdims=True)
        grb = jnp.concatenate([gr_ref[i] for i in range(LB_KB)], axis=1).astype(BF16)
        gib = jnp.concatenate([gi_ref[i] for i in range(LB_KB)], axis=1).astype(BF16)
        srb = jnp.concatenate([sr_ref[i] for i in range(LB_KB)], axis=1).astype(BF16)
        sib = jnp.concatenate([si_ref[i] for i in range(LB_KB)], axis=1).astype(BF16)
        du = (lax.dot_general(grb, wbr_ref[0], nt, preferred_element_type=F32)
              + lax.dot_general(gib, wbi_ref[0], nt, preferred_element_type=F32)
              + d_ref[...] * dy)
        du_ref[...] = du.astype(BF16)
        dwbr_ref[0] += lax.dot_general(ub, grb, tn, preferred_element_type=F32)
        dwbi_ref[0] += lax.dot_general(ub, gib, tn, preferred_element_type=F32)
        dwcr_ref[0] += lax.dot_general(srb, dyb, tn, preferred_element_type=F32)
        dwci_ref[0] -= lax.dot_general(sib, dyb, tn, preferred_element_type=F32)

    chan = pl.BlockSpec((q, BLK), lambda k, c: (nc - 1 - c, k))
    wb_spec = pl.BlockSpec((1, BLK, ST_KB), lambda k, c: (k, 0, 0))
    wc_spec = pl.BlockSpec((1, ST_KB, BLK), lambda k, c: (k, 0, 0))
    tab_in = [tabs[0], tabs[1], tabs[2], tabs[3], tabs[6], tabs[7]]
    tab_specs = [pl.BlockSpec((t.shape[0], ST_KB), lambda k, c: (0, k)) for t in tab_in]
    vec = pl.BlockSpec((1, BLK), lambda k, c: (0, k))
    svec = pl.BlockSpec((1, ST_KB), lambda k, c: (0, k))
    state_spec = pl.BlockSpec((LB_KB, q, BLK), lambda k, c: (k, nc - 1 - c, 0))
    du_spec = pl.BlockSpec((q, BLK), lambda k, c: (nc - 1 - c, 2 * D_MODEL // BLK + k))
    in_specs = ([chan, chan, chan, state_spec, state_spec, wb_spec, wb_spec, wc_spec, wc_spec, vec] + tab_specs
                + [pl.BlockSpec(memory_space=pl.ANY)])
    return pl.pallas_call(
        body, name="ssm_bwd", grid=(SSM_KB, nc),
        in_specs=in_specs,
        out_specs=[du_spec, vec, svec, svec, wb_spec, wb_spec, wc_spec, wc_spec],
        input_output_aliases={len(in_specs) - 1: 0},
        out_shape=[jax.ShapeDtypeStruct(dproj.shape, BF16), jax.ShapeDtypeStruct((1, D_MODEL), F32),
                   jax.ShapeDtypeStruct((1, N_STATE), F32), jax.ShapeDtypeStruct((1, N_STATE), F32),
                   jax.ShapeDtypeStruct((SSM_KB, BLK, ST_KB), F32), jax.ShapeDtypeStruct((SSM_KB, BLK, ST_KB), F32),
                   jax.ShapeDtypeStruct((SSM_KB, ST_KB, BLK), F32), jax.ShapeDtypeStruct((SSM_KB, ST_KB, BLK), F32)],
        scratch_shapes=[pltpu.VMEM((8, ST_KB), F32), pltpu.VMEM((8, ST_KB), F32),
                        pltpu.VMEM((LB_KB, q, BLK), F32), pltpu.VMEM((LB_KB, q, BLK), F32)],
        compiler_params=_params(dimension_semantics=("parallel", "arbitrary")),
    )(dz, y, u, s_re, s_im, wb_re, wb_im, wc_re, wc_im, d_skip, *tab_in, dproj)


def _discretize(lam_re, lam_im, log_dt, b_re, b_im):
    dt = jnp.exp(log_dt)[:, None]
    mag = jnp.exp(lam_re * dt)
    ar, ai = mag * jnp.cos(lam_im * dt), mag * jnp.sin(lam_im * dt)
    den = lam_re * lam_re + lam_im * lam_im
    nr, ni = ar - 1.0, ai
    fr, fi = (nr * lam_re + ni * lam_im) / den, (ni * lam_re - nr * lam_im) / den
    bbar_re = fr[..., None] * b_re - fi[..., None] * b_im
    bbar_im = fr[..., None] * b_im + fi[..., None] * b_re
    return ar, ai, bbar_re, bbar_im


def _block_diag_b(bbar):
    eye = jnp.eye(8, dtype=bbar.dtype)
    return jnp.einsum("kgpc,gh->kgchp", bbar.reshape(8, 8, SSM_STATE, SSM_GROUP_CH), eye).reshape(8, BLK, ST_KB)


def _block_diag_b_t(dwb):
    eye = jnp.eye(8, dtype=dwb.dtype)
    return jnp.einsum("kgchp,gh->kgpc", dwb.reshape(8, 8, SSM_GROUP_CH, 8, SSM_STATE), eye).reshape(
        SSM_GROUPS, SSM_STATE, SSM_GROUP_CH)


def _block_diag_c(c):
    eye = jnp.eye(8, dtype=c.dtype)
    return jnp.einsum("kgcp,gh->kgphc", c.reshape(8, 8, SSM_GROUP_CH, SSM_STATE), eye).reshape(8, ST_KB, BLK)


def _block_diag_c_t(dwc):
    eye = jnp.eye(8, dtype=dwc.dtype)
    return jnp.einsum("kgphc,gh->kgcp", dwc.reshape(8, 8, SSM_STATE, 8, SSM_GROUP_CH), eye).reshape(
        SSM_GROUPS, SSM_GROUP_CH, SSM_STATE)


def _powers(br, bi, n):
    pr, pi = br, bi
    cr, ci = br, bi
    while pr.shape[0] < n:
        pr, pi = (jnp.concatenate([pr, pr * cr - pi * ci], axis=0),
                  jnp.concatenate([pi, pr * ci + pi * cr], axis=0))
        cr, ci = cr * cr - ci * ci, 2.0 * cr * ci
    return pr[:n], pi[:n]


def _powers_desc(br, bi, n):
    pr, pi = br, bi
    cr, ci = br, bi
    while pr.shape[0] < n:
        pr, pi = (jnp.concatenate([pr * cr - pi * ci, pr], axis=0),
                  jnp.concatenate([pr * ci + pi * cr, pi], axis=0))
        cr, ci = cr * cr - ci * ci, 2.0 * cr * ci
    return pr, pi


def _power_tables(ar, ai, g):
    a1r, a1i = _powers(ar, ai, GRP)
    seqr, seqi = _powers(a1r[GRP - 1:], a1i[GRP - 1:], g)
    g2 = 1 << (g - 1).bit_length()
    revr, revi = _powers_desc(a1r[GRP - 1:], a1i[GRP - 1:], g2)
    revr, revi = revr[g2 - g:], revi[g2 - g:]
    sq_r, sq_i = [seqr[0:1]], [seqi[0:1]]
    while len(sq_r) < 8:
        r, i = sq_r[-1], sq_i[-1]
        sq_r.append(r * r - i * i)
        sq_i.append(2.0 * r * i)
    sqr, sqi = jnp.concatenate(sq_r, axis=0), jnp.concatenate(sq_i, axis=0)
    return a1r, a1i, sqr, sqi, seqr, seqi, revr, revi


HBM_SPEC = pl.BlockSpec(memory_space=pltpu.HBM)
SEM_SPEC = pl.BlockSpec(memory_space=pltpu.SEMAPHORE)
DATAFLOW = pltpu.SideEffectType.DATAFLOW_SIDE_EFFECTING


def _plain_rows(p, m):
    return p * m


def _ffn_in_rows(p, m):
    return ((p & 3) >> 1) * (4 * m) + (p >> 2) * (2 * m) + (p & 1) * m


def _peer_copies(src_refs, land_refs, send_sems, recv_sems, chunked, row_fns):
    x, y, c = lax.axis_index("x"), lax.axis_index("y"), lax.axis_index("c")
    me = 4 * x + 2 * y + c
    copies = []
    for a, (src, land) in enumerate(zip(src_refs, land_refs)):
        m = land.shape[0] // N_DEV
        for k in range(N_DEV - 1):
            rel = k + 1
            bx, by, bc = (rel >> 2) & 1, (rel >> 1) & 1, rel & 1
            peer = (x + bx - 2 * x * bx, y + by - 2 * y * by, c + bc - 2 * c * bc)
            p_idx = 4 * peer[0] + 2 * peer[1] + peer[2]
            copies.append(pltpu.make_async_remote_copy(
                src_ref=src.at[pl.ds(row_fns[a](p_idx, m), m), :] if chunked else src,
                dst_ref=land.at[pl.ds(me * m if chunked else row_fns[a](me, m), m), :],
                send_sem=send_sems[a * (N_DEV - 1) + k], recv_sem=recv_sems[a * (N_DEV - 1) + k],
                device_id=peer, device_id_type=MESH))
    return copies


def _send_start(name, srcs, lands, chunked, row_fns=None):
    n = len(srcs)
    ns = n * (N_DEV - 1)
    row_fns = row_fns or [_plain_rows] * n

    def body(*refs):
        src_refs, land_refs = refs[:n], refs[n:2 * n]
        send_sems, recv_sems = refs[2 * n:2 * n + ns], refs[2 * n + ns:2 * n + 2 * ns]
        token = refs[-1]
        for cp in _peer_copies(src_refs, land_refs, send_sems, recv_sems, chunked, row_fns):
            cp.start()
        token[...] = jnp.zeros_like(token)

    ins = [pltpu.with_memory_space_constraint(a, pltpu.HBM) for a in list(srcs) + list(lands)]
    out = pl.pallas_call(
        body, name=name,
        in_specs=[HBM_SPEC] * (2 * n),
        out_specs=[SEM_SPEC] * (2 * ns) + [HBM_SPEC] * (2 * n) + [pl.BlockSpec(memory_space=pltpu.VMEM)],
        out_shape=[pltpu.SemaphoreType.DMA(())] * (2 * ns)
        + [pltpu.HBM(a.shape, a.dtype) for a in list(srcs) + list(lands)]
        + [jax.ShapeDtypeStruct((8, BLK), F32)],
        input_output_aliases={i: i + 2 * ns for i in range(2 * n)},
        compiler_params=pltpu.CompilerParams(has_side_effects=DATAFLOW),
    )(*ins)
    return out[:ns], out[ns:2 * ns], out[2 * ns:2 * ns + n], out[2 * ns + n:2 * ns + 2 * n], out[-1]


def _send_wait(name, send_sems, recv_sems, srcs, lands, after, chunked, row_fns=None):
    n = len(srcs)
    ns = n * (N_DEV - 1)
    row_fns = row_fns or [_plain_rows] * n

    def body(*refs):
        src_refs, land_refs = refs[:n], refs[n:2 * n]
        s_sems, r_sems = refs[2 * n:2 * n + ns], refs[2 * n + ns:2 * n + 2 * ns]
        copies = _peer_copies(src_refs, land_refs, s_sems, r_sems, chunked, row_fns)
        for cp in copies:
            cp.wait_send()
        for cp in copies:
            cp.wait_recv()

    out = pl.pallas_call(
        body, name=name,
        in_specs=[HBM_SPEC] * (2 * n) + [SEM_SPEC] * (2 * ns) + [pl.BlockSpec(memory_space=pl.ANY)],
        out_specs=[HBM_SPEC] * (2 * n),
        out_shape=[pltpu.HBM(a.shape, a.dtype) for a in list(srcs) + list(lands)],
        input_output_aliases={i: i for i in range(2 * n)},
        compiler_params=pltpu.CompilerParams(has_side_effects=DATAFLOW),
    )(*srcs, *lands, *send_sems, *recv_sems, after)
    return out[n:]


def _sum_slots(name, recv, own):
    m, ncol = own.shape
    tr = m // 2 if (m // 2) % 16 == 0 else m
    g = m // tr

    def body(*refs):
        slots, own_ref, o_ref = refs[:N_DEV], refs[N_DEV], refs[N_DEV + 1]
        me = _my_index()
        tot = None
        for s in range(N_DEV):
            v = jnp.where(me == s, own_ref[...], slots[s][...].astype(F32))
            tot = v if tot is None else tot + v
        o_ref[...] = tot

    def slot_spec(s):
        return pl.BlockSpec((tr, ncol), lambda i: (s * g + i, 0))

    return pl.pallas_call(
        body, name=name, grid=(g,),
        in_specs=[slot_spec(s) for s in range(N_DEV)] + [pl.BlockSpec((tr, ncol), lambda i: (i, 0))],
        out_specs=pl.BlockSpec((tr, ncol), lambda i: (i, 0)),
        out_shape=jax.ShapeDtypeStruct((m, ncol), F32),
        compiler_params=_params(),
    )(*([recv] * N_DEV), own)


def _sum_gathered(name, gathered, rows):
    tr = _pick(rows, 512, 8)
    g = rows // tr

    def body(*refs):
        o_ref = refs[N_DEV]
        tot = refs[0][...]
        for s in range(1, N_DEV):
            tot = tot + refs[s][...]
        o_ref[...] = tot

    return pl.pallas_call(
        body, name=name, grid=(g,),
        in_specs=[pl.BlockSpec((tr, BLK), (lambda i, s=s: (s * g + i, 0))) for s in range(N_DEV)],
        out_specs=pl.BlockSpec((tr, BLK), lambda i: (i, 0)),
        out_shape=jax.ShapeDtypeStruct((rows, BLK), F32),
        compiler_params=_params(),
    )(*([gathered] * N_DEV))


def _adamw(name, w, g, m, v):
    r, c = w.shape
    tr = _pick(r, 256, 8) if r % 8 == 0 else r
    c1 = 1.0 - ADAM_B1 ** ADAM_STEP
    c2 = 1.0 - ADAM_B2 ** ADAM_STEP

    def body(w_ref, g_ref, m_ref, v_ref, d_ref, nm_ref, nv_ref):
        gv = g_ref[...]
        nm = ADAM_B1 * m_ref[...] + (1.0 - ADAM_B1) * gv
        nv = ADAM_B2 * v_ref[...] + (1.0 - ADAM_B2) * (gv * gv)
        m_hat = nm / c1
        v_hat = nv / c2
        d_ref[...] = -ADAM_LR * (m_hat / (jnp.sqrt(v_hat) + ADAM_EPS) + ADAM_WD * w_ref[...])
        nm_ref[...] = nm
        nv_ref[...] = nv

    spec = pl.BlockSpec((tr, c), lambda i: (i, 0))
    shape = jax.ShapeDtypeStruct((r, c), F32)
    return pl.pallas_call(
        body, name=name, grid=(r // tr,),
        in_specs=[spec] * 4, out_specs=[spec] * 3, out_shape=[shape] * 3,
        compiler_params=_params(),
    )(w, g, m, v)


def _adamw_many(name, ws, gs, ms, vs):
    n = len(ws)
    c1 = 1.0 - ADAM_B1 ** ADAM_STEP
    c2 = 1.0 - ADAM_B2 ** ADAM_STEP

    def body(*refs):
        for a in range(n):
            w_ref, g_ref, m_ref, v_ref = refs[a], refs[n + a], refs[2 * n + a], refs[3 * n + a]
            d_ref, nm_ref, nv_ref = refs[4 * n + a], refs[5 * n + a], refs[6 * n + a]
            gv = g_ref[...]
            nm = ADAM_B1 * m_ref[...] + (1.0 - ADAM_B1) * gv
            nv = ADAM_B2 * v_ref[...] + (1.0 - ADAM_B2) * (gv * gv)
            d_ref[...] = -ADAM_LR * ((nm / c1) / (jnp.sqrt(nv / c2) + ADAM_EPS) + ADAM_WD * w_ref[...])
            nm_ref[...] = nm
            nv_ref[...] = nv

    specs = [_full_spec(w.shape) for w in ws]
    shapes = [jax.ShapeDtypeStruct(w.shape, F32) for w in ws]
    out = pl.pallas_call(
        body, name=name, grid=(1,),
        in_specs=specs * 4, out_specs=specs * 3, out_shape=shapes * 3,
        compiler_params=_params(),
    )(*ws, *gs, *ms, *vs)
    return out[:n], out[n:2 * n], out[2 * n:]


PACK_ROWS = 128


def _pack(parts):
    flat = []
    for p in parts:
        v = p.reshape(-1)
        flat.append(jnp.pad(v, (0, (-v.shape[0]) % BLK)))
    v = jnp.concatenate(flat)
    v = jnp.pad(v, (0, (-v.shape[0]) % (PACK_ROWS * BLK)))
    return v.reshape(-1, BLK)


def _unpack(buf, shapes):
    flat = buf.reshape(-1)
    out, off = [], 0
    for shp in shapes:
        size = math.prod(shp)
        out.append(flat[off:off + size].reshape(shp))
        off += size + (-size) % BLK
    return out


def kernel(x, meta_tokens, norm_mix, w_in, q_norm, k_norm, attn_sinks, lam_re, lam_im, log_dt, ssm_b_re, ssm_b_im, ssm_c_re, ssm_c_im, ssm_d, w_glu, attn_branch_norm, ssm_branch_norm, w_out, norm_ffn, w_ffn_in, w_ffn_out, loss_target, m_meta_tokens, m_norm_mix, m_w_in, m_q_norm, m_k_norm, m_attn_sinks, m_lam_re, m_lam_im, m_log_dt, m_ssm_b_re, m_ssm_b_im, m_ssm_c_re, m_ssm_c_im, m_ssm_d, m_w_glu, m_attn_branch_norm, m_ssm_branch_norm, m_w_out, m_norm_ffn, m_w_ffn_in, m_w_ffn_out, v_meta_tokens, v_norm_mix, v_w_in, v_q_norm, v_k_norm, v_attn_sinks, v_lam_re, v_lam_im, v_log_dt, v_ssm_b_re, v_ssm_b_im, v_ssm_c_re, v_ssm_c_im, v_ssm_d, v_w_glu, v_attn_branch_norm, v_ssm_branch_norm, v_w_out, v_norm_ffn, v_w_ffn_in, v_w_ffn_out):
    args = dict(locals())
    weights = {n: args[n] for n in WEIGHTS}
    mom_m = {n: args["m_" + n] for n in WEIGHTS}
    mom_v = {n: args["v_" + n] for n in WEIGHTS}

    x2d = x[0]
    target2d = loss_target[0]
    s_len = x2d.shape[0]
    l_dim = s_len + BLK
    tm_row = _pick(l_dim, 320)
    tm_mm = _pick(l_dim, 1040)
    tl_tn = _pick(l_dim, 2080)
    tm_ffn = _pick(l_dim, 640)
    tm_big = _pick(l_dim, 2080)
    tm_shift = _pick(l_dim, 640, BLK)

    shard_in = w_in[0].T.astype(BF16)
    shard_glu = w_glu[0].T.astype(BF16)
    shard_out = w_out[0].astype(BF16)
    shard_ffn_in = w_ffn_in[0].T.astype(BF16)
    shard_ffn_out = w_ffn_out[0].astype(BF16)
    shard_meta = meta_tokens.T
    me = _my_index()

    def landing(shard, row_fn=_plain_rows):
        m_rows, cols = shard.shape
        return lax.dynamic_update_slice(lax.empty((N_DEV * m_rows, cols), shard.dtype), shard,
                                        (row_fn(me, m_rows), 0))

    first = [shard_in, shard_meta]
    ga = _send_start("gather_start_a", first, [landing(s) for s in first], chunked=False)
    later = [shard_glu + ga[4][0:1, 0:1].astype(BF16), shard_out, shard_ffn_in, shard_ffn_out]
    later_fns = [_plain_rows, _plain_rows, _ffn_in_rows, _plain_rows]
    gb = _send_start("gather_start_b", later, [landing(s, f) for s, f in zip(later, later_fns)], chunked=False,
                     row_fns=later_fns)

    nm_t = norm_mix + (ga[4][0:1, 0:1] + gb[4][0:1, 0:1])
    qn_t, kn_t = jnp.tile(q_norm, (1, N_Q_HEADS)), jnp.tile(k_norm, (1, N_KV_HEADS))
    e_mat = jnp.kron(jnp.eye(4, dtype=F32), jnp.ones((HEAD_DIM, HEAD_DIM), F32)).astype(BF16)

    def disc(lr, li, ldt, br, bi):
        return _discretize(lr[0], li[0], ldt[0], br[0], bi[0])

    (abar_re, abar_im, bbar_re, bbar_im), disc_vjp = jax.vjp(disc, lam_re, lam_im, log_dt, ssm_b_re, ssm_b_im)
    wb_re, wb_im = _block_diag_b(bbar_re).astype(BF16), _block_diag_b(bbar_im).astype(BF16)
    wc_re, wc_im = _block_diag_c(ssm_c_re[0]).astype(BF16), _block_diag_c(ssm_c_im[0]).astype(BF16)
    q_ssm = _pick(l_dim, 640, 64)
    tabs = _power_tables(abar_re.reshape(1, N_STATE), abar_im.reshape(1, N_STATE), q_ssm // GRP)

    h0, xn = _embed_norm(x2d, nm_t, tm_shift)
    wt_in, meta_t = _send_wait("gather_wait_a", ga[0], ga[1], ga[2], ga[3], xn, chunked=False)
    meta_pad = jnp.pad(meta_t.T, ((PAD, 0), (0, 0)))
    h0, xn = _embed_meta(meta_pad, nm_t, h0, xn)
    qkv_w, u_end = Q_W + 2 * KV_W, Q_W + 2 * KV_W + D_MODEL
    wt_in_p = jnp.concatenate([wt_in[u_end:], wt_in[qkv_w:u_end], wt_in[:qkv_w]], axis=0)
    qkv = _matmul("proj_qkv", xn, wt_in_p, nt=True, tm=tm_big, tn=qkv_w, tk=D_MODEL, n=qkv_w, w_off=2)
    u = _matmul("proj_u", xn, wt_in_p, nt=True, tm=tm_big, tn=D_MODEL, tk=D_MODEL, n=D_MODEL, w_off=2)
    gates = _matmul("proj_gates", xn, wt_in_p, nt=True, tm=tm_big, tn=D_MODEL, tk=D_MODEL, n=2 * D_MODEL, w_off=0,
                    out_dtype=BF16)
    qn, kf, vf = _qk_prep(qkv, qn_t, kn_t, e_mat, tm_row)
    attn, lse = _attn_fwd(qn, kf, vf, attn_sinks)
    y, z, s_re, s_im = _ssm_fwd_kb(u, wb_re, wb_im, wc_re, wc_im, ssm_d, tabs, q_ssm)
    wt_glu, w_out_f, wt_ffn_in, w_ffn_out_f = _send_wait("gather_wait_b", gb[0], gb[1], gb[2], gb[3], z,
                                                         chunked=False, row_fns=later_fns)
    zab = _matmul("glu_proj", z, wt_glu, nt=True, tm=tm_big, tn=1024, tk=D_MODEL, out_dtype=BF16)
    merged = _merge_fwd(attn, zab, gates, attn_branch_norm, ssm_branch_norm, tm_row)
    h1, hn = _matmul("out_proj", merged, w_out_f, nt=False, tm=tm_mm, tn=1024, tk=D_MODEL, res=h0,
                     norm_g=norm_ffn)
    gu, act = _ffn_in_swiglu(hn, wt_ffn_in, tm_ffn)
    h2 = _matmul("ffn_out", act, w_ffn_out_f, nt=False, tm=tm_mm, tn=1024, tk=D_FF, res=h1)
    dh2, dh2_b, loss_part = _loss_grad(h2, target2d, tm_shift)

    dgu = _d_act_swiglu(dh2_b, w_ffn_out_f, gu, tm_ffn)

    def exchange_start(name, grads_b, row_fns=None):
        return _send_start(name, grads_b, [jnp.zeros(g.shape, BF16) for g in grads_b], chunked=True,
                           row_fns=row_fns)

    g_ffn_out, g_ffn_out_b = _matmul_tn("g_ffn_out", act, dh2_b, tm=1408, tn=1024, tl=tl_tn)
    g_ffn_in_t, g_ffn_in_b = _matmul_tn("g_ffn_in", dgu, hn, tm=1408, tn=1024, tl=tl_tn)
    ffn_fns = [_ffn_in_rows, _plain_rows]
    ex1 = exchange_start("exchange_start_ffn", [g_ffn_in_b, g_ffn_out_b], ffn_fns)
    dhn = _matmul("d_hn", dgu, wt_ffn_in, nt=False, tm=tm_ffn, tn=1024, tk=2 * D_FF, out_dtype=BF16)
    dh1, dh1_b, g_norm_ffn = _norm_bwd_res("ffn_norm_bwd", h1, norm_ffn + ex1[4][0:1, 0:1], dhn, dh2, tm_row)
    dmerged = _matmul("d_merged", dh1_b, w_out_f, nt=True, tm=tm_big, tn=1024, tk=D_MODEL, out_dtype=BF16)
    dattn, dzab, dproj, g_abn, g_sbn = _merge_bwd(attn, zab, gates, attn_branch_norm, ssm_branch_norm,
                                                    dmerged, tm_row)
    g_out, g_out_b = _matmul_tn("g_out", merged, dh1_b, tm=1024, tn=1024, tl=tl_tn)
    dz = _matmul("d_z", dzab, wt_glu, nt=False, tm=tm_mm, tn=1024, tk=2 * D_MODEL, out_dtype=BF16)
    g_glu_t, g_glu_b = _matmul_tn("g_glu", dzab, z, tm=1024, tn=1024, tl=tl_tn)
    ex2 = exchange_start("exchange_start_mix", [g_glu_b, g_out_b])
    dproj, g_ssm_d, g_ar, g_ai, g_wbr, g_wbi, g_wcr, g_wci = _ssm_bwd_kb(
        dz, y, u, s_re, s_im, wb_re, wb_im, wc_re, wc_im, ssm_d + ex2[4][0:1, 0:1], tabs, q_ssm, dproj)
    dq, dkc, dkp, dkm, dvc, dvp, dvm, g_sinks = _attn_bwd(qn, kf, vf, attn_sinks, lse, attn, dattn)
    dproj, g_qn_t, g_kn_t = _qk_bwd(qkv, qn_t, kn_t, e_mat, dq, dkc, dkp, dkm, dvc, dvp, dvm, dproj)
    g_lam_re, g_lam_im, g_log_dt, g_b_re, g_b_im = disc_vjp(
        (g_ar.reshape(SSM_GROUPS, SSM_STATE), g_ai.reshape(SSM_GROUPS, SSM_STATE),
         _block_diag_b_t(g_wbr), _block_diag_b_t(g_wbi)))
    small_grads = {
        "q_norm": g_qn_t.reshape(N_Q_HEADS, HEAD_DIM).sum(0)[None],
        "k_norm": g_kn_t.reshape(N_KV_HEADS, HEAD_DIM).sum(0)[None], "attn_sinks": g_sinks,
        "lam_re": g_lam_re, "lam_im": g_lam_im, "log_dt": g_log_dt, "ssm_b_re": g_b_re, "ssm_b_im": g_b_im,
        "ssm_c_re": _block_diag_c_t(g_wcr)[None], "ssm_c_im": _block_diag_c_t(g_wci)[None],
        "ssm_d": g_ssm_d, "attn_branch_norm": g_abn, "ssm_branch_norm": g_sbn, "norm_ffn": g_norm_ffn,
    }
    early = [n for n in SMALL if n != "norm_mix"]
    packed_e = _pack([small_grads[n] for n in early])
    gs_e = _send_start("small_start_a", [packed_e], [landing(packed_e)], chunked=False)

    blocks_g, blocks_u, blocks_q = 2 * D_MODEL // 512, D_MODEL // 512, qkv_w // 512
    back = lambda i: jnp.where(i < blocks_g, i + blocks_q + blocks_u,
                               jnp.where(i < blocks_g + blocks_u, i - blocks_g + blocks_q, i - blocks_g - blocks_u))
    g_in_t, g_in_b = _matmul_tn("g_in", dproj, xn, tm=512, tn=1024, tl=tl_tn, after=gs_e[4], out_rows=back)
    ex3 = exchange_start("exchange_start_in", [g_in_b])
    dxn = _matmul("d_xn", dproj, wt_in_p, nt=False, tm=tm_ffn, tn=1024, tk=IN_COLS, out_dtype=BF16)
    grad_x2d, dmeta_blk, g_norm_mix = _final_bwd(h0, nm_t + ex3[4][0:1, 0:1], dxn, dh1, _pick(s_len, 512, BLK))
    packed_l = _pack([g_norm_mix, dmeta_blk[PAD:], loss_part])
    gs_l = _send_start("small_start_b", [packed_l], [landing(packed_l)], chunked=False)
    grads, deltas, new_m, new_v = {}, {}, {}, {}

    recv_ffn_in, recv_ffn_out = _send_wait("exchange_wait_ffn", ex1[0], ex1[1], ex1[2], ex1[3], gs_l[4],
                                           chunked=True, row_fns=ffn_fns)
    recv_glu, recv_out = _send_wait("exchange_wait_mix", ex2[0], ex2[1], ex2[2], ex2[3], recv_ffn_in,
                                    chunked=True)
    (recv_in,) = _send_wait("exchange_wait_in", ex3[0], ex3[1], ex3[2], ex3[3], recv_glu, chunked=True)
    big = [("w_in", g_in_t, True, recv_in, _plain_rows), ("w_glu", g_glu_t, True, recv_glu, _plain_rows),
           ("w_out", g_out, False, recv_out, _plain_rows), ("w_ffn_in", g_ffn_in_t, True, recv_ffn_in, _ffn_in_rows),
           ("w_ffn_out", g_ffn_out, False, recv_ffn_out, _plain_rows)]
    for name, g_full, transposed, recv, row_fn in big:
        m_rows = g_full.shape[0] // N_DEV
        own = lax.dynamic_slice(g_full, (row_fn(me, m_rows), 0), (m_rows, g_full.shape[1]))
        g_shard = _sum_slots("sum_" + name, recv, own)
        grads[name] = (g_shard.T if transposed else g_shard)[None]

    def adamw_2d(name):
        shp = weights[name].shape
        if name in ("w_in", "w_ffn_in"):
            as2d, back = (lambda a: a.reshape(shp[-2], shp[-1]).T), (lambda a: a.T.reshape(shp))
        else:
            as2d, back = (lambda a: a.reshape(shp[-2], shp[-1])), (lambda a: a.reshape(shp))
        d, nm, nv = _adamw("adamw_" + name, as2d(weights[name]), as2d(grads[name]), as2d(mom_m[name]),
                           as2d(mom_v[name]))
        deltas[name], new_m[name], new_v[name] = back(d), back(nm), back(nv)
        return d

    for name in ["w_in", "w_glu", "w_out", "w_ffn_in", "w_ffn_out"]:
        last = adamw_2d(name)

    def small_sum(tag, gs, packed, after):
        (gathered,) = _send_wait("small_wait_" + tag, gs[0], gs[1], gs[2], gs[3], after, chunked=False)
        return _sum_gathered("sum_small_" + tag, gathered, packed.shape[0])

    def small_adamw(tag, names):
        view = lambda n, a: jnp.swapaxes(a, -1, -2) if n.startswith("ssm_b") else a
        d, nm, nv = _adamw_many("adamw_small_" + tag, [view(n, weights[n]) for n in names],
                                [view(n, grads[n]) for n in names], [view(n, mom_m[n]) for n in names],
                                [view(n, mom_v[n]) for n in names])
        deltas.update((n, view(n, a)) for n, a in zip(names, d))
        new_m.update((n, view(n, a)) for n, a in zip(names, nm))
        new_v.update((n, view(n, a)) for n, a in zip(names, nv))

    g_sum_e = small_sum("a", gs_e, packed_e, last)
    grads.update(zip(early, _unpack(g_sum_e, [weights[n].shape for n in early])))
    wide = [n for n in early if n.startswith(("ssm_b", "ssm_c"))]
    small_adamw("wide", wide)
    g_sum_l = small_sum("b", gs_l, packed_l, g_sum_e)
    grads["norm_mix"], g_meta, loss_sum = _unpack(g_sum_l, [weights["norm_mix"].shape, (N_META, D_MODEL), (1, 1)])
    small_adamw("rest", [n for n in SMALL if n not in wide])
    grads["meta_tokens"] = lax.dynamic_slice(g_meta, (0, me * BLK), (N_META, BLK))
    adamw_2d("meta_tokens")

    loss = loss_sum[0, 0]
    return (loss, grad_x2d[None], *[grads[n] for n in WEIGHTS], *[deltas[n] for n in WEIGHTS],
            *[new_m[n] for n in WEIGHTS], *[new_v[n] for n in WEIGHTS])
```

```python
import math

import jax
import jax.numpy as jnp
from jax import lax
from jax.experimental import pallas as pl
from jax.experimental.pallas import tpu as pltpu

F32 = jnp.float32
BF16 = jnp.bfloat16

D_MODEL = 1024
N_META = 16
HEAD_DIM = 64
N_Q_HEADS = 16
N_KV_HEADS = 4
Q_W = N_Q_HEADS * HEAD_DIM
KV_W = N_KV_HEADS * HEAD_DIM
SSM_GROUPS = 64
SSM_GROUP_CH = 16
SSM_STATE = 64
N_STATE = SSM_GROUPS * SSM_STATE
D_FF = 2816
IN_COLS = Q_W + 2 * KV_W + 3 * D_MODEL
EPS = 1e-6
BLK = 128
PAD = BLK - N_META
N_DEV = 8
NEG = -1e30
SSM_KB = 8
ST_KB = N_STATE // SSM_KB
LB_KB = ST_KB // BLK
N_LB = N_STATE // BLK

ADAM_LR = 0.001
ADAM_B1 = 0.9
ADAM_B2 = 0.999
ADAM_EPS = 1e-08
ADAM_WD = 0.01
ADAM_STEP = 10

VMEM_LIMIT = 48 * 1024 * 1024
MESH = pl.DeviceIdType.MESH

SMALL = ["norm_mix", "q_norm", "k_norm", "attn_sinks", "lam_re", "lam_im", "log_dt", "ssm_b_re", "ssm_b_im",
         "ssm_c_re", "ssm_c_im", "ssm_d", "attn_branch_norm", "ssm_branch_norm", "norm_ffn"]
WEIGHTS = ["meta_tokens", "norm_mix", "w_in", "q_norm", "k_norm", "attn_sinks", "lam_re", "lam_im", "log_dt",
           "ssm_b_re", "ssm_b_im", "ssm_c_re", "ssm_c_im", "ssm_d", "w_glu", "attn_branch_norm",
           "ssm_branch_norm", "w_out", "norm_ffn", "w_ffn_in", "w_ffn_out"]


def _params(**kw):
    return pltpu.CompilerParams(vmem_limit_bytes=VMEM_LIMIT, **kw)


def _pick(n, cap, mult=16):
    best = None
    for d in range(mult, min(n, cap) + 1, mult):
        if n % d == 0:
            best = d
    assert best is not None, (n, cap, mult)
    return best


def _my_index():
    return 4 * lax.axis_index("x") + 2 * lax.axis_index("y") + lax.axis_index("c")


def _rms(x, g):
    r = lax.rsqrt(jnp.mean(x * x, axis=-1, keepdims=True) + EPS)
    return x * r * g


def _rms_bwd(x, g, dy):
    r = lax.rsqrt(jnp.mean(x * x, axis=-1, keepdims=True) + EPS)
    t = dy * g
    dx = r * t - x * (r * r * r) * jnp.mean(t * x, axis=-1, keepdims=True)
    dg = jnp.sum(dy * (x * r), axis=0, keepdims=True)
    return dx, dg


def _sigmoid(x):
    return jax.nn.sigmoid(x)


def _gelu(x):
    k = math.sqrt(2.0 / math.pi)
    return 0.5 * x * (1.0 + jnp.tanh(k * (x + 0.044715 * (x * x * x))))


def _gelu_grad(x):
    k = math.sqrt(2.0 / math.pi)
    t = jnp.tanh(k * (x + 0.044715 * (x * x * x)))
    return 0.5 * (1.0 + t) + 0.5 * x * (1.0 - t * t) * (k * (1.0 + 3.0 * 0.044715 * (x * x)))


def _head_mean(x, e_ref):
    hi = x.astype(BF16)
    lo = (x - hi.astype(F32)).astype(BF16)
    e = e_ref[...]
    out = []
    for b in range(x.shape[1] // 256):
        sl = slice(256 * b, 256 * b + 256)
        s = (jnp.dot(hi[:, sl], e, preferred_element_type=F32)
             + jnp.dot(lo[:, sl], e, preferred_element_type=F32))
        out.append(s)
    s = out[0] if len(out) == 1 else jnp.concatenate(out, axis=1)
    return s * (1.0 / HEAD_DIM)


def _head_rms(x, g, e_ref):
    r = lax.rsqrt(_head_mean(x * x, e_ref) + EPS)
    return x * r * g


def _head_rms_bwd(x, g, dy, e_ref):
    r = lax.rsqrt(_head_mean(x * x, e_ref) + EPS)
    t = dy * g
    dx = r * t - x * (r * r * r) * _head_mean(t * x, e_ref)
    dg = jnp.sum(dy * (x * r), axis=0, keepdims=True)
    return dx, dg


def _lane_half(shape):
    lane = lax.broadcasted_iota(jnp.int32, shape, len(shape) - 1)
    return (lane >> 6) & 1


def _matmul(name, a, w, *, nt, tm, tn, tk, n=None, w_off=0, res=None, norm_g=None, out_dtype=F32):
    m_dim, k_dim = a.shape
    n_dim = n if n is not None else (w.shape[0] if nt else w.shape[1])
    gm, gn, gk = m_dim // tm, n_dim // tn, k_dim // tk
    assert gm * tm == m_dim and gn * tn == n_dim and gk * tk == k_dim, (name, a.shape, w.shape, tm, tn, tk)
    assert norm_g is None or tn == n_dim
    direct = out_dtype == F32
    dn = (((1,), (1,)), ((), ())) if nt else (((1,), (0,)), ((), ()))

    def body(*refs):
        refs = list(refs)
        a_ref, w_ref = refs[0], refs[1]
        pos = 2
        r_ref = g_ref = on_ref = None
        if res is not None:
            r_ref, pos = refs[pos], pos + 1
        if norm_g is not None:
            g_ref, pos = refs[pos], pos + 1
        o_ref, pos = refs[pos], pos + 1
        if norm_g is not None:
            on_ref, pos = refs[pos], pos + 1
        part = lax.dot_general(a_ref[...], w_ref[...], dn, preferred_element_type=F32)
        if gk == 1:
            r = part if r_ref is None else r_ref[...] + part
            o_ref[...] = r.astype(out_dtype)
            if on_ref is not None:
                on_ref[...] = _rms(r, g_ref[...]).astype(BF16)
            return
        acc = o_ref if direct else refs[pos]
        k = pl.program_id(2)

        @pl.when(k == 0)
        def _():
            acc[...] = part if r_ref is None or not direct else r_ref[...] + part

        @pl.when(k > 0)
        def _():
            acc[...] += part

        @pl.when(k == gk - 1)
        def _():
            if not direct:
                r = acc[...]
                if r_ref is not None:
                    r = r_ref[...] + r
                o_ref[...] = r.astype(out_dtype)
            if on_ref is not None:
                on_ref[...] = _rms(o_ref[...].astype(F32), g_ref[...]).astype(BF16)

    if nt:
        w_spec = pl.BlockSpec((tn, tk), lambda i, j, k: (j + w_off, k))
    else:
        w_spec = pl.BlockSpec((tk, tn), lambda i, j, k: (k, j))
    in_specs = [pl.BlockSpec((tm, tk), lambda i, j, k: (i, k)), w_spec]
    args = [a, w]
    out_spec = pl.BlockSpec((tm, tn), lambda i, j, k: (i, j))
    out_specs, out_shape = [out_spec], [jax.ShapeDtypeStruct((m_dim, n_dim), out_dtype)]
    if res is not None:
        in_specs.append(out_spec)
        args.append(res)
    if norm_g is not None:
        in_specs.append(pl.BlockSpec((1, tn), lambda i, j, k: (0, 0)))
        args.append(norm_g)
        out_specs.append(out_spec)
        out_shape.append(jax.ShapeDtypeStruct((m_dim, n_dim), BF16))
    out = pl.pallas_call(
        body, name=name, grid=(gm, gn, gk),
        in_specs=in_specs, out_specs=out_specs, out_shape=out_shape,
        scratch_shapes=[] if direct or gk == 1 else [pltpu.VMEM((tm, tn), F32)],
        compiler_params=_params(dimension_semantics=("parallel", "parallel", "arbitrary")),
    )(*args)
    return out if norm_g is not None else out[0]


def _matmul_tn(name, a, b, *, tm, tn, tl, after=None, out_rows=None):
    l_dim, m_dim = a.shape
    n_dim = b.shape[1]
    gm, gn, gl = m_dim // tm, n_dim // tn, l_dim // tl
    assert gm * tm == m_dim and gn * tn == n_dim and gl * tl == l_dim, (name, a.shape, b.shape, tm, tn, tl)

    def body(*refs):
        a_ref, b_ref = refs[0], refs[1]
        o_ref, ob_ref = refs[-2], refs[-1]

        @pl.when(pl.program_id(2) == 0)
        def _():
            o_ref[...] = jnp.zeros_like(o_ref)

        o_ref[...] += lax.dot_general(a_ref[...], b_ref[...], (((0,), (0,)), ((), ())),
                                      preferred_element_type=F32)

        @pl.when(pl.program_id(2) == gl - 1)
        def _():
            ob_ref[...] = o_ref[...].astype(BF16)

    out_row = out_rows if out_rows is not None else (lambda i: i)
    out_spec = pl.BlockSpec((tm, tn), lambda i, j, l: (out_row(i), j))
    in_specs = [pl.BlockSpec((tl, tm), lambda i, j, l: (l, i)), pl.BlockSpec((tl, tn), lambda i, j, l: (l, j))]
    args = [a, b]
    if after is not None:
        in_specs.append(pl.BlockSpec(memory_space=pl.ANY))
        args.append(after)
    return pl.pallas_call(
        body, name=name, grid=(gm, gn, gl),
        in_specs=in_specs,
        out_specs=[out_spec, out_spec],
        out_shape=[jax.ShapeDtypeStruct((m_dim, n_dim), F32), jax.ShapeDtypeStruct((m_dim, n_dim), BF16)],
        compiler_params=_params(dimension_semantics=("parallel", "parallel", "arbitrary")),
    )(*args)


def _row_spec(tm, cols, f=None):
    if f is None:
        return pl.BlockSpec((tm, cols), lambda i: (i, 0))
    return pl.BlockSpec((tm, cols), lambda i: (f(i), 0))


def _full_spec(shape):
    nd = len(shape)
    return pl.BlockSpec(shape, lambda i: (0,) * nd)


def _shifted_specs(n_sub, n_blocks):
    return [_row_spec(BLK, D_MODEL, (lambda i, k=k: jnp.clip(n_sub * i - 1 + k, 0, n_blocks - 1)))
            for k in range(n_sub)]


def _embed_norm(x2d, g, tm):
    s_len = x2d.shape[0]
    l_dim = s_len + BLK
    n_sub = tm // BLK

    def body(*refs):
        x_refs, g_ref, h_ref, xn_ref = refs[:n_sub], refs[n_sub], refs[n_sub + 1], refs[n_sub + 2]
        i = pl.program_id(0)
        for k in range(n_sub):
            rows = slice(BLK * k, BLK * k + BLK)
            h = x_refs[k][...] * jnp.where(n_sub * i + k >= 1, 1.0, 0.0)
            h_ref[rows, :] = h
            xn_ref[rows, :] = _rms(h, g_ref[...]).astype(BF16)

    return pl.pallas_call(
        body, name="embed_norm", grid=(l_dim // tm,),
        in_specs=_shifted_specs(n_sub, s_len // BLK) + [_full_spec((1, D_MODEL))],
        out_specs=[_row_spec(tm, D_MODEL), _row_spec(tm, D_MODEL)],
        out_shape=[jax.ShapeDtypeStruct((l_dim, D_MODEL), F32),
                   jax.ShapeDtypeStruct((l_dim, D_MODEL), BF16)],
        compiler_params=_params(),
    )(*([x2d] * n_sub), g)


def _embed_meta(meta_pad, g, h0, xn):
    def body(mp_ref, g_ref, h_in, xn_in, h_ref, xn_ref):
        h_ref[...] = mp_ref[...]
        xn_ref[...] = _rms(mp_ref[...], g_ref[...]).astype(BF16)

    any_spec = pl.BlockSpec(memory_space=pl.ANY)
    return pl.pallas_call(
        body, name="embed_meta", grid=(1,),
        in_specs=[_full_spec((BLK, D_MODEL)), _full_spec((1, D_MODEL)), any_spec, any_spec],
        out_specs=[_row_spec(BLK, D_MODEL), _row_spec(BLK, D_MODEL)],
        out_shape=[jax.ShapeDtypeStruct(h0.shape, F32), jax.ShapeDtypeStruct(xn.shape, BF16)],
        input_output_aliases={2: 0, 3: 1},
        compiler_params=_params(),
    )(meta_pad, g, h0, xn)


KVX_W = 2 * N_KV_HEADS * BLK


def _qk_prep(qkv, q_norm_t, k_norm_t, e_mat, tm):
    l_dim = qkv.shape[0]

    def body(x_ref, qg_ref, kg_ref, e_ref, q_ref, kf_ref, vf_ref):
        x = x_ref[...]
        q = _head_rms(x[:, :Q_W], qg_ref[...], e_ref) * (HEAD_DIM ** -0.5)
        q_ref[...] = q.astype(BF16)
        k = _head_rms(x[:, Q_W:Q_W + KV_W], kg_ref[...], e_ref)
        v = x[:, Q_W + KV_W:Q_W + 2 * KV_W]
        half = _lane_half((tm, BLK))
        for src, dst in ((k, kf_ref), (v, vf_ref)):
            for kv in range(N_KV_HEADS):
                blk = src[:, BLK * (kv // 2):BLK * (kv // 2) + BLK]
                swapped = pltpu.roll(blk, HEAD_DIM, axis=1)
                for e in range(2):
                    val = blk if kv % 2 == e else swapped
                    idx = 2 * kv + e
                    dst[:, BLK * idx:BLK * idx + BLK] = jnp.where(half == e, val, 0.0).astype(BF16)

    return pl.pallas_call(
        body, name="qk_prep", grid=(l_dim // tm,),
        in_specs=[_row_spec(tm, Q_W + 2 * KV_W), _full_spec((1, Q_W)), _full_spec((1, KV_W)),
                  _full_spec((256, 256))],
        out_specs=[_row_spec(tm, Q_W), _row_spec(tm, KVX_W), _row_spec(tm, KVX_W)],
        out_shape=[jax.ShapeDtypeStruct((l_dim, Q_W), BF16),
                   jax.ShapeDtypeStruct((l_dim, KVX_W), BF16),
                   jax.ShapeDtypeStruct((l_dim, KVX_W), BF16)],
        compiler_params=_params(),
    )(qkv, q_norm_t, k_norm_t, e_mat)


def _merge_fwd(attn, zab, gates, abn, sbn, tm):
    l_dim = attn.shape[0]

    def body(a_ref, z_ref, g_ref, an_ref, sn_ref, o_ref):
        z = z_ref[...].astype(F32)
        g = g_ref[...].astype(F32)
        ssm = z[:, :D_MODEL] * _sigmoid(z[:, D_MODEL:])
        merged = (_sigmoid(g[:, :D_MODEL]) * _rms(a_ref[...], an_ref[...])
                  + _sigmoid(g[:, D_MODEL:]) * _rms(ssm, sn_ref[...]))
        o_ref[...] = merged.astype(BF16)

    return pl.pallas_call(
        body, name="merge_fwd", grid=(l_dim // tm,),
        in_specs=[_row_spec(tm, D_MODEL), _row_spec(tm, 2 * D_MODEL), _row_spec(tm, 2 * D_MODEL),
                  _full_spec((1, D_MODEL)), _full_spec((1, D_MODEL))],
        out_specs=_row_spec(tm, D_MODEL),
        out_shape=jax.ShapeDtypeStruct((l_dim, D_MODEL), BF16),
        compiler_params=_params(),
    )(attn, zab, gates, abn, sbn)


def _merge_bwd(attn, zab, gates, abn, sbn, dmerged, tm):
    l_dim = attn.shape[0]

    def body(a_ref, z_ref, g_ref, an_ref, sn_ref, dm_ref, da_ref, dz_ref, dg_ref, dan_ref, dsn_ref):
        @pl.when(pl.program_id(0) == 0)
        def _():
            dan_ref[...] = jnp.zeros_like(dan_ref)
            dsn_ref[...] = jnp.zeros_like(dsn_ref)

        z = z_ref[...].astype(F32)
        g = g_ref[...].astype(F32)
        dm = dm_ref[...].astype(F32)
        attn_v = a_ref[...]
        za, zb = z[:, :D_MODEL], z[:, D_MODEL:]
        sb = _sigmoid(zb)
        ssm = za * sb
        s_ga, s_gs = _sigmoid(g[:, :D_MODEL]), _sigmoid(g[:, D_MODEL:])
        a_n = _rms(attn_v, an_ref[...])
        s_n = _rms(ssm, sn_ref[...])
        dg_ref[:, :D_MODEL] = (dm * a_n * s_ga * (1.0 - s_ga)).astype(BF16)
        dg_ref[:, D_MODEL:] = (dm * s_n * s_gs * (1.0 - s_gs)).astype(BF16)
        dattn, dan = _rms_bwd(attn_v, an_ref[...], dm * s_ga)
        dssm, dsn = _rms_bwd(ssm, sn_ref[...], dm * s_gs)
        da_ref[...] = dattn.astype(BF16)
        dz_ref[:, :D_MODEL] = (dssm * sb).astype(BF16)
        dz_ref[:, D_MODEL:] = (dssm * za * sb * (1.0 - sb)).astype(BF16)
        dan_ref[...] += dan
        dsn_ref[...] += dsn

    return pl.pallas_call(
        body, name="merge_bwd", grid=(l_dim // tm,),
        in_specs=[_row_spec(tm, D_MODEL), _row_spec(tm, 2 * D_MODEL), _row_spec(tm, 2 * D_MODEL),
                  _full_spec((1, D_MODEL)), _full_spec((1, D_MODEL)), _row_spec(tm, D_MODEL)],
        out_specs=[_row_spec(tm, D_MODEL), _row_spec(tm, 2 * D_MODEL), _row_spec(tm, 2 * D_MODEL),
                   _full_spec((1, D_MODEL)), _full_spec((1, D_MODEL))],
        out_shape=[jax.ShapeDtypeStruct((l_dim, D_MODEL), BF16),
                   jax.ShapeDtypeStruct((l_dim, 2 * D_MODEL), BF16),
                   jax.ShapeDtypeStruct((l_dim, IN_COLS), BF16),
                   jax.ShapeDtypeStruct((1, D_MODEL), F32), jax.ShapeDtypeStruct((1, D_MODEL), F32)],
        compiler_params=_params(),
    )(attn, zab, gates, abn, sbn, dmerged)


FF_TILE = D_FF // 2


def _ffn_in_swiglu(hn, wt_ffn_in, tm):
    l_dim = hn.shape[0]
    nt = (((1,), (1,)), ((), ()))

    def body(a_ref, w_ref, gu_ref, act_ref):
        r = lax.dot_general(a_ref[...], w_ref[...], nt, preferred_element_type=F32)
        gate, up = r[:, :FF_TILE], r[:, FF_TILE:]
        gu_ref[...] = r.astype(BF16)
        act_ref[...] = (gate * _sigmoid(gate) * up).astype(BF16)

    return pl.pallas_call(
        body, name="ffn_in_swiglu", grid=(l_dim // tm, 2),
        in_specs=[pl.BlockSpec((tm, D_MODEL), lambda i, j: (i, 0)),
                  pl.BlockSpec((2 * FF_TILE, D_MODEL), lambda i, j: (j, 0))],
        out_specs=[pl.BlockSpec((tm, 2 * FF_TILE), lambda i, j: (i, j)),
                   pl.BlockSpec((tm, FF_TILE), lambda i, j: (i, j))],
        out_shape=[jax.ShapeDtypeStruct((l_dim, 2 * D_FF), BF16), jax.ShapeDtypeStruct((l_dim, D_FF), BF16)],
        compiler_params=_params(dimension_semantics=("parallel", "parallel")),
    )(hn, wt_ffn_in)


def _d_act_swiglu(dh2_b, w_ffn_out, gu, tm):
    l_dim = dh2_b.shape[0]
    nt = (((1,), (1,)), ((), ()))

    def body(d_ref, w_ref, gu_ref, o_ref):
        d = lax.dot_general(d_ref[...], w_ref[...], nt, preferred_element_type=F32)
        gate = gu_ref[:, :FF_TILE].astype(F32)
        up = gu_ref[:, FF_TILE:].astype(F32)
        s = _sigmoid(gate)
        o_ref[:, :FF_TILE] = (d * up * (s * (1.0 + gate * (1.0 - s)))).astype(BF16)
        o_ref[:, FF_TILE:] = (d * (gate * s)).astype(BF16)

    return pl.pallas_call(
        body, name="d_act_swiglu", grid=(l_dim // tm, 2),
        in_specs=[pl.BlockSpec((tm, D_MODEL), lambda i, j: (i, 0)),
                  pl.BlockSpec((FF_TILE, D_MODEL), lambda i, j: (j, 0)),
                  pl.BlockSpec((tm, 2 * FF_TILE), lambda i, j: (i, j))],
        out_specs=pl.BlockSpec((tm, 2 * FF_TILE), lambda i, j: (i, j)),
        out_shape=jax.ShapeDtypeStruct((l_dim, 2 * D_FF), BF16),
        compiler_params=_params(dimension_semantics=("parallel", "parallel")),
    )(dh2_b, w_ffn_out, gu)


def _loss_grad(h2, target2d, tm):
    l_dim = h2.shape[0]
    n_sub = tm // BLK

    def body(*refs):
        h_ref, t_refs = refs[0], refs[1:1 + n_sub]
        d_ref, db_ref, loss_ref = refs[1 + n_sub:]
        i = pl.program_id(0)

        @pl.when(i == 0)
        def _():
            loss_ref[...] = jnp.zeros_like(loss_ref)

        for k in range(n_sub):
            rows = slice(BLK * k, BLK * k + BLK)
            real = jnp.where(n_sub * i + k >= 1, 1.0, 0.0)
            err = (h_ref[rows, :] - t_refs[k][...]) * real
            d = err * (1.0 / D_MODEL)
            d_ref[rows, :] = d
            db_ref[rows, :] = d.astype(BF16)
            loss_ref[...] += 0.5 * jnp.sum(jnp.mean(err * err, axis=-1, keepdims=True), axis=0, keepdims=True)

    return pl.pallas_call(
        body, name="loss_grad", grid=(l_dim // tm,),
        in_specs=[_row_spec(tm, D_MODEL)] + _shifted_specs(n_sub, target2d.shape[0] // BLK),
        out_specs=[_row_spec(tm, D_MODEL), _row_spec(tm, D_MODEL), _full_spec((1, 1))],
        out_shape=[jax.ShapeDtypeStruct((l_dim, D_MODEL), F32), jax.ShapeDtypeStruct((l_dim, D_MODEL), BF16),
                   jax.ShapeDtypeStruct((1, 1), F32)],
        compiler_params=_params(),
    )(h2, *([target2d] * n_sub))


def _norm_bwd_res(name, h, g, dy, dres, tm):
    l_dim = h.shape[0]

    def body(h_ref, g_ref, dy_ref, dr_ref, o_ref, ob_ref, dg_ref):
        @pl.when(pl.program_id(0) == 0)
        def _():
            dg_ref[...] = jnp.zeros_like(dg_ref)

        dx, dg = _rms_bwd(h_ref[...], g_ref[...], dy_ref[...].astype(F32))
        out = dr_ref[...] + dx
        o_ref[...] = out
        ob_ref[...] = out.astype(BF16)
        dg_ref[...] += dg

    return pl.pallas_call(
        body, name=name, grid=(l_dim // tm,),
        in_specs=[_row_spec(tm, D_MODEL), _full_spec((1, D_MODEL)), _row_spec(tm, D_MODEL), _row_spec(tm, D_MODEL)],
        out_specs=[_row_spec(tm, D_MODEL), _row_spec(tm, D_MODEL), _full_spec((1, D_MODEL))],
        out_shape=[jax.ShapeDtypeStruct((l_dim, D_MODEL), F32), jax.ShapeDtypeStruct((l_dim, D_MODEL), BF16),
                   jax.ShapeDtypeStruct((1, D_MODEL), F32)],
        compiler_params=_params(),
    )(h, g, dy, dres)


def _final_bwd(h0, g, dxn, dh1, tm):
    l_dim = h0.shape[0]
    n_sub = tm // BLK
    n_tiles = (l_dim - BLK) // tm

    def sub_specs():
        return [_row_spec(BLK, D_MODEL, (lambda j, k=k: jnp.where(j < n_tiles, n_sub * j + 1 + k, 0)))
                for k in range(n_sub)]

    def body(*refs):
        h_refs, g_ref = refs[:n_sub], refs[n_sub]
        dy_refs, dr_refs = refs[n_sub + 1:2 * n_sub + 1], refs[2 * n_sub + 1:3 * n_sub + 1]
        gx_ref, gm_ref, dg_ref = refs[3 * n_sub + 1:]
        j = pl.program_id(0)

        @pl.when(j == 0)
        def _():
            dg_ref[...] = jnp.zeros_like(dg_ref)

        def block(k):
            dx, dg = _rms_bwd(h_refs[k][...], g_ref[...], dy_refs[k][...].astype(F32))
            dg_ref[...] += dg
            return dr_refs[k][...] + dx

        @pl.when(j < n_tiles)
        def _():
            for k in range(n_sub):
                gx_ref[BLK * k:BLK * k + BLK, :] = block(k)

        @pl.when(j == n_tiles)
        def _():
            gm_ref[...] = block(0)

    return pl.pallas_call(
        body, name="final_bwd", grid=(n_tiles + 1,),
        in_specs=sub_specs() + [_full_spec((1, D_MODEL))] + sub_specs() + sub_specs(),
        out_specs=[_row_spec(tm, D_MODEL, lambda j: jnp.minimum(j, n_tiles - 1)), _full_spec((BLK, D_MODEL)),
                   _full_spec((1, D_MODEL))],
        out_shape=[jax.ShapeDtypeStruct((l_dim - BLK, D_MODEL), F32), jax.ShapeDtypeStruct((BLK, D_MODEL), F32),
                   jax.ShapeDtypeStruct((1, D_MODEL), F32)],
        compiler_params=_params(),
    )(*([h0] * n_sub), g, *([dxn] * n_sub), *([dh1] * n_sub))


def _attn_valid(n):
    shape = (2 * BLK, 3 * BLK)
    qi = lax.broadcasted_iota(jnp.int32, shape, 0) & (BLK - 1)
    col = lax.broadcasted_iota(jnp.int32, shape, 1)
    kj = col & (BLK - 1)
    part = col >> 7
    nn = jnp.zeros(shape, jnp.int32) + n
    meta_ok = (part == 0) & (kj >= PAD) & (nn >= 1)
    prev_ok = (part == 1) & (kj > qi) & (nn >= 2)
    cur_ok = (part == 2) & (kj <= qi) & ((nn >= 1) | (kj >= PAD))
    return meta_ok | prev_ok | cur_ok


def _attn_scores(q_ref, kwin, sk_ref, valid, kv, e):
    qs = jnp.concatenate([q_ref[:, BLK * (2 * kv):BLK * (2 * kv) + BLK],
                          q_ref[:, BLK * (2 * kv + 1):BLK * (2 * kv + 1) + BLK]], axis=0)
    s = lax.dot_general(qs, kwin, (((1,), (1,)), ((), ())), preferred_element_type=F32)
    h0 = 4 * kv + e
    row = lax.broadcasted_iota(jnp.int32, (2 * BLK, 1), 0)
    sink = jnp.where(row < BLK, sk_ref[:, h0:h0 + 1], sk_ref[:, h0 + 2:h0 + 3])
    return qs, jnp.where(valid, s, NEG), sink


def _attn_specs(nb):
    prev = lambda i: jnp.maximum(i - 1, 0)
    zero = lambda i: 0
    kv_specs = [_row_spec(BLK, KVX_W, zero), _row_spec(BLK, KVX_W, prev), _row_spec(BLK, KVX_W)]
    return kv_specs


def _attn_fwd(qn, kf, vf, sinks):
    l_dim = qn.shape[0]
    nb = l_dim // BLK

    def body(q_ref, km_ref, kp_ref, kc_ref, vm_ref, vp_ref, vc_ref, sk_ref, o_ref, lse_ref):
        valid = _attn_valid(pl.program_id(0))
        lane = lax.broadcasted_iota(jnp.int32, (BLK, BLK), 1)
        lse_all = jnp.zeros((BLK, BLK), F32)
        for kv in range(N_KV_HEADS):
            outs = []
            for e in range(2):
                sl = slice(BLK * (2 * kv + e), BLK * (2 * kv + e) + BLK)
                kwin = jnp.concatenate([km_ref[:, sl], kp_ref[:, sl], kc_ref[:, sl]], axis=0)
                vwin = jnp.concatenate([vm_ref[:, sl], vp_ref[:, sl], vc_ref[:, sl]], axis=0)
                _, s, sink = _attn_scores(q_ref, kwin, sk_ref, valid, kv, e)
                m = jnp.maximum(jnp.max(s, axis=-1, keepdims=True), sink)
                ex = jnp.exp(s - m)
                den = jnp.sum(ex, axis=-1, keepdims=True) + jnp.exp(sink - m)
                p = ex * (1.0 / den)
                outs.append(jnp.dot(p.astype(BF16), vwin, preferred_element_type=F32))
                lse = m + jnp.log(den)
                lse_all = jnp.where(lane == 4 * kv + e, lse[:BLK], lse_all)
                lse_all = jnp.where(lane == 4 * kv + 2 + e, lse[BLK:], lse_all)
            o = outs[0] + outs[1]
            o_ref[:, BLK * (2 * kv):BLK * (2 * kv) + BLK] = o[:BLK]
            o_ref[:, BLK * (2 * kv + 1):BLK * (2 * kv + 1) + BLK] = o[BLK:]
        lse_ref[...] = lse_all

    kv_specs = _attn_specs(nb)
    return pl.pallas_call(
        body, name="attn_fwd", grid=(nb,),
        in_specs=[_row_spec(BLK, Q_W)] + kv_specs + kv_specs + [_full_spec((1, N_Q_HEADS))],
        out_specs=[_row_spec(BLK, Q_W), _row_spec(BLK, BLK)],
        out_shape=[jax.ShapeDtypeStruct((l_dim, Q_W), F32), jax.ShapeDtypeStruct((l_dim, BLK), F32)],
        compiler_params=_params(),
    )(qn, kf, kf, kf, vf, vf, vf, sinks)


def _attn_bwd(qn, kf, vf, sinks, lse, attn, dattn):
    l_dim = qn.shape[0]
    nb = l_dim // BLK
    wide = KV_W
    tn = (((0,), (0,)), ((), ()))
    nt = (((1,), (1,)), ((), ()))

    def body(q_ref, km_ref, kp_ref, kc_ref, vm_ref, vp_ref, vc_ref, sk_ref, lse_ref, o_ref, do_ref,
             dq_ref, dkc_ref, dkp_ref, dkm_ref, dvc_ref, dvp_ref, dvm_ref, dsk_ref):
        @pl.when(pl.program_id(0) == 0)
        def _():
            dkm_ref[...] = jnp.zeros_like(dkm_ref)
            dvm_ref[...] = jnp.zeros_like(dvm_ref)
            dsk_ref[...] = jnp.zeros_like(dsk_ref)

        valid = _attn_valid(pl.program_id(0))
        half = _lane_half((BLK, BLK))
        half2 = _lane_half((2 * BLK, BLK))
        half3 = _lane_half((3 * BLK, BLK))
        lane16 = lax.broadcasted_iota(jnp.int32, (1, N_Q_HEADS), 1)
        dsk = jnp.zeros((1, N_Q_HEADS), F32)
        folded_k, folded_v = [], []
        for kv in range(N_KV_HEADS):
            j0, j1 = 2 * kv, 2 * kv + 1
            do0 = do_ref[:, BLK * j0:BLK * j0 + BLK]
            do1 = do_ref[:, BLK * j1:BLK * j1 + BLK]
            do0f, do1f = do0.astype(F32), do1.astype(F32)
            prod0 = do0f * o_ref[:, BLK * j0:BLK * j0 + BLK]
            prod1 = do1f * o_ref[:, BLK * j1:BLK * j1 + BLK]
            dos = jnp.concatenate([do0, do1], axis=0)
            dqs, dks, dvs = [], [], []
            for e in range(2):
                sl = slice(BLK * (2 * kv + e), BLK * (2 * kv + e) + BLK)
                kwin = jnp.concatenate([km_ref[:, sl], kp_ref[:, sl], kc_ref[:, sl]], axis=0)
                vwin = jnp.concatenate([vm_ref[:, sl], vp_ref[:, sl], vc_ref[:, sl]], axis=0)
                qs, s, sink = _attn_scores(q_ref, kwin, sk_ref, valid, kv, e)
                h0 = 4 * kv + e
                lse_rows = jnp.concatenate([lse_ref[:, h0:h0 + 1], lse_ref[:, h0 + 2:h0 + 3]], axis=0)
                p = jnp.exp(s - lse_rows)
                p_sink = jnp.exp(sink - lse_rows)
                delta = jnp.concatenate(
                    [jnp.sum(jnp.where(half == e, prod0, 0.0), axis=-1, keepdims=True),
                     jnp.sum(jnp.where(half == e, prod1, 0.0), axis=-1, keepdims=True)], axis=0)
                dp = lax.dot_general(dos, vwin, nt, preferred_element_type=F32)
                ds = (p * (dp - delta)).astype(BF16)
                pb = p.astype(BF16)
                dqs.append(jnp.dot(ds, kwin, preferred_element_type=F32))
                dks.append(lax.dot_general(ds, qs, tn, preferred_element_type=F32))
                dvs.append(lax.dot_general(pb, dos, tn, preferred_element_type=F32))
                sink_g = -(p_sink * delta)
                g_lo = jnp.sum(sink_g[:BLK], axis=0, keepdims=True)
                g_hi = jnp.sum(sink_g[BLK:], axis=0, keepdims=True)
                dsk = dsk + jnp.where(lane16 == h0, g_lo, 0.0) + jnp.where(lane16 == h0 + 2, g_hi, 0.0)
            dq = jnp.where(half2 == 0, dqs[0], dqs[1])
            dq_ref[:, BLK * j0:BLK * j0 + BLK] = dq[:BLK].astype(BF16)
            dq_ref[:, BLK * j1:BLK * j1 + BLK] = dq[BLK:].astype(BF16)
            own = kv % 2
            folded_k.append(dks[own] + pltpu.roll(dks[1 - own], HEAD_DIM, axis=1))
            folded_v.append(dvs[own] + pltpu.roll(dvs[1 - own], HEAD_DIM, axis=1))
            if own == 1:
                cols = slice(BLK * (kv // 2), BLK * (kv // 2) + BLK)
                for folded, m_ref, p_ref, c_ref in ((folded_k, dkm_ref, dkp_ref, dkc_ref),
                                                    (folded_v, dvm_ref, dvp_ref, dvc_ref)):
                    both = jnp.where(half3 == 0, folded[0], folded[1])
                    m_ref[:, cols] += both[:BLK]
                    p_ref[:, cols] = both[BLK:2 * BLK].astype(BF16)
                    c_ref[:, cols] = both[2 * BLK:].astype(BF16)
                folded_k, folded_v = [], []
        dsk_ref[...] += dsk

    kv_specs = _attn_specs(nb)
    row_wide = _row_spec(BLK, wide)
    acc_wide = _full_spec((BLK, wide))
    big = jax.ShapeDtypeStruct((l_dim, wide), BF16)
    return pl.pallas_call(
        body, name="attn_bwd", grid=(nb,),
        in_specs=[_row_spec(BLK, Q_W)] + kv_specs + kv_specs
        + [_full_spec((1, N_Q_HEADS)), _row_spec(BLK, BLK), _row_spec(BLK, Q_W), _row_spec(BLK, Q_W)],
        out_specs=[_row_spec(BLK, Q_W), row_wide, row_wide, acc_wide, row_wide, row_wide, acc_wide,
                   _full_spec((1, N_Q_HEADS))],
        out_shape=[jax.ShapeDtypeStruct((l_dim, Q_W), BF16), big, big, jax.ShapeDtypeStruct((BLK, wide), F32),
                   big, big, jax.ShapeDtypeStruct((BLK, wide), F32), jax.ShapeDtypeStruct((1, N_Q_HEADS), F32)],
        compiler_params=_params(),
    )(qn, kf, kf, kf, vf, vf, vf, sinks, lse, attn, dattn)


def _qk_bwd(qkv, q_norm_t, k_norm_t, e_mat, dq, dkc, dkp, dkm, dvc, dvp, dvm, dproj):
    l_dim = qkv.shape[0]
    nb = l_dim // BLK
    wide = KV_W

    def body(x_ref, qg_ref, kg_ref, e_ref, dq_ref, dkc_ref, dkp_ref, dkm_ref, dvc_ref, dvp_ref, dvm_ref,
             dproj_ref, o_ref, dqg_ref, dkg_ref):
        i = pl.program_id(0)

        @pl.when(i == 0)
        def _():
            dqg_ref[...] = jnp.zeros_like(dqg_ref)
            dkg_ref[...] = jnp.zeros_like(dkg_ref)

        first = jnp.where(i == 0, 1.0, 0.0)
        not_last = jnp.where(i < nb - 1, 1.0, 0.0)
        dk_x = dkc_ref[...].astype(F32) + not_last * dkp_ref[...].astype(F32) + first * dkm_ref[...]
        dv_x = dvc_ref[...].astype(F32) + not_last * dvp_ref[...].astype(F32) + first * dvm_ref[...]
        x = x_ref[...]
        dqx, dqg = _head_rms_bwd(x[:, :Q_W], qg_ref[...], dq_ref[...].astype(F32) * (HEAD_DIM ** -0.5), e_ref)
        dkx, dkg = _head_rms_bwd(x[:, Q_W:Q_W + KV_W], kg_ref[...], dk_x, e_ref)
        o_ref[:, :Q_W] = dqx.astype(BF16)
        o_ref[:, Q_W:Q_W + KV_W] = dkx.astype(BF16)
        o_ref[:, Q_W + KV_W:] = dv_x.astype(BF16)
        dqg_ref[...] += dqg
        dkg_ref[...] += dkg

    nxt = lambda i: jnp.minimum(i + 1, nb - 1)
    row_wide = _row_spec(BLK, wide)
    nxt_wide = _row_spec(BLK, wide, nxt)
    acc_wide = _full_spec((BLK, wide))
    qkv_w = Q_W + 2 * KV_W
    in_specs = [_row_spec(BLK, qkv_w), _full_spec((1, Q_W)), _full_spec((1, KV_W)),
                _full_spec((256, 256)), _row_spec(BLK, Q_W),
                row_wide, nxt_wide, acc_wide, row_wide, nxt_wide, acc_wide, pl.BlockSpec(memory_space=pl.ANY)]
    return pl.pallas_call(
        body, name="qk_bwd", grid=(nb,),
        in_specs=in_specs,
        out_specs=[pl.BlockSpec((BLK, qkv_w), lambda i: (i, 3 * D_MODEL // qkv_w)),
                   _full_spec((1, Q_W)), _full_spec((1, KV_W))],
        input_output_aliases={len(in_specs) - 1: 0},
        out_shape=[jax.ShapeDtypeStruct(dproj.shape, BF16),
                   jax.ShapeDtypeStruct((1, Q_W), F32), jax.ShapeDtypeStruct((1, KV_W), F32)],
        compiler_params=_params(),
    )(qkv, q_norm_t, k_norm_t, e_mat, dq, dkc, dkp, dkm, dvc, dvp, dvm, dproj)


GRP = 8


def _strided(r, g):
    return pl.ds(r, g, stride=GRP)


def _lane_slab(ref, i, r, g):
    return ref[i, _strided(r, g), :]


def _chunk_carries(xr_ref, xi_ref, i, ar, ai, sqr_ref, sqi_ref, seq_r, seq_i, cin_r, cin_i, sign, reverse):
    g = xr_ref.shape[1] // GRP
    sr = si = None
    for r in (range(GRP - 1, -1, -1) if reverse else range(GRP)):
        xr, xi = _lane_slab(xr_ref, i, r, g), _lane_slab(xi_ref, i, r, g)
        if sr is not None:
            xr, xi = xr + ar * sr - ai * si, xi + ar * si + ai * sr
        sr, si = xr, xi
    row = lax.broadcasted_iota(jnp.int32, sr.shape, 0)
    idx, s = 0, 1
    while s < g:
        br = sqr_ref[idx:idx + 1, BLK * i:BLK * i + BLK]
        bi = sign * sqi_ref[idx:idx + 1, BLK * i:BLK * i + BLK]
        shift, keep = (g - s, row < g - s) if reverse else (s, row >= s)
        pr = jnp.where(keep, pltpu.roll(sr, shift, axis=0), 0.0)
        pi = jnp.where(keep, pltpu.roll(si, shift, axis=0), 0.0)
        sr, si = sr + br * pr - bi * pi, si + br * pi + bi * pr
        idx, s = idx + 1, 2 * s
    return sr + seq_r * cin_r - seq_i * cin_i, si + seq_r * cin_i + seq_i * cin_r


def _ssm_fwd_kb(u, wb_re, wb_im, wc_re, wc_im, d_skip, tabs, q):
    l_dim = u.shape[0]
    nc = l_dim // q
    g = q // GRP

    def body(u_ref, wbr_ref, wbi_ref, wcr_ref, wci_ref, d_ref, a1r_ref, a1i_ref, sqr_ref, sqi_ref,
             seqr_ref, seqi_ref, y_ref, z_ref, sr_ref, si_ref, cr_ref, ci_ref, xr_ref, xi_ref):
        @pl.when(pl.program_id(1) == 0)
        def _():
            cr_ref[...] = jnp.zeros_like(cr_ref)
            ci_ref[...] = jnp.zeros_like(ci_ref)

        u_kb = u_ref[...]
        ub = u_kb.astype(BF16)
        xr = jnp.dot(ub, wbr_ref[0], preferred_element_type=F32)
        xi = jnp.dot(ub, wbi_ref[0], preferred_element_type=F32)
        for i in range(LB_KB):
            xr_ref[i] = xr[:, BLK * i:BLK * i + BLK]
            xi_ref[i] = xi[:, BLK * i:BLK * i + BLK]
        row = lax.broadcasted_iota(jnp.int32, (g, BLK), 0)
        for i in range(LB_KB):
            lanes = slice(BLK * i, BLK * i + BLK)
            ar, ai = a1r_ref[0:1, lanes], a1i_ref[0:1, lanes]
            cin_r, cin_i = cr_ref[0:1, lanes], ci_ref[0:1, lanes]
            tr, ti = _chunk_carries(xr_ref, xi_ref, i, ar, ai, sqr_ref, sqi_ref, seqr_ref[:, lanes],
                                    seqi_ref[:, lanes], cin_r, cin_i, 1.0, reverse=False)
            cr_ref[0:1, lanes] = jnp.sum(jnp.where(row == g - 1, tr, 0.0), axis=0, keepdims=True)
            ci_ref[0:1, lanes] = jnp.sum(jnp.where(row == g - 1, ti, 0.0), axis=0, keepdims=True)
            pr = jnp.where(row == 0, cin_r, pltpu.roll(tr, 1, axis=0))
            pi = jnp.where(row == 0, cin_i, pltpu.roll(ti, 1, axis=0))
            for r in range(GRP):
                pr, pi = (_lane_slab(xr_ref, i, r, g) + ar * pr - ai * pi,
                          _lane_slab(xi_ref, i, r, g) + ar * pi + ai * pr)
                sr_ref[i, _strided(r, g), :] = pr
                si_ref[i, _strided(r, g), :] = pi
        s_r = jnp.concatenate([sr_ref[i] for i in range(LB_KB)], axis=1)
        s_i = jnp.concatenate([si_ref[i] for i in range(LB_KB)], axis=1)
        y = (jnp.dot(s_r.astype(BF16), wcr_ref[0], preferred_element_type=F32)
             - jnp.dot(s_i.astype(BF16), wci_ref[0], preferred_element_type=F32)
             + d_ref[...] * u_kb)
        y_ref[...] = y.astype(BF16)
        z_ref[...] = _gelu(y).astype(BF16)

    chan = pl.BlockSpec((q, BLK), lambda k, c: (c, k))
    wb_spec = pl.BlockSpec((1, BLK, ST_KB), lambda k, c: (k, 0, 0))
    wc_spec = pl.BlockSpec((1, ST_KB, BLK), lambda k, c: (k, 0, 0))
    tab_specs = [pl.BlockSpec((t.shape[0], ST_KB), lambda k, c: (0, k)) for t in tabs[:6]]
    state_spec = pl.BlockSpec((LB_KB, q, BLK), lambda k, c: (k, c, 0))
    state_shape = jax.ShapeDtypeStruct((N_LB, l_dim, BLK), F32)
    return pl.pallas_call(
        body, name="ssm_fwd", grid=(SSM_KB, nc),
        in_specs=[chan, wb_spec, wb_spec, wc_spec, wc_spec, pl.BlockSpec((1, BLK), lambda k, c: (0, k))] + tab_specs,
        out_specs=[chan, chan, state_spec, state_spec],
        out_shape=[jax.ShapeDtypeStruct((l_dim, D_MODEL), BF16), jax.ShapeDtypeStruct((l_dim, D_MODEL), BF16),
                   state_shape, state_shape],
        scratch_shapes=[pltpu.VMEM((8, ST_KB), F32), pltpu.VMEM((8, ST_KB), F32),
                        pltpu.VMEM((LB_KB, q, BLK), F32), pltpu.VMEM((LB_KB, q, BLK), F32)],
        compiler_params=_params(dimension_semantics=("parallel", "arbitrary")),
    )(u, wb_re, wb_im, wc_re, wc_im, d_skip, *tabs[:6])


def _ssm_bwd_kb(dz, y, u, s_re, s_im, wb_re, wb_im, wc_re, wc_im, d_skip, tabs, q, dproj):
    l_dim = u.shape[0]
    nc = l_dim // q
    g = q // GRP

    def body(dz_ref, y_ref, u_ref, sr_ref, si_ref, wbr_ref, wbi_ref, wcr_ref, wci_ref, d_ref,
             a1r_ref, a1i_ref, sqr_ref, sqi_ref, revr_ref, revi_ref, dproj_ref,
             du_ref, dd_ref, dar_ref, dai_ref, dwbr_ref, dwbi_ref, dwcr_ref, dwci_ref,
             cr_ref, ci_ref, gr_ref, gi_ref):
        @pl.when(pl.program_id(1) == 0)
        def _():
            for ref in (cr_ref, ci_ref, dd_ref, dar_ref, dai_ref, dwbr_ref, dwbi_ref, dwcr_ref, dwci_ref):
                ref[...] = jnp.zeros_like(ref)

        tn = (((0,), (0,)), ((), ()))
        nt = (((1,), (1,)), ((), ()))
        u_kb = u_ref[...]
        dy = dz_ref[...].astype(F32) * _gelu_grad(y_ref[...].astype(F32))
        dyb = dy.astype(BF16)
        ub = u_kb.astype(BF16)
        dd_ref[...] += jnp.sum(dy * u_kb, axis=0, keepdims=True)
        ds_r = lax.dot_general(dyb, wcr_ref[0], nt, preferred_element_type=F32)
        ds_i = -lax.dot_general(dyb, wci_ref[0], nt, preferred_element_type=F32)
        for i in range(LB_KB):
            gr_ref[i] = ds_r[:, BLK * i:BLK * i + BLK]
            gi_ref[i] = ds_i[:, BLK * i:BLK * i + BLK]
        row = lax.broadcasted_iota(jnp.int32, (g, BLK), 0)
        for i in range(LB_KB):
            lanes = slice(BLK * i, BLK * i + BLK)
            ar, ai = a1r_ref[0:1, lanes], -a1i_ref[0:1, lanes]
            cin_r, cin_i = cr_ref[0:1, lanes], ci_ref[0:1, lanes]
            tr, ti = _chunk_carries(gr_ref, gi_ref, i, ar, ai, sqr_ref, sqi_ref, revr_ref[:, lanes],
                                    -revi_ref[:, lanes], cin_r, cin_i, -1.0, reverse=True)
            cr_ref[0:1, lanes] = jnp.sum(jnp.where(row == 0, tr, 0.0), axis=0, keepdims=True)
            ci_ref[0:1, lanes] = jnp.sum(jnp.where(row == 0, ti, 0.0), axis=0, keepdims=True)
            nr = jnp.where(row == g - 1, cin_r, pltpu.roll(tr, g - 1, axis=0))
            ni = jnp.where(row == g - 1, cin_i, pltpu.roll(ti, g - 1, axis=0))
            acc_r = jnp.zeros((g, BLK), F32)
            acc_i = jnp.zeros((g, BLK), F32)
            for r in range(GRP - 1, -1, -1):
                s_r, s_i = _lane_slab(sr_ref, i, r, g), _lane_slab(si_ref, i, r, g)
                acc_r = acc_r + (nr * s_r + ni * s_i)
                acc_i = acc_i + (ni * s_r - nr * s_i)
                nr, ni = (_lane_slab(gr_ref, i, r, g) + ar * nr - ai * ni,
                          _lane_slab(gi_ref, i, r, g) + ar * ni + ai * nr)
                gr_ref[i, _strided(r, g), :] = nr
                gi_ref[i, _strided(r, g), :] = ni
            dar_ref[:, lanes] += jnp.sum(acc_r, axis=0, keepdims=True)
            dai_ref[:, lanes] += jnp.sum(acc_i, axis=0, keepdims=True)
        grb = jnp.concatenate([gr_ref[i] for i in range(LB_KB)], axis=1).astype(BF16)
        gib = jnp.concatenate([gi_ref[i] for i in range(LB_KB)], axis=1).astype(BF16)
        srb = jnp.concatenate([sr_ref[i] for i in range(LB_KB)], axis=1).astype(BF16)
        sib = jnp.concatenate([si_ref[i] for i in range(LB_KB)], axis=1).astype(BF16)
        du = (lax.dot_general(grb, wbr_ref[0], nt, preferred_element_type=F32)
              + lax.dot_general(gib, wbi_ref[0], nt, preferred_element_type=F32)
              + d_ref[...] * dy)
        du_ref[...] = du.astype(BF16)
        dwbr_ref[0] += lax.dot_general(ub, grb, tn, preferred_element_type=F32)
        dwbi_ref[0] += lax.dot_general(ub, gib, tn, preferred_element_type=F32)
        dwcr_ref[0] += lax.dot_general(srb, dyb, tn, preferred_element_type=F32)
        dwci_ref[0] -= lax.dot_general(sib, dyb, tn, preferred_element_type=F32)

    chan = pl.BlockSpec((q, BLK), lambda k, c: (nc - 1 - c, k))
    wb_spec = pl.BlockSpec((1, BLK, ST_KB), lambda k, c: (k, 0, 0))
    wc_spec = pl.BlockSpec((1, ST_KB, BLK), lambda k, c: (k, 0, 0))
    tab_in = [tabs[0], tabs[1], tabs[2], tabs[3], tabs[6], tabs[7]]
    tab_specs = [pl.BlockSpec((t.shape[0], ST_KB), lambda k, c: (0, k)) for t in tab_in]
    vec = pl.BlockSpec((1, BLK), lambda k, c: (0, k))
    svec = pl.BlockSpec((1, ST_KB), lambda k, c: (0, k))
    state_spec = pl.BlockSpec((LB_KB, q, BLK), lambda k, c: (k, nc - 1 - c, 0))
    du_spec = pl.BlockSpec((q, BLK), lambda k, c: (nc - 1 - c, 2 * D_MODEL // BLK + k))
    in_specs = ([chan, chan, chan, state_spec, state_spec, wb_spec, wb_spec, wc_spec, wc_spec, vec] + tab_specs
                + [pl.BlockSpec(memory_space=pl.ANY)])
    return pl.pallas_call(
        body, name="ssm_bwd", grid=(SSM_KB, nc),
        in_specs=in_specs,
        out_specs=[du_spec, vec, svec, svec, wb_spec, wb_spec, wc_spec, wc_spec],
        input_output_aliases={len(in_specs) - 1: 0},
        out_shape=[jax.ShapeDtypeStruct(dproj.shape, BF16), jax.ShapeDtypeStruct((1, D_MODEL), F32),
                   jax.ShapeDtypeStruct((1, N_STATE), F32), jax.ShapeDtypeStruct((1, N_STATE), F32),
                   jax.ShapeDtypeStruct((SSM_KB, BLK, ST_KB), F32), jax.ShapeDtypeStruct((SSM_KB, BLK, ST_KB), F32),
                   jax.ShapeDtypeStruct((SSM_KB, ST_KB, BLK), F32), jax.ShapeDtypeStruct((SSM_KB, ST_KB, BLK), F32)],
        scratch_shapes=[pltpu.VMEM((8, ST_KB), F32), pltpu.VMEM((8, ST_KB), F32),
                        pltpu.VMEM((LB_KB, q, BLK), F32), pltpu.VMEM((LB_KB, q, BLK), F32)],
        compiler_params=_params(dimension_semantics=("parallel", "arbitrary")),
    )(dz, y, u, s_re, s_im, wb_re, wb_im, wc_re, wc_im, d_skip, *tab_in, dproj)


def _discretize(lam_re, lam_im, log_dt, b_re, b_im):
    dt = jnp.exp(log_dt)[:, None]
    mag = jnp.exp(lam_re * dt)
    ar, ai = mag * jnp.cos(lam_im * dt), mag * jnp.sin(lam_im * dt)
    den = lam_re * lam_re + lam_im * lam_im
    nr, ni = ar - 1.0, ai
    fr, fi = (nr * lam_re + ni * lam_im) / den, (ni * lam_re - nr * lam_im) / den
    bbar_re = fr[..., None] * b_re - fi[..., None] * b_im
    bbar_im = fr[..., None] * b_im + fi[..., None] * b_re
    return ar, ai, bbar_re, bbar_im


def _block_diag_b(bbar):
    eye = jnp.eye(8, dtype=bbar.dtype)
    return jnp.einsum("kgpc,gh->kgchp", bbar.reshape(8, 8, SSM_STATE, SSM_GROUP_CH), eye).reshape(8, BLK, ST_KB)


def _block_diag_b_t(dwb):
    eye = jnp.eye(8, dtype=dwb.dtype)
    return jnp.einsum("kgchp,gh->kgpc", dwb.reshape(8, 8, SSM_GROUP_CH, 8, SSM_STATE), eye).reshape(
        SSM_GROUPS, SSM_STATE, SSM_GROUP_CH)


def _block_diag_c(c):
    eye = jnp.eye(8, dtype=c.dtype)
    return jnp.einsum("kgcp,gh->kgphc", c.reshape(8, 8, SSM_GROUP_CH, SSM_STATE), eye).reshape(8, ST_KB, BLK)


def _block_diag_c_t(dwc):
    eye = jnp.eye(8, dtype=dwc.dtype)
    return jnp.einsum("kgphc,gh->kgcp", dwc.reshape(8, 8, SSM_STATE, 8, SSM_GROUP_CH), eye).reshape(
        SSM_GROUPS, SSM_GROUP_CH, SSM_STATE)


def _powers(br, bi, n):
    pr, pi = br, bi
    cr, ci = br, bi
    while pr.shape[0] < n:
        pr, pi = (jnp.concatenate([pr, pr * cr - pi * ci], axis=0),
                  jnp.concatenate([pi, pr * ci + pi * cr], axis=0))
        cr, ci = cr * cr - ci * ci, 2.0 * cr * ci
    return pr[:n], pi[:n]


def _powers_desc(br, bi, n):
    pr, pi = br, bi
    cr, ci = br, bi
    while pr.shape[0] < n:
        pr, pi = (jnp.concatenate([pr * cr - pi * ci, pr], axis=0),
                  jnp.concatenate([pr * ci + pi * cr, pi], axis=0))
        cr, ci = cr * cr - ci * ci, 2.0 * cr * ci
    return pr, pi


def _power_tables(ar, ai, g):
    a1r, a1i = _powers(ar, ai, GRP)
    seqr, seqi = _powers(a1r[GRP - 1:], a1i[GRP - 1:], g)
    g2 = 1 << (g - 1).bit_length()
    revr, revi = _powers_desc(a1r[GRP - 1:], a1i[GRP - 1:], g2)
    revr, revi = revr[g2 - g:], revi[g2 - g:]
    sq_r, sq_i = [seqr[0:1]], [seqi[0:1]]
    while len(sq_r) < 8:
        r, i = sq_r[-1], sq_i[-1]
        sq_r.append(r * r - i * i)
        sq_i.append(2.0 * r * i)
    sqr, sqi = jnp.concatenate(sq_r, axis=0), jnp.concatenate(sq_i, axis=0)
    return a1r, a1i, sqr, sqi, seqr, seqi, revr, revi


HBM_SPEC = pl.BlockSpec(memory_space=pltpu.HBM)
SEM_SPEC = pl.BlockSpec(memory_space=pltpu.SEMAPHORE)
DATAFLOW = pltpu.SideEffectType.DATAFLOW_SIDE_EFFECTING


def _plain_rows(p, m):
    return p * m


def _ffn_in_rows(p, m):
    return ((p & 3) >> 1) * (4 * m) + (p >> 2) * (2 * m) + (p & 1) * m


def _peer_copies(src_refs, land_refs, send_sems, recv_sems, chunked, row_fns):
    x, y, c = lax.axis_index("x"), lax.axis_index("y"), lax.axis_index("c")
    me = 4 * x + 2 * y + c
    copies = []
    for a, (src, land) in enumerate(zip(src_refs, land_refs)):
        m = land.shape[0] // N_DEV
        for k in range(N_DEV - 1):
            rel = k + 1
            bx, by, bc = (rel >> 2) & 1, (rel >> 1) & 1, rel & 1
            peer = (x + bx - 2 * x * bx, y + by - 2 * y * by, c + bc - 2 * c * bc)
            p_idx = 4 * peer[0] + 2 * peer[1] + peer[2]
            copies.append(pltpu.make_async_remote_copy(
                src_ref=src.at[pl.ds(row_fns[a](p_idx, m), m), :] if chunked else src,
                dst_ref=land.at[pl.ds(me * m if chunked else row_fns[a](me, m), m), :],
                send_sem=send_sems[a * (N_DEV - 1) + k], recv_sem=recv_sems[a * (N_DEV - 1) + k],
                device_id=peer, device_id_type=MESH))
    return copies


def _own_copies(src_refs, land_outs, sems, chunked, row_fns):
    me = _my_index()
    copies = []
    for a, (src, land) in enumerate(zip(src_refs, land_outs)):
        m = land.shape[0] // N_DEV
        rows = row_fns[a](me, m)
        copies.append(pltpu.make_async_copy(src.at[pl.ds(rows, m), :] if chunked else src,
                                            land.at[pl.ds(me * m if chunked else rows, m), :], sems[a]))
    return copies


def _send_start(name, srcs, lands, chunked, row_fns=None):
    n = len(srcs)
    ns = n * (N_DEV - 1)
    row_fns = row_fns or [_plain_rows] * n

    def body(*refs):
        src_refs, land_refs = refs[:n], refs[n:2 * n]
        send_sems, own_sems = refs[2 * n:2 * n + ns], refs[2 * n + ns:2 * n + ns + n]
        recv_sems = refs[2 * n + ns + n:2 * n + 2 * ns + n]
        land_outs, token = refs[-1 - n:-1], refs[-1]
        for cp in _own_copies(src_refs, land_outs, own_sems, chunked, row_fns):
            cp.start()
        for cp in _peer_copies(src_refs, land_refs, send_sems, recv_sems, chunked, row_fns):
            cp.start()
        token[...] = jnp.zeros_like(token)

    ins = [pltpu.with_memory_space_constraint(a, pltpu.HBM) for a in list(srcs) + list(lands)]
    n_sems = 2 * ns + n
    out = pl.pallas_call(
        body, name=name,
        in_specs=[HBM_SPEC] * (2 * n),
        out_specs=[SEM_SPEC] * n_sems + [HBM_SPEC] * (2 * n) + [pl.BlockSpec(memory_space=pltpu.VMEM)],
        out_shape=[pltpu.SemaphoreType.DMA(())] * n_sems
        + [pltpu.HBM(a.shape, a.dtype) for a in list(srcs) + list(lands)]
        + [jax.ShapeDtypeStruct((8, BLK), F32)],
        input_output_aliases={i: i + n_sems for i in range(2 * n)},
        compiler_params=pltpu.CompilerParams(has_side_effects=DATAFLOW),
    )(*ins)
    return (out[:ns + n], out[ns + n:n_sems], out[n_sems:n_sems + n], out[n_sems + n:n_sems + 2 * n], out[-1])


def _send_wait(name, send_sems, recv_sems, srcs, lands, after, chunked, row_fns=None):
    n = len(srcs)
    ns = n * (N_DEV - 1)
    row_fns = row_fns or [_plain_rows] * n

    def body(*refs):
        src_refs, land_refs = refs[:n], refs[n:2 * n]
        s_sems, own_sems = refs[2 * n:2 * n + ns], refs[2 * n + ns:2 * n + ns + n]
        r_sems = refs[2 * n + ns + n:2 * n + 2 * ns + n]
        land_outs = refs[-n:]
        for cp in _own_copies(src_refs, land_outs, own_sems, chunked, row_fns):
            cp.wait()
        copies = _peer_copies(src_refs, land_refs, s_sems, r_sems, chunked, row_fns)
        for cp in copies:
            cp.wait_send()
        for cp in copies:
            cp.wait_recv()

    out = pl.pallas_call(
        body, name=name,
        in_specs=[HBM_SPEC] * (2 * n) + [SEM_SPEC] * (2 * ns + n) + [pl.BlockSpec(memory_space=pl.ANY)],
        out_specs=[HBM_SPEC] * (2 * n),
        out_shape=[pltpu.HBM(a.shape, a.dtype) for a in list(srcs) + list(lands)],
        input_output_aliases={i: i for i in range(2 * n)},
        compiler_params=pltpu.CompilerParams(has_side_effects=DATAFLOW),
    )(*srcs, *lands, *send_sems, *recv_sems, after)
    return out[n:]


def _sum_slots(name, recv, own):
    m, ncol = own.shape
    tr = m // 2 if (m // 2) % 16 == 0 else m
    g = m // tr

    def body(*refs):
        slots, own_ref, o_ref = refs[:N_DEV], refs[N_DEV], refs[N_DEV + 1]
        me = _my_index()
        tot = None
        for s in range(N_DEV):
            v = jnp.where(me == s, own_ref[...], slots[s][...].astype(F32))
            tot = v if tot is None else tot + v
        o_ref[...] = tot

    def slot_spec(s):
        return pl.BlockSpec((tr, ncol), lambda i: (s * g + i, 0))

    return pl.pallas_call(
        body, name=name, grid=(g,),
        in_specs=[slot_spec(s) for s in range(N_DEV)] + [pl.BlockSpec((tr, ncol), lambda i: (i, 0))],
        out_specs=pl.BlockSpec((tr, ncol), lambda i: (i, 0)),
        out_shape=jax.ShapeDtypeStruct((m, ncol), F32),
        compiler_params=_params(),
    )(*([recv] * N_DEV), own)


def _sum_gathered(name, gathered, rows):
    tr = _pick(rows, 512, 8)
    g = rows // tr

    def body(*refs):
        o_ref = refs[N_DEV]
        tot = refs[0][...]
        for s in range(1, N_DEV):
            tot = tot + refs[s][...]
        o_ref[...] = tot

    return pl.pallas_call(
        body, name=name, grid=(g,),
        in_specs=[pl.BlockSpec((tr, BLK), (lambda i, s=s: (s * g + i, 0))) for s in range(N_DEV)],
        out_specs=pl.BlockSpec((tr, BLK), lambda i: (i, 0)),
        out_shape=jax.ShapeDtypeStruct((rows, BLK), F32),
        compiler_params=_params(),
    )(*([gathered] * N_DEV))


def _adamw(name, w, g, m, v):
    r, c = w.shape
    tr = _pick(r, 256, 8) if r % 8 == 0 else r
    c1 = 1.0 - ADAM_B1 ** ADAM_STEP
    c2 = 1.0 - ADAM_B2 ** ADAM_STEP

    def body(w_ref, g_ref, m_ref, v_ref, d_ref, nm_ref, nv_ref):
        gv = g_ref[...]
        nm = ADAM_B1 * m_ref[...] + (1.0 - ADAM_B1) * gv
        nv = ADAM_B2 * v_ref[...] + (1.0 - ADAM_B2) * (gv * gv)
        m_hat = nm / c1
        v_hat = nv / c2
        d_ref[...] = -ADAM_LR * (m_hat / (jnp.sqrt(v_hat) + ADAM_EPS) + ADAM_WD * w_ref[...])
        nm_ref[...] = nm
        nv_ref[...] = nv

    spec = pl.BlockSpec((tr, c), lambda i: (i, 0))
    shape = jax.ShapeDtypeStruct((r, c), F32)
    return pl.pallas_call(
        body, name=name, grid=(r // tr,),
        in_specs=[spec] * 4, out_specs=[spec] * 3, out_shape=[shape] * 3,
        compiler_params=_params(),
    )(w, g, m, v)


def _adamw_many(name, ws, gs, ms, vs):
    n = len(ws)
    c1 = 1.0 - ADAM_B1 ** ADAM_STEP
    c2 = 1.0 - ADAM_B2 ** ADAM_STEP

    def body(*refs):
        for a in range(n):
            w_ref, g_ref, m_ref, v_ref = refs[a], refs[n + a], refs[2 * n + a], refs[3 * n + a]
            d_ref, nm_ref, nv_ref = refs[4 * n + a], refs[5 * n + a], refs[6 * n + a]
            gv = g_ref[...]
            nm = ADAM_B1 * m_ref[...] + (1.0 - ADAM_B1) * gv
            nv = ADAM_B2 * v_ref[...] + (1.0 - ADAM_B2) * (gv * gv)
            d_ref[...] = -ADAM_LR * ((nm / c1) / (jnp.sqrt(nv / c2) + ADAM_EPS) + ADAM_WD * w_ref[...])
            nm_ref[...] = nm
            nv_ref[...] = nv

    specs = [_full_spec(w.shape) for w in ws]
    shapes = [jax.ShapeDtypeStruct(w.shape, F32) for w in ws]
    out = pl.pallas_call(
        body, name=name, grid=(1,),
        in_specs=specs * 4, out_specs=specs * 3, out_shape=shapes * 3,
        compiler_params=_params(),
    )(*ws, *gs, *ms, *vs)
    return out[:n], out[n:2 * n], out[2 * n:]


PACK_ROWS = 128


def _pack(parts):
    flat = []
    for p in parts:
        v = p.reshape(-1)
        flat.append(jnp.pad(v, (0, (-v.shape[0]) % BLK)))
    v = jnp.concatenate(flat)
    v = jnp.pad(v, (0, (-v.shape[0]) % (PACK_ROWS * BLK)))
    return v.reshape(-1, BLK)


def _unpack(buf, shapes):
    flat = buf.reshape(-1)
    out, off = [], 0
    for shp in shapes:
        size = math.prod(shp)
        out.append(flat[off:off + size].reshape(shp))
        off += size + (-size) % BLK
    return out


def kernel(x, meta_tokens, norm_mix, w_in, q_norm, k_norm, attn_sinks, lam_re, lam_im, log_dt, ssm_b_re, ssm_b_im, ssm_c_re, ssm_c_im, ssm_d, w_glu, attn_branch_norm, ssm_branch_norm, w_out, norm_ffn, w_ffn_in, w_ffn_out, loss_target, m_meta_tokens, m_norm_mix, m_w_in, m_q_norm, m_k_norm, m_attn_sinks, m_lam_re, m_lam_im, m_log_dt, m_ssm_b_re, m_ssm_b_im, m_ssm_c_re, m_ssm_c_im, m_ssm_d, m_w_glu, m_attn_branch_norm, m_ssm_branch_norm, m_w_out, m_norm_ffn, m_w_ffn_in, m_w_ffn_out, v_meta_tokens, v_norm_mix, v_w_in, v_q_norm, v_k_norm, v_attn_sinks, v_lam_re, v_lam_im, v_log_dt, v_ssm_b_re, v_ssm_b_im, v_ssm_c_re, v_ssm_c_im, v_ssm_d, v_w_glu, v_attn_branch_norm, v_ssm_branch_norm, v_w_out, v_norm_ffn, v_w_ffn_in, v_w_ffn_out):
    args = dict(locals())
    weights = {n: args[n] for n in WEIGHTS}
    mom_m = {n: args["m_" + n] for n in WEIGHTS}
    mom_v = {n: args["v_" + n] for n in WEIGHTS}

    x2d = x[0]
    target2d = loss_target[0]
    s_len = x2d.shape[0]
    l_dim = s_len + BLK
    tm_row = _pick(l_dim, 320)
    tm_mm = _pick(l_dim, 1040)
    tl_tn = _pick(l_dim, 2080)
    tm_ffn = _pick(l_dim, 640)
    tm_big = _pick(l_dim, 2080)
    tm_shift = _pick(l_dim, 640, BLK)

    shard_in = w_in[0].T.astype(BF16)
    shard_glu = w_glu[0].T.astype(BF16)
    shard_out = w_out[0].astype(BF16)
    shard_ffn_in = w_ffn_in[0].T.astype(BF16)
    shard_ffn_out = w_ffn_out[0].astype(BF16)
    shard_meta = meta_tokens.T
    me = _my_index()

    def landing(shard):
        return lax.empty((N_DEV * shard.shape[0], shard.shape[1]), shard.dtype)

    first = [shard_in, shard_meta]
    ga = _send_start("gather_start_a", first, [landing(s) for s in first], chunked=False)
    later = [shard_glu + ga[4][0:1, 0:1].astype(BF16), shard_out, shard_ffn_in, shard_ffn_out]
    later_fns = [_plain_rows, _plain_rows, _ffn_in_rows, _plain_rows]
    gb = _send_start("gather_start_b", later, [landing(s) for s in later], chunked=False,
                     row_fns=later_fns)

    nm_t = norm_mix + (ga[4][0:1, 0:1] + gb[4][0:1, 0:1])
    qn_t, kn_t = jnp.tile(q_norm, (1, N_Q_HEADS)), jnp.tile(k_norm, (1, N_KV_HEADS))
    e_mat = jnp.kron(jnp.eye(4, dtype=F32), jnp.ones((HEAD_DIM, HEAD_DIM), F32)).astype(BF16)

    def disc(lr, li, ldt, br, bi):
        return _discretize(lr[0], li[0], ldt[0], br[0], bi[0])

    (abar_re, abar_im, bbar_re, bbar_im), disc_vjp = jax.vjp(disc, lam_re, lam_im, log_dt, ssm_b_re, ssm_b_im)
    wb_re, wb_im = _block_diag_b(bbar_re).astype(BF16), _block_diag_b(bbar_im).astype(BF16)
    wc_re, wc_im = _block_diag_c(ssm_c_re[0]).astype(BF16), _block_diag_c(ssm_c_im[0]).astype(BF16)
    q_ssm = _pick(l_dim, 640, 64)
    tabs = _power_tables(abar_re.reshape(1, N_STATE), abar_im.reshape(1, N_STATE), q_ssm // GRP)

    h0, xn = _embed_norm(x2d, nm_t, tm_shift)
    wt_in, meta_t = _send_wait("gather_wait_a", ga[0], ga[1], ga[2], ga[3], xn, chunked=False)
    meta_pad = jnp.pad(meta_t.T, ((PAD, 0), (0, 0)))
    h0, xn = _embed_meta(meta_pad, nm_t, h0, xn)
    qkv_w, u_end = Q_W + 2 * KV_W, Q_W + 2 * KV_W + D_MODEL
    wt_in_p = jnp.concatenate([wt_in[u_end:], wt_in[qkv_w:u_end], wt_in[:qkv_w]], axis=0)
    qkv = _matmul("proj_qkv", xn, wt_in_p, nt=True, tm=tm_big, tn=qkv_w, tk=D_MODEL, n=qkv_w, w_off=2)
    u = _matmul("proj_u", xn, wt_in_p, nt=True, tm=tm_big, tn=D_MODEL, tk=D_MODEL, n=D_MODEL, w_off=2)
    gates = _matmul("proj_gates", xn, wt_in_p, nt=True, tm=tm_big, tn=D_MODEL, tk=D_MODEL, n=2 * D_MODEL, w_off=0,
                    out_dtype=BF16)
    qn, kf, vf = _qk_prep(qkv, qn_t, kn_t, e_mat, tm_row)
    attn, lse = _attn_fwd(qn, kf, vf, attn_sinks)
    y, z, s_re, s_im = _ssm_fwd_kb(u, wb_re, wb_im, wc_re, wc_im, ssm_d, tabs, q_ssm)
    wt_glu, w_out_f, wt_ffn_in, w_ffn_out_f = _send_wait("gather_wait_b", gb[0], gb[1], gb[2], gb[3], z,
                                                         chunked=False, row_fns=later_fns)
    zab = _matmul("glu_proj", z, wt_glu, nt=True, tm=tm_big, tn=1024, tk=D_MODEL, out_dtype=BF16)
    merged = _merge_fwd(attn, zab, gates, attn_branch_norm, ssm_branch_norm, tm_row)
    h1, hn = _matmul("out_proj", merged, w_out_f, nt=False, tm=tm_mm, tn=1024, tk=D_MODEL, res=h0,
                     norm_g=norm_ffn)
    gu, act = _ffn_in_swiglu(hn, wt_ffn_in, tm_ffn)
    h2 = _matmul("ffn_out", act, w_ffn_out_f, nt=False, tm=tm_mm, tn=1024, tk=D_FF, res=h1)
    dh2, dh2_b, loss_part = _loss_grad(h2, target2d, tm_shift)

    dgu = _d_act_swiglu(dh2_b, w_ffn_out_f, gu, tm_ffn)

    def exchange_start(name, grads_b, row_fns=None):
        return _send_start(name, grads_b, [lax.empty(g.shape, BF16) for g in grads_b], chunked=True,
                           row_fns=row_fns)

    g_ffn_out, g_ffn_out_b = _matmul_tn("g_ffn_out", act, dh2_b, tm=1408, tn=1024, tl=tl_tn)
    g_ffn_in_t, g_ffn_in_b = _matmul_tn("g_ffn_in", dgu, hn, tm=1408, tn=1024, tl=tl_tn)
    ffn_fns = [_ffn_in_rows, _plain_rows]
    ex1 = exchange_start("exchange_start_ffn", [g_ffn_in_b, g_ffn_out_b], ffn_fns)
    dhn = _matmul("d_hn", dgu, wt_ffn_in, nt=False, tm=tm_ffn, tn=1024, tk=2 * D_FF, out_dtype=BF16)
    dh1, dh1_b, g_norm_ffn = _norm_bwd_res("ffn_norm_bwd", h1, norm_ffn + ex1[4][0:1, 0:1], dhn, dh2, tm_row)
    dmerged = _matmul("d_merged", dh1_b, w_out_f, nt=True, tm=tm_big, tn=1024, tk=D_MODEL, out_dtype=BF16)
    dattn, dzab, dproj, g_abn, g_sbn = _merge_bwd(attn, zab, gates, attn_branch_norm, ssm_branch_norm,
                                                    dmerged, tm_row)
    g_out, g_out_b = _matmul_tn("g_out", merged, dh1_b, tm=1024, tn=1024, tl=tl_tn)
    dz = _matmul("d_z", dzab, wt_glu, nt=False, tm=tm_mm, tn=1024, tk=2 * D_MODEL, out_dtype=BF16)
    g_glu_t, g_glu_b = _matmul_tn("g_glu", dzab, z, tm=1024, tn=1024, tl=tl_tn)
    ex2 = exchange_start("exchange_start_mix", [g_glu_b, g_out_b])
    dproj, g_ssm_d, g_ar, g_ai, g_wbr, g_wbi, g_wcr, g_wci = _ssm_bwd_kb(
        dz, y, u, s_re, s_im, wb_re, wb_im, wc_re, wc_im, ssm_d + ex2[4][0:1, 0:1], tabs, q_ssm, dproj)
    dq, dkc, dkp, dkm, dvc, dvp, dvm, g_sinks = _attn_bwd(qn, kf, vf, attn_sinks, lse, attn, dattn)
    dproj, g_qn_t, g_kn_t = _qk_bwd(qkv, qn_t, kn_t, e_mat, dq, dkc, dkp, dkm, dvc, dvp, dvm, dproj)
    g_lam_re, g_lam_im, g_log_dt, g_b_re, g_b_im = disc_vjp(
        (g_ar.reshape(SSM_GROUPS, SSM_STATE), g_ai.reshape(SSM_GROUPS, SSM_STATE),
         _block_diag_b_t(g_wbr), _block_diag_b_t(g_wbi)))
    small_grads = {
        "q_norm": g_qn_t.reshape(N_Q_HEADS, HEAD_DIM).sum(0)[None],
        "k_norm": g_kn_t.reshape(N_KV_HEADS, HEAD_DIM).sum(0)[None], "attn_sinks": g_sinks,
        "lam_re": g_lam_re, "lam_im": g_lam_im, "log_dt": g_log_dt, "ssm_b_re": g_b_re, "ssm_b_im": g_b_im,
        "ssm_c_re": _block_diag_c_t(g_wcr)[None], "ssm_c_im": _block_diag_c_t(g_wci)[None],
        "ssm_d": g_ssm_d, "attn_branch_norm": g_abn, "ssm_branch_norm": g_sbn, "norm_ffn": g_norm_ffn,
    }
    early = [n for n in SMALL if n != "norm_mix"]
    packed_e = _pack([small_grads[n] for n in early])
    gs_e = _send_start("small_start_a", [packed_e], [landing(packed_e)], chunked=False)

    blocks_g, blocks_u, blocks_q = 2 * D_MODEL // 512, D_MODEL // 512, qkv_w // 512
    back = lambda i: jnp.where(i < blocks_g, i + blocks_q + blocks_u,
                               jnp.where(i < blocks_g + blocks_u, i - blocks_g + blocks_q, i - blocks_g - blocks_u))
    g_in_t, g_in_b = _matmul_tn("g_in", dproj, xn, tm=512, tn=1024, tl=tl_tn, after=gs_e[4], out_rows=back)
    ex3 = exchange_start("exchange_start_in", [g_in_b])
    dxn = _matmul("d_xn", dproj, wt_in_p, nt=False, tm=tm_ffn, tn=1024, tk=IN_COLS, out_dtype=BF16)
    grad_x2d, dmeta_blk, g_norm_mix = _final_bwd(h0, nm_t + ex3[4][0:1, 0:1], dxn, dh1, _pick(s_len, 512, BLK))
    packed_l = _pack([g_norm_mix, dmeta_blk[PAD:], loss_part])
    gs_l = _send_start("small_start_b", [packed_l], [landing(packed_l)], chunked=False)
    grads, deltas, new_m, new_v = {}, {}, {}, {}

    recv_ffn_in, recv_ffn_out = _send_wait("exchange_wait_ffn", ex1[0], ex1[1], ex1[2], ex1[3], gs_l[4],
                                           chunked=True, row_fns=ffn_fns)
    recv_glu, recv_out = _send_wait("exchange_wait_mix", ex2[0], ex2[1], ex2[2], ex2[3], recv_ffn_in,
                                    chunked=True)
    (recv_in,) = _send_wait("exchange_wait_in", ex3[0], ex3[1], ex3[2], ex3[3], recv_glu, chunked=True)
    big = [("w_in", g_in_t, True, recv_in, _plain_rows), ("w_glu", g_glu_t, True, recv_glu, _plain_rows),
           ("w_out", g_out, False, recv_out, _plain_rows), ("w_ffn_in", g_ffn_in_t, True, recv_ffn_in, _ffn_in_rows),
           ("w_ffn_out", g_ffn_out, False, recv_ffn_out, _plain_rows)]
    for name, g_full, transposed, recv, row_fn in big:
        m_rows = g_full.shape[0] // N_DEV
        own = lax.dynamic_slice(g_full, (row_fn(me, m_rows), 0), (m_rows, g_full.shape[1]))
        g_shard = _sum_slots("sum_" + name, recv, own)
        grads[name] = (g_shard.T if transposed else g_shard)[None]

    def adamw_2d(name):
        shp = weights[name].shape
        if name in ("w_in", "w_ffn_in"):
            as2d, back = (lambda a: a.reshape(shp[-2], shp[-1]).T), (lambda a: a.T.reshape(shp))
        else:
            as2d, back = (lambda a: a.reshape(shp[-2], shp[-1])), (lambda a: a.reshape(shp))
        d, nm, nv = _adamw("adamw_" + name, as2d(weights[name]), as2d(grads[name]), as2d(mom_m[name]),
                           as2d(mom_v[name]))
        deltas[name], new_m[name], new_v[name] = back(d), back(nm), back(nv)
        return d

    for name in ["w_in", "w_glu", "w_out", "w_ffn_in", "w_ffn_out"]:
        last = adamw_2d(name)

    def small_sum(tag, gs, packed, after):
        (gathered,) = _send_wait("small_wait_" + tag, gs[0], gs[1], gs[2], gs[3], after, chunked=False)
        return _sum_gathered("sum_small_" + tag, gathered, packed.shape[0])

    def small_adamw(tag, names):
        view = lambda n, a: jnp.swapaxes(a, -1, -2) if n.startswith("ssm_b") else a
        d, nm, nv = _adamw_many("adamw_small_" + tag, [view(n, weights[n]) for n in names],
                                [view(n, grads[n]) for n in names], [view(n, mom_m[n]) for n in names],
                                [view(n, mom_v[n]) for n in names])
        deltas.update((n, view(n, a)) for n, a in zip(names, d))
        new_m.update((n, view(n, a)) for n, a in zip(names, nm))
        new_v.update((n, view(n, a)) for n, a in zip(names, nv))

    g_sum_e = small_sum("a", gs_e, packed_e, last)
    grads.update(zip(early, _unpack(g_sum_e, [weights[n].shape for n in early])))
    wide = [n for n in early if n.startswith(("ssm_b", "ssm_c"))]
    small_adamw("wide", wide)
    g_sum_l = small_sum("b", gs_l, packed_l, g_sum_e)
    grads["norm_mix"], g_meta, loss_sum = _unpack(g_sum_l, [weights["norm_mix"].shape, (N_META, D_MODEL), (1, 1)])
    small_adamw("rest", [n for n in SMALL if n not in wide])
    grads["meta_tokens"] = lax.dynamic_slice(g_meta, (0, me * BLK), (N_META, BLK))
    adamw_2d("meta_tokens")

    loss = loss_sum[0, 0]
    return (loss, grad_x2d[None], *[grads[n] for n in WEIGHTS], *[deltas[n] for n in WEIGHTS],
            *[new_m[n] for n in WEIGHTS], *[new_v[n] for n in WEIGHTS])
```

```python
import math

import jax
import jax.numpy as jnp
from jax import lax
from jax.experimental import pallas as pl
from jax.experimental.pallas import tpu as pltpu

F32 = jnp.float32
BF16 = jnp.bfloat16

D_MODEL = 1024
N_META = 16
HEAD_DIM = 64
N_Q_HEADS = 16
N_KV_HEADS = 4
Q_W = N_Q_HEADS * HEAD_DIM
KV_W = N_KV_HEADS * HEAD_DIM
SSM_GROUPS = 64
SSM_GROUP_CH = 16
SSM_STATE = 64
N_STATE = SSM_GROUPS * SSM_STATE
D_FF = 2816
IN_COLS = Q_W + 2 * KV_W + 3 * D_MODEL
EPS = 1e-6
BLK = 128
PAD = BLK - N_META
N_DEV = 8
NEG = -1e30
SSM_KB = 8
ST_KB = N_STATE // SSM_KB
LB_KB = ST_KB // BLK
N_LB = N_STATE // BLK

ADAM_LR = 0.001
ADAM_B1 = 0.9
ADAM_B2 = 0.999
ADAM_EPS = 1e-08
ADAM_WD = 0.01
ADAM_STEP = 10

VMEM_LIMIT = 48 * 1024 * 1024
MESH = pl.DeviceIdType.MESH

SMALL = ["norm_mix", "q_norm", "k_norm", "attn_sinks", "lam_re", "lam_im", "log_dt", "ssm_b_re", "ssm_b_im",
         "ssm_c_re", "ssm_c_im", "ssm_d", "attn_branch_norm", "ssm_branch_norm", "norm_ffn"]
WEIGHTS = ["meta_tokens", "norm_mix", "w_in", "q_norm", "k_norm", "attn_sinks", "lam_re", "lam_im", "log_dt",
           "ssm_b_re", "ssm_b_im", "ssm_c_re", "ssm_c_im", "ssm_d", "w_glu", "attn_branch_norm",
           "ssm_branch_norm", "w_out", "norm_ffn", "w_ffn_in", "w_ffn_out"]


def _params(**kw):
    return pltpu.CompilerParams(vmem_limit_bytes=VMEM_LIMIT, **kw)


def _pick(n, cap, mult=16):
    best = None
    for d in range(mult, min(n, cap) + 1, mult):
        if n % d == 0:
            best = d
    assert best is not None, (n, cap, mult)
    return best


def _my_index():
    return 4 * lax.axis_index("x") + 2 * lax.axis_index("y") + lax.axis_index("c")


def _rms(x, g):
    r = lax.rsqrt(jnp.mean(x * x, axis=-1, keepdims=True) + EPS)
    return x * r * g


def _rms_bwd(x, g, dy):
    r = lax.rsqrt(jnp.mean(x * x, axis=-1, keepdims=True) + EPS)
    t = dy * g
    dx = r * t - x * (r * r * r) * jnp.mean(t * x, axis=-1, keepdims=True)
    dg = jnp.sum(dy * (x * r), axis=0, keepdims=True)
    return dx, dg


def _sigmoid(x):
    return jax.nn.sigmoid(x)


def _gelu(x):
    k = math.sqrt(2.0 / math.pi)
    return 0.5 * x * (1.0 + jnp.tanh(k * (x + 0.044715 * (x * x * x))))


def _gelu_grad(x):
    k = math.sqrt(2.0 / math.pi)
    t = jnp.tanh(k * (x + 0.044715 * (x * x * x)))
    return 0.5 * (1.0 + t) + 0.5 * x * (1.0 - t * t) * (k * (1.0 + 3.0 * 0.044715 * (x * x)))


def _head_mean(x, e_ref):
    hi = x.astype(BF16)
    lo = (x - hi.astype(F32)).astype(BF16)
    e = e_ref[...]
    out = []
    for b in range(x.shape[1] // 256):
        sl = slice(256 * b, 256 * b + 256)
        s = (jnp.dot(hi[:, sl], e, preferred_element_type=F32)
             + jnp.dot(lo[:, sl], e, preferred_element_type=F32))
        out.append(s)
    s = out[0] if len(out) == 1 else jnp.concatenate(out, axis=1)
    return s * (1.0 / HEAD_DIM)


def _head_rms(x, g, e_ref):
    r = lax.rsqrt(_head_mean(x * x, e_ref) + EPS)
    return x * r * g


def _head_rms_bwd(x, g, dy, e_ref):
    r = lax.rsqrt(_head_mean(x * x, e_ref) + EPS)
    t = dy * g
    dx = r * t - x * (r * r * r) * _head_mean(t * x, e_ref)
    dg = jnp.sum(dy * (x * r), axis=0, keepdims=True)
    return dx, dg


def _lane_half(shape):
    lane = lax.broadcasted_iota(jnp.int32, shape, len(shape) - 1)
    return (lane >> 6) & 1


def _matmul(name, a, w, *, nt, tm, tn, tk, n=None, w_off=0, res=None, norm_g=None, out_dtype=F32):
    m_dim, k_dim = a.shape
    n_dim = n if n is not None else (w.shape[0] if nt else w.shape[1])
    gm, gn, gk = m_dim // tm, n_dim // tn, k_dim // tk
    assert gm * tm == m_dim and gn * tn == n_dim and gk * tk == k_dim, (name, a.shape, w.shape, tm, tn, tk)
    assert norm_g is None or tn == n_dim
    direct = out_dtype == F32
    dn = (((1,), (1,)), ((), ())) if nt else (((1,), (0,)), ((), ()))

    def body(*refs):
        refs = list(refs)
        a_ref, w_ref = refs[0], refs[1]
        pos = 2
        r_ref = g_ref = on_ref = None
        if res is not None:
            r_ref, pos = refs[pos], pos + 1
        if norm_g is not None:
            g_ref, pos = refs[pos], pos + 1
        o_ref, pos = refs[pos], pos + 1
        if norm_g is not None:
            on_ref, pos = refs[pos], pos + 1
        part = lax.dot_general(a_ref[...], w_ref[...], dn, preferred_element_type=F32)
        if gk == 1:
            r = part if r_ref is None else r_ref[...] + part
            o_ref[...] = r.astype(out_dtype)
            if on_ref is not None:
                on_ref[...] = _rms(r, g_ref[...]).astype(BF16)
            return
        acc = o_ref if direct else refs[pos]
        k = pl.program_id(2)

        @pl.when(k == 0)
        def _():
            acc[...] = part if r_ref is None or not direct else r_ref[...] + part

        @pl.when(k > 0)
        def _():
            acc[...] += part

        @pl.when(k == gk - 1)
        def _():
            if not direct:
                r = acc[...]
                if r_ref is not None:
                    r = r_ref[...] + r
                o_ref[...] = r.astype(out_dtype)
            if on_ref is not None:
                on_ref[...] = _rms(o_ref[...].astype(F32), g_ref[...]).astype(BF16)

    if nt:
        w_spec = pl.BlockSpec((tn, tk), lambda i, j, k: (j + w_off, k))
    else:
        w_spec = pl.BlockSpec((tk, tn), lambda i, j, k: (k, j))
    in_specs = [pl.BlockSpec((tm, tk), lambda i, j, k: (i, k)), w_spec]
    args = [a, w]
    out_spec = pl.BlockSpec((tm, tn), lambda i, j, k: (i, j))
    out_specs, out_shape = [out_spec], [jax.ShapeDtypeStruct((m_dim, n_dim), out_dtype)]
    if res is not None:
        in_specs.append(out_spec)
        args.append(res)
    if norm_g is not None:
        in_specs.append(pl.BlockSpec((1, tn), lambda i, j, k: (0, 0)))
        args.append(norm_g)
        out_specs.append(out_spec)
        out_shape.append(jax.ShapeDtypeStruct((m_dim, n_dim), BF16))
    out = pl.pallas_call(
        body, name=name, grid=(gm, gn, gk),
        in_specs=in_specs, out_specs=out_specs, out_shape=out_shape,
        scratch_shapes=[] if direct or gk == 1 else [pltpu.VMEM((tm, tn), F32)],
        compiler_params=_params(dimension_semantics=("parallel", "parallel", "arbitrary")),
    )(*args)
    return out if norm_g is not None else out[0]


def _matmul_tn(name, a, b, *, tm, tn, tl, after=None, out_rows=None):
    l_dim, m_dim = a.shape
    n_dim = b.shape[1]
    gm, gn, gl = m_dim // tm, n_dim // tn, l_dim // tl
    assert gm * tm == m_dim and gn * tn == n_dim and gl * tl == l_dim, (name, a.shape, b.shape, tm, tn, tl)

    def body(*refs):
        a_ref, b_ref = refs[0], refs[1]
        o_ref, ob_ref = refs[-2], refs[-1]

        @pl.when(pl.program_id(2) == 0)
        def _():
            o_ref[...] = jnp.zeros_like(o_ref)

        o_ref[...] += lax.dot_general(a_ref[...], b_ref[...], (((0,), (0,)), ((), ())),
                                      preferred_element_type=F32)

        @pl.when(pl.program_id(2) == gl - 1)
        def _():
            ob_ref[...] = o_ref[...].astype(BF16)

    out_row = out_rows if out_rows is not None else (lambda i: i)
    out_spec = pl.BlockSpec((tm, tn), lambda i, j, l: (out_row(i), j))
    in_specs = [pl.BlockSpec((tl, tm), lambda i, j, l: (l, i)), pl.BlockSpec((tl, tn), lambda i, j, l: (l, j))]
    args = [a, b]
    if after is not None:
        in_specs.append(pl.BlockSpec(memory_space=pl.ANY))
        args.append(after)
    return pl.pallas_call(
        body, name=name, grid=(gm, gn, gl),
        in_specs=in_specs,
        out_specs=[out_spec, out_spec],
        out_shape=[jax.ShapeDtypeStruct((m_dim, n_dim), F32), jax.ShapeDtypeStruct((m_dim, n_dim), BF16)],
        compiler_params=_params(dimension_semantics=("parallel", "parallel", "arbitrary")),
    )(*args)


def _row_spec(tm, cols, f=None):
    if f is None:
        return pl.BlockSpec((tm, cols), lambda i: (i, 0))
    return pl.BlockSpec((tm, cols), lambda i: (f(i), 0))


def _full_spec(shape):
    nd = len(shape)
    return pl.BlockSpec(shape, lambda i: (0,) * nd)


def _shifted_specs(n_sub, n_blocks):
    return [_row_spec(BLK, D_MODEL, (lambda i, k=k: jnp.clip(n_sub * i - 1 + k, 0, n_blocks - 1)))
            for k in range(n_sub)]


def _embed_norm(x2d, g, tm):
    s_len = x2d.shape[0]
    l_dim = s_len + BLK
    n_sub = tm // BLK

    def body(*refs):
        x_refs, g_ref, h_ref, xn_ref = refs[:n_sub], refs[n_sub], refs[n_sub + 1], refs[n_sub + 2]
        i = pl.program_id(0)
        for k in range(n_sub):
            rows = slice(BLK * k, BLK * k + BLK)
            h = x_refs[k][...] * jnp.where(n_sub * i + k >= 1, 1.0, 0.0)
            h_ref[rows, :] = h
            xn_ref[rows, :] = _rms(h, g_ref[...]).astype(BF16)

    return pl.pallas_call(
        body, name="embed_norm", grid=(l_dim // tm,),
        in_specs=_shifted_specs(n_sub, s_len // BLK) + [_full_spec((1, D_MODEL))],
        out_specs=[_row_spec(tm, D_MODEL), _row_spec(tm, D_MODEL)],
        out_shape=[jax.ShapeDtypeStruct((l_dim, D_MODEL), F32),
                   jax.ShapeDtypeStruct((l_dim, D_MODEL), BF16)],
        compiler_params=_params(),
    )(*([x2d] * n_sub), g)


def _embed_meta(meta_pad, g, h0, xn):
    def body(mp_ref, g_ref, h_in, xn_in, h_ref, xn_ref):
        h_ref[...] = mp_ref[...]
        xn_ref[...] = _rms(mp_ref[...], g_ref[...]).astype(BF16)

    any_spec = pl.BlockSpec(memory_space=pl.ANY)
    return pl.pallas_call(
        body, name="embed_meta", grid=(1,),
        in_specs=[_full_spec((BLK, D_MODEL)), _full_spec((1, D_MODEL)), any_spec, any_spec],
        out_specs=[_row_spec(BLK, D_MODEL), _row_spec(BLK, D_MODEL)],
        out_shape=[jax.ShapeDtypeStruct(h0.shape, F32), jax.ShapeDtypeStruct(xn.shape, BF16)],
        input_output_aliases={2: 0, 3: 1},
        compiler_params=_params(),
    )(meta_pad, g, h0, xn)


KVX_W = 2 * N_KV_HEADS * BLK


def _qk_prep(qkv, q_norm_t, k_norm_t, e_mat, tm):
    l_dim = qkv.shape[0]

    def body(x_ref, qg_ref, kg_ref, e_ref, q_ref, kf_ref, vf_ref):
        x = x_ref[...]
        q = _head_rms(x[:, :Q_W], qg_ref[...], e_ref) * (HEAD_DIM ** -0.5)
        q_ref[...] = q.astype(BF16)
        k = _head_rms(x[:, Q_W:Q_W + KV_W], kg_ref[...], e_ref)
        v = x[:, Q_W + KV_W:Q_W + 2 * KV_W]
        half = _lane_half((tm, BLK))
        for src, dst in ((k, kf_ref), (v, vf_ref)):
            for kv in range(N_KV_HEADS):
                blk = src[:, BLK * (kv // 2):BLK * (kv // 2) + BLK]
                swapped = pltpu.roll(blk, HEAD_DIM, axis=1)
                for e in range(2):
                    val = blk if kv % 2 == e else swapped
                    idx = 2 * kv + e
                    dst[:, BLK * idx:BLK * idx + BLK] = jnp.where(half == e, val, 0.0).astype(BF16)

    return pl.pallas_call(
        body, name="qk_prep", grid=(l_dim // tm,),
        in_specs=[_row_spec(tm, Q_W + 2 * KV_W), _full_spec((1, Q_W)), _full_spec((1, KV_W)),
                  _full_spec((256, 256))],
        out_specs=[_row_spec(tm, Q_W), _row_spec(tm, KVX_W), _row_spec(tm, KVX_W)],
        out_shape=[jax.ShapeDtypeStruct((l_dim, Q_W), BF16),
                   jax.ShapeDtypeStruct((l_dim, KVX_W), BF16),
                   jax.ShapeDtypeStruct((l_dim, KVX_W), BF16)],
        compiler_params=_params(),
    )(qkv, q_norm_t, k_norm_t, e_mat)


def _merge_fwd(attn, zab, gates, abn, sbn, tm):
    l_dim = attn.shape[0]

    def body(a_ref, z_ref, g_ref, an_ref, sn_ref, o_ref):
        z = z_ref[...].astype(F32)
        g = g_ref[...].astype(F32)
        ssm = z[:, :D_MODEL] * _sigmoid(z[:, D_MODEL:])
        merged = (_sigmoid(g[:, :D_MODEL]) * _rms(a_ref[...], an_ref[...])
                  + _sigmoid(g[:, D_MODEL:]) * _rms(ssm, sn_ref[...]))
        o_ref[...] = merged.astype(BF16)

    return pl.pallas_call(
        body, name="merge_fwd", grid=(l_dim // tm,),
        in_specs=[_row_spec(tm, D_MODEL), _row_spec(tm, 2 * D_MODEL), _row_spec(tm, 2 * D_MODEL),
                  _full_spec((1, D_MODEL)), _full_spec((1, D_MODEL))],
        out_specs=_row_spec(tm, D_MODEL),
        out_shape=jax.ShapeDtypeStruct((l_dim, D_MODEL), BF16),
        compiler_params=_params(),
    )(attn, zab, gates, abn, sbn)


def _merge_bwd(attn, zab, gates, abn, sbn, dmerged, tm):
    l_dim = attn.shape[0]

    def body(a_ref, z_ref, g_ref, an_ref, sn_ref, dm_ref, da_ref, dz_ref, dg_ref, dan_ref, dsn_ref):
        @pl.when(pl.program_id(0) == 0)
        def _():
            dan_ref[...] = jnp.zeros_like(dan_ref)
            dsn_ref[...] = jnp.zeros_like(dsn_ref)

        z = z_ref[...].astype(F32)
        g = g_ref[...].astype(F32)
        dm = dm_ref[...].astype(F32)
        attn_v = a_ref[...]
        za, zb = z[:, :D_MODEL], z[:, D_MODEL:]
        sb = _sigmoid(zb)
        ssm = za * sb
        s_ga, s_gs = _sigmoid(g[:, :D_MODEL]), _sigmoid(g[:, D_MODEL:])
        a_n = _rms(attn_v, an_ref[...])
        s_n = _rms(ssm, sn_ref[...])
        dg_ref[:, :D_MODEL] = (dm * a_n * s_ga * (1.0 - s_ga)).astype(BF16)
        dg_ref[:, D_MODEL:] = (dm * s_n * s_gs * (1.0 - s_gs)).astype(BF16)
        dattn, dan = _rms_bwd(attn_v, an_ref[...], dm * s_ga)
        dssm, dsn = _rms_bwd(ssm, sn_ref[...], dm * s_gs)
        da_ref[...] = dattn.astype(BF16)
        dz_ref[:, :D_MODEL] = (dssm * sb).astype(BF16)
        dz_ref[:, D_MODEL:] = (dssm * za * sb * (1.0 - sb)).astype(BF16)
        dan_ref[...] += dan
        dsn_ref[...] += dsn

    return pl.pallas_call(
        body, name="merge_bwd", grid=(l_dim // tm,),
        in_specs=[_row_spec(tm, D_MODEL), _row_spec(tm, 2 * D_MODEL), _row_spec(tm, 2 * D_MODEL),
                  _full_spec((1, D_MODEL)), _full_spec((1, D_MODEL)), _row_spec(tm, D_MODEL)],
        out_specs=[_row_spec(tm, D_MODEL), _row_spec(tm, 2 * D_MODEL), _row_spec(tm, 2 * D_MODEL),
                   _full_spec((1, D_MODEL)), _full_spec((1, D_MODEL))],
        out_shape=[jax.ShapeDtypeStruct((l_dim, D_MODEL), BF16),
                   jax.ShapeDtypeStruct((l_dim, 2 * D_MODEL), BF16),
                   jax.ShapeDtypeStruct((l_dim, IN_COLS), BF16),
                   jax.ShapeDtypeStruct((1, D_MODEL), F32), jax.ShapeDtypeStruct((1, D_MODEL), F32)],
        compiler_params=_params(),
    )(attn, zab, gates, abn, sbn, dmerged)


FF_TILE = D_FF // 2


def _ffn_in_swiglu(hn, wt_ffn_in, tm):
    l_dim = hn.shape[0]
    nt = (((1,), (1,)), ((), ()))

    def body(a_ref, w_ref, gu_ref, act_ref):
        r = lax.dot_general(a_ref[...], w_ref[...], nt, preferred_element_type=F32)
        gate, up = r[:, :FF_TILE], r[:, FF_TILE:]
        gu_ref[...] = r.astype(BF16)
        act_ref[...] = (gate * _sigmoid(gate) * up).astype(BF16)

    return pl.pallas_call(
        body, name="ffn_in_swiglu", grid=(l_dim // tm, 2),
        in_specs=[pl.BlockSpec((tm, D_MODEL), lambda i, j: (i, 0)),
                  pl.BlockSpec((2 * FF_TILE, D_MODEL), lambda i, j: (j, 0))],
        out_specs=[pl.BlockSpec((tm, 2 * FF_TILE), lambda i, j: (i, j)),
                   pl.BlockSpec((tm, FF_TILE), lambda i, j: (i, j))],
        out_shape=[jax.ShapeDtypeStruct((l_dim, 2 * D_FF), BF16), jax.ShapeDtypeStruct((l_dim, D_FF), BF16)],
        compiler_params=_params(dimension_semantics=("parallel", "parallel")),
    )(hn, wt_ffn_in)


def _d_act_swiglu(dh2_b, w_ffn_out, gu, tm):
    l_dim = dh2_b.shape[0]
    nt = (((1,), (1,)), ((), ()))

    def body(d_ref, w_ref, gu_ref, o_ref):
        d = lax.dot_general(d_ref[...], w_ref[...], nt, preferred_element_type=F32)
        gate = gu_ref[:, :FF_TILE].astype(F32)
        up = gu_ref[:, FF_TILE:].astype(F32)
        s = _sigmoid(gate)
        o_ref[:, :FF_TILE] = (d * up * (s * (1.0 + gate * (1.0 - s)))).astype(BF16)
        o_ref[:, FF_TILE:] = (d * (gate * s)).astype(BF16)

    return pl.pallas_call(
        body, name="d_act_swiglu", grid=(l_dim // tm, 2),
        in_specs=[pl.BlockSpec((tm, D_MODEL), lambda i, j: (i, 0)),
                  pl.BlockSpec((FF_TILE, D_MODEL), lambda i, j: (j, 0)),
                  pl.BlockSpec((tm, 2 * FF_TILE), lambda i, j: (i, j))],
        out_specs=pl.BlockSpec((tm, 2 * FF_TILE), lambda i, j: (i, j)),
        out_shape=jax.ShapeDtypeStruct((l_dim, 2 * D_FF), BF16),
        compiler_params=_params(dimension_semantics=("parallel", "parallel")),
    )(dh2_b, w_ffn_out, gu)


def _loss_grad(h2, target2d, tm):
    l_dim = h2.shape[0]
    n_sub = tm // BLK

    def body(*refs):
        h_ref, t_refs = refs[0], refs[1:1 + n_sub]
        d_ref, db_ref, loss_ref = refs[1 + n_sub:]
        i = pl.program_id(0)

        @pl.when(i == 0)
        def _():
            loss_ref[...] = jnp.zeros_like(loss_ref)

        for k in range(n_sub):
            rows = slice(BLK * k, BLK * k + BLK)
            real = jnp.where(n_sub * i + k >= 1, 1.0, 0.0)
            err = (h_ref[rows, :] - t_refs[k][...]) * real
            d = err * (1.0 / D_MODEL)
            d_ref[rows, :] = d
            db_ref[rows, :] = d.astype(BF16)
            loss_ref[...] += 0.5 * jnp.sum(jnp.mean(err * err, axis=-1, keepdims=True), axis=0, keepdims=True)

    return pl.pallas_call(
        body, name="loss_grad", grid=(l_dim // tm,),
        in_specs=[_row_spec(tm, D_MODEL)] + _shifted_specs(n_sub, target2d.shape[0] // BLK),
        out_specs=[_row_spec(tm, D_MODEL), _row_spec(tm, D_MODEL), _full_spec((1, 1))],
        out_shape=[jax.ShapeDtypeStruct((l_dim, D_MODEL), F32), jax.ShapeDtypeStruct((l_dim, D_MODEL), BF16),
                   jax.ShapeDtypeStruct((1, 1), F32)],
        compiler_params=_params(),
    )(h2, *([target2d] * n_sub))


def _norm_bwd_res(name, h, g, dy, dres, tm):
    l_dim = h.shape[0]

    def body(h_ref, g_ref, dy_ref, dr_ref, o_ref, ob_ref, dg_ref):
        @pl.when(pl.program_id(0) == 0)
        def _():
            dg_ref[...] = jnp.zeros_like(dg_ref)

        dx, dg = _rms_bwd(h_ref[...], g_ref[...], dy_ref[...].astype(F32))
        out = dr_ref[...] + dx
        o_ref[...] = out
        ob_ref[...] = out.astype(BF16)
        dg_ref[...] += dg

    return pl.pallas_call(
        body, name=name, grid=(l_dim // tm,),
        in_specs=[_row_spec(tm, D_MODEL), _full_spec((1, D_MODEL)), _row_spec(tm, D_MODEL), _row_spec(tm, D_MODEL)],
        out_specs=[_row_spec(tm, D_MODEL), _row_spec(tm, D_MODEL), _full_spec((1, D_MODEL))],
        out_shape=[jax.ShapeDtypeStruct((l_dim, D_MODEL), F32), jax.ShapeDtypeStruct((l_dim, D_MODEL), BF16),
                   jax.ShapeDtypeStruct((1, D_MODEL), F32)],
        compiler_params=_params(),
    )(h, g, dy, dres)


def _final_bwd(h0, g, dxn, dh1, tm):
    l_dim = h0.shape[0]
    n_sub = tm // BLK
    n_tiles = (l_dim - BLK) // tm

    def sub_specs():
        return [_row_spec(BLK, D_MODEL, (lambda j, k=k: jnp.where(j < n_tiles, n_sub * j + 1 + k, 0)))
                for k in range(n_sub)]

    def body(*refs):
        h_refs, g_ref = refs[:n_sub], refs[n_sub]
        dy_refs, dr_refs = refs[n_sub + 1:2 * n_sub + 1], refs[2 * n_sub + 1:3 * n_sub + 1]
        gx_ref, gm_ref, dg_ref = refs[3 * n_sub + 1:]
        j = pl.program_id(0)

        @pl.when(j == 0)
        def _():
            dg_ref[...] = jnp.zeros_like(dg_ref)

        def block(k):
            dx, dg = _rms_bwd(h_refs[k][...], g_ref[...], dy_refs[k][...].astype(F32))
            dg_ref[...] += dg
            return dr_refs[k][...] + dx

        @pl.when(j < n_tiles)
        def _():
            for k in range(n_sub):
                gx_ref[BLK * k:BLK * k + BLK, :] = block(k)

        @pl.when(j == n_tiles)
        def _():
            gm_ref[...] = block(0)

    return pl.pallas_call(
        body, name="final_bwd", grid=(n_tiles + 1,),
        in_specs=sub_specs() + [_full_spec((1, D_MODEL))] + sub_specs() + sub_specs(),
        out_specs=[_row_spec(tm, D_MODEL, lambda j: jnp.minimum(j, n_tiles - 1)), _full_spec((BLK, D_MODEL)),
                   _full_spec((1, D_MODEL))],
        out_shape=[jax.ShapeDtypeStruct((l_dim - BLK, D_MODEL), F32), jax.ShapeDtypeStruct((BLK, D_MODEL), F32),
                   jax.ShapeDtypeStruct((1, D_MODEL), F32)],
        compiler_params=_params(),
    )(*([h0] * n_sub), g, *([dxn] * n_sub), *([dh1] * n_sub))


def _attn_valid(n):
    shape = (2 * BLK, 3 * BLK)
    qi = lax.broadcasted_iota(jnp.int32, shape, 0) & (BLK - 1)
    col = lax.broadcasted_iota(jnp.int32, shape, 1)
    kj = col & (BLK - 1)
    part = col >> 7
    nn = jnp.zeros(shape, jnp.int32) + n
    meta_ok = (part == 0) & (kj >= PAD) & (nn >= 1)
    prev_ok = (part == 1) & (kj > qi) & (nn >= 2)
    cur_ok = (part == 2) & (kj <= qi) & ((nn >= 1) | (kj >= PAD))
    return meta_ok | prev_ok | cur_ok


def _attn_scores(q_ref, kwin, sk_ref, valid, kv, e):
    qs = jnp.concatenate([q_ref[:, BLK * (2 * kv):BLK * (2 * kv) + BLK],
                          q_ref[:, BLK * (2 * kv + 1):BLK * (2 * kv + 1) + BLK]], axis=0)
    s = lax.dot_general(qs, kwin, (((1,), (1,)), ((), ())), preferred_element_type=F32)
    h0 = 4 * kv + e
    row = lax.broadcasted_iota(jnp.int32, (2 * BLK, 1), 0)
    sink = jnp.where(row < BLK, sk_ref[:, h0:h0 + 1], sk_ref[:, h0 + 2:h0 + 3])
    return qs, jnp.where(valid, s, NEG), sink


def _attn_specs(nb):
    prev = lambda i: jnp.maximum(i - 1, 0)
    zero = lambda i: 0
    kv_specs = [_row_spec(BLK, KVX_W, zero), _row_spec(BLK, KVX_W, prev), _row_spec(BLK, KVX_W)]
    return kv_specs


def _attn_fwd(qn, kf, vf, sinks):
    l_dim = qn.shape[0]
    nb = l_dim // BLK

    def body(q_ref, km_ref, kp_ref, kc_ref, vm_ref, vp_ref, vc_ref, sk_ref, o_ref, lse_ref):
        valid = _attn_valid(pl.program_id(0))
        lane = lax.broadcasted_iota(jnp.int32, (BLK, BLK), 1)
        lse_all = jnp.zeros((BLK, BLK), F32)
        for kv in range(N_KV_HEADS):
            outs = []
            for e in range(2):
                sl = slice(BLK * (2 * kv + e), BLK * (2 * kv + e) + BLK)
                kwin = jnp.concatenate([km_ref[:, sl], kp_ref[:, sl], kc_ref[:, sl]], axis=0)
                vwin = jnp.concatenate([vm_ref[:, sl], vp_ref[:, sl], vc_ref[:, sl]], axis=0)
                _, s, sink = _attn_scores(q_ref, kwin, sk_ref, valid, kv, e)
                m = jnp.maximum(jnp.max(s, axis=-1, keepdims=True), sink)
                ex = jnp.exp(s - m)
                den = jnp.sum(ex, axis=-1, keepdims=True) + jnp.exp(sink - m)
                p = ex * (1.0 / den)
                outs.append(jnp.dot(p.astype(BF16), vwin, preferred_element_type=F32))
                lse = m + jnp.log(den)
                lse_all = jnp.where(lane == 4 * kv + e, lse[:BLK], lse_all)
                lse_all = jnp.where(lane == 4 * kv + 2 + e, lse[BLK:], lse_all)
            o = outs[0] + outs[1]
            o_ref[:, BLK * (2 * kv):BLK * (2 * kv) + BLK] = o[:BLK]
            o_ref[:, BLK * (2 * kv + 1):BLK * (2 * kv + 1) + BLK] = o[BLK:]
        lse_ref[...] = lse_all

    kv_specs = _attn_specs(nb)
    return pl.pallas_call(
        body, name="attn_fwd", grid=(nb,),
        in_specs=[_row_spec(BLK, Q_W)] + kv_specs + kv_specs + [_full_spec((1, N_Q_HEADS))],
        out_specs=[_row_spec(BLK, Q_W), _row_spec(BLK, BLK)],
        out_shape=[jax.ShapeDtypeStruct((l_dim, Q_W), F32), jax.ShapeDtypeStruct((l_dim, BLK), F32)],
        compiler_params=_params(),
    )(qn, kf, kf, kf, vf, vf, vf, sinks)


def _attn_bwd(qn, kf, vf, sinks, lse, attn, dattn):
    l_dim = qn.shape[0]
    nb = l_dim // BLK
    wide = KV_W
    tn = (((0,), (0,)), ((), ()))
    nt = (((1,), (1,)), ((), ()))

    def body(q_ref, km_ref, kp_ref, kc_ref, vm_ref, vp_ref, vc_ref, sk_ref, lse_ref, o_ref, do_ref,
             dq_ref, dkc_ref, dkp_ref, dkm_ref, dvc_ref, dvp_ref, dvm_ref, dsk_ref):
        @pl.when(pl.program_id(0) == 0)
        def _():
            dkm_ref[...] = jnp.zeros_like(dkm_ref)
            dvm_ref[...] = jnp.zeros_like(dvm_ref)
            dsk_ref[...] = jnp.zeros_like(dsk_ref)

        valid = _attn_valid(pl.program_id(0))
        half = _lane_half((BLK, BLK))
        half2 = _lane_half((2 * BLK, BLK))
        half3 = _lane_half((3 * BLK, BLK))
        lane16 = lax.broadcasted_iota(jnp.int32, (1, N_Q_HEADS), 1)
        dsk = jnp.zeros((1, N_Q_HEADS), F32)
        folded_k, folded_v = [], []
        for kv in range(N_KV_HEADS):
            j0, j1 = 2 * kv, 2 * kv + 1
            do0 = do_ref[:, BLK * j0:BLK * j0 + BLK]
            do1 = do_ref[:, BLK * j1:BLK * j1 + BLK]
            do0f, do1f = do0.astype(F32), do1.astype(F32)
            prod0 = do0f * o_ref[:, BLK * j0:BLK * j0 + BLK]
            prod1 = do1f * o_ref[:, BLK * j1:BLK * j1 + BLK]
            dos = jnp.concatenate([do0, do1], axis=0)
            dqs, dks, dvs = [], [], []
            for e in range(2):
                sl = slice(BLK * (2 * kv + e), BLK * (2 * kv + e) + BLK)
                kwin = jnp.concatenate([km_ref[:, sl], kp_ref[:, sl], kc_ref[:, sl]], axis=0)
                vwin = jnp.concatenate([vm_ref[:, sl], vp_ref[:, sl], vc_ref[:, sl]], axis=0)
                qs, s, sink = _attn_scores(q_ref, kwin, sk_ref, valid, kv, e)
                h0 = 4 * kv + e
                lse_rows = jnp.concatenate([lse_ref[:, h0:h0 + 1], lse_ref[:, h0 + 2:h0 + 3]], axis=0)
                p = jnp.exp(s - lse_rows)
                p_sink = jnp.exp(sink - lse_rows)
                delta = jnp.concatenate(
                    [jnp.sum(jnp.where(half == e, prod0, 0.0), axis=-1, keepdims=True),
                     jnp.sum(jnp.where(half == e, prod1, 0.0), axis=-1, keepdims=True)], axis=0)
                dp = lax.dot_general(dos, vwin, nt, preferred_element_type=F32)
                ds = (p * (dp - delta)).astype(BF16)
                pb = p.astype(BF16)
                dqs.append(jnp.dot(ds, kwin, preferred_element_type=F32))
                dks.append(lax.dot_general(ds, qs, tn, preferred_element_type=F32))
                dvs.append(lax.dot_general(pb, dos, tn, preferred_element_type=F32))
                sink_g = -(p_sink * delta)
                g_lo = jnp.sum(sink_g[:BLK], axis=0, keepdims=True)
                g_hi = jnp.sum(sink_g[BLK:], axis=0, keepdims=True)
                dsk = dsk + jnp.where(lane16 == h0, g_lo, 0.0) + jnp.where(lane16 == h0 + 2, g_hi, 0.0)
            dq = jnp.where(half2 == 0, dqs[0], dqs[1])
            dq_ref[:, BLK * j0:BLK * j0 + BLK] = dq[:BLK].astype(BF16)
            dq_ref[:, BLK * j1:BLK * j1 + BLK] = dq[BLK:].astype(BF16)
            own = kv % 2
            folded_k.append(dks[own] + pltpu.roll(dks[1 - own], HEAD_DIM, axis=1))
            folded_v.append(dvs[own] + pltpu.roll(dvs[1 - own], HEAD_DIM, axis=1))
            if own == 1:
                cols = slice(BLK * (kv // 2), BLK * (kv // 2) + BLK)
                for folded, m_ref, p_ref, c_ref in ((folded_k, dkm_ref, dkp_ref, dkc_ref),
                                                    (folded_v, dvm_ref, dvp_ref, dvc_ref)):
                    both = jnp.where(half3 == 0, folded[0], folded[1])
                    m_ref[:, cols] += both[:BLK]
                    p_ref[:, cols] = both[BLK:2 * BLK].astype(BF16)
                    c_ref[:, cols] = both[2 * BLK:].astype(BF16)
                folded_k, folded_v = [], []
        dsk_ref[...] += dsk

    kv_specs = _attn_specs(nb)
    row_wide = _row_spec(BLK, wide)
    acc_wide = _full_spec((BLK, wide))
    big = jax.ShapeDtypeStruct((l_dim, wide), BF16)
    return pl.pallas_call(
        body, name="attn_bwd", grid=(nb,),
        in_specs=[_row_spec(BLK, Q_W)] + kv_specs + kv_specs
        + [_full_spec((1, N_Q_HEADS)), _row_spec(BLK, BLK), _row_spec(BLK, Q_W), _row_spec(BLK, Q_W)],
        out_specs=[_row_spec(BLK, Q_W), row_wide, row_wide, acc_wide, row_wide, row_wide, acc_wide,
                   _full_spec((1, N_Q_HEADS))],
        out_shape=[jax.ShapeDtypeStruct((l_dim, Q_W), BF16), big, big, jax.ShapeDtypeStruct((BLK, wide), F32),
                   big, big, jax.ShapeDtypeStruct((BLK, wide), F32), jax.ShapeDtypeStruct((1, N_Q_HEADS), F32)],
        compiler_params=_params(),
    )(qn, kf, kf, kf, vf, vf, vf, sinks, lse, attn, dattn)


def _qk_bwd(qkv, q_norm_t, k_norm_t, e_mat, dq, dkc, dkp, dkm, dvc, dvp, dvm, dproj):
    l_dim = qkv.shape[0]
    nb = l_dim // BLK
    wide = KV_W

    def body(x_ref, qg_ref, kg_ref, e_ref, dq_ref, dkc_ref, dkp_ref, dkm_ref, dvc_ref, dvp_ref, dvm_ref,
             dproj_ref, o_ref, dqg_ref, dkg_ref):
        i = pl.program_id(0)

        @pl.when(i == 0)
        def _():
            dqg_ref[...] = jnp.zeros_like(dqg_ref)
            dkg_ref[...] = jnp.zeros_like(dkg_ref)

        first = jnp.where(i == 0, 1.0, 0.0)
        not_last = jnp.where(i < nb - 1, 1.0, 0.0)
        dk_x = dkc_ref[...].astype(F32) + not_last * dkp_ref[...].astype(F32) + first * dkm_ref[...]
        dv_x = dvc_ref[...].astype(F32) + not_last * dvp_ref[...].astype(F32) + first * dvm_ref[...]
        x = x_ref[...]
        dqx, dqg = _head_rms_bwd(x[:, :Q_W], qg_ref[...], dq_ref[...].astype(F32) * (HEAD_DIM ** -0.5), e_ref)
        dkx, dkg = _head_rms_bwd(x[:, Q_W:Q_W + KV_W], kg_ref[...], dk_x, e_ref)
        o_ref[:, :Q_W] = dqx.astype(BF16)
        o_ref[:, Q_W:Q_W + KV_W] = dkx.astype(BF16)
        o_ref[:, Q_W + KV_W:] = dv_x.astype(BF16)
        dqg_ref[...] += dqg
        dkg_ref[...] += dkg

    nxt = lambda i: jnp.minimum(i + 1, nb - 1)
    row_wide = _row_spec(BLK, wide)
    nxt_wide = _row_spec(BLK, wide, nxt)
    acc_wide = _full_spec((BLK, wide))
    qkv_w = Q_W + 2 * KV_W
    in_specs = [_row_spec(BLK, qkv_w), _full_spec((1, Q_W)), _full_spec((1, KV_W)),
                _full_spec((256, 256)), _row_spec(BLK, Q_W),
                row_wide, nxt_wide, acc_wide, row_wide, nxt_wide, acc_wide, pl.BlockSpec(memory_space=pl.ANY)]
    return pl.pallas_call(
        body, name="qk_bwd", grid=(nb,),
        in_specs=in_specs,
        out_specs=[pl.BlockSpec((BLK, qkv_w), lambda i: (i, 3 * D_MODEL // qkv_w)),
                   _full_spec((1, Q_W)), _full_spec((1, KV_W))],
        input_output_aliases={len(in_specs) - 1: 0},
        out_shape=[jax.ShapeDtypeStruct(dproj.shape, BF16),
                   jax.ShapeDtypeStruct((1, Q_W), F32), jax.ShapeDtypeStruct((1, KV_W), F32)],
        compiler_params=_params(),
    )(qkv, q_norm_t, k_norm_t, e_mat, dq, dkc, dkp, dkm, dvc, dvp, dvm, dproj)


GRP = 8


def _strided(r, g):
    return pl.ds(r, g, stride=GRP)


def _lane_slab(ref, i, r, g):
    return ref[i, _strided(r, g), :]


def _chunk_carries(xr_ref, xi_ref, i, ar, ai, sqr_ref, sqi_ref, seq_r, seq_i, cin_r, cin_i, sign, reverse):
    g = xr_ref.shape[1] // GRP
    sr = si = None
    for r in (range(GRP - 1, -1, -1) if reverse else range(GRP)):
        xr, xi = _lane_slab(xr_ref, i, r, g), _lane_slab(xi_ref, i, r, g)
        if sr is not None:
            xr, xi = xr + ar * sr - ai * si, xi + ar * si + ai * sr
        sr, si = xr, xi
    row = lax.broadcasted_iota(jnp.int32, sr.shape, 0)
    idx, s = 0, 1
    while s < g:
        br = sqr_ref[idx:idx + 1, BLK * i:BLK * i + BLK]
        bi = sign * sqi_ref[idx:idx + 1, BLK * i:BLK * i + BLK]
        shift, keep = (g - s, row < g - s) if reverse else (s, row >= s)
        pr = jnp.where(keep, pltpu.roll(sr, shift, axis=0), 0.0)
        pi = jnp.where(keep, pltpu.roll(si, shift, axis=0), 0.0)
        sr, si = sr + br * pr - bi * pi, si + br * pi + bi * pr
        idx, s = idx + 1, 2 * s
    return sr + seq_r * cin_r - seq_i * cin_i, si + seq_r * cin_i + seq_i * cin_r


def _ssm_fwd_kb(u, wb_re, wb_im, wc_re, wc_im, d_skip, tabs, q):
    l_dim = u.shape[0]
    nc = l_dim // q
    g = q // GRP

    def body(u_ref, wbr_ref, wbi_ref, wcr_ref, wci_ref, d_ref, a1r_ref, a1i_ref, sqr_ref, sqi_ref,
             seqr_ref, seqi_ref, y_ref, z_ref, sr_ref, si_ref, cr_ref, ci_ref, xr_ref, xi_ref):
        @pl.when(pl.program_id(1) == 0)
        def _():
            cr_ref[...] = jnp.zeros_like(cr_ref)
            ci_ref[...] = jnp.zeros_like(ci_ref)

        u_kb = u_ref[...]
        ub = u_kb.astype(BF16)
        xr = jnp.dot(ub, wbr_ref[0], preferred_element_type=F32)
        xi = jnp.dot(ub, wbi_ref[0], preferred_element_type=F32)
        for i in range(LB_KB):
            xr_ref[i] = xr[:, BLK * i:BLK * i + BLK]
            xi_ref[i] = xi[:, BLK * i:BLK * i + BLK]
        row = lax.broadcasted_iota(jnp.int32, (g, BLK), 0)
        for i in range(LB_KB):
            lanes = slice(BLK * i, BLK * i + BLK)
            ar, ai = a1r_ref[0:1, lanes], a1i_ref[0:1, lanes]
            cin_r, cin_i = cr_ref[0:1, lanes], ci_ref[0:1, lanes]
            tr, ti = _chunk_carries(xr_ref, xi_ref, i, ar, ai, sqr_ref, sqi_ref, seqr_ref[:, lanes],
                                    seqi_ref[:, lanes], cin_r, cin_i, 1.0, reverse=False)
            cr_ref[0:1, lanes] = jnp.sum(jnp.where(row == g - 1, tr, 0.0), axis=0, keepdims=True)
            ci_ref[0:1, lanes] = jnp.sum(jnp.where(row == g - 1, ti, 0.0), axis=0, keepdims=True)
            pr = jnp.where(row == 0, cin_r, pltpu.roll(tr, 1, axis=0))
            pi = jnp.where(row == 0, cin_i, pltpu.roll(ti, 1, axis=0))
            for r in range(GRP):
                pr, pi = (_lane_slab(xr_ref, i, r, g) + ar * pr - ai * pi,
                          _lane_slab(xi_ref, i, r, g) + ar * pi + ai * pr)
                sr_ref[i, _strided(r, g), :] = pr
                si_ref[i, _strided(r, g), :] = pi
        s_r = jnp.concatenate([sr_ref[i] for i in range(LB_KB)], axis=1)
        s_i = jnp.concatenate([si_ref[i] for i in range(LB_KB)], axis=1)
        y = (jnp.dot(s_r.astype(BF16), wcr_ref[0], preferred_element_type=F32)
             - jnp.dot(s_i.astype(BF16), wci_ref[0], preferred_element_type=F32)
             + d_ref[...] * u_kb)
        y_ref[...] = y.astype(BF16)
        z_ref[...] = _gelu(y).astype(BF16)

    chan = pl.BlockSpec((q, BLK), lambda k, c: (c, k))
    wb_spec = pl.BlockSpec((1, BLK, ST_KB), lambda k, c: (k, 0, 0))
    wc_spec = pl.BlockSpec((1, ST_KB, BLK), lambda k, c: (k, 0, 0))
    tab_specs = [pl.BlockSpec((t.shape[0], ST_KB), lambda k, c: (0, k)) for t in tabs[:6]]
    state_spec = pl.BlockSpec((LB_KB, q, BLK), lambda k, c: (k, c, 0))
    state_shape = jax.ShapeDtypeStruct((N_LB, l_dim, BLK), F32)
    return pl.pallas_call(
        body, name="ssm_fwd", grid=(SSM_KB, nc),
        in_specs=[chan, wb_spec, wb_spec, wc_spec, wc_spec, pl.BlockSpec((1, BLK), lambda k, c: (0, k))] + tab_specs,
        out_specs=[chan, chan, state_spec, state_spec],
        out_shape=[jax.ShapeDtypeStruct((l_dim, D_MODEL), BF16), jax.ShapeDtypeStruct((l_dim, D_MODEL), BF16),
                   state_shape, state_shape],
        scratch_shapes=[pltpu.VMEM((8, ST_KB), F32), pltpu.VMEM((8, ST_KB), F32),
                        pltpu.VMEM((LB_KB, q, BLK), F32), pltpu.VMEM((LB_KB, q, BLK), F32)],
        compiler_params=_params(dimension_semantics=("parallel", "arbitrary")),
    )(u, wb_re, wb_im, wc_re, wc_im, d_skip, *tabs[:6])


def _ssm_bwd_kb(dz, y, u, s_re, s_im, wb_re, wb_im, wc_re, wc_im, d_skip, tabs, q, dproj):
    l_dim = u.shape[0]
    nc = l_dim // q
    g = q // GRP

    def body(dz_ref, y_ref, u_ref, sr_ref, si_ref, wbr_ref, wbi_ref, wcr_ref, wci_ref, d_ref,
             a1r_ref, a1i_ref, sqr_ref, sqi_ref, revr_ref, revi_ref, dproj_ref,
             du_ref, dd_ref, dar_ref, dai_ref, dwbr_ref, dwbi_ref, dwcr_ref, dwci_ref,
             cr_ref, ci_ref, gr_ref, gi_ref):
        @pl.when(pl.program_id(1) == 0)
        def _():
            for ref in (cr_ref, ci_ref, dd_ref, dar_ref, dai_ref, dwbr_ref, dwbi_ref, dwcr_ref, dwci_ref):
                ref[...] = jnp.zeros_like(ref)

        tn = (((0,), (0,)), ((), ()))
        nt = (((1,), (1,)), ((), ()))
        u_kb = u_ref[...]
        dy = dz_ref[...].astype(F32) * _gelu_grad(y_ref[...].astype(F32))
        dyb = dy.astype(BF16)
        ub = u_kb.astype(BF16)
        dd_ref[...] += jnp.sum(dy * u_kb, axis=0, keepdims=True)
        ds_r = lax.dot_general(dyb, wcr_ref[0], nt, preferred_element_type=F32)
        ds_i = -lax.dot_general(dyb, wci_ref[0], nt, preferred_element_type=F32)
        for i in range(LB_KB):
            gr_ref[i] = ds_r[:, BLK * i:BLK * i + BLK]
            gi_ref[i] = ds_i[:, BLK * i:BLK * i + BLK]
        row = lax.broadcasted_iota(jnp.int32, (g, BLK), 0)
        for i in range(LB_KB):
            lanes = slice(BLK * i, BLK * i + BLK)
            ar, ai = a1r_ref[0:1, lanes], -a1i_ref[0:1, lanes]
            cin_r, cin_i = cr_ref[0:1, lanes], ci_ref[0:1, lanes]
            tr, ti = _chunk_carries(gr_ref, gi_ref, i, ar, ai, sqr_ref, sqi_ref, revr_ref[:, lanes],
                                    -revi_ref[:, lanes], cin_r, cin_i, -1.0, reverse=True)
            cr_ref[0:1, lanes] = jnp.sum(jnp.where(row == 0, tr, 0.0), axis=0, keepdims=True)
            ci_ref[0:1, lanes] = jnp.sum(jnp.where(row == 0, ti, 0.0), axis=0, keepdims=True)
            nr = jnp.where(row == g - 1, cin_r, pltpu.roll(tr, g - 1, axis=0))
            ni = jnp.where(row == g - 1, cin_i, pltpu.roll(ti, g - 1, axis=0))
            acc_r = jnp.zeros((g, BLK), F32)
            acc_i = jnp.zeros((g, BLK), F32)
            for r in range(GRP - 1, -1, -1):
                s_r, s_i = _lane_slab(sr_ref, i, r, g), _lane_slab(si_ref, i, r, g)
                acc_r = acc_r + (nr * s_r + ni * s_i)
                acc_i = acc_i + (ni * s_r - nr * s_i)
                nr, ni = (_lane_slab(gr_ref, i, r, g) + ar * nr - ai * ni,
                          _lane_slab(gi_ref, i, r, g) + ar * ni + ai * nr)
                gr_ref[i, _strided(r, g), :] = nr
                gi_ref[i, _strided(r, g), :] = ni
            dar_ref[:, lanes] += jnp.sum(acc_r, axis=0, keepdims=True)
            dai_ref[:, lanes] += jnp.sum(acc_i, axis=0, keepdims=True)
        grb = jnp.concatenate([gr_ref[i] for i in range(LB_KB)], axis=1).astype(BF16)
        gib = jnp.concatenate([gi_ref[i] for i in range(LB_KB)], axis=1).astype(BF16)
        srb = jnp.concatenate([sr_ref[i] for i in range(LB_KB)], axis=1).astype(BF16)
        sib = jnp.concatenate([si_ref[i] for i in range(LB_KB)], axis=1).astype(BF16)
        du = (lax.dot_general(grb, wbr_ref[0], nt, preferred_element_type=F32)
              + lax.dot_general(gib, wbi_ref[0], nt, preferred_element_type=F32)
              + d_ref[...] * dy)
        du_ref[...] = du.astype(BF16)
        dwbr_ref[0] += lax.dot_general(ub, grb, tn, preferred_element_type=F32)
        dwbi_ref[0] += lax.dot_general(ub, gib, tn, preferred_element_type=F32)
        dwcr_ref[0] += lax.dot_general(srb, dyb, tn, preferred_element_type=F32)
        dwci_ref[0] -= lax.dot_general(sib, dyb, tn, preferred_element_type=F32)

    chan = pl.BlockSpec((q, BLK), lambda k, c: (nc - 1 - c, k))
    wb_spec = pl.BlockSpec((1, BLK, ST_KB), lambda k, c: (k, 0, 0))
    wc_spec = pl.BlockSpec((1, ST_KB, BLK), lambda k, c: (k, 0, 0))
    tab_in = [tabs[0], tabs[1], tabs[2], tabs[3], tabs[6], tabs[7]]
    tab_specs = [pl.BlockSpec((t.shape[0], ST_KB), lambda k, c: (0, k)) for t in tab_in]
    vec = pl.BlockSpec((1, BLK), lambda k, c: (0, k))
    svec = pl.BlockSpec((1, ST_KB), lambda k, c: (0, k))
    state_spec = pl.BlockSpec((LB_KB, q, BLK), lambda k, c: (k, nc - 1 - c, 0))
    du_spec = pl.BlockSpec((q, BLK), lambda k, c: (nc - 1 - c, 2 * D_MODEL // BLK + k))
    in_specs = ([chan, chan, chan, state_spec, state_spec, wb_spec, wb_spec, wc_spec, wc_spec, vec] + tab_specs
                + [pl.BlockSpec(memory_space=pl.ANY)])
    return pl.pallas_call(
        body, name="ssm_bwd", grid=(SSM_KB, nc),
        in_specs=in_specs,
        out_specs=[du_spec, vec, svec, svec, wb_spec, wb_spec, wc_spec, wc_spec],
        input_output_aliases={len(in_specs) - 1: 0},
        out_shape=[jax.ShapeDtypeStruct(dproj.shape, BF16), jax.ShapeDtypeStruct((1, D_MODEL), F32),
                   jax.ShapeDtypeStruct((1, N_STATE), F32), jax.ShapeDtypeStruct((1, N_STATE), F32),
                   jax.ShapeDtypeStruct((SSM_KB, BLK, ST_KB), F32), jax.ShapeDtypeStruct((SSM_KB, BLK, ST_KB), F32),
                   jax.ShapeDtypeStruct((SSM_KB, ST_KB, BLK), F32), jax.ShapeDtypeStruct((SSM_KB, ST_KB, BLK), F32)],
        scratch_shapes=[pltpu.VMEM((8, ST_KB), F32), pltpu.VMEM((8, ST_KB), F32),
                        pltpu.VMEM((LB_KB, q, BLK), F32), pltpu.VMEM((LB_KB, q, BLK), F32)],
        compiler_params=_params(dimension_semantics=("parallel", "arbitrary")),
    )(dz, y, u, s_re, s_im, wb_re, wb_im, wc_re, wc_im, d_skip, *tab_in, dproj)


def _discretize(lam_re, lam_im, log_dt, b_re, b_im):
    dt = jnp.exp(log_dt)[:, None]
    mag = jnp.exp(lam_re * dt)
    ar, ai = mag * jnp.cos(lam_im * dt), mag * jnp.sin(lam_im * dt)
    den = lam_re * lam_re + lam_im * lam_im
    nr, ni = ar - 1.0, ai
    fr, fi = (nr * lam_re + ni * lam_im) / den, (ni * lam_re - nr * lam_im) / den
    bbar_re = fr[..., None] * b_re - fi[..., None] * b_im
    bbar_im = fr[..., None] * b_im + fi[..., None] * b_re
    return ar, ai, bbar_re, bbar_im


def _block_diag_b(bbar):
    eye = jnp.eye(8, dtype=bbar.dtype)
    return jnp.einsum("kgpc,gh->kgchp", bbar.reshape(8, 8, SSM_STATE, SSM_GROUP_CH), eye).reshape(8, BLK, ST_KB)


def _block_diag_b_t(dwb):
    eye = jnp.eye(8, dtype=dwb.dtype)
    return jnp.einsum("kgchp,gh->kgpc", dwb.reshape(8, 8, SSM_GROUP_CH, 8, SSM_STATE), eye).reshape(
        SSM_GROUPS, SSM_STATE, SSM_GROUP_CH)


def _block_diag_c(c):
    eye = jnp.eye(8, dtype=c.dtype)
    return jnp.einsum("kgcp,gh->kgphc", c.reshape(8, 8, SSM_GROUP_CH, SSM_STATE), eye).reshape(8, ST_KB, BLK)


def _block_diag_c_t(dwc):
    eye = jnp.eye(8, dtype=dwc.dtype)
    return jnp.einsum("kgphc,gh->kgcp", dwc.reshape(8, 8, SSM_STATE, 8, SSM_GROUP_CH), eye).reshape(
        SSM_GROUPS, SSM_GROUP_CH, SSM_STATE)


def _powers(br, bi, n):
    pr, pi = br, bi
    cr, ci = br, bi
    while pr.shape[0] < n:
        pr, pi = (jnp.concatenate([pr, pr * cr - pi * ci], axis=0),
                  jnp.concatenate([pi, pr * ci + pi * cr], axis=0))
        cr, ci = cr * cr - ci * ci, 2.0 * cr * ci
    return pr[:n], pi[:n]


def _powers_desc(br, bi, n):
    pr, pi = br, bi
    cr, ci = br, bi
    while pr.shape[0] < n:
        pr, pi = (jnp.concatenate([pr * cr - pi * ci, pr], axis=0),
                  jnp.concatenate([pr * ci + pi * cr, pi], axis=0))
        cr, ci = cr * cr - ci * ci, 2.0 * cr * ci
    return pr, pi


def _power_tables(ar, ai, g):
    a1r, a1i = _powers(ar, ai, GRP)
    seqr, seqi = _powers(a1r[GRP - 1:], a1i[GRP - 1:], g)
    g2 = 1 << (g - 1).bit_length()
    revr, revi = _powers_desc(a1r[GRP - 1:], a1i[GRP - 1:], g2)
    revr, revi = revr[g2 - g:], revi[g2 - g:]
    sq_r, sq_i = [seqr[0:1]], [seqi[0:1]]
    while len(sq_r) < 8:
        r, i = sq_r[-1], sq_i[-1]
        sq_r.append(r * r - i * i)
        sq_i.append(2.0 * r * i)
    sqr, sqi = jnp.concatenate(sq_r, axis=0), jnp.concatenate(sq_i, axis=0)
    return a1r, a1i, sqr, sqi, seqr, seqi, revr, revi


HBM_SPEC = pl.BlockSpec(memory_space=pltpu.HBM)
SEM_SPEC = pl.BlockSpec(memory_space=pltpu.SEMAPHORE)
DATAFLOW = pltpu.SideEffectType.DATAFLOW_SIDE_EFFECTING


def _plain_rows(p, m):
    return p * m


def _ffn_in_rows(p, m):
    return ((p & 3) >> 1) * (4 * m) + (p >> 2) * (2 * m) + (p & 1) * m


def _peer_copies(src_refs, land_refs, send_sems, recv_sems, chunked, row_fns):
    x, y, c = lax.axis_index("x"), lax.axis_index("y"), lax.axis_index("c")
    me = 4 * x + 2 * y + c
    copies = []
    for a, (src, land) in enumerate(zip(src_refs, land_refs)):
        m = land.shape[0] // N_DEV
        for k in range(N_DEV - 1):
            rel = k + 1
            bx, by, bc = (rel >> 2) & 1, (rel >> 1) & 1, rel & 1
            peer = (x + bx - 2 * x * bx, y + by - 2 * y * by, c + bc - 2 * c * bc)
            p_idx = 4 * peer[0] + 2 * peer[1] + peer[2]
            copies.append(pltpu.make_async_remote_copy(
                src_ref=src.at[pl.ds(row_fns[a](p_idx, m), m), :] if chunked else src,
                dst_ref=land.at[pl.ds(me * m if chunked else row_fns[a](me, m), m), :],
                send_sem=send_sems[a * (N_DEV - 1) + k], recv_sem=recv_sems[a * (N_DEV - 1) + k],
                device_id=peer, device_id_type=MESH))
    return copies


def _own_copies(src_refs, land_outs, sems, chunked, row_fns):
    me = _my_index()
    copies = []
    for a, (src, land) in enumerate(zip(src_refs, land_outs)):
        m = land.shape[0] // N_DEV
        rows = row_fns[a](me, m)
        copies.append(pltpu.make_async_copy(src.at[pl.ds(rows, m), :] if chunked else src,
                                            land.at[pl.ds(me * m if chunked else rows, m), :], sems[a]))
    return copies


def _send_start(name, srcs, lands, chunked, row_fns=None):
    n = len(srcs)
    ns = n * (N_DEV - 1)
    row_fns = row_fns or [_plain_rows] * n

    def body(*refs):
        src_refs, land_refs = refs[:n], refs[n:2 * n]
        send_sems, own_sems = refs[2 * n:2 * n + ns], refs[2 * n + ns:2 * n + ns + n]
        recv_sems = refs[2 * n + ns + n:2 * n + 2 * ns + n]
        land_outs, token = refs[-1 - n:-1], refs[-1]
        for cp in _own_copies(src_refs, land_outs, own_sems, chunked, row_fns):
            cp.start()
        for cp in _peer_copies(src_refs, land_refs, send_sems, recv_sems, chunked, row_fns):
            cp.start()
        token[...] = jnp.zeros_like(token)

    ins = [pltpu.with_memory_space_constraint(a, pltpu.HBM) for a in list(srcs) + list(lands)]
    n_sems = 2 * ns + n
    out = pl.pallas_call(
        body, name=name,
        in_specs=[HBM_SPEC] * (2 * n),
        out_specs=[SEM_SPEC] * n_sems + [HBM_SPEC] * (2 * n) + [pl.BlockSpec(memory_space=pltpu.VMEM)],
        out_shape=[pltpu.SemaphoreType.DMA(())] * n_sems
        + [pltpu.HBM(a.shape, a.dtype) for a in list(srcs) + list(lands)]
        + [jax.ShapeDtypeStruct((8, BLK), F32)],
        input_output_aliases={i: i + n_sems for i in range(2 * n)},
        compiler_params=pltpu.CompilerParams(has_side_effects=DATAFLOW),
    )(*ins)
    return (out[:ns + n], out[ns + n:n_sems], out[n_sems:n_sems + n], out[n_sems + n:n_sems + 2 * n], out[-1])


def _send_wait(name, send_sems, recv_sems, srcs, lands, after, chunked, row_fns=None):
    n = len(srcs)
    ns = n * (N_DEV - 1)
    row_fns = row_fns or [_plain_rows] * n
    after = list(after) if isinstance(after, (list, tuple)) else [after]

    def body(*refs):
        src_refs, land_refs = refs[:n], refs[n:2 * n]
        s_sems, own_sems = refs[2 * n:2 * n + ns], refs[2 * n + ns:2 * n + ns + n]
        r_sems = refs[2 * n + ns + n:2 * n + 2 * ns + n]
        land_outs = refs[-n:]
        for cp in _own_copies(src_refs, land_outs, own_sems, chunked, row_fns):
            cp.wait()
        copies = _peer_copies(src_refs, land_refs, s_sems, r_sems, chunked, row_fns)
        for cp in copies:
            cp.wait_send()
        for cp in copies:
            cp.wait_recv()

    out = pl.pallas_call(
        body, name=name,
        in_specs=[HBM_SPEC] * (2 * n) + [SEM_SPEC] * (2 * ns + n) + [pl.BlockSpec(memory_space=pl.ANY)] * len(after),
        out_specs=[HBM_SPEC] * (2 * n),
        out_shape=[pltpu.HBM(a.shape, a.dtype) for a in list(srcs) + list(lands)],
        input_output_aliases={i: i for i in range(2 * n)},
        compiler_params=pltpu.CompilerParams(has_side_effects=DATAFLOW),
    )(*srcs, *lands, *send_sems, *recv_sems, *after)
    return out[n:]


def _sum_slots(name, recv, own):
    m, ncol = own.shape
    tr = m // 2 if (m // 2) % 16 == 0 else m
    g = m // tr

    def body(*refs):
        slots, own_ref, o_ref = refs[:N_DEV], refs[N_DEV], refs[N_DEV + 1]
        me = _my_index()
        tot = None
        for s in range(N_DEV):
            v = jnp.where(me == s, own_ref[...], slots[s][...].astype(F32))
            tot = v if tot is None else tot + v
        o_ref[...] = tot

    def slot_spec(s):
        return pl.BlockSpec((tr, ncol), lambda i: (s * g + i, 0))

    return pl.pallas_call(
        body, name=name, grid=(g,),
        in_specs=[slot_spec(s) for s in range(N_DEV)] + [pl.BlockSpec((tr, ncol), lambda i: (i, 0))],
        out_specs=pl.BlockSpec((tr, ncol), lambda i: (i, 0)),
        out_shape=jax.ShapeDtypeStruct((m, ncol), F32),
        compiler_params=_params(),
    )(*([recv] * N_DEV), own)


def _sum_gathered(name, gathered, rows):
    tr = _pick(rows, 512, 8)
    g = rows // tr

    def body(*refs):
        o_ref = refs[N_DEV]
        tot = refs[0][...]
        for s in range(1, N_DEV):
            tot = tot + refs[s][...]
        o_ref[...] = tot

    return pl.pallas_call(
        body, name=name, grid=(g,),
        in_specs=[pl.BlockSpec((tr, BLK), (lambda i, s=s: (s * g + i, 0))) for s in range(N_DEV)],
        out_specs=pl.BlockSpec((tr, BLK), lambda i: (i, 0)),
        out_shape=jax.ShapeDtypeStruct((rows, BLK), F32),
        compiler_params=_params(),
    )(*([gathered] * N_DEV))


def _adamw(name, w, g, m, v):
    r, c = w.shape
    tr = _pick(r, 256, 8) if r % 8 == 0 else r
    c1 = 1.0 - ADAM_B1 ** ADAM_STEP
    c2 = 1.0 - ADAM_B2 ** ADAM_STEP

    def body(w_ref, g_ref, m_ref, v_ref, d_ref, nm_ref, nv_ref):
        gv = g_ref[...]
        nm = ADAM_B1 * m_ref[...] + (1.0 - ADAM_B1) * gv
        nv = ADAM_B2 * v_ref[...] + (1.0 - ADAM_B2) * (gv * gv)
        m_hat = nm / c1
        v_hat = nv / c2
        d_ref[...] = -ADAM_LR * (m_hat / (jnp.sqrt(v_hat) + ADAM_EPS) + ADAM_WD * w_ref[...])
        nm_ref[...] = nm
        nv_ref[...] = nv

    spec = pl.BlockSpec((tr, c), lambda i: (i, 0))
    shape = jax.ShapeDtypeStruct((r, c), F32)
    return pl.pallas_call(
        body, name=name, grid=(r // tr,),
        in_specs=[spec] * 4, out_specs=[spec] * 3, out_shape=[shape] * 3,
        compiler_params=_params(),
    )(w, g, m, v)


def _adamw_many(name, ws, gs, ms, vs):
    n = len(ws)
    c1 = 1.0 - ADAM_B1 ** ADAM_STEP
    c2 = 1.0 - ADAM_B2 ** ADAM_STEP

    def body(*refs):
        for a in range(n):
            w_ref, g_ref, m_ref, v_ref = refs[a], refs[n + a], refs[2 * n + a], refs[3 * n + a]
            d_ref, nm_ref, nv_ref = refs[4 * n + a], refs[5 * n + a], refs[6 * n + a]
            gv = g_ref[...]
            nm = ADAM_B1 * m_ref[...] + (1.0 - ADAM_B1) * gv
            nv = ADAM_B2 * v_ref[...] + (1.0 - ADAM_B2) * (gv * gv)
            d_ref[...] = -ADAM_LR * ((nm / c1) / (jnp.sqrt(nv / c2) + ADAM_EPS) + ADAM_WD * w_ref[...])
            nm_ref[...] = nm
            nv_ref[...] = nv

    specs = [_full_spec(w.shape) for w in ws]
    shapes = [jax.ShapeDtypeStruct(w.shape, F32) for w in ws]
    out = pl.pallas_call(
        body, name=name, grid=(1,),
        in_specs=specs * 4, out_specs=specs * 3, out_shape=shapes * 3,
        compiler_params=_params(),
    )(*ws, *gs, *ms, *vs)
    return out[:n], out[n:2 * n], out[2 * n:]


PACK_ROWS = 128


def _pack(parts):
    flat = []
    for p in parts:
        v = p.reshape(-1)
        flat.append(jnp.pad(v, (0, (-v.shape[0]) % BLK)))
    v = jnp.concatenate(flat)
    v = jnp.pad(v, (0, (-v.shape[0]) % (PACK_ROWS * BLK)))
    return v.reshape(-1, BLK)


def _unpack(buf, shapes):
    flat = buf.reshape(-1)
    out, off = [], 0
    for shp in shapes:
        size = math.prod(shp)
        out.append(flat[off:off + size].reshape(shp))
        off += size + (-size) % BLK
    return out


def kernel(x, meta_tokens, norm_mix, w_in, q_norm, k_norm, attn_sinks, lam_re, lam_im, log_dt, ssm_b_re, ssm_b_im, ssm_c_re, ssm_c_im, ssm_d, w_glu, attn_branch_norm, ssm_branch_norm, w_out, norm_ffn, w_ffn_in, w_ffn_out, loss_target, m_meta_tokens, m_norm_mix, m_w_in, m_q_norm, m_k_norm, m_attn_sinks, m_lam_re, m_lam_im, m_log_dt, m_ssm_b_re, m_ssm_b_im, m_ssm_c_re, m_ssm_c_im, m_ssm_d, m_w_glu, m_attn_branch_norm, m_ssm_branch_norm, m_w_out, m_norm_ffn, m_w_ffn_in, m_w_ffn_out, v_meta_tokens, v_norm_mix, v_w_in, v_q_norm, v_k_norm, v_attn_sinks, v_lam_re, v_lam_im, v_log_dt, v_ssm_b_re, v_ssm_b_im, v_ssm_c_re, v_ssm_c_im, v_ssm_d, v_w_glu, v_attn_branch_norm, v_ssm_branch_norm, v_w_out, v_norm_ffn, v_w_ffn_in, v_w_ffn_out):
    args = dict(locals())
    weights = {n: args[n] for n in WEIGHTS}
    mom_m = {n: args["m_" + n] for n in WEIGHTS}
    mom_v = {n: args["v_" + n] for n in WEIGHTS}

    x2d = x[0]
    target2d = loss_target[0]
    s_len = x2d.shape[0]
    l_dim = s_len + BLK
    tm_row = _pick(l_dim, 320)
    tm_mm = _pick(l_dim, 1040)
    tl_tn = _pick(l_dim, 2080)
    tm_ffn = _pick(l_dim, 640)
    tm_big = _pick(l_dim, 2080)
    tm_shift = _pick(l_dim, 640, BLK)

    shard_in = w_in[0].T.astype(BF16)
    shard_glu = w_glu[0].T.astype(BF16)
    shard_out = w_out[0].astype(BF16)
    shard_ffn_in = w_ffn_in[0].T.astype(BF16)
    shard_ffn_out = w_ffn_out[0].astype(BF16)
    shard_meta = meta_tokens.T
    me = _my_index()

    def landing(shard):
        return lax.empty((N_DEV * shard.shape[0], shard.shape[1]), shard.dtype)

    first = [shard_in, shard_meta]
    ga = _send_start("gather_start_a", first, [landing(s) for s in first], chunked=False)
    later = [shard_glu + ga[4][0:1, 0:1].astype(BF16), shard_out, shard_ffn_in, shard_ffn_out]
    later_fns = [_plain_rows, _plain_rows, _ffn_in_rows, _plain_rows]
    gb = _send_start("gather_start_b", later, [landing(s) for s in later], chunked=False,
                     row_fns=later_fns)

    nm_t = norm_mix + (ga[4][0:1, 0:1] + gb[4][0:1, 0:1])
    qn_t, kn_t = jnp.tile(q_norm, (1, N_Q_HEADS)), jnp.tile(k_norm, (1, N_KV_HEADS))
    e_mat = jnp.kron(jnp.eye(4, dtype=F32), jnp.ones((HEAD_DIM, HEAD_DIM), F32)).astype(BF16)

    def disc(lr, li, ldt, br, bi):
        return _discretize(lr[0], li[0], ldt[0], br[0], bi[0])

    (abar_re, abar_im, bbar_re, bbar_im), disc_vjp = jax.vjp(disc, lam_re, lam_im, log_dt, ssm_b_re, ssm_b_im)
    wb_re, wb_im = _block_diag_b(bbar_re).astype(BF16), _block_diag_b(bbar_im).astype(BF16)
    wc_re, wc_im = _block_diag_c(ssm_c_re[0]).astype(BF16), _block_diag_c(ssm_c_im[0]).astype(BF16)
    q_ssm = _pick(l_dim, 640, 64)
    tabs = _power_tables(abar_re.reshape(1, N_STATE), abar_im.reshape(1, N_STATE), q_ssm // GRP)

    h0, xn = _embed_norm(x2d, nm_t, tm_shift)
    wt_in, meta_t = _send_wait("gather_wait_a", ga[0], ga[1], ga[2], ga[3],
                               [xn, wb_re, wb_im, wc_re, wc_im, *tabs, e_mat, qn_t, kn_t], chunked=False)
    meta_pad = jnp.pad(meta_t.T, ((PAD, 0), (0, 0)))
    h0, xn = _embed_meta(meta_pad, nm_t, h0, xn)
    qkv_w, u_end = Q_W + 2 * KV_W, Q_W + 2 * KV_W + D_MODEL
    wt_in_p = jnp.concatenate([wt_in[u_end:], wt_in[qkv_w:u_end], wt_in[:qkv_w]], axis=0)
    qkv = _matmul("proj_qkv", xn, wt_in_p, nt=True, tm=tm_big, tn=qkv_w, tk=D_MODEL, n=qkv_w, w_off=2)
    u = _matmul("proj_u", xn, wt_in_p, nt=True, tm=tm_big, tn=D_MODEL, tk=D_MODEL, n=D_MODEL, w_off=2)
    gates = _matmul("proj_gates", xn, wt_in_p, nt=True, tm=tm_big, tn=D_MODEL, tk=D_MODEL, n=2 * D_MODEL, w_off=0,
                    out_dtype=BF16)
    qn, kf, vf = _qk_prep(qkv, qn_t, kn_t, e_mat, tm_row)
    attn, lse = _attn_fwd(qn, kf, vf, attn_sinks)
    y, z, s_re, s_im = _ssm_fwd_kb(u, wb_re, wb_im, wc_re, wc_im, ssm_d, tabs, q_ssm)
    wt_glu, w_out_f, wt_ffn_in, w_ffn_out_f = _send_wait("gather_wait_b", gb[0], gb[1], gb[2], gb[3], z,
                                                         chunked=False, row_fns=later_fns)
    zab = _matmul("glu_proj", z, wt_glu, nt=True, tm=tm_big, tn=1024, tk=D_MODEL, out_dtype=BF16)
    merged = _merge_fwd(attn, zab, gates, attn_branch_norm, ssm_branch_norm, tm_row)
    h1, hn = _matmul("out_proj", merged, w_out_f, nt=False, tm=tm_mm, tn=1024, tk=D_MODEL, res=h0,
                     norm_g=norm_ffn)
    gu, act = _ffn_in_swiglu(hn, wt_ffn_in, tm_ffn)
    h2 = _matmul("ffn_out", act, w_ffn_out_f, nt=False, tm=tm_mm, tn=1024, tk=D_FF, res=h1)
    dh2, dh2_b, loss_part = _loss_grad(h2, target2d, tm_shift)

    dgu = _d_act_swiglu(dh2_b, w_ffn_out_f, gu, tm_ffn)

    def exchange_start(name, grads_b, row_fns=None):
        return _send_start(name, grads_b, [lax.empty(g.shape, BF16) for g in grads_b], chunked=True,
                           row_fns=row_fns)

    g_ffn_out, g_ffn_out_b = _matmul_tn("g_ffn_out", act, dh2_b, tm=1408, tn=1024, tl=tl_tn)
    g_ffn_in_t, g_ffn_in_b = _matmul_tn("g_ffn_in", dgu, hn, tm=1408, tn=1024, tl=tl_tn)
    ffn_fns = [_ffn_in_rows, _plain_rows]
    ex1 = exchange_start("exchange_start_ffn", [g_ffn_in_b, g_ffn_out_b], ffn_fns)
    dhn = _matmul("d_hn", dgu, wt_ffn_in, nt=False, tm=tm_ffn, tn=1024, tk=2 * D_FF, out_dtype=BF16)
    dh1, dh1_b, g_norm_ffn = _norm_bwd_res("ffn_norm_bwd", h1, norm_ffn + ex1[4][0:1, 0:1], dhn, dh2, tm_row)
    dmerged = _matmul("d_merged", dh1_b, w_out_f, nt=True, tm=tm_big, tn=1024, tk=D_MODEL, out_dtype=BF16)
    dattn, dzab, dproj, g_abn, g_sbn = _merge_bwd(attn, zab, gates, attn_branch_norm, ssm_branch_norm,
                                                    dmerged, tm_row)
    g_out, g_out_b = _matmul_tn("g_out", merged, dh1_b, tm=1024, tn=1024, tl=tl_tn)
    dz = _matmul("d_z", dzab, wt_glu, nt=False, tm=tm_mm, tn=1024, tk=2 * D_MODEL, out_dtype=BF16)
    g_glu_t, g_glu_b = _matmul_tn("g_glu", dzab, z, tm=1024, tn=1024, tl=tl_tn)
    ex2 = exchange_start("exchange_start_mix", [g_glu_b, g_out_b])
    dproj, g_ssm_d, g_ar, g_ai, g_wbr, g_wbi, g_wcr, g_wci = _ssm_bwd_kb(
        dz, y, u, s_re, s_im, wb_re, wb_im, wc_re, wc_im, ssm_d + ex2[4][0:1, 0:1], tabs, q_ssm, dproj)
    dq, dkc, dkp, dkm, dvc, dvp, dvm, g_sinks = _attn_bwd(qn, kf, vf, attn_sinks, lse, attn, dattn)
    dproj, g_qn_t, g_kn_t = _qk_bwd(qkv, qn_t, kn_t, e_mat, dq, dkc, dkp, dkm, dvc, dvp, dvm, dproj)
    g_lam_re, g_lam_im, g_log_dt, g_b_re, g_b_im = disc_vjp(
        (g_ar.reshape(SSM_GROUPS, SSM_STATE), g_ai.reshape(SSM_GROUPS, SSM_STATE),
         _block_diag_b_t(g_wbr), _block_diag_b_t(g_wbi)))
    small_grads = {
        "q_norm": g_qn_t.reshape(N_Q_HEADS, HEAD_DIM).sum(0)[None],
        "k_norm": g_kn_t.reshape(N_KV_HEADS, HEAD_DIM).sum(0)[None], "attn_sinks": g_sinks,
        "lam_re": g_lam_re, "lam_im": g_lam_im, "log_dt": g_log_dt, "ssm_b_re": g_b_re, "ssm_b_im": g_b_im,
        "ssm_c_re": _block_diag_c_t(g_wcr)[None], "ssm_c_im": _block_diag_c_t(g_wci)[None],
        "ssm_d": g_ssm_d, "attn_branch_norm": g_abn, "ssm_branch_norm": g_sbn, "norm_ffn": g_norm_ffn,
    }
    early = [n for n in SMALL if n != "norm_mix"]
    packed_e = _pack([small_grads[n] for n in early])
    gs_e = _send_start("small_start_a", [packed_e], [landing(packed_e)], chunked=False)

    blocks_g, blocks_u, blocks_q = 2 * D_MODEL // 512, D_MODEL // 512, qkv_w // 512
    back = lambda i: jnp.where(i < blocks_g, i + blocks_q + blocks_u,
                               jnp.where(i < blocks_g + blocks_u, i - blocks_g + blocks_q, i - blocks_g - blocks_u))
    g_in_t, g_in_b = _matmul_tn("g_in", dproj, xn, tm=512, tn=1024, tl=tl_tn, after=gs_e[4], out_rows=back)
    ex3 = exchange_start("exchange_start_in", [g_in_b])
    dxn = _matmul("d_xn", dproj, wt_in_p, nt=False, tm=tm_ffn, tn=1024, tk=IN_COLS, out_dtype=BF16)
    grad_x2d, dmeta_blk, g_norm_mix = _final_bwd(h0, nm_t + ex3[4][0:1, 0:1], dxn, dh1, _pick(s_len, 512, BLK))
    packed_l = _pack([g_norm_mix, dmeta_blk[PAD:], loss_part])
    gs_l = _send_start("small_start_b", [packed_l], [landing(packed_l)], chunked=False)
    grads, deltas, new_m, new_v = {}, {}, {}, {}

    recv_ffn_in, recv_ffn_out = _send_wait("exchange_wait_ffn", ex1[0], ex1[1], ex1[2], ex1[3], gs_l[4],
                                           chunked=True, row_fns=ffn_fns)
    recv_glu, recv_out = _send_wait("exchange_wait_mix", ex2[0], ex2[1], ex2[2], ex2[3], recv_ffn_in,
                                    chunked=True)
    (recv_in,) = _send_wait("exchange_wait_in", ex3[0], ex3[1], ex3[2], ex3[3], recv_glu, chunked=True)
    big = [("w_in", g_in_t, True, recv_in, _plain_rows), ("w_glu", g_glu_t, True, recv_glu, _plain_rows),
           ("w_out", g_out, False, recv_out, _plain_rows), ("w_ffn_in", g_ffn_in_t, True, recv_ffn_in, _ffn_in_rows),
           ("w_ffn_out", g_ffn_out, False, recv_ffn_out, _plain_rows)]
    for name, g_full, transposed, recv, row_fn in big:
        m_rows = g_full.shape[0] // N_DEV
        own = lax.dynamic_slice(g_full, (row_fn(me, m_rows), 0), (m_rows, g_full.shape[1]))
        g_shard = _sum_slots("sum_" + name, recv, own)
        grads[name] = (g_shard.T if transposed else g_shard)[None]

    def adamw_2d(name):
        shp = weights[name].shape
        if name in ("w_in", "w_ffn_in"):
            as2d, back = (lambda a: a.reshape(shp[-2], shp[-1]).T), (lambda a: a.T.reshape(shp))
        else:
            as2d, back = (lambda a: a.reshape(shp[-2], shp[-1])), (lambda a: a.reshape(shp))
        d, nm, nv = _adamw("adamw_" + name, as2d(weights[name]), as2d(grads[name]), as2d(mom_m[name]),
                           as2d(mom_v[name]))
        deltas[name], new_m[name], new_v[name] = back(d), back(nm), back(nv)
        return d

    for name in ["w_in", "w_glu", "w_out", "w_ffn_in", "w_ffn_out"]:
        last = adamw_2d(name)

    def small_sum(tag, gs, packed, after):
        (gathered,) = _send_wait("small_wait_" + tag, gs[0], gs[1], gs[2], gs[3], after, chunked=False)
        return _sum_gathered("sum_small_" + tag, gathered, packed.shape[0])

    def small_adamw(tag, names):
        view = lambda n, a: jnp.swapaxes(a, -1, -2) if n.startswith("ssm_b") else a
        d, nm, nv = _adamw_many("adamw_small_" + tag, [view(n, weights[n]) for n in names],
                                [view(n, grads[n]) for n in names], [view(n, mom_m[n]) for n in names],
                                [view(n, mom_v[n]) for n in names])
        deltas.update((n, view(n, a)) for n, a in zip(names, d))
        new_m.update((n, view(n, a)) for n, a in zip(names, nm))
        new_v.update((n, view(n, a)) for n, a in zip(names, nv))

    g_sum_e = small_sum("a", gs_e, packed_e, last)
    grads.update(zip(early, _unpack(g_sum_e, [weights[n].shape for n in early])))
    wide = [n for n in early if n.startswith(("ssm_b", "ssm_c"))]
    small_adamw("wide", wide)
    g_sum_l = small_sum("b", gs_l, packed_l, g_sum_e)
    grads["norm_mix"], g_meta, loss_sum = _unpack(g_sum_l, [weights["norm_mix"].shape, (N_META, D_MODEL), (1, 1)])
    small_adamw("rest", [n for n in SMALL if n not in wide])
    grads["meta_tokens"] = lax.dynamic_slice(g_meta, (0, me * BLK), (N_META, BLK))
    adamw_2d("meta_tokens")

    loss = loss_sum[0, 0]
    return (loss, grad_x2d[None], *[grads[n] for n in WEIGHTS], *[deltas[n] for n in WEIGHTS],
            *[new_m[n] for n in WEIGHTS], *[new_v[n] for n in WEIGHTS])
```

```python
import math

import jax
import jax.numpy as jnp
from jax import lax
from jax.experimental import pallas as pl
from jax.experimental.pallas import tpu as pltpu

F32 = jnp.float32
BF16 = jnp.bfloat16

D_MODEL = 1024
N_META = 16
HEAD_DIM = 64
N_Q_HEADS = 16
N_KV_HEADS = 4
Q_W = N_Q_HEADS * HEAD_DIM
KV_W = N_KV_HEADS * HEAD_DIM
SSM_GROUPS = 64
SSM_GROUP_CH = 16
SSM_STATE = 64
N_STATE = SSM_GROUPS * SSM_STATE
D_FF = 2816
IN_COLS = Q_W + 2 * KV_W + 3 * D_MODEL
EPS = 1e-6
BLK = 128
PAD = BLK - N_META
N_DEV = 8
NEG = -1e30
SSM_KB = 8
ST_KB = N_STATE // SSM_KB
LB_KB = ST_KB // BLK
N_LB = N_STATE // BLK

ADAM_LR = 0.001
ADAM_B1 = 0.9
ADAM_B2 = 0.999
ADAM_EPS = 1e-08
ADAM_WD = 0.01
ADAM_STEP = 10

VMEM_LIMIT = 48 * 1024 * 1024
MESH = pl.DeviceIdType.MESH

SMALL = ["norm_mix", "q_norm", "k_norm", "attn_sinks", "lam_re", "lam_im", "log_dt", "ssm_b_re", "ssm_b_im",
         "ssm_c_re", "ssm_c_im", "ssm_d", "attn_branch_norm", "ssm_branch_norm", "norm_ffn"]
WEIGHTS = ["meta_tokens", "norm_mix", "w_in", "q_norm", "k_norm", "attn_sinks", "lam_re", "lam_im", "log_dt",
           "ssm_b_re", "ssm_b_im", "ssm_c_re", "ssm_c_im", "ssm_d", "w_glu", "attn_branch_norm",
           "ssm_branch_norm", "w_out", "norm_ffn", "w_ffn_in", "w_ffn_out"]


def _params(**kw):
    return pltpu.CompilerParams(vmem_limit_bytes=VMEM_LIMIT, **kw)


def _pick(n, cap, mult=16):
    best = None
    for d in range(mult, min(n, cap) + 1, mult):
        if n % d == 0:
            best = d
    assert best is not None, (n, cap, mult)
    return best


def _my_index():
    return 4 * lax.axis_index("x") + 2 * lax.axis_index("y") + lax.axis_index("c")


def _rms(x, g):
    r = lax.rsqrt(jnp.mean(x * x, axis=-1, keepdims=True) + EPS)
    return x * r * g


def _rms_bwd(x, g, dy):
    r = lax.rsqrt(jnp.mean(x * x, axis=-1, keepdims=True) + EPS)
    t = dy * g
    dx = r * t - x * (r * r * r) * jnp.mean(t * x, axis=-1, keepdims=True)
    dg = jnp.sum(dy * (x * r), axis=0, keepdims=True)
    return dx, dg


def _sigmoid(x):
    return jax.nn.sigmoid(x)


def _gelu(x):
    k = math.sqrt(2.0 / math.pi)
    return 0.5 * x * (1.0 + jnp.tanh(k * (x + 0.044715 * (x * x * x))))


def _gelu_grad(x):
    k = math.sqrt(2.0 / math.pi)
    t = jnp.tanh(k * (x + 0.044715 * (x * x * x)))
    return 0.5 * (1.0 + t) + 0.5 * x * (1.0 - t * t) * (k * (1.0 + 3.0 * 0.044715 * (x * x)))


def _head_mean(x, e_ref):
    hi = x.astype(BF16)
    lo = (x - hi.astype(F32)).astype(BF16)
    e = e_ref[...]
    out = []
    for b in range(x.shape[1] // 256):
        sl = slice(256 * b, 256 * b + 256)
        s = (jnp.dot(hi[:, sl], e, preferred_element_type=F32)
             + jnp.dot(lo[:, sl], e, preferred_element_type=F32))
        out.append(s)
    s = out[0] if len(out) == 1 else jnp.concatenate(out, axis=1)
    return s * (1.0 / HEAD_DIM)


def _head_rms(x, g, e_ref):
    r = lax.rsqrt(_head_mean(x * x, e_ref) + EPS)
    return x * r * g


def _head_rms_bwd(x, g, dy, e_ref):
    r = lax.rsqrt(_head_mean(x * x, e_ref) + EPS)
    t = dy * g
    dx = r * t - x * (r * r * r) * _head_mean(t * x, e_ref)
    dg = jnp.sum(dy * (x * r), axis=0, keepdims=True)
    return dx, dg


def _lane_half(shape):
    lane = lax.broadcasted_iota(jnp.int32, shape, len(shape) - 1)
    return (lane >> 6) & 1


def _matmul(name, a, w, *, nt, tm, tn, tk, n=None, w_off=0, res=None, norm_g=None, out_dtype=F32):
    m_dim, k_dim = a.shape
    n_dim = n if n is not None else (w.shape[0] if nt else w.shape[1])
    gm, gn, gk = m_dim // tm, n_dim // tn, k_dim // tk
    assert gm * tm == m_dim and gn * tn == n_dim and gk * tk == k_dim, (name, a.shape, w.shape, tm, tn, tk)
    assert norm_g is None or tn == n_dim
    direct = out_dtype == F32
    dn = (((1,), (1,)), ((), ())) if nt else (((1,), (0,)), ((), ()))

    def body(*refs):
        refs = list(refs)
        a_ref, w_ref = refs[0], refs[1]
        pos = 2
        r_ref = g_ref = on_ref = None
        if res is not None:
            r_ref, pos = refs[pos], pos + 1
        if norm_g is not None:
            g_ref, pos = refs[pos], pos + 1
        o_ref, pos = refs[pos], pos + 1
        if norm_g is not None:
            on_ref, pos = refs[pos], pos + 1
        part = lax.dot_general(a_ref[...], w_ref[...], dn, preferred_element_type=F32)
        if gk == 1:
            r = part if r_ref is None else r_ref[...] + part
            o_ref[...] = r.astype(out_dtype)
            if on_ref is not None:
                on_ref[...] = _rms(r, g_ref[...]).astype(BF16)
            return
        acc = o_ref if direct else refs[pos]
        k = pl.program_id(2)

        @pl.when(k == 0)
        def _():
            acc[...] = part if r_ref is None or not direct else r_ref[...] + part

        @pl.when(k > 0)
        def _():
            acc[...] += part

        @pl.when(k == gk - 1)
        def _():
            if not direct:
                r = acc[...]
                if r_ref is not None:
                    r = r_ref[...] + r
                o_ref[...] = r.astype(out_dtype)
            if on_ref is not None:
                on_ref[...] = _rms(o_ref[...].astype(F32), g_ref[...]).astype(BF16)

    if nt:
        w_spec = pl.BlockSpec((tn, tk), lambda i, j, k: (j + w_off, k))
    else:
        w_spec = pl.BlockSpec((tk, tn), lambda i, j, k: (k, j))
    in_specs = [pl.BlockSpec((tm, tk), lambda i, j, k: (i, k)), w_spec]
    args = [a, w]
    out_spec = pl.BlockSpec((tm, tn), lambda i, j, k: (i, j))
    out_specs, out_shape = [out_spec], [jax.ShapeDtypeStruct((m_dim, n_dim), out_dtype)]
    if res is not None:
        in_specs.append(out_spec)
        args.append(res)
    if norm_g is not None:
        in_specs.append(pl.BlockSpec((1, tn), lambda i, j, k: (0, 0)))
        args.append(norm_g)
        out_specs.append(out_spec)
        out_shape.append(jax.ShapeDtypeStruct((m_dim, n_dim), BF16))
    out = pl.pallas_call(
        body, name=name, grid=(gm, gn, gk),
        in_specs=in_specs, out_specs=out_specs, out_shape=out_shape,
        scratch_shapes=[] if direct or gk == 1 else [pltpu.VMEM((tm, tn), F32)],
        compiler_params=_params(dimension_semantics=("parallel", "parallel", "arbitrary")),
    )(*args)
    return out if norm_g is not None else out[0]


def _matmul_tn(name, a, b, *, tm, tn, tl, after=None, out_rows=None):
    l_dim, m_dim = a.shape
    n_dim = b.shape[1]
    gm, gn, gl = m_dim // tm, n_dim // tn, l_dim // tl
    assert gm * tm == m_dim and gn * tn == n_dim and gl * tl == l_dim, (name, a.shape, b.shape, tm, tn, tl)

    def body(*refs):
        a_ref, b_ref = refs[0], refs[1]
        o_ref, ob_ref = refs[-2], refs[-1]

        @pl.when(pl.program_id(2) == 0)
        def _():
            o_ref[...] = jnp.zeros_like(o_ref)

        o_ref[...] += lax.dot_general(a_ref[...], b_ref[...], (((0,), (0,)), ((), ())),
                                      preferred_element_type=F32)

        @pl.when(pl.program_id(2) == gl - 1)
        def _():
            ob_ref[...] = o_ref[...].astype(BF16)

    out_row = out_rows if out_rows is not None else (lambda i: i)
    out_spec = pl.BlockSpec((tm, tn), lambda i, j, l: (out_row(i), j))
    in_specs = [pl.BlockSpec((tl, tm), lambda i, j, l: (l, i)), pl.BlockSpec((tl, tn), lambda i, j, l: (l, j))]
    args = [a, b]
    if after is not None:
        in_specs.append(pl.BlockSpec(memory_space=pl.ANY))
        args.append(after)
    return pl.pallas_call(
        body, name=name, grid=(gm, gn, gl),
        in_specs=in_specs,
        out_specs=[out_spec, out_spec],
        out_shape=[jax.ShapeDtypeStruct((m_dim, n_dim), F32), jax.ShapeDtypeStruct((m_dim, n_dim), BF16)],
        compiler_params=_params(dimension_semantics=("parallel", "parallel", "arbitrary")),
    )(*args)


def _row_spec(tm, cols, f=None):
    if f is None:
        return pl.BlockSpec((tm, cols), lambda i: (i, 0))
    return pl.BlockSpec((tm, cols), lambda i: (f(i), 0))


def _full_spec(shape):
    nd = len(shape)
    return pl.BlockSpec(shape, lambda i: (0,) * nd)


def _shifted_specs(n_sub, n_blocks):
    return [_row_spec(BLK, D_MODEL, (lambda i, k=k: jnp.clip(n_sub * i - 1 + k, 0, n_blocks - 1)))
            for k in range(n_sub)]


def _embed_norm(x2d, g, tm):
    s_len = x2d.shape[0]
    l_dim = s_len + BLK
    n_sub = tm // BLK

    def body(*refs):
        x_refs, g_ref, h_ref, xn_ref = refs[:n_sub], refs[n_sub], refs[n_sub + 1], refs[n_sub + 2]
        i = pl.program_id(0)
        for k in range(n_sub):
            rows = slice(BLK * k, BLK * k + BLK)
            h = x_refs[k][...] * jnp.where(n_sub * i + k >= 1, 1.0, 0.0)
            h_ref[rows, :] = h
            xn_ref[rows, :] = _rms(h, g_ref[...]).astype(BF16)

    return pl.pallas_call(
        body, name="embed_norm", grid=(l_dim // tm,),
        in_specs=_shifted_specs(n_sub, s_len // BLK) + [_full_spec((1, D_MODEL))],
        out_specs=[_row_spec(tm, D_MODEL), _row_spec(tm, D_MODEL)],
        out_shape=[jax.ShapeDtypeStruct((l_dim, D_MODEL), F32),
                   jax.ShapeDtypeStruct((l_dim, D_MODEL), BF16)],
        compiler_params=_params(),
    )(*([x2d] * n_sub), g)


def _embed_meta(meta_pad, g, h0, xn):
    def body(mp_ref, g_ref, h_in, xn_in, h_ref, xn_ref):
        h_ref[...] = mp_ref[...]
        xn_ref[...] = _rms(mp_ref[...], g_ref[...]).astype(BF16)

    any_spec = pl.BlockSpec(memory_space=pl.ANY)
    return pl.pallas_call(
        body, name="embed_meta", grid=(1,),
        in_specs=[_full_spec((BLK, D_MODEL)), _full_spec((1, D_MODEL)), any_spec, any_spec],
        out_specs=[_row_spec(BLK, D_MODEL), _row_spec(BLK, D_MODEL)],
        out_shape=[jax.ShapeDtypeStruct(h0.shape, F32), jax.ShapeDtypeStruct(xn.shape, BF16)],
        input_output_aliases={2: 0, 3: 1},
        compiler_params=_params(),
    )(meta_pad, g, h0, xn)


KVX_W = 2 * N_KV_HEADS * BLK


def _qk_prep(qkv, q_norm_t, k_norm_t, e_mat, tm):
    l_dim = qkv.shape[0]

    def body(x_ref, qg_ref, kg_ref, e_ref, q_ref, kf_ref, vf_ref):
        x = x_ref[...]
        q = _head_rms(x[:, :Q_W], qg_ref[...], e_ref) * (HEAD_DIM ** -0.5)
        q_ref[...] = q.astype(BF16)
        k = _head_rms(x[:, Q_W:Q_W + KV_W], kg_ref[...], e_ref)
        v = x[:, Q_W + KV_W:Q_W + 2 * KV_W]
        half = _lane_half((tm, BLK))
        for src, dst in ((k, kf_ref), (v, vf_ref)):
            for kv in range(N_KV_HEADS):
                blk = src[:, BLK * (kv // 2):BLK * (kv // 2) + BLK]
                swapped = pltpu.roll(blk, HEAD_DIM, axis=1)
                for e in range(2):
                    val = blk if kv % 2 == e else swapped
                    idx = 2 * kv + e
                    dst[:, BLK * idx:BLK * idx + BLK] = jnp.where(half == e, val, 0.0).astype(BF16)

    return pl.pallas_call(
        body, name="qk_prep", grid=(l_dim // tm,),
        in_specs=[_row_spec(tm, Q_W + 2 * KV_W), _full_spec((1, Q_W)), _full_spec((1, KV_W)),
                  _full_spec((256, 256))],
        out_specs=[_row_spec(tm, Q_W), _row_spec(tm, KVX_W), _row_spec(tm, KVX_W)],
        out_shape=[jax.ShapeDtypeStruct((l_dim, Q_W), BF16),
                   jax.ShapeDtypeStruct((l_dim, KVX_W), BF16),
                   jax.ShapeDtypeStruct((l_dim, KVX_W), BF16)],
        compiler_params=_params(),
    )(qkv, q_norm_t, k_norm_t, e_mat)


def _merge_fwd(attn, zab, gates, abn, sbn, tm):
    l_dim = attn.shape[0]

    def body(a_ref, z_ref, g_ref, an_ref, sn_ref, o_ref):
        z = z_ref[...].astype(F32)
        g = g_ref[...].astype(F32)
        ssm = z[:, :D_MODEL] * _sigmoid(z[:, D_MODEL:])
        merged = (_sigmoid(g[:, :D_MODEL]) * _rms(a_ref[...], an_ref[...])
                  + _sigmoid(g[:, D_MODEL:]) * _rms(ssm, sn_ref[...]))
        o_ref[...] = merged.astype(BF16)

    return pl.pallas_call(
        body, name="merge_fwd", grid=(l_dim // tm,),
        in_specs=[_row_spec(tm, D_MODEL), _row_spec(tm, 2 * D_MODEL), _row_spec(tm, 2 * D_MODEL),
                  _full_spec((1, D_MODEL)), _full_spec((1, D_MODEL))],
        out_specs=_row_spec(tm, D_MODEL),
        out_shape=jax.ShapeDtypeStruct((l_dim, D_MODEL), BF16),
        compiler_params=_params(),
    )(attn, zab, gates, abn, sbn)


def _merge_bwd(attn, zab, gates, abn, sbn, dmerged, tm):
    l_dim = attn.shape[0]

    def body(a_ref, z_ref, g_ref, an_ref, sn_ref, dm_ref, da_ref, dz_ref, dg_ref, dan_ref, dsn_ref):
        @pl.when(pl.program_id(0) == 0)
        def _():
            dan_ref[...] = jnp.zeros_like(dan_ref)
            dsn_ref[...] = jnp.zeros_like(dsn_ref)

        z = z_ref[...].astype(F32)
        g = g_ref[...].astype(F32)
        dm = dm_ref[...].astype(F32)
        attn_v = a_ref[...]
        za, zb = z[:, :D_MODEL], z[:, D_MODEL:]
        sb = _sigmoid(zb)
        ssm = za * sb
        s_ga, s_gs = _sigmoid(g[:, :D_MODEL]), _sigmoid(g[:, D_MODEL:])
        a_n = _rms(attn_v, an_ref[...])
        s_n = _rms(ssm, sn_ref[...])
        dg_ref[:, :D_MODEL] = (dm * a_n * s_ga * (1.0 - s_ga)).astype(BF16)
        dg_ref[:, D_MODEL:] = (dm * s_n * s_gs * (1.0 - s_gs)).astype(BF16)
        dattn, dan = _rms_bwd(attn_v, an_ref[...], dm * s_ga)
        dssm, dsn = _rms_bwd(ssm, sn_ref[...], dm * s_gs)
        da_ref[...] = dattn.astype(BF16)
        dz_ref[:, :D_MODEL] = (dssm * sb).astype(BF16)
        dz_ref[:, D_MODEL:] = (dssm * za * sb * (1.0 - sb)).astype(BF16)
        dan_ref[...] += dan
        dsn_ref[...] += dsn

    return pl.pallas_call(
        body, name="merge_bwd", grid=(l_dim // tm,),
        in_specs=[_row_spec(tm, D_MODEL), _row_spec(tm, 2 * D_MODEL), _row_spec(tm, 2 * D_MODEL),
                  _full_spec((1, D_MODEL)), _full_spec((1, D_MODEL)), _row_spec(tm, D_MODEL)],
        out_specs=[_row_spec(tm, D_MODEL), _row_spec(tm, 2 * D_MODEL), _row_spec(tm, 2 * D_MODEL),
                   _full_spec((1, D_MODEL)), _full_spec((1, D_MODEL))],
        out_shape=[jax.ShapeDtypeStruct((l_dim, D_MODEL), BF16),
                   jax.ShapeDtypeStruct((l_dim, 2 * D_MODEL), BF16),
                   jax.ShapeDtypeStruct((l_dim, IN_COLS), BF16),
                   jax.ShapeDtypeStruct((1, D_MODEL), F32), jax.ShapeDtypeStruct((1, D_MODEL), F32)],
        compiler_params=_params(),
    )(attn, zab, gates, abn, sbn, dmerged)


FF_TILE = D_FF // 2


def _ffn_in_swiglu(hn, wt_ffn_in, tm):
    l_dim = hn.shape[0]
    nt = (((1,), (1,)), ((), ()))

    def body(a_ref, w_ref, gu_ref, act_ref):
        r = lax.dot_general(a_ref[...], w_ref[...], nt, preferred_element_type=F32)
        gate, up = r[:, :FF_TILE], r[:, FF_TILE:]
        gu_ref[...] = r.astype(BF16)
        act_ref[...] = (gate * _sigmoid(gate) * up).astype(BF16)

    return pl.pallas_call(
        body, name="ffn_in_swiglu", grid=(l_dim // tm, 2),
        in_specs=[pl.BlockSpec((tm, D_MODEL), lambda i, j: (i, 0)),
                  pl.BlockSpec((2 * FF_TILE, D_MODEL), lambda i, j: (j, 0))],
        out_specs=[pl.BlockSpec((tm, 2 * FF_TILE), lambda i, j: (i, j)),
                   pl.BlockSpec((tm, FF_TILE), lambda i, j: (i, j))],
        out_shape=[jax.ShapeDtypeStruct((l_dim, 2 * D_FF), BF16), jax.ShapeDtypeStruct((l_dim, D_FF), BF16)],
        compiler_params=_params(dimension_semantics=("parallel", "parallel")),
    )(hn, wt_ffn_in)


def _d_act_swiglu(dh2_b, w_ffn_out, gu, tm):
    l_dim = dh2_b.shape[0]
    nt = (((1,), (1,)), ((), ()))

    def body(d_ref, w_ref, gu_ref, o_ref):
        d = lax.dot_general(d_ref[...], w_ref[...], nt, preferred_element_type=F32)
        gate = gu_ref[:, :FF_TILE].astype(F32)
        up = gu_ref[:, FF_TILE:].astype(F32)
        s = _sigmoid(gate)
        o_ref[:, :FF_TILE] = (d * up * (s * (1.0 + gate * (1.0 - s)))).astype(BF16)
        o_ref[:, FF_TILE:] = (d * (gate * s)).astype(BF16)

    return pl.pallas_call(
        body, name="d_act_swiglu", grid=(l_dim // tm, 2),
        in_specs=[pl.BlockSpec((tm, D_MODEL), lambda i, j: (i, 0)),
                  pl.BlockSpec((FF_TILE, D_MODEL), lambda i, j: (j, 0)),
                  pl.BlockSpec((tm, 2 * FF_TILE), lambda i, j: (i, j))],
        out_specs=pl.BlockSpec((tm, 2 * FF_TILE), lambda i, j: (i, j)),
        out_shape=jax.ShapeDtypeStruct((l_dim, 2 * D_FF), BF16),
        compiler_params=_params(dimension_semantics=("parallel", "parallel")),
    )(dh2_b, w_ffn_out, gu)


def _loss_grad(h2, target2d, tm):
    l_dim = h2.shape[0]
    n_sub = tm // BLK

    def body(*refs):
        h_ref, t_refs = refs[0], refs[1:1 + n_sub]
        d_ref, db_ref, loss_ref = refs[1 + n_sub:]
        i = pl.program_id(0)

        @pl.when(i == 0)
        def _():
            loss_ref[...] = jnp.zeros_like(loss_ref)

        for k in range(n_sub):
            rows = slice(BLK * k, BLK * k + BLK)
            real = jnp.where(n_sub * i + k >= 1, 1.0, 0.0)
            err = (h_ref[rows, :] - t_refs[k][...]) * real
            d = err * (1.0 / D_MODEL)
            d_ref[rows, :] = d
            db_ref[rows, :] = d.astype(BF16)
            loss_ref[...] += 0.5 * jnp.sum(jnp.mean(err * err, axis=-1, keepdims=True), axis=0, keepdims=True)

    return pl.pallas_call(
        body, name="loss_grad", grid=(l_dim // tm,),
        in_specs=[_row_spec(tm, D_MODEL)] + _shifted_specs(n_sub, target2d.shape[0] // BLK),
        out_specs=[_row_spec(tm, D_MODEL), _row_spec(tm, D_MODEL), _full_spec((1, 1))],
        out_shape=[jax.ShapeDtypeStruct((l_dim, D_MODEL), F32), jax.ShapeDtypeStruct((l_dim, D_MODEL), BF16),
                   jax.ShapeDtypeStruct((1, 1), F32)],
        compiler_params=_params(),
    )(h2, *([target2d] * n_sub))


def _norm_bwd_res(name, h, g, dy, dres, tm):
    l_dim = h.shape[0]

    def body(h_ref, g_ref, dy_ref, dr_ref, o_ref, ob_ref, dg_ref):
        @pl.when(pl.program_id(0) == 0)
        def _():
            dg_ref[...] = jnp.zeros_like(dg_ref)

        dx, dg = _rms_bwd(h_ref[...], g_ref[...], dy_ref[...].astype(F32))
        out = dr_ref[...] + dx
        o_ref[...] = out
        ob_ref[...] = out.astype(BF16)
        dg_ref[...] += dg

    return pl.pallas_call(
        body, name=name, grid=(l_dim // tm,),
        in_specs=[_row_spec(tm, D_MODEL), _full_spec((1, D_MODEL)), _row_spec(tm, D_MODEL), _row_spec(tm, D_MODEL)],
        out_specs=[_row_spec(tm, D_MODEL), _row_spec(tm, D_MODEL), _full_spec((1, D_MODEL))],
        out_shape=[jax.ShapeDtypeStruct((l_dim, D_MODEL), F32), jax.ShapeDtypeStruct((l_dim, D_MODEL), BF16),
                   jax.ShapeDtypeStruct((1, D_MODEL), F32)],
        compiler_params=_params(),
    )(h, g, dy, dres)


def _final_bwd(h0, g, dxn, dh1, tm):
    l_dim = h0.shape[0]
    n_sub = tm // BLK
    n_tiles = (l_dim - BLK) // tm

    def sub_specs():
        return [_row_spec(BLK, D_MODEL, (lambda j, k=k: jnp.where(j < n_tiles, n_sub * j + 1 + k, 0)))
                for k in range(n_sub)]

    def body(*refs):
        h_refs, g_ref = refs[:n_sub], refs[n_sub]
        dy_refs, dr_refs = refs[n_sub + 1:2 * n_sub + 1], refs[2 * n_sub + 1:3 * n_sub + 1]
        gx_ref, gm_ref, dg_ref = refs[3 * n_sub + 1:]
        j = pl.program_id(0)

        @pl.when(j == 0)
        def _():
            dg_ref[...] = jnp.zeros_like(dg_ref)

        def block(k):
            dx, dg = _rms_bwd(h_refs[k][...], g_ref[...], dy_refs[k][...].astype(F32))
            dg_ref[...] += dg
            return dr_refs[k][...] + dx

        @pl.when(j < n_tiles)
        def _():
            for k in range(n_sub):
                gx_ref[BLK * k:BLK * k + BLK, :] = block(k)

        @pl.when(j == n_tiles)
        def _():
            gm_ref[...] = block(0)

    return pl.pallas_call(
        body, name="final_bwd", grid=(n_tiles + 1,),
        in_specs=sub_specs() + [_full_spec((1, D_MODEL))] + sub_specs() + sub_specs(),
        out_specs=[_row_spec(tm, D_MODEL, lambda j: jnp.minimum(j, n_tiles - 1)), _full_spec((BLK, D_MODEL)),
                   _full_spec((1, D_MODEL))],
        out_shape=[jax.ShapeDtypeStruct((l_dim - BLK, D_MODEL), F32), jax.ShapeDtypeStruct((BLK, D_MODEL), F32),
                   jax.ShapeDtypeStruct((1, D_MODEL), F32)],
        compiler_params=_params(),
    )(*([h0] * n_sub), g, *([dxn] * n_sub), *([dh1] * n_sub))


def _attn_valid(n):
    shape = (2 * BLK, 3 * BLK)
    qi = lax.broadcasted_iota(jnp.int32, shape, 0) & (BLK - 1)
    col = lax.broadcasted_iota(jnp.int32, shape, 1)
    kj = col & (BLK - 1)
    part = col >> 7
    nn = jnp.zeros(shape, jnp.int32) + n
    meta_ok = (part == 0) & (kj >= PAD) & (nn >= 1)
    prev_ok = (part == 1) & (kj > qi) & (nn >= 2)
    cur_ok = (part == 2) & (kj <= qi) & ((nn >= 1) | (kj >= PAD))
    return meta_ok | prev_ok | cur_ok


def _attn_scores(q_ref, kwin, sk_ref, valid, kv, e):
    qs = jnp.concatenate([q_ref[:, BLK * (2 * kv):BLK * (2 * kv) + BLK],
                          q_ref[:, BLK * (2 * kv + 1):BLK * (2 * kv + 1) + BLK]], axis=0)
    s = lax.dot_general(qs, kwin, (((1,), (1,)), ((), ())), preferred_element_type=F32)
    h0 = 4 * kv + e
    row = lax.broadcasted_iota(jnp.int32, (2 * BLK, 1), 0)
    sink = jnp.where(row < BLK, sk_ref[:, h0:h0 + 1], sk_ref[:, h0 + 2:h0 + 3])
    return qs, jnp.where(valid, s, NEG), sink


def _attn_specs(nb):
    prev = lambda i: jnp.maximum(i - 1, 0)
    zero = lambda i: 0
    kv_specs = [_row_spec(BLK, KVX_W, zero), _row_spec(BLK, KVX_W, prev), _row_spec(BLK, KVX_W)]
    return kv_specs


def _attn_fwd(qn, kf, vf, sinks):
    l_dim = qn.shape[0]
    nb = l_dim // BLK

    def body(q_ref, km_ref, kp_ref, kc_ref, vm_ref, vp_ref, vc_ref, sk_ref, o_ref, lse_ref):
        valid = _attn_valid(pl.program_id(0))
        lane = lax.broadcasted_iota(jnp.int32, (BLK, BLK), 1)
        lse_all = jnp.zeros((BLK, BLK), F32)
        for kv in range(N_KV_HEADS):
            outs = []
            for e in range(2):
                sl = slice(BLK * (2 * kv + e), BLK * (2 * kv + e) + BLK)
                kwin = jnp.concatenate([km_ref[:, sl], kp_ref[:, sl], kc_ref[:, sl]], axis=0)
                vwin = jnp.concatenate([vm_ref[:, sl], vp_ref[:, sl], vc_ref[:, sl]], axis=0)
                _, s, sink = _attn_scores(q_ref, kwin, sk_ref, valid, kv, e)
                m = jnp.maximum(jnp.max(s, axis=-1, keepdims=True), sink)
                ex = jnp.exp(s - m)
                den = jnp.sum(ex, axis=-1, keepdims=True) + jnp.exp(sink - m)
                p = ex * (1.0 / den)
                outs.append(jnp.dot(p.astype(BF16), vwin, preferred_element_type=F32))
                lse = m + jnp.log(den)
                lse_all = jnp.where(lane == 4 * kv + e, lse[:BLK], lse_all)
                lse_all = jnp.where(lane == 4 * kv + 2 + e, lse[BLK:], lse_all)
            o = outs[0] + outs[1]
            o_ref[:, BLK * (2 * kv):BLK * (2 * kv) + BLK] = o[:BLK]
            o_ref[:, BLK * (2 * kv + 1):BLK * (2 * kv + 1) + BLK] = o[BLK:]
        lse_ref[...] = lse_all

    kv_specs = _attn_specs(nb)
    return pl.pallas_call(
        body, name="attn_fwd", grid=(nb,),
        in_specs=[_row_spec(BLK, Q_W)] + kv_specs + kv_specs + [_full_spec((1, N_Q_HEADS))],
        out_specs=[_row_spec(BLK, Q_W), _row_spec(BLK, BLK)],
        out_shape=[jax.ShapeDtypeStruct((l_dim, Q_W), F32), jax.ShapeDtypeStruct((l_dim, BLK), F32)],
        compiler_params=_params(),
    )(qn, kf, kf, kf, vf, vf, vf, sinks)


def _attn_bwd(qn, kf, vf, sinks, lse, attn, dattn):
    l_dim = qn.shape[0]
    nb = l_dim // BLK
    wide = KV_W
    tn = (((0,), (0,)), ((), ()))
    nt = (((1,), (1,)), ((), ()))

    def body(q_ref, km_ref, kp_ref, kc_ref, vm_ref, vp_ref, vc_ref, sk_ref, lse_ref, o_ref, do_ref,
             dq_ref, dkc_ref, dkp_ref, dkm_ref, dvc_ref, dvp_ref, dvm_ref, dsk_ref):
        @pl.when(pl.program_id(0) == 0)
        def _():
            dkm_ref[...] = jnp.zeros_like(dkm_ref)
            dvm_ref[...] = jnp.zeros_like(dvm_ref)
            dsk_ref[...] = jnp.zeros_like(dsk_ref)

        valid = _attn_valid(pl.program_id(0))
        half = _lane_half((BLK, BLK))
        half2 = _lane_half((2 * BLK, BLK))
        half3 = _lane_half((3 * BLK, BLK))
        lane16 = lax.broadcasted_iota(jnp.int32, (1, N_Q_HEADS), 1)
        dsk = jnp.zeros((1, N_Q_HEADS), F32)
        folded_k, folded_v = [], []
        for kv in range(N_KV_HEADS):
            j0, j1 = 2 * kv, 2 * kv + 1
            do0 = do_ref[:, BLK * j0:BLK * j0 + BLK]
            do1 = do_ref[:, BLK * j1:BLK * j1 + BLK]
            do0f, do1f = do0.astype(F32), do1.astype(F32)
            prod0 = do0f * o_ref[:, BLK * j0:BLK * j0 + BLK]
            prod1 = do1f * o_ref[:, BLK * j1:BLK * j1 + BLK]
            dos = jnp.concatenate([do0, do1], axis=0)
            dqs, dks, dvs = [], [], []
            for e in range(2):
                sl = slice(BLK * (2 * kv + e), BLK * (2 * kv + e) + BLK)
                kwin = jnp.concatenate([km_ref[:, sl], kp_ref[:, sl], kc_ref[:, sl]], axis=0)
                vwin = jnp.concatenate([vm_ref[:, sl], vp_ref[:, sl], vc_ref[:, sl]], axis=0)
                qs, s, sink = _attn_scores(q_ref, kwin, sk_ref, valid, kv, e)
                h0 = 4 * kv + e
                lse_rows = jnp.concatenate([lse_ref[:, h0:h0 + 1], lse_ref[:, h0 + 2:h0 + 3]], axis=0)
                p = jnp.exp(s - lse_rows)
                p_sink = jnp.exp(sink - lse_rows)
                delta = jnp.concatenate(
                    [jnp.sum(jnp.where(half == e, prod0, 0.0), axis=-1, keepdims=True),
                     jnp.sum(jnp.where(half == e, prod1, 0.0), axis=-1, keepdims=True)], axis=0)
                dp = lax.dot_general(dos, vwin, nt, preferred_element_type=F32)
                ds = (p * (dp - delta)).astype(BF16)
                pb = p.astype(BF16)
                dqs.append(jnp.dot(ds, kwin, preferred_element_type=F32))
                dks.append(lax.dot_general(ds, qs, tn, preferred_element_type=F32))
                dvs.append(lax.dot_general(pb, dos, tn, preferred_element_type=F32))
                sink_g = -(p_sink * delta)
                g_lo = jnp.sum(sink_g[:BLK], axis=0, keepdims=True)
                g_hi = jnp.sum(sink_g[BLK:], axis=0, keepdims=True)
                dsk = dsk + jnp.where(lane16 == h0, g_lo, 0.0) + jnp.where(lane16 == h0 + 2, g_hi, 0.0)
            dq = jnp.where(half2 == 0, dqs[0], dqs[1])
            dq_ref[:, BLK * j0:BLK * j0 + BLK] = dq[:BLK].astype(BF16)
            dq_ref[:, BLK * j1:BLK * j1 + BLK] = dq[BLK:].astype(BF16)
            own = kv % 2
            folded_k.append(dks[own] + pltpu.roll(dks[1 - own], HEAD_DIM, axis=1))
            folded_v.append(dvs[own] + pltpu.roll(dvs[1 - own], HEAD_DIM, axis=1))
            if own == 1:
                cols = slice(BLK * (kv // 2), BLK * (kv // 2) + BLK)
                for folded, m_ref, p_ref, c_ref in ((folded_k, dkm_ref, dkp_ref, dkc_ref),
                                                    (folded_v, dvm_ref, dvp_ref, dvc_ref)):
                    both = jnp.where(half3 == 0, folded[0], folded[1])
                    m_ref[:, cols] += both[:BLK]
                    p_ref[:, cols] = both[BLK:2 * BLK].astype(BF16)
                    c_ref[:, cols] = both[2 * BLK:].astype(BF16)
                folded_k, folded_v = [], []
        dsk_ref[...] += dsk

    kv_specs = _attn_specs(nb)
    row_wide = _row_spec(BLK, wide)
    acc_wide = _full_spec((BLK, wide))
    big = jax.ShapeDtypeStruct((l_dim, wide), BF16)
    return pl.pallas_call(
        body, name="attn_bwd", grid=(nb,),
        in_specs=[_row_spec(BLK, Q_W)] + kv_specs + kv_specs
        + [_full_spec((1, N_Q_HEADS)), _row_spec(BLK, BLK), _row_spec(BLK, Q_W), _row_spec(BLK, Q_W)],
        out_specs=[_row_spec(BLK, Q_W), row_wide, row_wide, acc_wide, row_wide, row_wide, acc_wide,
                   _full_spec((1, N_Q_HEADS))],
        out_shape=[jax.ShapeDtypeStruct((l_dim, Q_W), BF16), big, big, jax.ShapeDtypeStruct((BLK, wide), F32),
                   big, big, jax.ShapeDtypeStruct((BLK, wide), F32), jax.ShapeDtypeStruct((1, N_Q_HEADS), F32)],
        compiler_params=_params(),
    )(qn, kf, kf, kf, vf, vf, vf, sinks, lse, attn, dattn)


def _qk_bwd(qkv, q_norm_t, k_norm_t, e_mat, dq, dkc, dkp, dkm, dvc, dvp, dvm, dproj):
    l_dim = qkv.shape[0]
    nb = l_dim // BLK
    wide = KV_W

    def body(x_ref, qg_ref, kg_ref, e_ref, dq_ref, dkc_ref, dkp_ref, dkm_ref, dvc_ref, dvp_ref, dvm_ref,
             dproj_ref, o_ref, dqg_ref, dkg_ref):
        i = pl.program_id(0)

        @pl.when(i == 0)
        def _():
            dqg_ref[...] = jnp.zeros_like(dqg_ref)
            dkg_ref[...] = jnp.zeros_like(dkg_ref)

        first = jnp.where(i == 0, 1.0, 0.0)
        not_last = jnp.where(i < nb - 1, 1.0, 0.0)
        dk_x = dkc_ref[...].astype(F32) + not_last * dkp_ref[...].astype(F32) + first * dkm_ref[...]
        dv_x = dvc_ref[...].astype(F32) + not_last * dvp_ref[...].astype(F32) + first * dvm_ref[...]
        x = x_ref[...]
        dqx, dqg = _head_rms_bwd(x[:, :Q_W], qg_ref[...], dq_ref[...].astype(F32) * (HEAD_DIM ** -0.5), e_ref)
        dkx, dkg = _head_rms_bwd(x[:, Q_W:Q_W + KV_W], kg_ref[...], dk_x, e_ref)
        o_ref[:, :Q_W] = dqx.astype(BF16)
        o_ref[:, Q_W:Q_W + KV_W] = dkx.astype(BF16)
        o_ref[:, Q_W + KV_W:] = dv_x.astype(BF16)
        dqg_ref[...] += dqg
        dkg_ref[...] += dkg

    nxt = lambda i: jnp.minimum(i + 1, nb - 1)
    row_wide = _row_spec(BLK, wide)
    nxt_wide = _row_spec(BLK, wide, nxt)
    acc_wide = _full_spec((BLK, wide))
    qkv_w = Q_W + 2 * KV_W
    in_specs = [_row_spec(BLK, qkv_w), _full_spec((1, Q_W)), _full_spec((1, KV_W)),
                _full_spec((256, 256)), _row_spec(BLK, Q_W),
                row_wide, nxt_wide, acc_wide, row_wide, nxt_wide, acc_wide, pl.BlockSpec(memory_space=pl.ANY)]
    return pl.pallas_call(
        body, name="qk_bwd", grid=(nb,),
        in_specs=in_specs,
        out_specs=[pl.BlockSpec((BLK, qkv_w), lambda i: (i, 3 * D_MODEL // qkv_w)),
                   _full_spec((1, Q_W)), _full_spec((1, KV_W))],
        input_output_aliases={len(in_specs) - 1: 0},
        out_shape=[jax.ShapeDtypeStruct(dproj.shape, BF16),
                   jax.ShapeDtypeStruct((1, Q_W), F32), jax.ShapeDtypeStruct((1, KV_W), F32)],
        compiler_params=_params(),
    )(qkv, q_norm_t, k_norm_t, e_mat, dq, dkc, dkp, dkm, dvc, dvp, dvm, dproj)


GRP = 8


def _strided(r, g):
    return pl.ds(r, g, stride=GRP)


def _lane_slab(ref, i, r, g):
    return ref[i, _strided(r, g), :]


def _chunk_carries(xr_ref, xi_ref, i, ar, ai, sqr_ref, sqi_ref, seq_r, seq_i, cin_r, cin_i, sign, reverse):
    g = xr_ref.shape[1] // GRP
    sr = si = None
    for r in (range(GRP - 1, -1, -1) if reverse else range(GRP)):
        xr, xi = _lane_slab(xr_ref, i, r, g), _lane_slab(xi_ref, i, r, g)
        if sr is not None:
            xr, xi = xr + ar * sr - ai * si, xi + ar * si + ai * sr
        sr, si = xr, xi
    row = lax.broadcasted_iota(jnp.int32, sr.shape, 0)
    idx, s = 0, 1
    while s < g:
        br = sqr_ref[idx:idx + 1, BLK * i:BLK * i + BLK]
        bi = sign * sqi_ref[idx:idx + 1, BLK * i:BLK * i + BLK]
        shift, keep = (g - s, row < g - s) if reverse else (s, row >= s)
        pr = jnp.where(keep, pltpu.roll(sr, shift, axis=0), 0.0)
        pi = jnp.where(keep, pltpu.roll(si, shift, axis=0), 0.0)
        sr, si = sr + br * pr - bi * pi, si + br * pi + bi * pr
        idx, s = idx + 1, 2 * s
    return sr + seq_r * cin_r - seq_i * cin_i, si + seq_r * cin_i + seq_i * cin_r


def _ssm_fwd_kb(u, wb_re, wb_im, wc_re, wc_im, d_skip, tabs, q):
    l_dim = u.shape[0]
    nc = l_dim // q
    g = q // GRP

    def body(u_ref, wbr_ref, wbi_ref, wcr_ref, wci_ref, d_ref, a1r_ref, a1i_ref, sqr_ref, sqi_ref,
             seqr_ref, seqi_ref, y_ref, z_ref, sr_ref, si_ref, cr_ref, ci_ref, xr_ref, xi_ref):
        @pl.when(pl.program_id(1) == 0)
        def _():
            cr_ref[...] = jnp.zeros_like(cr_ref)
            ci_ref[...] = jnp.zeros_like(ci_ref)

        u_kb = u_ref[...]
        ub = u_kb.astype(BF16)
        xr = jnp.dot(ub, wbr_ref[0], preferred_element_type=F32)
        xi = jnp.dot(ub, wbi_ref[0], preferred_element_type=F32)
        for i in range(LB_KB):
            xr_ref[i] = xr[:, BLK * i:BLK * i + BLK]
            xi_ref[i] = xi[:, BLK * i:BLK * i + BLK]
        row = lax.broadcasted_iota(jnp.int32, (g, BLK), 0)
        for i in range(LB_KB):
            lanes = slice(BLK * i, BLK * i + BLK)
            ar, ai = a1r_ref[0:1, lanes], a1i_ref[0:1, lanes]
            cin_r, cin_i = cr_ref[0:1, lanes], ci_ref[0:1, lanes]
            tr, ti = _chunk_carries(xr_ref, xi_ref, i, ar, ai, sqr_ref, sqi_ref, seqr_ref[:, lanes],
                                    seqi_ref[:, lanes], cin_r, cin_i, 1.0, reverse=False)
            cr_ref[0:1, lanes] = jnp.sum(jnp.where(row == g - 1, tr, 0.0), axis=0, keepdims=True)
            ci_ref[0:1, lanes] = jnp.sum(jnp.where(row == g - 1, ti, 0.0), axis=0, keepdims=True)
            pr = jnp.where(row == 0, cin_r, pltpu.roll(tr, 1, axis=0))
            pi = jnp.where(row == 0, cin_i, pltpu.roll(ti, 1, axis=0))
            for r in range(GRP):
                pr, pi = (_lane_slab(xr_ref, i, r, g) + ar * pr - ai * pi,
                          _lane_slab(xi_ref, i, r, g) + ar * pi + ai * pr)
                sr_ref[i, _strided(r, g), :] = pr
                si_ref[i, _strided(r, g), :] = pi
        s_r = jnp.concatenate([sr_ref[i] for i in range(LB_KB)], axis=1)
        s_i = jnp.concatenate([si_ref[i] for i in range(LB_KB)], axis=1)
        y = (jnp.dot(s_r.astype(BF16), wcr_ref[0], preferred_element_type=F32)
             - jnp.dot(s_i.astype(BF16), wci_ref[0], preferred_element_type=F32)
             + d_ref[...] * u_kb)
        y_ref[...] = y.astype(BF16)
        z_ref[...] = _gelu(y).astype(BF16)

    chan = pl.BlockSpec((q, BLK), lambda k, c: (c, k))
    wb_spec = pl.BlockSpec((1, BLK, ST_KB), lambda k, c: (k, 0, 0))
    wc_spec = pl.BlockSpec((1, ST_KB, BLK), lambda k, c: (k, 0, 0))
    tab_specs = [pl.BlockSpec((t.shape[0], ST_KB), lambda k, c: (0, k)) for t in tabs[:6]]
    state_spec = pl.BlockSpec((LB_KB, q, BLK), lambda k, c: (k, c, 0))
    state_shape = jax.ShapeDtypeStruct((N_LB, l_dim, BLK), F32)
    return pl.pallas_call(
        body, name="ssm_fwd", grid=(SSM_KB, nc),
        in_specs=[chan, wb_spec, wb_spec, wc_spec, wc_spec, pl.BlockSpec((1, BLK), lambda k, c: (0, k))] + tab_specs,
        out_specs=[chan, chan, state_spec, state_spec],
        out_shape=[jax.ShapeDtypeStruct((l_dim, D_MODEL), BF16), jax.ShapeDtypeStruct((l_dim, D_MODEL), BF16),
                   state_shape, state_shape],
        scratch_shapes=[pltpu.VMEM((8, ST_KB), F32), pltpu.VMEM((8, ST_KB), F32),
                        pltpu.VMEM((LB_KB, q, BLK), F32), pltpu.VMEM((LB_KB, q, BLK), F32)],
        compiler_params=_params(dimension_semantics=("parallel", "arbitrary")),
    )(u, wb_re, wb_im, wc_re, wc_im, d_skip, *tabs[:6])


def _ssm_bwd_kb(dz, y, u, s_re, s_im, wb_re, wb_im, wc_re, wc_im, d_skip, tabs, q, dproj):
    l_dim = u.shape[0]
    nc = l_dim // q
    g = q // GRP

    def body(dz_ref, y_ref, u_ref, sr_ref, si_ref, wbr_ref, wbi_ref, wcr_ref, wci_ref, d_ref,
             a1r_ref, a1i_ref, sqr_ref, sqi_ref, revr_ref, revi_ref, dproj_ref,
             du_ref, dd_ref, dar_ref, dai_ref, dwbr_ref, dwbi_ref, dwcr_ref, dwci_ref,
             cr_ref, ci_ref, gr_ref, gi_ref):
        @pl.when(pl.program_id(1) == 0)
        def _():
            for ref in (cr_ref, ci_ref, dd_ref, dar_ref, dai_ref, dwbr_ref, dwbi_ref, dwcr_ref, dwci_ref):
                ref[...] = jnp.zeros_like(ref)

        tn = (((0,), (0,)), ((), ()))
        nt = (((1,), (1,)), ((), ()))
        u_kb = u_ref[...]
        dy = dz_ref[...].astype(F32) * _gelu_grad(y_ref[...].astype(F32))
        dyb = dy.astype(BF16)
        ub = u_kb.astype(BF16)
        dd_ref[...] += jnp.sum(dy * u_kb, axis=0, keepdims=True)
        ds_r = lax.dot_general(dyb, wcr_ref[0], nt, preferred_element_type=F32)
        ds_i = -lax.dot_general(dyb, wci_ref[0], nt, preferred_element_type=F32)
        for i in range(LB_KB):
            gr_ref[i] = ds_r[:, BLK * i:BLK * i + BLK]
            gi_ref[i] = ds_i[:, BLK * i:BLK * i + BLK]
        row = lax.broadcasted_iota(jnp.int32, (g, BLK), 0)
        for i in range(LB_KB):
            lanes = slice(BLK * i, BLK * i + BLK)
            ar, ai = a1r_ref[0:1, lanes], -a1i_ref[0:1, lanes]
            cin_r, cin_i = cr_ref[0:1, lanes], ci_ref[0:1, lanes]
            tr, ti = _chunk_carries(gr_ref, gi_ref, i, ar, ai, sqr_ref, sqi_ref, revr_ref[:, lanes],
                                    -revi_ref[:, lanes], cin_r, cin_i, -1.0, reverse=True)
            cr_ref[0:1, lanes] = jnp.sum(jnp.where(row == 0, tr, 0.0), axis=0, keepdims=True)
            ci_ref[0:1, lanes] = jnp.sum(jnp.where(row == 0, ti, 0.0), axis=0, keepdims=True)
            nr = jnp.where(row == g - 1, cin_r, pltpu.roll(tr, g - 1, axis=0))
            ni = jnp.where(row == g - 1, cin_i, pltpu.roll(ti, g - 1, axis=0))
            acc_r = jnp.zeros((g, BLK), F32)
            acc_i = jnp.zeros((g, BLK), F32)
            for r in range(GRP - 1, -1, -1):
                s_r, s_i = _lane_slab(sr_ref, i, r, g), _lane_slab(si_ref, i, r, g)
                acc_r = acc_r + (nr * s_r + ni * s_i)
                acc_i = acc_i + (ni * s_r - nr * s_i)
                nr, ni = (_lane_slab(gr_ref, i, r, g) + ar * nr - ai * ni,
                          _lane_slab(gi_ref, i, r, g) + ar * ni + ai * nr)
                gr_ref[i, _strided(r, g), :] = nr
                gi_ref[i, _strided(r, g), :] = ni
            dar_ref[:, lanes] += jnp.sum(acc_r, axis=0, keepdims=True)
            dai_ref[:, lanes] += jnp.sum(acc_i, axis=0, keepdims=True)
        grb = jnp.concatenate([gr_ref[i] for i in range(LB_KB)], axis=1).astype(BF16)
        gib = jnp.concatenate([gi_ref[i] for i in range(LB_KB)], axis=1).astype(BF16)
        srb = jnp.concatenate([sr_ref[i] for i in range(LB_KB)], axis=1).astype(BF16)
        sib = jnp.concatenate([si_ref[i] for i in range(LB_KB)], axis=1).astype(BF16)
        du = (lax.dot_general(grb, wbr_ref[0], nt, preferred_element_type=F32)
              + lax.dot_general(gib, wbi_ref[0], nt, preferred_element_type=F32)
              + d_ref[...] * dy)
        du_ref[...] = du.astype(BF16)
        dwbr_ref[0] += lax.dot_general(ub, grb, tn, preferred_element_type=F32)
        dwbi_ref[0] += lax.dot_general(ub, gib, tn, preferred_element_type=F32)
        dwcr_ref[0] += lax.dot_general(srb, dyb, tn, preferred_element_type=F32)
        dwci_ref[0] -= lax.dot_general(sib, dyb, tn, preferred_element_type=F32)

    chan = pl.BlockSpec((q, BLK), lambda k, c: (nc - 1 - c, k))
    wb_spec = pl.BlockSpec((1, BLK, ST_KB), lambda k, c: (k, 0, 0))
    wc_spec = pl.BlockSpec((1, ST_KB, BLK), lambda k, c: (k, 0, 0))
    tab_in = [tabs[0], tabs[1], tabs[2], tabs[3], tabs[6], tabs[7]]
    tab_specs = [pl.BlockSpec((t.shape[0], ST_KB), lambda k, c: (0, k)) for t in tab_in]
    vec = pl.BlockSpec((1, BLK), lambda k, c: (0, k))
    svec = pl.BlockSpec((1, ST_KB), lambda k, c: (0, k))
    state_spec = pl.BlockSpec((LB_KB, q, BLK), lambda k, c: (k, nc - 1 - c, 0))
    du_spec = pl.BlockSpec((q, BLK), lambda k, c: (nc - 1 - c, 2 * D_MODEL // BLK + k))
    in_specs = ([chan, chan, chan, state_spec, state_spec, wb_spec, wb_spec, wc_spec, wc_spec, vec] + tab_specs
                + [pl.BlockSpec(memory_space=pl.ANY)])
    return pl.pallas_call(
        body, name="ssm_bwd", grid=(SSM_KB, nc),
        in_specs=in_specs,
        out_specs=[du_spec, vec, svec, svec, wb_spec, wb_spec, wc_spec, wc_spec],
        input_output_aliases={len(in_specs) - 1: 0},
        out_shape=[jax.ShapeDtypeStruct(dproj.shape, BF16), jax.ShapeDtypeStruct((1, D_MODEL), F32),
                   jax.ShapeDtypeStruct((1, N_STATE), F32), jax.ShapeDtypeStruct((1, N_STATE), F32),
                   jax.ShapeDtypeStruct((SSM_KB, BLK, ST_KB), F32), jax.ShapeDtypeStruct((SSM_KB, BLK, ST_KB), F32),
                   jax.ShapeDtypeStruct((SSM_KB, ST_KB, BLK), F32), jax.ShapeDtypeStruct((SSM_KB, ST_KB, BLK), F32)],
        scratch_shapes=[pltpu.VMEM((8, ST_KB), F32), pltpu.VMEM((8, ST_KB), F32),
                        pltpu.VMEM((LB_KB, q, BLK), F32), pltpu.VMEM((LB_KB, q, BLK), F32)],
        compiler_params=_params(dimension_semantics=("parallel", "arbitrary")),
    )(dz, y, u, s_re, s_im, wb_re, wb_im, wc_re, wc_im, d_skip, *tab_in, dproj)


def _discretize(lam_re, lam_im, log_dt, b_re, b_im):
    dt = jnp.exp(log_dt)[:, None]
    mag = jnp.exp(lam_re * dt)
    ar, ai = mag * jnp.cos(lam_im * dt), mag * jnp.sin(lam_im * dt)
    den = lam_re * lam_re + lam_im * lam_im
    nr, ni = ar - 1.0, ai
    fr, fi = (nr * lam_re + ni * lam_im) / den, (ni * lam_re - nr * lam_im) / den
    bbar_re = fr[..., None] * b_re - fi[..., None] * b_im
    bbar_im = fr[..., None] * b_im + fi[..., None] * b_re
    return ar, ai, bbar_re, bbar_im


def _block_diag_b(bbar):
    eye = jnp.eye(8, dtype=bbar.dtype)
    return jnp.einsum("kgpc,gh->kgchp", bbar.reshape(8, 8, SSM_STATE, SSM_GROUP_CH), eye).reshape(8, BLK, ST_KB)


def _block_diag_b_t(dwb):
    eye = jnp.eye(8, dtype=dwb.dtype)
    return jnp.einsum("kgchp,gh->kgpc", dwb.reshape(8, 8, SSM_GROUP_CH, 8, SSM_STATE), eye).reshape(
        SSM_GROUPS, SSM_STATE, SSM_GROUP_CH)


def _block_diag_c(c):
    eye = jnp.eye(8, dtype=c.dtype)
    return jnp.einsum("kgcp,gh->kgphc", c.reshape(8, 8, SSM_GROUP_CH, SSM_STATE), eye).reshape(8, ST_KB, BLK)


def _block_diag_c_t(dwc):
    eye = jnp.eye(8, dtype=dwc.dtype)
    return jnp.einsum("kgphc,gh->kgcp", dwc.reshape(8, 8, SSM_STATE, 8, SSM_GROUP_CH), eye).reshape(
        SSM_GROUPS, SSM_GROUP_CH, SSM_STATE)


def _powers(br, bi, n):
    pr, pi = br, bi
    cr, ci = br, bi
    while pr.shape[0] < n:
        pr, pi = (jnp.concatenate([pr, pr * cr - pi * ci], axis=0),
                  jnp.concatenate([pi, pr * ci + pi * cr], axis=0))
        cr, ci = cr * cr - ci * ci, 2.0 * cr * ci
    return pr[:n], pi[:n]


def _powers_desc(br, bi, n):
    pr, pi = br, bi
    cr, ci = br, bi
    while pr.shape[0] < n:
        pr, pi = (jnp.concatenate([pr * cr - pi * ci, pr], axis=0),
                  jnp.concatenate([pr * ci + pi * cr, pi], axis=0))
        cr, ci = cr * cr - ci * ci, 2.0 * cr * ci
    return pr, pi


def _power_tables(ar, ai, g):
    a1r, a1i = _powers(ar, ai, GRP)
    seqr, seqi = _powers(a1r[GRP - 1:], a1i[GRP - 1:], g)
    g2 = 1 << (g - 1).bit_length()
    revr, revi = _powers_desc(a1r[GRP - 1:], a1i[GRP - 1:], g2)
    revr, revi = revr[g2 - g:], revi[g2 - g:]
    sq_r, sq_i = [seqr[0:1]], [seqi[0:1]]
    while len(sq_r) < 8:
        r, i = sq_r[-1], sq_i[-1]
        sq_r.append(r * r - i * i)
        sq_i.append(2.0 * r * i)
    sqr, sqi = jnp.concatenate(sq_r, axis=0), jnp.concatenate(sq_i, axis=0)
    return a1r, a1i, sqr, sqi, seqr, seqi, revr, revi


HBM_SPEC = pl.BlockSpec(memory_space=pltpu.HBM)
SEM_SPEC = pl.BlockSpec(memory_space=pltpu.SEMAPHORE)
DATAFLOW = pltpu.SideEffectType.DATAFLOW_SIDE_EFFECTING


def _plain_rows(p, m):
    return p * m


def _ffn_in_rows(p, m):
    return ((p & 3) >> 1) * (4 * m) + (p >> 2) * (2 * m) + (p & 1) * m


def _peer_copies(src_refs, land_refs, send_sems, recv_sems, chunked, row_fns):
    x, y, c = lax.axis_index("x"), lax.axis_index("y"), lax.axis_index("c")
    me = 4 * x + 2 * y + c
    copies = []
    for a, (src, land) in enumerate(zip(src_refs, land_refs)):
        m = land.shape[0] // N_DEV
        for k in range(N_DEV - 1):
            rel = k + 1
            bx, by, bc = (rel >> 2) & 1, (rel >> 1) & 1, rel & 1
            peer = (x + bx - 2 * x * bx, y + by - 2 * y * by, c + bc - 2 * c * bc)
            p_idx = 4 * peer[0] + 2 * peer[1] + peer[2]
            copies.append(pltpu.make_async_remote_copy(
                src_ref=src.at[pl.ds(row_fns[a](p_idx, m), m), :] if chunked else src,
                dst_ref=land.at[pl.ds(me * m if chunked else row_fns[a](me, m), m), :],
                send_sem=send_sems[a * (N_DEV - 1) + k], recv_sem=recv_sems[a * (N_DEV - 1) + k],
                device_id=peer, device_id_type=MESH))
    return copies


def _own_copies(src_refs, land_outs, sems, chunked, row_fns):
    me = _my_index()
    copies = []
    for a, (src, land) in enumerate(zip(src_refs, land_outs)):
        m = land.shape[0] // N_DEV
        rows = row_fns[a](me, m)
        copies.append(pltpu.make_async_copy(src.at[pl.ds(rows, m), :] if chunked else src,
                                            land.at[pl.ds(me * m if chunked else rows, m), :], sems[a]))
    return copies


def _send_start(name, srcs, lands, chunked, row_fns=None):
    n = len(srcs)
    ns = n * (N_DEV - 1)
    row_fns = row_fns or [_plain_rows] * n

    def body(*refs):
        src_refs, land_refs = refs[:n], refs[n:2 * n]
        send_sems, own_sems = refs[2 * n:2 * n + ns], refs[2 * n + ns:2 * n + ns + n]
        recv_sems = refs[2 * n + ns + n:2 * n + 2 * ns + n]
        land_outs, token = refs[-1 - n:-1], refs[-1]
        for cp in _own_copies(src_refs, land_outs, own_sems, chunked, row_fns):
            cp.start()
        for cp in _peer_copies(src_refs, land_refs, send_sems, recv_sems, chunked, row_fns):
            cp.start()
        token[...] = jnp.zeros_like(token)

    ins = [pltpu.with_memory_space_constraint(a, pltpu.HBM) for a in list(srcs) + list(lands)]
    n_sems = 2 * ns + n
    out = pl.pallas_call(
        body, name=name,
        in_specs=[HBM_SPEC] * (2 * n),
        out_specs=[SEM_SPEC] * n_sems + [HBM_SPEC] * (2 * n) + [pl.BlockSpec(memory_space=pltpu.VMEM)],
        out_shape=[pltpu.SemaphoreType.DMA(())] * n_sems
        + [pltpu.HBM(a.shape, a.dtype) for a in list(srcs) + list(lands)]
        + [jax.ShapeDtypeStruct((8, BLK), F32)],
        input_output_aliases={i: i + n_sems for i in range(2 * n)},
        compiler_params=pltpu.CompilerParams(has_side_effects=DATAFLOW),
    )(*ins)
    return (out[:ns + n], out[ns + n:n_sems], out[n_sems:n_sems + n], out[n_sems + n:n_sems + 2 * n], out[-1])


def _send_wait(name, send_sems, recv_sems, srcs, lands, after, chunked, row_fns=None):
    n = len(srcs)
    ns = n * (N_DEV - 1)
    row_fns = row_fns or [_plain_rows] * n
    after = list(after) if isinstance(after, (list, tuple)) else [after]

    def body(*refs):
        src_refs, land_refs = refs[:n], refs[n:2 * n]
        s_sems, own_sems = refs[2 * n:2 * n + ns], refs[2 * n + ns:2 * n + ns + n]
        r_sems = refs[2 * n + ns + n:2 * n + 2 * ns + n]
        land_outs = refs[-n:]
        for cp in _own_copies(src_refs, land_outs, own_sems, chunked, row_fns):
            cp.wait()
        copies = _peer_copies(src_refs, land_refs, s_sems, r_sems, chunked, row_fns)
        for cp in copies:
            cp.wait_send()
        for cp in copies:
            cp.wait_recv()

    out = pl.pallas_call(
        body, name=name,
        in_specs=[HBM_SPEC] * (2 * n) + [SEM_SPEC] * (2 * ns + n) + [pl.BlockSpec(memory_space=pl.ANY)] * len(after),
        out_specs=[HBM_SPEC] * (2 * n),
        out_shape=[pltpu.HBM(a.shape, a.dtype) for a in list(srcs) + list(lands)],
        input_output_aliases={i: i for i in range(2 * n)},
        compiler_params=pltpu.CompilerParams(has_side_effects=DATAFLOW),
    )(*srcs, *lands, *send_sems, *recv_sems, *after)
    return out[n:]


def _sum_slots(name, recv, own, wmv=None):
    m, ncol = own.shape
    tr = m // 2 if (m // 2) % 16 == 0 else m
    g = m // tr
    c1 = 1.0 - ADAM_B1 ** ADAM_STEP
    c2 = 1.0 - ADAM_B2 ** ADAM_STEP

    def body(*refs):
        slots, own_ref, o_ref = refs[:N_DEV], refs[N_DEV], refs[N_DEV + 1 + (3 if wmv else 0)]
        me = _my_index()
        tot = None
        for s in range(N_DEV):
            v = jnp.where(me == s, own_ref[...], slots[s][...].astype(F32))
            tot = v if tot is None else tot + v
        o_ref[...] = tot
        if wmv:
            w_ref, m_ref, v_ref = refs[N_DEV + 1:N_DEV + 4]
            d_ref, nm_ref, nv_ref = refs[N_DEV + 5:N_DEV + 8]
            nm = ADAM_B1 * m_ref[...] + (1.0 - ADAM_B1) * tot
            nv = ADAM_B2 * v_ref[...] + (1.0 - ADAM_B2) * (tot * tot)
            d_ref[...] = -ADAM_LR * ((nm / c1) / (jnp.sqrt(nv / c2) + ADAM_EPS) + ADAM_WD * w_ref[...])
            nm_ref[...] = nm
            nv_ref[...] = nv

    def slot_spec(s):
        return pl.BlockSpec((tr, ncol), lambda i: (s * g + i, 0))

    tile = pl.BlockSpec((tr, ncol), lambda i: (i, 0))
    shape = jax.ShapeDtypeStruct((m, ncol), F32)
    n_extra = 3 if wmv else 0
    out = pl.pallas_call(
        body, name=name, grid=(g,),
        in_specs=[slot_spec(s) for s in range(N_DEV)] + [tile] * (1 + n_extra),
        out_specs=[tile] * (1 + n_extra),
        out_shape=[shape] * (1 + n_extra),
        compiler_params=_params(),
    )(*([recv] * N_DEV), own, *(wmv or ()))
    return out if wmv else out[0]


def _sum_gathered(name, gathered, rows):
    tr = _pick(rows, 512, 8)
    g = rows // tr

    def body(*refs):
        o_ref = refs[N_DEV]
        tot = refs[0][...]
        for s in range(1, N_DEV):
            tot = tot + refs[s][...]
        o_ref[...] = tot

    return pl.pallas_call(
        body, name=name, grid=(g,),
        in_specs=[pl.BlockSpec((tr, BLK), (lambda i, s=s: (s * g + i, 0))) for s in range(N_DEV)],
        out_specs=pl.BlockSpec((tr, BLK), lambda i: (i, 0)),
        out_shape=jax.ShapeDtypeStruct((rows, BLK), F32),
        compiler_params=_params(),
    )(*([gathered] * N_DEV))


def _adamw(name, w, g, m, v):
    r, c = w.shape
    tr = _pick(r, 256, 8) if r % 8 == 0 else r
    c1 = 1.0 - ADAM_B1 ** ADAM_STEP
    c2 = 1.0 - ADAM_B2 ** ADAM_STEP

    def body(w_ref, g_ref, m_ref, v_ref, d_ref, nm_ref, nv_ref):
        gv = g_ref[...]
        nm = ADAM_B1 * m_ref[...] + (1.0 - ADAM_B1) * gv
        nv = ADAM_B2 * v_ref[...] + (1.0 - ADAM_B2) * (gv * gv)
        m_hat = nm / c1
        v_hat = nv / c2
        d_ref[...] = -ADAM_LR * (m_hat / (jnp.sqrt(v_hat) + ADAM_EPS) + ADAM_WD * w_ref[...])
        nm_ref[...] = nm
        nv_ref[...] = nv

    spec = pl.BlockSpec((tr, c), lambda i: (i, 0))
    shape = jax.ShapeDtypeStruct((r, c), F32)
    return pl.pallas_call(
        body, name=name, grid=(r // tr,),
        in_specs=[spec] * 4, out_specs=[spec] * 3, out_shape=[shape] * 3,
        compiler_params=_params(),
    )(w, g, m, v)


def _adamw_many(name, ws, gs, ms, vs):
    n = len(ws)
    c1 = 1.0 - ADAM_B1 ** ADAM_STEP
    c2 = 1.0 - ADAM_B2 ** ADAM_STEP

    def body(*refs):
        for a in range(n):
            w_ref, g_ref, m_ref, v_ref = refs[a], refs[n + a], refs[2 * n + a], refs[3 * n + a]
            d_ref, nm_ref, nv_ref = refs[4 * n + a], refs[5 * n + a], refs[6 * n + a]
            gv = g_ref[...]
            nm = ADAM_B1 * m_ref[...] + (1.0 - ADAM_B1) * gv
            nv = ADAM_B2 * v_ref[...] + (1.0 - ADAM_B2) * (gv * gv)
            d_ref[...] = -ADAM_LR * ((nm / c1) / (jnp.sqrt(nv / c2) + ADAM_EPS) + ADAM_WD * w_ref[...])
            nm_ref[...] = nm
            nv_ref[...] = nv

    specs = [_full_spec(w.shape) for w in ws]
    shapes = [jax.ShapeDtypeStruct(w.shape, F32) for w in ws]
    out = pl.pallas_call(
        body, name=name, grid=(1,),
        in_specs=specs * 4, out_specs=specs * 3, out_shape=shapes * 3,
        compiler_params=_params(),
    )(*ws, *gs, *ms, *vs)
    return out[:n], out[n:2 * n], out[2 * n:]


PACK_ROWS = 128


def _pack(parts):
    flat = []
    for p in parts:
        v = p.reshape(-1)
        flat.append(jnp.pad(v, (0, (-v.shape[0]) % BLK)))
    v = jnp.concatenate(flat)
    v = jnp.pad(v, (0, (-v.shape[0]) % (PACK_ROWS * BLK)))
    return v.reshape(-1, BLK)


def _unpack(buf, shapes):
    flat = buf.reshape(-1)
    out, off = [], 0
    for shp in shapes:
        size = math.prod(shp)
        out.append(flat[off:off + size].reshape(shp))
        off += size + (-size) % BLK
    return out


def kernel(x, meta_tokens, norm_mix, w_in, q_norm, k_norm, attn_sinks, lam_re, lam_im, log_dt, ssm_b_re, ssm_b_im, ssm_c_re, ssm_c_im, ssm_d, w_glu, attn_branch_norm, ssm_branch_norm, w_out, norm_ffn, w_ffn_in, w_ffn_out, loss_target, m_meta_tokens, m_norm_mix, m_w_in, m_q_norm, m_k_norm, m_attn_sinks, m_lam_re, m_lam_im, m_log_dt, m_ssm_b_re, m_ssm_b_im, m_ssm_c_re, m_ssm_c_im, m_ssm_d, m_w_glu, m_attn_branch_norm, m_ssm_branch_norm, m_w_out, m_norm_ffn, m_w_ffn_in, m_w_ffn_out, v_meta_tokens, v_norm_mix, v_w_in, v_q_norm, v_k_norm, v_attn_sinks, v_lam_re, v_lam_im, v_log_dt, v_ssm_b_re, v_ssm_b_im, v_ssm_c_re, v_ssm_c_im, v_ssm_d, v_w_glu, v_attn_branch_norm, v_ssm_branch_norm, v_w_out, v_norm_ffn, v_w_ffn_in, v_w_ffn_out):
    args = dict(locals())
    weights = {n: args[n] for n in WEIGHTS}
    mom_m = {n: args["m_" + n] for n in WEIGHTS}
    mom_v = {n: args["v_" + n] for n in WEIGHTS}

    x2d = x[0]
    target2d = loss_target[0]
    s_len = x2d.shape[0]
    l_dim = s_len + BLK
    tm_row = _pick(l_dim, 320)
    tm_mm = _pick(l_dim, 1040)
    tl_tn = _pick(l_dim, 2080)
    tm_ffn = _pick(l_dim, 640)
    tm_big = _pick(l_dim, 2080)
    tm_shift = _pick(l_dim, 640, BLK)

    shard_in = w_in[0].T.astype(BF16)
    shard_glu = w_glu[0].T.astype(BF16)
    shard_out = w_out[0].astype(BF16)
    shard_ffn_in = w_ffn_in[0].T.astype(BF16)
    shard_ffn_out = w_ffn_out[0].astype(BF16)
    shard_meta = meta_tokens.T
    me = _my_index()

    def landing(shard):
        return lax.empty((N_DEV * shard.shape[0], shard.shape[1]), shard.dtype)

    first = [shard_in, shard_meta]
    ga = _send_start("gather_start_a", first, [landing(s) for s in first], chunked=False)
    later = [shard_glu + ga[4][0:1, 0:1].astype(BF16), shard_out, shard_ffn_in, shard_ffn_out]
    later_fns = [_plain_rows, _plain_rows, _ffn_in_rows, _plain_rows]
    gb = _send_start("gather_start_b", later, [landing(s) for s in later], chunked=False,
                     row_fns=later_fns)

    nm_t = norm_mix + (ga[4][0:1, 0:1] + gb[4][0:1, 0:1])
    qn_t, kn_t = jnp.tile(q_norm, (1, N_Q_HEADS)), jnp.tile(k_norm, (1, N_KV_HEADS))
    e_mat = jnp.kron(jnp.eye(4, dtype=F32), jnp.ones((HEAD_DIM, HEAD_DIM), F32)).astype(BF16)

    def disc(lr, li, ldt, br, bi):
        return _discretize(lr[0], li[0], ldt[0], br[0], bi[0])

    (abar_re, abar_im, bbar_re, bbar_im), disc_vjp = jax.vjp(disc, lam_re, lam_im, log_dt, ssm_b_re, ssm_b_im)
    wb_re, wb_im = _block_diag_b(bbar_re).astype(BF16), _block_diag_b(bbar_im).astype(BF16)
    wc_re, wc_im = _block_diag_c(ssm_c_re[0]).astype(BF16), _block_diag_c(ssm_c_im[0]).astype(BF16)
    q_ssm = _pick(l_dim, 640, 64)
    tabs = _power_tables(abar_re.reshape(1, N_STATE), abar_im.reshape(1, N_STATE), q_ssm // GRP)

    h0, xn = _embed_norm(x2d, nm_t, tm_shift)
    wt_in, meta_t = _send_wait("gather_wait_a", ga[0], ga[1], ga[2], ga[3],
                               [xn, wb_re, wb_im, wc_re, wc_im, *tabs, e_mat, qn_t, kn_t], chunked=False)
    meta_pad = jnp.pad(meta_t.T, ((PAD, 0), (0, 0)))
    h0, xn = _embed_meta(meta_pad, nm_t, h0, xn)
    qkv_w, u_end = Q_W + 2 * KV_W, Q_W + 2 * KV_W + D_MODEL
    wt_in_p = jnp.concatenate([wt_in[u_end:], wt_in[qkv_w:u_end], wt_in[:qkv_w]], axis=0)
    qkv = _matmul("proj_qkv", xn, wt_in_p, nt=True, tm=tm_big, tn=qkv_w, tk=D_MODEL, n=qkv_w, w_off=2)
    u = _matmul("proj_u", xn, wt_in_p, nt=True, tm=tm_big, tn=D_MODEL, tk=D_MODEL, n=D_MODEL, w_off=2)
    gates = _matmul("proj_gates", xn, wt_in_p, nt=True, tm=tm_big, tn=D_MODEL, tk=D_MODEL, n=2 * D_MODEL, w_off=0,
                    out_dtype=BF16)
    qn, kf, vf = _qk_prep(qkv, qn_t, kn_t, e_mat, tm_row)
    attn, lse = _attn_fwd(qn, kf, vf, attn_sinks)
    y, z, s_re, s_im = _ssm_fwd_kb(u, wb_re, wb_im, wc_re, wc_im, ssm_d, tabs, q_ssm)
    wt_glu, w_out_f, wt_ffn_in, w_ffn_out_f = _send_wait("gather_wait_b", gb[0], gb[1], gb[2], gb[3], z,
                                                         chunked=False, row_fns=later_fns)
    zab = _matmul("glu_proj", z, wt_glu, nt=True, tm=tm_big, tn=1024, tk=D_MODEL, out_dtype=BF16)
    merged = _merge_fwd(attn, zab, gates, attn_branch_norm, ssm_branch_norm, tm_row)
    h1, hn = _matmul("out_proj", merged, w_out_f, nt=False, tm=tm_mm, tn=1024, tk=D_MODEL, res=h0,
                     norm_g=norm_ffn)
    gu, act = _ffn_in_swiglu(hn, wt_ffn_in, tm_ffn)
    h2 = _matmul("ffn_out", act, w_ffn_out_f, nt=False, tm=tm_mm, tn=1024, tk=D_FF, res=h1)
    dh2, dh2_b, loss_part = _loss_grad(h2, target2d, tm_shift)

    dgu = _d_act_swiglu(dh2_b, w_ffn_out_f, gu, tm_ffn)

    def exchange_start(name, grads_b, row_fns=None):
        return _send_start(name, grads_b, [lax.empty(g.shape, BF16) for g in grads_b], chunked=True,
                           row_fns=row_fns)

    g_ffn_out, g_ffn_out_b = _matmul_tn("g_ffn_out", act, dh2_b, tm=1408, tn=1024, tl=tl_tn)
    g_ffn_in_t, g_ffn_in_b = _matmul_tn("g_ffn_in", dgu, hn, tm=1408, tn=1024, tl=tl_tn)
    ffn_fns = [_ffn_in_rows, _plain_rows]
    ex1 = exchange_start("exchange_start_ffn", [g_ffn_in_b, g_ffn_out_b], ffn_fns)
    dhn = _matmul("d_hn", dgu, wt_ffn_in, nt=False, tm=tm_ffn, tn=1024, tk=2 * D_FF, out_dtype=BF16)
    dh1, dh1_b, g_norm_ffn = _norm_bwd_res("ffn_norm_bwd", h1, norm_ffn + ex1[4][0:1, 0:1], dhn, dh2, tm_row)
    dmerged = _matmul("d_merged", dh1_b, w_out_f, nt=True, tm=tm_big, tn=1024, tk=D_MODEL, out_dtype=BF16)
    dattn, dzab, dproj, g_abn, g_sbn = _merge_bwd(attn, zab, gates, attn_branch_norm, ssm_branch_norm,
                                                    dmerged, tm_row)
    g_out, g_out_b = _matmul_tn("g_out", merged, dh1_b, tm=1024, tn=1024, tl=tl_tn)
    dz = _matmul("d_z", dzab, wt_glu, nt=False, tm=tm_mm, tn=1024, tk=2 * D_MODEL, out_dtype=BF16)
    g_glu_t, g_glu_b = _matmul_tn("g_glu", dzab, z, tm=1024, tn=1024, tl=tl_tn)
    ex2 = exchange_start("exchange_start_mix", [g_glu_b, g_out_b])
    dproj, g_ssm_d, g_ar, g_ai, g_wbr, g_wbi, g_wcr, g_wci = _ssm_bwd_kb(
        dz, y, u, s_re, s_im, wb_re, wb_im, wc_re, wc_im, ssm_d + ex2[4][0:1, 0:1], tabs, q_ssm, dproj)
    dq, dkc, dkp, dkm, dvc, dvp, dvm, g_sinks = _attn_bwd(qn, kf, vf, attn_sinks, lse, attn, dattn)
    dproj, g_qn_t, g_kn_t = _qk_bwd(qkv, qn_t, kn_t, e_mat, dq, dkc, dkp, dkm, dvc, dvp, dvm, dproj)
    g_lam_re, g_lam_im, g_log_dt, g_b_re, g_b_im = disc_vjp(
        (g_ar.reshape(SSM_GROUPS, SSM_STATE), g_ai.reshape(SSM_GROUPS, SSM_STATE),
         _block_diag_b_t(g_wbr), _block_diag_b_t(g_wbi)))
    small_grads = {
        "q_norm": g_qn_t.reshape(N_Q_HEADS, HEAD_DIM).sum(0)[None],
        "k_norm": g_kn_t.reshape(N_KV_HEADS, HEAD_DIM).sum(0)[None], "attn_sinks": g_sinks,
        "lam_re": g_lam_re, "lam_im": g_lam_im, "log_dt": g_log_dt, "ssm_b_re": g_b_re, "ssm_b_im": g_b_im,
        "ssm_c_re": _block_diag_c_t(g_wcr)[None], "ssm_c_im": _block_diag_c_t(g_wci)[None],
        "ssm_d": g_ssm_d, "attn_branch_norm": g_abn, "ssm_branch_norm": g_sbn, "norm_ffn": g_norm_ffn,
    }
    early = [n for n in SMALL if n != "norm_mix"]
    packed_e = _pack([small_grads[n] for n in early])
    gs_e = _send_start("small_start_a", [packed_e], [landing(packed_e)], chunked=False)

    blocks_g, blocks_u, blocks_q = 2 * D_MODEL // 512, D_MODEL // 512, qkv_w // 512
    back = lambda i: jnp.where(i < blocks_g, i + blocks_q + blocks_u,
                               jnp.where(i < blocks_g + blocks_u, i - blocks_g + blocks_q, i - blocks_g - blocks_u))
    g_in_t, g_in_b = _matmul_tn("g_in", dproj, xn, tm=512, tn=1024, tl=tl_tn, after=gs_e[4], out_rows=back)
    ex3 = exchange_start("exchange_start_in", [g_in_b])
    dxn = _matmul("d_xn", dproj, wt_in_p, nt=False, tm=tm_ffn, tn=1024, tk=IN_COLS, out_dtype=BF16)
    grad_x2d, dmeta_blk, g_norm_mix = _final_bwd(h0, nm_t + ex3[4][0:1, 0:1], dxn, dh1, _pick(s_len, 512, BLK))
    packed_l = _pack([g_norm_mix, dmeta_blk[PAD:], loss_part])
    gs_l = _send_start("small_start_b", [packed_l], [landing(packed_l)], chunked=False)
    grads, deltas, new_m, new_v = {}, {}, {}, {}

    recv_ffn_in, recv_ffn_out = _send_wait("exchange_wait_ffn", ex1[0], ex1[1], ex1[2], ex1[3], gs_l[4],
                                           chunked=True, row_fns=ffn_fns)
    recv_glu, recv_out = _send_wait("exchange_wait_mix", ex2[0], ex2[1], ex2[2], ex2[3], recv_ffn_in,
                                    chunked=True)
    (recv_in,) = _send_wait("exchange_wait_in", ex3[0], ex3[1], ex3[2], ex3[3], recv_glu, chunked=True)
    big = [("w_in", g_in_t, True, recv_in, _plain_rows), ("w_glu", g_glu_t, True, recv_glu, _plain_rows),
           ("w_out", g_out, False, recv_out, _plain_rows), ("w_ffn_in", g_ffn_in_t, True, recv_ffn_in, _ffn_in_rows),
           ("w_ffn_out", g_ffn_out, False, recv_ffn_out, _plain_rows)]
    for name, g_full, transposed, recv, row_fn in big:
        m_rows = g_full.shape[0] // N_DEV
        own = lax.dynamic_slice(g_full, (row_fn(me, m_rows), 0), (m_rows, g_full.shape[1]))
        if name == "w_glu":
            grads[name] = _sum_slots("sum_" + name, recv, own).T[None]
            continue
        shp = weights[name].shape
        if transposed:
            as2d, back = (lambda a: a.reshape(shp[-2], shp[-1]).T), (lambda a: a.T.reshape(shp))
        else:
            as2d, back = (lambda a: a.reshape(shp[-2], shp[-1])), (lambda a: a.reshape(shp))
        g_shard, d, nm, nv = _sum_slots("sum_adamw_" + name, recv, own,
                                        (as2d(weights[name]), as2d(mom_m[name]), as2d(mom_v[name])))
        grads[name], deltas[name], new_m[name], new_v[name] = back(g_shard), back(d), back(nm), back(nv)
        last = d

    def adamw_2d(name):
        shp = weights[name].shape
        as2d = lambda a: a.reshape(shp[-2], shp[-1])
        d, nm, nv = _adamw("adamw_" + name, as2d(weights[name]), as2d(grads[name]), as2d(mom_m[name]),
                           as2d(mom_v[name]))
        deltas[name], new_m[name], new_v[name] = d.reshape(shp), nm.reshape(shp), nv.reshape(shp)

    adamw_2d("w_glu")

    def small_sum(tag, gs, packed, after):
        (gathered,) = _send_wait("small_wait_" + tag, gs[0], gs[1], gs[2], gs[3], after, chunked=False)
        return _sum_gathered("sum_small_" + tag, gathered, packed.shape[0])

    def small_adamw(tag, names):
        view = lambda n, a: jnp.swapaxes(a, -1, -2) if n.startswith("ssm_b") else a
        d, nm, nv = _adamw_many("adamw_small_" + tag, [view(n, weights[n]) for n in names],
                                [view(n, grads[n]) for n in names], [view(n, mom_m[n]) for n in names],
                                [view(n, mom_v[n]) for n in names])
        deltas.update((n, view(n, a)) for n, a in zip(names, d))
        new_m.update((n, view(n, a)) for n, a in zip(names, nm))
        new_v.update((n, view(n, a)) for n, a in zip(names, nv))

    g_sum_e = small_sum("a", gs_e, packed_e, last)
    grads.update(zip(early, _unpack(g_sum_e, [weights[n].shape for n in early])))
    wide = [n for n in early if n.startswith(("ssm_b", "ssm_c"))]
    small_adamw("wide", wide)
    g_sum_l = small_sum("b", gs_l, packed_l, g_sum_e)
    grads["norm_mix"], g_meta, loss_sum = _unpack(g_sum_l, [weights["norm_mix"].shape, (N_META, D_MODEL), (1, 1)])
    small_adamw("rest", [n for n in SMALL if n not in wide])
    grads["meta_tokens"] = lax.dynamic_slice(g_meta, (0, me * BLK), (N_META, BLK))
    adamw_2d("meta_tokens")

    loss = loss_sum[0, 0]
    return (loss, grad_x2d[None], *[grads[n] for n in WEIGHTS], *[deltas[n] for n in WEIGHTS],
            *[new_m[n] for n in WEIGHTS], *[new_v[n] for n in WEIGHTS])
```

```python
import math

import jax
import jax.numpy as jnp
from jax import lax
from jax.experimental import pallas as pl
from jax.experimental.pallas import tpu as pltpu

F32 = jnp.float32
BF16 = jnp.bfloat16

D_MODEL = 1024
N_META = 16
HEAD_DIM = 64
N_Q_HEADS = 16
N_KV_HEADS = 4
Q_W = N_Q_HEADS * HEAD_DIM
KV_W = N_KV_HEADS * HEAD_DIM
SSM_GROUPS = 64
SSM_GROUP_CH = 16
SSM_STATE = 64
N_STATE = SSM_GROUPS * SSM_STATE
D_FF = 2816
IN_COLS = Q_W + 2 * KV_W + 3 * D_MODEL
EPS = 1e-6
BLK = 128
PAD = BLK - N_META
N_DEV = 8
NEG = -1e30
SSM_KB = 8
ST_KB = N_STATE // SSM_KB
LB_KB = ST_KB // BLK
N_LB = N_STATE // BLK

ADAM_LR = 0.001
ADAM_B1 = 0.9
ADAM_B2 = 0.999
ADAM_EPS = 1e-08
ADAM_WD = 0.01
ADAM_STEP = 10

VMEM_LIMIT = 48 * 1024 * 1024
MESH = pl.DeviceIdType.MESH

SMALL = ["norm_mix", "q_norm", "k_norm", "attn_sinks", "lam_re", "lam_im", "log_dt", "ssm_b_re", "ssm_b_im",
         "ssm_c_re", "ssm_c_im", "ssm_d", "attn_branch_norm", "ssm_branch_norm", "norm_ffn"]
WEIGHTS = ["meta_tokens", "norm_mix", "w_in", "q_norm", "k_norm", "attn_sinks", "lam_re", "lam_im", "log_dt",
           "ssm_b_re", "ssm_b_im", "ssm_c_re", "ssm_c_im", "ssm_d", "w_glu", "attn_branch_norm",
           "ssm_branch_norm", "w_out", "norm_ffn", "w_ffn_in", "w_ffn_out"]


def _params(**kw):
    return pltpu.CompilerParams(vmem_limit_bytes=VMEM_LIMIT, **kw)


def _pick(n, cap, mult=16):
    best = None
    for d in range(mult, min(n, cap) + 1, mult):
        if n % d == 0:
            best = d
    assert best is not None, (n, cap, mult)
    return best


def _my_index():
    return 4 * lax.axis_index("x") + 2 * lax.axis_index("y") + lax.axis_index("c")


def _rms(x, g):
    r = lax.rsqrt(jnp.mean(x * x, axis=-1, keepdims=True) + EPS)
    return x * r * g


def _rms_bwd(x, g, dy):
    r = lax.rsqrt(jnp.mean(x * x, axis=-1, keepdims=True) + EPS)
    t = dy * g
    dx = r * t - x * (r * r * r) * jnp.mean(t * x, axis=-1, keepdims=True)
    dg = jnp.sum(dy * (x * r), axis=0, keepdims=True)
    return dx, dg


def _sigmoid(x):
    return jax.nn.sigmoid(x)


def _gelu(x):
    k = math.sqrt(2.0 / math.pi)
    return 0.5 * x * (1.0 + jnp.tanh(k * (x + 0.044715 * (x * x * x))))


def _gelu_grad(x):
    k = math.sqrt(2.0 / math.pi)
    t = jnp.tanh(k * (x + 0.044715 * (x * x * x)))
    return 0.5 * (1.0 + t) + 0.5 * x * (1.0 - t * t) * (k * (1.0 + 3.0 * 0.044715 * (x * x)))


def _head_mean(x, e_ref):
    hi = x.astype(BF16)
    lo = (x - hi.astype(F32)).astype(BF16)
    e = e_ref[...]
    out = []
    for b in range(x.shape[1] // 256):
        sl = slice(256 * b, 256 * b + 256)
        s = (jnp.dot(hi[:, sl], e, preferred_element_type=F32)
             + jnp.dot(lo[:, sl], e, preferred_element_type=F32))
        out.append(s)
    s = out[0] if len(out) == 1 else jnp.concatenate(out, axis=1)
    return s * (1.0 / HEAD_DIM)


def _head_rms(x, g, e_ref):
    r = lax.rsqrt(_head_mean(x * x, e_ref) + EPS)
    return x * r * g


def _head_rms_bwd(x, g, dy, e_ref):
    r = lax.rsqrt(_head_mean(x * x, e_ref) + EPS)
    t = dy * g
    dx = r * t - x * (r * r * r) * _head_mean(t * x, e_ref)
    dg = jnp.sum(dy * (x * r), axis=0, keepdims=True)
    return dx, dg


def _lane_half(shape):
    lane = lax.broadcasted_iota(jnp.int32, shape, len(shape) - 1)
    return (lane >> 6) & 1


def _matmul(name, a, w, *, nt, tm, tn, tk, n=None, w_off=0, res=None, norm_g=None, out_dtype=F32):
    m_dim, k_dim = a.shape
    n_dim = n if n is not None else (w.shape[0] if nt else w.shape[1])
    gm, gn, gk = m_dim // tm, n_dim // tn, k_dim // tk
    assert gm * tm == m_dim and gn * tn == n_dim and gk * tk == k_dim, (name, a.shape, w.shape, tm, tn, tk)
    assert norm_g is None or tn == n_dim
    direct = out_dtype == F32
    dn = (((1,), (1,)), ((), ())) if nt else (((1,), (0,)), ((), ()))

    def body(*refs):
        refs = list(refs)
        a_ref, w_ref = refs[0], refs[1]
        pos = 2
        r_ref = g_ref = on_ref = None
        if res is not None:
            r_ref, pos = refs[pos], pos + 1
        if norm_g is not None:
            g_ref, pos = refs[pos], pos + 1
        o_ref, pos = refs[pos], pos + 1
        if norm_g is not None:
            on_ref, pos = refs[pos], pos + 1
        part = lax.dot_general(a_ref[...], w_ref[...], dn, preferred_element_type=F32)
        if gk == 1:
            r = part if r_ref is None else r_ref[...] + part
            o_ref[...] = r.astype(out_dtype)
            if on_ref is not None:
                on_ref[...] = _rms(r, g_ref[...]).astype(BF16)
            return
        acc = o_ref if direct else refs[pos]
        k = pl.program_id(2)

        @pl.when(k == 0)
        def _():
            acc[...] = part if r_ref is None or not direct else r_ref[...] + part

        @pl.when(k > 0)
        def _():
            acc[...] += part

        @pl.when(k == gk - 1)
        def _():
            if not direct:
                r = acc[...]
                if r_ref is not None:
                    r = r_ref[...] + r
                o_ref[...] = r.astype(out_dtype)
            if on_ref is not None:
                on_ref[...] = _rms(o_ref[...].astype(F32), g_ref[...]).astype(BF16)

    if nt:
        w_spec = pl.BlockSpec((tn, tk), lambda i, j, k: (j + w_off, k))
    else:
        w_spec = pl.BlockSpec((tk, tn), lambda i, j, k: (k, j))
    in_specs = [pl.BlockSpec((tm, tk), lambda i, j, k: (i, k)), w_spec]
    args = [a, w]
    out_spec = pl.BlockSpec((tm, tn), lambda i, j, k: (i, j))
    out_specs, out_shape = [out_spec], [jax.ShapeDtypeStruct((m_dim, n_dim), out_dtype)]
    if res is not None:
        in_specs.append(out_spec)
        args.append(res)
    if norm_g is not None:
        in_specs.append(pl.BlockSpec((1, tn), lambda i, j, k: (0, 0)))
        args.append(norm_g)
        out_specs.append(out_spec)
        out_shape.append(jax.ShapeDtypeStruct((m_dim, n_dim), BF16))
    out = pl.pallas_call(
        body, name=name, grid=(gm, gn, gk),
        in_specs=in_specs, out_specs=out_specs, out_shape=out_shape,
        scratch_shapes=[] if direct or gk == 1 else [pltpu.VMEM((tm, tn), F32)],
        compiler_params=_params(dimension_semantics=("parallel", "parallel", "arbitrary")),
    )(*args)
    return out if norm_g is not None else out[0]


def _matmul_tn(name, a, b, *, tm, tn, tl, after=None, out_rows=None):
    l_dim, m_dim = a.shape
    n_dim = b.shape[1]
    gm, gn, gl = m_dim // tm, n_dim // tn, l_dim // tl
    assert gm * tm == m_dim and gn * tn == n_dim and gl * tl == l_dim, (name, a.shape, b.shape, tm, tn, tl)

    def body(*refs):
        a_ref, b_ref = refs[0], refs[1]
        o_ref, ob_ref = refs[-2], refs[-1]

        @pl.when(pl.program_id(2) == 0)
        def _():
            o_ref[...] = jnp.zeros_like(o_ref)

        o_ref[...] += lax.dot_general(a_ref[...], b_ref[...], (((0,), (0,)), ((), ())),
                                      preferred_element_type=F32)

        @pl.when(pl.program_id(2) == gl - 1)
        def _():
            ob_ref[...] = o_ref[...].astype(BF16)

    out_row = out_rows if out_rows is not None else (lambda i: i)
    out_spec = pl.BlockSpec((tm, tn), lambda i, j, l: (out_row(i), j))
    in_specs = [pl.BlockSpec((tl, tm), lambda i, j, l: (l, i)), pl.BlockSpec((tl, tn), lambda i, j, l: (l, j))]
    args = [a, b]
    if after is not None:
        in_specs.append(pl.BlockSpec(memory_space=pl.ANY))
        args.append(after)
    return pl.pallas_call(
        body, name=name, grid=(gm, gn, gl),
        in_specs=in_specs,
        out_specs=[out_spec, out_spec],
        out_shape=[jax.ShapeDtypeStruct((m_dim, n_dim), F32), jax.ShapeDtypeStruct((m_dim, n_dim), BF16)],
        compiler_params=_params(dimension_semantics=("parallel", "parallel", "arbitrary")),
    )(*args)


def _row_spec(tm, cols, f=None):
    if f is None:
        return pl.BlockSpec((tm, cols), lambda i: (i, 0))
    return pl.BlockSpec((tm, cols), lambda i: (f(i), 0))


def _full_spec(shape):
    nd = len(shape)
    return pl.BlockSpec(shape, lambda i: (0,) * nd)


def _shifted_specs(n_sub, n_blocks):
    return [_row_spec(BLK, D_MODEL, (lambda i, k=k: jnp.clip(n_sub * i - 1 + k, 0, n_blocks - 1)))
            for k in range(n_sub)]


def _embed_norm(x2d, g, tm):
    s_len = x2d.shape[0]
    l_dim = s_len + BLK
    n_sub = tm // BLK

    def body(*refs):
        x_refs, g_ref, h_ref, xn_ref = refs[:n_sub], refs[n_sub], refs[n_sub + 1], refs[n_sub + 2]
        i = pl.program_id(0)
        for k in range(n_sub):
            rows = slice(BLK * k, BLK * k + BLK)
            h = x_refs[k][...] * jnp.where(n_sub * i + k >= 1, 1.0, 0.0)
            h_ref[rows, :] = h
            xn_ref[rows, :] = _rms(h, g_ref[...]).astype(BF16)

    return pl.pallas_call(
        body, name="embed_norm", grid=(l_dim // tm,),
        in_specs=_shifted_specs(n_sub, s_len // BLK) + [_full_spec((1, D_MODEL))],
        out_specs=[_row_spec(tm, D_MODEL), _row_spec(tm, D_MODEL)],
        out_shape=[jax.ShapeDtypeStruct((l_dim, D_MODEL), F32),
                   jax.ShapeDtypeStruct((l_dim, D_MODEL), BF16)],
        compiler_params=_params(),
    )(*([x2d] * n_sub), g)


def _embed_meta(meta_pad, g, h0, xn):
    def body(mp_ref, g_ref, h_in, xn_in, h_ref, xn_ref):
        h_ref[...] = mp_ref[...]
        xn_ref[...] = _rms(mp_ref[...], g_ref[...]).astype(BF16)

    any_spec = pl.BlockSpec(memory_space=pl.ANY)
    return pl.pallas_call(
        body, name="embed_meta", grid=(1,),
        in_specs=[_full_spec((BLK, D_MODEL)), _full_spec((1, D_MODEL)), any_spec, any_spec],
        out_specs=[_row_spec(BLK, D_MODEL), _row_spec(BLK, D_MODEL)],
        out_shape=[jax.ShapeDtypeStruct(h0.shape, F32), jax.ShapeDtypeStruct(xn.shape, BF16)],
        input_output_aliases={2: 0, 3: 1},
        compiler_params=_params(),
    )(meta_pad, g, h0, xn)


KVX_W = 2 * N_KV_HEADS * BLK


def _qk_prep(qkv, q_norm_t, k_norm_t, e_mat, tm):
    l_dim = qkv.shape[0]

    def body(x_ref, qg_ref, kg_ref, e_ref, q_ref, kf_ref, vf_ref):
        x = x_ref[...]
        q = _head_rms(x[:, :Q_W], qg_ref[...], e_ref) * (HEAD_DIM ** -0.5)
        q_ref[...] = q.astype(BF16)
        k = _head_rms(x[:, Q_W:Q_W + KV_W], kg_ref[...], e_ref)
        v = x[:, Q_W + KV_W:Q_W + 2 * KV_W]
        half = _lane_half((tm, BLK))
        for src, dst in ((k, kf_ref), (v, vf_ref)):
            for kv in range(N_KV_HEADS):
                blk = src[:, BLK * (kv // 2):BLK * (kv // 2) + BLK]
                swapped = pltpu.roll(blk, HEAD_DIM, axis=1)
                for e in range(2):
                    val = blk if kv % 2 == e else swapped
                    idx = 2 * kv + e
                    dst[:, BLK * idx:BLK * idx + BLK] = jnp.where(half == e, val, 0.0).astype(BF16)

    return pl.pallas_call(
        body, name="qk_prep", grid=(l_dim // tm,),
        in_specs=[_row_spec(tm, Q_W + 2 * KV_W), _full_spec((1, Q_W)), _full_spec((1, KV_W)),
                  _full_spec((256, 256))],
        out_specs=[_row_spec(tm, Q_W), _row_spec(tm, KVX_W), _row_spec(tm, KVX_W)],
        out_shape=[jax.ShapeDtypeStruct((l_dim, Q_W), BF16),
                   jax.ShapeDtypeStruct((l_dim, KVX_W), BF16),
                   jax.ShapeDtypeStruct((l_dim, KVX_W), BF16)],
        compiler_params=_params(),
    )(qkv, q_norm_t, k_norm_t, e_mat)


def _merge_fwd(attn, zab, gates, abn, sbn, tm):
    l_dim = attn.shape[0]

    def body(a_ref, z_ref, g_ref, an_ref, sn_ref, o_ref):
        z = z_ref[...].astype(F32)
        g = g_ref[...].astype(F32)
        ssm = z[:, :D_MODEL] * _sigmoid(z[:, D_MODEL:])
        merged = (_sigmoid(g[:, :D_MODEL]) * _rms(a_ref[...], an_ref[...])
                  + _sigmoid(g[:, D_MODEL:]) * _rms(ssm, sn_ref[...]))
        o_ref[...] = merged.astype(BF16)

    return pl.pallas_call(
        body, name="merge_fwd", grid=(l_dim // tm,),
        in_specs=[_row_spec(tm, D_MODEL), _row_spec(tm, 2 * D_MODEL), _row_spec(tm, 2 * D_MODEL),
                  _full_spec((1, D_MODEL)), _full_spec((1, D_MODEL))],
        out_specs=_row_spec(tm, D_MODEL),
        out_shape=jax.ShapeDtypeStruct((l_dim, D_MODEL), BF16),
        compiler_params=_params(),
    )(attn, zab, gates, abn, sbn)


def _merge_bwd(attn, zab, gates, abn, sbn, dmerged, tm):
    l_dim = attn.shape[0]

    def body(a_ref, z_ref, g_ref, an_ref, sn_ref, dm_ref, da_ref, dz_ref, dg_ref, dan_ref, dsn_ref):
        @pl.when(pl.program_id(0) == 0)
        def _():
            dan_ref[...] = jnp.zeros_like(dan_ref)
            dsn_ref[...] = jnp.zeros_like(dsn_ref)

        z = z_ref[...].astype(F32)
        g = g_ref[...].astype(F32)
        dm = dm_ref[...].astype(F32)
        attn_v = a_ref[...]
        za, zb = z[:, :D_MODEL], z[:, D_MODEL:]
        sb = _sigmoid(zb)
        ssm = za * sb
        s_ga, s_gs = _sigmoid(g[:, :D_MODEL]), _sigmoid(g[:, D_MODEL:])
        a_n = _rms(attn_v, an_ref[...])
        s_n = _rms(ssm, sn_ref[...])
        dg_ref[:, :D_MODEL] = (dm * a_n * s_ga * (1.0 - s_ga)).astype(BF16)
        dg_ref[:, D_MODEL:] = (dm * s_n * s_gs * (1.0 - s_gs)).astype(BF16)
        dattn, dan = _rms_bwd(attn_v, an_ref[...], dm * s_ga)
        dssm, dsn = _rms_bwd(ssm, sn_ref[...], dm * s_gs)
        da_ref[...] = dattn.astype(BF16)
        dz_ref[:, :D_MODEL] = (dssm * sb).astype(BF16)
        dz_ref[:, D_MODEL:] = (dssm * za * sb * (1.0 - sb)).astype(BF16)
        dan_ref[...] += dan
        dsn_ref[...] += dsn

    return pl.pallas_call(
        body, name="merge_bwd", grid=(l_dim // tm,),
        in_specs=[_row_spec(tm, D_MODEL), _row_spec(tm, 2 * D_MODEL), _row_spec(tm, 2 * D_MODEL),
                  _full_spec((1, D_MODEL)), _full_spec((1, D_MODEL)), _row_spec(tm, D_MODEL)],
        out_specs=[_row_spec(tm, D_MODEL), _row_spec(tm, 2 * D_MODEL), _row_spec(tm, 2 * D_MODEL),
                   _full_spec((1, D_MODEL)), _full_spec((1, D_MODEL))],
        out_shape=[jax.ShapeDtypeStruct((l_dim, D_MODEL), BF16),
                   jax.ShapeDtypeStruct((l_dim, 2 * D_MODEL), BF16),
                   jax.ShapeDtypeStruct((l_dim, IN_COLS), BF16),
                   jax.ShapeDtypeStruct((1, D_MODEL), F32), jax.ShapeDtypeStruct((1, D_MODEL), F32)],
        compiler_params=_params(),
    )(attn, zab, gates, abn, sbn, dmerged)


FF_TILE = D_FF // 2


def _ffn_in_swiglu(hn, wt_ffn_in, tm):
    l_dim = hn.shape[0]
    nt = (((1,), (1,)), ((), ()))

    def body(a_ref, w_ref, gu_ref, act_ref):
        r = lax.dot_general(a_ref[...], w_ref[...], nt, preferred_element_type=F32)
        gate, up = r[:, :FF_TILE], r[:, FF_TILE:]
        gu_ref[...] = r.astype(BF16)
        act_ref[...] = (gate * _sigmoid(gate) * up).astype(BF16)

    return pl.pallas_call(
        body, name="ffn_in_swiglu", grid=(l_dim // tm, 2),
        in_specs=[pl.BlockSpec((tm, D_MODEL), lambda i, j: (i, 0)),
                  pl.BlockSpec((2 * FF_TILE, D_MODEL), lambda i, j: (j, 0))],
        out_specs=[pl.BlockSpec((tm, 2 * FF_TILE), lambda i, j: (i, j)),
                   pl.BlockSpec((tm, FF_TILE), lambda i, j: (i, j))],
        out_shape=[jax.ShapeDtypeStruct((l_dim, 2 * D_FF), BF16), jax.ShapeDtypeStruct((l_dim, D_FF), BF16)],
        compiler_params=_params(dimension_semantics=("parallel", "parallel")),
    )(hn, wt_ffn_in)


def _d_act_swiglu(dh2_b, w_ffn_out, gu, tm):
    l_dim = dh2_b.shape[0]
    nt = (((1,), (1,)), ((), ()))

    def body(d_ref, w_ref, gu_ref, o_ref):
        d = lax.dot_general(d_ref[...], w_ref[...], nt, preferred_element_type=F32)
        gate = gu_ref[:, :FF_TILE].astype(F32)
        up = gu_ref[:, FF_TILE:].astype(F32)
        s = _sigmoid(gate)
        o_ref[:, :FF_TILE] = (d * up * (s * (1.0 + gate * (1.0 - s)))).astype(BF16)
        o_ref[:, FF_TILE:] = (d * (gate * s)).astype(BF16)

    return pl.pallas_call(
        body, name="d_act_swiglu", grid=(l_dim // tm, 2),
        in_specs=[pl.BlockSpec((tm, D_MODEL), lambda i, j: (i, 0)),
                  pl.BlockSpec((FF_TILE, D_MODEL), lambda i, j: (j, 0)),
                  pl.BlockSpec((tm, 2 * FF_TILE), lambda i, j: (i, j))],
        out_specs=pl.BlockSpec((tm, 2 * FF_TILE), lambda i, j: (i, j)),
        out_shape=jax.ShapeDtypeStruct((l_dim, 2 * D_FF), BF16),
        compiler_params=_params(dimension_semantics=("parallel", "parallel")),
    )(dh2_b, w_ffn_out, gu)


def _loss_grad(h2, target2d, tm):
    l_dim = h2.shape[0]
    n_sub = tm // BLK

    def body(*refs):
        h_ref, t_refs = refs[0], refs[1:1 + n_sub]
        d_ref, db_ref, loss_ref = refs[1 + n_sub:]
        i = pl.program_id(0)

        @pl.when(i == 0)
        def _():
            loss_ref[...] = jnp.zeros_like(loss_ref)

        for k in range(n_sub):
            rows = slice(BLK * k, BLK * k + BLK)
            real = jnp.where(n_sub * i + k >= 1, 1.0, 0.0)
            err = (h_ref[rows, :] - t_refs[k][...]) * real
            d = err * (1.0 / D_MODEL)
            d_ref[rows, :] = d
            db_ref[rows, :] = d.astype(BF16)
            loss_ref[...] += 0.5 * jnp.sum(jnp.mean(err * err, axis=-1, keepdims=True), axis=0, keepdims=True)

    return pl.pallas_call(
        body, name="loss_grad", grid=(l_dim // tm,),
        in_specs=[_row_spec(tm, D_MODEL)] + _shifted_specs(n_sub, target2d.shape[0] // BLK),
        out_specs=[_row_spec(tm, D_MODEL), _row_spec(tm, D_MODEL), _full_spec((1, 1))],
        out_shape=[jax.ShapeDtypeStruct((l_dim, D_MODEL), F32), jax.ShapeDtypeStruct((l_dim, D_MODEL), BF16),
                   jax.ShapeDtypeStruct((1, 1), F32)],
        compiler_params=_params(),
    )(h2, *([target2d] * n_sub))


def _norm_bwd_res(name, h, g, dy, dres, tm):
    l_dim = h.shape[0]

    def body(h_ref, g_ref, dy_ref, dr_ref, o_ref, ob_ref, dg_ref):
        @pl.when(pl.program_id(0) == 0)
        def _():
            dg_ref[...] = jnp.zeros_like(dg_ref)

        dx, dg = _rms_bwd(h_ref[...], g_ref[...], dy_ref[...].astype(F32))
        out = dr_ref[...] + dx
        o_ref[...] = out
        ob_ref[...] = out.astype(BF16)
        dg_ref[...] += dg

    return pl.pallas_call(
        body, name=name, grid=(l_dim // tm,),
        in_specs=[_row_spec(tm, D_MODEL), _full_spec((1, D_MODEL)), _row_spec(tm, D_MODEL), _row_spec(tm, D_MODEL)],
        out_specs=[_row_spec(tm, D_MODEL), _row_spec(tm, D_MODEL), _full_spec((1, D_MODEL))],
        out_shape=[jax.ShapeDtypeStruct((l_dim, D_MODEL), F32), jax.ShapeDtypeStruct((l_dim, D_MODEL), BF16),
                   jax.ShapeDtypeStruct((1, D_MODEL), F32)],
        compiler_params=_params(),
    )(h, g, dy, dres)


def _final_bwd(h0, g, dxn, dh1, tm):
    l_dim = h0.shape[0]
    n_sub = tm // BLK
    n_tiles = (l_dim - BLK) // tm

    def sub_specs():
        return [_row_spec(BLK, D_MODEL, (lambda j, k=k: jnp.where(j < n_tiles, n_sub * j + 1 + k, 0)))
                for k in range(n_sub)]

    def body(*refs):
        h_refs, g_ref = refs[:n_sub], refs[n_sub]
        dy_refs, dr_refs = refs[n_sub + 1:2 * n_sub + 1], refs[2 * n_sub + 1:3 * n_sub + 1]
        gx_ref, gm_ref, dg_ref = refs[3 * n_sub + 1:]
        j = pl.program_id(0)

        @pl.when(j == 0)
        def _():
            dg_ref[...] = jnp.zeros_like(dg_ref)

        def block(k):
            dx, dg = _rms_bwd(h_refs[k][...], g_ref[...], dy_refs[k][...].astype(F32))
            dg_ref[...] += dg
            return dr_refs[k][...] + dx

        @pl.when(j < n_tiles)
        def _():
            for k in range(n_sub):
                gx_ref[BLK * k:BLK * k + BLK, :] = block(k)

        @pl.when(j == n_tiles)
        def _():
            gm_ref[...] = block(0)

    return pl.pallas_call(
        body, name="final_bwd", grid=(n_tiles + 1,),
        in_specs=sub_specs() + [_full_spec((1, D_MODEL))] + sub_specs() + sub_specs(),
        out_specs=[_row_spec(tm, D_MODEL, lambda j: jnp.minimum(j, n_tiles - 1)), _full_spec((BLK, D_MODEL)),
                   _full_spec((1, D_MODEL))],
        out_shape=[jax.ShapeDtypeStruct((l_dim - BLK, D_MODEL), F32), jax.ShapeDtypeStruct((BLK, D_MODEL), F32),
                   jax.ShapeDtypeStruct((1, D_MODEL), F32)],
        compiler_params=_params(),
    )(*([h0] * n_sub), g, *([dxn] * n_sub), *([dh1] * n_sub))


def _attn_valid(n):
    shape = (2 * BLK, 3 * BLK)
    qi = lax.broadcasted_iota(jnp.int32, shape, 0) & (BLK - 1)
    col = lax.broadcasted_iota(jnp.int32, shape, 1)
    kj = col & (BLK - 1)
    part = col >> 7
    nn = jnp.zeros(shape, jnp.int32) + n
    meta_ok = (part == 0) & (kj >= PAD) & (nn >= 1)
    prev_ok = (part == 1) & (kj > qi) & (nn >= 2)
    cur_ok = (part == 2) & (kj <= qi) & ((nn >= 1) | (kj >= PAD))
    return meta_ok | prev_ok | cur_ok


def _attn_scores(q_ref, kwin, sk_ref, valid, kv, e):
    qs = jnp.concatenate([q_ref[:, BLK * (2 * kv):BLK * (2 * kv) + BLK],
                          q_ref[:, BLK * (2 * kv + 1):BLK * (2 * kv + 1) + BLK]], axis=0)
    s = lax.dot_general(qs, kwin, (((1,), (1,)), ((), ())), preferred_element_type=F32)
    h0 = 4 * kv + e
    row = lax.broadcasted_iota(jnp.int32, (2 * BLK, 1), 0)
    sink = jnp.where(row < BLK, sk_ref[:, h0:h0 + 1], sk_ref[:, h0 + 2:h0 + 3])
    return qs, jnp.where(valid, s, NEG), sink


def _attn_specs(nb):
    prev = lambda i: jnp.maximum(i - 1, 0)
    zero = lambda i: 0
    kv_specs = [_row_spec(BLK, KVX_W, zero), _row_spec(BLK, KVX_W, prev), _row_spec(BLK, KVX_W)]
    return kv_specs


def _attn_fwd(qn, kf, vf, sinks):
    l_dim = qn.shape[0]
    nb = l_dim // BLK

    def body(q_ref, km_ref, kp_ref, kc_ref, vm_ref, vp_ref, vc_ref, sk_ref, o_ref, lse_ref):
        valid = _attn_valid(pl.program_id(0))
        lane = lax.broadcasted_iota(jnp.int32, (BLK, BLK), 1)
        lse_all = jnp.zeros((BLK, BLK), F32)
        for kv in range(N_KV_HEADS):
            outs = []
            for e in range(2):
                sl = slice(BLK * (2 * kv + e), BLK * (2 * kv + e) + BLK)
                kwin = jnp.concatenate([km_ref[:, sl], kp_ref[:, sl], kc_ref[:, sl]], axis=0)
                vwin = jnp.concatenate([vm_ref[:, sl], vp_ref[:, sl], vc_ref[:, sl]], axis=0)
                _, s, sink = _attn_scores(q_ref, kwin, sk_ref, valid, kv, e)
                m = jnp.maximum(jnp.max(s, axis=-1, keepdims=True), sink)
                ex = jnp.exp(s - m)
                den = jnp.sum(ex, axis=-1, keepdims=True) + jnp.exp(sink - m)
                p = ex * (1.0 / den)
                outs.append(jnp.dot(p.astype(BF16), vwin, preferred_element_type=F32))
                lse = m + jnp.log(den)
                lse_all = jnp.where(lane == 4 * kv + e, lse[:BLK], lse_all)
                lse_all = jnp.where(lane == 4 * kv + 2 + e, lse[BLK:], lse_all)
            o = outs[0] + outs[1]
            o_ref[:, BLK * (2 * kv):BLK * (2 * kv) + BLK] = o[:BLK]
            o_ref[:, BLK * (2 * kv + 1):BLK * (2 * kv + 1) + BLK] = o[BLK:]
        lse_ref[...] = lse_all

    kv_specs = _attn_specs(nb)
    return pl.pallas_call(
        body, name="attn_fwd", grid=(nb,),
        in_specs=[_row_spec(BLK, Q_W)] + kv_specs + kv_specs + [_full_spec((1, N_Q_HEADS))],
        out_specs=[_row_spec(BLK, Q_W), _row_spec(BLK, BLK)],
        out_shape=[jax.ShapeDtypeStruct((l_dim, Q_W), F32), jax.ShapeDtypeStruct((l_dim, BLK), F32)],
        compiler_params=_params(),
    )(qn, kf, kf, kf, vf, vf, vf, sinks)


def _attn_bwd(qn, kf, vf, sinks, lse, attn, dattn):
    l_dim = qn.shape[0]
    nb = l_dim // BLK
    wide = KV_W
    tn = (((0,), (0,)), ((), ()))
    nt = (((1,), (1,)), ((), ()))

    def body(q_ref, km_ref, kp_ref, kc_ref, vm_ref, vp_ref, vc_ref, sk_ref, lse_ref, o_ref, do_ref,
             dq_ref, dkc_ref, dkp_ref, dkm_ref, dvc_ref, dvp_ref, dvm_ref, dsk_ref):
        @pl.when(pl.program_id(0) == 0)
        def _():
            dkm_ref[...] = jnp.zeros_like(dkm_ref)
            dvm_ref[...] = jnp.zeros_like(dvm_ref)
            dsk_ref[...] = jnp.zeros_like(dsk_ref)

        valid = _attn_valid(pl.program_id(0))
        half = _lane_half((BLK, BLK))
        half2 = _lane_half((2 * BLK, BLK))
        half3 = _lane_half((3 * BLK, BLK))
        lane16 = lax.broadcasted_iota(jnp.int32, (1, N_Q_HEADS), 1)
        dsk = jnp.zeros((1, N_Q_HEADS), F32)
        folded_k, folded_v = [], []
        for kv in range(N_KV_HEADS):
            j0, j1 = 2 * kv, 2 * kv + 1
            do0 = do_ref[:, BLK * j0:BLK * j0 + BLK]
            do1 = do_ref[:, BLK * j1:BLK * j1 + BLK]
            do0f, do1f = do0.astype(F32), do1.astype(F32)
            prod0 = do0f * o_ref[:, BLK * j0:BLK * j0 + BLK]
            prod1 = do1f * o_ref[:, BLK * j1:BLK * j1 + BLK]
            dos = jnp.concatenate([do0, do1], axis=0)
            dqs, dks, dvs = [], [], []
            for e in range(2):
                sl = slice(BLK * (2 * kv + e), BLK * (2 * kv + e) + BLK)
                kwin = jnp.concatenate([km_ref[:, sl], kp_ref[:, sl], kc_ref[:, sl]], axis=0)
                vwin = jnp.concatenate([vm_ref[:, sl], vp_ref[:, sl], vc_ref[:, sl]], axis=0)
                qs, s, sink = _attn_scores(q_ref, kwin, sk_ref, valid, kv, e)
                h0 = 4 * kv + e
                lse_rows = jnp.concatenate([lse_ref[:, h0:h0 + 1], lse_ref[:, h0 + 2:h0 + 3]], axis=0)
                p = jnp.exp(s - lse_rows)
                p_sink = jnp.exp(sink - lse_rows)
                delta = jnp.concatenate(
                    [jnp.sum(jnp.where(half == e, prod0, 0.0), axis=-1, keepdims=True),
                     jnp.sum(jnp.where(half == e, prod1, 0.0), axis=-1, keepdims=True)], axis=0)
                dp = lax.dot_general(dos, vwin, nt, preferred_element_type=F32)
                ds = (p * (dp - delta)).astype(BF16)
                pb = p.astype(BF16)
                dqs.append(jnp.dot(ds, kwin, preferred_element_type=F32))
                dks.append(lax.dot_general(ds, qs, tn, preferred_element_type=F32))
                dvs.append(lax.dot_general(pb, dos, tn, preferred_element_type=F32))
                sink_g = -(p_sink * delta)
                g_lo = jnp.sum(sink_g[:BLK], axis=0, keepdims=True)
                g_hi = jnp.sum(sink_g[BLK:], axis=0, keepdims=True)
                dsk = dsk + jnp.where(lane16 == h0, g_lo, 0.0) + jnp.where(lane16 == h0 + 2, g_hi, 0.0)
            dq = jnp.where(half2 == 0, dqs[0], dqs[1])
            dq_ref[:, BLK * j0:BLK * j0 + BLK] = dq[:BLK].astype(BF16)
            dq_ref[:, BLK * j1:BLK * j1 + BLK] = dq[BLK:].astype(BF16)
            own = kv % 2
            folded_k.append(dks[own] + pltpu.roll(dks[1 - own], HEAD_DIM, axis=1))
            folded_v.append(dvs[own] + pltpu.roll(dvs[1 - own], HEAD_DIM, axis=1))
            if own == 1:
                cols = slice(BLK * (kv // 2), BLK * (kv // 2) + BLK)
                for folded, m_ref, p_ref, c_ref in ((folded_k, dkm_ref, dkp_ref, dkc_ref),
                                                    (folded_v, dvm_ref, dvp_ref, dvc_ref)):
                    both = jnp.where(half3 == 0, folded[0], folded[1])
                    m_ref[:, cols] += both[:BLK]
                    p_ref[:, cols] = both[BLK:2 * BLK].astype(BF16)
                    c_ref[:, cols] = both[2 * BLK:].astype(BF16)
                folded_k, folded_v = [], []
        dsk_ref[...] += dsk

    kv_specs = _attn_specs(nb)
    row_wide = _row_spec(BLK, wide)
    acc_wide = _full_spec((BLK, wide))
    big = jax.ShapeDtypeStruct((l_dim, wide), BF16)
    return pl.pallas_call(
        body, name="attn_bwd", grid=(nb,),
        in_specs=[_row_spec(BLK, Q_W)] + kv_specs + kv_specs
        + [_full_spec((1, N_Q_HEADS)), _row_spec(BLK, BLK), _row_spec(BLK, Q_W), _row_spec(BLK, Q_W)],
        out_specs=[_row_spec(BLK, Q_W), row_wide, row_wide, acc_wide, row_wide, row_wide, acc_wide,
                   _full_spec((1, N_Q_HEADS))],
        out_shape=[jax.ShapeDtypeStruct((l_dim, Q_W), BF16), big, big, jax.ShapeDtypeStruct((BLK, wide), F32),
                   big, big, jax.ShapeDtypeStruct((BLK, wide), F32), jax.ShapeDtypeStruct((1, N_Q_HEADS), F32)],
        compiler_params=_params(),
    )(qn, kf, kf, kf, vf, vf, vf, sinks, lse, attn, dattn)


def _qk_bwd(qkv, q_norm_t, k_norm_t, e_mat, dq, dkc, dkp, dkm, dvc, dvp, dvm, dproj):
    l_dim = qkv.shape[0]
    nb = l_dim // BLK
    wide = KV_W

    def body(x_ref, qg_ref, kg_ref, e_ref, dq_ref, dkc_ref, dkp_ref, dkm_ref, dvc_ref, dvp_ref, dvm_ref,
             dproj_ref, o_ref, dqg_ref, dkg_ref):
        i = pl.program_id(0)

        @pl.when(i == 0)
        def _():
            dqg_ref[...] = jnp.zeros_like(dqg_ref)
            dkg_ref[...] = jnp.zeros_like(dkg_ref)

        first = jnp.where(i == 0, 1.0, 0.0)
        not_last = jnp.where(i < nb - 1, 1.0, 0.0)
        dk_x = dkc_ref[...].astype(F32) + not_last * dkp_ref[...].astype(F32) + first * dkm_ref[...]
        dv_x = dvc_ref[...].astype(F32) + not_last * dvp_ref[...].astype(F32) + first * dvm_ref[...]
        x = x_ref[...]
        dqx, dqg = _head_rms_bwd(x[:, :Q_W], qg_ref[...], dq_ref[...].astype(F32) * (HEAD_DIM ** -0.5), e_ref)
        dkx, dkg = _head_rms_bwd(x[:, Q_W:Q_W + KV_W], kg_ref[...], dk_x, e_ref)
        o_ref[:, :Q_W] = dqx.astype(BF16)
        o_ref[:, Q_W:Q_W + KV_W] = dkx.astype(BF16)
        o_ref[:, Q_W + KV_W:] = dv_x.astype(BF16)
        dqg_ref[...] += dqg
        dkg_ref[...] += dkg

    nxt = lambda i: jnp.minimum(i + 1, nb - 1)
    row_wide = _row_spec(BLK, wide)
    nxt_wide = _row_spec(BLK, wide, nxt)
    acc_wide = _full_spec((BLK, wide))
    qkv_w = Q_W + 2 * KV_W
    in_specs = [_row_spec(BLK, qkv_w), _full_spec((1, Q_W)), _full_spec((1, KV_W)),
                _full_spec((256, 256)), _row_spec(BLK, Q_W),
                row_wide, nxt_wide, acc_wide, row_wide, nxt_wide, acc_wide, pl.BlockSpec(memory_space=pl.ANY)]
    return pl.pallas_call(
        body, name="qk_bwd", grid=(nb,),
        in_specs=in_specs,
        out_specs=[pl.BlockSpec((BLK, qkv_w), lambda i: (i, 3 * D_MODEL // qkv_w)),
                   _full_spec((1, Q_W)), _full_spec((1, KV_W))],
        input_output_aliases={len(in_specs) - 1: 0},
        out_shape=[jax.ShapeDtypeStruct(dproj.shape, BF16),
                   jax.ShapeDtypeStruct((1, Q_W), F32), jax.ShapeDtypeStruct((1, KV_W), F32)],
        compiler_params=_params(),
    )(qkv, q_norm_t, k_norm_t, e_mat, dq, dkc, dkp, dkm, dvc, dvp, dvm, dproj)


GRP = 8


def _strided(r, g):
    return pl.ds(r, g, stride=GRP)


def _lane_slab(ref, i, r, g):
    return ref[i, _strided(r, g), :]


def _chunk_carries(xr_ref, xi_ref, i, ar, ai, sqr_ref, sqi_ref, seq_r, seq_i, cin_r, cin_i, sign, reverse):
    g = xr_ref.shape[1] // GRP
    sr = si = None
    for r in (range(GRP - 1, -1, -1) if reverse else range(GRP)):
        xr, xi = _lane_slab(xr_ref, i, r, g), _lane_slab(xi_ref, i, r, g)
        if sr is not None:
            xr, xi = xr + ar * sr - ai * si, xi + ar * si + ai * sr
        sr, si = xr, xi
    row = lax.broadcasted_iota(jnp.int32, sr.shape, 0)
    idx, s = 0, 1
    while s < g:
        br = sqr_ref[idx:idx + 1, BLK * i:BLK * i + BLK]
        bi = sign * sqi_ref[idx:idx + 1, BLK * i:BLK * i + BLK]
        shift, keep = (g - s, row < g - s) if reverse else (s, row >= s)
        pr = jnp.where(keep, pltpu.roll(sr, shift, axis=0), 0.0)
        pi = jnp.where(keep, pltpu.roll(si, shift, axis=0), 0.0)
        sr, si = sr + br * pr - bi * pi, si + br * pi + bi * pr
        idx, s = idx + 1, 2 * s
    return sr + seq_r * cin_r - seq_i * cin_i, si + seq_r * cin_i + seq_i * cin_r


def _ssm_fwd_kb(u, wb_re, wb_im, wc_re, wc_im, d_skip, tabs, q):
    l_dim = u.shape[0]
    nc = l_dim // q
    g = q // GRP

    def body(u_ref, wbr_ref, wbi_ref, wcr_ref, wci_ref, d_ref, a1r_ref, a1i_ref, sqr_ref, sqi_ref,
             seqr_ref, seqi_ref, y_ref, z_ref, sr_ref, si_ref, cr_ref, ci_ref, xr_ref, xi_ref):
        @pl.when(pl.program_id(1) == 0)
        def _():
            cr_ref[...] = jnp.zeros_like(cr_ref)
            ci_ref[...] = jnp.zeros_like(ci_ref)

        u_kb = u_ref[...]
        ub = u_kb.astype(BF16)
        xr = jnp.dot(ub, wbr_ref[0], preferred_element_type=F32)
        xi = jnp.dot(ub, wbi_ref[0], preferred_element_type=F32)
        for i in range(LB_KB):
            xr_ref[i] = xr[:, BLK * i:BLK * i + BLK]
            xi_ref[i] = xi[:, BLK * i:BLK * i + BLK]
        row = lax.broadcasted_iota(jnp.int32, (g, BLK), 0)
        for i in range(LB_KB):
            lanes = slice(BLK * i, BLK * i + BLK)
            ar, ai = a1r_ref[0:1, lanes], a1i_ref[0:1, lanes]
            cin_r, cin_i = cr_ref[0:1, lanes], ci_ref[0:1, lanes]
            tr, ti = _chunk_carries(xr_ref, xi_ref, i, ar, ai, sqr_ref, sqi_ref, seqr_ref[:, lanes],
                                    seqi_ref[:, lanes], cin_r, cin_i, 1.0, reverse=False)
            cr_ref[0:1, lanes] = jnp.sum(jnp.where(row == g - 1, tr, 0.0), axis=0, keepdims=True)
            ci_ref[0:1, lanes] = jnp.sum(jnp.where(row == g - 1, ti, 0.0), axis=0, keepdims=True)
            pr = jnp.where(row == 0, cin_r, pltpu.roll(tr, 1, axis=0))
            pi = jnp.where(row == 0, cin_i, pltpu.roll(ti, 1, axis=0))
            for r in range(GRP):
                pr, pi = (_lane_slab(xr_ref, i, r, g) + ar * pr - ai * pi,
                          _lane_slab(xi_ref, i, r, g) + ar * pi + ai * pr)
                sr_ref[i, _strided(r, g), :] = pr
                si_ref[i, _strided(r, g), :] = pi
        s_r = jnp.concatenate([sr_ref[i] for i in range(LB_KB)], axis=1)
        s_i = jnp.concatenate([si_ref[i] for i in range(LB_KB)], axis=1)
        y = (jnp.dot(s_r.astype(BF16), wcr_ref[0], preferred_element_type=F32)
             - jnp.dot(s_i.astype(BF16), wci_ref[0], preferred_element_type=F32)
             + d_ref[...] * u_kb)
        y_ref[...] = y.astype(BF16)
        z_ref[...] = _gelu(y).astype(BF16)

    chan = pl.BlockSpec((q, BLK), lambda k, c: (c, k))
    wb_spec = pl.BlockSpec((1, BLK, ST_KB), lambda k, c: (k, 0, 0))
    wc_spec = pl.BlockSpec((1, ST_KB, BLK), lambda k, c: (k, 0, 0))
    tab_specs = [pl.BlockSpec((t.shape[0], ST_KB), lambda k, c: (0, k)) for t in tabs[:6]]
    state_spec = pl.BlockSpec((LB_KB, q, BLK), lambda k, c: (k, c, 0))
    state_shape = jax.ShapeDtypeStruct((N_LB, l_dim, BLK), F32)
    return pl.pallas_call(
        body, name="ssm_fwd", grid=(SSM_KB, nc),
        in_specs=[chan, wb_spec, wb_spec, wc_spec, wc_spec, pl.BlockSpec((1, BLK), lambda k, c: (0, k))] + tab_specs,
        out_specs=[chan, chan, state_spec, state_spec],
        out_shape=[jax.ShapeDtypeStruct((l_dim, D_MODEL), BF16), jax.ShapeDtypeStruct((l_dim, D_MODEL), BF16),
                   state_shape, state_shape],
        scratch_shapes=[pltpu.VMEM((8, ST_KB), F32), pltpu.VMEM((8, ST_KB), F32),
                        pltpu.VMEM((LB_KB, q, BLK), F32), pltpu.VMEM((LB_KB, q, BLK), F32)],
        compiler_params=_params(dimension_semantics=("parallel", "arbitrary")),
    )(u, wb_re, wb_im, wc_re, wc_im, d_skip, *tabs[:6])


def _ssm_bwd_kb(dz, y, u, s_re, s_im, wb_re, wb_im, wc_re, wc_im, d_skip, tabs, q, dproj):
    l_dim = u.shape[0]
    nc = l_dim // q
    g = q // GRP

    def body(dz_ref, y_ref, u_ref, sr_ref, si_ref, wbr_ref, wbi_ref, wcr_ref, wci_ref, d_ref,
             a1r_ref, a1i_ref, sqr_ref, sqi_ref, revr_ref, revi_ref, dproj_ref,
             du_ref, dd_ref, dar_ref, dai_ref, dwbr_ref, dwbi_ref, dwcr_ref, dwci_ref,
             cr_ref, ci_ref, gr_ref, gi_ref):
        @pl.when(pl.program_id(1) == 0)
        def _():
            for ref in (cr_ref, ci_ref, dd_ref, dar_ref, dai_ref, dwbr_ref, dwbi_ref, dwcr_ref, dwci_ref):
                ref[...] = jnp.zeros_like(ref)

        tn = (((0,), (0,)), ((), ()))
        nt = (((1,), (1,)), ((), ()))
        u_kb = u_ref[...]
        dy = dz_ref[...].astype(F32) * _gelu_grad(y_ref[...].astype(F32))
        dyb = dy.astype(BF16)
        ub = u_kb.astype(BF16)
        dd_ref[...] += jnp.sum(dy * u_kb, axis=0, keepdims=True)
        ds_r = lax.dot_general(dyb, wcr_ref[0], nt, preferred_element_type=F32)
        ds_i = -lax.dot_general(dyb, wci_ref[0], nt, preferred_element_type=F32)
        for i in range(LB_KB):
            gr_ref[i] = ds_r[:, BLK * i:BLK * i + BLK]
            gi_ref[i] = ds_i[:, BLK * i:BLK * i + BLK]
        row = lax.broadcasted_iota(jnp.int32, (g, BLK), 0)
        for i in range(LB_KB):
            lanes = slice(BLK * i, BLK * i + BLK)
            ar, ai = a1r_ref[0:1, lanes], -a1i_ref[0:1, lanes]
            cin_r, cin_i = cr_ref[0:1, lanes], ci_ref[0:1, lanes]
            tr, ti = _chunk_carries(gr_ref, gi_ref, i, ar, ai, sqr_ref, sqi_ref, revr_ref[:, lanes],
                                    -revi_ref[:, lanes], cin_r, cin_i, -1.0, reverse=True)
            cr_ref[0:1, lanes] = jnp.sum(jnp.where(row == 0, tr, 0.0), axis=0, keepdims=True)
            ci_ref[0:1, lanes] = jnp.sum(jnp.where(row == 0, ti, 0.0), axis=0, keepdims=True)
            nr = jnp.where(row == g - 1, cin_r, pltpu.roll(tr, g - 1, axis=0))
            ni = jnp.where(row == g - 1, cin_i, pltpu.roll(ti, g - 1, axis=0))
            acc_r = jnp.zeros((g, BLK), F32)
            acc_i = jnp.zeros((g, BLK), F32)
            for r in range(GRP - 1, -1, -1):
                s_r, s_i = _lane_slab(sr_ref, i, r, g), _lane_slab(si_ref, i, r, g)
                acc_r = acc_r + (nr * s_r + ni * s_i)
                acc_i = acc_i + (ni * s_r - nr * s_i)
                nr, ni = (_lane_slab(gr_ref, i, r, g) + ar * nr - ai * ni,
                          _lane_slab(gi_ref, i, r, g) + ar * ni + ai * nr)
                gr_ref[i, _strided(r, g), :] = nr
                gi_ref[i, _strided(r, g), :] = ni
            dar_ref[:, lanes] += jnp.sum(acc_r, axis=0, keepdims=True)
            dai_ref[:, lanes] += jnp.sum(acc_i, axis=0, keepdims=True)
        grb = jnp.concatenate([gr_ref[i] for i in range(LB_KB)], axis=1).astype(BF16)
        gib = jnp.concatenate([gi_ref[i] for i in range(LB_KB)], axis=1).astype(BF16)
        srb = jnp.concatenate([sr_ref[i] for i in range(LB_KB)], axis=1).astype(BF16)
        sib = jnp.concatenate([si_ref[i] for i in range(LB_KB)], axis=1).astype(BF16)
        du = (lax.dot_general(grb, wbr_ref[0], nt, preferred_element_type=F32)
              + lax.dot_general(gib, wbi_ref[0], nt, preferred_element_type=F32)
              + d_ref[...] * dy)
        du_ref[...] = du.astype(BF16)
        dwbr_ref[0] += lax.dot_general(ub, grb, tn, preferred_element_type=F32)
        dwbi_ref[0] += lax.dot_general(ub, gib, tn, preferred_element_type=F32)
        dwcr_ref[0] += lax.dot_general(srb, dyb, tn, preferred_element_type=F32)
        dwci_ref[0] -= lax.dot_general(sib, dyb, tn, preferred_element_type=F32)

    chan = pl.BlockSpec((q, BLK), lambda k, c: (nc - 1 - c, k))
    wb_spec = pl.BlockSpec((1, BLK, ST_KB), lambda k, c: (k, 0, 0))
    wc_spec = pl.BlockSpec((1, ST_KB, BLK), lambda k, c: (k, 0, 0))
    tab_in = [tabs[0], tabs[1], tabs[2], tabs[3], tabs[6], tabs[7]]
    tab_specs = [pl.BlockSpec((t.shape[0], ST_KB), lambda k, c: (0, k)) for t in tab_in]
    vec = pl.BlockSpec((1, BLK), lambda k, c: (0, k))
    svec = pl.BlockSpec((1, ST_KB), lambda k, c: (0, k))
    state_spec = pl.BlockSpec((LB_KB, q, BLK), lambda k, c: (k, nc - 1 - c, 0))
    du_spec = pl.BlockSpec((q, BLK), lambda k, c: (nc - 1 - c, 2 * D_MODEL // BLK + k))
    in_specs = ([chan, chan, chan, state_spec, state_spec, wb_spec, wb_spec, wc_spec, wc_spec, vec] + tab_specs
                + [pl.BlockSpec(memory_space=pl.ANY)])
    return pl.pallas_call(
        body, name="ssm_bwd", grid=(SSM_KB, nc),
        in_specs=in_specs,
        out_specs=[du_spec, vec, svec, svec, wb_spec, wb_spec, wc_spec, wc_spec],
        input_output_aliases={len(in_specs) - 1: 0},
        out_shape=[jax.ShapeDtypeStruct(dproj.shape, BF16), jax.ShapeDtypeStruct((1, D_MODEL), F32),
                   jax.ShapeDtypeStruct((1, N_STATE), F32), jax.ShapeDtypeStruct((1, N_STATE), F32),
                   jax.ShapeDtypeStruct((SSM_KB, BLK, ST_KB), F32), jax.ShapeDtypeStruct((SSM_KB, BLK, ST_KB), F32),
                   jax.ShapeDtypeStruct((SSM_KB, ST_KB, BLK), F32), jax.ShapeDtypeStruct((SSM_KB, ST_KB, BLK), F32)],
        scratch_shapes=[pltpu.VMEM((8, ST_KB), F32), pltpu.VMEM((8, ST_KB), F32),
                        pltpu.VMEM((LB_KB, q, BLK), F32), pltpu.VMEM((LB_KB, q, BLK), F32)],
        compiler_params=_params(dimension_semantics=("parallel", "arbitrary")),
    )(dz, y, u, s_re, s_im, wb_re, wb_im, wc_re, wc_im, d_skip, *tab_in, dproj)


def _discretize(lam_re, lam_im, log_dt, b_re, b_im):
    dt = jnp.exp(log_dt)[:, None]
    mag = jnp.exp(lam_re * dt)
    ar, ai = mag * jnp.cos(lam_im * dt), mag * jnp.sin(lam_im * dt)
    den = lam_re * lam_re + lam_im * lam_im
    nr, ni = ar - 1.0, ai
    fr, fi = (nr * lam_re + ni * lam_im) / den, (ni * lam_re - nr * lam_im) / den
    bbar_re = fr[..., None] * b_re - fi[..., None] * b_im
    bbar_im = fr[..., None] * b_im + fi[..., None] * b_re
    return ar, ai, bbar_re, bbar_im


def _block_diag_b(bbar):
    eye = jnp.eye(8, dtype=bbar.dtype)
    return jnp.einsum("kgpc,gh->kgchp", bbar.reshape(8, 8, SSM_STATE, SSM_GROUP_CH), eye).reshape(8, BLK, ST_KB)


def _block_diag_b_t(dwb):
    eye = jnp.eye(8, dtype=dwb.dtype)
    return jnp.einsum("kgchp,gh->kgpc", dwb.reshape(8, 8, SSM_GROUP_CH, 8, SSM_STATE), eye).reshape(
        SSM_GROUPS, SSM_STATE, SSM_GROUP_CH)


def _block_diag_c(c):
    eye = jnp.eye(8, dtype=c.dtype)
    return jnp.einsum("kgcp,gh->kgphc", c.reshape(8, 8, SSM_GROUP_CH, SSM_STATE), eye).reshape(8, ST_KB, BLK)


def _block_diag_c_t(dwc):
    eye = jnp.eye(8, dtype=dwc.dtype)
    return jnp.einsum("kgphc,gh->kgcp", dwc.reshape(8, 8, SSM_STATE, 8, SSM_GROUP_CH), eye).reshape(
        SSM_GROUPS, SSM_GROUP_CH, SSM_STATE)


def _powers(br, bi, n):
    pr, pi = br, bi
    cr, ci = br, bi
    while pr.shape[0] < n:
        pr, pi = (jnp.concatenate([pr, pr * cr - pi * ci], axis=0),
                  jnp.concatenate([pi, pr * ci + pi * cr], axis=0))
        cr, ci = cr * cr - ci * ci, 2.0 * cr * ci
    return pr[:n], pi[:n]


def _powers_desc(br, bi, n):
    pr, pi = br, bi
    cr, ci = br, bi
    while pr.shape[0] < n:
        pr, pi = (jnp.concatenate([pr * cr - pi * ci, pr], axis=0),
                  jnp.concatenate([pr * ci + pi * cr, pi], axis=0))
        cr, ci = cr * cr - ci * ci, 2.0 * cr * ci
    return pr, pi


def _power_tables(ar, ai, g):
    a1r, a1i = _powers(ar, ai, GRP)
    seqr, seqi = _powers(a1r[GRP - 1:], a1i[GRP - 1:], g)
    g2 = 1 << (g - 1).bit_length()
    revr, revi = _powers_desc(a1r[GRP - 1:], a1i[GRP - 1:], g2)
    revr, revi = revr[g2 - g:], revi[g2 - g:]
    sq_r, sq_i = [seqr[0:1]], [seqi[0:1]]
    while len(sq_r) < 8:
        r, i = sq_r[-1], sq_i[-1]
        sq_r.append(r * r - i * i)
        sq_i.append(2.0 * r * i)
    sqr, sqi = jnp.concatenate(sq_r, axis=0), jnp.concatenate(sq_i, axis=0)
    return a1r, a1i, sqr, sqi, seqr, seqi, revr, revi


HBM_SPEC = pl.BlockSpec(memory_space=pltpu.HBM)
SEM_SPEC = pl.BlockSpec(memory_space=pltpu.SEMAPHORE)
DATAFLOW = pltpu.SideEffectType.DATAFLOW_SIDE_EFFECTING


def _plain_rows(p, m):
    return p * m


def _ffn_in_rows(p, m):
    return ((p & 3) >> 1) * (4 * m) + (p >> 2) * (2 * m) + (p & 1) * m


def _peer_copies(src_refs, land_refs, send_sems, recv_sems, chunked, row_fns):
    x, y, c = lax.axis_index("x"), lax.axis_index("y"), lax.axis_index("c")
    me = 4 * x + 2 * y + c
    copies = []
    for a, (src, land) in enumerate(zip(src_refs, land_refs)):
        m = land.shape[0] // N_DEV
        for k in range(N_DEV - 1):
            rel = k + 1
            bx, by, bc = (rel >> 2) & 1, (rel >> 1) & 1, rel & 1
            peer = (x + bx - 2 * x * bx, y + by - 2 * y * by, c + bc - 2 * c * bc)
            p_idx = 4 * peer[0] + 2 * peer[1] + peer[2]
            copies.append(pltpu.make_async_remote_copy(
                src_ref=src.at[pl.ds(row_fns[a](p_idx, m), m), :] if chunked else src,
                dst_ref=land.at[pl.ds(me * m if chunked else row_fns[a](me, m), m), :],
                send_sem=send_sems[a * (N_DEV - 1) + k], recv_sem=recv_sems[a * (N_DEV - 1) + k],
                device_id=peer, device_id_type=MESH))
    return copies


def _own_copies(src_refs, land_outs, sems, chunked, row_fns):
    me = _my_index()
    copies = []
    for a, (src, land) in enumerate(zip(src_refs, land_outs)):
        m = land.shape[0] // N_DEV
        rows = row_fns[a](me, m)
        copies.append(pltpu.make_async_copy(src.at[pl.ds(rows, m), :] if chunked else src,
                                            land.at[pl.ds(me * m if chunked else rows, m), :], sems[a]))
    return copies


def _send_start(name, srcs, lands, chunked, row_fns=None):
    n = len(srcs)
    ns = n * (N_DEV - 1)
    row_fns = row_fns or [_plain_rows] * n

    def body(*refs):
        src_refs, land_refs = refs[:n], refs[n:2 * n]
        send_sems, own_sems = refs[2 * n:2 * n + ns], refs[2 * n + ns:2 * n + ns + n]
        recv_sems = refs[2 * n + ns + n:2 * n + 2 * ns + n]
        land_outs, token = refs[-1 - n:-1], refs[-1]
        for cp in _own_copies(src_refs, land_outs, own_sems, chunked, row_fns):
            cp.start()
        for cp in _peer_copies(src_refs, land_refs, send_sems, recv_sems, chunked, row_fns):
            cp.start()
        token[...] = jnp.zeros_like(token)

    ins = [pltpu.with_memory_space_constraint(a, pltpu.HBM) for a in list(srcs) + list(lands)]
    n_sems = 2 * ns + n
    out = pl.pallas_call(
        body, name=name,
        in_specs=[HBM_SPEC] * (2 * n),
        out_specs=[SEM_SPEC] * n_sems + [HBM_SPEC] * (2 * n) + [pl.BlockSpec(memory_space=pltpu.VMEM)],
        out_shape=[pltpu.SemaphoreType.DMA(())] * n_sems
        + [pltpu.HBM(a.shape, a.dtype) for a in list(srcs) + list(lands)]
        + [jax.ShapeDtypeStruct((8, BLK), F32)],
        input_output_aliases={i: i + n_sems for i in range(2 * n)},
        compiler_params=pltpu.CompilerParams(has_side_effects=DATAFLOW),
    )(*ins)
    return (out[:ns + n], out[ns + n:n_sems], out[n_sems:n_sems + n], out[n_sems + n:n_sems + 2 * n], out[-1])


def _send_wait(name, send_sems, recv_sems, srcs, lands, after, chunked, row_fns=None):
    n = len(srcs)
    ns = n * (N_DEV - 1)
    row_fns = row_fns or [_plain_rows] * n
    after = list(after) if isinstance(after, (list, tuple)) else [after]

    def body(*refs):
        src_refs, land_refs = refs[:n], refs[n:2 * n]
        s_sems, own_sems = refs[2 * n:2 * n + ns], refs[2 * n + ns:2 * n + ns + n]
        r_sems = refs[2 * n + ns + n:2 * n + 2 * ns + n]
        land_outs = refs[-n:]
        for cp in _own_copies(src_refs, land_outs, own_sems, chunked, row_fns):
            cp.wait()
        copies = _peer_copies(src_refs, land_refs, s_sems, r_sems, chunked, row_fns)
        for cp in copies:
            cp.wait_send()
        for cp in copies:
            cp.wait_recv()

    out = pl.pallas_call(
        body, name=name,
        in_specs=[HBM_SPEC] * (2 * n) + [SEM_SPEC] * (2 * ns + n) + [pl.BlockSpec(memory_space=pl.ANY)] * len(after),
        out_specs=[HBM_SPEC] * (2 * n),
        out_shape=[pltpu.HBM(a.shape, a.dtype) for a in list(srcs) + list(lands)],
        input_output_aliases={i: i for i in range(2 * n)},
        compiler_params=pltpu.CompilerParams(has_side_effects=DATAFLOW),
    )(*srcs, *lands, *send_sems, *recv_sems, *after)
    return out[n:]


def _sum_slots(name, recv, own, wmv=None):
    m, ncol = own.shape
    tr = m // 2 if (m // 2) % 16 == 0 else m
    g = m // tr
    c1 = 1.0 - ADAM_B1 ** ADAM_STEP
    c2 = 1.0 - ADAM_B2 ** ADAM_STEP

    def body(*refs):
        slots, own_ref, o_ref = refs[:N_DEV], refs[N_DEV], refs[N_DEV + 1 + (3 if wmv else 0)]
        me = _my_index()
        tot = None
        for s in range(N_DEV):
            v = jnp.where(me == s, own_ref[...], slots[s][...].astype(F32))
            tot = v if tot is None else tot + v
        o_ref[...] = tot
        if wmv:
            w_ref, m_ref, v_ref = refs[N_DEV + 1:N_DEV + 4]
            d_ref, nm_ref, nv_ref = refs[N_DEV + 5:N_DEV + 8]
            nm = ADAM_B1 * m_ref[...] + (1.0 - ADAM_B1) * tot
            nv = ADAM_B2 * v_ref[...] + (1.0 - ADAM_B2) * (tot * tot)
            d_ref[...] = -ADAM_LR * ((nm / c1) / (jnp.sqrt(nv / c2) + ADAM_EPS) + ADAM_WD * w_ref[...])
            nm_ref[...] = nm
            nv_ref[...] = nv

    def slot_spec(s):
        return pl.BlockSpec((tr, ncol), lambda i: (s * g + i, 0))

    tile = pl.BlockSpec((tr, ncol), lambda i: (i, 0))
    shape = jax.ShapeDtypeStruct((m, ncol), F32)
    n_extra = 3 if wmv else 0
    out = pl.pallas_call(
        body, name=name, grid=(g,),
        in_specs=[slot_spec(s) for s in range(N_DEV)] + [tile] * (1 + n_extra),
        out_specs=[tile] * (1 + n_extra),
        out_shape=[shape] * (1 + n_extra),
        compiler_params=_params(),
    )(*([recv] * N_DEV), own, *(wmv or ()))
    return out if wmv else out[0]


def _sum_gathered(name, gathered, rows):
    tr = _pick(rows, 512, 8)
    g = rows // tr

    def body(*refs):
        o_ref = refs[N_DEV]
        tot = refs[0][...]
        for s in range(1, N_DEV):
            tot = tot + refs[s][...]
        o_ref[...] = tot

    return pl.pallas_call(
        body, name=name, grid=(g,),
        in_specs=[pl.BlockSpec((tr, BLK), (lambda i, s=s: (s * g + i, 0))) for s in range(N_DEV)],
        out_specs=pl.BlockSpec((tr, BLK), lambda i: (i, 0)),
        out_shape=jax.ShapeDtypeStruct((rows, BLK), F32),
        compiler_params=_params(),
    )(*([gathered] * N_DEV))


def _adamw(name, w, g, m, v):
    r, c = w.shape
    tr = _pick(r, 256, 8) if r % 8 == 0 else r
    c1 = 1.0 - ADAM_B1 ** ADAM_STEP
    c2 = 1.0 - ADAM_B2 ** ADAM_STEP

    def body(w_ref, g_ref, m_ref, v_ref, d_ref, nm_ref, nv_ref):
        gv = g_ref[...]
        nm = ADAM_B1 * m_ref[...] + (1.0 - ADAM_B1) * gv
        nv = ADAM_B2 * v_ref[...] + (1.0 - ADAM_B2) * (gv * gv)
        m_hat = nm / c1
        v_hat = nv / c2
        d_ref[...] = -ADAM_LR * (m_hat / (jnp.sqrt(v_hat) + ADAM_EPS) + ADAM_WD * w_ref[...])
        nm_ref[...] = nm
        nv_ref[...] = nv

    spec = pl.BlockSpec((tr, c), lambda i: (i, 0))
    shape = jax.ShapeDtypeStruct((r, c), F32)
    return pl.pallas_call(
        body, name=name, grid=(r // tr,),
        in_specs=[spec] * 4, out_specs=[spec] * 3, out_shape=[shape] * 3,
        compiler_params=_params(),
    )(w, g, m, v)


def _adamw_many(name, ws, gs, ms, vs):
    n = len(ws)
    c1 = 1.0 - ADAM_B1 ** ADAM_STEP
    c2 = 1.0 - ADAM_B2 ** ADAM_STEP

    def body(*refs):
        for a in range(n):
            w_ref, g_ref, m_ref, v_ref = refs[a], refs[n + a], refs[2 * n + a], refs[3 * n + a]
            d_ref, nm_ref, nv_ref = refs[4 * n + a], refs[5 * n + a], refs[6 * n + a]
            gv = g_ref[...]
            nm = ADAM_B1 * m_ref[...] + (1.0 - ADAM_B1) * gv
            nv = ADAM_B2 * v_ref[...] + (1.0 - ADAM_B2) * (gv * gv)
            d_ref[...] = -ADAM_LR * ((nm / c1) / (jnp.sqrt(nv / c2) + ADAM_EPS) + ADAM_WD * w_ref[...])
            nm_ref[...] = nm
            nv_ref[...] = nv

    specs = [_full_spec(w.shape) for w in ws]
    shapes = [jax.ShapeDtypeStruct(w.shape, F32) for w in ws]
    out = pl.pallas_call(
        body, name=name, grid=(1,),
        in_specs=specs * 4, out_specs=specs * 3, out_shape=shapes * 3,
        compiler_params=_params(),
    )(*ws, *gs, *ms, *vs)
    return out[:n], out[n:2 * n], out[2 * n:]


PACK_ROWS = 128


def _pack(parts):
    flat = []
    for p in parts:
        v = p.reshape(-1)
        flat.append(jnp.pad(v, (0, (-v.shape[0]) % BLK)))
    v = jnp.concatenate(flat)
    v = jnp.pad(v, (0, (-v.shape[0]) % (PACK_ROWS * BLK)))
    return v.reshape(-1, BLK)


def _unpack(buf, shapes):
    flat = buf.reshape(-1)
    out, off = [], 0
    for shp in shapes:
        size = math.prod(shp)
        out.append(flat[off:off + size].reshape(shp))
        off += size + (-size) % BLK
    return out


def kernel(x, meta_tokens, norm_mix, w_in, q_norm, k_norm, attn_sinks, lam_re, lam_im, log_dt, ssm_b_re, ssm_b_im, ssm_c_re, ssm_c_im, ssm_d, w_glu, attn_branch_norm, ssm_branch_norm, w_out, norm_ffn, w_ffn_in, w_ffn_out, loss_target, m_meta_tokens, m_norm_mix, m_w_in, m_q_norm, m_k_norm, m_attn_sinks, m_lam_re, m_lam_im, m_log_dt, m_ssm_b_re, m_ssm_b_im, m_ssm_c_re, m_ssm_c_im, m_ssm_d, m_w_glu, m_attn_branch_norm, m_ssm_branch_norm, m_w_out, m_norm_ffn, m_w_ffn_in, m_w_ffn_out, v_meta_tokens, v_norm_mix, v_w_in, v_q_norm, v_k_norm, v_attn_sinks, v_lam_re, v_lam_im, v_log_dt, v_ssm_b_re, v_ssm_b_im, v_ssm_c_re, v_ssm_c_im, v_ssm_d, v_w_glu, v_attn_branch_norm, v_ssm_branch_norm, v_w_out, v_norm_ffn, v_w_ffn_in, v_w_ffn_out):
    args = dict(locals())
    weights = {n: args[n] for n in WEIGHTS}
    mom_m = {n: args["m_" + n] for n in WEIGHTS}
    mom_v = {n: args["v_" + n] for n in WEIGHTS}

    x2d = x[0]
    target2d = loss_target[0]
    s_len = x2d.shape[0]
    l_dim = s_len + BLK
    tm_row = _pick(l_dim, 320)
    tm_mm = _pick(l_dim, 1040)
    tl_tn = _pick(l_dim, 2080)
    tm_ffn = _pick(l_dim, 640)
    tm_big = _pick(l_dim, 2080)
    tm_shift = _pick(l_dim, 640, BLK)

    shard_in = w_in[0].T.astype(BF16)
    shard_glu = w_glu[0].T.astype(BF16)
    shard_out = w_out[0].astype(BF16)
    shard_ffn_in = w_ffn_in[0].T.astype(BF16)
    shard_ffn_out = w_ffn_out[0].astype(BF16)
    shard_meta = meta_tokens.T
    me = _my_index()

    def landing(shard):
        return lax.empty((N_DEV * shard.shape[0], shard.shape[1]), shard.dtype)

    first = [shard_in, shard_meta]
    ga = _send_start("gather_start_a", first, [landing(s) for s in first], chunked=False)
    later = [shard_glu + ga[4][0:1, 0:1].astype(BF16), shard_out, shard_ffn_in, shard_ffn_out]
    later_fns = [_plain_rows, _plain_rows, _ffn_in_rows, _plain_rows]
    gb = _send_start("gather_start_b", later, [landing(s) for s in later], chunked=False,
                     row_fns=later_fns)

    nm_t = norm_mix + (ga[4][0:1, 0:1] + gb[4][0:1, 0:1])
    qn_t, kn_t = jnp.tile(q_norm, (1, N_Q_HEADS)), jnp.tile(k_norm, (1, N_KV_HEADS))
    e_mat = jnp.kron(jnp.eye(4, dtype=F32), jnp.ones((HEAD_DIM, HEAD_DIM), F32)).astype(BF16)

    def disc(lr, li, ldt, br, bi):
        return _discretize(lr[0], li[0], ldt[0], br[0], bi[0])

    (abar_re, abar_im, bbar_re, bbar_im), disc_vjp = jax.vjp(disc, lam_re, lam_im, log_dt, ssm_b_re, ssm_b_im)
    wb_re, wb_im = _block_diag_b(bbar_re).astype(BF16), _block_diag_b(bbar_im).astype(BF16)
    wc_re, wc_im = _block_diag_c(ssm_c_re[0]).astype(BF16), _block_diag_c(ssm_c_im[0]).astype(BF16)
    q_ssm = _pick(l_dim, 1664, 64)
    tabs = _power_tables(abar_re.reshape(1, N_STATE), abar_im.reshape(1, N_STATE), q_ssm // GRP)

    h0, xn = _embed_norm(x2d, nm_t, tm_shift)
    wt_in, meta_t = _send_wait("gather_wait_a", ga[0], ga[1], ga[2], ga[3],
                               [xn, wb_re, wb_im, wc_re, wc_im, *tabs, e_mat, qn_t, kn_t], chunked=False)
    meta_pad = jnp.pad(meta_t.T, ((PAD, 0), (0, 0)))
    h0, xn = _embed_meta(meta_pad, nm_t, h0, xn)
    qkv_w, u_end = Q_W + 2 * KV_W, Q_W + 2 * KV_W + D_MODEL
    wt_in_p = jnp.concatenate([wt_in[u_end:], wt_in[qkv_w:u_end], wt_in[:qkv_w]], axis=0)
    qkv = _matmul("proj_qkv", xn, wt_in_p, nt=True, tm=tm_big, tn=qkv_w, tk=D_MODEL, n=qkv_w, w_off=2)
    u = _matmul("proj_u", xn, wt_in_p, nt=True, tm=tm_big, tn=D_MODEL, tk=D_MODEL, n=D_MODEL, w_off=2)
    gates = _matmul("proj_gates", xn, wt_in_p, nt=True, tm=tm_big, tn=D_MODEL, tk=D_MODEL, n=2 * D_MODEL, w_off=0,
                    out_dtype=BF16)
    qn, kf, vf = _qk_prep(qkv, qn_t, kn_t, e_mat, tm_row)
    attn, lse = _attn_fwd(qn, kf, vf, attn_sinks)
    y, z, s_re, s_im = _ssm_fwd_kb(u, wb_re, wb_im, wc_re, wc_im, ssm_d, tabs, q_ssm)
    wt_glu, w_out_f, wt_ffn_in, w_ffn_out_f = _send_wait("gather_wait_b", gb[0], gb[1], gb[2], gb[3], z,
                                                         chunked=False, row_fns=later_fns)
    zab = _matmul("glu_proj", z, wt_glu, nt=True, tm=tm_big, tn=1024, tk=D_MODEL, out_dtype=BF16)
    merged = _merge_fwd(attn, zab, gates, attn_branch_norm, ssm_branch_norm, tm_row)
    h1, hn = _matmul("out_proj", merged, w_out_f, nt=False, tm=tm_mm, tn=1024, tk=D_MODEL, res=h0,
                     norm_g=norm_ffn)
    gu, act = _ffn_in_swiglu(hn, wt_ffn_in, tm_ffn)
    h2 = _matmul("ffn_out", act, w_ffn_out_f, nt=False, tm=tm_mm, tn=1024, tk=D_FF, res=h1)
    dh2, dh2_b, loss_part = _loss_grad(h2, target2d, tm_shift)

    dgu = _d_act_swiglu(dh2_b, w_ffn_out_f, gu, tm_ffn)

    def exchange_start(name, grads_b, row_fns=None):
        return _send_start(name, grads_b, [lax.empty(g.shape, BF16) for g in grads_b], chunked=True,
                           row_fns=row_fns)

    g_ffn_out, g_ffn_out_b = _matmul_tn("g_ffn_out", act, dh2_b, tm=1408, tn=1024, tl=tl_tn)
    g_ffn_in_t, g_ffn_in_b = _matmul_tn("g_ffn_in", dgu, hn, tm=1408, tn=1024, tl=tl_tn)
    ffn_fns = [_ffn_in_rows, _plain_rows]
    ex1 = exchange_start("exchange_start_ffn", [g_ffn_in_b, g_ffn_out_b], ffn_fns)
    dhn = _matmul("d_hn", dgu, wt_ffn_in, nt=False, tm=tm_ffn, tn=1024, tk=2 * D_FF, out_dtype=BF16)
    dh1, dh1_b, g_norm_ffn = _norm_bwd_res("ffn_norm_bwd", h1, norm_ffn + ex1[4][0:1, 0:1], dhn, dh2, tm_row)
    dmerged = _matmul("d_merged", dh1_b, w_out_f, nt=True, tm=tm_big, tn=1024, tk=D_MODEL, out_dtype=BF16)
    dattn, dzab, dproj, g_abn, g_sbn = _merge_bwd(attn, zab, gates, attn_branch_norm, ssm_branch_norm,
                                                    dmerged, tm_row)
    g_out, g_out_b = _matmul_tn("g_out", merged, dh1_b, tm=1024, tn=1024, tl=tl_tn)
    dz = _matmul("d_z", dzab, wt_glu, nt=False, tm=tm_mm, tn=1024, tk=2 * D_MODEL, out_dtype=BF16)
    g_glu_t, g_glu_b = _matmul_tn("g_glu", dzab, z, tm=1024, tn=1024, tl=tl_tn)
    ex2 = exchange_start("exchange_start_mix", [g_glu_b, g_out_b])
    dproj, g_ssm_d, g_ar, g_ai, g_wbr, g_wbi, g_wcr, g_wci = _ssm_bwd_kb(
        dz, y, u, s_re, s_im, wb_re, wb_im, wc_re, wc_im, ssm_d + ex2[4][0:1, 0:1], tabs, q_ssm, dproj)
    dq, dkc, dkp, dkm, dvc, dvp, dvm, g_sinks = _attn_bwd(qn, kf, vf, attn_sinks, lse, attn, dattn)
    dproj, g_qn_t, g_kn_t = _qk_bwd(qkv, qn_t, kn_t, e_mat, dq, dkc, dkp, dkm, dvc, dvp, dvm, dproj)
    g_lam_re, g_lam_im, g_log_dt, g_b_re, g_b_im = disc_vjp(
        (g_ar.reshape(SSM_GROUPS, SSM_STATE), g_ai.reshape(SSM_GROUPS, SSM_STATE),
         _block_diag_b_t(g_wbr), _block_diag_b_t(g_wbi)))
    small_grads = {
        "q_norm": g_qn_t.reshape(N_Q_HEADS, HEAD_DIM).sum(0)[None],
        "k_norm": g_kn_t.reshape(N_KV_HEADS, HEAD_DIM).sum(0)[None], "attn_sinks": g_sinks,
        "lam_re": g_lam_re, "lam_im": g_lam_im, "log_dt": g_log_dt, "ssm_b_re": g_b_re, "ssm_b_im": g_b_im,
        "ssm_c_re": _block_diag_c_t(g_wcr)[None], "ssm_c_im": _block_diag_c_t(g_wci)[None],
        "ssm_d": g_ssm_d, "attn_branch_norm": g_abn, "ssm_branch_norm": g_sbn, "norm_ffn": g_norm_ffn,
    }
    early = [n for n in SMALL if n != "norm_mix"]
    packed_e = _pack([small_grads[n] for n in early])
    gs_e = _send_start("small_start_a", [packed_e], [landing(packed_e)], chunked=False)

    blocks_g, blocks_u, blocks_q = 2 * D_MODEL // 512, D_MODEL // 512, qkv_w // 512
    back = lambda i: jnp.where(i < blocks_g, i + blocks_q + blocks_u,
                               jnp.where(i < blocks_g + blocks_u, i - blocks_g + blocks_q, i - blocks_g - blocks_u))
    g_in_t, g_in_b = _matmul_tn("g_in", dproj, xn, tm=512, tn=1024, tl=tl_tn, after=gs_e[4], out_rows=back)
    ex3 = exchange_start("exchange_start_in", [g_in_b])
    dxn = _matmul("d_xn", dproj, wt_in_p, nt=False, tm=tm_ffn, tn=1024, tk=IN_COLS, out_dtype=BF16)
    grad_x2d, dmeta_blk, g_norm_mix = _final_bwd(h0, nm_t + ex3[4][0:1, 0:1], dxn, dh1, _pick(s_len, 512, BLK))
    packed_l = _pack([g_norm_mix, dmeta_blk[PAD:], loss_part])
    gs_l = _send_start("small_start_b", [packed_l], [landing(packed_l)], chunked=False)
    grads, deltas, new_m, new_v = {}, {}, {}, {}

    recv_ffn_in, recv_ffn_out = _send_wait("exchange_wait_ffn", ex1[0], ex1[1], ex1[2], ex1[3], gs_l[4],
                                           chunked=True, row_fns=ffn_fns)
    recv_glu, recv_out = _send_wait("exchange_wait_mix", ex2[0], ex2[1], ex2[2], ex2[3], recv_ffn_in,
                                    chunked=True)
    (recv_in,) = _send_wait("exchange_wait_in", ex3[0], ex3[1], ex3[2], ex3[3], recv_glu, chunked=True)
    big = [("w_in", g_in_t, True, recv_in, _plain_rows), ("w_glu", g_glu_t, True, recv_glu, _plain_rows),
           ("w_out", g_out, False, recv_out, _plain_rows), ("w_ffn_in", g_ffn_in_t, True, recv_ffn_in, _ffn_in_rows),
           ("w_ffn_out", g_ffn_out, False, recv_ffn_out, _plain_rows)]
    for name, g_full, transposed, recv, row_fn in big:
        m_rows = g_full.shape[0] // N_DEV
        own = lax.dynamic_slice(g_full, (row_fn(me, m_rows), 0), (m_rows, g_full.shape[1]))
        if name == "w_glu":
            grads[name] = _sum_slots("sum_" + name, recv, own).T[None]
            continue
        shp = weights[name].shape
        if transposed:
            as2d, back = (lambda a: a.reshape(shp[-2], shp[-1]).T), (lambda a: a.T.reshape(shp))
        else:
            as2d, back = (lambda a: a.reshape(shp[-2], shp[-1])), (lambda a: a.reshape(shp))
        g_shard, d, nm, nv = _sum_slots("sum_adamw_" + name, recv, own,
                                        (as2d(weights[name]), as2d(mom_m[name]), as2d(mom_v[name])))
        grads[name], deltas[name], new_m[name], new_v[name] = back(g_shard), back(d), back(nm), back(nv)
        last = d

    def adamw_2d(name):
        shp = weights[name].shape
        as2d = lambda a: a.reshape(shp[-2], shp[-1])
        d, nm, nv = _adamw("adamw_" + name, as2d(weights[name]), as2d(grads[name]), as2d(mom_m[name]),
                           as2d(mom_v[name]))
        deltas[name], new_m[name], new_v[name] = d.reshape(shp), nm.reshape(shp), nv.reshape(shp)

    adamw_2d("w_glu")

    def small_sum(tag, gs, packed, after):
        (gathered,) = _send_wait("small_wait_" + tag, gs[0], gs[1], gs[2], gs[3], after, chunked=False)
        return _sum_gathered("sum_small_" + tag, gathered, packed.shape[0])

    def small_adamw(tag, names):
        view = lambda n, a: jnp.swapaxes(a, -1, -2) if n.startswith("ssm_b") else a
        d, nm, nv = _adamw_many("adamw_small_" + tag, [view(n, weights[n]) for n in names],
                                [view(n, grads[n]) for n in names], [view(n, mom_m[n]) for n in names],
                                [view(n, mom_v[n]) for n in names])
        deltas.update((n, view(n, a)) for n, a in zip(names, d))
        new_m.update((n, view(n, a)) for n, a in zip(names, nm))
        new_v.update((n, view(n, a)) for n, a in zip(names, nv))

    g_sum_e = small_sum("a", gs_e, packed_e, last)
    grads.update(zip(early, _unpack(g_sum_e, [weights[n].shape for n in early])))
    wide = [n for n in early if n.startswith(("ssm_b", "ssm_c"))]
    small_adamw("wide", wide)
    g_sum_l = small_sum("b", gs_l, packed_l, g_sum_e)
    grads["norm_mix"], g_meta, loss_sum = _unpack(g_sum_l, [weights["norm_mix"].shape, (N_META, D_MODEL), (1, 1)])
    small_adamw("rest", [n for n in SMALL if n not in wide])
    grads["meta_tokens"] = lax.dynamic_slice(g_meta, (0, me * BLK), (N_META, BLK))
    adamw_2d("meta_tokens")

    loss = loss_sum[0, 0]
    return (loss, grad_x2d[None], *[grads[n] for n in WEIGHTS], *[deltas[n] for n in WEIGHTS],
            *[new_m[n] for n in WEIGHTS], *[new_v[n] for n in WEIGHTS])
```

```python
import math

import numpy as np
import jax
import jax.numpy as jnp
from jax import lax
from jax.experimental import pallas as pl
from jax.experimental.pallas import tpu as pltpu

F32 = jnp.float32
BF16 = jnp.bfloat16

D_MODEL = 1024
N_META = 16
HEAD_DIM = 64
N_Q_HEADS = 16
N_KV_HEADS = 4
Q_W = N_Q_HEADS * HEAD_DIM
KV_W = N_KV_HEADS * HEAD_DIM
SSM_GROUPS = 64
SSM_GROUP_CH = 16
SSM_STATE = 64
N_STATE = SSM_GROUPS * SSM_STATE
D_FF = 2816
IN_COLS = Q_W + 2 * KV_W + 3 * D_MODEL
EPS = 1e-6
BLK = 128
PAD = BLK - N_META
N_DEV = 8
NEG = -1e30
SSM_KB = 8
ST_KB = N_STATE // SSM_KB
LB_KB = ST_KB // BLK
N_LB = N_STATE // BLK

ADAM_LR = 0.001
ADAM_B1 = 0.9
ADAM_B2 = 0.999
ADAM_EPS = 1e-08
ADAM_WD = 0.01
ADAM_STEP = 10

VMEM_LIMIT = 48 * 1024 * 1024
MESH = pl.DeviceIdType.MESH

SMALL = ["norm_mix", "q_norm", "k_norm", "attn_sinks", "lam_re", "lam_im", "log_dt", "ssm_b_re", "ssm_b_im",
         "ssm_c_re", "ssm_c_im", "ssm_d", "attn_branch_norm", "ssm_branch_norm", "norm_ffn"]
WEIGHTS = ["meta_tokens", "norm_mix", "w_in", "q_norm", "k_norm", "attn_sinks", "lam_re", "lam_im", "log_dt",
           "ssm_b_re", "ssm_b_im", "ssm_c_re", "ssm_c_im", "ssm_d", "w_glu", "attn_branch_norm",
           "ssm_branch_norm", "w_out", "norm_ffn", "w_ffn_in", "w_ffn_out"]


def _params(**kw):
    return pltpu.CompilerParams(vmem_limit_bytes=VMEM_LIMIT, **kw)


def _pick(n, cap, mult=16):
    best = None
    for d in range(mult, min(n, cap) + 1, mult):
        if n % d == 0:
            best = d
    assert best is not None, (n, cap, mult)
    return best


def _my_index():
    return 4 * lax.axis_index("x") + 2 * lax.axis_index("y") + lax.axis_index("c")


def _rms(x, g):
    r = lax.rsqrt(jnp.mean(x * x, axis=-1, keepdims=True) + EPS)
    return x * r * g


def _rms_bwd(x, g, dy):
    r = lax.rsqrt(jnp.mean(x * x, axis=-1, keepdims=True) + EPS)
    t = dy * g
    dx = r * t - x * (r * r * r) * jnp.mean(t * x, axis=-1, keepdims=True)
    dg = jnp.sum(dy * (x * r), axis=0, keepdims=True)
    return dx, dg


def _sigmoid(x):
    return jax.nn.sigmoid(x)


def _gelu(x):
    k = math.sqrt(2.0 / math.pi)
    return 0.5 * x * (1.0 + jnp.tanh(k * (x + 0.044715 * (x * x * x))))


def _gelu_grad(x):
    k = math.sqrt(2.0 / math.pi)
    t = jnp.tanh(k * (x + 0.044715 * (x * x * x)))
    return 0.5 * (1.0 + t) + 0.5 * x * (1.0 - t * t) * (k * (1.0 + 3.0 * 0.044715 * (x * x)))


def _head_mean(x, e_ref):
    hi = x.astype(BF16)
    lo = (x - hi.astype(F32)).astype(BF16)
    e = e_ref[...]
    out = []
    for b in range(x.shape[1] // 256):
        sl = slice(256 * b, 256 * b + 256)
        s = (jnp.dot(hi[:, sl], e, preferred_element_type=F32)
             + jnp.dot(lo[:, sl], e, preferred_element_type=F32))
        out.append(s)
    s = out[0] if len(out) == 1 else jnp.concatenate(out, axis=1)
    return s * (1.0 / HEAD_DIM)


def _head_rms(x, g, e_ref):
    r = lax.rsqrt(_head_mean(x * x, e_ref) + EPS)
    return x * r * g


def _head_rms_bwd(x, g, dy, e_ref):
    r = lax.rsqrt(_head_mean(x * x, e_ref) + EPS)
    t = dy * g
    dx = r * t - x * (r * r * r) * _head_mean(t * x, e_ref)
    dg = jnp.sum(dy * (x * r), axis=0, keepdims=True)
    return dx, dg


def _lane_half(shape):
    lane = lax.broadcasted_iota(jnp.int32, shape, len(shape) - 1)
    return (lane >> 6) & 1


def _matmul(name, a, w, *, nt, tm, tn, tk, n=None, w_off=0, res=None, norm_g=None, out_dtype=F32):
    m_dim, k_dim = a.shape
    n_dim = n if n is not None else (w.shape[0] if nt else w.shape[1])
    gm, gn, gk = m_dim // tm, n_dim // tn, k_dim // tk
    assert gm * tm == m_dim and gn * tn == n_dim and gk * tk == k_dim, (name, a.shape, w.shape, tm, tn, tk)
    assert norm_g is None or tn == n_dim
    direct = out_dtype == F32
    dn = (((1,), (1,)), ((), ())) if nt else (((1,), (0,)), ((), ()))

    def body(*refs):
        refs = list(refs)
        a_ref, w_ref = refs[0], refs[1]
        pos = 2
        r_ref = g_ref = on_ref = None
        if res is not None:
            r_ref, pos = refs[pos], pos + 1
        if norm_g is not None:
            g_ref, pos = refs[pos], pos + 1
        o_ref, pos = refs[pos], pos + 1
        if norm_g is not None:
            on_ref, pos = refs[pos], pos + 1
        part = lax.dot_general(a_ref[...], w_ref[...], dn, preferred_element_type=F32)
        if gk == 1:
            r = part if r_ref is None else r_ref[...] + part
            o_ref[...] = r.astype(out_dtype)
            if on_ref is not None:
                on_ref[...] = _rms(r, g_ref[...]).astype(BF16)
            return
        acc = o_ref if direct else refs[pos]
        k = pl.program_id(2)

        @pl.when(k == 0)
        def _():
            acc[...] = part if r_ref is None or not direct else r_ref[...] + part

        @pl.when(k > 0)
        def _():
            acc[...] += part

        @pl.when(k == gk - 1)
        def _():
            if not direct:
                r = acc[...]
                if r_ref is not None:
                    r = r_ref[...] + r
                o_ref[...] = r.astype(out_dtype)
            if on_ref is not None:
                on_ref[...] = _rms(o_ref[...].astype(F32), g_ref[...]).astype(BF16)

    if nt:
        w_spec = pl.BlockSpec((tn, tk), lambda i, j, k: (j + w_off, k))
    else:
        w_spec = pl.BlockSpec((tk, tn), lambda i, j, k: (k, j))
    in_specs = [pl.BlockSpec((tm, tk), lambda i, j, k: (i, k)), w_spec]
    args = [a, w]
    out_spec = pl.BlockSpec((tm, tn), lambda i, j, k: (i, j))
    out_specs, out_shape = [out_spec], [jax.ShapeDtypeStruct((m_dim, n_dim), out_dtype)]
    if res is not None:
        in_specs.append(out_spec)
        args.append(res)
    if norm_g is not None:
        in_specs.append(pl.BlockSpec((1, tn), lambda i, j, k: (0, 0)))
        args.append(norm_g)
        out_specs.append(out_spec)
        out_shape.append(jax.ShapeDtypeStruct((m_dim, n_dim), BF16))
    out = pl.pallas_call(
        body, name=name, grid=(gm, gn, gk),
        in_specs=in_specs, out_specs=out_specs, out_shape=out_shape,
        scratch_shapes=[] if direct or gk == 1 else [pltpu.VMEM((tm, tn), F32)],
        compiler_params=_params(dimension_semantics=("parallel", "parallel", "arbitrary")),
    )(*args)
    return out if norm_g is not None else out[0]


def _matmul_tn(name, a, b, *, tm, tn, tl, after=None, out_rows=None):
    l_dim, m_dim = a.shape
    n_dim = b.shape[1]
    gm, gn, gl = m_dim // tm, n_dim // tn, l_dim // tl
    assert gm * tm == m_dim and gn * tn == n_dim and gl * tl == l_dim, (name, a.shape, b.shape, tm, tn, tl)

    def body(*refs):
        a_ref, b_ref = refs[0], refs[1]
        o_ref, ob_ref = refs[-2], refs[-1]

        @pl.when(pl.program_id(2) == 0)
        def _():
            o_ref[...] = jnp.zeros_like(o_ref)

        o_ref[...] += lax.dot_general(a_ref[...], b_ref[...], (((0,), (0,)), ((), ())),
                                      preferred_element_type=F32)

        @pl.when(pl.program_id(2) == gl - 1)
        def _():
            ob_ref[...] = o_ref[...].astype(BF16)

    out_row = out_rows if out_rows is not None else (lambda i: i)
    out_spec = pl.BlockSpec((tm, tn), lambda i, j, l: (out_row(i), j))
    in_specs = [pl.BlockSpec((tl, tm), lambda i, j, l: (l, i)), pl.BlockSpec((tl, tn), lambda i, j, l: (l, j))]
    args = [a, b]
    if after is not None:
        in_specs.append(pl.BlockSpec(memory_space=pl.ANY))
        args.append(after)
    return pl.pallas_call(
        body, name=name, grid=(gm, gn, gl),
        in_specs=in_specs,
        out_specs=[out_spec, out_spec],
        out_shape=[jax.ShapeDtypeStruct((m_dim, n_dim), F32), jax.ShapeDtypeStruct((m_dim, n_dim), BF16)],
        compiler_params=_params(dimension_semantics=("parallel", "parallel", "arbitrary")),
    )(*args)


def _row_spec(tm, cols, f=None):
    if f is None:
        return pl.BlockSpec((tm, cols), lambda i: (i, 0))
    return pl.BlockSpec((tm, cols), lambda i: (f(i), 0))


def _full_spec(shape):
    nd = len(shape)
    return pl.BlockSpec(shape, lambda i: (0,) * nd)


def _shifted_specs(n_sub, n_blocks):
    return [_row_spec(BLK, D_MODEL, (lambda i, k=k: jnp.clip(n_sub * i - 1 + k, 0, n_blocks - 1)))
            for k in range(n_sub)]


def _embed_norm(x2d, g, tm):
    s_len = x2d.shape[0]
    l_dim = s_len + BLK
    n_sub = tm // BLK

    def body(*refs):
        x_refs, g_ref, h_ref, xn_ref = refs[:n_sub], refs[n_sub], refs[n_sub + 1], refs[n_sub + 2]
        i = pl.program_id(0)
        for k in range(n_sub):
            rows = slice(BLK * k, BLK * k + BLK)
            h = x_refs[k][...] * jnp.where(n_sub * i + k >= 1, 1.0, 0.0)
            h_ref[rows, :] = h
            xn_ref[rows, :] = _rms(h, g_ref[...]).astype(BF16)

    return pl.pallas_call(
        body, name="embed_norm", grid=(l_dim // tm,),
        in_specs=_shifted_specs(n_sub, s_len // BLK) + [_full_spec((1, D_MODEL))],
        out_specs=[_row_spec(tm, D_MODEL), _row_spec(tm, D_MODEL)],
        out_shape=[jax.ShapeDtypeStruct((l_dim, D_MODEL), F32),
                   jax.ShapeDtypeStruct((l_dim, D_MODEL), BF16)],
        compiler_params=_params(),
    )(*([x2d] * n_sub), g)


def _embed_meta(meta_pad, g, h0, xn):
    def body(mp_ref, g_ref, h_in, xn_in, h_ref, xn_ref):
        h_ref[...] = mp_ref[...]
        xn_ref[...] = _rms(mp_ref[...], g_ref[...]).astype(BF16)

    any_spec = pl.BlockSpec(memory_space=pl.ANY)
    return pl.pallas_call(
        body, name="embed_meta", grid=(1,),
        in_specs=[_full_spec((BLK, D_MODEL)), _full_spec((1, D_MODEL)), any_spec, any_spec],
        out_specs=[_row_spec(BLK, D_MODEL), _row_spec(BLK, D_MODEL)],
        out_shape=[jax.ShapeDtypeStruct(h0.shape, F32), jax.ShapeDtypeStruct(xn.shape, BF16)],
        input_output_aliases={2: 0, 3: 1},
        compiler_params=_params(),
    )(meta_pad, g, h0, xn)


KVX_W = 2 * N_KV_HEADS * BLK


def _qk_prep(qkv, q_norm_t, k_norm_t, e_mat, tm):
    l_dim = qkv.shape[0]

    def body(x_ref, qg_ref, kg_ref, e_ref, q_ref, kf_ref, vf_ref):
        x = x_ref[...]
        q = _head_rms(x[:, :Q_W], qg_ref[...], e_ref) * (HEAD_DIM ** -0.5)
        q_ref[...] = q.astype(BF16)
        k = _head_rms(x[:, Q_W:Q_W + KV_W], kg_ref[...], e_ref)
        v = x[:, Q_W + KV_W:Q_W + 2 * KV_W]
        half = _lane_half((tm, BLK))
        for src, dst in ((k, kf_ref), (v, vf_ref)):
            for kv in range(N_KV_HEADS):
                blk = src[:, BLK * (kv // 2):BLK * (kv // 2) + BLK]
                swapped = pltpu.roll(blk, HEAD_DIM, axis=1)
                for e in range(2):
                    val = blk if kv % 2 == e else swapped
                    idx = 2 * kv + e
                    dst[:, BLK * idx:BLK * idx + BLK] = jnp.where(half == e, val, 0.0).astype(BF16)

    return pl.pallas_call(
        body, name="qk_prep", grid=(l_dim // tm,),
        in_specs=[_row_spec(tm, Q_W + 2 * KV_W), _full_spec((1, Q_W)), _full_spec((1, KV_W)),
                  _full_spec((256, 256))],
        out_specs=[_row_spec(tm, Q_W), _row_spec(tm, KVX_W), _row_spec(tm, KVX_W)],
        out_shape=[jax.ShapeDtypeStruct((l_dim, Q_W), BF16),
                   jax.ShapeDtypeStruct((l_dim, KVX_W), BF16),
                   jax.ShapeDtypeStruct((l_dim, KVX_W), BF16)],
        compiler_params=_params(),
    )(qkv, q_norm_t, k_norm_t, e_mat)


def _merge_fwd(attn, zab, gates, abn, sbn, tm):
    l_dim = attn.shape[0]

    def body(a_ref, z_ref, g_ref, an_ref, sn_ref, o_ref):
        z = z_ref[...].astype(F32)
        g = g_ref[...].astype(F32)
        ssm = z[:, :D_MODEL] * _sigmoid(z[:, D_MODEL:])
        merged = (_sigmoid(g[:, :D_MODEL]) * _rms(a_ref[...], an_ref[...])
                  + _sigmoid(g[:, D_MODEL:]) * _rms(ssm, sn_ref[...]))
        o_ref[...] = merged.astype(BF16)

    return pl.pallas_call(
        body, name="merge_fwd", grid=(l_dim // tm,),
        in_specs=[_row_spec(tm, D_MODEL), _row_spec(tm, 2 * D_MODEL), _row_spec(tm, 2 * D_MODEL),
                  _full_spec((1, D_MODEL)), _full_spec((1, D_MODEL))],
        out_specs=_row_spec(tm, D_MODEL),
        out_shape=jax.ShapeDtypeStruct((l_dim, D_MODEL), BF16),
        compiler_params=_params(),
    )(attn, zab, gates, abn, sbn)


def _merge_bwd(attn, zab, gates, abn, sbn, dmerged, tm):
    l_dim = attn.shape[0]

    def body(a_ref, z_ref, g_ref, an_ref, sn_ref, dm_ref, da_ref, dz_ref, dg_ref, dan_ref, dsn_ref):
        @pl.when(pl.program_id(0) == 0)
        def _():
            dan_ref[...] = jnp.zeros_like(dan_ref)
            dsn_ref[...] = jnp.zeros_like(dsn_ref)

        z = z_ref[...].astype(F32)
        g = g_ref[...].astype(F32)
        dm = dm_ref[...].astype(F32)
        attn_v = a_ref[...]
        za, zb = z[:, :D_MODEL], z[:, D_MODEL:]
        sb = _sigmoid(zb)
        ssm = za * sb
        s_ga, s_gs = _sigmoid(g[:, :D_MODEL]), _sigmoid(g[:, D_MODEL:])
        a_n = _rms(attn_v, an_ref[...])
        s_n = _rms(ssm, sn_ref[...])
        dg_ref[:, :D_MODEL] = (dm * a_n * s_ga * (1.0 - s_ga)).astype(BF16)
        dg_ref[:, D_MODEL:] = (dm * s_n * s_gs * (1.0 - s_gs)).astype(BF16)
        dattn, dan = _rms_bwd(attn_v, an_ref[...], dm * s_ga)
        dssm, dsn = _rms_bwd(ssm, sn_ref[...], dm * s_gs)
        da_ref[...] = dattn.astype(BF16)
        dz_ref[:, :D_MODEL] = (dssm * sb).astype(BF16)
        dz_ref[:, D_MODEL:] = (dssm * za * sb * (1.0 - sb)).astype(BF16)
        dan_ref[...] += dan
        dsn_ref[...] += dsn

    return pl.pallas_call(
        body, name="merge_bwd", grid=(l_dim // tm,),
        in_specs=[_row_spec(tm, D_MODEL), _row_spec(tm, 2 * D_MODEL), _row_spec(tm, 2 * D_MODEL),
                  _full_spec((1, D_MODEL)), _full_spec((1, D_MODEL)), _row_spec(tm, D_MODEL)],
        out_specs=[_row_spec(tm, D_MODEL), _row_spec(tm, 2 * D_MODEL), _row_spec(tm, 2 * D_MODEL),
                   _full_spec((1, D_MODEL)), _full_spec((1, D_MODEL))],
        out_shape=[jax.ShapeDtypeStruct((l_dim, D_MODEL), BF16),
                   jax.ShapeDtypeStruct((l_dim, 2 * D_MODEL), BF16),
                   jax.ShapeDtypeStruct((l_dim, IN_COLS), BF16),
                   jax.ShapeDtypeStruct((1, D_MODEL), F32), jax.ShapeDtypeStruct((1, D_MODEL), F32)],
        compiler_params=_params(),
    )(attn, zab, gates, abn, sbn, dmerged)


FF_TILE = D_FF // 2


def _ffn_in_swiglu(hn, wt_ffn_in, tm):
    l_dim = hn.shape[0]
    nt = (((1,), (1,)), ((), ()))

    def body(a_ref, w_ref, gu_ref, act_ref):
        r = lax.dot_general(a_ref[...], w_ref[...], nt, preferred_element_type=F32)
        gate, up = r[:, :FF_TILE], r[:, FF_TILE:]
        gu_ref[...] = r.astype(BF16)
        act_ref[...] = (gate * _sigmoid(gate) * up).astype(BF16)

    return pl.pallas_call(
        body, name="ffn_in_swiglu", grid=(l_dim // tm, 2),
        in_specs=[pl.BlockSpec((tm, D_MODEL), lambda i, j: (i, 0)),
                  pl.BlockSpec((2 * FF_TILE, D_MODEL), lambda i, j: (j, 0))],
        out_specs=[pl.BlockSpec((tm, 2 * FF_TILE), lambda i, j: (i, j)),
                   pl.BlockSpec((tm, FF_TILE), lambda i, j: (i, j))],
        out_shape=[jax.ShapeDtypeStruct((l_dim, 2 * D_FF), BF16), jax.ShapeDtypeStruct((l_dim, D_FF), BF16)],
        compiler_params=_params(dimension_semantics=("parallel", "parallel")),
    )(hn, wt_ffn_in)


def _d_act_swiglu(dh2_b, w_ffn_out, gu, tm):
    l_dim = dh2_b.shape[0]
    nt = (((1,), (1,)), ((), ()))

    def body(d_ref, w_ref, gu_ref, o_ref):
        d = lax.dot_general(d_ref[...], w_ref[...], nt, preferred_element_type=F32)
        gate = gu_ref[:, :FF_TILE].astype(F32)
        up = gu_ref[:, FF_TILE:].astype(F32)
        s = _sigmoid(gate)
        o_ref[:, :FF_TILE] = (d * up * (s * (1.0 + gate * (1.0 - s)))).astype(BF16)
        o_ref[:, FF_TILE:] = (d * (gate * s)).astype(BF16)

    return pl.pallas_call(
        body, name="d_act_swiglu", grid=(l_dim // tm, 2),
        in_specs=[pl.BlockSpec((tm, D_MODEL), lambda i, j: (i, 0)),
                  pl.BlockSpec((FF_TILE, D_MODEL), lambda i, j: (j, 0)),
                  pl.BlockSpec((tm, 2 * FF_TILE), lambda i, j: (i, j))],
        out_specs=pl.BlockSpec((tm, 2 * FF_TILE), lambda i, j: (i, j)),
        out_shape=jax.ShapeDtypeStruct((l_dim, 2 * D_FF), BF16),
        compiler_params=_params(dimension_semantics=("parallel", "parallel")),
    )(dh2_b, w_ffn_out, gu)


def _loss_grad(h2, target2d, tm):
    l_dim = h2.shape[0]
    n_sub = tm // BLK

    def body(*refs):
        h_ref, t_refs = refs[0], refs[1:1 + n_sub]
        d_ref, db_ref, loss_ref = refs[1 + n_sub:]
        i = pl.program_id(0)

        @pl.when(i == 0)
        def _():
            loss_ref[...] = jnp.zeros_like(loss_ref)

        for k in range(n_sub):
            rows = slice(BLK * k, BLK * k + BLK)
            real = jnp.where(n_sub * i + k >= 1, 1.0, 0.0)
            err = (h_ref[rows, :] - t_refs[k][...]) * real
            d = err * (1.0 / D_MODEL)
            d_ref[rows, :] = d
            db_ref[rows, :] = d.astype(BF16)
            loss_ref[...] += 0.5 * jnp.sum(jnp.mean(err * err, axis=-1, keepdims=True), axis=0, keepdims=True)

    return pl.pallas_call(
        body, name="loss_grad", grid=(l_dim // tm,),
        in_specs=[_row_spec(tm, D_MODEL)] + _shifted_specs(n_sub, target2d.shape[0] // BLK),
        out_specs=[_row_spec(tm, D_MODEL), _row_spec(tm, D_MODEL), _full_spec((1, 1))],
        out_shape=[jax.ShapeDtypeStruct((l_dim, D_MODEL), F32), jax.ShapeDtypeStruct((l_dim, D_MODEL), BF16),
                   jax.ShapeDtypeStruct((1, 1), F32)],
        compiler_params=_params(),
    )(h2, *([target2d] * n_sub))


def _norm_bwd_res(name, h, g, dy, dres, tm):
    l_dim = h.shape[0]

    def body(h_ref, g_ref, dy_ref, dr_ref, o_ref, ob_ref, dg_ref):
        @pl.when(pl.program_id(0) == 0)
        def _():
            dg_ref[...] = jnp.zeros_like(dg_ref)

        dx, dg = _rms_bwd(h_ref[...], g_ref[...], dy_ref[...].astype(F32))
        out = dr_ref[...] + dx
        o_ref[...] = out
        ob_ref[...] = out.astype(BF16)
        dg_ref[...] += dg

    return pl.pallas_call(
        body, name=name, grid=(l_dim // tm,),
        in_specs=[_row_spec(tm, D_MODEL), _full_spec((1, D_MODEL)), _row_spec(tm, D_MODEL), _row_spec(tm, D_MODEL)],
        out_specs=[_row_spec(tm, D_MODEL), _row_spec(tm, D_MODEL), _full_spec((1, D_MODEL))],
        out_shape=[jax.ShapeDtypeStruct((l_dim, D_MODEL), F32), jax.ShapeDtypeStruct((l_dim, D_MODEL), BF16),
                   jax.ShapeDtypeStruct((1, D_MODEL), F32)],
        compiler_params=_params(),
    )(h, g, dy, dres)


def _final_bwd(h0, g, dxn, dh1, tm):
    l_dim = h0.shape[0]
    n_sub = tm // BLK
    n_tiles = (l_dim - BLK) // tm

    def sub_specs():
        return [_row_spec(BLK, D_MODEL, (lambda j, k=k: jnp.where(j < n_tiles, n_sub * j + 1 + k, 0)))
                for k in range(n_sub)]

    def body(*refs):
        h_refs, g_ref = refs[:n_sub], refs[n_sub]
        dy_refs, dr_refs = refs[n_sub + 1:2 * n_sub + 1], refs[2 * n_sub + 1:3 * n_sub + 1]
        gx_ref, gm_ref, dg_ref = refs[3 * n_sub + 1:]
        j = pl.program_id(0)

        @pl.when(j == 0)
        def _():
            dg_ref[...] = jnp.zeros_like(dg_ref)

        def block(k):
            dx, dg = _rms_bwd(h_refs[k][...], g_ref[...], dy_refs[k][...].astype(F32))
            dg_ref[...] += dg
            return dr_refs[k][...] + dx

        @pl.when(j < n_tiles)
        def _():
            for k in range(n_sub):
                gx_ref[BLK * k:BLK * k + BLK, :] = block(k)

        @pl.when(j == n_tiles)
        def _():
            gm_ref[...] = block(0)

    return pl.pallas_call(
        body, name="final_bwd", grid=(n_tiles + 1,),
        in_specs=sub_specs() + [_full_spec((1, D_MODEL))] + sub_specs() + sub_specs(),
        out_specs=[_row_spec(tm, D_MODEL, lambda j: jnp.minimum(j, n_tiles - 1)), _full_spec((BLK, D_MODEL)),
                   _full_spec((1, D_MODEL))],
        out_shape=[jax.ShapeDtypeStruct((l_dim - BLK, D_MODEL), F32), jax.ShapeDtypeStruct((BLK, D_MODEL), F32),
                   jax.ShapeDtypeStruct((1, D_MODEL), F32)],
        compiler_params=_params(),
    )(*([h0] * n_sub), g, *([dxn] * n_sub), *([dh1] * n_sub))


def _attn_bias():
    qi = np.arange(2 * BLK)[:, None] % BLK
    col = np.arange(3 * BLK)[None, :]
    kj, part = col % BLK, col // BLK
    out = []
    for n in range(3):
        meta_ok = (part == 0) & (kj >= PAD) & (n >= 1)
        prev_ok = (part == 1) & (kj > qi) & (n >= 2)
        cur_ok = (part == 2) & (kj <= qi) & ((n >= 1) | (kj >= PAD))
        out.append(np.where(meta_ok | prev_ok | cur_ok, 0.0, NEG))
    return jnp.asarray(np.stack(out), F32)


def _bias_spec():
    return pl.BlockSpec((1, 2 * BLK, 3 * BLK), lambda i: (jnp.minimum(i, 2), 0, 0))


def _attn_scores(q_ref, kwin, sk_ref, bias, kv, e):
    qs = jnp.concatenate([q_ref[:, BLK * (2 * kv):BLK * (2 * kv) + BLK],
                          q_ref[:, BLK * (2 * kv + 1):BLK * (2 * kv + 1) + BLK]], axis=0)
    s = lax.dot_general(qs, kwin, (((1,), (1,)), ((), ())), preferred_element_type=F32)
    h0 = 4 * kv + e
    row = lax.broadcasted_iota(jnp.int32, (2 * BLK, 1), 0)
    sink = jnp.where(row < BLK, sk_ref[:, h0:h0 + 1], sk_ref[:, h0 + 2:h0 + 3])
    return qs, s + bias, sink


def _attn_specs(nb):
    prev = lambda i: jnp.maximum(i - 1, 0)
    zero = lambda i: 0
    kv_specs = [_row_spec(BLK, KVX_W, zero), _row_spec(BLK, KVX_W, prev), _row_spec(BLK, KVX_W)]
    return kv_specs


def _attn_fwd(qn, kf, vf, sinks):
    l_dim = qn.shape[0]
    nb = l_dim // BLK

    def body(q_ref, km_ref, kp_ref, kc_ref, vm_ref, vp_ref, vc_ref, sk_ref, bias_ref, o_ref, lse_ref):
        bias = bias_ref[0]
        lane = lax.broadcasted_iota(jnp.int32, (BLK, BLK), 1)
        lse_all = jnp.zeros((BLK, BLK), F32)
        for kv in range(N_KV_HEADS):
            outs = []
            for e in range(2):
                sl = slice(BLK * (2 * kv + e), BLK * (2 * kv + e) + BLK)
                kwin = jnp.concatenate([km_ref[:, sl], kp_ref[:, sl], kc_ref[:, sl]], axis=0)
                vwin = jnp.concatenate([vm_ref[:, sl], vp_ref[:, sl], vc_ref[:, sl]], axis=0)
                _, s, sink = _attn_scores(q_ref, kwin, sk_ref, bias, kv, e)
                m = jnp.maximum(jnp.max(s, axis=-1, keepdims=True), sink)
                ex = jnp.exp(s - m)
                den = jnp.sum(ex, axis=-1, keepdims=True) + jnp.exp(sink - m)
                p = ex * (1.0 / den)
                outs.append(jnp.dot(p.astype(BF16), vwin, preferred_element_type=F32))
                lse = m + jnp.log(den)
                lse_all = jnp.where(lane == 4 * kv + e, lse[:BLK], lse_all)
                lse_all = jnp.where(lane == 4 * kv + 2 + e, lse[BLK:], lse_all)
            o = outs[0] + outs[1]
            o_ref[:, BLK * (2 * kv):BLK * (2 * kv) + BLK] = o[:BLK]
            o_ref[:, BLK * (2 * kv + 1):BLK * (2 * kv + 1) + BLK] = o[BLK:]
        lse_ref[...] = lse_all

    kv_specs = _attn_specs(nb)
    return pl.pallas_call(
        body, name="attn_fwd", grid=(nb,),
        in_specs=[_row_spec(BLK, Q_W)] + kv_specs + kv_specs + [_full_spec((1, N_Q_HEADS)), _bias_spec()],
        out_specs=[_row_spec(BLK, Q_W), _row_spec(BLK, BLK)],
        out_shape=[jax.ShapeDtypeStruct((l_dim, Q_W), F32), jax.ShapeDtypeStruct((l_dim, BLK), F32)],
        compiler_params=_params(),
    )(qn, kf, kf, kf, vf, vf, vf, sinks, _attn_bias())


def _attn_bwd(qn, kf, vf, sinks, lse, attn, dattn):
    l_dim = qn.shape[0]
    nb = l_dim // BLK
    wide = KV_W
    tn = (((0,), (0,)), ((), ()))
    nt = (((1,), (1,)), ((), ()))

    def body(q_ref, km_ref, kp_ref, kc_ref, vm_ref, vp_ref, vc_ref, sk_ref, bias_ref, lse_ref, o_ref, do_ref,
             dq_ref, dkc_ref, dkp_ref, dkm_ref, dvc_ref, dvp_ref, dvm_ref, dsk_ref):
        @pl.when(pl.program_id(0) == 0)
        def _():
            dkm_ref[...] = jnp.zeros_like(dkm_ref)
            dvm_ref[...] = jnp.zeros_like(dvm_ref)
            dsk_ref[...] = jnp.zeros_like(dsk_ref)

        bias = bias_ref[0]
        half = _lane_half((BLK, BLK))
        half2 = _lane_half((2 * BLK, BLK))
        half3 = _lane_half((3 * BLK, BLK))
        lane16 = lax.broadcasted_iota(jnp.int32, (1, N_Q_HEADS), 1)
        dsk = jnp.zeros((1, N_Q_HEADS), F32)
        folded_k, folded_v = [], []
        for kv in range(N_KV_HEADS):
            j0, j1 = 2 * kv, 2 * kv + 1
            do0 = do_ref[:, BLK * j0:BLK * j0 + BLK]
            do1 = do_ref[:, BLK * j1:BLK * j1 + BLK]
            do0f, do1f = do0.astype(F32), do1.astype(F32)
            prod0 = do0f * o_ref[:, BLK * j0:BLK * j0 + BLK]
            prod1 = do1f * o_ref[:, BLK * j1:BLK * j1 + BLK]
            dos = jnp.concatenate([do0, do1], axis=0)
            dqs, dks, dvs = [], [], []
            for e in range(2):
                sl = slice(BLK * (2 * kv + e), BLK * (2 * kv + e) + BLK)
                kwin = jnp.concatenate([km_ref[:, sl], kp_ref[:, sl], kc_ref[:, sl]], axis=0)
                vwin = jnp.concatenate([vm_ref[:, sl], vp_ref[:, sl], vc_ref[:, sl]], axis=0)
                qs, s, sink = _attn_scores(q_ref, kwin, sk_ref, bias, kv, e)
                h0 = 4 * kv + e
                lse_rows = jnp.concatenate([lse_ref[:, h0:h0 + 1], lse_ref[:, h0 + 2:h0 + 3]], axis=0)
                p = jnp.exp(s - lse_rows)
                p_sink = jnp.exp(sink - lse_rows)
                delta = jnp.concatenate(
                    [jnp.sum(jnp.where(half == e, prod0, 0.0), axis=-1, keepdims=True),
                     jnp.sum(jnp.where(half == e, prod1, 0.0), axis=-1, keepdims=True)], axis=0)
                dp = lax.dot_general(dos, vwin, nt, preferred_element_type=F32)
                ds = (p * (dp - delta)).astype(BF16)
                pb = p.astype(BF16)
                dqs.append(jnp.dot(ds, kwin, preferred_element_type=F32))
                dks.append(lax.dot_general(ds, qs, tn, preferred_element_type=F32))
                dvs.append(lax.dot_general(pb, dos, tn, preferred_element_type=F32))
                sink_g = -(p_sink * delta)
                g_lo = jnp.sum(sink_g[:BLK], axis=0, keepdims=True)
                g_hi = jnp.sum(sink_g[BLK:], axis=0, keepdims=True)
                dsk = dsk + jnp.where(lane16 == h0, g_lo, 0.0) + jnp.where(lane16 == h0 + 2, g_hi, 0.0)
            dq = jnp.where(half2 == 0, dqs[0], dqs[1])
            dq_ref[:, BLK * j0:BLK * j0 + BLK] = dq[:BLK].astype(BF16)
            dq_ref[:, BLK * j1:BLK * j1 + BLK] = dq[BLK:].astype(BF16)
            own = kv % 2
            folded_k.append(dks[own] + pltpu.roll(dks[1 - own], HEAD_DIM, axis=1))
            folded_v.append(dvs[own] + pltpu.roll(dvs[1 - own], HEAD_DIM, axis=1))
            if own == 1:
                cols = slice(BLK * (kv // 2), BLK * (kv // 2) + BLK)
                for folded, m_ref, p_ref, c_ref in ((folded_k, dkm_ref, dkp_ref, dkc_ref),
                                                    (folded_v, dvm_ref, dvp_ref, dvc_ref)):
                    both = jnp.where(half3 == 0, folded[0], folded[1])
                    m_ref[:, cols] += both[:BLK]
                    p_ref[:, cols] = both[BLK:2 * BLK].astype(BF16)
                    c_ref[:, cols] = both[2 * BLK:].astype(BF16)
                folded_k, folded_v = [], []
        dsk_ref[...] += dsk

    kv_specs = _attn_specs(nb)
    row_wide = _row_spec(BLK, wide)
    acc_wide = _full_spec((BLK, wide))
    big = jax.ShapeDtypeStruct((l_dim, wide), BF16)
    return pl.pallas_call(
        body, name="attn_bwd", grid=(nb,),
        in_specs=[_row_spec(BLK, Q_W)] + kv_specs + kv_specs
        + [_full_spec((1, N_Q_HEADS)), _bias_spec(), _row_spec(BLK, BLK), _row_spec(BLK, Q_W),
           _row_spec(BLK, Q_W)],
        out_specs=[_row_spec(BLK, Q_W), row_wide, row_wide, acc_wide, row_wide, row_wide, acc_wide,
                   _full_spec((1, N_Q_HEADS))],
        out_shape=[jax.ShapeDtypeStruct((l_dim, Q_W), BF16), big, big, jax.ShapeDtypeStruct((BLK, wide), F32),
                   big, big, jax.ShapeDtypeStruct((BLK, wide), F32), jax.ShapeDtypeStruct((1, N_Q_HEADS), F32)],
        compiler_params=_params(),
    )(qn, kf, kf, kf, vf, vf, vf, sinks, _attn_bias(), lse, attn, dattn)


def _qk_bwd(qkv, q_norm_t, k_norm_t, e_mat, dq, dkc, dkp, dkm, dvc, dvp, dvm, dproj):
    l_dim = qkv.shape[0]
    nb = l_dim // BLK
    wide = KV_W

    def body(x_ref, qg_ref, kg_ref, e_ref, dq_ref, dkc_ref, dkp_ref, dkm_ref, dvc_ref, dvp_ref, dvm_ref,
             dproj_ref, o_ref, dqg_ref, dkg_ref):
        i = pl.program_id(0)

        @pl.when(i == 0)
        def _():
            dqg_ref[...] = jnp.zeros_like(dqg_ref)
            dkg_ref[...] = jnp.zeros_like(dkg_ref)

        first = jnp.where(i == 0, 1.0, 0.0)
        not_last = jnp.where(i < nb - 1, 1.0, 0.0)
        dk_x = dkc_ref[...].astype(F32) + not_last * dkp_ref[...].astype(F32) + first * dkm_ref[...]
        dv_x = dvc_ref[...].astype(F32) + not_last * dvp_ref[...].astype(F32) + first * dvm_ref[...]
        x = x_ref[...]
        dqx, dqg = _head_rms_bwd(x[:, :Q_W], qg_ref[...], dq_ref[...].astype(F32) * (HEAD_DIM ** -0.5), e_ref)
        dkx, dkg = _head_rms_bwd(x[:, Q_W:Q_W + KV_W], kg_ref[...], dk_x, e_ref)
        o_ref[:, :Q_W] = dqx.astype(BF16)
        o_ref[:, Q_W:Q_W + KV_W] = dkx.astype(BF16)
        o_ref[:, Q_W + KV_W:] = dv_x.astype(BF16)
        dqg_ref[...] += dqg
        dkg_ref[...] += dkg

    nxt = lambda i: jnp.minimum(i + 1, nb - 1)
    row_wide = _row_spec(BLK, wide)
    nxt_wide = _row_spec(BLK, wide, nxt)
    acc_wide = _full_spec((BLK, wide))
    qkv_w = Q_W + 2 * KV_W
    in_specs = [_row_spec(BLK, qkv_w), _full_spec((1, Q_W)), _full_spec((1, KV_W)),
                _full_spec((256, 256)), _row_spec(BLK, Q_W),
                row_wide, nxt_wide, acc_wide, row_wide, nxt_wide, acc_wide, pl.BlockSpec(memory_space=pl.ANY)]
    return pl.pallas_call(
        body, name="qk_bwd", grid=(nb,),
        in_specs=in_specs,
        out_specs=[pl.BlockSpec((BLK, qkv_w), lambda i: (i, 3 * D_MODEL // qkv_w)),
                   _full_spec((1, Q_W)), _full_spec((1, KV_W))],
        input_output_aliases={len(in_specs) - 1: 0},
        out_shape=[jax.ShapeDtypeStruct(dproj.shape, BF16),
                   jax.ShapeDtypeStruct((1, Q_W), F32), jax.ShapeDtypeStruct((1, KV_W), F32)],
        compiler_params=_params(),
    )(qkv, q_norm_t, k_norm_t, e_mat, dq, dkc, dkp, dkm, dvc, dvp, dvm, dproj)


GRP = 8


def _strided(r, g):
    return pl.ds(r, g, stride=GRP)


def _lane_slab(ref, i, r, g):
    return ref[i, _strided(r, g), :]


def _chunk_carries(xr_ref, xi_ref, i, ar, ai, sqr_ref, sqi_ref, seq_r, seq_i, cin_r, cin_i, sign, reverse):
    g = xr_ref.shape[1] // GRP
    sr = si = None
    for r in (range(GRP - 1, -1, -1) if reverse else range(GRP)):
        xr, xi = _lane_slab(xr_ref, i, r, g), _lane_slab(xi_ref, i, r, g)
        if sr is not None:
            xr, xi = xr + ar * sr - ai * si, xi + ar * si + ai * sr
        sr, si = xr, xi
    row = lax.broadcasted_iota(jnp.int32, sr.shape, 0)
    idx, s = 0, 1
    while s < g:
        br = sqr_ref[idx:idx + 1, BLK * i:BLK * i + BLK]
        bi = sign * sqi_ref[idx:idx + 1, BLK * i:BLK * i + BLK]
        shift, keep = (g - s, row < g - s) if reverse else (s, row >= s)
        pr = jnp.where(keep, pltpu.roll(sr, shift, axis=0), 0.0)
        pi = jnp.where(keep, pltpu.roll(si, shift, axis=0), 0.0)
        sr, si = sr + br * pr - bi * pi, si + br * pi + bi * pr
        idx, s = idx + 1, 2 * s
    return sr + seq_r * cin_r - seq_i * cin_i, si + seq_r * cin_i + seq_i * cin_r


def _ssm_fwd_kb(u, wb_re, wb_im, wc_re, wc_im, d_skip, tabs, q):
    l_dim = u.shape[0]
    nc = l_dim // q
    g = q // GRP

    def body(u_ref, wbr_ref, wbi_ref, wcr_ref, wci_ref, d_ref, a1r_ref, a1i_ref, sqr_ref, sqi_ref,
             seqr_ref, seqi_ref, y_ref, z_ref, sr_ref, si_ref, cr_ref, ci_ref, xr_ref, xi_ref):
        @pl.when(pl.program_id(1) == 0)
        def _():
            cr_ref[...] = jnp.zeros_like(cr_ref)
            ci_ref[...] = jnp.zeros_like(ci_ref)

        u_kb = u_ref[...]
        ub = u_kb.astype(BF16)
        xr = jnp.dot(ub, wbr_ref[0], preferred_element_type=F32)
        xi = jnp.dot(ub, wbi_ref[0], preferred_element_type=F32)
        for i in range(LB_KB):
            xr_ref[i] = xr[:, BLK * i:BLK * i + BLK]
            xi_ref[i] = xi[:, BLK * i:BLK * i + BLK]
        row = lax.broadcasted_iota(jnp.int32, (g, BLK), 0)
        for i in range(LB_KB):
            lanes = slice(BLK * i, BLK * i + BLK)
            ar, ai = a1r_ref[0:1, lanes], a1i_ref[0:1, lanes]
            cin_r, cin_i = cr_ref[0:1, lanes], ci_ref[0:1, lanes]
            tr, ti = _chunk_carries(xr_ref, xi_ref, i, ar, ai, sqr_ref, sqi_ref, seqr_ref[:, lanes],
                                    seqi_ref[:, lanes], cin_r, cin_i, 1.0, reverse=False)
            cr_ref[0:1, lanes] = jnp.sum(jnp.where(row == g - 1, tr, 0.0), axis=0, keepdims=True)
            ci_ref[0:1, lanes] = jnp.sum(jnp.where(row == g - 1, ti, 0.0), axis=0, keepdims=True)
            pr = jnp.where(row == 0, cin_r, pltpu.roll(tr, 1, axis=0))
            pi = jnp.where(row == 0, cin_i, pltpu.roll(ti, 1, axis=0))
            for r in range(GRP):
                pr, pi = (_lane_slab(xr_ref, i, r, g) + ar * pr - ai * pi,
                          _lane_slab(xi_ref, i, r, g) + ar * pi + ai * pr)
                sr_ref[i, _strided(r, g), :] = pr
                si_ref[i, _strided(r, g), :] = pi
        s_r = jnp.concatenate([sr_ref[i] for i in range(LB_KB)], axis=1)
        s_i = jnp.concatenate([si_ref[i] for i in range(LB_KB)], axis=1)
        y = (jnp.dot(s_r.astype(BF16), wcr_ref[0], preferred_element_type=F32)
             - jnp.dot(s_i.astype(BF16), wci_ref[0], preferred_element_type=F32)
             + d_ref[...] * u_kb)
        y_ref[...] = y.astype(BF16)
        z_ref[...] = _gelu(y).astype(BF16)

    chan = pl.BlockSpec((q, BLK), lambda k, c: (c, k))
    wb_spec = pl.BlockSpec((1, BLK, ST_KB), lambda k, c: (k, 0, 0))
    wc_spec = pl.BlockSpec((1, ST_KB, BLK), lambda k, c: (k, 0, 0))
    tab_specs = [pl.BlockSpec((t.shape[0], ST_KB), lambda k, c: (0, k)) for t in tabs[:6]]
    state_spec = pl.BlockSpec((LB_KB, q, BLK), lambda k, c: (k, c, 0))
    state_shape = jax.ShapeDtypeStruct((N_LB, l_dim, BLK), F32)
    return pl.pallas_call(
        body, name="ssm_fwd", grid=(SSM_KB, nc),
        in_specs=[chan, wb_spec, wb_spec, wc_spec, wc_spec, pl.BlockSpec((1, BLK), lambda k, c: (0, k))] + tab_specs,
        out_specs=[chan, chan, state_spec, state_spec],
        out_shape=[jax.ShapeDtypeStruct((l_dim, D_MODEL), BF16), jax.ShapeDtypeStruct((l_dim, D_MODEL), BF16),
                   state_shape, state_shape],
        scratch_shapes=[pltpu.VMEM((8, ST_KB), F32), pltpu.VMEM((8, ST_KB), F32),
                        pltpu.VMEM((LB_KB, q, BLK), F32), pltpu.VMEM((LB_KB, q, BLK), F32)],
        compiler_params=_params(dimension_semantics=("parallel", "arbitrary")),
    )(u, wb_re, wb_im, wc_re, wc_im, d_skip, *tabs[:6])


def _ssm_bwd_kb(dz, y, u, s_re, s_im, wb_re, wb_im, wc_re, wc_im, d_skip, tabs, q, dproj):
    l_dim = u.shape[0]
    nc = l_dim // q
    g = q // GRP

    def body(dz_ref, y_ref, u_ref, sr_ref, si_ref, wbr_ref, wbi_ref, wcr_ref, wci_ref, d_ref,
             a1r_ref, a1i_ref, sqr_ref, sqi_ref, revr_ref, revi_ref, dproj_ref,
             du_ref, dd_ref, dar_ref, dai_ref, dwbr_ref, dwbi_ref, dwcr_ref, dwci_ref,
             cr_ref, ci_ref, gr_ref, gi_ref):
        @pl.when(pl.program_id(1) == 0)
        def _():
            for ref in (cr_ref, ci_ref, dd_ref, dar_ref, dai_ref, dwbr_ref, dwbi_ref, dwcr_ref, dwci_ref):
                ref[...] = jnp.zeros_like(ref)

        tn = (((0,), (0,)), ((), ()))
        nt = (((1,), (1,)), ((), ()))
        u_kb = u_ref[...]
        dy = dz_ref[...].astype(F32) * _gelu_grad(y_ref[...].astype(F32))
        dyb = dy.astype(BF16)
        ub = u_kb.astype(BF16)
        dd_ref[...] += jnp.sum(dy * u_kb, axis=0, keepdims=True)
        ds_r = lax.dot_general(dyb, wcr_ref[0], nt, preferred_element_type=F32)
        ds_i = -lax.dot_general(dyb, wci_ref[0], nt, preferred_element_type=F32)
        for i in range(LB_KB):
            gr_ref[i] = ds_r[:, BLK * i:BLK * i + BLK]
            gi_ref[i] = ds_i[:, BLK * i:BLK * i + BLK]
        row = lax.broadcasted_iota(jnp.int32, (g, BLK), 0)
        for i in range(LB_KB):
            lanes = slice(BLK * i, BLK * i + BLK)
            ar, ai = a1r_ref[0:1, lanes], -a1i_ref[0:1, lanes]
            cin_r, cin_i = cr_ref[0:1, lanes], ci_ref[0:1, lanes]
            tr, ti = _chunk_carries(gr_ref, gi_ref, i, ar, ai, sqr_ref, sqi_ref, revr_ref[:, lanes],
                                    -revi_ref[:, lanes], cin_r, cin_i, -1.0, reverse=True)
            cr_ref[0:1, lanes] = jnp.sum(jnp.where(row == 0, tr, 0.0), axis=0, keepdims=True)
            ci_ref[0:1, lanes] = jnp.sum(jnp.where(row == 0, ti, 0.0), axis=0, keepdims=True)
            nr = jnp.where(row == g - 1, cin_r, pltpu.roll(tr, g - 1, axis=0))
            ni = jnp.where(row == g - 1, cin_i, pltpu.roll(ti, g - 1, axis=0))
            acc_r = jnp.zeros((g, BLK), F32)
            acc_i = jnp.zeros((g, BLK), F32)
            for r in range(GRP - 1, -1, -1):
                s_r, s_i = _lane_slab(sr_ref, i, r, g), _lane_slab(si_ref, i, r, g)
                acc_r = acc_r + (nr * s_r + ni * s_i)
                acc_i = acc_i + (ni * s_r - nr * s_i)
                nr, ni = (_lane_slab(gr_ref, i, r, g) + ar * nr - ai * ni,
                          _lane_slab(gi_ref, i, r, g) + ar * ni + ai * nr)
                gr_ref[i, _strided(r, g), :] = nr
                gi_ref[i, _strided(r, g), :] = ni
            dar_ref[:, lanes] += jnp.sum(acc_r, axis=0, keepdims=True)
            dai_ref[:, lanes] += jnp.sum(acc_i, axis=0, keepdims=True)
        grb = jnp.concatenate([gr_ref[i] for i in range(LB_KB)], axis=1).astype(BF16)
        gib = jnp.concatenate([gi_ref[i] for i in range(LB_KB)], axis=1).astype(BF16)
        srb = jnp.concatenate([sr_ref[i] for i in range(LB_KB)], axis=1).astype(BF16)
        sib = jnp.concatenate([si_ref[i] for i in range(LB_KB)], axis=1).astype(BF16)
        du = (lax.dot_general(grb, wbr_ref[0], nt, preferred_element_type=F32)
              + lax.dot_general(gib, wbi_ref[0], nt, preferred_element_type=F32)
              + d_ref[...] * dy)
        du_ref[...] = du.astype(BF16)
        dwbr_ref[0] += lax.dot_general(ub, grb, tn, preferred_element_type=F32)
        dwbi_ref[0] += lax.dot_general(ub, gib, tn, preferred_element_type=F32)
        dwcr_ref[0] += lax.dot_general(srb, dyb, tn, preferred_element_type=F32)
        dwci_ref[0] -= lax.dot_general(sib, dyb, tn, preferred_element_type=F32)

    chan = pl.BlockSpec((q, BLK), lambda k, c: (nc - 1 - c, k))
    wb_spec = pl.BlockSpec((1, BLK, ST_KB), lambda k, c: (k, 0, 0))
    wc_spec = pl.BlockSpec((1, ST_KB, BLK), lambda k, c: (k, 0, 0))
    tab_in = [tabs[0], tabs[1], tabs[2], tabs[3], tabs[6], tabs[7]]
    tab_specs = [pl.BlockSpec((t.shape[0], ST_KB), lambda k, c: (0, k)) for t in tab_in]
    vec = pl.BlockSpec((1, BLK), lambda k, c: (0, k))
    svec = pl.BlockSpec((1, ST_KB), lambda k, c: (0, k))
    state_spec = pl.BlockSpec((LB_KB, q, BLK), lambda k, c: (k, nc - 1 - c, 0))
    du_spec = pl.BlockSpec((q, BLK), lambda k, c: (nc - 1 - c, 2 * D_MODEL // BLK + k))
    in_specs = ([chan, chan, chan, state_spec, state_spec, wb_spec, wb_spec, wc_spec, wc_spec, vec] + tab_specs
                + [pl.BlockSpec(memory_space=pl.ANY)])
    return pl.pallas_call(
        body, name="ssm_bwd", grid=(SSM_KB, nc),
        in_specs=in_specs,
        out_specs=[du_spec, vec, svec, svec, wb_spec, wb_spec, wc_spec, wc_spec],
        input_output_aliases={len(in_specs) - 1: 0},
        out_shape=[jax.ShapeDtypeStruct(dproj.shape, BF16), jax.ShapeDtypeStruct((1, D_MODEL), F32),
                   jax.ShapeDtypeStruct((1, N_STATE), F32), jax.ShapeDtypeStruct((1, N_STATE), F32),
                   jax.ShapeDtypeStruct((SSM_KB, BLK, ST_KB), F32), jax.ShapeDtypeStruct((SSM_KB, BLK, ST_KB), F32),
                   jax.ShapeDtypeStruct((SSM_KB, ST_KB, BLK), F32), jax.ShapeDtypeStruct((SSM_KB, ST_KB, BLK), F32)],
        scratch_shapes=[pltpu.VMEM((8, ST_KB), F32), pltpu.VMEM((8, ST_KB), F32),
                        pltpu.VMEM((LB_KB, q, BLK), F32), pltpu.VMEM((LB_KB, q, BLK), F32)],
        compiler_params=_params(dimension_semantics=("parallel", "arbitrary")),
    )(dz, y, u, s_re, s_im, wb_re, wb_im, wc_re, wc_im, d_skip, *tab_in, dproj)


def _discretize(lam_re, lam_im, log_dt, b_re, b_im):
    dt = jnp.exp(log_dt)[:, None]
    mag = jnp.exp(lam_re * dt)
    ar, ai = mag * jnp.cos(lam_im * dt), mag * jnp.sin(lam_im * dt)
    den = lam_re * lam_re + lam_im * lam_im
    nr, ni = ar - 1.0, ai
    fr, fi = (nr * lam_re + ni * lam_im) / den, (ni * lam_re - nr * lam_im) / den
    bbar_re = fr[..., None] * b_re - fi[..., None] * b_im
    bbar_im = fr[..., None] * b_im + fi[..., None] * b_re
    return ar, ai, bbar_re, bbar_im


def _block_diag_b(bbar):
    eye = jnp.eye(8, dtype=bbar.dtype)
    return jnp.einsum("kgpc,gh->kgchp", bbar.reshape(8, 8, SSM_STATE, SSM_GROUP_CH), eye).reshape(8, BLK, ST_KB)


def _block_diag_b_t(dwb):
    eye = jnp.eye(8, dtype=dwb.dtype)
    return jnp.einsum("kgchp,gh->kgpc", dwb.reshape(8, 8, SSM_GROUP_CH, 8, SSM_STATE), eye).reshape(
        SSM_GROUPS, SSM_STATE, SSM_GROUP_CH)


def _block_diag_c(c):
    eye = jnp.eye(8, dtype=c.dtype)
    return jnp.einsum("kgcp,gh->kgphc", c.reshape(8, 8, SSM_GROUP_CH, SSM_STATE), eye).reshape(8, ST_KB, BLK)


def _block_diag_c_t(dwc):
    eye = jnp.eye(8, dtype=dwc.dtype)
    return jnp.einsum("kgphc,gh->kgcp", dwc.reshape(8, 8, SSM_STATE, 8, SSM_GROUP_CH), eye).reshape(
        SSM_GROUPS, SSM_GROUP_CH, SSM_STATE)


def _powers(br, bi, n):
    pr, pi = br, bi
    cr, ci = br, bi
    while pr.shape[0] < n:
        pr, pi = (jnp.concatenate([pr, pr * cr - pi * ci], axis=0),
                  jnp.concatenate([pi, pr * ci + pi * cr], axis=0))
        cr, ci = cr * cr - ci * ci, 2.0 * cr * ci
    return pr[:n], pi[:n]


def _powers_desc(br, bi, n):
    pr, pi = br, bi
    cr, ci = br, bi
    while pr.shape[0] < n:
        pr, pi = (jnp.concatenate([pr * cr - pi * ci, pr], axis=0),
                  jnp.concatenate([pr * ci + pi * cr, pi], axis=0))
        cr, ci = cr * cr - ci * ci, 2.0 * cr * ci
    return pr, pi


def _power_tables(ar, ai, g):
    a1r, a1i = _powers(ar, ai, GRP)
    seqr, seqi = _powers(a1r[GRP - 1:], a1i[GRP - 1:], g)
    g2 = 1 << (g - 1).bit_length()
    revr, revi = _powers_desc(a1r[GRP - 1:], a1i[GRP - 1:], g2)
    revr, revi = revr[g2 - g:], revi[g2 - g:]
    sq_r, sq_i = [seqr[0:1]], [seqi[0:1]]
    while len(sq_r) < 8:
        r, i = sq_r[-1], sq_i[-1]
        sq_r.append(r * r - i * i)
        sq_i.append(2.0 * r * i)
    sqr, sqi = jnp.concatenate(sq_r, axis=0), jnp.concatenate(sq_i, axis=0)
    return a1r, a1i, sqr, sqi, seqr, seqi, revr, revi


HBM_SPEC = pl.BlockSpec(memory_space=pltpu.HBM)
SEM_SPEC = pl.BlockSpec(memory_space=pltpu.SEMAPHORE)
DATAFLOW = pltpu.SideEffectType.DATAFLOW_SIDE_EFFECTING


def _plain_rows(p, m):
    return p * m


def _ffn_in_rows(p, m):
    return ((p & 3) >> 1) * (4 * m) + (p >> 2) * (2 * m) + (p & 1) * m


def _peer_copies(src_refs, land_refs, send_sems, recv_sems, chunked, row_fns):
    x, y, c = lax.axis_index("x"), lax.axis_index("y"), lax.axis_index("c")
    me = 4 * x + 2 * y + c
    copies = []
    for a, (src, land) in enumerate(zip(src_refs, land_refs)):
        m = land.shape[0] // N_DEV
        for k in range(N_DEV - 1):
            rel = k + 1
            bx, by, bc = (rel >> 2) & 1, (rel >> 1) & 1, rel & 1
            peer = (x + bx - 2 * x * bx, y + by - 2 * y * by, c + bc - 2 * c * bc)
            p_idx = 4 * peer[0] + 2 * peer[1] + peer[2]
            copies.append(pltpu.make_async_remote_copy(
                src_ref=src.at[pl.ds(row_fns[a](p_idx, m), m), :] if chunked else src,
                dst_ref=land.at[pl.ds(me * m if chunked else row_fns[a](me, m), m), :],
                send_sem=send_sems[a * (N_DEV - 1) + k], recv_sem=recv_sems[a * (N_DEV - 1) + k],
                device_id=peer, device_id_type=MESH))
    return copies


def _own_copies(src_refs, land_outs, sems, chunked, row_fns):
    me = _my_index()
    copies = []
    for a, (src, land) in enumerate(zip(src_refs, land_outs)):
        m = land.shape[0] // N_DEV
        rows = row_fns[a](me, m)
        copies.append(pltpu.make_async_copy(src.at[pl.ds(rows, m), :] if chunked else src,
                                            land.at[pl.ds(me * m if chunked else rows, m), :], sems[a]))
    return copies


def _send_start(name, srcs, lands, chunked, row_fns=None):
    n = len(srcs)
    ns = n * (N_DEV - 1)
    row_fns = row_fns or [_plain_rows] * n

    def body(*refs):
        src_refs, land_refs = refs[:n], refs[n:2 * n]
        send_sems, own_sems = refs[2 * n:2 * n + ns], refs[2 * n + ns:2 * n + ns + n]
        recv_sems = refs[2 * n + ns + n:2 * n + 2 * ns + n]
        land_outs, token = refs[-1 - n:-1], refs[-1]
        for cp in _own_copies(src_refs, land_outs, own_sems, chunked, row_fns):
            cp.start()
        for cp in _peer_copies(src_refs, land_refs, send_sems, recv_sems, chunked, row_fns):
            cp.start()
        token[...] = jnp.zeros_like(token)

    ins = [pltpu.with_memory_space_constraint(a, pltpu.HBM) for a in list(srcs) + list(lands)]
    n_sems = 2 * ns + n
    out = pl.pallas_call(
        body, name=name,
        in_specs=[HBM_SPEC] * (2 * n),
        out_specs=[SEM_SPEC] * n_sems + [HBM_SPEC] * (2 * n) + [pl.BlockSpec(memory_space=pltpu.VMEM)],
        out_shape=[pltpu.SemaphoreType.DMA(())] * n_sems
        + [pltpu.HBM(a.shape, a.dtype) for a in list(srcs) + list(lands)]
        + [jax.ShapeDtypeStruct((8, BLK), F32)],
        input_output_aliases={i: i + n_sems for i in range(2 * n)},
        compiler_params=pltpu.CompilerParams(has_side_effects=DATAFLOW),
    )(*ins)
    return (out[:ns + n], out[ns + n:n_sems], out[n_sems:n_sems + n], out[n_sems + n:n_sems + 2 * n], out[-1])


def _send_wait(name, send_sems, recv_sems, srcs, lands, after, chunked, row_fns=None):
    n = len(srcs)
    ns = n * (N_DEV - 1)
    row_fns = row_fns or [_plain_rows] * n
    after = list(after) if isinstance(after, (list, tuple)) else [after]

    def body(*refs):
        src_refs, land_refs = refs[:n], refs[n:2 * n]
        s_sems, own_sems = refs[2 * n:2 * n + ns], refs[2 * n + ns:2 * n + ns + n]
        r_sems = refs[2 * n + ns + n:2 * n + 2 * ns + n]
        land_outs = refs[-n:]
        for cp in _own_copies(src_refs, land_outs, own_sems, chunked, row_fns):
            cp.wait()
        copies = _peer_copies(src_refs, land_refs, s_sems, r_sems, chunked, row_fns)
        for cp in copies:
            cp.wait_send()
        for cp in copies:
            cp.wait_recv()

    out = pl.pallas_call(
        body, name=name,
        in_specs=[HBM_SPEC] * (2 * n) + [SEM_SPEC] * (2 * ns + n) + [pl.BlockSpec(memory_space=pl.ANY)] * len(after),
        out_specs=[HBM_SPEC] * (2 * n),
        out_shape=[pltpu.HBM(a.shape, a.dtype) for a in list(srcs) + list(lands)],
        input_output_aliases={i: i for i in range(2 * n)},
        compiler_params=pltpu.CompilerParams(has_side_effects=DATAFLOW),
    )(*srcs, *lands, *send_sems, *recv_sems, *after)
    return out[n:]


def _sum_slots(name, recv, own, wmv=None):
    m, ncol = own.shape
    tr = m // 2 if (m // 2) % 16 == 0 else m
    g = m // tr
    c1 = 1.0 - ADAM_B1 ** ADAM_STEP
    c2 = 1.0 - ADAM_B2 ** ADAM_STEP

    def body(*refs):
        slots, own_ref, o_ref = refs[:N_DEV], refs[N_DEV], refs[N_DEV + 1 + (3 if wmv else 0)]
        me = _my_index()
        tot = None
        for s in range(N_DEV):
            v = jnp.where(me == s, own_ref[...], slots[s][...].astype(F32))
            tot = v if tot is None else tot + v
        o_ref[...] = tot
        if wmv:
            w_ref, m_ref, v_ref = refs[N_DEV + 1:N_DEV + 4]
            d_ref, nm_ref, nv_ref = refs[N_DEV + 5:N_DEV + 8]
            nm = ADAM_B1 * m_ref[...] + (1.0 - ADAM_B1) * tot
            nv = ADAM_B2 * v_ref[...] + (1.0 - ADAM_B2) * (tot * tot)
            d_ref[...] = -ADAM_LR * ((nm / c1) / (jnp.sqrt(nv / c2) + ADAM_EPS) + ADAM_WD * w_ref[...])
            nm_ref[...] = nm
            nv_ref[...] = nv

    def slot_spec(s):
        return pl.BlockSpec((tr, ncol), lambda i: (s * g + i, 0))

    tile = pl.BlockSpec((tr, ncol), lambda i: (i, 0))
    shape = jax.ShapeDtypeStruct((m, ncol), F32)
    n_extra = 3 if wmv else 0
    out = pl.pallas_call(
        body, name=name, grid=(g,),
        in_specs=[slot_spec(s) for s in range(N_DEV)] + [tile] * (1 + n_extra),
        out_specs=[tile] * (1 + n_extra),
        out_shape=[shape] * (1 + n_extra),
        compiler_params=_params(),
    )(*([recv] * N_DEV), own, *(wmv or ()))
    return out if wmv else out[0]


def _sum_gathered(name, gathered, rows):
    tr = _pick(rows, 512, 8)
    g = rows // tr

    def body(*refs):
        o_ref = refs[N_DEV]
        tot = refs[0][...]
        for s in range(1, N_DEV):
            tot = tot + refs[s][...]
        o_ref[...] = tot

    return pl.pallas_call(
        body, name=name, grid=(g,),
        in_specs=[pl.BlockSpec((tr, BLK), (lambda i, s=s: (s * g + i, 0))) for s in range(N_DEV)],
        out_specs=pl.BlockSpec((tr, BLK), lambda i: (i, 0)),
        out_shape=jax.ShapeDtypeStruct((rows, BLK), F32),
        compiler_params=_params(),
    )(*([gathered] * N_DEV))


def _adamw(name, w, g, m, v):
    r, c = w.shape
    tr = _pick(r, 256, 8) if r % 8 == 0 else r
    c1 = 1.0 - ADAM_B1 ** ADAM_STEP
    c2 = 1.0 - ADAM_B2 ** ADAM_STEP

    def body(w_ref, g_ref, m_ref, v_ref, d_ref, nm_ref, nv_ref):
        gv = g_ref[...]
        nm = ADAM_B1 * m_ref[...] + (1.0 - ADAM_B1) * gv
        nv = ADAM_B2 * v_ref[...] + (1.0 - ADAM_B2) * (gv * gv)
        m_hat = nm / c1
        v_hat = nv / c2
        d_ref[...] = -ADAM_LR * (m_hat / (jnp.sqrt(v_hat) + ADAM_EPS) + ADAM_WD * w_ref[...])
        nm_ref[...] = nm
        nv_ref[...] = nv

    spec = pl.BlockSpec((tr, c), lambda i: (i, 0))
    shape = jax.ShapeDtypeStruct((r, c), F32)
    return pl.pallas_call(
        body, name=name, grid=(r // tr,),
        in_specs=[spec] * 4, out_specs=[spec] * 3, out_shape=[shape] * 3,
        compiler_params=_params(),
    )(w, g, m, v)


def _adamw_many(name, ws, gs, ms, vs):
    n = len(ws)
    c1 = 1.0 - ADAM_B1 ** ADAM_STEP
    c2 = 1.0 - ADAM_B2 ** ADAM_STEP

    def body(*refs):
        for a in range(n):
            w_ref, g_ref, m_ref, v_ref = refs[a], refs[n + a], refs[2 * n + a], refs[3 * n + a]
            d_ref, nm_ref, nv_ref = refs[4 * n + a], refs[5 * n + a], refs[6 * n + a]
            gv = g_ref[...]
            nm = ADAM_B1 * m_ref[...] + (1.0 - ADAM_B1) * gv
            nv = ADAM_B2 * v_ref[...] + (1.0 - ADAM_B2) * (gv * gv)
            d_ref[...] = -ADAM_LR * ((nm / c1) / (jnp.sqrt(nv / c2) + ADAM_EPS) + ADAM_WD * w_ref[...])
            nm_ref[...] = nm
            nv_ref[...] = nv

    specs = [_full_spec(w.shape) for w in ws]
    shapes = [jax.ShapeDtypeStruct(w.shape, F32) for w in ws]
    out = pl.pallas_call(
        body, name=name, grid=(1,),
        in_specs=specs * 4, out_specs=specs * 3, out_shape=shapes * 3,
        compiler_params=_params(),
    )(*ws, *gs, *ms, *vs)
    return out[:n], out[n:2 * n], out[2 * n:]


PACK_ROWS = 128


def _pack(parts):
    flat = []
    for p in parts:
        v = p.reshape(-1)
        flat.append(jnp.pad(v, (0, (-v.shape[0]) % BLK)))
    v = jnp.concatenate(flat)
    v = jnp.pad(v, (0, (-v.shape[0]) % (PACK_ROWS * BLK)))
    return v.reshape(-1, BLK)


def _unpack(buf, shapes):
    flat = buf.reshape(-1)
    out, off = [], 0
    for shp in shapes:
        size = math.prod(shp)
        out.append(flat[off:off + size].reshape(shp))
        off += size + (-size) % BLK
    return out


def kernel(x, meta_tokens, norm_mix, w_in, q_norm, k_norm, attn_sinks, lam_re, lam_im, log_dt, ssm_b_re, ssm_b_im, ssm_c_re, ssm_c_im, ssm_d, w_glu, attn_branch_norm, ssm_branch_norm, w_out, norm_ffn, w_ffn_in, w_ffn_out, loss_target, m_meta_tokens, m_norm_mix, m_w_in, m_q_norm, m_k_norm, m_attn_sinks, m_lam_re, m_lam_im, m_log_dt, m_ssm_b_re, m_ssm_b_im, m_ssm_c_re, m_ssm_c_im, m_ssm_d, m_w_glu, m_attn_branch_norm, m_ssm_branch_norm, m_w_out, m_norm_ffn, m_w_ffn_in, m_w_ffn_out, v_meta_tokens, v_norm_mix, v_w_in, v_q_norm, v_k_norm, v_attn_sinks, v_lam_re, v_lam_im, v_log_dt, v_ssm_b_re, v_ssm_b_im, v_ssm_c_re, v_ssm_c_im, v_ssm_d, v_w_glu, v_attn_branch_norm, v_ssm_branch_norm, v_w_out, v_norm_ffn, v_w_ffn_in, v_w_ffn_out):
    args = dict(locals())
    weights = {n: args[n] for n in WEIGHTS}
    mom_m = {n: args["m_" + n] for n in WEIGHTS}
    mom_v = {n: args["v_" + n] for n in WEIGHTS}

    x2d = x[0]
    target2d = loss_target[0]
    s_len = x2d.shape[0]
    l_dim = s_len + BLK
    tm_row = _pick(l_dim, 320)
    tm_mm = _pick(l_dim, 1040)
    tl_tn = _pick(l_dim, 2080)
    tm_ffn = _pick(l_dim, 640)
    tm_big = _pick(l_dim, 2080)
    tm_shift = _pick(l_dim, 640, BLK)

    shard_in = w_in[0].T.astype(BF16)
    shard_glu = w_glu[0].T.astype(BF16)
    shard_out = w_out[0].astype(BF16)
    shard_ffn_in = w_ffn_in[0].T.astype(BF16)
    shard_ffn_out = w_ffn_out[0].astype(BF16)
    shard_meta = meta_tokens.T
    me = _my_index()

    def landing(shard):
        return lax.empty((N_DEV * shard.shape[0], shard.shape[1]), shard.dtype)

    first = [shard_in, shard_meta]
    ga = _send_start("gather_start_a", first, [landing(s) for s in first], chunked=False)
    later = [shard_glu + ga[4][0:1, 0:1].astype(BF16), shard_out, shard_ffn_in, shard_ffn_out]
    later_fns = [_plain_rows, _plain_rows, _ffn_in_rows, _plain_rows]
    gb = _send_start("gather_start_b", later, [landing(s) for s in later], chunked=False,
                     row_fns=later_fns)

    nm_t = norm_mix + (ga[4][0:1, 0:1] + gb[4][0:1, 0:1])
    qn_t, kn_t = jnp.tile(q_norm, (1, N_Q_HEADS)), jnp.tile(k_norm, (1, N_KV_HEADS))
    e_mat = jnp.kron(jnp.eye(4, dtype=F32), jnp.ones((HEAD_DIM, HEAD_DIM), F32)).astype(BF16)

    def disc(lr, li, ldt, br, bi):
        return _discretize(lr[0], li[0], ldt[0], br[0], bi[0])

    (abar_re, abar_im, bbar_re, bbar_im), disc_vjp = jax.vjp(disc, lam_re, lam_im, log_dt, ssm_b_re, ssm_b_im)
    wb_re, wb_im = _block_diag_b(bbar_re).astype(BF16), _block_diag_b(bbar_im).astype(BF16)
    wc_re, wc_im = _block_diag_c(ssm_c_re[0]).astype(BF16), _block_diag_c(ssm_c_im[0]).astype(BF16)
    q_ssm = _pick(l_dim, 640, 64)
    tabs = _power_tables(abar_re.reshape(1, N_STATE), abar_im.reshape(1, N_STATE), q_ssm // GRP)

    h0, xn = _embed_norm(x2d, nm_t, tm_shift)
    wt_in, meta_t = _send_wait("gather_wait_a", ga[0], ga[1], ga[2], ga[3],
                               [xn, wb_re, wb_im, wc_re, wc_im, *tabs, e_mat, qn_t, kn_t], chunked=False)
    meta_pad = jnp.pad(meta_t.T, ((PAD, 0), (0, 0)))
    h0, xn = _embed_meta(meta_pad, nm_t, h0, xn)
    qkv_w, u_end = Q_W + 2 * KV_W, Q_W + 2 * KV_W + D_MODEL
    wt_in_p = jnp.concatenate([wt_in[u_end:], wt_in[qkv_w:u_end], wt_in[:qkv_w]], axis=0)
    qkv = _matmul("proj_qkv", xn, wt_in_p, nt=True, tm=tm_big, tn=qkv_w, tk=D_MODEL, n=qkv_w, w_off=2)
    u = _matmul("proj_u", xn, wt_in_p, nt=True, tm=tm_big, tn=D_MODEL, tk=D_MODEL, n=D_MODEL, w_off=2)
    gates = _matmul("proj_gates", xn, wt_in_p, nt=True, tm=tm_big, tn=D_MODEL, tk=D_MODEL, n=2 * D_MODEL, w_off=0,
                    out_dtype=BF16)
    qn, kf, vf = _qk_prep(qkv, qn_t, kn_t, e_mat, tm_row)
    attn, lse = _attn_fwd(qn, kf, vf, attn_sinks)
    y, z, s_re, s_im = _ssm_fwd_kb(u, wb_re, wb_im, wc_re, wc_im, ssm_d, tabs, q_ssm)
    wt_glu, w_out_f, wt_ffn_in, w_ffn_out_f = _send_wait("gather_wait_b", gb[0], gb[1], gb[2], gb[3], z,
                                                         chunked=False, row_fns=later_fns)
    zab = _matmul("glu_proj", z, wt_glu, nt=True, tm=tm_big, tn=1024, tk=D_MODEL, out_dtype=BF16)
    merged = _merge_fwd(attn, zab, gates, attn_branch_norm, ssm_branch_norm, tm_row)
    h1, hn = _matmul("out_proj", merged, w_out_f, nt=False, tm=tm_mm, tn=1024, tk=D_MODEL, res=h0,
                     norm_g=norm_ffn)
    gu, act = _ffn_in_swiglu(hn, wt_ffn_in, tm_ffn)
    h2 = _matmul("ffn_out", act, w_ffn_out_f, nt=False, tm=tm_mm, tn=1024, tk=D_FF, res=h1)
    dh2, dh2_b, loss_part = _loss_grad(h2, target2d, tm_shift)

    dgu = _d_act_swiglu(dh2_b, w_ffn_out_f, gu, tm_ffn)

    def exchange_start(name, grads_b, row_fns=None):
        return _send_start(name, grads_b, [lax.empty(g.shape, BF16) for g in grads_b], chunked=True,
                           row_fns=row_fns)

    g_ffn_out, g_ffn_out_b = _matmul_tn("g_ffn_out", act, dh2_b, tm=1408, tn=1024, tl=tl_tn)
    g_ffn_in_t, g_ffn_in_b = _matmul_tn("g_ffn_in", dgu, hn, tm=1408, tn=1024, tl=tl_tn)
    ffn_fns = [_ffn_in_rows, _plain_rows]
    ex1 = exchange_start("exchange_start_ffn", [g_ffn_in_b, g_ffn_out_b], ffn_fns)
    dhn = _matmul("d_hn", dgu, wt_ffn_in, nt=False, tm=tm_ffn, tn=1024, tk=2 * D_FF, out_dtype=BF16)
    dh1, dh1_b, g_norm_ffn = _norm_bwd_res("ffn_norm_bwd", h1, norm_ffn + ex1[4][0:1, 0:1], dhn, dh2, tm_row)
    dmerged = _matmul("d_merged", dh1_b, w_out_f, nt=True, tm=tm_big, tn=1024, tk=D_MODEL, out_dtype=BF16)
    dattn, dzab, dproj, g_abn, g_sbn = _merge_bwd(attn, zab, gates, attn_branch_norm, ssm_branch_norm,
                                                    dmerged, tm_row)
    g_out, g_out_b = _matmul_tn("g_out", merged, dh1_b, tm=1024, tn=1024, tl=tl_tn)
    dz = _matmul("d_z", dzab, wt_glu, nt=False, tm=tm_mm, tn=1024, tk=2 * D_MODEL, out_dtype=BF16)
    g_glu_t, g_glu_b = _matmul_tn("g_glu", dzab, z, tm=1024, tn=1024, tl=tl_tn)
    ex2 = exchange_start("exchange_start_mix", [g_glu_b, g_out_b])
    dproj, g_ssm_d, g_ar, g_ai, g_wbr, g_wbi, g_wcr, g_wci = _ssm_bwd_kb(
        dz, y, u, s_re, s_im, wb_re, wb_im, wc_re, wc_im, ssm_d + ex2[4][0:1, 0:1], tabs, q_ssm, dproj)
    dq, dkc, dkp, dkm, dvc, dvp, dvm, g_sinks = _attn_bwd(qn, kf, vf, attn_sinks, lse, attn, dattn)
    dproj, g_qn_t, g_kn_t = _qk_bwd(qkv, qn_t, kn_t, e_mat, dq, dkc, dkp, dkm, dvc, dvp, dvm, dproj)
    g_lam_re, g_lam_im, g_log_dt, g_b_re, g_b_im = disc_vjp(
        (g_ar.reshape(SSM_GROUPS, SSM_STATE), g_ai.reshape(SSM_GROUPS, SSM_STATE),
         _block_diag_b_t(g_wbr), _block_diag_b_t(g_wbi)))
    small_grads = {
        "q_norm": g_qn_t.reshape(N_Q_HEADS, HEAD_DIM).sum(0)[None],
        "k_norm": g_kn_t.reshape(N_KV_HEADS, HEAD_DIM).sum(0)[None], "attn_sinks": g_sinks,
        "lam_re": g_lam_re, "lam_im": g_lam_im, "log_dt": g_log_dt, "ssm_b_re": g_b_re, "ssm_b_im": g_b_im,
        "ssm_c_re": _block_diag_c_t(g_wcr)[None], "ssm_c_im": _block_diag_c_t(g_wci)[None],
        "ssm_d": g_ssm_d, "attn_branch_norm": g_abn, "ssm_branch_norm": g_sbn, "norm_ffn": g_norm_ffn,
    }
    early = [n for n in SMALL if n != "norm_mix"]
    packed_e = _pack([small_grads[n] for n in early])
    gs_e = _send_start("small_start_a", [packed_e], [landing(packed_e)], chunked=False)

    blocks_g, blocks_u, blocks_q = 2 * D_MODEL // 512, D_MODEL // 512, qkv_w // 512
    back = lambda i: jnp.where(i < blocks_g, i + blocks_q + blocks_u,
                               jnp.where(i < blocks_g + blocks_u, i - blocks_g + blocks_q, i - blocks_g - blocks_u))
    g_in_t, g_in_b = _matmul_tn("g_in", dproj, xn, tm=512, tn=1024, tl=tl_tn, after=gs_e[4], out_rows=back)
    ex3 = exchange_start("exchange_start_in", [g_in_b])
    dxn = _matmul("d_xn", dproj, wt_in_p, nt=False, tm=tm_ffn, tn=1024, tk=IN_COLS, out_dtype=BF16)
    grad_x2d, dmeta_blk, g_norm_mix = _final_bwd(h0, nm_t + ex3[4][0:1, 0:1], dxn, dh1, _pick(s_len, 512, BLK))
    packed_l = _pack([g_norm_mix, dmeta_blk[PAD:], loss_part])
    gs_l = _send_start("small_start_b", [packed_l], [landing(packed_l)], chunked=False)
    grads, deltas, new_m, new_v = {}, {}, {}, {}

    recv_ffn_in, recv_ffn_out = _send_wait("exchange_wait_ffn", ex1[0], ex1[1], ex1[2], ex1[3], gs_l[4],
                                           chunked=True, row_fns=ffn_fns)
    recv_glu, recv_out = _send_wait("exchange_wait_mix", ex2[0], ex2[1], ex2[2], ex2[3], recv_ffn_in,
                                    chunked=True)
    (recv_in,) = _send_wait("exchange_wait_in", ex3[0], ex3[1], ex3[2], ex3[3], recv_glu, chunked=True)
    big = [("w_in", g_in_t, True, recv_in, _plain_rows), ("w_glu", g_glu_t, True, recv_glu, _plain_rows),
           ("w_out", g_out, False, recv_out, _plain_rows), ("w_ffn_in", g_ffn_in_t, True, recv_ffn_in, _ffn_in_rows),
           ("w_ffn_out", g_ffn_out, False, recv_ffn_out, _plain_rows)]
    for name, g_full, transposed, recv, row_fn in big:
        m_rows = g_full.shape[0] // N_DEV
        own = lax.dynamic_slice(g_full, (row_fn(me, m_rows), 0), (m_rows, g_full.shape[1]))
        if name == "w_glu":
            grads[name] = _sum_slots("sum_" + name, recv, own).T[None]
            continue
        shp = weights[name].shape
        if transposed:
            as2d, back = (lambda a: a.reshape(shp[-2], shp[-1]).T), (lambda a: a.T.reshape(shp))
        else:
            as2d, back = (lambda a: a.reshape(shp[-2], shp[-1])), (lambda a: a.reshape(shp))
        g_shard, d, nm, nv = _sum_slots("sum_adamw_" + name, recv, own,
                                        (as2d(weights[name]), as2d(mom_m[name]), as2d(mom_v[name])))
        grads[name], deltas[name], new_m[name], new_v[name] = back(g_shard), back(d), back(nm), back(nv)
        last = d

    def adamw_2d(name):
        shp = weights[name].shape
        as2d = lambda a: a.reshape(shp[-2], shp[-1])
        d, nm, nv = _adamw("adamw_" + name, as2d(weights[name]), as2d(grads[name]), as2d(mom_m[name]),
                           as2d(mom_v[name]))
        deltas[name], new_m[name], new_v[name] = d.reshape(shp), nm.reshape(shp), nv.reshape(shp)

    adamw_2d("w_glu")

    def small_sum(tag, gs, packed, after):
        (gathered,) = _send_wait("small_wait_" + tag, gs[0], gs[1], gs[2], gs[3], after, chunked=False)
        return _sum_gathered("sum_small_" + tag, gathered, packed.shape[0])

    def small_adamw(tag, names):
        view = lambda n, a: jnp.swapaxes(a, -1, -2) if n.startswith("ssm_b") else a
        d, nm, nv = _adamw_many("adamw_small_" + tag, [view(n, weights[n]) for n in names],
                                [view(n, grads[n]) for n in names], [view(n, mom_m[n]) for n in names],
                                [view(n, mom_v[n]) for n in names])
        deltas.update((n, view(n, a)) for n, a in zip(names, d))
        new_m.update((n, view(n, a)) for n, a in zip(names, nm))
        new_v.update((n, view(n, a)) for n, a in zip(names, nv))

    g_sum_e = small_sum("a", gs_e, packed_e, last)
    grads.update(zip(early, _unpack(g_sum_e, [weights[n].shape for n in early])))
    wide = [n for n in early if n.startswith(("ssm_b", "ssm_c"))]
    small_adamw("wide", wide)
    g_sum_l = small_sum("b", gs_l, packed_l, g_sum_e)
    grads["norm_mix"], g_meta, loss_sum = _unpack(g_sum_l, [weights["norm_mix"].shape, (N_META, D_MODEL), (1, 1)])
    small_adamw("rest", [n for n in SMALL if n not in wide])
    grads["meta_tokens"] = lax.dynamic_slice(g_meta, (0, me * BLK), (N_META, BLK))
    adamw_2d("meta_tokens")

    loss = loss_sum[0, 0]
    return (loss, grad_x2d[None], *[grads[n] for n in WEIGHTS], *[deltas[n] for n in WEIGHTS],
            *[new_m[n] for n in WEIGHTS], *[new_v[n] for n in WEIGHTS])
```

```python
import math

import numpy as np
import jax
import jax.numpy as jnp
from jax import lax
from jax.experimental import pallas as pl
from jax.experimental.pallas import tpu as pltpu

F32 = jnp.float32
BF16 = jnp.bfloat16

D_MODEL = 1024
N_META = 16
HEAD_DIM = 64
N_Q_HEADS = 16
N_KV_HEADS = 4
Q_W = N_Q_HEADS * HEAD_DIM
KV_W = N_KV_HEADS * HEAD_DIM
SSM_GROUPS = 64
SSM_GROUP_CH = 16
SSM_STATE = 64
N_STATE = SSM_GROUPS * SSM_STATE
D_FF = 2816
IN_COLS = Q_W + 2 * KV_W + 3 * D_MODEL
EPS = 1e-6
BLK = 128
PAD = BLK - N_META
N_DEV = 8
NEG = -1e30
SSM_KB = 8
ST_KB = N_STATE // SSM_KB
LB_KB = ST_KB // BLK
N_LB = N_STATE // BLK

ADAM_LR = 0.001
ADAM_B1 = 0.9
ADAM_B2 = 0.999
ADAM_EPS = 1e-08
ADAM_WD = 0.01
ADAM_STEP = 10

VMEM_LIMIT = 48 * 1024 * 1024
MESH = pl.DeviceIdType.MESH

SMALL = ["norm_mix", "q_norm", "k_norm", "attn_sinks", "lam_re", "lam_im", "log_dt", "ssm_b_re", "ssm_b_im",
         "ssm_c_re", "ssm_c_im", "ssm_d", "attn_branch_norm", "ssm_branch_norm", "norm_ffn"]
WEIGHTS = ["meta_tokens", "norm_mix", "w_in", "q_norm", "k_norm", "attn_sinks", "lam_re", "lam_im", "log_dt",
           "ssm_b_re", "ssm_b_im", "ssm_c_re", "ssm_c_im", "ssm_d", "w_glu", "attn_branch_norm",
           "ssm_branch_norm", "w_out", "norm_ffn", "w_ffn_in", "w_ffn_out"]


def _params(**kw):
    return pltpu.CompilerParams(vmem_limit_bytes=VMEM_LIMIT, **kw)


def _pick(n, cap, mult=16):
    best = None
    for d in range(mult, min(n, cap) + 1, mult):
        if n % d == 0:
            best = d
    assert best is not None, (n, cap, mult)
    return best


def _my_index():
    return 4 * lax.axis_index("x") + 2 * lax.axis_index("y") + lax.axis_index("c")


def _rms(x, g):
    r = lax.rsqrt(jnp.mean(x * x, axis=-1, keepdims=True) + EPS)
    return x * r * g


def _rms_bwd(x, g, dy):
    r = lax.rsqrt(jnp.mean(x * x, axis=-1, keepdims=True) + EPS)
    t = dy * g
    dx = r * t - x * (r * r * r) * jnp.mean(t * x, axis=-1, keepdims=True)
    dg = jnp.sum(dy * (x * r), axis=0, keepdims=True)
    return dx, dg


def _sigmoid(x):
    return jax.nn.sigmoid(x)


def _gelu(x):
    k = math.sqrt(2.0 / math.pi)
    return 0.5 * x * (1.0 + jnp.tanh(k * (x + 0.044715 * (x * x * x))))


def _gelu_grad(x):
    k = math.sqrt(2.0 / math.pi)
    t = jnp.tanh(k * (x + 0.044715 * (x * x * x)))
    return 0.5 * (1.0 + t) + 0.5 * x * (1.0 - t * t) * (k * (1.0 + 3.0 * 0.044715 * (x * x)))


def _head_mean(x, e_ref):
    hi = x.astype(BF16)
    lo = (x - hi.astype(F32)).astype(BF16)
    e = e_ref[...]
    out = []
    for b in range(x.shape[1] // 256):
        sl = slice(256 * b, 256 * b + 256)
        s = (jnp.dot(hi[:, sl], e, preferred_element_type=F32)
             + jnp.dot(lo[:, sl], e, preferred_element_type=F32))
        out.append(s)
    s = out[0] if len(out) == 1 else jnp.concatenate(out, axis=1)
    return s * (1.0 / HEAD_DIM)


def _head_rms(x, g, e_ref):
    r = lax.rsqrt(_head_mean(x * x, e_ref) + EPS)
    return x * r * g


def _head_rms_bwd(x, g, dy, e_ref):
    r = lax.rsqrt(_head_mean(x * x, e_ref) + EPS)
    t = dy * g
    dx = r * t - x * (r * r * r) * _head_mean(t * x, e_ref)
    dg = jnp.sum(dy * (x * r), axis=0, keepdims=True)
    return dx, dg


def _lane_half(shape):
    lane = lax.broadcasted_iota(jnp.int32, shape, len(shape) - 1)
    return (lane >> 6) & 1


def _matmul(name, a, w, *, nt, tm, tn, tk, n=None, w_off=0, res=None, norm_g=None, out_dtype=F32):
    m_dim, k_dim = a.shape
    n_dim = n if n is not None else (w.shape[0] if nt else w.shape[1])
    gm, gn, gk = m_dim // tm, n_dim // tn, k_dim // tk
    assert gm * tm == m_dim and gn * tn == n_dim and gk * tk == k_dim, (name, a.shape, w.shape, tm, tn, tk)
    assert norm_g is None or tn == n_dim
    direct = out_dtype == F32
    dn = (((1,), (1,)), ((), ())) if nt else (((1,), (0,)), ((), ()))

    def body(*refs):
        refs = list(refs)
        a_ref, w_ref = refs[0], refs[1]
        pos = 2
        r_ref = g_ref = on_ref = None
        if res is not None:
            r_ref, pos = refs[pos], pos + 1
        if norm_g is not None:
            g_ref, pos = refs[pos], pos + 1
        o_ref, pos = refs[pos], pos + 1
        if norm_g is not None:
            on_ref, pos = refs[pos], pos + 1
        part = lax.dot_general(a_ref[...], w_ref[...], dn, preferred_element_type=F32)
        if gk == 1:
            r = part if r_ref is None else r_ref[...] + part
            o_ref[...] = r.astype(out_dtype)
            if on_ref is not None:
                on_ref[...] = _rms(r, g_ref[...]).astype(BF16)
            return
        acc = o_ref if direct else refs[pos]
        k = pl.program_id(2)

        @pl.when(k == 0)
        def _():
            acc[...] = part if r_ref is None or not direct else r_ref[...] + part

        @pl.when(k > 0)
        def _():
            acc[...] += part

        @pl.when(k == gk - 1)
        def _():
            if not direct:
                r = acc[...]
                if r_ref is not None:
                    r = r_ref[...] + r
                o_ref[...] = r.astype(out_dtype)
            if on_ref is not None:
                on_ref[...] = _rms(o_ref[...].astype(F32), g_ref[...]).astype(BF16)

    if nt:
        w_spec = pl.BlockSpec((tn, tk), lambda i, j, k: (j + w_off, k))
    else:
        w_spec = pl.BlockSpec((tk, tn), lambda i, j, k: (k, j))
    in_specs = [pl.BlockSpec((tm, tk), lambda i, j, k: (i, k)), w_spec]
    args = [a, w]
    out_spec = pl.BlockSpec((tm, tn), lambda i, j, k: (i, j))
    out_specs, out_shape = [out_spec], [jax.ShapeDtypeStruct((m_dim, n_dim), out_dtype)]
    if res is not None:
        in_specs.append(out_spec)
        args.append(res)
    if norm_g is not None:
        in_specs.append(pl.BlockSpec((1, tn), lambda i, j, k: (0, 0)))
        args.append(norm_g)
        out_specs.append(out_spec)
        out_shape.append(jax.ShapeDtypeStruct((m_dim, n_dim), BF16))
    out = pl.pallas_call(
        body, name=name, grid=(gm, gn, gk),
        in_specs=in_specs, out_specs=out_specs, out_shape=out_shape,
        scratch_shapes=[] if direct or gk == 1 else [pltpu.VMEM((tm, tn), F32)],
        compiler_params=_params(dimension_semantics=("parallel", "parallel", "arbitrary")),
    )(*args)
    return out if norm_g is not None else out[0]


def _matmul_tn(name, a, b, *, tm, tn, tl, after=None, out_rows=None):
    l_dim, m_dim = a.shape
    n_dim = b.shape[1]
    gm, gn, gl = m_dim // tm, n_dim // tn, l_dim // tl
    assert gm * tm == m_dim and gn * tn == n_dim and gl * tl == l_dim, (name, a.shape, b.shape, tm, tn, tl)

    def body(*refs):
        a_ref, b_ref = refs[0], refs[1]
        o_ref, ob_ref = refs[-2], refs[-1]

        @pl.when(pl.program_id(2) == 0)
        def _():
            o_ref[...] = jnp.zeros_like(o_ref)

        o_ref[...] += lax.dot_general(a_ref[...], b_ref[...], (((0,), (0,)), ((), ())),
                                      preferred_element_type=F32)

        @pl.when(pl.program_id(2) == gl - 1)
        def _():
            ob_ref[...] = o_ref[...].astype(BF16)

    out_row = out_rows if out_rows is not None else (lambda i: i)
    out_spec = pl.BlockSpec((tm, tn), lambda i, j, l: (out_row(i), j))
    in_specs = [pl.BlockSpec((tl, tm), lambda i, j, l: (l, i)), pl.BlockSpec((tl, tn), lambda i, j, l: (l, j))]
    args = [a, b]
    if after is not None:
        in_specs.append(pl.BlockSpec(memory_space=pl.ANY))
        args.append(after)
    return pl.pallas_call(
        body, name=name, grid=(gm, gn, gl),
        in_specs=in_specs,
        out_specs=[out_spec, out_spec],
        out_shape=[jax.ShapeDtypeStruct((m_dim, n_dim), F32), jax.ShapeDtypeStruct((m_dim, n_dim), BF16)],
        compiler_params=_params(dimension_semantics=("parallel", "parallel", "arbitrary")),
    )(*args)


def _row_spec(tm, cols, f=None):
    if f is None:
        return pl.BlockSpec((tm, cols), lambda i: (i, 0))
    return pl.BlockSpec((tm, cols), lambda i: (f(i), 0))


def _full_spec(shape):
    nd = len(shape)
    return pl.BlockSpec(shape, lambda i: (0,) * nd)


def _shifted_specs(n_sub, n_blocks):
    return [_row_spec(BLK, D_MODEL, (lambda i, k=k: jnp.clip(n_sub * i - 1 + k, 0, n_blocks - 1)))
            for k in range(n_sub)]


def _embed_norm(x2d, g, tm):
    s_len = x2d.shape[0]
    l_dim = s_len + BLK
    n_sub = tm // BLK

    def body(*refs):
        x_refs, g_ref, h_ref, xn_ref = refs[:n_sub], refs[n_sub], refs[n_sub + 1], refs[n_sub + 2]
        i = pl.program_id(0)
        for k in range(n_sub):
            rows = slice(BLK * k, BLK * k + BLK)
            h = x_refs[k][...] * jnp.where(n_sub * i + k >= 1, 1.0, 0.0)
            h_ref[rows, :] = h
            xn_ref[rows, :] = _rms(h, g_ref[...]).astype(BF16)

    return pl.pallas_call(
        body, name="embed_norm", grid=(l_dim // tm,),
        in_specs=_shifted_specs(n_sub, s_len // BLK) + [_full_spec((1, D_MODEL))],
        out_specs=[_row_spec(tm, D_MODEL), _row_spec(tm, D_MODEL)],
        out_shape=[jax.ShapeDtypeStruct((l_dim, D_MODEL), F32),
                   jax.ShapeDtypeStruct((l_dim, D_MODEL), BF16)],
        compiler_params=_params(),
    )(*([x2d] * n_sub), g)


def _embed_meta(meta_pad, g, h0, xn):
    def body(mp_ref, g_ref, h_in, xn_in, h_ref, xn_ref):
        h_ref[...] = mp_ref[...]
        xn_ref[...] = _rms(mp_ref[...], g_ref[...]).astype(BF16)

    any_spec = pl.BlockSpec(memory_space=pl.ANY)
    return pl.pallas_call(
        body, name="embed_meta", grid=(1,),
        in_specs=[_full_spec((BLK, D_MODEL)), _full_spec((1, D_MODEL)), any_spec, any_spec],
        out_specs=[_row_spec(BLK, D_MODEL), _row_spec(BLK, D_MODEL)],
        out_shape=[jax.ShapeDtypeStruct(h0.shape, F32), jax.ShapeDtypeStruct(xn.shape, BF16)],
        input_output_aliases={2: 0, 3: 1},
        compiler_params=_params(),
    )(meta_pad, g, h0, xn)


KVX_W = 2 * N_KV_HEADS * BLK


def _qk_prep(qkv, q_norm_t, k_norm_t, e_mat, tm):
    l_dim = qkv.shape[0]

    def body(x_ref, qg_ref, kg_ref, e_ref, q_ref, kf_ref, vf_ref):
        x = x_ref[...]
        q = _head_rms(x[:, :Q_W], qg_ref[...], e_ref) * (HEAD_DIM ** -0.5)
        q_ref[...] = q.astype(BF16)
        k = _head_rms(x[:, Q_W:Q_W + KV_W], kg_ref[...], e_ref)
        v = x[:, Q_W + KV_W:Q_W + 2 * KV_W]
        half = _lane_half((tm, BLK))
        for src, dst in ((k, kf_ref), (v, vf_ref)):
            for kv in range(N_KV_HEADS):
                blk = src[:, BLK * (kv // 2):BLK * (kv // 2) + BLK]
                swapped = pltpu.roll(blk, HEAD_DIM, axis=1)
                for e in range(2):
                    val = blk if kv % 2 == e else swapped
                    idx = 2 * kv + e
                    dst[:, BLK * idx:BLK * idx + BLK] = jnp.where(half == e, val, 0.0).astype(BF16)

    return pl.pallas_call(
        body, name="qk_prep", grid=(l_dim // tm,),
        in_specs=[_row_spec(tm, Q_W + 2 * KV_W), _full_spec((1, Q_W)), _full_spec((1, KV_W)),
                  _full_spec((256, 256))],
        out_specs=[_row_spec(tm, Q_W), _row_spec(tm, KVX_W), _row_spec(tm, KVX_W)],
        out_shape=[jax.ShapeDtypeStruct((l_dim, Q_W), BF16),
                   jax.ShapeDtypeStruct((l_dim, KVX_W), BF16),
                   jax.ShapeDtypeStruct((l_dim, KVX_W), BF16)],
        compiler_params=_params(),
    )(qkv, q_norm_t, k_norm_t, e_mat)


def _merge_fwd(attn, zab, gates, abn, sbn, tm):
    l_dim = attn.shape[0]

    def body(a_ref, z_ref, g_ref, an_ref, sn_ref, o_ref):
        z = z_ref[...].astype(F32)
        g = g_ref[...].astype(F32)
        ssm = z[:, :D_MODEL] * _sigmoid(z[:, D_MODEL:])
        merged = (_sigmoid(g[:, :D_MODEL]) * _rms(a_ref[...], an_ref[...])
                  + _sigmoid(g[:, D_MODEL:]) * _rms(ssm, sn_ref[...]))
        o_ref[...] = merged.astype(BF16)

    return pl.pallas_call(
        body, name="merge_fwd", grid=(l_dim // tm,),
        in_specs=[_row_spec(tm, D_MODEL), _row_spec(tm, 2 * D_MODEL), _row_spec(tm, 2 * D_MODEL),
                  _full_spec((1, D_MODEL)), _full_spec((1, D_MODEL))],
        out_specs=_row_spec(tm, D_MODEL),
        out_shape=jax.ShapeDtypeStruct((l_dim, D_MODEL), BF16),
        compiler_params=_params(),
    )(attn, zab, gates, abn, sbn)


def _merge_bwd(attn, zab, gates, abn, sbn, dmerged, tm):
    l_dim = attn.shape[0]

    def body(a_ref, z_ref, g_ref, an_ref, sn_ref, dm_ref, da_ref, dz_ref, dg_ref, dan_ref, dsn_ref, dl_ref):
        @pl.when(pl.program_id(0) == 0)
        def _():
            dan_ref[...] = jnp.zeros_like(dan_ref)
            dsn_ref[...] = jnp.zeros_like(dsn_ref)

        z = z_ref[...].astype(F32)
        g = g_ref[...].astype(F32)
        dm = dm_ref[...].astype(F32)
        attn_v = a_ref[...]
        za, zb = z[:, :D_MODEL], z[:, D_MODEL:]
        sb = _sigmoid(zb)
        ssm = za * sb
        s_ga, s_gs = _sigmoid(g[:, :D_MODEL]), _sigmoid(g[:, D_MODEL:])
        a_n = _rms(attn_v, an_ref[...])
        s_n = _rms(ssm, sn_ref[...])
        dg_ref[:, :D_MODEL] = (dm * a_n * s_ga * (1.0 - s_ga)).astype(BF16)
        dg_ref[:, D_MODEL:] = (dm * s_n * s_gs * (1.0 - s_gs)).astype(BF16)
        dattn, dan = _rms_bwd(attn_v, an_ref[...], dm * s_ga)
        dssm, dsn = _rms_bwd(ssm, sn_ref[...], dm * s_gs)
        da = dattn.astype(BF16)
        da_ref[...] = da
        prod = da.astype(F32) * attn_v
        hi = prod.astype(BF16)
        lo = (prod - hi.astype(F32)).astype(BF16)
        ones = (lax.broadcasted_iota(jnp.int32, (D_MODEL, BLK), 0) // HEAD_DIM
                == lax.broadcasted_iota(jnp.int32, (D_MODEL, BLK), 1)).astype(BF16)
        dl_ref[...] = (jnp.dot(hi, ones, preferred_element_type=F32)
                       + jnp.dot(lo, ones, preferred_element_type=F32))
        dz_ref[:, :D_MODEL] = (dssm * sb).astype(BF16)
        dz_ref[:, D_MODEL:] = (dssm * za * sb * (1.0 - sb)).astype(BF16)
        dan_ref[...] += dan
        dsn_ref[...] += dsn

    return pl.pallas_call(
        body, name="merge_bwd", grid=(l_dim // tm,),
        in_specs=[_row_spec(tm, D_MODEL), _row_spec(tm, 2 * D_MODEL), _row_spec(tm, 2 * D_MODEL),
                  _full_spec((1, D_MODEL)), _full_spec((1, D_MODEL)), _row_spec(tm, D_MODEL)],
        out_specs=[_row_spec(tm, D_MODEL), _row_spec(tm, 2 * D_MODEL), _row_spec(tm, 2 * D_MODEL),
                   _full_spec((1, D_MODEL)), _full_spec((1, D_MODEL)), _row_spec(tm, BLK)],
        out_shape=[jax.ShapeDtypeStruct((l_dim, D_MODEL), BF16),
                   jax.ShapeDtypeStruct((l_dim, 2 * D_MODEL), BF16),
                   jax.ShapeDtypeStruct((l_dim, IN_COLS), BF16),
                   jax.ShapeDtypeStruct((1, D_MODEL), F32), jax.ShapeDtypeStruct((1, D_MODEL), F32),
                   jax.ShapeDtypeStruct((l_dim, BLK), F32)],
        compiler_params=_params(),
    )(attn, zab, gates, abn, sbn, dmerged)


FF_TILE = D_FF // 2


def _ffn_in_swiglu(hn, wt_ffn_in, tm):
    l_dim = hn.shape[0]
    nt = (((1,), (1,)), ((), ()))

    def body(a_ref, w_ref, gu_ref, act_ref):
        r = lax.dot_general(a_ref[...], w_ref[...], nt, preferred_element_type=F32)
        gate, up = r[:, :FF_TILE], r[:, FF_TILE:]
        gu_ref[...] = r.astype(BF16)
        act_ref[...] = (gate * _sigmoid(gate) * up).astype(BF16)

    return pl.pallas_call(
        body, name="ffn_in_swiglu", grid=(l_dim // tm, 2),
        in_specs=[pl.BlockSpec((tm, D_MODEL), lambda i, j: (i, 0)),
                  pl.BlockSpec((2 * FF_TILE, D_MODEL), lambda i, j: (j, 0))],
        out_specs=[pl.BlockSpec((tm, 2 * FF_TILE), lambda i, j: (i, j)),
                   pl.BlockSpec((tm, FF_TILE), lambda i, j: (i, j))],
        out_shape=[jax.ShapeDtypeStruct((l_dim, 2 * D_FF), BF16), jax.ShapeDtypeStruct((l_dim, D_FF), BF16)],
        compiler_params=_params(dimension_semantics=("parallel", "parallel")),
    )(hn, wt_ffn_in)


def _d_act_swiglu(dh2_b, w_ffn_out, gu, tm):
    l_dim = dh2_b.shape[0]
    nt = (((1,), (1,)), ((), ()))

    def body(d_ref, w_ref, gu_ref, o_ref):
        d = lax.dot_general(d_ref[...], w_ref[...], nt, preferred_element_type=F32)
        gate = gu_ref[:, :FF_TILE].astype(F32)
        up = gu_ref[:, FF_TILE:].astype(F32)
        s = _sigmoid(gate)
        o_ref[:, :FF_TILE] = (d * up * (s * (1.0 + gate * (1.0 - s)))).astype(BF16)
        o_ref[:, FF_TILE:] = (d * (gate * s)).astype(BF16)

    return pl.pallas_call(
        body, name="d_act_swiglu", grid=(l_dim // tm, 2),
        in_specs=[pl.BlockSpec((tm, D_MODEL), lambda i, j: (i, 0)),
                  pl.BlockSpec((FF_TILE, D_MODEL), lambda i, j: (j, 0)),
                  pl.BlockSpec((tm, 2 * FF_TILE), lambda i, j: (i, j))],
        out_specs=pl.BlockSpec((tm, 2 * FF_TILE), lambda i, j: (i, j)),
        out_shape=jax.ShapeDtypeStruct((l_dim, 2 * D_FF), BF16),
        compiler_params=_params(dimension_semantics=("parallel", "parallel")),
    )(dh2_b, w_ffn_out, gu)


def _loss_grad(h2, target2d, tm):
    l_dim = h2.shape[0]
    n_sub = tm // BLK

    def body(*refs):
        h_ref, t_refs = refs[0], refs[1:1 + n_sub]
        d_ref, db_ref, loss_ref = refs[1 + n_sub:]
        i = pl.program_id(0)

        @pl.when(i == 0)
        def _():
            loss_ref[...] = jnp.zeros_like(loss_ref)

        for k in range(n_sub):
            rows = slice(BLK * k, BLK * k + BLK)
            real = jnp.where(n_sub * i + k >= 1, 1.0, 0.0)
            err = (h_ref[rows, :] - t_refs[k][...]) * real
            d = err * (1.0 / D_MODEL)
            d_ref[rows, :] = d
            db_ref[rows, :] = d.astype(BF16)
            loss_ref[...] += 0.5 * jnp.sum(jnp.mean(err * err, axis=-1, keepdims=True), axis=0, keepdims=True)

    return pl.pallas_call(
        body, name="loss_grad", grid=(l_dim // tm,),
        in_specs=[_row_spec(tm, D_MODEL)] + _shifted_specs(n_sub, target2d.shape[0] // BLK),
        out_specs=[_row_spec(tm, D_MODEL), _row_spec(tm, D_MODEL), _full_spec((1, 1))],
        out_shape=[jax.ShapeDtypeStruct((l_dim, D_MODEL), F32), jax.ShapeDtypeStruct((l_dim, D_MODEL), BF16),
                   jax.ShapeDtypeStruct((1, 1), F32)],
        compiler_params=_params(),
    )(h2, *([target2d] * n_sub))


def _norm_bwd_res(name, h, g, dy, dres, tm):
    l_dim = h.shape[0]

    def body(h_ref, g_ref, dy_ref, dr_ref, o_ref, ob_ref, dg_ref):
        @pl.when(pl.program_id(0) == 0)
        def _():
            dg_ref[...] = jnp.zeros_like(dg_ref)

        dx, dg = _rms_bwd(h_ref[...], g_ref[...], dy_ref[...].astype(F32))
        out = dr_ref[...] + dx
        o_ref[...] = out
        ob_ref[...] = out.astype(BF16)
        dg_ref[...] += dg

    return pl.pallas_call(
        body, name=name, grid=(l_dim // tm,),
        in_specs=[_row_spec(tm, D_MODEL), _full_spec((1, D_MODEL)), _row_spec(tm, D_MODEL), _row_spec(tm, D_MODEL)],
        out_specs=[_row_spec(tm, D_MODEL), _row_spec(tm, D_MODEL), _full_spec((1, D_MODEL))],
        out_shape=[jax.ShapeDtypeStruct((l_dim, D_MODEL), F32), jax.ShapeDtypeStruct((l_dim, D_MODEL), BF16),
                   jax.ShapeDtypeStruct((1, D_MODEL), F32)],
        compiler_params=_params(),
    )(h, g, dy, dres)


def _final_bwd(h0, g, dxn, dh1, tm):
    l_dim = h0.shape[0]
    n_sub = tm // BLK
    n_tiles = (l_dim - BLK) // tm

    def sub_specs():
        return [_row_spec(BLK, D_MODEL, (lambda j, k=k: jnp.where(j < n_tiles, n_sub * j + 1 + k, 0)))
                for k in range(n_sub)]

    def body(*refs):
        h_refs, g_ref = refs[:n_sub], refs[n_sub]
        dy_refs, dr_refs = refs[n_sub + 1:2 * n_sub + 1], refs[2 * n_sub + 1:3 * n_sub + 1]
        gx_ref, gm_ref, dg_ref = refs[3 * n_sub + 1:]
        j = pl.program_id(0)

        @pl.when(j == 0)
        def _():
            dg_ref[...] = jnp.zeros_like(dg_ref)

        def block(k):
            dx, dg = _rms_bwd(h_refs[k][...], g_ref[...], dy_refs[k][...].astype(F32))
            dg_ref[...] += dg
            return dr_refs[k][...] + dx

        @pl.when(j < n_tiles)
        def _():
            for k in range(n_sub):
                gx_ref[BLK * k:BLK * k + BLK, :] = block(k)

        @pl.when(j == n_tiles)
        def _():
            gm_ref[...] = block(0)

    return pl.pallas_call(
        body, name="final_bwd", grid=(n_tiles + 1,),
        in_specs=sub_specs() + [_full_spec((1, D_MODEL))] + sub_specs() + sub_specs(),
        out_specs=[_row_spec(tm, D_MODEL, lambda j: jnp.minimum(j, n_tiles - 1)), _full_spec((BLK, D_MODEL)),
                   _full_spec((1, D_MODEL))],
        out_shape=[jax.ShapeDtypeStruct((l_dim - BLK, D_MODEL), F32), jax.ShapeDtypeStruct((BLK, D_MODEL), F32),
                   jax.ShapeDtypeStruct((1, D_MODEL), F32)],
        compiler_params=_params(),
    )(*([h0] * n_sub), g, *([dxn] * n_sub), *([dh1] * n_sub))


def _attn_bias():
    qi = np.arange(2 * BLK)[:, None] % BLK
    col = np.arange(3 * BLK)[None, :]
    kj, part = col % BLK, col // BLK
    out = []
    for n in range(3):
        meta_ok = (part == 0) & (kj >= PAD) & (n >= 1)
        prev_ok = (part == 1) & (kj > qi) & (n >= 2)
        cur_ok = (part == 2) & (kj <= qi) & ((n >= 1) | (kj >= PAD))
        out.append(np.where(meta_ok | prev_ok | cur_ok, 0.0, NEG))
    return jnp.asarray(np.stack(out), F32)


def _bias_spec():
    return pl.BlockSpec((1, 2 * BLK, 3 * BLK), lambda i: (jnp.minimum(i, 2), 0, 0))


def _attn_scores(q_ref, kwin, sk_ref, bias, kv, e):
    qs = jnp.concatenate([q_ref[:, BLK * (2 * kv):BLK * (2 * kv) + BLK],
                          q_ref[:, BLK * (2 * kv + 1):BLK * (2 * kv + 1) + BLK]], axis=0)
    s = lax.dot_general(qs, kwin, (((1,), (1,)), ((), ())), preferred_element_type=F32)
    h0 = 4 * kv + e
    row = lax.broadcasted_iota(jnp.int32, (2 * BLK, 1), 0)
    sink = jnp.where(row < BLK, sk_ref[:, h0:h0 + 1], sk_ref[:, h0 + 2:h0 + 3])
    return qs, s + bias, sink


def _attn_specs(nb):
    prev = lambda i: jnp.maximum(i - 1, 0)
    zero = lambda i: 0
    kv_specs = [_row_spec(BLK, KVX_W, zero), _row_spec(BLK, KVX_W, prev), _row_spec(BLK, KVX_W)]
    return kv_specs


def _attn_fwd(qn, kf, vf, sinks):
    l_dim = qn.shape[0]
    nb = l_dim // BLK

    def body(q_ref, km_ref, kp_ref, kc_ref, vm_ref, vp_ref, vc_ref, sk_ref, bias_ref, o_ref, lse_ref):
        bias = bias_ref[0]
        lane = lax.broadcasted_iota(jnp.int32, (BLK, BLK), 1)
        lse_all = jnp.zeros((BLK, BLK), F32)
        for kv in range(N_KV_HEADS):
            outs = []
            for e in range(2):
                sl = slice(BLK * (2 * kv + e), BLK * (2 * kv + e) + BLK)
                kwin = jnp.concatenate([km_ref[:, sl], kp_ref[:, sl], kc_ref[:, sl]], axis=0)
                vwin = jnp.concatenate([vm_ref[:, sl], vp_ref[:, sl], vc_ref[:, sl]], axis=0)
                _, s, sink = _attn_scores(q_ref, kwin, sk_ref, bias, kv, e)
                m = jnp.maximum(jnp.max(s, axis=-1, keepdims=True), sink)
                ex = jnp.exp(s - m)
                den = jnp.sum(ex, axis=-1, keepdims=True) + jnp.exp(sink - m)
                p = ex * (1.0 / den)
                outs.append(jnp.dot(p.astype(BF16), vwin, preferred_element_type=F32))
                lse = m + jnp.log(den)
                lse_all = jnp.where(lane == 4 * kv + e, lse[:BLK], lse_all)
                lse_all = jnp.where(lane == 4 * kv + 2 + e, lse[BLK:], lse_all)
            o = outs[0] + outs[1]
            o_ref[:, BLK * (2 * kv):BLK * (2 * kv) + BLK] = o[:BLK]
            o_ref[:, BLK * (2 * kv + 1):BLK * (2 * kv + 1) + BLK] = o[BLK:]
        lse_ref[...] = lse_all

    kv_specs = _attn_specs(nb)
    return pl.pallas_call(
        body, name="attn_fwd", grid=(nb,),
        in_specs=[_row_spec(BLK, Q_W)] + kv_specs + kv_specs + [_full_spec((1, N_Q_HEADS)), _bias_spec()],
        out_specs=[_row_spec(BLK, Q_W), _row_spec(BLK, BLK)],
        out_shape=[jax.ShapeDtypeStruct((l_dim, Q_W), F32), jax.ShapeDtypeStruct((l_dim, BLK), F32)],
        compiler_params=_params(),
    )(qn, kf, kf, kf, vf, vf, vf, sinks, _attn_bias())


def _attn_bwd(qn, kf, vf, sinks, lse, delta_rows, dattn):
    l_dim = qn.shape[0]
    nb = l_dim // BLK
    wide = KV_W
    tn = (((0,), (0,)), ((), ()))
    nt = (((1,), (1,)), ((), ()))

    def body(q_ref, km_ref, kp_ref, kc_ref, vm_ref, vp_ref, vc_ref, sk_ref, bias_ref, lse_ref, dl_ref, do_ref,
             dq_ref, dkc_ref, dkp_ref, dkm_ref, dvc_ref, dvp_ref, dvm_ref, dsk_ref):
        @pl.when(pl.program_id(0) == 0)
        def _():
            dkm_ref[...] = jnp.zeros_like(dkm_ref)
            dvm_ref[...] = jnp.zeros_like(dvm_ref)
            dsk_ref[...] = jnp.zeros_like(dsk_ref)

        bias = bias_ref[0]
        half2 = _lane_half((2 * BLK, BLK))
        half3 = _lane_half((3 * BLK, BLK))
        lane16 = lax.broadcasted_iota(jnp.int32, (1, N_Q_HEADS), 1)
        dsk = jnp.zeros((1, N_Q_HEADS), F32)
        folded_k, folded_v = [], []
        for kv in range(N_KV_HEADS):
            j0, j1 = 2 * kv, 2 * kv + 1
            do0 = do_ref[:, BLK * j0:BLK * j0 + BLK]
            do1 = do_ref[:, BLK * j1:BLK * j1 + BLK]
            dos = jnp.concatenate([do0, do1], axis=0)
            dqs, dks, dvs = [], [], []
            for e in range(2):
                sl = slice(BLK * (2 * kv + e), BLK * (2 * kv + e) + BLK)
                kwin = jnp.concatenate([km_ref[:, sl], kp_ref[:, sl], kc_ref[:, sl]], axis=0)
                vwin = jnp.concatenate([vm_ref[:, sl], vp_ref[:, sl], vc_ref[:, sl]], axis=0)
                qs, s, sink = _attn_scores(q_ref, kwin, sk_ref, bias, kv, e)
                h0 = 4 * kv + e
                lse_rows = jnp.concatenate([lse_ref[:, h0:h0 + 1], lse_ref[:, h0 + 2:h0 + 3]], axis=0)
                p = jnp.exp(s - lse_rows)
                p_sink = jnp.exp(sink - lse_rows)
                delta = jnp.concatenate([dl_ref[:, h0:h0 + 1], dl_ref[:, h0 + 2:h0 + 3]], axis=0)
                dp = lax.dot_general(dos, vwin, nt, preferred_element_type=F32)
                ds = (p * (dp - delta)).astype(BF16)
                pb = p.astype(BF16)
                dqs.append(jnp.dot(ds, kwin, preferred_element_type=F32))
                dks.append(lax.dot_general(ds, qs, tn, preferred_element_type=F32))
                dvs.append(lax.dot_general(pb, dos, tn, preferred_element_type=F32))
                sink_g = -(p_sink * delta)
                g_lo = jnp.sum(sink_g[:BLK], axis=0, keepdims=True)
                g_hi = jnp.sum(sink_g[BLK:], axis=0, keepdims=True)
                dsk = dsk + jnp.where(lane16 == h0, g_lo, 0.0) + jnp.where(lane16 == h0 + 2, g_hi, 0.0)
            dq = jnp.where(half2 == 0, dqs[0], dqs[1])
            dq_ref[:, BLK * j0:BLK * j0 + BLK] = dq[:BLK].astype(BF16)
            dq_ref[:, BLK * j1:BLK * j1 + BLK] = dq[BLK:].astype(BF16)
            own = kv % 2
            folded_k.append(dks[own] + pltpu.roll(dks[1 - own], HEAD_DIM, axis=1))
            folded_v.append(dvs[own] + pltpu.roll(dvs[1 - own], HEAD_DIM, axis=1))
            if own == 1:
                cols = slice(BLK * (kv // 2), BLK * (kv // 2) + BLK)
                for folded, m_ref, p_ref, c_ref in ((folded_k, dkm_ref, dkp_ref, dkc_ref),
                                                    (folded_v, dvm_ref, dvp_ref, dvc_ref)):
                    both = jnp.where(half3 == 0, folded[0], folded[1])
                    m_ref[:, cols] += both[:BLK]
                    p_ref[:, cols] = both[BLK:2 * BLK].astype(BF16)
                    c_ref[:, cols] = both[2 * BLK:].astype(BF16)
                folded_k, folded_v = [], []
        dsk_ref[...] += dsk

    kv_specs = _attn_specs(nb)
    row_wide = _row_spec(BLK, wide)
    acc_wide = _full_spec((BLK, wide))
    big = jax.ShapeDtypeStruct((l_dim, wide), BF16)
    return pl.pallas_call(
        body, name="attn_bwd", grid=(nb,),
        in_specs=[_row_spec(BLK, Q_W)] + kv_specs + kv_specs
        + [_full_spec((1, N_Q_HEADS)), _bias_spec(), _row_spec(BLK, BLK), _row_spec(BLK, BLK),
           _row_spec(BLK, Q_W)],
        out_specs=[_row_spec(BLK, Q_W), row_wide, row_wide, acc_wide, row_wide, row_wide, acc_wide,
                   _full_spec((1, N_Q_HEADS))],
        out_shape=[jax.ShapeDtypeStruct((l_dim, Q_W), BF16), big, big, jax.ShapeDtypeStruct((BLK, wide), F32),
                   big, big, jax.ShapeDtypeStruct((BLK, wide), F32), jax.ShapeDtypeStruct((1, N_Q_HEADS), F32)],
        compiler_params=_params(),
    )(qn, kf, kf, kf, vf, vf, vf, sinks, _attn_bias(), lse, delta_rows, dattn)


def _qk_bwd(qkv, q_norm_t, k_norm_t, e_mat, dq, dkc, dkp, dkm, dvc, dvp, dvm, dproj):
    l_dim = qkv.shape[0]
    nb = l_dim // BLK
    wide = KV_W

    def body(x_ref, qg_ref, kg_ref, e_ref, dq_ref, dkc_ref, dkp_ref, dkm_ref, dvc_ref, dvp_ref, dvm_ref,
             dproj_ref, o_ref, dqg_ref, dkg_ref):
        i = pl.program_id(0)

        @pl.when(i == 0)
        def _():
            dqg_ref[...] = jnp.zeros_like(dqg_ref)
            dkg_ref[...] = jnp.zeros_like(dkg_ref)

        first = jnp.where(i == 0, 1.0, 0.0)
        not_last = jnp.where(i < nb - 1, 1.0, 0.0)
        dk_x = dkc_ref[...].astype(F32) + not_last * dkp_ref[...].astype(F32) + first * dkm_ref[...]
        dv_x = dvc_ref[...].astype(F32) + not_last * dvp_ref[...].astype(F32) + first * dvm_ref[...]
        x = x_ref[...]
        dqx, dqg = _head_rms_bwd(x[:, :Q_W], qg_ref[...], dq_ref[...].astype(F32) * (HEAD_DIM ** -0.5), e_ref)
        dkx, dkg = _head_rms_bwd(x[:, Q_W:Q_W + KV_W], kg_ref[...], dk_x, e_ref)
        o_ref[:, :Q_W] = dqx.astype(BF16)
        o_ref[:, Q_W:Q_W + KV_W] = dkx.astype(BF16)
        o_ref[:, Q_W + KV_W:] = dv_x.astype(BF16)
        dqg_ref[...] += dqg
        dkg_ref[...] += dkg

    nxt = lambda i: jnp.minimum(i + 1, nb - 1)
    row_wide = _row_spec(BLK, wide)
    nxt_wide = _row_spec(BLK, wide, nxt)
    acc_wide = _full_spec((BLK, wide))
    qkv_w = Q_W + 2 * KV_W
    in_specs = [_row_spec(BLK, qkv_w), _full_spec((1, Q_W)), _full_spec((1, KV_W)),
                _full_spec((256, 256)), _row_spec(BLK, Q_W),
                row_wide, nxt_wide, acc_wide, row_wide, nxt_wide, acc_wide, pl.BlockSpec(memory_space=pl.ANY)]
    return pl.pallas_call(
        body, name="qk_bwd", grid=(nb,),
        in_specs=in_specs,
        out_specs=[pl.BlockSpec((BLK, qkv_w), lambda i: (i, 3 * D_MODEL // qkv_w)),
                   _full_spec((1, Q_W)), _full_spec((1, KV_W))],
        input_output_aliases={len(in_specs) - 1: 0},
        out_shape=[jax.ShapeDtypeStruct(dproj.shape, BF16),
                   jax.ShapeDtypeStruct((1, Q_W), F32), jax.ShapeDtypeStruct((1, KV_W), F32)],
        compiler_params=_params(),
    )(qkv, q_norm_t, k_norm_t, e_mat, dq, dkc, dkp, dkm, dvc, dvp, dvm, dproj)


GRP = 8


def _strided(r, g):
    return pl.ds(r, g, stride=GRP)


def _lane_slab(ref, i, r, g):
    return ref[i, _strided(r, g), :]


def _chunk_carries(xr_ref, xi_ref, i, ar, ai, sqr_ref, sqi_ref, seq_r, seq_i, cin_r, cin_i, sign, reverse):
    g = xr_ref.shape[1] // GRP
    sr = si = None
    for r in (range(GRP - 1, -1, -1) if reverse else range(GRP)):
        xr, xi = _lane_slab(xr_ref, i, r, g), _lane_slab(xi_ref, i, r, g)
        if sr is not None:
            xr, xi = xr + ar * sr - ai * si, xi + ar * si + ai * sr
        sr, si = xr, xi
    row = lax.broadcasted_iota(jnp.int32, sr.shape, 0)
    idx, s = 0, 1
    while s < g:
        br = sqr_ref[idx:idx + 1, BLK * i:BLK * i + BLK]
        bi = sign * sqi_ref[idx:idx + 1, BLK * i:BLK * i + BLK]
        shift, keep = (g - s, row < g - s) if reverse else (s, row >= s)
        pr = jnp.where(keep, pltpu.roll(sr, shift, axis=0), 0.0)
        pi = jnp.where(keep, pltpu.roll(si, shift, axis=0), 0.0)
        sr, si = sr + br * pr - bi * pi, si + br * pi + bi * pr
        idx, s = idx + 1, 2 * s
    return sr + seq_r * cin_r - seq_i * cin_i, si + seq_r * cin_i + seq_i * cin_r


def _ssm_fwd_kb(u, wb_re, wb_im, wc_re, wc_im, d_skip, tabs, q):
    l_dim = u.shape[0]
    nc = l_dim // q
    g = q // GRP

    def body(u_ref, wbr_ref, wbi_ref, wcr_ref, wci_ref, d_ref, a1r_ref, a1i_ref, sqr_ref, sqi_ref,
             seqr_ref, seqi_ref, y_ref, z_ref, sr_ref, si_ref, cr_ref, ci_ref, xr_ref, xi_ref):
        @pl.when(pl.program_id(1) == 0)
        def _():
            cr_ref[...] = jnp.zeros_like(cr_ref)
            ci_ref[...] = jnp.zeros_like(ci_ref)

        u_kb = u_ref[...]
        ub = u_kb.astype(BF16)
        xr = jnp.dot(ub, wbr_ref[0], preferred_element_type=F32)
        xi = jnp.dot(ub, wbi_ref[0], preferred_element_type=F32)
        for i in range(LB_KB):
            xr_ref[i] = xr[:, BLK * i:BLK * i + BLK]
            xi_ref[i] = xi[:, BLK * i:BLK * i + BLK]
        row = lax.broadcasted_iota(jnp.int32, (g, BLK), 0)
        for i in range(LB_KB):
            lanes = slice(BLK * i, BLK * i + BLK)
            ar, ai = a1r_ref[0:1, lanes], a1i_ref[0:1, lanes]
            cin_r, cin_i = cr_ref[0:1, lanes], ci_ref[0:1, lanes]
            tr, ti = _chunk_carries(xr_ref, xi_ref, i, ar, ai, sqr_ref, sqi_ref, seqr_ref[:, lanes],
                                    seqi_ref[:, lanes], cin_r, cin_i, 1.0, reverse=False)
            cr_ref[0:1, lanes] = jnp.sum(jnp.where(row == g - 1, tr, 0.0), axis=0, keepdims=True)
            ci_ref[0:1, lanes] = jnp.sum(jnp.where(row == g - 1, ti, 0.0), axis=0, keepdims=True)
            pr = jnp.where(row == 0, cin_r, pltpu.roll(tr, 1, axis=0))
            pi = jnp.where(row == 0, cin_i, pltpu.roll(ti, 1, axis=0))
            for r in range(GRP):
                pr, pi = (_lane_slab(xr_ref, i, r, g) + ar * pr - ai * pi,
                          _lane_slab(xi_ref, i, r, g) + ar * pi + ai * pr)
                sr_ref[i, _strided(r, g), :] = pr
                si_ref[i, _strided(r, g), :] = pi
        s_r = jnp.concatenate([sr_ref[i] for i in range(LB_KB)], axis=1)
        s_i = jnp.concatenate([si_ref[i] for i in range(LB_KB)], axis=1)
        y = (jnp.dot(s_r.astype(BF16), wcr_ref[0], preferred_element_type=F32)
             - jnp.dot(s_i.astype(BF16), wci_ref[0], preferred_element_type=F32)
             + d_ref[...] * u_kb)
        y_ref[...] = y.astype(BF16)
        z_ref[...] = _gelu(y).astype(BF16)

    chan = pl.BlockSpec((q, BLK), lambda k, c: (c, k))
    wb_spec = pl.BlockSpec((1, BLK, ST_KB), lambda k, c: (k, 0, 0))
    wc_spec = pl.BlockSpec((1, ST_KB, BLK), lambda k, c: (k, 0, 0))
    tab_specs = [pl.BlockSpec((t.shape[0], ST_KB), lambda k, c: (0, k)) for t in tabs[:6]]
    state_spec = pl.BlockSpec((LB_KB, q, BLK), lambda k, c: (k, c, 0))
    state_shape = jax.ShapeDtypeStruct((N_LB, l_dim, BLK), F32)
    return pl.pallas_call(
        body, name="ssm_fwd", grid=(SSM_KB, nc),
        in_specs=[chan, wb_spec, wb_spec, wc_spec, wc_spec, pl.BlockSpec((1, BLK), lambda k, c: (0, k))] + tab_specs,
        out_specs=[chan, chan, state_spec, state_spec],
        out_shape=[jax.ShapeDtypeStruct((l_dim, D_MODEL), BF16), jax.ShapeDtypeStruct((l_dim, D_MODEL), BF16),
                   state_shape, state_shape],
        scratch_shapes=[pltpu.VMEM((8, ST_KB), F32), pltpu.VMEM((8, ST_KB), F32),
                        pltpu.VMEM((LB_KB, q, BLK), F32), pltpu.VMEM((LB_KB, q, BLK), F32)],
        compiler_params=_params(dimension_semantics=("parallel", "arbitrary")),
    )(u, wb_re, wb_im, wc_re, wc_im, d_skip, *tabs[:6])


def _ssm_bwd_kb(dz, y, u, s_re, s_im, wb_re, wb_im, wc_re, wc_im, d_skip, tabs, q, dproj):
    l_dim = u.shape[0]
    nc = l_dim // q
    g = q // GRP

    def body(dz_ref, y_ref, u_ref, sr_ref, si_ref, wbr_ref, wbi_ref, wcr_ref, wci_ref, d_ref,
             a1r_ref, a1i_ref, sqr_ref, sqi_ref, revr_ref, revi_ref, dproj_ref,
             du_ref, dd_ref, dar_ref, dai_ref, dwbr_ref, dwbi_ref, dwcr_ref, dwci_ref,
             cr_ref, ci_ref, gr_ref, gi_ref):
        @pl.when(pl.program_id(1) == 0)
        def _():
            for ref in (cr_ref, ci_ref, dd_ref, dar_ref, dai_ref, dwbr_ref, dwbi_ref, dwcr_ref, dwci_ref):
                ref[...] = jnp.zeros_like(ref)

        tn = (((0,), (0,)), ((), ()))
        nt = (((1,), (1,)), ((), ()))
        u_kb = u_ref[...]
        dy = dz_ref[...].astype(F32) * _gelu_grad(y_ref[...].astype(F32))
        dyb = dy.astype(BF16)
        ub = u_kb.astype(BF16)
        dd_ref[...] += jnp.sum(dy * u_kb, axis=0, keepdims=True)
        ds_r = lax.dot_general(dyb, wcr_ref[0], nt, preferred_element_type=F32)
        ds_i = -lax.dot_general(dyb, wci_ref[0], nt, preferred_element_type=F32)
        for i in range(LB_KB):
            gr_ref[i] = ds_r[:, BLK * i:BLK * i + BLK]
            gi_ref[i] = ds_i[:, BLK * i:BLK * i + BLK]
        row = lax.broadcasted_iota(jnp.int32, (g, BLK), 0)
        for i in range(LB_KB):
            lanes = slice(BLK * i, BLK * i + BLK)
            ar, ai = a1r_ref[0:1, lanes], -a1i_ref[0:1, lanes]
            cin_r, cin_i = cr_ref[0:1, lanes], ci_ref[0:1, lanes]
            tr, ti = _chunk_carries(gr_ref, gi_ref, i, ar, ai, sqr_ref, sqi_ref, revr_ref[:, lanes],
                                    -revi_ref[:, lanes], cin_r, cin_i, -1.0, reverse=True)
            cr_ref[0:1, lanes] = jnp.sum(jnp.where(row == 0, tr, 0.0), axis=0, keepdims=True)
            ci_ref[0:1, lanes] = jnp.sum(jnp.where(row == 0, ti, 0.0), axis=0, keepdims=True)
            nr = jnp.where(row == g - 1, cin_r, pltpu.roll(tr, g - 1, axis=0))
            ni = jnp.where(row == g - 1, cin_i, pltpu.roll(ti, g - 1, axis=0))
            acc_r = jnp.zeros((g, BLK), F32)
            acc_i = jnp.zeros((g, BLK), F32)
            for r in range(GRP - 1, -1, -1):
                s_r, s_i = _lane_slab(sr_ref, i, r, g), _lane_slab(si_ref, i, r, g)
                acc_r = acc_r + (nr * s_r + ni * s_i)
                acc_i = acc_i + (ni * s_r - nr * s_i)
                nr, ni = (_lane_slab(gr_ref, i, r, g) + ar * nr - ai * ni,
                          _lane_slab(gi_ref, i, r, g) + ar * ni + ai * nr)
                gr_ref[i, _strided(r, g), :] = nr
                gi_ref[i, _strided(r, g), :] = ni
            dar_ref[:, lanes] += jnp.sum(acc_r, axis=0, keepdims=True)
            dai_ref[:, lanes] += jnp.sum(acc_i, axis=0, keepdims=True)
        grb = jnp.concatenate([gr_ref[i] for i in range(LB_KB)], axis=1).astype(BF16)
        gib = jnp.concatenate([gi_ref[i] for i in range(LB_KB)], axis=1).astype(BF16)
        srb = jnp.concatenate([sr_ref[i] for i in range(LB_KB)], axis=1).astype(BF16)
        sib = jnp.concatenate([si_ref[i] for i in range(LB_KB)], axis=1).astype(BF16)
        du = (lax.dot_general(grb, wbr_ref[0], nt, preferred_element_type=F32)
              + lax.dot_general(gib, wbi_ref[0], nt, preferred_element_type=F32)
              + d_ref[...] * dy)
        du_ref[...] = du.astype(BF16)
        dwbr_ref[0] += lax.dot_general(ub, grb, tn, preferred_element_type=F32)
        dwbi_ref[0] += lax.dot_general(ub, gib, tn, preferred_element_type=F32)
        dwcr_ref[0] += lax.dot_general(srb, dyb, tn, preferred_element_type=F32)
        dwci_ref[0] -= lax.dot_general(sib, dyb, tn, preferred_element_type=F32)

    chan = pl.BlockSpec((q, BLK), lambda k, c: (nc - 1 - c, k))
    wb_spec = pl.BlockSpec((1, BLK, ST_KB), lambda k, c: (k, 0, 0))
    wc_spec = pl.BlockSpec((1, ST_KB, BLK), lambda k, c: (k, 0, 0))
    tab_in = [tabs[0], tabs[1], tabs[2], tabs[3], tabs[6], tabs[7]]
    tab_specs = [pl.BlockSpec((t.shape[0], ST_KB), lambda k, c: (0, k)) for t in tab_in]
    vec = pl.BlockSpec((1, BLK), lambda k, c: (0, k))
    svec = pl.BlockSpec((1, ST_KB), lambda k, c: (0, k))
    state_spec = pl.BlockSpec((LB_KB, q, BLK), lambda k, c: (k, nc - 1 - c, 0))
    du_spec = pl.BlockSpec((q, BLK), lambda k, c: (nc - 1 - c, 2 * D_MODEL // BLK + k))
    in_specs = ([chan, chan, chan, state_spec, state_spec, wb_spec, wb_spec, wc_spec, wc_spec, vec] + tab_specs
                + [pl.BlockSpec(memory_space=pl.ANY)])
    return pl.pallas_call(
        body, name="ssm_bwd", grid=(SSM_KB, nc),
        in_specs=in_specs,
        out_specs=[du_spec, vec, svec, svec, wb_spec, wb_spec, wc_spec, wc_spec],
        input_output_aliases={len(in_specs) - 1: 0},
        out_shape=[jax.ShapeDtypeStruct(dproj.shape, BF16), jax.ShapeDtypeStruct((1, D_MODEL), F32),
                   jax.ShapeDtypeStruct((1, N_STATE), F32), jax.ShapeDtypeStruct((1, N_STATE), F32),
                   jax.ShapeDtypeStruct((SSM_KB, BLK, ST_KB), F32), jax.ShapeDtypeStruct((SSM_KB, BLK, ST_KB), F32),
                   jax.ShapeDtypeStruct((SSM_KB, ST_KB, BLK), F32), jax.ShapeDtypeStruct((SSM_KB, ST_KB, BLK), F32)],
        scratch_shapes=[pltpu.VMEM((8, ST_KB), F32), pltpu.VMEM((8, ST_KB), F32),
                        pltpu.VMEM((LB_KB, q, BLK), F32), pltpu.VMEM((LB_KB, q, BLK), F32)],
        compiler_params=_params(dimension_semantics=("parallel", "arbitrary")),
    )(dz, y, u, s_re, s_im, wb_re, wb_im, wc_re, wc_im, d_skip, *tab_in, dproj)


def _discretize(lam_re, lam_im, log_dt, b_re, b_im):
    dt = jnp.exp(log_dt)[:, None]
    mag = jnp.exp(lam_re * dt)
    ar, ai = mag * jnp.cos(lam_im * dt), mag * jnp.sin(lam_im * dt)
    den = lam_re * lam_re + lam_im * lam_im
    nr, ni = ar - 1.0, ai
    fr, fi = (nr * lam_re + ni * lam_im) / den, (ni * lam_re - nr * lam_im) / den
    bbar_re = fr[..., None] * b_re - fi[..., None] * b_im
    bbar_im = fr[..., None] * b_im + fi[..., None] * b_re
    return ar, ai, bbar_re, bbar_im


def _block_diag_b(bbar):
    eye = jnp.eye(8, dtype=bbar.dtype)
    return jnp.einsum("kgpc,gh->kgchp", bbar.reshape(8, 8, SSM_STATE, SSM_GROUP_CH), eye).reshape(8, BLK, ST_KB)


def _block_diag_b_t(dwb):
    eye = jnp.eye(8, dtype=dwb.dtype)
    return jnp.einsum("kgchp,gh->kgpc", dwb.reshape(8, 8, SSM_GROUP_CH, 8, SSM_STATE), eye).reshape(
        SSM_GROUPS, SSM_STATE, SSM_GROUP_CH)


def _block_diag_c(c):
    eye = jnp.eye(8, dtype=c.dtype)
    return jnp.einsum("kgcp,gh->kgphc", c.reshape(8, 8, SSM_GROUP_CH, SSM_STATE), eye).reshape(8, ST_KB, BLK)


def _block_diag_c_t(dwc):
    eye = jnp.eye(8, dtype=dwc.dtype)
    return jnp.einsum("kgphc,gh->kgcp", dwc.reshape(8, 8, SSM_STATE, 8, SSM_GROUP_CH), eye).reshape(
        SSM_GROUPS, SSM_GROUP_CH, SSM_STATE)


def _powers(br, bi, n):
    pr, pi = br, bi
    cr, ci = br, bi
    while pr.shape[0] < n:
        pr, pi = (jnp.concatenate([pr, pr * cr - pi * ci], axis=0),
                  jnp.concatenate([pi, pr * ci + pi * cr], axis=0))
        cr, ci = cr * cr - ci * ci, 2.0 * cr * ci
    return pr[:n], pi[:n]


def _powers_desc(br, bi, n):
    pr, pi = br, bi
    cr, ci = br, bi
    while pr.shape[0] < n:
        pr, pi = (jnp.concatenate([pr * cr - pi * ci, pr], axis=0),
                  jnp.concatenate([pr * ci + pi * cr, pi], axis=0))
        cr, ci = cr * cr - ci * ci, 2.0 * cr * ci
    return pr, pi


def _power_tables(ar, ai, g):
    a1r, a1i = _powers(ar, ai, GRP)
    seqr, seqi = _powers(a1r[GRP - 1:], a1i[GRP - 1:], g)
    g2 = 1 << (g - 1).bit_length()
    revr, revi = _powers_desc(a1r[GRP - 1:], a1i[GRP - 1:], g2)
    revr, revi = revr[g2 - g:], revi[g2 - g:]
    sq_r, sq_i = [seqr[0:1]], [seqi[0:1]]
    while len(sq_r) < 8:
        r, i = sq_r[-1], sq_i[-1]
        sq_r.append(r * r - i * i)
        sq_i.append(2.0 * r * i)
    sqr, sqi = jnp.concatenate(sq_r, axis=0), jnp.concatenate(sq_i, axis=0)
    return a1r, a1i, sqr, sqi, seqr, seqi, revr, revi


HBM_SPEC = pl.BlockSpec(memory_space=pltpu.HBM)
SEM_SPEC = pl.BlockSpec(memory_space=pltpu.SEMAPHORE)
DATAFLOW = pltpu.SideEffectType.DATAFLOW_SIDE_EFFECTING


def _plain_rows(p, m):
    return p * m


def _ffn_in_rows(p, m):
    return ((p & 3) >> 1) * (4 * m) + (p >> 2) * (2 * m) + (p & 1) * m


def _peer_copies(src_refs, land_refs, send_sems, recv_sems, chunked, row_fns):
    x, y, c = lax.axis_index("x"), lax.axis_index("y"), lax.axis_index("c")
    me = 4 * x + 2 * y + c
    copies = []
    for a, (src, land) in enumerate(zip(src_refs, land_refs)):
        m = land.shape[0] // N_DEV
        for k in range(N_DEV - 1):
            rel = k + 1
            bx, by, bc = (rel >> 2) & 1, (rel >> 1) & 1, rel & 1
            peer = (x + bx - 2 * x * bx, y + by - 2 * y * by, c + bc - 2 * c * bc)
            p_idx = 4 * peer[0] + 2 * peer[1] + peer[2]
            copies.append(pltpu.make_async_remote_copy(
                src_ref=src.at[pl.ds(row_fns[a](p_idx, m), m), :] if chunked else src,
                dst_ref=land.at[pl.ds(me * m if chunked else row_fns[a](me, m), m), :],
                send_sem=send_sems[a * (N_DEV - 1) + k], recv_sem=recv_sems[a * (N_DEV - 1) + k],
                device_id=peer, device_id_type=MESH))
    return copies


def _own_copies(src_refs, land_outs, sems, chunked, row_fns):
    me = _my_index()
    copies = []
    for a, (src, land) in enumerate(zip(src_refs, land_outs)):
        m = land.shape[0] // N_DEV
        rows = row_fns[a](me, m)
        copies.append(pltpu.make_async_copy(src.at[pl.ds(rows, m), :] if chunked else src,
                                            land.at[pl.ds(me * m if chunked else rows, m), :], sems[a]))
    return copies


def _send_start(name, srcs, lands, chunked, row_fns=None):
    n = len(srcs)
    ns = n * (N_DEV - 1)
    row_fns = row_fns or [_plain_rows] * n

    def body(*refs):
        src_refs, land_refs = refs[:n], refs[n:2 * n]
        send_sems, own_sems = refs[2 * n:2 * n + ns], refs[2 * n + ns:2 * n + ns + n]
        recv_sems = refs[2 * n + ns + n:2 * n + 2 * ns + n]
        land_outs, token = refs[-1 - n:-1], refs[-1]
        for cp in _own_copies(src_refs, land_outs, own_sems, chunked, row_fns):
            cp.start()
        for cp in _peer_copies(src_refs, land_refs, send_sems, recv_sems, chunked, row_fns):
            cp.start()
        token[...] = jnp.zeros_like(token)

    ins = [pltpu.with_memory_space_constraint(a, pltpu.HBM) for a in list(srcs) + list(lands)]
    n_sems = 2 * ns + n
    out = pl.pallas_call(
        body, name=name,
        in_specs=[HBM_SPEC] * (2 * n),
        out_specs=[SEM_SPEC] * n_sems + [HBM_SPEC] * (2 * n) + [pl.BlockSpec(memory_space=pltpu.VMEM)],
        out_shape=[pltpu.SemaphoreType.DMA(())] * n_sems
        + [pltpu.HBM(a.shape, a.dtype) for a in list(srcs) + list(lands)]
        + [jax.ShapeDtypeStruct((8, BLK), F32)],
        input_output_aliases={i: i + n_sems for i in range(2 * n)},
        compiler_params=pltpu.CompilerParams(has_side_effects=DATAFLOW),
    )(*ins)
    return (out[:ns + n], out[ns + n:n_sems], out[n_sems:n_sems + n], out[n_sems + n:n_sems + 2 * n], out[-1])


def _send_wait(name, send_sems, recv_sems, srcs, lands, after, chunked, row_fns=None):
    n = len(srcs)
    ns = n * (N_DEV - 1)
    row_fns = row_fns or [_plain_rows] * n
    after = list(after) if isinstance(after, (list, tuple)) else [after]

    def body(*refs):
        src_refs, land_refs = refs[:n], refs[n:2 * n]
        s_sems, own_sems = refs[2 * n:2 * n + ns], refs[2 * n + ns:2 * n + ns + n]
        r_sems = refs[2 * n + ns + n:2 * n + 2 * ns + n]
        land_outs = refs[-n:]
        for cp in _own_copies(src_refs, land_outs, own_sems, chunked, row_fns):
            cp.wait()
        copies = _peer_copies(src_refs, land_refs, s_sems, r_sems, chunked, row_fns)
        for cp in copies:
            cp.wait_send()
        for cp in copies:
            cp.wait_recv()

    out = pl.pallas_call(
        body, name=name,
        in_specs=[HBM_SPEC] * (2 * n) + [SEM_SPEC] * (2 * ns + n) + [pl.BlockSpec(memory_space=pl.ANY)] * len(after),
        out_specs=[HBM_SPEC] * (2 * n),
        out_shape=[pltpu.HBM(a.shape, a.dtype) for a in list(srcs) + list(lands)],
        input_output_aliases={i: i for i in range(2 * n)},
        compiler_params=pltpu.CompilerParams(has_side_effects=DATAFLOW),
    )(*srcs, *lands, *send_sems, *recv_sems, *after)
    return out[n:]


def _sum_slots(name, recv, own, wmv=None):
    m, ncol = own.shape
    tr = m // 2 if (m // 2) % 16 == 0 else m
    g = m // tr
    c1 = 1.0 - ADAM_B1 ** ADAM_STEP
    c2 = 1.0 - ADAM_B2 ** ADAM_STEP

    def body(*refs):
        slots, own_ref, o_ref = refs[:N_DEV], refs[N_DEV], refs[N_DEV + 1 + (3 if wmv else 0)]
        me = _my_index()
        tot = None
        for s in range(N_DEV):
            v = jnp.where(me == s, own_ref[...], slots[s][...].astype(F32))
            tot = v if tot is None else tot + v
        o_ref[...] = tot
        if wmv:
            w_ref, m_ref, v_ref = refs[N_DEV + 1:N_DEV + 4]
            d_ref, nm_ref, nv_ref = refs[N_DEV + 5:N_DEV + 8]
            nm = ADAM_B1 * m_ref[...] + (1.0 - ADAM_B1) * tot
            nv = ADAM_B2 * v_ref[...] + (1.0 - ADAM_B2) * (tot * tot)
            d_ref[...] = -ADAM_LR * ((nm / c1) / (jnp.sqrt(nv / c2) + ADAM_EPS) + ADAM_WD * w_ref[...])
            nm_ref[...] = nm
            nv_ref[...] = nv

    def slot_spec(s):
        return pl.BlockSpec((tr, ncol), lambda i: (s * g + i, 0))

    tile = pl.BlockSpec((tr, ncol), lambda i: (i, 0))
    shape = jax.ShapeDtypeStruct((m, ncol), F32)
    n_extra = 3 if wmv else 0
    out = pl.pallas_call(
        body, name=name, grid=(g,),
        in_specs=[slot_spec(s) for s in range(N_DEV)] + [tile] * (1 + n_extra),
        out_specs=[tile] * (1 + n_extra),
        out_shape=[shape] * (1 + n_extra),
        compiler_params=_params(),
    )(*([recv] * N_DEV), own, *(wmv or ()))
    return out if wmv else out[0]


def _sum_gathered(name, gathered, rows):
    tr = _pick(rows, 512, 8)
    g = rows // tr

    def body(*refs):
        o_ref = refs[N_DEV]
        tot = refs[0][...]
        for s in range(1, N_DEV):
            tot = tot + refs[s][...]
        o_ref[...] = tot

    return pl.pallas_call(
        body, name=name, grid=(g,),
        in_specs=[pl.BlockSpec((tr, BLK), (lambda i, s=s: (s * g + i, 0))) for s in range(N_DEV)],
        out_specs=pl.BlockSpec((tr, BLK), lambda i: (i, 0)),
        out_shape=jax.ShapeDtypeStruct((rows, BLK), F32),
        compiler_params=_params(),
    )(*([gathered] * N_DEV))


def _adamw(name, w, g, m, v):
    r, c = w.shape
    tr = _pick(r, 256, 8) if r % 8 == 0 else r
    c1 = 1.0 - ADAM_B1 ** ADAM_STEP
    c2 = 1.0 - ADAM_B2 ** ADAM_STEP

    def body(w_ref, g_ref, m_ref, v_ref, d_ref, nm_ref, nv_ref):
        gv = g_ref[...]
        nm = ADAM_B1 * m_ref[...] + (1.0 - ADAM_B1) * gv
        nv = ADAM_B2 * v_ref[...] + (1.0 - ADAM_B2) * (gv * gv)
        m_hat = nm / c1
        v_hat = nv / c2
        d_ref[...] = -ADAM_LR * (m_hat / (jnp.sqrt(v_hat) + ADAM_EPS) + ADAM_WD * w_ref[...])
        nm_ref[...] = nm
        nv_ref[...] = nv

    spec = pl.BlockSpec((tr, c), lambda i: (i, 0))
    shape = jax.ShapeDtypeStruct((r, c), F32)
    return pl.pallas_call(
        body, name=name, grid=(r // tr,),
        in_specs=[spec] * 4, out_specs=[spec] * 3, out_shape=[shape] * 3,
        compiler_params=_params(),
    )(w, g, m, v)


def _adamw_many(name, ws, gs, ms, vs):
    n = len(ws)
    c1 = 1.0 - ADAM_B1 ** ADAM_STEP
    c2 = 1.0 - ADAM_B2 ** ADAM_STEP

    def body(*refs):
        for a in range(n):
            w_ref, g_ref, m_ref, v_ref = refs[a], refs[n + a], refs[2 * n + a], refs[3 * n + a]
            d_ref, nm_ref, nv_ref = refs[4 * n + a], refs[5 * n + a], refs[6 * n + a]
            gv = g_ref[...]
            nm = ADAM_B1 * m_ref[...] + (1.0 - ADAM_B1) * gv
            nv = ADAM_B2 * v_ref[...] + (1.0 - ADAM_B2) * (gv * gv)
            d_ref[...] = -ADAM_LR * ((nm / c1) / (jnp.sqrt(nv / c2) + ADAM_EPS) + ADAM_WD * w_ref[...])
            nm_ref[...] = nm
            nv_ref[...] = nv

    specs = [_full_spec(w.shape) for w in ws]
    shapes = [jax.ShapeDtypeStruct(w.shape, F32) for w in ws]
    out = pl.pallas_call(
        body, name=name, grid=(1,),
        in_specs=specs * 4, out_specs=specs * 3, out_shape=shapes * 3,
        compiler_params=_params(),
    )(*ws, *gs, *ms, *vs)
    return out[:n], out[n:2 * n], out[2 * n:]


PACK_ROWS = 128


def _pack(parts):
    flat = []
    for p in parts:
        v = p.reshape(-1)
        flat.append(jnp.pad(v, (0, (-v.shape[0]) % BLK)))
    v = jnp.concatenate(flat)
    v = jnp.pad(v, (0, (-v.shape[0]) % (PACK_ROWS * BLK)))
    return v.reshape(-1, BLK)


def _unpack(buf, shapes):
    flat = buf.reshape(-1)
    out, off = [], 0
    for shp in shapes:
        size = math.prod(shp)
        out.append(flat[off:off + size].reshape(shp))
        off += size + (-size) % BLK
    return out


def kernel(x, meta_tokens, norm_mix, w_in, q_norm, k_norm, attn_sinks, lam_re, lam_im, log_dt, ssm_b_re, ssm_b_im, ssm_c_re, ssm_c_im, ssm_d, w_glu, attn_branch_norm, ssm_branch_norm, w_out, norm_ffn, w_ffn_in, w_ffn_out, loss_target, m_meta_tokens, m_norm_mix, m_w_in, m_q_norm, m_k_norm, m_attn_sinks, m_lam_re, m_lam_im, m_log_dt, m_ssm_b_re, m_ssm_b_im, m_ssm_c_re, m_ssm_c_im, m_ssm_d, m_w_glu, m_attn_branch_norm, m_ssm_branch_norm, m_w_out, m_norm_ffn, m_w_ffn_in, m_w_ffn_out, v_meta_tokens, v_norm_mix, v_w_in, v_q_norm, v_k_norm, v_attn_sinks, v_lam_re, v_lam_im, v_log_dt, v_ssm_b_re, v_ssm_b_im, v_ssm_c_re, v_ssm_c_im, v_ssm_d, v_w_glu, v_attn_branch_norm, v_ssm_branch_norm, v_w_out, v_norm_ffn, v_w_ffn_in, v_w_ffn_out):
    args = dict(locals())
    weights = {n: args[n] for n in WEIGHTS}
    mom_m = {n: args["m_" + n] for n in WEIGHTS}
    mom_v = {n: args["v_" + n] for n in WEIGHTS}

    x2d = x[0]
    target2d = loss_target[0]
    s_len = x2d.shape[0]
    l_dim = s_len + BLK
    tm_row = _pick(l_dim, 320)
    tm_mm = _pick(l_dim, 1040)
    tl_tn = _pick(l_dim, 2080)
    tm_ffn = _pick(l_dim, 640)
    tm_big = _pick(l_dim, 2080)
    tm_shift = _pick(l_dim, 640, BLK)

    shard_in = w_in[0].T.astype(BF16)
    shard_glu = w_glu[0].T.astype(BF16)
    shard_out = w_out[0].astype(BF16)
    shard_ffn_in = w_ffn_in[0].T.astype(BF16)
    shard_ffn_out = w_ffn_out[0].astype(BF16)
    shard_meta = meta_tokens.T
    me = _my_index()

    def landing(shard):
        return lax.empty((N_DEV * shard.shape[0], shard.shape[1]), shard.dtype)

    first = [shard_in, shard_meta]
    ga = _send_start("gather_start_a", first, [landing(s) for s in first], chunked=False)
    later = [shard_glu + ga[4][0:1, 0:1].astype(BF16), shard_out, shard_ffn_in, shard_ffn_out]
    later_fns = [_plain_rows, _plain_rows, _ffn_in_rows, _plain_rows]
    gb = _send_start("gather_start_b", later, [landing(s) for s in later], chunked=False,
                     row_fns=later_fns)

    nm_t = norm_mix + (ga[4][0:1, 0:1] + gb[4][0:1, 0:1])
    qn_t, kn_t = jnp.tile(q_norm, (1, N_Q_HEADS)), jnp.tile(k_norm, (1, N_KV_HEADS))
    e_mat = jnp.kron(jnp.eye(4, dtype=F32), jnp.ones((HEAD_DIM, HEAD_DIM), F32)).astype(BF16)

    def disc(lr, li, ldt, br, bi):
        return _discretize(lr[0], li[0], ldt[0], br[0], bi[0])

    (abar_re, abar_im, bbar_re, bbar_im), disc_vjp = jax.vjp(disc, lam_re, lam_im, log_dt, ssm_b_re, ssm_b_im)
    wb_re, wb_im = _block_diag_b(bbar_re).astype(BF16), _block_diag_b(bbar_im).astype(BF16)
    wc_re, wc_im = _block_diag_c(ssm_c_re[0]).astype(BF16), _block_diag_c(ssm_c_im[0]).astype(BF16)
    q_ssm = _pick(l_dim, 640, 64)
    tabs = _power_tables(abar_re.reshape(1, N_STATE), abar_im.reshape(1, N_STATE), q_ssm // GRP)

    h0, xn = _embed_norm(x2d, nm_t, tm_shift)
    wt_in, meta_t = _send_wait("gather_wait_a", ga[0], ga[1], ga[2], ga[3],
                               [xn, wb_re, wb_im, wc_re, wc_im, *tabs, e_mat, qn_t, kn_t], chunked=False)
    meta_pad = jnp.pad(meta_t.T, ((PAD, 0), (0, 0)))
    h0, xn = _embed_meta(meta_pad, nm_t, h0, xn)
    qkv_w, u_end = Q_W + 2 * KV_W, Q_W + 2 * KV_W + D_MODEL
    wt_in_p = jnp.concatenate([wt_in[u_end:], wt_in[qkv_w:u_end], wt_in[:qkv_w]], axis=0)
    qkv = _matmul("proj_qkv", xn, wt_in_p, nt=True, tm=tm_big, tn=qkv_w, tk=D_MODEL, n=qkv_w, w_off=2)
    u = _matmul("proj_u", xn, wt_in_p, nt=True, tm=tm_big, tn=D_MODEL, tk=D_MODEL, n=D_MODEL, w_off=2)
    gates = _matmul("proj_gates", xn, wt_in_p, nt=True, tm=tm_big, tn=D_MODEL, tk=D_MODEL, n=2 * D_MODEL, w_off=0,
                    out_dtype=BF16)
    qn, kf, vf = _qk_prep(qkv, qn_t, kn_t, e_mat, tm_row)
    attn, lse = _attn_fwd(qn, kf, vf, attn_sinks)
    y, z, s_re, s_im = _ssm_fwd_kb(u, wb_re, wb_im, wc_re, wc_im, ssm_d, tabs, q_ssm)
    wt_glu, w_out_f, wt_ffn_in, w_ffn_out_f = _send_wait("gather_wait_b", gb[0], gb[1], gb[2], gb[3], z,
                                                         chunked=False, row_fns=later_fns)
    zab = _matmul("glu_proj", z, wt_glu, nt=True, tm=tm_big, tn=1024, tk=D_MODEL, out_dtype=BF16)
    merged = _merge_fwd(attn, zab, gates, attn_branch_norm, ssm_branch_norm, tm_row)
    h1, hn = _matmul("out_proj", merged, w_out_f, nt=False, tm=tm_mm, tn=1024, tk=D_MODEL, res=h0,
                     norm_g=norm_ffn)
    gu, act = _ffn_in_swiglu(hn, wt_ffn_in, tm_ffn)
    h2 = _matmul("ffn_out", act, w_ffn_out_f, nt=False, tm=tm_mm, tn=1024, tk=D_FF, res=h1)
    dh2, dh2_b, loss_part = _loss_grad(h2, target2d, tm_shift)

    dgu = _d_act_swiglu(dh2_b, w_ffn_out_f, gu, tm_ffn)

    def exchange_start(name, grads_b, row_fns=None):
        return _send_start(name, grads_b, [lax.empty(g.shape, BF16) for g in grads_b], chunked=True,
                           row_fns=row_fns)

    g_ffn_out, g_ffn_out_b = _matmul_tn("g_ffn_out", act, dh2_b, tm=1408, tn=1024, tl=tl_tn)
    g_ffn_in_t, g_ffn_in_b = _matmul_tn("g_ffn_in", dgu, hn, tm=1408, tn=1024, tl=tl_tn)
    ffn_fns = [_ffn_in_rows, _plain_rows]
    ex1 = exchange_start("exchange_start_ffn", [g_ffn_in_b, g_ffn_out_b], ffn_fns)
    dhn = _matmul("d_hn", dgu, wt_ffn_in, nt=False, tm=tm_ffn, tn=1024, tk=2 * D_FF, out_dtype=BF16)
    dh1, dh1_b, g_norm_ffn = _norm_bwd_res("ffn_norm_bwd", h1, norm_ffn + ex1[4][0:1, 0:1], dhn, dh2, tm_row)
    dmerged = _matmul("d_merged", dh1_b, w_out_f, nt=True, tm=tm_big, tn=1024, tk=D_MODEL, out_dtype=BF16)
    dattn, dzab, dproj, g_abn, g_sbn, delta_rows = _merge_bwd(attn, zab, gates, attn_branch_norm, ssm_branch_norm,
                                                    dmerged, tm_row)
    g_out, g_out_b = _matmul_tn("g_out", merged, dh1_b, tm=1024, tn=1024, tl=tl_tn)
    dz = _matmul("d_z", dzab, wt_glu, nt=False, tm=tm_mm, tn=1024, tk=2 * D_MODEL, out_dtype=BF16)
    g_glu_t, g_glu_b = _matmul_tn("g_glu", dzab, z, tm=1024, tn=1024, tl=tl_tn)
    ex2 = exchange_start("exchange_start_mix", [g_glu_b, g_out_b])
    dproj, g_ssm_d, g_ar, g_ai, g_wbr, g_wbi, g_wcr, g_wci = _ssm_bwd_kb(
        dz, y, u, s_re, s_im, wb_re, wb_im, wc_re, wc_im, ssm_d + ex2[4][0:1, 0:1], tabs, q_ssm, dproj)
    dq, dkc, dkp, dkm, dvc, dvp, dvm, g_sinks = _attn_bwd(qn, kf, vf, attn_sinks, lse, delta_rows, dattn)
    dproj, g_qn_t, g_kn_t = _qk_bwd(qkv, qn_t, kn_t, e_mat, dq, dkc, dkp, dkm, dvc, dvp, dvm, dproj)
    g_lam_re, g_lam_im, g_log_dt, g_b_re, g_b_im = disc_vjp(
        (g_ar.reshape(SSM_GROUPS, SSM_STATE), g_ai.reshape(SSM_GROUPS, SSM_STATE),
         _block_diag_b_t(g_wbr), _block_diag_b_t(g_wbi)))
    small_grads = {
        "q_norm": g_qn_t.reshape(N_Q_HEADS, HEAD_DIM).sum(0)[None],
        "k_norm": g_kn_t.reshape(N_KV_HEADS, HEAD_DIM).sum(0)[None], "attn_sinks": g_sinks,
        "lam_re": g_lam_re, "lam_im": g_lam_im, "log_dt": g_log_dt, "ssm_b_re": g_b_re, "ssm_b_im": g_b_im,
        "ssm_c_re": _block_diag_c_t(g_wcr)[None], "ssm_c_im": _block_diag_c_t(g_wci)[None],
        "ssm_d": g_ssm_d, "attn_branch_norm": g_abn, "ssm_branch_norm": g_sbn, "norm_ffn": g_norm_ffn,
    }
    early = [n for n in SMALL if n != "norm_mix"]
    packed_e = _pack([small_grads[n] for n in early])
    gs_e = _send_start("small_start_a", [packed_e], [landing(packed_e)], chunked=False)

    blocks_g, blocks_u, blocks_q = 2 * D_MODEL // 512, D_MODEL // 512, qkv_w // 512
    back = lambda i: jnp.where(i < blocks_g, i + blocks_q + blocks_u,
                               jnp.where(i < blocks_g + blocks_u, i - blocks_g + blocks_q, i - blocks_g - blocks_u))
    g_in_t, g_in_b = _matmul_tn("g_in", dproj, xn, tm=512, tn=1024, tl=tl_tn, after=gs_e[4], out_rows=back)
    ex3 = exchange_start("exchange_start_in", [g_in_b])
    dxn = _matmul("d_xn", dproj, wt_in_p, nt=False, tm=tm_ffn, tn=1024, tk=IN_COLS, out_dtype=BF16)
    grad_x2d, dmeta_blk, g_norm_mix = _final_bwd(h0, nm_t + ex3[4][0:1, 0:1], dxn, dh1, _pick(s_len, 512, BLK))
    packed_l = _pack([g_norm_mix, dmeta_blk[PAD:], loss_part])
    gs_l = _send_start("small_start_b", [packed_l], [landing(packed_l)], chunked=False)
    grads, deltas, new_m, new_v = {}, {}, {}, {}

    recv_ffn_in, recv_ffn_out = _send_wait("exchange_wait_ffn", ex1[0], ex1[1], ex1[2], ex1[3], gs_l[4],
                                           chunked=True, row_fns=ffn_fns)
    recv_glu, recv_out = _send_wait("exchange_wait_mix", ex2[0], ex2[1], ex2[2], ex2[3], recv_ffn_in,
                                    chunked=True)
    (recv_in,) = _send_wait("exchange_wait_in", ex3[0], ex3[1], ex3[2], ex3[3], recv_glu, chunked=True)
    big = [("w_in", g_in_t, True, recv_in, _plain_rows), ("w_glu", g_glu_t, True, recv_glu, _plain_rows),
           ("w_out", g_out, False, recv_out, _plain_rows), ("w_ffn_in", g_ffn_in_t, True, recv_ffn_in, _ffn_in_rows),
           ("w_ffn_out", g_ffn_out, False, recv_ffn_out, _plain_rows)]
    for name, g_full, transposed, recv, row_fn in big:
        m_rows = g_full.shape[0] // N_DEV
        own = lax.dynamic_slice(g_full, (row_fn(me, m_rows), 0), (m_rows, g_full.shape[1]))
        if name == "w_glu":
            grads[name] = _sum_slots("sum_" + name, recv, own).T[None]
            continue
        shp = weights[name].shape
        if transposed:
            as2d, back = (lambda a: a.reshape(shp[-2], shp[-1]).T), (lambda a: a.T.reshape(shp))
        else:
            as2d, back = (lambda a: a.reshape(shp[-2], shp[-1])), (lambda a: a.reshape(shp))
        g_shard, d, nm, nv = _sum_slots("sum_adamw_" + name, recv, own,
                                        (as2d(weights[name]), as2d(mom_m[name]), as2d(mom_v[name])))
        grads[name], deltas[name], new_m[name], new_v[name] = back(g_shard), back(d), back(nm), back(nv)
        last = d

    def adamw_2d(name):
        shp = weights[name].shape
        as2d = lambda a: a.reshape(shp[-2], shp[-1])
        d, nm, nv = _adamw("adamw_" + name, as2d(weights[name]), as2d(grads[name]), as2d(mom_m[name]),
                           as2d(mom_v[name]))
        deltas[name], new_m[name], new_v[name] = d.reshape(shp), nm.reshape(shp), nv.reshape(shp)

    adamw_2d("w_glu")

    def small_sum(tag, gs, packed, after):
        (gathered,) = _send_wait("small_wait_" + tag, gs[0], gs[1], gs[2], gs[3], after, chunked=False)
        return _sum_gathered("sum_small_" + tag, gathered, packed.shape[0])

    def small_adamw(tag, names):
        view = lambda n, a: jnp.swapaxes(a, -1, -2) if n.startswith("ssm_b") else a
        d, nm, nv = _adamw_many("adamw_small_" + tag, [view(n, weights[n]) for n in names],
                                [view(n, grads[n]) for n in names], [view(n, mom_m[n]) for n in names],
                                [view(n, mom_v[n]) for n in names])
        deltas.update((n, view(n, a)) for n, a in zip(names, d))
        new_m.update((n, view(n, a)) for n, a in zip(names, nm))
        new_v.update((n, view(n, a)) for n, a in zip(names, nv))

    g_sum_e = small_sum("a", gs_e, packed_e, last)
    grads.update(zip(early, _unpack(g_sum_e, [weights[n].shape for n in early])))
    wide = [n for n in early if n.startswith(("ssm_b", "ssm_c"))]
    small_adamw("wide", wide)
    g_sum_l = small_sum("b", gs_l, packed_l, g_sum_e)
    grads["norm_mix"], g_meta, loss_sum = _unpack(g_sum_l, [weights["norm_mix"].shape, (N_META, D_MODEL), (1, 1)])
    small_adamw("rest", [n for n in SMALL if n not in wide])
    grads["meta_tokens"] = lax.dynamic_slice(g_meta, (0, me * BLK), (N_META, BLK))
    adamw_2d("meta_tokens")

    loss = loss_sum[0, 0]
    return (loss, grad_x2d[None], *[grads[n] for n in WEIGHTS], *[deltas[n] for n in WEIGHTS],
            *[new_m[n] for n in WEIGHTS], *[new_v[n] for n in WEIGHTS])
```

```python
import math

import numpy as np
import jax
import jax.numpy as jnp
from jax import lax
from jax.experimental import pallas as pl
from jax.experimental.pallas import tpu as pltpu

F32 = jnp.float32
BF16 = jnp.bfloat16

D_MODEL = 1024
N_META = 16
HEAD_DIM = 64
N_Q_HEADS = 16
N_KV_HEADS = 4
Q_W = N_Q_HEADS * HEAD_DIM
KV_W = N_KV_HEADS * HEAD_DIM
SSM_GROUPS = 64
SSM_GROUP_CH = 16
SSM_STATE = 64
N_STATE = SSM_GROUPS * SSM_STATE
D_FF = 2816
IN_COLS = Q_W + 2 * KV_W + 3 * D_MODEL
EPS = 1e-6
BLK = 128
PAD = BLK - N_META
N_DEV = 8
NEG = -1e30
SSM_KB = 8
ST_KB = N_STATE // SSM_KB
LB_KB = ST_KB // BLK
N_LB = N_STATE // BLK

ADAM_LR = 0.001
ADAM_B1 = 0.9
ADAM_B2 = 0.999
ADAM_EPS = 1e-08
ADAM_WD = 0.01
ADAM_STEP = 10

VMEM_LIMIT = 48 * 1024 * 1024
MESH = pl.DeviceIdType.MESH

SMALL = ["norm_mix", "q_norm", "k_norm", "attn_sinks", "lam_re", "lam_im", "log_dt", "ssm_b_re", "ssm_b_im",
         "ssm_c_re", "ssm_c_im", "ssm_d", "attn_branch_norm", "ssm_branch_norm", "norm_ffn"]
WEIGHTS = ["meta_tokens", "norm_mix", "w_in", "q_norm", "k_norm", "attn_sinks", "lam_re", "lam_im", "log_dt",
           "ssm_b_re", "ssm_b_im", "ssm_c_re", "ssm_c_im", "ssm_d", "w_glu", "attn_branch_norm",
           "ssm_branch_norm", "w_out", "norm_ffn", "w_ffn_in", "w_ffn_out"]


def _params(**kw):
    return pltpu.CompilerParams(vmem_limit_bytes=VMEM_LIMIT, **kw)


def _pick(n, cap, mult=16):
    best = None
    for d in range(mult, min(n, cap) + 1, mult):
        if n % d == 0:
            best = d
    assert best is not None, (n, cap, mult)
    return best


def _my_index():
    return 4 * lax.axis_index("x") + 2 * lax.axis_index("y") + lax.axis_index("c")


def _rms(x, g):
    r = lax.rsqrt(jnp.mean(x * x, axis=-1, keepdims=True) + EPS)
    return x * r * g


def _rms_bwd(x, g, dy):
    r = lax.rsqrt(jnp.mean(x * x, axis=-1, keepdims=True) + EPS)
    t = dy * g
    dx = r * t - x * (r * r * r) * jnp.mean(t * x, axis=-1, keepdims=True)
    dg = jnp.sum(dy * (x * r), axis=0, keepdims=True)
    return dx, dg


def _sigmoid(x):
    return jax.nn.sigmoid(x)


def _gelu(x):
    k = math.sqrt(2.0 / math.pi)
    return 0.5 * x * (1.0 + jnp.tanh(k * (x + 0.044715 * (x * x * x))))


def _gelu_grad(x):
    k = math.sqrt(2.0 / math.pi)
    t = jnp.tanh(k * (x + 0.044715 * (x * x * x)))
    return 0.5 * (1.0 + t) + 0.5 * x * (1.0 - t * t) * (k * (1.0 + 3.0 * 0.044715 * (x * x)))


def _head_mean(x, e_ref):
    hi = x.astype(BF16)
    lo = (x - hi.astype(F32)).astype(BF16)
    e = e_ref[...]
    out = []
    for b in range(x.shape[1] // 256):
        sl = slice(256 * b, 256 * b + 256)
        s = (jnp.dot(hi[:, sl], e, preferred_element_type=F32)
             + jnp.dot(lo[:, sl], e, preferred_element_type=F32))
        out.append(s)
    s = out[0] if len(out) == 1 else jnp.concatenate(out, axis=1)
    return s * (1.0 / HEAD_DIM)


def _head_rms(x, g, e_ref):
    r = lax.rsqrt(_head_mean(x * x, e_ref) + EPS)
    return x * r * g


def _head_rms_bwd(x, g, dy, e_ref):
    r = lax.rsqrt(_head_mean(x * x, e_ref) + EPS)
    t = dy * g
    dx = r * t - x * (r * r * r) * _head_mean(t * x, e_ref)
    dg = jnp.sum(dy * (x * r), axis=0, keepdims=True)
    return dx, dg


def _lane_half(shape):
    lane = lax.broadcasted_iota(jnp.int32, shape, len(shape) - 1)
    return (lane >> 6) & 1


def _matmul(name, a, w, *, nt, tm, tn, tk, n=None, w_off=0, res=None, norm_g=None, out_dtype=F32):
    m_dim, k_dim = a.shape
    n_dim = n if n is not None else (w.shape[0] if nt else w.shape[1])
    gm, gn, gk = m_dim // tm, n_dim // tn, k_dim // tk
    assert gm * tm == m_dim and gn * tn == n_dim and gk * tk == k_dim, (name, a.shape, w.shape, tm, tn, tk)
    assert norm_g is None or tn == n_dim
    direct = out_dtype == F32
    dn = (((1,), (1,)), ((), ())) if nt else (((1,), (0,)), ((), ()))

    def body(*refs):
        refs = list(refs)
        a_ref, w_ref = refs[0], refs[1]
        pos = 2
        r_ref = g_ref = on_ref = None
        if res is not None:
            r_ref, pos = refs[pos], pos + 1
        if norm_g is not None:
            g_ref, pos = refs[pos], pos + 1
        o_ref, pos = refs[pos], pos + 1
        if norm_g is not None:
            on_ref, pos = refs[pos], pos + 1
        part = lax.dot_general(a_ref[...], w_ref[...], dn, preferred_element_type=F32)
        if gk == 1:
            r = part if r_ref is None else r_ref[...] + part
            o_ref[...] = r.astype(out_dtype)
            if on_ref is not None:
                on_ref[...] = _rms(r, g_ref[...]).astype(BF16)
            return
        acc = o_ref if direct else refs[pos]
        k = pl.program_id(2)

        @pl.when(k == 0)
        def _():
            acc[...] = part if r_ref is None or not direct else r_ref[...] + part

        @pl.when(k > 0)
        def _():
            acc[...] += part

        @pl.when(k == gk - 1)
        def _():
            if not direct:
                r = acc[...]
                if r_ref is not None:
                    r = r_ref[...] + r
                o_ref[...] = r.astype(out_dtype)
            if on_ref is not None:
                on_ref[...] = _rms(o_ref[...].astype(F32), g_ref[...]).astype(BF16)

    if nt:
        w_spec = pl.BlockSpec((tn, tk), lambda i, j, k: (j + w_off, k))
    else:
        w_spec = pl.BlockSpec((tk, tn), lambda i, j, k: (k, j))
    in_specs = [pl.BlockSpec((tm, tk), lambda i, j, k: (i, k)), w_spec]
    args = [a, w]
    out_spec = pl.BlockSpec((tm, tn), lambda i, j, k: (i, j))
    out_specs, out_shape = [out_spec], [jax.ShapeDtypeStruct((m_dim, n_dim), out_dtype)]
    if res is not None:
        in_specs.append(out_spec)
        args.append(res)
    if norm_g is not None:
        in_specs.append(pl.BlockSpec((1, tn), lambda i, j, k: (0, 0)))
        args.append(norm_g)
        out_specs.append(out_spec)
        out_shape.append(jax.ShapeDtypeStruct((m_dim, n_dim), BF16))
    out = pl.pallas_call(
        body, name=name, grid=(gm, gn, gk),
        in_specs=in_specs, out_specs=out_specs, out_shape=out_shape,
        scratch_shapes=[] if direct or gk == 1 else [pltpu.VMEM((tm, tn), F32)],
        compiler_params=_params(dimension_semantics=("parallel", "parallel", "arbitrary")),
    )(*args)
    return out if norm_g is not None else out[0]


def _matmul_tn(name, a, b, *, tm, tn, tl, after=None, out_rows=None):
    l_dim, m_dim = a.shape
    n_dim = b.shape[1]
    gm, gn, gl = m_dim // tm, n_dim // tn, l_dim // tl
    assert gm * tm == m_dim and gn * tn == n_dim and gl * tl == l_dim, (name, a.shape, b.shape, tm, tn, tl)

    def body(*refs):
        a_ref, b_ref = refs[0], refs[1]
        o_ref, ob_ref = refs[-2], refs[-1]

        @pl.when(pl.program_id(2) == 0)
        def _():
            o_ref[...] = jnp.zeros_like(o_ref)

        o_ref[...] += lax.dot_general(a_ref[...], b_ref[...], (((0,), (0,)), ((), ())),
                                      preferred_element_type=F32)

        @pl.when(pl.program_id(2) == gl - 1)
        def _():
            ob_ref[...] = o_ref[...].astype(BF16)

    out_row = out_rows if out_rows is not None else (lambda i: i)
    out_spec = pl.BlockSpec((tm, tn), lambda i, j, l: (out_row(i), j))
    in_specs = [pl.BlockSpec((tl, tm), lambda i, j, l: (l, i)), pl.BlockSpec((tl, tn), lambda i, j, l: (l, j))]
    args = [a, b]
    if after is not None:
        in_specs.append(pl.BlockSpec(memory_space=pl.ANY))
        args.append(after)
    return pl.pallas_call(
        body, name=name, grid=(gm, gn, gl),
        in_specs=in_specs,
        out_specs=[out_spec, out_spec],
        out_shape=[jax.ShapeDtypeStruct((m_dim, n_dim), F32), jax.ShapeDtypeStruct((m_dim, n_dim), BF16)],
        compiler_params=_params(dimension_semantics=("parallel", "parallel", "arbitrary")),
    )(*args)


def _row_spec(tm, cols, f=None):
    if f is None:
        return pl.BlockSpec((tm, cols), lambda i: (i, 0))
    return pl.BlockSpec((tm, cols), lambda i: (f(i), 0))


def _full_spec(shape):
    nd = len(shape)
    return pl.BlockSpec(shape, lambda i: (0,) * nd)


def _shifted_specs(n_sub, n_blocks):
    return [_row_spec(BLK, D_MODEL, (lambda i, k=k: jnp.clip(n_sub * i - 1 + k, 0, n_blocks - 1)))
            for k in range(n_sub)]


def _embed_norm(x2d, g, tm):
    s_len = x2d.shape[0]
    l_dim = s_len + BLK
    n_sub = tm // BLK

    def body(*refs):
        x_refs, g_ref, h_ref, xn_ref = refs[:n_sub], refs[n_sub], refs[n_sub + 1], refs[n_sub + 2]
        i = pl.program_id(0)
        for k in range(n_sub):
            rows = slice(BLK * k, BLK * k + BLK)
            h = x_refs[k][...] * jnp.where(n_sub * i + k >= 1, 1.0, 0.0)
            h_ref[rows, :] = h
            xn_ref[rows, :] = _rms(h, g_ref[...]).astype(BF16)

    return pl.pallas_call(
        body, name="embed_norm", grid=(l_dim // tm,),
        in_specs=_shifted_specs(n_sub, s_len // BLK) + [_full_spec((1, D_MODEL))],
        out_specs=[_row_spec(tm, D_MODEL), _row_spec(tm, D_MODEL)],
        out_shape=[jax.ShapeDtypeStruct((l_dim, D_MODEL), F32),
                   jax.ShapeDtypeStruct((l_dim, D_MODEL), BF16)],
        compiler_params=_params(),
    )(*([x2d] * n_sub), g)


def _embed_meta(meta_pad, g, h0, xn):
    def body(mp_ref, g_ref, h_in, xn_in, h_ref, xn_ref):
        h_ref[...] = mp_ref[...]
        xn_ref[...] = _rms(mp_ref[...], g_ref[...]).astype(BF16)

    any_spec = pl.BlockSpec(memory_space=pl.ANY)
    return pl.pallas_call(
        body, name="embed_meta", grid=(1,),
        in_specs=[_full_spec((BLK, D_MODEL)), _full_spec((1, D_MODEL)), any_spec, any_spec],
        out_specs=[_row_spec(BLK, D_MODEL), _row_spec(BLK, D_MODEL)],
        out_shape=[jax.ShapeDtypeStruct(h0.shape, F32), jax.ShapeDtypeStruct(xn.shape, BF16)],
        input_output_aliases={2: 0, 3: 1},
        compiler_params=_params(),
    )(meta_pad, g, h0, xn)


KVX_W = 2 * N_KV_HEADS * BLK


def _qk_prep(qkv, q_norm_t, k_norm_t, e_mat, tm):
    l_dim = qkv.shape[0]

    def body(x_ref, qg_ref, kg_ref, e_ref, q_ref, kf_ref, vf_ref):
        x = x_ref[...]
        q = _head_rms(x[:, :Q_W], qg_ref[...], e_ref) * (HEAD_DIM ** -0.5)
        q_ref[...] = q.astype(BF16)
        k = _head_rms(x[:, Q_W:Q_W + KV_W], kg_ref[...], e_ref)
        v = x[:, Q_W + KV_W:Q_W + 2 * KV_W]
        half = _lane_half((tm, BLK))
        for src, dst in ((k, kf_ref), (v, vf_ref)):
            for kv in range(N_KV_HEADS):
                blk = src[:, BLK * (kv // 2):BLK * (kv // 2) + BLK]
                swapped = pltpu.roll(blk, HEAD_DIM, axis=1)
                for e in range(2):
                    val = blk if kv % 2 == e else swapped
                    idx = 2 * kv + e
                    dst[:, BLK * idx:BLK * idx + BLK] = jnp.where(half == e, val, 0.0).astype(BF16)

    return pl.pallas_call(
        body, name="qk_prep", grid=(l_dim // tm,),
        in_specs=[_row_spec(tm, Q_W + 2 * KV_W), _full_spec((1, Q_W)), _full_spec((1, KV_W)),
                  _full_spec((256, 256))],
        out_specs=[_row_spec(tm, Q_W), _row_spec(tm, KVX_W), _row_spec(tm, KVX_W)],
        out_shape=[jax.ShapeDtypeStruct((l_dim, Q_W), BF16),
                   jax.ShapeDtypeStruct((l_dim, KVX_W), BF16),
                   jax.ShapeDtypeStruct((l_dim, KVX_W), BF16)],
        compiler_params=_params(),
    )(qkv, q_norm_t, k_norm_t, e_mat)


def _merge_fwd(attn, zab, gates, abn, sbn, tm):
    l_dim = attn.shape[0]

    def body(a_ref, z_ref, g_ref, an_ref, sn_ref, o_ref):
        z = z_ref[...].astype(F32)
        g = g_ref[...].astype(F32)
        ssm = z[:, :D_MODEL] * _sigmoid(z[:, D_MODEL:])
        merged = (_sigmoid(g[:, :D_MODEL]) * _rms(a_ref[...], an_ref[...])
                  + _sigmoid(g[:, D_MODEL:]) * _rms(ssm, sn_ref[...]))
        o_ref[...] = merged.astype(BF16)

    return pl.pallas_call(
        body, name="merge_fwd", grid=(l_dim // tm,),
        in_specs=[_row_spec(tm, D_MODEL), _row_spec(tm, 2 * D_MODEL), _row_spec(tm, 2 * D_MODEL),
                  _full_spec((1, D_MODEL)), _full_spec((1, D_MODEL))],
        out_specs=_row_spec(tm, D_MODEL),
        out_shape=jax.ShapeDtypeStruct((l_dim, D_MODEL), BF16),
        compiler_params=_params(),
    )(attn, zab, gates, abn, sbn)


def _merge_bwd(attn, zab, gates, abn, sbn, dmerged, tm):
    l_dim = attn.shape[0]

    def body(a_ref, z_ref, g_ref, an_ref, sn_ref, dm_ref, da_ref, dz_ref, dg_ref, dan_ref, dsn_ref, dl_ref):
        @pl.when(pl.program_id(0) == 0)
        def _():
            dan_ref[...] = jnp.zeros_like(dan_ref)
            dsn_ref[...] = jnp.zeros_like(dsn_ref)

        z = z_ref[...].astype(F32)
        g = g_ref[...].astype(F32)
        dm = dm_ref[...].astype(F32)
        attn_v = a_ref[...]
        za, zb = z[:, :D_MODEL], z[:, D_MODEL:]
        sb = _sigmoid(zb)
        ssm = za * sb
        s_ga, s_gs = _sigmoid(g[:, :D_MODEL]), _sigmoid(g[:, D_MODEL:])
        a_n = _rms(attn_v, an_ref[...])
        s_n = _rms(ssm, sn_ref[...])
        dg_ref[:, :D_MODEL] = (dm * a_n * s_ga * (1.0 - s_ga)).astype(BF16)
        dg_ref[:, D_MODEL:] = (dm * s_n * s_gs * (1.0 - s_gs)).astype(BF16)
        dattn, dan = _rms_bwd(attn_v, an_ref[...], dm * s_ga)
        dssm, dsn = _rms_bwd(ssm, sn_ref[...], dm * s_gs)
        da = dattn.astype(BF16)
        da_ref[...] = da
        prod = da.astype(F32) * attn_v
        hi = prod.astype(BF16)
        lo = (prod - hi.astype(F32)).astype(BF16)
        ones = (lax.broadcasted_iota(jnp.int32, (D_MODEL, BLK), 0) // HEAD_DIM
                == lax.broadcasted_iota(jnp.int32, (D_MODEL, BLK), 1)).astype(BF16)
        dl_ref[...] = (jnp.dot(hi, ones, preferred_element_type=F32)
                       + jnp.dot(lo, ones, preferred_element_type=F32))
        dz_ref[:, :D_MODEL] = (dssm * sb).astype(BF16)
        dz_ref[:, D_MODEL:] = (dssm * za * sb * (1.0 - sb)).astype(BF16)
        dan_ref[...] += dan
        dsn_ref[...] += dsn

    return pl.pallas_call(
        body, name="merge_bwd", grid=(l_dim // tm,),
        in_specs=[_row_spec(tm, D_MODEL), _row_spec(tm, 2 * D_MODEL), _row_spec(tm, 2 * D_MODEL),
                  _full_spec((1, D_MODEL)), _full_spec((1, D_MODEL)), _row_spec(tm, D_MODEL)],
        out_specs=[_row_spec(tm, D_MODEL), _row_spec(tm, 2 * D_MODEL), _row_spec(tm, 2 * D_MODEL),
                   _full_spec((1, D_MODEL)), _full_spec((1, D_MODEL)), _row_spec(tm, BLK)],
        out_shape=[jax.ShapeDtypeStruct((l_dim, D_MODEL), BF16),
                   jax.ShapeDtypeStruct((l_dim, 2 * D_MODEL), BF16),
                   jax.ShapeDtypeStruct((l_dim, IN_COLS), BF16),
                   jax.ShapeDtypeStruct((1, D_MODEL), F32), jax.ShapeDtypeStruct((1, D_MODEL), F32),
                   jax.ShapeDtypeStruct((l_dim, BLK), F32)],
        compiler_params=_params(),
    )(attn, zab, gates, abn, sbn, dmerged)


FF_TILE = D_FF // 2


def _ffn_in_swiglu(hn, wt_ffn_in, tm):
    l_dim = hn.shape[0]
    nt = (((1,), (1,)), ((), ()))

    def body(a_ref, w_ref, gu_ref, act_ref):
        r = lax.dot_general(a_ref[...], w_ref[...], nt, preferred_element_type=F32)
        gate, up = r[:, :FF_TILE], r[:, FF_TILE:]
        gu_ref[...] = r.astype(BF16)
        act_ref[...] = (gate * _sigmoid(gate) * up).astype(BF16)

    return pl.pallas_call(
        body, name="ffn_in_swiglu", grid=(l_dim // tm, 2),
        in_specs=[pl.BlockSpec((tm, D_MODEL), lambda i, j: (i, 0)),
                  pl.BlockSpec((2 * FF_TILE, D_MODEL), lambda i, j: (j, 0))],
        out_specs=[pl.BlockSpec((tm, 2 * FF_TILE), lambda i, j: (i, j)),
                   pl.BlockSpec((tm, FF_TILE), lambda i, j: (i, j))],
        out_shape=[jax.ShapeDtypeStruct((l_dim, 2 * D_FF), BF16), jax.ShapeDtypeStruct((l_dim, D_FF), BF16)],
        compiler_params=_params(dimension_semantics=("parallel", "parallel")),
    )(hn, wt_ffn_in)


def _d_act_swiglu(dh2_b, w_ffn_out, gu, tm):
    l_dim = dh2_b.shape[0]
    nt = (((1,), (1,)), ((), ()))

    def body(d_ref, w_ref, gu_ref, o_ref):
        d = lax.dot_general(d_ref[...], w_ref[...], nt, preferred_element_type=F32)
        gate = gu_ref[:, :FF_TILE].astype(F32)
        up = gu_ref[:, FF_TILE:].astype(F32)
        s = _sigmoid(gate)
        o_ref[:, :FF_TILE] = (d * up * (s * (1.0 + gate * (1.0 - s)))).astype(BF16)
        o_ref[:, FF_TILE:] = (d * (gate * s)).astype(BF16)

    return pl.pallas_call(
        body, name="d_act_swiglu", grid=(l_dim // tm, 2),
        in_specs=[pl.BlockSpec((tm, D_MODEL), lambda i, j: (i, 0)),
                  pl.BlockSpec((FF_TILE, D_MODEL), lambda i, j: (j, 0)),
                  pl.BlockSpec((tm, 2 * FF_TILE), lambda i, j: (i, j))],
        out_specs=pl.BlockSpec((tm, 2 * FF_TILE), lambda i, j: (i, j)),
        out_shape=jax.ShapeDtypeStruct((l_dim, 2 * D_FF), BF16),
        compiler_params=_params(dimension_semantics=("parallel", "parallel")),
    )(dh2_b, w_ffn_out, gu)


def _loss_grad(h2, target2d, tm):
    l_dim = h2.shape[0]
    n_sub = tm // BLK

    def body(*refs):
        h_ref, t_refs = refs[0], refs[1:1 + n_sub]
        d_ref, db_ref, loss_ref = refs[1 + n_sub:]
        i = pl.program_id(0)

        @pl.when(i == 0)
        def _():
            loss_ref[...] = jnp.zeros_like(loss_ref)

        for k in range(n_sub):
            rows = slice(BLK * k, BLK * k + BLK)
            real = jnp.where(n_sub * i + k >= 1, 1.0, 0.0)
            err = (h_ref[rows, :] - t_refs[k][...]) * real
            d = err * (1.0 / D_MODEL)
            d_ref[rows, :] = d
            db_ref[rows, :] = d.astype(BF16)
            loss_ref[...] += 0.5 * jnp.sum(jnp.mean(err * err, axis=-1, keepdims=True), axis=0, keepdims=True)

    return pl.pallas_call(
        body, name="loss_grad", grid=(l_dim // tm,),
        in_specs=[_row_spec(tm, D_MODEL)] + _shifted_specs(n_sub, target2d.shape[0] // BLK),
        out_specs=[_row_spec(tm, D_MODEL), _row_spec(tm, D_MODEL), _full_spec((1, 1))],
        out_shape=[jax.ShapeDtypeStruct((l_dim, D_MODEL), F32), jax.ShapeDtypeStruct((l_dim, D_MODEL), BF16),
                   jax.ShapeDtypeStruct((1, 1), F32)],
        compiler_params=_params(),
    )(h2, *([target2d] * n_sub))


def _norm_bwd_res(name, h, g, dy, dres, tm):
    l_dim = h.shape[0]

    def body(h_ref, g_ref, dy_ref, dr_ref, o_ref, ob_ref, dg_ref):
        @pl.when(pl.program_id(0) == 0)
        def _():
            dg_ref[...] = jnp.zeros_like(dg_ref)

        dx, dg = _rms_bwd(h_ref[...], g_ref[...], dy_ref[...].astype(F32))
        out = dr_ref[...] + dx
        o_ref[...] = out
        ob_ref[...] = out.astype(BF16)
        dg_ref[...] += dg

    return pl.pallas_call(
        body, name=name, grid=(l_dim // tm,),
        in_specs=[_row_spec(tm, D_MODEL), _full_spec((1, D_MODEL)), _row_spec(tm, D_MODEL), _row_spec(tm, D_MODEL)],
        out_specs=[_row_spec(tm, D_MODEL), _row_spec(tm, D_MODEL), _full_spec((1, D_MODEL))],
        out_shape=[jax.ShapeDtypeStruct((l_dim, D_MODEL), F32), jax.ShapeDtypeStruct((l_dim, D_MODEL), BF16),
                   jax.ShapeDtypeStruct((1, D_MODEL), F32)],
        compiler_params=_params(),
    )(h, g, dy, dres)


def _final_bwd(h0, g, dxn, dh1, tm):
    l_dim = h0.shape[0]
    n_sub = tm // BLK
    n_tiles = (l_dim - BLK) // tm

    def sub_specs():
        return [_row_spec(BLK, D_MODEL, (lambda j, k=k: jnp.where(j < n_tiles, n_sub * j + 1 + k, 0)))
                for k in range(n_sub)]

    def body(*refs):
        h_refs, g_ref = refs[:n_sub], refs[n_sub]
        dy_refs, dr_refs = refs[n_sub + 1:2 * n_sub + 1], refs[2 * n_sub + 1:3 * n_sub + 1]
        gx_ref, gm_ref, dg_ref = refs[3 * n_sub + 1:]
        j = pl.program_id(0)

        @pl.when(j == 0)
        def _():
            dg_ref[...] = jnp.zeros_like(dg_ref)

        def block(k):
            dx, dg = _rms_bwd(h_refs[k][...], g_ref[...], dy_refs[k][...].astype(F32))
            dg_ref[...] += dg
            return dr_refs[k][...] + dx

        @pl.when(j < n_tiles)
        def _():
            for k in range(n_sub):
                gx_ref[BLK * k:BLK * k + BLK, :] = block(k)

        @pl.when(j == n_tiles)
        def _():
            gm_ref[...] = block(0)

    return pl.pallas_call(
        body, name="final_bwd", grid=(n_tiles + 1,),
        in_specs=sub_specs() + [_full_spec((1, D_MODEL))] + sub_specs() + sub_specs(),
        out_specs=[_row_spec(tm, D_MODEL, lambda j: jnp.minimum(j, n_tiles - 1)), _full_spec((BLK, D_MODEL)),
                   _full_spec((1, D_MODEL))],
        out_shape=[jax.ShapeDtypeStruct((l_dim - BLK, D_MODEL), F32), jax.ShapeDtypeStruct((BLK, D_MODEL), F32),
                   jax.ShapeDtypeStruct((1, D_MODEL), F32)],
        compiler_params=_params(),
    )(*([h0] * n_sub), g, *([dxn] * n_sub), *([dh1] * n_sub))


def _attn_bias():
    qi = np.arange(2 * BLK)[:, None] % BLK
    col = np.arange(3 * BLK)[None, :]
    kj, part = col % BLK, col // BLK
    out = []
    for n in range(3):
        meta_ok = (part == 0) & (kj >= PAD) & (n >= 1)
        prev_ok = (part == 1) & (kj > qi) & (n >= 2)
        cur_ok = (part == 2) & (kj <= qi) & ((n >= 1) | (kj >= PAD))
        out.append(np.where(meta_ok | prev_ok | cur_ok, 0.0, NEG))
    return jnp.asarray(np.stack(out), F32)


def _bias_spec():
    return pl.BlockSpec((1, 2 * BLK, 3 * BLK), lambda i: (jnp.minimum(i, 2), 0, 0))


def _attn_scores(q_ref, kwin, sk_ref, bias, kv, e):
    qs = jnp.concatenate([q_ref[:, BLK * (2 * kv):BLK * (2 * kv) + BLK],
                          q_ref[:, BLK * (2 * kv + 1):BLK * (2 * kv + 1) + BLK]], axis=0)
    s = lax.dot_general(qs, kwin, (((1,), (1,)), ((), ())), preferred_element_type=F32)
    h0 = 4 * kv + e
    row = lax.broadcasted_iota(jnp.int32, (2 * BLK, 1), 0)
    sink = jnp.where(row < BLK, sk_ref[:, h0:h0 + 1], sk_ref[:, h0 + 2:h0 + 3])
    return qs, s + bias, sink


def _attn_specs(nb):
    prev = lambda i: jnp.maximum(i - 1, 0)
    zero = lambda i: 0
    kv_specs = [_row_spec(BLK, KVX_W, zero), _row_spec(BLK, KVX_W, prev), _row_spec(BLK, KVX_W)]
    return kv_specs


def _attn_fwd(qn, kf, vf, sinks):
    l_dim = qn.shape[0]
    nb = l_dim // BLK

    def body(q_ref, km_ref, kp_ref, kc_ref, vm_ref, vp_ref, vc_ref, sk_ref, bias_ref, o_ref, lse_ref):
        bias = bias_ref[0]
        lane = lax.broadcasted_iota(jnp.int32, (BLK, BLK), 1)
        lse_all = jnp.zeros((BLK, BLK), F32)
        for kv in range(N_KV_HEADS):
            outs = []
            for e in range(2):
                sl = slice(BLK * (2 * kv + e), BLK * (2 * kv + e) + BLK)
                kwin = jnp.concatenate([km_ref[:, sl], kp_ref[:, sl], kc_ref[:, sl]], axis=0)
                vwin = jnp.concatenate([vm_ref[:, sl], vp_ref[:, sl], vc_ref[:, sl]], axis=0)
                _, s, sink = _attn_scores(q_ref, kwin, sk_ref, bias, kv, e)
                m = jnp.maximum(jnp.max(s, axis=-1, keepdims=True), sink)
                ex = jnp.exp(s - m)
                den = jnp.sum(ex, axis=-1, keepdims=True) + jnp.exp(sink - m)
                p = ex * (1.0 / den)
                outs.append(jnp.dot(p.astype(BF16), vwin, preferred_element_type=F32))
                lse = m + jnp.log(den)
                lse_all = jnp.where(lane == 4 * kv + e, lse[:BLK], lse_all)
                lse_all = jnp.where(lane == 4 * kv + 2 + e, lse[BLK:], lse_all)
            o = outs[0] + outs[1]
            o_ref[:, BLK * (2 * kv):BLK * (2 * kv) + BLK] = o[:BLK]
            o_ref[:, BLK * (2 * kv + 1):BLK * (2 * kv + 1) + BLK] = o[BLK:]
        lse_ref[...] = lse_all

    kv_specs = _attn_specs(nb)
    return pl.pallas_call(
        body, name="attn_fwd", grid=(nb,),
        in_specs=[_row_spec(BLK, Q_W)] + kv_specs + kv_specs + [_full_spec((1, N_Q_HEADS)), _bias_spec()],
        out_specs=[_row_spec(BLK, Q_W), _row_spec(BLK, BLK)],
        out_shape=[jax.ShapeDtypeStruct((l_dim, Q_W), F32), jax.ShapeDtypeStruct((l_dim, BLK), F32)],
        compiler_params=_params(),
    )(qn, kf, kf, kf, vf, vf, vf, sinks, _attn_bias())


def _attn_bwd(qn, kf, vf, sinks, lse, delta_rows, dattn):
    l_dim = qn.shape[0]
    nb = l_dim // BLK
    wide = KV_W
    tn = (((0,), (0,)), ((), ()))
    nt = (((1,), (1,)), ((), ()))

    def body(q_ref, km_ref, kp_ref, kc_ref, vm_ref, vp_ref, vc_ref, sk_ref, bias_ref, lse_ref, dl_ref, do_ref,
             dq_ref, dkc_ref, dkp_ref, dkm_ref, dvc_ref, dvp_ref, dvm_ref, dsk_ref):
        @pl.when(pl.program_id(0) == 0)
        def _():
            dkm_ref[...] = jnp.zeros_like(dkm_ref)
            dvm_ref[...] = jnp.zeros_like(dvm_ref)
            dsk_ref[...] = jnp.zeros_like(dsk_ref)

        bias = bias_ref[0]
        half2 = _lane_half((2 * BLK, BLK))
        half3 = _lane_half((3 * BLK, BLK))
        folded_k, folded_v = [], []
        for kv in range(N_KV_HEADS):
            j0, j1 = 2 * kv, 2 * kv + 1
            do0 = do_ref[:, BLK * j0:BLK * j0 + BLK]
            do1 = do_ref[:, BLK * j1:BLK * j1 + BLK]
            dos = jnp.concatenate([do0, do1], axis=0)
            dqs, dks, dvs = [], [], []
            for e in range(2):
                sl = slice(BLK * (2 * kv + e), BLK * (2 * kv + e) + BLK)
                kwin = jnp.concatenate([km_ref[:, sl], kp_ref[:, sl], kc_ref[:, sl]], axis=0)
                vwin = jnp.concatenate([vm_ref[:, sl], vp_ref[:, sl], vc_ref[:, sl]], axis=0)
                qs, s, sink = _attn_scores(q_ref, kwin, sk_ref, bias, kv, e)
                h0 = 4 * kv + e
                lse_rows = jnp.concatenate([lse_ref[:, h0:h0 + 1], lse_ref[:, h0 + 2:h0 + 3]], axis=0)
                p = jnp.exp(s - lse_rows)
                delta = jnp.concatenate([dl_ref[:, h0:h0 + 1], dl_ref[:, h0 + 2:h0 + 3]], axis=0)
                dp = lax.dot_general(dos, vwin, nt, preferred_element_type=F32)
                ds = (p * (dp - delta)).astype(BF16)
                pb = p.astype(BF16)
                dqs.append(jnp.dot(ds, kwin, preferred_element_type=F32))
                dks.append(lax.dot_general(ds, qs, tn, preferred_element_type=F32))
                dvs.append(lax.dot_general(pb, dos, tn, preferred_element_type=F32))
            dq = jnp.where(half2 == 0, dqs[0], dqs[1])
            dq_ref[:, BLK * j0:BLK * j0 + BLK] = dq[:BLK].astype(BF16)
            dq_ref[:, BLK * j1:BLK * j1 + BLK] = dq[BLK:].astype(BF16)
            own = kv % 2
            folded_k.append(dks[own] + pltpu.roll(dks[1 - own], HEAD_DIM, axis=1))
            folded_v.append(dvs[own] + pltpu.roll(dvs[1 - own], HEAD_DIM, axis=1))
            if own == 1:
                cols = slice(BLK * (kv // 2), BLK * (kv // 2) + BLK)
                for folded, m_ref, p_ref, c_ref in ((folded_k, dkm_ref, dkp_ref, dkc_ref),
                                                    (folded_v, dvm_ref, dvp_ref, dvc_ref)):
                    both = jnp.where(half3 == 0, folded[0], folded[1])
                    m_ref[:, cols] += both[:BLK]
                    p_ref[:, cols] = both[BLK:2 * BLK].astype(BF16)
                    c_ref[:, cols] = both[2 * BLK:].astype(BF16)
                folded_k, folded_v = [], []
        p_sink = jnp.exp(sk_ref[...] - lse_ref[:, :N_Q_HEADS])
        dsk_ref[...] -= jnp.sum(p_sink * dl_ref[:, :N_Q_HEADS], axis=0, keepdims=True)

    kv_specs = _attn_specs(nb)
    row_wide = _row_spec(BLK, wide)
    acc_wide = _full_spec((BLK, wide))
    big = jax.ShapeDtypeStruct((l_dim, wide), BF16)
    return pl.pallas_call(
        body, name="attn_bwd", grid=(nb,),
        in_specs=[_row_spec(BLK, Q_W)] + kv_specs + kv_specs
        + [_full_spec((1, N_Q_HEADS)), _bias_spec(), _row_spec(BLK, BLK), _row_spec(BLK, BLK),
           _row_spec(BLK, Q_W)],
        out_specs=[_row_spec(BLK, Q_W), row_wide, row_wide, acc_wide, row_wide, row_wide, acc_wide,
                   _full_spec((1, N_Q_HEADS))],
        out_shape=[jax.ShapeDtypeStruct((l_dim, Q_W), BF16), big, big, jax.ShapeDtypeStruct((BLK, wide), F32),
                   big, big, jax.ShapeDtypeStruct((BLK, wide), F32), jax.ShapeDtypeStruct((1, N_Q_HEADS), F32)],
        compiler_params=_params(),
    )(qn, kf, kf, kf, vf, vf, vf, sinks, _attn_bias(), lse, delta_rows, dattn)


def _qk_bwd(qkv, q_norm_t, k_norm_t, e_mat, dq, dkc, dkp, dkm, dvc, dvp, dvm, dproj):
    l_dim = qkv.shape[0]
    nb = l_dim // BLK
    wide = KV_W

    def body(x_ref, qg_ref, kg_ref, e_ref, dq_ref, dkc_ref, dkp_ref, dkm_ref, dvc_ref, dvp_ref, dvm_ref,
             dproj_ref, o_ref, dqg_ref, dkg_ref):
        i = pl.program_id(0)

        @pl.when(i == 0)
        def _():
            dqg_ref[...] = jnp.zeros_like(dqg_ref)
            dkg_ref[...] = jnp.zeros_like(dkg_ref)

        first = jnp.where(i == 0, 1.0, 0.0)
        not_last = jnp.where(i < nb - 1, 1.0, 0.0)
        dk_x = dkc_ref[...].astype(F32) + not_last * dkp_ref[...].astype(F32) + first * dkm_ref[...]
        dv_x = dvc_ref[...].astype(F32) + not_last * dvp_ref[...].astype(F32) + first * dvm_ref[...]
        x = x_ref[...]
        dqx, dqg = _head_rms_bwd(x[:, :Q_W], qg_ref[...], dq_ref[...].astype(F32) * (HEAD_DIM ** -0.5), e_ref)
        dkx, dkg = _head_rms_bwd(x[:, Q_W:Q_W + KV_W], kg_ref[...], dk_x, e_ref)
        o_ref[:, :Q_W] = dqx.astype(BF16)
        o_ref[:, Q_W:Q_W + KV_W] = dkx.astype(BF16)
        o_ref[:, Q_W + KV_W:] = dv_x.astype(BF16)
        dqg_ref[...] += dqg
        dkg_ref[...] += dkg

    nxt = lambda i: jnp.minimum(i + 1, nb - 1)
    row_wide = _row_spec(BLK, wide)
    nxt_wide = _row_spec(BLK, wide, nxt)
    acc_wide = _full_spec((BLK, wide))
    qkv_w = Q_W + 2 * KV_W
    in_specs = [_row_spec(BLK, qkv_w), _full_spec((1, Q_W)), _full_spec((1, KV_W)),
                _full_spec((256, 256)), _row_spec(BLK, Q_W),
                row_wide, nxt_wide, acc_wide, row_wide, nxt_wide, acc_wide, pl.BlockSpec(memory_space=pl.ANY)]
    return pl.pallas_call(
        body, name="qk_bwd", grid=(nb,),
        in_specs=in_specs,
        out_specs=[pl.BlockSpec((BLK, qkv_w), lambda i: (i, 3 * D_MODEL // qkv_w)),
                   _full_spec((1, Q_W)), _full_spec((1, KV_W))],
        input_output_aliases={len(in_specs) - 1: 0},
        out_shape=[jax.ShapeDtypeStruct(dproj.shape, BF16),
                   jax.ShapeDtypeStruct((1, Q_W), F32), jax.ShapeDtypeStruct((1, KV_W), F32)],
        compiler_params=_params(),
    )(qkv, q_norm_t, k_norm_t, e_mat, dq, dkc, dkp, dkm, dvc, dvp, dvm, dproj)


GRP = 8


def _strided(r, g):
    return pl.ds(r, g, stride=GRP)


def _lane_slab(ref, i, r, g):
    return ref[i, _strided(r, g), :]


def _chunk_carries(xr_ref, xi_ref, i, ar, ai, sqr_ref, sqi_ref, seq_r, seq_i, cin_r, cin_i, sign, reverse):
    g = xr_ref.shape[1] // GRP
    sr = si = None
    for r in (range(GRP - 1, -1, -1) if reverse else range(GRP)):
        xr, xi = _lane_slab(xr_ref, i, r, g), _lane_slab(xi_ref, i, r, g)
        if sr is not None:
            xr, xi = xr + ar * sr - ai * si, xi + ar * si + ai * sr
        sr, si = xr, xi
    row = lax.broadcasted_iota(jnp.int32, sr.shape, 0)
    idx, s = 0, 1
    while s < g:
        br = sqr_ref[idx:idx + 1, BLK * i:BLK * i + BLK]
        bi = sign * sqi_ref[idx:idx + 1, BLK * i:BLK * i + BLK]
        shift, keep = (g - s, row < g - s) if reverse else (s, row >= s)
        pr = jnp.where(keep, pltpu.roll(sr, shift, axis=0), 0.0)
        pi = jnp.where(keep, pltpu.roll(si, shift, axis=0), 0.0)
        sr, si = sr + br * pr - bi * pi, si + br * pi + bi * pr
        idx, s = idx + 1, 2 * s
    return sr + seq_r * cin_r - seq_i * cin_i, si + seq_r * cin_i + seq_i * cin_r


def _ssm_fwd_kb(u, wb_re, wb_im, wc_re, wc_im, d_skip, tabs, q):
    l_dim = u.shape[0]
    nc = l_dim // q
    g = q // GRP

    def body(u_ref, wbr_ref, wbi_ref, wcr_ref, wci_ref, d_ref, a1r_ref, a1i_ref, sqr_ref, sqi_ref,
             seqr_ref, seqi_ref, y_ref, z_ref, sr_ref, si_ref, cr_ref, ci_ref, xr_ref, xi_ref):
        @pl.when(pl.program_id(1) == 0)
        def _():
            cr_ref[...] = jnp.zeros_like(cr_ref)
            ci_ref[...] = jnp.zeros_like(ci_ref)

        u_kb = u_ref[...]
        ub = u_kb.astype(BF16)
        xr = jnp.dot(ub, wbr_ref[0], preferred_element_type=F32)
        xi = jnp.dot(ub, wbi_ref[0], preferred_element_type=F32)
        for i in range(LB_KB):
            xr_ref[i] = xr[:, BLK * i:BLK * i + BLK]
            xi_ref[i] = xi[:, BLK * i:BLK * i + BLK]
        row = lax.broadcasted_iota(jnp.int32, (g, BLK), 0)
        for i in range(LB_KB):
            lanes = slice(BLK * i, BLK * i + BLK)
            ar, ai = a1r_ref[0:1, lanes], a1i_ref[0:1, lanes]
            cin_r, cin_i = cr_ref[0:1, lanes], ci_ref[0:1, lanes]
            tr, ti = _chunk_carries(xr_ref, xi_ref, i, ar, ai, sqr_ref, sqi_ref, seqr_ref[:, lanes],
                                    seqi_ref[:, lanes], cin_r, cin_i, 1.0, reverse=False)
            cr_ref[0:1, lanes] = jnp.sum(jnp.where(row == g - 1, tr, 0.0), axis=0, keepdims=True)
            ci_ref[0:1, lanes] = jnp.sum(jnp.where(row == g - 1, ti, 0.0), axis=0, keepdims=True)
            pr = jnp.where(row == 0, cin_r, pltpu.roll(tr, 1, axis=0))
            pi = jnp.where(row == 0, cin_i, pltpu.roll(ti, 1, axis=0))
            for r in range(GRP):
                pr, pi = (_lane_slab(xr_ref, i, r, g) + ar * pr - ai * pi,
                          _lane_slab(xi_ref, i, r, g) + ar * pi + ai * pr)
                sr_ref[i, _strided(r, g), :] = pr
                si_ref[i, _strided(r, g), :] = pi
        s_r = jnp.concatenate([sr_ref[i] for i in range(LB_KB)], axis=1)
        s_i = jnp.concatenate([si_ref[i] for i in range(LB_KB)], axis=1)
        y = (jnp.dot(s_r.astype(BF16), wcr_ref[0], preferred_element_type=F32)
             - jnp.dot(s_i.astype(BF16), wci_ref[0], preferred_element_type=F32)
             + d_ref[...] * u_kb)
        y_ref[...] = y.astype(BF16)
        z_ref[...] = _gelu(y).astype(BF16)

    chan = pl.BlockSpec((q, BLK), lambda k, c: (c, k))
    wb_spec = pl.BlockSpec((1, BLK, ST_KB), lambda k, c: (k, 0, 0))
    wc_spec = pl.BlockSpec((1, ST_KB, BLK), lambda k, c: (k, 0, 0))
    tab_specs = [pl.BlockSpec((t.shape[0], ST_KB), lambda k, c: (0, k)) for t in tabs[:6]]
    state_spec = pl.BlockSpec((LB_KB, q, BLK), lambda k, c: (k, c, 0))
    state_shape = jax.ShapeDtypeStruct((N_LB, l_dim, BLK), F32)
    return pl.pallas_call(
        body, name="ssm_fwd", grid=(SSM_KB, nc),
        in_specs=[chan, wb_spec, wb_spec, wc_spec, wc_spec, pl.BlockSpec((1, BLK), lambda k, c: (0, k))] + tab_specs,
        out_specs=[chan, chan, state_spec, state_spec],
        out_shape=[jax.ShapeDtypeStruct((l_dim, D_MODEL), BF16), jax.ShapeDtypeStruct((l_dim, D_MODEL), BF16),
                   state_shape, state_shape],
        scratch_shapes=[pltpu.VMEM((8, ST_KB), F32), pltpu.VMEM((8, ST_KB), F32),
                        pltpu.VMEM((LB_KB, q, BLK), F32), pltpu.VMEM((LB_KB, q, BLK), F32)],
        compiler_params=_params(dimension_semantics=("parallel", "arbitrary")),
    )(u, wb_re, wb_im, wc_re, wc_im, d_skip, *tabs[:6])


def _ssm_bwd_kb(dz, y, u, s_re, s_im, wb_re, wb_im, wc_re, wc_im, d_skip, tabs, q, dproj):
    l_dim = u.shape[0]
    nc = l_dim // q
    g = q // GRP

    def body(dz_ref, y_ref, u_ref, sr_ref, si_ref, wbr_ref, wbi_ref, wcr_ref, wci_ref, d_ref,
             a1r_ref, a1i_ref, sqr_ref, sqi_ref, revr_ref, revi_ref, dproj_ref,
             du_ref, dd_ref, dar_ref, dai_ref, dwbr_ref, dwbi_ref, dwcr_ref, dwci_ref,
             cr_ref, ci_ref, gr_ref, gi_ref):
        @pl.when(pl.program_id(1) == 0)
        def _():
            for ref in (cr_ref, ci_ref, dd_ref, dar_ref, dai_ref, dwbr_ref, dwbi_ref, dwcr_ref, dwci_ref):
                ref[...] = jnp.zeros_like(ref)

        tn = (((0,), (0,)), ((), ()))
        nt = (((1,), (1,)), ((), ()))
        u_kb = u_ref[...]
        dy = dz_ref[...].astype(F32) * _gelu_grad(y_ref[...].astype(F32))
        dyb = dy.astype(BF16)
        ub = u_kb.astype(BF16)
        dd_ref[...] += jnp.sum(dy * u_kb, axis=0, keepdims=True)
        ds_r = lax.dot_general(dyb, wcr_ref[0], nt, preferred_element_type=F32)
        ds_i = -lax.dot_general(dyb, wci_ref[0], nt, preferred_element_type=F32)
        for i in range(LB_KB):
            gr_ref[i] = ds_r[:, BLK * i:BLK * i + BLK]
            gi_ref[i] = ds_i[:, BLK * i:BLK * i + BLK]
        row = lax.broadcasted_iota(jnp.int32, (g, BLK), 0)
        for i in range(LB_KB):
            lanes = slice(BLK * i, BLK * i + BLK)
            ar, ai = a1r_ref[0:1, lanes], -a1i_ref[0:1, lanes]
            cin_r, cin_i = cr_ref[0:1, lanes], ci_ref[0:1, lanes]
            tr, ti = _chunk_carries(gr_ref, gi_ref, i, ar, ai, sqr_ref, sqi_ref, revr_ref[:, lanes],
                                    -revi_ref[:, lanes], cin_r, cin_i, -1.0, reverse=True)
            cr_ref[0:1, lanes] = jnp.sum(jnp.where(row == 0, tr, 0.0), axis=0, keepdims=True)
            ci_ref[0:1, lanes] = jnp.sum(jnp.where(row == 0, ti, 0.0), axis=0, keepdims=True)
            nr = jnp.where(row == g - 1, cin_r, pltpu.roll(tr, g - 1, axis=0))
            ni = jnp.where(row == g - 1, cin_i, pltpu.roll(ti, g - 1, axis=0))
            acc_r = jnp.zeros((g, BLK), F32)
            acc_i = jnp.zeros((g, BLK), F32)
            for r in range(GRP - 1, -1, -1):
                s_r, s_i = _lane_slab(sr_ref, i, r, g), _lane_slab(si_ref, i, r, g)
                acc_r = acc_r + (nr * s_r + ni * s_i)
                acc_i = acc_i + (ni * s_r - nr * s_i)
                nr, ni = (_lane_slab(gr_ref, i, r, g) + ar * nr - ai * ni,
                          _lane_slab(gi_ref, i, r, g) + ar * ni + ai * nr)
                gr_ref[i, _strided(r, g), :] = nr
                gi_ref[i, _strided(r, g), :] = ni
            dar_ref[:, lanes] += jnp.sum(acc_r, axis=0, keepdims=True)
            dai_ref[:, lanes] += jnp.sum(acc_i, axis=0, keepdims=True)
        grb = jnp.concatenate([gr_ref[i] for i in range(LB_KB)], axis=1).astype(BF16)
        gib = jnp.concatenate([gi_ref[i] for i in range(LB_KB)], axis=1).astype(BF16)
        srb = jnp.concatenate([sr_ref[i] for i in range(LB_KB)], axis=1).astype(BF16)
        sib = jnp.concatenate([si_ref[i] for i in range(LB_KB)], axis=1).astype(BF16)
        du = (lax.dot_general(grb, wbr_ref[0], nt, preferred_element_type=F32)
              + lax.dot_general(gib, wbi_ref[0], nt, preferred_element_type=F32)
              + d_ref[...] * dy)
        du_ref[...] = du.astype(BF16)
        dwbr_ref[0] += lax.dot_general(ub, grb, tn, preferred_element_type=F32)
        dwbi_ref[0] += lax.dot_general(ub, gib, tn, preferred_element_type=F32)
        dwcr_ref[0] += lax.dot_general(srb, dyb, tn, preferred_element_type=F32)
        dwci_ref[0] -= lax.dot_general(sib, dyb, tn, preferred_element_type=F32)

    chan = pl.BlockSpec((q, BLK), lambda k, c: (nc - 1 - c, k))
    wb_spec = pl.BlockSpec((1, BLK, ST_KB), lambda k, c: (k, 0, 0))
    wc_spec = pl.BlockSpec((1, ST_KB, BLK), lambda k, c: (k, 0, 0))
    tab_in = [tabs[0], tabs[1], tabs[2], tabs[3], tabs[6], tabs[7]]
    tab_specs = [pl.BlockSpec((t.shape[0], ST_KB), lambda k, c: (0, k)) for t in tab_in]
    vec = pl.BlockSpec((1, BLK), lambda k, c: (0, k))
    svec = pl.BlockSpec((1, ST_KB), lambda k, c: (0, k))
    state_spec = pl.BlockSpec((LB_KB, q, BLK), lambda k, c: (k, nc - 1 - c, 0))
    du_spec = pl.BlockSpec((q, BLK), lambda k, c: (nc - 1 - c, 2 * D_MODEL // BLK + k))
    in_specs = ([chan, chan, chan, state_spec, state_spec, wb_spec, wb_spec, wc_spec, wc_spec, vec] + tab_specs
                + [pl.BlockSpec(memory_space=pl.ANY)])
    return pl.pallas_call(
        body, name="ssm_bwd", grid=(SSM_KB, nc),
        in_specs=in_specs,
        out_specs=[du_spec, vec, svec, svec, wb_spec, wb_spec, wc_spec, wc_spec],
        input_output_aliases={len(in_specs) - 1: 0},
        out_shape=[jax.ShapeDtypeStruct(dproj.shape, BF16), jax.ShapeDtypeStruct((1, D_MODEL), F32),
                   jax.ShapeDtypeStruct((1, N_STATE), F32), jax.ShapeDtypeStruct((1, N_STATE), F32),
                   jax.ShapeDtypeStruct((SSM_KB, BLK, ST_KB), F32), jax.ShapeDtypeStruct((SSM_KB, BLK, ST_KB), F32),
                   jax.ShapeDtypeStruct((SSM_KB, ST_KB, BLK), F32), jax.ShapeDtypeStruct((SSM_KB, ST_KB, BLK), F32)],
        scratch_shapes=[pltpu.VMEM((8, ST_KB), F32), pltpu.VMEM((8, ST_KB), F32),
                        pltpu.VMEM((LB_KB, q, BLK), F32), pltpu.VMEM((LB_KB, q, BLK), F32)],
        compiler_params=_params(dimension_semantics=("parallel", "arbitrary")),
    )(dz, y, u, s_re, s_im, wb_re, wb_im, wc_re, wc_im, d_skip, *tab_in, dproj)


def _discretize(lam_re, lam_im, log_dt, b_re, b_im):
    dt = jnp.exp(log_dt)[:, None]
    mag = jnp.exp(lam_re * dt)
    ar, ai = mag * jnp.cos(lam_im * dt), mag * jnp.sin(lam_im * dt)
    den = lam_re * lam_re + lam_im * lam_im
    nr, ni = ar - 1.0, ai
    fr, fi = (nr * lam_re + ni * lam_im) / den, (ni * lam_re - nr * lam_im) / den
    bbar_re = fr[..., None] * b_re - fi[..., None] * b_im
    bbar_im = fr[..., None] * b_im + fi[..., None] * b_re
    return ar, ai, bbar_re, bbar_im


def _block_diag_b(bbar):
    eye = jnp.eye(8, dtype=bbar.dtype)
    return jnp.einsum("kgpc,gh->kgchp", bbar.reshape(8, 8, SSM_STATE, SSM_GROUP_CH), eye).reshape(8, BLK, ST_KB)


def _block_diag_b_t(dwb):
    eye = jnp.eye(8, dtype=dwb.dtype)
    return jnp.einsum("kgchp,gh->kgpc", dwb.reshape(8, 8, SSM_GROUP_CH, 8, SSM_STATE), eye).reshape(
        SSM_GROUPS, SSM_STATE, SSM_GROUP_CH)


def _block_diag_c(c):
    eye = jnp.eye(8, dtype=c.dtype)
    return jnp.einsum("kgcp,gh->kgphc", c.reshape(8, 8, SSM_GROUP_CH, SSM_STATE), eye).reshape(8, ST_KB, BLK)


def _block_diag_c_t(dwc):
    eye = jnp.eye(8, dtype=dwc.dtype)
    return jnp.einsum("kgphc,gh->kgcp", dwc.reshape(8, 8, SSM_STATE, 8, SSM_GROUP_CH), eye).reshape(
        SSM_GROUPS, SSM_GROUP_CH, SSM_STATE)


def _powers(br, bi, n):
    pr, pi = br, bi
    cr, ci = br, bi
    while pr.shape[0] < n:
        pr, pi = (jnp.concatenate([pr, pr * cr - pi * ci], axis=0),
                  jnp.concatenate([pi, pr * ci + pi * cr], axis=0))
        cr, ci = cr * cr - ci * ci, 2.0 * cr * ci
    return pr[:n], pi[:n]


def _powers_desc(br, bi, n):
    pr, pi = br, bi
    cr, ci = br, bi
    while pr.shape[0] < n:
        pr, pi = (jnp.concatenate([pr * cr - pi * ci, pr], axis=0),
                  jnp.concatenate([pr * ci + pi * cr, pi], axis=0))
        cr, ci = cr * cr - ci * ci, 2.0 * cr * ci
    return pr, pi


def _power_tables(ar, ai, g):
    a1r, a1i = _powers(ar, ai, GRP)
    seqr, seqi = _powers(a1r[GRP - 1:], a1i[GRP - 1:], g)
    g2 = 1 << (g - 1).bit_length()
    revr, revi = _powers_desc(a1r[GRP - 1:], a1i[GRP - 1:], g2)
    revr, revi = revr[g2 - g:], revi[g2 - g:]
    sq_r, sq_i = [seqr[0:1]], [seqi[0:1]]
    while len(sq_r) < 8:
        r, i = sq_r[-1], sq_i[-1]
        sq_r.append(r * r - i * i)
        sq_i.append(2.0 * r * i)
    sqr, sqi = jnp.concatenate(sq_r, axis=0), jnp.concatenate(sq_i, axis=0)
    return a1r, a1i, sqr, sqi, seqr, seqi, revr, revi


HBM_SPEC = pl.BlockSpec(memory_space=pltpu.HBM)
SEM_SPEC = pl.BlockSpec(memory_space=pltpu.SEMAPHORE)
DATAFLOW = pltpu.SideEffectType.DATAFLOW_SIDE_EFFECTING


def _plain_rows(p, m):
    return p * m


def _ffn_in_rows(p, m):
    return ((p & 3) >> 1) * (4 * m) + (p >> 2) * (2 * m) + (p & 1) * m


def _peer_copies(src_refs, land_refs, send_sems, recv_sems, chunked, row_fns):
    x, y, c = lax.axis_index("x"), lax.axis_index("y"), lax.axis_index("c")
    me = 4 * x + 2 * y + c
    copies = []
    for a, (src, land) in enumerate(zip(src_refs, land_refs)):
        m = land.shape[0] // N_DEV
        for k in range(N_DEV - 1):
            rel = k + 1
            bx, by, bc = (rel >> 2) & 1, (rel >> 1) & 1, rel & 1
            peer = (x + bx - 2 * x * bx, y + by - 2 * y * by, c + bc - 2 * c * bc)
            p_idx = 4 * peer[0] + 2 * peer[1] + peer[2]
            copies.append(pltpu.make_async_remote_copy(
                src_ref=src.at[pl.ds(row_fns[a](p_idx, m), m), :] if chunked else src,
                dst_ref=land.at[pl.ds(me * m if chunked else row_fns[a](me, m), m), :],
                send_sem=send_sems[a * (N_DEV - 1) + k], recv_sem=recv_sems[a * (N_DEV - 1) + k],
                device_id=peer, device_id_type=MESH))
    return copies


def _own_copies(src_refs, land_outs, sems, chunked, row_fns):
    me = _my_index()
    copies = []
    for a, (src, land) in enumerate(zip(src_refs, land_outs)):
        m = land.shape[0] // N_DEV
        rows = row_fns[a](me, m)
        copies.append(pltpu.make_async_copy(src.at[pl.ds(rows, m), :] if chunked else src,
                                            land.at[pl.ds(me * m if chunked else rows, m), :], sems[a]))
    return copies


def _send_start(name, srcs, lands, chunked, row_fns=None):
    n = len(srcs)
    ns = n * (N_DEV - 1)
    row_fns = row_fns or [_plain_rows] * n

    def body(*refs):
        src_refs, land_refs = refs[:n], refs[n:2 * n]
        send_sems, own_sems = refs[2 * n:2 * n + ns], refs[2 * n + ns:2 * n + ns + n]
        recv_sems = refs[2 * n + ns + n:2 * n + 2 * ns + n]
        land_outs, token = refs[-1 - n:-1], refs[-1]
        for cp in _own_copies(src_refs, land_outs, own_sems, chunked, row_fns):
            cp.start()
        for cp in _peer_copies(src_refs, land_refs, send_sems, recv_sems, chunked, row_fns):
            cp.start()
        token[...] = jnp.zeros_like(token)

    ins = [pltpu.with_memory_space_constraint(a, pltpu.HBM) for a in list(srcs) + list(lands)]
    n_sems = 2 * ns + n
    out = pl.pallas_call(
        body, name=name,
        in_specs=[HBM_SPEC] * (2 * n),
        out_specs=[SEM_SPEC] * n_sems + [HBM_SPEC] * (2 * n) + [pl.BlockSpec(memory_space=pltpu.VMEM)],
        out_shape=[pltpu.SemaphoreType.DMA(())] * n_sems
        + [pltpu.HBM(a.shape, a.dtype) for a in list(srcs) + list(lands)]
        + [jax.ShapeDtypeStruct((8, BLK), F32)],
        input_output_aliases={i: i + n_sems for i in range(2 * n)},
        compiler_params=pltpu.CompilerParams(has_side_effects=DATAFLOW),
    )(*ins)
    return (out[:ns + n], out[ns + n:n_sems], out[n_sems:n_sems + n], out[n_sems + n:n_sems + 2 * n], out[-1])


def _send_wait(name, send_sems, recv_sems, srcs, lands, after, chunked, row_fns=None):
    n = len(srcs)
    ns = n * (N_DEV - 1)
    row_fns = row_fns or [_plain_rows] * n
    after = list(after) if isinstance(after, (list, tuple)) else [after]

    def body(*refs):
        src_refs, land_refs = refs[:n], refs[n:2 * n]
        s_sems, own_sems = refs[2 * n:2 * n + ns], refs[2 * n + ns:2 * n + ns + n]
        r_sems = refs[2 * n + ns + n:2 * n + 2 * ns + n]
        land_outs = refs[-n:]
        for cp in _own_copies(src_refs, land_outs, own_sems, chunked, row_fns):
            cp.wait()
        copies = _peer_copies(src_refs, land_refs, s_sems, r_sems, chunked, row_fns)
        for cp in copies:
            cp.wait_send()
        for cp in copies:
            cp.wait_recv()

    out = pl.pallas_call(
        body, name=name,
        in_specs=[HBM_SPEC] * (2 * n) + [SEM_SPEC] * (2 * ns + n) + [pl.BlockSpec(memory_space=pl.ANY)] * len(after),
        out_specs=[HBM_SPEC] * (2 * n),
        out_shape=[pltpu.HBM(a.shape, a.dtype) for a in list(srcs) + list(lands)],
        input_output_aliases={i: i for i in range(2 * n)},
        compiler_params=pltpu.CompilerParams(has_side_effects=DATAFLOW),
    )(*srcs, *lands, *send_sems, *recv_sems, *after)
    return out[n:]


def _sum_slots(name, recv, own, wmv=None):
    m, ncol = own.shape
    tr = m // 2 if (m // 2) % 16 == 0 else m
    g = m // tr
    c1 = 1.0 - ADAM_B1 ** ADAM_STEP
    c2 = 1.0 - ADAM_B2 ** ADAM_STEP

    def body(*refs):
        slots, own_ref, o_ref = refs[:N_DEV], refs[N_DEV], refs[N_DEV + 1 + (3 if wmv else 0)]
        me = _my_index()
        tot = None
        for s in range(N_DEV):
            v = jnp.where(me == s, own_ref[...], slots[s][...].astype(F32))
            tot = v if tot is None else tot + v
        o_ref[...] = tot
        if wmv:
            w_ref, m_ref, v_ref = refs[N_DEV + 1:N_DEV + 4]
            d_ref, nm_ref, nv_ref = refs[N_DEV + 5:N_DEV + 8]
            nm = ADAM_B1 * m_ref[...] + (1.0 - ADAM_B1) * tot
            nv = ADAM_B2 * v_ref[...] + (1.0 - ADAM_B2) * (tot * tot)
            d_ref[...] = -ADAM_LR * ((nm / c1) / (jnp.sqrt(nv / c2) + ADAM_EPS) + ADAM_WD * w_ref[...])
            nm_ref[...] = nm
            nv_ref[...] = nv

    def slot_spec(s):
        return pl.BlockSpec((tr, ncol), lambda i: (s * g + i, 0))

    tile = pl.BlockSpec((tr, ncol), lambda i: (i, 0))
    shape = jax.ShapeDtypeStruct((m, ncol), F32)
    n_extra = 3 if wmv else 0
    out = pl.pallas_call(
        body, name=name, grid=(g,),
        in_specs=[slot_spec(s) for s in range(N_DEV)] + [tile] * (1 + n_extra),
        out_specs=[tile] * (1 + n_extra),
        out_shape=[shape] * (1 + n_extra),
        compiler_params=_params(),
    )(*([recv] * N_DEV), own, *(wmv or ()))
    return out if wmv else out[0]


def _sum_gathered(name, gathered, rows):
    tr = _pick(rows, 512, 8)
    g = rows // tr

    def body(*refs):
        o_ref = refs[N_DEV]
        tot = refs[0][...]
        for s in range(1, N_DEV):
            tot = tot + refs[s][...]
        o_ref[...] = tot

    return pl.pallas_call(
        body, name=name, grid=(g,),
        in_specs=[pl.BlockSpec((tr, BLK), (lambda i, s=s: (s * g + i, 0))) for s in range(N_DEV)],
        out_specs=pl.BlockSpec((tr, BLK), lambda i: (i, 0)),
        out_shape=jax.ShapeDtypeStruct((rows, BLK), F32),
        compiler_params=_params(),
    )(*([gathered] * N_DEV))


def _adamw(name, w, g, m, v):
    r, c = w.shape
    tr = _pick(r, 256, 8) if r % 8 == 0 else r
    c1 = 1.0 - ADAM_B1 ** ADAM_STEP
    c2 = 1.0 - ADAM_B2 ** ADAM_STEP

    def body(w_ref, g_ref, m_ref, v_ref, d_ref, nm_ref, nv_ref):
        gv = g_ref[...]
        nm = ADAM_B1 * m_ref[...] + (1.0 - ADAM_B1) * gv
        nv = ADAM_B2 * v_ref[...] + (1.0 - ADAM_B2) * (gv * gv)
        m_hat = nm / c1
        v_hat = nv / c2
        d_ref[...] = -ADAM_LR * (m_hat / (jnp.sqrt(v_hat) + ADAM_EPS) + ADAM_WD * w_ref[...])
        nm_ref[...] = nm
        nv_ref[...] = nv

    spec = pl.BlockSpec((tr, c), lambda i: (i, 0))
    shape = jax.ShapeDtypeStruct((r, c), F32)
    return pl.pallas_call(
        body, name=name, grid=(r // tr,),
        in_specs=[spec] * 4, out_specs=[spec] * 3, out_shape=[shape] * 3,
        compiler_params=_params(),
    )(w, g, m, v)


def _adamw_many(name, ws, gs, ms, vs):
    n = len(ws)
    c1 = 1.0 - ADAM_B1 ** ADAM_STEP
    c2 = 1.0 - ADAM_B2 ** ADAM_STEP

    def body(*refs):
        for a in range(n):
            w_ref, g_ref, m_ref, v_ref = refs[a], refs[n + a], refs[2 * n + a], refs[3 * n + a]
            d_ref, nm_ref, nv_ref = refs[4 * n + a], refs[5 * n + a], refs[6 * n + a]
            gv = g_ref[...]
            nm = ADAM_B1 * m_ref[...] + (1.0 - ADAM_B1) * gv
            nv = ADAM_B2 * v_ref[...] + (1.0 - ADAM_B2) * (gv * gv)
            d_ref[...] = -ADAM_LR * ((nm / c1) / (jnp.sqrt(nv / c2) + ADAM_EPS) + ADAM_WD * w_ref[...])
            nm_ref[...] = nm
            nv_ref[...] = nv

    specs = [_full_spec(w.shape) for w in ws]
    shapes = [jax.ShapeDtypeStruct(w.shape, F32) for w in ws]
    out = pl.pallas_call(
        body, name=name, grid=(1,),
        in_specs=specs * 4, out_specs=specs * 3, out_shape=shapes * 3,
        compiler_params=_params(),
    )(*ws, *gs, *ms, *vs)
    return out[:n], out[n:2 * n], out[2 * n:]


PACK_ROWS = 128


def _pack(parts):
    flat = []
    for p in parts:
        v = p.reshape(-1)
        flat.append(jnp.pad(v, (0, (-v.shape[0]) % BLK)))
    v = jnp.concatenate(flat)
    v = jnp.pad(v, (0, (-v.shape[0]) % (PACK_ROWS * BLK)))
    return v.reshape(-1, BLK)


def _unpack(buf, shapes):
    flat = buf.reshape(-1)
    out, off = [], 0
    for shp in shapes:
        size = math.prod(shp)
        out.append(flat[off:off + size].reshape(shp))
        off += size + (-size) % BLK
    return out


def kernel(x, meta_tokens, norm_mix, w_in, q_norm, k_norm, attn_sinks, lam_re, lam_im, log_dt, ssm_b_re, ssm_b_im, ssm_c_re, ssm_c_im, ssm_d, w_glu, attn_branch_norm, ssm_branch_norm, w_out, norm_ffn, w_ffn_in, w_ffn_out, loss_target, m_meta_tokens, m_norm_mix, m_w_in, m_q_norm, m_k_norm, m_attn_sinks, m_lam_re, m_lam_im, m_log_dt, m_ssm_b_re, m_ssm_b_im, m_ssm_c_re, m_ssm_c_im, m_ssm_d, m_w_glu, m_attn_branch_norm, m_ssm_branch_norm, m_w_out, m_norm_ffn, m_w_ffn_in, m_w_ffn_out, v_meta_tokens, v_norm_mix, v_w_in, v_q_norm, v_k_norm, v_attn_sinks, v_lam_re, v_lam_im, v_log_dt, v_ssm_b_re, v_ssm_b_im, v_ssm_c_re, v_ssm_c_im, v_ssm_d, v_w_glu, v_attn_branch_norm, v_ssm_branch_norm, v_w_out, v_norm_ffn, v_w_ffn_in, v_w_ffn_out):
    args = dict(locals())
    weights = {n: args[n] for n in WEIGHTS}
    mom_m = {n: args["m_" + n] for n in WEIGHTS}
    mom_v = {n: args["v_" + n] for n in WEIGHTS}

    x2d = x[0]
    target2d = loss_target[0]
    s_len = x2d.shape[0]
    l_dim = s_len + BLK
    tm_row = _pick(l_dim, 320)
    tm_mm = _pick(l_dim, 1040)
    tl_tn = _pick(l_dim, 2080)
    tm_ffn = _pick(l_dim, 640)
    tm_big = _pick(l_dim, 2080)
    tm_shift = _pick(l_dim, 640, BLK)

    shard_in = w_in[0].T.astype(BF16)
    shard_glu = w_glu[0].T.astype(BF16)
    shard_out = w_out[0].astype(BF16)
    shard_ffn_in = w_ffn_in[0].T.astype(BF16)
    shard_ffn_out = w_ffn_out[0].astype(BF16)
    shard_meta = meta_tokens.T
    me = _my_index()

    def landing(shard):
        return lax.empty((N_DEV * shard.shape[0], shard.shape[1]), shard.dtype)

    first = [shard_in, shard_meta]
    ga = _send_start("gather_start_a", first, [landing(s) for s in first], chunked=False)
    later = [shard_glu + ga[4][0:1, 0:1].astype(BF16), shard_out, shard_ffn_in, shard_ffn_out]
    later_fns = [_plain_rows, _plain_rows, _ffn_in_rows, _plain_rows]
    gb = _send_start("gather_start_b", later, [landing(s) for s in later], chunked=False,
                     row_fns=later_fns)

    nm_t = norm_mix + (ga[4][0:1, 0:1] + gb[4][0:1, 0:1])
    qn_t, kn_t = jnp.tile(q_norm, (1, N_Q_HEADS)), jnp.tile(k_norm, (1, N_KV_HEADS))
    e_mat = jnp.kron(jnp.eye(4, dtype=F32), jnp.ones((HEAD_DIM, HEAD_DIM), F32)).astype(BF16)

    def disc(lr, li, ldt, br, bi):
        return _discretize(lr[0], li[0], ldt[0], br[0], bi[0])

    (abar_re, abar_im, bbar_re, bbar_im), disc_vjp = jax.vjp(disc, lam_re, lam_im, log_dt, ssm_b_re, ssm_b_im)
    wb_re, wb_im = _block_diag_b(bbar_re).astype(BF16), _block_diag_b(bbar_im).astype(BF16)
    wc_re, wc_im = _block_diag_c(ssm_c_re[0]).astype(BF16), _block_diag_c(ssm_c_im[0]).astype(BF16)
    q_ssm = _pick(l_dim, 640, 64)
    tabs = _power_tables(abar_re.reshape(1, N_STATE), abar_im.reshape(1, N_STATE), q_ssm // GRP)

    h0, xn = _embed_norm(x2d, nm_t, tm_shift)
    wt_in, meta_t = _send_wait("gather_wait_a", ga[0], ga[1], ga[2], ga[3],
                               [xn, wb_re, wb_im, wc_re, wc_im, *tabs, e_mat, qn_t, kn_t], chunked=False)
    meta_pad = jnp.pad(meta_t.T, ((PAD, 0), (0, 0)))
    h0, xn = _embed_meta(meta_pad, nm_t, h0, xn)
    qkv_w, u_end = Q_W + 2 * KV_W, Q_W + 2 * KV_W + D_MODEL
    wt_in_p = jnp.concatenate([wt_in[u_end:], wt_in[qkv_w:u_end], wt_in[:qkv_w]], axis=0)
    qkv = _matmul("proj_qkv", xn, wt_in_p, nt=True, tm=tm_big, tn=qkv_w, tk=D_MODEL, n=qkv_w, w_off=2)
    u = _matmul("proj_u", xn, wt_in_p, nt=True, tm=tm_big, tn=D_MODEL, tk=D_MODEL, n=D_MODEL, w_off=2)
    gates = _matmul("proj_gates", xn, wt_in_p, nt=True, tm=tm_big, tn=D_MODEL, tk=D_MODEL, n=2 * D_MODEL, w_off=0,
                    out_dtype=BF16)
    qn, kf, vf = _qk_prep(qkv, qn_t, kn_t, e_mat, tm_row)
    attn, lse = _attn_fwd(qn, kf, vf, attn_sinks)
    y, z, s_re, s_im = _ssm_fwd_kb(u, wb_re, wb_im, wc_re, wc_im, ssm_d, tabs, q_ssm)
    wt_glu, w_out_f, wt_ffn_in, w_ffn_out_f = _send_wait("gather_wait_b", gb[0], gb[1], gb[2], gb[3], z,
                                                         chunked=False, row_fns=later_fns)
    zab = _matmul("glu_proj", z, wt_glu, nt=True, tm=tm_big, tn=1024, tk=D_MODEL, out_dtype=BF16)
    merged = _merge_fwd(attn, zab, gates, attn_branch_norm, ssm_branch_norm, tm_row)
    h1, hn = _matmul("out_proj", merged, w_out_f, nt=False, tm=tm_mm, tn=1024, tk=D_MODEL, res=h0,
                     norm_g=norm_ffn)
    gu, act = _ffn_in_swiglu(hn, wt_ffn_in, tm_ffn)
    h2 = _matmul("ffn_out", act, w_ffn_out_f, nt=False, tm=tm_mm, tn=1024, tk=D_FF, res=h1)
    dh2, dh2_b, loss_part = _loss_grad(h2, target2d, tm_shift)

    dgu = _d_act_swiglu(dh2_b, w_ffn_out_f, gu, tm_ffn)

    def exchange_start(name, grads_b, row_fns=None):
        return _send_start(name, grads_b, [lax.empty(g.shape, BF16) for g in grads_b], chunked=True,
                           row_fns=row_fns)

    g_ffn_out, g_ffn_out_b = _matmul_tn("g_ffn_out", act, dh2_b, tm=1408, tn=1024, tl=tl_tn)
    g_ffn_in_t, g_ffn_in_b = _matmul_tn("g_ffn_in", dgu, hn, tm=1408, tn=1024, tl=tl_tn)
    ffn_fns = [_ffn_in_rows, _plain_rows]
    ex1 = exchange_start("exchange_start_ffn", [g_ffn_in_b, g_ffn_out_b], ffn_fns)
    dhn = _matmul("d_hn", dgu, wt_ffn_in, nt=False, tm=tm_ffn, tn=1024, tk=2 * D_FF, out_dtype=BF16)
    dh1, dh1_b, g_norm_ffn = _norm_bwd_res("ffn_norm_bwd", h1, norm_ffn + ex1[4][0:1, 0:1], dhn, dh2, tm_row)
    dmerged = _matmul("d_merged", dh1_b, w_out_f, nt=True, tm=tm_big, tn=1024, tk=D_MODEL, out_dtype=BF16)
    dattn, dzab, dproj, g_abn, g_sbn, delta_rows = _merge_bwd(attn, zab, gates, attn_branch_norm, ssm_branch_norm,
                                                    dmerged, tm_row)
    g_out, g_out_b = _matmul_tn("g_out", merged, dh1_b, tm=1024, tn=1024, tl=tl_tn)
    dz = _matmul("d_z", dzab, wt_glu, nt=False, tm=tm_mm, tn=1024, tk=2 * D_MODEL, out_dtype=BF16)
    g_glu_t, g_glu_b = _matmul_tn("g_glu", dzab, z, tm=1024, tn=1024, tl=tl_tn)
    ex2 = exchange_start("exchange_start_mix", [g_glu_b, g_out_b])
    dproj, g_ssm_d, g_ar, g_ai, g_wbr, g_wbi, g_wcr, g_wci = _ssm_bwd_kb(
        dz, y, u, s_re, s_im, wb_re, wb_im, wc_re, wc_im, ssm_d + ex2[4][0:1, 0:1], tabs, q_ssm, dproj)
    dq, dkc, dkp, dkm, dvc, dvp, dvm, g_sinks = _attn_bwd(qn, kf, vf, attn_sinks, lse, delta_rows, dattn)
    dproj, g_qn_t, g_kn_t = _qk_bwd(qkv, qn_t, kn_t, e_mat, dq, dkc, dkp, dkm, dvc, dvp, dvm, dproj)
    g_lam_re, g_lam_im, g_log_dt, g_b_re, g_b_im = disc_vjp(
        (g_ar.reshape(SSM_GROUPS, SSM_STATE), g_ai.reshape(SSM_GROUPS, SSM_STATE),
         _block_diag_b_t(g_wbr), _block_diag_b_t(g_wbi)))
    small_grads = {
        "q_norm": g_qn_t.reshape(N_Q_HEADS, HEAD_DIM).sum(0)[None],
        "k_norm": g_kn_t.reshape(N_KV_HEADS, HEAD_DIM).sum(0)[None], "attn_sinks": g_sinks,
        "lam_re": g_lam_re, "lam_im": g_lam_im, "log_dt": g_log_dt, "ssm_b_re": g_b_re, "ssm_b_im": g_b_im,
        "ssm_c_re": _block_diag_c_t(g_wcr)[None], "ssm_c_im": _block_diag_c_t(g_wci)[None],
        "ssm_d": g_ssm_d, "attn_branch_norm": g_abn, "ssm_branch_norm": g_sbn, "norm_ffn": g_norm_ffn,
    }
    early = [n for n in SMALL if n != "norm_mix"]
    packed_e = _pack([small_grads[n] for n in early])
    gs_e = _send_start("small_start_a", [packed_e], [landing(packed_e)], chunked=False)

    blocks_g, blocks_u, blocks_q = 2 * D_MODEL // 512, D_MODEL // 512, qkv_w // 512
    back = lambda i: jnp.where(i < blocks_g, i + blocks_q + blocks_u,
                               jnp.where(i < blocks_g + blocks_u, i - blocks_g + blocks_q, i - blocks_g - blocks_u))
    g_in_t, g_in_b = _matmul_tn("g_in", dproj, xn, tm=512, tn=1024, tl=tl_tn, after=gs_e[4], out_rows=back)
    ex3 = exchange_start("exchange_start_in", [g_in_b])
    dxn = _matmul("d_xn", dproj, wt_in_p, nt=False, tm=tm_ffn, tn=1024, tk=IN_COLS, out_dtype=BF16)
    grad_x2d, dmeta_blk, g_norm_mix = _final_bwd(h0, nm_t + ex3[4][0:1, 0:1], dxn, dh1, _pick(s_len, 512, BLK))
    packed_l = _pack([g_norm_mix, dmeta_blk[PAD:], loss_part])
    gs_l = _send_start("small_start_b", [packed_l], [landing(packed_l)], chunked=False)
    grads, deltas, new_m, new_v = {}, {}, {}, {}

    recv_ffn_in, recv_ffn_out = _send_wait("exchange_wait_ffn", ex1[0], ex1[1], ex1[2], ex1[3], gs_l[4],
                                           chunked=True, row_fns=ffn_fns)
    recv_glu, recv_out = _send_wait("exchange_wait_mix", ex2[0], ex2[1], ex2[2], ex2[3], recv_ffn_in,
                                    chunked=True)
    (recv_in,) = _send_wait("exchange_wait_in", ex3[0], ex3[1], ex3[2], ex3[3], recv_glu, chunked=True)
    big = [("w_in", g_in_t, True, recv_in, _plain_rows), ("w_glu", g_glu_t, True, recv_glu, _plain_rows),
           ("w_out", g_out, False, recv_out, _plain_rows), ("w_ffn_in", g_ffn_in_t, True, recv_ffn_in, _ffn_in_rows),
           ("w_ffn_out", g_ffn_out, False, recv_ffn_out, _plain_rows)]
    for name, g_full, transposed, recv, row_fn in big:
        m_rows = g_full.shape[0] // N_DEV
        own = lax.dynamic_slice(g_full, (row_fn(me, m_rows), 0), (m_rows, g_full.shape[1]))
        if name == "w_glu":
            grads[name] = _sum_slots("sum_" + name, recv, own).T[None]
            continue
        shp = weights[name].shape
        if transposed:
            as2d, back = (lambda a: a.reshape(shp[-2], shp[-1]).T), (lambda a: a.T.reshape(shp))
        else:
            as2d, back = (lambda a: a.reshape(shp[-2], shp[-1])), (lambda a: a.reshape(shp))
        g_shard, d, nm, nv = _sum_slots("sum_adamw_" + name, recv, own,
                                        (as2d(weights[name]), as2d(mom_m[name]), as2d(mom_v[name])))
        grads[name], deltas[name], new_m[name], new_v[name] = back(g_shard), back(d), back(nm), back(nv)
        last = d

    def adamw_2d(name):
        shp = weights[name].shape
        as2d = lambda a: a.reshape(shp[-2], shp[-1])
        d, nm, nv = _adamw("adamw_" + name, as2d(weights[name]), as2d(grads[name]), as2d(mom_m[name]),
                           as2d(mom_v[name]))
        deltas[name], new_m[name], new_v[name] = d.reshape(shp), nm.reshape(shp), nv.reshape(shp)

    adamw_2d("w_glu")

    def small_sum(tag, gs, packed, after):
        (gathered,) = _send_wait("small_wait_" + tag, gs[0], gs[1], gs[2], gs[3], after, chunked=False)
        return _sum_gathered("sum_small_" + tag, gathered, packed.shape[0])

    def small_adamw(tag, names):
        view = lambda n, a: jnp.swapaxes(a, -1, -2) if n.startswith("ssm_b") else a
        d, nm, nv = _adamw_many("adamw_small_" + tag, [view(n, weights[n]) for n in names],
                                [view(n, grads[n]) for n in names], [view(n, mom_m[n]) for n in names],
                                [view(n, mom_v[n]) for n in names])
        deltas.update((n, view(n, a)) for n, a in zip(names, d))
        new_m.update((n, view(n, a)) for n, a in zip(names, nm))
        new_v.update((n, view(n, a)) for n, a in zip(names, nv))

    g_sum_e = small_sum("a", gs_e, packed_e, last)
    grads.update(zip(early, _unpack(g_sum_e, [weights[n].shape for n in early])))
    wide = [n for n in early if n.startswith(("ssm_b", "ssm_c"))]
    small_adamw("wide", wide)
    g_sum_l = small_sum("b", gs_l, packed_l, g_sum_e)
    grads["norm_mix"], g_meta, loss_sum = _unpack(g_sum_l, [weights["norm_mix"].shape, (N_META, D_MODEL), (1, 1)])
    small_adamw("rest", [n for n in SMALL if n not in wide])
    grads["meta_tokens"] = lax.dynamic_slice(g_meta, (0, me * BLK), (N_META, BLK))
    adamw_2d("meta_tokens")

    loss = loss_sum[0, 0]
    return (loss, grad_x2d[None], *[grads[n] for n in WEIGHTS], *[deltas[n] for n in WEIGHTS],
            *[new_m[n] for n in WEIGHTS], *[new_v[n] for n in WEIGHTS])
```
